```python
import math
import jax, jax.numpy as jnp
from jax import lax
import numpy as np

D_MODEL = 1024
BATCH = 8
SEQ = 2048
DEPTH = 2

N_A_LAYERS = DEPTH // 2
N_B_LAYERS = DEPTH - N_A_LAYERS
EPS = 1e-5

SSM_EXPAND = 2
SSM_INNER = SSM_EXPAND * D_MODEL
SSM_HEAD_DIM = 64
SSM_HEADS = SSM_INNER // SSM_HEAD_DIM
SSM_GROUPS = 8
SSM_STATE = 128
SSM_CONV = 4
SSM_CHUNK = 256
SSM_CONV_DIM = SSM_INNER + 2 * SSM_GROUPS * SSM_STATE
SSM_PROJ = 2 * SSM_INNER + 2 * SSM_GROUPS * SSM_STATE + SSM_HEADS

ATT_HEAD_DIM = 64
ATT_Q_HEADS = D_MODEL // ATT_HEAD_DIM
ATT_KV_HEADS = 4
ATT_GROUP = ATT_Q_HEADS // ATT_KV_HEADS
WINDOW = 128
ROPE_THETA = 10000.0

FFN_DIM = 2816
FFN_CONV = 3

kernel_name = "yoco_mamba2_swa_sink_convffn"


def rms_norm(x, g):
    xf = x.astype(jnp.float32)
    xf = xf * lax.rsqrt(jnp.mean(xf * xf, axis=-1, keepdims=True) + EPS)
    return xf.astype(x.dtype) * g


def group_rms_norm(y, g, groups):
    b, s, c = y.shape
    yf = y.astype(jnp.float32).reshape(b, s, groups, c // groups)
    yf = yf * lax.rsqrt(jnp.mean(yf * yf, axis=-1, keepdims=True) + EPS)
    return yf.reshape(b, s, c).astype(y.dtype) * g


def causal_dwconv(x, w, bias):
    width, ch = w.shape
    out = lax.conv_general_dilated(
        x, w[:, None, :].astype(x.dtype), window_strides=(1,),
        padding=[(width - 1, 0)], dimension_numbers=("NWC", "WIO", "NWC"),
        feature_group_count=ch)
    return out + bias


def rotary(x, positions):
    half = x.shape[-1] // 2
    inv_freq = ROPE_THETA ** (-jnp.arange(half, dtype=jnp.float32) / half)
    ang = positions.astype(jnp.float32)[..., None] * inv_freq
    cos = jnp.cos(ang)[:, :, None, :]
    sin = jnp.sin(ang)[:, :, None, :]
    xf = x.astype(jnp.float32)
    x1, x2 = xf[..., :half], xf[..., half:]
    return jnp.concatenate([x1 * cos - x2 * sin, x2 * cos + x1 * sin], axis=-1).astype(x.dtype)


def ssd_chunked(x, dt, A, Bm, Cm):
    b, s, h, p = x.shape
    g, n = Bm.shape[2], Bm.shape[3]
    e = h // g
    pad = (-s) % SSM_CHUNK
    x = jnp.pad(x, ((0, 0), (0, pad), (0, 0), (0, 0)))
    dt = jnp.pad(dt, ((0, 0), (0, pad), (0, 0)))
    Bm = jnp.pad(Bm, ((0, 0), (0, pad), (0, 0), (0, 0)))
    Cm = jnp.pad(Cm, ((0, 0), (0, pad), (0, 0), (0, 0)))
    L = SSM_CHUNK
    c = (s + pad) // L
    xd = (x * dt[..., None]).reshape(b, c, L, g, e, p)
    a = (dt * A).reshape(b, c, L, g, e)
    a_cum = jnp.cumsum(a, axis=2)
    Bc = Bm.reshape(b, c, L, g, n)
    Cc = Cm.reshape(b, c, L, g, n)
    seg = a_cum[:, :, :, None] - a_cum[:, :, None, :]
    causal = jnp.tril(jnp.ones((L, L), dtype=bool))[None, None, :, :, None, None]
    decay = jnp.exp(jnp.where(causal, seg, -jnp.inf))
    cb = jnp.einsum("bclgn,bcsgn->bclsg", Cc, Bc)
    w = cb[..., None] * decay
    y_diag = jnp.einsum("bclsge,bcsgep->bclgep", w, xd)
    decay_to_end = jnp.exp(a_cum[:, :, -1:] - a_cum)
    states = jnp.einsum("bclgn,bclge,bclgep->bcgepn", Bc, decay_to_end, xd)
    chunk_decay = jnp.exp(a_cum[:, :, -1])

    def step(state, inp):
        dec, new = inp
        return state * dec[..., None, None] + new, state

    init = jnp.zeros((b, g, e, p, n), jnp.float32)
    _, prev = lax.scan(step, init, (jnp.moveaxis(chunk_decay, 1, 0), jnp.moveaxis(states, 1, 0)))
    prev = jnp.moveaxis(prev, 0, 1)
    y_off = jnp.einsum("bclgn,bcgepn->bclgep", Cc, prev) * jnp.exp(a_cum)[..., None]
    y = (y_diag + y_off).reshape(b, c * L, h, p)
    return y[:, :s]


def mamba2_mixer(h, in_proj, conv_w, conv_b, dt_bias, A_log, D, gnorm, out_proj):
    b, s, _ = h.shape
    zxbcdt = h @ in_proj
    z, xBC, dt = jnp.split(zxbcdt, [SSM_INNER, SSM_INNER + SSM_CONV_DIM], axis=-1)
    xBC = jax.nn.silu(causal_dwconv(xBC, conv_w, conv_b))
    xs, Bm, Cm = jnp.split(xBC, [SSM_INNER, SSM_INNER + SSM_GROUPS * SSM_STATE], axis=-1)
    xs = xs.reshape(b, s, SSM_HEADS, SSM_HEAD_DIM).astype(jnp.float32)
    Bm = Bm.reshape(b, s, SSM_GROUPS, SSM_STATE).astype(jnp.float32)
    Cm = Cm.reshape(b, s, SSM_GROUPS, SSM_STATE).astype(jnp.float32)
    dt = jax.nn.softplus(dt.astype(jnp.float32) + dt_bias.astype(jnp.float32))
    A = -jnp.exp(A_log.astype(jnp.float32))
    y = ssd_chunked(xs, dt, A, Bm, Cm) + xs * D.astype(jnp.float32)[:, None]
    y = y.reshape(b, s, SSM_INNER).astype(h.dtype)
    y = group_rms_norm(y * jax.nn.silu(z), gnorm, SSM_GROUPS)
    return y @ out_proj


def sliding_window_sink_attention(q, k, v, sinks):
    b, s, _, d = q.shape
    nb = s // WINDOW
    qb = q.reshape(b, nb, WINDOW, ATT_KV_HEADS, ATT_GROUP, d)
    kb = k.reshape(b, nb, WINDOW, ATT_KV_HEADS, d)
    vb = v.reshape(b, nb, WINDOW, ATT_KV_HEADS, d)
    blk_pad = ((0, 0), (1, 0), (0, 0), (0, 0), (0, 0))
    k_band = jnp.concatenate([jnp.pad(kb, blk_pad)[:, :-1], kb], axis=2)
    v_band = jnp.concatenate([jnp.pad(vb, blk_pad)[:, :-1], vb], axis=2)
    scores = jnp.einsum("bnqhgd,bnkhd->bnhgqk", qb, k_band).astype(jnp.float32) * (d ** -0.5)
    qi = jnp.arange(WINDOW)[:, None]
    ki = jnp.arange(2 * WINDOW)[None, :]
    rel = qi + WINDOW - ki
    kpos = jnp.arange(nb)[:, None, None] * WINDOW + ki[None] - WINDOW
    mask = (rel >= 0)[None] & (rel < WINDOW)[None] & (kpos >= 0)
    scores = jnp.where(mask[None, :, None, None], scores, -jnp.inf)
    sink = jnp.broadcast_to(
        sinks.astype(jnp.float32).reshape(ATT_KV_HEADS, ATT_GROUP)[None, None, :, :, None, None],
        scores.shape[:-1] + (1,))
    probs = jax.nn.softmax(jnp.concatenate([scores, sink], axis=-1), axis=-1)[..., :-1]
    out = jnp.einsum("bnhgqk,bnkhd->bnqhgd", probs.astype(v.dtype), v_band)
    return out.reshape(b, s, ATT_Q_HEADS * d)


def conv_ffn(x, norm_g, w_in, conv_w, conv_b, w_down):
    h = rms_norm(x, norm_g)
    gate, val = jnp.split(h @ w_in, [FFN_DIM], axis=-1)
    gate = causal_dwconv(gate, conv_w, conv_b)
    return (jax.nn.silu(gate) * val) @ w_down


def _dense(key, shape, fan_in):
    return jax.random.normal(key, shape, jnp.float32) * (fan_in ** -0.5)


def _gain(key, shape):
    return 1.0 + 0.02 * jax.random.normal(key, shape, jnp.float32)


def _small(key, shape):
    return 0.02 * jax.random.normal(key, shape, jnp.float32)


def _fwd_setup_inputs(seed: int = 0) -> dict:
    key = jax.random.key(seed)
    ks = jax.random.split(key, 32)
    qkv_dim = ATT_KV_HEADS * ATT_HEAD_DIM
    q_dim = ATT_Q_HEADS * ATT_HEAD_DIM
    x = jax.random.normal(ks[0], (BATCH, SEQ, D_MODEL), jnp.float32)
    positions = (jnp.arange(SEQ, dtype=jnp.int32)[None, :]
                 + jax.random.randint(ks[1], (BATCH, 1), 0, 4096, dtype=jnp.int32))
    dt0 = jnp.exp(jax.random.uniform(ks[6], (N_A_LAYERS, SSM_HEADS), jnp.float32,
                                     minval=math.log(1e-3), maxval=math.log(1e-1)))
    return {
        "x": x,
        "positions": positions,
        "a_norm": _gain(ks[2], (N_A_LAYERS, D_MODEL)),
        "a_in_proj": _dense(ks[3], (N_A_LAYERS, D_MODEL, SSM_PROJ), D_MODEL),
        "a_conv_w": jax.random.normal(ks[4], (N_A_LAYERS, SSM_CONV, SSM_CONV_DIM), jnp.float32) * (SSM_CONV ** -0.5),
        "a_conv_b": _small(ks[5], (N_A_LAYERS, SSM_CONV_DIM)),
        "a_dt_bias": dt0 + jnp.log(-jnp.expm1(-dt0)),
        "a_A_log": jnp.log(jax.random.uniform(ks[7], (N_A_LAYERS, SSM_HEADS), jnp.float32, minval=1.0, maxval=16.0)),
        "a_D": 1.0 + 0.1 * jax.random.normal(ks[8], (N_A_LAYERS, SSM_HEADS), jnp.float32),
        "a_gnorm": _gain(ks[9], (N_A_LAYERS, SSM_INNER)),
        "a_out_proj": _dense(ks[10], (N_A_LAYERS, SSM_INNER, D_MODEL), SSM_INNER),
        "kv_norm": _gain(ks[11], (D_MODEL,)),
        "w_kv": _dense(ks[12], (D_MODEL, 2 * qkv_dim), D_MODEL),
        "b_kv": _small(ks[13], (2 * qkv_dim,)),
        "k_norm": _gain(ks[14], (ATT_HEAD_DIM,)),
        "b_norm": _gain(ks[15], (N_B_LAYERS, D_MODEL)),
        "w_q": _dense(ks[16], (N_B_LAYERS, D_MODEL, q_dim), D_MODEL),
        "b_q": _small(ks[17], (N_B_LAYERS, q_dim)),
        "q_norm": _gain(ks[18], (N_B_LAYERS, ATT_HEAD_DIM)),
        "sinks": jax.random.normal(ks[19], (N_B_LAYERS, ATT_Q_HEADS), jnp.float32),
        "w_o": _dense(ks[20], (N_B_LAYERS, q_dim, D_MODEL), q_dim),
        "b_o": _small(ks[21], (N_B_LAYERS, D_MODEL)),
        "f_norm": _gain(ks[22], (DEPTH, D_MODEL)),
        "f_w_in": _dense(ks[23], (DEPTH, D_MODEL, 2 * FFN_DIM), D_MODEL),
        "f_conv_w": jax.random.normal(ks[24], (DEPTH, FFN_CONV, FFN_DIM), jnp.float32) * (FFN_CONV ** -0.5),
        "f_conv_b": _small(ks[25], (DEPTH, FFN_DIM)),
        "f_w_down": _dense(ks[26], (DEPTH, FFN_DIM, D_MODEL), FFN_DIM),
    }


def _fwd_reference(x, positions, a_norm, a_in_proj, a_conv_w, a_conv_b, a_dt_bias, a_A_log, a_D,
              a_gnorm, a_out_proj, kv_norm, w_kv, b_kv, k_norm, b_norm, w_q, b_q, q_norm,
              sinks, w_o, b_o, f_norm, f_w_in, f_conv_w, f_conv_b, f_w_down):
    b, s, _ = x.shape
    k_shared = None
    v_shared = None
    for layer in range(DEPTH):
        if layer < N_A_LAYERS:
            i = layer
            h = rms_norm(x, a_norm[i])
            x = x + mamba2_mixer(h, a_in_proj[i], a_conv_w[i], a_conv_b[i], a_dt_bias[i],
                                 a_A_log[i], a_D[i], a_gnorm[i], a_out_proj[i])
        else:
            i = layer - N_A_LAYERS
            if i == 0:
                kv = rms_norm(x, kv_norm) @ w_kv + b_kv
                k_shared, v_shared = jnp.split(kv, 2, axis=-1)
                k_shared = k_shared.reshape(b, s, ATT_KV_HEADS, ATT_HEAD_DIM)
                v_shared = v_shared.reshape(b, s, ATT_KV_HEADS, ATT_HEAD_DIM)
                k_shared = rotary(rms_norm(k_shared, k_norm), positions)
            h = rms_norm(x, b_norm[i])
            q = (h @ w_q[i] + b_q[i]).reshape(b, s, ATT_Q_HEADS, ATT_HEAD_DIM)
            q = rotary(rms_norm(q, q_norm[i]), positions)
            att = sliding_window_sink_attention(q, k_shared, v_shared, sinks[i])
            x = x + att @ w_o[i] + b_o[i]
        x = x + conv_ffn(x, f_norm[layer], f_w_in[layer], f_conv_w[layer], f_conv_b[layer], f_w_down[layer])
    return x


import jax as _jax
import jax.numpy as _jnp

TWIN_FORMAT = 'train_step'
FWD_PARAMS = ['x', 'positions', 'a_norm', 'a_in_proj', 'a_conv_w', 'a_conv_b', 'a_dt_bias', 'a_A_log', 'a_D', 'a_gnorm', 'a_out_proj', 'kv_norm', 'w_kv', 'b_kv', 'k_norm', 'b_norm', 'w_q', 'b_q', 'q_norm', 'sinks', 'w_o', 'b_o', 'f_norm', 'f_w_in', 'f_conv_w', 'f_conv_b', 'f_w_down']
TWIN_WEIGHTS = ['a_norm', 'a_in_proj', 'a_conv_w', 'a_conv_b', 'a_dt_bias', 'a_A_log', 'a_D', 'a_gnorm', 'a_out_proj', 'kv_norm', 'w_kv', 'b_kv', 'k_norm', 'b_norm', 'w_q', 'b_q', 'q_norm', 'sinks', 'w_o', 'b_o', 'f_norm', 'f_w_in', 'f_conv_w', 'f_conv_b', 'f_w_down']
TWIN_DIFF_INPUT = 'x'
TWIN_INPUTS = ['x', 'positions', 'a_norm', 'a_in_proj', 'a_conv_w', 'a_conv_b', 'a_dt_bias', 'a_A_log', 'a_D', 'a_gnorm', 'a_out_proj', 'kv_norm', 'w_kv', 'b_kv', 'k_norm', 'b_norm', 'w_q', 'b_q', 'q_norm', 'sinks', 'w_o', 'b_o', 'f_norm', 'f_w_in', 'f_conv_w', 'f_conv_b', 'f_w_down', 'loss_target', 'm_a_norm', 'm_a_in_proj', 'm_a_conv_w', 'm_a_conv_b', 'm_a_dt_bias', 'm_a_A_log', 'm_a_D', 'm_a_gnorm', 'm_a_out_proj', 'm_kv_norm', 'm_w_kv', 'm_b_kv', 'm_k_norm', 'm_b_norm', 'm_w_q', 'm_b_q', 'm_q_norm', 'm_sinks', 'm_w_o', 'm_b_o', 'm_f_norm', 'm_f_w_in', 'm_f_conv_w', 'm_f_conv_b', 'm_f_w_down', 'v_a_norm', 'v_a_in_proj', 'v_a_conv_w', 'v_a_conv_b', 'v_a_dt_bias', 'v_a_A_log', 'v_a_D', 'v_a_gnorm', 'v_a_out_proj', 'v_kv_norm', 'v_w_kv', 'v_b_kv', 'v_k_norm', 'v_b_norm', 'v_w_q', 'v_b_q', 'v_q_norm', 'v_sinks', 'v_w_o', 'v_b_o', 'v_f_norm', 'v_f_w_in', 'v_f_conv_w', 'v_f_conv_b', 'v_f_w_down']
TWIN_OUTPUTS = ['loss', 'grad_x', 'grad_a_norm', 'grad_a_in_proj', 'grad_a_conv_w', 'grad_a_conv_b', 'grad_a_dt_bias', 'grad_a_A_log', 'grad_a_D', 'grad_a_gnorm', 'grad_a_out_proj', 'grad_kv_norm', 'grad_w_kv', 'grad_b_kv', 'grad_k_norm', 'grad_b_norm', 'grad_w_q', 'grad_b_q', 'grad_q_norm', 'grad_sinks', 'grad_w_o', 'grad_b_o', 'grad_f_norm', 'grad_f_w_in', 'grad_f_conv_w', 'grad_f_conv_b', 'grad_f_w_down', 'delta_a_norm', 'delta_a_in_proj', 'delta_a_conv_w', 'delta_a_conv_b', 'delta_a_dt_bias', 'delta_a_A_log', 'delta_a_D', 'delta_a_gnorm', 'delta_a_out_proj', 'delta_kv_norm', 'delta_w_kv', 'delta_b_kv', 'delta_k_norm', 'delta_b_norm', 'delta_w_q', 'delta_b_q', 'delta_q_norm', 'delta_sinks', 'delta_w_o', 'delta_b_o', 'delta_f_norm', 'delta_f_w_in', 'delta_f_conv_w', 'delta_f_conv_b', 'delta_f_w_down', 'new_m_a_norm', 'new_m_a_in_proj', 'new_m_a_conv_w', 'new_m_a_conv_b', 'new_m_a_dt_bias', 'new_m_a_A_log', 'new_m_a_D', 'new_m_a_gnorm', 'new_m_a_out_proj', 'new_m_kv_norm', 'new_m_w_kv', 'new_m_b_kv', 'new_m_k_norm', 'new_m_b_norm', 'new_m_w_q', 'new_m_b_q', 'new_m_q_norm', 'new_m_sinks', 'new_m_w_o', 'new_m_b_o', 'new_m_f_norm', 'new_m_f_w_in', 'new_m_f_conv_w', 'new_m_f_conv_b', 'new_m_f_w_down', 'new_v_a_norm', 'new_v_a_in_proj', 'new_v_a_conv_w', 'new_v_a_conv_b', 'new_v_a_dt_bias', 'new_v_a_A_log', 'new_v_a_D', 'new_v_a_gnorm', 'new_v_a_out_proj', 'new_v_kv_norm', 'new_v_w_kv', 'new_v_b_kv', 'new_v_k_norm', 'new_v_b_norm', 'new_v_w_q', 'new_v_b_q', 'new_v_q_norm', 'new_v_sinks', 'new_v_w_o', 'new_v_b_o', 'new_v_f_norm', 'new_v_f_w_in', 'new_v_f_conv_w', 'new_v_f_conv_b', 'new_v_f_w_down']
TWIN_LEAF_KINDS = {'loss': 'loss', 'grad_x': 'grad_x', 'grad_a_norm': 'grad_w', 'grad_a_in_proj': 'grad_w', 'grad_a_conv_w': 'grad_w', 'grad_a_conv_b': 'grad_w', 'grad_a_dt_bias': 'grad_w', 'grad_a_A_log': 'grad_w', 'grad_a_D': 'grad_w', 'grad_a_gnorm': 'grad_w', 'grad_a_out_proj': 'grad_w', 'grad_kv_norm': 'grad_w', 'grad_w_kv': 'grad_w', 'grad_b_kv': 'grad_w', 'grad_k_norm': 'grad_w', 'grad_b_norm': 'grad_w', 'grad_w_q': 'grad_w', 'grad_b_q': 'grad_w', 'grad_q_norm': 'grad_w', 'grad_sinks': 'grad_w', 'grad_w_o': 'grad_w', 'grad_b_o': 'grad_w', 'grad_f_norm': 'grad_w', 'grad_f_w_in': 'grad_w', 'grad_f_conv_w': 'grad_w', 'grad_f_conv_b': 'grad_w', 'grad_f_w_down': 'grad_w', 'delta_a_norm': 'delta_w', 'delta_a_in_proj': 'delta_w', 'delta_a_conv_w': 'delta_w', 'delta_a_conv_b': 'delta_w', 'delta_a_dt_bias': 'delta_w', 'delta_a_A_log': 'delta_w', 'delta_a_D': 'delta_w', 'delta_a_gnorm': 'delta_w', 'delta_a_out_proj': 'delta_w', 'delta_kv_norm': 'delta_w', 'delta_w_kv': 'delta_w', 'delta_b_kv': 'delta_w', 'delta_k_norm': 'delta_w', 'delta_b_norm': 'delta_w', 'delta_w_q': 'delta_w', 'delta_b_q': 'delta_w', 'delta_q_norm': 'delta_w', 'delta_sinks': 'delta_w', 'delta_w_o': 'delta_w', 'delta_b_o': 'delta_w', 'delta_f_norm': 'delta_w', 'delta_f_w_in': 'delta_w', 'delta_f_conv_w': 'delta_w', 'delta_f_conv_b': 'delta_w', 'delta_f_w_down': 'delta_w', 'new_m_a_norm': 'new_m', 'new_m_a_in_proj': 'new_m', 'new_m_a_conv_w': 'new_m', 'new_m_a_conv_b': 'new_m', 'new_m_a_dt_bias': 'new_m', 'new_m_a_A_log': 'new_m', 'new_m_a_D': 'new_m', 'new_m_a_gnorm': 'new_m', 'new_m_a_out_proj': 'new_m', 'new_m_kv_norm': 'new_m', 'new_m_w_kv': 'new_m', 'new_m_b_kv': 'new_m', 'new_m_k_norm': 'new_m', 'new_m_b_norm': 'new_m', 'new_m_w_q': 'new_m', 'new_m_b_q': 'new_m', 'new_m_q_norm': 'new_m', 'new_m_sinks': 'new_m', 'new_m_w_o': 'new_m', 'new_m_b_o': 'new_m', 'new_m_f_norm': 'new_m', 'new_m_f_w_in': 'new_m', 'new_m_f_conv_w': 'new_m', 'new_m_f_conv_b': 'new_m', 'new_m_f_w_down': 'new_m', 'new_v_a_norm': 'new_v', 'new_v_a_in_proj': 'new_v', 'new_v_a_conv_w': 'new_v', 'new_v_a_conv_b': 'new_v', 'new_v_a_dt_bias': 'new_v', 'new_v_a_A_log': 'new_v', 'new_v_a_D': 'new_v', 'new_v_a_gnorm': 'new_v', 'new_v_a_out_proj': 'new_v', 'new_v_kv_norm': 'new_v', 'new_v_w_kv': 'new_v', 'new_v_b_kv': 'new_v', 'new_v_k_norm': 'new_v', 'new_v_b_norm': 'new_v', 'new_v_w_q': 'new_v', 'new_v_b_q': 'new_v', 'new_v_q_norm': 'new_v', 'new_v_sinks': 'new_v', 'new_v_w_o': 'new_v', 'new_v_b_o': 'new_v', 'new_v_f_norm': 'new_v', 'new_v_f_w_in': 'new_v', 'new_v_f_conv_w': 'new_v', 'new_v_f_conv_b': 'new_v', 'new_v_f_w_down': 'new_v'}


def _forward(args):
    return _fwd_reference(*[args[k] for k in FWD_PARAMS])


def _output_shape():
    out = _jax.eval_shape(lambda: _forward(_fwd_setup_inputs(0)))
    return out.shape, out.dtype

N_MICROBATCH = 1
ADAM_LR = 0.001
ADAM_B1 = 0.9
ADAM_B2 = 0.999
ADAM_EPS = 1e-08
ADAM_WD = 0.01
ADAM_STEP = 10
PER_EXAMPLE_BATCH_AXIS = {'x': 0, 'positions': 0, 'loss_target': 0}
SHARED_INPUTS = []
_WEIGHT_DTYPES = {'a_norm': _jnp.float32, 'a_in_proj': _jnp.float32, 'a_conv_w': _jnp.float32, 'a_conv_b': _jnp.float32, 'a_dt_bias': _jnp.float32, 'a_A_log': _jnp.float32, 'a_D': _jnp.float32, 'a_gnorm': _jnp.float32, 'a_out_proj': _jnp.float32, 'kv_norm': _jnp.float32, 'w_kv': _jnp.float32, 'b_kv': _jnp.float32, 'k_norm': _jnp.float32, 'b_norm': _jnp.float32, 'w_q': _jnp.float32, 'b_q': _jnp.float32, 'q_norm': _jnp.float32, 'sinks': _jnp.float32, 'w_o': _jnp.float32, 'b_o': _jnp.float32, 'f_norm': _jnp.float32, 'f_w_in': _jnp.float32, 'f_conv_w': _jnp.float32, 'f_conv_b': _jnp.float32, 'f_w_down': _jnp.float32}
MOMENT_SCALE = {'a_norm': 6.236453e-01, 'a_in_proj': 2.431538e-01, 'a_conv_w': 4.533994e-01, 'a_conv_b': 1.544207e+00, 'a_dt_bias': 1.585285e+00, 'a_A_log': 2.597188e+00, 'a_D': 2.980187e+00, 'a_gnorm': 1.136474e+01, 'a_out_proj': 1.407523e+00, 'kv_norm': 8.993942e-01, 'w_kv': 9.880826e-01, 'b_kv': 7.933231e+00, 'k_norm': 3.498457e+00, 'b_norm': 5.394586e-02, 'w_q': 5.605092e-02, 'b_q': 9.291215e-02, 'q_norm': 3.483896e+00, 'sinks': 6.446992e-01, 'w_o': 5.805461e-01, 'b_o': 4.586302e+00, 'f_norm': 1.303949e+01, 'f_w_in': 2.776719e-01, 'f_conv_w': 1.513549e+00, 'f_conv_b': 1.830343e+00, 'f_w_down': 3.004515e-01}


def _to_microbatches(a, axis):
    t = _jnp.moveaxis(a, axis, 0)
    t = t.reshape((N_MICROBATCH, t.shape[0] // N_MICROBATCH) + t.shape[1:])
    return _jnp.moveaxis(t, 1, axis + 1)


def setup_inputs(seed: int = 0) -> dict:
    inp = _fwd_setup_inputs(seed)
    key = _jax.random.fold_in(_jax.random.key(seed), 7919)
    shape, _ = _output_shape()
    out = dict(inp)
    out["loss_target"] = _jax.random.normal(_jax.random.fold_in(key, 0), shape, _jnp.float32)
    for i, name in enumerate(TWIN_WEIGHTS):
        w = inp[name].astype(_jnp.float32)
        if MOMENT_SCALE is None:
            s = _jnp.sqrt(_jnp.mean(_jnp.square(w)) + 1e-30)
        else:
            s = MOMENT_SCALE[name]
        km, kv = _jax.random.split(_jax.random.fold_in(key, i + 1))
        out[name] = w
        out["m_" + name] = s * _jax.random.normal(km, w.shape, _jnp.float32)
        out["v_" + name] = (s * s) * _jax.random.uniform(kv, w.shape, _jnp.float32, 0.5, 1.5)
    if N_MICROBATCH > 1:
        for name, axis in PER_EXAMPLE_BATCH_AXIS.items():
            out[name] = _to_microbatches(out[name], axis)
    return {'x': out['x'], 'positions': out['positions'], 'a_norm': out['a_norm'], 'a_in_proj': out['a_in_proj'], 'a_conv_w': out['a_conv_w'], 'a_conv_b': out['a_conv_b'], 'a_dt_bias': out['a_dt_bias'], 'a_A_log': out['a_A_log'], 'a_D': out['a_D'], 'a_gnorm': out['a_gnorm'], 'a_out_proj': out['a_out_proj'], 'kv_norm': out['kv_norm'], 'w_kv': out['w_kv'], 'b_kv': out['b_kv'], 'k_norm': out['k_norm'], 'b_norm': out['b_norm'], 'w_q': out['w_q'], 'b_q': out['b_q'], 'q_norm': out['q_norm'], 'sinks': out['sinks'], 'w_o': out['w_o'], 'b_o': out['b_o'], 'f_norm': out['f_norm'], 'f_w_in': out['f_w_in'], 'f_conv_w': out['f_conv_w'], 'f_conv_b': out['f_conv_b'], 'f_w_down': out['f_w_down'], 'loss_target': out['loss_target'], 'm_a_norm': out['m_a_norm'], 'm_a_in_proj': out['m_a_in_proj'], 'm_a_conv_w': out['m_a_conv_w'], 'm_a_conv_b': out['m_a_conv_b'], 'm_a_dt_bias': out['m_a_dt_bias'], 'm_a_A_log': out['m_a_A_log'], 'm_a_D': out['m_a_D'], 'm_a_gnorm': out['m_a_gnorm'], 'm_a_out_proj': out['m_a_out_proj'], 'm_kv_norm': out['m_kv_norm'], 'm_w_kv': out['m_w_kv'], 'm_b_kv': out['m_b_kv'], 'm_k_norm': out['m_k_norm'], 'm_b_norm': out['m_b_norm'], 'm_w_q': out['m_w_q'], 'm_b_q': out['m_b_q'], 'm_q_norm': out['m_q_norm'], 'm_sinks': out['m_sinks'], 'm_w_o': out['m_w_o'], 'm_b_o': out['m_b_o'], 'm_f_norm': out['m_f_norm'], 'm_f_w_in': out['m_f_w_in'], 'm_f_conv_w': out['m_f_conv_w'], 'm_f_conv_b': out['m_f_conv_b'], 'm_f_w_down': out['m_f_w_down'], 'v_a_norm': out['v_a_norm'], 'v_a_in_proj': out['v_a_in_proj'], 'v_a_conv_w': out['v_a_conv_w'], 'v_a_conv_b': out['v_a_conv_b'], 'v_a_dt_bias': out['v_a_dt_bias'], 'v_a_A_log': out['v_a_A_log'], 'v_a_D': out['v_a_D'], 'v_a_gnorm': out['v_a_gnorm'], 'v_a_out_proj': out['v_a_out_proj'], 'v_kv_norm': out['v_kv_norm'], 'v_w_kv': out['v_w_kv'], 'v_b_kv': out['v_b_kv'], 'v_k_norm': out['v_k_norm'], 'v_b_norm': out['v_b_norm'], 'v_w_q': out['v_w_q'], 'v_b_q': out['v_b_q'], 'v_q_norm': out['v_q_norm'], 'v_sinks': out['v_sinks'], 'v_w_o': out['v_w_o'], 'v_b_o': out['v_b_o'], 'v_f_norm': out['v_f_norm'], 'v_f_w_in': out['v_f_w_in'], 'v_f_conv_w': out['v_f_conv_w'], 'v_f_conv_b': out['v_f_conv_b'], 'v_f_w_down': out['v_f_w_down']}


def _loss(weights, diff, rest, loss_target):
    with _jax.named_scope("forward"):
        args = {**rest, TWIN_DIFF_INPUT: diff, **{k: w.astype(_WEIGHT_DTYPES[k]) for k, w in weights.items()}}
        y = _forward(args)
    with _jax.named_scope("loss_head"):
        err = _jnp.square(y.astype(_jnp.float32) - loss_target)
        return 0.5 * _jnp.sum(_jnp.mean(err, axis=-1)) if err.ndim else 0.5 * err


def _adamw(w, g, m, v):
    m = ADAM_B1 * m + (1.0 - ADAM_B1) * g
    v = ADAM_B2 * v + (1.0 - ADAM_B2) * _jnp.square(g)
    m_hat = m / (1.0 - ADAM_B1 ** ADAM_STEP)
    v_hat = v / (1.0 - ADAM_B2 ** ADAM_STEP)
    delta = -ADAM_LR * (m_hat / (_jnp.sqrt(v_hat) + ADAM_EPS) + ADAM_WD * w)
    return delta, m, v


def reference(x, positions, a_norm, a_in_proj, a_conv_w, a_conv_b, a_dt_bias, a_A_log, a_D, a_gnorm, a_out_proj, kv_norm, w_kv, b_kv, k_norm, b_norm, w_q, b_q, q_norm, sinks, w_o, b_o, f_norm, f_w_in, f_conv_w, f_conv_b, f_w_down, loss_target, m_a_norm, m_a_in_proj, m_a_conv_w, m_a_conv_b, m_a_dt_bias, m_a_A_log, m_a_D, m_a_gnorm, m_a_out_proj, m_kv_norm, m_w_kv, m_b_kv, m_k_norm, m_b_norm, m_w_q, m_b_q, m_q_norm, m_sinks, m_w_o, m_b_o, m_f_norm, m_f_w_in, m_f_conv_w, m_f_conv_b, m_f_w_down, v_a_norm, v_a_in_proj, v_a_conv_w, v_a_conv_b, v_a_dt_bias, v_a_A_log, v_a_D, v_a_gnorm, v_a_out_proj, v_kv_norm, v_w_kv, v_b_kv, v_k_norm, v_b_norm, v_w_q, v_b_q, v_q_norm, v_sinks, v_w_o, v_b_o, v_f_norm, v_f_w_in, v_f_conv_w, v_f_conv_b, v_f_w_down):
    given = dict(x=x, positions=positions, a_norm=a_norm, a_in_proj=a_in_proj, a_conv_w=a_conv_w, a_conv_b=a_conv_b, a_dt_bias=a_dt_bias, a_A_log=a_A_log, a_D=a_D, a_gnorm=a_gnorm, a_out_proj=a_out_proj, kv_norm=kv_norm, w_kv=w_kv, b_kv=b_kv, k_norm=k_norm, b_norm=b_norm, w_q=w_q, b_q=b_q, q_norm=q_norm, sinks=sinks, w_o=w_o, b_o=b_o, f_norm=f_norm, f_w_in=f_w_in, f_conv_w=f_conv_w, f_conv_b=f_conv_b, f_w_down=f_w_down, loss_target=loss_target, m_a_norm=m_a_norm, m_a_in_proj=m_a_in_proj, m_a_conv_w=m_a_conv_w, m_a_conv_b=m_a_conv_b, m_a_dt_bias=m_a_dt_bias, m_a_A_log=m_a_A_log, m_a_D=m_a_D, m_a_gnorm=m_a_gnorm, m_a_out_proj=m_a_out_proj, m_kv_norm=m_kv_norm, m_w_kv=m_w_kv, m_b_kv=m_b_kv, m_k_norm=m_k_norm, m_b_norm=m_b_norm, m_w_q=m_w_q, m_b_q=m_b_q, m_q_norm=m_q_norm, m_sinks=m_sinks, m_w_o=m_w_o, m_b_o=m_b_o, m_f_norm=m_f_norm, m_f_w_in=m_f_w_in, m_f_conv_w=m_f_conv_w, m_f_conv_b=m_f_conv_b, m_f_w_down=m_f_w_down, v_a_norm=v_a_norm, v_a_in_proj=v_a_in_proj, v_a_conv_w=v_a_conv_w, v_a_conv_b=v_a_conv_b, v_a_dt_bias=v_a_dt_bias, v_a_A_log=v_a_A_log, v_a_D=v_a_D, v_a_gnorm=v_a_gnorm, v_a_out_proj=v_a_out_proj, v_kv_norm=v_kv_norm, v_w_kv=v_w_kv, v_b_kv=v_b_kv, v_k_norm=v_k_norm, v_b_norm=v_b_norm, v_w_q=v_w_q, v_b_q=v_b_q, v_q_norm=v_q_norm, v_sinks=v_sinks, v_w_o=v_w_o, v_b_o=v_b_o, v_f_norm=v_f_norm, v_f_w_in=v_f_w_in, v_f_conv_w=v_f_conv_w, v_f_conv_b=v_f_conv_b, v_f_w_down=v_f_w_down)
    weights = {n: given[n] for n in TWIN_WEIGHTS}
    shared = {n: given[n] for n in SHARED_INPUTS}
    per_example = {n: given[n] for n in ['x', 'positions']}
    grad_fn = _jax.value_and_grad(_loss, argnums=(0, 1))

    def one_microbatch(ex, loss_target):
        ex = dict(ex)
        diff = ex.pop(TWIN_DIFF_INPUT)
        return grad_fn(weights, diff, {**shared, **ex}, loss_target)

    if N_MICROBATCH == 1:
        loss, (grad_w, grad_x) = one_microbatch(per_example, given["loss_target"])
    else:
        def body(carry, xs):
            loss_sum, grad_sum = carry
            l_k, (gw_k, gx_k) = one_microbatch(xs[0], xs[1])
            with _jax.named_scope("update"):
                return (loss_sum + l_k, _jax.tree.map(_jnp.add, grad_sum, gw_k)), gx_k

        init = (_jnp.zeros((), _jnp.float32), _jax.tree.map(_jnp.zeros_like, weights))
        (loss, grad_w), grad_x = _jax.lax.scan(body, init, (per_example, given["loss_target"]))
    with _jax.named_scope("update"):
        delta_w, new_m, new_v = {}, {}, {}
        for n in TWIN_WEIGHTS:
            delta_w[n], new_m[n], new_v[n] = _adamw(weights[n], grad_w[n], given["m_" + n], given["v_" + n])
    return (loss, grad_x, *[grad_w[n] for n in TWIN_WEIGHTS], *[delta_w[n] for n in TWIN_WEIGHTS],
            *[new_m[n] for n in TWIN_WEIGHTS], *[new_v[n] for n in TWIN_WEIGHTS])
```

```python
import functools
import math

import jax
import jax.numpy as jnp
from jax import lax
from jax.experimental import pallas as pl
from jax.experimental.pallas import tpu as pltpu

F32 = jnp.float32
BF16 = jnp.bfloat16

EPS = 1e-5
CHUNK = 256
WINDOW = 128
HEAD = 64
SSM_HEADS = 32
SSM_GROUPS = 8
SSM_STATE = 128
ATT_KV = 4
ATT_G = 4
ROPE_THETA = 10000.0
NEG = -1e30
VMEM_LIMIT = 56 * 1024 * 1024

ADAM_LR, ADAM_B1, ADAM_B2, ADAM_EPS, ADAM_WD, ADAM_STEP = 0.001, 0.9, 0.999, 1e-08, 0.01, 10


def _cp(n_axes):
    return pltpu.CompilerParams(dimension_semantics=("arbitrary",) * n_axes, vmem_limit_bytes=VMEM_LIMIT)


def _pick(dim, prefs):
    for p in prefs:
        if dim % p == 0:
            return p
    return dim


def _iota(shape, dim):
    return lax.broadcasted_iota(jnp.int32, shape, dim)


def _dot(a, b, ca=1, cb=0):
    return lax.dot_general(a, b, (((ca,), (cb,)), ((), ())), preferred_element_type=F32)


def _dot3(x, ind):
    h = x.astype(BF16)
    r = x - h.astype(F32)
    m = r.astype(BF16)
    lo = (r - m.astype(F32)).astype(BF16)
    return _dot(h, ind) + _dot(m, ind) + _dot(lo, ind)


def _sigmoid(x):
    return jax.nn.sigmoid(x)


def _mm(a, b, *, name, ta=False, tb=False, bias=None, res=None, out_dtype=F32, b_koff=0, tm=None, tn=None, tk=None):
    if ta:
        K, M = a.shape
    else:
        M, K = a.shape
    N = b.shape[0] if tb else b.shape[1]
    tm = tm or _pick(M, (1024, 512, 256, 128))
    tn = tn or _pick(N, (512, 256, 128))
    tk = tk or (K if K <= 2048 else _pick(K, (2048, 1408, 1024, 512)))
    assert M % tm == 0 and N % tn == 0 and K % tk == 0 and b_koff % tk == 0
    nk = K // tk
    kb0 = b_koff // tk
    has_bias, has_res = bias is not None, res is not None

    def body(*refs):
        a_ref, b_ref = refs[0], refs[1]
        pos = 2
        bias_ref = res_ref = acc_ref = None
        if has_bias:
            bias_ref = refs[pos]
            pos += 1
        if has_res:
            res_ref = refs[pos]
            pos += 1
        o_ref = refs[pos]
        if nk > 1:
            acc_ref = refs[pos + 1]
        part = _dot(a_ref[...].astype(BF16), b_ref[...].astype(BF16), 0 if ta else 1, 1 if tb else 0)

        def finish(acc):
            if has_bias:
                acc = acc + bias_ref[...]
            if has_res:
                acc = acc + res_ref[...]
            o_ref[...] = acc.astype(out_dtype)

        if nk == 1:
            finish(part)
        else:
            k = pl.program_id(2)

            @pl.when(k == 0)
            def _():
                acc_ref[...] = part

            @pl.when(k > 0)
            def _():
                acc_ref[...] += part

            @pl.when(k == nk - 1)
            def _():
                finish(acc_ref[...])

    a_spec = pl.BlockSpec((tk, tm), lambda i, j, k: (k, i)) if ta else pl.BlockSpec((tm, tk), lambda i, j, k: (i, k))
    b_spec = (pl.BlockSpec((tn, tk), lambda i, j, k: (j, k + kb0)) if tb
              else pl.BlockSpec((tk, tn), lambda i, j, k: (k + kb0, j)))
    in_specs, args = [a_spec, b_spec], [a, b]
    if has_bias:
        in_specs.append(pl.BlockSpec((1, tn), lambda i, j, k: (0, j)))
        args.append(bias)
    if has_res:
        in_specs.append(pl.BlockSpec((tm, tn), lambda i, j, k: (i, j)))
        args.append(res)
    return pl.pallas_call(
        body, name=name, grid=(M // tm, N // tn, nk), in_specs=in_specs,
        out_specs=pl.BlockSpec((tm, tn), lambda i, j, k: (i, j)),
        out_shape=jax.ShapeDtypeStruct((M, N), out_dtype),
        scratch_shapes=[pltpu.VMEM((tm, tn), F32)] if nk > 1 else [],
        compiler_params=_cp(3),
    )(*args)


def _rms_fwd(x, gains, *, name, tr=256):
    S, D = x.shape
    n = len(gains)

    def body(*refs):
        xv = refs[0][...]
        xh = xv * lax.rsqrt(jnp.mean(xv * xv, axis=-1, keepdims=True) + EPS)
        for q in range(n):
            refs[1 + n + q][...] = (xh * refs[1 + q][...]).astype(BF16)

    row = pl.BlockSpec((tr, D), lambda i: (i, 0))
    vec = pl.BlockSpec((1, D), lambda i: (0, 0))
    return pl.pallas_call(
        body, name=name, grid=(S // tr,), in_specs=[row] + [vec] * n, out_specs=[row] * n,
        out_shape=[jax.ShapeDtypeStruct((S, D), BF16)] * n, compiler_params=_cp(1),
    )(x, *gains)


def _rms_bwd(x, gains, dhs, dres, *, name, tr=256, want_colsum=False):
    S, D = x.shape
    n = len(gains)
    steps = S // tr

    def body(*refs):
        x_ref = refs[0]
        g_refs = refs[1:1 + n]
        dh_refs = refs[1 + n:1 + 2 * n]
        dres_ref = refs[1 + 2 * n]
        dx_ref = refs[2 + 2 * n]
        dg_refs = refs[3 + 2 * n:3 + 3 * n]
        cs_ref = refs[3 + 3 * n] if want_colsum else None
        i = pl.program_id(0)
        xv = x_ref[...]
        r = lax.rsqrt(jnp.mean(xv * xv, axis=-1, keepdims=True) + EPS)
        xh = xv * r
        dx = dres_ref[...]
        for q in range(n):
            dh = dh_refs[q][...]
            dxh = dh * g_refs[q][...]
            dx = dx + r * (dxh - xh * jnp.mean(dxh * xh, axis=-1, keepdims=True))
            part = jnp.sum(dh * xh, axis=0, keepdims=True)

            @pl.when(i == 0)
            def _():
                dg_refs[q][...] = part

            @pl.when(i > 0)
            def _():
                dg_refs[q][...] += part

        dx_ref[...] = dx
        if want_colsum:
            cpart = jnp.sum(dx, axis=0, keepdims=True)

            @pl.when(i == 0)
            def _():
                cs_ref[...] = cpart

            @pl.when(i > 0)
            def _():
                cs_ref[...] += cpart

    row = pl.BlockSpec((tr, D), lambda i: (i, 0))
    vec = pl.BlockSpec((1, D), lambda i: (0, 0))
    n_vec_out = n + (1 if want_colsum else 0)
    outs = pl.pallas_call(
        body, name=name, grid=(steps,), in_specs=[row] + [vec] * n + [row] * n + [row],
        out_specs=[row] + [vec] * n_vec_out,
        out_shape=[jax.ShapeDtypeStruct((S, D), F32)] + [jax.ShapeDtypeStruct((1, D), F32)] * n_vec_out,
        compiler_params=_cp(1),
    )(x, *gains, *dhs, dres)
    return outs


def _colsum(x, *, name, tr=256):
    S, D = x.shape

    def body(x_ref, o_ref):
        i = pl.program_id(0)
        part = jnp.sum(x_ref[...].astype(F32), axis=0, keepdims=True)

        @pl.when(i == 0)
        def _():
            o_ref[...] = part

        @pl.when(i > 0)
        def _():
            o_ref[...] += part

    return pl.pallas_call(
        body, name=name, grid=(S // tr,), in_specs=[pl.BlockSpec((tr, D), lambda i: (i, 0))],
        out_specs=pl.BlockSpec((1, D), lambda i: (0, 0)), out_shape=jax.ShapeDtypeStruct((1, D), F32),
        compiler_params=_cp(1),
    )(x)


def _loss(y, t, *, name="loss", tr=256):
    S, D = y.shape
    steps = S // tr

    def body(y_ref, t_ref, dy_ref, l_ref, acc_ref):
        i = pl.program_id(0)
        e = y_ref[...] - t_ref[...]
        dy_ref[...] = e * (1.0 / D)
        part = jnp.sum(e * e, axis=0, keepdims=True)

        @pl.when(i == 0)
        def _():
            acc_ref[...] = part

        @pl.when(i > 0)
        def _():
            acc_ref[...] += part

        @pl.when(i == steps - 1)
        def _():
            tot = jnp.sum(acc_ref[...], axis=1, keepdims=True) * (0.5 / D)
            l_ref[...] = jnp.broadcast_to(tot, (8, 128))

    row = pl.BlockSpec((tr, D), lambda i: (i, 0))
    return pl.pallas_call(
        body, name=name, grid=(steps,), in_specs=[row, row],
        out_specs=[row, pl.BlockSpec((8, 128), lambda i: (0, 0))],
        out_shape=[jax.ShapeDtypeStruct((S, D), F32), jax.ShapeDtypeStruct((8, 128), F32)],
        scratch_shapes=[pltpu.VMEM((1, D), F32)], compiler_params=_cp(1),
    )(y, t)


def _conv_pre(x, w_ref, b_ref, width):
    S = x.shape[0]
    row = _iota(x.shape, 0)
    acc = b_ref[...] + w_ref[pl.ds(width - 1, 1), :] * x
    shifted = []
    for s in range(1, width):
        xs = jnp.where(row >= s, pltpu.roll(x, s, axis=0), 0.0)
        shifted.append(xs)
        acc = acc + w_ref[pl.ds(width - 1 - s, 1), :] * xs
    return acc, shifted


def _conv_back(dacc, x, shifted, w_ref, width):
    S = x.shape[0]
    row = _iota(x.shape, 0)
    dx = w_ref[pl.ds(width - 1, 1), :] * dacc
    dws = [None] * width
    dws[width - 1] = jnp.sum(dacc * x, axis=0, keepdims=True)
    for s in range(1, width):
        back = jnp.where(row < S - s, pltpu.roll(dacc, S - s, axis=0), 0.0)
        dx = dx + w_ref[pl.ds(width - 1 - s, 1), :] * back
        dws[width - 1 - s] = jnp.sum(dacc * shifted[s - 1], axis=0, keepdims=True)
    db = jnp.sum(dacc, axis=0, keepdims=True)
    return dx, dws, db


def _conv_silu_fwd(xin, col0, C, w, b, *, name, tc=512):
    S = xin.shape[0]
    width = w.shape[0]
    off = col0 // tc

    def body(x_ref, w_ref, b_ref, o_ref):
        acc, _ = _conv_pre(x_ref[...], w_ref, b_ref, width)
        o_ref[...] = acc * _sigmoid(acc)

    return pl.pallas_call(
        body, name=name, grid=(C // tc,),
        in_specs=[pl.BlockSpec((S, tc), lambda j: (0, j + off)), pl.BlockSpec((width, tc), lambda j: (0, j)),
                  pl.BlockSpec((1, tc), lambda j: (0, j))],
        out_specs=pl.BlockSpec((S, tc), lambda j: (0, j)), out_shape=jax.ShapeDtypeStruct((S, C), F32),
        compiler_params=_cp(1),
    )(xin, w, b)


def _conv_silu_bwd(xin, col0, C, w, b, douts, *, name, tc=256):
    S = xin.shape[0]
    width = w.shape[0]
    off = col0 // tc
    nd = len(douts)
    ranges = [(o // tc, (o + d.shape[1]) // tc) for d, o in douts]

    def body(*refs):
        x_ref, w_ref, b_ref = refs[0], refs[1], refs[2]
        d_refs = refs[3:3 + nd]
        dx_ref, dw_ref, db_ref = refs[3 + nd], refs[4 + nd], refs[5 + nd]
        j = pl.program_id(0)
        x = x_ref[...]
        acc, shifted = _conv_pre(x, w_ref, b_ref, width)
        sg = _sigmoid(acc)
        dout = jnp.zeros_like(x)
        for q in range(nd):
            lo, hi = ranges[q]
            dout = dout + jnp.where((j >= lo) & (j < hi), d_refs[q][...], 0.0)
        dacc = dout * (sg * (1.0 + acc * (1.0 - sg)))
        dx, dws, db = _conv_back(dacc, x, shifted, w_ref, width)
        dx_ref[...] = dx.astype(BF16)
        for k in range(width):
            dw_ref[pl.ds(k, 1), :] = dws[k]
        db_ref[...] = db

    d_specs = [pl.BlockSpec((S, tc), (lambda j, lo=lo, hi=hi: (0, jnp.clip(j - lo, 0, hi - lo - 1)))) for lo, hi in ranges]
    return pl.pallas_call(
        body, name=name, grid=(C // tc,),
        in_specs=[pl.BlockSpec((S, tc), lambda j: (0, j + off)), pl.BlockSpec((width, tc), lambda j: (0, j)),
                  pl.BlockSpec((1, tc), lambda j: (0, j))] + d_specs,
        out_specs=[pl.BlockSpec((S, tc), lambda j: (0, j)), pl.BlockSpec((width, tc), lambda j: (0, j)),
                   pl.BlockSpec((1, tc), lambda j: (0, j))],
        out_shape=[jax.ShapeDtypeStruct((S, C), BF16), jax.ShapeDtypeStruct((width, C), F32),
                   jax.ShapeDtypeStruct((1, C), F32)],
        compiler_params=_cp(1),
    )(xin, w, b, *[d for d, _ in douts])


def _ffn_act_fwd(u, w, b, *, name, tc=256):
    S, F2 = u.shape
    Fd = F2 // 2
    width = w.shape[0]
    nb = Fd // tc

    def body(g_ref, v_ref, w_ref, b_ref, o_ref):
        acc, _ = _conv_pre(g_ref[...], w_ref, b_ref, width)
        o_ref[...] = (acc * _sigmoid(acc) * v_ref[...]).astype(BF16)

    return pl.pallas_call(
        body, name=name, grid=(nb,),
        in_specs=[pl.BlockSpec((S, tc), lambda j: (0, j)), pl.BlockSpec((S, tc), lambda j: (0, j + nb)),
                  pl.BlockSpec((width, tc), lambda j: (0, j)), pl.BlockSpec((1, tc), lambda j: (0, j))],
        out_specs=pl.BlockSpec((S, tc), lambda j: (0, j)), out_shape=jax.ShapeDtypeStruct((S, Fd), BF16),
        compiler_params=_cp(1),
    )(u, u, w, b)


def _ffn_act_bwd(u, w, b, da, *, name, tc=256):
    S, F2 = u.shape
    Fd = F2 // 2
    width = w.shape[0]
    nb = Fd // tc

    def body(g_ref, v_ref, w_ref, b_ref, da_ref, dg_ref, dv_ref, dw_ref, db_ref):
        x = g_ref[...]
        acc, shifted = _conv_pre(x, w_ref, b_ref, width)
        sg = _sigmoid(acc)
        dav = da_ref[...]
        dv_ref[...] = (dav * acc * sg).astype(BF16)
        dacc = dav * v_ref[...] * (sg * (1.0 + acc * (1.0 - sg)))
        dx, dws, db = _conv_back(dacc, x, shifted, w_ref, width)
        dg_ref[...] = dx.astype(BF16)
        for k in range(width):
            dw_ref[pl.ds(k, 1), :] = dws[k]
        db_ref[...] = db

    blk = pl.BlockSpec((S, tc), lambda j: (0, j))
    return pl.pallas_call(
        body, name=name, grid=(nb,),
        in_specs=[blk, pl.BlockSpec((S, tc), lambda j: (0, j + nb)), pl.BlockSpec((width, tc), lambda j: (0, j)),
                  pl.BlockSpec((1, tc), lambda j: (0, j)), blk],
        out_specs=[blk, blk, pl.BlockSpec((width, tc), lambda j: (0, j)), pl.BlockSpec((1, tc), lambda j: (0, j))],
        out_shape=[jax.ShapeDtypeStruct((S, Fd), BF16), jax.ShapeDtypeStruct((S, Fd), BF16),
                   jax.ShapeDtypeStruct((width, Fd), F32), jax.ShapeDtypeStruct((1, Fd), F32)],
        compiler_params=_cp(1),
    )(u, u, w, b, da)


def _ssd_prep(dtr, dt_bias, a_log, *, name="ssd_prep"):
    S = dtr.shape[0]

    def body(d_ref, b_ref, al_ref, dt_ref, ac_ref, sg_ref):
        lane = _iota((CHUNK, 128), 1)
        valid = lane < SSM_HEADS
        z = d_ref[...] + b_ref[...]
        dt = jnp.where(valid, jnp.maximum(z, 0.0) + jnp.log(1.0 + jnp.exp(-jnp.abs(z))), 0.0)
        a = dt * (-jnp.exp(al_ref[...]))
        row = _iota((CHUNK, 128), 0)
        k = 1
        while k < CHUNK:
            a = a + jnp.where(row >= k, pltpu.roll(a, k, axis=0), 0.0)
            k *= 2
        dt_ref[...] = dt
        ac_ref[...] = a
        sg_ref[...] = jnp.where(valid, _sigmoid(z), 0.0)

    blk = pl.BlockSpec((CHUNK, 128), lambda i: (i, 0))
    vec = pl.BlockSpec((1, 128), lambda i: (0, 0))
    return pl.pallas_call(
        body, name=name, grid=(S // CHUNK,), in_specs=[blk, vec, vec], out_specs=[blk, blk, blk],
        out_shape=[jax.ShapeDtypeStruct((S, 128), F32)] * 3, compiler_params=_cp(1),
    )(dtr, dt_bias, a_log)


def _to_groups(v):
    S = v.shape[0]
    g = v[:, :SSM_HEADS].reshape(S, SSM_GROUPS, 4).transpose(1, 0, 2)
    return jnp.pad(g, ((0, 0), (0, 0), (0, 124)))


def _from_groups(vg):
    S = vg.shape[1]
    v = vg[:, :, :4].transpose(1, 0, 2).reshape(S, SSM_HEADS)
    return jnp.pad(v, ((0, 0), (0, 128 - SSM_HEADS)))


def _expand4(v, lanes):
    out = jnp.broadcast_to(v[:, 3:4], lanes.shape)
    for hh in (2, 1, 0):
        out = jnp.where(lanes < 64 * (hh + 1), v[:, hh:hh + 1], out)
    return out


def _ssd_fwd(xbc, dt_g, ac_g, ac_t, *, name="ssd_fwd"):
    S = xbc.shape[0]
    nc = S // CHUNK
    Lc = CHUNK

    def body(x_ref, b_ref, c_ref, dt_ref, ac_ref, act_ref, y_ref, st_out_ref, st_ref):
        g = pl.program_id(0)
        c = pl.program_id(1)

        @pl.when(c == 0)
        def _():
            st_ref[...] = jnp.zeros_like(st_ref)

        bv = b_ref[...]
        cbf = c_ref[...].astype(BF16)
        cb = _dot(cbf, bv.astype(BF16), 1, 1)
        causal = _iota((Lc, Lc), 0) >= _iota((Lc, Lc), 1)
        lane256 = _iota((Lc, 256), 1)
        lane128 = _iota((Lc, 128), 1)
        row128 = _iota((128, 128), 0)
        dtg, acg = dt_ref[...], ac_ref[...]
        ac_last = ac_ref[pl.ds(Lc - 1, 1), :]
        dt4 = _expand4(dtg, lane256)
        ac4 = _expand4(acg, lane256)
        e4 = jnp.exp(ac4)
        xdb = (x_ref[...] * dt4).astype(BF16)
        st_out_ref[...] = st_ref[...]
        for p in range(2):
            xd_p = xdb[:, 128 * p:128 * (p + 1)]
            st_p = st_ref[p]
            ys, sn, cds = [], [], []
            for q in range(2):
                hh = 2 * p + q
                a_col = acg[:, hh:hh + 1]
                a_row = act_ref[pl.ds(4 * g + hh, 1), :]
                dec = jnp.exp(jnp.where(causal, a_col - a_row, NEG))
                w = (cb * dec).astype(BF16)
                ys.append(_dot(w, xd_p))
                al = ac_last[:, hh:hh + 1]
                dte = jnp.exp(al - a_col)
                sn.append(_dot(xd_p, (bv * dte).astype(BF16), 0, 0))
                cds.append(jnp.exp(al))
            y_diag = jnp.where(lane128 < 64, ys[0], ys[1])
            y_off = _dot(cbf, st_p.astype(BF16), 1, 1) * e4[:, 128 * p:128 * (p + 1)]
            y_ref[:, 128 * p:128 * (p + 1)] = y_diag + y_off
            st_ref[p] = jnp.where(row128 < 64, st_p * cds[0] + sn[0], st_p * cds[1] + sn[1])

    per_g = lambda g, c: (g, c, 0)
    return pl.pallas_call(
        body, name=name, grid=(SSM_GROUPS, nc),
        in_specs=[pl.BlockSpec((Lc, 256), lambda g, c: (c, g)),
                  pl.BlockSpec((Lc, 128), lambda g, c: (c, 16 + g)),
                  pl.BlockSpec((Lc, 128), lambda g, c: (c, 24 + g)),
                  pl.BlockSpec((None, Lc, 128), per_g), pl.BlockSpec((None, Lc, 128), per_g),
                  pl.BlockSpec((SSM_HEADS, Lc), lambda g, c: (0, c))],
        out_specs=[pl.BlockSpec((Lc, 256), lambda g, c: (c, g)),
                   pl.BlockSpec((None, None, 2, 128, 128), lambda g, c: (g, c, 0, 0, 0))],
        out_shape=[jax.ShapeDtypeStruct((S, 2048), F32), jax.ShapeDtypeStruct((SSM_GROUPS, nc, 2, 128, 128), F32)],
        scratch_shapes=[pltpu.VMEM((2, 128, 128), F32)], compiler_params=_cp(2),
    )(xbc, xbc, xbc, dt_g, ac_g, ac_t)


def _ssd_bwd(xbc, dt_g, ac_g, ac_t, states, dy, dexp, *, name="ssd_bwd"):
    S = xbc.shape[0]
    nc = S // CHUNK
    Lc = CHUNK

    def body(x_ref, b_ref, c_ref, dt_ref, ac_ref, act_ref, st_ref, dy_ref, d_ref, dx_ref, db_ref, dc_ref, dh_ref, ds_ref):
        g = pl.program_id(0)
        cc = pl.program_id(1)

        @pl.when(cc == 0)
        def _():
            ds_ref[...] = jnp.zeros_like(ds_ref)

        bv = b_ref[...]
        cv = c_ref[...]
        bbf, cbf = bv.astype(BF16), cv.astype(BF16)
        cb = _dot(cbf, bbf, 1, 1)
        causal = _iota((Lc, Lc), 0) >= _iota((Lc, Lc), 1)
        lane256 = _iota((Lc, 256), 1)
        lane128 = _iota((Lc, 128), 1)
        row128 = _iota((128, 128), 0)
        dtg, acg = dt_ref[...], ac_ref[...]
        ac_last = ac_ref[pl.ds(Lc - 1, 1), :]
        dt4 = _expand4(dtg, lane256)
        ac4 = _expand4(acg, lane256)
        acl4 = _expand4(ac_last, _iota((1, 256), 1))
        e4 = jnp.exp(ac4)
        dte4 = jnp.exp(acl4 - ac4)
        xv = x_ref[...]
        xd = xv * dt4
        xdb = xd.astype(BF16)
        dyv = dy_ref[...]
        dcb = jnp.zeros((Lc, Lc), F32)
        dc_acc = jnp.zeros((Lc, 128), F32)
        db_acc = jnp.zeros((Lc, 128), F32)
        ind_rows = _iota((256, 128), 0) >> 6
        ind_cols = _iota((256, 128), 1)
        ind_a = (ind_rows == ind_cols).astype(BF16)
        ind_b = (ind_rows + 4 == ind_cols).astype(BF16)
        u_parts, t_parts, dxd_parts, ends = [], [], [], []
        for p in range(2):
            sl = slice(128 * p, 128 * (p + 1))
            xd_p, xdb_p, dy_p = xd[:, sl], xdb[:, sl], dyv[:, sl]
            dyb_p = dy_p.astype(BF16)
            e_p, dte_p = e4[:, sl], dte4[:, sl]
            sp = st_ref[p]
            spb = sp.astype(BF16)
            dsn = ds_ref[p]
            dsnb = dsn.astype(BF16)
            yds, dxds, cds = [], [], []
            for q in range(2):
                hh = 2 * p + q
                a_col = acg[:, hh:hh + 1]
                a_row = act_ref[pl.ds(4 * g + hh, 1), :]
                dec = jnp.exp(jnp.where(causal, a_col - a_row, NEG))
                w = (cb * dec).astype(BF16)
                head = (lane128 < 64) if q == 0 else (lane128 >= 64)
                dym = jnp.where(head, dyb_p, jnp.zeros_like(dyb_p))
                dw = _dot(dym, xdb_p, 1, 1)
                dcb = dcb + dw * dec
                yds.append(_dot(w, xdb_p))
                dxds.append(_dot(w, dyb_p, 0, 0))
                cds.append(jnp.exp(ac_last[:, hh:hh + 1]))
            y_diag = jnp.where(lane128 < 64, yds[0], yds[1])
            dxd_diag = jnp.where(lane128 < 64, dxds[0], dxds[1])
            y_off = _dot(cbf, spb, 1, 1) * e_p
            dgp = dy_p * e_p
            dgb = dgp.astype(BF16)
            dc_acc = dc_acc + _dot(dgb, spb)
            dsp = _dot(dgb, cbf, 0, 0)
            cd_col = jnp.where(row128[:, 0:1] < 64, cds[0], cds[1])
            qm = _dot(bbf, dsnb, 1, 1)
            dxd_state = dte_p * qm
            db_acc = db_acc + _dot((xd_p * dte_p).astype(BF16), dsnb)
            t_p = xd_p * dxd_state
            prod = dsn * sp
            e0 = jnp.sum(jnp.sum(jnp.where(row128 < 64, prod, 0.0), axis=1, keepdims=True), axis=0, keepdims=True)
            e1 = jnp.sum(jnp.sum(jnp.where(row128 >= 64, prod, 0.0), axis=1, keepdims=True), axis=0, keepdims=True)
            tcol = jnp.sum(t_p, axis=0, keepdims=True)
            lane1 = _iota((1, 128), 1)
            t0 = jnp.sum(jnp.where(lane1 < 64, tcol, 0.0), axis=1, keepdims=True)
            t1 = jnp.sum(jnp.where(lane1 >= 64, tcol, 0.0), axis=1, keepdims=True)
            ends.append(e0 * cds[0] + t0)
            ends.append(e1 * cds[1] + t1)
            ds_ref[p] = dsn * cd_col + dsp
            u_parts.append(dyb_p.astype(F32) * y_diag - xdb_p.astype(F32) * dxd_diag + dy_p * y_off - t_p)
            dxd_parts.append(dxd_diag + dxd_state)
        dxd = jnp.concatenate(dxd_parts, axis=1)
        u_all = jnp.concatenate(u_parts, axis=1)
        dx_ref[...] = dxd * dt4 + dyv * d_ref[...]
        dcbb = dcb.astype(BF16)
        dc_ref[...] = dc_acc + _dot(dcbb, bbf)
        db_ref[...] = db_acc + _dot(dcbb, cbf, 0, 0)
        lane = _iota((Lc, 128), 1)
        endv = jnp.zeros((Lc, 128), F32)
        for hh in range(4):
            endv = jnp.where(lane == 8 + hh, ends[hh], endv)
        dh_ref[...] = _dot3(dxd * xv, ind_a) + _dot3(u_all, ind_b) + endv

    rev = lambda c: nc - 1 - c
    per_g = lambda g, c: (g, rev(c), 0)
    return pl.pallas_call(
        body, name=name, grid=(SSM_GROUPS, nc),
        in_specs=[pl.BlockSpec((Lc, 256), lambda g, c: (rev(c), g)),
                  pl.BlockSpec((Lc, 128), lambda g, c: (rev(c), 16 + g)),
                  pl.BlockSpec((Lc, 128), lambda g, c: (rev(c), 24 + g)),
                  pl.BlockSpec((None, Lc, 128), per_g), pl.BlockSpec((None, Lc, 128), per_g),
                  pl.BlockSpec((SSM_HEADS, Lc), lambda g, c: (0, rev(c))),
                  pl.BlockSpec((None, None, 2, 128, 128), lambda g, c: (g, rev(c), 0, 0, 0)),
                  pl.BlockSpec((Lc, 256), lambda g, c: (rev(c), g)),
                  pl.BlockSpec((1, 256), lambda g, c: (0, g))],
        out_specs=[pl.BlockSpec((Lc, 256), lambda g, c: (rev(c), g)),
                   pl.BlockSpec((Lc, 128), lambda g, c: (rev(c), g)),
                   pl.BlockSpec((Lc, 128), lambda g, c: (rev(c), g)),
                   pl.BlockSpec((None, Lc, 128), per_g)],
        out_shape=[jax.ShapeDtypeStruct((S, 2048), F32), jax.ShapeDtypeStruct((S, 1024), F32),
                   jax.ShapeDtypeStruct((S, 1024), F32), jax.ShapeDtypeStruct((SSM_GROUPS, S, 128), F32)],
        scratch_shapes=[pltpu.VMEM((2, 128, 128), F32)], compiler_params=_cp(2),
    )(xbc, xbc, xbc, dt_g, ac_g, ac_t, states, dy, dexp)


def _ssd_post(dhead, dt_g, sg_g, alog_g, *, name="ssd_post"):
    S = dhead.shape[1]
    nc = S // CHUNK
    Lc = CHUNK

    def body(dh_ref, dt_ref, sg_ref, al_ref, o_ref, s_ref):
        c = pl.program_id(1)
        dh = dh_ref[...]
        a_neg = -jnp.exp(al_ref[...])
        lane = _iota((Lc, 128), 1)
        row = _iota((Lc, 128), 0)
        dac = jnp.where(lane < 4, pltpu.roll(dh, 124, axis=1), 0.0)
        end = jnp.where(lane < 4, pltpu.roll(dh, 120, axis=1), 0.0)
        k = 1
        while k < Lc:
            dac = dac + jnp.where(row < Lc - k, pltpu.roll(dac, Lc - k, axis=0), 0.0)
            k *= 2
        da = dac + end
        dtv = dt_ref[...]
        ddt = jnp.where(lane < 4, da * a_neg + dh, 0.0)
        ddtr = ddt * sg_ref[...]
        o_ref[...] = ddtr
        dal = jnp.sum(da * dtv, axis=0, keepdims=True) * a_neg
        dbias = jnp.sum(ddtr, axis=0, keepdims=True)
        row8 = _iota((8, 128), 0)
        part = jnp.where(row8 == 0, dal, jnp.where(row8 == 1, dbias, 0.0))

        @pl.when(c == 0)
        def _():
            s_ref[...] = part

        @pl.when(c > 0)
        def _():
            s_ref[...] += part

    per = pl.BlockSpec((None, Lc, 128), lambda g, c: (g, c, 0))
    return pl.pallas_call(
        body, name=name, grid=(SSM_GROUPS, nc),
        in_specs=[per, per, per, pl.BlockSpec((None, 1, 128), lambda g, c: (g, 0, 0))],
        out_specs=[per, pl.BlockSpec((None, 8, 128), lambda g, c: (g, 0, 0))],
        out_shape=[jax.ShapeDtypeStruct((SSM_GROUPS, S, 128), F32), jax.ShapeDtypeStruct((SSM_GROUPS, 8, 128), F32)],
        compiler_params=_cp(2),
    )(dhead, dt_g, sg_g, alog_g)


def _gate_fwd(y, xbc, zx, dexp, gn, *, name="gate_fwd", tr=256):
    S = y.shape[0]
    W = 2048
    gw = W // SSM_GROUPS

    def body(y_ref, x_ref, z_ref, d_ref, g_ref, o_ref):
        z = z_ref[...]
        u = (y_ref[...] + x_ref[...] * d_ref[...]) * (z * _sigmoid(z))
        gv = g_ref[...]
        for q in range(SSM_GROUPS):
            sl = slice(gw * q, gw * (q + 1))
            uq = u[:, sl]
            r = lax.rsqrt(jnp.mean(uq * uq, axis=-1, keepdims=True) + EPS)
            o_ref[:, sl] = (uq * r * gv[:, sl]).astype(BF16)

    row = pl.BlockSpec((tr, W), lambda i: (i, 0))
    vec = pl.BlockSpec((1, W), lambda i: (0, 0))
    return pl.pallas_call(
        body, name=name, grid=(S // tr,), in_specs=[row, row, row, vec, vec], out_specs=row,
        out_shape=jax.ShapeDtypeStruct((S, W), BF16), compiler_params=_cp(1),
    )(y, xbc, zx, dexp, gn)


def _gate_bwd(y, xbc, zx, dexp, gn, dout, *, name="gate_bwd", tr=256):
    S = y.shape[0]
    W = 2048
    gw = W // SSM_GROUPS
    steps = S // tr

    def body(y_ref, x_ref, z_ref, d_ref, g_ref, do_ref, dy_ref, dz_ref, dg_ref, dd_ref, acc_ref):
        i = pl.program_id(0)

        @pl.when(i == 0)
        def _():
            acc_ref[...] = jnp.zeros_like(acc_ref)

        z = z_ref[...]
        sg = _sigmoid(z)
        sz = z * sg
        xs = x_ref[...]
        yt = y_ref[...] + xs * d_ref[...]
        u = yt * sz
        gv = g_ref[...]
        do = do_ref[...]
        dgs = []
        for q in range(SSM_GROUPS):
            sl = slice(gw * q, gw * (q + 1))
            uq = u[:, sl]
            r = lax.rsqrt(jnp.mean(uq * uq, axis=-1, keepdims=True) + EPS)
            uh = uq * r
            dq = do[:, sl]
            duh = dq * gv[:, sl]
            duq = r * (duh - uh * jnp.mean(duh * uh, axis=-1, keepdims=True))
            dgs.append(jnp.sum(dq * uh, axis=0, keepdims=True))
            dyt = duq * sz[:, sl]
            dy_ref[:, sl] = dyt
            dz_ref[:, sl] = (duq * yt[:, sl] * (sg[:, sl] * (1.0 + z[:, sl] * (1.0 - sg[:, sl])))).astype(BF16)
            acc_ref[:, sl] += jnp.sum(dyt * xs[:, sl], axis=0, keepdims=True)
        dg = jnp.concatenate(dgs, axis=1)

        @pl.when(i == 0)
        def _():
            dg_ref[...] = dg

        @pl.when(i > 0)
        def _():
            dg_ref[...] += dg

        @pl.when(i == steps - 1)
        def _():
            ind = ((_iota((W, 128), 0) >> 6) == _iota((W, 128), 1)).astype(BF16)
            dd_ref[...] = _dot3(jnp.broadcast_to(acc_ref[...], (8, W)), ind)[0:1, :]

    row = pl.BlockSpec((tr, W), lambda i: (i, 0))
    vec = pl.BlockSpec((1, W), lambda i: (0, 0))
    return pl.pallas_call(
        body, name=name, grid=(steps,), in_specs=[row, row, row, vec, vec, row],
        out_specs=[row, row, vec, pl.BlockSpec((1, 128), lambda i: (0, 0))],
        out_shape=[jax.ShapeDtypeStruct((S, W), F32), jax.ShapeDtypeStruct((S, W), BF16),
                   jax.ShapeDtypeStruct((1, W), F32), jax.ShapeDtypeStruct((1, 128), F32)],
        scratch_shapes=[pltpu.VMEM((1, W), F32)], compiler_params=_cp(1),
    )(y, xbc, zx, dexp, gn, dout)


def _rope_tables(pos, shape):
    lane = _iota(shape, 1)
    j = (lane & 31).astype(F32)
    inv = jnp.exp(j * (-math.log(ROPE_THETA) / 32.0))
    ang = pos * inv
    return jnp.cos(ang), jnp.sin(ang), (lane & 63) < 32


def _hn_inds(W):
    ind = ((_iota((W, 128), 0) >> 6) == _iota((W, 128), 1)).astype(BF16)
    ind_t = ((_iota((128, W), 1) >> 6) == _iota((128, W), 0)).astype(BF16)
    return ind, ind_t


def _hnrope_fwd(xin, col0, W, gain_w, posf, *, name, tr=256):
    S = xin.shape[0]
    off = col0 // W

    def body(x_ref, g_ref, p_ref, o_ref):
        x = x_ref[...]
        ind, ind_t = _hn_inds(W)
        r = lax.rsqrt(_dot3(x * x, ind) * (1.0 / HEAD) + EPS)
        xn = x * _dot3(r, ind_t) * g_ref[...]
        cs, sn, half = _rope_tables(p_ref[...], (tr, W))
        rot = jnp.where(half, -pltpu.roll(xn, W - 32, axis=1), pltpu.roll(xn, 32, axis=1))
        o_ref[...] = (xn * cs + rot * sn).astype(BF16)

    return pl.pallas_call(
        body, name=name, grid=(S // tr,),
        in_specs=[pl.BlockSpec((tr, W), lambda i: (i, off)), pl.BlockSpec((1, W), lambda i: (0, 0)),
                  pl.BlockSpec((tr, 1), lambda i: (i, 0))],
        out_specs=pl.BlockSpec((tr, W), lambda i: (i, 0)), out_shape=jax.ShapeDtypeStruct((S, W), BF16),
        compiler_params=_cp(1),
    )(xin, gain_w, posf)


def _hnrope_bwd(xin, col0, W, gain_w, posf, dout, *, name, tr=256):
    S = xin.shape[0]
    off = col0 // W
    steps = S // tr

    def body(x_ref, g_ref, p_ref, do_ref, dx_ref, cs_ref, dg_ref, acc_ref):
        i = pl.program_id(0)
        x = x_ref[...]
        ind, ind_t = _hn_inds(W)
        r = lax.rsqrt(_dot3(x * x, ind) * (1.0 / HEAD) + EPS)
        rw = _dot3(r, ind_t)
        xh = x * rw
        cs, sn, half = _rope_tables(p_ref[...], (tr, W))
        do = do_ref[...].astype(F32)
        gs = do * sn
        g1 = do * cs + jnp.where(half, pltpu.roll(gs, W - 32, axis=1), -pltpu.roll(gs, 32, axis=1))
        dxh = g1 * g_ref[...]
        t = _dot3(dxh * xh, ind) * (1.0 / HEAD)
        dx = rw * (dxh - xh * _dot3(t, ind_t))
        dx_ref[...] = dx.astype(BF16)
        cpart = jnp.sum(dx, axis=0, keepdims=True)
        gpart = jnp.sum(g1 * xh, axis=0, keepdims=True)

        @pl.when(i == 0)
        def _():
            cs_ref[...] = cpart
            acc_ref[...] = gpart

        @pl.when(i > 0)
        def _():
            cs_ref[...] += cpart
            acc_ref[...] += gpart

        @pl.when(i == steps - 1)
        def _():
            fold = ((_iota((W, 128), 0) & 63) == _iota((W, 128), 1)).astype(BF16)
            dg_ref[...] = _dot3(jnp.broadcast_to(acc_ref[...], (8, W)), fold)[0:1, :]

    return pl.pallas_call(
        body, name=name, grid=(steps,),
        in_specs=[pl.BlockSpec((tr, W), lambda i: (i, off)), pl.BlockSpec((1, W), lambda i: (0, 0)),
                  pl.BlockSpec((tr, 1), lambda i: (i, 0)), pl.BlockSpec((tr, W), lambda i: (i, 0))],
        out_specs=[pl.BlockSpec((tr, W), lambda i: (i, 0)), pl.BlockSpec((1, W), lambda i: (0, 0)),
                   pl.BlockSpec((1, 128), lambda i: (0, 0))],
        out_shape=[jax.ShapeDtypeStruct((S, W), BF16), jax.ShapeDtypeStruct((1, W), F32),
                   jax.ShapeDtypeStruct((1, 128), F32)],
        scratch_shapes=[pltpu.VMEM((1, W), F32)], compiler_params=_cp(1),
    )(xin, gain_w, posf, dout)


def _attn_probs(q, kb, sink_ref, h, i):
    s = _dot(q, kb, 1, 1) * (HEAD ** -0.5)
    r = _iota((4 * WINDOW, 2 * WINDOW), 0)
    ki = _iota((4 * WINDOW, 2 * WINDOW), 1)
    rel = (r & (WINDOW - 1)) + WINDOW - ki
    mask = (rel >= 0) & (rel < WINDOW) & ((ki >= WINDOW) | (i > 0))
    s = jnp.where(mask, s, NEG)
    r1 = _iota((4 * WINDOW, 1), 0)
    sink = jnp.where(r1 < WINDOW, sink_ref[4 * h], jnp.where(r1 < 2 * WINDOW, sink_ref[4 * h + 1],
                     jnp.where(r1 < 3 * WINDOW, sink_ref[4 * h + 2], sink_ref[4 * h + 3])))
    m = jnp.maximum(jnp.max(s, axis=1, keepdims=True), sink)
    p = jnp.exp(s - m)
    ps = jnp.exp(sink - m)
    inv = 1.0 / (jnp.sum(p, axis=1, keepdims=True) + ps)
    return p * inv, ps * inv


def _attn_specs(S):
    qspec = pl.BlockSpec((None, ATT_G, WINDOW, HEAD), lambda h, i: (h, 0, i, 0))
    cur = pl.BlockSpec((None, WINDOW, HEAD), lambda h, i: (h, i, 0))
    prev = pl.BlockSpec((None, WINDOW, HEAD), lambda h, i: (h, jnp.maximum(i - 1, 0), 0))
    return qspec, cur, prev


def _attn_fwd(qh, kh, vh, sinks, *, name="attn_fwd"):
    S = kh.shape[1]
    nb = S // WINDOW

    def body(s_ref, q_ref, kc_ref, kp_ref, vc_ref, vp_ref, o_ref):
        h, i = pl.program_id(0), pl.program_id(1)
        q = q_ref[...].reshape(ATT_G * WINDOW, HEAD)
        kb = jnp.concatenate([kp_ref[...], kc_ref[...]], axis=0)
        vb = jnp.concatenate([vp_ref[...], vc_ref[...]], axis=0)
        probs, _ = _attn_probs(q, kb, s_ref, h, i)
        o = _dot(probs.astype(BF16), vb)
        o_ref[...] = o.reshape(ATT_G, WINDOW, HEAD).astype(BF16)

    qspec, cur, prev = _attn_specs(S)
    return pl.pallas_call(
        body, name=name, grid=(ATT_KV, nb),
        in_specs=[pl.BlockSpec(memory_space=pltpu.SMEM), qspec, cur, prev, cur, prev], out_specs=qspec,
        out_shape=jax.ShapeDtypeStruct((ATT_KV, ATT_G, S, HEAD), BF16), compiler_params=_cp(2),
    )(sinks, qh, kh, kh, vh, vh)


def _attn_bwd(qh, kh, vh, sinks, doh, *, name="attn_bwd"):
    S = kh.shape[1]
    nb = S // WINDOW

    def body(s_ref, q_ref, kc_ref, kp_ref, vc_ref, vp_ref, do_ref, dq_ref, dk_ref, dv_ref, dsk_ref):
        h, i = pl.program_id(0), pl.program_id(1)

        @pl.when(i == 0)
        def _():
            dk_ref[...] = jnp.zeros_like(dk_ref)
            dv_ref[...] = jnp.zeros_like(dv_ref)
            dsk_ref[...] = jnp.zeros_like(dsk_ref)

        q = q_ref[...].reshape(ATT_G * WINDOW, HEAD)
        do = do_ref[...].reshape(ATT_G * WINDOW, HEAD)
        kb = jnp.concatenate([kp_ref[...], kc_ref[...]], axis=0)
        vb = jnp.concatenate([vp_ref[...], vc_ref[...]], axis=0)
        probs, psink = _attn_probs(q, kb, s_ref, h, i)
        dp = _dot(do, vb, 1, 1)
        delta = jnp.sum(probs * dp, axis=1, keepdims=True)
        ds = (probs * (dp - delta)).astype(BF16)
        scale = HEAD ** -0.5
        dq_ref[...] = (_dot(ds, kb) * scale).reshape(ATT_G, WINDOW, HEAD)
        dkb = _dot(ds, q, 0, 0) * scale
        dvb = _dot(probs.astype(BF16), do, 0, 0)
        cur = pl.multiple_of(i * WINDOW, WINDOW)
        dk_ref[pl.ds(cur, WINDOW), :] += dkb[WINDOW:, :]
        dv_ref[pl.ds(cur, WINDOW), :] += dvb[WINDOW:, :]

        @pl.when(i > 0)
        def _():
            prv = pl.multiple_of((i - 1) * WINDOW, WINDOW)
            dk_ref[pl.ds(prv, WINDOW), :] += dkb[:WINDOW, :]
            dv_ref[pl.ds(prv, WINDOW), :] += dvb[:WINDOW, :]

        dsr = -psink * delta
        lane = _iota((8, 128), 1)
        row = _iota((8, 128), 0)
        upd = jnp.zeros((8, 128), F32)
        for gq in range(ATT_G):
            v = jnp.sum(dsr[gq * WINDOW:(gq + 1) * WINDOW, :], axis=0, keepdims=True)
            upd = jnp.where((lane == gq) & (row == 0), v, upd)
        dsk_ref[...] += upd

    qspec, cur, prev = _attn_specs(S)
    full = pl.BlockSpec((None, S, HEAD), lambda h, i: (h, 0, 0))
    return pl.pallas_call(
        body, name=name, grid=(ATT_KV, nb),
        in_specs=[pl.BlockSpec(memory_space=pltpu.SMEM), qspec, cur, prev, cur, prev, qspec],
        out_specs=[qspec, full, full, pl.BlockSpec((None, 8, 128), lambda h, i: (h, 0, 0))],
        out_shape=[jax.ShapeDtypeStruct((ATT_KV, ATT_G, S, HEAD), F32), jax.ShapeDtypeStruct((ATT_KV, S, HEAD), F32),
                   jax.ShapeDtypeStruct((ATT_KV, S, HEAD), F32), jax.ShapeDtypeStruct((ATT_KV, 8, 128), F32)],
        compiler_params=_cp(2),
    )(sinks, qh, kh, kh, vh, vh, doh)


def _heads_major(t, nh):
    S = t.shape[0]
    return t.reshape(S, nh, HEAD).transpose(1, 0, 2)


def _tokens_major(t):
    nh, S, _ = t.shape
    return t.transpose(1, 0, 2).reshape(S, nh * HEAD)


def _local_step(x, posf, target, w):
    S, D = x.shape
    gr = {}

    (h1,) = _rms_fwd(x, [w["a_norm"]], name="a_norm_f")
    zx = _mm(h1, w["w_zx"], name="in_proj_zx")
    dtr = _mm(h1, w["w_dt"], name="in_proj_dt")
    xbc = _conv_silu_fwd(zx, 2048, 4096, w["a_conv_w"], w["a_conv_b"], name="a_conv_f")
    dt, acum, sgd = _ssd_prep(dtr, w["a_dt_bias"], w["a_A_log"])
    dt_g, ac_g, sg_g = _to_groups(dt), _to_groups(acum), _to_groups(sgd)
    ac_t = acum[:, :SSM_HEADS].T
    y_ssd, states = _ssd_fwd(xbc, dt_g, ac_g, ac_t)
    yg = _gate_fwd(y_ssd, xbc, zx, w["a_Dexp"], w["a_gnorm"])
    x1 = _mm(yg, w["a_out_proj"], res=x, name="out_proj")

    def ffn_fwd(xin, l):
        (h,) = _rms_fwd(xin, [w["f_norm"][l]], name=f"f_norm_f{l}")
        u = _mm(h, w["f_w_in"][l], name=f"f_in{l}")
        a = _ffn_act_fwd(u, w["f_conv_w"][l], w["f_conv_b"][l], name=f"f_act_f{l}")
        xo = _mm(a, w["f_w_down"][l], res=xin, name=f"f_down{l}")
        return xo, (h, u)

    x2, ffn0 = ffn_fwd(x1, 0)

    hk, hq = _rms_fwd(x2, [w["kv_norm"], w["b_norm"]], name="kvq_norm_f")
    kv = _mm(hk, w["w_kv"], bias=w["b_kv"], name="kv_proj")
    q = _mm(hq, w["w_q"], bias=w["b_q"], name="q_proj")
    kr = _hnrope_fwd(kv, 0, 256, w["k_norm_w"], posf, name="k_rope_f")
    qr = _hnrope_fwd(q, 0, 1024, w["q_norm_w"], posf, name="q_rope_f")
    qh = _heads_major(qr, 16).reshape(ATT_KV, ATT_G, S, HEAD)
    kh = _heads_major(kr, ATT_KV)
    vh = _heads_major(kv[:, 256:].astype(BF16), ATT_KV)
    att_h = _attn_fwd(qh, kh, vh, w["sinks"])
    att = _tokens_major(att_h.reshape(16, S, HEAD))
    x3 = _mm(att, w["w_o"], bias=w["b_o"], res=x2, name="o_proj")
    x4, ffn1 = ffn_fwd(x3, 1)

    dy, loss_part = _loss(x4, target)

    def ffn_bwd(xin, l, saved, dyo, want_colsum):
        h, u = saved
        da = _mm(dyo, w["f_w_down"][l], tb=True, name=f"f_down_dx{l}")
        a = _ffn_act_fwd(u, w["f_conv_w"][l], w["f_conv_b"][l], name=f"f_act_r{l}")
        dgate, dval, dcw, dcb = _ffn_act_bwd(u, w["f_conv_w"][l], w["f_conv_b"][l], da, name=f"f_act_b{l}")
        dw_down = _mm(a, dyo, ta=True, name=f"f_down_dw{l}")
        dw_gate = _mm(h, dgate, ta=True, name=f"f_in_dwg{l}")
        dw_val = _mm(h, dval, ta=True, name=f"f_in_dwv{l}")
        Fd = dgate.shape[1]
        dh = _mm(dgate, w["f_w_in"][l], tb=True, name=f"f_in_dxg{l}")
        dh = _mm(dval, w["f_w_in"][l], tb=True, b_koff=Fd, res=dh, name=f"f_in_dxv{l}")
        outs = _rms_bwd(xin, [w["f_norm"][l]], [dh], dyo, name=f"f_norm_b{l}", want_colsum=want_colsum)
        g = dict(f_norm=outs[1], f_w_gate=dw_gate, f_w_val=dw_val, f_conv_w=dcw, f_conv_b=dcb, f_w_down=dw_down)
        return outs[0], g, (outs[2] if want_colsum else None)

    dx3, gr["ffn1"], db_o = ffn_bwd(x3, 1, ffn1, dy, True)
    gr["b_o"] = db_o
    gr["w_o"] = _mm(att, dx3, ta=True, name="o_proj_dw")
    datt = _mm(dx3, w["w_o"], tb=True, out_dtype=BF16, name="o_proj_dx")
    doh = _heads_major(datt, 16).reshape(ATT_KV, ATT_G, S, HEAD)
    dqh, dkh, dvh, dsk = _attn_bwd(qh, kh, vh, w["sinks"], doh)
    gr["sinks"] = dsk[:, 0, :4].reshape(1, 16)
    dqr = _tokens_major(dqh.reshape(16, S, HEAD))
    dkr = _tokens_major(dkh)
    dv = _tokens_major(dvh).astype(BF16)
    dq, db_q, dqn = _hnrope_bwd(q, 0, 1024, w["q_norm_w"], posf, dqr, name="q_rope_b")
    dk, db_k, dkn = _hnrope_bwd(kv, 0, 256, w["k_norm_w"], posf, dkr, name="k_rope_b")
    gr["q_norm"], gr["k_norm"] = dqn[:, :HEAD], dkn[:, :HEAD]
    gr["b_q"] = db_q
    gr["b_kv"] = jnp.concatenate([db_k, _colsum(dv, name="dv_colsum")], axis=1)
    gr["w_q"] = _mm(hq, dq, ta=True, name="q_proj_dw")
    gr["w_kv"] = jnp.concatenate([_mm(hk, dk, ta=True, name="k_proj_dw"), _mm(hk, dv, ta=True, name="v_proj_dw")], axis=1)
    dhq = _mm(dq, w["w_q"], tb=True, name="q_proj_dx")
    dhk = _mm(dk, w["w_kv"], tb=True, name="k_proj_dx")
    dhk = _mm(dv, w["w_kv"], tb=True, b_koff=256, res=dhk, name="v_proj_dx")
    dx2, gr["kv_norm"], gr["b_norm"] = _rms_bwd(x2, [w["kv_norm"], w["b_norm"]], [dhk, dhq], dx3, name="kvq_norm_b")

    dx1, gr["ffn0"], _ = ffn_bwd(x1, 0, ffn0, dx2, False)

    gr["a_out_proj"] = _mm(yg, dx1, ta=True, name="out_proj_dw")
    dyg = _mm(dx1, w["a_out_proj"], tb=True, name="out_proj_dx")
    dy_ssd, dz, gr["a_gnorm"], dD = _gate_bwd(y_ssd, xbc, zx, w["a_Dexp"], w["a_gnorm"], dyg)
    gr["a_D"] = dD[:, :SSM_HEADS]
    dxs, dB, dC, dhead = _ssd_bwd(xbc, dt_g, ac_g, ac_t, states, dy_ssd, w["a_Dexp"])
    ddtr_g, dsmall = _ssd_post(dhead, dt_g, sg_g, w["a_A_log_g"])
    gr["a_A_log"] = dsmall[:, 0, :4].reshape(1, SSM_HEADS)
    gr["a_dt_bias"] = dsmall[:, 1, :4].reshape(1, SSM_HEADS)
    ddtr = _from_groups(ddtr_g).astype(BF16)
    dxbc, gr["a_conv_w"], gr["a_conv_b"] = _conv_silu_bwd(
        zx, 2048, 4096, w["a_conv_w"], w["a_conv_b"], [(dxs, 0), (dB, 2048), (dC, 3072)], name="a_conv_b")
    gr["w_z"] = _mm(h1, dz, ta=True, name="in_proj_dwz")
    gr["w_x"] = _mm(h1, dxbc, ta=True, name="in_proj_dwx")
    gr["w_dt"] = _mm(h1, ddtr, ta=True, name="in_proj_dwdt")
    dh1 = _mm(dz, w["w_zx"], tb=True, name="in_proj_dxz")
    dh1 = _mm(dxbc, w["w_zx"], tb=True, b_koff=2048, res=dh1, name="in_proj_dxx")
    dh1 = _mm(ddtr, w["w_dt"], tb=True, res=dh1, name="in_proj_dxdt")
    dx0, gr["a_norm"] = _rms_bwd(x, [w["a_norm"]], [dh1], dx1, name="a_norm_b")
    return loss_part, dx0, gr


def _prep_weights(full):
    w = {}
    ip = full["a_in_proj"][0]
    w["w_zx"] = ip[:, :6144].astype(BF16)
    w["w_dt"] = jnp.pad(ip[:, 6144:], ((0, 0), (0, 128 - SSM_HEADS))).astype(BF16)
    w["a_norm"] = full["a_norm"]
    w["a_conv_w"] = full["a_conv_w"][0]
    w["a_conv_b"] = full["a_conv_b"]
    pad32 = lambda v: jnp.pad(v, ((0, 0), (0, 128 - SSM_HEADS)))
    w["a_dt_bias"] = pad32(full["a_dt_bias"])
    w["a_A_log"] = pad32(full["a_A_log"])
    w["a_A_log_g"] = jnp.pad(full["a_A_log"].reshape(SSM_GROUPS, 1, 4), ((0, 0), (0, 0), (0, 124)))
    w["a_Dexp"] = jnp.repeat(full["a_D"], HEAD, axis=1)
    w["a_gnorm"] = full["a_gnorm"]
    w["a_out_proj"] = full["a_out_proj"][0].astype(BF16)
    w["f_norm"] = [full["f_norm"][l:l + 1] for l in range(2)]
    w["f_w_in"] = [full["f_w_in"][l].astype(BF16) for l in range(2)]
    w["f_conv_w"] = [full["f_conv_w"][l] for l in range(2)]
    w["f_conv_b"] = [full["f_conv_b"][l:l + 1] for l in range(2)]
    w["f_w_down"] = [full["f_w_down"][l].astype(BF16) for l in range(2)]
    w["kv_norm"] = full["kv_norm"].reshape(1, -1)
    w["w_kv"] = full["w_kv"].astype(BF16)
    w["b_kv"] = full["b_kv"].reshape(1, -1)
    w["k_norm_w"] = jnp.tile(full["k_norm"].reshape(1, HEAD), (1, ATT_KV))
    w["b_norm"] = full["b_norm"]
    w["w_q"] = full["w_q"][0].astype(BF16)
    w["b_q"] = full["b_q"]
    w["q_norm_w"] = jnp.tile(full["q_norm"], (1, ATT_KV * ATT_G))
    w["sinks"] = full["sinks"].reshape(-1)
    w["w_o"] = full["w_o"][0].astype(BF16)
    w["b_o"] = full["b_o"]
    return w


def _full_grads(gr):
    g = {}
    g["a_norm"] = gr["a_norm"]
    g["a_in_proj"] = jnp.concatenate([gr["w_z"], gr["w_x"], gr["w_dt"][:, :SSM_HEADS]], axis=1)[None]
    g["a_conv_w"] = gr["a_conv_w"][None]
    g["a_conv_b"] = gr["a_conv_b"]
    g["a_dt_bias"], g["a_A_log"], g["a_D"] = gr["a_dt_bias"], gr["a_A_log"], gr["a_D"]
    g["a_gnorm"] = gr["a_gnorm"]
    g["a_out_proj"] = gr["a_out_proj"][None]
    g["kv_norm"] = gr["kv_norm"].reshape(-1)
    g["w_kv"] = gr["w_kv"]
    g["b_kv"] = gr["b_kv"].reshape(-1)
    g["k_norm"] = gr["k_norm"].reshape(-1)
    g["b_norm"] = gr["b_norm"]
    g["w_q"] = gr["w_q"][None]
    g["b_q"] = gr["b_q"]
    g["q_norm"] = gr["q_norm"]
    g["sinks"] = gr["sinks"]
    g["w_o"] = gr["w_o"][None]
    g["b_o"] = gr["b_o"]
    f = [gr["ffn0"], gr["ffn1"]]
    g["f_norm"] = jnp.concatenate([f[0]["f_norm"], f[1]["f_norm"]], axis=0)
    g["f_w_in"] = jnp.stack([jnp.concatenate([f[l]["f_w_gate"], f[l]["f_w_val"]], axis=1) for l in range(2)])
    g["f_conv_w"] = jnp.stack([f[l]["f_conv_w"] for l in range(2)])
    g["f_conv_b"] = jnp.concatenate([f[l]["f_conv_b"] for l in range(2)], axis=0)
    g["f_w_down"] = jnp.stack([f[l]["f_w_down"] for l in range(2)])
    return g


MESH = pl.DeviceIdType.MESH
N_CHIPS = 4
WEIGHTS = ("a_norm", "a_in_proj", "a_conv_w", "a_conv_b", "a_dt_bias", "a_A_log", "a_D", "a_gnorm", "a_out_proj",
           "kv_norm", "w_kv", "b_kv", "k_norm", "b_norm", "w_q", "b_q", "q_norm", "sinks", "w_o", "b_o", "f_norm",
           "f_w_in", "f_conv_w", "f_conv_b", "f_w_down")
BIG = (("a_in_proj", 2), ("a_out_proj", 1), ("w_kv", 0), ("w_q", 1), ("w_o", 1), ("f_w_in", 2), ("f_w_down", 1))
SMALL_CUT = (("a_norm", 1), ("a_conv_w", 2), ("a_conv_b", 1), ("a_gnorm", 1), ("f_conv_w", 2))
SMALL_REP = ("a_dt_bias", "a_A_log", "a_D", "kv_norm", "b_kv", "k_norm", "b_norm", "b_q", "q_norm", "sinks", "b_o",
             "f_norm", "f_conv_b")
HALF_ROW_ALIGN = 16
PACK_LANES = 1024


def _coords():
    return lax.axis_index("x"), lax.axis_index("y"), lax.axis_index("c")


def _other_chips(x, y):
    return [(1 - x, y), (x, 1 - y), (1 - x, 1 - y)]


def _pack(arrs, rows_align, lanes, dtype, lead=()):
    flat = jnp.concatenate([a.reshape(-1).astype(dtype) for a in arrs])
    n_lead = math.prod(lead) if lead else 1
    per = n_lead * rows_align * lanes
    total = -(-flat.shape[0] // per) * per
    flat = jnp.pad(flat, (0, total - flat.shape[0]))
    return flat.reshape(lead + (total // (n_lead * lanes), lanes))


def _unpack(flat, shapes):
    out, off = [], 0
    for s in shapes:
        n = math.prod(s)
        out.append(flat[off:off + n].reshape(s))
        off += n
    return out


def _remote(src, dst, send, recv, k, dev):
    return pltpu.make_async_remote_copy(src_ref=src, dst_ref=dst, send_sem=send.at[k], recv_sem=recv.at[k],
                                        device_id=dev, device_id_type=MESH)


_ANY = pl.BlockSpec(memory_space=pl.ANY)


def _gather_weights(wp, sp):
    R = wp.shape[1]

    def body(wp_ref, sp_ref, out_ref, sout_ref, send, recv, loc):
        x, y, c = _coords()
        me = 2 * x + y
        chips = _other_chips(x, y)
        sib = (x, y, 1 - c)
        l0 = pltpu.make_async_copy(wp_ref, out_ref.at[me], loc.at[0])
        l1 = pltpu.make_async_copy(sp_ref, sout_ref.at[me], loc.at[1])
        l0.start()
        l1.start()
        sends = []
        for j, (cx, cy) in enumerate(chips):
            sends.append(_remote(wp_ref.at[c], out_ref.at[me, c], send, recv, j, (cx, cy, c)))
            sends.append(_remote(sp_ref, sout_ref.at[me], send, recv, 3 + j, (cx, cy, c)))
        for cp in sends:
            cp.start()
        for j, (cx, cy) in enumerate(chips):
            src = 2 * cx + cy
            _remote(wp_ref.at[c], out_ref.at[src, c], send, recv, j, (cx, cy, c)).wait_recv()
            fwd = _remote(out_ref.at[src, c], out_ref.at[src, c], send, recv, 6 + j, sib)
            fwd.start()
            sends.append(fwd)
        for j, (cx, cy) in enumerate(chips):
            src = 2 * cx + cy
            _remote(sp_ref, sout_ref.at[src], send, recv, 3 + j, (cx, cy, c)).wait_recv()
            _remote(out_ref.at[src, 1 - c], out_ref.at[src, 1 - c], send, recv, 6 + j, sib).wait_recv()
        for cp in sends:
            cp.wait_send()
        l0.wait()
        l1.wait()

    return pl.pallas_call(
        body, name="gather_weights", in_specs=[_ANY, _ANY], out_specs=[_ANY, _ANY],
        out_shape=[jax.ShapeDtypeStruct((N_CHIPS, 2, R, PACK_LANES), wp.dtype),
                   jax.ShapeDtypeStruct((N_CHIPS,) + sp.shape, sp.dtype)],
        scratch_shapes=[pltpu.SemaphoreType.DMA((9,)), pltpu.SemaphoreType.DMA((9,)), pltpu.SemaphoreType.DMA((2,))],
    )(wp, sp)


def _allreduce_small(v):
    SR = v.shape[0]

    def body(v_ref, o_ref, buf, send, recv):
        x, y, c = _coords()
        me = 4 * x + 2 * y + c
        buf[me] = v_ref[...]
        peers = []
        for k in range(1, 8):
            px = 1 - x if k & 4 else x
            py = 1 - y if k & 2 else y
            pc = 1 - c if k & 1 else c
            peers.append((px, py, pc))
        cps = [_remote(v_ref, buf.at[me], send, recv, k, p) for k, p in enumerate(peers)]
        for cp in cps:
            cp.start()
        for k, (px, py, pc) in enumerate(peers):
            _remote(v_ref, buf.at[4 * px + 2 * py + pc], send, recv, k, (px, py, pc)).wait_recv()
        for cp in cps:
            cp.wait_send()
        acc = buf[0]
        for s in range(1, 8):
            acc = acc + buf[s]
        o_ref[...] = acc

    vm = pl.BlockSpec(memory_space=pltpu.VMEM)
    return pl.pallas_call(
        body, name="allreduce_small", in_specs=[vm], out_specs=vm, out_shape=jax.ShapeDtypeStruct(v.shape, F32),
        scratch_shapes=[pltpu.VMEM((8, SR, 128), F32), pltpu.SemaphoreType.DMA((7,)), pltpu.SemaphoreType.DMA((7,))],
    )(v)


def _rs_to_sibling(g):
    R = g.shape[2]

    def body(g_ref, a_ref, send, recv):
        x, y, c = _coords()
        cps = [_remote(g_ref.at[j, 1 - c], a_ref.at[j], send, recv, j, (x, y, 1 - c)) for j in range(N_CHIPS)]
        for cp in cps:
            cp.start()
        for cp in cps:
            cp.wait()

    return pl.pallas_call(
        body, name="rs_to_sibling", in_specs=[_ANY], out_specs=_ANY,
        out_shape=jax.ShapeDtypeStruct((N_CHIPS, R, PACK_LANES), g.dtype),
        scratch_shapes=[pltpu.SemaphoreType.DMA((N_CHIPS,)), pltpu.SemaphoreType.DMA((N_CHIPS,))],
    )(g)


def _rs_row_tile(R):
    return _pick(R, (496, 256, 128, 112, 16))


def _rs_add_pair(g, a, c_idx):
    R = g.shape[2]
    tr = _rs_row_tile(R)

    def body(c_ref, g_ref, a_ref, p_ref):
        p_ref[...] = (g_ref[...].astype(F32) + a_ref[...].astype(F32)).astype(BF16)

    return pl.pallas_call(
        body, name="rs_add_pair",
        grid_spec=pltpu.PrefetchScalarGridSpec(
            num_scalar_prefetch=1, grid=(N_CHIPS, R // tr),
            in_specs=[pl.BlockSpec((None, None, tr, PACK_LANES), lambda j, i, c_ref: (j, c_ref[0], i, 0)),
                      pl.BlockSpec((None, tr, PACK_LANES), lambda j, i, c_ref: (j, i, 0))],
            out_specs=pl.BlockSpec((None, tr, PACK_LANES), lambda j, i, c_ref: (j, i, 0))),
        out_shape=jax.ShapeDtypeStruct((N_CHIPS, R, PACK_LANES), BF16), compiler_params=_cp(2),
    )(c_idx, g, a)


def _rs_to_chips(p):
    R = p.shape[1]

    def body(p_ref, r_ref, send, recv):
        x, y, c = _coords()
        cps = [_remote(p_ref.at[2 * cx + cy], r_ref.at[k], send, recv, k, (cx, cy, c))
               for k, (cx, cy) in enumerate(_other_chips(x, y))]
        for cp in cps:
            cp.start()
        for cp in cps:
            cp.wait()

    return pl.pallas_call(
        body, name="rs_to_chips", in_specs=[_ANY], out_specs=_ANY,
        out_shape=jax.ShapeDtypeStruct((3, R, PACK_LANES), p.dtype),
        scratch_shapes=[pltpu.SemaphoreType.DMA((3,)), pltpu.SemaphoreType.DMA((3,))],
    )(p)


def _rs_add_chips(p, r, me_idx):
    R = p.shape[1]
    tr = _rs_row_tile(R)

    def body(me_ref, p_ref, r0_ref, r1_ref, r2_ref, o_ref):
        o_ref[...] = ((p_ref[...].astype(F32) + r0_ref[...].astype(F32)) + r1_ref[...].astype(F32)) + r2_ref[...].astype(F32)

    def rk(k):
        return pl.BlockSpec((None, tr, PACK_LANES), lambda i, me_ref, k=k: (k, i, 0))

    return pl.pallas_call(
        body, name="rs_add_chips",
        grid_spec=pltpu.PrefetchScalarGridSpec(
            num_scalar_prefetch=1, grid=(R // tr,),
            in_specs=[pl.BlockSpec((None, tr, PACK_LANES), lambda i, me_ref: (me_ref[0], i, 0)), rk(0), rk(1), rk(2)],
            out_specs=pl.BlockSpec((tr, PACK_LANES), lambda i, me_ref: (i, 0))),
        out_shape=jax.ShapeDtypeStruct((R, PACK_LANES), F32), compiler_params=_cp(1),
    )(me_idx, p, r, r, r)


def _rs_join_halves(h):
    R = h.shape[0]

    def body(h_ref, o_ref, send, recv, loc):
        x, y, c = _coords()
        mine = pltpu.make_async_copy(h_ref, o_ref.at[c], loc.at[0])
        mine.start()
        cp = _remote(h_ref, o_ref.at[c], send, recv, 0, (x, y, 1 - c))
        cp.start()
        _remote(h_ref, o_ref.at[1 - c], send, recv, 0, (x, y, 1 - c)).wait_recv()
        cp.wait_send()
        mine.wait()

    return pl.pallas_call(
        body, name="rs_join_halves", in_specs=[_ANY], out_specs=_ANY,
        out_shape=jax.ShapeDtypeStruct((2, R, PACK_LANES), F32),
        scratch_shapes=[pltpu.SemaphoreType.DMA((1,)), pltpu.SemaphoreType.DMA((1,)), pltpu.SemaphoreType.DMA((1,))],
    )(h)


def _adamw(w, g, m, v, *, name):
    Rr, C = w.shape
    tr = _pick(Rr, (256, 128, 64, 8))
    bc1 = 1.0 - ADAM_B1 ** ADAM_STEP
    bc2 = 1.0 - ADAM_B2 ** ADAM_STEP

    def body(w_ref, g_ref, m_ref, v_ref, d_ref, mo_ref, vo_ref):
        gv = g_ref[...]
        mn = ADAM_B1 * m_ref[...] + (1.0 - ADAM_B1) * gv
        vn = ADAM_B2 * v_ref[...] + (1.0 - ADAM_B2) * (gv * gv)
        mo_ref[...] = mn
        vo_ref[...] = vn
        d_ref[...] = -ADAM_LR * ((mn / bc1) / (jnp.sqrt(vn / bc2) + ADAM_EPS) + ADAM_WD * w_ref[...])

    blk = pl.BlockSpec((tr, C), lambda i: (i, 0))
    return pl.pallas_call(
        body, name=name, grid=(Rr // tr,), in_specs=[blk] * 4, out_specs=[blk] * 3,
        out_shape=[jax.ShapeDtypeStruct((Rr, C), F32)] * 3, compiler_params=_cp(1),
    )(w, g, m, v)


def kernel(x, positions, a_norm, a_in_proj, a_conv_w, a_conv_b, a_dt_bias, a_A_log, a_D, a_gnorm, a_out_proj,
           kv_norm, w_kv, b_kv, k_norm, b_norm, w_q, b_q, q_norm, sinks, w_o, b_o, f_norm, f_w_in, f_conv_w,
           f_conv_b, f_w_down, loss_target, m_a_norm, m_a_in_proj, m_a_conv_w, m_a_conv_b, m_a_dt_bias, m_a_A_log,
           m_a_D, m_a_gnorm, m_a_out_proj, m_kv_norm, m_w_kv, m_b_kv, m_k_norm, m_b_norm, m_w_q, m_b_q, m_q_norm,
           m_sinks, m_w_o, m_b_o, m_f_norm, m_f_w_in, m_f_conv_w, m_f_conv_b, m_f_w_down, v_a_norm, v_a_in_proj,
           v_a_conv_w, v_a_conv_b, v_a_dt_bias, v_a_A_log, v_a_D, v_a_gnorm, v_a_out_proj, v_kv_norm, v_w_kv,
           v_b_kv, v_k_norm, v_b_norm, v_w_q, v_b_q, v_q_norm, v_sinks, v_w_o, v_b_o, v_f_norm, v_f_w_in,
           v_f_conv_w, v_f_conv_b, v_f_w_down):
    wl = dict(zip(WEIGHTS, (a_norm, a_in_proj, a_conv_w, a_conv_b, a_dt_bias, a_A_log, a_D, a_gnorm, a_out_proj,
                            kv_norm, w_kv, b_kv, k_norm, b_norm, w_q, b_q, q_norm, sinks, w_o, b_o, f_norm, f_w_in,
                            f_conv_w, f_conv_b, f_w_down)))
    ml = dict(zip(WEIGHTS, (m_a_norm, m_a_in_proj, m_a_conv_w, m_a_conv_b, m_a_dt_bias, m_a_A_log, m_a_D, m_a_gnorm,
                            m_a_out_proj, m_kv_norm, m_w_kv, m_b_kv, m_k_norm, m_b_norm, m_w_q, m_b_q, m_q_norm,
                            m_sinks, m_w_o, m_b_o, m_f_norm, m_f_w_in, m_f_conv_w, m_f_conv_b, m_f_w_down)))
    vl = dict(zip(WEIGHTS, (v_a_norm, v_a_in_proj, v_a_conv_w, v_a_conv_b, v_a_dt_bias, v_a_A_log, v_a_D, v_a_gnorm,
                            v_a_out_proj, v_kv_norm, v_w_kv, v_b_kv, v_k_norm, v_b_norm, v_w_q, v_b_q, v_q_norm,
                            v_sinks, v_w_o, v_b_o, v_f_norm, v_f_w_in, v_f_conv_w, v_f_conv_b, v_f_w_down)))
    xi, yi, ci = _coords()
    me = 2 * xi + yi
    S = x.shape[1]

    wp = _pack([wl[n] for n, _ in BIG], HALF_ROW_ALIGN, PACK_LANES, BF16, lead=(2,))
    sp = _pack([wl[n] for n, _ in SMALL_CUT], 8, 128, F32)
    gw, gs = _gather_weights(wp, sp)
    gw = gw.reshape(N_CHIPS, -1)
    gs = gs.reshape(N_CHIPS, -1)
    full = {n: wl[n] for n in SMALL_REP}
    for names, src in ((BIG, gw), (SMALL_CUT, gs)):
        pieces = [_unpack(src[j], [wl[n].shape for n, _ in names]) for j in range(N_CHIPS)]
        for q, (n, ax) in enumerate(names):
            full[n] = jnp.concatenate([pieces[j][q] for j in range(N_CHIPS)], axis=ax)

    posf = positions.reshape(S, 1).astype(F32)
    loss_part, dx0, gr = _local_step(x[0], posf, loss_target[0], _prep_weights(full))
    g = _full_grads(gr)

    small_names = [n for n, _ in SMALL_CUT] + list(SMALL_REP)
    sv = _pack([g[n] for n in small_names] + [loss_part[0:1, 0:1]], 8, 128, F32)
    sred = _allreduce_small(sv).reshape(-1)
    small_shapes = [g[n].shape for n in small_names] + [(1,)]
    sg = dict(zip(small_names + ["loss"], _unpack(sred, small_shapes)))
    loss = sg["loss"].reshape(())
    g_small = {}
    for n, ax in SMALL_CUT:
        size = wl[n].shape[ax]
        g_small[n] = lax.dynamic_slice_in_dim(sg[n], me * size, size, axis=ax)
    for n in SMALL_REP:
        g_small[n] = sg[n].reshape(wl[n].shape)

    gp = jnp.stack([_pack([jnp.split(g[n], N_CHIPS, axis=ax)[j] for n, ax in BIG], HALF_ROW_ALIGN, PACK_LANES, BF16,
                          lead=(2,)) for j in range(N_CHIPS)])
    c_idx = jnp.reshape(ci, (1,)).astype(jnp.int32)
    me_idx = jnp.reshape(me, (1,)).astype(jnp.int32)
    pair = _rs_add_pair(gp, _rs_to_sibling(gp), c_idx)
    half = _rs_add_chips(pair, _rs_to_chips(pair), me_idx)
    red = _rs_join_halves(half).reshape(-1)
    g_big = dict(zip([n for n, _ in BIG], _unpack(red, [wl[n].shape for n, _ in BIG])))

    grads, delta, new_m, new_v = {}, {}, {}, {}
    for n, _ in BIG:
        shp = wl[n].shape
        two = (-1, shp[-1])
        d, mn, vn = _adamw(wl[n].reshape(two), g_big[n].reshape(two), ml[n].reshape(two), vl[n].reshape(two),
                           name="adamw_" + n)
        grads[n], delta[n], new_m[n], new_v[n] = g_big[n], d.reshape(shp), mn.reshape(shp), vn.reshape(shp)
    pk = lambda d: _pack([d[n] for n in small_names], 8, 128, F32)
    d, mn, vn = _adamw(pk(wl), pk(g_small), pk(ml), pk(vl), name="adamw_small")
    shapes = [wl[n].shape for n in small_names]
    for n, dd, mm, vv in zip(small_names, _unpack(d.reshape(-1), shapes), _unpack(mn.reshape(-1), shapes),
                             _unpack(vn.reshape(-1), shapes)):
        grads[n], delta[n], new_m[n], new_v[n] = g_small[n], dd, mm, vv

    return (loss, dx0[None], *[grads[n] for n in WEIGHTS], *[delta[n] for n in WEIGHTS],
            *[new_m[n] for n in WEIGHTS], *[new_v[n] for n in WEIGHTS])
```

```python
import functools
import math

import jax
import jax.numpy as jnp
from jax import lax
from jax.experimental import pallas as pl
from jax.experimental.pallas import tpu as pltpu

F32 = jnp.float32
BF16 = jnp.bfloat16

EPS = 1e-5
CHUNK = 256
WINDOW = 128
HEAD = 64
SSM_HEADS = 32
SSM_GROUPS = 8
SSM_STATE = 128
ATT_KV = 4
ATT_G = 4
ROPE_THETA = 10000.0
NEG = -1e30
VMEM_LIMIT = 56 * 1024 * 1024

ADAM_LR, ADAM_B1, ADAM_B2, ADAM_EPS, ADAM_WD, ADAM_STEP = 0.001, 0.9, 0.999, 1e-08, 0.01, 10


def _cp(n_axes):
    return pltpu.CompilerParams(dimension_semantics=("arbitrary",) * n_axes, vmem_limit_bytes=VMEM_LIMIT)


def _pick(dim, prefs):
    for p in prefs:
        if dim % p == 0:
            return p
    return dim


def _iota(shape, dim):
    return lax.broadcasted_iota(jnp.int32, shape, dim)


def _dot(a, b, ca=1, cb=0):
    return lax.dot_general(a, b, (((ca,), (cb,)), ((), ())), preferred_element_type=F32)


def _dot3(x, ind):
    h = x.astype(BF16)
    r = x - h.astype(F32)
    m = r.astype(BF16)
    lo = (r - m.astype(F32)).astype(BF16)
    return _dot(h, ind) + _dot(m, ind) + _dot(lo, ind)


def _sigmoid(x):
    return jax.nn.sigmoid(x)


def _mm(a, b, *, name, ta=False, tb=False, bias=None, res=None, out_dtype=F32, b_koff=0, tm=None, tn=None, tk=None):
    if ta:
        K, M = a.shape
    else:
        M, K = a.shape
    N = b.shape[0] if tb else b.shape[1]
    tm = tm or _pick(M, (1024, 512, 256, 128))
    tn = tn or _pick(N, (512, 256, 128))
    tk = tk or (K if K <= 2048 else _pick(K, (2048, 1408, 1024, 512)))
    assert M % tm == 0 and N % tn == 0 and K % tk == 0 and b_koff % tk == 0
    nk = K // tk
    kb0 = b_koff // tk
    has_bias, has_res = bias is not None, res is not None

    def body(*refs):
        a_ref, b_ref = refs[0], refs[1]
        pos = 2
        bias_ref = res_ref = acc_ref = None
        if has_bias:
            bias_ref = refs[pos]
            pos += 1
        if has_res:
            res_ref = refs[pos]
            pos += 1
        o_ref = refs[pos]
        if nk > 1:
            acc_ref = refs[pos + 1]
        part = _dot(a_ref[...].astype(BF16), b_ref[...].astype(BF16), 0 if ta else 1, 1 if tb else 0)

        def finish(acc):
            if has_bias:
                acc = acc + bias_ref[...]
            if has_res:
                acc = acc + res_ref[...]
            o_ref[...] = acc.astype(out_dtype)

        if nk == 1:
            finish(part)
        else:
            k = pl.program_id(2)

            @pl.when(k == 0)
            def _():
                acc_ref[...] = part

            @pl.when(k > 0)
            def _():
                acc_ref[...] += part

            @pl.when(k == nk - 1)
            def _():
                finish(acc_ref[...])

    a_spec = pl.BlockSpec((tk, tm), lambda i, j, k: (k, i)) if ta else pl.BlockSpec((tm, tk), lambda i, j, k: (i, k))
    b_spec = (pl.BlockSpec((tn, tk), lambda i, j, k: (j, k + kb0)) if tb
              else pl.BlockSpec((tk, tn), lambda i, j, k: (k + kb0, j)))
    in_specs, args = [a_spec, b_spec], [a, b]
    if has_bias:
        in_specs.append(pl.BlockSpec((1, tn), lambda i, j, k: (0, j)))
        args.append(bias)
    if has_res:
        in_specs.append(pl.BlockSpec((tm, tn), lambda i, j, k: (i, j)))
        args.append(res)
    return pl.pallas_call(
        body, name=name, grid=(M // tm, N // tn, nk), in_specs=in_specs,
        out_specs=pl.BlockSpec((tm, tn), lambda i, j, k: (i, j)),
        out_shape=jax.ShapeDtypeStruct((M, N), out_dtype),
        scratch_shapes=[pltpu.VMEM((tm, tn), F32)] if nk > 1 else [],
        compiler_params=_cp(3),
    )(*args)


def _rms_fwd(x, gains, *, name, tr=256):
    S, D = x.shape
    n = len(gains)

    def body(*refs):
        xv = refs[0][...]
        xh = xv * lax.rsqrt(jnp.mean(xv * xv, axis=-1, keepdims=True) + EPS)
        for q in range(n):
            refs[1 + n + q][...] = (xh * refs[1 + q][...]).astype(BF16)

    row = pl.BlockSpec((tr, D), lambda i: (i, 0))
    vec = pl.BlockSpec((1, D), lambda i: (0, 0))
    return pl.pallas_call(
        body, name=name, grid=(S // tr,), in_specs=[row] + [vec] * n, out_specs=[row] * n,
        out_shape=[jax.ShapeDtypeStruct((S, D), BF16)] * n, compiler_params=_cp(1),
    )(x, *gains)


def _rms_bwd(x, gains, dhs, dres, *, name, tr=256, want_colsum=False):
    S, D = x.shape
    n = len(gains)
    steps = S // tr

    def body(*refs):
        x_ref = refs[0]
        g_refs = refs[1:1 + n]
        dh_refs = refs[1 + n:1 + 2 * n]
        dres_ref = refs[1 + 2 * n]
        dx_ref = refs[2 + 2 * n]
        dg_refs = refs[3 + 2 * n:3 + 3 * n]
        cs_ref = refs[3 + 3 * n] if want_colsum else None
        i = pl.program_id(0)
        xv = x_ref[...]
        r = lax.rsqrt(jnp.mean(xv * xv, axis=-1, keepdims=True) + EPS)
        xh = xv * r
        dx = dres_ref[...]
        for q in range(n):
            dh = dh_refs[q][...]
            dxh = dh * g_refs[q][...]
            dx = dx + r * (dxh - xh * jnp.mean(dxh * xh, axis=-1, keepdims=True))
            part = jnp.sum(dh * xh, axis=0, keepdims=True)

            @pl.when(i == 0)
            def _():
                dg_refs[q][...] = part

            @pl.when(i > 0)
            def _():
                dg_refs[q][...] += part

        dx_ref[...] = dx
        if want_colsum:
            cpart = jnp.sum(dx, axis=0, keepdims=True)

            @pl.when(i == 0)
            def _():
                cs_ref[...] = cpart

            @pl.when(i > 0)
            def _():
                cs_ref[...] += cpart

    row = pl.BlockSpec((tr, D), lambda i: (i, 0))
    vec = pl.BlockSpec((1, D), lambda i: (0, 0))
    n_vec_out = n + (1 if want_colsum else 0)
    outs = pl.pallas_call(
        body, name=name, grid=(steps,), in_specs=[row] + [vec] * n + [row] * n + [row],
        out_specs=[row] + [vec] * n_vec_out,
        out_shape=[jax.ShapeDtypeStruct((S, D), F32)] + [jax.ShapeDtypeStruct((1, D), F32)] * n_vec_out,
        compiler_params=_cp(1),
    )(x, *gains, *dhs, dres)
    return outs


def _colsum(x, *, name, tr=256):
    S, D = x.shape

    def body(x_ref, o_ref):
        i = pl.program_id(0)
        part = jnp.sum(x_ref[...].astype(F32), axis=0, keepdims=True)

        @pl.when(i == 0)
        def _():
            o_ref[...] = part

        @pl.when(i > 0)
        def _():
            o_ref[...] += part

    return pl.pallas_call(
        body, name=name, grid=(S // tr,), in_specs=[pl.BlockSpec((tr, D), lambda i: (i, 0))],
        out_specs=pl.BlockSpec((1, D), lambda i: (0, 0)), out_shape=jax.ShapeDtypeStruct((1, D), F32),
        compiler_params=_cp(1),
    )(x)


def _loss(y, t, *, name="loss", tr=256):
    S, D = y.shape
    steps = S // tr

    def body(y_ref, t_ref, dy_ref, l_ref, acc_ref):
        i = pl.program_id(0)
        e = y_ref[...] - t_ref[...]
        dy_ref[...] = e * (1.0 / D)
        part = jnp.sum(e * e, axis=0, keepdims=True)

        @pl.when(i == 0)
        def _():
            acc_ref[...] = part

        @pl.when(i > 0)
        def _():
            acc_ref[...] += part

        @pl.when(i == steps - 1)
        def _():
            tot = jnp.sum(acc_ref[...], axis=1, keepdims=True) * (0.5 / D)
            l_ref[...] = jnp.broadcast_to(tot, (8, 128))

    row = pl.BlockSpec((tr, D), lambda i: (i, 0))
    return pl.pallas_call(
        body, name=name, grid=(steps,), in_specs=[row, row],
        out_specs=[row, pl.BlockSpec((8, 128), lambda i: (0, 0))],
        out_shape=[jax.ShapeDtypeStruct((S, D), F32), jax.ShapeDtypeStruct((8, 128), F32)],
        scratch_shapes=[pltpu.VMEM((1, D), F32)], compiler_params=_cp(1),
    )(y, t)


def _conv_pre(x, w_ref, b_ref, width):
    S = x.shape[0]
    row = _iota(x.shape, 0)
    acc = b_ref[...] + w_ref[pl.ds(width - 1, 1), :] * x
    shifted = []
    for s in range(1, width):
        xs = jnp.where(row >= s, pltpu.roll(x, s, axis=0), 0.0)
        shifted.append(xs)
        acc = acc + w_ref[pl.ds(width - 1 - s, 1), :] * xs
    return acc, shifted


def _conv_back(dacc, x, shifted, w_ref, width):
    S = x.shape[0]
    row = _iota(x.shape, 0)
    dx = w_ref[pl.ds(width - 1, 1), :] * dacc
    dws = [None] * width
    dws[width - 1] = jnp.sum(dacc * x, axis=0, keepdims=True)
    for s in range(1, width):
        back = jnp.where(row < S - s, pltpu.roll(dacc, S - s, axis=0), 0.0)
        dx = dx + w_ref[pl.ds(width - 1 - s, 1), :] * back
        dws[width - 1 - s] = jnp.sum(dacc * shifted[s - 1], axis=0, keepdims=True)
    db = jnp.sum(dacc, axis=0, keepdims=True)
    return dx, dws, db


def _conv_silu_fwd(xin, col0, C, w, b, *, name, tc=512):
    S = xin.shape[0]
    width = w.shape[0]
    off = col0 // tc

    def body(x_ref, w_ref, b_ref, o_ref):
        acc, _ = _conv_pre(x_ref[...], w_ref, b_ref, width)
        o_ref[...] = acc * _sigmoid(acc)

    return pl.pallas_call(
        body, name=name, grid=(C // tc,),
        in_specs=[pl.BlockSpec((S, tc), lambda j: (0, j + off)), pl.BlockSpec((width, tc), lambda j: (0, j)),
                  pl.BlockSpec((1, tc), lambda j: (0, j))],
        out_specs=pl.BlockSpec((S, tc), lambda j: (0, j)), out_shape=jax.ShapeDtypeStruct((S, C), F32),
        compiler_params=_cp(1),
    )(xin, w, b)


def _conv_silu_bwd(xin, col0, C, w, b, douts, *, name, tc=256):
    S = xin.shape[0]
    width = w.shape[0]
    off = col0 // tc
    nd = len(douts)
    ranges = [(o // tc, (o + d.shape[1]) // tc) for d, o in douts]

    def body(*refs):
        x_ref, w_ref, b_ref = refs[0], refs[1], refs[2]
        d_refs = refs[3:3 + nd]
        dx_ref, dw_ref, db_ref = refs[3 + nd], refs[4 + nd], refs[5 + nd]
        j = pl.program_id(0)
        x = x_ref[...]
        acc, shifted = _conv_pre(x, w_ref, b_ref, width)
        sg = _sigmoid(acc)
        dout = jnp.zeros_like(x)
        for q in range(nd):
            lo, hi = ranges[q]
            dout = dout + jnp.where((j >= lo) & (j < hi), d_refs[q][...], 0.0)
        dacc = dout * (sg * (1.0 + acc * (1.0 - sg)))
        dx, dws, db = _conv_back(dacc, x, shifted, w_ref, width)
        dx_ref[...] = dx.astype(BF16)
        for k in range(width):
            dw_ref[pl.ds(k, 1), :] = dws[k]
        db_ref[...] = db

    d_specs = [pl.BlockSpec((S, tc), (lambda j, lo=lo, hi=hi: (0, jnp.clip(j - lo, 0, hi - lo - 1)))) for lo, hi in ranges]
    return pl.pallas_call(
        body, name=name, grid=(C // tc,),
        in_specs=[pl.BlockSpec((S, tc), lambda j: (0, j + off)), pl.BlockSpec((width, tc), lambda j: (0, j)),
                  pl.BlockSpec((1, tc), lambda j: (0, j))] + d_specs,
        out_specs=[pl.BlockSpec((S, tc), lambda j: (0, j)), pl.BlockSpec((width, tc), lambda j: (0, j)),
                   pl.BlockSpec((1, tc), lambda j: (0, j))],
        out_shape=[jax.ShapeDtypeStruct((S, C), BF16), jax.ShapeDtypeStruct((width, C), F32),
                   jax.ShapeDtypeStruct((1, C), F32)],
        compiler_params=_cp(1),
    )(xin, w, b, *[d for d, _ in douts])


def _ffn_act_fwd(u, w, b, *, name, tc=256):
    S, F2 = u.shape
    Fd = F2 // 2
    width = w.shape[0]
    nb = Fd // tc

    def body(g_ref, v_ref, w_ref, b_ref, o_ref):
        acc, _ = _conv_pre(g_ref[...], w_ref, b_ref, width)
        o_ref[...] = (acc * _sigmoid(acc) * v_ref[...]).astype(BF16)

    return pl.pallas_call(
        body, name=name, grid=(nb,),
        in_specs=[pl.BlockSpec((S, tc), lambda j: (0, j)), pl.BlockSpec((S, tc), lambda j: (0, j + nb)),
                  pl.BlockSpec((width, tc), lambda j: (0, j)), pl.BlockSpec((1, tc), lambda j: (0, j))],
        out_specs=pl.BlockSpec((S, tc), lambda j: (0, j)), out_shape=jax.ShapeDtypeStruct((S, Fd), BF16),
        compiler_params=_cp(1),
    )(u, u, w, b)


def _ffn_act_bwd(u, w, b, da, *, name, tc=256):
    S, F2 = u.shape
    Fd = F2 // 2
    width = w.shape[0]
    nb = Fd // tc

    def body(g_ref, v_ref, w_ref, b_ref, da_ref, dg_ref, dv_ref, dw_ref, db_ref):
        x = g_ref[...]
        acc, shifted = _conv_pre(x, w_ref, b_ref, width)
        sg = _sigmoid(acc)
        dav = da_ref[...]
        dv_ref[...] = (dav * acc * sg).astype(BF16)
        dacc = dav * v_ref[...] * (sg * (1.0 + acc * (1.0 - sg)))
        dx, dws, db = _conv_back(dacc, x, shifted, w_ref, width)
        dg_ref[...] = dx.astype(BF16)
        for k in range(width):
            dw_ref[pl.ds(k, 1), :] = dws[k]
        db_ref[...] = db

    blk = pl.BlockSpec((S, tc), lambda j: (0, j))
    return pl.pallas_call(
        body, name=name, grid=(nb,),
        in_specs=[blk, pl.BlockSpec((S, tc), lambda j: (0, j + nb)), pl.BlockSpec((width, tc), lambda j: (0, j)),
                  pl.BlockSpec((1, tc), lambda j: (0, j)), blk],
        out_specs=[blk, blk, pl.BlockSpec((width, tc), lambda j: (0, j)), pl.BlockSpec((1, tc), lambda j: (0, j))],
        out_shape=[jax.ShapeDtypeStruct((S, Fd), BF16), jax.ShapeDtypeStruct((S, Fd), BF16),
                   jax.ShapeDtypeStruct((width, Fd), F32), jax.ShapeDtypeStruct((1, Fd), F32)],
        compiler_params=_cp(1),
    )(u, u, w, b, da)


def _ssd_prep(dtr, dt_bias, a_log, *, name="ssd_prep"):
    S = dtr.shape[0]

    def body(d_ref, b_ref, al_ref, dt_ref, ac_ref, sg_ref):
        lane = _iota((CHUNK, 128), 1)
        valid = lane < SSM_HEADS
        z = d_ref[...] + b_ref[...]
        dt = jnp.where(valid, jnp.maximum(z, 0.0) + jnp.log(1.0 + jnp.exp(-jnp.abs(z))), 0.0)
        a = dt * (-jnp.exp(al_ref[...]))
        row = _iota((CHUNK, 128), 0)
        k = 1
        while k < CHUNK:
            a = a + jnp.where(row >= k, pltpu.roll(a, k, axis=0), 0.0)
            k *= 2
        dt_ref[...] = dt
        ac_ref[...] = a
        sg_ref[...] = jnp.where(valid, _sigmoid(z), 0.0)

    blk = pl.BlockSpec((CHUNK, 128), lambda i: (i, 0))
    vec = pl.BlockSpec((1, 128), lambda i: (0, 0))
    return pl.pallas_call(
        body, name=name, grid=(S // CHUNK,), in_specs=[blk, vec, vec], out_specs=[blk, blk, blk],
        out_shape=[jax.ShapeDtypeStruct((S, 128), F32)] * 3, compiler_params=_cp(1),
    )(dtr, dt_bias, a_log)


def _to_groups(v):
    S = v.shape[0]
    g = v[:, :SSM_HEADS].reshape(S, SSM_GROUPS, 4).transpose(1, 0, 2)
    return jnp.pad(g, ((0, 0), (0, 0), (0, 124)))


def _from_groups(vg):
    S = vg.shape[1]
    v = vg[:, :, :4].transpose(1, 0, 2).reshape(S, SSM_HEADS)
    return jnp.pad(v, ((0, 0), (0, 128 - SSM_HEADS)))


def _expand4(v, lanes):
    out = jnp.broadcast_to(v[:, 3:4], lanes.shape)
    for hh in (2, 1, 0):
        out = jnp.where(lanes < 64 * (hh + 1), v[:, hh:hh + 1], out)
    return out


def _ssd_fwd(xbc, dt_g, ac_g, ac_t, *, name="ssd_fwd"):
    S = xbc.shape[0]
    nc = S // CHUNK
    Lc = CHUNK

    def body(x_ref, b_ref, c_ref, dt_ref, ac_ref, act_ref, y_ref, st_out_ref, st_ref):
        g = pl.program_id(0)
        c = pl.program_id(1)

        @pl.when(c == 0)
        def _():
            st_ref[...] = jnp.zeros_like(st_ref)

        bv = b_ref[...]
        cbf = c_ref[...].astype(BF16)
        cb = _dot(cbf, bv.astype(BF16), 1, 1)
        causal = _iota((Lc, Lc), 0) >= _iota((Lc, Lc), 1)
        lane256 = _iota((Lc, 256), 1)
        lane128 = _iota((Lc, 128), 1)
        row128 = _iota((128, 128), 0)
        dtg, acg = dt_ref[...], ac_ref[...]
        ac_last = ac_ref[pl.ds(Lc - 1, 1), :]
        dt4 = _expand4(dtg, lane256)
        ac4 = _expand4(acg, lane256)
        e4 = jnp.exp(ac4)
        xdb = (x_ref[...] * dt4).astype(BF16)
        st_out_ref[...] = st_ref[...]
        for p in range(2):
            xd_p = xdb[:, 128 * p:128 * (p + 1)]
            st_p = st_ref[p]
            ys, sn, cds = [], [], []
            for q in range(2):
                hh = 2 * p + q
                a_col = acg[:, hh:hh + 1]
                a_row = act_ref[pl.ds(4 * g + hh, 1), :]
                dec = jnp.exp(jnp.where(causal, a_col - a_row, NEG))
                w = (cb * dec).astype(BF16)
                ys.append(_dot(w, xd_p))
                al = ac_last[:, hh:hh + 1]
                dte = jnp.exp(al - a_col)
                sn.append(_dot(xd_p, (bv * dte).astype(BF16), 0, 0))
                cds.append(jnp.exp(al))
            y_diag = jnp.where(lane128 < 64, ys[0], ys[1])
            y_off = _dot(cbf, st_p.astype(BF16), 1, 1) * e4[:, 128 * p:128 * (p + 1)]
            y_ref[:, 128 * p:128 * (p + 1)] = y_diag + y_off
            st_ref[p] = jnp.where(row128 < 64, st_p * cds[0] + sn[0], st_p * cds[1] + sn[1])

    per_g = lambda g, c: (g, c, 0)
    return pl.pallas_call(
        body, name=name, grid=(SSM_GROUPS, nc),
        in_specs=[pl.BlockSpec((Lc, 256), lambda g, c: (c, g)),
                  pl.BlockSpec((Lc, 128), lambda g, c: (c, 16 + g)),
                  pl.BlockSpec((Lc, 128), lambda g, c: (c, 24 + g)),
                  pl.BlockSpec((None, Lc, 128), per_g), pl.BlockSpec((None, Lc, 128), per_g),
                  pl.BlockSpec((SSM_HEADS, Lc), lambda g, c: (0, c))],
        out_specs=[pl.BlockSpec((Lc, 256), lambda g, c: (c, g)),
                   pl.BlockSpec((None, None, 2, 128, 128), lambda g, c: (g, c, 0, 0, 0))],
        out_shape=[jax.ShapeDtypeStruct((S, 2048), F32), jax.ShapeDtypeStruct((SSM_GROUPS, nc, 2, 128, 128), F32)],
        scratch_shapes=[pltpu.VMEM((2, 128, 128), F32)], compiler_params=_cp(2),
    )(xbc, xbc, xbc, dt_g, ac_g, ac_t)


def _ssd_bwd(xbc, dt_g, ac_g, ac_t, states, dy, dexp, *, name="ssd_bwd"):
    S = xbc.shape[0]
    nc = S // CHUNK
    Lc = CHUNK

    def body(x_ref, b_ref, c_ref, dt_ref, ac_ref, act_ref, st_ref, dy_ref, d_ref, dx_ref, db_ref, dc_ref, dh_ref, ds_ref):
        g = pl.program_id(0)
        cc = pl.program_id(1)

        @pl.when(cc == 0)
        def _():
            ds_ref[...] = jnp.zeros_like(ds_ref)

        bv = b_ref[...]
        cv = c_ref[...]
        bbf, cbf = bv.astype(BF16), cv.astype(BF16)
        cb = _dot(cbf, bbf, 1, 1)
        causal = _iota((Lc, Lc), 0) >= _iota((Lc, Lc), 1)
        lane256 = _iota((Lc, 256), 1)
        lane128 = _iota((Lc, 128), 1)
        row128 = _iota((128, 128), 0)
        dtg, acg = dt_ref[...], ac_ref[...]
        ac_last = ac_ref[pl.ds(Lc - 1, 1), :]
        dt4 = _expand4(dtg, lane256)
        ac4 = _expand4(acg, lane256)
        acl4 = _expand4(ac_last, _iota((1, 256), 1))
        e4 = jnp.exp(ac4)
        dte4 = jnp.exp(acl4 - ac4)
        xv = x_ref[...]
        xd = xv * dt4
        xdb = xd.astype(BF16)
        dyv = dy_ref[...]
        dcb = jnp.zeros((Lc, Lc), F32)
        dc_acc = jnp.zeros((Lc, 128), F32)
        db_acc = jnp.zeros((Lc, 128), F32)
        ind_rows = _iota((256, 128), 0) >> 6
        ind_cols = _iota((256, 128), 1)
        ind_a = (ind_rows == ind_cols).astype(BF16)
        ind_b = (ind_rows + 4 == ind_cols).astype(BF16)
        u_parts, t_parts, dxd_parts, ends = [], [], [], []
        for p in range(2):
            sl = slice(128 * p, 128 * (p + 1))
            xd_p, xdb_p, dy_p = xd[:, sl], xdb[:, sl], dyv[:, sl]
            dyb_p = dy_p.astype(BF16)
            e_p, dte_p = e4[:, sl], dte4[:, sl]
            sp = st_ref[p]
            spb = sp.astype(BF16)
            dsn = ds_ref[p]
            dsnb = dsn.astype(BF16)
            yds, dxds, cds = [], [], []
            for q in range(2):
                hh = 2 * p + q
                a_col = acg[:, hh:hh + 1]
                a_row = act_ref[pl.ds(4 * g + hh, 1), :]
                dec = jnp.exp(jnp.where(causal, a_col - a_row, NEG))
                w = (cb * dec).astype(BF16)
                head = (lane128 < 64) if q == 0 else (lane128 >= 64)
                dym = jnp.where(head, dyb_p, jnp.zeros_like(dyb_p))
                dw = _dot(dym, xdb_p, 1, 1)
                dcb = dcb + dw * dec
                yds.append(_dot(w, xdb_p))
                dxds.append(_dot(w, dyb_p, 0, 0))
                cds.append(jnp.exp(ac_last[:, hh:hh + 1]))
            y_diag = jnp.where(lane128 < 64, yds[0], yds[1])
            dxd_diag = jnp.where(lane128 < 64, dxds[0], dxds[1])
            y_off = _dot(cbf, spb, 1, 1) * e_p
            dgp = dy_p * e_p
            dgb = dgp.astype(BF16)
            dc_acc = dc_acc + _dot(dgb, spb)
            dsp = _dot(dgb, cbf, 0, 0)
            cd_col = jnp.where(row128[:, 0:1] < 64, cds[0], cds[1])
            qm = _dot(bbf, dsnb, 1, 1)
            dxd_state = dte_p * qm
            db_acc = db_acc + _dot((xd_p * dte_p).astype(BF16), dsnb)
            t_p = xd_p * dxd_state
            prod = dsn * sp
            e0 = jnp.sum(jnp.sum(jnp.where(row128 < 64, prod, 0.0), axis=1, keepdims=True), axis=0, keepdims=True)
            e1 = jnp.sum(jnp.sum(jnp.where(row128 >= 64, prod, 0.0), axis=1, keepdims=True), axis=0, keepdims=True)
            tcol = jnp.sum(t_p, axis=0, keepdims=True)
            lane1 = _iota((1, 128), 1)
            t0 = jnp.sum(jnp.where(lane1 < 64, tcol, 0.0), axis=1, keepdims=True)
            t1 = jnp.sum(jnp.where(lane1 >= 64, tcol, 0.0), axis=1, keepdims=True)
            ends.append(e0 * cds[0] + t0)
            ends.append(e1 * cds[1] + t1)
            ds_ref[p] = dsn * cd_col + dsp
            u_parts.append(dyb_p.astype(F32) * y_diag - xdb_p.astype(F32) * dxd_diag + dy_p * y_off - t_p)
            dxd_parts.append(dxd_diag + dxd_state)
        dxd = jnp.concatenate(dxd_parts, axis=1)
        u_all = jnp.concatenate(u_parts, axis=1)
        dx_ref[...] = dxd * dt4 + dyv * d_ref[...]
        dcbb = dcb.astype(BF16)
        dc_ref[...] = dc_acc + _dot(dcbb, bbf)
        db_ref[...] = db_acc + _dot(dcbb, cbf, 0, 0)
        lane = _iota((Lc, 128), 1)
        endv = jnp.zeros((Lc, 128), F32)
        for hh in range(4):
            endv = jnp.where(lane == 8 + hh, ends[hh], endv)
        dh_ref[...] = _dot3(dxd * xv, ind_a) + _dot3(u_all, ind_b) + endv

    rev = lambda c: nc - 1 - c
    per_g = lambda g, c: (g, rev(c), 0)
    return pl.pallas_call(
        body, name=name, grid=(SSM_GROUPS, nc),
        in_specs=[pl.BlockSpec((Lc, 256), lambda g, c: (rev(c), g)),
                  pl.BlockSpec((Lc, 128), lambda g, c: (rev(c), 16 + g)),
                  pl.BlockSpec((Lc, 128), lambda g, c: (rev(c), 24 + g)),
                  pl.BlockSpec((None, Lc, 128), per_g), pl.BlockSpec((None, Lc, 128), per_g),
                  pl.BlockSpec((SSM_HEADS, Lc), lambda g, c: (0, rev(c))),
                  pl.BlockSpec((None, None, 2, 128, 128), lambda g, c: (g, rev(c), 0, 0, 0)),
                  pl.BlockSpec((Lc, 256), lambda g, c: (rev(c), g)),
                  pl.BlockSpec((1, 256), lambda g, c: (0, g))],
        out_specs=[pl.BlockSpec((Lc, 256), lambda g, c: (rev(c), g)),
                   pl.BlockSpec((Lc, 128), lambda g, c: (rev(c), g)),
                   pl.BlockSpec((Lc, 128), lambda g, c: (rev(c), g)),
                   pl.BlockSpec((None, Lc, 128), per_g)],
        out_shape=[jax.ShapeDtypeStruct((S, 2048), F32), jax.ShapeDtypeStruct((S, 1024), F32),
                   jax.ShapeDtypeStruct((S, 1024), F32), jax.ShapeDtypeStruct((SSM_GROUPS, S, 128), F32)],
        scratch_shapes=[pltpu.VMEM((2, 128, 128), F32)], compiler_params=_cp(2),
    )(xbc, xbc, xbc, dt_g, ac_g, ac_t, states, dy, dexp)


def _ssd_post(dhead, dt_g, sg_g, alog_g, *, name="ssd_post"):
    S = dhead.shape[1]
    nc = S // CHUNK
    Lc = CHUNK

    def body(dh_ref, dt_ref, sg_ref, al_ref, o_ref, s_ref):
        c = pl.program_id(1)
        dh = dh_ref[...]
        a_neg = -jnp.exp(al_ref[...])
        lane = _iota((Lc, 128), 1)
        row = _iota((Lc, 128), 0)
        dac = jnp.where(lane < 4, pltpu.roll(dh, 124, axis=1), 0.0)
        end = jnp.where(lane < 4, pltpu.roll(dh, 120, axis=1), 0.0)
        k = 1
        while k < Lc:
            dac = dac + jnp.where(row < Lc - k, pltpu.roll(dac, Lc - k, axis=0), 0.0)
            k *= 2
        da = dac + end
        dtv = dt_ref[...]
        ddt = jnp.where(lane < 4, da * a_neg + dh, 0.0)
        ddtr = ddt * sg_ref[...]
        o_ref[...] = ddtr
        dal = jnp.sum(da * dtv, axis=0, keepdims=True) * a_neg
        dbias = jnp.sum(ddtr, axis=0, keepdims=True)
        row8 = _iota((8, 128), 0)
        part = jnp.where(row8 == 0, dal, jnp.where(row8 == 1, dbias, 0.0))

        @pl.when(c == 0)
        def _():
            s_ref[...] = part

        @pl.when(c > 0)
        def _():
            s_ref[...] += part

    per = pl.BlockSpec((None, Lc, 128), lambda g, c: (g, c, 0))
    return pl.pallas_call(
        body, name=name, grid=(SSM_GROUPS, nc),
        in_specs=[per, per, per, pl.BlockSpec((None, 1, 128), lambda g, c: (g, 0, 0))],
        out_specs=[per, pl.BlockSpec((None, 8, 128), lambda g, c: (g, 0, 0))],
        out_shape=[jax.ShapeDtypeStruct((SSM_GROUPS, S, 128), F32), jax.ShapeDtypeStruct((SSM_GROUPS, 8, 128), F32)],
        compiler_params=_cp(2),
    )(dhead, dt_g, sg_g, alog_g)


def _gate_fwd(y, xbc, zx, dexp, gn, *, name="gate_fwd", tr=256):
    S = y.shape[0]
    W = 2048
    gw = W // SSM_GROUPS

    def body(y_ref, x_ref, z_ref, d_ref, g_ref, o_ref):
        z = z_ref[...]
        u = (y_ref[...] + x_ref[...] * d_ref[...]) * (z * _sigmoid(z))
        gv = g_ref[...]
        for q in range(SSM_GROUPS):
            sl = slice(gw * q, gw * (q + 1))
            uq = u[:, sl]
            r = lax.rsqrt(jnp.mean(uq * uq, axis=-1, keepdims=True) + EPS)
            o_ref[:, sl] = (uq * r * gv[:, sl]).astype(BF16)

    row = pl.BlockSpec((tr, W), lambda i: (i, 0))
    vec = pl.BlockSpec((1, W), lambda i: (0, 0))
    return pl.pallas_call(
        body, name=name, grid=(S // tr,), in_specs=[row, row, row, vec, vec], out_specs=row,
        out_shape=jax.ShapeDtypeStruct((S, W), BF16), compiler_params=_cp(1),
    )(y, xbc, zx, dexp, gn)


def _gate_bwd(y, xbc, zx, dexp, gn, dout, *, name="gate_bwd", tr=256):
    S = y.shape[0]
    W = 2048
    gw = W // SSM_GROUPS
    steps = S // tr

    def body(y_ref, x_ref, z_ref, d_ref, g_ref, do_ref, dy_ref, dz_ref, dg_ref, dd_ref, acc_ref):
        i = pl.program_id(0)

        @pl.when(i == 0)
        def _():
            acc_ref[...] = jnp.zeros_like(acc_ref)

        z = z_ref[...]
        sg = _sigmoid(z)
        sz = z * sg
        xs = x_ref[...]
        yt = y_ref[...] + xs * d_ref[...]
        u = yt * sz
        gv = g_ref[...]
        do = do_ref[...]
        dgs = []
        for q in range(SSM_GROUPS):
            sl = slice(gw * q, gw * (q + 1))
            uq = u[:, sl]
            r = lax.rsqrt(jnp.mean(uq * uq, axis=-1, keepdims=True) + EPS)
            uh = uq * r
            dq = do[:, sl]
            duh = dq * gv[:, sl]
            duq = r * (duh - uh * jnp.mean(duh * uh, axis=-1, keepdims=True))
            dgs.append(jnp.sum(dq * uh, axis=0, keepdims=True))
            dyt = duq * sz[:, sl]
            dy_ref[:, sl] = dyt
            dz_ref[:, sl] = (duq * yt[:, sl] * (sg[:, sl] * (1.0 + z[:, sl] * (1.0 - sg[:, sl])))).astype(BF16)
            acc_ref[:, sl] += jnp.sum(dyt * xs[:, sl], axis=0, keepdims=True)
        dg = jnp.concatenate(dgs, axis=1)

        @pl.when(i == 0)
        def _():
            dg_ref[...] = dg

        @pl.when(i > 0)
        def _():
            dg_ref[...] += dg

        @pl.when(i == steps - 1)
        def _():
            ind = ((_iota((W, 128), 0) >> 6) == _iota((W, 128), 1)).astype(BF16)
            dd_ref[...] = _dot3(jnp.broadcast_to(acc_ref[...], (8, W)), ind)[0:1, :]

    row = pl.BlockSpec((tr, W), lambda i: (i, 0))
    vec = pl.BlockSpec((1, W), lambda i: (0, 0))
    return pl.pallas_call(
        body, name=name, grid=(steps,), in_specs=[row, row, row, vec, vec, row],
        out_specs=[row, row, vec, pl.BlockSpec((1, 128), lambda i: (0, 0))],
        out_shape=[jax.ShapeDtypeStruct((S, W), F32), jax.ShapeDtypeStruct((S, W), BF16),
                   jax.ShapeDtypeStruct((1, W), F32), jax.ShapeDtypeStruct((1, 128), F32)],
        scratch_shapes=[pltpu.VMEM((1, W), F32)], compiler_params=_cp(1),
    )(y, xbc, zx, dexp, gn, dout)


def _rope_tables(pos, shape):
    lane = _iota(shape, 1)
    j = (lane & 31).astype(F32)
    inv = jnp.exp(j * (-math.log(ROPE_THETA) / 32.0))
    ang = pos * inv
    return jnp.cos(ang), jnp.sin(ang), (lane & 63) < 32


def _hn_inds(W):
    ind = ((_iota((W, 128), 0) >> 6) == _iota((W, 128), 1)).astype(BF16)
    ind_t = ((_iota((128, W), 1) >> 6) == _iota((128, W), 0)).astype(BF16)
    return ind, ind_t


def _hnrope_fwd(xin, col0, W, gain_w, posf, *, name, tr=256):
    S = xin.shape[0]
    off = col0 // W

    def body(x_ref, g_ref, p_ref, o_ref):
        x = x_ref[...]
        ind, ind_t = _hn_inds(W)
        r = lax.rsqrt(_dot3(x * x, ind) * (1.0 / HEAD) + EPS)
        xn = x * _dot3(r, ind_t) * g_ref[...]
        cs, sn, half = _rope_tables(p_ref[...], (tr, W))
        rot = jnp.where(half, -pltpu.roll(xn, W - 32, axis=1), pltpu.roll(xn, 32, axis=1))
        o_ref[...] = (xn * cs + rot * sn).astype(BF16)

    return pl.pallas_call(
        body, name=name, grid=(S // tr,),
        in_specs=[pl.BlockSpec((tr, W), lambda i: (i, off)), pl.BlockSpec((1, W), lambda i: (0, 0)),
                  pl.BlockSpec((tr, 1), lambda i: (i, 0))],
        out_specs=pl.BlockSpec((tr, W), lambda i: (i, 0)), out_shape=jax.ShapeDtypeStruct((S, W), BF16),
        compiler_params=_cp(1),
    )(xin, gain_w, posf)


def _hnrope_bwd(xin, col0, W, gain_w, posf, dout, *, name, tr=256):
    S = xin.shape[0]
    off = col0 // W
    steps = S // tr

    def body(x_ref, g_ref, p_ref, do_ref, dx_ref, cs_ref, dg_ref, acc_ref):
        i = pl.program_id(0)
        x = x_ref[...]
        ind, ind_t = _hn_inds(W)
        r = lax.rsqrt(_dot3(x * x, ind) * (1.0 / HEAD) + EPS)
        rw = _dot3(r, ind_t)
        xh = x * rw
        cs, sn, half = _rope_tables(p_ref[...], (tr, W))
        do = do_ref[...].astype(F32)
        gs = do * sn
        g1 = do * cs + jnp.where(half, pltpu.roll(gs, W - 32, axis=1), -pltpu.roll(gs, 32, axis=1))
        dxh = g1 * g_ref[...]
        t = _dot3(dxh * xh, ind) * (1.0 / HEAD)
        dx = rw * (dxh - xh * _dot3(t, ind_t))
        dx_ref[...] = dx.astype(BF16)
        cpart = jnp.sum(dx, axis=0, keepdims=True)
        gpart = jnp.sum(g1 * xh, axis=0, keepdims=True)

        @pl.when(i == 0)
        def _():
            cs_ref[...] = cpart
            acc_ref[...] = gpart

        @pl.when(i > 0)
        def _():
            cs_ref[...] += cpart
            acc_ref[...] += gpart

        @pl.when(i == steps - 1)
        def _():
            fold = ((_iota((W, 128), 0) & 63) == _iota((W, 128), 1)).astype(BF16)
            dg_ref[...] = _dot3(jnp.broadcast_to(acc_ref[...], (8, W)), fold)[0:1, :]

    return pl.pallas_call(
        body, name=name, grid=(steps,),
        in_specs=[pl.BlockSpec((tr, W), lambda i: (i, off)), pl.BlockSpec((1, W), lambda i: (0, 0)),
                  pl.BlockSpec((tr, 1), lambda i: (i, 0)), pl.BlockSpec((tr, W), lambda i: (i, 0))],
        out_specs=[pl.BlockSpec((tr, W), lambda i: (i, 0)), pl.BlockSpec((1, W), lambda i: (0, 0)),
                   pl.BlockSpec((1, 128), lambda i: (0, 0))],
        out_shape=[jax.ShapeDtypeStruct((S, W), BF16), jax.ShapeDtypeStruct((1, W), F32),
                   jax.ShapeDtypeStruct((1, 128), F32)],
        scratch_shapes=[pltpu.VMEM((1, W), F32)], compiler_params=_cp(1),
    )(xin, gain_w, posf, dout)


def _attn_probs(q, kb, sink_ref, h, i):
    s = _dot(q, kb, 1, 1) * (HEAD ** -0.5)
    r = _iota((4 * WINDOW, 2 * WINDOW), 0)
    ki = _iota((4 * WINDOW, 2 * WINDOW), 1)
    rel = (r & (WINDOW - 1)) + WINDOW - ki
    mask = (rel >= 0) & (rel < WINDOW) & ((ki >= WINDOW) | (i > 0))
    s = jnp.where(mask, s, NEG)
    r1 = _iota((4 * WINDOW, 1), 0)
    sink = jnp.where(r1 < WINDOW, sink_ref[4 * h], jnp.where(r1 < 2 * WINDOW, sink_ref[4 * h + 1],
                     jnp.where(r1 < 3 * WINDOW, sink_ref[4 * h + 2], sink_ref[4 * h + 3])))
    m = jnp.maximum(jnp.max(s, axis=1, keepdims=True), sink)
    p = jnp.exp(s - m)
    ps = jnp.exp(sink - m)
    inv = 1.0 / (jnp.sum(p, axis=1, keepdims=True) + ps)
    return p * inv, ps * inv


def _attn_specs(S):
    qspec = pl.BlockSpec((None, ATT_G, WINDOW, HEAD), lambda h, i: (h, 0, i, 0))
    cur = pl.BlockSpec((None, WINDOW, HEAD), lambda h, i: (h, i, 0))
    prev = pl.BlockSpec((None, WINDOW, HEAD), lambda h, i: (h, jnp.maximum(i - 1, 0), 0))
    return qspec, cur, prev


def _attn_fwd(qh, kh, vh, sinks, *, name="attn_fwd"):
    S = kh.shape[1]
    nb = S // WINDOW

    def body(s_ref, q_ref, kc_ref, kp_ref, vc_ref, vp_ref, o_ref):
        h, i = pl.program_id(0), pl.program_id(1)
        q = q_ref[...].reshape(ATT_G * WINDOW, HEAD)
        kb = jnp.concatenate([kp_ref[...], kc_ref[...]], axis=0)
        vb = jnp.concatenate([vp_ref[...], vc_ref[...]], axis=0)
        probs, _ = _attn_probs(q, kb, s_ref, h, i)
        o = _dot(probs.astype(BF16), vb)
        o_ref[...] = o.reshape(ATT_G, WINDOW, HEAD).astype(BF16)

    qspec, cur, prev = _attn_specs(S)
    return pl.pallas_call(
        body, name=name, grid=(ATT_KV, nb),
        in_specs=[pl.BlockSpec(memory_space=pltpu.SMEM), qspec, cur, prev, cur, prev], out_specs=qspec,
        out_shape=jax.ShapeDtypeStruct((ATT_KV, ATT_G, S, HEAD), BF16), compiler_params=_cp(2),
    )(sinks, qh, kh, kh, vh, vh)


def _attn_bwd(qh, kh, vh, sinks, doh, *, name="attn_bwd"):
    S = kh.shape[1]
    nb = S // WINDOW

    def body(s_ref, q_ref, kc_ref, kp_ref, vc_ref, vp_ref, do_ref, dq_ref, dk_ref, dv_ref, dsk_ref):
        h, i = pl.program_id(0), pl.program_id(1)

        @pl.when(i == 0)
        def _():
            dk_ref[...] = jnp.zeros_like(dk_ref)
            dv_ref[...] = jnp.zeros_like(dv_ref)
            dsk_ref[...] = jnp.zeros_like(dsk_ref)

        q = q_ref[...].reshape(ATT_G * WINDOW, HEAD)
        do = do_ref[...].reshape(ATT_G * WINDOW, HEAD)
        kb = jnp.concatenate([kp_ref[...], kc_ref[...]], axis=0)
        vb = jnp.concatenate([vp_ref[...], vc_ref[...]], axis=0)
        probs, psink = _attn_probs(q, kb, s_ref, h, i)
        dp = _dot(do, vb, 1, 1)
        delta = jnp.sum(probs * dp, axis=1, keepdims=True)
        ds = (probs * (dp - delta)).astype(BF16)
        scale = HEAD ** -0.5
        dq_ref[...] = (_dot(ds, kb) * scale).reshape(ATT_G, WINDOW, HEAD)
        dkb = _dot(ds, q, 0, 0) * scale
        dvb = _dot(probs.astype(BF16), do, 0, 0)
        cur = pl.multiple_of(i * WINDOW, WINDOW)
        dk_ref[pl.ds(cur, WINDOW), :] += dkb[WINDOW:, :]
        dv_ref[pl.ds(cur, WINDOW), :] += dvb[WINDOW:, :]

        @pl.when(i > 0)
        def _():
            prv = pl.multiple_of((i - 1) * WINDOW, WINDOW)
            dk_ref[pl.ds(prv, WINDOW), :] += dkb[:WINDOW, :]
            dv_ref[pl.ds(prv, WINDOW), :] += dvb[:WINDOW, :]

        dsr = -psink * delta
        lane = _iota((8, 128), 1)
        row = _iota((8, 128), 0)
        upd = jnp.zeros((8, 128), F32)
        for gq in range(ATT_G):
            v = jnp.sum(dsr[gq * WINDOW:(gq + 1) * WINDOW, :], axis=0, keepdims=True)
            upd = jnp.where((lane == gq) & (row == 0), v, upd)
        dsk_ref[...] += upd

    qspec, cur, prev = _attn_specs(S)
    full = pl.BlockSpec((None, S, HEAD), lambda h, i: (h, 0, 0))
    return pl.pallas_call(
        body, name=name, grid=(ATT_KV, nb),
        in_specs=[pl.BlockSpec(memory_space=pltpu.SMEM), qspec, cur, prev, cur, prev, qspec],
        out_specs=[qspec, full, full, pl.BlockSpec((None, 8, 128), lambda h, i: (h, 0, 0))],
        out_shape=[jax.ShapeDtypeStruct((ATT_KV, ATT_G, S, HEAD), F32), jax.ShapeDtypeStruct((ATT_KV, S, HEAD), F32),
                   jax.ShapeDtypeStruct((ATT_KV, S, HEAD), F32), jax.ShapeDtypeStruct((ATT_KV, 8, 128), F32)],
        compiler_params=_cp(2),
    )(sinks, qh, kh, kh, vh, vh, doh)


def _heads_major(t, nh):
    S = t.shape[0]
    return t.reshape(S, nh, HEAD).transpose(1, 0, 2)


def _tokens_major(t):
    nh, S, _ = t.shape
    return t.transpose(1, 0, 2).reshape(S, nh * HEAD)


def _local_step(x, posf, target, w):
    S, D = x.shape
    gr = {}

    (h1,) = _rms_fwd(x, [w["a_norm"]], name="a_norm_f")
    zx = _mm(h1, w["w_zx"], name="in_proj_zx")
    dtr = _mm(h1, w["w_dt"], name="in_proj_dt")
    xbc = _conv_silu_fwd(zx, 2048, 4096, w["a_conv_w"], w["a_conv_b"], name="a_conv_f")
    dt, acum, sgd = _ssd_prep(dtr, w["a_dt_bias"], w["a_A_log"])
    dt_g, ac_g, sg_g = _to_groups(dt), _to_groups(acum), _to_groups(sgd)
    ac_t = acum[:, :SSM_HEADS].T
    y_ssd, states = _ssd_fwd(xbc, dt_g, ac_g, ac_t)
    yg = _gate_fwd(y_ssd, xbc, zx, w["a_Dexp"], w["a_gnorm"])
    x1 = _mm(yg, w["a_out_proj"], res=x, name="out_proj")

    def ffn_fwd(xin, l):
        (h,) = _rms_fwd(xin, [w["f_norm"][l]], name=f"f_norm_f{l}")
        u = _mm(h, w["f_w_in"][l], name=f"f_in{l}")
        a = _ffn_act_fwd(u, w["f_conv_w"][l], w["f_conv_b"][l], name=f"f_act_f{l}")
        xo = _mm(a, w["f_w_down"][l], res=xin, name=f"f_down{l}")
        return xo, (h, u)

    x2, ffn0 = ffn_fwd(x1, 0)

    hk, hq = _rms_fwd(x2, [w["kv_norm"], w["b_norm"]], name="kvq_norm_f")
    kv = _mm(hk, w["w_kv"], bias=w["b_kv"], name="kv_proj")
    q = _mm(hq, w["w_q"], bias=w["b_q"], name="q_proj")
    kr = _hnrope_fwd(kv, 0, 256, w["k_norm_w"], posf, name="k_rope_f")
    qr = _hnrope_fwd(q, 0, 1024, w["q_norm_w"], posf, name="q_rope_f")
    qh = _heads_major(qr, 16).reshape(ATT_KV, ATT_G, S, HEAD)
    kh = _heads_major(kr, ATT_KV)
    vh = _heads_major(kv[:, 256:].astype(BF16), ATT_KV)
    att_h = _attn_fwd(qh, kh, vh, w["sinks"])
    att = _tokens_major(att_h.reshape(16, S, HEAD))
    x3 = _mm(att, w["w_o"], bias=w["b_o"], res=x2, name="o_proj")
    x4, ffn1 = ffn_fwd(x3, 1)

    dy, loss_part = _loss(x4, target)

    def ffn_bwd(xin, l, saved, dyo, want_colsum):
        h, u = saved
        da = _mm(dyo, w["f_w_down"][l], tb=True, name=f"f_down_dx{l}")
        a = _ffn_act_fwd(u, w["f_conv_w"][l], w["f_conv_b"][l], name=f"f_act_r{l}")
        dgate, dval, dcw, dcb = _ffn_act_bwd(u, w["f_conv_w"][l], w["f_conv_b"][l], da, name=f"f_act_b{l}")
        dw_down = _mm(a, dyo, ta=True, name=f"f_down_dw{l}")
        dw_gate = _mm(h, dgate, ta=True, name=f"f_in_dwg{l}")
        dw_val = _mm(h, dval, ta=True, name=f"f_in_dwv{l}")
        Fd = dgate.shape[1]
        dh = _mm(dgate, w["f_w_in"][l], tb=True, name=f"f_in_dxg{l}")
        dh = _mm(dval, w["f_w_in"][l], tb=True, b_koff=Fd, res=dh, name=f"f_in_dxv{l}")
        outs = _rms_bwd(xin, [w["f_norm"][l]], [dh], dyo, name=f"f_norm_b{l}", want_colsum=want_colsum)
        g = dict(f_norm=outs[1], f_w_gate=dw_gate, f_w_val=dw_val, f_conv_w=dcw, f_conv_b=dcb, f_w_down=dw_down)
        return outs[0], g, (outs[2] if want_colsum else None)

    dx3, gr["ffn1"], db_o = ffn_bwd(x3, 1, ffn1, dy, True)
    gr["b_o"] = db_o
    gr["w_o"] = _mm(att, dx3, ta=True, name="o_proj_dw")
    datt = _mm(dx3, w["w_o"], tb=True, out_dtype=BF16, name="o_proj_dx")
    doh = _heads_major(datt, 16).reshape(ATT_KV, ATT_G, S, HEAD)
    dqh, dkh, dvh, dsk = _attn_bwd(qh, kh, vh, w["sinks"], doh)
    gr["sinks"] = dsk[:, 0, :4].reshape(1, 16)
    dqr = _tokens_major(dqh.reshape(16, S, HEAD))
    dkr = _tokens_major(dkh)
    dv = _tokens_major(dvh).astype(BF16)
    dq, db_q, dqn = _hnrope_bwd(q, 0, 1024, w["q_norm_w"], posf, dqr, name="q_rope_b")
    dk, db_k, dkn = _hnrope_bwd(kv, 0, 256, w["k_norm_w"], posf, dkr, name="k_rope_b")
    gr["q_norm"], gr["k_norm"] = dqn[:, :HEAD], dkn[:, :HEAD]
    gr["b_q"] = db_q
    gr["b_kv"] = jnp.concatenate([db_k, _colsum(dv, name="dv_colsum")], axis=1)
    gr["w_q"] = _mm(hq, dq, ta=True, name="q_proj_dw")
    gr["w_kv"] = jnp.concatenate([_mm(hk, dk, ta=True, name="k_proj_dw"), _mm(hk, dv, ta=True, name="v_proj_dw")], axis=1)
    dhq = _mm(dq, w["w_q"], tb=True, name="q_proj_dx")
    dhk = _mm(dk, w["w_kv"], tb=True, name="k_proj_dx")
    dhk = _mm(dv, w["w_kv"], tb=True, b_koff=256, res=dhk, name="v_proj_dx")
    dx2, gr["kv_norm"], gr["b_norm"] = _rms_bwd(x2, [w["kv_norm"], w["b_norm"]], [dhk, dhq], dx3, name="kvq_norm_b")

    dx1, gr["ffn0"], _ = ffn_bwd(x1, 0, ffn0, dx2, False)

    gr["a_out_proj"] = _mm(yg, dx1, ta=True, name="out_proj_dw")
    dyg = _mm(dx1, w["a_out_proj"], tb=True, name="out_proj_dx")
    dy_ssd, dz, gr["a_gnorm"], dD = _gate_bwd(y_ssd, xbc, zx, w["a_Dexp"], w["a_gnorm"], dyg)
    gr["a_D"] = dD[:, :SSM_HEADS]
    dxs, dB, dC, dhead = _ssd_bwd(xbc, dt_g, ac_g, ac_t, states, dy_ssd, w["a_Dexp"])
    ddtr_g, dsmall = _ssd_post(dhead, dt_g, sg_g, w["a_A_log_g"])
    gr["a_A_log"] = dsmall[:, 0, :4].reshape(1, SSM_HEADS)
    gr["a_dt_bias"] = dsmall[:, 1, :4].reshape(1, SSM_HEADS)
    ddtr = _from_groups(ddtr_g).astype(BF16)
    dxbc, gr["a_conv_w"], gr["a_conv_b"] = _conv_silu_bwd(
        zx, 2048, 4096, w["a_conv_w"], w["a_conv_b"], [(dxs, 0), (dB, 2048), (dC, 3072)], name="a_conv_b")
    gr["w_z"] = _mm(h1, dz, ta=True, name="in_proj_dwz")
    gr["w_x"] = _mm(h1, dxbc, ta=True, name="in_proj_dwx")
    gr["w_dt"] = _mm(h1, ddtr, ta=True, name="in_proj_dwdt")
    dh1 = _mm(dz, w["w_zx"], tb=True, name="in_proj_dxz")
    dh1 = _mm(dxbc, w["w_zx"], tb=True, b_koff=2048, res=dh1, name="in_proj_dxx")
    dh1 = _mm(ddtr, w["w_dt"], tb=True, res=dh1, name="in_proj_dxdt")
    dx0, gr["a_norm"] = _rms_bwd(x, [w["a_norm"]], [dh1], dx1, name="a_norm_b")
    return loss_part, dx0, gr


def _prep_weights(full):
    w = {}
    ip = full["a_in_proj"][0]
    w["w_zx"] = ip[:, :6144].astype(BF16)
    w["w_dt"] = jnp.pad(ip[:, 6144:], ((0, 0), (0, 128 - SSM_HEADS))).astype(BF16)
    w["a_norm"] = full["a_norm"]
    w["a_conv_w"] = full["a_conv_w"][0]
    w["a_conv_b"] = full["a_conv_b"]
    pad32 = lambda v: jnp.pad(v, ((0, 0), (0, 128 - SSM_HEADS)))
    w["a_dt_bias"] = pad32(full["a_dt_bias"])
    w["a_A_log"] = pad32(full["a_A_log"])
    w["a_A_log_g"] = jnp.pad(full["a_A_log"].reshape(SSM_GROUPS, 1, 4), ((0, 0), (0, 0), (0, 124)))
    w["a_Dexp"] = jnp.repeat(full["a_D"], HEAD, axis=1)
    w["a_gnorm"] = full["a_gnorm"]
    w["a_out_proj"] = full["a_out_proj"][0].astype(BF16)
    w["f_norm"] = [full["f_norm"][l:l + 1] for l in range(2)]
    w["f_w_in"] = [full["f_w_in"][l].astype(BF16) for l in range(2)]
    w["f_conv_w"] = [full["f_conv_w"][l] for l in range(2)]
    w["f_conv_b"] = [full["f_conv_b"][l:l + 1] for l in range(2)]
    w["f_w_down"] = [full["f_w_down"][l].astype(BF16) for l in range(2)]
    w["kv_norm"] = full["kv_norm"].reshape(1, -1)
    w["w_kv"] = full["w_kv"].astype(BF16)
    w["b_kv"] = full["b_kv"].reshape(1, -1)
    w["k_norm_w"] = jnp.tile(full["k_norm"].reshape(1, HEAD), (1, ATT_KV))
    w["b_norm"] = full["b_norm"]
    w["w_q"] = full["w_q"][0].astype(BF16)
    w["b_q"] = full["b_q"]
    w["q_norm_w"] = jnp.tile(full["q_norm"], (1, ATT_KV * ATT_G))
    w["sinks"] = full["sinks"].reshape(-1)
    w["w_o"] = full["w_o"][0].astype(BF16)
    w["b_o"] = full["b_o"]
    return w


def _full_grads(gr):
    g = {}
    g["a_norm"] = gr["a_norm"]
    g["a_in_proj"] = jnp.concatenate([gr["w_z"], gr["w_x"], gr["w_dt"][:, :SSM_HEADS]], axis=1)[None]
    g["a_conv_w"] = gr["a_conv_w"][None]
    g["a_conv_b"] = gr["a_conv_b"]
    g["a_dt_bias"], g["a_A_log"], g["a_D"] = gr["a_dt_bias"], gr["a_A_log"], gr["a_D"]
    g["a_gnorm"] = gr["a_gnorm"]
    g["a_out_proj"] = gr["a_out_proj"][None]
    g["kv_norm"] = gr["kv_norm"].reshape(-1)
    g["w_kv"] = gr["w_kv"]
    g["b_kv"] = gr["b_kv"].reshape(-1)
    g["k_norm"] = gr["k_norm"].reshape(-1)
    g["b_norm"] = gr["b_norm"]
    g["w_q"] = gr["w_q"][None]
    g["b_q"] = gr["b_q"]
    g["q_norm"] = gr["q_norm"]
    g["sinks"] = gr["sinks"]
    g["w_o"] = gr["w_o"][None]
    g["b_o"] = gr["b_o"]
    f = [gr["ffn0"], gr["ffn1"]]
    g["f_norm"] = jnp.concatenate([f[0]["f_norm"], f[1]["f_norm"]], axis=0)
    g["f_w_in"] = jnp.stack([jnp.concatenate([f[l]["f_w_gate"], f[l]["f_w_val"]], axis=1) for l in range(2)])
    g["f_conv_w"] = jnp.stack([f[l]["f_conv_w"] for l in range(2)])
    g["f_conv_b"] = jnp.concatenate([f[l]["f_conv_b"] for l in range(2)], axis=0)
    g["f_w_down"] = jnp.stack([f[l]["f_w_down"] for l in range(2)])
    return g


MESH = pl.DeviceIdType.MESH
N_CHIPS = 4
WEIGHTS = ("a_norm", "a_in_proj", "a_conv_w", "a_conv_b", "a_dt_bias", "a_A_log", "a_D", "a_gnorm", "a_out_proj",
           "kv_norm", "w_kv", "b_kv", "k_norm", "b_norm", "w_q", "b_q", "q_norm", "sinks", "w_o", "b_o", "f_norm",
           "f_w_in", "f_conv_w", "f_conv_b", "f_w_down")
BIG = (("a_in_proj", 2), ("a_out_proj", 1), ("w_kv", 0), ("w_q", 1), ("w_o", 1), ("f_w_in", 2), ("f_w_down", 1))
SMALL_CUT = (("a_norm", 1), ("a_conv_w", 2), ("a_conv_b", 1), ("a_gnorm", 1), ("f_conv_w", 2))
SMALL_REP = ("a_dt_bias", "a_A_log", "a_D", "kv_norm", "b_kv", "k_norm", "b_norm", "b_q", "q_norm", "sinks", "b_o",
             "f_norm", "f_conv_b")
HALF_ROW_ALIGN = 16
PACK_LANES = 1024


def _coords():
    return lax.axis_index("x"), lax.axis_index("y"), lax.axis_index("c")


def _other_chips(x, y):
    return [(1 - x, y), (x, 1 - y), (1 - x, 1 - y)]


def _pack(arrs, rows_align, lanes, dtype, lead=()):
    flat = jnp.concatenate([a.reshape(-1).astype(dtype) for a in arrs])
    n_lead = math.prod(lead) if lead else 1
    per = n_lead * rows_align * lanes
    total = -(-flat.shape[0] // per) * per
    flat = jnp.pad(flat, (0, total - flat.shape[0]))
    return flat.reshape(lead + (total // (n_lead * lanes), lanes))


def _unpack(flat, shapes):
    out, off = [], 0
    for s in shapes:
        n = math.prod(s)
        out.append(flat[off:off + n].reshape(s))
        off += n
    return out


def _remote(src, dst, send, recv, k, dev):
    return pltpu.make_async_remote_copy(src_ref=src, dst_ref=dst, send_sem=send.at[k], recv_sem=recv.at[k],
                                        device_id=dev, device_id_type=MESH)


_ANY = pl.BlockSpec(memory_space=pl.ANY)


def _gather_weights(wp, sp):
    R = wp.shape[1]

    def body(wp_ref, sp_ref, out_ref, sout_ref, send, recv, loc):
        x, y, c = _coords()
        me = 2 * x + y
        chips = _other_chips(x, y)
        sib = (x, y, 1 - c)
        l1 = pltpu.make_async_copy(sp_ref, sout_ref.at[me], loc.at[0])
        l1.start()
        sends = [_remote(wp_ref, out_ref.at[me], send, recv, 9, sib)]
        for j, (cx, cy) in enumerate(chips):
            sends.append(_remote(wp_ref.at[c], out_ref.at[me, c], send, recv, j, (cx, cy, c)))
            sends.append(_remote(sp_ref, sout_ref.at[me], send, recv, 3 + j, (cx, cy, c)))
        for cp in sends:
            cp.start()
        for j, (cx, cy) in enumerate(chips):
            src = 2 * cx + cy
            _remote(wp_ref.at[c], out_ref.at[src, c], send, recv, j, (cx, cy, c)).wait_recv()
            fwd = _remote(out_ref.at[src, c], out_ref.at[src, c], send, recv, 6 + j, sib)
            fwd.start()
            sends.append(fwd)
        for j, (cx, cy) in enumerate(chips):
            src = 2 * cx + cy
            _remote(sp_ref, sout_ref.at[src], send, recv, 3 + j, (cx, cy, c)).wait_recv()
            _remote(out_ref.at[src, 1 - c], out_ref.at[src, 1 - c], send, recv, 6 + j, sib).wait_recv()
        _remote(wp_ref, out_ref.at[me], send, recv, 9, sib).wait_recv()
        for cp in sends:
            cp.wait_send()
        l1.wait()

    return pl.pallas_call(
        body, name="gather_weights", in_specs=[_ANY, _ANY], out_specs=[_ANY, _ANY],
        out_shape=[jax.ShapeDtypeStruct((N_CHIPS, 2, R, PACK_LANES), wp.dtype),
                   jax.ShapeDtypeStruct((N_CHIPS,) + sp.shape, sp.dtype)],
        scratch_shapes=[pltpu.SemaphoreType.DMA((10,)), pltpu.SemaphoreType.DMA((10,)), pltpu.SemaphoreType.DMA((1,))],
    )(wp, sp)


def _allreduce_small(v):
    SR = v.shape[0]

    def body(v_ref, o_ref, buf, send, recv):
        x, y, c = _coords()
        me = 4 * x + 2 * y + c
        buf[me] = v_ref[...]
        peers = []
        for k in range(1, 8):
            px = 1 - x if k & 4 else x
            py = 1 - y if k & 2 else y
            pc = 1 - c if k & 1 else c
            peers.append((px, py, pc))
        cps = [_remote(v_ref, buf.at[me], send, recv, k, p) for k, p in enumerate(peers)]
        for cp in cps:
            cp.start()
        for k, (px, py, pc) in enumerate(peers):
            _remote(v_ref, buf.at[4 * px + 2 * py + pc], send, recv, k, (px, py, pc)).wait_recv()
        for cp in cps:
            cp.wait_send()
        acc = buf[0]
        for s in range(1, 8):
            acc = acc + buf[s]
        o_ref[...] = acc

    vm = pl.BlockSpec(memory_space=pltpu.VMEM)
    return pl.pallas_call(
        body, name="allreduce_small", in_specs=[vm], out_specs=vm, out_shape=jax.ShapeDtypeStruct(v.shape, F32),
        scratch_shapes=[pltpu.VMEM((8, SR, 128), F32), pltpu.SemaphoreType.DMA((7,)), pltpu.SemaphoreType.DMA((7,))],
    )(v)


def _rs_to_sibling(g):
    R = g.shape[2]

    def body(g_ref, a_ref, send, recv):
        x, y, c = _coords()
        cps = [_remote(g_ref.at[j, 1 - c], a_ref.at[j], send, recv, j, (x, y, 1 - c)) for j in range(N_CHIPS)]
        for cp in cps:
            cp.start()
        for cp in cps:
            cp.wait()

    return pl.pallas_call(
        body, name="rs_to_sibling", in_specs=[_ANY], out_specs=_ANY,
        out_shape=jax.ShapeDtypeStruct((N_CHIPS, R, PACK_LANES), g.dtype),
        scratch_shapes=[pltpu.SemaphoreType.DMA((N_CHIPS,)), pltpu.SemaphoreType.DMA((N_CHIPS,))],
    )(g)


def _rs_row_tile(R):
    return _pick(R, (496, 256, 128, 112, 16))


def _rs_add_pair(g, a, c_idx):
    R = g.shape[2]
    tr = _rs_row_tile(R)

    def body(c_ref, g_ref, a_ref, p_ref):
        p_ref[...] = (g_ref[...].astype(F32) + a_ref[...].astype(F32)).astype(BF16)

    return pl.pallas_call(
        body, name="rs_add_pair",
        grid_spec=pltpu.PrefetchScalarGridSpec(
            num_scalar_prefetch=1, grid=(N_CHIPS, R // tr),
            in_specs=[pl.BlockSpec((None, None, tr, PACK_LANES), lambda j, i, c_ref: (j, c_ref[0], i, 0)),
                      pl.BlockSpec((None, tr, PACK_LANES), lambda j, i, c_ref: (j, i, 0))],
            out_specs=pl.BlockSpec((None, tr, PACK_LANES), lambda j, i, c_ref: (j, i, 0))),
        out_shape=jax.ShapeDtypeStruct((N_CHIPS, R, PACK_LANES), BF16), compiler_params=_cp(2),
    )(c_idx, g, a)


def _rs_to_chips(p):
    R = p.shape[1]

    def body(p_ref, r_ref, send, recv):
        x, y, c = _coords()
        cps = [_remote(p_ref.at[2 * cx + cy], r_ref.at[k], send, recv, k, (cx, cy, c))
               for k, (cx, cy) in enumerate(_other_chips(x, y))]
        for cp in cps:
            cp.start()
        for cp in cps:
            cp.wait()

    return pl.pallas_call(
        body, name="rs_to_chips", in_specs=[_ANY], out_specs=_ANY,
        out_shape=jax.ShapeDtypeStruct((3, R, PACK_LANES), p.dtype),
        scratch_shapes=[pltpu.SemaphoreType.DMA((3,)), pltpu.SemaphoreType.DMA((3,))],
    )(p)


def _rs_add_chips(p, r, idx):
    R = p.shape[1]
    tr = _rs_row_tile(R)

    def body(idx_ref, p_ref, r0_ref, r1_ref, r2_ref, o_ref):
        o_ref[...] = ((p_ref[...].astype(F32) + r0_ref[...].astype(F32)) + r1_ref[...].astype(F32)) + r2_ref[...].astype(F32)

    def rk(k):
        return pl.BlockSpec((None, tr, PACK_LANES), lambda i, idx_ref, k=k: (k, i, 0))

    return pl.pallas_call(
        body, name="rs_add_chips",
        grid_spec=pltpu.PrefetchScalarGridSpec(
            num_scalar_prefetch=1, grid=(R // tr,),
            in_specs=[pl.BlockSpec((None, tr, PACK_LANES), lambda i, idx_ref: (idx_ref[0], i, 0)), rk(0), rk(1), rk(2)],
            out_specs=pl.BlockSpec((None, tr, PACK_LANES), lambda i, idx_ref: (idx_ref[1], i, 0))),
        out_shape=jax.ShapeDtypeStruct((2, R, PACK_LANES), F32), compiler_params=_cp(1),
    )(idx, p, r, r, r)


def _rs_join_halves(h):
    def body(h_ref, o_ref, send, recv):
        x, y, c = _coords()
        cp = _remote(o_ref.at[c], o_ref.at[c], send, recv, 0, (x, y, 1 - c))
        cp.start()
        _remote(o_ref.at[1 - c], o_ref.at[1 - c], send, recv, 0, (x, y, 1 - c)).wait_recv()
        cp.wait_send()

    return pl.pallas_call(
        body, name="rs_join_halves", in_specs=[_ANY], out_specs=_ANY, input_output_aliases={0: 0},
        out_shape=jax.ShapeDtypeStruct(h.shape, F32),
        scratch_shapes=[pltpu.SemaphoreType.DMA((1,)), pltpu.SemaphoreType.DMA((1,))],
    )(h)


def _adamw(w, g, m, v, *, name):
    Rr, C = w.shape
    tr = _pick(Rr, (256, 128, 64, 8))
    bc1 = 1.0 - ADAM_B1 ** ADAM_STEP
    bc2 = 1.0 - ADAM_B2 ** ADAM_STEP

    def body(w_ref, g_ref, m_ref, v_ref, d_ref, mo_ref, vo_ref):
        gv = g_ref[...]
        mn = ADAM_B1 * m_ref[...] + (1.0 - ADAM_B1) * gv
        vn = ADAM_B2 * v_ref[...] + (1.0 - ADAM_B2) * (gv * gv)
        mo_ref[...] = mn
        vo_ref[...] = vn
        d_ref[...] = -ADAM_LR * ((mn / bc1) / (jnp.sqrt(vn / bc2) + ADAM_EPS) + ADAM_WD * w_ref[...])

    blk = pl.BlockSpec((tr, C), lambda i: (i, 0))
    return pl.pallas_call(
        body, name=name, grid=(Rr // tr,), in_specs=[blk] * 4, out_specs=[blk] * 3,
        out_shape=[jax.ShapeDtypeStruct((Rr, C), F32)] * 3, compiler_params=_cp(1),
    )(w, g, m, v)


def kernel(x, positions, a_norm, a_in_proj, a_conv_w, a_conv_b, a_dt_bias, a_A_log, a_D, a_gnorm, a_out_proj,
           kv_norm, w_kv, b_kv, k_norm, b_norm, w_q, b_q, q_norm, sinks, w_o, b_o, f_norm, f_w_in, f_conv_w,
           f_conv_b, f_w_down, loss_target, m_a_norm, m_a_in_proj, m_a_conv_w, m_a_conv_b, m_a_dt_bias, m_a_A_log,
           m_a_D, m_a_gnorm, m_a_out_proj, m_kv_norm, m_w_kv, m_b_kv, m_k_norm, m_b_norm, m_w_q, m_b_q, m_q_norm,
           m_sinks, m_w_o, m_b_o, m_f_norm, m_f_w_in, m_f_conv_w, m_f_conv_b, m_f_w_down, v_a_norm, v_a_in_proj,
           v_a_conv_w, v_a_conv_b, v_a_dt_bias, v_a_A_log, v_a_D, v_a_gnorm, v_a_out_proj, v_kv_norm, v_w_kv,
           v_b_kv, v_k_norm, v_b_norm, v_w_q, v_b_q, v_q_norm, v_sinks, v_w_o, v_b_o, v_f_norm, v_f_w_in,
           v_f_conv_w, v_f_conv_b, v_f_w_down):
    wl = dict(zip(WEIGHTS, (a_norm, a_in_proj, a_conv_w, a_conv_b, a_dt_bias, a_A_log, a_D, a_gnorm, a_out_proj,
                            kv_norm, w_kv, b_kv, k_norm, b_norm, w_q, b_q, q_norm, sinks, w_o, b_o, f_norm, f_w_in,
                            f_conv_w, f_conv_b, f_w_down)))
    ml = dict(zip(WEIGHTS, (m_a_norm, m_a_in_proj, m_a_conv_w, m_a_conv_b, m_a_dt_bias, m_a_A_log, m_a_D, m_a_gnorm,
                            m_a_out_proj, m_kv_norm, m_w_kv, m_b_kv, m_k_norm, m_b_norm, m_w_q, m_b_q, m_q_norm,
                            m_sinks, m_w_o, m_b_o, m_f_norm, m_f_w_in, m_f_conv_w, m_f_conv_b, m_f_w_down)))
    vl = dict(zip(WEIGHTS, (v_a_norm, v_a_in_proj, v_a_conv_w, v_a_conv_b, v_a_dt_bias, v_a_A_log, v_a_D, v_a_gnorm,
                            v_a_out_proj, v_kv_norm, v_w_kv, v_b_kv, v_k_norm, v_b_norm, v_w_q, v_b_q, v_q_norm,
                            v_sinks, v_w_o, v_b_o, v_f_norm, v_f_w_in, v_f_conv_w, v_f_conv_b, v_f_w_down)))
    xi, yi, ci = _coords()
    me = 2 * xi + yi
    S = x.shape[1]

    wp = _pack([wl[n] for n, _ in BIG], HALF_ROW_ALIGN, PACK_LANES, BF16, lead=(2,))
    sp = _pack([wl[n] for n, _ in SMALL_CUT], 8, 128, F32)
    gw, gs = _gather_weights(wp, sp)
    gw = gw.reshape(N_CHIPS, -1)
    gs = gs.reshape(N_CHIPS, -1)
    full = {n: wl[n] for n in SMALL_REP}
    for names, src in ((BIG, gw), (SMALL_CUT, gs)):
        pieces = [_unpack(src[j], [wl[n].shape for n, _ in names]) for j in range(N_CHIPS)]
        for q, (n, ax) in enumerate(names):
            full[n] = jnp.concatenate([pieces[j][q] for j in range(N_CHIPS)], axis=ax)

    posf = positions.reshape(S, 1).astype(F32)
    loss_part, dx0, gr = _local_step(x[0], posf, loss_target[0], _prep_weights(full))
    g = _full_grads(gr)

    small_names = [n for n, _ in SMALL_CUT] + list(SMALL_REP)
    sv = _pack([g[n] for n in small_names] + [loss_part[0:1, 0:1]], 8, 128, F32)
    sred = _allreduce_small(sv).reshape(-1)
    small_shapes = [g[n].shape for n in small_names] + [(1,)]
    sg = dict(zip(small_names + ["loss"], _unpack(sred, small_shapes)))
    loss = sg["loss"].reshape(())
    g_small = {}
    for n, ax in SMALL_CUT:
        size = wl[n].shape[ax]
        g_small[n] = lax.dynamic_slice_in_dim(sg[n], me * size, size, axis=ax)
    for n in SMALL_REP:
        g_small[n] = sg[n].reshape(wl[n].shape)

    gp = jnp.stack([_pack([jnp.split(g[n], N_CHIPS, axis=ax)[j] for n, ax in BIG], HALF_ROW_ALIGN, PACK_LANES, BF16,
                          lead=(2,)) for j in range(N_CHIPS)])
    c_idx = jnp.reshape(ci, (1,)).astype(jnp.int32)
    me_c = jnp.stack([me, ci]).astype(jnp.int32)
    pair = _rs_add_pair(gp, _rs_to_sibling(gp), c_idx)
    half = _rs_add_chips(pair, _rs_to_chips(pair), me_c)
    red = _rs_join_halves(half).reshape(-1)
    g_big = dict(zip([n for n, _ in BIG], _unpack(red, [wl[n].shape for n, _ in BIG])))

    grads, delta, new_m, new_v = {}, {}, {}, {}
    for n, _ in BIG:
        shp = wl[n].shape
        two = (-1, shp[-1])
        d, mn, vn = _adamw(wl[n].reshape(two), g_big[n].reshape(two), ml[n].reshape(two), vl[n].reshape(two),
                           name="adamw_" + n)
        grads[n], delta[n], new_m[n], new_v[n] = g_big[n], d.reshape(shp), mn.reshape(shp), vn.reshape(shp)
    pk = lambda d: _pack([d[n] for n in small_names], 8, 128, F32)
    d, mn, vn = _adamw(pk(wl), pk(g_small), pk(ml), pk(vl), name="adamw_small")
    shapes = [wl[n].shape for n in small_names]
    for n, dd, mm, vv in zip(small_names, _unpack(d.reshape(-1), shapes), _unpack(mn.reshape(-1), shapes),
                             _unpack(vn.reshape(-1), shapes)):
        grads[n], delta[n], new_m[n], new_v[n] = g_small[n], dd, mm, vv

    return (loss, dx0[None], *[grads[n] for n in WEIGHTS], *[delta[n] for n in WEIGHTS],
            *[new_m[n] for n in WEIGHTS], *[new_v[n] for n in WEIGHTS])
```

```python
import math

import jax
import jax.numpy as jnp
from jax import lax
from jax.experimental import pallas as pl
from jax.experimental.pallas import tpu as pltpu

F32 = jnp.float32
BF16 = jnp.bfloat16

EPS = 1e-5
CHUNK = 256
WINDOW = 128
HEAD = 64
SSM_HEADS = 32
SSM_GROUPS = 8
SSM_STATE = 128
ATT_KV = 4
ATT_G = 4
ROPE_THETA = 10000.0
NEG = -1e30
N_CHIPS = 4
VMEM_LIMIT = 56 * 1024 * 1024

ADAM_LR, ADAM_B1, ADAM_B2, ADAM_EPS, ADAM_WD, ADAM_STEP = 0.001, 0.9, 0.999, 1e-08, 0.01, 10


def _cp(n_axes):
    return pltpu.CompilerParams(dimension_semantics=("arbitrary",) * n_axes, vmem_limit_bytes=VMEM_LIMIT)


def _pick(dim, prefs):
    for p in prefs:
        if dim % p == 0:
            return p
    return dim


def _iota(shape, dim):
    return lax.broadcasted_iota(jnp.int32, shape, dim)


def _dot(a, b, ca=1, cb=0):
    return lax.dot_general(a, b, (((ca,), (cb,)), ((), ())), preferred_element_type=F32)


def _dot3(x, ind):
    h = x.astype(BF16)
    r = x - h.astype(F32)
    m = r.astype(BF16)
    lo = (r - m.astype(F32)).astype(BF16)
    return _dot(h, ind) + _dot(m, ind) + _dot(lo, ind)


def _sigmoid(x):
    return jax.nn.sigmoid(x)


def _mm(a, b, *, name, ta=False, tb=False, bias=None, res=None, out_dtype=F32, b_koff=0, tm=None, tn=None, tk=None,
        dims=None, a_spec=None, b_spec=None, o_spec=None, o_shape=None):
    if dims is not None:
        M, N, K = dims
    else:
        if ta:
            K, M = a.shape
        else:
            M, K = a.shape
        N = b.shape[0] if tb else b.shape[1]
    tm = tm or _pick(M, (1024, 512, 256, 128))
    tn = tn or _pick(N, (512, 256, 128))
    tk = tk or (K if K <= 2048 else _pick(K, (2048, 1408, 1024, 512)))
    assert M % tm == 0 and N % tn == 0 and K % tk == 0 and b_koff % tk == 0
    nk = K // tk
    kb0 = b_koff // tk
    has_bias, has_res = bias is not None, res is not None

    def body(*refs):
        a_ref, b_ref = refs[0], refs[1]
        pos = 2
        bias_ref = res_ref = acc_ref = None
        if has_bias:
            bias_ref = refs[pos]
            pos += 1
        if has_res:
            res_ref = refs[pos]
            pos += 1
        o_ref = refs[pos]
        if nk > 1:
            acc_ref = refs[pos + 1]
        part = _dot(a_ref[...].astype(BF16), b_ref[...].astype(BF16), 0 if ta else 1, 1 if tb else 0)

        def finish(acc):
            if has_bias:
                acc = acc + bias_ref[...]
            if has_res:
                acc = acc + res_ref[...]
            o_ref[...] = acc.astype(out_dtype)

        if nk == 1:
            finish(part)
        else:
            k = pl.program_id(2)

            @pl.when(k == 0)
            def _():
                acc_ref[...] = part

            @pl.when(k > 0)
            def _():
                acc_ref[...] += part

            @pl.when(k == nk - 1)
            def _():
                finish(acc_ref[...])

    if a_spec is None:
        a_spec = pl.BlockSpec((tk, tm), lambda i, j, k: (k, i)) if ta else pl.BlockSpec((tm, tk), lambda i, j, k: (i, k))
    if b_spec is None:
        b_spec = (pl.BlockSpec((tn, tk), lambda i, j, k: (j, k + kb0)) if tb
                  else pl.BlockSpec((tk, tn), lambda i, j, k: (k + kb0, j)))
    if o_spec is None:
        o_spec = pl.BlockSpec((tm, tn), lambda i, j, k: (i, j))
    in_specs, args = [a_spec, b_spec], [a, b]
    if has_bias:
        in_specs.append(pl.BlockSpec((1, tn), lambda i, j, k: (0, j)))
        args.append(bias)
    if has_res:
        in_specs.append(pl.BlockSpec((tm, tn), lambda i, j, k: (i, j)))
        args.append(res)
    return pl.pallas_call(
        body, name=name, grid=(M // tm, N // tn, nk), in_specs=in_specs, out_specs=o_spec,
        out_shape=jax.ShapeDtypeStruct(o_shape or (M, N), out_dtype),
        scratch_shapes=[pltpu.VMEM((tm, tn), F32)] if nk > 1 else [],
        compiler_params=_cp(3),
    )(*args)


def _rms_fwd(x, gains, *, name, tr=256):
    S, D = x.shape
    n = len(gains)

    def body(*refs):
        xv = refs[0][...]
        xh = xv * lax.rsqrt(jnp.mean(xv * xv, axis=-1, keepdims=True) + EPS)
        for q in range(n):
            refs[1 + n + q][...] = (xh * refs[1 + q][...]).astype(BF16)

    row = pl.BlockSpec((tr, D), lambda i: (i, 0))
    vec = pl.BlockSpec((1, D), lambda i: (0, 0))
    return pl.pallas_call(
        body, name=name, grid=(S // tr,), in_specs=[row] + [vec] * n, out_specs=[row] * n,
        out_shape=[jax.ShapeDtypeStruct((S, D), BF16)] * n, compiler_params=_cp(1),
    )(x, *gains)


def _rms_bwd(x, gains, dhs, dres, *, name, tr=256, want_colsum=False):
    S, D = x.shape
    n = len(gains)
    steps = S // tr

    def body(*refs):
        x_ref = refs[0]
        g_refs = refs[1:1 + n]
        dh_refs = refs[1 + n:1 + 2 * n]
        dres_ref = refs[1 + 2 * n]
        dx_ref = refs[2 + 2 * n]
        dg_refs = refs[3 + 2 * n:3 + 3 * n]
        cs_ref = refs[3 + 3 * n] if want_colsum else None
        i = pl.program_id(0)
        xv = x_ref[...]
        r = lax.rsqrt(jnp.mean(xv * xv, axis=-1, keepdims=True) + EPS)
        xh = xv * r
        dx = dres_ref[...]
        for q in range(n):
            dh = dh_refs[q][...]
            dxh = dh * g_refs[q][...]
            dx = dx + r * (dxh - xh * jnp.mean(dxh * xh, axis=-1, keepdims=True))
            part = jnp.sum(dh * xh, axis=0, keepdims=True)

            @pl.when(i == 0)
            def _():
                dg_refs[q][...] = part

            @pl.when(i > 0)
            def _():
                dg_refs[q][...] += part

        dx_ref[...] = dx
        if want_colsum:
            cpart = jnp.sum(dx, axis=0, keepdims=True)

            @pl.when(i == 0)
            def _():
                cs_ref[...] = cpart

            @pl.when(i > 0)
            def _():
                cs_ref[...] += cpart

    row = pl.BlockSpec((tr, D), lambda i: (i, 0))
    vec = pl.BlockSpec((1, D), lambda i: (0, 0))
    n_vec_out = n + (1 if want_colsum else 0)
    outs = pl.pallas_call(
        body, name=name, grid=(steps,), in_specs=[row] + [vec] * n + [row] * n + [row],
        out_specs=[row] + [vec] * n_vec_out,
        out_shape=[jax.ShapeDtypeStruct((S, D), F32)] + [jax.ShapeDtypeStruct((1, D), F32)] * n_vec_out,
        compiler_params=_cp(1),
    )(x, *gains, *dhs, dres)
    return outs


def _colsum(x, *, name, tr=256):
    S, D = x.shape

    def body(x_ref, o_ref):
        i = pl.program_id(0)
        part = jnp.sum(x_ref[...].astype(F32), axis=0, keepdims=True)

        @pl.when(i == 0)
        def _():
            o_ref[...] = part

        @pl.when(i > 0)
        def _():
            o_ref[...] += part

    return pl.pallas_call(
        body, name=name, grid=(S // tr,), in_specs=[pl.BlockSpec((tr, D), lambda i: (i, 0))],
        out_specs=pl.BlockSpec((1, D), lambda i: (0, 0)), out_shape=jax.ShapeDtypeStruct((1, D), F32),
        compiler_params=_cp(1),
    )(x)


def _loss(y, t, *, name="loss", tr=256):
    S, D = y.shape
    steps = S // tr

    def body(y_ref, t_ref, dy_ref, l_ref, acc_ref):
        i = pl.program_id(0)
        e = y_ref[...] - t_ref[...]
        dy_ref[...] = e * (1.0 / D)
        part = jnp.sum(e * e, axis=0, keepdims=True)

        @pl.when(i == 0)
        def _():
            acc_ref[...] = part

        @pl.when(i > 0)
        def _():
            acc_ref[...] += part

        @pl.when(i == steps - 1)
        def _():
            tot = jnp.sum(acc_ref[...], axis=1, keepdims=True) * (0.5 / D)
            l_ref[...] = jnp.broadcast_to(tot, (8, 128))

    row = pl.BlockSpec((tr, D), lambda i: (i, 0))
    return pl.pallas_call(
        body, name=name, grid=(steps,), in_specs=[row, row],
        out_specs=[row, pl.BlockSpec((8, 128), lambda i: (0, 0))],
        out_shape=[jax.ShapeDtypeStruct((S, D), F32), jax.ShapeDtypeStruct((8, 128), F32)],
        scratch_shapes=[pltpu.VMEM((1, D), F32)], compiler_params=_cp(1),
    )(y, t)


def _conv_pre(x, w_ref, b_ref, width):
    row = _iota(x.shape, 0)
    acc = b_ref[...] + w_ref[pl.ds(width - 1, 1), :] * x
    shifted = []
    for s in range(1, width):
        xs = jnp.where(row >= s, pltpu.roll(x, s, axis=0), 0.0)
        shifted.append(xs)
        acc = acc + w_ref[pl.ds(width - 1 - s, 1), :] * xs
    return acc, shifted


def _conv_back(dacc, x, shifted, w_ref, width):
    S = x.shape[0]
    row = _iota(x.shape, 0)
    dx = w_ref[pl.ds(width - 1, 1), :] * dacc
    dws = [None] * width
    dws[width - 1] = jnp.sum(dacc * x, axis=0, keepdims=True)
    for s in range(1, width):
        back = jnp.where(row < S - s, pltpu.roll(dacc, S - s, axis=0), 0.0)
        dx = dx + w_ref[pl.ds(width - 1 - s, 1), :] * back
        dws[width - 1 - s] = jnp.sum(dacc * shifted[s - 1], axis=0, keepdims=True)
    db = jnp.sum(dacc, axis=0, keepdims=True)
    return dx, dws, db


def _conv_silu_fwd(xin, col0, C, w, b, *, name, tc=512):
    S = xin.shape[0]
    width = w.shape[0]
    off = col0 // tc

    def body(x_ref, w_ref, b_ref, o_ref):
        acc, _ = _conv_pre(x_ref[...], w_ref, b_ref, width)
        o_ref[...] = acc * _sigmoid(acc)

    return pl.pallas_call(
        body, name=name, grid=(C // tc,),
        in_specs=[pl.BlockSpec((S, tc), lambda j: (0, j + off)), pl.BlockSpec((width, tc), lambda j: (0, j)),
                  pl.BlockSpec((1, tc), lambda j: (0, j))],
        out_specs=pl.BlockSpec((S, tc), lambda j: (0, j)), out_shape=jax.ShapeDtypeStruct((S, C), F32),
        compiler_params=_cp(1),
    )(xin, w, b)


def _conv_silu_bwd(xin, col0, C, w, b, douts, *, name, tc=256):
    S = xin.shape[0]
    width = w.shape[0]
    off = col0 // tc
    nd = len(douts)
    ranges = [(o // tc, (o + d.shape[1]) // tc) for d, o in douts]

    def body(*refs):
        x_ref, w_ref, b_ref = refs[0], refs[1], refs[2]
        d_refs = refs[3:3 + nd]
        dx_ref, dw_ref, db_ref = refs[3 + nd], refs[4 + nd], refs[5 + nd]
        j = pl.program_id(0)
        x = x_ref[...]
        acc, shifted = _conv_pre(x, w_ref, b_ref, width)
        sg = _sigmoid(acc)
        dout = jnp.zeros_like(x)
        for q in range(nd):
            lo, hi = ranges[q]
            dout = dout + jnp.where((j >= lo) & (j < hi), d_refs[q][...], 0.0)
        dacc = dout * (sg * (1.0 + acc * (1.0 - sg)))
        dx, dws, db = _conv_back(dacc, x, shifted, w_ref, width)
        dx_ref[...] = dx.astype(BF16)
        for k in range(width):
            dw_ref[pl.ds(k, 1), :] = dws[k]
        db_ref[...] = db

    d_specs = [pl.BlockSpec((S, tc), (lambda j, lo=lo, hi=hi: (0, jnp.clip(j - lo, 0, hi - lo - 1)))) for lo, hi in ranges]
    return pl.pallas_call(
        body, name=name, grid=(C // tc,),
        in_specs=[pl.BlockSpec((S, tc), lambda j: (0, j + off)), pl.BlockSpec((width, tc), lambda j: (0, j)),
                  pl.BlockSpec((1, tc), lambda j: (0, j))] + d_specs,
        out_specs=[pl.BlockSpec((S, tc), lambda j: (0, j)), pl.BlockSpec((width, tc), lambda j: (0, j)),
                   pl.BlockSpec((1, tc), lambda j: (0, j))],
        out_shape=[jax.ShapeDtypeStruct((S, C), BF16), jax.ShapeDtypeStruct((width, C), F32),
                   jax.ShapeDtypeStruct((1, C), F32)],
        compiler_params=_cp(1),
    )(xin, w, b, *[d for d, _ in douts])


def _ffn_act_fwd(u, w, b, *, name, tc=256):
    S, F2 = u.shape
    Fd = F2 // 2
    width = w.shape[0]
    nb = Fd // tc

    def body(g_ref, v_ref, w_ref, b_ref, o_ref):
        acc, _ = _conv_pre(g_ref[...], w_ref, b_ref, width)
        o_ref[...] = (acc * _sigmoid(acc) * v_ref[...]).astype(BF16)

    return pl.pallas_call(
        body, name=name, grid=(nb,),
        in_specs=[pl.BlockSpec((S, tc), lambda j: (0, j)), pl.BlockSpec((S, tc), lambda j: (0, j + nb)),
                  pl.BlockSpec((width, tc), lambda j: (0, j)), pl.BlockSpec((1, tc), lambda j: (0, j))],
        out_specs=pl.BlockSpec((S, tc), lambda j: (0, j)), out_shape=jax.ShapeDtypeStruct((S, Fd), BF16),
        compiler_params=_cp(1),
    )(u, u, w, b)


def _ffn_act_bwd(u, w, b, da, *, name, tc=256):
    S, F2 = u.shape
    Fd = F2 // 2
    width = w.shape[0]
    nb = Fd // tc

    def body(g_ref, v_ref, w_ref, b_ref, da_ref, du_ref, dw_ref, db_ref):
        x = g_ref[...]
        acc, shifted = _conv_pre(x, w_ref, b_ref, width)
        sg = _sigmoid(acc)
        dav = da_ref[...]
        du_ref[1] = (dav * acc * sg).astype(BF16)
        dacc = dav * v_ref[...] * (sg * (1.0 + acc * (1.0 - sg)))
        dx, dws, db = _conv_back(dacc, x, shifted, w_ref, width)
        du_ref[0] = dx.astype(BF16)
        for k in range(width):
            dw_ref[pl.ds(k, 1), :] = dws[k]
        db_ref[...] = db

    blk = pl.BlockSpec((S, tc), lambda j: (0, j))
    return pl.pallas_call(
        body, name=name, grid=(nb,),
        in_specs=[blk, pl.BlockSpec((S, tc), lambda j: (0, j + nb)), pl.BlockSpec((width, tc), lambda j: (0, j)),
                  pl.BlockSpec((1, tc), lambda j: (0, j)), blk],
        out_specs=[pl.BlockSpec((2, S, tc), lambda j: (0, 0, j)), pl.BlockSpec((width, tc), lambda j: (0, j)),
                   pl.BlockSpec((1, tc), lambda j: (0, j))],
        out_shape=[jax.ShapeDtypeStruct((2, S, Fd), BF16),
                   jax.ShapeDtypeStruct((width, Fd), F32), jax.ShapeDtypeStruct((1, Fd), F32)],
        compiler_params=_cp(1),
    )(u, u, w, b, da)


def _ssd_prep(dtr, dt_bias, a_log, *, name="ssd_prep"):
    S = dtr.shape[0]

    def body(d_ref, b_ref, al_ref, dt_ref, ac_ref, sg_ref):
        lane = _iota((CHUNK, 128), 1)
        valid = lane < SSM_HEADS
        z = d_ref[...] + b_ref[...]
        dt = jnp.where(valid, jnp.maximum(z, 0.0) + jnp.log(1.0 + jnp.exp(-jnp.abs(z))), 0.0)
        a = dt * (-jnp.exp(al_ref[...]))
        row = _iota((CHUNK, 128), 0)
        k = 1
        while k < CHUNK:
            a = a + jnp.where(row >= k, pltpu.roll(a, k, axis=0), 0.0)
            k *= 2
        dt_ref[...] = dt
        ac_ref[...] = a
        sg_ref[...] = jnp.where(valid, _sigmoid(z), 0.0)

    blk = pl.BlockSpec((CHUNK, 128), lambda i: (i, 0))
    vec = pl.BlockSpec((1, 128), lambda i: (0, 0))
    return pl.pallas_call(
        body, name=name, grid=(S // CHUNK,), in_specs=[blk, vec, vec], out_specs=[blk, blk, blk],
        out_shape=[jax.ShapeDtypeStruct((S, 128), F32)] * 3, compiler_params=_cp(1),
    )(dtr, dt_bias, a_log)


def _to_groups(v):
    S = v.shape[0]
    g = v[:, :SSM_HEADS].reshape(S, SSM_GROUPS, 4).transpose(1, 0, 2)
    return jnp.pad(g, ((0, 0), (0, 0), (0, 124)))


def _from_groups(vg):
    S = vg.shape[1]
    v = vg[:, :, :4].transpose(1, 0, 2).reshape(S, SSM_HEADS)
    return jnp.pad(v, ((0, 0), (0, 128 - SSM_HEADS)))


def _expand4(v, lanes):
    out = jnp.broadcast_to(v[:, 3:4], lanes.shape)
    for hh in (2, 1, 0):
        out = jnp.where(lanes < 64 * (hh + 1), v[:, hh:hh + 1], out)
    return out


def _ssd_fwd(xbc, dt_g, ac_g, ac_t, *, name="ssd_fwd"):
    S = xbc.shape[0]
    nc = S // CHUNK
    Lc = CHUNK

    def body(x_ref, b_ref, c_ref, dt_ref, ac_ref, act_ref, y_ref, st_out_ref, st_ref):
        g = pl.program_id(0)
        c = pl.program_id(1)

        @pl.when(c == 0)
        def _():
            st_ref[...] = jnp.zeros_like(st_ref)

        bv = b_ref[...]
        cbf = c_ref[...].astype(BF16)
        cb = _dot(cbf, bv.astype(BF16), 1, 1)
        causal = _iota((Lc, Lc), 0) >= _iota((Lc, Lc), 1)
        lane256 = _iota((Lc, 256), 1)
        lane128 = _iota((Lc, 128), 1)
        row128 = _iota((128, 128), 0)
        dtg, acg = dt_ref[...], ac_ref[...]
        ac_last = ac_ref[pl.ds(Lc - 1, 1), :]
        dt4 = _expand4(dtg, lane256)
        ac4 = _expand4(acg, lane256)
        e4 = jnp.exp(ac4)
        xdb = (x_ref[...] * dt4).astype(BF16)
        st_out_ref[...] = st_ref[...]
        for p in range(2):
            xd_p = xdb[:, 128 * p:128 * (p + 1)]
            st_p = st_ref[p]
            ys, sn, cds = [], [], []
            for q in range(2):
                hh = 2 * p + q
                a_col = acg[:, hh:hh + 1]
                a_row = act_ref[pl.ds(4 * g + hh, 1), :]
                dec = jnp.exp(jnp.where(causal, a_col - a_row, NEG))
                w = (cb * dec).astype(BF16)
                ys.append(_dot(w, xd_p))
                al = ac_last[:, hh:hh + 1]
                dte = jnp.exp(al - a_col)
                sn.append(_dot(xd_p, (bv * dte).astype(BF16), 0, 0))
                cds.append(jnp.exp(al))
            y_diag = jnp.where(lane128 < 64, ys[0], ys[1])
            y_off = _dot(cbf, st_p.astype(BF16), 1, 1) * e4[:, 128 * p:128 * (p + 1)]
            y_ref[:, 128 * p:128 * (p + 1)] = y_diag + y_off
            st_ref[p] = jnp.where(row128 < 64, st_p * cds[0] + sn[0], st_p * cds[1] + sn[1])

    per_g = lambda g, c: (g, c, 0)
    return pl.pallas_call(
        body, name=name, grid=(SSM_GROUPS, nc),
        in_specs=[pl.BlockSpec((Lc, 256), lambda g, c: (c, g)),
                  pl.BlockSpec((Lc, 128), lambda g, c: (c, 16 + g)),
                  pl.BlockSpec((Lc, 128), lambda g, c: (c, 24 + g)),
                  pl.BlockSpec((None, Lc, 128), per_g), pl.BlockSpec((None, Lc, 128), per_g),
                  pl.BlockSpec((SSM_HEADS, Lc), lambda g, c: (0, c))],
        out_specs=[pl.BlockSpec((Lc, 256), lambda g, c: (c, g)),
                   pl.BlockSpec((None, None, 2, 128, 128), lambda g, c: (g, c, 0, 0, 0))],
        out_shape=[jax.ShapeDtypeStruct((S, 2048), F32), jax.ShapeDtypeStruct((SSM_GROUPS, nc, 2, 128, 128), F32)],
        scratch_shapes=[pltpu.VMEM((2, 128, 128), F32)], compiler_params=_cp(2),
    )(xbc, xbc, xbc, dt_g, ac_g, ac_t)


def _ssd_bwd(xbc, dt_g, ac_g, ac_t, states, dy, dexp, *, name="ssd_bwd"):
    S = xbc.shape[0]
    nc = S // CHUNK
    Lc = CHUNK

    def body(x_ref, b_ref, c_ref, dt_ref, ac_ref, act_ref, st_ref, dy_ref, d_ref, dx_ref, db_ref, dc_ref, dh_ref, ds_ref):
        g = pl.program_id(0)
        cc = pl.program_id(1)

        @pl.when(cc == 0)
        def _():
            ds_ref[...] = jnp.zeros_like(ds_ref)

        bv = b_ref[...]
        cv = c_ref[...]
        bbf, cbf = bv.astype(BF16), cv.astype(BF16)
        cb = _dot(cbf, bbf, 1, 1)
        causal = _iota((Lc, Lc), 0) >= _iota((Lc, Lc), 1)
        lane256 = _iota((Lc, 256), 1)
        lane128 = _iota((Lc, 128), 1)
        row128 = _iota((128, 128), 0)
        dtg, acg = dt_ref[...], ac_ref[...]
        ac_last = ac_ref[pl.ds(Lc - 1, 1), :]
        dt4 = _expand4(dtg, lane256)
        ac4 = _expand4(acg, lane256)
        acl4 = _expand4(ac_last, _iota((1, 256), 1))
        e4 = jnp.exp(ac4)
        dte4 = jnp.exp(acl4 - ac4)
        xv = x_ref[...]
        xd = xv * dt4
        xdb = xd.astype(BF16)
        dyv = dy_ref[...]
        dcb = jnp.zeros((Lc, Lc), F32)
        dc_acc = jnp.zeros((Lc, 128), F32)
        db_acc = jnp.zeros((Lc, 128), F32)
        ind_rows = _iota((256, 128), 0) >> 6
        ind_cols = _iota((256, 128), 1)
        ind_a = (ind_rows == ind_cols).astype(BF16)
        ind_b = (ind_rows + 4 == ind_cols).astype(BF16)
        u_parts, dxd_parts, ends = [], [], []
        for p in range(2):
            sl = slice(128 * p, 128 * (p + 1))
            xd_p, xdb_p, dy_p = xd[:, sl], xdb[:, sl], dyv[:, sl]
            dyb_p = dy_p.astype(BF16)
            e_p, dte_p = e4[:, sl], dte4[:, sl]
            sp = st_ref[p]
            spb = sp.astype(BF16)
            dsn = ds_ref[p]
            dsnb = dsn.astype(BF16)
            yds, dxds, cds = [], [], []
            for q in range(2):
                hh = 2 * p + q
                a_col = acg[:, hh:hh + 1]
                a_row = act_ref[pl.ds(4 * g + hh, 1), :]
                dec = jnp.exp(jnp.where(causal, a_col - a_row, NEG))
                w = (cb * dec).astype(BF16)
                head = (lane128 < 64) if q == 0 else (lane128 >= 64)
                dym = jnp.where(head, dyb_p, jnp.zeros_like(dyb_p))
                dw = _dot(dym, xdb_p, 1, 1)
                dcb = dcb + dw * dec
                yds.append(_dot(w, xdb_p))
                dxds.append(_dot(w, dyb_p, 0, 0))
                cds.append(jnp.exp(ac_last[:, hh:hh + 1]))
            y_diag = jnp.where(lane128 < 64, yds[0], yds[1])
            dxd_diag = jnp.where(lane128 < 64, dxds[0], dxds[1])
            y_off = _dot(cbf, spb, 1, 1) * e_p
            dgp = dy_p * e_p
            dgb = dgp.astype(BF16)
            dc_acc = dc_acc + _dot(dgb, spb)
            dsp = _dot(dgb, cbf, 0, 0)
            cd_col = jnp.where(row128[:, 0:1] < 64, cds[0], cds[1])
            qm = _dot(bbf, dsnb, 1, 1)
            dxd_state = dte_p * qm
            db_acc = db_acc + _dot((xd_p * dte_p).astype(BF16), dsnb)
            t_p = xd_p * dxd_state
            prod = dsn * sp
            e0 = jnp.sum(jnp.sum(jnp.where(row128 < 64, prod, 0.0), axis=1, keepdims=True), axis=0, keepdims=True)
            e1 = jnp.sum(jnp.sum(jnp.where(row128 >= 64, prod, 0.0), axis=1, keepdims=True), axis=0, keepdims=True)
            tcol = jnp.sum(t_p, axis=0, keepdims=True)
            lane1 = _iota((1, 128), 1)
            t0 = jnp.sum(jnp.where(lane1 < 64, tcol, 0.0), axis=1, keepdims=True)
            t1 = jnp.sum(jnp.where(lane1 >= 64, tcol, 0.0), axis=1, keepdims=True)
            ends.append(e0 * cds[0] + t0)
            ends.append(e1 * cds[1] + t1)
            ds_ref[p] = dsn * cd_col + dsp
            u_parts.append(dyb_p.astype(F32) * y_diag - xdb_p.astype(F32) * dxd_diag + dy_p * y_off - t_p)
            dxd_parts.append(dxd_diag + dxd_state)
        dxd = jnp.concatenate(dxd_parts, axis=1)
        u_all = jnp.concatenate(u_parts, axis=1)
        dx_ref[...] = dxd * dt4 + dyv * d_ref[...]
        dcbb = dcb.astype(BF16)
        dc_ref[...] = dc_acc + _dot(dcbb, bbf)
        db_ref[...] = db_acc + _dot(dcbb, cbf, 0, 0)
        lane = _iota((Lc, 128), 1)
        endv = jnp.zeros((Lc, 128), F32)
        for hh in range(4):
            endv = jnp.where(lane == 8 + hh, ends[hh], endv)
        dh_ref[...] = _dot3(dxd * xv, ind_a) + _dot3(u_all, ind_b) + endv

    rev = lambda c: nc - 1 - c
    per_g = lambda g, c: (g, rev(c), 0)
    return pl.pallas_call(
        body, name=name, grid=(SSM_GROUPS, nc),
        in_specs=[pl.BlockSpec((Lc, 256), lambda g, c: (rev(c), g)),
                  pl.BlockSpec((Lc, 128), lambda g, c: (rev(c), 16 + g)),
                  pl.BlockSpec((Lc, 128), lambda g, c: (rev(c), 24 + g)),
                  pl.BlockSpec((None, Lc, 128), per_g), pl.BlockSpec((None, Lc, 128), per_g),
                  pl.BlockSpec((SSM_HEADS, Lc), lambda g, c: (0, rev(c))),
                  pl.BlockSpec((None, None, 2, 128, 128), lambda g, c: (g, rev(c), 0, 0, 0)),
                  pl.BlockSpec((Lc, 256), lambda g, c: (rev(c), g)),
                  pl.BlockSpec((1, 256), lambda g, c: (0, g))],
        out_specs=[pl.BlockSpec((Lc, 256), lambda g, c: (rev(c), g)),
                   pl.BlockSpec((Lc, 128), lambda g, c: (rev(c), g)),
                   pl.BlockSpec((Lc, 128), lambda g, c: (rev(c), g)),
                   pl.BlockSpec((None, Lc, 128), per_g)],
        out_shape=[jax.ShapeDtypeStruct((S, 2048), F32), jax.ShapeDtypeStruct((S, 1024), F32),
                   jax.ShapeDtypeStruct((S, 1024), F32), jax.ShapeDtypeStruct((SSM_GROUPS, S, 128), F32)],
        scratch_shapes=[pltpu.VMEM((2, 128, 128), F32)], compiler_params=_cp(2),
    )(xbc, xbc, xbc, dt_g, ac_g, ac_t, states, dy, dexp)


def _ssd_post(dhead, dt_g, sg_g, alog_g, *, name="ssd_post"):
    S = dhead.shape[1]
    nc = S // CHUNK
    Lc = CHUNK

    def body(dh_ref, dt_ref, sg_ref, al_ref, o_ref, s_ref):
        c = pl.program_id(1)
        dh = dh_ref[...]
        a_neg = -jnp.exp(al_ref[...])
        lane = _iota((Lc, 128), 1)
        row = _iota((Lc, 128), 0)
        dac = jnp.where(lane < 4, pltpu.roll(dh, 124, axis=1), 0.0)
        end = jnp.where(lane < 4, pltpu.roll(dh, 120, axis=1), 0.0)
        k = 1
        while k < Lc:
            dac = dac + jnp.where(row < Lc - k, pltpu.roll(dac, Lc - k, axis=0), 0.0)
            k *= 2
        da = dac + end
        dtv = dt_ref[...]
        ddt = jnp.where(lane < 4, da * a_neg + dh, 0.0)
        ddtr = ddt * sg_ref[...]
        o_ref[...] = ddtr
        dal = jnp.sum(da * dtv, axis=0, keepdims=True) * a_neg
        dbias = jnp.sum(ddtr, axis=0, keepdims=True)
        row8 = _iota((8, 128), 0)
        part = jnp.where(row8 == 0, dal, jnp.where(row8 == 1, dbias, 0.0))

        @pl.when(c == 0)
        def _():
            s_ref[...] = part

        @pl.when(c > 0)
        def _():
            s_ref[...] += part

    per = pl.BlockSpec((None, Lc, 128), lambda g, c: (g, c, 0))
    return pl.pallas_call(
        body, name=name, grid=(SSM_GROUPS, nc),
        in_specs=[per, per, per, pl.BlockSpec((None, 1, 128), lambda g, c: (g, 0, 0))],
        out_specs=[per, pl.BlockSpec((None, 8, 128), lambda g, c: (g, 0, 0))],
        out_shape=[jax.ShapeDtypeStruct((SSM_GROUPS, S, 128), F32), jax.ShapeDtypeStruct((SSM_GROUPS, 8, 128), F32)],
        compiler_params=_cp(2),
    )(dhead, dt_g, sg_g, alog_g)


def _gate_fwd(y, xbc, zx, dexp, gn, *, name="gate_fwd", tr=256):
    S = y.shape[0]
    W = 2048
    gw = W // SSM_GROUPS

    def body(y_ref, x_ref, z_ref, d_ref, g_ref, o_ref):
        z = z_ref[...]
        u = (y_ref[...] + x_ref[...] * d_ref[...]) * (z * _sigmoid(z))
        gv = g_ref[...]
        for q in range(SSM_GROUPS):
            sl = slice(gw * q, gw * (q + 1))
            uq = u[:, sl]
            r = lax.rsqrt(jnp.mean(uq * uq, axis=-1, keepdims=True) + EPS)
            o_ref[:, sl] = (uq * r * gv[:, sl]).astype(BF16)

    row = pl.BlockSpec((tr, W), lambda i: (i, 0))
    vec = pl.BlockSpec((1, W), lambda i: (0, 0))
    return pl.pallas_call(
        body, name=name, grid=(S // tr,), in_specs=[row, row, row, vec, vec], out_specs=row,
        out_shape=jax.ShapeDtypeStruct((S, W), BF16), compiler_params=_cp(1),
    )(y, xbc, zx, dexp, gn)


def _gate_bwd(y, xbc, zx, dexp, gn, dout, *, name="gate_bwd", tr=256):
    S = y.shape[0]
    W = 2048
    gw = W // SSM_GROUPS
    steps = S // tr

    def body(y_ref, x_ref, z_ref, d_ref, g_ref, do_ref, dy_ref, dz_ref, dg_ref, dd_ref, acc_ref):
        i = pl.program_id(0)

        @pl.when(i == 0)
        def _():
            acc_ref[...] = jnp.zeros_like(acc_ref)

        z = z_ref[...]
        sg = _sigmoid(z)
        sz = z * sg
        xs = x_ref[...]
        yt = y_ref[...] + xs * d_ref[...]
        u = yt * sz
        gv = g_ref[...]
        do = do_ref[...]
        dgs = []
        for q in range(SSM_GROUPS):
            sl = slice(gw * q, gw * (q + 1))
            uq = u[:, sl]
            r = lax.rsqrt(jnp.mean(uq * uq, axis=-1, keepdims=True) + EPS)
            uh = uq * r
            dq = do[:, sl]
            duh = dq * gv[:, sl]
            duq = r * (duh - uh * jnp.mean(duh * uh, axis=-1, keepdims=True))
            dgs.append(jnp.sum(dq * uh, axis=0, keepdims=True))
            dyt = duq * sz[:, sl]
            dy_ref[:, sl] = dyt
            dz_ref[:, sl] = (duq * yt[:, sl] * (sg[:, sl] * (1.0 + z[:, sl] * (1.0 - sg[:, sl])))).astype(BF16)
            acc_ref[:, sl] += jnp.sum(dyt * xs[:, sl], axis=0, keepdims=True)
        dg = jnp.concatenate(dgs, axis=1)

        @pl.when(i == 0)
        def _():
            dg_ref[...] = dg

        @pl.when(i > 0)
        def _():
            dg_ref[...] += dg

        @pl.when(i == steps - 1)
        def _():
            ind = ((_iota((W, 128), 0) >> 6) == _iota((W, 128), 1)).astype(BF16)
            dd_ref[...] = _dot3(jnp.broadcast_to(acc_ref[...], (8, W)), ind)[0:1, :]

    row = pl.BlockSpec((tr, W), lambda i: (i, 0))
    vec = pl.BlockSpec((1, W), lambda i: (0, 0))
    return pl.pallas_call(
        body, name=name, grid=(steps,), in_specs=[row, row, row, vec, vec, row],
        out_specs=[row, row, vec, pl.BlockSpec((1, 128), lambda i: (0, 0))],
        out_shape=[jax.ShapeDtypeStruct((S, W), F32), jax.ShapeDtypeStruct((S, W), BF16),
                   jax.ShapeDtypeStruct((1, W), F32), jax.ShapeDtypeStruct((1, 128), F32)],
        scratch_shapes=[pltpu.VMEM((1, W), F32)], compiler_params=_cp(1),
    )(y, xbc, zx, dexp, gn, dout)


def _rope_tables(pos, shape):
    lane = _iota(shape, 1)
    j = (lane & 31).astype(F32)
    inv = jnp.exp(j * (-math.log(ROPE_THETA) / 32.0))
    ang = pos * inv
    return jnp.cos(ang), jnp.sin(ang), (lane & 63) < 32


def _hn_inds(W):
    ind = ((_iota((W, 128), 0) >> 6) == _iota((W, 128), 1)).astype(BF16)
    ind_t = ((_iota((128, W), 1) >> 6) == _iota((128, W), 0)).astype(BF16)
    return ind, ind_t


def _hnrope_fwd(xin, col0, W, gain_w, posf, *, name, tr=256):
    S = xin.shape[0]
    off = col0 // W

    def body(x_ref, g_ref, p_ref, o_ref):
        x = x_ref[...]
        ind, ind_t = _hn_inds(W)
        r = lax.rsqrt(_dot3(x * x, ind) * (1.0 / HEAD) + EPS)
        xn = x * _dot3(r, ind_t) * g_ref[...]
        cs, sn, half = _rope_tables(p_ref[...], (tr, W))
        rot = jnp.where(half, -pltpu.roll(xn, W - 32, axis=1), pltpu.roll(xn, 32, axis=1))
        o_ref[...] = (xn * cs + rot * sn).astype(BF16)

    return pl.pallas_call(
        body, name=name, grid=(S // tr,),
        in_specs=[pl.BlockSpec((tr, W), lambda i: (i, off)), pl.BlockSpec((1, W), lambda i: (0, 0)),
                  pl.BlockSpec((tr, 1), lambda i: (i, 0))],
        out_specs=pl.BlockSpec((tr, W), lambda i: (i, 0)), out_shape=jax.ShapeDtypeStruct((S, W), BF16),
        compiler_params=_cp(1),
    )(xin, gain_w, posf)


def _hnrope_bwd(xin, col0, W, gain_w, posf, dout, *, name, tr=256):
    S = xin.shape[0]
    off = col0 // W
    steps = S // tr

    def body(x_ref, g_ref, p_ref, do_ref, dx_ref, cs_ref, dg_ref, acc_ref):
        i = pl.program_id(0)
        x = x_ref[...]
        ind, ind_t = _hn_inds(W)
        r = lax.rsqrt(_dot3(x * x, ind) * (1.0 / HEAD) + EPS)
        rw = _dot3(r, ind_t)
        xh = x * rw
        cs, sn, half = _rope_tables(p_ref[...], (tr, W))
        do = do_ref[...].astype(F32)
        gs = do * sn
        g1 = do * cs + jnp.where(half, pltpu.roll(gs, W - 32, axis=1), -pltpu.roll(gs, 32, axis=1))
        dxh = g1 * g_ref[...]
        t = _dot3(dxh * xh, ind) * (1.0 / HEAD)
        dx = rw * (dxh - xh * _dot3(t, ind_t))
        dx_ref[...] = dx.astype(BF16)
        cpart = jnp.sum(dx, axis=0, keepdims=True)
        gpart = jnp.sum(g1 * xh, axis=0, keepdims=True)

        @pl.when(i == 0)
        def _():
            cs_ref[...] = cpart
            acc_ref[...] = gpart

        @pl.when(i > 0)
        def _():
            cs_ref[...] += cpart
            acc_ref[...] += gpart

        @pl.when(i == steps - 1)
        def _():
            fold = ((_iota((W, 128), 0) & 63) == _iota((W, 128), 1)).astype(BF16)
            dg_ref[...] = _dot3(jnp.broadcast_to(acc_ref[...], (8, W)), fold)[0:1, :]

    return pl.pallas_call(
        body, name=name, grid=(steps,),
        in_specs=[pl.BlockSpec((tr, W), lambda i: (i, off)), pl.BlockSpec((1, W), lambda i: (0, 0)),
                  pl.BlockSpec((tr, 1), lambda i: (i, 0)), pl.BlockSpec((tr, W), lambda i: (i, 0))],
        out_specs=[pl.BlockSpec((tr, W), lambda i: (i, 0)), pl.BlockSpec((1, W), lambda i: (0, 0)),
                   pl.BlockSpec((1, 128), lambda i: (0, 0))],
        out_shape=[jax.ShapeDtypeStruct((S, W), BF16), jax.ShapeDtypeStruct((1, W), F32),
                   jax.ShapeDtypeStruct((1, 128), F32)],
        scratch_shapes=[pltpu.VMEM((1, W), F32)], compiler_params=_cp(1),
    )(xin, gain_w, posf, dout)


def _attn_probs(q, kb, sink_ref, h, i):
    s = _dot(q, kb, 1, 1) * (HEAD ** -0.5)
    r = _iota((4 * WINDOW, 2 * WINDOW), 0)
    ki = _iota((4 * WINDOW, 2 * WINDOW), 1)
    rel = (r & (WINDOW - 1)) + WINDOW - ki
    mask = (rel >= 0) & (rel < WINDOW) & ((ki >= WINDOW) | (i > 0))
    s = jnp.where(mask, s, NEG)
    r1 = _iota((4 * WINDOW, 1), 0)
    sink = jnp.where(r1 < WINDOW, sink_ref[4 * h], jnp.where(r1 < 2 * WINDOW, sink_ref[4 * h + 1],
                     jnp.where(r1 < 3 * WINDOW, sink_ref[4 * h + 2], sink_ref[4 * h + 3])))
    m = jnp.maximum(jnp.max(s, axis=1, keepdims=True), sink)
    p = jnp.exp(s - m)
    ps = jnp.exp(sink - m)
    inv = 1.0 / (jnp.sum(p, axis=1, keepdims=True) + ps)
    return p * inv, ps * inv


def _attn_specs(S):
    qspec = pl.BlockSpec((None, ATT_G, WINDOW, HEAD), lambda h, i: (h, 0, i, 0))
    cur = pl.BlockSpec((None, WINDOW, HEAD), lambda h, i: (h, i, 0))
    prev = pl.BlockSpec((None, WINDOW, HEAD), lambda h, i: (h, jnp.maximum(i - 1, 0), 0))
    return qspec, cur, prev


def _attn_fwd(qh, kh, vh, sinks, *, name="attn_fwd"):
    S = kh.shape[1]
    nb = S // WINDOW

    def body(s_ref, q_ref, kc_ref, kp_ref, vc_ref, vp_ref, o_ref):
        h, i = pl.program_id(0), pl.program_id(1)
        q = q_ref[...].reshape(ATT_G * WINDOW, HEAD)
        kb = jnp.concatenate([kp_ref[...], kc_ref[...]], axis=0)
        vb = jnp.concatenate([vp_ref[...], vc_ref[...]], axis=0)
        probs, _ = _attn_probs(q, kb, s_ref, h, i)
        o = _dot(probs.astype(BF16), vb)
        o_ref[...] = o.reshape(ATT_G, WINDOW, HEAD).astype(BF16)

    qspec, cur, prev = _attn_specs(S)
    return pl.pallas_call(
        body, name=name, grid=(ATT_KV, nb),
        in_specs=[pl.BlockSpec(memory_space=pltpu.SMEM), qspec, cur, prev, cur, prev], out_specs=qspec,
        out_shape=jax.ShapeDtypeStruct((ATT_KV, ATT_G, S, HEAD), BF16), compiler_params=_cp(2),
    )(sinks, qh, kh, kh, vh, vh)


def _attn_bwd(qh, kh, vh, sinks, doh, *, name="attn_bwd"):
    S = kh.shape[1]
    nb = S // WINDOW

    def body(s_ref, q_ref, kc_ref, kp_ref, vc_ref, vp_ref, do_ref, dq_ref, dk_ref, dv_ref, dsk_ref):
        h, i = pl.program_id(0), pl.program_id(1)

        @pl.when(i == 0)
        def _():
            dk_ref[...] = jnp.zeros_like(dk_ref)
            dv_ref[...] = jnp.zeros_like(dv_ref)
            dsk_ref[...] = jnp.zeros_like(dsk_ref)

        q = q_ref[...].reshape(ATT_G * WINDOW, HEAD)
        do = do_ref[...].reshape(ATT_G * WINDOW, HEAD)
        kb = jnp.concatenate([kp_ref[...], kc_ref[...]], axis=0)
        vb = jnp.concatenate([vp_ref[...], vc_ref[...]], axis=0)
        probs, psink = _attn_probs(q, kb, s_ref, h, i)
        dp = _dot(do, vb, 1, 1)
        delta = jnp.sum(probs * dp, axis=1, keepdims=True)
        ds = (probs * (dp - delta)).astype(BF16)
        scale = HEAD ** -0.5
        dq_ref[...] = (_dot(ds, kb) * scale).reshape(ATT_G, WINDOW, HEAD)
        dkb = _dot(ds, q, 0, 0) * scale
        dvb = _dot(probs.astype(BF16), do, 0, 0)
        cur = pl.multiple_of(i * WINDOW, WINDOW)
        dk_ref[pl.ds(cur, WINDOW), :] += dkb[WINDOW:, :]
        dv_ref[pl.ds(cur, WINDOW), :] += dvb[WINDOW:, :]

        @pl.when(i > 0)
        def _():
            prv = pl.multiple_of((i - 1) * WINDOW, WINDOW)
            dk_ref[pl.ds(prv, WINDOW), :] += dkb[:WINDOW, :]
            dv_ref[pl.ds(prv, WINDOW), :] += dvb[:WINDOW, :]

        dsr = -psink * delta
        lane = _iota((8, 128), 1)
        row = _iota((8, 128), 0)
        upd = jnp.zeros((8, 128), F32)
        for gq in range(ATT_G):
            v = jnp.sum(dsr[gq * WINDOW:(gq + 1) * WINDOW, :], axis=0, keepdims=True)
            upd = jnp.where((lane == gq) & (row == 0), v, upd)
        dsk_ref[...] += upd

    qspec, cur, prev = _attn_specs(S)
    full = pl.BlockSpec((None, S, HEAD), lambda h, i: (h, 0, 0))
    return pl.pallas_call(
        body, name=name, grid=(ATT_KV, nb),
        in_specs=[pl.BlockSpec(memory_space=pltpu.SMEM), qspec, cur, prev, cur, prev, qspec],
        out_specs=[qspec, full, full, pl.BlockSpec((None, 8, 128), lambda h, i: (h, 0, 0))],
        out_shape=[jax.ShapeDtypeStruct((ATT_KV, ATT_G, S, HEAD), F32), jax.ShapeDtypeStruct((ATT_KV, S, HEAD), F32),
                   jax.ShapeDtypeStruct((ATT_KV, S, HEAD), F32), jax.ShapeDtypeStruct((ATT_KV, 8, 128), F32)],
        compiler_params=_cp(2),
    )(sinks, qh, kh, kh, vh, vh, doh)


def _heads_major(t, nh):
    S = t.shape[0]
    return t.reshape(S, nh, HEAD).transpose(1, 0, 2)


def _tokens_major(t):
    nh, S, _ = t.shape
    return t.transpose(1, 0, 2).reshape(S, nh * HEAD)


def _local_step(x, posf, target, w):
    S, D = x.shape
    gr = {}

    (h1,) = _rms_fwd(x, [w["a_norm"]], name="a_norm_f")
    zx = _mm(h1, w["w_zx"], name="in_proj_zx")
    dtr = _mm(h1, w["w_dt"], name="in_proj_dt")
    xbc = _conv_silu_fwd(zx, 2048, 4096, w["a_conv_w"], w["a_conv_b"], name="a_conv_f")
    dt, acum, sgd = _ssd_prep(dtr, w["a_dt_bias"], w["a_A_log"])
    dt_g, ac_g, sg_g = _to_groups(dt), _to_groups(acum), _to_groups(sgd)
    ac_t = acum[:, :SSM_HEADS].T
    y_ssd, states = _ssd_fwd(xbc, dt_g, ac_g, ac_t)
    yg = _gate_fwd(y_ssd, xbc, zx, w["a_Dexp"], w["a_gnorm"])
    x1 = _mm(yg, w["a_out_proj"], res=x, name="out_proj")

    FW = w["f_w_in"][0].shape[2]

    def ffn_fwd(xin, l):
        (h,) = _rms_fwd(xin, [w["f_norm"][l]], name=f"f_norm_f{l}")
        u = _mm(h, w["f_w_in"][l], name=f"f_in{l}", dims=(S, N_CHIPS * FW, D), tn=FW,
                b_spec=pl.BlockSpec((None, D, FW), lambda i, j, k: (j, 0, 0)))
        a = _ffn_act_fwd(u, w["f_conv_w"][l], w["f_conv_b"][l], name=f"f_act_f{l}")
        xo = _mm(a, w["f_w_down"][l], res=xin, name=f"f_down{l}")
        return xo, (h, u)

    x2, ffn0 = ffn_fwd(x1, 0)

    hk, hq = _rms_fwd(x2, [w["kv_norm"], w["b_norm"]], name="kvq_norm_f")
    kv = _mm(hk, w["w_kv"], bias=w["b_kv"], name="kv_proj")
    q = _mm(hq, w["w_q"], bias=w["b_q"], name="q_proj")
    kr = _hnrope_fwd(kv, 0, 256, w["k_norm_w"], posf, name="k_rope_f")
    qr = _hnrope_fwd(q, 0, 1024, w["q_norm_w"], posf, name="q_rope_f")
    qh = _heads_major(qr, 16).reshape(ATT_KV, ATT_G, S, HEAD)
    kh = _heads_major(kr, ATT_KV)
    vh = _heads_major(kv[:, 256:].astype(BF16), ATT_KV)
    att_h = _attn_fwd(qh, kh, vh, w["sinks"])
    att = _tokens_major(att_h.reshape(16, S, HEAD))
    x3 = _mm(att, w["w_o"], bias=w["b_o"], res=x2, name="o_proj")
    x4, ffn1 = ffn_fwd(x3, 1)

    dy, loss_part = _loss(x4, target)

    def ffn_bwd(xin, l, saved, dyo, want_colsum):
        h, u = saved
        da = _mm(dyo, w["f_w_down"][l], tb=True, name=f"f_down_dx{l}")
        a = _ffn_act_fwd(u, w["f_conv_w"][l], w["f_conv_b"][l], name=f"f_act_r{l}")
        du, dcw, dcb = _ffn_act_bwd(u, w["f_conv_w"][l], w["f_conv_b"][l], da, name=f"f_act_b{l}")
        dw_down = _mm(a, dyo, ta=True, out_dtype=BF16, name=f"f_down_dw{l}")
        dw_in = _mm(h, du, ta=True, out_dtype=BF16, name=f"f_in_dw{l}", dims=(D, N_CHIPS * FW, S), tm=D, tn=FW, tk=S,
                    b_spec=pl.BlockSpec((None, S, FW), lambda i, j, k: (j // 2, 0, j % 2)),
                    o_spec=pl.BlockSpec((None, D, FW), lambda i, j, k: (j, i, 0)), o_shape=(N_CHIPS, D, FW))
        ts = _pick(S, (1024, 512, 256))
        dh = _mm(du, w["f_w_in"][l], tb=True, name=f"f_in_dx{l}", dims=(S, D, N_CHIPS * FW), tm=ts, tn=512, tk=FW,
                 a_spec=pl.BlockSpec((None, ts, FW), lambda i, j, k: (k // 2, i, k % 2)),
                 b_spec=pl.BlockSpec((None, 512, FW), lambda i, j, k: (k, j, 0)))
        outs = _rms_bwd(xin, [w["f_norm"][l]], [dh], dyo, name=f"f_norm_b{l}", want_colsum=want_colsum)
        g = dict(f_norm=outs[1], f_w_in=dw_in, f_conv_w=dcw, f_conv_b=dcb, f_w_down=dw_down)
        return outs[0], g, (outs[2] if want_colsum else None)

    dx3, gr["ffn1"], db_o = ffn_bwd(x3, 1, ffn1, dy, True)
    gr["b_o"] = db_o
    gr["w_o"] = _mm(att, dx3, ta=True, out_dtype=BF16, name="o_proj_dw")
    datt = _mm(dx3, w["w_o"], tb=True, out_dtype=BF16, name="o_proj_dx")
    doh = _heads_major(datt, 16).reshape(ATT_KV, ATT_G, S, HEAD)
    dqh, dkh, dvh, dsk = _attn_bwd(qh, kh, vh, w["sinks"], doh)
    gr["sinks"] = dsk[:, 0, :4].reshape(1, 16)
    dqr = _tokens_major(dqh.reshape(16, S, HEAD))
    dkr = _tokens_major(dkh)
    dv = _tokens_major(dvh).astype(BF16)
    dq, db_q, dqn = _hnrope_bwd(q, 0, 1024, w["q_norm_w"], posf, dqr, name="q_rope_b")
    dk, db_k, dkn = _hnrope_bwd(kv, 0, 256, w["k_norm_w"], posf, dkr, name="k_rope_b")
    gr["q_norm"], gr["k_norm"] = dqn[:, :HEAD], dkn[:, :HEAD]
    gr["b_q"] = db_q
    gr["b_kv"] = jnp.concatenate([db_k, _colsum(dv, name="dv_colsum")], axis=1)
    dkv = jnp.concatenate([dk, dv], axis=1)
    gr["w_q"] = _mm(hq, dq, ta=True, out_dtype=BF16, name="q_proj_dw")
    gr["w_kv"] = _mm(hk, dkv, ta=True, out_dtype=BF16, name="kv_proj_dw")
    dhq = _mm(dq, w["w_q"], tb=True, name="q_proj_dx")
    dhk = _mm(dkv, w["w_kv"], tb=True, name="kv_proj_dx")
    dx2, gr["kv_norm"], gr["b_norm"] = _rms_bwd(x2, [w["kv_norm"], w["b_norm"]], [dhk, dhq], dx3, name="kvq_norm_b")

    dx1, gr["ffn0"], _ = ffn_bwd(x1, 0, ffn0, dx2, False)

    gr["a_out_proj"] = _mm(yg, dx1, ta=True, out_dtype=BF16, name="out_proj_dw")
    dyg = _mm(dx1, w["a_out_proj"], tb=True, name="out_proj_dx")
    dy_ssd, dz, gr["a_gnorm"], dD = _gate_bwd(y_ssd, xbc, zx, w["a_Dexp"], w["a_gnorm"], dyg)
    gr["a_D"] = dD[:, :SSM_HEADS]
    dxs, dB, dC, dhead = _ssd_bwd(xbc, dt_g, ac_g, ac_t, states, dy_ssd, w["a_Dexp"])
    ddtr_g, dsmall = _ssd_post(dhead, dt_g, sg_g, w["a_A_log_g"])
    gr["a_A_log"] = dsmall[:, 0, :4].reshape(1, SSM_HEADS)
    gr["a_dt_bias"] = dsmall[:, 1, :4].reshape(1, SSM_HEADS)
    ddtr = _from_groups(ddtr_g).astype(BF16)
    dxbc, gr["a_conv_w"], gr["a_conv_b"] = _conv_silu_bwd(
        zx, 2048, 4096, w["a_conv_w"], w["a_conv_b"], [(dxs, 0), (dB, 2048), (dC, 3072)], name="a_conv_b")
    gr["w_z"] = _mm(h1, dz, ta=True, out_dtype=BF16, name="in_proj_dwz")
    gr["w_x"] = _mm(h1, dxbc, ta=True, out_dtype=BF16, name="in_proj_dwx")
    gr["w_dt"] = _mm(h1, ddtr, ta=True, out_dtype=BF16, name="in_proj_dwdt")
    dh1 = _mm(dz, w["w_zx"], tb=True, name="in_proj_dxz")
    dh1 = _mm(dxbc, w["w_zx"], tb=True, b_koff=2048, res=dh1, name="in_proj_dxx")
    dh1 = _mm(ddtr, w["w_dt"], tb=True, res=dh1, name="in_proj_dxdt")
    dx0, gr["a_norm"] = _rms_bwd(x, [w["a_norm"]], [dh1], dx1, name="a_norm_b")
    return loss_part, dx0, gr


def _prep_small(full, w):
    w["a_norm"] = full["a_norm"]
    w["a_conv_w"] = full["a_conv_w"][0]
    w["a_conv_b"] = full["a_conv_b"]
    pad32 = lambda v: jnp.pad(v, ((0, 0), (0, 128 - SSM_HEADS)))
    w["a_dt_bias"] = pad32(full["a_dt_bias"])
    w["a_A_log"] = pad32(full["a_A_log"])
    w["a_A_log_g"] = jnp.pad(full["a_A_log"].reshape(SSM_GROUPS, 1, 4), ((0, 0), (0, 0), (0, 124)))
    w["a_Dexp"] = jnp.repeat(full["a_D"], HEAD, axis=1)
    w["a_gnorm"] = full["a_gnorm"]
    w["f_norm"] = [full["f_norm"][l:l + 1] for l in range(2)]
    w["f_conv_w"] = [full["f_conv_w"][l] for l in range(2)]
    w["f_conv_b"] = [full["f_conv_b"][l:l + 1] for l in range(2)]
    w["kv_norm"] = full["kv_norm"].reshape(1, -1)
    w["b_kv"] = full["b_kv"].reshape(1, -1)
    w["k_norm_w"] = jnp.tile(full["k_norm"].reshape(1, HEAD), (1, ATT_KV))
    w["b_norm"] = full["b_norm"]
    w["b_q"] = full["b_q"]
    w["q_norm_w"] = jnp.tile(full["q_norm"], (1, ATT_KV * ATT_G))
    w["sinks"] = full["sinks"].reshape(-1)
    w["b_o"] = full["b_o"]
    return w


def _split_in_proj(ip):
    return ip[:, :6144].astype(BF16), jnp.pad(ip[:, 6144:], ((0, 0), (0, 128 - SSM_HEADS))).astype(BF16)


def _prep_weights(full):
    w = _prep_small(full, {})
    w["w_zx"], w["w_dt"] = _split_in_proj(full["a_in_proj"][0])
    w["a_out_proj"] = full["a_out_proj"][0].astype(BF16)
    w["f_w_in"] = [full["f_w_in"][l].reshape(1024, N_CHIPS, -1).transpose(1, 0, 2).astype(BF16) for l in range(2)]
    w["f_w_down"] = [full["f_w_down"][l].astype(BF16) for l in range(2)]
    w["w_kv"] = full["w_kv"].astype(BF16)
    w["w_q"] = full["w_q"][0].astype(BF16)
    w["w_o"] = full["w_o"][0].astype(BF16)
    return w


def _small_grads(gr):
    g = {}
    g["a_norm"] = gr["a_norm"]
    g["a_conv_w"] = gr["a_conv_w"][None]
    g["a_conv_b"] = gr["a_conv_b"]
    g["a_dt_bias"], g["a_A_log"], g["a_D"] = gr["a_dt_bias"], gr["a_A_log"], gr["a_D"]
    g["a_gnorm"] = gr["a_gnorm"]
    g["kv_norm"] = gr["kv_norm"].reshape(-1)
    g["b_kv"] = gr["b_kv"].reshape(-1)
    g["k_norm"] = gr["k_norm"].reshape(-1)
    g["b_norm"] = gr["b_norm"]
    g["b_q"] = gr["b_q"]
    g["q_norm"] = gr["q_norm"]
    g["sinks"] = gr["sinks"]
    g["b_o"] = gr["b_o"]
    f = [gr["ffn0"], gr["ffn1"]]
    g["f_norm"] = jnp.concatenate([f[0]["f_norm"], f[1]["f_norm"]], axis=0)
    g["f_conv_w"] = jnp.stack([f[l]["f_conv_w"] for l in range(2)])
    g["f_conv_b"] = jnp.concatenate([f[l]["f_conv_b"] for l in range(2)], axis=0)
    return g


def _in_proj_grad(gr):
    return jnp.concatenate([gr["w_z"], gr["w_x"], gr["w_dt"][:, :SSM_HEADS]], axis=1)


def _full_grads(gr):
    g = _small_grads(gr)
    f32 = lambda t: t.astype(F32)
    g["a_in_proj"] = f32(_in_proj_grad(gr))[None]
    g["a_out_proj"] = f32(gr["a_out_proj"])[None]
    g["w_kv"] = f32(gr["w_kv"])
    g["w_q"] = f32(gr["w_q"])[None]
    g["w_o"] = f32(gr["w_o"])[None]
    f = [gr["ffn0"], gr["ffn1"]]
    g["f_w_in"] = jnp.stack([f32(f[l]["f_w_in"]).transpose(1, 0, 2).reshape(1024, -1) for l in range(2)])
    g["f_w_down"] = jnp.stack([f32(f[l]["f_w_down"]) for l in range(2)])
    return g


MESH = pl.DeviceIdType.MESH
WEIGHTS = ("a_norm", "a_in_proj", "a_conv_w", "a_conv_b", "a_dt_bias", "a_A_log", "a_D", "a_gnorm", "a_out_proj",
           "kv_norm", "w_kv", "b_kv", "k_norm", "b_norm", "w_q", "b_q", "q_norm", "sinks", "w_o", "b_o", "f_norm",
           "f_w_in", "f_conv_w", "f_conv_b", "f_w_down")
MATS = (("in_proj", "a_in_proj", 0), ("out_proj", "a_out_proj", 0), ("w_kv", "w_kv", None), ("w_q", "w_q", 0),
        ("w_o", "w_o", 0), ("f_in0", "f_w_in", 0), ("f_in1", "f_w_in", 1), ("f_down0", "f_w_down", 0),
        ("f_down1", "f_w_down", 1))
SMALL_CUT = (("a_norm", 1), ("a_conv_w", 2), ("a_conv_b", 1), ("a_gnorm", 1), ("f_conv_w", 2))
SMALL_REP = ("a_dt_bias", "a_A_log", "a_D", "kv_norm", "b_kv", "k_norm", "b_norm", "b_q", "q_norm", "sinks", "b_o",
             "f_norm", "f_conv_b")


def _coords():
    return lax.axis_index("x"), lax.axis_index("y"), lax.axis_index("c")


def _other_chips(x, y):
    return [(1 - x, y), (x, 1 - y), (1 - x, 1 - y)]


def _pack(arrs, rows_align, lanes, dtype):
    flat = jnp.concatenate([a.reshape(-1).astype(dtype) for a in arrs])
    per = rows_align * lanes
    total = -(-flat.shape[0] // per) * per
    return jnp.pad(flat, (0, total - flat.shape[0])).reshape(total // lanes, lanes)


def _unpack(flat, shapes):
    out, off = [], 0
    for s in shapes:
        n = math.prod(s)
        out.append(flat[off:off + n].reshape(s))
        off += n
    return out


def _remote(src, dst, send, recv, k, dev):
    return pltpu.make_async_remote_copy(src_ref=src, dst_ref=dst, send_sem=send.at[k], recv_sem=recv.at[k],
                                        device_id=dev, device_id_type=MESH)


_ANY = pl.BlockSpec(memory_space=pl.ANY)


def _halves(t):
    r, c = t.shape
    return t.reshape(2, r // 2, c)


def _gather_weights(shards, sp):
    n = len(shards)
    n_sem = 7 * n + 3

    def body(*refs):
        sh, sp_ref = refs[:n], refs[n]
        outs, sout = refs[n + 1:2 * n + 1], refs[2 * n + 1]
        send, recv, loc = refs[2 * n + 2:]
        x, y, c = _coords()
        me = 2 * x + y
        chips = _other_chips(x, y)
        sib = (x, y, 1 - c)
        l1 = pltpu.make_async_copy(sp_ref, sout.at[me], loc.at[0])
        l1.start()
        sends = []
        for j, (cx, cy) in enumerate(chips):
            sends.append(_remote(sp_ref, sout.at[me], send, recv, 7 * n + j, (cx, cy, c)))
            for t in range(n):
                sends.append(_remote(sh[t].at[c], outs[t].at[me, c], send, recv, 7 * t + j, (cx, cy, c)))
        for t in range(n):
            sends.append(_remote(sh[t], outs[t].at[me], send, recv, 7 * t + 6, sib))
        for cp in sends:
            cp.start()
        for j, (cx, cy) in enumerate(chips):
            src = 2 * cx + cy
            for t in range(n):
                _remote(sh[t].at[c], outs[t].at[src, c], send, recv, 7 * t + j, (cx, cy, c)).wait_recv()
                fwd = _remote(outs[t].at[src, c], outs[t].at[src, c], send, recv, 7 * t + 3 + j, sib)
                fwd.start()
                sends.append(fwd)
        for j, (cx, cy) in enumerate(chips):
            src = 2 * cx + cy
            _remote(sp_ref, sout.at[src], send, recv, 7 * n + j, (cx, cy, c)).wait_recv()
            for t in range(n):
                _remote(outs[t].at[src, 1 - c], outs[t].at[src, 1 - c], send, recv, 7 * t + 3 + j, sib).wait_recv()
        for t in range(n):
            _remote(sh[t], outs[t].at[me], send, recv, 7 * t + 6, sib).wait_recv()
        for cp in sends:
            cp.wait_send()
        l1.wait()

    res = pl.pallas_call(
        body, name="gather_weights", in_specs=[_ANY] * (n + 1), out_specs=[_ANY] * (n + 1),
        out_shape=[jax.ShapeDtypeStruct((N_CHIPS,) + t.shape, t.dtype) for t in shards]
        + [jax.ShapeDtypeStruct((N_CHIPS,) + sp.shape, sp.dtype)],
        scratch_shapes=[pltpu.SemaphoreType.DMA((n_sem,)), pltpu.SemaphoreType.DMA((n_sem,)),
                        pltpu.SemaphoreType.DMA((1,))],
    )(*shards, sp)
    return res[:n], res[n]


def _allreduce_small(v):
    SR = v.shape[0]

    def body(v_ref, o_ref, buf, send, recv):
        x, y, c = _coords()
        me = 4 * x + 2 * y + c
        buf[me] = v_ref[...]
        peers = []
        for k in range(1, 8):
            px = 1 - x if k & 4 else x
            py = 1 - y if k & 2 else y
            pc = 1 - c if k & 1 else c
            peers.append((px, py, pc))
        cps = [_remote(v_ref, buf.at[me], send, recv, k, p) for k, p in enumerate(peers)]
        for cp in cps:
            cp.start()
        for k, (px, py, pc) in enumerate(peers):
            _remote(v_ref, buf.at[4 * px + 2 * py + pc], send, recv, k, (px, py, pc)).wait_recv()
        for cp in cps:
            cp.wait_send()
        acc = buf[0]
        for s in range(1, 8):
            acc = acc + buf[s]
        o_ref[...] = acc

    vm = pl.BlockSpec(memory_space=pltpu.VMEM)
    return pl.pallas_call(
        body, name="allreduce_small", in_specs=[vm], out_specs=vm, out_shape=jax.ShapeDtypeStruct(v.shape, F32),
        scratch_shapes=[pltpu.VMEM((8, SR, 128), F32), pltpu.SemaphoreType.DMA((7,)), pltpu.SemaphoreType.DMA((7,))],
    )(v)


def _rs_to_sibling(gs):
    n = len(gs)

    def body(*refs):
        g, a = refs[:n], refs[n:2 * n]
        send, recv = refs[2 * n:]
        x, y, c = _coords()
        cps = [_remote(g[t].at[:, 1 - c], a[t], send, recv, t, (x, y, 1 - c)) for t in range(n)]
        for cp in cps:
            cp.start()
        for cp in cps:
            cp.wait()

    return pl.pallas_call(
        body, name="rs_to_sibling", in_specs=[_ANY] * n, out_specs=[_ANY] * n,
        out_shape=[jax.ShapeDtypeStruct((N_CHIPS,) + g.shape[2:], g.dtype) for g in gs],
        scratch_shapes=[pltpu.SemaphoreType.DMA((n,)), pltpu.SemaphoreType.DMA((n,))],
    )(*gs)


def _rs_add_pair(g, a, c_idx, *, name):
    _, _, rh, cols = g.shape

    def body(c_ref, g_ref, a_ref, p_ref):
        p_ref[...] = (g_ref[...].astype(F32) + a_ref[...].astype(F32)).astype(BF16)

    return pl.pallas_call(
        body, name=name,
        grid_spec=pltpu.PrefetchScalarGridSpec(
            num_scalar_prefetch=1, grid=(N_CHIPS,),
            in_specs=[pl.BlockSpec((None, None, rh, cols), lambda j, c_ref: (j, c_ref[0], 0, 0)),
                      pl.BlockSpec((None, rh, cols), lambda j, c_ref: (j, 0, 0))],
            out_specs=pl.BlockSpec((None, rh, cols), lambda j, c_ref: (j, 0, 0))),
        out_shape=jax.ShapeDtypeStruct((N_CHIPS, rh, cols), BF16), compiler_params=_cp(1),
    )(c_idx, g, a)


def _rs_to_chips(ps):
    n = len(ps)

    def body(*refs):
        p, r = refs[:n], refs[n:2 * n]
        send, recv = refs[2 * n:]
        x, y, c = _coords()
        cps = [_remote(p[t].at[2 * cx + cy], r[t].at[k], send, recv, 3 * t + k, (cx, cy, c))
               for k, (cx, cy) in enumerate(_other_chips(x, y)) for t in range(n)]
        for cp in cps:
            cp.start()
        for cp in cps:
            cp.wait()

    return pl.pallas_call(
        body, name="rs_to_chips", in_specs=[_ANY] * n, out_specs=[_ANY] * n,
        out_shape=[jax.ShapeDtypeStruct((3,) + p.shape[1:], p.dtype) for p in ps],
        scratch_shapes=[pltpu.SemaphoreType.DMA((3 * n,)), pltpu.SemaphoreType.DMA((3 * n,))],
    )(*ps)


def _rs_add_chips(p, r, idx, *, name):
    _, rh, cols = p.shape

    def body(idx_ref, p_ref, r0_ref, r1_ref, r2_ref, o_ref):
        o_ref[...] = ((p_ref[...].astype(F32) + r0_ref[...].astype(F32)) + r1_ref[...].astype(F32)) + r2_ref[...].astype(F32)

    def rk(k):
        return pl.BlockSpec((None, rh, cols), lambda i, idx_ref, k=k: (k, 0, 0))

    return pl.pallas_call(
        body, name=name,
        grid_spec=pltpu.PrefetchScalarGridSpec(
            num_scalar_prefetch=1, grid=(1,),
            in_specs=[pl.BlockSpec((None, rh, cols), lambda i, idx_ref: (idx_ref[0], 0, 0)), rk(0), rk(1), rk(2)],
            out_specs=pl.BlockSpec((None, rh, cols), lambda i, idx_ref: (idx_ref[1], 0, 0))),
        out_shape=jax.ShapeDtypeStruct((2, rh, cols), F32), compiler_params=_cp(1),
    )(idx, p, r, r, r)


def _rs_join_halves(hs):
    n = len(hs)

    def body(*refs):
        o = refs[n:2 * n]
        send, recv = refs[2 * n:]
        x, y, c = _coords()
        cps = [_remote(o[t].at[c], o[t].at[c], send, recv, t, (x, y, 1 - c)) for t in range(n)]
        for cp in cps:
            cp.start()
        for t in range(n):
            _remote(o[t].at[1 - c], o[t].at[1 - c], send, recv, t, (x, y, 1 - c)).wait_recv()
        for cp in cps:
            cp.wait_send()

    return pl.pallas_call(
        body, name="rs_join_halves", in_specs=[_ANY] * n, out_specs=[_ANY] * n,
        input_output_aliases={t: t for t in range(n)},
        out_shape=[jax.ShapeDtypeStruct(h.shape, F32) for h in hs],
        scratch_shapes=[pltpu.SemaphoreType.DMA((n,)), pltpu.SemaphoreType.DMA((n,))],
    )(*hs)


def _adamw(w, gs, m, v, *, name):
    L, Rr, C = w.shape
    tr = _pick(Rr, (256, 128, 64, 8))
    bc1 = 1.0 - ADAM_B1 ** ADAM_STEP
    bc2 = 1.0 - ADAM_B2 ** ADAM_STEP

    def body(*refs):
        w_ref, m_ref, v_ref = refs[0], refs[1], refs[2]
        g_refs = refs[3:3 + L]
        d_ref, mo_ref, vo_ref, go_ref = refs[3 + L:]
        layer = pl.program_id(0)
        gv = g_refs[0][...]
        for q in range(1, L):
            gv = jnp.where(layer == q, g_refs[q][...], gv)
        mn = ADAM_B1 * m_ref[...] + (1.0 - ADAM_B1) * gv
        vn = ADAM_B2 * v_ref[...] + (1.0 - ADAM_B2) * (gv * gv)
        go_ref[...] = gv
        mo_ref[...] = mn
        vo_ref[...] = vn
        d_ref[...] = -ADAM_LR * ((mn / bc1) / (jnp.sqrt(vn / bc2) + ADAM_EPS) + ADAM_WD * w_ref[...])

    blk = pl.BlockSpec((None, tr, C), lambda l, i: (l, i, 0))
    gblk = pl.BlockSpec((tr, C), lambda l, i: (i, 0))
    return pl.pallas_call(
        body, name=name, grid=(L, Rr // tr), in_specs=[blk] * 3 + [gblk] * L, out_specs=[blk] * 4,
        out_shape=[jax.ShapeDtypeStruct((L, Rr, C), F32)] * 4, compiler_params=_cp(2),
    )(w, m, v, *gs)


def kernel(x, positions, a_norm, a_in_proj, a_conv_w, a_conv_b, a_dt_bias, a_A_log, a_D, a_gnorm, a_out_proj,
           kv_norm, w_kv, b_kv, k_norm, b_norm, w_q, b_q, q_norm, sinks, w_o, b_o, f_norm, f_w_in, f_conv_w,
           f_conv_b, f_w_down, loss_target, m_a_norm, m_a_in_proj, m_a_conv_w, m_a_conv_b, m_a_dt_bias, m_a_A_log,
           m_a_D, m_a_gnorm, m_a_out_proj, m_kv_norm, m_w_kv, m_b_kv, m_k_norm, m_b_norm, m_w_q, m_b_q, m_q_norm,
           m_sinks, m_w_o, m_b_o, m_f_norm, m_f_w_in, m_f_conv_w, m_f_conv_b, m_f_w_down, v_a_norm, v_a_in_proj,
           v_a_conv_w, v_a_conv_b, v_a_dt_bias, v_a_A_log, v_a_D, v_a_gnorm, v_a_out_proj, v_kv_norm, v_w_kv,
           v_b_kv, v_k_norm, v_b_norm, v_w_q, v_b_q, v_q_norm, v_sinks, v_w_o, v_b_o, v_f_norm, v_f_w_in,
           v_f_conv_w, v_f_conv_b, v_f_w_down):
    wl = dict(zip(WEIGHTS, (a_norm, a_in_proj, a_conv_w, a_conv_b, a_dt_bias, a_A_log, a_D, a_gnorm, a_out_proj,
                            kv_norm, w_kv, b_kv, k_norm, b_norm, w_q, b_q, q_norm, sinks, w_o, b_o, f_norm, f_w_in,
                            f_conv_w, f_conv_b, f_w_down)))
    ml = dict(zip(WEIGHTS, (m_a_norm, m_a_in_proj, m_a_conv_w, m_a_conv_b, m_a_dt_bias, m_a_A_log, m_a_D, m_a_gnorm,
                            m_a_out_proj, m_kv_norm, m_w_kv, m_b_kv, m_k_norm, m_b_norm, m_w_q, m_b_q, m_q_norm,
                            m_sinks, m_w_o, m_b_o, m_f_norm, m_f_w_in, m_f_conv_w, m_f_conv_b, m_f_w_down)))
    vl = dict(zip(WEIGHTS, (v_a_norm, v_a_in_proj, v_a_conv_w, v_a_conv_b, v_a_dt_bias, v_a_A_log, v_a_D, v_a_gnorm,
                            v_a_out_proj, v_kv_norm, v_w_kv, v_b_kv, v_k_norm, v_b_norm, v_w_q, v_b_q, v_q_norm,
                            v_sinks, v_w_o, v_b_o, v_f_norm, v_f_w_in, v_f_conv_w, v_f_conv_b, v_f_w_down)))
    xi, yi, ci = _coords()
    me = 2 * xi + yi
    S = x.shape[1]

    def block_of(n, layer):
        t = wl[n]
        return t if layer is None else t[layer]

    shards = [_halves(block_of(wn, layer).astype(BF16)) for _, wn, layer in MATS]
    sp = _pack([wl[n] for n, _ in SMALL_CUT], 8, 128, F32)
    gathered, gs = _gather_weights(shards, sp)
    gt = {name: t.reshape(N_CHIPS, -1, t.shape[-1]) for (name, _, _), t in zip(MATS, gathered)}
    full = {n: wl[n] for n in SMALL_REP}
    gs = gs.reshape(N_CHIPS, -1)
    pieces = [_unpack(gs[j], [wl[n].shape for n, _ in SMALL_CUT]) for j in range(N_CHIPS)]
    for q, (n, ax) in enumerate(SMALL_CUT):
        full[n] = jnp.concatenate([pieces[j][q] for j in range(N_CHIPS)], axis=ax)
    w = _prep_small(full, {})
    rows = lambda t: t.reshape(-1, t.shape[-1])
    w["w_zx"], w["w_dt"] = _split_in_proj(gt["in_proj"].transpose(1, 0, 2).reshape(1024, -1))
    w["a_out_proj"], w["w_kv"], w["w_q"], w["w_o"] = (rows(gt[k]) for k in ("out_proj", "w_kv", "w_q", "w_o"))
    w["f_w_in"] = [gt["f_in0"], gt["f_in1"]]
    w["f_w_down"] = [rows(gt["f_down0"]), rows(gt["f_down1"])]

    posf = positions.reshape(S, 1).astype(F32)
    loss_part, dx0, gr = _local_step(x[0], posf, loss_target[0], w)
    g = _small_grads(gr)

    small_names = [n for n, _ in SMALL_CUT] + list(SMALL_REP)
    sv = _pack([g[n] for n in small_names] + [loss_part[0:1, 0:1]], 8, 128, F32)
    sred = _allreduce_small(sv).reshape(-1)
    small_shapes = [g[n].shape for n in small_names] + [(1,)]
    sg = dict(zip(small_names + ["loss"], _unpack(sred, small_shapes)))
    loss = sg["loss"].reshape(())
    g_small = {}
    for n, ax in SMALL_CUT:
        size = wl[n].shape[ax]
        g_small[n] = lax.dynamic_slice_in_dim(sg[n], me * size, size, axis=ax)
    for n in SMALL_REP:
        g_small[n] = sg[n].reshape(wl[n].shape)

    gm = {"in_proj": _in_proj_grad(gr).reshape(1024, N_CHIPS, -1).transpose(1, 0, 2),
          "out_proj": gr["a_out_proj"], "w_kv": gr["w_kv"], "w_q": gr["w_q"], "w_o": gr["w_o"],
          "f_in0": gr["ffn0"]["f_w_in"], "f_in1": gr["ffn1"]["f_w_in"],
          "f_down0": gr["ffn0"]["f_w_down"], "f_down1": gr["ffn1"]["f_w_down"]}
    glist = [gm[name].reshape(N_CHIPS, 2, -1, gm[name].shape[-1]) for name, _, _ in MATS]
    c_idx = jnp.reshape(ci, (1,)).astype(jnp.int32)
    me_c = jnp.stack([me, ci]).astype(jnp.int32)
    from_sib = _rs_to_sibling(glist)
    pairs = [_rs_add_pair(gq, aq, c_idx, name="rs_add_pair_" + name) for gq, aq, (name, _, _) in zip(glist, from_sib, MATS)]
    from_chips = _rs_to_chips(pairs)
    halves = [_rs_add_chips(pq, rq, me_c, name="rs_add_chips_" + name) for pq, rq, (name, _, _) in zip(pairs, from_chips, MATS)]
    reduced = dict(zip([name for name, _, _ in MATS], [rows(t) for t in _rs_join_halves(halves)]))

    grads, delta, new_m, new_v = {}, {}, {}, {}
    for wn in ("a_in_proj", "a_out_proj", "w_kv", "w_q", "w_o", "f_w_in", "f_w_down"):
        gl = [reduced[name] for name, n2, _ in MATS if n2 == wn]
        shp = wl[wn].shape
        three = (len(gl),) + gl[0].shape
        d, mn, vn, go = _adamw(wl[wn].reshape(three), gl, ml[wn].reshape(three), vl[wn].reshape(three), name="adamw_" + wn)
        grads[wn], delta[wn], new_m[wn], new_v[wn] = go.reshape(shp), d.reshape(shp), mn.reshape(shp), vn.reshape(shp)
    pk = lambda d: _pack([d[n] for n in small_names], 8, 128, F32)[None]
    d, mn, vn, _ = _adamw(pk(wl), [pk(g_small)[0]], pk(ml), pk(vl), name="adamw_small")
    shapes = [wl[n].shape for n in small_names]
    for n, dd, mm, vv in zip(small_names, _unpack(d.reshape(-1), shapes), _unpack(mn.reshape(-1), shapes),
                             _unpack(vn.reshape(-1), shapes)):
        grads[n], delta[n], new_m[n], new_v[n] = g_small[n], dd, mm, vv

    return (loss, dx0[None], *[grads[n] for n in WEIGHTS], *[delta[n] for n in WEIGHTS],
            *[new_m[n] for n in WEIGHTS], *[new_v[n] for n in WEIGHTS])
```

```python
import math

import jax
import jax.numpy as jnp
from jax import lax
from jax.experimental import pallas as pl
from jax.experimental.pallas import tpu as pltpu

F32 = jnp.float32
BF16 = jnp.bfloat16

EPS = 1e-5
CHUNK = 256
WINDOW = 128
HEAD = 64
SSM_HEADS = 32
SSM_GROUPS = 8
SSM_STATE = 128
ATT_KV = 4
ATT_G = 4
ROPE_THETA = 10000.0
NEG = -1e30
N_CHIPS = 4
VMEM_LIMIT = 56 * 1024 * 1024

ADAM_LR, ADAM_B1, ADAM_B2, ADAM_EPS, ADAM_WD, ADAM_STEP = 0.001, 0.9, 0.999, 1e-08, 0.01, 10


def _cp(n_axes):
    return pltpu.CompilerParams(dimension_semantics=("arbitrary",) * n_axes, vmem_limit_bytes=VMEM_LIMIT)


def _pick(dim, prefs):
    for p in prefs:
        if dim % p == 0:
            return p
    return dim


def _iota(shape, dim):
    return lax.broadcasted_iota(jnp.int32, shape, dim)


def _dot(a, b, ca=1, cb=0):
    return lax.dot_general(a, b, (((ca,), (cb,)), ((), ())), preferred_element_type=F32)


def _dot3(x, ind):
    h = x.astype(BF16)
    r = x - h.astype(F32)
    m = r.astype(BF16)
    lo = (r - m.astype(F32)).astype(BF16)
    return _dot(h, ind) + _dot(m, ind) + _dot(lo, ind)


def _sigmoid(x):
    return jax.nn.sigmoid(x)


def _mm(a, b, *, name, ta=False, tb=False, bias=None, res=None, out_dtype=F32, b_koff=0, tm=None, tn=None, tk=None,
        dims=None, a_spec=None, b_spec=None, o_spec=None, o_shape=None):
    if dims is not None:
        M, N, K = dims
    else:
        if ta:
            K, M = a.shape
        else:
            M, K = a.shape
        N = b.shape[0] if tb else b.shape[1]
    tm = tm or _pick(M, (1024, 1408, 512, 256, 128))
    tn = tn or _pick(N, (512, 1408, 256, 128))
    tk = tk or (K if K <= 2048 else _pick(K, (2048, 1408, 1024, 512)))
    assert M % tm == 0 and N % tn == 0 and K % tk == 0 and b_koff % tk == 0
    nk = K // tk
    kb0 = b_koff // tk
    has_bias, has_res = bias is not None, res is not None

    def body(*refs):
        a_ref, b_ref = refs[0], refs[1]
        pos = 2
        bias_ref = res_ref = acc_ref = None
        if has_bias:
            bias_ref = refs[pos]
            pos += 1
        if has_res:
            res_ref = refs[pos]
            pos += 1
        o_ref = refs[pos]
        if nk > 1:
            acc_ref = refs[pos + 1]
        part = _dot(a_ref[...].astype(BF16), b_ref[...].astype(BF16), 0 if ta else 1, 1 if tb else 0)

        def finish(acc):
            if has_bias:
                acc = acc + bias_ref[...]
            if has_res:
                acc = acc + res_ref[...]
            o_ref[...] = acc.astype(out_dtype)

        if nk == 1:
            finish(part)
        else:
            k = pl.program_id(2)

            @pl.when(k == 0)
            def _():
                acc_ref[...] = part

            @pl.when(k > 0)
            def _():
                acc_ref[...] += part

            @pl.when(k == nk - 1)
            def _():
                finish(acc_ref[...])

    if a_spec is None:
        a_spec = pl.BlockSpec((tk, tm), lambda i, j, k: (k, i)) if ta else pl.BlockSpec((tm, tk), lambda i, j, k: (i, k))
    if b_spec is None:
        b_spec = (pl.BlockSpec((tn, tk), lambda i, j, k: (j, k + kb0)) if tb
                  else pl.BlockSpec((tk, tn), lambda i, j, k: (k + kb0, j)))
    if o_spec is None:
        o_spec = pl.BlockSpec((tm, tn), lambda i, j, k: (i, j))
    in_specs, args = [a_spec, b_spec], [a, b]
    if has_bias:
        in_specs.append(pl.BlockSpec((1, tn), lambda i, j, k: (0, j)))
        args.append(bias)
    if has_res:
        in_specs.append(pl.BlockSpec((tm, tn), lambda i, j, k: (i, j)))
        args.append(res)
    return pl.pallas_call(
        body, name=name, grid=(M // tm, N // tn, nk), in_specs=in_specs, out_specs=o_spec,
        out_shape=jax.ShapeDtypeStruct(o_shape or (M, N), out_dtype),
        scratch_shapes=[pltpu.VMEM((tm, tn), F32)] if nk > 1 else [],
        compiler_params=_cp(3),
    )(*args)


def _rms_fwd(x, gains, *, name, tr=256):
    S, D = x.shape
    n = len(gains)

    def body(*refs):
        xv = refs[0][...]
        xh = xv * lax.rsqrt(jnp.mean(xv * xv, axis=-1, keepdims=True) + EPS)
        for q in range(n):
            refs[1 + n + q][...] = (xh * refs[1 + q][...]).astype(BF16)

    row = pl.BlockSpec((tr, D), lambda i: (i, 0))
    vec = pl.BlockSpec((1, D), lambda i: (0, 0))
    return pl.pallas_call(
        body, name=name, grid=(S // tr,), in_specs=[row] + [vec] * n, out_specs=[row] * n,
        out_shape=[jax.ShapeDtypeStruct((S, D), BF16)] * n, compiler_params=_cp(1),
    )(x, *gains)


def _rms_bwd(x, gains, dhs, dres, *, name, tr=256, want_colsum=False):
    S, D = x.shape
    n = len(gains)
    steps = S // tr

    def body(*refs):
        x_ref = refs[0]
        g_refs = refs[1:1 + n]
        dh_refs = refs[1 + n:1 + 2 * n]
        dres_ref = refs[1 + 2 * n]
        dx_ref = refs[2 + 2 * n]
        dg_refs = refs[3 + 2 * n:3 + 3 * n]
        cs_ref = refs[3 + 3 * n] if want_colsum else None
        i = pl.program_id(0)
        xv = x_ref[...]
        r = lax.rsqrt(jnp.mean(xv * xv, axis=-1, keepdims=True) + EPS)
        xh = xv * r
        dx = dres_ref[...]
        for q in range(n):
            dh = dh_refs[q][...]
            dxh = dh * g_refs[q][...]
            dx = dx + r * (dxh - xh * jnp.mean(dxh * xh, axis=-1, keepdims=True))
            part = jnp.sum(dh * xh, axis=0, keepdims=True)

            @pl.when(i == 0)
            def _():
                dg_refs[q][...] = part

            @pl.when(i > 0)
            def _():
                dg_refs[q][...] += part

        dx_ref[...] = dx
        if want_colsum:
            cpart = jnp.sum(dx, axis=0, keepdims=True)

            @pl.when(i == 0)
            def _():
                cs_ref[...] = cpart

            @pl.when(i > 0)
            def _():
                cs_ref[...] += cpart

    row = pl.BlockSpec((tr, D), lambda i: (i, 0))
    vec = pl.BlockSpec((1, D), lambda i: (0, 0))
    n_vec_out = n + (1 if want_colsum else 0)
    outs = pl.pallas_call(
        body, name=name, grid=(steps,), in_specs=[row] + [vec] * n + [row] * n + [row],
        out_specs=[row] + [vec] * n_vec_out,
        out_shape=[jax.ShapeDtypeStruct((S, D), F32)] + [jax.ShapeDtypeStruct((1, D), F32)] * n_vec_out,
        compiler_params=_cp(1),
    )(x, *gains, *dhs, dres)
    return outs


def _colsum(x, *, name, tr=256):
    S, D = x.shape

    def body(x_ref, o_ref):
        i = pl.program_id(0)
        part = jnp.sum(x_ref[...].astype(F32), axis=0, keepdims=True)

        @pl.when(i == 0)
        def _():
            o_ref[...] = part

        @pl.when(i > 0)
        def _():
            o_ref[...] += part

    return pl.pallas_call(
        body, name=name, grid=(S // tr,), in_specs=[pl.BlockSpec((tr, D), lambda i: (i, 0))],
        out_specs=pl.BlockSpec((1, D), lambda i: (0, 0)), out_shape=jax.ShapeDtypeStruct((1, D), F32),
        compiler_params=_cp(1),
    )(x)


def _loss(y, t, *, name="loss", tr=256):
    S, D = y.shape
    steps = S // tr

    def body(y_ref, t_ref, dy_ref, l_ref, acc_ref):
        i = pl.program_id(0)
        e = y_ref[...] - t_ref[...]
        dy_ref[...] = e * (1.0 / D)
        part = jnp.sum(e * e, axis=0, keepdims=True)

        @pl.when(i == 0)
        def _():
            acc_ref[...] = part

        @pl.when(i > 0)
        def _():
            acc_ref[...] += part

        @pl.when(i == steps - 1)
        def _():
            tot = jnp.sum(acc_ref[...], axis=1, keepdims=True) * (0.5 / D)
            l_ref[...] = jnp.broadcast_to(tot, (8, 128))

    row = pl.BlockSpec((tr, D), lambda i: (i, 0))
    return pl.pallas_call(
        body, name=name, grid=(steps,), in_specs=[row, row],
        out_specs=[row, pl.BlockSpec((8, 128), lambda i: (0, 0))],
        out_shape=[jax.ShapeDtypeStruct((S, D), F32), jax.ShapeDtypeStruct((8, 128), F32)],
        scratch_shapes=[pltpu.VMEM((1, D), F32)], compiler_params=_cp(1),
    )(y, t)


def _conv_pre(x, w_ref, b_ref, width):
    row = _iota(x.shape, 0)
    acc = b_ref[...] + w_ref[pl.ds(width - 1, 1), :] * x
    shifted = []
    for s in range(1, width):
        xs = jnp.where(row >= s, pltpu.roll(x, s, axis=0), 0.0)
        shifted.append(xs)
        acc = acc + w_ref[pl.ds(width - 1 - s, 1), :] * xs
    return acc, shifted


def _conv_back(dacc, x, shifted, w_ref, width):
    S = x.shape[0]
    row = _iota(x.shape, 0)
    dx = w_ref[pl.ds(width - 1, 1), :] * dacc
    dws = [None] * width
    dws[width - 1] = jnp.sum(dacc * x, axis=0, keepdims=True)
    for s in range(1, width):
        back = jnp.where(row < S - s, pltpu.roll(dacc, S - s, axis=0), 0.0)
        dx = dx + w_ref[pl.ds(width - 1 - s, 1), :] * back
        dws[width - 1 - s] = jnp.sum(dacc * shifted[s - 1], axis=0, keepdims=True)
    db = jnp.sum(dacc, axis=0, keepdims=True)
    return dx, dws, db


def _conv_silu_fwd(xin, col0, C, w, b, *, name, tc=512):
    S = xin.shape[0]
    width = w.shape[0]
    off = col0 // tc

    def body(x_ref, w_ref, b_ref, o_ref):
        acc, _ = _conv_pre(x_ref[...], w_ref, b_ref, width)
        o_ref[...] = acc * _sigmoid(acc)

    return pl.pallas_call(
        body, name=name, grid=(C // tc,),
        in_specs=[pl.BlockSpec((S, tc), lambda j: (0, j + off)), pl.BlockSpec((width, tc), lambda j: (0, j)),
                  pl.BlockSpec((1, tc), lambda j: (0, j))],
        out_specs=pl.BlockSpec((S, tc), lambda j: (0, j)), out_shape=jax.ShapeDtypeStruct((S, C), F32),
        compiler_params=_cp(1),
    )(xin, w, b)


def _conv_silu_bwd(xin, col0, C, w, b, douts, *, name, tc=256):
    S = xin.shape[0]
    width = w.shape[0]
    off = col0 // tc
    nd = len(douts)
    ranges = [(o // tc, (o + d.shape[1]) // tc) for d, o in douts]

    def body(*refs):
        x_ref, w_ref, b_ref = refs[0], refs[1], refs[2]
        d_refs = refs[3:3 + nd]
        dx_ref, dw_ref, db_ref = refs[3 + nd], refs[4 + nd], refs[5 + nd]
        j = pl.program_id(0)
        x = x_ref[...]
        acc, shifted = _conv_pre(x, w_ref, b_ref, width)
        sg = _sigmoid(acc)
        dout = jnp.zeros_like(x)
        for q in range(nd):
            lo, hi = ranges[q]
            dout = dout + jnp.where((j >= lo) & (j < hi), d_refs[q][...], 0.0)
        dacc = dout * (sg * (1.0 + acc * (1.0 - sg)))
        dx, dws, db = _conv_back(dacc, x, shifted, w_ref, width)
        dx_ref[...] = dx.astype(BF16)
        for k in range(width):
            dw_ref[pl.ds(k, 1), :] = dws[k]
        db_ref[...] = db

    d_specs = [pl.BlockSpec((S, tc), (lambda j, lo=lo, hi=hi: (0, jnp.clip(j - lo, 0, hi - lo - 1)))) for lo, hi in ranges]
    return pl.pallas_call(
        body, name=name, grid=(C // tc,),
        in_specs=[pl.BlockSpec((S, tc), lambda j: (0, j + off)), pl.BlockSpec((width, tc), lambda j: (0, j)),
                  pl.BlockSpec((1, tc), lambda j: (0, j))] + d_specs,
        out_specs=[pl.BlockSpec((S, tc), lambda j: (0, j)), pl.BlockSpec((width, tc), lambda j: (0, j)),
                   pl.BlockSpec((1, tc), lambda j: (0, j))],
        out_shape=[jax.ShapeDtypeStruct((S, C), BF16), jax.ShapeDtypeStruct((width, C), F32),
                   jax.ShapeDtypeStruct((1, C), F32)],
        compiler_params=_cp(1),
    )(xin, w, b, *[d for d, _ in douts])


def _ffn_act_fwd(u, w, b, *, name, tc=256):
    S, F2 = u.shape
    Fd = F2 // 2
    width = w.shape[0]
    nb = Fd // tc

    def body(g_ref, v_ref, w_ref, b_ref, o_ref):
        acc, _ = _conv_pre(g_ref[...], w_ref, b_ref, width)
        o_ref[...] = (acc * _sigmoid(acc) * v_ref[...]).astype(BF16)

    return pl.pallas_call(
        body, name=name, grid=(nb,),
        in_specs=[pl.BlockSpec((S, tc), lambda j: (0, j)), pl.BlockSpec((S, tc), lambda j: (0, j + nb)),
                  pl.BlockSpec((width, tc), lambda j: (0, j)), pl.BlockSpec((1, tc), lambda j: (0, j))],
        out_specs=pl.BlockSpec((S, tc), lambda j: (0, j)), out_shape=jax.ShapeDtypeStruct((S, Fd), BF16),
        compiler_params=_cp(1),
    )(u, u, w, b)


def _ffn_act_bwd(u, w, b, da, *, name, tc=256):
    S, F2 = u.shape
    Fd = F2 // 2
    width = w.shape[0]
    nb = Fd // tc

    def body(g_ref, v_ref, w_ref, b_ref, da_ref, du_ref, dw_ref, db_ref, a_ref):
        x = g_ref[...]
        acc, shifted = _conv_pre(x, w_ref, b_ref, width)
        sg = _sigmoid(acc)
        dav = da_ref[...]
        val = v_ref[...]
        silu = acc * sg
        a_ref[...] = (silu * val).astype(BF16)
        du_ref[1] = (dav * silu).astype(BF16)
        dacc = dav * val * (sg * (1.0 + acc * (1.0 - sg)))
        dx, dws, db = _conv_back(dacc, x, shifted, w_ref, width)
        du_ref[0] = dx.astype(BF16)
        for k in range(width):
            dw_ref[pl.ds(k, 1), :] = dws[k]
        db_ref[...] = db

    blk = pl.BlockSpec((S, tc), lambda j: (0, j))
    return pl.pallas_call(
        body, name=name, grid=(nb,),
        in_specs=[blk, pl.BlockSpec((S, tc), lambda j: (0, j + nb)), pl.BlockSpec((width, tc), lambda j: (0, j)),
                  pl.BlockSpec((1, tc), lambda j: (0, j)), blk],
        out_specs=[pl.BlockSpec((2, S, tc), lambda j: (0, 0, j)), pl.BlockSpec((width, tc), lambda j: (0, j)),
                   pl.BlockSpec((1, tc), lambda j: (0, j)), blk],
        out_shape=[jax.ShapeDtypeStruct((2, S, Fd), BF16),
                   jax.ShapeDtypeStruct((width, Fd), F32), jax.ShapeDtypeStruct((1, Fd), F32),
                   jax.ShapeDtypeStruct((S, Fd), BF16)],
        compiler_params=_cp(1),
    )(u, u, w, b, da)


def _ssd_prep(dtr, dt_bias, a_log, *, name="ssd_prep"):
    S = dtr.shape[0]

    def body(d_ref, b_ref, al_ref, dt_ref, ac_ref, sg_ref):
        lane = _iota((CHUNK, 128), 1)
        valid = lane < SSM_HEADS
        z = d_ref[...] + b_ref[...]
        dt = jnp.where(valid, jnp.maximum(z, 0.0) + jnp.log(1.0 + jnp.exp(-jnp.abs(z))), 0.0)
        a = dt * (-jnp.exp(al_ref[...]))
        row = _iota((CHUNK, 128), 0)
        k = 1
        while k < CHUNK:
            a = a + jnp.where(row >= k, pltpu.roll(a, k, axis=0), 0.0)
            k *= 2
        dt_ref[...] = dt
        ac_ref[...] = a
        sg_ref[...] = jnp.where(valid, _sigmoid(z), 0.0)

    blk = pl.BlockSpec((CHUNK, 128), lambda i: (i, 0))
    vec = pl.BlockSpec((1, 128), lambda i: (0, 0))
    return pl.pallas_call(
        body, name=name, grid=(S // CHUNK,), in_specs=[blk, vec, vec], out_specs=[blk, blk, blk],
        out_shape=[jax.ShapeDtypeStruct((S, 128), F32)] * 3, compiler_params=_cp(1),
    )(dtr, dt_bias, a_log)


def _to_groups(v):
    S = v.shape[0]
    g = v[:, :SSM_HEADS].reshape(S, SSM_GROUPS, 4).transpose(1, 0, 2)
    return jnp.pad(g, ((0, 0), (0, 0), (0, 124)))


def _from_groups(vg):
    S = vg.shape[1]
    v = vg[:, :, :4].transpose(1, 0, 2).reshape(S, SSM_HEADS)
    return jnp.pad(v, ((0, 0), (0, 128 - SSM_HEADS)))


def _expand4(v, lanes):
    out = jnp.broadcast_to(v[:, 3:4], lanes.shape)
    for hh in (2, 1, 0):
        out = jnp.where(lanes < 64 * (hh + 1), v[:, hh:hh + 1], out)
    return out


def _ssd_fwd(xbc, dt_g, ac_g, ac_t, *, name="ssd_fwd"):
    S = xbc.shape[0]
    nc = S // CHUNK
    Lc = CHUNK

    def body(x_ref, b_ref, c_ref, dt_ref, ac_ref, act_ref, y_ref, st_out_ref, st_ref):
        g = pl.program_id(0)
        c = pl.program_id(1)

        @pl.when(c == 0)
        def _():
            st_ref[...] = jnp.zeros_like(st_ref)

        bv = b_ref[...]
        cbf = c_ref[...].astype(BF16)
        cb = _dot(cbf, bv.astype(BF16), 1, 1)
        causal = _iota((Lc, Lc), 0) >= _iota((Lc, Lc), 1)
        lane256 = _iota((Lc, 256), 1)
        lane128 = _iota((Lc, 128), 1)
        row128 = _iota((128, 128), 0)
        dtg, acg = dt_ref[...], ac_ref[...]
        ac_last = ac_ref[pl.ds(Lc - 1, 1), :]
        dt4 = _expand4(dtg, lane256)
        ac4 = _expand4(acg, lane256)
        e4 = jnp.exp(ac4)
        xdb = (x_ref[...] * dt4).astype(BF16)
        st_out_ref[...] = st_ref[...]
        for p in range(2):
            xd_p = xdb[:, 128 * p:128 * (p + 1)]
            st_p = st_ref[p]
            ys, sn, cds = [], [], []
            for q in range(2):
                hh = 2 * p + q
                a_col = acg[:, hh:hh + 1]
                a_row = act_ref[pl.ds(4 * g + hh, 1), :]
                dec = jnp.exp(jnp.where(causal, a_col - a_row, NEG))
                w = (cb * dec).astype(BF16)
                ys.append(_dot(w, xd_p))
                al = ac_last[:, hh:hh + 1]
                dte = jnp.exp(al - a_col)
                sn.append(_dot(xd_p, (bv * dte).astype(BF16), 0, 0))
                cds.append(jnp.exp(al))
            y_diag = jnp.where(lane128 < 64, ys[0], ys[1])
            y_off = _dot(cbf, st_p.astype(BF16), 1, 1) * e4[:, 128 * p:128 * (p + 1)]
            y_ref[:, 128 * p:128 * (p + 1)] = y_diag + y_off
            st_ref[p] = jnp.where(row128 < 64, st_p * cds[0] + sn[0], st_p * cds[1] + sn[1])

    per_g = lambda g, c: (g, c, 0)
    return pl.pallas_call(
        body, name=name, grid=(SSM_GROUPS, nc),
        in_specs=[pl.BlockSpec((Lc, 256), lambda g, c: (c, g)),
                  pl.BlockSpec((Lc, 128), lambda g, c: (c, 16 + g)),
                  pl.BlockSpec((Lc, 128), lambda g, c: (c, 24 + g)),
                  pl.BlockSpec((None, Lc, 128), per_g), pl.BlockSpec((None, Lc, 128), per_g),
                  pl.BlockSpec((SSM_HEADS, Lc), lambda g, c: (0, c))],
        out_specs=[pl.BlockSpec((Lc, 256), lambda g, c: (c, g)),
                   pl.BlockSpec((None, None, 2, 128, 128), lambda g, c: (g, c, 0, 0, 0))],
        out_shape=[jax.ShapeDtypeStruct((S, 2048), F32), jax.ShapeDtypeStruct((SSM_GROUPS, nc, 2, 128, 128), F32)],
        scratch_shapes=[pltpu.VMEM((2, 128, 128), F32)], compiler_params=_cp(2),
    )(xbc, xbc, xbc, dt_g, ac_g, ac_t)


def _ssd_bwd(xbc, dt_g, ac_g, ac_t, states, dy, dexp, *, name="ssd_bwd"):
    S = xbc.shape[0]
    nc = S // CHUNK
    Lc = CHUNK

    def body(x_ref, b_ref, c_ref, dt_ref, ac_ref, act_ref, st_ref, dy_ref, d_ref, dx_ref, db_ref, dc_ref, dh_ref, ds_ref):
        g = pl.program_id(0)
        cc = pl.program_id(1)

        @pl.when(cc == 0)
        def _():
            ds_ref[...] = jnp.zeros_like(ds_ref)

        bv = b_ref[...]
        cv = c_ref[...]
        bbf, cbf = bv.astype(BF16), cv.astype(BF16)
        cb = _dot(cbf, bbf, 1, 1)
        causal = _iota((Lc, Lc), 0) >= _iota((Lc, Lc), 1)
        lane256 = _iota((Lc, 256), 1)
        lane128 = _iota((Lc, 128), 1)
        row128 = _iota((128, 128), 0)
        dtg, acg = dt_ref[...], ac_ref[...]
        ac_last = ac_ref[pl.ds(Lc - 1, 1), :]
        dt4 = _expand4(dtg, lane256)
        ac4 = _expand4(acg, lane256)
        acl4 = _expand4(ac_last, _iota((1, 256), 1))
        e4 = jnp.exp(ac4)
        dte4 = jnp.exp(acl4 - ac4)
        xv = x_ref[...]
        xd = xv * dt4
        xdb = xd.astype(BF16)
        dyv = dy_ref[...]
        dcb = jnp.zeros((Lc, Lc), F32)
        dc_acc = jnp.zeros((Lc, 128), F32)
        db_acc = jnp.zeros((Lc, 128), F32)
        ind_rows = _iota((256, 128), 0) >> 6
        ind_cols = _iota((256, 128), 1)
        ind_a = (ind_rows == ind_cols).astype(BF16)
        ind_b = (ind_rows + 4 == ind_cols).astype(BF16)
        u_parts, dxd_parts, ends = [], [], []
        for p in range(2):
            sl = slice(128 * p, 128 * (p + 1))
            xd_p, xdb_p, dy_p = xd[:, sl], xdb[:, sl], dyv[:, sl]
            dyb_p = dy_p.astype(BF16)
            e_p, dte_p = e4[:, sl], dte4[:, sl]
            sp = st_ref[p]
            spb = sp.astype(BF16)
            dsn = ds_ref[p]
            dsnb = dsn.astype(BF16)
            yds, dxds, cds = [], [], []
            for q in range(2):
                hh = 2 * p + q
                a_col = acg[:, hh:hh + 1]
                a_row = act_ref[pl.ds(4 * g + hh, 1), :]
                dec = jnp.exp(jnp.where(causal, a_col - a_row, NEG))
                w = (cb * dec).astype(BF16)
                head = (lane128 < 64) if q == 0 else (lane128 >= 64)
                dym = jnp.where(head, dyb_p, jnp.zeros_like(dyb_p))
                dw = _dot(dym, xdb_p, 1, 1)
                dcb = dcb + dw * dec
                yds.append(_dot(w, xdb_p))
                dxds.append(_dot(w, dyb_p, 0, 0))
                cds.append(jnp.exp(ac_last[:, hh:hh + 1]))
            y_diag = jnp.where(lane128 < 64, yds[0], yds[1])
            dxd_diag = jnp.where(lane128 < 64, dxds[0], dxds[1])
            y_off = _dot(cbf, spb, 1, 1) * e_p
            dgp = dy_p * e_p
            dgb = dgp.astype(BF16)
            dc_acc = dc_acc + _dot(dgb, spb)
            dsp = _dot(dgb, cbf, 0, 0)
            cd_col = jnp.where(row128[:, 0:1] < 64, cds[0], cds[1])
            qm = _dot(bbf, dsnb, 1, 1)
            dxd_state = dte_p * qm
            db_acc = db_acc + _dot((xd_p * dte_p).astype(BF16), dsnb)
            t_p = xd_p * dxd_state
            prod = dsn * sp
            e0 = jnp.sum(jnp.sum(jnp.where(row128 < 64, prod, 0.0), axis=1, keepdims=True), axis=0, keepdims=True)
            e1 = jnp.sum(jnp.sum(jnp.where(row128 >= 64, prod, 0.0), axis=1, keepdims=True), axis=0, keepdims=True)
            tcol = jnp.sum(t_p, axis=0, keepdims=True)
            lane1 = _iota((1, 128), 1)
            t0 = jnp.sum(jnp.where(lane1 < 64, tcol, 0.0), axis=1, keepdims=True)
            t1 = jnp.sum(jnp.where(lane1 >= 64, tcol, 0.0), axis=1, keepdims=True)
            ends.append(e0 * cds[0] + t0)
            ends.append(e1 * cds[1] + t1)
            ds_ref[p] = dsn * cd_col + dsp
            u_parts.append(dyb_p.astype(F32) * y_diag - xdb_p.astype(F32) * dxd_diag + dy_p * y_off - t_p)
            dxd_parts.append(dxd_diag + dxd_state)
        dxd = jnp.concatenate(dxd_parts, axis=1)
        u_all = jnp.concatenate(u_parts, axis=1)
        dx_ref[...] = dxd * dt4 + dyv * d_ref[...]
        dcbb = dcb.astype(BF16)
        dc_ref[...] = dc_acc + _dot(dcbb, bbf)
        db_ref[...] = db_acc + _dot(dcbb, cbf, 0, 0)
        lane = _iota((Lc, 128), 1)
        endv = jnp.zeros((Lc, 128), F32)
        for hh in range(4):
            endv = jnp.where(lane == 8 + hh, ends[hh], endv)
        dh_ref[...] = _dot3(dxd * xv, ind_a) + _dot3(u_all, ind_b) + endv

    rev = lambda c: nc - 1 - c
    per_g = lambda g, c: (g, rev(c), 0)
    return pl.pallas_call(
        body, name=name, grid=(SSM_GROUPS, nc),
        in_specs=[pl.BlockSpec((Lc, 256), lambda g, c: (rev(c), g)),
                  pl.BlockSpec((Lc, 128), lambda g, c: (rev(c), 16 + g)),
                  pl.BlockSpec((Lc, 128), lambda g, c: (rev(c), 24 + g)),
                  pl.BlockSpec((None, Lc, 128), per_g), pl.BlockSpec((None, Lc, 128), per_g),
                  pl.BlockSpec((SSM_HEADS, Lc), lambda g, c: (0, rev(c))),
                  pl.BlockSpec((None, None, 2, 128, 128), lambda g, c: (g, rev(c), 0, 0, 0)),
                  pl.BlockSpec((Lc, 256), lambda g, c: (rev(c), g)),
                  pl.BlockSpec((1, 256), lambda g, c: (0, g))],
        out_specs=[pl.BlockSpec((Lc, 256), lambda g, c: (rev(c), g)),
                   pl.BlockSpec((Lc, 128), lambda g, c: (rev(c), g)),
                   pl.BlockSpec((Lc, 128), lambda g, c: (rev(c), g)),
                   pl.BlockSpec((None, Lc, 128), per_g)],
        out_shape=[jax.ShapeDtypeStruct((S, 2048), F32), jax.ShapeDtypeStruct((S, 1024), F32),
                   jax.ShapeDtypeStruct((S, 1024), F32), jax.ShapeDtypeStruct((SSM_GROUPS, S, 128), F32)],
        scratch_shapes=[pltpu.VMEM((2, 128, 128), F32)], compiler_params=_cp(2),
    )(xbc, xbc, xbc, dt_g, ac_g, ac_t, states, dy, dexp)


def _ssd_post(dhead, dt_g, sg_g, alog_g, *, name="ssd_post"):
    S = dhead.shape[1]
    nc = S // CHUNK
    Lc = CHUNK

    def body(dh_ref, dt_ref, sg_ref, al_ref, o_ref, s_ref):
        c = pl.program_id(1)
        dh = dh_ref[...]
        a_neg = -jnp.exp(al_ref[...])
        lane = _iota((Lc, 128), 1)
        row = _iota((Lc, 128), 0)
        dac = jnp.where(lane < 4, pltpu.roll(dh, 124, axis=1), 0.0)
        end = jnp.where(lane < 4, pltpu.roll(dh, 120, axis=1), 0.0)
        k = 1
        while k < Lc:
            dac = dac + jnp.where(row < Lc - k, pltpu.roll(dac, Lc - k, axis=0), 0.0)
            k *= 2
        da = dac + end
        dtv = dt_ref[...]
        ddt = jnp.where(lane < 4, da * a_neg + dh, 0.0)
        ddtr = ddt * sg_ref[...]
        o_ref[...] = ddtr
        dal = jnp.sum(da * dtv, axis=0, keepdims=True) * a_neg
        dbias = jnp.sum(ddtr, axis=0, keepdims=True)
        row8 = _iota((8, 128), 0)
        part = jnp.where(row8 == 0, dal, jnp.where(row8 == 1, dbias, 0.0))

        @pl.when(c == 0)
        def _():
            s_ref[...] = part

        @pl.when(c > 0)
        def _():
            s_ref[...] += part

    per = pl.BlockSpec((None, Lc, 128), lambda g, c: (g, c, 0))
    return pl.pallas_call(
        body, name=name, grid=(SSM_GROUPS, nc),
        in_specs=[per, per, per, pl.BlockSpec((None, 1, 128), lambda g, c: (g, 0, 0))],
        out_specs=[per, pl.BlockSpec((None, 8, 128), lambda g, c: (g, 0, 0))],
        out_shape=[jax.ShapeDtypeStruct((SSM_GROUPS, S, 128), F32), jax.ShapeDtypeStruct((SSM_GROUPS, 8, 128), F32)],
        compiler_params=_cp(2),
    )(dhead, dt_g, sg_g, alog_g)


def _gate_fwd(y, xbc, zx, dexp, gn, *, name="gate_fwd", tr=256):
    S = y.shape[0]
    W = 2048
    gw = W // SSM_GROUPS

    def body(y_ref, x_ref, z_ref, d_ref, g_ref, o_ref):
        z = z_ref[...]
        u = (y_ref[...] + x_ref[...] * d_ref[...]) * (z * _sigmoid(z))
        gv = g_ref[...]
        for q in range(SSM_GROUPS):
            sl = slice(gw * q, gw * (q + 1))
            uq = u[:, sl]
            r = lax.rsqrt(jnp.mean(uq * uq, axis=-1, keepdims=True) + EPS)
            o_ref[:, sl] = (uq * r * gv[:, sl]).astype(BF16)

    row = pl.BlockSpec((tr, W), lambda i: (i, 0))
    vec = pl.BlockSpec((1, W), lambda i: (0, 0))
    return pl.pallas_call(
        body, name=name, grid=(S // tr,), in_specs=[row, row, row, vec, vec], out_specs=row,
        out_shape=jax.ShapeDtypeStruct((S, W), BF16), compiler_params=_cp(1),
    )(y, xbc, zx, dexp, gn)


def _gate_bwd(y, xbc, zx, dexp, gn, dout, *, name="gate_bwd", tr=256):
    S = y.shape[0]
    W = 2048
    gw = W // SSM_GROUPS
    steps = S // tr

    def body(y_ref, x_ref, z_ref, d_ref, g_ref, do_ref, dy_ref, dz_ref, dg_ref, dd_ref, acc_ref):
        i = pl.program_id(0)

        @pl.when(i == 0)
        def _():
            acc_ref[...] = jnp.zeros_like(acc_ref)

        z = z_ref[...]
        sg = _sigmoid(z)
        sz = z * sg
        xs = x_ref[...]
        yt = y_ref[...] + xs * d_ref[...]
        u = yt * sz
        gv = g_ref[...]
        do = do_ref[...]
        dgs = []
        for q in range(SSM_GROUPS):
            sl = slice(gw * q, gw * (q + 1))
            uq = u[:, sl]
            r = lax.rsqrt(jnp.mean(uq * uq, axis=-1, keepdims=True) + EPS)
            uh = uq * r
            dq = do[:, sl]
            duh = dq * gv[:, sl]
            duq = r * (duh - uh * jnp.mean(duh * uh, axis=-1, keepdims=True))
            dgs.append(jnp.sum(dq * uh, axis=0, keepdims=True))
            dyt = duq * sz[:, sl]
            dy_ref[:, sl] = dyt
            dz_ref[:, sl] = (duq * yt[:, sl] * (sg[:, sl] * (1.0 + z[:, sl] * (1.0 - sg[:, sl])))).astype(BF16)
            acc_ref[:, sl] += jnp.sum(dyt * xs[:, sl], axis=0, keepdims=True)
        dg = jnp.concatenate(dgs, axis=1)

        @pl.when(i == 0)
        def _():
            dg_ref[...] = dg

        @pl.when(i > 0)
        def _():
            dg_ref[...] += dg

        @pl.when(i == steps - 1)
        def _():
            ind = ((_iota((W, 128), 0) >> 6) == _iota((W, 128), 1)).astype(BF16)
            dd_ref[...] = _dot3(jnp.broadcast_to(acc_ref[...], (8, W)), ind)[0:1, :]

    row = pl.BlockSpec((tr, W), lambda i: (i, 0))
    vec = pl.BlockSpec((1, W), lambda i: (0, 0))
    return pl.pallas_call(
        body, name=name, grid=(steps,), in_specs=[row, row, row, vec, vec, row],
        out_specs=[row, row, vec, pl.BlockSpec((1, 128), lambda i: (0, 0))],
        out_shape=[jax.ShapeDtypeStruct((S, W), F32), jax.ShapeDtypeStruct((S, W), BF16),
                   jax.ShapeDtypeStruct((1, W), F32), jax.ShapeDtypeStruct((1, 128), F32)],
        scratch_shapes=[pltpu.VMEM((1, W), F32)], compiler_params=_cp(1),
    )(y, xbc, zx, dexp, gn, dout)


def _rope_cs(posf, *, name="rope_tables", tr=256):
    S = posf.shape[0]

    def body(p_ref, c_ref, s_ref):
        j = (_iota((tr, 128), 1) & 31).astype(F32)
        ang = p_ref[...] * jnp.exp(j * (-math.log(ROPE_THETA) / 32.0))
        c_ref[...] = jnp.cos(ang)
        s_ref[...] = jnp.sin(ang)

    blk = pl.BlockSpec((tr, 128), lambda i: (i, 0))
    return pl.pallas_call(
        body, name=name, grid=(S // tr,), in_specs=[pl.BlockSpec((tr, 1), lambda i: (i, 0))], out_specs=[blk, blk],
        out_shape=[jax.ShapeDtypeStruct((S, 128), F32)] * 2, compiler_params=_cp(1),
    )(posf)


def _rope_tables(c_ref, s_ref, shape):
    reps = shape[1] // 128
    return jnp.tile(c_ref[...], (1, reps)), jnp.tile(s_ref[...], (1, reps)), (_iota(shape, 1) & 63) < 32


def _hn_inds(W):
    ind = ((_iota((W, 128), 0) >> 6) == _iota((W, 128), 1)).astype(BF16)
    ind_t = ((_iota((128, W), 1) >> 6) == _iota((128, W), 0)).astype(BF16)
    return ind, ind_t


def _hnrope_fwd(xin, col0, W, gain_w, rope, *, name, tr=256):
    S = xin.shape[0]
    off = col0 // W

    def body(x_ref, g_ref, c_ref, s_ref, o_ref):
        x = x_ref[...]
        ind, ind_t = _hn_inds(W)
        r = lax.rsqrt(_dot3(x * x, ind) * (1.0 / HEAD) + EPS)
        xn = x * _dot3(r, ind_t) * g_ref[...]
        cs, sn, half = _rope_tables(c_ref, s_ref, (tr, W))
        rot = jnp.where(half, -pltpu.roll(xn, W - 32, axis=1), pltpu.roll(xn, 32, axis=1))
        o_ref[...] = (xn * cs + rot * sn).astype(BF16)

    tab = pl.BlockSpec((tr, 128), lambda i: (i, 0))
    return pl.pallas_call(
        body, name=name, grid=(S // tr,),
        in_specs=[pl.BlockSpec((tr, W), lambda i: (i, off)), pl.BlockSpec((1, W), lambda i: (0, 0)), tab, tab],
        out_specs=pl.BlockSpec((tr, W), lambda i: (i, 0)), out_shape=jax.ShapeDtypeStruct((S, W), BF16),
        compiler_params=_cp(1),
    )(xin, gain_w, *rope)


def _hnrope_bwd(xin, col0, W, gain_w, rope, dout, *, name, tr=256):
    S = xin.shape[0]
    off = col0 // W
    steps = S // tr

    def body(x_ref, g_ref, c_ref, s_ref, do_ref, dx_ref, cs_ref, dg_ref, acc_ref):
        i = pl.program_id(0)
        x = x_ref[...]
        ind, ind_t = _hn_inds(W)
        r = lax.rsqrt(_dot3(x * x, ind) * (1.0 / HEAD) + EPS)
        rw = _dot3(r, ind_t)
        xh = x * rw
        cs, sn, half = _rope_tables(c_ref, s_ref, (tr, W))
        do = do_ref[...].astype(F32)
        gs = do * sn
        g1 = do * cs + jnp.where(half, pltpu.roll(gs, W - 32, axis=1), -pltpu.roll(gs, 32, axis=1))
        dxh = g1 * g_ref[...]
        t = _dot3(dxh * xh, ind) * (1.0 / HEAD)
        dx = rw * (dxh - xh * _dot3(t, ind_t))
        dx_ref[...] = dx.astype(BF16)
        cpart = jnp.sum(dx, axis=0, keepdims=True)
        gpart = jnp.sum(g1 * xh, axis=0, keepdims=True)

        @pl.when(i == 0)
        def _():
            cs_ref[...] = cpart
            acc_ref[...] = gpart

        @pl.when(i > 0)
        def _():
            cs_ref[...] += cpart
            acc_ref[...] += gpart

        @pl.when(i == steps - 1)
        def _():
            fold = ((_iota((W, 128), 0) & 63) == _iota((W, 128), 1)).astype(BF16)
            dg_ref[...] = _dot3(jnp.broadcast_to(acc_ref[...], (8, W)), fold)[0:1, :]

    tab = pl.BlockSpec((tr, 128), lambda i: (i, 0))
    return pl.pallas_call(
        body, name=name, grid=(steps,),
        in_specs=[pl.BlockSpec((tr, W), lambda i: (i, off)), pl.BlockSpec((1, W), lambda i: (0, 0)), tab, tab,
                  pl.BlockSpec((tr, W), lambda i: (i, 0))],
        out_specs=[pl.BlockSpec((tr, W), lambda i: (i, 0)), pl.BlockSpec((1, W), lambda i: (0, 0)),
                   pl.BlockSpec((1, 128), lambda i: (0, 0))],
        out_shape=[jax.ShapeDtypeStruct((S, W), BF16), jax.ShapeDtypeStruct((1, W), F32),
                   jax.ShapeDtypeStruct((1, 128), F32)],
        scratch_shapes=[pltpu.VMEM((1, W), F32)], compiler_params=_cp(1),
    )(xin, gain_w, *rope, dout)


def _attn_probs(q, kb, sink_ref, h, i):
    s = _dot(q, kb, 1, 1) * (HEAD ** -0.5)
    r = _iota((4 * WINDOW, 2 * WINDOW), 0)
    ki = _iota((4 * WINDOW, 2 * WINDOW), 1)
    rel = (r & (WINDOW - 1)) + WINDOW - ki
    mask = (rel >= 0) & (rel < WINDOW) & ((ki >= WINDOW) | (i > 0))
    s = jnp.where(mask, s, NEG)
    r1 = _iota((4 * WINDOW, 1), 0)
    sink = jnp.where(r1 < WINDOW, sink_ref[4 * h], jnp.where(r1 < 2 * WINDOW, sink_ref[4 * h + 1],
                     jnp.where(r1 < 3 * WINDOW, sink_ref[4 * h + 2], sink_ref[4 * h + 3])))
    m = jnp.maximum(jnp.max(s, axis=1, keepdims=True), sink)
    p = jnp.exp(s - m)
    ps = jnp.exp(sink - m)
    inv = 1.0 / (jnp.sum(p, axis=1, keepdims=True) + ps)
    return p * inv, ps * inv


def _attn_specs(S):
    qspec = pl.BlockSpec((None, ATT_G, WINDOW, HEAD), lambda h, i: (h, 0, i, 0))
    cur = pl.BlockSpec((None, WINDOW, HEAD), lambda h, i: (h, i, 0))
    prev = pl.BlockSpec((None, WINDOW, HEAD), lambda h, i: (h, jnp.maximum(i - 1, 0), 0))
    return qspec, cur, prev


def _attn_fwd(qh, kh, vh, sinks, *, name="attn_fwd"):
    S = kh.shape[1]
    nb = S // WINDOW

    def body(s_ref, q_ref, kc_ref, kp_ref, vc_ref, vp_ref, o_ref):
        h, i = pl.program_id(0), pl.program_id(1)
        q = q_ref[...].reshape(ATT_G * WINDOW, HEAD)
        kb = jnp.concatenate([kp_ref[...], kc_ref[...]], axis=0)
        vb = jnp.concatenate([vp_ref[...], vc_ref[...]], axis=0)
        probs, _ = _attn_probs(q, kb, s_ref, h, i)
        o = _dot(probs.astype(BF16), vb)
        o_ref[...] = o.reshape(ATT_G, WINDOW, HEAD).astype(BF16)

    qspec, cur, prev = _attn_specs(S)
    return pl.pallas_call(
        body, name=name, grid=(ATT_KV, nb),
        in_specs=[pl.BlockSpec(memory_space=pltpu.SMEM), qspec, cur, prev, cur, prev], out_specs=qspec,
        out_shape=jax.ShapeDtypeStruct((ATT_KV, ATT_G, S, HEAD), BF16), compiler_params=_cp(2),
    )(sinks, qh, kh, kh, vh, vh)


def _attn_bwd(qh, kh, vh, sinks, doh, *, name="attn_bwd"):
    S = kh.shape[1]
    nb = S // WINDOW

    def body(s_ref, q_ref, kc_ref, kp_ref, vc_ref, vp_ref, do_ref, dq_ref, dk_ref, dv_ref, dsk_ref):
        h, i = pl.program_id(0), pl.program_id(1)

        @pl.when(i == 0)
        def _():
            dk_ref[...] = jnp.zeros_like(dk_ref)
            dv_ref[...] = jnp.zeros_like(dv_ref)
            dsk_ref[...] = jnp.zeros_like(dsk_ref)

        q = q_ref[...].reshape(ATT_G * WINDOW, HEAD)
        do = do_ref[...].reshape(ATT_G * WINDOW, HEAD)
        kb = jnp.concatenate([kp_ref[...], kc_ref[...]], axis=0)
        vb = jnp.concatenate([vp_ref[...], vc_ref[...]], axis=0)
        probs, psink = _attn_probs(q, kb, s_ref, h, i)
        dp = _dot(do, vb, 1, 1)
        delta = jnp.sum(probs * dp, axis=1, keepdims=True)
        ds = (probs * (dp - delta)).astype(BF16)
        scale = HEAD ** -0.5
        dq_ref[...] = (_dot(ds, kb) * scale).reshape(ATT_G, WINDOW, HEAD)
        dkb = _dot(ds, q, 0, 0) * scale
        dvb = _dot(probs.astype(BF16), do, 0, 0)
        cur = pl.multiple_of(i * WINDOW, WINDOW)
        dk_ref[pl.ds(cur, WINDOW), :] += dkb[WINDOW:, :]
        dv_ref[pl.ds(cur, WINDOW), :] += dvb[WINDOW:, :]

        @pl.when(i > 0)
        def _():
            prv = pl.multiple_of((i - 1) * WINDOW, WINDOW)
            dk_ref[pl.ds(prv, WINDOW), :] += dkb[:WINDOW, :]
            dv_ref[pl.ds(prv, WINDOW), :] += dvb[:WINDOW, :]

        dsr = -psink * delta
        lane = _iota((8, 128), 1)
        row = _iota((8, 128), 0)
        upd = jnp.zeros((8, 128), F32)
        for gq in range(ATT_G):
            v = jnp.sum(dsr[gq * WINDOW:(gq + 1) * WINDOW, :], axis=0, keepdims=True)
            upd = jnp.where((lane == gq) & (row == 0), v, upd)
        dsk_ref[...] += upd

    qspec, cur, prev = _attn_specs(S)
    full = pl.BlockSpec((None, S, HEAD), lambda h, i: (h, 0, 0))
    return pl.pallas_call(
        body, name=name, grid=(ATT_KV, nb),
        in_specs=[pl.BlockSpec(memory_space=pltpu.SMEM), qspec, cur, prev, cur, prev, qspec],
        out_specs=[qspec, full, full, pl.BlockSpec((None, 8, 128), lambda h, i: (h, 0, 0))],
        out_shape=[jax.ShapeDtypeStruct((ATT_KV, ATT_G, S, HEAD), F32), jax.ShapeDtypeStruct((ATT_KV, S, HEAD), F32),
                   jax.ShapeDtypeStruct((ATT_KV, S, HEAD), F32), jax.ShapeDtypeStruct((ATT_KV, 8, 128), F32)],
        compiler_params=_cp(2),
    )(sinks, qh, kh, kh, vh, vh, doh)


def _heads_major(t, nh):
    S = t.shape[0]
    return t.reshape(S, nh, HEAD).transpose(1, 0, 2)


def _tokens_major(t):
    nh, S, _ = t.shape
    return t.transpose(1, 0, 2).reshape(S, nh * HEAD)


def _local_step(x, posf, target, w):
    S, D = x.shape
    gr = {}

    (h1,) = _rms_fwd(x, [w["a_norm"]], name="a_norm_f")
    zx = _mm(h1, w["w_zx"], name="in_proj_zx")
    dtr = _mm(h1, w["w_dt"], name="in_proj_dt")
    xbc = _conv_silu_fwd(zx, 2048, 4096, w["a_conv_w"], w["a_conv_b"], name="a_conv_f")
    dt, acum, sgd = _ssd_prep(dtr, w["a_dt_bias"], w["a_A_log"])
    dt_g, ac_g, sg_g = _to_groups(dt), _to_groups(acum), _to_groups(sgd)
    ac_t = acum[:, :SSM_HEADS].T
    y_ssd, states = _ssd_fwd(xbc, dt_g, ac_g, ac_t)
    yg = _gate_fwd(y_ssd, xbc, zx, w["a_Dexp"], w["a_gnorm"])
    x1 = _mm(yg, w["a_out_proj"], res=x, name="out_proj")

    FW = w["f_w_in"][0].shape[2]

    def ffn_fwd(xin, l):
        (h,) = _rms_fwd(xin, [w["f_norm"][l]], name=f"f_norm_f{l}")
        u = _mm(h, w["f_w_in"][l], name=f"f_in{l}", dims=(S, N_CHIPS * FW, D), tn=FW,
                b_spec=pl.BlockSpec((None, D, FW), lambda i, j, k: (j, 0, 0)))
        a = _ffn_act_fwd(u, w["f_conv_w"][l], w["f_conv_b"][l], name=f"f_act_f{l}")
        xo = _mm(a, w["f_w_down"][l], res=xin, name=f"f_down{l}")
        return xo, (h, u)

    x2, ffn0 = ffn_fwd(x1, 0)

    hk, hq = _rms_fwd(x2, [w["kv_norm"], w["b_norm"]], name="kvq_norm_f")
    kv = _mm(hk, w["w_kv"], bias=w["b_kv"], name="kv_proj")
    q = _mm(hq, w["w_q"], bias=w["b_q"], name="q_proj")
    rope = _rope_cs(posf)
    kr = _hnrope_fwd(kv, 0, 256, w["k_norm_w"], rope, name="k_rope_f")
    qr = _hnrope_fwd(q, 0, 1024, w["q_norm_w"], rope, name="q_rope_f")
    qh = _heads_major(qr, 16).reshape(ATT_KV, ATT_G, S, HEAD)
    kh = _heads_major(kr, ATT_KV)
    vh = _heads_major(kv[:, 256:].astype(BF16), ATT_KV)
    att_h = _attn_fwd(qh, kh, vh, w["sinks"])
    att = _tokens_major(att_h.reshape(16, S, HEAD))
    x3 = _mm(att, w["w_o"], bias=w["b_o"], res=x2, name="o_proj")
    x4, ffn1 = ffn_fwd(x3, 1)

    dy, loss_part = _loss(x4, target)

    def ffn_bwd(xin, l, saved, dyo, want_colsum):
        h, u = saved
        da = _mm(dyo, w["f_w_down"][l], tb=True, name=f"f_down_dx{l}")
        du, dcw, dcb, a = _ffn_act_bwd(u, w["f_conv_w"][l], w["f_conv_b"][l], da, name=f"f_act_b{l}")
        dw_down = _mm(a, dyo, ta=True, out_dtype=BF16, name=f"f_down_dw{l}")
        dw_in = _mm(h, du, ta=True, out_dtype=BF16, name=f"f_in_dw{l}", dims=(D, N_CHIPS * FW, S), tm=D, tn=FW, tk=S,
                    b_spec=pl.BlockSpec((None, S, FW), lambda i, j, k: (j // 2, 0, j % 2)),
                    o_spec=pl.BlockSpec((None, D, FW), lambda i, j, k: (j, i, 0)), o_shape=(N_CHIPS, D, FW))
        ts = _pick(S, (1024, 512, 256))
        dh = _mm(du, w["f_w_in"][l], tb=True, name=f"f_in_dx{l}", dims=(S, D, N_CHIPS * FW), tm=ts, tn=512, tk=FW,
                 a_spec=pl.BlockSpec((None, ts, FW), lambda i, j, k: (k // 2, i, k % 2)),
                 b_spec=pl.BlockSpec((None, 512, FW), lambda i, j, k: (k, j, 0)))
        outs = _rms_bwd(xin, [w["f_norm"][l]], [dh], dyo, name=f"f_norm_b{l}", want_colsum=want_colsum)
        g = dict(f_norm=outs[1], f_w_in=dw_in, f_conv_w=dcw, f_conv_b=dcb, f_w_down=dw_down)
        return outs[0], g, (outs[2] if want_colsum else None)

    dx3, gr["ffn1"], db_o = ffn_bwd(x3, 1, ffn1, dy, True)
    gr["b_o"] = db_o
    gr["w_o"] = _mm(att, dx3, ta=True, out_dtype=BF16, name="o_proj_dw")
    datt = _mm(dx3, w["w_o"], tb=True, out_dtype=BF16, name="o_proj_dx")
    doh = _heads_major(datt, 16).reshape(ATT_KV, ATT_G, S, HEAD)
    dqh, dkh, dvh, dsk = _attn_bwd(qh, kh, vh, w["sinks"], doh)
    gr["sinks"] = dsk[:, 0, :4].reshape(1, 16)
    dqr = _tokens_major(dqh.reshape(16, S, HEAD))
    dkr = _tokens_major(dkh)
    dv = _tokens_major(dvh).astype(BF16)
    dq, db_q, dqn = _hnrope_bwd(q, 0, 1024, w["q_norm_w"], rope, dqr, name="q_rope_b")
    dk, db_k, dkn = _hnrope_bwd(kv, 0, 256, w["k_norm_w"], rope, dkr, name="k_rope_b")
    gr["q_norm"], gr["k_norm"] = dqn[:, :HEAD], dkn[:, :HEAD]
    gr["b_q"] = db_q
    gr["b_kv"] = jnp.concatenate([db_k, _colsum(dv, name="dv_colsum")], axis=1)
    dkv = jnp.concatenate([dk, dv], axis=1)
    gr["w_q"] = _mm(hq, dq, ta=True, out_dtype=BF16, name="q_proj_dw")
    gr["w_kv"] = _mm(hk, dkv, ta=True, out_dtype=BF16, name="kv_proj_dw")
    dhq = _mm(dq, w["w_q"], tb=True, name="q_proj_dx")
    dhk = _mm(dkv, w["w_kv"], tb=True, name="kv_proj_dx")
    dx2, gr["kv_norm"], gr["b_norm"] = _rms_bwd(x2, [w["kv_norm"], w["b_norm"]], [dhk, dhq], dx3, name="kvq_norm_b")

    dx1, gr["ffn0"], _ = ffn_bwd(x1, 0, ffn0, dx2, False)

    gr["a_out_proj"] = _mm(yg, dx1, ta=True, out_dtype=BF16, name="out_proj_dw")
    dyg = _mm(dx1, w["a_out_proj"], tb=True, name="out_proj_dx")
    dy_ssd, dz, gr["a_gnorm"], dD = _gate_bwd(y_ssd, xbc, zx, w["a_Dexp"], w["a_gnorm"], dyg)
    gr["a_D"] = dD[:, :SSM_HEADS]
    dxs, dB, dC, dhead = _ssd_bwd(xbc, dt_g, ac_g, ac_t, states, dy_ssd, w["a_Dexp"])
    ddtr_g, dsmall = _ssd_post(dhead, dt_g, sg_g, w["a_A_log_g"])
    gr["a_A_log"] = dsmall[:, 0, :4].reshape(1, SSM_HEADS)
    gr["a_dt_bias"] = dsmall[:, 1, :4].reshape(1, SSM_HEADS)
    ddtr = _from_groups(ddtr_g).astype(BF16)
    dxbc, gr["a_conv_w"], gr["a_conv_b"] = _conv_silu_bwd(
        zx, 2048, 4096, w["a_conv_w"], w["a_conv_b"], [(dxs, 0), (dB, 2048), (dC, 3072)], name="a_conv_b")
    gr["w_z"] = _mm(h1, dz, ta=True, out_dtype=BF16, name="in_proj_dwz")
    gr["w_x"] = _mm(h1, dxbc, ta=True, out_dtype=BF16, name="in_proj_dwx")
    gr["w_dt"] = _mm(h1, ddtr, ta=True, out_dtype=BF16, name="in_proj_dwdt")
    dh1 = _mm(dz, w["w_zx"], tb=True, name="in_proj_dxz")
    dh1 = _mm(dxbc, w["w_zx"], tb=True, b_koff=2048, res=dh1, name="in_proj_dxx")
    dh1 = _mm(ddtr, w["w_dt"], tb=True, res=dh1, name="in_proj_dxdt")
    dx0, gr["a_norm"] = _rms_bwd(x, [w["a_norm"]], [dh1], dx1, name="a_norm_b")
    return loss_part, dx0, gr


def _prep_small(full, w):
    w["a_norm"] = full["a_norm"]
    w["a_conv_w"] = full["a_conv_w"][0]
    w["a_conv_b"] = full["a_conv_b"]
    pad32 = lambda v: jnp.pad(v, ((0, 0), (0, 128 - SSM_HEADS)))
    w["a_dt_bias"] = pad32(full["a_dt_bias"])
    w["a_A_log"] = pad32(full["a_A_log"])
    w["a_A_log_g"] = jnp.pad(full["a_A_log"].reshape(SSM_GROUPS, 1, 4), ((0, 0), (0, 0), (0, 124)))
    w["a_Dexp"] = jnp.repeat(full["a_D"], HEAD, axis=1)
    w["a_gnorm"] = full["a_gnorm"]
    w["f_norm"] = [full["f_norm"][l:l + 1] for l in range(2)]
    w["f_conv_w"] = [full["f_conv_w"][l] for l in range(2)]
    w["f_conv_b"] = [full["f_conv_b"][l:l + 1] for l in range(2)]
    w["kv_norm"] = full["kv_norm"].reshape(1, -1)
    w["b_kv"] = full["b_kv"].reshape(1, -1)
    w["k_norm_w"] = jnp.tile(full["k_norm"].reshape(1, HEAD), (1, ATT_KV))
    w["b_norm"] = full["b_norm"]
    w["b_q"] = full["b_q"]
    w["q_norm_w"] = jnp.tile(full["q_norm"], (1, ATT_KV * ATT_G))
    w["sinks"] = full["sinks"].reshape(-1)
    w["b_o"] = full["b_o"]
    return w


def _split_in_proj(ip):
    return ip[:, :6144].astype(BF16), jnp.pad(ip[:, 6144:], ((0, 0), (0, 128 - SSM_HEADS))).astype(BF16)


def _prep_weights(full):
    w = _prep_small(full, {})
    w["w_zx"], w["w_dt"] = _split_in_proj(full["a_in_proj"][0])
    w["a_out_proj"] = full["a_out_proj"][0].astype(BF16)
    w["f_w_in"] = [full["f_w_in"][l].reshape(1024, N_CHIPS, -1).transpose(1, 0, 2).astype(BF16) for l in range(2)]
    w["f_w_down"] = [full["f_w_down"][l].astype(BF16) for l in range(2)]
    w["w_kv"] = full["w_kv"].astype(BF16)
    w["w_q"] = full["w_q"][0].astype(BF16)
    w["w_o"] = full["w_o"][0].astype(BF16)
    return w


def _small_grads(gr):
    g = {}
    g["a_norm"] = gr["a_norm"]
    g["a_conv_w"] = gr["a_conv_w"][None]
    g["a_conv_b"] = gr["a_conv_b"]
    g["a_dt_bias"], g["a_A_log"], g["a_D"] = gr["a_dt_bias"], gr["a_A_log"], gr["a_D"]
    g["a_gnorm"] = gr["a_gnorm"]
    g["kv_norm"] = gr["kv_norm"].reshape(-1)
    g["b_kv"] = gr["b_kv"].reshape(-1)
    g["k_norm"] = gr["k_norm"].reshape(-1)
    g["b_norm"] = gr["b_norm"]
    g["b_q"] = gr["b_q"]
    g["q_norm"] = gr["q_norm"]
    g["sinks"] = gr["sinks"]
    g["b_o"] = gr["b_o"]
    f = [gr["ffn0"], gr["ffn1"]]
    g["f_norm"] = jnp.concatenate([f[0]["f_norm"], f[1]["f_norm"]], axis=0)
    g["f_conv_w"] = jnp.stack([f[l]["f_conv_w"] for l in range(2)])
    g["f_conv_b"] = jnp.concatenate([f[l]["f_conv_b"] for l in range(2)], axis=0)
    return g


def _in_proj_grad(gr):
    return jnp.concatenate([gr["w_z"], gr["w_x"], gr["w_dt"][:, :SSM_HEADS]], axis=1)


def _full_grads(gr):
    g = _small_grads(gr)
    f32 = lambda t: t.astype(F32)
    g["a_in_proj"] = f32(_in_proj_grad(gr))[None]
    g["a_out_proj"] = f32(gr["a_out_proj"])[None]
    g["w_kv"] = f32(gr["w_kv"])
    g["w_q"] = f32(gr["w_q"])[None]
    g["w_o"] = f32(gr["w_o"])[None]
    f = [gr["ffn0"], gr["ffn1"]]
    g["f_w_in"] = jnp.stack([f32(f[l]["f_w_in"]).transpose(1, 0, 2).reshape(1024, -1) for l in range(2)])
    g["f_w_down"] = jnp.stack([f32(f[l]["f_w_down"]) for l in range(2)])
    return g


MESH = pl.DeviceIdType.MESH
WEIGHTS = ("a_norm", "a_in_proj", "a_conv_w", "a_conv_b", "a_dt_bias", "a_A_log", "a_D", "a_gnorm", "a_out_proj",
           "kv_norm", "w_kv", "b_kv", "k_norm", "b_norm", "w_q", "b_q", "q_norm", "sinks", "w_o", "b_o", "f_norm",
           "f_w_in", "f_conv_w", "f_conv_b", "f_w_down")
MATS = (("in_proj", "a_in_proj", 0), ("out_proj", "a_out_proj", 0), ("w_kv", "w_kv", None), ("w_q", "w_q", 0),
        ("w_o", "w_o", 0), ("f_in0", "f_w_in", 0), ("f_in1", "f_w_in", 1), ("f_down0", "f_w_down", 0),
        ("f_down1", "f_w_down", 1))
SMALL_CUT = (("a_norm", 1), ("a_conv_w", 2), ("a_conv_b", 1), ("a_gnorm", 1), ("f_conv_w", 2))
SMALL_REP = ("a_dt_bias", "a_A_log", "a_D", "kv_norm", "b_kv", "k_norm", "b_norm", "b_q", "q_norm", "sinks", "b_o",
             "f_norm", "f_conv_b")


def _coords():
    return lax.axis_index("x"), lax.axis_index("y"), lax.axis_index("c")


def _other_chips(x, y):
    return [(1 - x, y), (x, 1 - y), (1 - x, 1 - y)]


def _pack(arrs, rows_align, lanes, dtype):
    flat = jnp.concatenate([a.reshape(-1).astype(dtype) for a in arrs])
    per = rows_align * lanes
    total = -(-flat.shape[0] // per) * per
    return jnp.pad(flat, (0, total - flat.shape[0])).reshape(total // lanes, lanes)


def _unpack(flat, shapes):
    out, off = [], 0
    for s in shapes:
        n = math.prod(s)
        out.append(flat[off:off + n].reshape(s))
        off += n
    return out


def _remote(src, dst, send, recv, k, dev):
    return pltpu.make_async_remote_copy(src_ref=src, dst_ref=dst, send_sem=send.at[k], recv_sem=recv.at[k],
                                        device_id=dev, device_id_type=MESH)


_ANY = pl.BlockSpec(memory_space=pl.ANY)


def _halves(t):
    r, c = t.shape
    return t.reshape(2, r // 2, c)


def _gather_weights(shards, sp):
    n = len(shards)
    n_sem = 7 * n + 3

    def body(*refs):
        sh, sp_ref = refs[:n], refs[n]
        outs, sout = refs[n + 1:2 * n + 1], refs[2 * n + 1]
        send, recv, loc = refs[2 * n + 2:]
        x, y, c = _coords()
        me = 2 * x + y
        chips = _other_chips(x, y)
        sib = (x, y, 1 - c)
        l1 = pltpu.make_async_copy(sp_ref, sout.at[me], loc.at[0])
        l1.start()
        sends = []
        for j, (cx, cy) in enumerate(chips):
            sends.append(_remote(sp_ref, sout.at[me], send, recv, 7 * n + j, (cx, cy, c)))
            for t in range(n):
                sends.append(_remote(sh[t].at[c], outs[t].at[me, c], send, recv, 7 * t + j, (cx, cy, c)))
        for t in range(n):
            sends.append(_remote(sh[t], outs[t].at[me], send, recv, 7 * t + 6, sib))
        for cp in sends:
            cp.start()
        for j, (cx, cy) in enumerate(chips):
            src = 2 * cx + cy
            for t in range(n):
                _remote(sh[t].at[c], outs[t].at[src, c], send, recv, 7 * t + j, (cx, cy, c)).wait_recv()
                fwd = _remote(outs[t].at[src, c], outs[t].at[src, c], send, recv, 7 * t + 3 + j, sib)
                fwd.start()
                sends.append(fwd)
        for j, (cx, cy) in enumerate(chips):
            src = 2 * cx + cy
            _remote(sp_ref, sout.at[src], send, recv, 7 * n + j, (cx, cy, c)).wait_recv()
            for t in range(n):
                _remote(outs[t].at[src, 1 - c], outs[t].at[src, 1 - c], send, recv, 7 * t + 3 + j, sib).wait_recv()
        for t in range(n):
            _remote(sh[t], outs[t].at[me], send, recv, 7 * t + 6, sib).wait_recv()
        for cp in sends:
            cp.wait_send()
        l1.wait()

    res = pl.pallas_call(
        body, name="gather_weights", in_specs=[_ANY] * (n + 1), out_specs=[_ANY] * (n + 1),
        out_shape=[jax.ShapeDtypeStruct((N_CHIPS,) + t.shape, t.dtype) for t in shards]
        + [jax.ShapeDtypeStruct((N_CHIPS,) + sp.shape, sp.dtype)],
        scratch_shapes=[pltpu.SemaphoreType.DMA((n_sem,)), pltpu.SemaphoreType.DMA((n_sem,)),
                        pltpu.SemaphoreType.DMA((1,))],
    )(*shards, sp)
    return res[:n], res[n]


def _allreduce_small(v):
    SR = v.shape[0]

    def body(v_ref, o_ref, buf, send, recv):
        x, y, c = _coords()
        me = 4 * x + 2 * y + c
        buf[me] = v_ref[...]
        peers = []
        for k in range(1, 8):
            px = 1 - x if k & 4 else x
            py = 1 - y if k & 2 else y
            pc = 1 - c if k & 1 else c
            peers.append((px, py, pc))
        cps = [_remote(v_ref, buf.at[me], send, recv, k, p) for k, p in enumerate(peers)]
        for cp in cps:
            cp.start()
        for k, (px, py, pc) in enumerate(peers):
            _remote(v_ref, buf.at[4 * px + 2 * py + pc], send, recv, k, (px, py, pc)).wait_recv()
        for cp in cps:
            cp.wait_send()
        acc = buf[0]
        for s in range(1, 8):
            acc = acc + buf[s]
        o_ref[...] = acc

    vm = pl.BlockSpec(memory_space=pltpu.VMEM)
    return pl.pallas_call(
        body, name="allreduce_small", in_specs=[vm], out_specs=vm, out_shape=jax.ShapeDtypeStruct(v.shape, F32),
        scratch_shapes=[pltpu.VMEM((8, SR, 128), F32), pltpu.SemaphoreType.DMA((7,)), pltpu.SemaphoreType.DMA((7,))],
    )(v)


def _rs_to_sibling(gs):
    n = len(gs)

    def body(*refs):
        g, a = refs[:n], refs[n:2 * n]
        send, recv = refs[2 * n:]
        x, y, c = _coords()
        cps = [_remote(g[t].at[:, 1 - c], a[t], send, recv, t, (x, y, 1 - c)) for t in range(n)]
        for cp in cps:
            cp.start()
        for cp in cps:
            cp.wait()

    return pl.pallas_call(
        body, name="rs_to_sibling", in_specs=[_ANY] * n, out_specs=[_ANY] * n,
        out_shape=[jax.ShapeDtypeStruct((N_CHIPS,) + g.shape[2:], g.dtype) for g in gs],
        scratch_shapes=[pltpu.SemaphoreType.DMA((n,)), pltpu.SemaphoreType.DMA((n,))],
    )(*gs)


def _rs_add_pair(g, a, c_idx, *, name):
    _, _, rh, cols = g.shape

    def body(c_ref, g_ref, a_ref, p_ref):
        p_ref[...] = (g_ref[...].astype(F32) + a_ref[...].astype(F32)).astype(BF16)

    return pl.pallas_call(
        body, name=name,
        grid_spec=pltpu.PrefetchScalarGridSpec(
            num_scalar_prefetch=1, grid=(N_CHIPS,),
            in_specs=[pl.BlockSpec((None, None, rh, cols), lambda j, c_ref: (j, c_ref[0], 0, 0)),
                      pl.BlockSpec((None, rh, cols), lambda j, c_ref: (j, 0, 0))],
            out_specs=pl.BlockSpec((None, rh, cols), lambda j, c_ref: (j, 0, 0))),
        out_shape=jax.ShapeDtypeStruct((N_CHIPS, rh, cols), BF16), compiler_params=_cp(1),
    )(c_idx, g, a)


def _rs_to_chips(ps):
    n = len(ps)

    def body(*refs):
        p, r = refs[:n], refs[n:2 * n]
        send, recv = refs[2 * n:]
        x, y, c = _coords()
        cps = [_remote(p[t].at[2 * cx + cy], r[t].at[k], send, recv, 3 * t + k, (cx, cy, c))
               for k, (cx, cy) in enumerate(_other_chips(x, y)) for t in range(n)]
        for cp in cps:
            cp.start()
        for cp in cps:
            cp.wait()

    return pl.pallas_call(
        body, name="rs_to_chips", in_specs=[_ANY] * n, out_specs=[_ANY] * n,
        out_shape=[jax.ShapeDtypeStruct((3,) + p.shape[1:], p.dtype) for p in ps],
        scratch_shapes=[pltpu.SemaphoreType.DMA((3 * n,)), pltpu.SemaphoreType.DMA((3 * n,))],
    )(*ps)


def _rs_add_chips(p, r, idx, *, name):
    _, rh, cols = p.shape

    def body(idx_ref, p_ref, r0_ref, r1_ref, r2_ref, o_ref):
        o_ref[...] = ((p_ref[...].astype(F32) + r0_ref[...].astype(F32)) + r1_ref[...].astype(F32)) + r2_ref[...].astype(F32)

    def rk(k):
        return pl.BlockSpec((None, rh, cols), lambda i, idx_ref, k=k: (k, 0, 0))

    return pl.pallas_call(
        body, name=name,
        grid_spec=pltpu.PrefetchScalarGridSpec(
            num_scalar_prefetch=1, grid=(1,),
            in_specs=[pl.BlockSpec((None, rh, cols), lambda i, idx_ref: (idx_ref[0], 0, 0)), rk(0), rk(1), rk(2)],
            out_specs=pl.BlockSpec((None, rh, cols), lambda i, idx_ref: (idx_ref[1], 0, 0))),
        out_shape=jax.ShapeDtypeStruct((2, rh, cols), F32), compiler_params=_cp(1),
    )(idx, p, r, r, r)


def _rs_join_halves(hs):
    n = len(hs)

    def body(*refs):
        o = refs[n:2 * n]
        send, recv = refs[2 * n:]
        x, y, c = _coords()
        cps = [_remote(o[t].at[c], o[t].at[c], send, recv, t, (x, y, 1 - c)) for t in range(n)]
        for cp in cps:
            cp.start()
        for t in range(n):
            _remote(o[t].at[1 - c], o[t].at[1 - c], send, recv, t, (x, y, 1 - c)).wait_recv()
        for cp in cps:
            cp.wait_send()

    return pl.pallas_call(
        body, name="rs_join_halves", in_specs=[_ANY] * n, out_specs=[_ANY] * n,
        input_output_aliases={t: t for t in range(n)},
        out_shape=[jax.ShapeDtypeStruct(h.shape, F32) for h in hs],
        scratch_shapes=[pltpu.SemaphoreType.DMA((n,)), pltpu.SemaphoreType.DMA((n,))],
    )(*hs)


def _adamw(w, gs, m, v, *, name):
    L, Rr, C = w.shape
    tr = _pick(Rr, (256, 128, 64, 8))
    bc1 = 1.0 - ADAM_B1 ** ADAM_STEP
    bc2 = 1.0 - ADAM_B2 ** ADAM_STEP

    def body(*refs):
        w_ref, m_ref, v_ref = refs[0], refs[1], refs[2]
        g_refs = refs[3:3 + L]
        d_ref, mo_ref, vo_ref, go_ref = refs[3 + L:]
        layer = pl.program_id(0)
        gv = g_refs[0][...]
        for q in range(1, L):
            gv = jnp.where(layer == q, g_refs[q][...], gv)
        mn = ADAM_B1 * m_ref[...] + (1.0 - ADAM_B1) * gv
        vn = ADAM_B2 * v_ref[...] + (1.0 - ADAM_B2) * (gv * gv)
        go_ref[...] = gv
        mo_ref[...] = mn
        vo_ref[...] = vn
        d_ref[...] = -ADAM_LR * ((mn / bc1) / (jnp.sqrt(vn / bc2) + ADAM_EPS) + ADAM_WD * w_ref[...])

    blk = pl.BlockSpec((None, tr, C), lambda l, i: (l, i, 0))
    gblk = pl.BlockSpec((tr, C), lambda l, i: (i, 0))
    return pl.pallas_call(
        body, name=name, grid=(L, Rr // tr), in_specs=[blk] * 3 + [gblk] * L, out_specs=[blk] * 4,
        out_shape=[jax.ShapeDtypeStruct((L, Rr, C), F32)] * 4, compiler_params=_cp(2),
    )(w, m, v, *gs)


def kernel(x, positions, a_norm, a_in_proj, a_conv_w, a_conv_b, a_dt_bias, a_A_log, a_D, a_gnorm, a_out_proj,
           kv_norm, w_kv, b_kv, k_norm, b_norm, w_q, b_q, q_norm, sinks, w_o, b_o, f_norm, f_w_in, f_conv_w,
           f_conv_b, f_w_down, loss_target, m_a_norm, m_a_in_proj, m_a_conv_w, m_a_conv_b, m_a_dt_bias, m_a_A_log,
           m_a_D, m_a_gnorm, m_a_out_proj, m_kv_norm, m_w_kv, m_b_kv, m_k_norm, m_b_norm, m_w_q, m_b_q, m_q_norm,
           m_sinks, m_w_o, m_b_o, m_f_norm, m_f_w_in, m_f_conv_w, m_f_conv_b, m_f_w_down, v_a_norm, v_a_in_proj,
           v_a_conv_w, v_a_conv_b, v_a_dt_bias, v_a_A_log, v_a_D, v_a_gnorm, v_a_out_proj, v_kv_norm, v_w_kv,
           v_b_kv, v_k_norm, v_b_norm, v_w_q, v_b_q, v_q_norm, v_sinks, v_w_o, v_b_o, v_f_norm, v_f_w_in,
           v_f_conv_w, v_f_conv_b, v_f_w_down):
    wl = dict(zip(WEIGHTS, (a_norm, a_in_proj, a_conv_w, a_conv_b, a_dt_bias, a_A_log, a_D, a_gnorm, a_out_proj,
                            kv_norm, w_kv, b_kv, k_norm, b_norm, w_q, b_q, q_norm, sinks, w_o, b_o, f_norm, f_w_in,
                            f_conv_w, f_conv_b, f_w_down)))
    ml = dict(zip(WEIGHTS, (m_a_norm, m_a_in_proj, m_a_conv_w, m_a_conv_b, m_a_dt_bias, m_a_A_log, m_a_D, m_a_gnorm,
                            m_a_out_proj, m_kv_norm, m_w_kv, m_b_kv, m_k_norm, m_b_norm, m_w_q, m_b_q, m_q_norm,
                            m_sinks, m_w_o, m_b_o, m_f_norm, m_f_w_in, m_f_conv_w, m_f_conv_b, m_f_w_down)))
    vl = dict(zip(WEIGHTS, (v_a_norm, v_a_in_proj, v_a_conv_w, v_a_conv_b, v_a_dt_bias, v_a_A_log, v_a_D, v_a_gnorm,
                            v_a_out_proj, v_kv_norm, v_w_kv, v_b_kv, v_k_norm, v_b_norm, v_w_q, v_b_q, v_q_norm,
                            v_sinks, v_w_o, v_b_o, v_f_norm, v_f_w_in, v_f_conv_w, v_f_conv_b, v_f_w_down)))
    xi, yi, ci = _coords()
    me = 2 * xi + yi
    S = x.shape[1]

    def block_of(n, layer):
        t = wl[n]
        return t if layer is None else t[layer]

    shards = [_halves(block_of(wn, layer).astype(BF16)) for _, wn, layer in MATS]
    sp = _pack([wl[n] for n, _ in SMALL_CUT], 8, 128, F32)
    gathered, gs = _gather_weights(shards, sp)
    gt = {name: t.reshape(N_CHIPS, -1, t.shape[-1]) for (name, _, _), t in zip(MATS, gathered)}
    full = {n: wl[n] for n in SMALL_REP}
    gs = gs.reshape(N_CHIPS, -1)
    pieces = [_unpack(gs[j], [wl[n].shape for n, _ in SMALL_CUT]) for j in range(N_CHIPS)]
    for q, (n, ax) in enumerate(SMALL_CUT):
        full[n] = jnp.concatenate([pieces[j][q] for j in range(N_CHIPS)], axis=ax)
    w = _prep_small(full, {})
    rows = lambda t: t.reshape(-1, t.shape[-1])
    w["w_zx"], w["w_dt"] = _split_in_proj(gt["in_proj"].transpose(1, 0, 2).reshape(1024, -1))
    w["a_out_proj"], w["w_kv"], w["w_q"], w["w_o"] = (rows(gt[k]) for k in ("out_proj", "w_kv", "w_q", "w_o"))
    w["f_w_in"] = [gt["f_in0"], gt["f_in1"]]
    w["f_w_down"] = [rows(gt["f_down0"]), rows(gt["f_down1"])]

    posf = positions.reshape(S, 1).astype(F32)
    loss_part, dx0, gr = _local_step(x[0], posf, loss_target[0], w)
    g = _small_grads(gr)

    small_names = [n for n, _ in SMALL_CUT] + list(SMALL_REP)
    sv = _pack([g[n] for n in small_names] + [loss_part[0:1, 0:1]], 8, 128, F32)
    sred = _allreduce_small(sv).reshape(-1)
    small_shapes = [g[n].shape for n in small_names] + [(1,)]
    sg = dict(zip(small_names + ["loss"], _unpack(sred, small_shapes)))
    loss = sg["loss"].reshape(())
    g_small = {}
    for n, ax in SMALL_CUT:
        size = wl[n].shape[ax]
        g_small[n] = lax.dynamic_slice_in_dim(sg[n], me * size, size, axis=ax)
    for n in SMALL_REP:
        g_small[n] = sg[n].reshape(wl[n].shape)

    gm = {"in_proj": _in_proj_grad(gr).reshape(1024, N_CHIPS, -1).transpose(1, 0, 2),
          "out_proj": gr["a_out_proj"], "w_kv": gr["w_kv"], "w_q": gr["w_q"], "w_o": gr["w_o"],
          "f_in0": gr["ffn0"]["f_w_in"], "f_in1": gr["ffn1"]["f_w_in"],
          "f_down0": gr["ffn0"]["f_w_down"], "f_down1": gr["ffn1"]["f_w_down"]}
    glist = [gm[name].reshape(N_CHIPS, 2, -1, gm[name].shape[-1]) for name, _, _ in MATS]
    c_idx = jnp.reshape(ci, (1,)).astype(jnp.int32)
    me_c = jnp.stack([me, ci]).astype(jnp.int32)
    from_sib = _rs_to_sibling(glist)
    pairs = [_rs_add_pair(gq, aq, c_idx, name="rs_add_pair_" + name) for gq, aq, (name, _, _) in zip(glist, from_sib, MATS)]
    from_chips = _rs_to_chips(pairs)
    halves = [_rs_add_chips(pq, rq, me_c, name="rs_add_chips_" + name) for pq, rq, (name, _, _) in zip(pairs, from_chips, MATS)]
    reduced = dict(zip([name for name, _, _ in MATS], [rows(t) for t in _rs_join_halves(halves)]))

    grads, delta, new_m, new_v = {}, {}, {}, {}
    for wn in ("a_in_proj", "a_out_proj", "w_kv", "w_q", "w_o", "f_w_in", "f_w_down"):
        gl = [reduced[name] for name, n2, _ in MATS if n2 == wn]
        shp = wl[wn].shape
        three = (len(gl),) + gl[0].shape
        d, mn, vn, go = _adamw(wl[wn].reshape(three), gl, ml[wn].reshape(three), vl[wn].reshape(three), name="adamw_" + wn)
        grads[wn], delta[wn], new_m[wn], new_v[wn] = go.reshape(shp), d.reshape(shp), mn.reshape(shp), vn.reshape(shp)
    pk = lambda d: _pack([d[n] for n in small_names], 8, 128, F32)[None]
    d, mn, vn, _ = _adamw(pk(wl), [pk(g_small)[0]], pk(ml), pk(vl), name="adamw_small")
    shapes = [wl[n].shape for n in small_names]
    for n, dd, mm, vv in zip(small_names, _unpack(d.reshape(-1), shapes), _unpack(mn.reshape(-1), shapes),
                             _unpack(vn.reshape(-1), shapes)):
        grads[n], delta[n], new_m[n], new_v[n] = g_small[n], dd, mm, vv

    return (loss, dx0[None], *[grads[n] for n in WEIGHTS], *[delta[n] for n in WEIGHTS],
            *[new_m[n] for n in WEIGHTS], *[new_v[n] for n in WEIGHTS])
```

```python
import math

import jax
import jax.numpy as jnp
from jax import lax
from jax.experimental import pallas as pl
from jax.experimental.pallas import tpu as pltpu

F32 = jnp.float32
BF16 = jnp.bfloat16

EPS = 1e-5
CHUNK = 256
WINDOW = 128
HEAD = 64
SSM_HEADS = 32
SSM_GROUPS = 8
SSM_STATE = 128
ATT_KV = 4
ATT_G = 4
ROPE_THETA = 10000.0
NEG = -1e30
N_CHIPS = 4
VMEM_LIMIT = 56 * 1024 * 1024

ADAM_LR, ADAM_B1, ADAM_B2, ADAM_EPS, ADAM_WD, ADAM_STEP = 0.001, 0.9, 0.999, 1e-08, 0.01, 10


def _cp(n_axes):
    return pltpu.CompilerParams(dimension_semantics=("arbitrary",) * n_axes, vmem_limit_bytes=VMEM_LIMIT)


def _pick(dim, prefs):
    for p in prefs:
        if dim % p == 0:
            return p
    return dim


def _iota(shape, dim):
    return lax.broadcasted_iota(jnp.int32, shape, dim)


def _dot(a, b, ca=1, cb=0):
    return lax.dot_general(a, b, (((ca,), (cb,)), ((), ())), preferred_element_type=F32)


def _dot3(x, ind):
    h = x.astype(BF16)
    r = x - h.astype(F32)
    m = r.astype(BF16)
    lo = (r - m.astype(F32)).astype(BF16)
    return _dot(h, ind) + _dot(m, ind) + _dot(lo, ind)


def _sigmoid(x):
    return jax.nn.sigmoid(x)


def _mm(a, b, *, name, ta=False, tb=False, bias=None, res=None, out_dtype=F32, b_koff=0, tm=None, tn=None, tk=None,
        dims=None, a_spec=None, b_spec=None, o_spec=None, o_shape=None, dep=None):
    if dims is not None:
        M, N, K = dims
    else:
        if ta:
            K, M = a.shape
        else:
            M, K = a.shape
        N = b.shape[0] if tb else b.shape[1]
    tm = tm or _pick(M, (1024, 1408, 512, 256, 128))
    tn = tn or _pick(N, (512, 1408, 256, 128))
    tk = tk or (K if K <= 2048 else _pick(K, (2048, 1408, 1024, 512)))
    assert M % tm == 0 and N % tn == 0 and K % tk == 0 and b_koff % tk == 0
    nk = K // tk
    kb0 = b_koff // tk
    has_bias, has_res = bias is not None, res is not None

    def body(*refs):
        a_ref, b_ref = refs[0], refs[1]
        pos = 2
        bias_ref = res_ref = acc_ref = None
        if has_bias:
            bias_ref = refs[pos]
            pos += 1
        if has_res:
            res_ref = refs[pos]
            pos += 1
        if dep is not None:
            pos += 1
        o_ref = refs[pos]
        if nk > 1:
            acc_ref = refs[pos + 1]
        part = _dot(a_ref[...].astype(BF16), b_ref[...].astype(BF16), 0 if ta else 1, 1 if tb else 0)

        def finish(acc):
            if has_bias:
                acc = acc + bias_ref[...]
            if has_res:
                acc = acc + res_ref[...]
            o_ref[...] = acc.astype(out_dtype)

        if nk == 1:
            finish(part)
        else:
            k = pl.program_id(2)

            @pl.when(k == 0)
            def _():
                acc_ref[...] = part

            @pl.when(k > 0)
            def _():
                acc_ref[...] += part

            @pl.when(k == nk - 1)
            def _():
                finish(acc_ref[...])

    if a_spec is None:
        a_spec = pl.BlockSpec((tk, tm), lambda i, j, k: (k, i)) if ta else pl.BlockSpec((tm, tk), lambda i, j, k: (i, k))
    if b_spec is None:
        b_spec = (pl.BlockSpec((tn, tk), lambda i, j, k: (j, k + kb0)) if tb
                  else pl.BlockSpec((tk, tn), lambda i, j, k: (k + kb0, j)))
    if o_spec is None:
        o_spec = pl.BlockSpec((tm, tn), lambda i, j, k: (i, j))
    in_specs, args = [a_spec, b_spec], [a, b]
    if has_bias:
        in_specs.append(pl.BlockSpec((1, tn), lambda i, j, k: (0, j)))
        args.append(bias)
    if has_res:
        in_specs.append(pl.BlockSpec((tm, tn), lambda i, j, k: (i, j)))
        args.append(res)
    if dep is not None:
        in_specs.append(pl.BlockSpec(memory_space=pl.ANY))
        args.append(dep)
    return pl.pallas_call(
        body, name=name, grid=(M // tm, N // tn, nk), in_specs=in_specs, out_specs=o_spec,
        out_shape=jax.ShapeDtypeStruct(o_shape or (M, N), out_dtype),
        scratch_shapes=[pltpu.VMEM((tm, tn), F32)] if nk > 1 else [],
        compiler_params=_cp(3),
    )(*args)


def _rms_fwd(x, gains, *, name, tr=256, dep=None):
    S, D = x.shape
    n = len(gains)
    nd = 0 if dep is None else 1

    def body(*refs):
        xv = refs[0][...]
        xh = xv * lax.rsqrt(jnp.mean(xv * xv, axis=-1, keepdims=True) + EPS)
        for q in range(n):
            refs[1 + n + nd + q][...] = (xh * refs[1 + q][...]).astype(BF16)

    row = pl.BlockSpec((tr, D), lambda i: (i, 0))
    vec = pl.BlockSpec((1, D), lambda i: (0, 0))
    return pl.pallas_call(
        body, name=name, grid=(S // tr,), in_specs=[row] + [vec] * n + [pl.BlockSpec(memory_space=pl.ANY)] * nd,
        out_specs=[row] * n, out_shape=[jax.ShapeDtypeStruct((S, D), BF16)] * n, compiler_params=_cp(1),
    )(x, *gains, *([] if dep is None else [dep]))


def _rms_bwd(x, gains, dhs, dres, *, name, tr=256, want_colsum=False):
    S, D = x.shape
    n = len(gains)
    steps = S // tr

    def body(*refs):
        x_ref = refs[0]
        g_refs = refs[1:1 + n]
        dh_refs = refs[1 + n:1 + 2 * n]
        dres_ref = refs[1 + 2 * n]
        dx_ref = refs[2 + 2 * n]
        dg_refs = refs[3 + 2 * n:3 + 3 * n]
        cs_ref = refs[3 + 3 * n] if want_colsum else None
        i = pl.program_id(0)
        xv = x_ref[...]
        r = lax.rsqrt(jnp.mean(xv * xv, axis=-1, keepdims=True) + EPS)
        xh = xv * r
        dx = dres_ref[...]
        for q in range(n):
            dh = dh_refs[q][...]
            dxh = dh * g_refs[q][...]
            dx = dx + r * (dxh - xh * jnp.mean(dxh * xh, axis=-1, keepdims=True))
            part = jnp.sum(dh * xh, axis=0, keepdims=True)

            @pl.when(i == 0)
            def _():
                dg_refs[q][...] = part

            @pl.when(i > 0)
            def _():
                dg_refs[q][...] += part

        dx_ref[...] = dx
        if want_colsum:
            cpart = jnp.sum(dx, axis=0, keepdims=True)

            @pl.when(i == 0)
            def _():
                cs_ref[...] = cpart

            @pl.when(i > 0)
            def _():
                cs_ref[...] += cpart

    row = pl.BlockSpec((tr, D), lambda i: (i, 0))
    vec = pl.BlockSpec((1, D), lambda i: (0, 0))
    n_vec_out = n + (1 if want_colsum else 0)
    outs = pl.pallas_call(
        body, name=name, grid=(steps,), in_specs=[row] + [vec] * n + [row] * n + [row],
        out_specs=[row] + [vec] * n_vec_out,
        out_shape=[jax.ShapeDtypeStruct((S, D), F32)] + [jax.ShapeDtypeStruct((1, D), F32)] * n_vec_out,
        compiler_params=_cp(1),
    )(x, *gains, *dhs, dres)
    return outs


def _colsum(x, *, name, tr=256):
    S, D = x.shape

    def body(x_ref, o_ref):
        i = pl.program_id(0)
        part = jnp.sum(x_ref[...].astype(F32), axis=0, keepdims=True)

        @pl.when(i == 0)
        def _():
            o_ref[...] = part

        @pl.when(i > 0)
        def _():
            o_ref[...] += part

    return pl.pallas_call(
        body, name=name, grid=(S // tr,), in_specs=[pl.BlockSpec((tr, D), lambda i: (i, 0))],
        out_specs=pl.BlockSpec((1, D), lambda i: (0, 0)), out_shape=jax.ShapeDtypeStruct((1, D), F32),
        compiler_params=_cp(1),
    )(x)


def _loss(y, t, *, name="loss", tr=256):
    S, D = y.shape
    steps = S // tr

    def body(y_ref, t_ref, dy_ref, l_ref, acc_ref):
        i = pl.program_id(0)
        e = y_ref[...] - t_ref[...]
        dy_ref[...] = e * (1.0 / D)
        part = jnp.sum(e * e, axis=0, keepdims=True)

        @pl.when(i == 0)
        def _():
            acc_ref[...] = part

        @pl.when(i > 0)
        def _():
            acc_ref[...] += part

        @pl.when(i == steps - 1)
        def _():
            tot = jnp.sum(acc_ref[...], axis=1, keepdims=True) * (0.5 / D)
            l_ref[...] = jnp.broadcast_to(tot, (8, 128))

    row = pl.BlockSpec((tr, D), lambda i: (i, 0))
    return pl.pallas_call(
        body, name=name, grid=(steps,), in_specs=[row, row],
        out_specs=[row, pl.BlockSpec((8, 128), lambda i: (0, 0))],
        out_shape=[jax.ShapeDtypeStruct((S, D), F32), jax.ShapeDtypeStruct((8, 128), F32)],
        scratch_shapes=[pltpu.VMEM((1, D), F32)], compiler_params=_cp(1),
    )(y, t)


def _conv_pre(x, w_ref, b_ref, width):
    row = _iota(x.shape, 0)
    acc = b_ref[...] + w_ref[pl.ds(width - 1, 1), :] * x
    shifted = []
    for s in range(1, width):
        xs = jnp.where(row >= s, pltpu.roll(x, s, axis=0), 0.0)
        shifted.append(xs)
        acc = acc + w_ref[pl.ds(width - 1 - s, 1), :] * xs
    return acc, shifted


def _conv_back(dacc, x, shifted, w_ref, width):
    S = x.shape[0]
    row = _iota(x.shape, 0)
    dx = w_ref[pl.ds(width - 1, 1), :] * dacc
    dws = [None] * width
    dws[width - 1] = jnp.sum(dacc * x, axis=0, keepdims=True)
    for s in range(1, width):
        back = jnp.where(row < S - s, pltpu.roll(dacc, S - s, axis=0), 0.0)
        dx = dx + w_ref[pl.ds(width - 1 - s, 1), :] * back
        dws[width - 1 - s] = jnp.sum(dacc * shifted[s - 1], axis=0, keepdims=True)
    db = jnp.sum(dacc, axis=0, keepdims=True)
    return dx, dws, db


def _conv_silu_fwd(xin, col0, C, w, b, *, name, tc=512):
    S = xin.shape[0]
    width = w.shape[0]
    off = col0 // tc

    def body(x_ref, w_ref, b_ref, o_ref):
        acc, _ = _conv_pre(x_ref[...], w_ref, b_ref, width)
        o_ref[...] = acc * _sigmoid(acc)

    return pl.pallas_call(
        body, name=name, grid=(C // tc,),
        in_specs=[pl.BlockSpec((S, tc), lambda j: (0, j + off)), pl.BlockSpec((width, tc), lambda j: (0, j)),
                  pl.BlockSpec((1, tc), lambda j: (0, j))],
        out_specs=pl.BlockSpec((S, tc), lambda j: (0, j)), out_shape=jax.ShapeDtypeStruct((S, C), F32),
        compiler_params=_cp(1),
    )(xin, w, b)


def _conv_silu_bwd(xin, col0, C, w, b, douts, *, name, tc=256):
    S = xin.shape[0]
    width = w.shape[0]
    off = col0 // tc
    nd = len(douts)
    ranges = [(o // tc, (o + d.shape[1]) // tc) for d, o in douts]

    def body(*refs):
        x_ref, w_ref, b_ref = refs[0], refs[1], refs[2]
        d_refs = refs[3:3 + nd]
        dx_ref, dw_ref, db_ref = refs[3 + nd], refs[4 + nd], refs[5 + nd]
        j = pl.program_id(0)
        x = x_ref[...]
        acc, shifted = _conv_pre(x, w_ref, b_ref, width)
        sg = _sigmoid(acc)
        dout = jnp.zeros_like(x)
        for q in range(nd):
            lo, hi = ranges[q]
            dout = dout + jnp.where((j >= lo) & (j < hi), d_refs[q][...], 0.0)
        dacc = dout * (sg * (1.0 + acc * (1.0 - sg)))
        dx, dws, db = _conv_back(dacc, x, shifted, w_ref, width)
        dx_ref[...] = dx.astype(BF16)
        for k in range(width):
            dw_ref[pl.ds(k, 1), :] = dws[k]
        db_ref[...] = db

    d_specs = [pl.BlockSpec((S, tc), (lambda j, lo=lo, hi=hi: (0, jnp.clip(j - lo, 0, hi - lo - 1)))) for lo, hi in ranges]
    return pl.pallas_call(
        body, name=name, grid=(C // tc,),
        in_specs=[pl.BlockSpec((S, tc), lambda j: (0, j + off)), pl.BlockSpec((width, tc), lambda j: (0, j)),
                  pl.BlockSpec((1, tc), lambda j: (0, j))] + d_specs,
        out_specs=[pl.BlockSpec((S, tc), lambda j: (0, j)), pl.BlockSpec((width, tc), lambda j: (0, j)),
                   pl.BlockSpec((1, tc), lambda j: (0, j))],
        out_shape=[jax.ShapeDtypeStruct((S, C), BF16), jax.ShapeDtypeStruct((width, C), F32),
                   jax.ShapeDtypeStruct((1, C), F32)],
        compiler_params=_cp(1),
    )(xin, w, b, *[d for d, _ in douts])


def _ffn_act_fwd(u, w, b, *, name, tc=256):
    S, F2 = u.shape
    Fd = F2 // 2
    width = w.shape[0]
    nb = Fd // tc

    def body(g_ref, v_ref, w_ref, b_ref, o_ref):
        acc, _ = _conv_pre(g_ref[...], w_ref, b_ref, width)
        o_ref[...] = (acc * _sigmoid(acc) * v_ref[...]).astype(BF16)

    return pl.pallas_call(
        body, name=name, grid=(nb,),
        in_specs=[pl.BlockSpec((S, tc), lambda j: (0, j)), pl.BlockSpec((S, tc), lambda j: (0, j + nb)),
                  pl.BlockSpec((width, tc), lambda j: (0, j)), pl.BlockSpec((1, tc), lambda j: (0, j))],
        out_specs=pl.BlockSpec((S, tc), lambda j: (0, j)), out_shape=jax.ShapeDtypeStruct((S, Fd), BF16),
        compiler_params=_cp(1),
    )(u, u, w, b)


def _ffn_act_bwd(u, w, b, da, *, name, tc=256):
    S, F2 = u.shape
    Fd = F2 // 2
    width = w.shape[0]
    nb = Fd // tc

    def body(g_ref, v_ref, w_ref, b_ref, da_ref, du_ref, dw_ref, db_ref, a_ref):
        x = g_ref[...]
        acc, shifted = _conv_pre(x, w_ref, b_ref, width)
        sg = _sigmoid(acc)
        dav = da_ref[...]
        val = v_ref[...]
        silu = acc * sg
        a_ref[...] = (silu * val).astype(BF16)
        du_ref[1] = (dav * silu).astype(BF16)
        dacc = dav * val * (sg * (1.0 + acc * (1.0 - sg)))
        dx, dws, db = _conv_back(dacc, x, shifted, w_ref, width)
        du_ref[0] = dx.astype(BF16)
        for k in range(width):
            dw_ref[pl.ds(k, 1), :] = dws[k]
        db_ref[...] = db

    blk = pl.BlockSpec((S, tc), lambda j: (0, j))
    return pl.pallas_call(
        body, name=name, grid=(nb,),
        in_specs=[blk, pl.BlockSpec((S, tc), lambda j: (0, j + nb)), pl.BlockSpec((width, tc), lambda j: (0, j)),
                  pl.BlockSpec((1, tc), lambda j: (0, j)), blk],
        out_specs=[pl.BlockSpec((2, S, tc), lambda j: (0, 0, j)), pl.BlockSpec((width, tc), lambda j: (0, j)),
                   pl.BlockSpec((1, tc), lambda j: (0, j)), blk],
        out_shape=[jax.ShapeDtypeStruct((2, S, Fd), BF16),
                   jax.ShapeDtypeStruct((width, Fd), F32), jax.ShapeDtypeStruct((1, Fd), F32),
                   jax.ShapeDtypeStruct((S, Fd), BF16)],
        compiler_params=_cp(1),
    )(u, u, w, b, da)


def _ssd_prep(dtr, dt_bias, a_log, *, name="ssd_prep"):
    S = dtr.shape[0]

    def body(d_ref, b_ref, al_ref, dt_ref, ac_ref, sg_ref):
        lane = _iota((CHUNK, 128), 1)
        valid = lane < SSM_HEADS
        z = d_ref[...] + b_ref[...]
        dt = jnp.where(valid, jnp.maximum(z, 0.0) + jnp.log(1.0 + jnp.exp(-jnp.abs(z))), 0.0)
        a = dt * (-jnp.exp(al_ref[...]))
        row = _iota((CHUNK, 128), 0)
        k = 1
        while k < CHUNK:
            a = a + jnp.where(row >= k, pltpu.roll(a, k, axis=0), 0.0)
            k *= 2
        dt_ref[...] = dt
        ac_ref[...] = a
        sg_ref[...] = jnp.where(valid, _sigmoid(z), 0.0)

    blk = pl.BlockSpec((CHUNK, 128), lambda i: (i, 0))
    vec = pl.BlockSpec((1, 128), lambda i: (0, 0))
    return pl.pallas_call(
        body, name=name, grid=(S // CHUNK,), in_specs=[blk, vec, vec], out_specs=[blk, blk, blk],
        out_shape=[jax.ShapeDtypeStruct((S, 128), F32)] * 3, compiler_params=_cp(1),
    )(dtr, dt_bias, a_log)


def _to_groups(v):
    S = v.shape[0]
    g = v[:, :SSM_HEADS].reshape(S, SSM_GROUPS, 4).transpose(1, 0, 2)
    return jnp.pad(g, ((0, 0), (0, 0), (0, 124)))


def _from_groups(vg):
    S = vg.shape[1]
    v = vg[:, :, :4].transpose(1, 0, 2).reshape(S, SSM_HEADS)
    return jnp.pad(v, ((0, 0), (0, 128 - SSM_HEADS)))


def _expand4(v, lanes):
    out = jnp.broadcast_to(v[:, 3:4], lanes.shape)
    for hh in (2, 1, 0):
        out = jnp.where(lanes < 64 * (hh + 1), v[:, hh:hh + 1], out)
    return out


def _ssd_fwd(xbc, dt_g, ac_g, ac_t, *, name="ssd_fwd"):
    S = xbc.shape[0]
    nc = S // CHUNK
    Lc = CHUNK

    def body(x_ref, b_ref, c_ref, dt_ref, ac_ref, act_ref, y_ref, st_out_ref, st_ref):
        g = pl.program_id(0)
        c = pl.program_id(1)

        @pl.when(c == 0)
        def _():
            st_ref[...] = jnp.zeros_like(st_ref)

        bv = b_ref[...]
        cbf = c_ref[...].astype(BF16)
        cb = _dot(cbf, bv.astype(BF16), 1, 1)
        causal = _iota((Lc, Lc), 0) >= _iota((Lc, Lc), 1)
        lane256 = _iota((Lc, 256), 1)
        lane128 = _iota((Lc, 128), 1)
        row128 = _iota((128, 128), 0)
        dtg, acg = dt_ref[...], ac_ref[...]
        ac_last = ac_ref[pl.ds(Lc - 1, 1), :]
        dt4 = _expand4(dtg, lane256)
        ac4 = _expand4(acg, lane256)
        e4 = jnp.exp(ac4)
        xdb = (x_ref[...] * dt4).astype(BF16)
        st_out_ref[...] = st_ref[...]
        for p in range(2):
            xd_p = xdb[:, 128 * p:128 * (p + 1)]
            st_p = st_ref[p]
            ys, sn, cds = [], [], []
            for q in range(2):
                hh = 2 * p + q
                a_col = acg[:, hh:hh + 1]
                a_row = act_ref[pl.ds(4 * g + hh, 1), :]
                dec = jnp.exp(jnp.where(causal, a_col - a_row, NEG))
                w = (cb * dec).astype(BF16)
                ys.append(_dot(w, xd_p))
                al = ac_last[:, hh:hh + 1]
                dte = jnp.exp(al - a_col)
                sn.append(_dot(xd_p, (bv * dte).astype(BF16), 0, 0))
                cds.append(jnp.exp(al))
            y_diag = jnp.where(lane128 < 64, ys[0], ys[1])
            y_off = _dot(cbf, st_p.astype(BF16), 1, 1) * e4[:, 128 * p:128 * (p + 1)]
            y_ref[:, 128 * p:128 * (p + 1)] = y_diag + y_off
            st_ref[p] = jnp.where(row128 < 64, st_p * cds[0] + sn[0], st_p * cds[1] + sn[1])

    per_g = lambda g, c: (g, c, 0)
    return pl.pallas_call(
        body, name=name, grid=(SSM_GROUPS, nc),
        in_specs=[pl.BlockSpec((Lc, 256), lambda g, c: (c, g)),
                  pl.BlockSpec((Lc, 128), lambda g, c: (c, 16 + g)),
                  pl.BlockSpec((Lc, 128), lambda g, c: (c, 24 + g)),
                  pl.BlockSpec((None, Lc, 128), per_g), pl.BlockSpec((None, Lc, 128), per_g),
                  pl.BlockSpec((SSM_HEADS, Lc), lambda g, c: (0, c))],
        out_specs=[pl.BlockSpec((Lc, 256), lambda g, c: (c, g)),
                   pl.BlockSpec((None, None, 2, 128, 128), lambda g, c: (g, c, 0, 0, 0))],
        out_shape=[jax.ShapeDtypeStruct((S, 2048), F32), jax.ShapeDtypeStruct((SSM_GROUPS, nc, 2, 128, 128), F32)],
        scratch_shapes=[pltpu.VMEM((2, 128, 128), F32)], compiler_params=_cp(2),
    )(xbc, xbc, xbc, dt_g, ac_g, ac_t)


def _ssd_bwd(xbc, dt_g, ac_g, ac_t, states, dy, dexp, *, name="ssd_bwd"):
    S = xbc.shape[0]
    nc = S // CHUNK
    Lc = CHUNK

    def body(x_ref, b_ref, c_ref, dt_ref, ac_ref, act_ref, st_ref, dy_ref, d_ref, dx_ref, db_ref, dc_ref, dh_ref, ds_ref):
        g = pl.program_id(0)
        cc = pl.program_id(1)

        @pl.when(cc == 0)
        def _():
            ds_ref[...] = jnp.zeros_like(ds_ref)

        bv = b_ref[...]
        cv = c_ref[...]
        bbf, cbf = bv.astype(BF16), cv.astype(BF16)
        cb = _dot(cbf, bbf, 1, 1)
        causal = _iota((Lc, Lc), 0) >= _iota((Lc, Lc), 1)
        lane256 = _iota((Lc, 256), 1)
        lane128 = _iota((Lc, 128), 1)
        row128 = _iota((128, 128), 0)
        dtg, acg = dt_ref[...], ac_ref[...]
        ac_last = ac_ref[pl.ds(Lc - 1, 1), :]
        dt4 = _expand4(dtg, lane256)
        ac4 = _expand4(acg, lane256)
        acl4 = _expand4(ac_last, _iota((1, 256), 1))
        e4 = jnp.exp(ac4)
        dte4 = jnp.exp(acl4 - ac4)
        xv = x_ref[...]
        xd = xv * dt4
        xdb = xd.astype(BF16)
        dyv = dy_ref[...]
        dcb = jnp.zeros((Lc, Lc), F32)
        dc_acc = jnp.zeros((Lc, 128), F32)
        db_acc = jnp.zeros((Lc, 128), F32)
        ind_rows = _iota((256, 128), 0) >> 6
        ind_cols = _iota((256, 128), 1)
        ind_a = (ind_rows == ind_cols).astype(BF16)
        ind_b = (ind_rows + 4 == ind_cols).astype(BF16)
        u_parts, dxd_parts, ends = [], [], []
        for p in range(2):
            sl = slice(128 * p, 128 * (p + 1))
            xd_p, xdb_p, dy_p = xd[:, sl], xdb[:, sl], dyv[:, sl]
            dyb_p = dy_p.astype(BF16)
            e_p, dte_p = e4[:, sl], dte4[:, sl]
            sp = st_ref[p]
            spb = sp.astype(BF16)
            dsn = ds_ref[p]
            dsnb = dsn.astype(BF16)
            yds, dxds, cds = [], [], []
            for q in range(2):
                hh = 2 * p + q
                a_col = acg[:, hh:hh + 1]
                a_row = act_ref[pl.ds(4 * g + hh, 1), :]
                dec = jnp.exp(jnp.where(causal, a_col - a_row, NEG))
                w = (cb * dec).astype(BF16)
                head = (lane128 < 64) if q == 0 else (lane128 >= 64)
                dym = jnp.where(head, dyb_p, jnp.zeros_like(dyb_p))
                dw = _dot(dym, xdb_p, 1, 1)
                dcb = dcb + dw * dec
                yds.append(_dot(w, xdb_p))
                dxds.append(_dot(w, dyb_p, 0, 0))
                cds.append(jnp.exp(ac_last[:, hh:hh + 1]))
            y_diag = jnp.where(lane128 < 64, yds[0], yds[1])
            dxd_diag = jnp.where(lane128 < 64, dxds[0], dxds[1])
            y_off = _dot(cbf, spb, 1, 1) * e_p
            dgp = dy_p * e_p
            dgb = dgp.astype(BF16)
            dc_acc = dc_acc + _dot(dgb, spb)
            dsp = _dot(dgb, cbf, 0, 0)
            cd_col = jnp.where(row128[:, 0:1] < 64, cds[0], cds[1])
            qm = _dot(bbf, dsnb, 1, 1)
            dxd_state = dte_p * qm
            db_acc = db_acc + _dot((xd_p * dte_p).astype(BF16), dsnb)
            t_p = xd_p * dxd_state
            prod = dsn * sp
            e0 = jnp.sum(jnp.sum(jnp.where(row128 < 64, prod, 0.0), axis=1, keepdims=True), axis=0, keepdims=True)
            e1 = jnp.sum(jnp.sum(jnp.where(row128 >= 64, prod, 0.0), axis=1, keepdims=True), axis=0, keepdims=True)
            tcol = jnp.sum(t_p, axis=0, keepdims=True)
            lane1 = _iota((1, 128), 1)
            t0 = jnp.sum(jnp.where(lane1 < 64, tcol, 0.0), axis=1, keepdims=True)
            t1 = jnp.sum(jnp.where(lane1 >= 64, tcol, 0.0), axis=1, keepdims=True)
            ends.append(e0 * cds[0] + t0)
            ends.append(e1 * cds[1] + t1)
            ds_ref[p] = dsn * cd_col + dsp
            u_parts.append(dyb_p.astype(F32) * y_diag - xdb_p.astype(F32) * dxd_diag + dy_p * y_off - t_p)
            dxd_parts.append(dxd_diag + dxd_state)
        dxd = jnp.concatenate(dxd_parts, axis=1)
        u_all = jnp.concatenate(u_parts, axis=1)
        dx_ref[...] = dxd * dt4 + dyv * d_ref[...]
        dcbb = dcb.astype(BF16)
        dc_ref[...] = dc_acc + _dot(dcbb, bbf)
        db_ref[...] = db_acc + _dot(dcbb, cbf, 0, 0)
        lane = _iota((Lc, 128), 1)
        endv = jnp.zeros((Lc, 128), F32)
        for hh in range(4):
            endv = jnp.where(lane == 8 + hh, ends[hh], endv)
        dh_ref[...] = _dot3(dxd * xv, ind_a) + _dot3(u_all, ind_b) + endv

    rev = lambda c: nc - 1 - c
    per_g = lambda g, c: (g, rev(c), 0)
    return pl.pallas_call(
        body, name=name, grid=(SSM_GROUPS, nc),
        in_specs=[pl.BlockSpec((Lc, 256), lambda g, c: (rev(c), g)),
                  pl.BlockSpec((Lc, 128), lambda g, c: (rev(c), 16 + g)),
                  pl.BlockSpec((Lc, 128), lambda g, c: (rev(c), 24 + g)),
                  pl.BlockSpec((None, Lc, 128), per_g), pl.BlockSpec((None, Lc, 128), per_g),
                  pl.BlockSpec((SSM_HEADS, Lc), lambda g, c: (0, rev(c))),
                  pl.BlockSpec((None, None, 2, 128, 128), lambda g, c: (g, rev(c), 0, 0, 0)),
                  pl.BlockSpec((Lc, 256), lambda g, c: (rev(c), g)),
                  pl.BlockSpec((1, 256), lambda g, c: (0, g))],
        out_specs=[pl.BlockSpec((Lc, 256), lambda g, c: (rev(c), g)),
                   pl.BlockSpec((Lc, 128), lambda g, c: (rev(c), g)),
                   pl.BlockSpec((Lc, 128), lambda g, c: (rev(c), g)),
                   pl.BlockSpec((None, Lc, 128), per_g)],
        out_shape=[jax.ShapeDtypeStruct((S, 2048), F32), jax.ShapeDtypeStruct((S, 1024), F32),
                   jax.ShapeDtypeStruct((S, 1024), F32), jax.ShapeDtypeStruct((SSM_GROUPS, S, 128), F32)],
        scratch_shapes=[pltpu.VMEM((2, 128, 128), F32)], compiler_params=_cp(2),
    )(xbc, xbc, xbc, dt_g, ac_g, ac_t, states, dy, dexp)


def _ssd_post(dhead, dt_g, sg_g, alog_g, *, name="ssd_post"):
    S = dhead.shape[1]
    nc = S // CHUNK
    Lc = CHUNK

    def body(dh_ref, dt_ref, sg_ref, al_ref, o_ref, s_ref):
        c = pl.program_id(1)
        dh = dh_ref[...]
        a_neg = -jnp.exp(al_ref[...])
        lane = _iota((Lc, 128), 1)
        row = _iota((Lc, 128), 0)
        dac = jnp.where(lane < 4, pltpu.roll(dh, 124, axis=1), 0.0)
        end = jnp.where(lane < 4, pltpu.roll(dh, 120, axis=1), 0.0)
        k = 1
        while k < Lc:
            dac = dac + jnp.where(row < Lc - k, pltpu.roll(dac, Lc - k, axis=0), 0.0)
            k *= 2
        da = dac + end
        dtv = dt_ref[...]
        ddt = jnp.where(lane < 4, da * a_neg + dh, 0.0)
        ddtr = ddt * sg_ref[...]
        o_ref[...] = ddtr
        dal = jnp.sum(da * dtv, axis=0, keepdims=True) * a_neg
        dbias = jnp.sum(ddtr, axis=0, keepdims=True)
        row8 = _iota((8, 128), 0)
        part = jnp.where(row8 == 0, dal, jnp.where(row8 == 1, dbias, 0.0))

        @pl.when(c == 0)
        def _():
            s_ref[...] = part

        @pl.when(c > 0)
        def _():
            s_ref[...] += part

    per = pl.BlockSpec((None, Lc, 128), lambda g, c: (g, c, 0))
    return pl.pallas_call(
        body, name=name, grid=(SSM_GROUPS, nc),
        in_specs=[per, per, per, pl.BlockSpec((None, 1, 128), lambda g, c: (g, 0, 0))],
        out_specs=[per, pl.BlockSpec((None, 8, 128), lambda g, c: (g, 0, 0))],
        out_shape=[jax.ShapeDtypeStruct((SSM_GROUPS, S, 128), F32), jax.ShapeDtypeStruct((SSM_GROUPS, 8, 128), F32)],
        compiler_params=_cp(2),
    )(dhead, dt_g, sg_g, alog_g)


def _gate_fwd(y, xbc, zx, dexp, gn, *, name="gate_fwd", tr=256):
    S = y.shape[0]
    W = 2048
    gw = W // SSM_GROUPS

    def body(y_ref, x_ref, z_ref, d_ref, g_ref, o_ref):
        z = z_ref[...]
        u = (y_ref[...] + x_ref[...] * d_ref[...]) * (z * _sigmoid(z))
        gv = g_ref[...]
        for q in range(SSM_GROUPS):
            sl = slice(gw * q, gw * (q + 1))
            uq = u[:, sl]
            r = lax.rsqrt(jnp.mean(uq * uq, axis=-1, keepdims=True) + EPS)
            o_ref[:, sl] = (uq * r * gv[:, sl]).astype(BF16)

    row = pl.BlockSpec((tr, W), lambda i: (i, 0))
    vec = pl.BlockSpec((1, W), lambda i: (0, 0))
    return pl.pallas_call(
        body, name=name, grid=(S // tr,), in_specs=[row, row, row, vec, vec], out_specs=row,
        out_shape=jax.ShapeDtypeStruct((S, W), BF16), compiler_params=_cp(1),
    )(y, xbc, zx, dexp, gn)


def _gate_bwd(y, xbc, zx, dexp, gn, dout, *, name="gate_bwd", tr=256):
    S = y.shape[0]
    W = 2048
    gw = W // SSM_GROUPS
    steps = S // tr

    def body(y_ref, x_ref, z_ref, d_ref, g_ref, do_ref, dy_ref, dz_ref, dg_ref, dd_ref, acc_ref):
        i = pl.program_id(0)

        @pl.when(i == 0)
        def _():
            acc_ref[...] = jnp.zeros_like(acc_ref)

        z = z_ref[...]
        sg = _sigmoid(z)
        sz = z * sg
        xs = x_ref[...]
        yt = y_ref[...] + xs * d_ref[...]
        u = yt * sz
        gv = g_ref[...]
        do = do_ref[...]
        dgs = []
        for q in range(SSM_GROUPS):
            sl = slice(gw * q, gw * (q + 1))
            uq = u[:, sl]
            r = lax.rsqrt(jnp.mean(uq * uq, axis=-1, keepdims=True) + EPS)
            uh = uq * r
            dq = do[:, sl]
            duh = dq * gv[:, sl]
            duq = r * (duh - uh * jnp.mean(duh * uh, axis=-1, keepdims=True))
            dgs.append(jnp.sum(dq * uh, axis=0, keepdims=True))
            dyt = duq * sz[:, sl]
            dy_ref[:, sl] = dyt
            dz_ref[:, sl] = (duq * yt[:, sl] * (sg[:, sl] * (1.0 + z[:, sl] * (1.0 - sg[:, sl])))).astype(BF16)
            acc_ref[:, sl] += jnp.sum(dyt * xs[:, sl], axis=0, keepdims=True)
        dg = jnp.concatenate(dgs, axis=1)

        @pl.when(i == 0)
        def _():
            dg_ref[...] = dg

        @pl.when(i > 0)
        def _():
            dg_ref[...] += dg

        @pl.when(i == steps - 1)
        def _():
            ind = ((_iota((W, 128), 0) >> 6) == _iota((W, 128), 1)).astype(BF16)
            dd_ref[...] = _dot3(jnp.broadcast_to(acc_ref[...], (8, W)), ind)[0:1, :]

    row = pl.BlockSpec((tr, W), lambda i: (i, 0))
    vec = pl.BlockSpec((1, W), lambda i: (0, 0))
    return pl.pallas_call(
        body, name=name, grid=(steps,), in_specs=[row, row, row, vec, vec, row],
        out_specs=[row, row, vec, pl.BlockSpec((1, 128), lambda i: (0, 0))],
        out_shape=[jax.ShapeDtypeStruct((S, W), F32), jax.ShapeDtypeStruct((S, W), BF16),
                   jax.ShapeDtypeStruct((1, W), F32), jax.ShapeDtypeStruct((1, 128), F32)],
        scratch_shapes=[pltpu.VMEM((1, W), F32)], compiler_params=_cp(1),
    )(y, xbc, zx, dexp, gn, dout)


def _rope_cs(posf, *, name="rope_tables", tr=256):
    S = posf.shape[0]

    def body(p_ref, c_ref, s_ref):
        j = (_iota((tr, 128), 1) & 31).astype(F32)
        ang = p_ref[...] * jnp.exp(j * (-math.log(ROPE_THETA) / 32.0))
        c_ref[...] = jnp.cos(ang)
        s_ref[...] = jnp.sin(ang)

    blk = pl.BlockSpec((tr, 128), lambda i: (i, 0))
    return pl.pallas_call(
        body, name=name, grid=(S // tr,), in_specs=[pl.BlockSpec((tr, 1), lambda i: (i, 0))], out_specs=[blk, blk],
        out_shape=[jax.ShapeDtypeStruct((S, 128), F32)] * 2, compiler_params=_cp(1),
    )(posf)


def _rope_tables(c_ref, s_ref, shape):
    reps = shape[1] // 128
    return jnp.tile(c_ref[...], (1, reps)), jnp.tile(s_ref[...], (1, reps)), (_iota(shape, 1) & 63) < 32


def _hn_inds(W):
    ind = ((_iota((W, 128), 0) >> 6) == _iota((W, 128), 1)).astype(BF16)
    ind_t = ((_iota((128, W), 1) >> 6) == _iota((128, W), 0)).astype(BF16)
    return ind, ind_t


def _hnrope_fwd(xin, col0, W, gain_w, rope, *, name, tr=256):
    S = xin.shape[0]
    off = col0 // W

    def body(x_ref, g_ref, c_ref, s_ref, o_ref):
        x = x_ref[...]
        ind, ind_t = _hn_inds(W)
        r = lax.rsqrt(_dot3(x * x, ind) * (1.0 / HEAD) + EPS)
        xn = x * _dot3(r, ind_t) * g_ref[...]
        cs, sn, half = _rope_tables(c_ref, s_ref, (tr, W))
        rot = jnp.where(half, -pltpu.roll(xn, W - 32, axis=1), pltpu.roll(xn, 32, axis=1))
        o_ref[...] = (xn * cs + rot * sn).astype(BF16)

    tab = pl.BlockSpec((tr, 128), lambda i: (i, 0))
    return pl.pallas_call(
        body, name=name, grid=(S // tr,),
        in_specs=[pl.BlockSpec((tr, W), lambda i: (i, off)), pl.BlockSpec((1, W), lambda i: (0, 0)), tab, tab],
        out_specs=pl.BlockSpec((tr, W), lambda i: (i, 0)), out_shape=jax.ShapeDtypeStruct((S, W), BF16),
        compiler_params=_cp(1),
    )(xin, gain_w, *rope)


def _hnrope_bwd(xin, col0, W, gain_w, rope, dout, *, name, tr=256):
    S = xin.shape[0]
    off = col0 // W
    steps = S // tr

    def body(x_ref, g_ref, c_ref, s_ref, do_ref, dx_ref, cs_ref, dg_ref, acc_ref):
        i = pl.program_id(0)
        x = x_ref[...]
        ind, ind_t = _hn_inds(W)
        r = lax.rsqrt(_dot3(x * x, ind) * (1.0 / HEAD) + EPS)
        rw = _dot3(r, ind_t)
        xh = x * rw
        cs, sn, half = _rope_tables(c_ref, s_ref, (tr, W))
        do = do_ref[...].astype(F32)
        gs = do * sn
        g1 = do * cs + jnp.where(half, pltpu.roll(gs, W - 32, axis=1), -pltpu.roll(gs, 32, axis=1))
        dxh = g1 * g_ref[...]
        t = _dot3(dxh * xh, ind) * (1.0 / HEAD)
        dx = rw * (dxh - xh * _dot3(t, ind_t))
        dx_ref[...] = dx.astype(BF16)
        cpart = jnp.sum(dx, axis=0, keepdims=True)
        gpart = jnp.sum(g1 * xh, axis=0, keepdims=True)

        @pl.when(i == 0)
        def _():
            cs_ref[...] = cpart
            acc_ref[...] = gpart

        @pl.when(i > 0)
        def _():
            cs_ref[...] += cpart
            acc_ref[...] += gpart

        @pl.when(i == steps - 1)
        def _():
            fold = ((_iota((W, 128), 0) & 63) == _iota((W, 128), 1)).astype(BF16)
            dg_ref[...] = _dot3(jnp.broadcast_to(acc_ref[...], (8, W)), fold)[0:1, :]

    tab = pl.BlockSpec((tr, 128), lambda i: (i, 0))
    return pl.pallas_call(
        body, name=name, grid=(steps,),
        in_specs=[pl.BlockSpec((tr, W), lambda i: (i, off)), pl.BlockSpec((1, W), lambda i: (0, 0)), tab, tab,
                  pl.BlockSpec((tr, W), lambda i: (i, 0))],
        out_specs=[pl.BlockSpec((tr, W), lambda i: (i, 0)), pl.BlockSpec((1, W), lambda i: (0, 0)),
                   pl.BlockSpec((1, 128), lambda i: (0, 0))],
        out_shape=[jax.ShapeDtypeStruct((S, W), BF16), jax.ShapeDtypeStruct((1, W), F32),
                   jax.ShapeDtypeStruct((1, 128), F32)],
        scratch_shapes=[pltpu.VMEM((1, W), F32)], compiler_params=_cp(1),
    )(xin, gain_w, *rope, dout)


def _attn_probs(q, kb, sink_ref, h, i):
    s = _dot(q, kb, 1, 1) * (HEAD ** -0.5)
    r = _iota((4 * WINDOW, 2 * WINDOW), 0)
    ki = _iota((4 * WINDOW, 2 * WINDOW), 1)
    rel = (r & (WINDOW - 1)) + WINDOW - ki
    mask = (rel >= 0) & (rel < WINDOW) & ((ki >= WINDOW) | (i > 0))
    s = jnp.where(mask, s, NEG)
    r1 = _iota((4 * WINDOW, 1), 0)
    sink = jnp.where(r1 < WINDOW, sink_ref[4 * h], jnp.where(r1 < 2 * WINDOW, sink_ref[4 * h + 1],
                     jnp.where(r1 < 3 * WINDOW, sink_ref[4 * h + 2], sink_ref[4 * h + 3])))
    m = jnp.maximum(jnp.max(s, axis=1, keepdims=True), sink)
    p = jnp.exp(s - m)
    ps = jnp.exp(sink - m)
    inv = 1.0 / (jnp.sum(p, axis=1, keepdims=True) + ps)
    return p * inv, ps * inv


def _attn_specs(S):
    qspec = pl.BlockSpec((None, ATT_G, WINDOW, HEAD), lambda h, i: (h, 0, i, 0))
    cur = pl.BlockSpec((None, WINDOW, HEAD), lambda h, i: (h, i, 0))
    prev = pl.BlockSpec((None, WINDOW, HEAD), lambda h, i: (h, jnp.maximum(i - 1, 0), 0))
    return qspec, cur, prev


def _attn_fwd(qh, kh, vh, sinks, *, name="attn_fwd"):
    S = kh.shape[1]
    nb = S // WINDOW

    def body(s_ref, q_ref, kc_ref, kp_ref, vc_ref, vp_ref, o_ref):
        h, i = pl.program_id(0), pl.program_id(1)
        q = q_ref[...].reshape(ATT_G * WINDOW, HEAD)
        kb = jnp.concatenate([kp_ref[...], kc_ref[...]], axis=0)
        vb = jnp.concatenate([vp_ref[...], vc_ref[...]], axis=0)
        probs, _ = _attn_probs(q, kb, s_ref, h, i)
        o = _dot(probs.astype(BF16), vb)
        o_ref[...] = o.reshape(ATT_G, WINDOW, HEAD).astype(BF16)

    qspec, cur, prev = _attn_specs(S)
    return pl.pallas_call(
        body, name=name, grid=(ATT_KV, nb),
        in_specs=[pl.BlockSpec(memory_space=pltpu.SMEM), qspec, cur, prev, cur, prev], out_specs=qspec,
        out_shape=jax.ShapeDtypeStruct((ATT_KV, ATT_G, S, HEAD), BF16), compiler_params=_cp(2),
    )(sinks, qh, kh, kh, vh, vh)


def _attn_bwd(qh, kh, vh, sinks, doh, *, name="attn_bwd"):
    S = kh.shape[1]
    nb = S // WINDOW

    def body(s_ref, q_ref, kc_ref, kp_ref, vc_ref, vp_ref, do_ref, dq_ref, dk_ref, dv_ref, dsk_ref):
        h, i = pl.program_id(0), pl.program_id(1)

        @pl.when(i == 0)
        def _():
            dk_ref[...] = jnp.zeros_like(dk_ref)
            dv_ref[...] = jnp.zeros_like(dv_ref)
            dsk_ref[...] = jnp.zeros_like(dsk_ref)

        q = q_ref[...].reshape(ATT_G * WINDOW, HEAD)
        do = do_ref[...].reshape(ATT_G * WINDOW, HEAD)
        kb = jnp.concatenate([kp_ref[...], kc_ref[...]], axis=0)
        vb = jnp.concatenate([vp_ref[...], vc_ref[...]], axis=0)
        probs, psink = _attn_probs(q, kb, s_ref, h, i)
        dp = _dot(do, vb, 1, 1)
        delta = jnp.sum(probs * dp, axis=1, keepdims=True)
        ds = (probs * (dp - delta)).astype(BF16)
        scale = HEAD ** -0.5
        dq_ref[...] = (_dot(ds, kb) * scale).reshape(ATT_G, WINDOW, HEAD)
        dkb = _dot(ds, q, 0, 0) * scale
        dvb = _dot(probs.astype(BF16), do, 0, 0)
        cur = pl.multiple_of(i * WINDOW, WINDOW)
        dk_ref[pl.ds(cur, WINDOW), :] += dkb[WINDOW:, :]
        dv_ref[pl.ds(cur, WINDOW), :] += dvb[WINDOW:, :]

        @pl.when(i > 0)
        def _():
            prv = pl.multiple_of((i - 1) * WINDOW, WINDOW)
            dk_ref[pl.ds(prv, WINDOW), :] += dkb[:WINDOW, :]
            dv_ref[pl.ds(prv, WINDOW), :] += dvb[:WINDOW, :]

        dsr = -psink * delta
        lane = _iota((8, 128), 1)
        row = _iota((8, 128), 0)
        upd = jnp.zeros((8, 128), F32)
        for gq in range(ATT_G):
            v = jnp.sum(dsr[gq * WINDOW:(gq + 1) * WINDOW, :], axis=0, keepdims=True)
            upd = jnp.where((lane == gq) & (row == 0), v, upd)
        dsk_ref[...] += upd

    qspec, cur, prev = _attn_specs(S)
    full = pl.BlockSpec((None, S, HEAD), lambda h, i: (h, 0, 0))
    return pl.pallas_call(
        body, name=name, grid=(ATT_KV, nb),
        in_specs=[pl.BlockSpec(memory_space=pltpu.SMEM), qspec, cur, prev, cur, prev, qspec],
        out_specs=[qspec, full, full, pl.BlockSpec((None, 8, 128), lambda h, i: (h, 0, 0))],
        out_shape=[jax.ShapeDtypeStruct((ATT_KV, ATT_G, S, HEAD), F32), jax.ShapeDtypeStruct((ATT_KV, S, HEAD), F32),
                   jax.ShapeDtypeStruct((ATT_KV, S, HEAD), F32), jax.ShapeDtypeStruct((ATT_KV, 8, 128), F32)],
        compiler_params=_cp(2),
    )(sinks, qh, kh, kh, vh, vh, doh)


def _heads_major(t, nh):
    S = t.shape[0]
    return t.reshape(S, nh, HEAD).transpose(1, 0, 2)


def _tokens_major(t):
    nh, S, _ = t.shape
    return t.transpose(1, 0, 2).reshape(S, nh * HEAD)


class _NoComm:
    def late_weights(self, w, after):
        return w

    def grads(self, group, tensors, after):
        return None


def _local_step(x, posf, target, w, comm=None):
    S, D = x.shape
    gr = {}
    comm = comm or _NoComm()

    (h1,) = _rms_fwd(x, [w["a_norm"]], name="a_norm_f", dep=w.get("dep"))
    zx = _mm(h1, w["w_zx"], name="in_proj_zx")
    dtr = _mm(h1, w["w_dt"], name="in_proj_dt")
    xbc = _conv_silu_fwd(zx, 2048, 4096, w["a_conv_w"], w["a_conv_b"], name="a_conv_f")
    dt, acum, sgd = _ssd_prep(dtr, w["a_dt_bias"], w["a_A_log"])
    dt_g, ac_g, sg_g = _to_groups(dt), _to_groups(acum), _to_groups(sgd)
    ac_t = acum[:, :SSM_HEADS].T
    y_ssd, states = _ssd_fwd(xbc, dt_g, ac_g, ac_t)
    yg = _gate_fwd(y_ssd, xbc, zx, w["a_Dexp"], w["a_gnorm"])
    x1 = _mm(yg, w["a_out_proj"], res=x, name="out_proj")

    w = comm.late_weights(w, x1)
    FW = w["f_w_in"][0].shape[2]

    def ffn_fwd(xin, l):
        (h,) = _rms_fwd(xin, [w["f_norm"][l]], name=f"f_norm_f{l}")
        u = _mm(h, w["f_w_in"][l], name=f"f_in{l}", dims=(S, N_CHIPS * FW, D), tn=FW,
                b_spec=pl.BlockSpec((None, D, FW), lambda i, j, k: (j, 0, 0)))
        a = _ffn_act_fwd(u, w["f_conv_w"][l], w["f_conv_b"][l], name=f"f_act_f{l}")
        xo = _mm(a, w["f_w_down"][l], res=xin, name=f"f_down{l}")
        return xo, (h, u)

    x2, ffn0 = ffn_fwd(x1, 0)

    hk, hq = _rms_fwd(x2, [w["kv_norm"], w["b_norm"]], name="kvq_norm_f")
    kv = _mm(hk, w["w_kv"], bias=w["b_kv"], name="kv_proj")
    q = _mm(hq, w["w_q"], bias=w["b_q"], name="q_proj")
    rope = _rope_cs(posf)
    kr = _hnrope_fwd(kv, 0, 256, w["k_norm_w"], rope, name="k_rope_f")
    qr = _hnrope_fwd(q, 0, 1024, w["q_norm_w"], rope, name="q_rope_f")
    qh = _heads_major(qr, 16).reshape(ATT_KV, ATT_G, S, HEAD)
    kh = _heads_major(kr, ATT_KV)
    vh = _heads_major(kv[:, 256:].astype(BF16), ATT_KV)
    att_h = _attn_fwd(qh, kh, vh, w["sinks"])
    att = _tokens_major(att_h.reshape(16, S, HEAD))
    x3 = _mm(att, w["w_o"], bias=w["b_o"], res=x2, name="o_proj")
    x4, ffn1 = ffn_fwd(x3, 1)

    dy, loss_part = _loss(x4, target)

    def ffn_bwd(xin, l, saved, dyo, want_colsum, dep=None):
        h, u = saved
        da = _mm(dyo, w["f_w_down"][l], tb=True, name=f"f_down_dx{l}", dep=dep)
        du, dcw, dcb, a = _ffn_act_bwd(u, w["f_conv_w"][l], w["f_conv_b"][l], da, name=f"f_act_b{l}")
        dw_down = _mm(a, dyo, ta=True, out_dtype=BF16, name=f"f_down_dw{l}")
        dw_in = _mm(h, du, ta=True, out_dtype=BF16, name=f"f_in_dw{l}", dims=(D, N_CHIPS * FW, S), tm=D, tn=FW, tk=S,
                    b_spec=pl.BlockSpec((None, S, FW), lambda i, j, k: (j // 2, 0, j % 2)),
                    o_spec=pl.BlockSpec((None, D, FW), lambda i, j, k: (j, i, 0)), o_shape=(N_CHIPS, D, FW))
        ts = _pick(S, (1024, 512, 256))
        dh = _mm(du, w["f_w_in"][l], tb=True, name=f"f_in_dx{l}", dims=(S, D, N_CHIPS * FW), tm=ts, tn=512, tk=FW,
                 a_spec=pl.BlockSpec((None, ts, FW), lambda i, j, k: (k // 2, i, k % 2)),
                 b_spec=pl.BlockSpec((None, 512, FW), lambda i, j, k: (k, j, 0)))
        outs = _rms_bwd(xin, [w["f_norm"][l]], [dh], dyo, name=f"f_norm_b{l}", want_colsum=want_colsum)
        g = dict(f_norm=outs[1], f_w_in=dw_in, f_conv_w=dcw, f_conv_b=dcb, f_w_down=dw_down)
        return outs[0], g, (outs[2] if want_colsum else None)

    dx3, gr["ffn1"], db_o = ffn_bwd(x3, 1, ffn1, dy, True)
    gr["b_o"] = db_o
    gr["w_o"] = _mm(att, dx3, ta=True, out_dtype=BF16, name="o_proj_dw")
    datt = _mm(dx3, w["w_o"], tb=True, out_dtype=BF16, name="o_proj_dx")
    doh = _heads_major(datt, 16).reshape(ATT_KV, ATT_G, S, HEAD)
    dqh, dkh, dvh, dsk = _attn_bwd(qh, kh, vh, w["sinks"], doh)
    gr["sinks"] = dsk[:, 0, :4].reshape(1, 16)
    dqr = _tokens_major(dqh.reshape(16, S, HEAD))
    dkr = _tokens_major(dkh)
    dv = _tokens_major(dvh).astype(BF16)
    dq, db_q, dqn = _hnrope_bwd(q, 0, 1024, w["q_norm_w"], rope, dqr, name="q_rope_b")
    dk, db_k, dkn = _hnrope_bwd(kv, 0, 256, w["k_norm_w"], rope, dkr, name="k_rope_b")
    gr["q_norm"], gr["k_norm"] = dqn[:, :HEAD], dkn[:, :HEAD]
    gr["b_q"] = db_q
    gr["b_kv"] = jnp.concatenate([db_k, _colsum(dv, name="dv_colsum")], axis=1)
    dkv = jnp.concatenate([dk, dv], axis=1)
    gr["w_q"] = _mm(hq, dq, ta=True, out_dtype=BF16, name="q_proj_dw")
    gr["w_kv"] = _mm(hk, dkv, ta=True, out_dtype=BF16, name="kv_proj_dw")
    tok = comm.grads(1, dict(f_down1=gr["ffn1"]["f_w_down"], f_in1=gr["ffn1"]["f_w_in"], w_o=gr["w_o"], w_q=gr["w_q"],
                             w_kv=gr["w_kv"]), None)
    dhq = _mm(dq, w["w_q"], tb=True, name="q_proj_dx", dep=tok)
    dhk = _mm(dkv, w["w_kv"], tb=True, name="kv_proj_dx")
    dx2, gr["kv_norm"], gr["b_norm"] = _rms_bwd(x2, [w["kv_norm"], w["b_norm"]], [dhk, dhq], dx3, name="kvq_norm_b")

    dx1, gr["ffn0"], _ = ffn_bwd(x1, 0, ffn0, dx2, False)

    tok = comm.grads(2, dict(f_down0=gr["ffn0"]["f_w_down"], f_in0=gr["ffn0"]["f_w_in"]), dx1)
    gr["a_out_proj"] = _mm(yg, dx1, ta=True, out_dtype=BF16, name="out_proj_dw")
    dyg = _mm(dx1, w["a_out_proj"], tb=True, name="out_proj_dx", dep=tok)
    dy_ssd, dz, gr["a_gnorm"], dD = _gate_bwd(y_ssd, xbc, zx, w["a_Dexp"], w["a_gnorm"], dyg)
    gr["a_D"] = dD[:, :SSM_HEADS]
    dxs, dB, dC, dhead = _ssd_bwd(xbc, dt_g, ac_g, ac_t, states, dy_ssd, w["a_Dexp"])
    ddtr_g, dsmall = _ssd_post(dhead, dt_g, sg_g, w["a_A_log_g"])
    gr["a_A_log"] = dsmall[:, 0, :4].reshape(1, SSM_HEADS)
    gr["a_dt_bias"] = dsmall[:, 1, :4].reshape(1, SSM_HEADS)
    ddtr = _from_groups(ddtr_g).astype(BF16)
    dxbc, gr["a_conv_w"], gr["a_conv_b"] = _conv_silu_bwd(
        zx, 2048, 4096, w["a_conv_w"], w["a_conv_b"], [(dxs, 0), (dB, 2048), (dC, 3072)], name="a_conv_b")
    gr["w_z"] = _mm(h1, dz, ta=True, out_dtype=BF16, name="in_proj_dwz")
    gr["w_x"] = _mm(h1, dxbc, ta=True, out_dtype=BF16, name="in_proj_dwx")
    gr["w_dt"] = _mm(h1, ddtr, ta=True, out_dtype=BF16, name="in_proj_dwdt")
    dh1 = _mm(dz, w["w_zx"], tb=True, name="in_proj_dxz")
    dh1 = _mm(dxbc, w["w_zx"], tb=True, b_koff=2048, res=dh1, name="in_proj_dxx")
    dh1 = _mm(ddtr, w["w_dt"], tb=True, res=dh1, name="in_proj_dxdt")
    dx0, gr["a_norm"] = _rms_bwd(x, [w["a_norm"]], [dh1], dx1, name="a_norm_b")
    comm.grads(3, dict(out_proj=gr["a_out_proj"], in_proj=_in_proj_grad(gr).reshape(D, N_CHIPS, -1).transpose(1, 0, 2)), dx0)
    return loss_part, dx0, gr


def _prep_small(full, w):
    w["a_norm"] = full["a_norm"]
    w["a_conv_w"] = full["a_conv_w"][0]
    w["a_conv_b"] = full["a_conv_b"]
    pad32 = lambda v: jnp.pad(v, ((0, 0), (0, 128 - SSM_HEADS)))
    w["a_dt_bias"] = pad32(full["a_dt_bias"])
    w["a_A_log"] = pad32(full["a_A_log"])
    w["a_A_log_g"] = jnp.pad(full["a_A_log"].reshape(SSM_GROUPS, 1, 4), ((0, 0), (0, 0), (0, 124)))
    w["a_Dexp"] = jnp.repeat(full["a_D"], HEAD, axis=1)
    w["a_gnorm"] = full["a_gnorm"]
    w["f_norm"] = [full["f_norm"][l:l + 1] for l in range(2)]
    w["f_conv_w"] = [full["f_conv_w"][l] for l in range(2)]
    w["f_conv_b"] = [full["f_conv_b"][l:l + 1] for l in range(2)]
    w["kv_norm"] = full["kv_norm"].reshape(1, -1)
    w["b_kv"] = full["b_kv"].reshape(1, -1)
    w["k_norm_w"] = jnp.tile(full["k_norm"].reshape(1, HEAD), (1, ATT_KV))
    w["b_norm"] = full["b_norm"]
    w["b_q"] = full["b_q"]
    w["q_norm_w"] = jnp.tile(full["q_norm"], (1, ATT_KV * ATT_G))
    w["sinks"] = full["sinks"].reshape(-1)
    w["b_o"] = full["b_o"]
    return w


def _split_in_proj(ip):
    return ip[:, :6144].astype(BF16), jnp.pad(ip[:, 6144:], ((0, 0), (0, 128 - SSM_HEADS))).astype(BF16)


def _prep_weights(full):
    w = _prep_small(full, {})
    w["w_zx"], w["w_dt"] = _split_in_proj(full["a_in_proj"][0])
    w["a_out_proj"] = full["a_out_proj"][0].astype(BF16)
    w["f_w_in"] = [full["f_w_in"][l].reshape(1024, N_CHIPS, -1).transpose(1, 0, 2).astype(BF16) for l in range(2)]
    w["f_w_down"] = [full["f_w_down"][l].astype(BF16) for l in range(2)]
    w["w_kv"] = full["w_kv"].astype(BF16)
    w["w_q"] = full["w_q"][0].astype(BF16)
    w["w_o"] = full["w_o"][0].astype(BF16)
    return w


def _small_grads(gr):
    g = {}
    g["a_norm"] = gr["a_norm"]
    g["a_conv_w"] = gr["a_conv_w"][None]
    g["a_conv_b"] = gr["a_conv_b"]
    g["a_dt_bias"], g["a_A_log"], g["a_D"] = gr["a_dt_bias"], gr["a_A_log"], gr["a_D"]
    g["a_gnorm"] = gr["a_gnorm"]
    g["kv_norm"] = gr["kv_norm"].reshape(-1)
    g["b_kv"] = gr["b_kv"].reshape(-1)
    g["k_norm"] = gr["k_norm"].reshape(-1)
    g["b_norm"] = gr["b_norm"]
    g["b_q"] = gr["b_q"]
    g["q_norm"] = gr["q_norm"]
    g["sinks"] = gr["sinks"]
    g["b_o"] = gr["b_o"]
    f = [gr["ffn0"], gr["ffn1"]]
    g["f_norm"] = jnp.concatenate([f[0]["f_norm"], f[1]["f_norm"]], axis=0)
    g["f_conv_w"] = jnp.stack([f[l]["f_conv_w"] for l in range(2)])
    g["f_conv_b"] = jnp.concatenate([f[l]["f_conv_b"] for l in range(2)], axis=0)
    return g


def _in_proj_grad(gr):
    return jnp.concatenate([gr["w_z"], gr["w_x"], gr["w_dt"][:, :SSM_HEADS]], axis=1)


def _full_grads(gr):
    g = _small_grads(gr)
    f32 = lambda t: t.astype(F32)
    g["a_in_proj"] = f32(_in_proj_grad(gr))[None]
    g["a_out_proj"] = f32(gr["a_out_proj"])[None]
    g["w_kv"] = f32(gr["w_kv"])
    g["w_q"] = f32(gr["w_q"])[None]
    g["w_o"] = f32(gr["w_o"])[None]
    f = [gr["ffn0"], gr["ffn1"]]
    g["f_w_in"] = jnp.stack([f32(f[l]["f_w_in"]).transpose(1, 0, 2).reshape(1024, -1) for l in range(2)])
    g["f_w_down"] = jnp.stack([f32(f[l]["f_w_down"]) for l in range(2)])
    return g


MESH = pl.DeviceIdType.MESH
WEIGHTS = ("a_norm", "a_in_proj", "a_conv_w", "a_conv_b", "a_dt_bias", "a_A_log", "a_D", "a_gnorm", "a_out_proj",
           "kv_norm", "w_kv", "b_kv", "k_norm", "b_norm", "w_q", "b_q", "q_norm", "sinks", "w_o", "b_o", "f_norm",
           "f_w_in", "f_conv_w", "f_conv_b", "f_w_down")
MATS = (("in_proj", "a_in_proj", 0), ("out_proj", "a_out_proj", 0), ("w_kv", "w_kv", None), ("w_q", "w_q", 0),
        ("w_o", "w_o", 0), ("f_in0", "f_w_in", 0), ("f_in1", "f_w_in", 1), ("f_down0", "f_w_down", 0),
        ("f_down1", "f_w_down", 1))
SMALL_CUT = (("a_norm", 1), ("a_conv_w", 2), ("a_conv_b", 1), ("a_gnorm", 1), ("f_conv_w", 2))
SMALL_REP = ("a_dt_bias", "a_A_log", "a_D", "kv_norm", "b_kv", "k_norm", "b_norm", "b_q", "q_norm", "sinks", "b_o",
             "f_norm", "f_conv_b")


def _coords():
    return lax.axis_index("x"), lax.axis_index("y"), lax.axis_index("c")


def _other_chips(x, y):
    return [(1 - x, y), (x, 1 - y), (1 - x, 1 - y)]


def _pack(arrs, rows_align, lanes, dtype):
    flat = jnp.concatenate([a.reshape(-1).astype(dtype) for a in arrs])
    per = rows_align * lanes
    total = -(-flat.shape[0] // per) * per
    return jnp.pad(flat, (0, total - flat.shape[0])).reshape(total // lanes, lanes)


def _unpack(flat, shapes):
    out, off = [], 0
    for s in shapes:
        n = math.prod(s)
        out.append(flat[off:off + n].reshape(s))
        off += n
    return out


def _remote(src, dst, send, recv, k, dev):
    return pltpu.make_async_remote_copy(src_ref=src, dst_ref=dst, send_sem=send.at[k], recv_sem=recv.at[k],
                                        device_id=dev, device_id_type=MESH)


_ANY = pl.BlockSpec(memory_space=pl.ANY)


def _halves(t):
    r, c = t.shape
    return t.reshape(2, r // 2, c)


def _gather_weights(shards, sp):
    n = len(shards)
    n_sem = 7 * n + 3

    def body(*refs):
        sh, sp_ref = refs[:n], refs[n]
        outs, sout = refs[n + 1:2 * n + 1], refs[2 * n + 1]
        send, recv, loc = refs[2 * n + 2:]
        x, y, c = _coords()
        me = 2 * x + y
        chips = _other_chips(x, y)
        sib = (x, y, 1 - c)
        l1 = pltpu.make_async_copy(sp_ref, sout.at[me], loc.at[0])
        l1.start()
        sends = []
        for j, (cx, cy) in enumerate(chips):
            sends.append(_remote(sp_ref, sout.at[me], send, recv, 7 * n + j, (cx, cy, c)))
            for t in range(n):
                sends.append(_remote(sh[t].at[c], outs[t].at[me, c], send, recv, 7 * t + j, (cx, cy, c)))
        for t in range(n):
            sends.append(_remote(sh[t], outs[t].at[me], send, recv, 7 * t + 6, sib))
        for cp in sends:
            cp.start()
        for j, (cx, cy) in enumerate(chips):
            src = 2 * cx + cy
            for t in range(n):
                _remote(sh[t].at[c], outs[t].at[src, c], send, recv, 7 * t + j, (cx, cy, c)).wait_recv()
                fwd = _remote(outs[t].at[src, c], outs[t].at[src, c], send, recv, 7 * t + 3 + j, sib)
                fwd.start()
                sends.append(fwd)
        for j, (cx, cy) in enumerate(chips):
            src = 2 * cx + cy
            _remote(sp_ref, sout.at[src], send, recv, 7 * n + j, (cx, cy, c)).wait_recv()
            for t in range(n):
                _remote(outs[t].at[src, 1 - c], outs[t].at[src, 1 - c], send, recv, 7 * t + 3 + j, sib).wait_recv()
        for t in range(n):
            _remote(sh[t], outs[t].at[me], send, recv, 7 * t + 6, sib).wait_recv()
        for cp in sends:
            cp.wait_send()
        l1.wait()

    res = pl.pallas_call(
        body, name="gather_weights", in_specs=[_ANY] * (n + 1), out_specs=[_ANY] * (n + 1),
        out_shape=[jax.ShapeDtypeStruct((N_CHIPS,) + t.shape, t.dtype) for t in shards]
        + [jax.ShapeDtypeStruct((N_CHIPS,) + sp.shape, sp.dtype)],
        scratch_shapes=[pltpu.SemaphoreType.DMA((n_sem,)), pltpu.SemaphoreType.DMA((n_sem,)),
                        pltpu.SemaphoreType.DMA((1,))],
    )(*shards, sp)
    return res[:n], res[n]


_HBM = pl.BlockSpec(memory_space=pltpu.HBM)
_SEMS = pl.BlockSpec(memory_space=pltpu.SEMAPHORE)
_DATAFLOW = pltpu.SideEffectType.DATAFLOW_SIDE_EFFECTING


def _in_hbm(a):
    return pltpu.with_memory_space_constraint(a, pltpu.HBM)


def _gather_copies(sh, land, send, recv):
    x, y, c = _coords()
    me = 2 * x + y
    out = []
    for t in range(len(sh)):
        for j, (cx, cy) in enumerate(_other_chips(x, y)):
            dev = (cx, cy, c)
            out.append((_remote(sh[t].at[c], land[t].at[me, c], send, recv, 4 * t + j, dev),
                        _remote(sh[t].at[c], land[t].at[2 * cx + cy, c], send, recv, 4 * t + j, dev)))
        sib = (x, y, 1 - c)
        out.append((_remote(sh[t], land[t].at[me], send, recv, 4 * t + 3, sib),
                    _remote(sh[t], land[t].at[me], send, recv, 4 * t + 3, sib)))
    return out


def _gather_start(shards, after, *, name):
    n = len(shards)

    def body(*refs):
        sh, land = refs[:n], refs[n:2 * n]
        send, recv = refs[2 * n + 1], refs[2 * n + 2]
        token = refs[-1]
        for mine, _ in _gather_copies(sh, land, send, recv):
            mine.start()
        token[...] = jnp.zeros_like(token)

    lands = [_in_hbm(lax.empty((N_CHIPS,) + s.shape, s.dtype)) for s in shards]
    res = pl.pallas_call(
        body, name=name, in_specs=[_HBM] * (2 * n) + [_ANY],
        out_specs=[_SEMS, _SEMS] + [_HBM] * (2 * n) + [pl.BlockSpec(memory_space=pltpu.VMEM)],
        out_shape=[pltpu.SemaphoreType.DMA((4 * n,)), pltpu.SemaphoreType.DMA((4 * n,))]
        + [pltpu.HBM(s.shape, s.dtype) for s in shards] + [pltpu.HBM(l.shape, l.dtype) for l in lands]
        + [jax.ShapeDtypeStruct((8, 128), F32)],
        input_output_aliases={t: 2 + t for t in range(2 * n)},
        compiler_params=pltpu.CompilerParams(has_side_effects=_DATAFLOW),
    )(*[_in_hbm(s) for s in shards], *lands, after)
    return res[0], res[1], res[2:2 + n], res[2 + n:2 + 2 * n], res[-1]


def _gather_wait(send, recv, shards, lands, after, *, name):
    n = len(shards)

    def body(*refs):
        sh, land = refs[:n], refs[n:2 * n]
        send_r, recv_r = refs[2 * n], refs[2 * n + 1]
        for mine, theirs in _gather_copies(sh, land, send_r, recv_r):
            mine.wait_send()
            theirs.wait_recv()

    res = pl.pallas_call(
        body, name=name, in_specs=[_HBM] * (2 * n) + [_SEMS, _SEMS, _ANY], out_specs=[_HBM] * (2 * n),
        out_shape=[pltpu.HBM(s.shape, s.dtype) for s in shards] + [pltpu.HBM(l.shape, l.dtype) for l in lands],
        input_output_aliases={t: t for t in range(2 * n)},
        compiler_params=pltpu.CompilerParams(has_side_effects=_DATAFLOW),
    )(*shards, *lands, send, recv, after)
    return res[n:]


def _gather_forward(lands, *, name):
    n = len(lands)

    def body(*refs):
        o = refs[n:2 * n]
        send, recv = refs[2 * n:]
        x, y, c = _coords()
        sib = (x, y, 1 - c)
        srcs = [2 * cx + cy for cx, cy in _other_chips(x, y)]
        cps = [_remote(o[t].at[s, c], o[t].at[s, c], send, recv, 3 * t + j, sib) for t in range(n) for j, s in enumerate(srcs)]
        for cp in cps:
            cp.start()
        for t in range(n):
            for j, s in enumerate(srcs):
                _remote(o[t].at[s, 1 - c], o[t].at[s, 1 - c], send, recv, 3 * t + j, sib).wait_recv()
        for cp in cps:
            cp.wait_send()

    return pl.pallas_call(
        body, name=name, in_specs=[_ANY] * n, out_specs=[_ANY] * n, input_output_aliases={t: t for t in range(n)},
        out_shape=[jax.ShapeDtypeStruct(l.shape, l.dtype) for l in lands],
        scratch_shapes=[pltpu.SemaphoreType.DMA((3 * n,)), pltpu.SemaphoreType.DMA((3 * n,))],
    )(*lands)


def _allreduce_small(v):
    SR = v.shape[0]

    def body(v_ref, o_ref, buf, send, recv):
        x, y, c = _coords()
        me = 4 * x + 2 * y + c
        buf[me] = v_ref[...]
        peers = []
        for k in range(1, 8):
            px = 1 - x if k & 4 else x
            py = 1 - y if k & 2 else y
            pc = 1 - c if k & 1 else c
            peers.append((px, py, pc))
        cps = [_remote(v_ref, buf.at[me], send, recv, k, p) for k, p in enumerate(peers)]
        for cp in cps:
            cp.start()
        for k, (px, py, pc) in enumerate(peers):
            _remote(v_ref, buf.at[4 * px + 2 * py + pc], send, recv, k, (px, py, pc)).wait_recv()
        for cp in cps:
            cp.wait_send()
        acc = buf[0]
        for s in range(1, 8):
            acc = acc + buf[s]
        o_ref[...] = acc

    vm = pl.BlockSpec(memory_space=pltpu.VMEM)
    return pl.pallas_call(
        body, name="allreduce_small", in_specs=[vm], out_specs=vm, out_shape=jax.ShapeDtypeStruct(v.shape, F32),
        scratch_shapes=[pltpu.VMEM((8, SR, 128), F32), pltpu.SemaphoreType.DMA((7,)), pltpu.SemaphoreType.DMA((7,))],
    )(v)


def _rs_to_sibling(gs, *, name):
    n = len(gs)

    def body(*refs):
        g, a = refs[:n], refs[n:2 * n]
        send, recv = refs[2 * n:]
        x, y, c = _coords()
        cps = [_remote(g[t].at[:, 1 - c], a[t], send, recv, t, (x, y, 1 - c)) for t in range(n)]
        for cp in cps:
            cp.start()
        for cp in cps:
            cp.wait()

    return pl.pallas_call(
        body, name=name, in_specs=[_ANY] * n, out_specs=[_ANY] * n,
        out_shape=[jax.ShapeDtypeStruct((N_CHIPS,) + g.shape[2:], g.dtype) for g in gs],
        scratch_shapes=[pltpu.SemaphoreType.DMA((n,)), pltpu.SemaphoreType.DMA((n,))],
    )(*gs)


def _rs_add_pair(g, a, c_idx, *, name):
    _, _, rh, cols = g.shape

    def body(c_ref, g_ref, a_ref, p_ref):
        p_ref[...] = (g_ref[...].astype(F32) + a_ref[...].astype(F32)).astype(BF16)

    return pl.pallas_call(
        body, name=name,
        grid_spec=pltpu.PrefetchScalarGridSpec(
            num_scalar_prefetch=1, grid=(N_CHIPS,),
            in_specs=[pl.BlockSpec((None, None, rh, cols), lambda j, c_ref: (j, c_ref[0], 0, 0)),
                      pl.BlockSpec((None, rh, cols), lambda j, c_ref: (j, 0, 0))],
            out_specs=pl.BlockSpec((None, rh, cols), lambda j, c_ref: (j, 0, 0))),
        out_shape=jax.ShapeDtypeStruct((N_CHIPS, rh, cols), BF16), compiler_params=_cp(1),
    )(c_idx, g, a)


def _chips_copies(p, r, send, recv):
    x, y, c = _coords()
    return [_remote(p[t].at[2 * cx + cy], r[t].at[k], send, recv, 3 * t + k, (cx, cy, c))
            for k, (cx, cy) in enumerate(_other_chips(x, y)) for t in range(len(p))]


def _rs_chips_start(ps, *, name):
    n = len(ps)

    def body(*refs):
        p, r = refs[:n], refs[n:2 * n]
        send, recv = refs[2 * n], refs[2 * n + 1]
        token = refs[-1]
        for cp in _chips_copies(p, r, send, recv):
            cp.start()
        token[...] = jnp.zeros_like(token)

    lands = [_in_hbm(lax.empty((3,) + p.shape[1:], p.dtype)) for p in ps]
    res = pl.pallas_call(
        body, name=name, in_specs=[_HBM] * (2 * n),
        out_specs=[_SEMS, _SEMS] + [_HBM] * (2 * n) + [pl.BlockSpec(memory_space=pltpu.VMEM)],
        out_shape=[pltpu.SemaphoreType.DMA((3 * n,)), pltpu.SemaphoreType.DMA((3 * n,))]
        + [pltpu.HBM(p.shape, p.dtype) for p in ps] + [pltpu.HBM(l.shape, l.dtype) for l in lands]
        + [jax.ShapeDtypeStruct((8, 128), F32)],
        input_output_aliases={t: 2 + t for t in range(2 * n)},
        compiler_params=pltpu.CompilerParams(has_side_effects=_DATAFLOW),
    )(*[_in_hbm(p) for p in ps], *lands)
    return res[0], res[1], res[2:2 + n], res[2 + n:2 + 2 * n], res[-1]


def _rs_chips_wait(send, recv, ps, lands, after, *, name):
    n = len(ps)

    def body(*refs):
        p, r = refs[:n], refs[n:2 * n]
        for cp in _chips_copies(p, r, refs[2 * n], refs[2 * n + 1]):
            cp.wait_send()
            cp.wait_recv()

    res = pl.pallas_call(
        body, name=name, in_specs=[_HBM] * (2 * n) + [_SEMS, _SEMS] + [_ANY] * len(after), out_specs=[_HBM] * (2 * n),
        out_shape=[pltpu.HBM(p.shape, p.dtype) for p in ps] + [pltpu.HBM(l.shape, l.dtype) for l in lands],
        input_output_aliases={t: t for t in range(2 * n)},
        compiler_params=pltpu.CompilerParams(has_side_effects=_DATAFLOW),
    )(*ps, *lands, send, recv, *after)
    return res[:n], res[n:]


def _rs_add_chips(p, r, idx, *, name):
    _, rh, cols = p.shape

    def body(idx_ref, p_ref, r0_ref, r1_ref, r2_ref, o_ref):
        o_ref[...] = ((p_ref[...].astype(F32) + r0_ref[...].astype(F32)) + r1_ref[...].astype(F32)) + r2_ref[...].astype(F32)

    def rk(k):
        return pl.BlockSpec((None, rh, cols), lambda i, idx_ref, k=k: (k, 0, 0))

    return pl.pallas_call(
        body, name=name,
        grid_spec=pltpu.PrefetchScalarGridSpec(
            num_scalar_prefetch=1, grid=(1,),
            in_specs=[pl.BlockSpec((None, rh, cols), lambda i, idx_ref: (idx_ref[0], 0, 0)), rk(0), rk(1), rk(2)],
            out_specs=pl.BlockSpec((None, rh, cols), lambda i, idx_ref: (idx_ref[1], 0, 0))),
        out_shape=jax.ShapeDtypeStruct((2, rh, cols), F32), compiler_params=_cp(1),
    )(idx, p, r, r, r)


def _rs_join_halves(hs, *, name):
    n = len(hs)

    def body(*refs):
        o = refs[n:2 * n]
        send, recv = refs[2 * n:]
        x, y, c = _coords()
        cps = [_remote(o[t].at[c], o[t].at[c], send, recv, t, (x, y, 1 - c)) for t in range(n)]
        for cp in cps:
            cp.start()
        for t in range(n):
            _remote(o[t].at[1 - c], o[t].at[1 - c], send, recv, t, (x, y, 1 - c)).wait_recv()
        for cp in cps:
            cp.wait_send()

    return pl.pallas_call(
        body, name=name, in_specs=[_ANY] * n, out_specs=[_ANY] * n,
        input_output_aliases={t: t for t in range(n)},
        out_shape=[jax.ShapeDtypeStruct(h.shape, F32) for h in hs],
        scratch_shapes=[pltpu.SemaphoreType.DMA((n,)), pltpu.SemaphoreType.DMA((n,))],
    )(*hs)


def _adamw(w, gs, m, v, *, name, dep=None):
    L, Rr, C = w.shape
    tr = _pick(Rr, (256, 128, 64, 8))
    bc1 = 1.0 - ADAM_B1 ** ADAM_STEP
    bc2 = 1.0 - ADAM_B2 ** ADAM_STEP
    nd = 0 if dep is None else 1

    def body(*refs):
        w_ref, m_ref, v_ref = refs[0], refs[1], refs[2]
        g_refs = refs[3:3 + L]
        d_ref, mo_ref, vo_ref, go_ref = refs[3 + L + nd:]
        layer = pl.program_id(0)
        gv = g_refs[0][...]
        for q in range(1, L):
            gv = jnp.where(layer == q, g_refs[q][...], gv)
        mn = ADAM_B1 * m_ref[...] + (1.0 - ADAM_B1) * gv
        vn = ADAM_B2 * v_ref[...] + (1.0 - ADAM_B2) * (gv * gv)
        go_ref[...] = gv
        mo_ref[...] = mn
        vo_ref[...] = vn
        d_ref[...] = -ADAM_LR * ((mn / bc1) / (jnp.sqrt(vn / bc2) + ADAM_EPS) + ADAM_WD * w_ref[...])

    blk = pl.BlockSpec((None, tr, C), lambda l, i: (l, i, 0))
    gblk = pl.BlockSpec((tr, C), lambda l, i: (i, 0))
    return pl.pallas_call(
        body, name=name, grid=(L, Rr // tr), in_specs=[blk] * 3 + [gblk] * L + [_ANY] * nd, out_specs=[blk] * 4,
        out_shape=[jax.ShapeDtypeStruct((L, Rr, C), F32)] * 4, compiler_params=_cp(2),
    )(w, m, v, *gs, *([] if dep is None else [dep]))


def kernel(x, positions, a_norm, a_in_proj, a_conv_w, a_conv_b, a_dt_bias, a_A_log, a_D, a_gnorm, a_out_proj,
           kv_norm, w_kv, b_kv, k_norm, b_norm, w_q, b_q, q_norm, sinks, w_o, b_o, f_norm, f_w_in, f_conv_w,
           f_conv_b, f_w_down, loss_target, m_a_norm, m_a_in_proj, m_a_conv_w, m_a_conv_b, m_a_dt_bias, m_a_A_log,
           m_a_D, m_a_gnorm, m_a_out_proj, m_kv_norm, m_w_kv, m_b_kv, m_k_norm, m_b_norm, m_w_q, m_b_q, m_q_norm,
           m_sinks, m_w_o, m_b_o, m_f_norm, m_f_w_in, m_f_conv_w, m_f_conv_b, m_f_w_down, v_a_norm, v_a_in_proj,
           v_a_conv_w, v_a_conv_b, v_a_dt_bias, v_a_A_log, v_a_D, v_a_gnorm, v_a_out_proj, v_kv_norm, v_w_kv,
           v_b_kv, v_k_norm, v_b_norm, v_w_q, v_b_q, v_q_norm, v_sinks, v_w_o, v_b_o, v_f_norm, v_f_w_in,
           v_f_conv_w, v_f_conv_b, v_f_w_down):
    wl = dict(zip(WEIGHTS, (a_norm, a_in_proj, a_conv_w, a_conv_b, a_dt_bias, a_A_log, a_D, a_gnorm, a_out_proj,
                            kv_norm, w_kv, b_kv, k_norm, b_norm, w_q, b_q, q_norm, sinks, w_o, b_o, f_norm, f_w_in,
                            f_conv_w, f_conv_b, f_w_down)))
    ml = dict(zip(WEIGHTS, (m_a_norm, m_a_in_proj, m_a_conv_w, m_a_conv_b, m_a_dt_bias, m_a_A_log, m_a_D, m_a_gnorm,
                            m_a_out_proj, m_kv_norm, m_w_kv, m_b_kv, m_k_norm, m_b_norm, m_w_q, m_b_q, m_q_norm,
                            m_sinks, m_w_o, m_b_o, m_f_norm, m_f_w_in, m_f_conv_w, m_f_conv_b, m_f_w_down)))
    vl = dict(zip(WEIGHTS, (v_a_norm, v_a_in_proj, v_a_conv_w, v_a_conv_b, v_a_dt_bias, v_a_A_log, v_a_D, v_a_gnorm,
                            v_a_out_proj, v_kv_norm, v_w_kv, v_b_kv, v_k_norm, v_b_norm, v_w_q, v_b_q, v_q_norm,
                            v_sinks, v_w_o, v_b_o, v_f_norm, v_f_w_in, v_f_conv_w, v_f_conv_b, v_f_w_down)))
    xi, yi, ci = _coords()
    me = 2 * xi + yi
    S = x.shape[1]

    def block_of(n, layer):
        t = wl[n]
        return t if layer is None else t[layer]

    rows = lambda t: t.reshape(-1, t.shape[-1])
    c_idx = jnp.reshape(ci, (1,)).astype(jnp.int32)
    me_c = jnp.stack([me, ci]).astype(jnp.int32)
    early = ("in_proj", "out_proj")
    late = tuple(name for name, _, _ in MATS if name not in early)
    shards = {name: _halves(block_of(wn, layer).astype(BF16)) for name, wn, layer in MATS}

    sp = _pack([wl[n] for n, _ in SMALL_CUT], 8, 128, F32)
    gathered, gs = _gather_weights([shards[k] for k in early], sp)
    gt = {k: t.reshape(N_CHIPS, -1, t.shape[-1]) for k, t in zip(early, gathered)}
    started = _gather_start([shards[k] for k in late], gs, name="gather_late_start")
    full = {n: wl[n] for n in SMALL_REP}
    gs = gs.reshape(N_CHIPS, -1)
    pieces = [_unpack(gs[j], [wl[n].shape for n, _ in SMALL_CUT]) for j in range(N_CHIPS)]
    for q, (n, ax) in enumerate(SMALL_CUT):
        full[n] = jnp.concatenate([pieces[j][q] for j in range(N_CHIPS)], axis=ax)
    w = _prep_small(full, {})
    w["w_zx"], w["w_dt"] = _split_in_proj(gt["in_proj"].transpose(1, 0, 2).reshape(1024, -1))
    w["a_out_proj"] = rows(gt["out_proj"])
    w["dep"] = started[4]

    class Comm:
        pending = None
        token = None
        reduced = {}

        def late_weights(self, w, after):
            lands = _gather_wait(started[0], started[1], started[2], started[3], after, name="gather_late_wait")
            lands = _gather_forward(lands, name="gather_late_forward")
            lt = {k: t.reshape(N_CHIPS, -1, t.shape[-1]) for k, t in zip(late, lands)}
            w = dict(w)
            w["w_kv"], w["w_q"], w["w_o"] = (rows(lt[k]) for k in ("w_kv", "w_q", "w_o"))
            w["f_w_in"] = [lt["f_in0"], lt["f_in1"]]
            w["f_w_down"] = [rows(lt["f_down0"]), rows(lt["f_down1"])]
            return w

        def finish(self, after):
            names, send, recv, ps, lands, tag = self.pending
            ps, rs = _rs_chips_wait(send, recv, ps, lands, after, name=f"rs_chips_wait{tag}")
            halves = [_rs_add_chips(pq, rq, me_c, name="rs_add_chips_" + k) for k, pq, rq in zip(names, ps, rs)]
            joined = _rs_join_halves(halves, name=f"rs_join_halves{tag}")
            self.reduced.update({k: rows(t) for k, t in zip(names, joined)})
            self.pending = None

        def grads(self, group, tensors, after):
            if self.pending is not None:
                self.finish([after])
            names = list(tensors)
            glist = [tensors[k].reshape(N_CHIPS, 2, -1, tensors[k].shape[-1]) for k in names]
            from_sib = _rs_to_sibling(glist, name=f"rs_to_sibling{group}")
            pairs = [_rs_add_pair(gq, aq, c_idx, name="rs_add_pair_" + k) for k, gq, aq in zip(names, glist, from_sib)]
            send, recv, ps, lands, token = _rs_chips_start(pairs, name=f"rs_chips_start{group}")
            self.pending = (names, send, recv, ps, lands, group)
            self.token = token
            return token

    comm = Comm()

    posf = positions.reshape(S, 1).astype(F32)
    loss_part, dx0, gr = _local_step(x[0], posf, loss_target[0], w, comm)
    g = _small_grads(gr)

    small_names = [n for n, _ in SMALL_CUT] + list(SMALL_REP)
    sv = _pack([g[n] for n in small_names] + [loss_part[0:1, 0:1]], 8, 128, F32)
    sred = _allreduce_small(sv).reshape(-1)
    small_shapes = [g[n].shape for n in small_names] + [(1,)]
    sg = dict(zip(small_names + ["loss"], _unpack(sred, small_shapes)))
    loss = sg["loss"].reshape(())
    g_small = {}
    for n, ax in SMALL_CUT:
        size = wl[n].shape[ax]
        g_small[n] = lax.dynamic_slice_in_dim(sg[n], me * size, size, axis=ax)
    for n in SMALL_REP:
        g_small[n] = sg[n].reshape(wl[n].shape)

    grads, delta, new_m, new_v = {}, {}, {}, {}

    def update(wn, dep):
        gl = [comm.reduced[name] for name, n2, _ in MATS if n2 == wn]
        shp = wl[wn].shape
        three = (len(gl),) + gl[0].shape
        d, mn, vn, go = _adamw(wl[wn].reshape(three), gl, ml[wn].reshape(three), vl[wn].reshape(three),
                               name="adamw_" + wn, dep=dep)
        grads[wn], delta[wn], new_m[wn], new_v[wn] = go.reshape(shp), d.reshape(shp), mn.reshape(shp), vn.reshape(shp)
        return d

    comm.finish([update(wn, comm.token) for wn in ("f_w_in", "f_w_down", "w_q", "w_o", "w_kv")])
    for wn in ("a_in_proj", "a_out_proj"):
        update(wn, None)
    pk = lambda d: _pack([d[n] for n in small_names], 8, 128, F32)[None]
    d, mn, vn, _ = _adamw(pk(wl), [pk(g_small)[0]], pk(ml), pk(vl), name="adamw_small")
    shapes = [wl[n].shape for n in small_names]
    for n, dd, mm, vv in zip(small_names, _unpack(d.reshape(-1), shapes), _unpack(mn.reshape(-1), shapes),
                             _unpack(vn.reshape(-1), shapes)):
        grads[n], delta[n], new_m[n], new_v[n] = g_small[n], dd, mm, vv

    return (loss, dx0[None], *[grads[n] for n in WEIGHTS], *[delta[n] for n in WEIGHTS],
            *[new_m[n] for n in WEIGHTS], *[new_v[n] for n in WEIGHTS])
```

```python
import math

import jax
import jax.numpy as jnp
from jax import lax
from jax.experimental import pallas as pl
from jax.experimental.pallas import tpu as pltpu

F32 = jnp.float32
BF16 = jnp.bfloat16

EPS = 1e-5
CHUNK = 256
WINDOW = 128
HEAD = 64
SSM_HEADS = 32
SSM_GROUPS = 8
SSM_STATE = 128
ATT_KV = 4
ATT_G = 4
ROPE_THETA = 10000.0
NEG = -1e30
N_CHIPS = 4
VMEM_LIMIT = 56 * 1024 * 1024

ADAM_LR, ADAM_B1, ADAM_B2, ADAM_EPS, ADAM_WD, ADAM_STEP = 0.001, 0.9, 0.999, 1e-08, 0.01, 10


def _cp(n_axes):
    return pltpu.CompilerParams(dimension_semantics=("arbitrary",) * n_axes, vmem_limit_bytes=VMEM_LIMIT)


def _pick(dim, prefs):
    for p in prefs:
        if dim % p == 0:
            return p
    return dim


def _iota(shape, dim):
    return lax.broadcasted_iota(jnp.int32, shape, dim)


def _dot(a, b, ca=1, cb=0):
    return lax.dot_general(a, b, (((ca,), (cb,)), ((), ())), preferred_element_type=F32)


def _dot3(x, ind):
    h = x.astype(BF16)
    r = x - h.astype(F32)
    m = r.astype(BF16)
    lo = (r - m.astype(F32)).astype(BF16)
    return _dot(h, ind) + _dot(m, ind) + _dot(lo, ind)


def _sigmoid(x):
    return jax.nn.sigmoid(x)


def _mm(a, b, *, name, ta=False, tb=False, bias=None, res=None, out_dtype=F32, b_koff=0, tm=None, tn=None, tk=None,
        dims=None, a_spec=None, b_spec=None, o_spec=None, o_shape=None, dep=None):
    if dims is not None:
        M, N, K = dims
    else:
        if ta:
            K, M = a.shape
        else:
            M, K = a.shape
        N = b.shape[0] if tb else b.shape[1]
    tm = tm or _pick(M, (1024, 1408, 512, 256, 128))
    tn = tn or _pick(N, (512, 1408, 256, 128))
    tk = tk or (K if K <= 2048 else _pick(K, (2048, 1408, 1024, 512)))
    assert M % tm == 0 and N % tn == 0 and K % tk == 0 and b_koff % tk == 0
    nk = K // tk
    kb0 = b_koff // tk
    has_bias, has_res = bias is not None, res is not None

    def body(*refs):
        a_ref, b_ref = refs[0], refs[1]
        pos = 2
        bias_ref = res_ref = acc_ref = None
        if has_bias:
            bias_ref = refs[pos]
            pos += 1
        if has_res:
            res_ref = refs[pos]
            pos += 1
        if dep is not None:
            pos += 1
        o_ref = refs[pos]
        if nk > 1:
            acc_ref = refs[pos + 1]
        part = _dot(a_ref[...].astype(BF16), b_ref[...].astype(BF16), 0 if ta else 1, 1 if tb else 0)

        def finish(acc):
            if has_bias:
                acc = acc + bias_ref[...]
            if has_res:
                acc = acc + res_ref[...]
            o_ref[...] = acc.astype(out_dtype)

        if nk == 1:
            finish(part)
        else:
            k = pl.program_id(2)

            @pl.when(k == 0)
            def _():
                acc_ref[...] = part

            @pl.when(k > 0)
            def _():
                acc_ref[...] += part

            @pl.when(k == nk - 1)
            def _():
                finish(acc_ref[...])

    if a_spec is None:
        a_spec = pl.BlockSpec((tk, tm), lambda i, j, k: (k, i)) if ta else pl.BlockSpec((tm, tk), lambda i, j, k: (i, k))
    if b_spec is None:
        b_spec = (pl.BlockSpec((tn, tk), lambda i, j, k: (j, k + kb0)) if tb
                  else pl.BlockSpec((tk, tn), lambda i, j, k: (k + kb0, j)))
    if o_spec is None:
        o_spec = pl.BlockSpec((tm, tn), lambda i, j, k: (i, j))
    in_specs, args = [a_spec, b_spec], [a, b]
    if has_bias:
        in_specs.append(pl.BlockSpec((1, tn), lambda i, j, k: (0, j)))
        args.append(bias)
    if has_res:
        in_specs.append(pl.BlockSpec((tm, tn), lambda i, j, k: (i, j)))
        args.append(res)
    if dep is not None:
        in_specs.append(pl.BlockSpec(memory_space=pl.ANY))
        args.append(dep)
    return pl.pallas_call(
        body, name=name, grid=(M // tm, N // tn, nk), in_specs=in_specs, out_specs=o_spec,
        out_shape=jax.ShapeDtypeStruct(o_shape or (M, N), out_dtype),
        scratch_shapes=[pltpu.VMEM((tm, tn), F32)] if nk > 1 else [],
        compiler_params=_cp(3),
    )(*args)


def _rms_fwd(x, gains, *, name, tr=256, dep=None):
    S, D = x.shape
    n = len(gains)
    nd = 0 if dep is None else 1

    def body(*refs):
        xv = refs[0][...]
        xh = xv * lax.rsqrt(jnp.mean(xv * xv, axis=-1, keepdims=True) + EPS)
        for q in range(n):
            refs[1 + n + nd + q][...] = (xh * refs[1 + q][...]).astype(BF16)

    row = pl.BlockSpec((tr, D), lambda i: (i, 0))
    vec = pl.BlockSpec((1, D), lambda i: (0, 0))
    return pl.pallas_call(
        body, name=name, grid=(S // tr,), in_specs=[row] + [vec] * n + [pl.BlockSpec(memory_space=pl.ANY)] * nd,
        out_specs=[row] * n, out_shape=[jax.ShapeDtypeStruct((S, D), BF16)] * n, compiler_params=_cp(1),
    )(x, *gains, *([] if dep is None else [dep]))


def _rms_bwd(x, gains, dhs, dres, *, name, tr=256, want_colsum=False):
    S, D = x.shape
    n = len(gains)
    steps = S // tr

    def body(*refs):
        x_ref = refs[0]
        g_refs = refs[1:1 + n]
        dh_refs = refs[1 + n:1 + 2 * n]
        dres_ref = refs[1 + 2 * n]
        dx_ref = refs[2 + 2 * n]
        dg_refs = refs[3 + 2 * n:3 + 3 * n]
        cs_ref = refs[3 + 3 * n] if want_colsum else None
        i = pl.program_id(0)
        xv = x_ref[...]
        r = lax.rsqrt(jnp.mean(xv * xv, axis=-1, keepdims=True) + EPS)
        xh = xv * r
        dx = dres_ref[...]
        for q in range(n):
            dh = dh_refs[q][...]
            dxh = dh * g_refs[q][...]
            dx = dx + r * (dxh - xh * jnp.mean(dxh * xh, axis=-1, keepdims=True))
            part = jnp.sum(dh * xh, axis=0, keepdims=True)

            @pl.when(i == 0)
            def _():
                dg_refs[q][...] = part

            @pl.when(i > 0)
            def _():
                dg_refs[q][...] += part

        dx_ref[...] = dx
        if want_colsum:
            cpart = jnp.sum(dx, axis=0, keepdims=True)

            @pl.when(i == 0)
            def _():
                cs_ref[...] = cpart

            @pl.when(i > 0)
            def _():
                cs_ref[...] += cpart

    row = pl.BlockSpec((tr, D), lambda i: (i, 0))
    vec = pl.BlockSpec((1, D), lambda i: (0, 0))
    n_vec_out = n + (1 if want_colsum else 0)
    outs = pl.pallas_call(
        body, name=name, grid=(steps,), in_specs=[row] + [vec] * n + [row] * n + [row],
        out_specs=[row] + [vec] * n_vec_out,
        out_shape=[jax.ShapeDtypeStruct((S, D), F32)] + [jax.ShapeDtypeStruct((1, D), F32)] * n_vec_out,
        compiler_params=_cp(1),
    )(x, *gains, *dhs, dres)
    return outs


def _colsum(x, *, name, tr=256):
    S, D = x.shape

    def body(x_ref, o_ref):
        i = pl.program_id(0)
        part = jnp.sum(x_ref[...].astype(F32), axis=0, keepdims=True)

        @pl.when(i == 0)
        def _():
            o_ref[...] = part

        @pl.when(i > 0)
        def _():
            o_ref[...] += part

    return pl.pallas_call(
        body, name=name, grid=(S // tr,), in_specs=[pl.BlockSpec((tr, D), lambda i: (i, 0))],
        out_specs=pl.BlockSpec((1, D), lambda i: (0, 0)), out_shape=jax.ShapeDtypeStruct((1, D), F32),
        compiler_params=_cp(1),
    )(x)


def _loss(y, t, *, name="loss", tr=256):
    S, D = y.shape
    steps = S // tr

    def body(y_ref, t_ref, dy_ref, l_ref, acc_ref):
        i = pl.program_id(0)
        e = y_ref[...] - t_ref[...]
        dy_ref[...] = e * (1.0 / D)
        part = jnp.sum(e * e, axis=0, keepdims=True)

        @pl.when(i == 0)
        def _():
            acc_ref[...] = part

        @pl.when(i > 0)
        def _():
            acc_ref[...] += part

        @pl.when(i == steps - 1)
        def _():
            tot = jnp.sum(acc_ref[...], axis=1, keepdims=True) * (0.5 / D)
            l_ref[...] = jnp.broadcast_to(tot, (8, 128))

    row = pl.BlockSpec((tr, D), lambda i: (i, 0))
    return pl.pallas_call(
        body, name=name, grid=(steps,), in_specs=[row, row],
        out_specs=[row, pl.BlockSpec((8, 128), lambda i: (0, 0))],
        out_shape=[jax.ShapeDtypeStruct((S, D), F32), jax.ShapeDtypeStruct((8, 128), F32)],
        scratch_shapes=[pltpu.VMEM((1, D), F32)], compiler_params=_cp(1),
    )(y, t)


def _conv_pre(x, w_ref, b_ref, width):
    row = _iota(x.shape, 0)
    acc = b_ref[...] + w_ref[pl.ds(width - 1, 1), :] * x
    shifted = []
    for s in range(1, width):
        xs = jnp.where(row >= s, pltpu.roll(x, s, axis=0), 0.0)
        shifted.append(xs)
        acc = acc + w_ref[pl.ds(width - 1 - s, 1), :] * xs
    return acc, shifted


def _conv_back(dacc, x, shifted, w_ref, width):
    S = x.shape[0]
    row = _iota(x.shape, 0)
    dx = w_ref[pl.ds(width - 1, 1), :] * dacc
    dws = [None] * width
    dws[width - 1] = jnp.sum(dacc * x, axis=0, keepdims=True)
    for s in range(1, width):
        back = jnp.where(row < S - s, pltpu.roll(dacc, S - s, axis=0), 0.0)
        dx = dx + w_ref[pl.ds(width - 1 - s, 1), :] * back
        dws[width - 1 - s] = jnp.sum(dacc * shifted[s - 1], axis=0, keepdims=True)
    db = jnp.sum(dacc, axis=0, keepdims=True)
    return dx, dws, db


def _conv_silu_fwd(xin, col0, C, w, b, *, name, tc=512):
    S = xin.shape[0]
    width = w.shape[0]
    off = col0 // tc

    def body(x_ref, w_ref, b_ref, o_ref):
        acc, _ = _conv_pre(x_ref[...], w_ref, b_ref, width)
        o_ref[...] = acc * _sigmoid(acc)

    return pl.pallas_call(
        body, name=name, grid=(C // tc,),
        in_specs=[pl.BlockSpec((S, tc), lambda j: (0, j + off)), pl.BlockSpec((width, tc), lambda j: (0, j)),
                  pl.BlockSpec((1, tc), lambda j: (0, j))],
        out_specs=pl.BlockSpec((S, tc), lambda j: (0, j)), out_shape=jax.ShapeDtypeStruct((S, C), F32),
        compiler_params=_cp(1),
    )(xin, w, b)


def _conv_silu_bwd(xin, col0, C, w, b, douts, *, name, tc=256):
    S = xin.shape[0]
    width = w.shape[0]
    off = col0 // tc
    nd = len(douts)
    ranges = [(o // tc, (o + d.shape[1]) // tc) for d, o in douts]

    def body(*refs):
        x_ref, w_ref, b_ref = refs[0], refs[1], refs[2]
        d_refs = refs[3:3 + nd]
        dx_ref, dw_ref, db_ref = refs[3 + nd], refs[4 + nd], refs[5 + nd]
        j = pl.program_id(0)
        x = x_ref[...]
        acc, shifted = _conv_pre(x, w_ref, b_ref, width)
        sg = _sigmoid(acc)
        dout = jnp.zeros_like(x)
        for q in range(nd):
            lo, hi = ranges[q]
            dout = dout + jnp.where((j >= lo) & (j < hi), d_refs[q][...], 0.0)
        dacc = dout * (sg * (1.0 + acc * (1.0 - sg)))
        dx, dws, db = _conv_back(dacc, x, shifted, w_ref, width)
        dx_ref[...] = dx.astype(BF16)
        for k in range(width):
            dw_ref[pl.ds(k, 1), :] = dws[k]
        db_ref[...] = db

    d_specs = [pl.BlockSpec((S, tc), (lambda j, lo=lo, hi=hi: (0, jnp.clip(j - lo, 0, hi - lo - 1)))) for lo, hi in ranges]
    return pl.pallas_call(
        body, name=name, grid=(C // tc,),
        in_specs=[pl.BlockSpec((S, tc), lambda j: (0, j + off)), pl.BlockSpec((width, tc), lambda j: (0, j)),
                  pl.BlockSpec((1, tc), lambda j: (0, j))] + d_specs,
        out_specs=[pl.BlockSpec((S, tc), lambda j: (0, j)), pl.BlockSpec((width, tc), lambda j: (0, j)),
                   pl.BlockSpec((1, tc), lambda j: (0, j))],
        out_shape=[jax.ShapeDtypeStruct((S, C), BF16), jax.ShapeDtypeStruct((width, C), F32),
                   jax.ShapeDtypeStruct((1, C), F32)],
        compiler_params=_cp(1),
    )(xin, w, b, *[d for d, _ in douts])


def _ffn_act_fwd(u, w, b, *, name, tc=256):
    S, F2 = u.shape
    Fd = F2 // 2
    width = w.shape[0]
    nb = Fd // tc

    def body(g_ref, v_ref, w_ref, b_ref, o_ref):
        acc, _ = _conv_pre(g_ref[...], w_ref, b_ref, width)
        o_ref[...] = (acc * _sigmoid(acc) * v_ref[...]).astype(BF16)

    return pl.pallas_call(
        body, name=name, grid=(nb,),
        in_specs=[pl.BlockSpec((S, tc), lambda j: (0, j)), pl.BlockSpec((S, tc), lambda j: (0, j + nb)),
                  pl.BlockSpec((width, tc), lambda j: (0, j)), pl.BlockSpec((1, tc), lambda j: (0, j))],
        out_specs=pl.BlockSpec((S, tc), lambda j: (0, j)), out_shape=jax.ShapeDtypeStruct((S, Fd), BF16),
        compiler_params=_cp(1),
    )(u, u, w, b)


def _ffn_act_bwd(u, w, b, da, *, name, tc=256):
    S, F2 = u.shape
    Fd = F2 // 2
    width = w.shape[0]
    nb = Fd // tc

    def body(g_ref, v_ref, w_ref, b_ref, da_ref, du_ref, dw_ref, db_ref, a_ref):
        x = g_ref[...]
        acc, shifted = _conv_pre(x, w_ref, b_ref, width)
        sg = _sigmoid(acc)
        dav = da_ref[...]
        val = v_ref[...]
        silu = acc * sg
        a_ref[...] = (silu * val).astype(BF16)
        du_ref[1] = (dav * silu).astype(BF16)
        dacc = dav * val * (sg * (1.0 + acc * (1.0 - sg)))
        dx, dws, db = _conv_back(dacc, x, shifted, w_ref, width)
        du_ref[0] = dx.astype(BF16)
        for k in range(width):
            dw_ref[pl.ds(k, 1), :] = dws[k]
        db_ref[...] = db

    blk = pl.BlockSpec((S, tc), lambda j: (0, j))
    return pl.pallas_call(
        body, name=name, grid=(nb,),
        in_specs=[blk, pl.BlockSpec((S, tc), lambda j: (0, j + nb)), pl.BlockSpec((width, tc), lambda j: (0, j)),
                  pl.BlockSpec((1, tc), lambda j: (0, j)), blk],
        out_specs=[pl.BlockSpec((2, S, tc), lambda j: (0, 0, j)), pl.BlockSpec((width, tc), lambda j: (0, j)),
                   pl.BlockSpec((1, tc), lambda j: (0, j)), blk],
        out_shape=[jax.ShapeDtypeStruct((2, S, Fd), BF16),
                   jax.ShapeDtypeStruct((width, Fd), F32), jax.ShapeDtypeStruct((1, Fd), F32),
                   jax.ShapeDtypeStruct((S, Fd), BF16)],
        compiler_params=_cp(1),
    )(u, u, w, b, da)


def _ssd_prep(dtr, dt_bias, a_log, *, name="ssd_prep"):
    S = dtr.shape[0]

    def body(d_ref, b_ref, al_ref, dt_ref, ac_ref, sg_ref, act_ref):
        lane = _iota((CHUNK, 128), 1)
        valid = lane < SSM_HEADS
        z = d_ref[...] + b_ref[...]
        dt = jnp.where(valid, jnp.maximum(z, 0.0) + jnp.log(1.0 + jnp.exp(-jnp.abs(z))), 0.0)
        a = dt * (-jnp.exp(al_ref[...]))
        row = _iota((CHUNK, 128), 0)
        k = 1
        while k < CHUNK:
            a = a + jnp.where(row >= k, pltpu.roll(a, k, axis=0), 0.0)
            k *= 2
        sg = jnp.where(valid, _sigmoid(z), 0.0)
        for arr, ref in ((dt, dt_ref), (a, ac_ref), (sg, sg_ref)):
            for g in range(SSM_GROUPS):
                ref[g] = jnp.where(lane < 4, arr if g == 0 else pltpu.roll(arr, 128 - 4 * g, axis=1), 0.0)
        act_ref[...] = a.T[:SSM_HEADS, :]

    blk = pl.BlockSpec((CHUNK, 128), lambda i: (i, 0))
    vec = pl.BlockSpec((1, 128), lambda i: (0, 0))
    grp = pl.BlockSpec((SSM_GROUPS, CHUNK, 128), lambda i: (0, i, 0))
    return pl.pallas_call(
        body, name=name, grid=(S // CHUNK,), in_specs=[blk, vec, vec],
        out_specs=[grp, grp, grp, pl.BlockSpec((SSM_HEADS, CHUNK), lambda i: (0, i))],
        out_shape=[jax.ShapeDtypeStruct((SSM_GROUPS, S, 128), F32)] * 3 + [jax.ShapeDtypeStruct((SSM_HEADS, S), F32)],
        compiler_params=_cp(1),
    )(dtr, dt_bias, a_log)


def _expand4(v, lanes):
    out = jnp.broadcast_to(v[:, 3:4], lanes.shape)
    for hh in (2, 1, 0):
        out = jnp.where(lanes < 64 * (hh + 1), v[:, hh:hh + 1], out)
    return out


def _ssd_fwd(xbc, dt_g, ac_g, ac_t, *, name="ssd_fwd"):
    S = xbc.shape[0]
    nc = S // CHUNK
    Lc = CHUNK

    def body(x_ref, b_ref, c_ref, dt_ref, ac_ref, act_ref, y_ref, st_out_ref, st_ref):
        g = pl.program_id(0)
        c = pl.program_id(1)

        @pl.when(c == 0)
        def _():
            st_ref[...] = jnp.zeros_like(st_ref)

        bv = b_ref[...]
        cbf = c_ref[...].astype(BF16)
        cb = _dot(cbf, bv.astype(BF16), 1, 1)
        causal = _iota((Lc, Lc), 0) >= _iota((Lc, Lc), 1)
        lane256 = _iota((Lc, 256), 1)
        lane128 = _iota((Lc, 128), 1)
        row128 = _iota((128, 128), 0)
        dtg, acg = dt_ref[...], ac_ref[...]
        ac_last = ac_ref[pl.ds(Lc - 1, 1), :]
        dt4 = _expand4(dtg, lane256)
        ac4 = _expand4(acg, lane256)
        e4 = jnp.exp(ac4)
        xdb = (x_ref[...] * dt4).astype(BF16)
        st_out_ref[...] = st_ref[...]
        for p in range(2):
            xd_p = xdb[:, 128 * p:128 * (p + 1)]
            st_p = st_ref[p]
            ys, sn, cds = [], [], []
            for q in range(2):
                hh = 2 * p + q
                a_col = acg[:, hh:hh + 1]
                a_row = act_ref[pl.ds(4 * g + hh, 1), :]
                dec = jnp.exp(jnp.where(causal, a_col - a_row, NEG))
                w = (cb * dec).astype(BF16)
                ys.append(_dot(w, xd_p))
                al = ac_last[:, hh:hh + 1]
                dte = jnp.exp(al - a_col)
                sn.append(_dot(xd_p, (bv * dte).astype(BF16), 0, 0))
                cds.append(jnp.exp(al))
            y_diag = jnp.where(lane128 < 64, ys[0], ys[1])
            y_off = _dot(cbf, st_p.astype(BF16), 1, 1) * e4[:, 128 * p:128 * (p + 1)]
            y_ref[:, 128 * p:128 * (p + 1)] = y_diag + y_off
            st_ref[p] = jnp.where(row128 < 64, st_p * cds[0] + sn[0], st_p * cds[1] + sn[1])

    per_g = lambda g, c: (g, c, 0)
    return pl.pallas_call(
        body, name=name, grid=(SSM_GROUPS, nc),
        in_specs=[pl.BlockSpec((Lc, 256), lambda g, c: (c, g)),
                  pl.BlockSpec((Lc, 128), lambda g, c: (c, 16 + g)),
                  pl.BlockSpec((Lc, 128), lambda g, c: (c, 24 + g)),
                  pl.BlockSpec((None, Lc, 128), per_g), pl.BlockSpec((None, Lc, 128), per_g),
                  pl.BlockSpec((SSM_HEADS, Lc), lambda g, c: (0, c))],
        out_specs=[pl.BlockSpec((Lc, 256), lambda g, c: (c, g)),
                   pl.BlockSpec((None, None, 2, 128, 128), lambda g, c: (g, c, 0, 0, 0))],
        out_shape=[jax.ShapeDtypeStruct((S, 2048), F32), jax.ShapeDtypeStruct((SSM_GROUPS, nc, 2, 128, 128), F32)],
        scratch_shapes=[pltpu.VMEM((2, 128, 128), F32)], compiler_params=_cp(2),
    )(xbc, xbc, xbc, dt_g, ac_g, ac_t)


def _ssd_bwd(xbc, dt_g, ac_g, ac_t, states, dy, dexp, *, name="ssd_bwd"):
    S = xbc.shape[0]
    nc = S // CHUNK
    Lc = CHUNK

    def body(x_ref, b_ref, c_ref, dt_ref, ac_ref, act_ref, st_ref, dy_ref, d_ref, dx_ref, db_ref, dc_ref, dh_ref, ds_ref):
        g = pl.program_id(0)
        cc = pl.program_id(1)

        @pl.when(cc == 0)
        def _():
            ds_ref[...] = jnp.zeros_like(ds_ref)

        bv = b_ref[...]
        cv = c_ref[...]
        bbf, cbf = bv.astype(BF16), cv.astype(BF16)
        cb = _dot(cbf, bbf, 1, 1)
        causal = _iota((Lc, Lc), 0) >= _iota((Lc, Lc), 1)
        lane256 = _iota((Lc, 256), 1)
        lane128 = _iota((Lc, 128), 1)
        row128 = _iota((128, 128), 0)
        dtg, acg = dt_ref[...], ac_ref[...]
        ac_last = ac_ref[pl.ds(Lc - 1, 1), :]
        dt4 = _expand4(dtg, lane256)
        ac4 = _expand4(acg, lane256)
        acl4 = _expand4(ac_last, _iota((1, 256), 1))
        e4 = jnp.exp(ac4)
        dte4 = jnp.exp(acl4 - ac4)
        xv = x_ref[...]
        xd = xv * dt4
        xdb = xd.astype(BF16)
        dyv = dy_ref[...]
        dcb = jnp.zeros((Lc, Lc), F32)
        dc_acc = jnp.zeros((Lc, 128), F32)
        db_acc = jnp.zeros((Lc, 128), F32)
        ind_rows = _iota((256, 128), 0) >> 6
        ind_cols = _iota((256, 128), 1)
        ind_a = (ind_rows == ind_cols).astype(BF16)
        ind_b = (ind_rows + 4 == ind_cols).astype(BF16)
        u_parts, dxd_parts, ends = [], [], []
        for p in range(2):
            sl = slice(128 * p, 128 * (p + 1))
            xd_p, xdb_p, dy_p = xd[:, sl], xdb[:, sl], dyv[:, sl]
            dyb_p = dy_p.astype(BF16)
            e_p, dte_p = e4[:, sl], dte4[:, sl]
            sp = st_ref[p]
            spb = sp.astype(BF16)
            dsn = ds_ref[p]
            dsnb = dsn.astype(BF16)
            yds, dxds, cds = [], [], []
            for q in range(2):
                hh = 2 * p + q
                a_col = acg[:, hh:hh + 1]
                a_row = act_ref[pl.ds(4 * g + hh, 1), :]
                dec = jnp.exp(jnp.where(causal, a_col - a_row, NEG))
                w = (cb * dec).astype(BF16)
                head = (lane128 < 64) if q == 0 else (lane128 >= 64)
                dym = jnp.where(head, dyb_p, jnp.zeros_like(dyb_p))
                dw = _dot(dym, xdb_p, 1, 1)
                dcb = dcb + dw * dec
                yds.append(_dot(w, xdb_p))
                dxds.append(_dot(w, dyb_p, 0, 0))
                cds.append(jnp.exp(ac_last[:, hh:hh + 1]))
            y_diag = jnp.where(lane128 < 64, yds[0], yds[1])
            dxd_diag = jnp.where(lane128 < 64, dxds[0], dxds[1])
            y_off = _dot(cbf, spb, 1, 1) * e_p
            dgp = dy_p * e_p
            dgb = dgp.astype(BF16)
            dc_acc = dc_acc + _dot(dgb, spb)
            dsp = _dot(dgb, cbf, 0, 0)
            cd_col = jnp.where(row128[:, 0:1] < 64, cds[0], cds[1])
            qm = _dot(bbf, dsnb, 1, 1)
            dxd_state = dte_p * qm
            db_acc = db_acc + _dot((xd_p * dte_p).astype(BF16), dsnb)
            t_p = xd_p * dxd_state
            prod = dsn * sp
            e0 = jnp.sum(jnp.sum(jnp.where(row128 < 64, prod, 0.0), axis=1, keepdims=True), axis=0, keepdims=True)
            e1 = jnp.sum(jnp.sum(jnp.where(row128 >= 64, prod, 0.0), axis=1, keepdims=True), axis=0, keepdims=True)
            tcol = jnp.sum(t_p, axis=0, keepdims=True)
            lane1 = _iota((1, 128), 1)
            t0 = jnp.sum(jnp.where(lane1 < 64, tcol, 0.0), axis=1, keepdims=True)
            t1 = jnp.sum(jnp.where(lane1 >= 64, tcol, 0.0), axis=1, keepdims=True)
            ends.append(e0 * cds[0] + t0)
            ends.append(e1 * cds[1] + t1)
            ds_ref[p] = dsn * cd_col + dsp
            u_parts.append(dyb_p.astype(F32) * y_diag - xdb_p.astype(F32) * dxd_diag + dy_p * y_off - t_p)
            dxd_parts.append(dxd_diag + dxd_state)
        dxd = jnp.concatenate(dxd_parts, axis=1)
        u_all = jnp.concatenate(u_parts, axis=1)
        dx_ref[...] = dxd * dt4 + dyv * d_ref[...]
        dcbb = dcb.astype(BF16)
        dc_ref[...] = dc_acc + _dot(dcbb, bbf)
        db_ref[...] = db_acc + _dot(dcbb, cbf, 0, 0)
        lane = _iota((Lc, 128), 1)
        endv = jnp.zeros((Lc, 128), F32)
        for hh in range(4):
            endv = jnp.where(lane == 8 + hh, ends[hh], endv)
        dh_ref[...] = _dot3(dxd * xv, ind_a) + _dot3(u_all, ind_b) + endv

    rev = lambda c: nc - 1 - c
    per_g = lambda g, c: (g, rev(c), 0)
    return pl.pallas_call(
        body, name=name, grid=(SSM_GROUPS, nc),
        in_specs=[pl.BlockSpec((Lc, 256), lambda g, c: (rev(c), g)),
                  pl.BlockSpec((Lc, 128), lambda g, c: (rev(c), 16 + g)),
                  pl.BlockSpec((Lc, 128), lambda g, c: (rev(c), 24 + g)),
                  pl.BlockSpec((None, Lc, 128), per_g), pl.BlockSpec((None, Lc, 128), per_g),
                  pl.BlockSpec((SSM_HEADS, Lc), lambda g, c: (0, rev(c))),
                  pl.BlockSpec((None, None, 2, 128, 128), lambda g, c: (g, rev(c), 0, 0, 0)),
                  pl.BlockSpec((Lc, 256), lambda g, c: (rev(c), g)),
                  pl.BlockSpec((1, 256), lambda g, c: (0, g))],
        out_specs=[pl.BlockSpec((Lc, 256), lambda g, c: (rev(c), g)),
                   pl.BlockSpec((Lc, 128), lambda g, c: (rev(c), g)),
                   pl.BlockSpec((Lc, 128), lambda g, c: (rev(c), g)),
                   pl.BlockSpec((None, Lc, 128), per_g)],
        out_shape=[jax.ShapeDtypeStruct((S, 2048), F32), jax.ShapeDtypeStruct((S, 1024), F32),
                   jax.ShapeDtypeStruct((S, 1024), F32), jax.ShapeDtypeStruct((SSM_GROUPS, S, 128), F32)],
        scratch_shapes=[pltpu.VMEM((2, 128, 128), F32)], compiler_params=_cp(2),
    )(xbc, xbc, xbc, dt_g, ac_g, ac_t, states, dy, dexp)


def _ssd_post(dhead, dt_g, sg_g, alog_g, *, name="ssd_post"):
    S = dhead.shape[1]
    nc = S // CHUNK
    Lc = CHUNK

    def body(dh_ref, dt_ref, sg_ref, al_ref, o_ref, s_ref):
        @pl.when(pl.program_id(0) == 0)
        def _():
            s_ref[...] = jnp.zeros_like(s_ref)

        lane = _iota((Lc, 128), 1)
        row = _iota((Lc, 128), 0)
        row8 = _iota((8, 128), 0)
        out = jnp.zeros((Lc, 128), F32)
        for g in range(SSM_GROUPS):
            dh = dh_ref[g]
            a_neg = -jnp.exp(al_ref[g])
            dac = jnp.where(lane < 4, pltpu.roll(dh, 124, axis=1), 0.0)
            end = jnp.where(lane < 4, pltpu.roll(dh, 120, axis=1), 0.0)
            k = 1
            while k < Lc:
                dac = dac + jnp.where(row < Lc - k, pltpu.roll(dac, Lc - k, axis=0), 0.0)
                k *= 2
            da = dac + end
            ddt = jnp.where(lane < 4, da * a_neg + dh, 0.0)
            ddtr = ddt * sg_ref[g]
            out = out + (ddtr if g == 0 else pltpu.roll(ddtr, 4 * g, axis=1))
            dal = jnp.sum(da * dt_ref[g], axis=0, keepdims=True) * a_neg
            dbias = jnp.sum(ddtr, axis=0, keepdims=True)
            part = jnp.where(row8 == 0, dal, jnp.where(row8 == 1, dbias, 0.0))
            s_ref[g] += part
        o_ref[...] = out.astype(BF16)

    grp = pl.BlockSpec((SSM_GROUPS, Lc, 128), lambda c: (0, c, 0))
    whole = lambda r: pl.BlockSpec((SSM_GROUPS, r, 128), lambda c: (0, 0, 0))
    return pl.pallas_call(
        body, name=name, grid=(nc,), in_specs=[grp, grp, grp, whole(1)],
        out_specs=[pl.BlockSpec((Lc, 128), lambda c: (c, 0)), whole(8)],
        out_shape=[jax.ShapeDtypeStruct((S, 128), BF16), jax.ShapeDtypeStruct((SSM_GROUPS, 8, 128), F32)],
        compiler_params=_cp(1),
    )(dhead, dt_g, sg_g, alog_g)


def _gate_fwd(y, xbc, zx, dexp, gn, *, name="gate_fwd", tr=256):
    S = y.shape[0]
    W = 2048
    gw = W // SSM_GROUPS

    def body(y_ref, x_ref, z_ref, d_ref, g_ref, o_ref):
        z = z_ref[...]
        u = (y_ref[...] + x_ref[...] * d_ref[...]) * (z * _sigmoid(z))
        gv = g_ref[...]
        for q in range(SSM_GROUPS):
            sl = slice(gw * q, gw * (q + 1))
            uq = u[:, sl]
            r = lax.rsqrt(jnp.mean(uq * uq, axis=-1, keepdims=True) + EPS)
            o_ref[:, sl] = (uq * r * gv[:, sl]).astype(BF16)

    row = pl.BlockSpec((tr, W), lambda i: (i, 0))
    vec = pl.BlockSpec((1, W), lambda i: (0, 0))
    return pl.pallas_call(
        body, name=name, grid=(S // tr,), in_specs=[row, row, row, vec, vec], out_specs=row,
        out_shape=jax.ShapeDtypeStruct((S, W), BF16), compiler_params=_cp(1),
    )(y, xbc, zx, dexp, gn)


def _gate_bwd(y, xbc, zx, dexp, gn, dout, *, name="gate_bwd", tr=256):
    S = y.shape[0]
    W = 2048
    gw = W // SSM_GROUPS
    steps = S // tr

    def body(y_ref, x_ref, z_ref, d_ref, g_ref, do_ref, dy_ref, dz_ref, dg_ref, dd_ref, acc_ref):
        i = pl.program_id(0)

        @pl.when(i == 0)
        def _():
            acc_ref[...] = jnp.zeros_like(acc_ref)

        z = z_ref[...]
        sg = _sigmoid(z)
        sz = z * sg
        xs = x_ref[...]
        yt = y_ref[...] + xs * d_ref[...]
        u = yt * sz
        gv = g_ref[...]
        do = do_ref[...]
        dgs = []
        for q in range(SSM_GROUPS):
            sl = slice(gw * q, gw * (q + 1))
            uq = u[:, sl]
            r = lax.rsqrt(jnp.mean(uq * uq, axis=-1, keepdims=True) + EPS)
            uh = uq * r
            dq = do[:, sl]
            duh = dq * gv[:, sl]
            duq = r * (duh - uh * jnp.mean(duh * uh, axis=-1, keepdims=True))
            dgs.append(jnp.sum(dq * uh, axis=0, keepdims=True))
            dyt = duq * sz[:, sl]
            dy_ref[:, sl] = dyt
            dz_ref[:, sl] = (duq * yt[:, sl] * (sg[:, sl] * (1.0 + z[:, sl] * (1.0 - sg[:, sl])))).astype(BF16)
            acc_ref[:, sl] += jnp.sum(dyt * xs[:, sl], axis=0, keepdims=True)
        dg = jnp.concatenate(dgs, axis=1)

        @pl.when(i == 0)
        def _():
            dg_ref[...] = dg

        @pl.when(i > 0)
        def _():
            dg_ref[...] += dg

        @pl.when(i == steps - 1)
        def _():
            ind = ((_iota((W, 128), 0) >> 6) == _iota((W, 128), 1)).astype(BF16)
            dd_ref[...] = _dot3(jnp.broadcast_to(acc_ref[...], (8, W)), ind)[0:1, :]

    row = pl.BlockSpec((tr, W), lambda i: (i, 0))
    vec = pl.BlockSpec((1, W), lambda i: (0, 0))
    return pl.pallas_call(
        body, name=name, grid=(steps,), in_specs=[row, row, row, vec, vec, row],
        out_specs=[row, row, vec, pl.BlockSpec((1, 128), lambda i: (0, 0))],
        out_shape=[jax.ShapeDtypeStruct((S, W), F32), jax.ShapeDtypeStruct((S, W), BF16),
                   jax.ShapeDtypeStruct((1, W), F32), jax.ShapeDtypeStruct((1, 128), F32)],
        scratch_shapes=[pltpu.VMEM((1, W), F32)], compiler_params=_cp(1),
    )(y, xbc, zx, dexp, gn, dout)


def _rope_cs(posf, *, name="rope_tables", tr=256):
    S = posf.shape[0]

    def body(p_ref, c_ref, s_ref):
        j = (_iota((tr, 128), 1) & 31).astype(F32)
        ang = p_ref[...] * jnp.exp(j * (-math.log(ROPE_THETA) / 32.0))
        c_ref[...] = jnp.cos(ang)
        s_ref[...] = jnp.sin(ang)

    blk = pl.BlockSpec((tr, 128), lambda i: (i, 0))
    return pl.pallas_call(
        body, name=name, grid=(S // tr,), in_specs=[pl.BlockSpec((tr, 1), lambda i: (i, 0))], out_specs=[blk, blk],
        out_shape=[jax.ShapeDtypeStruct((S, 128), F32)] * 2, compiler_params=_cp(1),
    )(posf)


def _rope_tables(c_ref, s_ref, shape):
    reps = shape[1] // 128
    return jnp.tile(c_ref[...], (1, reps)), jnp.tile(s_ref[...], (1, reps)), (_iota(shape, 1) & 63) < 32


def _hn_inds(W):
    ind = ((_iota((W, 128), 0) >> 6) == _iota((W, 128), 1)).astype(BF16)
    ind_t = ((_iota((128, W), 1) >> 6) == _iota((128, W), 0)).astype(BF16)
    return ind, ind_t


def _hnrope_fwd(xin, col0, W, gain_w, rope, *, name, tr=256):
    S = xin.shape[0]
    off = col0 // W
    nh = W // HEAD

    def body(x_ref, g_ref, c_ref, s_ref, o_ref):
        x = x_ref[...]
        ind, ind_t = _hn_inds(W)
        r = lax.rsqrt(_dot3(x * x, ind) * (1.0 / HEAD) + EPS)
        xn = x * _dot3(r, ind_t) * g_ref[...]
        cs, sn, half = _rope_tables(c_ref, s_ref, (tr, W))
        rot = jnp.where(half, -pltpu.roll(xn, W - 32, axis=1), pltpu.roll(xn, 32, axis=1))
        out = (xn * cs + rot * sn).astype(BF16)
        for h in range(nh):
            o_ref[h] = out[:, HEAD * h:HEAD * (h + 1)]

    tab = pl.BlockSpec((tr, 128), lambda i: (i, 0))
    return pl.pallas_call(
        body, name=name, grid=(S // tr,),
        in_specs=[pl.BlockSpec((tr, W), lambda i: (i, off)), pl.BlockSpec((1, W), lambda i: (0, 0)), tab, tab],
        out_specs=pl.BlockSpec((nh, tr, HEAD), lambda i: (0, i, 0)), out_shape=jax.ShapeDtypeStruct((nh, S, HEAD), BF16),
        compiler_params=_cp(1),
    )(xin, gain_w, *rope)


def _hnrope_bwd(xin, col0, W, gain_w, rope, dout, *, name, tr=256):
    S = xin.shape[0]
    off = col0 // W
    steps = S // tr
    nh = W // HEAD

    def body(x_ref, g_ref, c_ref, s_ref, do_ref, dx_ref, cs_ref, dg_ref, acc_ref):
        i = pl.program_id(0)
        x = x_ref[...]
        ind, ind_t = _hn_inds(W)
        r = lax.rsqrt(_dot3(x * x, ind) * (1.0 / HEAD) + EPS)
        rw = _dot3(r, ind_t)
        xh = x * rw
        cs, sn, half = _rope_tables(c_ref, s_ref, (tr, W))
        do = jnp.concatenate([do_ref[h] for h in range(nh)], axis=1).astype(F32)
        gs = do * sn
        g1 = do * cs + jnp.where(half, pltpu.roll(gs, W - 32, axis=1), -pltpu.roll(gs, 32, axis=1))
        dxh = g1 * g_ref[...]
        t = _dot3(dxh * xh, ind) * (1.0 / HEAD)
        dx = rw * (dxh - xh * _dot3(t, ind_t))
        dx_ref[...] = dx.astype(BF16)
        cpart = jnp.sum(dx, axis=0, keepdims=True)
        gpart = jnp.sum(g1 * xh, axis=0, keepdims=True)

        @pl.when(i == 0)
        def _():
            cs_ref[...] = cpart
            acc_ref[...] = gpart

        @pl.when(i > 0)
        def _():
            cs_ref[...] += cpart
            acc_ref[...] += gpart

        @pl.when(i == steps - 1)
        def _():
            fold = ((_iota((W, 128), 0) & 63) == _iota((W, 128), 1)).astype(BF16)
            dg_ref[...] = _dot3(jnp.broadcast_to(acc_ref[...], (8, W)), fold)[0:1, :]

    tab = pl.BlockSpec((tr, 128), lambda i: (i, 0))
    return pl.pallas_call(
        body, name=name, grid=(steps,),
        in_specs=[pl.BlockSpec((tr, W), lambda i: (i, off)), pl.BlockSpec((1, W), lambda i: (0, 0)), tab, tab,
                  pl.BlockSpec((nh, tr, HEAD), lambda i: (0, i, 0))],
        out_specs=[pl.BlockSpec((tr, W), lambda i: (i, 0)), pl.BlockSpec((1, W), lambda i: (0, 0)),
                   pl.BlockSpec((1, 128), lambda i: (0, 0))],
        out_shape=[jax.ShapeDtypeStruct((S, W), BF16), jax.ShapeDtypeStruct((1, W), F32),
                   jax.ShapeDtypeStruct((1, 128), F32)],
        scratch_shapes=[pltpu.VMEM((1, W), F32)], compiler_params=_cp(1),
    )(xin, gain_w, *rope, dout)


def _attn_probs(q, kb, sink_ref, h, i):
    s = _dot(q, kb, 1, 1) * (HEAD ** -0.5)
    r = _iota((4 * WINDOW, 2 * WINDOW), 0)
    ki = _iota((4 * WINDOW, 2 * WINDOW), 1)
    rel = (r & (WINDOW - 1)) + WINDOW - ki
    mask = (rel >= 0) & (rel < WINDOW) & ((ki >= WINDOW) | (i > 0))
    s = jnp.where(mask, s, NEG)
    r1 = _iota((4 * WINDOW, 1), 0)
    sink = jnp.where(r1 < WINDOW, sink_ref[4 * h], jnp.where(r1 < 2 * WINDOW, sink_ref[4 * h + 1],
                     jnp.where(r1 < 3 * WINDOW, sink_ref[4 * h + 2], sink_ref[4 * h + 3])))
    m = jnp.maximum(jnp.max(s, axis=1, keepdims=True), sink)
    p = jnp.exp(s - m)
    ps = jnp.exp(sink - m)
    inv = 1.0 / (jnp.sum(p, axis=1, keepdims=True) + ps)
    return p * inv, ps * inv


def _attn_specs(S):
    qspec = pl.BlockSpec((None, ATT_G, WINDOW, HEAD), lambda h, i: (h, 0, i, 0))
    cur = pl.BlockSpec((None, WINDOW, HEAD), lambda h, i: (h, i, 0))
    prev = pl.BlockSpec((None, WINDOW, HEAD), lambda h, i: (h, jnp.maximum(i - 1, 0), 0))
    return qspec, cur, prev


def _attn_fwd(qh, kh, vh, sinks, *, name="attn_fwd"):
    S = kh.shape[1]
    nb = S // WINDOW

    def body(s_ref, q_ref, kc_ref, kp_ref, vc_ref, vp_ref, o_ref):
        h, i = pl.program_id(0), pl.program_id(1)
        q = q_ref[...].reshape(ATT_G * WINDOW, HEAD)
        kb = jnp.concatenate([kp_ref[...], kc_ref[...]], axis=0)
        vb = jnp.concatenate([vp_ref[...], vc_ref[...]], axis=0)
        probs, _ = _attn_probs(q, kb, s_ref, h, i)
        o = _dot(probs.astype(BF16), vb).astype(BF16)
        o_ref[...] = jnp.concatenate([o[WINDOW * g:WINDOW * (g + 1)] for g in range(ATT_G)], axis=1)

    qspec, cur, prev = _attn_specs(S)
    return pl.pallas_call(
        body, name=name, grid=(ATT_KV, nb),
        in_specs=[pl.BlockSpec(memory_space=pltpu.SMEM), qspec, cur, prev, cur, prev],
        out_specs=pl.BlockSpec((WINDOW, ATT_G * HEAD), lambda h, i: (i, h)),
        out_shape=jax.ShapeDtypeStruct((S, ATT_KV * ATT_G * HEAD), BF16), compiler_params=_cp(2),
    )(sinks, qh, kh, kh, vh, vh)


def _attn_bwd(qh, kh, vh, sinks, doh, *, name="attn_bwd"):
    S = kh.shape[1]
    nb = S // WINDOW

    def body(s_ref, q_ref, kc_ref, kp_ref, vc_ref, vp_ref, do_ref, dq_ref, dk_ref, dv_ref, dsk_ref):
        h, i = pl.program_id(0), pl.program_id(1)

        @pl.when(i == 0)
        def _():
            dk_ref[...] = jnp.zeros_like(dk_ref)
            dv_ref[...] = jnp.zeros_like(dv_ref)
            dsk_ref[...] = jnp.zeros_like(dsk_ref)

        q = q_ref[...].reshape(ATT_G * WINDOW, HEAD)
        dov = do_ref[...]
        do = jnp.concatenate([dov[:, HEAD * g:HEAD * (g + 1)] for g in range(ATT_G)], axis=0)
        kb = jnp.concatenate([kp_ref[...], kc_ref[...]], axis=0)
        vb = jnp.concatenate([vp_ref[...], vc_ref[...]], axis=0)
        probs, psink = _attn_probs(q, kb, s_ref, h, i)
        dp = _dot(do, vb, 1, 1)
        delta = jnp.sum(probs * dp, axis=1, keepdims=True)
        ds = (probs * (dp - delta)).astype(BF16)
        scale = HEAD ** -0.5
        dq_ref[...] = (_dot(ds, kb) * scale).reshape(ATT_G, WINDOW, HEAD)
        dkb = _dot(ds, q, 0, 0) * scale
        dvb = _dot(probs.astype(BF16), do, 0, 0)
        cur = pl.multiple_of(i * WINDOW, WINDOW)
        dk_ref[pl.ds(cur, WINDOW), :] += dkb[WINDOW:, :]
        dv_ref[pl.ds(cur, WINDOW), :] += dvb[WINDOW:, :]

        @pl.when(i > 0)
        def _():
            prv = pl.multiple_of((i - 1) * WINDOW, WINDOW)
            dk_ref[pl.ds(prv, WINDOW), :] += dkb[:WINDOW, :]
            dv_ref[pl.ds(prv, WINDOW), :] += dvb[:WINDOW, :]

        dsr = -psink * delta
        lane = _iota((8, 128), 1)
        row = _iota((8, 128), 0)
        upd = jnp.zeros((8, 128), F32)
        for gq in range(ATT_G):
            v = jnp.sum(dsr[gq * WINDOW:(gq + 1) * WINDOW, :], axis=0, keepdims=True)
            upd = jnp.where((lane == gq) & (row == 0), v, upd)
        dsk_ref[...] += upd

    qspec, cur, prev = _attn_specs(S)
    full = pl.BlockSpec((None, S, HEAD), lambda h, i: (h, 0, 0))
    return pl.pallas_call(
        body, name=name, grid=(ATT_KV, nb),
        in_specs=[pl.BlockSpec(memory_space=pltpu.SMEM), qspec, cur, prev, cur, prev,
                  pl.BlockSpec((WINDOW, ATT_G * HEAD), lambda h, i: (i, h))],
        out_specs=[qspec, full, full, pl.BlockSpec((None, 8, 128), lambda h, i: (h, 0, 0))],
        out_shape=[jax.ShapeDtypeStruct((ATT_KV, ATT_G, S, HEAD), F32), jax.ShapeDtypeStruct((ATT_KV, S, HEAD), F32),
                   jax.ShapeDtypeStruct((ATT_KV, S, HEAD), F32), jax.ShapeDtypeStruct((ATT_KV, 8, 128), F32)],
        compiler_params=_cp(2),
    )(sinks, qh, kh, kh, vh, vh, doh)


def _heads_major(t, nh):
    S = t.shape[0]
    return t.reshape(S, nh, HEAD).transpose(1, 0, 2)


def _tokens_major(t):
    nh, S, _ = t.shape
    return t.transpose(1, 0, 2).reshape(S, nh * HEAD)


class _NoComm:
    def late_weights(self, w, after):
        return w

    def grads(self, group, tensors, after):
        return None


def _local_step(x, posf, target, w, comm=None):
    S, D = x.shape
    gr = {}
    comm = comm or _NoComm()

    (h1,) = _rms_fwd(x, [w["a_norm"]], name="a_norm_f", dep=w.get("dep"))
    zx = _mm(h1, w["w_zx"], name="in_proj_zx")
    dtr = _mm(h1, w["w_dt"], name="in_proj_dt")
    xbc = _conv_silu_fwd(zx, 2048, 4096, w["a_conv_w"], w["a_conv_b"], name="a_conv_f")
    dt_g, ac_g, sg_g, ac_t = _ssd_prep(dtr, w["a_dt_bias"], w["a_A_log"])
    y_ssd, states = _ssd_fwd(xbc, dt_g, ac_g, ac_t)
    yg = _gate_fwd(y_ssd, xbc, zx, w["a_Dexp"], w["a_gnorm"])
    x1 = _mm(yg, w["a_out_proj"], res=x, name="out_proj")

    w = comm.late_weights(w, x1)
    FW = w["f_w_in"][0].shape[2]

    def ffn_fwd(xin, l):
        (h,) = _rms_fwd(xin, [w["f_norm"][l]], name=f"f_norm_f{l}")
        u = _mm(h, w["f_w_in"][l], name=f"f_in{l}", dims=(S, N_CHIPS * FW, D), tn=FW,
                b_spec=pl.BlockSpec((None, D, FW), lambda i, j, k: (j, 0, 0)))
        a = _ffn_act_fwd(u, w["f_conv_w"][l], w["f_conv_b"][l], name=f"f_act_f{l}")
        xo = _mm(a, w["f_w_down"][l], res=xin, name=f"f_down{l}")
        return xo, (h, u)

    x2, ffn0 = ffn_fwd(x1, 0)

    hk, hq = _rms_fwd(x2, [w["kv_norm"], w["b_norm"]], name="kvq_norm_f")
    kv = _mm(hk, w["w_kv"], bias=w["b_kv"], name="kv_proj")
    q = _mm(hq, w["w_q"], bias=w["b_q"], name="q_proj")
    rope = _rope_cs(posf)
    kr = _hnrope_fwd(kv, 0, 256, w["k_norm_w"], rope, name="k_rope_f")
    qr = _hnrope_fwd(q, 0, 1024, w["q_norm_w"], rope, name="q_rope_f")
    qh = qr.reshape(ATT_KV, ATT_G, S, HEAD)
    kh = kr
    vh = _heads_major(kv[:, 256:].astype(BF16), ATT_KV)
    att = _attn_fwd(qh, kh, vh, w["sinks"])
    x3 = _mm(att, w["w_o"], bias=w["b_o"], res=x2, name="o_proj")
    x4, ffn1 = ffn_fwd(x3, 1)

    dy, loss_part = _loss(x4, target)

    def ffn_bwd(xin, l, saved, dyo, want_colsum, dep=None):
        h, u = saved
        da = _mm(dyo, w["f_w_down"][l], tb=True, name=f"f_down_dx{l}", dep=dep)
        du, dcw, dcb, a = _ffn_act_bwd(u, w["f_conv_w"][l], w["f_conv_b"][l], da, name=f"f_act_b{l}")
        dw_down = _mm(a, dyo, ta=True, out_dtype=BF16, name=f"f_down_dw{l}")
        dw_in = _mm(h, du, ta=True, out_dtype=BF16, name=f"f_in_dw{l}", dims=(D, N_CHIPS * FW, S), tm=D, tn=FW, tk=S,
                    b_spec=pl.BlockSpec((None, S, FW), lambda i, j, k: (j // 2, 0, j % 2)),
                    o_spec=pl.BlockSpec((None, D, FW), lambda i, j, k: (j, i, 0)), o_shape=(N_CHIPS, D, FW))
        ts = _pick(S, (1024, 512, 256))
        dh = _mm(du, w["f_w_in"][l], tb=True, name=f"f_in_dx{l}", dims=(S, D, N_CHIPS * FW), tm=ts, tn=512, tk=FW,
                 a_spec=pl.BlockSpec((None, ts, FW), lambda i, j, k: (k // 2, i, k % 2)),
                 b_spec=pl.BlockSpec((None, 512, FW), lambda i, j, k: (k, j, 0)))
        outs = _rms_bwd(xin, [w["f_norm"][l]], [dh], dyo, name=f"f_norm_b{l}", want_colsum=want_colsum)
        g = dict(f_norm=outs[1], f_w_in=dw_in, f_conv_w=dcw, f_conv_b=dcb, f_w_down=dw_down)
        return outs[0], g, (outs[2] if want_colsum else None)

    dx3, gr["ffn1"], db_o = ffn_bwd(x3, 1, ffn1, dy, True)
    gr["b_o"] = db_o
    gr["w_o"] = _mm(att, dx3, ta=True, out_dtype=BF16, name="o_proj_dw")
    datt = _mm(dx3, w["w_o"], tb=True, out_dtype=BF16, name="o_proj_dx")
    dqh, dkh, dvh, dsk = _attn_bwd(qh, kh, vh, w["sinks"], datt)
    gr["sinks"] = dsk[:, 0, :4].reshape(1, 16)
    dv = _tokens_major(dvh).astype(BF16)
    dq, db_q, dqn = _hnrope_bwd(q, 0, 1024, w["q_norm_w"], rope, dqh.reshape(16, S, HEAD), name="q_rope_b")
    dk, db_k, dkn = _hnrope_bwd(kv, 0, 256, w["k_norm_w"], rope, dkh, name="k_rope_b")
    gr["q_norm"], gr["k_norm"] = dqn[:, :HEAD], dkn[:, :HEAD]
    gr["b_q"] = db_q
    gr["b_kv"] = jnp.concatenate([db_k, _colsum(dv, name="dv_colsum")], axis=1)
    dkv = jnp.concatenate([dk, dv], axis=1)
    gr["w_q"] = _mm(hq, dq, ta=True, out_dtype=BF16, name="q_proj_dw")
    gr["w_kv"] = _mm(hk, dkv, ta=True, out_dtype=BF16, name="kv_proj_dw")
    tok = comm.grads(1, dict(f_down1=gr["ffn1"]["f_w_down"], f_in1=gr["ffn1"]["f_w_in"], w_o=gr["w_o"], w_q=gr["w_q"],
                             w_kv=gr["w_kv"]), None)
    dhq = _mm(dq, w["w_q"], tb=True, name="q_proj_dx", dep=tok)
    dhk = _mm(dkv, w["w_kv"], tb=True, name="kv_proj_dx")
    dx2, gr["kv_norm"], gr["b_norm"] = _rms_bwd(x2, [w["kv_norm"], w["b_norm"]], [dhk, dhq], dx3, name="kvq_norm_b")

    dx1, gr["ffn0"], _ = ffn_bwd(x1, 0, ffn0, dx2, False)

    tok = comm.grads(2, dict(f_down0=gr["ffn0"]["f_w_down"], f_in0=gr["ffn0"]["f_w_in"]), dx1)
    gr["a_out_proj"] = _mm(yg, dx1, ta=True, out_dtype=BF16, name="out_proj_dw")
    dyg = _mm(dx1, w["a_out_proj"], tb=True, name="out_proj_dx", dep=tok)
    dy_ssd, dz, gr["a_gnorm"], dD = _gate_bwd(y_ssd, xbc, zx, w["a_Dexp"], w["a_gnorm"], dyg)
    gr["a_D"] = dD[:, :SSM_HEADS]
    dxs, dB, dC, dhead = _ssd_bwd(xbc, dt_g, ac_g, ac_t, states, dy_ssd, w["a_Dexp"])
    ddtr, dsmall = _ssd_post(dhead, dt_g, sg_g, w["a_A_log_g"])
    gr["a_A_log"] = dsmall[:, 0, :4].reshape(1, SSM_HEADS)
    gr["a_dt_bias"] = dsmall[:, 1, :4].reshape(1, SSM_HEADS)
    dxbc, gr["a_conv_w"], gr["a_conv_b"] = _conv_silu_bwd(
        zx, 2048, 4096, w["a_conv_w"], w["a_conv_b"], [(dxs, 0), (dB, 2048), (dC, 3072)], name="a_conv_b")
    gr["w_z"] = _mm(h1, dz, ta=True, out_dtype=BF16, name="in_proj_dwz")
    gr["w_x"] = _mm(h1, dxbc, ta=True, out_dtype=BF16, name="in_proj_dwx")
    gr["w_dt"] = _mm(h1, ddtr, ta=True, out_dtype=BF16, name="in_proj_dwdt")
    dh1 = _mm(dz, w["w_zx"], tb=True, name="in_proj_dxz")
    dh1 = _mm(dxbc, w["w_zx"], tb=True, b_koff=2048, res=dh1, name="in_proj_dxx")
    dh1 = _mm(ddtr, w["w_dt"], tb=True, res=dh1, name="in_proj_dxdt")
    dx0, gr["a_norm"] = _rms_bwd(x, [w["a_norm"]], [dh1], dx1, name="a_norm_b")
    comm.grads(3, dict(out_proj=gr["a_out_proj"], in_proj=_in_proj_grad(gr).reshape(D, N_CHIPS, -1).transpose(1, 0, 2)), dx0)
    return loss_part, dx0, gr


def _prep_small(full, w):
    w["a_norm"] = full["a_norm"]
    w["a_conv_w"] = full["a_conv_w"][0]
    w["a_conv_b"] = full["a_conv_b"]
    pad32 = lambda v: jnp.pad(v, ((0, 0), (0, 128 - SSM_HEADS)))
    w["a_dt_bias"] = pad32(full["a_dt_bias"])
    w["a_A_log"] = pad32(full["a_A_log"])
    w["a_A_log_g"] = jnp.pad(full["a_A_log"].reshape(SSM_GROUPS, 1, 4), ((0, 0), (0, 0), (0, 124)))
    w["a_Dexp"] = jnp.repeat(full["a_D"], HEAD, axis=1)
    w["a_gnorm"] = full["a_gnorm"]
    w["f_norm"] = [full["f_norm"][l:l + 1] for l in range(2)]
    w["f_conv_w"] = [full["f_conv_w"][l] for l in range(2)]
    w["f_conv_b"] = [full["f_conv_b"][l:l + 1] for l in range(2)]
    w["kv_norm"] = full["kv_norm"].reshape(1, -1)
    w["b_kv"] = full["b_kv"].reshape(1, -1)
    w["k_norm_w"] = jnp.tile(full["k_norm"].reshape(1, HEAD), (1, ATT_KV))
    w["b_norm"] = full["b_norm"]
    w["b_q"] = full["b_q"]
    w["q_norm_w"] = jnp.tile(full["q_norm"], (1, ATT_KV * ATT_G))
    w["sinks"] = full["sinks"].reshape(-1)
    w["b_o"] = full["b_o"]
    return w


def _split_in_proj(ip):
    return ip[:, :6144].astype(BF16), jnp.pad(ip[:, 6144:], ((0, 0), (0, 128 - SSM_HEADS))).astype(BF16)


def _prep_weights(full):
    w = _prep_small(full, {})
    w["w_zx"], w["w_dt"] = _split_in_proj(full["a_in_proj"][0])
    w["a_out_proj"] = full["a_out_proj"][0].astype(BF16)
    w["f_w_in"] = [full["f_w_in"][l].reshape(1024, N_CHIPS, -1).transpose(1, 0, 2).astype(BF16) for l in range(2)]
    w["f_w_down"] = [full["f_w_down"][l].astype(BF16) for l in range(2)]
    w["w_kv"] = full["w_kv"].astype(BF16)
    w["w_q"] = full["w_q"][0].astype(BF16)
    w["w_o"] = full["w_o"][0].astype(BF16)
    return w


def _small_grads(gr):
    g = {}
    g["a_norm"] = gr["a_norm"]
    g["a_conv_w"] = gr["a_conv_w"][None]
    g["a_conv_b"] = gr["a_conv_b"]
    g["a_dt_bias"], g["a_A_log"], g["a_D"] = gr["a_dt_bias"], gr["a_A_log"], gr["a_D"]
    g["a_gnorm"] = gr["a_gnorm"]
    g["kv_norm"] = gr["kv_norm"].reshape(-1)
    g["b_kv"] = gr["b_kv"].reshape(-1)
    g["k_norm"] = gr["k_norm"].reshape(-1)
    g["b_norm"] = gr["b_norm"]
    g["b_q"] = gr["b_q"]
    g["q_norm"] = gr["q_norm"]
    g["sinks"] = gr["sinks"]
    g["b_o"] = gr["b_o"]
    f = [gr["ffn0"], gr["ffn1"]]
    g["f_norm"] = jnp.concatenate([f[0]["f_norm"], f[1]["f_norm"]], axis=0)
    g["f_conv_w"] = jnp.stack([f[l]["f_conv_w"] for l in range(2)])
    g["f_conv_b"] = jnp.concatenate([f[l]["f_conv_b"] for l in range(2)], axis=0)
    return g


def _in_proj_grad(gr):
    return jnp.concatenate([gr["w_z"], gr["w_x"], gr["w_dt"][:, :SSM_HEADS]], axis=1)


def _full_grads(gr):
    g = _small_grads(gr)
    f32 = lambda t: t.astype(F32)
    g["a_in_proj"] = f32(_in_proj_grad(gr))[None]
    g["a_out_proj"] = f32(gr["a_out_proj"])[None]
    g["w_kv"] = f32(gr["w_kv"])
    g["w_q"] = f32(gr["w_q"])[None]
    g["w_o"] = f32(gr["w_o"])[None]
    f = [gr["ffn0"], gr["ffn1"]]
    g["f_w_in"] = jnp.stack([f32(f[l]["f_w_in"]).transpose(1, 0, 2).reshape(1024, -1) for l in range(2)])
    g["f_w_down"] = jnp.stack([f32(f[l]["f_w_down"]) for l in range(2)])
    return g


MESH = pl.DeviceIdType.MESH
WEIGHTS = ("a_norm", "a_in_proj", "a_conv_w", "a_conv_b", "a_dt_bias", "a_A_log", "a_D", "a_gnorm", "a_out_proj",
           "kv_norm", "w_kv", "b_kv", "k_norm", "b_norm", "w_q", "b_q", "q_norm", "sinks", "w_o", "b_o", "f_norm",
           "f_w_in", "f_conv_w", "f_conv_b", "f_w_down")
MATS = (("in_proj", "a_in_proj", 0), ("out_proj", "a_out_proj", 0), ("w_kv", "w_kv", None), ("w_q", "w_q", 0),
        ("w_o", "w_o", 0), ("f_in0", "f_w_in", 0), ("f_in1", "f_w_in", 1), ("f_down0", "f_w_down", 0),
        ("f_down1", "f_w_down", 1))
SMALL_CUT = (("a_norm", 1), ("a_conv_w", 2), ("a_conv_b", 1), ("a_gnorm", 1), ("f_conv_w", 2))
SMALL_REP = ("a_dt_bias", "a_A_log", "a_D", "kv_norm", "b_kv", "k_norm", "b_norm", "b_q", "q_norm", "sinks", "b_o",
             "f_norm", "f_conv_b")


def _coords():
    return lax.axis_index("x"), lax.axis_index("y"), lax.axis_index("c")


def _other_chips(x, y):
    return [(1 - x, y), (x, 1 - y), (1 - x, 1 - y)]


def _pack(arrs, rows_align, lanes, dtype):
    flat = jnp.concatenate([a.reshape(-1).astype(dtype) for a in arrs])
    per = rows_align * lanes
    total = -(-flat.shape[0] // per) * per
    return jnp.pad(flat, (0, total - flat.shape[0])).reshape(total // lanes, lanes)


def _unpack(flat, shapes):
    out, off = [], 0
    for s in shapes:
        n = math.prod(s)
        out.append(flat[off:off + n].reshape(s))
        off += n
    return out


def _remote(src, dst, send, recv, k, dev):
    return pltpu.make_async_remote_copy(src_ref=src, dst_ref=dst, send_sem=send.at[k], recv_sem=recv.at[k],
                                        device_id=dev, device_id_type=MESH)


_ANY = pl.BlockSpec(memory_space=pl.ANY)


def _halves(t):
    r, c = t.shape
    return t.reshape(2, r // 2, c)


def _gather_weights(shards, sp):
    n = len(shards)
    n_sem = 7 * n + 3

    def body(*refs):
        sh, sp_ref = refs[:n], refs[n]
        outs, sout = refs[n + 1:2 * n + 1], refs[2 * n + 1]
        send, recv, loc = refs[2 * n + 2:]
        x, y, c = _coords()
        me = 2 * x + y
        chips = _other_chips(x, y)
        sib = (x, y, 1 - c)
        l1 = pltpu.make_async_copy(sp_ref, sout.at[me], loc.at[0])
        l1.start()
        sends = []
        for j, (cx, cy) in enumerate(chips):
            sends.append(_remote(sp_ref, sout.at[me], send, recv, 7 * n + j, (cx, cy, c)))
            for t in range(n):
                sends.append(_remote(sh[t].at[c], outs[t].at[me, c], send, recv, 7 * t + j, (cx, cy, c)))
        for t in range(n):
            sends.append(_remote(sh[t], outs[t].at[me], send, recv, 7 * t + 6, sib))
        for cp in sends:
            cp.start()
        for j, (cx, cy) in enumerate(chips):
            src = 2 * cx + cy
            for t in range(n):
                _remote(sh[t].at[c], outs[t].at[src, c], send, recv, 7 * t + j, (cx, cy, c)).wait_recv()
                fwd = _remote(outs[t].at[src, c], outs[t].at[src, c], send, recv, 7 * t + 3 + j, sib)
                fwd.start()
                sends.append(fwd)
        for j, (cx, cy) in enumerate(chips):
            src = 2 * cx + cy
            _remote(sp_ref, sout.at[src], send, recv, 7 * n + j, (cx, cy, c)).wait_recv()
            for t in range(n):
                _remote(outs[t].at[src, 1 - c], outs[t].at[src, 1 - c], send, recv, 7 * t + 3 + j, sib).wait_recv()
        for t in range(n):
            _remote(sh[t], outs[t].at[me], send, recv, 7 * t + 6, sib).wait_recv()
        for cp in sends:
            cp.wait_send()
        l1.wait()

    res = pl.pallas_call(
        body, name="gather_weights", in_specs=[_ANY] * (n + 1), out_specs=[_ANY] * (n + 1),
        out_shape=[jax.ShapeDtypeStruct((N_CHIPS,) + t.shape, t.dtype) for t in shards]
        + [jax.ShapeDtypeStruct((N_CHIPS,) + sp.shape, sp.dtype)],
        scratch_shapes=[pltpu.SemaphoreType.DMA((n_sem,)), pltpu.SemaphoreType.DMA((n_sem,)),
                        pltpu.SemaphoreType.DMA((1,))],
    )(*shards, sp)
    return res[:n], res[n]


_HBM = pl.BlockSpec(memory_space=pltpu.HBM)
_SEMS = pl.BlockSpec(memory_space=pltpu.SEMAPHORE)
_DATAFLOW = pltpu.SideEffectType.DATAFLOW_SIDE_EFFECTING


def _in_hbm(a):
    return pltpu.with_memory_space_constraint(a, pltpu.HBM)


def _gather_copies(sh, land, send, recv):
    x, y, c = _coords()
    me = 2 * x + y
    out = []
    for t in range(len(sh)):
        for j, (cx, cy) in enumerate(_other_chips(x, y)):
            dev = (cx, cy, c)
            out.append((_remote(sh[t].at[c], land[t].at[me, c], send, recv, 4 * t + j, dev),
                        _remote(sh[t].at[c], land[t].at[2 * cx + cy, c], send, recv, 4 * t + j, dev)))
        sib = (x, y, 1 - c)
        out.append((_remote(sh[t], land[t].at[me], send, recv, 4 * t + 3, sib),
                    _remote(sh[t], land[t].at[me], send, recv, 4 * t + 3, sib)))
    return out


def _gather_start(shards, after, *, name):
    n = len(shards)

    def body(*refs):
        sh, land = refs[:n], refs[n:2 * n]
        send, recv = refs[2 * n + 1], refs[2 * n + 2]
        token = refs[-1]
        for mine, _ in _gather_copies(sh, land, send, recv):
            mine.start()
        token[...] = jnp.zeros_like(token)

    lands = [_in_hbm(lax.empty((N_CHIPS,) + s.shape, s.dtype)) for s in shards]
    res = pl.pallas_call(
        body, name=name, in_specs=[_HBM] * (2 * n) + [_ANY],
        out_specs=[_SEMS, _SEMS] + [_HBM] * (2 * n) + [pl.BlockSpec(memory_space=pltpu.VMEM)],
        out_shape=[pltpu.SemaphoreType.DMA((4 * n,)), pltpu.SemaphoreType.DMA((4 * n,))]
        + [pltpu.HBM(s.shape, s.dtype) for s in shards] + [pltpu.HBM(l.shape, l.dtype) for l in lands]
        + [jax.ShapeDtypeStruct((8, 128), F32)],
        input_output_aliases={t: 2 + t for t in range(2 * n)},
        compiler_params=pltpu.CompilerParams(has_side_effects=_DATAFLOW),
    )(*[_in_hbm(s) for s in shards], *lands, after)
    return res[0], res[1], res[2:2 + n], res[2 + n:2 + 2 * n], res[-1]


def _gather_wait(send, recv, shards, lands, after, *, name):
    n = len(shards)

    def body(*refs):
        sh, land = refs[:n], refs[n:2 * n]
        send_r, recv_r = refs[2 * n], refs[2 * n + 1]
        for mine, theirs in _gather_copies(sh, land, send_r, recv_r):
            mine.wait_send()
            theirs.wait_recv()

    res = pl.pallas_call(
        body, name=name, in_specs=[_HBM] * (2 * n) + [_SEMS, _SEMS, _ANY], out_specs=[_HBM] * (2 * n),
        out_shape=[pltpu.HBM(s.shape, s.dtype) for s in shards] + [pltpu.HBM(l.shape, l.dtype) for l in lands],
        input_output_aliases={t: t for t in range(2 * n)},
        compiler_params=pltpu.CompilerParams(has_side_effects=_DATAFLOW),
    )(*shards, *lands, send, recv, after)
    return res[n:]


def _gather_forward(lands, *, name):
    n = len(lands)

    def body(*refs):
        o = refs[n:2 * n]
        send, recv = refs[2 * n:]
        x, y, c = _coords()
        sib = (x, y, 1 - c)
        srcs = [2 * cx + cy for cx, cy in _other_chips(x, y)]
        cps = [_remote(o[t].at[s, c], o[t].at[s, c], send, recv, 3 * t + j, sib) for t in range(n) for j, s in enumerate(srcs)]
        for cp in cps:
            cp.start()
        for t in range(n):
            for j, s in enumerate(srcs):
                _remote(o[t].at[s, 1 - c], o[t].at[s, 1 - c], send, recv, 3 * t + j, sib).wait_recv()
        for cp in cps:
            cp.wait_send()

    return pl.pallas_call(
        body, name=name, in_specs=[_ANY] * n, out_specs=[_ANY] * n, input_output_aliases={t: t for t in range(n)},
        out_shape=[jax.ShapeDtypeStruct(l.shape, l.dtype) for l in lands],
        scratch_shapes=[pltpu.SemaphoreType.DMA((3 * n,)), pltpu.SemaphoreType.DMA((3 * n,))],
    )(*lands)


def _allreduce_small(v):
    SR = v.shape[0]

    def body(v_ref, o_ref, buf, send, recv):
        x, y, c = _coords()
        me = 4 * x + 2 * y + c
        buf[me] = v_ref[...]
        peers = []
        for k in range(1, 8):
            px = 1 - x if k & 4 else x
            py = 1 - y if k & 2 else y
            pc = 1 - c if k & 1 else c
            peers.append((px, py, pc))
        cps = [_remote(v_ref, buf.at[me], send, recv, k, p) for k, p in enumerate(peers)]
        for cp in cps:
            cp.start()
        for k, (px, py, pc) in enumerate(peers):
            _remote(v_ref, buf.at[4 * px + 2 * py + pc], send, recv, k, (px, py, pc)).wait_recv()
        for cp in cps:
            cp.wait_send()
        acc = buf[0]
        for s in range(1, 8):
            acc = acc + buf[s]
        o_ref[...] = acc

    vm = pl.BlockSpec(memory_space=pltpu.VMEM)
    return pl.pallas_call(
        body, name="allreduce_small", in_specs=[vm], out_specs=vm, out_shape=jax.ShapeDtypeStruct(v.shape, F32),
        scratch_shapes=[pltpu.VMEM((8, SR, 128), F32), pltpu.SemaphoreType.DMA((7,)), pltpu.SemaphoreType.DMA((7,))],
    )(v)


def _rs_to_sibling(gs, *, name):
    n = len(gs)

    def body(*refs):
        g, a = refs[:n], refs[n:2 * n]
        send, recv = refs[2 * n:]
        x, y, c = _coords()
        cps = [_remote(g[t].at[:, 1 - c], a[t], send, recv, t, (x, y, 1 - c)) for t in range(n)]
        for cp in cps:
            cp.start()
        for cp in cps:
            cp.wait()

    return pl.pallas_call(
        body, name=name, in_specs=[_ANY] * n, out_specs=[_ANY] * n,
        out_shape=[jax.ShapeDtypeStruct((N_CHIPS,) + g.shape[2:], g.dtype) for g in gs],
        scratch_shapes=[pltpu.SemaphoreType.DMA((n,)), pltpu.SemaphoreType.DMA((n,))],
    )(*gs)


def _rs_add_pair(g, a, c_idx, *, name):
    _, _, rh, cols = g.shape

    def body(c_ref, g_ref, a_ref, p_ref):
        p_ref[...] = (g_ref[...].astype(F32) + a_ref[...].astype(F32)).astype(BF16)

    return pl.pallas_call(
        body, name=name,
        grid_spec=pltpu.PrefetchScalarGridSpec(
            num_scalar_prefetch=1, grid=(N_CHIPS,),
            in_specs=[pl.BlockSpec((None, None, rh, cols), lambda j, c_ref: (j, c_ref[0], 0, 0)),
                      pl.BlockSpec((None, rh, cols), lambda j, c_ref: (j, 0, 0))],
            out_specs=pl.BlockSpec((None, rh, cols), lambda j, c_ref: (j, 0, 0))),
        out_shape=jax.ShapeDtypeStruct((N_CHIPS, rh, cols), BF16), compiler_params=_cp(1),
    )(c_idx, g, a)


def _chips_copies(p, r, send, recv):
    x, y, c = _coords()
    return [_remote(p[t].at[2 * cx + cy], r[t].at[k], send, recv, 3 * t + k, (cx, cy, c))
            for k, (cx, cy) in enumerate(_other_chips(x, y)) for t in range(len(p))]


def _rs_chips_start(ps, *, name):
    n = len(ps)

    def body(*refs):
        p, r = refs[:n], refs[n:2 * n]
        send, recv = refs[2 * n], refs[2 * n + 1]
        token = refs[-1]
        for cp in _chips_copies(p, r, send, recv):
            cp.start()
        token[...] = jnp.zeros_like(token)

    lands = [_in_hbm(lax.empty((3,) + p.shape[1:], p.dtype)) for p in ps]
    res = pl.pallas_call(
        body, name=name, in_specs=[_HBM] * (2 * n),
        out_specs=[_SEMS, _SEMS] + [_HBM] * (2 * n) + [pl.BlockSpec(memory_space=pltpu.VMEM)],
        out_shape=[pltpu.SemaphoreType.DMA((3 * n,)), pltpu.SemaphoreType.DMA((3 * n,))]
        + [pltpu.HBM(p.shape, p.dtype) for p in ps] + [pltpu.HBM(l.shape, l.dtype) for l in lands]
        + [jax.ShapeDtypeStruct((8, 128), F32)],
        input_output_aliases={t: 2 + t for t in range(2 * n)},
        compiler_params=pltpu.CompilerParams(has_side_effects=_DATAFLOW),
    )(*[_in_hbm(p) for p in ps], *lands)
    return res[0], res[1], res[2:2 + n], res[2 + n:2 + 2 * n], res[-1]


def _rs_chips_wait(send, recv, ps, lands, after, *, name):
    n = len(ps)

    def body(*refs):
        p, r = refs[:n], refs[n:2 * n]
        for cp in _chips_copies(p, r, refs[2 * n], refs[2 * n + 1]):
            cp.wait_send()
            cp.wait_recv()

    res = pl.pallas_call(
        body, name=name, in_specs=[_HBM] * (2 * n) + [_SEMS, _SEMS] + [_ANY] * len(after), out_specs=[_HBM] * (2 * n),
        out_shape=[pltpu.HBM(p.shape, p.dtype) for p in ps] + [pltpu.HBM(l.shape, l.dtype) for l in lands],
        input_output_aliases={t: t for t in range(2 * n)},
        compiler_params=pltpu.CompilerParams(has_side_effects=_DATAFLOW),
    )(*ps, *lands, send, recv, *after)
    return res[:n], res[n:]


def _rs_add_chips(p, r, idx, *, name):
    _, rh, cols = p.shape

    def body(idx_ref, p_ref, r0_ref, r1_ref, r2_ref, o_ref):
        o_ref[...] = ((p_ref[...].astype(F32) + r0_ref[...].astype(F32)) + r1_ref[...].astype(F32)) + r2_ref[...].astype(F32)

    def rk(k):
        return pl.BlockSpec((None, rh, cols), lambda i, idx_ref, k=k: (k, 0, 0))

    return pl.pallas_call(
        body, name=name,
        grid_spec=pltpu.PrefetchScalarGridSpec(
            num_scalar_prefetch=1, grid=(1,),
            in_specs=[pl.BlockSpec((None, rh, cols), lambda i, idx_ref: (idx_ref[0], 0, 0)), rk(0), rk(1), rk(2)],
            out_specs=pl.BlockSpec((None, rh, cols), lambda i, idx_ref: (idx_ref[1], 0, 0))),
        out_shape=jax.ShapeDtypeStruct((2, rh, cols), F32), compiler_params=_cp(1),
    )(idx, p, r, r, r)


def _rs_join_halves(hs, *, name):
    n = len(hs)

    def body(*refs):
        o = refs[n:2 * n]
        send, recv = refs[2 * n:]
        x, y, c = _coords()
        cps = [_remote(o[t].at[c], o[t].at[c], send, recv, t, (x, y, 1 - c)) for t in range(n)]
        for cp in cps:
            cp.start()
        for t in range(n):
            _remote(o[t].at[1 - c], o[t].at[1 - c], send, recv, t, (x, y, 1 - c)).wait_recv()
        for cp in cps:
            cp.wait_send()

    return pl.pallas_call(
        body, name=name, in_specs=[_ANY] * n, out_specs=[_ANY] * n,
        input_output_aliases={t: t for t in range(n)},
        out_shape=[jax.ShapeDtypeStruct(h.shape, F32) for h in hs],
        scratch_shapes=[pltpu.SemaphoreType.DMA((n,)), pltpu.SemaphoreType.DMA((n,))],
    )(*hs)


def _adamw(w, gs, m, v, *, name, dep=None):
    L, Rr, C = w.shape
    tr, tc = _pick(Rr, (256, 128, 64)), C
    if tr == Rr and Rr * C > 512 * 1024:
        tc = 256
    bc1 = 1.0 - ADAM_B1 ** ADAM_STEP
    bc2 = 1.0 - ADAM_B2 ** ADAM_STEP
    nd = 0 if dep is None else 1

    def body(*refs):
        w_ref, m_ref, v_ref = refs[0], refs[1], refs[2]
        g_refs = refs[3:3 + L]
        d_ref, mo_ref, vo_ref, go_ref = refs[3 + L + nd:]
        layer = pl.program_id(0)
        gv = g_refs[0][...]
        for q in range(1, L):
            gv = jnp.where(layer == q, g_refs[q][...], gv)
        mn = ADAM_B1 * m_ref[...] + (1.0 - ADAM_B1) * gv
        vn = ADAM_B2 * v_ref[...] + (1.0 - ADAM_B2) * (gv * gv)
        go_ref[...] = gv
        mo_ref[...] = mn
        vo_ref[...] = vn
        d_ref[...] = -ADAM_LR * ((mn / bc1) / (jnp.sqrt(vn / bc2) + ADAM_EPS) + ADAM_WD * w_ref[...])

    blk = pl.BlockSpec((None, tr, tc), lambda l, i, j: (l, i, j))
    gblk = pl.BlockSpec((tr, tc), lambda l, i, j: (i, j))
    return pl.pallas_call(
        body, name=name, grid=(L, Rr // tr, C // tc), in_specs=[blk] * 3 + [gblk] * L + [_ANY] * nd, out_specs=[blk] * 4,
        out_shape=[jax.ShapeDtypeStruct((L, Rr, C), F32)] * 4, compiler_params=_cp(3),
    )(w, m, v, *gs, *([] if dep is None else [dep]))


def kernel(x, positions, a_norm, a_in_proj, a_conv_w, a_conv_b, a_dt_bias, a_A_log, a_D, a_gnorm, a_out_proj,
           kv_norm, w_kv, b_kv, k_norm, b_norm, w_q, b_q, q_norm, sinks, w_o, b_o, f_norm, f_w_in, f_conv_w,
           f_conv_b, f_w_down, loss_target, m_a_norm, m_a_in_proj, m_a_conv_w, m_a_conv_b, m_a_dt_bias, m_a_A_log,
           m_a_D, m_a_gnorm, m_a_out_proj, m_kv_norm, m_w_kv, m_b_kv, m_k_norm, m_b_norm, m_w_q, m_b_q, m_q_norm,
           m_sinks, m_w_o, m_b_o, m_f_norm, m_f_w_in, m_f_conv_w, m_f_conv_b, m_f_w_down, v_a_norm, v_a_in_proj,
           v_a_conv_w, v_a_conv_b, v_a_dt_bias, v_a_A_log, v_a_D, v_a_gnorm, v_a_out_proj, v_kv_norm, v_w_kv,
           v_b_kv, v_k_norm, v_b_norm, v_w_q, v_b_q, v_q_norm, v_sinks, v_w_o, v_b_o, v_f_norm, v_f_w_in,
           v_f_conv_w, v_f_conv_b, v_f_w_down):
    wl = dict(zip(WEIGHTS, (a_norm, a_in_proj, a_conv_w, a_conv_b, a_dt_bias, a_A_log, a_D, a_gnorm, a_out_proj,
                            kv_norm, w_kv, b_kv, k_norm, b_norm, w_q, b_q, q_norm, sinks, w_o, b_o, f_norm, f_w_in,
                            f_conv_w, f_conv_b, f_w_down)))
    ml = dict(zip(WEIGHTS, (m_a_norm, m_a_in_proj, m_a_conv_w, m_a_conv_b, m_a_dt_bias, m_a_A_log, m_a_D, m_a_gnorm,
                            m_a_out_proj, m_kv_norm, m_w_kv, m_b_kv, m_k_norm, m_b_norm, m_w_q, m_b_q, m_q_norm,
                            m_sinks, m_w_o, m_b_o, m_f_norm, m_f_w_in, m_f_conv_w, m_f_conv_b, m_f_w_down)))
    vl = dict(zip(WEIGHTS, (v_a_norm, v_a_in_proj, v_a_conv_w, v_a_conv_b, v_a_dt_bias, v_a_A_log, v_a_D, v_a_gnorm,
                            v_a_out_proj, v_kv_norm, v_w_kv, v_b_kv, v_k_norm, v_b_norm, v_w_q, v_b_q, v_q_norm,
                            v_sinks, v_w_o, v_b_o, v_f_norm, v_f_w_in, v_f_conv_w, v_f_conv_b, v_f_w_down)))
    xi, yi, ci = _coords()
    me = 2 * xi + yi
    S = x.shape[1]

    def block_of(n, layer):
        t = wl[n]
        return t if layer is None else t[layer]

    rows = lambda t: t.reshape(-1, t.shape[-1])
    c_idx = jnp.reshape(ci, (1,)).astype(jnp.int32)
    me_c = jnp.stack([me, ci]).astype(jnp.int32)
    early = ("in_proj", "out_proj")
    late = tuple(name for name, _, _ in MATS if name not in early)
    shards = {name: _halves(block_of(wn, layer).astype(BF16)) for name, wn, layer in MATS}

    sp = _pack([wl[n] for n, _ in SMALL_CUT], 8, 128, F32)
    gathered, gs = _gather_weights([shards[k] for k in early], sp)
    gt = {k: t.reshape(N_CHIPS, -1, t.shape[-1]) for k, t in zip(early, gathered)}
    started = _gather_start([shards[k] for k in late], gs, name="gather_late_start")
    full = {n: wl[n] for n in SMALL_REP}
    gs = gs.reshape(N_CHIPS, -1)
    pieces = [_unpack(gs[j], [wl[n].shape for n, _ in SMALL_CUT]) for j in range(N_CHIPS)]
    for q, (n, ax) in enumerate(SMALL_CUT):
        full[n] = jnp.concatenate([pieces[j][q] for j in range(N_CHIPS)], axis=ax)
    w = _prep_small(full, {})
    w["w_zx"], w["w_dt"] = _split_in_proj(gt["in_proj"].transpose(1, 0, 2).reshape(1024, -1))
    w["a_out_proj"] = rows(gt["out_proj"])
    w["dep"] = started[4]

    class Comm:
        pending = None
        token = None
        reduced = {}

        def late_weights(self, w, after):
            lands = _gather_wait(started[0], started[1], started[2], started[3], after, name="gather_late_wait")
            lands = _gather_forward(lands, name="gather_late_forward")
            lt = {k: t.reshape(N_CHIPS, -1, t.shape[-1]) for k, t in zip(late, lands)}
            w = dict(w)
            w["w_kv"], w["w_q"], w["w_o"] = (rows(lt[k]) for k in ("w_kv", "w_q", "w_o"))
            w["f_w_in"] = [lt["f_in0"], lt["f_in1"]]
            w["f_w_down"] = [rows(lt["f_down0"]), rows(lt["f_down1"])]
            return w

        def finish(self, after):
            names, send, recv, ps, lands, tag = self.pending
            ps, rs = _rs_chips_wait(send, recv, ps, lands, after, name=f"rs_chips_wait{tag}")
            halves = [_rs_add_chips(pq, rq, me_c, name="rs_add_chips_" + k) for k, pq, rq in zip(names, ps, rs)]
            joined = _rs_join_halves(halves, name=f"rs_join_halves{tag}")
            self.reduced.update({k: rows(t) for k, t in zip(names, joined)})
            self.pending = None

        def grads(self, group, tensors, after):
            if self.pending is not None:
                self.finish([after])
            names = list(tensors)
            glist = [tensors[k].reshape(N_CHIPS, 2, -1, tensors[k].shape[-1]) for k in names]
            from_sib = _rs_to_sibling(glist, name=f"rs_to_sibling{group}")
            pairs = [_rs_add_pair(gq, aq, c_idx, name="rs_add_pair_" + k) for k, gq, aq in zip(names, glist, from_sib)]
            send, recv, ps, lands, token = _rs_chips_start(pairs, name=f"rs_chips_start{group}")
            self.pending = (names, send, recv, ps, lands, group)
            self.token = token
            return token

    comm = Comm()

    posf = positions.reshape(S, 1).astype(F32)
    loss_part, dx0, gr = _local_step(x[0], posf, loss_target[0], w, comm)
    g = _small_grads(gr)

    small_names = [n for n, _ in SMALL_CUT] + list(SMALL_REP)
    sv = _pack([g[n] for n in small_names] + [loss_part[0:1, 0:1]], 8, 128, F32)
    sred = _allreduce_small(sv).reshape(-1)
    small_shapes = [g[n].shape for n in small_names] + [(1,)]
    sg = dict(zip(small_names + ["loss"], _unpack(sred, small_shapes)))
    loss = sg["loss"].reshape(())
    g_small = {}
    for n, ax in SMALL_CUT:
        size = wl[n].shape[ax]
        g_small[n] = lax.dynamic_slice_in_dim(sg[n], me * size, size, axis=ax)
    for n in SMALL_REP:
        g_small[n] = sg[n].reshape(wl[n].shape)

    grads, delta, new_m, new_v = {}, {}, {}, {}

    def update(wn, dep):
        gl = [comm.reduced[name] for name, n2, _ in MATS if n2 == wn]
        shp = wl[wn].shape
        three = (len(gl),) + gl[0].shape
        flip = shp[-1] % 128 != 0
        view = (lambda t: t.reshape(three).transpose(0, 2, 1)) if flip else (lambda t: t.reshape(three))
        back = (lambda t: t.transpose(0, 2, 1).reshape(shp)) if flip else (lambda t: t.reshape(shp))
        if flip:
            gl = [t.T for t in gl]
        d, mn, vn, go = _adamw(view(wl[wn]), gl, view(ml[wn]), view(vl[wn]), name="adamw_" + wn, dep=dep)
        grads[wn], delta[wn], new_m[wn], new_v[wn] = back(go), back(d), back(mn), back(vn)
        return d

    comm.finish([update(wn, comm.token) for wn in ("f_w_in", "f_w_down", "w_q", "w_o", "w_kv")])
    for wn in ("a_in_proj", "a_out_proj"):
        update(wn, None)
    pk = lambda d: _pack([d[n] for n in small_names], 8, 128, F32)[None]
    d, mn, vn, _ = _adamw(pk(wl), [pk(g_small)[0]], pk(ml), pk(vl), name="adamw_small")
    shapes = [wl[n].shape for n in small_names]
    for n, dd, mm, vv in zip(small_names, _unpack(d.reshape(-1), shapes), _unpack(mn.reshape(-1), shapes),
                             _unpack(vn.reshape(-1), shapes)):
        grads[n], delta[n], new_m[n], new_v[n] = g_small[n], dd, mm, vv

    return (loss, dx0[None], *[grads[n] for n in WEIGHTS], *[delta[n] for n in WEIGHTS],
            *[new_m[n] for n in WEIGHTS], *[new_v[n] for n in WEIGHTS])
```

```python
import math

import jax
import jax.numpy as jnp
from jax import lax
from jax.experimental import pallas as pl
from jax.experimental.pallas import tpu as pltpu

F32 = jnp.float32
BF16 = jnp.bfloat16

EPS = 1e-5
CHUNK = 256
WINDOW = 128
HEAD = 64
SSM_HEADS = 32
SSM_GROUPS = 8
SSM_STATE = 128
ATT_KV = 4
ATT_G = 4
ROPE_THETA = 10000.0
NEG = -1e30
N_CHIPS = 4
VMEM_LIMIT = 56 * 1024 * 1024

ADAM_LR, ADAM_B1, ADAM_B2, ADAM_EPS, ADAM_WD, ADAM_STEP = 0.001, 0.9, 0.999, 1e-08, 0.01, 10


def _cp(n_axes):
    return pltpu.CompilerParams(dimension_semantics=("arbitrary",) * n_axes, vmem_limit_bytes=VMEM_LIMIT)


def _pick(dim, prefs):
    for p in prefs:
        if dim % p == 0:
            return p
    return dim


def _iota(shape, dim):
    return lax.broadcasted_iota(jnp.int32, shape, dim)


def _dot(a, b, ca=1, cb=0):
    return lax.dot_general(a, b, (((ca,), (cb,)), ((), ())), preferred_element_type=F32)


def _dot3(x, ind):
    h = x.astype(BF16)
    r = x - h.astype(F32)
    m = r.astype(BF16)
    lo = (r - m.astype(F32)).astype(BF16)
    return _dot(h, ind) + _dot(m, ind) + _dot(lo, ind)


def _sigmoid(x):
    return jax.nn.sigmoid(x)


def _mm(a, b, *, name, ta=False, tb=False, bias=None, res=None, out_dtype=F32, b_koff=0, tm=None, tn=None, tk=None,
        dims=None, a_spec=None, b_spec=None, o_spec=None, o_shape=None, dep=None, more=()):
    if dims is not None:
        M, N, K = dims
    else:
        if ta:
            K, M = a.shape
        else:
            M, K = a.shape
        N = b.shape[0] if tb else b.shape[1]
    tm = tm or _pick(M, (1024, 1408, 512, 256, 128))
    tn = tn or _pick(N, (512, 1408, 256, 128))
    tk = tk or (K if K <= 2048 else _pick(K, (2048, 1408, 1024, 512)))
    assert M % tm == 0 and N % tn == 0 and K % tk == 0 and b_koff % tk == 0
    nk = K // tk
    kb0 = b_koff // tk
    has_bias, has_res = bias is not None, res is not None

    def body(*refs):
        a_ref, b_ref = refs[0], refs[1]
        pos = 2
        bias_ref = res_ref = acc_ref = None
        if has_bias:
            bias_ref = refs[pos]
            pos += 1
        if has_res:
            res_ref = refs[pos]
            pos += 1
        if dep is not None:
            pos += 1
        extra = refs[pos:pos + 2 * len(more)]
        pos += 2 * len(more)
        o_ref = refs[pos]
        if nk > 1:
            acc_ref = refs[pos + 1]
        part = _dot(a_ref[...].astype(BF16), b_ref[...].astype(BF16), 0 if ta else 1, 1 if tb else 0)
        for q in range(len(more)):
            part = part + _dot(extra[2 * q][...].astype(BF16), extra[2 * q + 1][...].astype(BF16),
                               0 if ta else 1, 1 if tb else 0)

        def finish(acc):
            if has_bias:
                acc = acc + bias_ref[...]
            if has_res:
                acc = acc + res_ref[...]
            o_ref[...] = acc.astype(out_dtype)

        if nk == 1:
            finish(part)
        else:
            k = pl.program_id(2)

            @pl.when(k == 0)
            def _():
                acc_ref[...] = part

            @pl.when(k > 0)
            def _():
                acc_ref[...] += part

            @pl.when(k == nk - 1)
            def _():
                finish(acc_ref[...])

    if a_spec is None:
        a_spec = pl.BlockSpec((tk, tm), lambda i, j, k: (k, i)) if ta else pl.BlockSpec((tm, tk), lambda i, j, k: (i, k))
    if b_spec is None:
        b_spec = (pl.BlockSpec((tn, tk), lambda i, j, k: (j, k + kb0)) if tb
                  else pl.BlockSpec((tk, tn), lambda i, j, k: (k + kb0, j)))
    if o_spec is None:
        o_spec = pl.BlockSpec((tm, tn), lambda i, j, k: (i, j))
    in_specs, args = [a_spec, b_spec], [a, b]
    if has_bias:
        in_specs.append(pl.BlockSpec((1, tn), lambda i, j, k: (0, j)))
        args.append(bias)
    if has_res:
        in_specs.append(pl.BlockSpec((tm, tn), lambda i, j, k: (i, j)))
        args.append(res)
    if dep is not None:
        in_specs.append(pl.BlockSpec(memory_space=pl.ANY))
        args.append(dep)
    for sa, sb in more:
        in_specs += [sa, sb]
        args += [a, b]
    return pl.pallas_call(
        body, name=name, grid=(M // tm, N // tn, nk), in_specs=in_specs, out_specs=o_spec,
        out_shape=jax.ShapeDtypeStruct(o_shape or (M, N), out_dtype),
        scratch_shapes=[pltpu.VMEM((tm, tn), F32)] if nk > 1 else [],
        compiler_params=_cp(3),
    )(*args)


def _rms_fwd(x, gains, *, name, tr=256, dep=None):
    S, D = x.shape
    n = len(gains)
    nd = 0 if dep is None else 1

    def body(*refs):
        xv = refs[0][...]
        xh = xv * lax.rsqrt(jnp.mean(xv * xv, axis=-1, keepdims=True) + EPS)
        for q in range(n):
            refs[1 + n + nd + q][...] = (xh * refs[1 + q][...]).astype(BF16)

    row = pl.BlockSpec((tr, D), lambda i: (i, 0))
    vec = pl.BlockSpec((1, D), lambda i: (0, 0))
    return pl.pallas_call(
        body, name=name, grid=(S // tr,), in_specs=[row] + [vec] * n + [pl.BlockSpec(memory_space=pl.ANY)] * nd,
        out_specs=[row] * n, out_shape=[jax.ShapeDtypeStruct((S, D), BF16)] * n, compiler_params=_cp(1),
    )(x, *gains, *([] if dep is None else [dep]))


def _rms_bwd(x, gains, dhs, dres, *, name, tr=256, want_colsum=False):
    S, D = x.shape
    n = len(gains)
    steps = S // tr

    def body(*refs):
        x_ref = refs[0]
        g_refs = refs[1:1 + n]
        dh_refs = refs[1 + n:1 + 2 * n]
        dres_ref = refs[1 + 2 * n]
        dx_ref = refs[2 + 2 * n]
        dg_refs = refs[3 + 2 * n:3 + 3 * n]
        cs_ref = refs[3 + 3 * n] if want_colsum else None
        i = pl.program_id(0)
        xv = x_ref[...]
        r = lax.rsqrt(jnp.mean(xv * xv, axis=-1, keepdims=True) + EPS)
        xh = xv * r
        dx = dres_ref[...]
        for q in range(n):
            dh = dh_refs[q][...]
            dxh = dh * g_refs[q][...]
            dx = dx + r * (dxh - xh * jnp.mean(dxh * xh, axis=-1, keepdims=True))
            part = jnp.sum(dh * xh, axis=0, keepdims=True)

            @pl.when(i == 0)
            def _():
                dg_refs[q][...] = part

            @pl.when(i > 0)
            def _():
                dg_refs[q][...] += part

        dx_ref[...] = dx
        if want_colsum:
            cpart = jnp.sum(dx, axis=0, keepdims=True)

            @pl.when(i == 0)
            def _():
                cs_ref[...] = cpart

            @pl.when(i > 0)
            def _():
                cs_ref[...] += cpart

    row = pl.BlockSpec((tr, D), lambda i: (i, 0))
    vec = pl.BlockSpec((1, D), lambda i: (0, 0))
    n_vec_out = n + (1 if want_colsum else 0)
    outs = pl.pallas_call(
        body, name=name, grid=(steps,), in_specs=[row] + [vec] * n + [row] * n + [row],
        out_specs=[row] + [vec] * n_vec_out,
        out_shape=[jax.ShapeDtypeStruct((S, D), F32)] + [jax.ShapeDtypeStruct((1, D), F32)] * n_vec_out,
        compiler_params=_cp(1),
    )(x, *gains, *dhs, dres)
    return outs


def _colsum(x, *, name, tr=256):
    S, D = x.shape

    def body(x_ref, o_ref):
        i = pl.program_id(0)
        part = jnp.sum(x_ref[...].astype(F32), axis=0, keepdims=True)

        @pl.when(i == 0)
        def _():
            o_ref[...] = part

        @pl.when(i > 0)
        def _():
            o_ref[...] += part

    return pl.pallas_call(
        body, name=name, grid=(S // tr,), in_specs=[pl.BlockSpec((tr, D), lambda i: (i, 0))],
        out_specs=pl.BlockSpec((1, D), lambda i: (0, 0)), out_shape=jax.ShapeDtypeStruct((1, D), F32),
        compiler_params=_cp(1),
    )(x)


def _loss(y, t, *, name="loss", tr=256):
    S, D = y.shape
    steps = S // tr

    def body(y_ref, t_ref, dy_ref, l_ref, acc_ref):
        i = pl.program_id(0)
        e = y_ref[...] - t_ref[...]
        dy_ref[...] = e * (1.0 / D)
        part = jnp.sum(e * e, axis=0, keepdims=True)

        @pl.when(i == 0)
        def _():
            acc_ref[...] = part

        @pl.when(i > 0)
        def _():
            acc_ref[...] += part

        @pl.when(i == steps - 1)
        def _():
            tot = jnp.sum(acc_ref[...], axis=1, keepdims=True) * (0.5 / D)
            l_ref[...] = jnp.broadcast_to(tot, (8, 128))

    row = pl.BlockSpec((tr, D), lambda i: (i, 0))
    return pl.pallas_call(
        body, name=name, grid=(steps,), in_specs=[row, row],
        out_specs=[row, pl.BlockSpec((8, 128), lambda i: (0, 0))],
        out_shape=[jax.ShapeDtypeStruct((S, D), F32), jax.ShapeDtypeStruct((8, 128), F32)],
        scratch_shapes=[pltpu.VMEM((1, D), F32)], compiler_params=_cp(1),
    )(y, t)


def _conv_pre(x, w_ref, b_ref, width):
    row = _iota(x.shape, 0)
    acc = b_ref[...] + w_ref[pl.ds(width - 1, 1), :] * x
    shifted = []
    for s in range(1, width):
        xs = jnp.where(row >= s, pltpu.roll(x, s, axis=0), 0.0)
        shifted.append(xs)
        acc = acc + w_ref[pl.ds(width - 1 - s, 1), :] * xs
    return acc, shifted


def _conv_back(dacc, x, shifted, w_ref, width):
    S = x.shape[0]
    row = _iota(x.shape, 0)
    dx = w_ref[pl.ds(width - 1, 1), :] * dacc
    dws = [None] * width
    dws[width - 1] = jnp.sum(dacc * x, axis=0, keepdims=True)
    for s in range(1, width):
        back = jnp.where(row < S - s, pltpu.roll(dacc, S - s, axis=0), 0.0)
        dx = dx + w_ref[pl.ds(width - 1 - s, 1), :] * back
        dws[width - 1 - s] = jnp.sum(dacc * shifted[s - 1], axis=0, keepdims=True)
    db = jnp.sum(dacc, axis=0, keepdims=True)
    return dx, dws, db


def _conv_silu_fwd(xin, col0, C, w, b, *, name, tc=512):
    S = xin.shape[0]
    width = w.shape[0]
    off = col0 // tc

    def body(x_ref, w_ref, b_ref, o_ref):
        acc, _ = _conv_pre(x_ref[...], w_ref, b_ref, width)
        o_ref[...] = acc * _sigmoid(acc)

    return pl.pallas_call(
        body, name=name, grid=(C // tc,),
        in_specs=[pl.BlockSpec((S, tc), lambda j: (0, j + off)), pl.BlockSpec((width, tc), lambda j: (0, j)),
                  pl.BlockSpec((1, tc), lambda j: (0, j))],
        out_specs=pl.BlockSpec((S, tc), lambda j: (0, j)), out_shape=jax.ShapeDtypeStruct((S, C), F32),
        compiler_params=_cp(1),
    )(xin, w, b)


def _conv_silu_bwd(xin, col0, C, w, b, douts, *, name, tc=256):
    S = xin.shape[0]
    width = w.shape[0]
    off = col0 // tc
    nd = len(douts)
    ranges = [(o // tc, (o + d.shape[1]) // tc) for d, o in douts]

    def body(*refs):
        x_ref, w_ref, b_ref = refs[0], refs[1], refs[2]
        d_refs = refs[3:3 + nd]
        dx_ref, dw_ref, db_ref = refs[3 + nd], refs[4 + nd], refs[5 + nd]
        j = pl.program_id(0)
        x = x_ref[...]
        acc, shifted = _conv_pre(x, w_ref, b_ref, width)
        sg = _sigmoid(acc)
        dout = jnp.zeros_like(x)
        for q in range(nd):
            lo, hi = ranges[q]
            dout = dout + jnp.where((j >= lo) & (j < hi), d_refs[q][...], 0.0)
        dacc = dout * (sg * (1.0 + acc * (1.0 - sg)))
        dx, dws, db = _conv_back(dacc, x, shifted, w_ref, width)
        dx_ref[...] = dx.astype(BF16)
        for k in range(width):
            dw_ref[pl.ds(k, 1), :] = dws[k]
        db_ref[...] = db

    d_specs = [pl.BlockSpec((S, tc), (lambda j, lo=lo, hi=hi: (0, jnp.clip(j - lo, 0, hi - lo - 1)))) for lo, hi in ranges]
    return pl.pallas_call(
        body, name=name, grid=(C // tc,),
        in_specs=[pl.BlockSpec((S, tc), lambda j: (0, j + off)), pl.BlockSpec((width, tc), lambda j: (0, j)),
                  pl.BlockSpec((1, tc), lambda j: (0, j))] + d_specs,
        out_specs=[pl.BlockSpec((S, tc), lambda j: (0, j)), pl.BlockSpec((width, tc), lambda j: (0, j)),
                   pl.BlockSpec((1, tc), lambda j: (0, j))],
        out_shape=[jax.ShapeDtypeStruct((S, C), BF16), jax.ShapeDtypeStruct((width, C), F32),
                   jax.ShapeDtypeStruct((1, C), F32)],
        compiler_params=_cp(1),
    )(xin, w, b, *[d for d, _ in douts])


def _ffn_act_fwd(u, w, b, *, name, tc=256):
    S, F2 = u.shape
    Fd = F2 // 2
    width = w.shape[0]
    nb = Fd // tc

    def body(g_ref, v_ref, w_ref, b_ref, o_ref):
        acc, _ = _conv_pre(g_ref[...], w_ref, b_ref, width)
        o_ref[...] = (acc * _sigmoid(acc) * v_ref[...]).astype(BF16)

    return pl.pallas_call(
        body, name=name, grid=(nb,),
        in_specs=[pl.BlockSpec((S, tc), lambda j: (0, j)), pl.BlockSpec((S, tc), lambda j: (0, j + nb)),
                  pl.BlockSpec((width, tc), lambda j: (0, j)), pl.BlockSpec((1, tc), lambda j: (0, j))],
        out_specs=pl.BlockSpec((S, tc), lambda j: (0, j)), out_shape=jax.ShapeDtypeStruct((S, Fd), BF16),
        compiler_params=_cp(1),
    )(u, u, w, b)


def _ffn_act_bwd(u, w, b, da, *, name, tc=256):
    S, F2 = u.shape
    Fd = F2 // 2
    width = w.shape[0]
    nb = Fd // tc

    def body(g_ref, v_ref, w_ref, b_ref, da_ref, du_ref, dw_ref, db_ref, a_ref):
        x = g_ref[...]
        acc, shifted = _conv_pre(x, w_ref, b_ref, width)
        sg = _sigmoid(acc)
        dav = da_ref[...]
        val = v_ref[...]
        silu = acc * sg
        a_ref[...] = (silu * val).astype(BF16)
        du_ref[1] = (dav * silu).astype(BF16)
        dacc = dav * val * (sg * (1.0 + acc * (1.0 - sg)))
        dx, dws, db = _conv_back(dacc, x, shifted, w_ref, width)
        du_ref[0] = dx.astype(BF16)
        for k in range(width):
            dw_ref[pl.ds(k, 1), :] = dws[k]
        db_ref[...] = db

    blk = pl.BlockSpec((S, tc), lambda j: (0, j))
    return pl.pallas_call(
        body, name=name, grid=(nb,),
        in_specs=[blk, pl.BlockSpec((S, tc), lambda j: (0, j + nb)), pl.BlockSpec((width, tc), lambda j: (0, j)),
                  pl.BlockSpec((1, tc), lambda j: (0, j)), blk],
        out_specs=[pl.BlockSpec((2, S, tc), lambda j: (0, 0, j)), pl.BlockSpec((width, tc), lambda j: (0, j)),
                   pl.BlockSpec((1, tc), lambda j: (0, j)), blk],
        out_shape=[jax.ShapeDtypeStruct((2, S, Fd), BF16),
                   jax.ShapeDtypeStruct((width, Fd), F32), jax.ShapeDtypeStruct((1, Fd), F32),
                   jax.ShapeDtypeStruct((S, Fd), BF16)],
        compiler_params=_cp(1),
    )(u, u, w, b, da)


def _ssd_prep(dtr, dt_bias, a_log, *, name="ssd_prep"):
    S = dtr.shape[0]

    def body(d_ref, b_ref, al_ref, dt_ref, ac_ref, sg_ref, act_ref):
        lane = _iota((CHUNK, 128), 1)
        valid = lane < SSM_HEADS
        z = d_ref[...] + b_ref[...]
        dt = jnp.where(valid, jnp.maximum(z, 0.0) + jnp.log(1.0 + jnp.exp(-jnp.abs(z))), 0.0)
        a = dt * (-jnp.exp(al_ref[...]))
        row = _iota((CHUNK, 128), 0)
        k = 1
        while k < CHUNK:
            a = a + jnp.where(row >= k, pltpu.roll(a, k, axis=0), 0.0)
            k *= 2
        sg = jnp.where(valid, _sigmoid(z), 0.0)
        for arr, ref in ((dt, dt_ref), (a, ac_ref), (sg, sg_ref)):
            for g in range(SSM_GROUPS):
                ref[g] = jnp.where(lane < 4, arr if g == 0 else pltpu.roll(arr, 128 - 4 * g, axis=1), 0.0)
        act_ref[...] = a.T[:SSM_HEADS, :]

    blk = pl.BlockSpec((CHUNK, 128), lambda i: (i, 0))
    vec = pl.BlockSpec((1, 128), lambda i: (0, 0))
    grp = pl.BlockSpec((SSM_GROUPS, CHUNK, 128), lambda i: (0, i, 0))
    return pl.pallas_call(
        body, name=name, grid=(S // CHUNK,), in_specs=[blk, vec, vec],
        out_specs=[grp, grp, grp, pl.BlockSpec((SSM_HEADS, CHUNK), lambda i: (0, i))],
        out_shape=[jax.ShapeDtypeStruct((SSM_GROUPS, S, 128), F32)] * 3 + [jax.ShapeDtypeStruct((SSM_HEADS, S), F32)],
        compiler_params=_cp(1),
    )(dtr, dt_bias, a_log)


def _expand4(v, lanes):
    out = jnp.broadcast_to(v[:, 3:4], lanes.shape)
    for hh in (2, 1, 0):
        out = jnp.where(lanes < 64 * (hh + 1), v[:, hh:hh + 1], out)
    return out


def _ssd_fwd(xbc, dt_g, ac_g, ac_t, *, name="ssd_fwd"):
    S = xbc.shape[0]
    nc = S // CHUNK
    Lc = CHUNK

    def body(x_ref, b_ref, c_ref, dt_ref, ac_ref, act_ref, y_ref, st_out_ref, st_ref):
        g = pl.program_id(0)
        c = pl.program_id(1)

        @pl.when(c == 0)
        def _():
            st_ref[...] = jnp.zeros_like(st_ref)

        bv = b_ref[...]
        cbf = c_ref[...].astype(BF16)
        cb = _dot(cbf, bv.astype(BF16), 1, 1)
        causal = _iota((Lc, Lc), 0) >= _iota((Lc, Lc), 1)
        lane256 = _iota((Lc, 256), 1)
        lane128 = _iota((Lc, 128), 1)
        row128 = _iota((128, 128), 0)
        dtg, acg = dt_ref[...], ac_ref[...]
        ac_last = ac_ref[pl.ds(Lc - 1, 1), :]
        dt4 = _expand4(dtg, lane256)
        ac4 = _expand4(acg, lane256)
        e4 = jnp.exp(ac4)
        xdb = (x_ref[...] * dt4).astype(BF16)
        st_out_ref[...] = st_ref[...]
        for p in range(2):
            xd_p = xdb[:, 128 * p:128 * (p + 1)]
            st_p = st_ref[p]
            ys, sn, cds = [], [], []
            for q in range(2):
                hh = 2 * p + q
                a_col = acg[:, hh:hh + 1]
                a_row = act_ref[pl.ds(4 * g + hh, 1), :]
                dec = jnp.exp(jnp.where(causal, a_col - a_row, NEG))
                w = (cb * dec).astype(BF16)
                ys.append(_dot(w, xd_p))
                al = ac_last[:, hh:hh + 1]
                dte = jnp.exp(al - a_col)
                sn.append(_dot(xd_p, (bv * dte).astype(BF16), 0, 0))
                cds.append(jnp.exp(al))
            y_diag = jnp.where(lane128 < 64, ys[0], ys[1])
            y_off = _dot(cbf, st_p.astype(BF16), 1, 1) * e4[:, 128 * p:128 * (p + 1)]
            y_ref[:, 128 * p:128 * (p + 1)] = y_diag + y_off
            st_ref[p] = jnp.where(row128 < 64, st_p * cds[0] + sn[0], st_p * cds[1] + sn[1])

    per_g = lambda g, c: (g, c, 0)
    return pl.pallas_call(
        body, name=name, grid=(SSM_GROUPS, nc),
        in_specs=[pl.BlockSpec((Lc, 256), lambda g, c: (c, g)),
                  pl.BlockSpec((Lc, 128), lambda g, c: (c, 16 + g)),
                  pl.BlockSpec((Lc, 128), lambda g, c: (c, 24 + g)),
                  pl.BlockSpec((None, Lc, 128), per_g), pl.BlockSpec((None, Lc, 128), per_g),
                  pl.BlockSpec((SSM_HEADS, Lc), lambda g, c: (0, c))],
        out_specs=[pl.BlockSpec((Lc, 256), lambda g, c: (c, g)),
                   pl.BlockSpec((None, None, 2, 128, 128), lambda g, c: (g, c, 0, 0, 0))],
        out_shape=[jax.ShapeDtypeStruct((S, 2048), F32), jax.ShapeDtypeStruct((SSM_GROUPS, nc, 2, 128, 128), F32)],
        scratch_shapes=[pltpu.VMEM((2, 128, 128), F32)], compiler_params=_cp(2),
    )(xbc, xbc, xbc, dt_g, ac_g, ac_t)


def _ssd_bwd(xbc, dt_g, ac_g, ac_t, states, dy, dexp, *, name="ssd_bwd"):
    S = xbc.shape[0]
    nc = S // CHUNK
    Lc = CHUNK

    def body(x_ref, b_ref, c_ref, dt_ref, ac_ref, act_ref, st_ref, dy_ref, d_ref, dx_ref, db_ref, dc_ref, dh_ref, ds_ref):
        g = pl.program_id(0)
        cc = pl.program_id(1)

        @pl.when(cc == 0)
        def _():
            ds_ref[...] = jnp.zeros_like(ds_ref)

        bv = b_ref[...]
        cv = c_ref[...]
        bbf, cbf = bv.astype(BF16), cv.astype(BF16)
        cb = _dot(cbf, bbf, 1, 1)
        causal = _iota((Lc, Lc), 0) >= _iota((Lc, Lc), 1)
        lane256 = _iota((Lc, 256), 1)
        lane128 = _iota((Lc, 128), 1)
        row128 = _iota((128, 128), 0)
        dtg, acg = dt_ref[...], ac_ref[...]
        ac_last = ac_ref[pl.ds(Lc - 1, 1), :]
        dt4 = _expand4(dtg, lane256)
        ac4 = _expand4(acg, lane256)
        acl4 = _expand4(ac_last, _iota((1, 256), 1))
        e4 = jnp.exp(ac4)
        dte4 = jnp.exp(acl4 - ac4)
        xv = x_ref[...]
        xd = xv * dt4
        xdb = xd.astype(BF16)
        dyv = dy_ref[...]
        dcb = jnp.zeros((Lc, Lc), F32)
        dc_acc = jnp.zeros((Lc, 128), F32)
        db_acc = jnp.zeros((Lc, 128), F32)
        ind_rows = _iota((256, 128), 0) >> 6
        ind_cols = _iota((256, 128), 1)
        ind_a = (ind_rows == ind_cols).astype(BF16)
        ind_b = (ind_rows + 4 == ind_cols).astype(BF16)
        u_parts, dxd_parts, ends = [], [], []
        for p in range(2):
            sl = slice(128 * p, 128 * (p + 1))
            xd_p, xdb_p, dy_p = xd[:, sl], xdb[:, sl], dyv[:, sl]
            dyb_p = dy_p.astype(BF16)
            e_p, dte_p = e4[:, sl], dte4[:, sl]
            sp = st_ref[p]
            spb = sp.astype(BF16)
            dsn = ds_ref[p]
            dsnb = dsn.astype(BF16)
            yds, dxds, cds = [], [], []
            for q in range(2):
                hh = 2 * p + q
                a_col = acg[:, hh:hh + 1]
                a_row = act_ref[pl.ds(4 * g + hh, 1), :]
                dec = jnp.exp(jnp.where(causal, a_col - a_row, NEG))
                w = (cb * dec).astype(BF16)
                head = (lane128 < 64) if q == 0 else (lane128 >= 64)
                dym = jnp.where(head, dyb_p, jnp.zeros_like(dyb_p))
                dw = _dot(dym, xdb_p, 1, 1)
                dcb = dcb + dw * dec
                yds.append(_dot(w, xdb_p))
                dxds.append(_dot(w, dyb_p, 0, 0))
                cds.append(jnp.exp(ac_last[:, hh:hh + 1]))
            y_diag = jnp.where(lane128 < 64, yds[0], yds[1])
            dxd_diag = jnp.where(lane128 < 64, dxds[0], dxds[1])
            y_off = _dot(cbf, spb, 1, 1) * e_p
            dgp = dy_p * e_p
            dgb = dgp.astype(BF16)
            dc_acc = dc_acc + _dot(dgb, spb)
            dsp = _dot(dgb, cbf, 0, 0)
            cd_col = jnp.where(row128[:, 0:1] < 64, cds[0], cds[1])
            qm = _dot(bbf, dsnb, 1, 1)
            dxd_state = dte_p * qm
            db_acc = db_acc + _dot((xd_p * dte_p).astype(BF16), dsnb)
            t_p = xd_p * dxd_state
            prod = dsn * sp
            e0 = jnp.sum(jnp.sum(jnp.where(row128 < 64, prod, 0.0), axis=1, keepdims=True), axis=0, keepdims=True)
            e1 = jnp.sum(jnp.sum(jnp.where(row128 >= 64, prod, 0.0), axis=1, keepdims=True), axis=0, keepdims=True)
            tcol = jnp.sum(t_p, axis=0, keepdims=True)
            lane1 = _iota((1, 128), 1)
            t0 = jnp.sum(jnp.where(lane1 < 64, tcol, 0.0), axis=1, keepdims=True)
            t1 = jnp.sum(jnp.where(lane1 >= 64, tcol, 0.0), axis=1, keepdims=True)
            ends.append(e0 * cds[0] + t0)
            ends.append(e1 * cds[1] + t1)
            ds_ref[p] = dsn * cd_col + dsp
            u_parts.append(dyb_p.astype(F32) * y_diag - xdb_p.astype(F32) * dxd_diag + dy_p * y_off - t_p)
            dxd_parts.append(dxd_diag + dxd_state)
        dxd = jnp.concatenate(dxd_parts, axis=1)
        u_all = jnp.concatenate(u_parts, axis=1)
        dx_ref[...] = dxd * dt4 + dyv * d_ref[...]
        dcbb = dcb.astype(BF16)
        dc_ref[...] = dc_acc + _dot(dcbb, bbf)
        db_ref[...] = db_acc + _dot(dcbb, cbf, 0, 0)
        lane = _iota((Lc, 128), 1)
        endv = jnp.zeros((Lc, 128), F32)
        for hh in range(4):
            endv = jnp.where(lane == 8 + hh, ends[hh], endv)
        dh_ref[...] = _dot3(dxd * xv, ind_a) + _dot3(u_all, ind_b) + endv

    rev = lambda c: nc - 1 - c
    per_g = lambda g, c: (g, rev(c), 0)
    return pl.pallas_call(
        body, name=name, grid=(SSM_GROUPS, nc),
        in_specs=[pl.BlockSpec((Lc, 256), lambda g, c: (rev(c), g)),
                  pl.BlockSpec((Lc, 128), lambda g, c: (rev(c), 16 + g)),
                  pl.BlockSpec((Lc, 128), lambda g, c: (rev(c), 24 + g)),
                  pl.BlockSpec((None, Lc, 128), per_g), pl.BlockSpec((None, Lc, 128), per_g),
                  pl.BlockSpec((SSM_HEADS, Lc), lambda g, c: (0, rev(c))),
                  pl.BlockSpec((None, None, 2, 128, 128), lambda g, c: (g, rev(c), 0, 0, 0)),
                  pl.BlockSpec((Lc, 256), lambda g, c: (rev(c), g)),
                  pl.BlockSpec((1, 256), lambda g, c: (0, g))],
        out_specs=[pl.BlockSpec((Lc, 256), lambda g, c: (rev(c), g)),
                   pl.BlockSpec((Lc, 128), lambda g, c: (rev(c), g)),
                   pl.BlockSpec((Lc, 128), lambda g, c: (rev(c), g)),
                   pl.BlockSpec((None, Lc, 128), per_g)],
        out_shape=[jax.ShapeDtypeStruct((S, 2048), F32), jax.ShapeDtypeStruct((S, 1024), F32),
                   jax.ShapeDtypeStruct((S, 1024), F32), jax.ShapeDtypeStruct((SSM_GROUPS, S, 128), F32)],
        scratch_shapes=[pltpu.VMEM((2, 128, 128), F32)], compiler_params=_cp(2),
    )(xbc, xbc, xbc, dt_g, ac_g, ac_t, states, dy, dexp)


def _ssd_post(dhead, dt_g, sg_g, alog_g, *, name="ssd_post"):
    S = dhead.shape[1]
    nc = S // CHUNK
    Lc = CHUNK

    def body(dh_ref, dt_ref, sg_ref, al_ref, o_ref, s_ref):
        @pl.when(pl.program_id(0) == 0)
        def _():
            s_ref[...] = jnp.zeros_like(s_ref)

        lane = _iota((Lc, 128), 1)
        row = _iota((Lc, 128), 0)
        row8 = _iota((8, 128), 0)
        out = jnp.zeros((Lc, 128), F32)
        for g in range(SSM_GROUPS):
            dh = dh_ref[g]
            a_neg = -jnp.exp(al_ref[g])
            dac = jnp.where(lane < 4, pltpu.roll(dh, 124, axis=1), 0.0)
            end = jnp.where(lane < 4, pltpu.roll(dh, 120, axis=1), 0.0)
            k = 1
            while k < Lc:
                dac = dac + jnp.where(row < Lc - k, pltpu.roll(dac, Lc - k, axis=0), 0.0)
                k *= 2
            da = dac + end
            ddt = jnp.where(lane < 4, da * a_neg + dh, 0.0)
            ddtr = ddt * sg_ref[g]
            out = out + (ddtr if g == 0 else pltpu.roll(ddtr, 4 * g, axis=1))
            dal = jnp.sum(da * dt_ref[g], axis=0, keepdims=True) * a_neg
            dbias = jnp.sum(ddtr, axis=0, keepdims=True)
            part = jnp.where(row8 == 0, dal, jnp.where(row8 == 1, dbias, 0.0))
            s_ref[g] += part
        o_ref[...] = out.astype(BF16)

    grp = pl.BlockSpec((SSM_GROUPS, Lc, 128), lambda c: (0, c, 0))
    whole = lambda r: pl.BlockSpec((SSM_GROUPS, r, 128), lambda c: (0, 0, 0))
    return pl.pallas_call(
        body, name=name, grid=(nc,), in_specs=[grp, grp, grp, whole(1)],
        out_specs=[pl.BlockSpec((Lc, 128), lambda c: (c, 0)), whole(8)],
        out_shape=[jax.ShapeDtypeStruct((S, 128), BF16), jax.ShapeDtypeStruct((SSM_GROUPS, 8, 128), F32)],
        compiler_params=_cp(1),
    )(dhead, dt_g, sg_g, alog_g)


def _gate_fwd(y, xbc, zx, dexp, gn, *, name="gate_fwd", tr=256):
    S = y.shape[0]
    W = 2048
    gw = W // SSM_GROUPS

    def body(y_ref, x_ref, z_ref, d_ref, g_ref, o_ref):
        z = z_ref[...]
        u = (y_ref[...] + x_ref[...] * d_ref[...]) * (z * _sigmoid(z))
        gv = g_ref[...]
        for q in range(SSM_GROUPS):
            sl = slice(gw * q, gw * (q + 1))
            uq = u[:, sl]
            r = lax.rsqrt(jnp.mean(uq * uq, axis=-1, keepdims=True) + EPS)
            o_ref[:, sl] = (uq * r * gv[:, sl]).astype(BF16)

    row = pl.BlockSpec((tr, W), lambda i: (i, 0))
    vec = pl.BlockSpec((1, W), lambda i: (0, 0))
    return pl.pallas_call(
        body, name=name, grid=(S // tr,), in_specs=[row, row, row, vec, vec], out_specs=row,
        out_shape=jax.ShapeDtypeStruct((S, W), BF16), compiler_params=_cp(1),
    )(y, xbc, zx, dexp, gn)


def _gate_bwd(y, xbc, zx, dexp, gn, dout, *, name="gate_bwd", tr=256):
    S = y.shape[0]
    W = 2048
    gw = W // SSM_GROUPS
    steps = S // tr

    def body(y_ref, x_ref, z_ref, d_ref, g_ref, do_ref, dy_ref, dz_ref, dg_ref, dd_ref, acc_ref):
        i = pl.program_id(0)

        @pl.when(i == 0)
        def _():
            acc_ref[...] = jnp.zeros_like(acc_ref)

        z = z_ref[...]
        sg = _sigmoid(z)
        sz = z * sg
        xs = x_ref[...]
        yt = y_ref[...] + xs * d_ref[...]
        u = yt * sz
        gv = g_ref[...]
        do = do_ref[...]
        dgs = []
        for q in range(SSM_GROUPS):
            sl = slice(gw * q, gw * (q + 1))
            uq = u[:, sl]
            r = lax.rsqrt(jnp.mean(uq * uq, axis=-1, keepdims=True) + EPS)
            uh = uq * r
            dq = do[:, sl]
            duh = dq * gv[:, sl]
            duq = r * (duh - uh * jnp.mean(duh * uh, axis=-1, keepdims=True))
            dgs.append(jnp.sum(dq * uh, axis=0, keepdims=True))
            dyt = duq * sz[:, sl]
            dy_ref[:, sl] = dyt
            dz_ref[:, sl] = (duq * yt[:, sl] * (sg[:, sl] * (1.0 + z[:, sl] * (1.0 - sg[:, sl])))).astype(BF16)
            acc_ref[:, sl] += jnp.sum(dyt * xs[:, sl], axis=0, keepdims=True)
        dg = jnp.concatenate(dgs, axis=1)

        @pl.when(i == 0)
        def _():
            dg_ref[...] = dg

        @pl.when(i > 0)
        def _():
            dg_ref[...] += dg

        @pl.when(i == steps - 1)
        def _():
            ind = ((_iota((W, 128), 0) >> 6) == _iota((W, 128), 1)).astype(BF16)
            dd_ref[...] = _dot3(jnp.broadcast_to(acc_ref[...], (8, W)), ind)[0:1, :]

    row = pl.BlockSpec((tr, W), lambda i: (i, 0))
    vec = pl.BlockSpec((1, W), lambda i: (0, 0))
    return pl.pallas_call(
        body, name=name, grid=(steps,), in_specs=[row, row, row, vec, vec, row],
        out_specs=[row, row, vec, pl.BlockSpec((1, 128), lambda i: (0, 0))],
        out_shape=[jax.ShapeDtypeStruct((S, W), F32), jax.ShapeDtypeStruct((S, W), BF16),
                   jax.ShapeDtypeStruct((1, W), F32), jax.ShapeDtypeStruct((1, 128), F32)],
        scratch_shapes=[pltpu.VMEM((1, W), F32)], compiler_params=_cp(1),
    )(y, xbc, zx, dexp, gn, dout)


def _rope_cs(posf, *, name="rope_tables", tr=256):
    S = posf.shape[0]

    def body(p_ref, c_ref, s_ref):
        j = (_iota((tr, 128), 1) & 31).astype(F32)
        ang = p_ref[...] * jnp.exp(j * (-math.log(ROPE_THETA) / 32.0))
        c_ref[...] = jnp.cos(ang)
        s_ref[...] = jnp.sin(ang)

    blk = pl.BlockSpec((tr, 128), lambda i: (i, 0))
    return pl.pallas_call(
        body, name=name, grid=(S // tr,), in_specs=[pl.BlockSpec((tr, 1), lambda i: (i, 0))], out_specs=[blk, blk],
        out_shape=[jax.ShapeDtypeStruct((S, 128), F32)] * 2, compiler_params=_cp(1),
    )(posf)


def _rope_tables(c_ref, s_ref, shape):
    reps = shape[1] // 128
    return jnp.tile(c_ref[...], (1, reps)), jnp.tile(s_ref[...], (1, reps)), (_iota(shape, 1) & 63) < 32


def _hn_inds(W):
    ind = ((_iota((W, 128), 0) >> 6) == _iota((W, 128), 1)).astype(BF16)
    ind_t = ((_iota((128, W), 1) >> 6) == _iota((128, W), 0)).astype(BF16)
    return ind, ind_t


def _hnrope_fwd(xin, col0, W, gain_w, rope, *, name, tr=256):
    S = xin.shape[0]
    off = col0 // W
    nh = W // HEAD

    def body(x_ref, g_ref, c_ref, s_ref, o_ref):
        x = x_ref[...]
        ind, ind_t = _hn_inds(W)
        r = lax.rsqrt(_dot3(x * x, ind) * (1.0 / HEAD) + EPS)
        xn = x * _dot3(r, ind_t) * g_ref[...]
        cs, sn, half = _rope_tables(c_ref, s_ref, (tr, W))
        rot = jnp.where(half, -pltpu.roll(xn, W - 32, axis=1), pltpu.roll(xn, 32, axis=1))
        out = (xn * cs + rot * sn).astype(BF16)
        for h in range(nh):
            o_ref[h] = out[:, HEAD * h:HEAD * (h + 1)]

    tab = pl.BlockSpec((tr, 128), lambda i: (i, 0))
    return pl.pallas_call(
        body, name=name, grid=(S // tr,),
        in_specs=[pl.BlockSpec((tr, W), lambda i: (i, off)), pl.BlockSpec((1, W), lambda i: (0, 0)), tab, tab],
        out_specs=pl.BlockSpec((nh, tr, HEAD), lambda i: (0, i, 0)), out_shape=jax.ShapeDtypeStruct((nh, S, HEAD), BF16),
        compiler_params=_cp(1),
    )(xin, gain_w, *rope)


def _hnrope_bwd(xin, col0, W, gain_w, rope, dout, *, name, tr=256):
    S = xin.shape[0]
    off = col0 // W
    steps = S // tr
    nh = W // HEAD

    def body(x_ref, g_ref, c_ref, s_ref, do_ref, dx_ref, cs_ref, dg_ref, acc_ref):
        i = pl.program_id(0)
        x = x_ref[...]
        ind, ind_t = _hn_inds(W)
        r = lax.rsqrt(_dot3(x * x, ind) * (1.0 / HEAD) + EPS)
        rw = _dot3(r, ind_t)
        xh = x * rw
        cs, sn, half = _rope_tables(c_ref, s_ref, (tr, W))
        do = jnp.concatenate([do_ref[h] for h in range(nh)], axis=1).astype(F32)
        gs = do * sn
        g1 = do * cs + jnp.where(half, pltpu.roll(gs, W - 32, axis=1), -pltpu.roll(gs, 32, axis=1))
        dxh = g1 * g_ref[...]
        t = _dot3(dxh * xh, ind) * (1.0 / HEAD)
        dx = rw * (dxh - xh * _dot3(t, ind_t))
        dx_ref[...] = dx.astype(BF16)
        cpart = jnp.sum(dx, axis=0, keepdims=True)
        gpart = jnp.sum(g1 * xh, axis=0, keepdims=True)

        @pl.when(i == 0)
        def _():
            cs_ref[...] = cpart
            acc_ref[...] = gpart

        @pl.when(i > 0)
        def _():
            cs_ref[...] += cpart
            acc_ref[...] += gpart

        @pl.when(i == steps - 1)
        def _():
            fold = ((_iota((W, 128), 0) & 63) == _iota((W, 128), 1)).astype(BF16)
            dg_ref[...] = _dot3(jnp.broadcast_to(acc_ref[...], (8, W)), fold)[0:1, :]

    tab = pl.BlockSpec((tr, 128), lambda i: (i, 0))
    return pl.pallas_call(
        body, name=name, grid=(steps,),
        in_specs=[pl.BlockSpec((tr, W), lambda i: (i, off)), pl.BlockSpec((1, W), lambda i: (0, 0)), tab, tab,
                  pl.BlockSpec((nh, tr, HEAD), lambda i: (0, i, 0))],
        out_specs=[pl.BlockSpec((tr, W), lambda i: (i, 0)), pl.BlockSpec((1, W), lambda i: (0, 0)),
                   pl.BlockSpec((1, 128), lambda i: (0, 0))],
        out_shape=[jax.ShapeDtypeStruct((S, W), BF16), jax.ShapeDtypeStruct((1, W), F32),
                   jax.ShapeDtypeStruct((1, 128), F32)],
        scratch_shapes=[pltpu.VMEM((1, W), F32)], compiler_params=_cp(1),
    )(xin, gain_w, *rope, dout)


def _attn_probs(q, kb, sink_ref, h, i):
    s = _dot(q, kb, 1, 1) * (HEAD ** -0.5)
    r = _iota((4 * WINDOW, 2 * WINDOW), 0)
    ki = _iota((4 * WINDOW, 2 * WINDOW), 1)
    rel = (r & (WINDOW - 1)) + WINDOW - ki
    mask = (rel >= 0) & (rel < WINDOW) & ((ki >= WINDOW) | (i > 0))
    s = jnp.where(mask, s, NEG)
    r1 = _iota((4 * WINDOW, 1), 0)
    sink = jnp.where(r1 < WINDOW, sink_ref[4 * h], jnp.where(r1 < 2 * WINDOW, sink_ref[4 * h + 1],
                     jnp.where(r1 < 3 * WINDOW, sink_ref[4 * h + 2], sink_ref[4 * h + 3])))
    m = jnp.maximum(jnp.max(s, axis=1, keepdims=True), sink)
    p = jnp.exp(s - m)
    ps = jnp.exp(sink - m)
    inv = 1.0 / (jnp.sum(p, axis=1, keepdims=True) + ps)
    return p * inv, ps * inv


ATT_HPS = 2


def _attn_specs(S):
    qspec = pl.BlockSpec((ATT_HPS, ATT_G, WINDOW, HEAD), lambda h, i: (h, 0, i, 0))
    cur = pl.BlockSpec((ATT_HPS, WINDOW, HEAD), lambda h, i: (h, i, 0))
    prev = pl.BlockSpec((ATT_HPS, WINDOW, HEAD), lambda h, i: (h, jnp.maximum(i - 1, 0), 0))
    tok = pl.BlockSpec((WINDOW, ATT_HPS * ATT_G * HEAD), lambda h, i: (i, h))
    return qspec, cur, prev, tok


def _attn_fwd(qh, kh, vh, sinks, *, name="attn_fwd"):
    S = kh.shape[1]
    nb = S // WINDOW

    def body(s_ref, q_ref, kc_ref, kp_ref, vc_ref, vp_ref, o_ref):
        h2, i = pl.program_id(0), pl.program_id(1)
        outs = []
        for hh in range(ATT_HPS):
            q = q_ref[hh].reshape(ATT_G * WINDOW, HEAD)
            kb = jnp.concatenate([kp_ref[hh], kc_ref[hh]], axis=0)
            vb = jnp.concatenate([vp_ref[hh], vc_ref[hh]], axis=0)
            probs, _ = _attn_probs(q, kb, s_ref, ATT_HPS * h2 + hh, i)
            o = _dot(probs.astype(BF16), vb).astype(BF16)
            outs += [o[WINDOW * g:WINDOW * (g + 1)] for g in range(ATT_G)]
        o_ref[...] = jnp.concatenate(outs, axis=1)

    qspec, cur, prev, tok = _attn_specs(S)
    return pl.pallas_call(
        body, name=name, grid=(ATT_KV // ATT_HPS, nb),
        in_specs=[pl.BlockSpec(memory_space=pltpu.SMEM), qspec, cur, prev, cur, prev], out_specs=tok,
        out_shape=jax.ShapeDtypeStruct((S, ATT_KV * ATT_G * HEAD), BF16), compiler_params=_cp(2),
    )(sinks, qh, kh, kh, vh, vh)


def _attn_bwd(qh, kh, vh, sinks, doh, *, name="attn_bwd"):
    S = kh.shape[1]
    nb = S // WINDOW

    def body(s_ref, q_ref, kc_ref, kp_ref, vc_ref, vp_ref, do_ref, dq_ref, dk_ref, dv_ref, dsk_ref):
        h2, i = pl.program_id(0), pl.program_id(1)

        @pl.when(i == 0)
        def _():
            dk_ref[...] = jnp.zeros_like(dk_ref)
            dv_ref[...] = jnp.zeros_like(dv_ref)
            dsk_ref[...] = jnp.zeros_like(dsk_ref)

        dov = do_ref[...]
        cur = pl.multiple_of(i * WINDOW, WINDOW)
        lane = _iota((8, 128), 1)
        row = _iota((8, 128), 0)
        scale = HEAD ** -0.5
        for hh in range(ATT_HPS):
            q = q_ref[hh].reshape(ATT_G * WINDOW, HEAD)
            do = jnp.concatenate([dov[:, HEAD * (ATT_G * hh + g):HEAD * (ATT_G * hh + g + 1)] for g in range(ATT_G)], axis=0)
            kb = jnp.concatenate([kp_ref[hh], kc_ref[hh]], axis=0)
            vb = jnp.concatenate([vp_ref[hh], vc_ref[hh]], axis=0)
            probs, psink = _attn_probs(q, kb, s_ref, ATT_HPS * h2 + hh, i)
            dp = _dot(do, vb, 1, 1)
            delta = jnp.sum(probs * dp, axis=1, keepdims=True)
            ds = (probs * (dp - delta)).astype(BF16)
            dq_ref[hh] = (_dot(ds, kb) * scale).reshape(ATT_G, WINDOW, HEAD)
            dkb = _dot(ds, q, 0, 0) * scale
            dvb = _dot(probs.astype(BF16), do, 0, 0)
            dk_ref[hh, pl.ds(cur, WINDOW), :] += dkb[WINDOW:, :]
            dv_ref[hh, pl.ds(cur, WINDOW), :] += dvb[WINDOW:, :]

            @pl.when(i > 0)
            def _():
                prv = pl.multiple_of((i - 1) * WINDOW, WINDOW)
                dk_ref[hh, pl.ds(prv, WINDOW), :] += dkb[:WINDOW, :]
                dv_ref[hh, pl.ds(prv, WINDOW), :] += dvb[:WINDOW, :]

            dsr = -psink * delta
            upd = jnp.zeros((8, 128), F32)
            for gq in range(ATT_G):
                v = jnp.sum(dsr[gq * WINDOW:(gq + 1) * WINDOW, :], axis=0, keepdims=True)
                upd = jnp.where((lane == gq) & (row == 0), v, upd)
            dsk_ref[hh] += upd

    qspec, cur, prev, tok = _attn_specs(S)
    full = pl.BlockSpec((ATT_HPS, S, HEAD), lambda h, i: (h, 0, 0))
    return pl.pallas_call(
        body, name=name, grid=(ATT_KV // ATT_HPS, nb),
        in_specs=[pl.BlockSpec(memory_space=pltpu.SMEM), qspec, cur, prev, cur, prev, tok],
        out_specs=[qspec, full, full, pl.BlockSpec((ATT_HPS, 8, 128), lambda h, i: (h, 0, 0))],
        out_shape=[jax.ShapeDtypeStruct((ATT_KV, ATT_G, S, HEAD), F32), jax.ShapeDtypeStruct((ATT_KV, S, HEAD), F32),
                   jax.ShapeDtypeStruct((ATT_KV, S, HEAD), F32), jax.ShapeDtypeStruct((ATT_KV, 8, 128), F32)],
        compiler_params=_cp(2),
    )(sinks, qh, kh, kh, vh, vh, doh)


def _heads_major(t, nh):
    S = t.shape[0]
    return t.reshape(S, nh, HEAD).transpose(1, 0, 2)


def _tokens_major(t):
    nh, S, _ = t.shape
    return t.transpose(1, 0, 2).reshape(S, nh * HEAD)


class _NoComm:
    def late_weights(self, w, after):
        return w

    def grads(self, group, tensors, after):
        return None


def _local_step(x, posf, target, w, comm=None):
    S, D = x.shape
    gr = {}
    comm = comm or _NoComm()

    (h1,) = _rms_fwd(x, [w["a_norm"]], name="a_norm_f", dep=w.get("dep"))
    zx = _mm(h1, w["w_zx"], name="in_proj_zx")
    dtr = _mm(h1, w["w_dt"], name="in_proj_dt")
    xbc = _conv_silu_fwd(zx, 2048, 4096, w["a_conv_w"], w["a_conv_b"], name="a_conv_f")
    dt_g, ac_g, sg_g, ac_t = _ssd_prep(dtr, w["a_dt_bias"], w["a_A_log"])
    y_ssd, states = _ssd_fwd(xbc, dt_g, ac_g, ac_t)
    yg = _gate_fwd(y_ssd, xbc, zx, w["a_Dexp"], w["a_gnorm"])
    x1 = _mm(yg, w["a_out_proj"], res=x, name="out_proj")

    w = comm.late_weights(w, x1)
    FW = w["f_w_in"][0].shape[2]

    def ffn_fwd(xin, l):
        (h,) = _rms_fwd(xin, [w["f_norm"][l]], name=f"f_norm_f{l}")
        u = _mm(h, w["f_w_in"][l], name=f"f_in{l}", dims=(S, N_CHIPS * FW, D), tn=FW,
                b_spec=pl.BlockSpec((None, D, FW), lambda i, j, k: (j, 0, 0)))
        a = _ffn_act_fwd(u, w["f_conv_w"][l], w["f_conv_b"][l], name=f"f_act_f{l}")
        xo = _mm(a, w["f_w_down"][l], res=xin, tk=a.shape[1], name=f"f_down{l}")
        return xo, (h, u)

    x2, ffn0 = ffn_fwd(x1, 0)

    hk, hq = _rms_fwd(x2, [w["kv_norm"], w["b_norm"]], name="kvq_norm_f")
    kv = _mm(hk, w["w_kv"], bias=w["b_kv"], name="kv_proj")
    q = _mm(hq, w["w_q"], bias=w["b_q"], name="q_proj")
    rope = _rope_cs(posf)
    kr = _hnrope_fwd(kv, 0, 256, w["k_norm_w"], rope, name="k_rope_f")
    qr = _hnrope_fwd(q, 0, 1024, w["q_norm_w"], rope, name="q_rope_f")
    qh = qr.reshape(ATT_KV, ATT_G, S, HEAD)
    kh = kr
    vh = _heads_major(kv[:, 256:].astype(BF16), ATT_KV)
    att = _attn_fwd(qh, kh, vh, w["sinks"])
    x3 = _mm(att, w["w_o"], bias=w["b_o"], res=x2, name="o_proj")
    x4, ffn1 = ffn_fwd(x3, 1)

    dy, loss_part = _loss(x4, target)

    def ffn_bwd(xin, l, saved, dyo, want_colsum, dep=None):
        h, u = saved
        da = _mm(dyo, w["f_w_down"][l], tb=True, name=f"f_down_dx{l}", dep=dep)
        du, dcw, dcb, a = _ffn_act_bwd(u, w["f_conv_w"][l], w["f_conv_b"][l], da, name=f"f_act_b{l}")
        dw_down = _mm(a, dyo, ta=True, out_dtype=BF16, name=f"f_down_dw{l}")
        dw_in = _mm(h, du, ta=True, out_dtype=BF16, name=f"f_in_dw{l}", dims=(D, N_CHIPS * FW, S), tm=D, tn=FW, tk=S,
                    b_spec=pl.BlockSpec((None, S, FW), lambda i, j, k: (j // 2, 0, j % 2)),
                    o_spec=pl.BlockSpec((None, D, FW), lambda i, j, k: (j, i, 0)), o_shape=(N_CHIPS, D, FW))
        ts = _pick(S, (1024, 512, 256))
        pieces = [(pl.BlockSpec((None, ts, FW), lambda i, j, k, q=q: (q // 2, i, q % 2)),
                   pl.BlockSpec((None, 512, FW), lambda i, j, k, q=q: (q, j, 0))) for q in range(N_CHIPS)]
        dh = _mm(du, w["f_w_in"][l], tb=True, name=f"f_in_dx{l}", dims=(S, D, FW), tm=ts, tn=512, tk=FW,
                 a_spec=pieces[0][0], b_spec=pieces[0][1], more=pieces[1:])
        outs = _rms_bwd(xin, [w["f_norm"][l]], [dh], dyo, name=f"f_norm_b{l}", want_colsum=want_colsum)
        g = dict(f_norm=outs[1], f_w_in=dw_in, f_conv_w=dcw, f_conv_b=dcb, f_w_down=dw_down)
        return outs[0], g, (outs[2] if want_colsum else None)

    dx3, gr["ffn1"], db_o = ffn_bwd(x3, 1, ffn1, dy, True)
    gr["b_o"] = db_o
    gr["w_o"] = _mm(att, dx3, ta=True, out_dtype=BF16, name="o_proj_dw")
    datt = _mm(dx3, w["w_o"], tb=True, out_dtype=BF16, name="o_proj_dx")
    dqh, dkh, dvh, dsk = _attn_bwd(qh, kh, vh, w["sinks"], datt)
    gr["sinks"] = dsk[:, 0, :4].reshape(1, 16)
    dv = _tokens_major(dvh).astype(BF16)
    dq, db_q, dqn = _hnrope_bwd(q, 0, 1024, w["q_norm_w"], rope, dqh.reshape(16, S, HEAD), name="q_rope_b")
    dk, db_k, dkn = _hnrope_bwd(kv, 0, 256, w["k_norm_w"], rope, dkh, name="k_rope_b")
    gr["q_norm"], gr["k_norm"] = dqn[:, :HEAD], dkn[:, :HEAD]
    gr["b_q"] = db_q
    gr["b_kv"] = jnp.concatenate([db_k, _colsum(dv, name="dv_colsum")], axis=1)
    dkv = jnp.concatenate([dk, dv], axis=1)
    gr["w_q"] = _mm(hq, dq, ta=True, out_dtype=BF16, name="q_proj_dw")
    gr["w_kv"] = _mm(hk, dkv, ta=True, out_dtype=BF16, name="kv_proj_dw")
    tok = comm.grads(1, dict(f_down1=gr["ffn1"]["f_w_down"], f_in1=gr["ffn1"]["f_w_in"], w_o=gr["w_o"], w_q=gr["w_q"],
                             w_kv=gr["w_kv"]), None)
    dhq = _mm(dq, w["w_q"], tb=True, name="q_proj_dx", dep=tok)
    dhk = _mm(dkv, w["w_kv"], tb=True, name="kv_proj_dx")
    dx2, gr["kv_norm"], gr["b_norm"] = _rms_bwd(x2, [w["kv_norm"], w["b_norm"]], [dhk, dhq], dx3, name="kvq_norm_b")

    dx1, gr["ffn0"], _ = ffn_bwd(x1, 0, ffn0, dx2, False)

    tok = comm.grads(2, dict(f_down0=gr["ffn0"]["f_w_down"], f_in0=gr["ffn0"]["f_w_in"]), dx1)
    gr["a_out_proj"] = _mm(yg, dx1, ta=True, out_dtype=BF16, name="out_proj_dw")
    dyg = _mm(dx1, w["a_out_proj"], tb=True, name="out_proj_dx", dep=tok)
    dy_ssd, dz, gr["a_gnorm"], dD = _gate_bwd(y_ssd, xbc, zx, w["a_Dexp"], w["a_gnorm"], dyg)
    gr["a_D"] = dD[:, :SSM_HEADS]
    dxs, dB, dC, dhead = _ssd_bwd(xbc, dt_g, ac_g, ac_t, states, dy_ssd, w["a_Dexp"])
    ddtr, dsmall = _ssd_post(dhead, dt_g, sg_g, w["a_A_log_g"])
    gr["a_A_log"] = dsmall[:, 0, :4].reshape(1, SSM_HEADS)
    gr["a_dt_bias"] = dsmall[:, 1, :4].reshape(1, SSM_HEADS)
    dxbc, gr["a_conv_w"], gr["a_conv_b"] = _conv_silu_bwd(
        zx, 2048, 4096, w["a_conv_w"], w["a_conv_b"], [(dxs, 0), (dB, 2048), (dC, 3072)], name="a_conv_b")
    gr["w_z"] = _mm(h1, dz, ta=True, out_dtype=BF16, name="in_proj_dwz")
    gr["w_x"] = _mm(h1, dxbc, ta=True, out_dtype=BF16, name="in_proj_dwx")
    gr["w_dt"] = _mm(h1, ddtr, ta=True, out_dtype=BF16, name="in_proj_dwdt")
    dh1 = _mm(dz, w["w_zx"], tb=True, name="in_proj_dxz")
    ts = _pick(S, (1024, 512, 256))
    halves = [(pl.BlockSpec((ts, 2048), lambda i, j, k, q=q: (i, q)), pl.BlockSpec((512, 2048), lambda i, j, k, q=q: (j, 1 + q)))
              for q in range(2)]
    dh1 = _mm(dxbc, w["w_zx"], tb=True, res=dh1, name="in_proj_dxx", dims=(S, D, 2048), tm=ts, tn=512, tk=2048,
              a_spec=halves[0][0], b_spec=halves[0][1], more=halves[1:])
    dh1 = _mm(ddtr, w["w_dt"], tb=True, res=dh1, name="in_proj_dxdt")
    dx0, gr["a_norm"] = _rms_bwd(x, [w["a_norm"]], [dh1], dx1, name="a_norm_b")
    comm.grads(3, dict(out_proj=gr["a_out_proj"], in_proj=_in_proj_grad(gr).reshape(D, N_CHIPS, -1).transpose(1, 0, 2)), dx0)
    return loss_part, dx0, gr


def _prep_small(full, w):
    w["a_norm"] = full["a_norm"]
    w["a_conv_w"] = full["a_conv_w"][0]
    w["a_conv_b"] = full["a_conv_b"]
    pad32 = lambda v: jnp.pad(v, ((0, 0), (0, 128 - SSM_HEADS)))
    w["a_dt_bias"] = pad32(full["a_dt_bias"])
    w["a_A_log"] = pad32(full["a_A_log"])
    w["a_A_log_g"] = jnp.pad(full["a_A_log"].reshape(SSM_GROUPS, 1, 4), ((0, 0), (0, 0), (0, 124)))
    w["a_Dexp"] = jnp.repeat(full["a_D"], HEAD, axis=1)
    w["a_gnorm"] = full["a_gnorm"]
    w["f_norm"] = [full["f_norm"][l:l + 1] for l in range(2)]
    w["f_conv_w"] = [full["f_conv_w"][l] for l in range(2)]
    w["f_conv_b"] = [full["f_conv_b"][l:l + 1] for l in range(2)]
    w["kv_norm"] = full["kv_norm"].reshape(1, -1)
    w["b_kv"] = full["b_kv"].reshape(1, -1)
    w["k_norm_w"] = jnp.tile(full["k_norm"].reshape(1, HEAD), (1, ATT_KV))
    w["b_norm"] = full["b_norm"]
    w["b_q"] = full["b_q"]
    w["q_norm_w"] = jnp.tile(full["q_norm"], (1, ATT_KV * ATT_G))
    w["sinks"] = full["sinks"].reshape(-1)
    w["b_o"] = full["b_o"]
    return w


def _split_in_proj(ip):
    return ip[:, :6144].astype(BF16), jnp.pad(ip[:, 6144:], ((0, 0), (0, 128 - SSM_HEADS))).astype(BF16)


def _prep_weights(full):
    w = _prep_small(full, {})
    w["w_zx"], w["w_dt"] = _split_in_proj(full["a_in_proj"][0])
    w["a_out_proj"] = full["a_out_proj"][0].astype(BF16)
    w["f_w_in"] = [full["f_w_in"][l].reshape(1024, N_CHIPS, -1).transpose(1, 0, 2).astype(BF16) for l in range(2)]
    w["f_w_down"] = [full["f_w_down"][l].astype(BF16) for l in range(2)]
    w["w_kv"] = full["w_kv"].astype(BF16)
    w["w_q"] = full["w_q"][0].astype(BF16)
    w["w_o"] = full["w_o"][0].astype(BF16)
    return w


def _small_grads(gr):
    g = {}
    g["a_norm"] = gr["a_norm"]
    g["a_conv_w"] = gr["a_conv_w"][None]
    g["a_conv_b"] = gr["a_conv_b"]
    g["a_dt_bias"], g["a_A_log"], g["a_D"] = gr["a_dt_bias"], gr["a_A_log"], gr["a_D"]
    g["a_gnorm"] = gr["a_gnorm"]
    g["kv_norm"] = gr["kv_norm"].reshape(-1)
    g["b_kv"] = gr["b_kv"].reshape(-1)
    g["k_norm"] = gr["k_norm"].reshape(-1)
    g["b_norm"] = gr["b_norm"]
    g["b_q"] = gr["b_q"]
    g["q_norm"] = gr["q_norm"]
    g["sinks"] = gr["sinks"]
    g["b_o"] = gr["b_o"]
    f = [gr["ffn0"], gr["ffn1"]]
    g["f_norm"] = jnp.concatenate([f[0]["f_norm"], f[1]["f_norm"]], axis=0)
    g["f_conv_w"] = jnp.stack([f[l]["f_conv_w"] for l in range(2)])
    g["f_conv_b"] = jnp.concatenate([f[l]["f_conv_b"] for l in range(2)], axis=0)
    return g


def _in_proj_grad(gr):
    return jnp.concatenate([gr["w_z"], gr["w_x"], gr["w_dt"][:, :SSM_HEADS]], axis=1)


def _full_grads(gr):
    g = _small_grads(gr)
    f32 = lambda t: t.astype(F32)
    g["a_in_proj"] = f32(_in_proj_grad(gr))[None]
    g["a_out_proj"] = f32(gr["a_out_proj"])[None]
    g["w_kv"] = f32(gr["w_kv"])
    g["w_q"] = f32(gr["w_q"])[None]
    g["w_o"] = f32(gr["w_o"])[None]
    f = [gr["ffn0"], gr["ffn1"]]
    g["f_w_in"] = jnp.stack([f32(f[l]["f_w_in"]).transpose(1, 0, 2).reshape(1024, -1) for l in range(2)])
    g["f_w_down"] = jnp.stack([f32(f[l]["f_w_down"]) for l in range(2)])
    return g


MESH = pl.DeviceIdType.MESH
WEIGHTS = ("a_norm", "a_in_proj", "a_conv_w", "a_conv_b", "a_dt_bias", "a_A_log", "a_D", "a_gnorm", "a_out_proj",
           "kv_norm", "w_kv", "b_kv", "k_norm", "b_norm", "w_q", "b_q", "q_norm", "sinks", "w_o", "b_o", "f_norm",
           "f_w_in", "f_conv_w", "f_conv_b", "f_w_down")
MATS = (("in_proj", "a_in_proj", 0), ("out_proj", "a_out_proj", 0), ("w_kv", "w_kv", None), ("w_q", "w_q", 0),
        ("w_o", "w_o", 0), ("f_in0", "f_w_in", 0), ("f_in1", "f_w_in", 1), ("f_down0", "f_w_down", 0),
        ("f_down1", "f_w_down", 1))
SMALL_CUT = (("a_norm", 1), ("a_conv_w", 2), ("a_conv_b", 1), ("a_gnorm", 1), ("f_conv_w", 2))
SMALL_REP = ("a_dt_bias", "a_A_log", "a_D", "kv_norm", "b_kv", "k_norm", "b_norm", "b_q", "q_norm", "sinks", "b_o",
             "f_norm", "f_conv_b")


def _coords():
    return lax.axis_index("x"), lax.axis_index("y"), lax.axis_index("c")


def _other_chips(x, y):
    return [(1 - x, y), (x, 1 - y), (1 - x, 1 - y)]


def _pack(arrs, rows_align, lanes, dtype):
    flat = jnp.concatenate([a.reshape(-1).astype(dtype) for a in arrs])
    per = rows_align * lanes
    total = -(-flat.shape[0] // per) * per
    return jnp.pad(flat, (0, total - flat.shape[0])).reshape(total // lanes, lanes)


def _unpack(flat, shapes):
    out, off = [], 0
    for s in shapes:
        n = math.prod(s)
        out.append(flat[off:off + n].reshape(s))
        off += n
    return out


def _remote(src, dst, send, recv, k, dev):
    return pltpu.make_async_remote_copy(src_ref=src, dst_ref=dst, send_sem=send.at[k], recv_sem=recv.at[k],
                                        device_id=dev, device_id_type=MESH)


_ANY = pl.BlockSpec(memory_space=pl.ANY)


def _halves(t):
    r, c = t.shape
    return t.reshape(2, r // 2, c)


def _gather_weights(shards, sp):
    n = len(shards)
    n_sem = 7 * n + 3

    def body(*refs):
        sh, sp_ref = refs[:n], refs[n]
        outs, sout = refs[n + 1:2 * n + 1], refs[2 * n + 1]
        send, recv, loc = refs[2 * n + 2:]
        x, y, c = _coords()
        me = 2 * x + y
        chips = _other_chips(x, y)
        sib = (x, y, 1 - c)
        l1 = pltpu.make_async_copy(sp_ref, sout.at[me], loc.at[0])
        l1.start()
        sends = []
        for j, (cx, cy) in enumerate(chips):
            sends.append(_remote(sp_ref, sout.at[me], send, recv, 7 * n + j, (cx, cy, c)))
            for t in range(n):
                sends.append(_remote(sh[t].at[c], outs[t].at[me, c], send, recv, 7 * t + j, (cx, cy, c)))
        for t in range(n):
            sends.append(_remote(sh[t], outs[t].at[me], send, recv, 7 * t + 6, sib))
        for cp in sends:
            cp.start()
        for j, (cx, cy) in enumerate(chips):
            src = 2 * cx + cy
            for t in range(n):
                _remote(sh[t].at[c], outs[t].at[src, c], send, recv, 7 * t + j, (cx, cy, c)).wait_recv()
                fwd = _remote(outs[t].at[src, c], outs[t].at[src, c], send, recv, 7 * t + 3 + j, sib)
                fwd.start()
                sends.append(fwd)
        for j, (cx, cy) in enumerate(chips):
            src = 2 * cx + cy
            _remote(sp_ref, sout.at[src], send, recv, 7 * n + j, (cx, cy, c)).wait_recv()
            for t in range(n):
                _remote(outs[t].at[src, 1 - c], outs[t].at[src, 1 - c], send, recv, 7 * t + 3 + j, sib).wait_recv()
        for t in range(n):
            _remote(sh[t], outs[t].at[me], send, recv, 7 * t + 6, sib).wait_recv()
        for cp in sends:
            cp.wait_send()
        l1.wait()

    res = pl.pallas_call(
        body, name="gather_weights", in_specs=[_ANY] * (n + 1), out_specs=[_ANY] * (n + 1),
        out_shape=[jax.ShapeDtypeStruct((N_CHIPS,) + t.shape, t.dtype) for t in shards]
        + [jax.ShapeDtypeStruct((N_CHIPS,) + sp.shape, sp.dtype)],
        scratch_shapes=[pltpu.SemaphoreType.DMA((n_sem,)), pltpu.SemaphoreType.DMA((n_sem,)),
                        pltpu.SemaphoreType.DMA((1,))],
    )(*shards, sp)
    return res[:n], res[n]


_HBM = pl.BlockSpec(memory_space=pltpu.HBM)
_SEMS = pl.BlockSpec(memory_space=pltpu.SEMAPHORE)
_DATAFLOW = pltpu.SideEffectType.DATAFLOW_SIDE_EFFECTING


def _in_hbm(a):
    return pltpu.with_memory_space_constraint(a, pltpu.HBM)


def _gather_copies(sh, land, send, recv):
    x, y, c = _coords()
    me = 2 * x + y
    out = []
    for t in range(len(sh)):
        for j, (cx, cy) in enumerate(_other_chips(x, y)):
            dev = (cx, cy, c)
            out.append((_remote(sh[t].at[c], land[t].at[me, c], send, recv, 4 * t + j, dev),
                        _remote(sh[t].at[c], land[t].at[2 * cx + cy, c], send, recv, 4 * t + j, dev)))
        sib = (x, y, 1 - c)
        out.append((_remote(sh[t], land[t].at[me], send, recv, 4 * t + 3, sib),
                    _remote(sh[t], land[t].at[me], send, recv, 4 * t + 3, sib)))
    return out


def _gather_start(shards, after, *, name):
    n = len(shards)

    def body(*refs):
        sh, land = refs[:n], refs[n:2 * n]
        send, recv = refs[2 * n + 1], refs[2 * n + 2]
        token = refs[-1]
        for mine, _ in _gather_copies(sh, land, send, recv):
            mine.start()
        token[...] = jnp.zeros_like(token)

    lands = [_in_hbm(lax.empty((N_CHIPS,) + s.shape, s.dtype)) for s in shards]
    res = pl.pallas_call(
        body, name=name, in_specs=[_HBM] * (2 * n) + [_ANY],
        out_specs=[_SEMS, _SEMS] + [_HBM] * (2 * n) + [pl.BlockSpec(memory_space=pltpu.VMEM)],
        out_shape=[pltpu.SemaphoreType.DMA((4 * n,)), pltpu.SemaphoreType.DMA((4 * n,))]
        + [pltpu.HBM(s.shape, s.dtype) for s in shards] + [pltpu.HBM(l.shape, l.dtype) for l in lands]
        + [jax.ShapeDtypeStruct((8, 128), F32)],
        input_output_aliases={t: 2 + t for t in range(2 * n)},
        compiler_params=pltpu.CompilerParams(has_side_effects=_DATAFLOW),
    )(*[_in_hbm(s) for s in shards], *lands, after)
    return res[0], res[1], res[2:2 + n], res[2 + n:2 + 2 * n], res[-1]


def _gather_wait(send, recv, shards, lands, after, *, name):
    n = len(shards)

    def body(*refs):
        sh, land = refs[:n], refs[n:2 * n]
        send_r, recv_r = refs[2 * n], refs[2 * n + 1]
        for mine, theirs in _gather_copies(sh, land, send_r, recv_r):
            mine.wait_send()
            theirs.wait_recv()

    res = pl.pallas_call(
        body, name=name, in_specs=[_HBM] * (2 * n) + [_SEMS, _SEMS, _ANY], out_specs=[_HBM] * (2 * n),
        out_shape=[pltpu.HBM(s.shape, s.dtype) for s in shards] + [pltpu.HBM(l.shape, l.dtype) for l in lands],
        input_output_aliases={t: t for t in range(2 * n)},
        compiler_params=pltpu.CompilerParams(has_side_effects=_DATAFLOW),
    )(*shards, *lands, send, recv, after)
    return res[n:]


def _gather_forward(lands, *, name):
    n = len(lands)

    def body(*refs):
        o = refs[n:2 * n]
        send, recv = refs[2 * n:]
        x, y, c = _coords()
        sib = (x, y, 1 - c)
        srcs = [2 * cx + cy for cx, cy in _other_chips(x, y)]
        cps = [_remote(o[t].at[s, c], o[t].at[s, c], send, recv, 3 * t + j, sib) for t in range(n) for j, s in enumerate(srcs)]
        for cp in cps:
            cp.start()
        for t in range(n):
            for j, s in enumerate(srcs):
                _remote(o[t].at[s, 1 - c], o[t].at[s, 1 - c], send, recv, 3 * t + j, sib).wait_recv()
        for cp in cps:
            cp.wait_send()

    return pl.pallas_call(
        body, name=name, in_specs=[_ANY] * n, out_specs=[_ANY] * n, input_output_aliases={t: t for t in range(n)},
        out_shape=[jax.ShapeDtypeStruct(l.shape, l.dtype) for l in lands],
        scratch_shapes=[pltpu.SemaphoreType.DMA((3 * n,)), pltpu.SemaphoreType.DMA((3 * n,))],
    )(*lands)


def _allreduce_small(v):
    SR = v.shape[0]

    def body(v_ref, o_ref, buf, send, recv):
        x, y, c = _coords()
        me = 4 * x + 2 * y + c
        buf[me] = v_ref[...]
        peers = []
        for k in range(1, 8):
            px = 1 - x if k & 4 else x
            py = 1 - y if k & 2 else y
            pc = 1 - c if k & 1 else c
            peers.append((px, py, pc))
        cps = [_remote(v_ref, buf.at[me], send, recv, k, p) for k, p in enumerate(peers)]
        for cp in cps:
            cp.start()
        for k, (px, py, pc) in enumerate(peers):
            _remote(v_ref, buf.at[4 * px + 2 * py + pc], send, recv, k, (px, py, pc)).wait_recv()
        for cp in cps:
            cp.wait_send()
        acc = buf[0]
        for s in range(1, 8):
            acc = acc + buf[s]
        o_ref[...] = acc

    vm = pl.BlockSpec(memory_space=pltpu.VMEM)
    return pl.pallas_call(
        body, name="allreduce_small", in_specs=[vm], out_specs=vm, out_shape=jax.ShapeDtypeStruct(v.shape, F32),
        scratch_shapes=[pltpu.VMEM((8, SR, 128), F32), pltpu.SemaphoreType.DMA((7,)), pltpu.SemaphoreType.DMA((7,))],
    )(v)


def _rs_to_sibling(gs, *, name):
    n = len(gs)

    def body(*refs):
        g, a = refs[:n], refs[n:2 * n]
        send, recv = refs[2 * n:]
        x, y, c = _coords()
        cps = [_remote(g[t].at[:, 1 - c], a[t], send, recv, t, (x, y, 1 - c)) for t in range(n)]
        for cp in cps:
            cp.start()
        for cp in cps:
            cp.wait()

    return pl.pallas_call(
        body, name=name, in_specs=[_ANY] * n, out_specs=[_ANY] * n,
        out_shape=[jax.ShapeDtypeStruct((N_CHIPS,) + g.shape[2:], g.dtype) for g in gs],
        scratch_shapes=[pltpu.SemaphoreType.DMA((n,)), pltpu.SemaphoreType.DMA((n,))],
    )(*gs)


RS_ROW_SPLIT = 2


def _rs_add_pair(gs, as_, c_idx, *, name):
    n = len(gs)

    def body(c_ref, *refs):
        for t in range(n):
            refs[2 * n + t][...] = (refs[t][...].astype(F32) + refs[n + t][...].astype(F32)).astype(BF16)

    def gspec(g):
        _, _, rh, cols = g.shape
        return pl.BlockSpec((None, None, rh // RS_ROW_SPLIT, cols), lambda j, i, c_ref: (j, c_ref[0], i, 0))

    def pspec(g):
        _, _, rh, cols = g.shape
        return pl.BlockSpec((None, rh // RS_ROW_SPLIT, cols), lambda j, i, c_ref: (j, i, 0))

    return pl.pallas_call(
        body, name=name,
        grid_spec=pltpu.PrefetchScalarGridSpec(
            num_scalar_prefetch=1, grid=(N_CHIPS, RS_ROW_SPLIT),
            in_specs=[gspec(g) for g in gs] + [pspec(g) for g in gs], out_specs=[pspec(g) for g in gs]),
        out_shape=[jax.ShapeDtypeStruct((N_CHIPS,) + g.shape[2:], BF16) for g in gs], compiler_params=_cp(2),
    )(c_idx, *gs, *as_)


def _chips_copies(p, r, send, recv):
    x, y, c = _coords()
    return [_remote(p[t].at[2 * cx + cy], r[t].at[k], send, recv, 3 * t + k, (cx, cy, c))
            for k, (cx, cy) in enumerate(_other_chips(x, y)) for t in range(len(p))]


def _rs_chips_start(ps, *, name):
    n = len(ps)

    def body(*refs):
        p, r = refs[:n], refs[n:2 * n]
        send, recv = refs[2 * n], refs[2 * n + 1]
        token = refs[-1]
        for cp in _chips_copies(p, r, send, recv):
            cp.start()
        token[...] = jnp.zeros_like(token)

    lands = [_in_hbm(lax.empty((3,) + p.shape[1:], p.dtype)) for p in ps]
    res = pl.pallas_call(
        body, name=name, in_specs=[_HBM] * (2 * n),
        out_specs=[_SEMS, _SEMS] + [_HBM] * (2 * n) + [pl.BlockSpec(memory_space=pltpu.VMEM)],
        out_shape=[pltpu.SemaphoreType.DMA((3 * n,)), pltpu.SemaphoreType.DMA((3 * n,))]
        + [pltpu.HBM(p.shape, p.dtype) for p in ps] + [pltpu.HBM(l.shape, l.dtype) for l in lands]
        + [jax.ShapeDtypeStruct((8, 128), F32)],
        input_output_aliases={t: 2 + t for t in range(2 * n)},
        compiler_params=pltpu.CompilerParams(has_side_effects=_DATAFLOW),
    )(*[_in_hbm(p) for p in ps], *lands)
    return res[0], res[1], res[2:2 + n], res[2 + n:2 + 2 * n], res[-1]


def _rs_chips_wait(send, recv, ps, lands, after, *, name):
    n = len(ps)

    def body(*refs):
        p, r = refs[:n], refs[n:2 * n]
        for cp in _chips_copies(p, r, refs[2 * n], refs[2 * n + 1]):
            cp.wait_send()
            cp.wait_recv()

    res = pl.pallas_call(
        body, name=name, in_specs=[_HBM] * (2 * n) + [_SEMS, _SEMS] + [_ANY] * len(after), out_specs=[_HBM] * (2 * n),
        out_shape=[pltpu.HBM(p.shape, p.dtype) for p in ps] + [pltpu.HBM(l.shape, l.dtype) for l in lands],
        input_output_aliases={t: t for t in range(2 * n)},
        compiler_params=pltpu.CompilerParams(has_side_effects=_DATAFLOW),
    )(*ps, *lands, send, recv, *after)
    return res[:n], res[n:]


def _rs_add_chips(ps, rs, idx, *, name):
    n = len(ps)

    def body(idx_ref, *refs):
        for t in range(n):
            p_ref, r0, r1, r2 = refs[4 * t:4 * t + 4]
            refs[4 * n + t][...] = ((p_ref[...].astype(F32) + r0[...].astype(F32)) + r1[...].astype(F32)) + r2[...].astype(F32)

    in_specs, args = [], []
    for p, r in zip(ps, rs):
        _, rh, cols = p.shape
        blk = (None, rh // RS_ROW_SPLIT, cols)
        in_specs.append(pl.BlockSpec(blk, lambda i, idx_ref: (idx_ref[0], i, 0)))
        in_specs += [pl.BlockSpec(blk, lambda i, idx_ref, k=k: (k, i, 0)) for k in range(3)]
        args += [p, r, r, r]
    out_specs = [pl.BlockSpec((None, p.shape[1] // RS_ROW_SPLIT, p.shape[2]), lambda i, idx_ref: (idx_ref[1], i, 0))
                 for p in ps]
    return pl.pallas_call(
        body, name=name,
        grid_spec=pltpu.PrefetchScalarGridSpec(num_scalar_prefetch=1, grid=(RS_ROW_SPLIT,), in_specs=in_specs,
                                               out_specs=out_specs),
        out_shape=[jax.ShapeDtypeStruct((2,) + p.shape[1:], F32) for p in ps], compiler_params=_cp(1),
    )(idx, *args)


def _rs_join_halves(hs, *, name):
    n = len(hs)

    def body(*refs):
        o = refs[n:2 * n]
        send, recv = refs[2 * n:]
        x, y, c = _coords()
        cps = [_remote(o[t].at[c], o[t].at[c], send, recv, t, (x, y, 1 - c)) for t in range(n)]
        for cp in cps:
            cp.start()
        for t in range(n):
            _remote(o[t].at[1 - c], o[t].at[1 - c], send, recv, t, (x, y, 1 - c)).wait_recv()
        for cp in cps:
            cp.wait_send()

    return pl.pallas_call(
        body, name=name, in_specs=[_ANY] * n, out_specs=[_ANY] * n,
        input_output_aliases={t: t for t in range(n)},
        out_shape=[jax.ShapeDtypeStruct(h.shape, F32) for h in hs],
        scratch_shapes=[pltpu.SemaphoreType.DMA((n,)), pltpu.SemaphoreType.DMA((n,))],
    )(*hs)


def _adamw(w, gs, m, v, *, name, dep=None):
    L, Rr, C = w.shape
    tr, tc = _pick(Rr, (256, 128, 64)), C
    if tr == Rr and Rr * C > 512 * 1024:
        tc = 256
    bc1 = 1.0 - ADAM_B1 ** ADAM_STEP
    bc2 = 1.0 - ADAM_B2 ** ADAM_STEP
    nd = 0 if dep is None else 1

    def body(*refs):
        w_ref, m_ref, v_ref = refs[0], refs[1], refs[2]
        g_refs = refs[3:3 + L]
        d_ref, mo_ref, vo_ref, go_ref = refs[3 + L + nd:]
        layer = pl.program_id(0)
        gv = g_refs[0][...]
        for q in range(1, L):
            gv = jnp.where(layer == q, g_refs[q][...], gv)
        mn = ADAM_B1 * m_ref[...] + (1.0 - ADAM_B1) * gv
        vn = ADAM_B2 * v_ref[...] + (1.0 - ADAM_B2) * (gv * gv)
        go_ref[...] = gv
        mo_ref[...] = mn
        vo_ref[...] = vn
        d_ref[...] = -ADAM_LR * ((mn / bc1) / (jnp.sqrt(vn / bc2) + ADAM_EPS) + ADAM_WD * w_ref[...])

    blk = pl.BlockSpec((None, tr, tc), lambda l, i, j: (l, i, j))
    gblk = pl.BlockSpec((tr, tc), lambda l, i, j: (i, j))
    return pl.pallas_call(
        body, name=name, grid=(L, Rr // tr, C // tc), in_specs=[blk] * 3 + [gblk] * L + [_ANY] * nd, out_specs=[blk] * 4,
        out_shape=[jax.ShapeDtypeStruct((L, Rr, C), F32)] * 4, compiler_params=_cp(3),
    )(w, m, v, *gs, *([] if dep is None else [dep]))


def kernel(x, positions, a_norm, a_in_proj, a_conv_w, a_conv_b, a_dt_bias, a_A_log, a_D, a_gnorm, a_out_proj,
           kv_norm, w_kv, b_kv, k_norm, b_norm, w_q, b_q, q_norm, sinks, w_o, b_o, f_norm, f_w_in, f_conv_w,
           f_conv_b, f_w_down, loss_target, m_a_norm, m_a_in_proj, m_a_conv_w, m_a_conv_b, m_a_dt_bias, m_a_A_log,
           m_a_D, m_a_gnorm, m_a_out_proj, m_kv_norm, m_w_kv, m_b_kv, m_k_norm, m_b_norm, m_w_q, m_b_q, m_q_norm,
           m_sinks, m_w_o, m_b_o, m_f_norm, m_f_w_in, m_f_conv_w, m_f_conv_b, m_f_w_down, v_a_norm, v_a_in_proj,
           v_a_conv_w, v_a_conv_b, v_a_dt_bias, v_a_A_log, v_a_D, v_a_gnorm, v_a_out_proj, v_kv_norm, v_w_kv,
           v_b_kv, v_k_norm, v_b_norm, v_w_q, v_b_q, v_q_norm, v_sinks, v_w_o, v_b_o, v_f_norm, v_f_w_in,
           v_f_conv_w, v_f_conv_b, v_f_w_down):
    wl = dict(zip(WEIGHTS, (a_norm, a_in_proj, a_conv_w, a_conv_b, a_dt_bias, a_A_log, a_D, a_gnorm, a_out_proj,
                            kv_norm, w_kv, b_kv, k_norm, b_norm, w_q, b_q, q_norm, sinks, w_o, b_o, f_norm, f_w_in,
                            f_conv_w, f_conv_b, f_w_down)))
    ml = dict(zip(WEIGHTS, (m_a_norm, m_a_in_proj, m_a_conv_w, m_a_conv_b, m_a_dt_bias, m_a_A_log, m_a_D, m_a_gnorm,
                            m_a_out_proj, m_kv_norm, m_w_kv, m_b_kv, m_k_norm, m_b_norm, m_w_q, m_b_q, m_q_norm,
                            m_sinks, m_w_o, m_b_o, m_f_norm, m_f_w_in, m_f_conv_w, m_f_conv_b, m_f_w_down)))
    vl = dict(zip(WEIGHTS, (v_a_norm, v_a_in_proj, v_a_conv_w, v_a_conv_b, v_a_dt_bias, v_a_A_log, v_a_D, v_a_gnorm,
                            v_a_out_proj, v_kv_norm, v_w_kv, v_b_kv, v_k_norm, v_b_norm, v_w_q, v_b_q, v_q_norm,
                            v_sinks, v_w_o, v_b_o, v_f_norm, v_f_w_in, v_f_conv_w, v_f_conv_b, v_f_w_down)))
    xi, yi, ci = _coords()
    me = 2 * xi + yi
    S = x.shape[1]

    def block_of(n, layer):
        t = wl[n]
        return t if layer is None else t[layer]

    rows = lambda t: t.reshape(-1, t.shape[-1])
    c_idx = jnp.reshape(ci, (1,)).astype(jnp.int32)
    me_c = jnp.stack([me, ci]).astype(jnp.int32)
    early = ("in_proj", "out_proj")
    late = tuple(name for name, _, _ in MATS if name not in early)
    shards = {name: _halves(block_of(wn, layer).astype(BF16)) for name, wn, layer in MATS}

    sp = _pack([wl[n] for n, _ in SMALL_CUT], 8, 128, F32)
    gathered, gs = _gather_weights([shards[k] for k in early], sp)
    gt = {k: t.reshape(N_CHIPS, -1, t.shape[-1]) for k, t in zip(early, gathered)}
    started = _gather_start([shards[k] for k in late], gs, name="gather_late_start")
    full = {n: wl[n] for n in SMALL_REP}
    gs = gs.reshape(N_CHIPS, -1)
    pieces = [_unpack(gs[j], [wl[n].shape for n, _ in SMALL_CUT]) for j in range(N_CHIPS)]
    for q, (n, ax) in enumerate(SMALL_CUT):
        full[n] = jnp.concatenate([pieces[j][q] for j in range(N_CHIPS)], axis=ax)
    w = _prep_small(full, {})
    w["w_zx"], w["w_dt"] = _split_in_proj(gt["in_proj"].transpose(1, 0, 2).reshape(1024, -1))
    w["a_out_proj"] = rows(gt["out_proj"])
    w["dep"] = started[4]

    class Comm:
        pending = None
        token = None
        reduced = {}

        def late_weights(self, w, after):
            lands = _gather_wait(started[0], started[1], started[2], started[3], after, name="gather_late_wait")
            lands = _gather_forward(lands, name="gather_late_forward")
            lt = {k: t.reshape(N_CHIPS, -1, t.shape[-1]) for k, t in zip(late, lands)}
            w = dict(w)
            w["w_kv"], w["w_q"], w["w_o"] = (rows(lt[k]) for k in ("w_kv", "w_q", "w_o"))
            w["f_w_in"] = [lt["f_in0"], lt["f_in1"]]
            w["f_w_down"] = [rows(lt["f_down0"]), rows(lt["f_down1"])]
            return w

        def finish(self, after):
            names, send, recv, ps, lands, tag = self.pending
            ps, rs = _rs_chips_wait(send, recv, ps, lands, after, name=f"rs_chips_wait{tag}")
            halves = _rs_add_chips(ps, rs, me_c, name=f"rs_add_chips{tag}")
            joined = _rs_join_halves(halves, name=f"rs_join_halves{tag}")
            self.reduced.update({k: rows(t) for k, t in zip(names, joined)})
            self.pending = None

        def grads(self, group, tensors, after):
            if self.pending is not None:
                self.finish([after])
            names = list(tensors)
            glist = [tensors[k].reshape(N_CHIPS, 2, -1, tensors[k].shape[-1]) for k in names]
            from_sib = _rs_to_sibling(glist, name=f"rs_to_sibling{group}")
            pairs = _rs_add_pair(glist, from_sib, c_idx, name=f"rs_add_pair{group}")
            send, recv, ps, lands, token = _rs_chips_start(pairs, name=f"rs_chips_start{group}")
            self.pending = (names, send, recv, ps, lands, group)
            self.token = token
            return token

    comm = Comm()

    posf = positions.reshape(S, 1).astype(F32)
    loss_part, dx0, gr = _local_step(x[0], posf, loss_target[0], w, comm)
    g = _small_grads(gr)

    small_names = [n for n, _ in SMALL_CUT] + list(SMALL_REP)
    sv = _pack([g[n] for n in small_names] + [loss_part[0:1, 0:1]], 8, 128, F32)
    sred = _allreduce_small(sv).reshape(-1)
    small_shapes = [g[n].shape for n in small_names] + [(1,)]
    sg = dict(zip(small_names + ["loss"], _unpack(sred, small_shapes)))
    loss = sg["loss"].reshape(())
    g_small = {}
    for n, ax in SMALL_CUT:
        size = wl[n].shape[ax]
        g_small[n] = lax.dynamic_slice_in_dim(sg[n], me * size, size, axis=ax)
    for n in SMALL_REP:
        g_small[n] = sg[n].reshape(wl[n].shape)

    grads, delta, new_m, new_v = {}, {}, {}, {}

    def update(wn, dep):
        gl = [comm.reduced[name] for name, n2, _ in MATS if n2 == wn]
        shp = wl[wn].shape
        three = (len(gl),) + gl[0].shape
        flip = shp[-1] % 128 != 0
        view = (lambda t: t.reshape(three).transpose(0, 2, 1)) if flip else (lambda t: t.reshape(three))
        back = (lambda t: t.transpose(0, 2, 1).reshape(shp)) if flip else (lambda t: t.reshape(shp))
        if flip:
            gl = [t.T for t in gl]
        d, mn, vn, go = _adamw(view(wl[wn]), gl, view(ml[wn]), view(vl[wn]), name="adamw_" + wn, dep=dep)
        grads[wn], delta[wn], new_m[wn], new_v[wn] = back(go), back(d), back(mn), back(vn)
        return d

    comm.finish([update(wn, comm.token) for wn in ("f_w_in", "f_w_down", "w_q", "w_o", "w_kv")])
    for wn in ("a_in_proj", "a_out_proj"):
        update(wn, None)
    pk = lambda d: _pack([d[n] for n in small_names], 8, 128, F32)[None]
    d, mn, vn, _ = _adamw(pk(wl), [pk(g_small)[0]], pk(ml), pk(vl), name="adamw_small")
    shapes = [wl[n].shape for n in small_names]
    for n, dd, mm, vv in zip(small_names, _unpack(d.reshape(-1), shapes), _unpack(mn.reshape(-1), shapes),
                             _unpack(vn.reshape(-1), shapes)):
        grads[n], delta[n], new_m[n], new_v[n] = g_small[n], dd, mm, vv

    return (loss, dx0[None], *[grads[n] for n in WEIGHTS], *[delta[n] for n in WEIGHTS],
            *[new_m[n] for n in WEIGHTS], *[new_v[n] for n in WEIGHTS])
```

```python
import math

import jax
import jax.numpy as jnp
from jax import lax
from jax.experimental import pallas as pl
from jax.experimental.pallas import tpu as pltpu

F32 = jnp.float32
BF16 = jnp.bfloat16

EPS = 1e-5
CHUNK = 256
WINDOW = 128
HEAD = 64
SSM_HEADS = 32
SSM_GROUPS = 8
SSM_STATE = 128
ATT_KV = 4
ATT_G = 4
ROPE_THETA = 10000.0
NEG = -1e30
N_CHIPS = 4
VMEM_LIMIT = 56 * 1024 * 1024

ADAM_LR, ADAM_B1, ADAM_B2, ADAM_EPS, ADAM_WD, ADAM_STEP = 0.001, 0.9, 0.999, 1e-08, 0.01, 10


def _cp(n_axes):
    return pltpu.CompilerParams(dimension_semantics=("arbitrary",) * n_axes, vmem_limit_bytes=VMEM_LIMIT)


def _pick(dim, prefs):
    for p in prefs:
        if dim % p == 0:
            return p
    return dim


def _iota(shape, dim):
    return lax.broadcasted_iota(jnp.int32, shape, dim)


def _dot(a, b, ca=1, cb=0):
    return lax.dot_general(a, b, (((ca,), (cb,)), ((), ())), preferred_element_type=F32)


def _dot3(x, ind):
    h = x.astype(BF16)
    r = x - h.astype(F32)
    m = r.astype(BF16)
    lo = (r - m.astype(F32)).astype(BF16)
    return _dot(h, ind) + _dot(m, ind) + _dot(lo, ind)


def _sigmoid(x):
    return jax.nn.sigmoid(x)


def _mm(a, b, *, name, ta=False, tb=False, bias=None, res=None, out_dtype=F32, b_koff=0, tm=None, tn=None, tk=None,
        dims=None, a_spec=None, b_spec=None, o_spec=None, o_shape=None, dep=None, more=()):
    if dims is not None:
        M, N, K = dims
    else:
        if ta:
            K, M = a.shape
        else:
            M, K = a.shape
        N = b.shape[0] if tb else b.shape[1]
    tm = tm or _pick(M, (1024, 1408, 512, 256, 128))
    tn = tn or _pick(N, (512, 1408, 256, 128))
    tk = tk or (K if K <= 2048 else _pick(K, (2048, 1408, 1024, 512)))
    assert M % tm == 0 and N % tn == 0 and K % tk == 0 and b_koff % tk == 0
    nk = K // tk
    kb0 = b_koff // tk
    has_bias, has_res = bias is not None, res is not None

    def body(*refs):
        a_ref, b_ref = refs[0], refs[1]
        pos = 2
        bias_ref = res_ref = acc_ref = None
        if has_bias:
            bias_ref = refs[pos]
            pos += 1
        if has_res:
            res_ref = refs[pos]
            pos += 1
        if dep is not None:
            pos += 1
        extra = refs[pos:pos + 2 * len(more)]
        pos += 2 * len(more)
        o_ref = refs[pos]
        if nk > 1:
            acc_ref = refs[pos + 1]
        part = _dot(a_ref[...].astype(BF16), b_ref[...].astype(BF16), 0 if ta else 1, 1 if tb else 0)
        for q in range(len(more)):
            part = part + _dot(extra[2 * q][...].astype(BF16), extra[2 * q + 1][...].astype(BF16),
                               0 if ta else 1, 1 if tb else 0)

        def finish(acc):
            if has_bias:
                acc = acc + bias_ref[...]
            if has_res:
                acc = acc + res_ref[...]
            o_ref[...] = acc.astype(out_dtype)

        if nk == 1:
            finish(part)
        else:
            k = pl.program_id(2)

            @pl.when(k == 0)
            def _():
                acc_ref[...] = part

            @pl.when(k > 0)
            def _():
                acc_ref[...] += part

            @pl.when(k == nk - 1)
            def _():
                finish(acc_ref[...])

    if a_spec is None:
        a_spec = pl.BlockSpec((tk, tm), lambda i, j, k: (k, i)) if ta else pl.BlockSpec((tm, tk), lambda i, j, k: (i, k))
    if b_spec is None:
        b_spec = (pl.BlockSpec((tn, tk), lambda i, j, k: (j, k + kb0)) if tb
                  else pl.BlockSpec((tk, tn), lambda i, j, k: (k + kb0, j)))
    if o_spec is None:
        o_spec = pl.BlockSpec((tm, tn), lambda i, j, k: (i, j))
    in_specs, args = [a_spec, b_spec], [a, b]
    if has_bias:
        in_specs.append(pl.BlockSpec((1, tn), lambda i, j, k: (0, j)))
        args.append(bias)
    if has_res:
        in_specs.append(pl.BlockSpec((tm, tn), lambda i, j, k: (i, j)))
        args.append(res)
    if dep is not None:
        in_specs.append(pl.BlockSpec(memory_space=pl.ANY))
        args.append(dep)
    for sa, sb in more:
        in_specs += [sa, sb]
        args += [a, b]
    return pl.pallas_call(
        body, name=name, grid=(M // tm, N // tn, nk), in_specs=in_specs, out_specs=o_spec,
        out_shape=jax.ShapeDtypeStruct(o_shape or (M, N), out_dtype),
        scratch_shapes=[pltpu.VMEM((tm, tn), F32)] if nk > 1 else [],
        compiler_params=_cp(3),
    )(*args)


def _rms_fwd(x, gains, *, name, tr=256, dep=None):
    S, D = x.shape
    n = len(gains)
    nd = 0 if dep is None else 1

    def body(*refs):
        xv = refs[0][...]
        xh = xv * lax.rsqrt(jnp.mean(xv * xv, axis=-1, keepdims=True) + EPS)
        for q in range(n):
            refs[1 + n + nd + q][...] = (xh * refs[1 + q][...]).astype(BF16)

    row = pl.BlockSpec((tr, D), lambda i: (i, 0))
    vec = pl.BlockSpec((1, D), lambda i: (0, 0))
    return pl.pallas_call(
        body, name=name, grid=(S // tr,), in_specs=[row] + [vec] * n + [pl.BlockSpec(memory_space=pl.ANY)] * nd,
        out_specs=[row] * n, out_shape=[jax.ShapeDtypeStruct((S, D), BF16)] * n, compiler_params=_cp(1),
    )(x, *gains, *([] if dep is None else [dep]))


def _rms_bwd(x, gains, dhs, dres, *, name, tr=256, want_colsum=False):
    S, D = x.shape
    n = len(gains)
    steps = S // tr

    def body(*refs):
        x_ref = refs[0]
        g_refs = refs[1:1 + n]
        dh_refs = refs[1 + n:1 + 2 * n]
        dres_ref = refs[1 + 2 * n]
        dx_ref = refs[2 + 2 * n]
        dg_refs = refs[3 + 2 * n:3 + 3 * n]
        cs_ref = refs[3 + 3 * n] if want_colsum else None
        i = pl.program_id(0)
        xv = x_ref[...]
        r = lax.rsqrt(jnp.mean(xv * xv, axis=-1, keepdims=True) + EPS)
        xh = xv * r
        dx = dres_ref[...]
        for q in range(n):
            dh = dh_refs[q][...]
            dxh = dh * g_refs[q][...]
            dx = dx + r * (dxh - xh * jnp.mean(dxh * xh, axis=-1, keepdims=True))
            part = jnp.sum(dh * xh, axis=0, keepdims=True)

            @pl.when(i == 0)
            def _():
                dg_refs[q][...] = part

            @pl.when(i > 0)
            def _():
                dg_refs[q][...] += part

        dx_ref[...] = dx
        if want_colsum:
            cpart = jnp.sum(dx, axis=0, keepdims=True)

            @pl.when(i == 0)
            def _():
                cs_ref[...] = cpart

            @pl.when(i > 0)
            def _():
                cs_ref[...] += cpart

    row = pl.BlockSpec((tr, D), lambda i: (i, 0))
    vec = pl.BlockSpec((1, D), lambda i: (0, 0))
    n_vec_out = n + (1 if want_colsum else 0)
    outs = pl.pallas_call(
        body, name=name, grid=(steps,), in_specs=[row] + [vec] * n + [row] * n + [row],
        out_specs=[row] + [vec] * n_vec_out,
        out_shape=[jax.ShapeDtypeStruct((S, D), F32)] + [jax.ShapeDtypeStruct((1, D), F32)] * n_vec_out,
        compiler_params=_cp(1),
    )(x, *gains, *dhs, dres)
    return outs


def _colsum(x, *, name, tr=256):
    S, D = x.shape

    def body(x_ref, o_ref):
        i = pl.program_id(0)
        part = jnp.sum(x_ref[...].astype(F32), axis=0, keepdims=True)

        @pl.when(i == 0)
        def _():
            o_ref[...] = part

        @pl.when(i > 0)
        def _():
            o_ref[...] += part

    return pl.pallas_call(
        body, name=name, grid=(S // tr,), in_specs=[pl.BlockSpec((tr, D), lambda i: (i, 0))],
        out_specs=pl.BlockSpec((1, D), lambda i: (0, 0)), out_shape=jax.ShapeDtypeStruct((1, D), F32),
        compiler_params=_cp(1),
    )(x)


def _loss(y, t, *, name="loss", tr=256):
    S, D = y.shape
    steps = S // tr

    def body(y_ref, t_ref, dy_ref, l_ref, acc_ref):
        i = pl.program_id(0)
        e = y_ref[...] - t_ref[...]
        dy_ref[...] = e * (1.0 / D)
        part = jnp.sum(e * e, axis=0, keepdims=True)

        @pl.when(i == 0)
        def _():
            acc_ref[...] = part

        @pl.when(i > 0)
        def _():
            acc_ref[...] += part

        @pl.when(i == steps - 1)
        def _():
            tot = jnp.sum(acc_ref[...], axis=1, keepdims=True) * (0.5 / D)
            l_ref[...] = jnp.broadcast_to(tot, (8, 128))

    row = pl.BlockSpec((tr, D), lambda i: (i, 0))
    return pl.pallas_call(
        body, name=name, grid=(steps,), in_specs=[row, row],
        out_specs=[row, pl.BlockSpec((8, 128), lambda i: (0, 0))],
        out_shape=[jax.ShapeDtypeStruct((S, D), F32), jax.ShapeDtypeStruct((8, 128), F32)],
        scratch_shapes=[pltpu.VMEM((1, D), F32)], compiler_params=_cp(1),
    )(y, t)


STRIP = 64
HALO = 8


def _strips(S, tc):
    return [(r0, slice(l0, l0 + 128)) for l0 in range(0, tc, 128) for r0 in range(S - STRIP, -1, -STRIP)]


def _with_halo(ref, r0, ls):
    if r0 == 0:
        return jnp.concatenate([jnp.zeros((HALO, 128), F32), ref[0:STRIP, ls]], axis=0)
    return ref[r0 - HALO:r0 + STRIP, ls]


def _conv_strip(xw, w_ref, b_ref, ls, width):
    acc = b_ref[:, ls] + w_ref[pl.ds(width - 1, 1), ls] * xw[HALO:]
    shifted = []
    for s in range(1, width):
        xs = pltpu.roll(xw, s, axis=0)[HALO:]
        shifted.append(xs)
        acc = acc + w_ref[pl.ds(width - 1 - s, 1), ls] * xs
    return acc, shifted


def _conv_strip_back(dacc, after, xc, shifted, w_ref, ls, width):
    ext = jnp.concatenate([dacc, after], axis=0)
    dx = w_ref[pl.ds(width - 1, 1), ls] * dacc
    dws = [None] * width
    dws[width - 1] = jnp.sum(dacc * xc, axis=0, keepdims=True)
    for s in range(1, width):
        dx = dx + w_ref[pl.ds(width - 1 - s, 1), ls] * pltpu.roll(ext, STRIP + HALO - s, axis=0)[:STRIP]
        dws[width - 1 - s] = jnp.sum(dacc * shifted[s - 1], axis=0, keepdims=True)
    return dx, dws, jnp.sum(dacc, axis=0, keepdims=True)


def _conv_back_block(S, tc, width, w_ref, b_ref, x_ref, dacc_of, dx_store, dw_ref, db_ref):
    for l0 in range(0, tc, 128):
        ls = slice(l0, l0 + 128)
        after = jnp.zeros((HALO, 128), F32)
        tot = None
        for r0 in range(S - STRIP, -1, -STRIP):
            xw = _with_halo(x_ref, r0, ls)
            acc, shifted = _conv_strip(xw, w_ref, b_ref, ls, width)
            dacc = dacc_of(r0, ls, acc, _sigmoid(acc))
            dx, dws, db = _conv_strip_back(dacc, after, xw[HALO:], shifted, w_ref, ls, width)
            dx_store(r0, ls, dx)
            after = dacc[:HALO]
            part = dws + [db]
            tot = part if tot is None else [p + q for p, q in zip(tot, part)]
        for k in range(width):
            dw_ref[pl.ds(k, 1), ls] = tot[k]
        db_ref[:, ls] = tot[width]


def _conv_silu_fwd(xin, col0, C, w, b, *, name, tc=512):
    S = xin.shape[0]
    width = w.shape[0]
    off = col0 // tc

    def body(x_ref, w_ref, b_ref, o_ref):
        for r0, ls in _strips(S, tc):
            acc, _ = _conv_strip(_with_halo(x_ref, r0, ls), w_ref, b_ref, ls, width)
            o_ref[r0:r0 + STRIP, ls] = acc * _sigmoid(acc)

    return pl.pallas_call(
        body, name=name, grid=(C // tc,),
        in_specs=[pl.BlockSpec((S, tc), lambda j: (0, j + off)), pl.BlockSpec((width, tc), lambda j: (0, j)),
                  pl.BlockSpec((1, tc), lambda j: (0, j))],
        out_specs=pl.BlockSpec((S, tc), lambda j: (0, j)), out_shape=jax.ShapeDtypeStruct((S, C), F32),
        compiler_params=_cp(1),
    )(xin, w, b)


def _conv_silu_bwd(xin, col0, C, w, b, douts, *, name, tc=256):
    S = xin.shape[0]
    width = w.shape[0]
    off = col0 // tc
    nd = len(douts)
    ranges = [(o // tc, (o + d.shape[1]) // tc) for d, o in douts]

    def body(*refs):
        x_ref, w_ref, b_ref = refs[0], refs[1], refs[2]
        d_refs = refs[3:3 + nd]
        dx_ref, dw_ref, db_ref = refs[3 + nd], refs[4 + nd], refs[5 + nd]
        j = pl.program_id(0)

        def dacc_of(r0, ls, acc, sg):
            dout = jnp.zeros((STRIP, 128), F32)
            for q in range(nd):
                lo, hi = ranges[q]
                dout = dout + jnp.where((j >= lo) & (j < hi), d_refs[q][r0:r0 + STRIP, ls], 0.0)
            return dout * (sg * (1.0 + acc * (1.0 - sg)))

        def dx_store(r0, ls, dx):
            dx_ref[r0:r0 + STRIP, ls] = dx.astype(BF16)

        _conv_back_block(S, tc, width, w_ref, b_ref, x_ref, dacc_of, dx_store, dw_ref, db_ref)

    d_specs = [pl.BlockSpec((S, tc), (lambda j, lo=lo, hi=hi: (0, jnp.clip(j - lo, 0, hi - lo - 1)))) for lo, hi in ranges]
    return pl.pallas_call(
        body, name=name, grid=(C // tc,),
        in_specs=[pl.BlockSpec((S, tc), lambda j: (0, j + off)), pl.BlockSpec((width, tc), lambda j: (0, j)),
                  pl.BlockSpec((1, tc), lambda j: (0, j))] + d_specs,
        out_specs=[pl.BlockSpec((S, tc), lambda j: (0, j)), pl.BlockSpec((width, tc), lambda j: (0, j)),
                   pl.BlockSpec((1, tc), lambda j: (0, j))],
        out_shape=[jax.ShapeDtypeStruct((S, C), BF16), jax.ShapeDtypeStruct((width, C), F32),
                   jax.ShapeDtypeStruct((1, C), F32)],
        compiler_params=_cp(1),
    )(xin, w, b, *[d for d, _ in douts])


def _ffn_act_fwd(u, w, b, *, name, tc=256):
    S, F2 = u.shape
    Fd = F2 // 2
    width = w.shape[0]
    nb = Fd // tc

    def body(g_ref, v_ref, w_ref, b_ref, o_ref):
        for r0, ls in _strips(S, tc):
            acc, _ = _conv_strip(_with_halo(g_ref, r0, ls), w_ref, b_ref, ls, width)
            o_ref[r0:r0 + STRIP, ls] = (acc * _sigmoid(acc) * v_ref[r0:r0 + STRIP, ls]).astype(BF16)

    return pl.pallas_call(
        body, name=name, grid=(nb,),
        in_specs=[pl.BlockSpec((S, tc), lambda j: (0, j)), pl.BlockSpec((S, tc), lambda j: (0, j + nb)),
                  pl.BlockSpec((width, tc), lambda j: (0, j)), pl.BlockSpec((1, tc), lambda j: (0, j))],
        out_specs=pl.BlockSpec((S, tc), lambda j: (0, j)), out_shape=jax.ShapeDtypeStruct((S, Fd), BF16),
        compiler_params=_cp(1),
    )(u, u, w, b)


def _ffn_act_bwd(u, w, b, da, *, name, tc=256):
    S, F2 = u.shape
    Fd = F2 // 2
    width = w.shape[0]
    nb = Fd // tc

    def body(g_ref, v_ref, w_ref, b_ref, da_ref, du_ref, dw_ref, db_ref, a_ref):
        def dacc_of(r0, ls, acc, sg):
            rs = slice(r0, r0 + STRIP)
            dav, val, silu = da_ref[rs, ls], v_ref[rs, ls], acc * sg
            a_ref[rs, ls] = (silu * val).astype(BF16)
            du_ref[1, rs, ls] = (dav * silu).astype(BF16)
            return dav * val * (sg * (1.0 + acc * (1.0 - sg)))

        def dx_store(r0, ls, dx):
            du_ref[0, r0:r0 + STRIP, ls] = dx.astype(BF16)

        _conv_back_block(S, tc, width, w_ref, b_ref, g_ref, dacc_of, dx_store, dw_ref, db_ref)

    blk = pl.BlockSpec((S, tc), lambda j: (0, j))
    return pl.pallas_call(
        body, name=name, grid=(nb,),
        in_specs=[blk, pl.BlockSpec((S, tc), lambda j: (0, j + nb)), pl.BlockSpec((width, tc), lambda j: (0, j)),
                  pl.BlockSpec((1, tc), lambda j: (0, j)), blk],
        out_specs=[pl.BlockSpec((2, S, tc), lambda j: (0, 0, j)), pl.BlockSpec((width, tc), lambda j: (0, j)),
                   pl.BlockSpec((1, tc), lambda j: (0, j)), blk],
        out_shape=[jax.ShapeDtypeStruct((2, S, Fd), BF16),
                   jax.ShapeDtypeStruct((width, Fd), F32), jax.ShapeDtypeStruct((1, Fd), F32),
                   jax.ShapeDtypeStruct((S, Fd), BF16)],
        compiler_params=_cp(1),
    )(u, u, w, b, da)


def _ssd_prep(dtr, dt_bias, a_log, *, name="ssd_prep"):
    S = dtr.shape[0]

    def body(d_ref, b_ref, al_ref, dt_ref, ac_ref, sg_ref, act_ref):
        lane = _iota((CHUNK, 128), 1)
        valid = lane < SSM_HEADS
        z = d_ref[...] + b_ref[...]
        dt = jnp.where(valid, jnp.maximum(z, 0.0) + jnp.log(1.0 + jnp.exp(-jnp.abs(z))), 0.0)
        a = dt * (-jnp.exp(al_ref[...]))
        row = _iota((CHUNK, 128), 0)
        k = 1
        while k < CHUNK:
            a = a + jnp.where(row >= k, pltpu.roll(a, k, axis=0), 0.0)
            k *= 2
        sg = jnp.where(valid, _sigmoid(z), 0.0)
        for arr, ref in ((dt, dt_ref), (a, ac_ref), (sg, sg_ref)):
            for g in range(SSM_GROUPS):
                ref[g] = jnp.where(lane < 4, arr if g == 0 else pltpu.roll(arr, 128 - 4 * g, axis=1), 0.0)
        act_ref[...] = a.T[:SSM_HEADS, :]

    blk = pl.BlockSpec((CHUNK, 128), lambda i: (i, 0))
    vec = pl.BlockSpec((1, 128), lambda i: (0, 0))
    grp = pl.BlockSpec((SSM_GROUPS, CHUNK, 128), lambda i: (0, i, 0))
    return pl.pallas_call(
        body, name=name, grid=(S // CHUNK,), in_specs=[blk, vec, vec],
        out_specs=[grp, grp, grp, pl.BlockSpec((SSM_HEADS, CHUNK), lambda i: (0, i))],
        out_shape=[jax.ShapeDtypeStruct((SSM_GROUPS, S, 128), F32)] * 3 + [jax.ShapeDtypeStruct((SSM_HEADS, S), F32)],
        compiler_params=_cp(1),
    )(dtr, dt_bias, a_log)


def _expand4(v, lanes):
    out = jnp.broadcast_to(v[:, 3:4], lanes.shape)
    for hh in (2, 1, 0):
        out = jnp.where(lanes < 64 * (hh + 1), v[:, hh:hh + 1], out)
    return out


def _ssd_fwd(xbc, dt_g, ac_g, ac_t, *, name="ssd_fwd"):
    S = xbc.shape[0]
    nc = S // CHUNK
    Lc = CHUNK

    def body(x_ref, b_ref, c_ref, dt_ref, ac_ref, act_ref, y_ref, st_out_ref, st_ref):
        g = pl.program_id(0)
        c = pl.program_id(1)

        @pl.when(c == 0)
        def _():
            st_ref[...] = jnp.zeros_like(st_ref)

        bv = b_ref[...]
        cbf = c_ref[...].astype(BF16)
        cb = _dot(cbf, bv.astype(BF16), 1, 1)
        causal = _iota((Lc, Lc), 0) >= _iota((Lc, Lc), 1)
        lane256 = _iota((Lc, 256), 1)
        lane128 = _iota((Lc, 128), 1)
        row128 = _iota((128, 128), 0)
        dtg, acg = dt_ref[...], ac_ref[...]
        ac_last = ac_ref[pl.ds(Lc - 1, 1), :]
        dt4 = _expand4(dtg, lane256)
        ac4 = _expand4(acg, lane256)
        e4 = jnp.exp(ac4)
        xdb = (x_ref[...] * dt4).astype(BF16)
        st_out_ref[...] = st_ref[...]
        for p in range(2):
            xd_p = xdb[:, 128 * p:128 * (p + 1)]
            st_p = st_ref[p]
            ys, sn, cds = [], [], []
            for q in range(2):
                hh = 2 * p + q
                a_col = acg[:, hh:hh + 1]
                a_row = act_ref[pl.ds(4 * g + hh, 1), :]
                dec = jnp.exp(jnp.where(causal, a_col - a_row, NEG))
                w = (cb * dec).astype(BF16)
                ys.append(_dot(w, xd_p))
                al = ac_last[:, hh:hh + 1]
                dte = jnp.exp(al - a_col)
                sn.append(_dot(xd_p, (bv * dte).astype(BF16), 0, 0))
                cds.append(jnp.exp(al))
            y_diag = jnp.where(lane128 < 64, ys[0], ys[1])
            y_off = _dot(cbf, st_p.astype(BF16), 1, 1) * e4[:, 128 * p:128 * (p + 1)]
            y_ref[:, 128 * p:128 * (p + 1)] = y_diag + y_off
            st_ref[p] = jnp.where(row128 < 64, st_p * cds[0] + sn[0], st_p * cds[1] + sn[1])

    per_g = lambda g, c: (g, c, 0)
    return pl.pallas_call(
        body, name=name, grid=(SSM_GROUPS, nc),
        in_specs=[pl.BlockSpec((Lc, 256), lambda g, c: (c, g)),
                  pl.BlockSpec((Lc, 128), lambda g, c: (c, 16 + g)),
                  pl.BlockSpec((Lc, 128), lambda g, c: (c, 24 + g)),
                  pl.BlockSpec((None, Lc, 128), per_g), pl.BlockSpec((None, Lc, 128), per_g),
                  pl.BlockSpec((SSM_HEADS, Lc), lambda g, c: (0, c))],
        out_specs=[pl.BlockSpec((Lc, 256), lambda g, c: (c, g)),
                   pl.BlockSpec((None, None, 2, 128, 128), lambda g, c: (g, c, 0, 0, 0))],
        out_shape=[jax.ShapeDtypeStruct((S, 2048), F32), jax.ShapeDtypeStruct((SSM_GROUPS, nc, 2, 128, 128), F32)],
        scratch_shapes=[pltpu.VMEM((2, 128, 128), F32)], compiler_params=_cp(2),
    )(xbc, xbc, xbc, dt_g, ac_g, ac_t)


def _ssd_bwd(xbc, dt_g, ac_g, ac_t, states, dy, dexp, *, name="ssd_bwd"):
    S = xbc.shape[0]
    nc = S // CHUNK
    Lc = CHUNK

    def body(x_ref, b_ref, c_ref, dt_ref, ac_ref, act_ref, st_ref, dy_ref, d_ref, dx_ref, db_ref, dc_ref, dh_ref, ds_ref):
        g = pl.program_id(0)
        cc = pl.program_id(1)

        @pl.when(cc == 0)
        def _():
            ds_ref[...] = jnp.zeros_like(ds_ref)

        bv = b_ref[...]
        cv = c_ref[...]
        bbf, cbf = bv.astype(BF16), cv.astype(BF16)
        cb = _dot(cbf, bbf, 1, 1)
        causal = _iota((Lc, Lc), 0) >= _iota((Lc, Lc), 1)
        lane256 = _iota((Lc, 256), 1)
        lane128 = _iota((Lc, 128), 1)
        row128 = _iota((128, 128), 0)
        dtg, acg = dt_ref[...], ac_ref[...]
        ac_last = ac_ref[pl.ds(Lc - 1, 1), :]
        dt4 = _expand4(dtg, lane256)
        ac4 = _expand4(acg, lane256)
        acl4 = _expand4(ac_last, _iota((1, 256), 1))
        e4 = jnp.exp(ac4)
        dte4 = jnp.exp(acl4 - ac4)
        xv = x_ref[...]
        xd = xv * dt4
        xdb = xd.astype(BF16)
        dyv = dy_ref[...]
        dcb = jnp.zeros((Lc, Lc), F32)
        dc_acc = jnp.zeros((Lc, 128), F32)
        db_acc = jnp.zeros((Lc, 128), F32)
        ind_rows = _iota((256, 128), 0) >> 6
        ind_cols = _iota((256, 128), 1)
        ind_a = (ind_rows == ind_cols).astype(BF16)
        ind_b = (ind_rows + 4 == ind_cols).astype(BF16)
        u_parts, dxd_parts, ends = [], [], []
        for p in range(2):
            sl = slice(128 * p, 128 * (p + 1))
            xd_p, xdb_p, dy_p = xd[:, sl], xdb[:, sl], dyv[:, sl]
            dyb_p = dy_p.astype(BF16)
            e_p, dte_p = e4[:, sl], dte4[:, sl]
            sp = st_ref[p]
            spb = sp.astype(BF16)
            dsn = ds_ref[p]
            dsnb = dsn.astype(BF16)
            yds, dxds, cds = [], [], []
            for q in range(2):
                hh = 2 * p + q
                a_col = acg[:, hh:hh + 1]
                a_row = act_ref[pl.ds(4 * g + hh, 1), :]
                dec = jnp.exp(jnp.where(causal, a_col - a_row, NEG))
                w = (cb * dec).astype(BF16)
                head = (lane128 < 64) if q == 0 else (lane128 >= 64)
                dym = jnp.where(head, dyb_p, jnp.zeros_like(dyb_p))
                dw = _dot(dym, xdb_p, 1, 1)
                dcb = dcb + dw * dec
                yds.append(_dot(w, xdb_p))
                dxds.append(_dot(w, dyb_p, 0, 0))
                cds.append(jnp.exp(ac_last[:, hh:hh + 1]))
            y_diag = jnp.where(lane128 < 64, yds[0], yds[1])
            dxd_diag = jnp.where(lane128 < 64, dxds[0], dxds[1])
            y_off = _dot(cbf, spb, 1, 1) * e_p
            dgp = dy_p * e_p
            dgb = dgp.astype(BF16)
            dc_acc = dc_acc + _dot(dgb, spb)
            dsp = _dot(dgb, cbf, 0, 0)
            cd_col = jnp.where(row128[:, 0:1] < 64, cds[0], cds[1])
            qm = _dot(bbf, dsnb, 1, 1)
            dxd_state = dte_p * qm
            db_acc = db_acc + _dot((xd_p * dte_p).astype(BF16), dsnb)
            t_p = xd_p * dxd_state
            prod = dsn * sp
            e0 = jnp.sum(jnp.sum(jnp.where(row128 < 64, prod, 0.0), axis=1, keepdims=True), axis=0, keepdims=True)
            e1 = jnp.sum(jnp.sum(jnp.where(row128 >= 64, prod, 0.0), axis=1, keepdims=True), axis=0, keepdims=True)
            tcol = jnp.sum(t_p, axis=0, keepdims=True)
            lane1 = _iota((1, 128), 1)
            t0 = jnp.sum(jnp.where(lane1 < 64, tcol, 0.0), axis=1, keepdims=True)
            t1 = jnp.sum(jnp.where(lane1 >= 64, tcol, 0.0), axis=1, keepdims=True)
            ends.append(e0 * cds[0] + t0)
            ends.append(e1 * cds[1] + t1)
            ds_ref[p] = dsn * cd_col + dsp
            u_parts.append(dyb_p.astype(F32) * y_diag - xdb_p.astype(F32) * dxd_diag + dy_p * y_off - t_p)
            dxd_parts.append(dxd_diag + dxd_state)
        dxd = jnp.concatenate(dxd_parts, axis=1)
        u_all = jnp.concatenate(u_parts, axis=1)
        dx_ref[...] = dxd * dt4 + dyv * d_ref[...]
        dcbb = dcb.astype(BF16)
        dc_ref[...] = dc_acc + _dot(dcbb, bbf)
        db_ref[...] = db_acc + _dot(dcbb, cbf, 0, 0)
        lane = _iota((Lc, 128), 1)
        endv = jnp.zeros((Lc, 128), F32)
        for hh in range(4):
            endv = jnp.where(lane == 8 + hh, ends[hh], endv)
        dh_ref[...] = _dot3(dxd * xv, ind_a) + _dot3(u_all, ind_b) + endv

    rev = lambda c: nc - 1 - c
    per_g = lambda g, c: (g, rev(c), 0)
    return pl.pallas_call(
        body, name=name, grid=(SSM_GROUPS, nc),
        in_specs=[pl.BlockSpec((Lc, 256), lambda g, c: (rev(c), g)),
                  pl.BlockSpec((Lc, 128), lambda g, c: (rev(c), 16 + g)),
                  pl.BlockSpec((Lc, 128), lambda g, c: (rev(c), 24 + g)),
                  pl.BlockSpec((None, Lc, 128), per_g), pl.BlockSpec((None, Lc, 128), per_g),
                  pl.BlockSpec((SSM_HEADS, Lc), lambda g, c: (0, rev(c))),
                  pl.BlockSpec((None, None, 2, 128, 128), lambda g, c: (g, rev(c), 0, 0, 0)),
                  pl.BlockSpec((Lc, 256), lambda g, c: (rev(c), g)),
                  pl.BlockSpec((1, 256), lambda g, c: (0, g))],
        out_specs=[pl.BlockSpec((Lc, 256), lambda g, c: (rev(c), g)),
                   pl.BlockSpec((Lc, 128), lambda g, c: (rev(c), g)),
                   pl.BlockSpec((Lc, 128), lambda g, c: (rev(c), g)),
                   pl.BlockSpec((None, Lc, 128), per_g)],
        out_shape=[jax.ShapeDtypeStruct((S, 2048), F32), jax.ShapeDtypeStruct((S, 1024), F32),
                   jax.ShapeDtypeStruct((S, 1024), F32), jax.ShapeDtypeStruct((SSM_GROUPS, S, 128), F32)],
        scratch_shapes=[pltpu.VMEM((2, 128, 128), F32)], compiler_params=_cp(2),
    )(xbc, xbc, xbc, dt_g, ac_g, ac_t, states, dy, dexp)


def _ssd_post(dhead, dt_g, sg_g, alog_g, *, name="ssd_post"):
    S = dhead.shape[1]
    nc = S // CHUNK
    Lc = CHUNK

    def body(dh_ref, dt_ref, sg_ref, al_ref, o_ref, s_ref):
        @pl.when(pl.program_id(0) == 0)
        def _():
            s_ref[...] = jnp.zeros_like(s_ref)

        lane = _iota((Lc, 128), 1)
        row = _iota((Lc, 128), 0)
        row8 = _iota((8, 128), 0)
        out = jnp.zeros((Lc, 128), F32)
        for g in range(SSM_GROUPS):
            dh = dh_ref[g]
            a_neg = -jnp.exp(al_ref[g])
            dac = jnp.where(lane < 4, pltpu.roll(dh, 124, axis=1), 0.0)
            end = jnp.where(lane < 4, pltpu.roll(dh, 120, axis=1), 0.0)
            k = 1
            while k < Lc:
                dac = dac + jnp.where(row < Lc - k, pltpu.roll(dac, Lc - k, axis=0), 0.0)
                k *= 2
            da = dac + end
            ddt = jnp.where(lane < 4, da * a_neg + dh, 0.0)
            ddtr = ddt * sg_ref[g]
            out = out + (ddtr if g == 0 else pltpu.roll(ddtr, 4 * g, axis=1))
            dal = jnp.sum(da * dt_ref[g], axis=0, keepdims=True) * a_neg
            dbias = jnp.sum(ddtr, axis=0, keepdims=True)
            part = jnp.where(row8 == 0, dal, jnp.where(row8 == 1, dbias, 0.0))
            s_ref[g] += part
        o_ref[...] = out.astype(BF16)

    grp = pl.BlockSpec((SSM_GROUPS, Lc, 128), lambda c: (0, c, 0))
    whole = lambda r: pl.BlockSpec((SSM_GROUPS, r, 128), lambda c: (0, 0, 0))
    return pl.pallas_call(
        body, name=name, grid=(nc,), in_specs=[grp, grp, grp, whole(1)],
        out_specs=[pl.BlockSpec((Lc, 128), lambda c: (c, 0)), whole(8)],
        out_shape=[jax.ShapeDtypeStruct((S, 128), BF16), jax.ShapeDtypeStruct((SSM_GROUPS, 8, 128), F32)],
        compiler_params=_cp(1),
    )(dhead, dt_g, sg_g, alog_g)


def _gate_fwd(y, xbc, zx, dexp, gn, *, name="gate_fwd", tr=256):
    S = y.shape[0]
    W = 2048
    gw = W // SSM_GROUPS

    def body(y_ref, x_ref, z_ref, d_ref, g_ref, o_ref):
        z = z_ref[...]
        u = (y_ref[...] + x_ref[...] * d_ref[...]) * (z * _sigmoid(z))
        gv = g_ref[...]
        for q in range(SSM_GROUPS):
            sl = slice(gw * q, gw * (q + 1))
            uq = u[:, sl]
            r = lax.rsqrt(jnp.mean(uq * uq, axis=-1, keepdims=True) + EPS)
            o_ref[:, sl] = (uq * r * gv[:, sl]).astype(BF16)

    row = pl.BlockSpec((tr, W), lambda i: (i, 0))
    vec = pl.BlockSpec((1, W), lambda i: (0, 0))
    return pl.pallas_call(
        body, name=name, grid=(S // tr,), in_specs=[row, row, row, vec, vec], out_specs=row,
        out_shape=jax.ShapeDtypeStruct((S, W), BF16), compiler_params=_cp(1),
    )(y, xbc, zx, dexp, gn)


def _gate_bwd(y, xbc, zx, dexp, gn, dout, *, name="gate_bwd", tr=256):
    S = y.shape[0]
    W = 2048
    gw = W // SSM_GROUPS
    steps = S // tr

    def body(y_ref, x_ref, z_ref, d_ref, g_ref, do_ref, dy_ref, dz_ref, dg_ref, dd_ref, acc_ref):
        i = pl.program_id(0)

        @pl.when(i == 0)
        def _():
            acc_ref[...] = jnp.zeros_like(acc_ref)

        z = z_ref[...]
        sg = _sigmoid(z)
        sz = z * sg
        xs = x_ref[...]
        yt = y_ref[...] + xs * d_ref[...]
        u = yt * sz
        gv = g_ref[...]
        do = do_ref[...]
        dgs = []
        for q in range(SSM_GROUPS):
            sl = slice(gw * q, gw * (q + 1))
            uq = u[:, sl]
            r = lax.rsqrt(jnp.mean(uq * uq, axis=-1, keepdims=True) + EPS)
            uh = uq * r
            dq = do[:, sl]
            duh = dq * gv[:, sl]
            duq = r * (duh - uh * jnp.mean(duh * uh, axis=-1, keepdims=True))
            dgs.append(jnp.sum(dq * uh, axis=0, keepdims=True))
            dyt = duq * sz[:, sl]
            dy_ref[:, sl] = dyt
            dz_ref[:, sl] = (duq * yt[:, sl] * (sg[:, sl] * (1.0 + z[:, sl] * (1.0 - sg[:, sl])))).astype(BF16)
            acc_ref[:, sl] += jnp.sum(dyt * xs[:, sl], axis=0, keepdims=True)
        dg = jnp.concatenate(dgs, axis=1)

        @pl.when(i == 0)
        def _():
            dg_ref[...] = dg

        @pl.when(i > 0)
        def _():
            dg_ref[...] += dg

        @pl.when(i == steps - 1)
        def _():
            ind = ((_iota((W, 128), 0) >> 6) == _iota((W, 128), 1)).astype(BF16)
            dd_ref[...] = _dot3(jnp.broadcast_to(acc_ref[...], (8, W)), ind)[0:1, :]

    row = pl.BlockSpec((tr, W), lambda i: (i, 0))
    vec = pl.BlockSpec((1, W), lambda i: (0, 0))
    return pl.pallas_call(
        body, name=name, grid=(steps,), in_specs=[row, row, row, vec, vec, row],
        out_specs=[row, row, vec, pl.BlockSpec((1, 128), lambda i: (0, 0))],
        out_shape=[jax.ShapeDtypeStruct((S, W), F32), jax.ShapeDtypeStruct((S, W), BF16),
                   jax.ShapeDtypeStruct((1, W), F32), jax.ShapeDtypeStruct((1, 128), F32)],
        scratch_shapes=[pltpu.VMEM((1, W), F32)], compiler_params=_cp(1),
    )(y, xbc, zx, dexp, gn, dout)


def _rope_cs(posf, *, name="rope_tables", tr=256):
    S = posf.shape[0]

    def body(p_ref, c_ref, s_ref):
        j = (_iota((tr, 128), 1) & 31).astype(F32)
        ang = p_ref[...] * jnp.exp(j * (-math.log(ROPE_THETA) / 32.0))
        c_ref[...] = jnp.cos(ang)
        s_ref[...] = jnp.sin(ang)

    blk = pl.BlockSpec((tr, 128), lambda i: (i, 0))
    return pl.pallas_call(
        body, name=name, grid=(S // tr,), in_specs=[pl.BlockSpec((tr, 1), lambda i: (i, 0))], out_specs=[blk, blk],
        out_shape=[jax.ShapeDtypeStruct((S, 128), F32)] * 2, compiler_params=_cp(1),
    )(posf)


def _rope_tables(c_ref, s_ref, shape):
    reps = shape[1] // 128
    return jnp.tile(c_ref[...], (1, reps)), jnp.tile(s_ref[...], (1, reps)), (_iota(shape, 1) & 63) < 32


def _hn_inds(W):
    ind = ((_iota((W, 128), 0) >> 6) == _iota((W, 128), 1)).astype(BF16)
    ind_t = ((_iota((128, W), 1) >> 6) == _iota((128, W), 0)).astype(BF16)
    return ind, ind_t


def _hnrope_fwd(xin, col0, W, gain_w, rope, *, name, tr=256):
    S = xin.shape[0]
    off = col0 // W
    nh = W // HEAD

    def body(x_ref, g_ref, c_ref, s_ref, o_ref):
        x = x_ref[...]
        ind, ind_t = _hn_inds(W)
        r = lax.rsqrt(_dot3(x * x, ind) * (1.0 / HEAD) + EPS)
        xn = x * _dot3(r, ind_t) * g_ref[...]
        cs, sn, half = _rope_tables(c_ref, s_ref, (tr, W))
        rot = jnp.where(half, -pltpu.roll(xn, W - 32, axis=1), pltpu.roll(xn, 32, axis=1))
        out = (xn * cs + rot * sn).astype(BF16)
        for h in range(nh):
            o_ref[h] = out[:, HEAD * h:HEAD * (h + 1)]

    tab = pl.BlockSpec((tr, 128), lambda i: (i, 0))
    return pl.pallas_call(
        body, name=name, grid=(S // tr,),
        in_specs=[pl.BlockSpec((tr, W), lambda i: (i, off)), pl.BlockSpec((1, W), lambda i: (0, 0)), tab, tab],
        out_specs=pl.BlockSpec((nh, tr, HEAD), lambda i: (0, i, 0)), out_shape=jax.ShapeDtypeStruct((nh, S, HEAD), BF16),
        compiler_params=_cp(1),
    )(xin, gain_w, *rope)


def _hnrope_bwd(xin, col0, W, gain_w, rope, dout, *, name, tr=256):
    S = xin.shape[0]
    off = col0 // W
    steps = S // tr
    nh = W // HEAD

    def body(x_ref, g_ref, c_ref, s_ref, do_ref, dx_ref, cs_ref, dg_ref, acc_ref):
        i = pl.program_id(0)
        x = x_ref[...]
        ind, ind_t = _hn_inds(W)
        r = lax.rsqrt(_dot3(x * x, ind) * (1.0 / HEAD) + EPS)
        rw = _dot3(r, ind_t)
        xh = x * rw
        cs, sn, half = _rope_tables(c_ref, s_ref, (tr, W))
        do = jnp.concatenate([do_ref[h] for h in range(nh)], axis=1).astype(F32)
        gs = do * sn
        g1 = do * cs + jnp.where(half, pltpu.roll(gs, W - 32, axis=1), -pltpu.roll(gs, 32, axis=1))
        dxh = g1 * g_ref[...]
        t = _dot3(dxh * xh, ind) * (1.0 / HEAD)
        dx = rw * (dxh - xh * _dot3(t, ind_t))
        dx_ref[...] = dx.astype(BF16)
        cpart = jnp.sum(dx, axis=0, keepdims=True)
        gpart = jnp.sum(g1 * xh, axis=0, keepdims=True)

        @pl.when(i == 0)
        def _():
            cs_ref[...] = cpart
            acc_ref[...] = gpart

        @pl.when(i > 0)
        def _():
            cs_ref[...] += cpart
            acc_ref[...] += gpart

        @pl.when(i == steps - 1)
        def _():
            fold = ((_iota((W, 128), 0) & 63) == _iota((W, 128), 1)).astype(BF16)
            dg_ref[...] = _dot3(jnp.broadcast_to(acc_ref[...], (8, W)), fold)[0:1, :]

    tab = pl.BlockSpec((tr, 128), lambda i: (i, 0))
    return pl.pallas_call(
        body, name=name, grid=(steps,),
        in_specs=[pl.BlockSpec((tr, W), lambda i: (i, off)), pl.BlockSpec((1, W), lambda i: (0, 0)), tab, tab,
                  pl.BlockSpec((nh, tr, HEAD), lambda i: (0, i, 0))],
        out_specs=[pl.BlockSpec((tr, W), lambda i: (i, 0)), pl.BlockSpec((1, W), lambda i: (0, 0)),
                   pl.BlockSpec((1, 128), lambda i: (0, 0))],
        out_shape=[jax.ShapeDtypeStruct((S, W), BF16), jax.ShapeDtypeStruct((1, W), F32),
                   jax.ShapeDtypeStruct((1, 128), F32)],
        scratch_shapes=[pltpu.VMEM((1, W), F32)], compiler_params=_cp(1),
    )(xin, gain_w, *rope, dout)


def _attn_probs(q, kb, sink_ref, h, i):
    s = _dot(q, kb, 1, 1) * (HEAD ** -0.5)
    r = _iota((4 * WINDOW, 2 * WINDOW), 0)
    ki = _iota((4 * WINDOW, 2 * WINDOW), 1)
    rel = (r & (WINDOW - 1)) + WINDOW - ki
    mask = (rel >= 0) & (rel < WINDOW) & ((ki >= WINDOW) | (i > 0))
    s = jnp.where(mask, s, NEG)
    r1 = _iota((4 * WINDOW, 1), 0)
    sink = jnp.where(r1 < WINDOW, sink_ref[4 * h], jnp.where(r1 < 2 * WINDOW, sink_ref[4 * h + 1],
                     jnp.where(r1 < 3 * WINDOW, sink_ref[4 * h + 2], sink_ref[4 * h + 3])))
    m = jnp.maximum(jnp.max(s, axis=1, keepdims=True), sink)
    p = jnp.exp(s - m)
    ps = jnp.exp(sink - m)
    inv = 1.0 / (jnp.sum(p, axis=1, keepdims=True) + ps)
    return p * inv, ps * inv


ATT_HPS = 2


def _attn_specs(S):
    qspec = pl.BlockSpec((ATT_HPS, ATT_G, WINDOW, HEAD), lambda h, i: (h, 0, i, 0))
    cur = pl.BlockSpec((ATT_HPS, WINDOW, HEAD), lambda h, i: (h, i, 0))
    prev = pl.BlockSpec((ATT_HPS, WINDOW, HEAD), lambda h, i: (h, jnp.maximum(i - 1, 0), 0))
    tok = pl.BlockSpec((WINDOW, ATT_HPS * ATT_G * HEAD), lambda h, i: (i, h))
    return qspec, cur, prev, tok


def _attn_fwd(qh, kh, vh, sinks, *, name="attn_fwd"):
    S = kh.shape[1]
    nb = S // WINDOW

    def body(s_ref, q_ref, kc_ref, kp_ref, vc_ref, vp_ref, o_ref):
        h2, i = pl.program_id(0), pl.program_id(1)
        outs = []
        for hh in range(ATT_HPS):
            q = q_ref[hh].reshape(ATT_G * WINDOW, HEAD)
            kb = jnp.concatenate([kp_ref[hh], kc_ref[hh]], axis=0)
            vb = jnp.concatenate([vp_ref[hh], vc_ref[hh]], axis=0)
            probs, _ = _attn_probs(q, kb, s_ref, ATT_HPS * h2 + hh, i)
            o = _dot(probs.astype(BF16), vb).astype(BF16)
            outs += [o[WINDOW * g:WINDOW * (g + 1)] for g in range(ATT_G)]
        o_ref[...] = jnp.concatenate(outs, axis=1)

    qspec, cur, prev, tok = _attn_specs(S)
    return pl.pallas_call(
        body, name=name, grid=(ATT_KV // ATT_HPS, nb),
        in_specs=[pl.BlockSpec(memory_space=pltpu.SMEM), qspec, cur, prev, cur, prev], out_specs=tok,
        out_shape=jax.ShapeDtypeStruct((S, ATT_KV * ATT_G * HEAD), BF16), compiler_params=_cp(2),
    )(sinks, qh, kh, kh, vh, vh)


def _attn_bwd(qh, kh, vh, sinks, doh, *, name="attn_bwd"):
    S = kh.shape[1]
    nb = S // WINDOW

    def body(s_ref, q_ref, kc_ref, kp_ref, vc_ref, vp_ref, do_ref, dq_ref, dk_ref, dv_ref, dsk_ref):
        h2, i = pl.program_id(0), pl.program_id(1)

        @pl.when(i == 0)
        def _():
            dk_ref[...] = jnp.zeros_like(dk_ref)
            dv_ref[...] = jnp.zeros_like(dv_ref)
            dsk_ref[...] = jnp.zeros_like(dsk_ref)

        dov = do_ref[...]
        cur = pl.multiple_of(i * WINDOW, WINDOW)
        lane = _iota((8, 128), 1)
        row = _iota((8, 128), 0)
        scale = HEAD ** -0.5
        for hh in range(ATT_HPS):
            q = q_ref[hh].reshape(ATT_G * WINDOW, HEAD)
            do = jnp.concatenate([dov[:, HEAD * (ATT_G * hh + g):HEAD * (ATT_G * hh + g + 1)] for g in range(ATT_G)], axis=0)
            kb = jnp.concatenate([kp_ref[hh], kc_ref[hh]], axis=0)
            vb = jnp.concatenate([vp_ref[hh], vc_ref[hh]], axis=0)
            probs, psink = _attn_probs(q, kb, s_ref, ATT_HPS * h2 + hh, i)
            dp = _dot(do, vb, 1, 1)
            delta = jnp.sum(probs * dp, axis=1, keepdims=True)
            ds = (probs * (dp - delta)).astype(BF16)
            dq_ref[hh] = (_dot(ds, kb) * scale).reshape(ATT_G, WINDOW, HEAD)
            dkb = _dot(ds, q, 0, 0) * scale
            dvb = _dot(probs.astype(BF16), do, 0, 0)
            dk_ref[hh, pl.ds(cur, WINDOW), :] += dkb[WINDOW:, :]
            dv_ref[hh, pl.ds(cur, WINDOW), :] += dvb[WINDOW:, :]

            @pl.when(i > 0)
            def _():
                prv = pl.multiple_of((i - 1) * WINDOW, WINDOW)
                dk_ref[hh, pl.ds(prv, WINDOW), :] += dkb[:WINDOW, :]
                dv_ref[hh, pl.ds(prv, WINDOW), :] += dvb[:WINDOW, :]

            dsr = -psink * delta
            upd = jnp.zeros((8, 128), F32)
            for gq in range(ATT_G):
                v = jnp.sum(dsr[gq * WINDOW:(gq + 1) * WINDOW, :], axis=0, keepdims=True)
                upd = jnp.where((lane == gq) & (row == 0), v, upd)
            dsk_ref[hh] += upd

    qspec, cur, prev, tok = _attn_specs(S)
    full = pl.BlockSpec((ATT_HPS, S, HEAD), lambda h, i: (h, 0, 0))
    return pl.pallas_call(
        body, name=name, grid=(ATT_KV // ATT_HPS, nb),
        in_specs=[pl.BlockSpec(memory_space=pltpu.SMEM), qspec, cur, prev, cur, prev, tok],
        out_specs=[qspec, full, full, pl.BlockSpec((ATT_HPS, 8, 128), lambda h, i: (h, 0, 0))],
        out_shape=[jax.ShapeDtypeStruct((ATT_KV, ATT_G, S, HEAD), F32), jax.ShapeDtypeStruct((ATT_KV, S, HEAD), F32),
                   jax.ShapeDtypeStruct((ATT_KV, S, HEAD), F32), jax.ShapeDtypeStruct((ATT_KV, 8, 128), F32)],
        compiler_params=_cp(2),
    )(sinks, qh, kh, kh, vh, vh, doh)


def _heads_major(t, nh):
    S = t.shape[0]
    return t.reshape(S, nh, HEAD).transpose(1, 0, 2)


def _tokens_major(t):
    nh, S, _ = t.shape
    return t.transpose(1, 0, 2).reshape(S, nh * HEAD)


class _NoComm:
    def late_weights(self, w, after):
        return w

    def grads(self, group, tensors, after):
        return None


def _local_step(x, posf, target, w, comm=None):
    S, D = x.shape
    gr = {}
    comm = comm or _NoComm()

    (h1,) = _rms_fwd(x, [w["a_norm"]], name="a_norm_f", dep=w.get("dep"))
    zx = _mm(h1, w["w_zx"], name="in_proj_zx")
    dtr = _mm(h1, w["w_dt"], name="in_proj_dt")
    xbc = _conv_silu_fwd(zx, 2048, 4096, w["a_conv_w"], w["a_conv_b"], name="a_conv_f")
    dt_g, ac_g, sg_g, ac_t = _ssd_prep(dtr, w["a_dt_bias"], w["a_A_log"])
    y_ssd, states = _ssd_fwd(xbc, dt_g, ac_g, ac_t)
    yg = _gate_fwd(y_ssd, xbc, zx, w["a_Dexp"], w["a_gnorm"])
    x1 = _mm(yg, w["a_out_proj"], res=x, name="out_proj")

    w = comm.late_weights(w, x1)
    FW = w["f_w_in"][0].shape[2]

    def ffn_fwd(xin, l):
        (h,) = _rms_fwd(xin, [w["f_norm"][l]], name=f"f_norm_f{l}")
        u = _mm(h, w["f_w_in"][l], name=f"f_in{l}", dims=(S, N_CHIPS * FW, D), tn=FW,
                b_spec=pl.BlockSpec((None, D, FW), lambda i, j, k: (j, 0, 0)))
        a = _ffn_act_fwd(u, w["f_conv_w"][l], w["f_conv_b"][l], name=f"f_act_f{l}")
        xo = _mm(a, w["f_w_down"][l], res=xin, tk=a.shape[1], name=f"f_down{l}")
        return xo, (h, u)

    x2, ffn0 = ffn_fwd(x1, 0)

    hk, hq = _rms_fwd(x2, [w["kv_norm"], w["b_norm"]], name="kvq_norm_f")
    kv = _mm(hk, w["w_kv"], bias=w["b_kv"], name="kv_proj")
    q = _mm(hq, w["w_q"], bias=w["b_q"], name="q_proj")
    rope = _rope_cs(posf)
    kr = _hnrope_fwd(kv, 0, 256, w["k_norm_w"], rope, name="k_rope_f")
    qr = _hnrope_fwd(q, 0, 1024, w["q_norm_w"], rope, name="q_rope_f")
    qh = qr.reshape(ATT_KV, ATT_G, S, HEAD)
    kh = kr
    vh = _heads_major(kv[:, 256:].astype(BF16), ATT_KV)
    att = _attn_fwd(qh, kh, vh, w["sinks"])
    x3 = _mm(att, w["w_o"], bias=w["b_o"], res=x2, name="o_proj")
    x4, ffn1 = ffn_fwd(x3, 1)

    dy, loss_part = _loss(x4, target)

    def ffn_bwd(xin, l, saved, dyo, want_colsum, dep=None):
        h, u = saved
        da = _mm(dyo, w["f_w_down"][l], tb=True, name=f"f_down_dx{l}", dep=dep)
        du, dcw, dcb, a = _ffn_act_bwd(u, w["f_conv_w"][l], w["f_conv_b"][l], da, name=f"f_act_b{l}")
        dw_down = _mm(a, dyo, ta=True, out_dtype=BF16, name=f"f_down_dw{l}")
        dw_in = _mm(h, du, ta=True, out_dtype=BF16, name=f"f_in_dw{l}", dims=(D, N_CHIPS * FW, S), tm=D, tn=FW, tk=S,
                    b_spec=pl.BlockSpec((None, S, FW), lambda i, j, k: (j // 2, 0, j % 2)),
                    o_spec=pl.BlockSpec((None, D, FW), lambda i, j, k: (j, i, 0)), o_shape=(N_CHIPS, D, FW))
        ts = _pick(S, (1024, 512, 256))
        pieces = [(pl.BlockSpec((None, ts, FW), lambda i, j, k, q=q: (q // 2, i, q % 2)),
                   pl.BlockSpec((None, 512, FW), lambda i, j, k, q=q: (q, j, 0))) for q in range(N_CHIPS)]
        dh = _mm(du, w["f_w_in"][l], tb=True, name=f"f_in_dx{l}", dims=(S, D, FW), tm=ts, tn=512, tk=FW,
                 a_spec=pieces[0][0], b_spec=pieces[0][1], more=pieces[1:])
        outs = _rms_bwd(xin, [w["f_norm"][l]], [dh], dyo, name=f"f_norm_b{l}", want_colsum=want_colsum)
        g = dict(f_norm=outs[1], f_w_in=dw_in, f_conv_w=dcw, f_conv_b=dcb, f_w_down=dw_down)
        return outs[0], g, (outs[2] if want_colsum else None)

    dx3, gr["ffn1"], db_o = ffn_bwd(x3, 1, ffn1, dy, True)
    gr["b_o"] = db_o
    gr["w_o"] = _mm(att, dx3, ta=True, out_dtype=BF16, name="o_proj_dw")
    datt = _mm(dx3, w["w_o"], tb=True, out_dtype=BF16, name="o_proj_dx")
    dqh, dkh, dvh, dsk = _attn_bwd(qh, kh, vh, w["sinks"], datt)
    gr["sinks"] = dsk[:, 0, :4].reshape(1, 16)
    dv = _tokens_major(dvh).astype(BF16)
    dq, db_q, dqn = _hnrope_bwd(q, 0, 1024, w["q_norm_w"], rope, dqh.reshape(16, S, HEAD), name="q_rope_b")
    dk, db_k, dkn = _hnrope_bwd(kv, 0, 256, w["k_norm_w"], rope, dkh, name="k_rope_b")
    gr["q_norm"], gr["k_norm"] = dqn[:, :HEAD], dkn[:, :HEAD]
    gr["b_q"] = db_q
    gr["b_kv"] = jnp.concatenate([db_k, _colsum(dv, name="dv_colsum")], axis=1)
    dkv = jnp.concatenate([dk, dv], axis=1)
    gr["w_q"] = _mm(hq, dq, ta=True, out_dtype=BF16, name="q_proj_dw")
    gr["w_kv"] = _mm(hk, dkv, ta=True, out_dtype=BF16, name="kv_proj_dw")
    tok = comm.grads(1, dict(f_down1=gr["ffn1"]["f_w_down"], f_in1=gr["ffn1"]["f_w_in"], w_o=gr["w_o"], w_q=gr["w_q"],
                             w_kv=gr["w_kv"]), None)
    dhq = _mm(dq, w["w_q"], tb=True, name="q_proj_dx", dep=tok)
    dhk = _mm(dkv, w["w_kv"], tb=True, name="kv_proj_dx")
    dx2, gr["kv_norm"], gr["b_norm"] = _rms_bwd(x2, [w["kv_norm"], w["b_norm"]], [dhk, dhq], dx3, name="kvq_norm_b")

    dx1, gr["ffn0"], _ = ffn_bwd(x1, 0, ffn0, dx2, False)

    tok = comm.grads(2, dict(f_down0=gr["ffn0"]["f_w_down"], f_in0=gr["ffn0"]["f_w_in"]), dx1)
    gr["a_out_proj"] = _mm(yg, dx1, ta=True, out_dtype=BF16, name="out_proj_dw")
    dyg = _mm(dx1, w["a_out_proj"], tb=True, name="out_proj_dx", dep=tok)
    dy_ssd, dz, gr["a_gnorm"], dD = _gate_bwd(y_ssd, xbc, zx, w["a_Dexp"], w["a_gnorm"], dyg)
    gr["a_D"] = dD[:, :SSM_HEADS]
    dxs, dB, dC, dhead = _ssd_bwd(xbc, dt_g, ac_g, ac_t, states, dy_ssd, w["a_Dexp"])
    ddtr, dsmall = _ssd_post(dhead, dt_g, sg_g, w["a_A_log_g"])
    gr["a_A_log"] = dsmall[:, 0, :4].reshape(1, SSM_HEADS)
    gr["a_dt_bias"] = dsmall[:, 1, :4].reshape(1, SSM_HEADS)
    dxbc, gr["a_conv_w"], gr["a_conv_b"] = _conv_silu_bwd(
        zx, 2048, 4096, w["a_conv_w"], w["a_conv_b"], [(dxs, 0), (dB, 2048), (dC, 3072)], name="a_conv_b")
    gr["w_z"] = _mm(h1, dz, ta=True, out_dtype=BF16, name="in_proj_dwz")
    gr["w_x"] = _mm(h1, dxbc, ta=True, out_dtype=BF16, name="in_proj_dwx")
    gr["w_dt"] = _mm(h1, ddtr, ta=True, out_dtype=BF16, name="in_proj_dwdt")
    dh1 = _mm(dz, w["w_zx"], tb=True, name="in_proj_dxz")
    ts = _pick(S, (1024, 512, 256))
    halves = [(pl.BlockSpec((ts, 2048), lambda i, j, k, q=q: (i, q)), pl.BlockSpec((512, 2048), lambda i, j, k, q=q: (j, 1 + q)))
              for q in range(2)]
    dh1 = _mm(dxbc, w["w_zx"], tb=True, res=dh1, name="in_proj_dxx", dims=(S, D, 2048), tm=ts, tn=512, tk=2048,
              a_spec=halves[0][0], b_spec=halves[0][1], more=halves[1:])
    dh1 = _mm(ddtr, w["w_dt"], tb=True, res=dh1, name="in_proj_dxdt")
    dx0, gr["a_norm"] = _rms_bwd(x, [w["a_norm"]], [dh1], dx1, name="a_norm_b")
    comm.grads(3, dict(out_proj=gr["a_out_proj"], in_proj=_in_proj_grad(gr).reshape(D, N_CHIPS, -1).transpose(1, 0, 2)), dx0)
    return loss_part, dx0, gr


def _prep_small(full, w):
    w["a_norm"] = full["a_norm"]
    w["a_conv_w"] = full["a_conv_w"][0]
    w["a_conv_b"] = full["a_conv_b"]
    pad32 = lambda v: jnp.pad(v, ((0, 0), (0, 128 - SSM_HEADS)))
    w["a_dt_bias"] = pad32(full["a_dt_bias"])
    w["a_A_log"] = pad32(full["a_A_log"])
    w["a_A_log_g"] = jnp.pad(full["a_A_log"].reshape(SSM_GROUPS, 1, 4), ((0, 0), (0, 0), (0, 124)))
    w["a_Dexp"] = jnp.repeat(full["a_D"], HEAD, axis=1)
    w["a_gnorm"] = full["a_gnorm"]
    w["f_norm"] = [full["f_norm"][l:l + 1] for l in range(2)]
    w["f_conv_w"] = [full["f_conv_w"][l] for l in range(2)]
    w["f_conv_b"] = [full["f_conv_b"][l:l + 1] for l in range(2)]
    w["kv_norm"] = full["kv_norm"].reshape(1, -1)
    w["b_kv"] = full["b_kv"].reshape(1, -1)
    w["k_norm_w"] = jnp.tile(full["k_norm"].reshape(1, HEAD), (1, ATT_KV))
    w["b_norm"] = full["b_norm"]
    w["b_q"] = full["b_q"]
    w["q_norm_w"] = jnp.tile(full["q_norm"], (1, ATT_KV * ATT_G))
    w["sinks"] = full["sinks"].reshape(-1)
    w["b_o"] = full["b_o"]
    return w


def _split_in_proj(ip):
    return ip[:, :6144].astype(BF16), jnp.pad(ip[:, 6144:], ((0, 0), (0, 128 - SSM_HEADS))).astype(BF16)


def _prep_weights(full):
    w = _prep_small(full, {})
    w["w_zx"], w["w_dt"] = _split_in_proj(full["a_in_proj"][0])
    w["a_out_proj"] = full["a_out_proj"][0].astype(BF16)
    w["f_w_in"] = [full["f_w_in"][l].reshape(1024, N_CHIPS, -1).transpose(1, 0, 2).astype(BF16) for l in range(2)]
    w["f_w_down"] = [full["f_w_down"][l].astype(BF16) for l in range(2)]
    w["w_kv"] = full["w_kv"].astype(BF16)
    w["w_q"] = full["w_q"][0].astype(BF16)
    w["w_o"] = full["w_o"][0].astype(BF16)
    return w


def _small_grads(gr):
    g = {}
    g["a_norm"] = gr["a_norm"]
    g["a_conv_w"] = gr["a_conv_w"][None]
    g["a_conv_b"] = gr["a_conv_b"]
    g["a_dt_bias"], g["a_A_log"], g["a_D"] = gr["a_dt_bias"], gr["a_A_log"], gr["a_D"]
    g["a_gnorm"] = gr["a_gnorm"]
    g["kv_norm"] = gr["kv_norm"].reshape(-1)
    g["b_kv"] = gr["b_kv"].reshape(-1)
    g["k_norm"] = gr["k_norm"].reshape(-1)
    g["b_norm"] = gr["b_norm"]
    g["b_q"] = gr["b_q"]
    g["q_norm"] = gr["q_norm"]
    g["sinks"] = gr["sinks"]
    g["b_o"] = gr["b_o"]
    f = [gr["ffn0"], gr["ffn1"]]
    g["f_norm"] = jnp.concatenate([f[0]["f_norm"], f[1]["f_norm"]], axis=0)
    g["f_conv_w"] = jnp.stack([f[l]["f_conv_w"] for l in range(2)])
    g["f_conv_b"] = jnp.concatenate([f[l]["f_conv_b"] for l in range(2)], axis=0)
    return g


def _in_proj_grad(gr):
    return jnp.concatenate([gr["w_z"], gr["w_x"], gr["w_dt"][:, :SSM_HEADS]], axis=1)


def _full_grads(gr):
    g = _small_grads(gr)
    f32 = lambda t: t.astype(F32)
    g["a_in_proj"] = f32(_in_proj_grad(gr))[None]
    g["a_out_proj"] = f32(gr["a_out_proj"])[None]
    g["w_kv"] = f32(gr["w_kv"])
    g["w_q"] = f32(gr["w_q"])[None]
    g["w_o"] = f32(gr["w_o"])[None]
    f = [gr["ffn0"], gr["ffn1"]]
    g["f_w_in"] = jnp.stack([f32(f[l]["f_w_in"]).transpose(1, 0, 2).reshape(1024, -1) for l in range(2)])
    g["f_w_down"] = jnp.stack([f32(f[l]["f_w_down"]) for l in range(2)])
    return g


MESH = pl.DeviceIdType.MESH
WEIGHTS = ("a_norm", "a_in_proj", "a_conv_w", "a_conv_b", "a_dt_bias", "a_A_log", "a_D", "a_gnorm", "a_out_proj",
           "kv_norm", "w_kv", "b_kv", "k_norm", "b_norm", "w_q", "b_q", "q_norm", "sinks", "w_o", "b_o", "f_norm",
           "f_w_in", "f_conv_w", "f_conv_b", "f_w_down")
MATS = (("in_proj", "a_in_proj", 0), ("out_proj", "a_out_proj", 0), ("w_kv", "w_kv", None), ("w_q", "w_q", 0),
        ("w_o", "w_o", 0), ("f_in0", "f_w_in", 0), ("f_in1", "f_w_in", 1), ("f_down0", "f_w_down", 0),
        ("f_down1", "f_w_down", 1))
SMALL_CUT = (("a_norm", 1), ("a_conv_w", 2), ("a_conv_b", 1), ("a_gnorm", 1), ("f_conv_w", 2))
SMALL_REP = ("a_dt_bias", "a_A_log", "a_D", "kv_norm", "b_kv", "k_norm", "b_norm", "b_q", "q_norm", "sinks", "b_o",
             "f_norm", "f_conv_b")


def _coords():
    return lax.axis_index("x"), lax.axis_index("y"), lax.axis_index("c")


def _other_chips(x, y):
    return [(1 - x, y), (x, 1 - y), (1 - x, 1 - y)]


def _pack(arrs, rows_align, lanes, dtype):
    flat = jnp.concatenate([a.reshape(-1).astype(dtype) for a in arrs])
    per = rows_align * lanes
    total = -(-flat.shape[0] // per) * per
    return jnp.pad(flat, (0, total - flat.shape[0])).reshape(total // lanes, lanes)


def _unpack(flat, shapes):
    out, off = [], 0
    for s in shapes:
        n = math.prod(s)
        out.append(flat[off:off + n].reshape(s))
        off += n
    return out


def _remote(src, dst, send, recv, k, dev):
    return pltpu.make_async_remote_copy(src_ref=src, dst_ref=dst, send_sem=send.at[k], recv_sem=recv.at[k],
                                        device_id=dev, device_id_type=MESH)


_ANY = pl.BlockSpec(memory_space=pl.ANY)


def _halves(t):
    r, c = t.shape
    return t.reshape(2, r // 2, c)


def _gather_weights(shards, sp):
    n = len(shards)
    n_sem = 7 * n + 3

    def body(*refs):
        sh, sp_ref = refs[:n], refs[n]
        outs, sout = refs[n + 1:2 * n + 1], refs[2 * n + 1]
        send, recv, loc = refs[2 * n + 2:]
        x, y, c = _coords()
        me = 2 * x + y
        chips = _other_chips(x, y)
        sib = (x, y, 1 - c)
        l1 = pltpu.make_async_copy(sp_ref, sout.at[me], loc.at[0])
        l1.start()
        sends = []
        for j, (cx, cy) in enumerate(chips):
            sends.append(_remote(sp_ref, sout.at[me], send, recv, 7 * n + j, (cx, cy, c)))
            for t in range(n):
                sends.append(_remote(sh[t].at[c], outs[t].at[me, c], send, recv, 7 * t + j, (cx, cy, c)))
        for t in range(n):
            sends.append(_remote(sh[t], outs[t].at[me], send, recv, 7 * t + 6, sib))
        for cp in sends:
            cp.start()
        for j, (cx, cy) in enumerate(chips):
            src = 2 * cx + cy
            for t in range(n):
                _remote(sh[t].at[c], outs[t].at[src, c], send, recv, 7 * t + j, (cx, cy, c)).wait_recv()
                fwd = _remote(outs[t].at[src, c], outs[t].at[src, c], send, recv, 7 * t + 3 + j, sib)
                fwd.start()
                sends.append(fwd)
        for j, (cx, cy) in enumerate(chips):
            src = 2 * cx + cy
            _remote(sp_ref, sout.at[src], send, recv, 7 * n + j, (cx, cy, c)).wait_recv()
            for t in range(n):
                _remote(outs[t].at[src, 1 - c], outs[t].at[src, 1 - c], send, recv, 7 * t + 3 + j, sib).wait_recv()
        for t in range(n):
            _remote(sh[t], outs[t].at[me], send, recv, 7 * t + 6, sib).wait_recv()
        for cp in sends:
            cp.wait_send()
        l1.wait()

    res = pl.pallas_call(
        body, name="gather_weights", in_specs=[_ANY] * (n + 1), out_specs=[_ANY] * (n + 1),
        out_shape=[jax.ShapeDtypeStruct((N_CHIPS,) + t.shape, t.dtype) for t in shards]
        + [jax.ShapeDtypeStruct((N_CHIPS,) + sp.shape, sp.dtype)],
        scratch_shapes=[pltpu.SemaphoreType.DMA((n_sem,)), pltpu.SemaphoreType.DMA((n_sem,)),
                        pltpu.SemaphoreType.DMA((1,))],
    )(*shards, sp)
    return res[:n], res[n]


_HBM = pl.BlockSpec(memory_space=pltpu.HBM)
_SEMS = pl.BlockSpec(memory_space=pltpu.SEMAPHORE)
_DATAFLOW = pltpu.SideEffectType.DATAFLOW_SIDE_EFFECTING


def _in_hbm(a):
    return pltpu.with_memory_space_constraint(a, pltpu.HBM)


def _gather_copies(sh, land, send, recv):
    x, y, c = _coords()
    me = 2 * x + y
    out = []
    for t in range(len(sh)):
        for j, (cx, cy) in enumerate(_other_chips(x, y)):
            dev = (cx, cy, c)
            out.append((_remote(sh[t].at[c], land[t].at[me, c], send, recv, 4 * t + j, dev),
                        _remote(sh[t].at[c], land[t].at[2 * cx + cy, c], send, recv, 4 * t + j, dev)))
        sib = (x, y, 1 - c)
        out.append((_remote(sh[t], land[t].at[me], send, recv, 4 * t + 3, sib),
                    _remote(sh[t], land[t].at[me], send, recv, 4 * t + 3, sib)))
    return out


def _gather_start(shards, after, *, name):
    n = len(shards)

    def body(*refs):
        sh, land = refs[:n], refs[n:2 * n]
        send, recv = refs[2 * n + 1], refs[2 * n + 2]
        token = refs[-1]
        for mine, _ in _gather_copies(sh, land, send, recv):
            mine.start()
        token[...] = jnp.zeros_like(token)

    lands = [_in_hbm(lax.empty((N_CHIPS,) + s.shape, s.dtype)) for s in shards]
    res = pl.pallas_call(
        body, name=name, in_specs=[_HBM] * (2 * n) + [_ANY],
        out_specs=[_SEMS, _SEMS] + [_HBM] * (2 * n) + [pl.BlockSpec(memory_space=pltpu.VMEM)],
        out_shape=[pltpu.SemaphoreType.DMA((4 * n,)), pltpu.SemaphoreType.DMA((4 * n,))]
        + [pltpu.HBM(s.shape, s.dtype) for s in shards] + [pltpu.HBM(l.shape, l.dtype) for l in lands]
        + [jax.ShapeDtypeStruct((8, 128), F32)],
        input_output_aliases={t: 2 + t for t in range(2 * n)},
        compiler_params=pltpu.CompilerParams(has_side_effects=_DATAFLOW),
    )(*[_in_hbm(s) for s in shards], *lands, after)
    return res[0], res[1], res[2:2 + n], res[2 + n:2 + 2 * n], res[-1]


def _gather_wait(send, recv, shards, lands, after, *, name):
    n = len(shards)

    def body(*refs):
        sh, land = refs[:n], refs[n:2 * n]
        send_r, recv_r = refs[2 * n], refs[2 * n + 1]
        for mine, theirs in _gather_copies(sh, land, send_r, recv_r):
            mine.wait_send()
            theirs.wait_recv()

    res = pl.pallas_call(
        body, name=name, in_specs=[_HBM] * (2 * n) + [_SEMS, _SEMS, _ANY], out_specs=[_HBM] * (2 * n),
        out_shape=[pltpu.HBM(s.shape, s.dtype) for s in shards] + [pltpu.HBM(l.shape, l.dtype) for l in lands],
        input_output_aliases={t: t for t in range(2 * n)},
        compiler_params=pltpu.CompilerParams(has_side_effects=_DATAFLOW),
    )(*shards, *lands, send, recv, after)
    return res[n:]


def _gather_forward(lands, *, name):
    n = len(lands)

    def body(*refs):
        o = refs[n:2 * n]
        send, recv = refs[2 * n:]
        x, y, c = _coords()
        sib = (x, y, 1 - c)
        srcs = [2 * cx + cy for cx, cy in _other_chips(x, y)]
        cps = [_remote(o[t].at[s, c], o[t].at[s, c], send, recv, 3 * t + j, sib) for t in range(n) for j, s in enumerate(srcs)]
        for cp in cps:
            cp.start()
        for t in range(n):
            for j, s in enumerate(srcs):
                _remote(o[t].at[s, 1 - c], o[t].at[s, 1 - c], send, recv, 3 * t + j, sib).wait_recv()
        for cp in cps:
            cp.wait_send()

    return pl.pallas_call(
        body, name=name, in_specs=[_ANY] * n, out_specs=[_ANY] * n, input_output_aliases={t: t for t in range(n)},
        out_shape=[jax.ShapeDtypeStruct(l.shape, l.dtype) for l in lands],
        scratch_shapes=[pltpu.SemaphoreType.DMA((3 * n,)), pltpu.SemaphoreType.DMA((3 * n,))],
    )(*lands)


def _allreduce_small(v):
    SR = v.shape[0]

    def body(v_ref, o_ref, buf, send, recv):
        x, y, c = _coords()
        me = 4 * x + 2 * y + c
        buf[me] = v_ref[...]
        peers = []
        for k in range(1, 8):
            px = 1 - x if k & 4 else x
            py = 1 - y if k & 2 else y
            pc = 1 - c if k & 1 else c
            peers.append((px, py, pc))
        cps = [_remote(v_ref, buf.at[me], send, recv, k, p) for k, p in enumerate(peers)]
        for cp in cps:
            cp.start()
        for k, (px, py, pc) in enumerate(peers):
            _remote(v_ref, buf.at[4 * px + 2 * py + pc], send, recv, k, (px, py, pc)).wait_recv()
        for cp in cps:
            cp.wait_send()
        acc = buf[0]
        for s in range(1, 8):
            acc = acc + buf[s]
        o_ref[...] = acc

    vm = pl.BlockSpec(memory_space=pltpu.VMEM)
    return pl.pallas_call(
        body, name="allreduce_small", in_specs=[vm], out_specs=vm, out_shape=jax.ShapeDtypeStruct(v.shape, F32),
        scratch_shapes=[pltpu.VMEM((8, SR, 128), F32), pltpu.SemaphoreType.DMA((7,)), pltpu.SemaphoreType.DMA((7,))],
    )(v)


def _rs_to_sibling(gs, *, name):
    n = len(gs)

    def body(*refs):
        g, a = refs[:n], refs[n:2 * n]
        send, recv = refs[2 * n:]
        x, y, c = _coords()
        cps = [_remote(g[t].at[:, 1 - c], a[t], send, recv, t, (x, y, 1 - c)) for t in range(n)]
        for cp in cps:
            cp.start()
        for cp in cps:
            cp.wait()

    return pl.pallas_call(
        body, name=name, in_specs=[_ANY] * n, out_specs=[_ANY] * n,
        out_shape=[jax.ShapeDtypeStruct((N_CHIPS,) + g.shape[2:], g.dtype) for g in gs],
        scratch_shapes=[pltpu.SemaphoreType.DMA((n,)), pltpu.SemaphoreType.DMA((n,))],
    )(*gs)


RS_ROW_SPLIT = 2


def _rs_add_pair(gs, as_, c_idx, *, name):
    n = len(gs)

    def body(c_ref, *refs):
        for t in range(n):
            refs[2 * n + t][...] = (refs[t][...].astype(F32) + refs[n + t][...].astype(F32)).astype(BF16)

    def gspec(g):
        _, _, rh, cols = g.shape
        return pl.BlockSpec((None, None, rh // RS_ROW_SPLIT, cols), lambda j, i, c_ref: (j, c_ref[0], i, 0))

    def pspec(g):
        _, _, rh, cols = g.shape
        return pl.BlockSpec((None, rh // RS_ROW_SPLIT, cols), lambda j, i, c_ref: (j, i, 0))

    return pl.pallas_call(
        body, name=name,
        grid_spec=pltpu.PrefetchScalarGridSpec(
            num_scalar_prefetch=1, grid=(N_CHIPS, RS_ROW_SPLIT),
            in_specs=[gspec(g) for g in gs] + [pspec(g) for g in gs], out_specs=[pspec(g) for g in gs]),
        out_shape=[jax.ShapeDtypeStruct((N_CHIPS,) + g.shape[2:], BF16) for g in gs], compiler_params=_cp(2),
    )(c_idx, *gs, *as_)


def _chips_copies(p, r, send, recv):
    x, y, c = _coords()
    return [_remote(p[t].at[2 * cx + cy], r[t].at[k], send, recv, 3 * t + k, (cx, cy, c))
            for k, (cx, cy) in enumerate(_other_chips(x, y)) for t in range(len(p))]


def _rs_chips_start(ps, *, name):
    n = len(ps)

    def body(*refs):
        p, r = refs[:n], refs[n:2 * n]
        send, recv = refs[2 * n], refs[2 * n + 1]
        token = refs[-1]
        for cp in _chips_copies(p, r, send, recv):
            cp.start()
        token[...] = jnp.zeros_like(token)

    lands = [_in_hbm(lax.empty((3,) + p.shape[1:], p.dtype)) for p in ps]
    res = pl.pallas_call(
        body, name=name, in_specs=[_HBM] * (2 * n),
        out_specs=[_SEMS, _SEMS] + [_HBM] * (2 * n) + [pl.BlockSpec(memory_space=pltpu.VMEM)],
        out_shape=[pltpu.SemaphoreType.DMA((3 * n,)), pltpu.SemaphoreType.DMA((3 * n,))]
        + [pltpu.HBM(p.shape, p.dtype) for p in ps] + [pltpu.HBM(l.shape, l.dtype) for l in lands]
        + [jax.ShapeDtypeStruct((8, 128), F32)],
        input_output_aliases={t: 2 + t for t in range(2 * n)},
        compiler_params=pltpu.CompilerParams(has_side_effects=_DATAFLOW),
    )(*[_in_hbm(p) for p in ps], *lands)
    return res[0], res[1], res[2:2 + n], res[2 + n:2 + 2 * n], res[-1]


def _rs_chips_wait(send, recv, ps, lands, after, *, name):
    n = len(ps)

    def body(*refs):
        p, r = refs[:n], refs[n:2 * n]
        for cp in _chips_copies(p, r, refs[2 * n], refs[2 * n + 1]):
            cp.wait_send()
            cp.wait_recv()

    res = pl.pallas_call(
        body, name=name, in_specs=[_HBM] * (2 * n) + [_SEMS, _SEMS] + [_ANY] * len(after), out_specs=[_HBM] * (2 * n),
        out_shape=[pltpu.HBM(p.shape, p.dtype) for p in ps] + [pltpu.HBM(l.shape, l.dtype) for l in lands],
        input_output_aliases={t: t for t in range(2 * n)},
        compiler_params=pltpu.CompilerParams(has_side_effects=_DATAFLOW),
    )(*ps, *lands, send, recv, *after)
    return res[:n], res[n:]


def _rs_add_chips(ps, rs, idx, *, name):
    n = len(ps)

    def body(idx_ref, *refs):
        for t in range(n):
            p_ref, r0, r1, r2 = refs[4 * t:4 * t + 4]
            refs[4 * n + t][...] = ((p_ref[...].astype(F32) + r0[...].astype(F32)) + r1[...].astype(F32)) + r2[...].astype(F32)

    in_specs, args = [], []
    for p, r in zip(ps, rs):
        _, rh, cols = p.shape
        blk = (None, rh // RS_ROW_SPLIT, cols)
        in_specs.append(pl.BlockSpec(blk, lambda i, idx_ref: (idx_ref[0], i, 0)))
        in_specs += [pl.BlockSpec(blk, lambda i, idx_ref, k=k: (k, i, 0)) for k in range(3)]
        args += [p, r, r, r]
    out_specs = [pl.BlockSpec((None, p.shape[1] // RS_ROW_SPLIT, p.shape[2]), lambda i, idx_ref: (idx_ref[1], i, 0))
                 for p in ps]
    return pl.pallas_call(
        body, name=name,
        grid_spec=pltpu.PrefetchScalarGridSpec(num_scalar_prefetch=1, grid=(RS_ROW_SPLIT,), in_specs=in_specs,
                                               out_specs=out_specs),
        out_shape=[jax.ShapeDtypeStruct((2,) + p.shape[1:], F32) for p in ps], compiler_params=_cp(1),
    )(idx, *args)


def _rs_join_halves(hs, *, name):
    n = len(hs)

    def body(*refs):
        o = refs[n:2 * n]
        send, recv = refs[2 * n:]
        x, y, c = _coords()
        cps = [_remote(o[t].at[c], o[t].at[c], send, recv, t, (x, y, 1 - c)) for t in range(n)]
        for cp in cps:
            cp.start()
        for t in range(n):
            _remote(o[t].at[1 - c], o[t].at[1 - c], send, recv, t, (x, y, 1 - c)).wait_recv()
        for cp in cps:
            cp.wait_send()

    return pl.pallas_call(
        body, name=name, in_specs=[_ANY] * n, out_specs=[_ANY] * n,
        input_output_aliases={t: t for t in range(n)},
        out_shape=[jax.ShapeDtypeStruct(h.shape, F32) for h in hs],
        scratch_shapes=[pltpu.SemaphoreType.DMA((n,)), pltpu.SemaphoreType.DMA((n,))],
    )(*hs)


def _adamw(w, gs, m, v, *, name, dep=None):
    L, Rr, C = w.shape
    tr, tc = _pick(Rr, (256, 128, 64)), C
    if tr == Rr and Rr * C > 512 * 1024:
        tc = 256
    bc1 = 1.0 - ADAM_B1 ** ADAM_STEP
    bc2 = 1.0 - ADAM_B2 ** ADAM_STEP
    nd = 0 if dep is None else 1

    def body(*refs):
        w_ref, m_ref, v_ref = refs[0], refs[1], refs[2]
        g_refs = refs[3:3 + L]
        d_ref, mo_ref, vo_ref, go_ref = refs[3 + L + nd:]
        layer = pl.program_id(0)
        gv = g_refs[0][...]
        for q in range(1, L):
            gv = jnp.where(layer == q, g_refs[q][...], gv)
        mn = ADAM_B1 * m_ref[...] + (1.0 - ADAM_B1) * gv
        vn = ADAM_B2 * v_ref[...] + (1.0 - ADAM_B2) * (gv * gv)
        go_ref[...] = gv
        mo_ref[...] = mn
        vo_ref[...] = vn
        d_ref[...] = -ADAM_LR * ((mn / bc1) / (jnp.sqrt(vn / bc2) + ADAM_EPS) + ADAM_WD * w_ref[...])

    blk = pl.BlockSpec((None, tr, tc), lambda l, i, j: (l, i, j))
    gblks = [pl.BlockSpec((tr, tc), lambda l, i, j, q=q: (jnp.where(l == q, i, 0), jnp.where(l == q, j, 0))) for q in range(L)]
    return pl.pallas_call(
        body, name=name, grid=(L, Rr // tr, C // tc), in_specs=[blk] * 3 + gblks + [_ANY] * nd, out_specs=[blk] * 4,
        out_shape=[jax.ShapeDtypeStruct((L, Rr, C), F32)] * 4, compiler_params=_cp(3),
    )(w, m, v, *gs, *([] if dep is None else [dep]))


def kernel(x, positions, a_norm, a_in_proj, a_conv_w, a_conv_b, a_dt_bias, a_A_log, a_D, a_gnorm, a_out_proj,
           kv_norm, w_kv, b_kv, k_norm, b_norm, w_q, b_q, q_norm, sinks, w_o, b_o, f_norm, f_w_in, f_conv_w,
           f_conv_b, f_w_down, loss_target, m_a_norm, m_a_in_proj, m_a_conv_w, m_a_conv_b, m_a_dt_bias, m_a_A_log,
           m_a_D, m_a_gnorm, m_a_out_proj, m_kv_norm, m_w_kv, m_b_kv, m_k_norm, m_b_norm, m_w_q, m_b_q, m_q_norm,
           m_sinks, m_w_o, m_b_o, m_f_norm, m_f_w_in, m_f_conv_w, m_f_conv_b, m_f_w_down, v_a_norm, v_a_in_proj,
           v_a_conv_w, v_a_conv_b, v_a_dt_bias, v_a_A_log, v_a_D, v_a_gnorm, v_a_out_proj, v_kv_norm, v_w_kv,
           v_b_kv, v_k_norm, v_b_norm, v_w_q, v_b_q, v_q_norm, v_sinks, v_w_o, v_b_o, v_f_norm, v_f_w_in,
           v_f_conv_w, v_f_conv_b, v_f_w_down):
    wl = dict(zip(WEIGHTS, (a_norm, a_in_proj, a_conv_w, a_conv_b, a_dt_bias, a_A_log, a_D, a_gnorm, a_out_proj,
                            kv_norm, w_kv, b_kv, k_norm, b_norm, w_q, b_q, q_norm, sinks, w_o, b_o, f_norm, f_w_in,
                            f_conv_w, f_conv_b, f_w_down)))
    ml = dict(zip(WEIGHTS, (m_a_norm, m_a_in_proj, m_a_conv_w, m_a_conv_b, m_a_dt_bias, m_a_A_log, m_a_D, m_a_gnorm,
                            m_a_out_proj, m_kv_norm, m_w_kv, m_b_kv, m_k_norm, m_b_norm, m_w_q, m_b_q, m_q_norm,
                            m_sinks, m_w_o, m_b_o, m_f_norm, m_f_w_in, m_f_conv_w, m_f_conv_b, m_f_w_down)))
    vl = dict(zip(WEIGHTS, (v_a_norm, v_a_in_proj, v_a_conv_w, v_a_conv_b, v_a_dt_bias, v_a_A_log, v_a_D, v_a_gnorm,
                            v_a_out_proj, v_kv_norm, v_w_kv, v_b_kv, v_k_norm, v_b_norm, v_w_q, v_b_q, v_q_norm,
                            v_sinks, v_w_o, v_b_o, v_f_norm, v_f_w_in, v_f_conv_w, v_f_conv_b, v_f_w_down)))
    xi, yi, ci = _coords()
    me = 2 * xi + yi
    S = x.shape[1]

    def block_of(n, layer):
        t = wl[n]
        return t if layer is None else t[layer]

    rows = lambda t: t.reshape(-1, t.shape[-1])
    c_idx = jnp.reshape(ci, (1,)).astype(jnp.int32)
    me_c = jnp.stack([me, ci]).astype(jnp.int32)
    early = ("in_proj", "out_proj")
    late = tuple(name for name, _, _ in MATS if name not in early)
    shards = {name: _halves(block_of(wn, layer).astype(BF16)) for name, wn, layer in MATS}

    sp = _pack([wl[n] for n, _ in SMALL_CUT], 8, 128, F32)
    gathered, gs = _gather_weights([shards[k] for k in early], sp)
    gt = {k: t.reshape(N_CHIPS, -1, t.shape[-1]) for k, t in zip(early, gathered)}
    started = _gather_start([shards[k] for k in late], gs, name="gather_late_start")
    full = {n: wl[n] for n in SMALL_REP}
    gs = gs.reshape(N_CHIPS, -1)
    pieces = [_unpack(gs[j], [wl[n].shape for n, _ in SMALL_CUT]) for j in range(N_CHIPS)]
    for q, (n, ax) in enumerate(SMALL_CUT):
        full[n] = jnp.concatenate([pieces[j][q] for j in range(N_CHIPS)], axis=ax)
    w = _prep_small(full, {})
    w["w_zx"], w["w_dt"] = _split_in_proj(gt["in_proj"].transpose(1, 0, 2).reshape(1024, -1))
    w["a_out_proj"] = rows(gt["out_proj"])
    w["dep"] = started[4]

    class Comm:
        pending = None
        token = None
        reduced = {}

        def late_weights(self, w, after):
            lands = _gather_wait(started[0], started[1], started[2], started[3], after, name="gather_late_wait")
            lands = _gather_forward(lands, name="gather_late_forward")
            lt = {k: t.reshape(N_CHIPS, -1, t.shape[-1]) for k, t in zip(late, lands)}
            w = dict(w)
            w["w_kv"], w["w_q"], w["w_o"] = (rows(lt[k]) for k in ("w_kv", "w_q", "w_o"))
            w["f_w_in"] = [lt["f_in0"], lt["f_in1"]]
            w["f_w_down"] = [rows(lt["f_down0"]), rows(lt["f_down1"])]
            return w

        def finish(self, after):
            names, send, recv, ps, lands, tag = self.pending
            ps, rs = _rs_chips_wait(send, recv, ps, lands, after, name=f"rs_chips_wait{tag}")
            halves = _rs_add_chips(ps, rs, me_c, name=f"rs_add_chips{tag}")
            joined = _rs_join_halves(halves, name=f"rs_join_halves{tag}")
            self.reduced.update({k: rows(t) for k, t in zip(names, joined)})
            self.pending = None

        def grads(self, group, tensors, after):
            if self.pending is not None:
                self.finish([after])
            names = list(tensors)
            glist = [tensors[k].reshape(N_CHIPS, 2, -1, tensors[k].shape[-1]) for k in names]
            from_sib = _rs_to_sibling(glist, name=f"rs_to_sibling{group}")
            pairs = _rs_add_pair(glist, from_sib, c_idx, name=f"rs_add_pair{group}")
            send, recv, ps, lands, token = _rs_chips_start(pairs, name=f"rs_chips_start{group}")
            self.pending = (names, send, recv, ps, lands, group)
            self.token = token
            return token

    comm = Comm()

    posf = positions.reshape(S, 1).astype(F32)
    loss_part, dx0, gr = _local_step(x[0], posf, loss_target[0], w, comm)
    g = _small_grads(gr)

    small_names = [n for n, _ in SMALL_CUT] + list(SMALL_REP)
    sv = _pack([g[n] for n in small_names] + [loss_part[0:1, 0:1]], 8, 128, F32)
    sred = _allreduce_small(sv).reshape(-1)
    small_shapes = [g[n].shape for n in small_names] + [(1,)]
    sg = dict(zip(small_names + ["loss"], _unpack(sred, small_shapes)))
    loss = sg["loss"].reshape(())
    g_small = {}
    for n, ax in SMALL_CUT:
        size = wl[n].shape[ax]
        g_small[n] = lax.dynamic_slice_in_dim(sg[n], me * size, size, axis=ax)
    for n in SMALL_REP:
        g_small[n] = sg[n].reshape(wl[n].shape)

    grads, delta, new_m, new_v = {}, {}, {}, {}

    def update(wn, dep):
        gl = [comm.reduced[name] for name, n2, _ in MATS if n2 == wn]
        shp = wl[wn].shape
        three = (len(gl),) + gl[0].shape
        flip = shp[-1] % 128 != 0
        view = (lambda t: t.reshape(three).transpose(0, 2, 1)) if flip else (lambda t: t.reshape(three))
        back = (lambda t: t.transpose(0, 2, 1).reshape(shp)) if flip else (lambda t: t.reshape(shp))
        if flip:
            gl = [t.T for t in gl]
        d, mn, vn, go = _adamw(view(wl[wn]), gl, view(ml[wn]), view(vl[wn]), name="adamw_" + wn, dep=dep)
        grads[wn], delta[wn], new_m[wn], new_v[wn] = back(go), back(d), back(mn), back(vn)
        return d

    comm.finish([update(wn, comm.token) for wn in ("f_w_in", "f_w_down", "w_q", "w_o", "w_kv")])
    for wn in ("a_in_proj", "a_out_proj"):
        update(wn, None)
    pk = lambda d: _pack([d[n] for n in small_names], 8, 128, F32)[None]
    d, mn, vn, _ = _adamw(pk(wl), [pk(g_small)[0]], pk(ml), pk(vl), name="adamw_small")
    shapes = [wl[n].shape for n in small_names]
    for n, dd, mm, vv in zip(small_names, _unpack(d.reshape(-1), shapes), _unpack(mn.reshape(-1), shapes),
                             _unpack(vn.reshape(-1), shapes)):
        grads[n], delta[n], new_m[n], new_v[n] = g_small[n], dd, mm, vv

    return (loss, dx0[None], *[grads[n] for n in WEIGHTS], *[delta[n] for n in WEIGHTS],
            *[new_m[n] for n in WEIGHTS], *[new_v[n] for n in WEIGHTS])
```

```python
import math

import jax
import jax.numpy as jnp
from jax import lax
from jax.experimental import pallas as pl
from jax.experimental.pallas import tpu as pltpu

F32 = jnp.float32
BF16 = jnp.bfloat16

EPS = 1e-5
CHUNK = 256
WINDOW = 128
HEAD = 64
SSM_HEADS = 32
SSM_GROUPS = 8
SSM_STATE = 128
ATT_KV = 4
ATT_G = 4
ROPE_THETA = 10000.0
NEG = -1e30
N_CHIPS = 4
VMEM_LIMIT = 56 * 1024 * 1024

ADAM_LR, ADAM_B1, ADAM_B2, ADAM_EPS, ADAM_WD, ADAM_STEP = 0.001, 0.9, 0.999, 1e-08, 0.01, 10


def _cp(n_axes):
    return pltpu.CompilerParams(dimension_semantics=("arbitrary",) * n_axes, vmem_limit_bytes=VMEM_LIMIT)


def _pick(dim, prefs):
    for p in prefs:
        if dim % p == 0:
            return p
    return dim


def _iota(shape, dim):
    return lax.broadcasted_iota(jnp.int32, shape, dim)


def _dot(a, b, ca=1, cb=0):
    return lax.dot_general(a, b, (((ca,), (cb,)), ((), ())), preferred_element_type=F32)


def _dot3(x, ind):
    h = x.astype(BF16)
    r = x - h.astype(F32)
    m = r.astype(BF16)
    lo = (r - m.astype(F32)).astype(BF16)
    return _dot(h, ind) + _dot(m, ind) + _dot(lo, ind)


def _sigmoid(x):
    return jax.nn.sigmoid(x)


def _mm(a, b, *, name, ta=False, tb=False, bias=None, res=None, out_dtype=F32, b_koff=0, tm=None, tn=None, tk=None,
        dims=None, a_spec=None, b_spec=None, o_spec=None, o_shape=None, dep=None, more=()):
    if dims is not None:
        M, N, K = dims
    else:
        if ta:
            K, M = a.shape
        else:
            M, K = a.shape
        N = b.shape[0] if tb else b.shape[1]
    tm = tm or _pick(M, (1024, 1408, 512, 256, 128))
    tn = tn or _pick(N, (512, 1408, 256, 128))
    tk = tk or (K if K <= 2048 else _pick(K, (2048, 1408, 1024, 512)))
    assert M % tm == 0 and N % tn == 0 and K % tk == 0 and b_koff % tk == 0
    nk = K // tk
    kb0 = b_koff // tk
    has_bias, has_res = bias is not None, res is not None

    def body(*refs):
        a_ref, b_ref = refs[0], refs[1]
        pos = 2
        bias_ref = res_ref = acc_ref = None
        if has_bias:
            bias_ref = refs[pos]
            pos += 1
        if has_res:
            res_ref = refs[pos]
            pos += 1
        if dep is not None:
            pos += 1
        extra = refs[pos:pos + 2 * len(more)]
        pos += 2 * len(more)
        o_ref = refs[pos]
        if nk > 1:
            acc_ref = refs[pos + 1]
        part = _dot(a_ref[...].astype(BF16), b_ref[...].astype(BF16), 0 if ta else 1, 1 if tb else 0)
        for q in range(len(more)):
            part = part + _dot(extra[2 * q][...].astype(BF16), extra[2 * q + 1][...].astype(BF16),
                               0 if ta else 1, 1 if tb else 0)

        def finish(acc):
            if has_bias:
                acc = acc + bias_ref[...]
            if has_res:
                acc = acc + res_ref[...]
            o_ref[...] = acc.astype(out_dtype)

        if nk == 1:
            finish(part)
        else:
            k = pl.program_id(2)

            @pl.when(k == 0)
            def _():
                acc_ref[...] = part

            @pl.when(k > 0)
            def _():
                acc_ref[...] += part

            @pl.when(k == nk - 1)
            def _():
                finish(acc_ref[...])

    if a_spec is None:
        a_spec = pl.BlockSpec((tk, tm), lambda i, j, k: (k, i)) if ta else pl.BlockSpec((tm, tk), lambda i, j, k: (i, k))
    if b_spec is None:
        b_spec = (pl.BlockSpec((tn, tk), lambda i, j, k: (j, k + kb0)) if tb
                  else pl.BlockSpec((tk, tn), lambda i, j, k: (k + kb0, j)))
    if o_spec is None:
        o_spec = pl.BlockSpec((tm, tn), lambda i, j, k: (i, j))
    in_specs, args = [a_spec, b_spec], [a, b]
    if has_bias:
        in_specs.append(pl.BlockSpec((1, tn), lambda i, j, k: (0, j)))
        args.append(bias)
    if has_res:
        in_specs.append(pl.BlockSpec((tm, tn), lambda i, j, k: (i, j)))
        args.append(res)
    if dep is not None:
        in_specs.append(pl.BlockSpec(memory_space=pl.ANY))
        args.append(dep)
    for sa, sb in more:
        in_specs += [sa, sb]
        args += [a, b]
    return pl.pallas_call(
        body, name=name, grid=(M // tm, N // tn, nk), in_specs=in_specs, out_specs=o_spec,
        out_shape=jax.ShapeDtypeStruct(o_shape or (M, N), out_dtype),
        scratch_shapes=[pltpu.VMEM((tm, tn), F32)] if nk > 1 else [],
        compiler_params=_cp(3),
    )(*args)


def _rms_fwd(x, gains, *, name, tr=256, dep=None):
    S, D = x.shape
    n = len(gains)
    nd = 0 if dep is None else 1

    def body(*refs):
        xv = refs[0][...]
        xh = xv * lax.rsqrt(jnp.mean(xv * xv, axis=-1, keepdims=True) + EPS)
        for q in range(n):
            refs[1 + n + nd + q][...] = (xh * refs[1 + q][...]).astype(BF16)

    row = pl.BlockSpec((tr, D), lambda i: (i, 0))
    vec = pl.BlockSpec((1, D), lambda i: (0, 0))
    return pl.pallas_call(
        body, name=name, grid=(S // tr,), in_specs=[row] + [vec] * n + [pl.BlockSpec(memory_space=pl.ANY)] * nd,
        out_specs=[row] * n, out_shape=[jax.ShapeDtypeStruct((S, D), BF16)] * n, compiler_params=_cp(1),
    )(x, *gains, *([] if dep is None else [dep]))


def _rms_bwd(x, gains, dhs, dres, *, name, tr=256, want_colsum=False):
    S, D = x.shape
    n = len(gains)
    steps = S // tr

    def body(*refs):
        x_ref = refs[0]
        g_refs = refs[1:1 + n]
        dh_refs = refs[1 + n:1 + 2 * n]
        dres_ref = refs[1 + 2 * n]
        dx_ref = refs[2 + 2 * n]
        dg_refs = refs[3 + 2 * n:3 + 3 * n]
        cs_ref = refs[3 + 3 * n] if want_colsum else None
        i = pl.program_id(0)
        xv = x_ref[...]
        r = lax.rsqrt(jnp.mean(xv * xv, axis=-1, keepdims=True) + EPS)
        xh = xv * r
        dx = dres_ref[...]
        for q in range(n):
            dh = dh_refs[q][...]
            dxh = dh * g_refs[q][...]
            dx = dx + r * (dxh - xh * jnp.mean(dxh * xh, axis=-1, keepdims=True))
            part = jnp.sum(dh * xh, axis=0, keepdims=True)

            @pl.when(i == 0)
            def _():
                dg_refs[q][...] = part

            @pl.when(i > 0)
            def _():
                dg_refs[q][...] += part

        dx_ref[...] = dx
        if want_colsum:
            cpart = jnp.sum(dx, axis=0, keepdims=True)

            @pl.when(i == 0)
            def _():
                cs_ref[...] = cpart

            @pl.when(i > 0)
            def _():
                cs_ref[...] += cpart

    row = pl.BlockSpec((tr, D), lambda i: (i, 0))
    vec = pl.BlockSpec((1, D), lambda i: (0, 0))
    n_vec_out = n + (1 if want_colsum else 0)
    outs = pl.pallas_call(
        body, name=name, grid=(steps,), in_specs=[row] + [vec] * n + [row] * n + [row],
        out_specs=[row] + [vec] * n_vec_out,
        out_shape=[jax.ShapeDtypeStruct((S, D), F32)] + [jax.ShapeDtypeStruct((1, D), F32)] * n_vec_out,
        compiler_params=_cp(1),
    )(x, *gains, *dhs, dres)
    return outs


def _colsum(x, *, name, tr=256):
    S, D = x.shape

    def body(x_ref, o_ref):
        i = pl.program_id(0)
        part = jnp.sum(x_ref[...].astype(F32), axis=0, keepdims=True)

        @pl.when(i == 0)
        def _():
            o_ref[...] = part

        @pl.when(i > 0)
        def _():
            o_ref[...] += part

    return pl.pallas_call(
        body, name=name, grid=(S // tr,), in_specs=[pl.BlockSpec((tr, D), lambda i: (i, 0))],
        out_specs=pl.BlockSpec((1, D), lambda i: (0, 0)), out_shape=jax.ShapeDtypeStruct((1, D), F32),
        compiler_params=_cp(1),
    )(x)


def _loss(y, t, *, name="loss", tr=256):
    S, D = y.shape
    steps = S // tr

    def body(y_ref, t_ref, dy_ref, l_ref, acc_ref):
        i = pl.program_id(0)
        e = y_ref[...] - t_ref[...]
        dy_ref[...] = e * (1.0 / D)
        part = jnp.sum(e * e, axis=0, keepdims=True)

        @pl.when(i == 0)
        def _():
            acc_ref[...] = part

        @pl.when(i > 0)
        def _():
            acc_ref[...] += part

        @pl.when(i == steps - 1)
        def _():
            tot = jnp.sum(acc_ref[...], axis=1, keepdims=True) * (0.5 / D)
            l_ref[...] = jnp.broadcast_to(tot, (8, 128))

    row = pl.BlockSpec((tr, D), lambda i: (i, 0))
    return pl.pallas_call(
        body, name=name, grid=(steps,), in_specs=[row, row],
        out_specs=[row, pl.BlockSpec((8, 128), lambda i: (0, 0))],
        out_shape=[jax.ShapeDtypeStruct((S, D), F32), jax.ShapeDtypeStruct((8, 128), F32)],
        scratch_shapes=[pltpu.VMEM((1, D), F32)], compiler_params=_cp(1),
    )(y, t)


STRIP = 64
HALO = 8


def _strips(S, tc):
    return [(r0, slice(l0, l0 + 128)) for l0 in range(0, tc, 128) for r0 in range(S - STRIP, -1, -STRIP)]


def _with_halo(ref, r0, ls):
    if r0 == 0:
        return jnp.concatenate([jnp.zeros((HALO, 128), F32), ref[0:STRIP, ls]], axis=0)
    return ref[r0 - HALO:r0 + STRIP, ls]


def _conv_strip(xw, w_ref, b_ref, ls, width):
    acc = b_ref[:, ls] + w_ref[pl.ds(width - 1, 1), ls] * xw[HALO:]
    shifted = []
    for s in range(1, width):
        xs = pltpu.roll(xw, s, axis=0)[HALO:]
        shifted.append(xs)
        acc = acc + w_ref[pl.ds(width - 1 - s, 1), ls] * xs
    return acc, shifted


def _conv_strip_back(dacc, after, xc, shifted, w_ref, ls, width):
    ext = jnp.concatenate([dacc, after], axis=0)
    dx = w_ref[pl.ds(width - 1, 1), ls] * dacc
    dws = [None] * width
    dws[width - 1] = jnp.sum(dacc * xc, axis=0, keepdims=True)
    for s in range(1, width):
        dx = dx + w_ref[pl.ds(width - 1 - s, 1), ls] * pltpu.roll(ext, STRIP + HALO - s, axis=0)[:STRIP]
        dws[width - 1 - s] = jnp.sum(dacc * shifted[s - 1], axis=0, keepdims=True)
    return dx, dws, jnp.sum(dacc, axis=0, keepdims=True)


def _conv_back_block(S, tc, width, w_ref, b_ref, x_ref, dacc_of, dx_store, dw_ref, db_ref):
    for l0 in range(0, tc, 128):
        ls = slice(l0, l0 + 128)
        after = jnp.zeros((HALO, 128), F32)
        tot = None
        for r0 in range(S - STRIP, -1, -STRIP):
            xw = _with_halo(x_ref, r0, ls)
            acc, shifted = _conv_strip(xw, w_ref, b_ref, ls, width)
            dacc = dacc_of(r0, ls, acc, _sigmoid(acc))
            dx, dws, db = _conv_strip_back(dacc, after, xw[HALO:], shifted, w_ref, ls, width)
            dx_store(r0, ls, dx)
            after = dacc[:HALO]
            part = dws + [db]
            tot = part if tot is None else [p + q for p, q in zip(tot, part)]
        for k in range(width):
            dw_ref[pl.ds(k, 1), ls] = tot[k]
        db_ref[:, ls] = tot[width]


def _conv_silu_fwd(xin, col0, C, w, b, *, name, tc=512):
    S = xin.shape[0]
    width = w.shape[0]
    off = col0 // tc

    def body(x_ref, w_ref, b_ref, o_ref):
        for r0, ls in _strips(S, tc):
            acc, _ = _conv_strip(_with_halo(x_ref, r0, ls), w_ref, b_ref, ls, width)
            o_ref[r0:r0 + STRIP, ls] = acc * _sigmoid(acc)

    return pl.pallas_call(
        body, name=name, grid=(C // tc,),
        in_specs=[pl.BlockSpec((S, tc), lambda j: (0, j + off)), pl.BlockSpec((width, tc), lambda j: (0, j)),
                  pl.BlockSpec((1, tc), lambda j: (0, j))],
        out_specs=pl.BlockSpec((S, tc), lambda j: (0, j)), out_shape=jax.ShapeDtypeStruct((S, C), F32),
        compiler_params=_cp(1),
    )(xin, w, b)


def _conv_silu_bwd(xin, col0, C, w, b, douts, *, name, tc=256):
    S = xin.shape[0]
    width = w.shape[0]
    off = col0 // tc
    nd = len(douts)
    ranges = [(o // tc, (o + d.shape[1]) // tc) for d, o in douts]

    def body(*refs):
        x_ref, w_ref, b_ref = refs[0], refs[1], refs[2]
        d_refs = refs[3:3 + nd]
        dx_ref, dw_ref, db_ref = refs[3 + nd], refs[4 + nd], refs[5 + nd]
        j = pl.program_id(0)

        def dacc_of(r0, ls, acc, sg):
            dout = jnp.zeros((STRIP, 128), F32)
            for q in range(nd):
                lo, hi = ranges[q]
                dout = dout + jnp.where((j >= lo) & (j < hi), d_refs[q][r0:r0 + STRIP, ls], 0.0)
            return dout * (sg * (1.0 + acc * (1.0 - sg)))

        def dx_store(r0, ls, dx):
            dx_ref[r0:r0 + STRIP, ls] = dx.astype(BF16)

        _conv_back_block(S, tc, width, w_ref, b_ref, x_ref, dacc_of, dx_store, dw_ref, db_ref)

    d_specs = [pl.BlockSpec((S, tc), (lambda j, lo=lo, hi=hi: (0, jnp.clip(j - lo, 0, hi - lo - 1)))) for lo, hi in ranges]
    return pl.pallas_call(
        body, name=name, grid=(C // tc,),
        in_specs=[pl.BlockSpec((S, tc), lambda j: (0, j + off)), pl.BlockSpec((width, tc), lambda j: (0, j)),
                  pl.BlockSpec((1, tc), lambda j: (0, j))] + d_specs,
        out_specs=[pl.BlockSpec((S, tc), lambda j: (0, j)), pl.BlockSpec((width, tc), lambda j: (0, j)),
                   pl.BlockSpec((1, tc), lambda j: (0, j))],
        out_shape=[jax.ShapeDtypeStruct((S, C), BF16), jax.ShapeDtypeStruct((width, C), F32),
                   jax.ShapeDtypeStruct((1, C), F32)],
        compiler_params=_cp(1),
    )(xin, w, b, *[d for d, _ in douts])


def _ffn_act_fwd(u, w, b, *, name, tc=256):
    S, F2 = u.shape
    Fd = F2 // 2
    width = w.shape[0]
    nb = Fd // tc

    def body(g_ref, v_ref, w_ref, b_ref, o_ref):
        for r0, ls in _strips(S, tc):
            acc, _ = _conv_strip(_with_halo(g_ref, r0, ls), w_ref, b_ref, ls, width)
            o_ref[r0:r0 + STRIP, ls] = (acc * _sigmoid(acc) * v_ref[r0:r0 + STRIP, ls]).astype(BF16)

    return pl.pallas_call(
        body, name=name, grid=(nb,),
        in_specs=[pl.BlockSpec((S, tc), lambda j: (0, j)), pl.BlockSpec((S, tc), lambda j: (0, j + nb)),
                  pl.BlockSpec((width, tc), lambda j: (0, j)), pl.BlockSpec((1, tc), lambda j: (0, j))],
        out_specs=pl.BlockSpec((S, tc), lambda j: (0, j)), out_shape=jax.ShapeDtypeStruct((S, Fd), BF16),
        compiler_params=_cp(1),
    )(u, u, w, b)


def _ffn_act_bwd(u, w, b, da, *, name, tc=256):
    S, F2 = u.shape
    Fd = F2 // 2
    width = w.shape[0]
    nb = Fd // tc

    def body(g_ref, v_ref, w_ref, b_ref, da_ref, du_ref, dw_ref, db_ref, a_ref):
        def dacc_of(r0, ls, acc, sg):
            rs = slice(r0, r0 + STRIP)
            dav, val, silu = da_ref[rs, ls], v_ref[rs, ls], acc * sg
            a_ref[rs, ls] = (silu * val).astype(BF16)
            du_ref[1, rs, ls] = (dav * silu).astype(BF16)
            return dav * val * (sg * (1.0 + acc * (1.0 - sg)))

        def dx_store(r0, ls, dx):
            du_ref[0, r0:r0 + STRIP, ls] = dx.astype(BF16)

        _conv_back_block(S, tc, width, w_ref, b_ref, g_ref, dacc_of, dx_store, dw_ref, db_ref)

    blk = pl.BlockSpec((S, tc), lambda j: (0, j))
    return pl.pallas_call(
        body, name=name, grid=(nb,),
        in_specs=[blk, pl.BlockSpec((S, tc), lambda j: (0, j + nb)), pl.BlockSpec((width, tc), lambda j: (0, j)),
                  pl.BlockSpec((1, tc), lambda j: (0, j)), blk],
        out_specs=[pl.BlockSpec((2, S, tc), lambda j: (0, 0, j)), pl.BlockSpec((width, tc), lambda j: (0, j)),
                   pl.BlockSpec((1, tc), lambda j: (0, j)), blk],
        out_shape=[jax.ShapeDtypeStruct((2, S, Fd), BF16),
                   jax.ShapeDtypeStruct((width, Fd), F32), jax.ShapeDtypeStruct((1, Fd), F32),
                   jax.ShapeDtypeStruct((S, Fd), BF16)],
        compiler_params=_cp(1),
    )(u, u, w, b, da)


def _ssd_prep(dtr, dt_bias, a_log, *, name="ssd_prep"):
    S = dtr.shape[0]

    def body(d_ref, b_ref, al_ref, dt_ref, ac_ref, sg_ref, act_ref):
        lane = _iota((CHUNK, 128), 1)
        valid = lane < SSM_HEADS
        z = d_ref[...] + b_ref[...]
        dt = jnp.where(valid, jnp.maximum(z, 0.0) + jnp.log(1.0 + jnp.exp(-jnp.abs(z))), 0.0)
        a = dt * (-jnp.exp(al_ref[...]))
        row = _iota((CHUNK, 128), 0)
        k = 1
        while k < CHUNK:
            a = a + jnp.where(row >= k, pltpu.roll(a, k, axis=0), 0.0)
            k *= 2
        sg = jnp.where(valid, _sigmoid(z), 0.0)
        for arr, ref in ((dt, dt_ref), (a, ac_ref), (sg, sg_ref)):
            for g in range(SSM_GROUPS):
                ref[g] = jnp.where(lane < 4, arr if g == 0 else pltpu.roll(arr, 128 - 4 * g, axis=1), 0.0)
        act_ref[...] = a.T[:SSM_HEADS, :]

    blk = pl.BlockSpec((CHUNK, 128), lambda i: (i, 0))
    vec = pl.BlockSpec((1, 128), lambda i: (0, 0))
    grp = pl.BlockSpec((SSM_GROUPS, CHUNK, 128), lambda i: (0, i, 0))
    return pl.pallas_call(
        body, name=name, grid=(S // CHUNK,), in_specs=[blk, vec, vec],
        out_specs=[grp, grp, grp, pl.BlockSpec((SSM_HEADS, CHUNK), lambda i: (0, i))],
        out_shape=[jax.ShapeDtypeStruct((SSM_GROUPS, S, 128), F32)] * 3 + [jax.ShapeDtypeStruct((SSM_HEADS, S), F32)],
        compiler_params=_cp(1),
    )(dtr, dt_bias, a_log)


SSD_GPS = 2


def _expand4(v, lanes):
    out = jnp.broadcast_to(v[:, 3:4], lanes.shape)
    for hh in (2, 1, 0):
        out = jnp.where(lanes < 64 * (hh + 1), v[:, hh:hh + 1], out)
    return out


def _ssd_fwd(xbc, dt_g, ac_g, ac_t, *, name="ssd_fwd"):
    S = xbc.shape[0]
    nc = S // CHUNK
    Lc = CHUNK

    def body(x_ref, b_ref, c_ref, dt_ref, ac_ref, act_ref, y_ref, st_out_ref, st_ref):
        g2 = pl.program_id(0)
        c = pl.program_id(1)

        @pl.when(c == 0)
        def _():
            st_ref[...] = jnp.zeros_like(st_ref)

        causal = _iota((Lc, Lc), 0) >= _iota((Lc, Lc), 1)
        lane256 = _iota((Lc, 256), 1)
        lane128 = _iota((Lc, 128), 1)
        row128 = _iota((128, 128), 0)
        for gg in range(SSD_GPS):
            g = SSD_GPS * g2 + gg
            bv = b_ref[:, 128 * gg:128 * (gg + 1)]
            cbf = c_ref[:, 128 * gg:128 * (gg + 1)].astype(BF16)
            cb = _dot(cbf, bv.astype(BF16), 1, 1)
            dtg, acg = dt_ref[gg], ac_ref[gg]
            ac_last = ac_ref[gg, pl.ds(Lc - 1, 1), :]
            dt4 = _expand4(dtg, lane256)
            ac4 = _expand4(acg, lane256)
            e4 = jnp.exp(ac4)
            xdb = (x_ref[:, 256 * gg:256 * (gg + 1)] * dt4).astype(BF16)
            st_out_ref[gg] = st_ref[gg]
            for p in range(2):
                xd_p = xdb[:, 128 * p:128 * (p + 1)]
                st_p = st_ref[gg, p]
                ys, sn, cds = [], [], []
                for q in range(2):
                    hh = 2 * p + q
                    a_col = acg[:, hh:hh + 1]
                    a_row = act_ref[pl.ds(4 * g + hh, 1), :]
                    dec = jnp.exp(jnp.where(causal, a_col - a_row, NEG))
                    w = (cb * dec).astype(BF16)
                    ys.append(_dot(w, xd_p))
                    al = ac_last[:, hh:hh + 1]
                    dte = jnp.exp(al - a_col)
                    sn.append(_dot(xd_p, (bv * dte).astype(BF16), 0, 0))
                    cds.append(jnp.exp(al))
                y_diag = jnp.where(lane128 < 64, ys[0], ys[1])
                y_off = _dot(cbf, st_p.astype(BF16), 1, 1) * e4[:, 128 * p:128 * (p + 1)]
                y_ref[:, 256 * gg + 128 * p:256 * gg + 128 * (p + 1)] = y_diag + y_off
                st_ref[gg, p] = jnp.where(row128 < 64, st_p * cds[0] + sn[0], st_p * cds[1] + sn[1])

    G = SSD_GPS
    per_g = lambda g, c: (g, c, 0)
    return pl.pallas_call(
        body, name=name, grid=(SSM_GROUPS // G, nc),
        in_specs=[pl.BlockSpec((Lc, 256 * G), lambda g, c: (c, g)),
                  pl.BlockSpec((Lc, 128 * G), lambda g, c: (c, 16 // G + g)),
                  pl.BlockSpec((Lc, 128 * G), lambda g, c: (c, 24 // G + g)),
                  pl.BlockSpec((G, Lc, 128), per_g), pl.BlockSpec((G, Lc, 128), per_g),
                  pl.BlockSpec((SSM_HEADS, Lc), lambda g, c: (0, c))],
        out_specs=[pl.BlockSpec((Lc, 256 * G), lambda g, c: (c, g)),
                   pl.BlockSpec((G, None, 2, 128, 128), lambda g, c: (g, c, 0, 0, 0))],
        out_shape=[jax.ShapeDtypeStruct((S, 2048), F32), jax.ShapeDtypeStruct((SSM_GROUPS, nc, 2, 128, 128), F32)],
        scratch_shapes=[pltpu.VMEM((G, 2, 128, 128), F32)], compiler_params=_cp(2),
    )(xbc, xbc, xbc, dt_g, ac_g, ac_t)


def _ssd_bwd(xbc, dt_g, ac_g, ac_t, states, dy, dexp, *, name="ssd_bwd"):
    S = xbc.shape[0]
    nc = S // CHUNK
    Lc = CHUNK

    def body(x_ref, b_ref, c_ref, dt_ref, ac_ref, act_ref, st_ref, dy_ref, d_ref, dx_ref, db_ref, dc_ref, dh_ref, ds_ref):
        g2 = pl.program_id(0)
        cc = pl.program_id(1)

        @pl.when(cc == 0)
        def _():
            ds_ref[...] = jnp.zeros_like(ds_ref)

        causal = _iota((Lc, Lc), 0) >= _iota((Lc, Lc), 1)
        lane256 = _iota((Lc, 256), 1)
        lane128 = _iota((Lc, 128), 1)
        row128 = _iota((128, 128), 0)
        ind_rows = _iota((256, 128), 0) >> 6
        ind_cols = _iota((256, 128), 1)
        ind_a = (ind_rows == ind_cols).astype(BF16)
        ind_b = (ind_rows + 4 == ind_cols).astype(BF16)
        for gg in range(SSD_GPS):
            g = SSD_GPS * g2 + gg
            bv = b_ref[:, 128 * gg:128 * (gg + 1)]
            cv = c_ref[:, 128 * gg:128 * (gg + 1)]
            bbf, cbf = bv.astype(BF16), cv.astype(BF16)
            cb = _dot(cbf, bbf, 1, 1)
            dtg, acg = dt_ref[gg], ac_ref[gg]
            ac_last = ac_ref[gg, pl.ds(Lc - 1, 1), :]
            dt4 = _expand4(dtg, lane256)
            ac4 = _expand4(acg, lane256)
            acl4 = _expand4(ac_last, _iota((1, 256), 1))
            e4 = jnp.exp(ac4)
            dte4 = jnp.exp(acl4 - ac4)
            xv = x_ref[:, 256 * gg:256 * (gg + 1)]
            xd = xv * dt4
            xdb = xd.astype(BF16)
            dyv = dy_ref[:, 256 * gg:256 * (gg + 1)]
            dcb = jnp.zeros((Lc, Lc), F32)
            dc_acc = jnp.zeros((Lc, 128), F32)
            db_acc = jnp.zeros((Lc, 128), F32)
            u_parts, dxd_parts, ends = [], [], []
            for p in range(2):
                sl = slice(128 * p, 128 * (p + 1))
                xd_p, xdb_p, dy_p = xd[:, sl], xdb[:, sl], dyv[:, sl]
                dyb_p = dy_p.astype(BF16)
                e_p, dte_p = e4[:, sl], dte4[:, sl]
                sp = st_ref[gg, p]
                spb = sp.astype(BF16)
                dsn = ds_ref[gg, p]
                dsnb = dsn.astype(BF16)
                yds, dxds, cds = [], [], []
                for q in range(2):
                    hh = 2 * p + q
                    a_col = acg[:, hh:hh + 1]
                    a_row = act_ref[pl.ds(4 * g + hh, 1), :]
                    dec = jnp.exp(jnp.where(causal, a_col - a_row, NEG))
                    w = (cb * dec).astype(BF16)
                    head = (lane128 < 64) if q == 0 else (lane128 >= 64)
                    dym = jnp.where(head, dyb_p, jnp.zeros_like(dyb_p))
                    dw = _dot(dym, xdb_p, 1, 1)
                    dcb = dcb + dw * dec
                    yds.append(_dot(w, xdb_p))
                    dxds.append(_dot(w, dyb_p, 0, 0))
                    cds.append(jnp.exp(ac_last[:, hh:hh + 1]))
                y_diag = jnp.where(lane128 < 64, yds[0], yds[1])
                dxd_diag = jnp.where(lane128 < 64, dxds[0], dxds[1])
                y_off = _dot(cbf, spb, 1, 1) * e_p
                dgp = dy_p * e_p
                dgb = dgp.astype(BF16)
                dc_acc = dc_acc + _dot(dgb, spb)
                dsp = _dot(dgb, cbf, 0, 0)
                cd_col = jnp.where(row128[:, 0:1] < 64, cds[0], cds[1])
                qm = _dot(bbf, dsnb, 1, 1)
                dxd_state = dte_p * qm
                db_acc = db_acc + _dot((xd_p * dte_p).astype(BF16), dsnb)
                t_p = xd_p * dxd_state
                prod = dsn * sp
                e0 = jnp.sum(jnp.sum(jnp.where(row128 < 64, prod, 0.0), axis=1, keepdims=True), axis=0, keepdims=True)
                e1 = jnp.sum(jnp.sum(jnp.where(row128 >= 64, prod, 0.0), axis=1, keepdims=True), axis=0, keepdims=True)
                tcol = jnp.sum(t_p, axis=0, keepdims=True)
                lane1 = _iota((1, 128), 1)
                t0 = jnp.sum(jnp.where(lane1 < 64, tcol, 0.0), axis=1, keepdims=True)
                t1 = jnp.sum(jnp.where(lane1 >= 64, tcol, 0.0), axis=1, keepdims=True)
                ends.append(e0 * cds[0] + t0)
                ends.append(e1 * cds[1] + t1)
                ds_ref[gg, p] = dsn * cd_col + dsp
                u_parts.append(dyb_p.astype(F32) * y_diag - xdb_p.astype(F32) * dxd_diag + dy_p * y_off - t_p)
                dxd_parts.append(dxd_diag + dxd_state)
            dxd = jnp.concatenate(dxd_parts, axis=1)
            u_all = jnp.concatenate(u_parts, axis=1)
            dx_ref[:, 256 * gg:256 * (gg + 1)] = dxd * dt4 + dyv * d_ref[:, 256 * gg:256 * (gg + 1)]
            dcbb = dcb.astype(BF16)
            dc_ref[:, 128 * gg:128 * (gg + 1)] = dc_acc + _dot(dcbb, bbf)
            db_ref[:, 128 * gg:128 * (gg + 1)] = db_acc + _dot(dcbb, cbf, 0, 0)
            lane = _iota((Lc, 128), 1)
            endv = jnp.zeros((Lc, 128), F32)
            for hh in range(4):
                endv = jnp.where(lane == 8 + hh, ends[hh], endv)
            dh_ref[gg] = _dot3(dxd * xv, ind_a) + _dot3(u_all, ind_b) + endv

    G = SSD_GPS
    rev = lambda c: nc - 1 - c
    per_g = lambda g, c: (g, rev(c), 0)
    return pl.pallas_call(
        body, name=name, grid=(SSM_GROUPS // G, nc),
        in_specs=[pl.BlockSpec((Lc, 256 * G), lambda g, c: (rev(c), g)),
                  pl.BlockSpec((Lc, 128 * G), lambda g, c: (rev(c), 16 // G + g)),
                  pl.BlockSpec((Lc, 128 * G), lambda g, c: (rev(c), 24 // G + g)),
                  pl.BlockSpec((G, Lc, 128), per_g), pl.BlockSpec((G, Lc, 128), per_g),
                  pl.BlockSpec((SSM_HEADS, Lc), lambda g, c: (0, rev(c))),
                  pl.BlockSpec((G, None, 2, 128, 128), lambda g, c: (g, rev(c), 0, 0, 0)),
                  pl.BlockSpec((Lc, 256 * G), lambda g, c: (rev(c), g)),
                  pl.BlockSpec((1, 256 * G), lambda g, c: (0, g))],
        out_specs=[pl.BlockSpec((Lc, 256 * G), lambda g, c: (rev(c), g)),
                   pl.BlockSpec((Lc, 128 * G), lambda g, c: (rev(c), g)),
                   pl.BlockSpec((Lc, 128 * G), lambda g, c: (rev(c), g)),
                   pl.BlockSpec((G, Lc, 128), per_g)],
        out_shape=[jax.ShapeDtypeStruct((S, 2048), F32), jax.ShapeDtypeStruct((S, 1024), F32),
                   jax.ShapeDtypeStruct((S, 1024), F32), jax.ShapeDtypeStruct((SSM_GROUPS, S, 128), F32)],
        scratch_shapes=[pltpu.VMEM((G, 2, 128, 128), F32)], compiler_params=_cp(2),
    )(xbc, xbc, xbc, dt_g, ac_g, ac_t, states, dy, dexp)


def _ssd_post(dhead, dt_g, sg_g, alog_g, *, name="ssd_post"):
    S = dhead.shape[1]
    nc = S // CHUNK
    Lc = CHUNK

    def body(dh_ref, dt_ref, sg_ref, al_ref, o_ref, s_ref):
        @pl.when(pl.program_id(0) == 0)
        def _():
            s_ref[...] = jnp.zeros_like(s_ref)

        lane = _iota((Lc, 128), 1)
        row = _iota((Lc, 128), 0)
        row8 = _iota((8, 128), 0)
        out = jnp.zeros((Lc, 128), F32)
        for g in range(SSM_GROUPS):
            dh = dh_ref[g]
            a_neg = -jnp.exp(al_ref[g])
            dac = jnp.where(lane < 4, pltpu.roll(dh, 124, axis=1), 0.0)
            end = jnp.where(lane < 4, pltpu.roll(dh, 120, axis=1), 0.0)
            k = 1
            while k < Lc:
                dac = dac + jnp.where(row < Lc - k, pltpu.roll(dac, Lc - k, axis=0), 0.0)
                k *= 2
            da = dac + end
            ddt = jnp.where(lane < 4, da * a_neg + dh, 0.0)
            ddtr = ddt * sg_ref[g]
            out = out + (ddtr if g == 0 else pltpu.roll(ddtr, 4 * g, axis=1))
            dal = jnp.sum(da * dt_ref[g], axis=0, keepdims=True) * a_neg
            dbias = jnp.sum(ddtr, axis=0, keepdims=True)
            part = jnp.where(row8 == 0, dal, jnp.where(row8 == 1, dbias, 0.0))
            s_ref[g] += part
        o_ref[...] = out.astype(BF16)

    grp = pl.BlockSpec((SSM_GROUPS, Lc, 128), lambda c: (0, c, 0))
    whole = lambda r: pl.BlockSpec((SSM_GROUPS, r, 128), lambda c: (0, 0, 0))
    return pl.pallas_call(
        body, name=name, grid=(nc,), in_specs=[grp, grp, grp, whole(1)],
        out_specs=[pl.BlockSpec((Lc, 128), lambda c: (c, 0)), whole(8)],
        out_shape=[jax.ShapeDtypeStruct((S, 128), BF16), jax.ShapeDtypeStruct((SSM_GROUPS, 8, 128), F32)],
        compiler_params=_cp(1),
    )(dhead, dt_g, sg_g, alog_g)


def _gate_fwd(y, xbc, zx, dexp, gn, *, name="gate_fwd", tr=256):
    S = y.shape[0]
    W = 2048
    gw = W // SSM_GROUPS

    def body(y_ref, x_ref, z_ref, d_ref, g_ref, o_ref):
        z = z_ref[...]
        u = (y_ref[...] + x_ref[...] * d_ref[...]) * (z * _sigmoid(z))
        gv = g_ref[...]
        for q in range(SSM_GROUPS):
            sl = slice(gw * q, gw * (q + 1))
            uq = u[:, sl]
            r = lax.rsqrt(jnp.mean(uq * uq, axis=-1, keepdims=True) + EPS)
            o_ref[:, sl] = (uq * r * gv[:, sl]).astype(BF16)

    row = pl.BlockSpec((tr, W), lambda i: (i, 0))
    vec = pl.BlockSpec((1, W), lambda i: (0, 0))
    return pl.pallas_call(
        body, name=name, grid=(S // tr,), in_specs=[row, row, row, vec, vec], out_specs=row,
        out_shape=jax.ShapeDtypeStruct((S, W), BF16), compiler_params=_cp(1),
    )(y, xbc, zx, dexp, gn)


def _gate_bwd(y, xbc, zx, dexp, gn, dout, *, name="gate_bwd", tr=256):
    S = y.shape[0]
    W = 2048
    gw = W // SSM_GROUPS
    steps = S // tr

    def body(y_ref, x_ref, z_ref, d_ref, g_ref, do_ref, dy_ref, dz_ref, dg_ref, dd_ref, acc_ref):
        i = pl.program_id(0)

        @pl.when(i == 0)
        def _():
            acc_ref[...] = jnp.zeros_like(acc_ref)

        z = z_ref[...]
        sg = _sigmoid(z)
        sz = z * sg
        xs = x_ref[...]
        yt = y_ref[...] + xs * d_ref[...]
        u = yt * sz
        gv = g_ref[...]
        do = do_ref[...]
        dgs = []
        for q in range(SSM_GROUPS):
            sl = slice(gw * q, gw * (q + 1))
            uq = u[:, sl]
            r = lax.rsqrt(jnp.mean(uq * uq, axis=-1, keepdims=True) + EPS)
            uh = uq * r
            dq = do[:, sl]
            duh = dq * gv[:, sl]
            duq = r * (duh - uh * jnp.mean(duh * uh, axis=-1, keepdims=True))
            dgs.append(jnp.sum(dq * uh, axis=0, keepdims=True))
            dyt = duq * sz[:, sl]
            dy_ref[:, sl] = dyt
            dz_ref[:, sl] = (duq * yt[:, sl] * (sg[:, sl] * (1.0 + z[:, sl] * (1.0 - sg[:, sl])))).astype(BF16)
            acc_ref[:, sl] += jnp.sum(dyt * xs[:, sl], axis=0, keepdims=True)
        dg = jnp.concatenate(dgs, axis=1)

        @pl.when(i == 0)
        def _():
            dg_ref[...] = dg

        @pl.when(i > 0)
        def _():
            dg_ref[...] += dg

        @pl.when(i == steps - 1)
        def _():
            ind = ((_iota((W, 128), 0) >> 6) == _iota((W, 128), 1)).astype(BF16)
            dd_ref[...] = _dot3(jnp.broadcast_to(acc_ref[...], (8, W)), ind)[0:1, :]

    row = pl.BlockSpec((tr, W), lambda i: (i, 0))
    vec = pl.BlockSpec((1, W), lambda i: (0, 0))
    return pl.pallas_call(
        body, name=name, grid=(steps,), in_specs=[row, row, row, vec, vec, row],
        out_specs=[row, row, vec, pl.BlockSpec((1, 128), lambda i: (0, 0))],
        out_shape=[jax.ShapeDtypeStruct((S, W), F32), jax.ShapeDtypeStruct((S, W), BF16),
                   jax.ShapeDtypeStruct((1, W), F32), jax.ShapeDtypeStruct((1, 128), F32)],
        scratch_shapes=[pltpu.VMEM((1, W), F32)], compiler_params=_cp(1),
    )(y, xbc, zx, dexp, gn, dout)


def _rope_cs(posf, *, name="rope_tables", tr=256):
    S = posf.shape[0]

    def body(p_ref, c_ref, s_ref):
        j = (_iota((tr, 128), 1) & 31).astype(F32)
        ang = p_ref[...] * jnp.exp(j * (-math.log(ROPE_THETA) / 32.0))
        c_ref[...] = jnp.cos(ang)
        s_ref[...] = jnp.sin(ang)

    blk = pl.BlockSpec((tr, 128), lambda i: (i, 0))
    return pl.pallas_call(
        body, name=name, grid=(S // tr,), in_specs=[pl.BlockSpec((tr, 1), lambda i: (i, 0))], out_specs=[blk, blk],
        out_shape=[jax.ShapeDtypeStruct((S, 128), F32)] * 2, compiler_params=_cp(1),
    )(posf)


def _rope_tables(c_ref, s_ref, shape):
    reps = shape[1] // 128
    return jnp.tile(c_ref[...], (1, reps)), jnp.tile(s_ref[...], (1, reps)), (_iota(shape, 1) & 63) < 32


def _hn_inds(W):
    ind = ((_iota((W, 128), 0) >> 6) == _iota((W, 128), 1)).astype(BF16)
    ind_t = ((_iota((128, W), 1) >> 6) == _iota((128, W), 0)).astype(BF16)
    return ind, ind_t


def _hnrope_fwd(xin, col0, W, gain_w, rope, *, name, tr=256):
    S = xin.shape[0]
    off = col0 // W
    nh = W // HEAD

    def body(x_ref, g_ref, c_ref, s_ref, o_ref):
        x = x_ref[...]
        ind, ind_t = _hn_inds(W)
        r = lax.rsqrt(_dot3(x * x, ind) * (1.0 / HEAD) + EPS)
        xn = x * _dot3(r, ind_t) * g_ref[...]
        cs, sn, half = _rope_tables(c_ref, s_ref, (tr, W))
        rot = jnp.where(half, -pltpu.roll(xn, W - 32, axis=1), pltpu.roll(xn, 32, axis=1))
        out = (xn * cs + rot * sn).astype(BF16)
        for h in range(nh):
            o_ref[h] = out[:, HEAD * h:HEAD * (h + 1)]

    tab = pl.BlockSpec((tr, 128), lambda i: (i, 0))
    return pl.pallas_call(
        body, name=name, grid=(S // tr,),
        in_specs=[pl.BlockSpec((tr, W), lambda i: (i, off)), pl.BlockSpec((1, W), lambda i: (0, 0)), tab, tab],
        out_specs=pl.BlockSpec((nh, tr, HEAD), lambda i: (0, i, 0)), out_shape=jax.ShapeDtypeStruct((nh, S, HEAD), BF16),
        compiler_params=_cp(1),
    )(xin, gain_w, *rope)


def _hnrope_bwd(xin, col0, W, gain_w, rope, dout, *, name, tr=256):
    S = xin.shape[0]
    off = col0 // W
    steps = S // tr
    nh = W // HEAD

    def body(x_ref, g_ref, c_ref, s_ref, do_ref, dx_ref, cs_ref, dg_ref, acc_ref):
        i = pl.program_id(0)
        x = x_ref[...]
        ind, ind_t = _hn_inds(W)
        r = lax.rsqrt(_dot3(x * x, ind) * (1.0 / HEAD) + EPS)
        rw = _dot3(r, ind_t)
        xh = x * rw
        cs, sn, half = _rope_tables(c_ref, s_ref, (tr, W))
        do = jnp.concatenate([do_ref[h] for h in range(nh)], axis=1).astype(F32)
        gs = do * sn
        g1 = do * cs + jnp.where(half, pltpu.roll(gs, W - 32, axis=1), -pltpu.roll(gs, 32, axis=1))
        dxh = g1 * g_ref[...]
        t = _dot3(dxh * xh, ind) * (1.0 / HEAD)
        dx = rw * (dxh - xh * _dot3(t, ind_t))
        dx_ref[...] = dx.astype(BF16)
        cpart = jnp.sum(dx, axis=0, keepdims=True)
        gpart = jnp.sum(g1 * xh, axis=0, keepdims=True)

        @pl.when(i == 0)
        def _():
            cs_ref[...] = cpart
            acc_ref[...] = gpart

        @pl.when(i > 0)
        def _():
            cs_ref[...] += cpart
            acc_ref[...] += gpart

        @pl.when(i == steps - 1)
        def _():
            fold = ((_iota((W, 128), 0) & 63) == _iota((W, 128), 1)).astype(BF16)
            dg_ref[...] = _dot3(jnp.broadcast_to(acc_ref[...], (8, W)), fold)[0:1, :]

    tab = pl.BlockSpec((tr, 128), lambda i: (i, 0))
    return pl.pallas_call(
        body, name=name, grid=(steps,),
        in_specs=[pl.BlockSpec((tr, W), lambda i: (i, off)), pl.BlockSpec((1, W), lambda i: (0, 0)), tab, tab,
                  pl.BlockSpec((nh, tr, HEAD), lambda i: (0, i, 0))],
        out_specs=[pl.BlockSpec((tr, W), lambda i: (i, 0)), pl.BlockSpec((1, W), lambda i: (0, 0)),
                   pl.BlockSpec((1, 128), lambda i: (0, 0))],
        out_shape=[jax.ShapeDtypeStruct((S, W), BF16), jax.ShapeDtypeStruct((1, W), F32),
                   jax.ShapeDtypeStruct((1, 128), F32)],
        scratch_shapes=[pltpu.VMEM((1, W), F32)], compiler_params=_cp(1),
    )(xin, gain_w, *rope, dout)


def _attn_band():
    qi = jnp.arange(ATT_G * WINDOW)[:, None] % WINDOW
    ki = jnp.arange(2 * WINDOW)[None, :]
    rel = qi + WINDOW - ki
    ok = (rel >= 0) & (rel < WINDOW)
    return jnp.stack([jnp.where(ok & (ki >= WINDOW), 0.0, NEG), jnp.where(ok, 0.0, NEG)]).astype(F32)


def _attn_probs(q, kb, sink_ref, band_ref, h, i):
    s = _dot(q, kb, 1, 1) * (HEAD ** -0.5) + band_ref[jnp.minimum(i, 1)]
    r1 = _iota((4 * WINDOW, 1), 0)
    sink = jnp.where(r1 < WINDOW, sink_ref[4 * h], jnp.where(r1 < 2 * WINDOW, sink_ref[4 * h + 1],
                     jnp.where(r1 < 3 * WINDOW, sink_ref[4 * h + 2], sink_ref[4 * h + 3])))
    m = jnp.maximum(jnp.max(s, axis=1, keepdims=True), sink)
    p = jnp.exp(s - m)
    ps = jnp.exp(sink - m)
    inv = 1.0 / (jnp.sum(p, axis=1, keepdims=True) + ps)
    return p * inv, ps * inv


ATT_HPS = 2
_BAND = pl.BlockSpec((2, ATT_G * WINDOW, 2 * WINDOW), lambda h, i: (0, 0, 0))


def _attn_specs(S):
    qspec = pl.BlockSpec((ATT_HPS, ATT_G, WINDOW, HEAD), lambda h, i: (h, 0, i, 0))
    cur = pl.BlockSpec((ATT_HPS, WINDOW, HEAD), lambda h, i: (h, i, 0))
    prev = pl.BlockSpec((ATT_HPS, WINDOW, HEAD), lambda h, i: (h, jnp.maximum(i - 1, 0), 0))
    tok = pl.BlockSpec((WINDOW, ATT_HPS * ATT_G * HEAD), lambda h, i: (i, h))
    return qspec, cur, prev, tok


def _attn_fwd(qh, kh, vh, sinks, *, name="attn_fwd"):
    S = kh.shape[1]
    nb = S // WINDOW

    def body(s_ref, band_ref, q_ref, kc_ref, kp_ref, vc_ref, vp_ref, o_ref):
        h2, i = pl.program_id(0), pl.program_id(1)
        outs = []
        for hh in range(ATT_HPS):
            q = q_ref[hh].reshape(ATT_G * WINDOW, HEAD)
            kb = jnp.concatenate([kp_ref[hh], kc_ref[hh]], axis=0)
            vb = jnp.concatenate([vp_ref[hh], vc_ref[hh]], axis=0)
            probs, _ = _attn_probs(q, kb, s_ref, band_ref, ATT_HPS * h2 + hh, i)
            o = _dot(probs.astype(BF16), vb).astype(BF16)
            outs += [o[WINDOW * g:WINDOW * (g + 1)] for g in range(ATT_G)]
        o_ref[...] = jnp.concatenate(outs, axis=1)

    qspec, cur, prev, tok = _attn_specs(S)
    return pl.pallas_call(
        body, name=name, grid=(ATT_KV // ATT_HPS, nb),
        in_specs=[pl.BlockSpec(memory_space=pltpu.SMEM), _BAND, qspec, cur, prev, cur, prev], out_specs=tok,
        out_shape=jax.ShapeDtypeStruct((S, ATT_KV * ATT_G * HEAD), BF16), compiler_params=_cp(2),
    )(sinks, _attn_band(), qh, kh, kh, vh, vh)


def _attn_bwd(qh, kh, vh, sinks, doh, *, name="attn_bwd"):
    S = kh.shape[1]
    nb = S // WINDOW

    def body(s_ref, band_ref, q_ref, kc_ref, kp_ref, vc_ref, vp_ref, do_ref, dq_ref, dk_ref, dv_ref, dsk_ref):
        h2, i = pl.program_id(0), pl.program_id(1)

        @pl.when(i == 0)
        def _():
            dk_ref[...] = jnp.zeros_like(dk_ref)
            dv_ref[...] = jnp.zeros_like(dv_ref)
            dsk_ref[...] = jnp.zeros_like(dsk_ref)

        dov = do_ref[...]
        cur = pl.multiple_of(i * WINDOW, WINDOW)
        lane = _iota((8, 128), 1)
        row = _iota((8, 128), 0)
        scale = HEAD ** -0.5
        for hh in range(ATT_HPS):
            q = q_ref[hh].reshape(ATT_G * WINDOW, HEAD)
            do = jnp.concatenate([dov[:, HEAD * (ATT_G * hh + g):HEAD * (ATT_G * hh + g + 1)] for g in range(ATT_G)], axis=0)
            kb = jnp.concatenate([kp_ref[hh], kc_ref[hh]], axis=0)
            vb = jnp.concatenate([vp_ref[hh], vc_ref[hh]], axis=0)
            probs, psink = _attn_probs(q, kb, s_ref, band_ref, ATT_HPS * h2 + hh, i)
            dp = _dot(do, vb, 1, 1)
            delta = jnp.sum(probs * dp, axis=1, keepdims=True)
            ds = (probs * (dp - delta)).astype(BF16)
            dq_ref[hh] = (_dot(ds, kb) * scale).reshape(ATT_G, WINDOW, HEAD)
            dkb = _dot(ds, q, 0, 0) * scale
            dvb = _dot(probs.astype(BF16), do, 0, 0)
            dk_ref[hh, pl.ds(cur, WINDOW), :] += dkb[WINDOW:, :]
            dv_ref[hh, pl.ds(cur, WINDOW), :] += dvb[WINDOW:, :]

            @pl.when(i > 0)
            def _():
                prv = pl.multiple_of((i - 1) * WINDOW, WINDOW)
                dk_ref[hh, pl.ds(prv, WINDOW), :] += dkb[:WINDOW, :]
                dv_ref[hh, pl.ds(prv, WINDOW), :] += dvb[:WINDOW, :]

            dsr = -psink * delta
            upd = jnp.zeros((8, 128), F32)
            for gq in range(ATT_G):
                v = jnp.sum(dsr[gq * WINDOW:(gq + 1) * WINDOW, :], axis=0, keepdims=True)
                upd = jnp.where((lane == gq) & (row == 0), v, upd)
            dsk_ref[hh] += upd

    qspec, cur, prev, tok = _attn_specs(S)
    full = pl.BlockSpec((ATT_HPS, S, HEAD), lambda h, i: (h, 0, 0))
    return pl.pallas_call(
        body, name=name, grid=(ATT_KV // ATT_HPS, nb),
        in_specs=[pl.BlockSpec(memory_space=pltpu.SMEM), _BAND, qspec, cur, prev, cur, prev, tok],
        out_specs=[qspec, full, full, pl.BlockSpec((ATT_HPS, 8, 128), lambda h, i: (h, 0, 0))],
        out_shape=[jax.ShapeDtypeStruct((ATT_KV, ATT_G, S, HEAD), F32), jax.ShapeDtypeStruct((ATT_KV, S, HEAD), F32),
                   jax.ShapeDtypeStruct((ATT_KV, S, HEAD), F32), jax.ShapeDtypeStruct((ATT_KV, 8, 128), F32)],
        compiler_params=_cp(2),
    )(sinks, _attn_band(), qh, kh, kh, vh, vh, doh)


def _heads_major(t, nh):
    S = t.shape[0]
    return t.reshape(S, nh, HEAD).transpose(1, 0, 2)


def _tokens_major(t):
    nh, S, _ = t.shape
    return t.transpose(1, 0, 2).reshape(S, nh * HEAD)


class _NoComm:
    def late_weights(self, w, after):
        return w

    def grads(self, group, tensors, after):
        return None


def _local_step(x, posf, target, w, comm=None):
    S, D = x.shape
    gr = {}
    comm = comm or _NoComm()

    (h1,) = _rms_fwd(x, [w["a_norm"]], name="a_norm_f", dep=w.get("dep"))
    zx = _mm(h1, w["w_zx"], name="in_proj_zx")
    dtr = _mm(h1, w["w_dt"], name="in_proj_dt")
    xbc = _conv_silu_fwd(zx, 2048, 4096, w["a_conv_w"], w["a_conv_b"], name="a_conv_f")
    dt_g, ac_g, sg_g, ac_t = _ssd_prep(dtr, w["a_dt_bias"], w["a_A_log"])
    y_ssd, states = _ssd_fwd(xbc, dt_g, ac_g, ac_t)
    yg = _gate_fwd(y_ssd, xbc, zx, w["a_Dexp"], w["a_gnorm"])
    x1 = _mm(yg, w["a_out_proj"], res=x, name="out_proj")

    w = comm.late_weights(w, x1)
    FW = w["f_w_in"][0].shape[2]

    def ffn_fwd(xin, l):
        (h,) = _rms_fwd(xin, [w["f_norm"][l]], name=f"f_norm_f{l}")
        u = _mm(h, w["f_w_in"][l], name=f"f_in{l}", dims=(S, N_CHIPS * FW, D), tn=FW,
                b_spec=pl.BlockSpec((None, D, FW), lambda i, j, k: (j, 0, 0)))
        a = _ffn_act_fwd(u, w["f_conv_w"][l], w["f_conv_b"][l], name=f"f_act_f{l}")
        xo = _mm(a, w["f_w_down"][l], res=xin, tk=a.shape[1], name=f"f_down{l}")
        return xo, (h, u)

    x2, ffn0 = ffn_fwd(x1, 0)

    hk, hq = _rms_fwd(x2, [w["kv_norm"], w["b_norm"]], name="kvq_norm_f")
    kv = _mm(hk, w["w_kv"], bias=w["b_kv"], name="kv_proj")
    q = _mm(hq, w["w_q"], bias=w["b_q"], name="q_proj")
    rope = _rope_cs(posf)
    kr = _hnrope_fwd(kv, 0, 256, w["k_norm_w"], rope, name="k_rope_f")
    qr = _hnrope_fwd(q, 0, 1024, w["q_norm_w"], rope, name="q_rope_f")
    qh = qr.reshape(ATT_KV, ATT_G, S, HEAD)
    kh = kr
    vh = _heads_major(kv[:, 256:].astype(BF16), ATT_KV)
    att = _attn_fwd(qh, kh, vh, w["sinks"])
    x3 = _mm(att, w["w_o"], bias=w["b_o"], res=x2, name="o_proj")
    x4, ffn1 = ffn_fwd(x3, 1)

    dy, loss_part = _loss(x4, target)

    def ffn_bwd(xin, l, saved, dyo, want_colsum, dep=None):
        h, u = saved
        da = _mm(dyo, w["f_w_down"][l], tb=True, name=f"f_down_dx{l}", dep=dep)
        du, dcw, dcb, a = _ffn_act_bwd(u, w["f_conv_w"][l], w["f_conv_b"][l], da, name=f"f_act_b{l}")
        dw_down = _mm(a, dyo, ta=True, out_dtype=BF16, name=f"f_down_dw{l}")
        dw_in = _mm(h, du, ta=True, out_dtype=BF16, name=f"f_in_dw{l}", dims=(D, N_CHIPS * FW, S), tm=D, tn=FW, tk=S,
                    b_spec=pl.BlockSpec((None, S, FW), lambda i, j, k: (j // 2, 0, j % 2)),
                    o_spec=pl.BlockSpec((None, D, FW), lambda i, j, k: (j, i, 0)), o_shape=(N_CHIPS, D, FW))
        ts = _pick(S, (1024, 512, 256))
        pieces = [(pl.BlockSpec((None, ts, FW), lambda i, j, k, q=q: (q // 2, i, q % 2)),
                   pl.BlockSpec((None, 512, FW), lambda i, j, k, q=q: (q, j, 0))) for q in range(N_CHIPS)]
        dh = _mm(du, w["f_w_in"][l], tb=True, name=f"f_in_dx{l}", dims=(S, D, FW), tm=ts, tn=512, tk=FW,
                 a_spec=pieces[0][0], b_spec=pieces[0][1], more=pieces[1:])
        outs = _rms_bwd(xin, [w["f_norm"][l]], [dh], dyo, name=f"f_norm_b{l}", want_colsum=want_colsum)
        g = dict(f_norm=outs[1], f_w_in=dw_in, f_conv_w=dcw, f_conv_b=dcb, f_w_down=dw_down)
        return outs[0], g, (outs[2] if want_colsum else None)

    dx3, gr["ffn1"], db_o = ffn_bwd(x3, 1, ffn1, dy, True)
    gr["b_o"] = db_o
    gr["w_o"] = _mm(att, dx3, ta=True, out_dtype=BF16, name="o_proj_dw")
    datt = _mm(dx3, w["w_o"], tb=True, out_dtype=BF16, name="o_proj_dx")
    dqh, dkh, dvh, dsk = _attn_bwd(qh, kh, vh, w["sinks"], datt)
    gr["sinks"] = dsk[:, 0, :4].reshape(1, 16)
    dv = _tokens_major(dvh).astype(BF16)
    dq, db_q, dqn = _hnrope_bwd(q, 0, 1024, w["q_norm_w"], rope, dqh.reshape(16, S, HEAD), name="q_rope_b")
    dk, db_k, dkn = _hnrope_bwd(kv, 0, 256, w["k_norm_w"], rope, dkh, name="k_rope_b")
    gr["q_norm"], gr["k_norm"] = dqn[:, :HEAD], dkn[:, :HEAD]
    gr["b_q"] = db_q
    gr["b_kv"] = jnp.concatenate([db_k, _colsum(dv, name="dv_colsum")], axis=1)
    dkv = jnp.concatenate([dk, dv], axis=1)
    gr["w_q"] = _mm(hq, dq, ta=True, out_dtype=BF16, name="q_proj_dw")
    gr["w_kv"] = _mm(hk, dkv, ta=True, out_dtype=BF16, name="kv_proj_dw")
    tok = comm.grads(1, dict(f_down1=gr["ffn1"]["f_w_down"], f_in1=gr["ffn1"]["f_w_in"], w_o=gr["w_o"], w_q=gr["w_q"],
                             w_kv=gr["w_kv"]), None)
    dhq = _mm(dq, w["w_q"], tb=True, name="q_proj_dx", dep=tok)
    dhk = _mm(dkv, w["w_kv"], tb=True, name="kv_proj_dx")
    dx2, gr["kv_norm"], gr["b_norm"] = _rms_bwd(x2, [w["kv_norm"], w["b_norm"]], [dhk, dhq], dx3, name="kvq_norm_b")

    dx1, gr["ffn0"], _ = ffn_bwd(x1, 0, ffn0, dx2, False)

    tok = comm.grads(2, dict(f_down0=gr["ffn0"]["f_w_down"], f_in0=gr["ffn0"]["f_w_in"]), dx1)
    gr["a_out_proj"] = _mm(yg, dx1, ta=True, out_dtype=BF16, name="out_proj_dw")
    dyg = _mm(dx1, w["a_out_proj"], tb=True, name="out_proj_dx", dep=tok)
    dy_ssd, dz, gr["a_gnorm"], dD = _gate_bwd(y_ssd, xbc, zx, w["a_Dexp"], w["a_gnorm"], dyg)
    gr["a_D"] = dD[:, :SSM_HEADS]
    dxs, dB, dC, dhead = _ssd_bwd(xbc, dt_g, ac_g, ac_t, states, dy_ssd, w["a_Dexp"])
    ddtr, dsmall = _ssd_post(dhead, dt_g, sg_g, w["a_A_log_g"])
    gr["a_A_log"] = dsmall[:, 0, :4].reshape(1, SSM_HEADS)
    gr["a_dt_bias"] = dsmall[:, 1, :4].reshape(1, SSM_HEADS)
    dxbc, gr["a_conv_w"], gr["a_conv_b"] = _conv_silu_bwd(
        zx, 2048, 4096, w["a_conv_w"], w["a_conv_b"], [(dxs, 0), (dB, 2048), (dC, 3072)], name="a_conv_b")
    gr["w_z"] = _mm(h1, dz, ta=True, out_dtype=BF16, name="in_proj_dwz")
    gr["w_x"] = _mm(h1, dxbc, ta=True, out_dtype=BF16, name="in_proj_dwx")
    gr["w_dt"] = _mm(h1, ddtr, ta=True, out_dtype=BF16, name="in_proj_dwdt")
    dh1 = _mm(dz, w["w_zx"], tb=True, name="in_proj_dxz")
    ts = _pick(S, (1024, 512, 256))
    halves = [(pl.BlockSpec((ts, 2048), lambda i, j, k, q=q: (i, q)), pl.BlockSpec((512, 2048), lambda i, j, k, q=q: (j, 1 + q)))
              for q in range(2)]
    dh1 = _mm(dxbc, w["w_zx"], tb=True, res=dh1, name="in_proj_dxx", dims=(S, D, 2048), tm=ts, tn=512, tk=2048,
              a_spec=halves[0][0], b_spec=halves[0][1], more=halves[1:])
    dh1 = _mm(ddtr, w["w_dt"], tb=True, res=dh1, name="in_proj_dxdt")
    dx0, gr["a_norm"] = _rms_bwd(x, [w["a_norm"]], [dh1], dx1, name="a_norm_b")
    comm.grads(3, dict(out_proj=gr["a_out_proj"], in_proj=_in_proj_grad(gr).reshape(D, N_CHIPS, -1).transpose(1, 0, 2)), dx0)
    return loss_part, dx0, gr


def _prep_small(full, w):
    w["a_norm"] = full["a_norm"]
    w["a_conv_w"] = full["a_conv_w"][0]
    w["a_conv_b"] = full["a_conv_b"]
    pad32 = lambda v: jnp.pad(v, ((0, 0), (0, 128 - SSM_HEADS)))
    w["a_dt_bias"] = pad32(full["a_dt_bias"])
    w["a_A_log"] = pad32(full["a_A_log"])
    w["a_A_log_g"] = jnp.pad(full["a_A_log"].reshape(SSM_GROUPS, 1, 4), ((0, 0), (0, 0), (0, 124)))
    w["a_Dexp"] = jnp.repeat(full["a_D"], HEAD, axis=1)
    w["a_gnorm"] = full["a_gnorm"]
    w["f_norm"] = [full["f_norm"][l:l + 1] for l in range(2)]
    w["f_conv_w"] = [full["f_conv_w"][l] for l in range(2)]
    w["f_conv_b"] = [full["f_conv_b"][l:l + 1] for l in range(2)]
    w["kv_norm"] = full["kv_norm"].reshape(1, -1)
    w["b_kv"] = full["b_kv"].reshape(1, -1)
    w["k_norm_w"] = jnp.tile(full["k_norm"].reshape(1, HEAD), (1, ATT_KV))
    w["b_norm"] = full["b_norm"]
    w["b_q"] = full["b_q"]
    w["q_norm_w"] = jnp.tile(full["q_norm"], (1, ATT_KV * ATT_G))
    w["sinks"] = full["sinks"].reshape(-1)
    w["b_o"] = full["b_o"]
    return w


def _split_in_proj(ip):
    return ip[:, :6144].astype(BF16), jnp.pad(ip[:, 6144:], ((0, 0), (0, 128 - SSM_HEADS))).astype(BF16)


def _prep_weights(full):
    w = _prep_small(full, {})
    w["w_zx"], w["w_dt"] = _split_in_proj(full["a_in_proj"][0])
    w["a_out_proj"] = full["a_out_proj"][0].astype(BF16)
    w["f_w_in"] = [full["f_w_in"][l].reshape(1024, N_CHIPS, -1).transpose(1, 0, 2).astype(BF16) for l in range(2)]
    w["f_w_down"] = [full["f_w_down"][l].astype(BF16) for l in range(2)]
    w["w_kv"] = full["w_kv"].astype(BF16)
    w["w_q"] = full["w_q"][0].astype(BF16)
    w["w_o"] = full["w_o"][0].astype(BF16)
    return w


def _small_grads(gr):
    g = {}
    g["a_norm"] = gr["a_norm"]
    g["a_conv_w"] = gr["a_conv_w"][None]
    g["a_conv_b"] = gr["a_conv_b"]
    g["a_dt_bias"], g["a_A_log"], g["a_D"] = gr["a_dt_bias"], gr["a_A_log"], gr["a_D"]
    g["a_gnorm"] = gr["a_gnorm"]
    g["kv_norm"] = gr["kv_norm"].reshape(-1)
    g["b_kv"] = gr["b_kv"].reshape(-1)
    g["k_norm"] = gr["k_norm"].reshape(-1)
    g["b_norm"] = gr["b_norm"]
    g["b_q"] = gr["b_q"]
    g["q_norm"] = gr["q_norm"]
    g["sinks"] = gr["sinks"]
    g["b_o"] = gr["b_o"]
    f = [gr["ffn0"], gr["ffn1"]]
    g["f_norm"] = jnp.concatenate([f[0]["f_norm"], f[1]["f_norm"]], axis=0)
    g["f_conv_w"] = jnp.stack([f[l]["f_conv_w"] for l in range(2)])
    g["f_conv_b"] = jnp.concatenate([f[l]["f_conv_b"] for l in range(2)], axis=0)
    return g


def _in_proj_grad(gr):
    return jnp.concatenate([gr["w_z"], gr["w_x"], gr["w_dt"][:, :SSM_HEADS]], axis=1)


def _full_grads(gr):
    g = _small_grads(gr)
    f32 = lambda t: t.astype(F32)
    g["a_in_proj"] = f32(_in_proj_grad(gr))[None]
    g["a_out_proj"] = f32(gr["a_out_proj"])[None]
    g["w_kv"] = f32(gr["w_kv"])
    g["w_q"] = f32(gr["w_q"])[None]
    g["w_o"] = f32(gr["w_o"])[None]
    f = [gr["ffn0"], gr["ffn1"]]
    g["f_w_in"] = jnp.stack([f32(f[l]["f_w_in"]).transpose(1, 0, 2).reshape(1024, -1) for l in range(2)])
    g["f_w_down"] = jnp.stack([f32(f[l]["f_w_down"]) for l in range(2)])
    return g


MESH = pl.DeviceIdType.MESH
WEIGHTS = ("a_norm", "a_in_proj", "a_conv_w", "a_conv_b", "a_dt_bias", "a_A_log", "a_D", "a_gnorm", "a_out_proj",
           "kv_norm", "w_kv", "b_kv", "k_norm", "b_norm", "w_q", "b_q", "q_norm", "sinks", "w_o", "b_o", "f_norm",
           "f_w_in", "f_conv_w", "f_conv_b", "f_w_down")
MATS = (("in_proj", "a_in_proj", 0), ("out_proj", "a_out_proj", 0), ("w_kv", "w_kv", None), ("w_q", "w_q", 0),
        ("w_o", "w_o", 0), ("f_in0", "f_w_in", 0), ("f_in1", "f_w_in", 1), ("f_down0", "f_w_down", 0),
        ("f_down1", "f_w_down", 1))
SMALL_CUT = (("a_norm", 1), ("a_conv_w", 2), ("a_conv_b", 1), ("a_gnorm", 1), ("f_conv_w", 2))
SMALL_REP = ("a_dt_bias", "a_A_log", "a_D", "kv_norm", "b_kv", "k_norm", "b_norm", "b_q", "q_norm", "sinks", "b_o",
             "f_norm", "f_conv_b")


def _coords():
    return lax.axis_index("x"), lax.axis_index("y"), lax.axis_index("c")


def _other_chips(x, y):
    return [(1 - x, y), (x, 1 - y), (1 - x, 1 - y)]


def _pack(arrs, rows_align, lanes, dtype):
    flat = jnp.concatenate([a.reshape(-1).astype(dtype) for a in arrs])
    per = rows_align * lanes
    total = -(-flat.shape[0] // per) * per
    return jnp.pad(flat, (0, total - flat.shape[0])).reshape(total // lanes, lanes)


def _unpack(flat, shapes):
    out, off = [], 0
    for s in shapes:
        n = math.prod(s)
        out.append(flat[off:off + n].reshape(s))
        off += n
    return out


def _remote(src, dst, send, recv, k, dev):
    return pltpu.make_async_remote_copy(src_ref=src, dst_ref=dst, send_sem=send.at[k], recv_sem=recv.at[k],
                                        device_id=dev, device_id_type=MESH)


_ANY = pl.BlockSpec(memory_space=pl.ANY)


def _halves(t):
    r, c = t.shape
    return t.reshape(2, r // 2, c)


def _gather_weights(shards, sp):
    n = len(shards)
    n_sem = 7 * n + 3

    def body(*refs):
        sh, sp_ref = refs[:n], refs[n]
        outs, sout = refs[n + 1:2 * n + 1], refs[2 * n + 1]
        send, recv, loc = refs[2 * n + 2:]
        x, y, c = _coords()
        me = 2 * x + y
        chips = _other_chips(x, y)
        sib = (x, y, 1 - c)
        l1 = pltpu.make_async_copy(sp_ref, sout.at[me], loc.at[0])
        l1.start()
        sends = []
        for j, (cx, cy) in enumerate(chips):
            sends.append(_remote(sp_ref, sout.at[me], send, recv, 7 * n + j, (cx, cy, c)))
            for t in range(n):
                sends.append(_remote(sh[t].at[c], outs[t].at[me, c], send, recv, 7 * t + j, (cx, cy, c)))
        for t in range(n):
            sends.append(_remote(sh[t], outs[t].at[me], send, recv, 7 * t + 6, sib))
        for cp in sends:
            cp.start()
        for j, (cx, cy) in enumerate(chips):
            src = 2 * cx + cy
            for t in range(n):
                _remote(sh[t].at[c], outs[t].at[src, c], send, recv, 7 * t + j, (cx, cy, c)).wait_recv()
                fwd = _remote(outs[t].at[src, c], outs[t].at[src, c], send, recv, 7 * t + 3 + j, sib)
                fwd.start()
                sends.append(fwd)
        for j, (cx, cy) in enumerate(chips):
            src = 2 * cx + cy
            _remote(sp_ref, sout.at[src], send, recv, 7 * n + j, (cx, cy, c)).wait_recv()
            for t in range(n):
                _remote(outs[t].at[src, 1 - c], outs[t].at[src, 1 - c], send, recv, 7 * t + 3 + j, sib).wait_recv()
        for t in range(n):
            _remote(sh[t], outs[t].at[me], send, recv, 7 * t + 6, sib).wait_recv()
        for cp in sends:
            cp.wait_send()
        l1.wait()

    res = pl.pallas_call(
        body, name="gather_weights", in_specs=[_ANY] * (n + 1), out_specs=[_ANY] * (n + 1),
        out_shape=[jax.ShapeDtypeStruct((N_CHIPS,) + t.shape, t.dtype) for t in shards]
        + [jax.ShapeDtypeStruct((N_CHIPS,) + sp.shape, sp.dtype)],
        scratch_shapes=[pltpu.SemaphoreType.DMA((n_sem,)), pltpu.SemaphoreType.DMA((n_sem,)),
                        pltpu.SemaphoreType.DMA((1,))],
    )(*shards, sp)
    return res[:n], res[n]


_HBM = pl.BlockSpec(memory_space=pltpu.HBM)
_SEMS = pl.BlockSpec(memory_space=pltpu.SEMAPHORE)
_DATAFLOW = pltpu.SideEffectType.DATAFLOW_SIDE_EFFECTING


def _in_hbm(a):
    return pltpu.with_memory_space_constraint(a, pltpu.HBM)


def _gather_copies(sh, land, send, recv):
    x, y, c = _coords()
    me = 2 * x + y
    out = []
    for t in range(len(sh)):
        for j, (cx, cy) in enumerate(_other_chips(x, y)):
            dev = (cx, cy, c)
            out.append((_remote(sh[t].at[c], land[t].at[me, c], send, recv, 4 * t + j, dev),
                        _remote(sh[t].at[c], land[t].at[2 * cx + cy, c], send, recv, 4 * t + j, dev)))
        sib = (x, y, 1 - c)
        out.append((_remote(sh[t], land[t].at[me], send, recv, 4 * t + 3, sib),
                    _remote(sh[t], land[t].at[me], send, recv, 4 * t + 3, sib)))
    return out


def _gather_start(shards, after, *, name):
    n = len(shards)

    def body(*refs):
        sh, land = refs[:n], refs[n:2 * n]
        send, recv = refs[2 * n + 1], refs[2 * n + 2]
        token = refs[-1]
        for mine, _ in _gather_copies(sh, land, send, recv):
            mine.start()
        token[...] = jnp.zeros_like(token)

    lands = [_in_hbm(lax.empty((N_CHIPS,) + s.shape, s.dtype)) for s in shards]
    res = pl.pallas_call(
        body, name=name, in_specs=[_HBM] * (2 * n) + [_ANY],
        out_specs=[_SEMS, _SEMS] + [_HBM] * (2 * n) + [pl.BlockSpec(memory_space=pltpu.VMEM)],
        out_shape=[pltpu.SemaphoreType.DMA((4 * n,)), pltpu.SemaphoreType.DMA((4 * n,))]
        + [pltpu.HBM(s.shape, s.dtype) for s in shards] + [pltpu.HBM(l.shape, l.dtype) for l in lands]
        + [jax.ShapeDtypeStruct((8, 128), F32)],
        input_output_aliases={t: 2 + t for t in range(2 * n)},
        compiler_params=pltpu.CompilerParams(has_side_effects=_DATAFLOW),
    )(*[_in_hbm(s) for s in shards], *lands, after)
    return res[0], res[1], res[2:2 + n], res[2 + n:2 + 2 * n], res[-1]


def _gather_wait(send, recv, shards, lands, after, *, name):
    n = len(shards)

    def body(*refs):
        sh, land = refs[:n], refs[n:2 * n]
        send_r, recv_r = refs[2 * n], refs[2 * n + 1]
        for mine, theirs in _gather_copies(sh, land, send_r, recv_r):
            mine.wait_send()
            theirs.wait_recv()

    res = pl.pallas_call(
        body, name=name, in_specs=[_HBM] * (2 * n) + [_SEMS, _SEMS, _ANY], out_specs=[_HBM] * (2 * n),
        out_shape=[pltpu.HBM(s.shape, s.dtype) for s in shards] + [pltpu.HBM(l.shape, l.dtype) for l in lands],
        input_output_aliases={t: t for t in range(2 * n)},
        compiler_params=pltpu.CompilerParams(has_side_effects=_DATAFLOW),
    )(*shards, *lands, send, recv, after)
    return res[n:]


def _gather_forward(lands, *, name):
    n = len(lands)

    def body(*refs):
        o = refs[n:2 * n]
        send, recv = refs[2 * n:]
        x, y, c = _coords()
        sib = (x, y, 1 - c)
        srcs = [2 * cx + cy for cx, cy in _other_chips(x, y)]
        cps = [_remote(o[t].at[s, c], o[t].at[s, c], send, recv, 3 * t + j, sib) for t in range(n) for j, s in enumerate(srcs)]
        for cp in cps:
            cp.start()
        for t in range(n):
            for j, s in enumerate(srcs):
                _remote(o[t].at[s, 1 - c], o[t].at[s, 1 - c], send, recv, 3 * t + j, sib).wait_recv()
        for cp in cps:
            cp.wait_send()

    return pl.pallas_call(
        body, name=name, in_specs=[_ANY] * n, out_specs=[_ANY] * n, input_output_aliases={t: t for t in range(n)},
        out_shape=[jax.ShapeDtypeStruct(l.shape, l.dtype) for l in lands],
        scratch_shapes=[pltpu.SemaphoreType.DMA((3 * n,)), pltpu.SemaphoreType.DMA((3 * n,))],
    )(*lands)


def _allreduce_small(v):
    SR = v.shape[0]

    def body(v_ref, o_ref, buf, send, recv):
        x, y, c = _coords()
        me = 4 * x + 2 * y + c
        buf[me] = v_ref[...]
        peers = []
        for k in range(1, 8):
            px = 1 - x if k & 4 else x
            py = 1 - y if k & 2 else y
            pc = 1 - c if k & 1 else c
            peers.append((px, py, pc))
        cps = [_remote(v_ref, buf.at[me], send, recv, k, p) for k, p in enumerate(peers)]
        for cp in cps:
            cp.start()
        for k, (px, py, pc) in enumerate(peers):
            _remote(v_ref, buf.at[4 * px + 2 * py + pc], send, recv, k, (px, py, pc)).wait_recv()
        for cp in cps:
            cp.wait_send()
        acc = buf[0]
        for s in range(1, 8):
            acc = acc + buf[s]
        o_ref[...] = acc

    vm = pl.BlockSpec(memory_space=pltpu.VMEM)
    return pl.pallas_call(
        body, name="allreduce_small", in_specs=[vm], out_specs=vm, out_shape=jax.ShapeDtypeStruct(v.shape, F32),
        scratch_shapes=[pltpu.VMEM((8, SR, 128), F32), pltpu.SemaphoreType.DMA((7,)), pltpu.SemaphoreType.DMA((7,))],
    )(v)


def _rs_to_sibling(gs, *, name):
    n = len(gs)

    def body(*refs):
        g, a = refs[:n], refs[n:2 * n]
        send, recv = refs[2 * n:]
        x, y, c = _coords()
        cps = [_remote(g[t].at[:, 1 - c], a[t], send, recv, t, (x, y, 1 - c)) for t in range(n)]
        for cp in cps:
            cp.start()
        for cp in cps:
            cp.wait()

    return pl.pallas_call(
        body, name=name, in_specs=[_ANY] * n, out_specs=[_ANY] * n,
        out_shape=[jax.ShapeDtypeStruct((N_CHIPS,) + g.shape[2:], g.dtype) for g in gs],
        scratch_shapes=[pltpu.SemaphoreType.DMA((n,)), pltpu.SemaphoreType.DMA((n,))],
    )(*gs)


RS_ROW_SPLIT = 2


def _rs_add_pair(gs, as_, c_idx, *, name):
    n = len(gs)

    def body(c_ref, *refs):
        for t in range(n):
            refs[2 * n + t][...] = (refs[t][...].astype(F32) + refs[n + t][...].astype(F32)).astype(BF16)

    def gspec(g):
        _, _, rh, cols = g.shape
        return pl.BlockSpec((None, None, rh // RS_ROW_SPLIT, cols), lambda j, i, c_ref: (j, c_ref[0], i, 0))

    def pspec(g):
        _, _, rh, cols = g.shape
        return pl.BlockSpec((None, rh // RS_ROW_SPLIT, cols), lambda j, i, c_ref: (j, i, 0))

    return pl.pallas_call(
        body, name=name,
        grid_spec=pltpu.PrefetchScalarGridSpec(
            num_scalar_prefetch=1, grid=(N_CHIPS, RS_ROW_SPLIT),
            in_specs=[gspec(g) for g in gs] + [pspec(g) for g in gs], out_specs=[pspec(g) for g in gs]),
        out_shape=[jax.ShapeDtypeStruct((N_CHIPS,) + g.shape[2:], BF16) for g in gs], compiler_params=_cp(2),
    )(c_idx, *gs, *as_)


def _chips_copies(p, r, send, recv):
    x, y, c = _coords()
    return [_remote(p[t].at[2 * cx + cy], r[t].at[k], send, recv, 3 * t + k, (cx, cy, c))
            for k, (cx, cy) in enumerate(_other_chips(x, y)) for t in range(len(p))]


def _rs_chips_start(ps, *, name):
    n = len(ps)

    def body(*refs):
        p, r = refs[:n], refs[n:2 * n]
        send, recv = refs[2 * n], refs[2 * n + 1]
        token = refs[-1]
        for cp in _chips_copies(p, r, send, recv):
            cp.start()
        token[...] = jnp.zeros_like(token)

    lands = [_in_hbm(lax.empty((3,) + p.shape[1:], p.dtype)) for p in ps]
    res = pl.pallas_call(
        body, name=name, in_specs=[_HBM] * (2 * n),
        out_specs=[_SEMS, _SEMS] + [_HBM] * (2 * n) + [pl.BlockSpec(memory_space=pltpu.VMEM)],
        out_shape=[pltpu.SemaphoreType.DMA((3 * n,)), pltpu.SemaphoreType.DMA((3 * n,))]
        + [pltpu.HBM(p.shape, p.dtype) for p in ps] + [pltpu.HBM(l.shape, l.dtype) for l in lands]
        + [jax.ShapeDtypeStruct((8, 128), F32)],
        input_output_aliases={t: 2 + t for t in range(2 * n)},
        compiler_params=pltpu.CompilerParams(has_side_effects=_DATAFLOW),
    )(*[_in_hbm(p) for p in ps], *lands)
    return res[0], res[1], res[2:2 + n], res[2 + n:2 + 2 * n], res[-1]


def _rs_chips_wait(send, recv, ps, lands, after, *, name):
    n = len(ps)

    def body(*refs):
        p, r = refs[:n], refs[n:2 * n]
        for cp in _chips_copies(p, r, refs[2 * n], refs[2 * n + 1]):
            cp.wait_send()
            cp.wait_recv()

    res = pl.pallas_call(
        body, name=name, in_specs=[_HBM] * (2 * n) + [_SEMS, _SEMS] + [_ANY] * len(after), out_specs=[_HBM] * (2 * n),
        out_shape=[pltpu.HBM(p.shape, p.dtype) for p in ps] + [pltpu.HBM(l.shape, l.dtype) for l in lands],
        input_output_aliases={t: t for t in range(2 * n)},
        compiler_params=pltpu.CompilerParams(has_side_effects=_DATAFLOW),
    )(*ps, *lands, send, recv, *after)
    return res[:n], res[n:]


def _rs_add_chips(ps, rs, idx, *, name):
    n = len(ps)

    def body(idx_ref, *refs):
        for t in range(n):
            p_ref, r0, r1, r2 = refs[4 * t:4 * t + 4]
            refs[4 * n + t][...] = ((p_ref[...].astype(F32) + r0[...].astype(F32)) + r1[...].astype(F32)) + r2[...].astype(F32)

    in_specs, args = [], []
    for p, r in zip(ps, rs):
        _, rh, cols = p.shape
        blk = (None, rh // RS_ROW_SPLIT, cols)
        in_specs.append(pl.BlockSpec(blk, lambda i, idx_ref: (idx_ref[0], i, 0)))
        in_specs += [pl.BlockSpec(blk, lambda i, idx_ref, k=k: (k, i, 0)) for k in range(3)]
        args += [p, r, r, r]
    out_specs = [pl.BlockSpec((None, p.shape[1] // RS_ROW_SPLIT, p.shape[2]), lambda i, idx_ref: (idx_ref[1], i, 0))
                 for p in ps]
    return pl.pallas_call(
        body, name=name,
        grid_spec=pltpu.PrefetchScalarGridSpec(num_scalar_prefetch=1, grid=(RS_ROW_SPLIT,), in_specs=in_specs,
                                               out_specs=out_specs),
        out_shape=[jax.ShapeDtypeStruct((2,) + p.shape[1:], F32) for p in ps], compiler_params=_cp(1),
    )(idx, *args)


def _rs_join_halves(hs, *, name):
    n = len(hs)

    def body(*refs):
        o = refs[n:2 * n]
        send, recv = refs[2 * n:]
        x, y, c = _coords()
        cps = [_remote(o[t].at[c], o[t].at[c], send, recv, t, (x, y, 1 - c)) for t in range(n)]
        for cp in cps:
            cp.start()
        for t in range(n):
            _remote(o[t].at[1 - c], o[t].at[1 - c], send, recv, t, (x, y, 1 - c)).wait_recv()
        for cp in cps:
            cp.wait_send()

    return pl.pallas_call(
        body, name=name, in_specs=[_ANY] * n, out_specs=[_ANY] * n,
        input_output_aliases={t: t for t in range(n)},
        out_shape=[jax.ShapeDtypeStruct(h.shape, F32) for h in hs],
        scratch_shapes=[pltpu.SemaphoreType.DMA((n,)), pltpu.SemaphoreType.DMA((n,))],
    )(*hs)


def _adamw(w, gs, m, v, *, name, dep=None):
    L, Rr, C = w.shape
    tr, tc = _pick(Rr, (256, 128, 64)), C
    if tr == Rr and Rr * C > 512 * 1024:
        tc = 256
    bc1 = 1.0 - ADAM_B1 ** ADAM_STEP
    bc2 = 1.0 - ADAM_B2 ** ADAM_STEP
    nd = 0 if dep is None else 1

    def body(*refs):
        w_ref, m_ref, v_ref = refs[0], refs[1], refs[2]
        g_refs = refs[3:3 + L]
        d_ref, mo_ref, vo_ref, go_ref = refs[3 + L + nd:]
        layer = pl.program_id(0)
        gv = g_refs[0][...]
        for q in range(1, L):
            gv = jnp.where(layer == q, g_refs[q][...], gv)
        mn = ADAM_B1 * m_ref[...] + (1.0 - ADAM_B1) * gv
        vn = ADAM_B2 * v_ref[...] + (1.0 - ADAM_B2) * (gv * gv)
        go_ref[...] = gv
        mo_ref[...] = mn
        vo_ref[...] = vn
        d_ref[...] = -ADAM_LR * ((mn / bc1) / (jnp.sqrt(vn / bc2) + ADAM_EPS) + ADAM_WD * w_ref[...])

    blk = pl.BlockSpec((None, tr, tc), lambda l, i, j: (l, i, j))
    gblks = [pl.BlockSpec((tr, tc), lambda l, i, j, q=q: (jnp.where(l == q, i, 0), jnp.where(l == q, j, 0))) for q in range(L)]
    return pl.pallas_call(
        body, name=name, grid=(L, Rr // tr, C // tc), in_specs=[blk] * 3 + gblks + [_ANY] * nd, out_specs=[blk] * 4,
        out_shape=[jax.ShapeDtypeStruct((L, Rr, C), F32)] * 4, compiler_params=_cp(3),
    )(w, m, v, *gs, *([] if dep is None else [dep]))


def kernel(x, positions, a_norm, a_in_proj, a_conv_w, a_conv_b, a_dt_bias, a_A_log, a_D, a_gnorm, a_out_proj,
           kv_norm, w_kv, b_kv, k_norm, b_norm, w_q, b_q, q_norm, sinks, w_o, b_o, f_norm, f_w_in, f_conv_w,
           f_conv_b, f_w_down, loss_target, m_a_norm, m_a_in_proj, m_a_conv_w, m_a_conv_b, m_a_dt_bias, m_a_A_log,
           m_a_D, m_a_gnorm, m_a_out_proj, m_kv_norm, m_w_kv, m_b_kv, m_k_norm, m_b_norm, m_w_q, m_b_q, m_q_norm,
           m_sinks, m_w_o, m_b_o, m_f_norm, m_f_w_in, m_f_conv_w, m_f_conv_b, m_f_w_down, v_a_norm, v_a_in_proj,
           v_a_conv_w, v_a_conv_b, v_a_dt_bias, v_a_A_log, v_a_D, v_a_gnorm, v_a_out_proj, v_kv_norm, v_w_kv,
           v_b_kv, v_k_norm, v_b_norm, v_w_q, v_b_q, v_q_norm, v_sinks, v_w_o, v_b_o, v_f_norm, v_f_w_in,
           v_f_conv_w, v_f_conv_b, v_f_w_down):
    wl = dict(zip(WEIGHTS, (a_norm, a_in_proj, a_conv_w, a_conv_b, a_dt_bias, a_A_log, a_D, a_gnorm, a_out_proj,
                            kv_norm, w_kv, b_kv, k_norm, b_norm, w_q, b_q, q_norm, sinks, w_o, b_o, f_norm, f_w_in,
                            f_conv_w, f_conv_b, f_w_down)))
    ml = dict(zip(WEIGHTS, (m_a_norm, m_a_in_proj, m_a_conv_w, m_a_conv_b, m_a_dt_bias, m_a_A_log, m_a_D, m_a_gnorm,
                            m_a_out_proj, m_kv_norm, m_w_kv, m_b_kv, m_k_norm, m_b_norm, m_w_q, m_b_q, m_q_norm,
                            m_sinks, m_w_o, m_b_o, m_f_norm, m_f_w_in, m_f_conv_w, m_f_conv_b, m_f_w_down)))
    vl = dict(zip(WEIGHTS, (v_a_norm, v_a_in_proj, v_a_conv_w, v_a_conv_b, v_a_dt_bias, v_a_A_log, v_a_D, v_a_gnorm,
                            v_a_out_proj, v_kv_norm, v_w_kv, v_b_kv, v_k_norm, v_b_norm, v_w_q, v_b_q, v_q_norm,
                            v_sinks, v_w_o, v_b_o, v_f_norm, v_f_w_in, v_f_conv_w, v_f_conv_b, v_f_w_down)))
    xi, yi, ci = _coords()
    me = 2 * xi + yi
    S = x.shape[1]

    def block_of(n, layer):
        t = wl[n]
        return t if layer is None else t[layer]

    rows = lambda t: t.reshape(-1, t.shape[-1])
    c_idx = jnp.reshape(ci, (1,)).astype(jnp.int32)
    me_c = jnp.stack([me, ci]).astype(jnp.int32)
    early = ("in_proj", "out_proj")
    late = tuple(name for name, _, _ in MATS if name not in early)
    shards = {name: _halves(block_of(wn, layer).astype(BF16)) for name, wn, layer in MATS}

    sp = _pack([wl[n] for n, _ in SMALL_CUT], 8, 128, F32)
    gathered, gs = _gather_weights([shards[k] for k in early], sp)
    gt = {k: t.reshape(N_CHIPS, -1, t.shape[-1]) for k, t in zip(early, gathered)}
    started = _gather_start([shards[k] for k in late], gs, name="gather_late_start")
    full = {n: wl[n] for n in SMALL_REP}
    gs = gs.reshape(N_CHIPS, -1)
    pieces = [_unpack(gs[j], [wl[n].shape for n, _ in SMALL_CUT]) for j in range(N_CHIPS)]
    for q, (n, ax) in enumerate(SMALL_CUT):
        full[n] = jnp.concatenate([pieces[j][q] for j in range(N_CHIPS)], axis=ax)
    w = _prep_small(full, {})
    w["w_zx"], w["w_dt"] = _split_in_proj(gt["in_proj"].transpose(1, 0, 2).reshape(1024, -1))
    w["a_out_proj"] = rows(gt["out_proj"])
    w["dep"] = started[4]

    class Comm:
        pending = None
        token = None
        reduced = {}

        def late_weights(self, w, after):
            lands = _gather_wait(started[0], started[1], started[2], started[3], after, name="gather_late_wait")
            lands = _gather_forward(lands, name="gather_late_forward")
            lt = {k: t.reshape(N_CHIPS, -1, t.shape[-1]) for k, t in zip(late, lands)}
            w = dict(w)
            w["w_kv"], w["w_q"], w["w_o"] = (rows(lt[k]) for k in ("w_kv", "w_q", "w_o"))
            w["f_w_in"] = [lt["f_in0"], lt["f_in1"]]
            w["f_w_down"] = [rows(lt["f_down0"]), rows(lt["f_down1"])]
            return w

        def finish(self, after):
            names, send, recv, ps, lands, tag = self.pending
            ps, rs = _rs_chips_wait(send, recv, ps, lands, after, name=f"rs_chips_wait{tag}")
            halves = _rs_add_chips(ps, rs, me_c, name=f"rs_add_chips{tag}")
            joined = _rs_join_halves(halves, name=f"rs_join_halves{tag}")
            self.reduced.update({k: rows(t) for k, t in zip(names, joined)})
            self.pending = None

        def grads(self, group, tensors, after):
            if self.pending is not None:
                self.finish([after])
            names = list(tensors)
            glist = [tensors[k].reshape(N_CHIPS, 2, -1, tensors[k].shape[-1]) for k in names]
            from_sib = _rs_to_sibling(glist, name=f"rs_to_sibling{group}")
            pairs = _rs_add_pair(glist, from_sib, c_idx, name=f"rs_add_pair{group}")
            send, recv, ps, lands, token = _rs_chips_start(pairs, name=f"rs_chips_start{group}")
            self.pending = (names, send, recv, ps, lands, group)
            self.token = token
            return token

    comm = Comm()

    posf = positions.reshape(S, 1).astype(F32)
    loss_part, dx0, gr = _local_step(x[0], posf, loss_target[0], w, comm)
    g = _small_grads(gr)

    small_names = [n for n, _ in SMALL_CUT] + list(SMALL_REP)
    sv = _pack([g[n] for n in small_names] + [loss_part[0:1, 0:1]], 8, 128, F32)
    sred = _allreduce_small(sv).reshape(-1)
    small_shapes = [g[n].shape for n in small_names] + [(1,)]
    sg = dict(zip(small_names + ["loss"], _unpack(sred, small_shapes)))
    loss = sg["loss"].reshape(())
    g_small = {}
    for n, ax in SMALL_CUT:
        size = wl[n].shape[ax]
        g_small[n] = lax.dynamic_slice_in_dim(sg[n], me * size, size, axis=ax)
    for n in SMALL_REP:
        g_small[n] = sg[n].reshape(wl[n].shape)

    grads, delta, new_m, new_v = {}, {}, {}, {}

    def update(wn, dep):
        gl = [comm.reduced[name] for name, n2, _ in MATS if n2 == wn]
        shp = wl[wn].shape
        three = (len(gl),) + gl[0].shape
        flip = shp[-1] % 128 != 0
        view = (lambda t: t.reshape(three).transpose(0, 2, 1)) if flip else (lambda t: t.reshape(three))
        back = (lambda t: t.transpose(0, 2, 1).reshape(shp)) if flip else (lambda t: t.reshape(shp))
        if flip:
            gl = [t.T for t in gl]
        d, mn, vn, go = _adamw(view(wl[wn]), gl, view(ml[wn]), view(vl[wn]), name="adamw_" + wn, dep=dep)
        grads[wn], delta[wn], new_m[wn], new_v[wn] = back(go), back(d), back(mn), back(vn)
        return d

    comm.finish([update(wn, comm.token) for wn in ("f_w_in", "f_w_down", "w_q", "w_o", "w_kv")])
    for wn in ("a_in_proj", "a_out_proj"):
        update(wn, None)
    pk = lambda d: _pack([d[n] for n in small_names], 8, 128, F32)[None]
    d, mn, vn, _ = _adamw(pk(wl), [pk(g_small)[0]], pk(ml), pk(vl), name="adamw_small")
    shapes = [wl[n].shape for n in small_names]
    for n, dd, mm, vv in zip(small_names, _unpack(d.reshape(-1), shapes), _unpack(mn.reshape(-1), shapes),
                             _unpack(vn.reshape(-1), shapes)):
        grads[n], delta[n], new_m[n], new_v[n] = g_small[n], dd, mm, vv

    return (loss, dx0[None], *[grads[n] for n in WEIGHTS], *[delta[n] for n in WEIGHTS],
            *[new_m[n] for n in WEIGHTS], *[new_v[n] for n in WEIGHTS])
```

```python
import math

import jax
import jax.numpy as jnp
from jax import lax
from jax.experimental import pallas as pl
from jax.experimental.pallas import tpu as pltpu

F32 = jnp.float32
BF16 = jnp.bfloat16

EPS = 1e-5
CHUNK = 256
WINDOW = 128
HEAD = 64
SSM_HEADS = 32
SSM_GROUPS = 8
SSM_STATE = 128
ATT_KV = 4
ATT_G = 4
ROPE_THETA = 10000.0
NEG = -1e30
N_CHIPS = 4
VMEM_LIMIT = 56 * 1024 * 1024

ADAM_LR, ADAM_B1, ADAM_B2, ADAM_EPS, ADAM_WD, ADAM_STEP = 0.001, 0.9, 0.999, 1e-08, 0.01, 10


def _cp(n_axes):
    return pltpu.CompilerParams(dimension_semantics=("arbitrary",) * n_axes, vmem_limit_bytes=VMEM_LIMIT)


def _pick(dim, prefs):
    for p in prefs:
        if dim % p == 0:
            return p
    return dim


def _iota(shape, dim):
    return lax.broadcasted_iota(jnp.int32, shape, dim)


def _dot(a, b, ca=1, cb=0):
    return lax.dot_general(a, b, (((ca,), (cb,)), ((), ())), preferred_element_type=F32)


def _dot3(x, ind):
    h = x.astype(BF16)
    r = x - h.astype(F32)
    m = r.astype(BF16)
    lo = (r - m.astype(F32)).astype(BF16)
    return _dot(h, ind) + _dot(m, ind) + _dot(lo, ind)


def _sigmoid(x):
    return jax.nn.sigmoid(x)


def _mm(a, b, *, name, ta=False, tb=False, bias=None, res=None, out_dtype=F32, b_koff=0, tm=None, tn=None, tk=None,
        dims=None, a_spec=None, b_spec=None, o_spec=None, o_shape=None, dep=None, more=()):
    if dims is not None:
        M, N, K = dims
    else:
        if ta:
            K, M = a.shape
        else:
            M, K = a.shape
        N = b.shape[0] if tb else b.shape[1]
    tm = tm or _pick(M, (1024, 1408, 512, 256, 128))
    tn = tn or _pick(N, (512, 1408, 256, 128))
    tk = tk or (K if K <= 2048 else _pick(K, (2048, 1408, 1024, 512)))
    assert M % tm == 0 and N % tn == 0 and K % tk == 0 and b_koff % tk == 0
    nk = K // tk
    kb0 = b_koff // tk
    has_bias, has_res = bias is not None, res is not None

    def body(*refs):
        a_ref, b_ref = refs[0], refs[1]
        pos = 2
        bias_ref = res_ref = acc_ref = None
        if has_bias:
            bias_ref = refs[pos]
            pos += 1
        if has_res:
            res_ref = refs[pos]
            pos += 1
        if dep is not None:
            pos += 1
        extra = refs[pos:pos + 2 * len(more)]
        pos += 2 * len(more)
        o_ref = refs[pos]
        if nk > 1:
            acc_ref = refs[pos + 1]
        part = _dot(a_ref[...].astype(BF16), b_ref[...].astype(BF16), 0 if ta else 1, 1 if tb else 0)
        for q in range(len(more)):
            part = part + _dot(extra[2 * q][...].astype(BF16), extra[2 * q + 1][...].astype(BF16),
                               0 if ta else 1, 1 if tb else 0)

        def finish(acc):
            if has_bias:
                acc = acc + bias_ref[...]
            if has_res:
                acc = acc + res_ref[...]
            o_ref[...] = acc.astype(out_dtype)

        if nk == 1:
            finish(part)
        else:
            k = pl.program_id(2)

            @pl.when(k == 0)
            def _():
                acc_ref[...] = part

            @pl.when(k > 0)
            def _():
                acc_ref[...] += part

            @pl.when(k == nk - 1)
            def _():
                finish(acc_ref[...])

    if a_spec is None:
        a_spec = pl.BlockSpec((tk, tm), lambda i, j, k: (k, i)) if ta else pl.BlockSpec((tm, tk), lambda i, j, k: (i, k))
    if b_spec is None:
        b_spec = (pl.BlockSpec((tn, tk), lambda i, j, k: (j, k + kb0)) if tb
                  else pl.BlockSpec((tk, tn), lambda i, j, k: (k + kb0, j)))
    if o_spec is None:
        o_spec = pl.BlockSpec((tm, tn), lambda i, j, k: (i, j))
    in_specs, args = [a_spec, b_spec], [a, b]
    if has_bias:
        in_specs.append(pl.BlockSpec((1, tn), lambda i, j, k: (0, j)))
        args.append(bias)
    if has_res:
        in_specs.append(pl.BlockSpec((tm, tn), lambda i, j, k: (i, j)))
        args.append(res)
    if dep is not None:
        in_specs.append(pl.BlockSpec(memory_space=pl.ANY))
        args.append(dep)
    for sa, sb in more:
        in_specs += [sa, sb]
        args += [a, b]
    return pl.pallas_call(
        body, name=name, grid=(M // tm, N // tn, nk), in_specs=in_specs, out_specs=o_spec,
        out_shape=jax.ShapeDtypeStruct(o_shape or (M, N), out_dtype),
        scratch_shapes=[pltpu.VMEM((tm, tn), F32)] if nk > 1 else [],
        compiler_params=_cp(3),
    )(*args)


def _rms_fwd(x, gains, *, name, tr=256, dep=None):
    S, D = x.shape
    n = len(gains)
    nd = 0 if dep is None else 1

    def body(*refs):
        xv = refs[0][...]
        xh = xv * lax.rsqrt(jnp.mean(xv * xv, axis=-1, keepdims=True) + EPS)
        for q in range(n):
            refs[1 + n + nd + q][...] = (xh * refs[1 + q][...]).astype(BF16)

    row = pl.BlockSpec((tr, D), lambda i: (i, 0))
    vec = pl.BlockSpec((1, D), lambda i: (0, 0))
    return pl.pallas_call(
        body, name=name, grid=(S // tr,), in_specs=[row] + [vec] * n + [pl.BlockSpec(memory_space=pl.ANY)] * nd,
        out_specs=[row] * n, out_shape=[jax.ShapeDtypeStruct((S, D), BF16)] * n, compiler_params=_cp(1),
    )(x, *gains, *([] if dep is None else [dep]))


def _rms_bwd(x, gains, dhs, dres, *, name, tr=256, want_colsum=False):
    S, D = x.shape
    n = len(gains)
    steps = S // tr

    def body(*refs):
        x_ref = refs[0]
        g_refs = refs[1:1 + n]
        dh_refs = refs[1 + n:1 + 2 * n]
        dres_ref = refs[1 + 2 * n]
        dx_ref = refs[2 + 2 * n]
        dg_refs = refs[3 + 2 * n:3 + 3 * n]
        cs_ref = refs[3 + 3 * n] if want_colsum else None
        i = pl.program_id(0)
        xv = x_ref[...]
        r = lax.rsqrt(jnp.mean(xv * xv, axis=-1, keepdims=True) + EPS)
        xh = xv * r
        dx = dres_ref[...]
        for q in range(n):
            dh = dh_refs[q][...]
            dxh = dh * g_refs[q][...]
            dx = dx + r * (dxh - xh * jnp.mean(dxh * xh, axis=-1, keepdims=True))
            part = jnp.sum(dh * xh, axis=0, keepdims=True)

            @pl.when(i == 0)
            def _():
                dg_refs[q][...] = part

            @pl.when(i > 0)
            def _():
                dg_refs[q][...] += part

        dx_ref[...] = dx
        if want_colsum:
            cpart = jnp.sum(dx, axis=0, keepdims=True)

            @pl.when(i == 0)
            def _():
                cs_ref[...] = cpart

            @pl.when(i > 0)
            def _():
                cs_ref[...] += cpart

    row = pl.BlockSpec((tr, D), lambda i: (i, 0))
    vec = pl.BlockSpec((1, D), lambda i: (0, 0))
    n_vec_out = n + (1 if want_colsum else 0)
    outs = pl.pallas_call(
        body, name=name, grid=(steps,), in_specs=[row] + [vec] * n + [row] * n + [row],
        out_specs=[row] + [vec] * n_vec_out,
        out_shape=[jax.ShapeDtypeStruct((S, D), F32)] + [jax.ShapeDtypeStruct((1, D), F32)] * n_vec_out,
        compiler_params=_cp(1),
    )(x, *gains, *dhs, dres)
    return outs


def _colsum(x, *, name, tr=256):
    S, D = x.shape

    def body(x_ref, o_ref):
        i = pl.program_id(0)
        part = jnp.sum(x_ref[...].astype(F32), axis=0, keepdims=True)

        @pl.when(i == 0)
        def _():
            o_ref[...] = part

        @pl.when(i > 0)
        def _():
            o_ref[...] += part

    return pl.pallas_call(
        body, name=name, grid=(S // tr,), in_specs=[pl.BlockSpec((tr, D), lambda i: (i, 0))],
        out_specs=pl.BlockSpec((1, D), lambda i: (0, 0)), out_shape=jax.ShapeDtypeStruct((1, D), F32),
        compiler_params=_cp(1),
    )(x)


def _loss(y, t, *, name="loss", tr=256):
    S, D = y.shape
    steps = S // tr

    def body(y_ref, t_ref, dy_ref, l_ref, acc_ref):
        i = pl.program_id(0)
        e = y_ref[...] - t_ref[...]
        dy_ref[...] = e * (1.0 / D)
        part = jnp.sum(e * e, axis=0, keepdims=True)

        @pl.when(i == 0)
        def _():
            acc_ref[...] = part

        @pl.when(i > 0)
        def _():
            acc_ref[...] += part

        @pl.when(i == steps - 1)
        def _():
            tot = jnp.sum(acc_ref[...], axis=1, keepdims=True) * (0.5 / D)
            l_ref[...] = jnp.broadcast_to(tot, (8, 128))

    row = pl.BlockSpec((tr, D), lambda i: (i, 0))
    return pl.pallas_call(
        body, name=name, grid=(steps,), in_specs=[row, row],
        out_specs=[row, pl.BlockSpec((8, 128), lambda i: (0, 0))],
        out_shape=[jax.ShapeDtypeStruct((S, D), F32), jax.ShapeDtypeStruct((8, 128), F32)],
        scratch_shapes=[pltpu.VMEM((1, D), F32)], compiler_params=_cp(1),
    )(y, t)


STRIP = 64
HALO = 8


def _strips(S, tc):
    return [(r0, slice(l0, l0 + 128)) for l0 in range(0, tc, 128) for r0 in range(S - STRIP, -1, -STRIP)]


def _with_halo(ref, r0, ls):
    if r0 == 0:
        return jnp.concatenate([jnp.zeros((HALO, 128), F32), ref[0:STRIP, ls]], axis=0)
    return ref[r0 - HALO:r0 + STRIP, ls]


def _conv_strip(xw, w_ref, b_ref, ls, width):
    acc = b_ref[:, ls] + w_ref[pl.ds(width - 1, 1), ls] * xw[HALO:]
    shifted = []
    for s in range(1, width):
        xs = pltpu.roll(xw, s, axis=0)[HALO:]
        shifted.append(xs)
        acc = acc + w_ref[pl.ds(width - 1 - s, 1), ls] * xs
    return acc, shifted


def _conv_strip_back(dacc, after, xc, shifted, w_ref, ls, width):
    ext = jnp.concatenate([dacc, after], axis=0)
    dx = w_ref[pl.ds(width - 1, 1), ls] * dacc
    dws = [None] * width
    dws[width - 1] = jnp.sum(dacc * xc, axis=0, keepdims=True)
    for s in range(1, width):
        dx = dx + w_ref[pl.ds(width - 1 - s, 1), ls] * pltpu.roll(ext, STRIP + HALO - s, axis=0)[:STRIP]
        dws[width - 1 - s] = jnp.sum(dacc * shifted[s - 1], axis=0, keepdims=True)
    return dx, dws, jnp.sum(dacc, axis=0, keepdims=True)


def _conv_back_block(S, tc, width, w_ref, b_ref, x_ref, dacc_of, dx_store, dw_ref, db_ref):
    for l0 in range(0, tc, 128):
        ls = slice(l0, l0 + 128)
        after = jnp.zeros((HALO, 128), F32)
        tot = None
        for r0 in range(S - STRIP, -1, -STRIP):
            xw = _with_halo(x_ref, r0, ls)
            acc, shifted = _conv_strip(xw, w_ref, b_ref, ls, width)
            dacc = dacc_of(r0, ls, acc, _sigmoid(acc))
            dx, dws, db = _conv_strip_back(dacc, after, xw[HALO:], shifted, w_ref, ls, width)
            dx_store(r0, ls, dx)
            after = dacc[:HALO]
            part = dws + [db]
            tot = part if tot is None else [p + q for p, q in zip(tot, part)]
        for k in range(width):
            dw_ref[pl.ds(k, 1), ls] = tot[k]
        db_ref[:, ls] = tot[width]


def _conv_silu_fwd(xin, col0, C, w, b, *, name, tc=512):
    S = xin.shape[0]
    width = w.shape[0]
    off = col0 // tc

    def body(x_ref, w_ref, b_ref, o_ref):
        for r0, ls in _strips(S, tc):
            acc, _ = _conv_strip(_with_halo(x_ref, r0, ls), w_ref, b_ref, ls, width)
            o_ref[r0:r0 + STRIP, ls] = acc * _sigmoid(acc)

    return pl.pallas_call(
        body, name=name, grid=(C // tc,),
        in_specs=[pl.BlockSpec((S, tc), lambda j: (0, j + off)), pl.BlockSpec((width, tc), lambda j: (0, j)),
                  pl.BlockSpec((1, tc), lambda j: (0, j))],
        out_specs=pl.BlockSpec((S, tc), lambda j: (0, j)), out_shape=jax.ShapeDtypeStruct((S, C), F32),
        compiler_params=_cp(1),
    )(xin, w, b)


def _conv_silu_bwd(xin, col0, C, w, b, douts, *, name, tc=256):
    S = xin.shape[0]
    width = w.shape[0]
    off = col0 // tc
    nd = len(douts)
    ranges = [(o // tc, (o + d.shape[1]) // tc) for d, o in douts]

    def body(*refs):
        x_ref, w_ref, b_ref = refs[0], refs[1], refs[2]
        d_refs = refs[3:3 + nd]
        dx_ref, dw_ref, db_ref = refs[3 + nd], refs[4 + nd], refs[5 + nd]
        j = pl.program_id(0)

        def dacc_of(r0, ls, acc, sg):
            dout = jnp.zeros((STRIP, 128), F32)
            for q in range(nd):
                lo, hi = ranges[q]
                dout = dout + jnp.where((j >= lo) & (j < hi), d_refs[q][r0:r0 + STRIP, ls], 0.0)
            return dout * (sg * (1.0 + acc * (1.0 - sg)))

        def dx_store(r0, ls, dx):
            dx_ref[r0:r0 + STRIP, ls] = dx.astype(BF16)

        _conv_back_block(S, tc, width, w_ref, b_ref, x_ref, dacc_of, dx_store, dw_ref, db_ref)

    d_specs = [pl.BlockSpec((S, tc), (lambda j, lo=lo, hi=hi: (0, jnp.clip(j - lo, 0, hi - lo - 1)))) for lo, hi in ranges]
    return pl.pallas_call(
        body, name=name, grid=(C // tc,),
        in_specs=[pl.BlockSpec((S, tc), lambda j: (0, j + off)), pl.BlockSpec((width, tc), lambda j: (0, j)),
                  pl.BlockSpec((1, tc), lambda j: (0, j))] + d_specs,
        out_specs=[pl.BlockSpec((S, tc), lambda j: (0, j)), pl.BlockSpec((width, tc), lambda j: (0, j)),
                   pl.BlockSpec((1, tc), lambda j: (0, j))],
        out_shape=[jax.ShapeDtypeStruct((S, C), BF16), jax.ShapeDtypeStruct((width, C), F32),
                   jax.ShapeDtypeStruct((1, C), F32)],
        compiler_params=_cp(1),
    )(xin, w, b, *[d for d, _ in douts])


def _ffn_act_fwd(u, w, b, *, name, tc=256):
    S, F2 = u.shape
    Fd = F2 // 2
    width = w.shape[0]
    nb = Fd // tc

    def body(g_ref, v_ref, w_ref, b_ref, o_ref):
        for r0, ls in _strips(S, tc):
            acc, _ = _conv_strip(_with_halo(g_ref, r0, ls), w_ref, b_ref, ls, width)
            o_ref[r0:r0 + STRIP, ls] = (acc * _sigmoid(acc) * v_ref[r0:r0 + STRIP, ls]).astype(BF16)

    return pl.pallas_call(
        body, name=name, grid=(nb,),
        in_specs=[pl.BlockSpec((S, tc), lambda j: (0, j)), pl.BlockSpec((S, tc), lambda j: (0, j + nb)),
                  pl.BlockSpec((width, tc), lambda j: (0, j)), pl.BlockSpec((1, tc), lambda j: (0, j))],
        out_specs=pl.BlockSpec((S, tc), lambda j: (0, j)), out_shape=jax.ShapeDtypeStruct((S, Fd), BF16),
        compiler_params=_cp(1),
    )(u, u, w, b)


def _ffn_act_bwd(u, w, b, da, *, name, tc=256):
    S, F2 = u.shape
    Fd = F2 // 2
    width = w.shape[0]
    nb = Fd // tc

    def body(g_ref, v_ref, w_ref, b_ref, da_ref, du_ref, dw_ref, db_ref, a_ref):
        def dacc_of(r0, ls, acc, sg):
            rs = slice(r0, r0 + STRIP)
            dav, val, silu = da_ref[rs, ls], v_ref[rs, ls], acc * sg
            a_ref[rs, ls] = (silu * val).astype(BF16)
            du_ref[1, rs, ls] = (dav * silu).astype(BF16)
            return dav * val * (sg * (1.0 + acc * (1.0 - sg)))

        def dx_store(r0, ls, dx):
            du_ref[0, r0:r0 + STRIP, ls] = dx.astype(BF16)

        _conv_back_block(S, tc, width, w_ref, b_ref, g_ref, dacc_of, dx_store, dw_ref, db_ref)

    blk = pl.BlockSpec((S, tc), lambda j: (0, j))
    return pl.pallas_call(
        body, name=name, grid=(nb,),
        in_specs=[blk, pl.BlockSpec((S, tc), lambda j: (0, j + nb)), pl.BlockSpec((width, tc), lambda j: (0, j)),
                  pl.BlockSpec((1, tc), lambda j: (0, j)), blk],
        out_specs=[pl.BlockSpec((2, S, tc), lambda j: (0, 0, j)), pl.BlockSpec((width, tc), lambda j: (0, j)),
                   pl.BlockSpec((1, tc), lambda j: (0, j)), blk],
        out_shape=[jax.ShapeDtypeStruct((2, S, Fd), BF16),
                   jax.ShapeDtypeStruct((width, Fd), F32), jax.ShapeDtypeStruct((1, Fd), F32),
                   jax.ShapeDtypeStruct((S, Fd), BF16)],
        compiler_params=_cp(1),
    )(u, u, w, b, da)


def _ssd_prep(dtr, dt_bias, a_log, *, name="ssd_prep"):
    S = dtr.shape[0]

    def body(d_ref, b_ref, al_ref, dt_ref, ac_ref, sg_ref, act_ref):
        lane = _iota((CHUNK, 128), 1)
        valid = lane < SSM_HEADS
        z = d_ref[...] + b_ref[...]
        dt = jnp.where(valid, jnp.maximum(z, 0.0) + jnp.log(1.0 + jnp.exp(-jnp.abs(z))), 0.0)
        a = dt * (-jnp.exp(al_ref[...]))
        row = _iota((CHUNK, 128), 0)
        k = 1
        while k < CHUNK:
            a = a + jnp.where(row >= k, pltpu.roll(a, k, axis=0), 0.0)
            k *= 2
        sg = jnp.where(valid, _sigmoid(z), 0.0)
        for arr, ref in ((dt, dt_ref), (a, ac_ref), (sg, sg_ref)):
            for g in range(SSM_GROUPS):
                ref[g] = jnp.where(lane < 4, arr if g == 0 else pltpu.roll(arr, 128 - 4 * g, axis=1), 0.0)
        act_ref[...] = a.T[:SSM_HEADS, :]

    blk = pl.BlockSpec((CHUNK, 128), lambda i: (i, 0))
    vec = pl.BlockSpec((1, 128), lambda i: (0, 0))
    grp = pl.BlockSpec((SSM_GROUPS, CHUNK, 128), lambda i: (0, i, 0))
    return pl.pallas_call(
        body, name=name, grid=(S // CHUNK,), in_specs=[blk, vec, vec],
        out_specs=[grp, grp, grp, pl.BlockSpec((SSM_HEADS, CHUNK), lambda i: (0, i))],
        out_shape=[jax.ShapeDtypeStruct((SSM_GROUPS, S, 128), F32)] * 3 + [jax.ShapeDtypeStruct((SSM_HEADS, S), F32)],
        compiler_params=_cp(1),
    )(dtr, dt_bias, a_log)


SSD_GPS = 2


def _expand4(v, lanes):
    out = jnp.broadcast_to(v[:, 3:4], lanes.shape)
    for hh in (2, 1, 0):
        out = jnp.where(lanes < 64 * (hh + 1), v[:, hh:hh + 1], out)
    return out


def _ssd_fwd(xbc, dt_g, ac_g, ac_t, *, name="ssd_fwd"):
    S = xbc.shape[0]
    nc = S // CHUNK
    Lc = CHUNK

    def body(x_ref, b_ref, c_ref, dt_ref, ac_ref, act_ref, y_ref, st_out_ref, st_ref):
        g2 = pl.program_id(0)
        c = pl.program_id(1)

        @pl.when(c == 0)
        def _():
            st_ref[...] = jnp.zeros_like(st_ref)

        causal = _iota((Lc, Lc), 0) >= _iota((Lc, Lc), 1)
        lane256 = _iota((Lc, 256), 1)
        lane128 = _iota((Lc, 128), 1)
        row128 = _iota((128, 128), 0)
        for gg in range(SSD_GPS):
            g = SSD_GPS * g2 + gg
            bv = b_ref[:, 128 * gg:128 * (gg + 1)]
            cbf = c_ref[:, 128 * gg:128 * (gg + 1)].astype(BF16)
            cb = _dot(cbf, bv.astype(BF16), 1, 1)
            dtg, acg = dt_ref[gg], ac_ref[gg]
            ac_last = ac_ref[gg, pl.ds(Lc - 1, 1), :]
            dt4 = _expand4(dtg, lane256)
            ac4 = _expand4(acg, lane256)
            e4 = jnp.exp(ac4)
            xdb = (x_ref[:, 256 * gg:256 * (gg + 1)] * dt4).astype(BF16)
            st_out_ref[gg] = st_ref[gg]
            for p in range(2):
                xd_p = xdb[:, 128 * p:128 * (p + 1)]
                st_p = st_ref[gg, p]
                ys, sn, cds = [], [], []
                for q in range(2):
                    hh = 2 * p + q
                    a_col = acg[:, hh:hh + 1]
                    a_row = act_ref[pl.ds(4 * g + hh, 1), :]
                    dec = jnp.exp(jnp.where(causal, a_col - a_row, NEG))
                    w = (cb * dec).astype(BF16)
                    ys.append(_dot(w, xd_p))
                    al = ac_last[:, hh:hh + 1]
                    dte = jnp.exp(al - a_col)
                    sn.append(_dot(xd_p, (bv * dte).astype(BF16), 0, 0))
                    cds.append(jnp.exp(al))
                y_diag = jnp.where(lane128 < 64, ys[0], ys[1])
                y_off = _dot(cbf, st_p.astype(BF16), 1, 1) * e4[:, 128 * p:128 * (p + 1)]
                y_ref[:, 256 * gg + 128 * p:256 * gg + 128 * (p + 1)] = y_diag + y_off
                st_ref[gg, p] = jnp.where(row128 < 64, st_p * cds[0] + sn[0], st_p * cds[1] + sn[1])

    G = SSD_GPS
    per_g = lambda g, c: (g, c, 0)
    return pl.pallas_call(
        body, name=name, grid=(SSM_GROUPS // G, nc),
        in_specs=[pl.BlockSpec((Lc, 256 * G), lambda g, c: (c, g)),
                  pl.BlockSpec((Lc, 128 * G), lambda g, c: (c, 16 // G + g)),
                  pl.BlockSpec((Lc, 128 * G), lambda g, c: (c, 24 // G + g)),
                  pl.BlockSpec((G, Lc, 128), per_g), pl.BlockSpec((G, Lc, 128), per_g),
                  pl.BlockSpec((SSM_HEADS, Lc), lambda g, c: (0, c))],
        out_specs=[pl.BlockSpec((Lc, 256 * G), lambda g, c: (c, g)),
                   pl.BlockSpec((G, None, 2, 128, 128), lambda g, c: (g, c, 0, 0, 0))],
        out_shape=[jax.ShapeDtypeStruct((S, 2048), F32), jax.ShapeDtypeStruct((SSM_GROUPS, nc, 2, 128, 128), F32)],
        scratch_shapes=[pltpu.VMEM((G, 2, 128, 128), F32)], compiler_params=_cp(2),
    )(xbc, xbc, xbc, dt_g, ac_g, ac_t)


def _ssd_bwd(xbc, dt_g, ac_g, ac_t, states, dy, dexp, *, name="ssd_bwd"):
    S = xbc.shape[0]
    nc = S // CHUNK
    Lc = CHUNK

    def body(x_ref, b_ref, c_ref, dt_ref, ac_ref, act_ref, st_ref, dy_ref, d_ref, dx_ref, db_ref, dc_ref, dh_ref, ds_ref):
        g2 = pl.program_id(0)
        cc = pl.program_id(1)

        @pl.when(cc == 0)
        def _():
            ds_ref[...] = jnp.zeros_like(ds_ref)

        causal = _iota((Lc, Lc), 0) >= _iota((Lc, Lc), 1)
        lane256 = _iota((Lc, 256), 1)
        lane128 = _iota((Lc, 128), 1)
        row128 = _iota((128, 128), 0)
        ind_rows = _iota((256, 128), 0) >> 6
        ind_cols = _iota((256, 128), 1)
        ind_a = (ind_rows == ind_cols).astype(BF16)
        ind_b = (ind_rows + 4 == ind_cols).astype(BF16)
        for gg in range(SSD_GPS):
            g = SSD_GPS * g2 + gg
            bv = b_ref[:, 128 * gg:128 * (gg + 1)]
            cv = c_ref[:, 128 * gg:128 * (gg + 1)]
            bbf, cbf = bv.astype(BF16), cv.astype(BF16)
            cb = _dot(cbf, bbf, 1, 1)
            dtg, acg = dt_ref[gg], ac_ref[gg]
            ac_last = ac_ref[gg, pl.ds(Lc - 1, 1), :]
            dt4 = _expand4(dtg, lane256)
            ac4 = _expand4(acg, lane256)
            acl4 = _expand4(ac_last, _iota((1, 256), 1))
            e4 = jnp.exp(ac4)
            dte4 = jnp.exp(acl4 - ac4)
            xv = x_ref[:, 256 * gg:256 * (gg + 1)]
            xd = xv * dt4
            xdb = xd.astype(BF16)
            dyv = dy_ref[:, 256 * gg:256 * (gg + 1)]
            dcb = jnp.zeros((Lc, Lc), F32)
            dc_acc = jnp.zeros((Lc, 128), F32)
            db_acc = jnp.zeros((Lc, 128), F32)
            u_parts, dxd_parts, ends = [], [], []
            for p in range(2):
                sl = slice(128 * p, 128 * (p + 1))
                xd_p, xdb_p, dy_p = xd[:, sl], xdb[:, sl], dyv[:, sl]
                dyb_p = dy_p.astype(BF16)
                e_p, dte_p = e4[:, sl], dte4[:, sl]
                sp = st_ref[gg, p]
                spb = sp.astype(BF16)
                dsn = ds_ref[gg, p]
                dsnb = dsn.astype(BF16)
                yds, dxds, cds = [], [], []
                for q in range(2):
                    hh = 2 * p + q
                    a_col = acg[:, hh:hh + 1]
                    a_row = act_ref[pl.ds(4 * g + hh, 1), :]
                    dec = jnp.exp(jnp.where(causal, a_col - a_row, NEG))
                    w = (cb * dec).astype(BF16)
                    head = (lane128 < 64) if q == 0 else (lane128 >= 64)
                    dym = jnp.where(head, dyb_p, jnp.zeros_like(dyb_p))
                    dw = _dot(dym, xdb_p, 1, 1)
                    dcb = dcb + dw * dec
                    yds.append(_dot(w, xdb_p))
                    dxds.append(_dot(w, dyb_p, 0, 0))
                    cds.append(jnp.exp(ac_last[:, hh:hh + 1]))
                y_diag = jnp.where(lane128 < 64, yds[0], yds[1])
                dxd_diag = jnp.where(lane128 < 64, dxds[0], dxds[1])
                y_off = _dot(cbf, spb, 1, 1) * e_p
                dgp = dy_p * e_p
                dgb = dgp.astype(BF16)
                dc_acc = dc_acc + _dot(dgb, spb)
                dsp = _dot(dgb, cbf, 0, 0)
                cd_col = jnp.where(row128[:, 0:1] < 64, cds[0], cds[1])
                qm = _dot(bbf, dsnb, 1, 1)
                dxd_state = dte_p * qm
                db_acc = db_acc + _dot((xd_p * dte_p).astype(BF16), dsnb)
                t_p = xd_p * dxd_state
                prod = dsn * sp
                e0 = jnp.sum(jnp.sum(jnp.where(row128 < 64, prod, 0.0), axis=1, keepdims=True), axis=0, keepdims=True)
                e1 = jnp.sum(jnp.sum(jnp.where(row128 >= 64, prod, 0.0), axis=1, keepdims=True), axis=0, keepdims=True)
                tcol = jnp.sum(t_p, axis=0, keepdims=True)
                lane1 = _iota((1, 128), 1)
                t0 = jnp.sum(jnp.where(lane1 < 64, tcol, 0.0), axis=1, keepdims=True)
                t1 = jnp.sum(jnp.where(lane1 >= 64, tcol, 0.0), axis=1, keepdims=True)
                ends.append(e0 * cds[0] + t0)
                ends.append(e1 * cds[1] + t1)
                ds_ref[gg, p] = dsn * cd_col + dsp
                u_parts.append(dyb_p.astype(F32) * y_diag - xdb_p.astype(F32) * dxd_diag + dy_p * y_off - t_p)
                dxd_parts.append(dxd_diag + dxd_state)
            dxd = jnp.concatenate(dxd_parts, axis=1)
            u_all = jnp.concatenate(u_parts, axis=1)
            dx_ref[:, 256 * gg:256 * (gg + 1)] = dxd * dt4 + dyv * d_ref[:, 256 * gg:256 * (gg + 1)]
            dcbb = dcb.astype(BF16)
            dc_ref[:, 128 * gg:128 * (gg + 1)] = dc_acc + _dot(dcbb, bbf)
            db_ref[:, 128 * gg:128 * (gg + 1)] = db_acc + _dot(dcbb, cbf, 0, 0)
            lane = _iota((Lc, 128), 1)
            endv = jnp.zeros((Lc, 128), F32)
            for hh in range(4):
                endv = jnp.where(lane == 8 + hh, ends[hh], endv)
            dh_ref[gg] = _dot3(dxd * xv, ind_a) + _dot3(u_all, ind_b) + endv

    G = SSD_GPS
    rev = lambda c: nc - 1 - c
    per_g = lambda g, c: (g, rev(c), 0)
    return pl.pallas_call(
        body, name=name, grid=(SSM_GROUPS // G, nc),
        in_specs=[pl.BlockSpec((Lc, 256 * G), lambda g, c: (rev(c), g)),
                  pl.BlockSpec((Lc, 128 * G), lambda g, c: (rev(c), 16 // G + g)),
                  pl.BlockSpec((Lc, 128 * G), lambda g, c: (rev(c), 24 // G + g)),
                  pl.BlockSpec((G, Lc, 128), per_g), pl.BlockSpec((G, Lc, 128), per_g),
                  pl.BlockSpec((SSM_HEADS, Lc), lambda g, c: (0, rev(c))),
                  pl.BlockSpec((G, None, 2, 128, 128), lambda g, c: (g, rev(c), 0, 0, 0)),
                  pl.BlockSpec((Lc, 256 * G), lambda g, c: (rev(c), g)),
                  pl.BlockSpec((1, 256 * G), lambda g, c: (0, g))],
        out_specs=[pl.BlockSpec((Lc, 256 * G), lambda g, c: (rev(c), g)),
                   pl.BlockSpec((Lc, 128 * G), lambda g, c: (rev(c), g)),
                   pl.BlockSpec((Lc, 128 * G), lambda g, c: (rev(c), g)),
                   pl.BlockSpec((G, Lc, 128), per_g)],
        out_shape=[jax.ShapeDtypeStruct((S, 2048), F32), jax.ShapeDtypeStruct((S, 1024), F32),
                   jax.ShapeDtypeStruct((S, 1024), F32), jax.ShapeDtypeStruct((SSM_GROUPS, S, 128), F32)],
        scratch_shapes=[pltpu.VMEM((G, 2, 128, 128), F32)], compiler_params=_cp(2),
    )(xbc, xbc, xbc, dt_g, ac_g, ac_t, states, dy, dexp)


def _ssd_post(dhead, dt_g, sg_g, alog_g, *, name="ssd_post"):
    S = dhead.shape[1]
    nc = S // CHUNK
    Lc = CHUNK

    def body(dh_ref, dt_ref, sg_ref, al_ref, o_ref, s_ref):
        @pl.when(pl.program_id(0) == 0)
        def _():
            s_ref[...] = jnp.zeros_like(s_ref)

        lane = _iota((Lc, 128), 1)
        row = _iota((Lc, 128), 0)
        row8 = _iota((8, 128), 0)
        out = jnp.zeros((Lc, 128), F32)
        for g in range(SSM_GROUPS):
            dh = dh_ref[g]
            a_neg = -jnp.exp(al_ref[g])
            dac = jnp.where(lane < 4, pltpu.roll(dh, 124, axis=1), 0.0)
            end = jnp.where(lane < 4, pltpu.roll(dh, 120, axis=1), 0.0)
            k = 1
            while k < Lc:
                dac = dac + jnp.where(row < Lc - k, pltpu.roll(dac, Lc - k, axis=0), 0.0)
                k *= 2
            da = dac + end
            ddt = jnp.where(lane < 4, da * a_neg + dh, 0.0)
            ddtr = ddt * sg_ref[g]
            out = out + (ddtr if g == 0 else pltpu.roll(ddtr, 4 * g, axis=1))
            dal = jnp.sum(da * dt_ref[g], axis=0, keepdims=True) * a_neg
            dbias = jnp.sum(ddtr, axis=0, keepdims=True)
            part = jnp.where(row8 == 0, dal, jnp.where(row8 == 1, dbias, 0.0))
            s_ref[g] += part
        o_ref[...] = out.astype(BF16)

    grp = pl.BlockSpec((SSM_GROUPS, Lc, 128), lambda c: (0, c, 0))
    whole = lambda r: pl.BlockSpec((SSM_GROUPS, r, 128), lambda c: (0, 0, 0))
    return pl.pallas_call(
        body, name=name, grid=(nc,), in_specs=[grp, grp, grp, whole(1)],
        out_specs=[pl.BlockSpec((Lc, 128), lambda c: (c, 0)), whole(8)],
        out_shape=[jax.ShapeDtypeStruct((S, 128), BF16), jax.ShapeDtypeStruct((SSM_GROUPS, 8, 128), F32)],
        compiler_params=_cp(1),
    )(dhead, dt_g, sg_g, alog_g)


def _gate_fwd(y, xbc, zx, dexp, gn, *, name="gate_fwd", tr=256):
    S = y.shape[0]
    W = 2048
    gw = W // SSM_GROUPS

    def body(y_ref, x_ref, z_ref, d_ref, g_ref, o_ref):
        z = z_ref[...]
        u = (y_ref[...] + x_ref[...] * d_ref[...]) * (z * _sigmoid(z))
        gv = g_ref[...]
        for q in range(SSM_GROUPS):
            sl = slice(gw * q, gw * (q + 1))
            uq = u[:, sl]
            r = lax.rsqrt(jnp.mean(uq * uq, axis=-1, keepdims=True) + EPS)
            o_ref[:, sl] = (uq * r * gv[:, sl]).astype(BF16)

    row = pl.BlockSpec((tr, W), lambda i: (i, 0))
    vec = pl.BlockSpec((1, W), lambda i: (0, 0))
    return pl.pallas_call(
        body, name=name, grid=(S // tr,), in_specs=[row, row, row, vec, vec], out_specs=row,
        out_shape=jax.ShapeDtypeStruct((S, W), BF16), compiler_params=_cp(1),
    )(y, xbc, zx, dexp, gn)


def _gate_bwd(y, xbc, zx, dexp, gn, dout, *, name="gate_bwd", tr=256):
    S = y.shape[0]
    W = 2048
    gw = W // SSM_GROUPS
    steps = S // tr

    def body(y_ref, x_ref, z_ref, d_ref, g_ref, do_ref, dy_ref, dz_ref, dg_ref, dd_ref, acc_ref):
        i = pl.program_id(0)

        @pl.when(i == 0)
        def _():
            acc_ref[...] = jnp.zeros_like(acc_ref)

        z = z_ref[...]
        sg = _sigmoid(z)
        sz = z * sg
        xs = x_ref[...]
        yt = y_ref[...] + xs * d_ref[...]
        u = yt * sz
        gv = g_ref[...]
        do = do_ref[...]
        dgs = []
        for q in range(SSM_GROUPS):
            sl = slice(gw * q, gw * (q + 1))
            uq = u[:, sl]
            r = lax.rsqrt(jnp.mean(uq * uq, axis=-1, keepdims=True) + EPS)
            uh = uq * r
            dq = do[:, sl]
            duh = dq * gv[:, sl]
            duq = r * (duh - uh * jnp.mean(duh * uh, axis=-1, keepdims=True))
            dgs.append(jnp.sum(dq * uh, axis=0, keepdims=True))
            dyt = duq * sz[:, sl]
            dy_ref[:, sl] = dyt
            dz_ref[:, sl] = (duq * yt[:, sl] * (sg[:, sl] * (1.0 + z[:, sl] * (1.0 - sg[:, sl])))).astype(BF16)
            acc_ref[:, sl] += jnp.sum(dyt * xs[:, sl], axis=0, keepdims=True)
        dg = jnp.concatenate(dgs, axis=1)

        @pl.when(i == 0)
        def _():
            dg_ref[...] = dg

        @pl.when(i > 0)
        def _():
            dg_ref[...] += dg

        @pl.when(i == steps - 1)
        def _():
            ind = ((_iota((W, 128), 0) >> 6) == _iota((W, 128), 1)).astype(BF16)
            dd_ref[...] = _dot3(jnp.broadcast_to(acc_ref[...], (8, W)), ind)[0:1, :]

    row = pl.BlockSpec((tr, W), lambda i: (i, 0))
    vec = pl.BlockSpec((1, W), lambda i: (0, 0))
    return pl.pallas_call(
        body, name=name, grid=(steps,), in_specs=[row, row, row, vec, vec, row],
        out_specs=[row, row, vec, pl.BlockSpec((1, 128), lambda i: (0, 0))],
        out_shape=[jax.ShapeDtypeStruct((S, W), F32), jax.ShapeDtypeStruct((S, W), BF16),
                   jax.ShapeDtypeStruct((1, W), F32), jax.ShapeDtypeStruct((1, 128), F32)],
        scratch_shapes=[pltpu.VMEM((1, W), F32)], compiler_params=_cp(1),
    )(y, xbc, zx, dexp, gn, dout)


def _rope_cs(posf, *, name="rope_tables", tr=256):
    S = posf.shape[0]

    def body(p_ref, c_ref, s_ref):
        j = (_iota((tr, 128), 1) & 31).astype(F32)
        ang = p_ref[...] * jnp.exp(j * (-math.log(ROPE_THETA) / 32.0))
        c_ref[...] = jnp.cos(ang)
        s_ref[...] = jnp.sin(ang)

    blk = pl.BlockSpec((tr, 128), lambda i: (i, 0))
    return pl.pallas_call(
        body, name=name, grid=(S // tr,), in_specs=[pl.BlockSpec((tr, 1), lambda i: (i, 0))], out_specs=[blk, blk],
        out_shape=[jax.ShapeDtypeStruct((S, 128), F32)] * 2, compiler_params=_cp(1),
    )(posf)


def _rope_tables(c_ref, s_ref, shape):
    reps = shape[1] // 128
    return jnp.tile(c_ref[...], (1, reps)), jnp.tile(s_ref[...], (1, reps)), (_iota(shape, 1) & 63) < 32


def _hn_inds(W):
    ind = ((_iota((W, 128), 0) >> 6) == _iota((W, 128), 1)).astype(BF16)
    ind_t = ((_iota((128, W), 1) >> 6) == _iota((128, W), 0)).astype(BF16)
    return ind, ind_t


def _hnrope_fwd(xin, col0, W, gain_w, rope, *, name, tr=256):
    S = xin.shape[0]
    off = col0 // W
    nh = W // HEAD

    def body(x_ref, g_ref, c_ref, s_ref, o_ref):
        x = x_ref[...]
        ind, ind_t = _hn_inds(W)
        r = lax.rsqrt(_dot3(x * x, ind) * (1.0 / HEAD) + EPS)
        xn = x * _dot3(r, ind_t) * g_ref[...]
        cs, sn, half = _rope_tables(c_ref, s_ref, (tr, W))
        rot = jnp.where(half, -pltpu.roll(xn, W - 32, axis=1), pltpu.roll(xn, 32, axis=1))
        out = (xn * cs + rot * sn).astype(BF16)
        for h in range(nh):
            o_ref[h] = out[:, HEAD * h:HEAD * (h + 1)]

    tab = pl.BlockSpec((tr, 128), lambda i: (i, 0))
    return pl.pallas_call(
        body, name=name, grid=(S // tr,),
        in_specs=[pl.BlockSpec((tr, W), lambda i: (i, off)), pl.BlockSpec((1, W), lambda i: (0, 0)), tab, tab],
        out_specs=pl.BlockSpec((nh, tr, HEAD), lambda i: (0, i, 0)), out_shape=jax.ShapeDtypeStruct((nh, S, HEAD), BF16),
        compiler_params=_cp(1),
    )(xin, gain_w, *rope)


def _hnrope_bwd(xin, col0, W, gain_w, rope, dout, *, name, tr=256):
    S = xin.shape[0]
    off = col0 // W
    steps = S // tr
    nh = W // HEAD

    def body(x_ref, g_ref, c_ref, s_ref, do_ref, dx_ref, cs_ref, dg_ref, acc_ref):
        i = pl.program_id(0)
        x = x_ref[...]
        ind, ind_t = _hn_inds(W)
        r = lax.rsqrt(_dot3(x * x, ind) * (1.0 / HEAD) + EPS)
        rw = _dot3(r, ind_t)
        xh = x * rw
        cs, sn, half = _rope_tables(c_ref, s_ref, (tr, W))
        do = jnp.concatenate([do_ref[h] for h in range(nh)], axis=1).astype(F32)
        gs = do * sn
        g1 = do * cs + jnp.where(half, pltpu.roll(gs, W - 32, axis=1), -pltpu.roll(gs, 32, axis=1))
        dxh = g1 * g_ref[...]
        t = _dot3(dxh * xh, ind) * (1.0 / HEAD)
        dx = rw * (dxh - xh * _dot3(t, ind_t))
        dx_ref[...] = dx.astype(BF16)
        cpart = jnp.sum(dx, axis=0, keepdims=True)
        gpart = jnp.sum(g1 * xh, axis=0, keepdims=True)

        @pl.when(i == 0)
        def _():
            cs_ref[...] = cpart
            acc_ref[...] = gpart

        @pl.when(i > 0)
        def _():
            cs_ref[...] += cpart
            acc_ref[...] += gpart

        @pl.when(i == steps - 1)
        def _():
            fold = ((_iota((W, 128), 0) & 63) == _iota((W, 128), 1)).astype(BF16)
            dg_ref[...] = _dot3(jnp.broadcast_to(acc_ref[...], (8, W)), fold)[0:1, :]

    tab = pl.BlockSpec((tr, 128), lambda i: (i, 0))
    return pl.pallas_call(
        body, name=name, grid=(steps,),
        in_specs=[pl.BlockSpec((tr, W), lambda i: (i, off)), pl.BlockSpec((1, W), lambda i: (0, 0)), tab, tab,
                  pl.BlockSpec((nh, tr, HEAD), lambda i: (0, i, 0))],
        out_specs=[pl.BlockSpec((tr, W), lambda i: (i, 0)), pl.BlockSpec((1, W), lambda i: (0, 0)),
                   pl.BlockSpec((1, 128), lambda i: (0, 0))],
        out_shape=[jax.ShapeDtypeStruct((S, W), BF16), jax.ShapeDtypeStruct((1, W), F32),
                   jax.ShapeDtypeStruct((1, 128), F32)],
        scratch_shapes=[pltpu.VMEM((1, W), F32)], compiler_params=_cp(1),
    )(xin, gain_w, *rope, dout)


def _attn_band():
    qi = jnp.arange(ATT_G * WINDOW)[:, None] % WINDOW
    ki = jnp.arange(2 * WINDOW)[None, :]
    rel = qi + WINDOW - ki
    ok = (rel >= 0) & (rel < WINDOW)
    return jnp.stack([jnp.where(ok & (ki >= WINDOW), 0.0, NEG), jnp.where(ok, 0.0, NEG)]).astype(F32)


def _attn_probs(q, kb, sink_ref, band_ref, h, i):
    s = _dot(q, kb, 1, 1) * (HEAD ** -0.5) + band_ref[jnp.minimum(i, 1)]
    r1 = _iota((4 * WINDOW, 1), 0)
    sink = jnp.where(r1 < WINDOW, sink_ref[4 * h], jnp.where(r1 < 2 * WINDOW, sink_ref[4 * h + 1],
                     jnp.where(r1 < 3 * WINDOW, sink_ref[4 * h + 2], sink_ref[4 * h + 3])))
    m = jnp.maximum(jnp.max(s, axis=1, keepdims=True), sink)
    p = jnp.exp(s - m)
    ps = jnp.exp(sink - m)
    inv = 1.0 / (jnp.sum(p, axis=1, keepdims=True) + ps)
    return p * inv, ps * inv


ATT_HPS = 2
_BAND = pl.BlockSpec((2, ATT_G * WINDOW, 2 * WINDOW), lambda h, i: (0, 0, 0))


def _attn_specs(S):
    qspec = pl.BlockSpec((ATT_HPS, ATT_G, WINDOW, HEAD), lambda h, i: (h, 0, i, 0))
    cur = pl.BlockSpec((ATT_HPS, WINDOW, HEAD), lambda h, i: (h, i, 0))
    prev = pl.BlockSpec((ATT_HPS, WINDOW, HEAD), lambda h, i: (h, jnp.maximum(i - 1, 0), 0))
    tok = pl.BlockSpec((WINDOW, ATT_HPS * ATT_G * HEAD), lambda h, i: (i, h))
    return qspec, cur, prev, tok


def _attn_fwd(qh, kh, vh, sinks, *, name="attn_fwd"):
    S = kh.shape[1]
    nb = S // WINDOW

    def body(s_ref, band_ref, q_ref, kc_ref, kp_ref, vc_ref, vp_ref, o_ref):
        h2, i = pl.program_id(0), pl.program_id(1)
        outs = []
        for hh in range(ATT_HPS):
            q = q_ref[hh].reshape(ATT_G * WINDOW, HEAD)
            kb = jnp.concatenate([kp_ref[hh], kc_ref[hh]], axis=0)
            vb = jnp.concatenate([vp_ref[hh], vc_ref[hh]], axis=0)
            probs, _ = _attn_probs(q, kb, s_ref, band_ref, ATT_HPS * h2 + hh, i)
            o = _dot(probs.astype(BF16), vb).astype(BF16)
            outs += [o[WINDOW * g:WINDOW * (g + 1)] for g in range(ATT_G)]
        o_ref[...] = jnp.concatenate(outs, axis=1)

    qspec, cur, prev, tok = _attn_specs(S)
    return pl.pallas_call(
        body, name=name, grid=(ATT_KV // ATT_HPS, nb),
        in_specs=[pl.BlockSpec(memory_space=pltpu.SMEM), _BAND, qspec, cur, prev, cur, prev], out_specs=tok,
        out_shape=jax.ShapeDtypeStruct((S, ATT_KV * ATT_G * HEAD), BF16), compiler_params=_cp(2),
    )(sinks, _attn_band(), qh, kh, kh, vh, vh)


def _attn_bwd(qh, kh, vh, sinks, doh, *, name="attn_bwd"):
    S = kh.shape[1]
    nb = S // WINDOW

    def body(s_ref, band_ref, q_ref, kc_ref, kp_ref, vc_ref, vp_ref, do_ref, dq_ref, dk_ref, dv_ref, dsk_ref):
        h2, i = pl.program_id(0), pl.program_id(1)

        @pl.when(i == 0)
        def _():
            dk_ref[...] = jnp.zeros_like(dk_ref)
            dv_ref[...] = jnp.zeros_like(dv_ref)
            dsk_ref[...] = jnp.zeros_like(dsk_ref)

        dov = do_ref[...]
        cur = pl.multiple_of(i * WINDOW, WINDOW)
        lane = _iota((8, 128), 1)
        row = _iota((8, 128), 0)
        scale = HEAD ** -0.5
        for hh in range(ATT_HPS):
            q = q_ref[hh].reshape(ATT_G * WINDOW, HEAD)
            do = jnp.concatenate([dov[:, HEAD * (ATT_G * hh + g):HEAD * (ATT_G * hh + g + 1)] for g in range(ATT_G)], axis=0)
            kb = jnp.concatenate([kp_ref[hh], kc_ref[hh]], axis=0)
            vb = jnp.concatenate([vp_ref[hh], vc_ref[hh]], axis=0)
            probs, psink = _attn_probs(q, kb, s_ref, band_ref, ATT_HPS * h2 + hh, i)
            dp = _dot(do, vb, 1, 1)
            delta = jnp.sum(probs * dp, axis=1, keepdims=True)
            ds = (probs * (dp - delta)).astype(BF16)
            dq_ref[hh] = (_dot(ds, kb) * scale).reshape(ATT_G, WINDOW, HEAD)
            dkb = _dot(ds, q, 0, 0) * scale
            dvb = _dot(probs.astype(BF16), do, 0, 0)
            dk_ref[hh, pl.ds(cur, WINDOW), :] += dkb[WINDOW:, :]
            dv_ref[hh, pl.ds(cur, WINDOW), :] += dvb[WINDOW:, :]
            prv = pl.multiple_of(jnp.maximum(i - 1, 0) * WINDOW, WINDOW)
            dk_ref[hh, pl.ds(prv, WINDOW), :] += dkb[:WINDOW, :]
            dv_ref[hh, pl.ds(prv, WINDOW), :] += dvb[:WINDOW, :]

            dsr = -psink * delta
            upd = jnp.zeros((8, 128), F32)
            for gq in range(ATT_G):
                v = jnp.sum(dsr[gq * WINDOW:(gq + 1) * WINDOW, :], axis=0, keepdims=True)
                upd = jnp.where((lane == gq) & (row == 0), v, upd)
            dsk_ref[hh] += upd

    qspec, cur, prev, tok = _attn_specs(S)
    full = pl.BlockSpec((ATT_HPS, S, HEAD), lambda h, i: (h, 0, 0))
    return pl.pallas_call(
        body, name=name, grid=(ATT_KV // ATT_HPS, nb),
        in_specs=[pl.BlockSpec(memory_space=pltpu.SMEM), _BAND, qspec, cur, prev, cur, prev, tok],
        out_specs=[qspec, full, full, pl.BlockSpec((ATT_HPS, 8, 128), lambda h, i: (h, 0, 0))],
        out_shape=[jax.ShapeDtypeStruct((ATT_KV, ATT_G, S, HEAD), F32), jax.ShapeDtypeStruct((ATT_KV, S, HEAD), F32),
                   jax.ShapeDtypeStruct((ATT_KV, S, HEAD), F32), jax.ShapeDtypeStruct((ATT_KV, 8, 128), F32)],
        compiler_params=_cp(2),
    )(sinks, _attn_band(), qh, kh, kh, vh, vh, doh)


def _heads_major(t, nh):
    S = t.shape[0]
    return t.reshape(S, nh, HEAD).transpose(1, 0, 2)


def _tokens_major(t):
    nh, S, _ = t.shape
    return t.transpose(1, 0, 2).reshape(S, nh * HEAD)


class _NoComm:
    def late_weights(self, w, after):
        return w

    def grads(self, group, tensors, after):
        return None


def _local_step(x, posf, target, w, comm=None):
    S, D = x.shape
    gr = {}
    comm = comm or _NoComm()

    (h1,) = _rms_fwd(x, [w["a_norm"]], name="a_norm_f", dep=w.get("dep"))
    zx = _mm(h1, w["w_zx"], name="in_proj_zx")
    dtr = _mm(h1, w["w_dt"], name="in_proj_dt")
    xbc = _conv_silu_fwd(zx, 2048, 4096, w["a_conv_w"], w["a_conv_b"], name="a_conv_f")
    dt_g, ac_g, sg_g, ac_t = _ssd_prep(dtr, w["a_dt_bias"], w["a_A_log"])
    y_ssd, states = _ssd_fwd(xbc, dt_g, ac_g, ac_t)
    yg = _gate_fwd(y_ssd, xbc, zx, w["a_Dexp"], w["a_gnorm"])
    x1 = _mm(yg, w["a_out_proj"], res=x, name="out_proj")

    w = comm.late_weights(w, x1)
    FW = w["f_w_in"][0].shape[2]

    def ffn_fwd(xin, l):
        (h,) = _rms_fwd(xin, [w["f_norm"][l]], name=f"f_norm_f{l}")
        u = _mm(h, w["f_w_in"][l], name=f"f_in{l}", dims=(S, N_CHIPS * FW, D), tn=FW,
                b_spec=pl.BlockSpec((None, D, FW), lambda i, j, k: (j, 0, 0)))
        a = _ffn_act_fwd(u, w["f_conv_w"][l], w["f_conv_b"][l], name=f"f_act_f{l}")
        xo = _mm(a, w["f_w_down"][l], res=xin, tk=a.shape[1], name=f"f_down{l}")
        return xo, (h, u)

    x2, ffn0 = ffn_fwd(x1, 0)

    hk, hq = _rms_fwd(x2, [w["kv_norm"], w["b_norm"]], name="kvq_norm_f")
    kv = _mm(hk, w["w_kv"], bias=w["b_kv"], name="kv_proj")
    q = _mm(hq, w["w_q"], bias=w["b_q"], name="q_proj")
    rope = _rope_cs(posf)
    kr = _hnrope_fwd(kv, 0, 256, w["k_norm_w"], rope, name="k_rope_f")
    qr = _hnrope_fwd(q, 0, 1024, w["q_norm_w"], rope, name="q_rope_f")
    qh = qr.reshape(ATT_KV, ATT_G, S, HEAD)
    kh = kr
    vh = _heads_major(kv[:, 256:].astype(BF16), ATT_KV)
    att = _attn_fwd(qh, kh, vh, w["sinks"])
    x3 = _mm(att, w["w_o"], bias=w["b_o"], res=x2, name="o_proj")
    x4, ffn1 = ffn_fwd(x3, 1)

    dy, loss_part = _loss(x4, target)

    def ffn_bwd(xin, l, saved, dyo, want_colsum, dep=None):
        h, u = saved
        da = _mm(dyo, w["f_w_down"][l], tb=True, name=f"f_down_dx{l}", dep=dep)
        du, dcw, dcb, a = _ffn_act_bwd(u, w["f_conv_w"][l], w["f_conv_b"][l], da, name=f"f_act_b{l}")
        dw_down = _mm(a, dyo, ta=True, out_dtype=BF16, name=f"f_down_dw{l}")
        dw_in = _mm(h, du, ta=True, out_dtype=BF16, name=f"f_in_dw{l}", dims=(D, N_CHIPS * FW, S), tm=D, tn=FW, tk=S,
                    b_spec=pl.BlockSpec((None, S, FW), lambda i, j, k: (j // 2, 0, j % 2)),
                    o_spec=pl.BlockSpec((None, D, FW), lambda i, j, k: (j, i, 0)), o_shape=(N_CHIPS, D, FW))
        ts = _pick(S, (1024, 512, 256))
        pieces = [(pl.BlockSpec((None, ts, FW), lambda i, j, k, q=q: (q // 2, i, q % 2)),
                   pl.BlockSpec((None, 512, FW), lambda i, j, k, q=q: (q, j, 0))) for q in range(N_CHIPS)]
        dh = _mm(du, w["f_w_in"][l], tb=True, name=f"f_in_dx{l}", dims=(S, D, FW), tm=ts, tn=512, tk=FW,
                 a_spec=pieces[0][0], b_spec=pieces[0][1], more=pieces[1:])
        outs = _rms_bwd(xin, [w["f_norm"][l]], [dh], dyo, name=f"f_norm_b{l}", want_colsum=want_colsum)
        g = dict(f_norm=outs[1], f_w_in=dw_in, f_conv_w=dcw, f_conv_b=dcb, f_w_down=dw_down)
        return outs[0], g, (outs[2] if want_colsum else None)

    dx3, gr["ffn1"], db_o = ffn_bwd(x3, 1, ffn1, dy, True)
    gr["b_o"] = db_o
    gr["w_o"] = _mm(att, dx3, ta=True, out_dtype=BF16, name="o_proj_dw")
    datt = _mm(dx3, w["w_o"], tb=True, out_dtype=BF16, name="o_proj_dx")
    dqh, dkh, dvh, dsk = _attn_bwd(qh, kh, vh, w["sinks"], datt)
    gr["sinks"] = dsk[:, 0, :4].reshape(1, 16)
    dv = _tokens_major(dvh).astype(BF16)
    dq, db_q, dqn = _hnrope_bwd(q, 0, 1024, w["q_norm_w"], rope, dqh.reshape(16, S, HEAD), name="q_rope_b")
    dk, db_k, dkn = _hnrope_bwd(kv, 0, 256, w["k_norm_w"], rope, dkh, name="k_rope_b")
    gr["q_norm"], gr["k_norm"] = dqn[:, :HEAD], dkn[:, :HEAD]
    gr["b_q"] = db_q
    gr["b_kv"] = jnp.concatenate([db_k, _colsum(dv, name="dv_colsum")], axis=1)
    dkv = jnp.concatenate([dk, dv], axis=1)
    gr["w_q"] = _mm(hq, dq, ta=True, out_dtype=BF16, name="q_proj_dw")
    gr["w_kv"] = _mm(hk, dkv, ta=True, out_dtype=BF16, name="kv_proj_dw")
    tok = comm.grads(1, dict(f_down1=gr["ffn1"]["f_w_down"], f_in1=gr["ffn1"]["f_w_in"], w_o=gr["w_o"], w_q=gr["w_q"],
                             w_kv=gr["w_kv"]), None)
    dhq = _mm(dq, w["w_q"], tb=True, name="q_proj_dx", dep=tok)
    dhk = _mm(dkv, w["w_kv"], tb=True, name="kv_proj_dx")
    dx2, gr["kv_norm"], gr["b_norm"] = _rms_bwd(x2, [w["kv_norm"], w["b_norm"]], [dhk, dhq], dx3, name="kvq_norm_b")

    dx1, gr["ffn0"], _ = ffn_bwd(x1, 0, ffn0, dx2, False)

    gr["a_out_proj"] = _mm(yg, dx1, ta=True, out_dtype=BF16, name="out_proj_dw")
    tok = comm.grads(2, dict(f_down0=gr["ffn0"]["f_w_down"], f_in0=gr["ffn0"]["f_w_in"], out_proj=gr["a_out_proj"]), dx1)
    dyg = _mm(dx1, w["a_out_proj"], tb=True, name="out_proj_dx", dep=tok)
    dy_ssd, dz, gr["a_gnorm"], dD = _gate_bwd(y_ssd, xbc, zx, w["a_Dexp"], w["a_gnorm"], dyg)
    gr["a_D"] = dD[:, :SSM_HEADS]
    dxs, dB, dC, dhead = _ssd_bwd(xbc, dt_g, ac_g, ac_t, states, dy_ssd, w["a_Dexp"])
    ddtr, dsmall = _ssd_post(dhead, dt_g, sg_g, w["a_A_log_g"])
    gr["a_A_log"] = dsmall[:, 0, :4].reshape(1, SSM_HEADS)
    gr["a_dt_bias"] = dsmall[:, 1, :4].reshape(1, SSM_HEADS)
    dxbc, gr["a_conv_w"], gr["a_conv_b"] = _conv_silu_bwd(
        zx, 2048, 4096, w["a_conv_w"], w["a_conv_b"], [(dxs, 0), (dB, 2048), (dC, 3072)], name="a_conv_b")
    gr["w_z"] = _mm(h1, dz, ta=True, out_dtype=BF16, name="in_proj_dwz")
    gr["w_x"] = _mm(h1, dxbc, ta=True, out_dtype=BF16, name="in_proj_dwx")
    gr["w_dt"] = _mm(h1, ddtr, ta=True, out_dtype=BF16, name="in_proj_dwdt")
    dh1 = _mm(dz, w["w_zx"], tb=True, name="in_proj_dxz")
    ts = _pick(S, (1024, 512, 256))
    halves = [(pl.BlockSpec((ts, 2048), lambda i, j, k, q=q: (i, q)), pl.BlockSpec((512, 2048), lambda i, j, k, q=q: (j, 1 + q)))
              for q in range(2)]
    dh1 = _mm(dxbc, w["w_zx"], tb=True, res=dh1, name="in_proj_dxx", dims=(S, D, 2048), tm=ts, tn=512, tk=2048,
              a_spec=halves[0][0], b_spec=halves[0][1], more=halves[1:])
    dh1 = _mm(ddtr, w["w_dt"], tb=True, res=dh1, name="in_proj_dxdt")
    dx0, gr["a_norm"] = _rms_bwd(x, [w["a_norm"]], [dh1], dx1, name="a_norm_b")
    comm.grads(3, dict(in_proj=_in_proj_grad(gr).reshape(D, N_CHIPS, -1).transpose(1, 0, 2)), dx0)
    return loss_part, dx0, gr


def _prep_small(full, w):
    w["a_norm"] = full["a_norm"]
    w["a_conv_w"] = full["a_conv_w"][0]
    w["a_conv_b"] = full["a_conv_b"]
    pad32 = lambda v: jnp.pad(v, ((0, 0), (0, 128 - SSM_HEADS)))
    w["a_dt_bias"] = pad32(full["a_dt_bias"])
    w["a_A_log"] = pad32(full["a_A_log"])
    w["a_A_log_g"] = jnp.pad(full["a_A_log"].reshape(SSM_GROUPS, 1, 4), ((0, 0), (0, 0), (0, 124)))
    w["a_Dexp"] = jnp.repeat(full["a_D"], HEAD, axis=1)
    w["a_gnorm"] = full["a_gnorm"]
    w["f_norm"] = [full["f_norm"][l:l + 1] for l in range(2)]
    w["f_conv_w"] = [full["f_conv_w"][l] for l in range(2)]
    w["f_conv_b"] = [full["f_conv_b"][l:l + 1] for l in range(2)]
    w["kv_norm"] = full["kv_norm"].reshape(1, -1)
    w["b_kv"] = full["b_kv"].reshape(1, -1)
    w["k_norm_w"] = jnp.tile(full["k_norm"].reshape(1, HEAD), (1, ATT_KV))
    w["b_norm"] = full["b_norm"]
    w["b_q"] = full["b_q"]
    w["q_norm_w"] = jnp.tile(full["q_norm"], (1, ATT_KV * ATT_G))
    w["sinks"] = full["sinks"].reshape(-1)
    w["b_o"] = full["b_o"]
    return w


def _split_in_proj(ip):
    return ip[:, :6144].astype(BF16), jnp.pad(ip[:, 6144:], ((0, 0), (0, 128 - SSM_HEADS))).astype(BF16)


def _prep_weights(full):
    w = _prep_small(full, {})
    w["w_zx"], w["w_dt"] = _split_in_proj(full["a_in_proj"][0])
    w["a_out_proj"] = full["a_out_proj"][0].astype(BF16)
    w["f_w_in"] = [full["f_w_in"][l].reshape(1024, N_CHIPS, -1).transpose(1, 0, 2).astype(BF16) for l in range(2)]
    w["f_w_down"] = [full["f_w_down"][l].astype(BF16) for l in range(2)]
    w["w_kv"] = full["w_kv"].astype(BF16)
    w["w_q"] = full["w_q"][0].astype(BF16)
    w["w_o"] = full["w_o"][0].astype(BF16)
    return w


def _small_grads(gr):
    g = {}
    g["a_norm"] = gr["a_norm"]
    g["a_conv_w"] = gr["a_conv_w"][None]
    g["a_conv_b"] = gr["a_conv_b"]
    g["a_dt_bias"], g["a_A_log"], g["a_D"] = gr["a_dt_bias"], gr["a_A_log"], gr["a_D"]
    g["a_gnorm"] = gr["a_gnorm"]
    g["kv_norm"] = gr["kv_norm"].reshape(-1)
    g["b_kv"] = gr["b_kv"].reshape(-1)
    g["k_norm"] = gr["k_norm"].reshape(-1)
    g["b_norm"] = gr["b_norm"]
    g["b_q"] = gr["b_q"]
    g["q_norm"] = gr["q_norm"]
    g["sinks"] = gr["sinks"]
    g["b_o"] = gr["b_o"]
    f = [gr["ffn0"], gr["ffn1"]]
    g["f_norm"] = jnp.concatenate([f[0]["f_norm"], f[1]["f_norm"]], axis=0)
    g["f_conv_w"] = jnp.stack([f[l]["f_conv_w"] for l in range(2)])
    g["f_conv_b"] = jnp.concatenate([f[l]["f_conv_b"] for l in range(2)], axis=0)
    return g


def _in_proj_grad(gr):
    return jnp.concatenate([gr["w_z"], gr["w_x"], gr["w_dt"][:, :SSM_HEADS]], axis=1)


def _full_grads(gr):
    g = _small_grads(gr)
    f32 = lambda t: t.astype(F32)
    g["a_in_proj"] = f32(_in_proj_grad(gr))[None]
    g["a_out_proj"] = f32(gr["a_out_proj"])[None]
    g["w_kv"] = f32(gr["w_kv"])
    g["w_q"] = f32(gr["w_q"])[None]
    g["w_o"] = f32(gr["w_o"])[None]
    f = [gr["ffn0"], gr["ffn1"]]
    g["f_w_in"] = jnp.stack([f32(f[l]["f_w_in"]).transpose(1, 0, 2).reshape(1024, -1) for l in range(2)])
    g["f_w_down"] = jnp.stack([f32(f[l]["f_w_down"]) for l in range(2)])
    return g


MESH = pl.DeviceIdType.MESH
WEIGHTS = ("a_norm", "a_in_proj", "a_conv_w", "a_conv_b", "a_dt_bias", "a_A_log", "a_D", "a_gnorm", "a_out_proj",
           "kv_norm", "w_kv", "b_kv", "k_norm", "b_norm", "w_q", "b_q", "q_norm", "sinks", "w_o", "b_o", "f_norm",
           "f_w_in", "f_conv_w", "f_conv_b", "f_w_down")
MATS = (("in_proj", "a_in_proj", 0), ("out_proj", "a_out_proj", 0), ("w_kv", "w_kv", None), ("w_q", "w_q", 0),
        ("w_o", "w_o", 0), ("f_in0", "f_w_in", 0), ("f_in1", "f_w_in", 1), ("f_down0", "f_w_down", 0),
        ("f_down1", "f_w_down", 1))
SMALL_CUT = (("a_norm", 1), ("a_conv_w", 2), ("a_conv_b", 1), ("a_gnorm", 1), ("f_conv_w", 2))
SMALL_REP = ("a_dt_bias", "a_A_log", "a_D", "kv_norm", "b_kv", "k_norm", "b_norm", "b_q", "q_norm", "sinks", "b_o",
             "f_norm", "f_conv_b")


def _coords():
    return lax.axis_index("x"), lax.axis_index("y"), lax.axis_index("c")


def _other_chips(x, y):
    return [(1 - x, y), (x, 1 - y), (1 - x, 1 - y)]


def _pack(arrs, rows_align, lanes, dtype):
    flat = jnp.concatenate([a.reshape(-1).astype(dtype) for a in arrs])
    per = rows_align * lanes
    total = -(-flat.shape[0] // per) * per
    return jnp.pad(flat, (0, total - flat.shape[0])).reshape(total // lanes, lanes)


def _unpack(flat, shapes):
    out, off = [], 0
    for s in shapes:
        n = math.prod(s)
        out.append(flat[off:off + n].reshape(s))
        off += n
    return out


def _remote(src, dst, send, recv, k, dev):
    return pltpu.make_async_remote_copy(src_ref=src, dst_ref=dst, send_sem=send.at[k], recv_sem=recv.at[k],
                                        device_id=dev, device_id_type=MESH)


_ANY = pl.BlockSpec(memory_space=pl.ANY)


def _halves(t):
    r, c = t.shape
    return t.reshape(2, r // 2, c)


def _gather_weights(shards, sp):
    n = len(shards)
    n_sem = 7 * n + 3

    def body(*refs):
        sh, sp_ref = refs[:n], refs[n]
        outs, sout = refs[n + 1:2 * n + 1], refs[2 * n + 1]
        send, recv, loc = refs[2 * n + 2:]
        x, y, c = _coords()
        me = 2 * x + y
        chips = _other_chips(x, y)
        sib = (x, y, 1 - c)
        l1 = pltpu.make_async_copy(sp_ref, sout.at[me], loc.at[0])
        l1.start()
        sends = []
        for j, (cx, cy) in enumerate(chips):
            sends.append(_remote(sp_ref, sout.at[me], send, recv, 7 * n + j, (cx, cy, c)))
            for t in range(n):
                sends.append(_remote(sh[t].at[c], outs[t].at[me, c], send, recv, 7 * t + j, (cx, cy, c)))
        for t in range(n):
            sends.append(_remote(sh[t], outs[t].at[me], send, recv, 7 * t + 6, sib))
        for cp in sends:
            cp.start()
        for j, (cx, cy) in enumerate(chips):
            src = 2 * cx + cy
            for t in range(n):
                _remote(sh[t].at[c], outs[t].at[src, c], send, recv, 7 * t + j, (cx, cy, c)).wait_recv()
                fwd = _remote(outs[t].at[src, c], outs[t].at[src, c], send, recv, 7 * t + 3 + j, sib)
                fwd.start()
                sends.append(fwd)
        for j, (cx, cy) in enumerate(chips):
            src = 2 * cx + cy
            _remote(sp_ref, sout.at[src], send, recv, 7 * n + j, (cx, cy, c)).wait_recv()
            for t in range(n):
                _remote(outs[t].at[src, 1 - c], outs[t].at[src, 1 - c], send, recv, 7 * t + 3 + j, sib).wait_recv()
        for t in range(n):
            _remote(sh[t], outs[t].at[me], send, recv, 7 * t + 6, sib).wait_recv()
        for cp in sends:
            cp.wait_send()
        l1.wait()

    res = pl.pallas_call(
        body, name="gather_weights", in_specs=[_ANY] * (n + 1), out_specs=[_ANY] * (n + 1),
        out_shape=[jax.ShapeDtypeStruct((N_CHIPS,) + t.shape, t.dtype) for t in shards]
        + [jax.ShapeDtypeStruct((N_CHIPS,) + sp.shape, sp.dtype)],
        scratch_shapes=[pltpu.SemaphoreType.DMA((n_sem,)), pltpu.SemaphoreType.DMA((n_sem,)),
                        pltpu.SemaphoreType.DMA((1,))],
    )(*shards, sp)
    return res[:n], res[n]


_HBM = pl.BlockSpec(memory_space=pltpu.HBM)
_SEMS = pl.BlockSpec(memory_space=pltpu.SEMAPHORE)
_DATAFLOW = pltpu.SideEffectType.DATAFLOW_SIDE_EFFECTING


def _in_hbm(a):
    return pltpu.with_memory_space_constraint(a, pltpu.HBM)


def _gather_copies(sh, land, send, recv):
    x, y, c = _coords()
    me = 2 * x + y
    out = []
    for t in range(len(sh)):
        for j, (cx, cy) in enumerate(_other_chips(x, y)):
            dev = (cx, cy, c)
            out.append((_remote(sh[t].at[c], land[t].at[me, c], send, recv, 4 * t + j, dev),
                        _remote(sh[t].at[c], land[t].at[2 * cx + cy, c], send, recv, 4 * t + j, dev)))
        sib = (x, y, 1 - c)
        out.append((_remote(sh[t], land[t].at[me], send, recv, 4 * t + 3, sib),
                    _remote(sh[t], land[t].at[me], send, recv, 4 * t + 3, sib)))
    return out


def _gather_start(shards, after, *, name):
    n = len(shards)

    def body(*refs):
        sh, land = refs[:n], refs[n:2 * n]
        send, recv = refs[2 * n + 1], refs[2 * n + 2]
        token = refs[-1]
        for mine, _ in _gather_copies(sh, land, send, recv):
            mine.start()
        token[...] = jnp.zeros_like(token)

    lands = [_in_hbm(lax.empty((N_CHIPS,) + s.shape, s.dtype)) for s in shards]
    res = pl.pallas_call(
        body, name=name, in_specs=[_HBM] * (2 * n) + [_ANY],
        out_specs=[_SEMS, _SEMS] + [_HBM] * (2 * n) + [pl.BlockSpec(memory_space=pltpu.VMEM)],
        out_shape=[pltpu.SemaphoreType.DMA((4 * n,)), pltpu.SemaphoreType.DMA((4 * n,))]
        + [pltpu.HBM(s.shape, s.dtype) for s in shards] + [pltpu.HBM(l.shape, l.dtype) for l in lands]
        + [jax.ShapeDtypeStruct((8, 128), F32)],
        input_output_aliases={t: 2 + t for t in range(2 * n)},
        compiler_params=pltpu.CompilerParams(has_side_effects=_DATAFLOW),
    )(*[_in_hbm(s) for s in shards], *lands, after)
    return res[0], res[1], res[2:2 + n], res[2 + n:2 + 2 * n], res[-1]


def _gather_wait(send, recv, shards, lands, after, *, name):
    n = len(shards)

    def body(*refs):
        sh, land = refs[:n], refs[n:2 * n]
        send_r, recv_r = refs[2 * n], refs[2 * n + 1]
        for mine, theirs in _gather_copies(sh, land, send_r, recv_r):
            mine.wait_send()
            theirs.wait_recv()

    res = pl.pallas_call(
        body, name=name, in_specs=[_HBM] * (2 * n) + [_SEMS, _SEMS, _ANY], out_specs=[_HBM] * (2 * n),
        out_shape=[pltpu.HBM(s.shape, s.dtype) for s in shards] + [pltpu.HBM(l.shape, l.dtype) for l in lands],
        input_output_aliases={t: t for t in range(2 * n)},
        compiler_params=pltpu.CompilerParams(has_side_effects=_DATAFLOW),
    )(*shards, *lands, send, recv, after)
    return res[n:]


def _gather_forward(lands, *, name):
    n = len(lands)

    def body(*refs):
        o = refs[n:2 * n]
        send, recv = refs[2 * n:]
        x, y, c = _coords()
        sib = (x, y, 1 - c)
        srcs = [2 * cx + cy for cx, cy in _other_chips(x, y)]
        cps = [_remote(o[t].at[s, c], o[t].at[s, c], send, recv, 3 * t + j, sib) for t in range(n) for j, s in enumerate(srcs)]
        for cp in cps:
            cp.start()
        for t in range(n):
            for j, s in enumerate(srcs):
                _remote(o[t].at[s, 1 - c], o[t].at[s, 1 - c], send, recv, 3 * t + j, sib).wait_recv()
        for cp in cps:
            cp.wait_send()

    return pl.pallas_call(
        body, name=name, in_specs=[_ANY] * n, out_specs=[_ANY] * n, input_output_aliases={t: t for t in range(n)},
        out_shape=[jax.ShapeDtypeStruct(l.shape, l.dtype) for l in lands],
        scratch_shapes=[pltpu.SemaphoreType.DMA((3 * n,)), pltpu.SemaphoreType.DMA((3 * n,))],
    )(*lands)


def _small_copies(v, land, send, recv):
    x, y, c = _coords()
    me = 4 * x + 2 * y + c
    out = []
    for k in range(1, 8):
        px = 1 - x if k & 4 else x
        py = 1 - y if k & 2 else y
        pc = 1 - c if k & 1 else c
        out.append((_remote(v, land.at[me], send, recv, k - 1, (px, py, pc)),
                    _remote(v, land.at[4 * px + 2 * py + pc], send, recv, k - 1, (px, py, pc))))
    return out


def _small_start(v, after, *, name):
    def body(v_ref, land_ref, after_ref, send, recv, v_thru, land_thru, token):
        for mine, _ in _small_copies(v_ref, land_ref, send, recv):
            mine.start()
        token[...] = jnp.zeros_like(token)

    land = _in_hbm(lax.empty((8,) + v.shape, v.dtype))
    return pl.pallas_call(
        body, name=name, in_specs=[_HBM, _HBM, _ANY],
        out_specs=[_SEMS, _SEMS, _HBM, _HBM, pl.BlockSpec(memory_space=pltpu.VMEM)],
        out_shape=[pltpu.SemaphoreType.DMA((7,)), pltpu.SemaphoreType.DMA((7,)), pltpu.HBM(v.shape, v.dtype),
                   pltpu.HBM(land.shape, land.dtype), jax.ShapeDtypeStruct((8, 128), F32)],
        input_output_aliases={0: 2, 1: 3}, compiler_params=pltpu.CompilerParams(has_side_effects=_DATAFLOW),
    )(_in_hbm(v), land, after)


def _small_wait(send, recv, v, land, after, *, name):
    def body(v_ref, land_ref, send_r, recv_r, *rest):
        for mine, theirs in _small_copies(v_ref, land_ref, send_r, recv_r):
            mine.wait_send()
            theirs.wait_recv()

    return pl.pallas_call(
        body, name=name, in_specs=[_HBM, _HBM, _SEMS, _SEMS] + [_ANY] * len(after), out_specs=[_HBM, _HBM],
        out_shape=[pltpu.HBM(v.shape, v.dtype), pltpu.HBM(land.shape, land.dtype)],
        input_output_aliases={0: 0, 1: 1}, compiler_params=pltpu.CompilerParams(has_side_effects=_DATAFLOW),
    )(v, land, send, recv, *after)


def _small_sum(v, land, me_idx, *, name="small_sum"):
    def body(me_ref, v_ref, land_ref, o_ref):
        acc = None
        for s in range(8):
            term = jnp.where(me_ref[0] == s, v_ref[...], land_ref[s])
            acc = term if acc is None else acc + term
        o_ref[...] = acc

    whole = lambda shape: pl.BlockSpec(shape, lambda i, me_ref: (0,) * len(shape))
    return pl.pallas_call(
        body, name=name,
        grid_spec=pltpu.PrefetchScalarGridSpec(num_scalar_prefetch=1, grid=(1,), in_specs=[whole(v.shape), whole(land.shape)],
                                               out_specs=whole(v.shape)),
        out_shape=jax.ShapeDtypeStruct(v.shape, F32), compiler_params=_cp(1),
    )(me_idx, v, land)


def _rs_to_sibling(gs, *, name):
    n = len(gs)

    def body(*refs):
        g, a = refs[:n], refs[n:2 * n]
        send, recv = refs[2 * n:]
        x, y, c = _coords()
        cps = [_remote(g[t].at[:, 1 - c], a[t], send, recv, t, (x, y, 1 - c)) for t in range(n)]
        for cp in cps:
            cp.start()
        for cp in cps:
            cp.wait()

    return pl.pallas_call(
        body, name=name, in_specs=[_ANY] * n, out_specs=[_ANY] * n,
        out_shape=[jax.ShapeDtypeStruct((N_CHIPS,) + g.shape[2:], g.dtype) for g in gs],
        scratch_shapes=[pltpu.SemaphoreType.DMA((n,)), pltpu.SemaphoreType.DMA((n,))],
    )(*gs)


RS_ROW_SPLIT = 2


def _rs_add_pair(gs, as_, c_idx, *, name):
    n = len(gs)

    def body(c_ref, *refs):
        for t in range(n):
            refs[2 * n + t][...] = (refs[t][...].astype(F32) + refs[n + t][...].astype(F32)).astype(BF16)

    def gspec(g):
        _, _, rh, cols = g.shape
        return pl.BlockSpec((None, None, rh // RS_ROW_SPLIT, cols), lambda j, i, c_ref: (j, c_ref[0], i, 0))

    def pspec(g):
        _, _, rh, cols = g.shape
        return pl.BlockSpec((None, rh // RS_ROW_SPLIT, cols), lambda j, i, c_ref: (j, i, 0))

    return pl.pallas_call(
        body, name=name,
        grid_spec=pltpu.PrefetchScalarGridSpec(
            num_scalar_prefetch=1, grid=(N_CHIPS, RS_ROW_SPLIT),
            in_specs=[gspec(g) for g in gs] + [pspec(g) for g in gs], out_specs=[pspec(g) for g in gs]),
        out_shape=[jax.ShapeDtypeStruct((N_CHIPS,) + g.shape[2:], BF16) for g in gs], compiler_params=_cp(2),
    )(c_idx, *gs, *as_)


def _chips_copies(p, r, send, recv):
    x, y, c = _coords()
    return [_remote(p[t].at[2 * cx + cy], r[t].at[k], send, recv, 3 * t + k, (cx, cy, c))
            for k, (cx, cy) in enumerate(_other_chips(x, y)) for t in range(len(p))]


def _rs_chips_start(ps, *, name):
    n = len(ps)

    def body(*refs):
        p, r = refs[:n], refs[n:2 * n]
        send, recv = refs[2 * n], refs[2 * n + 1]
        token = refs[-1]
        for cp in _chips_copies(p, r, send, recv):
            cp.start()
        token[...] = jnp.zeros_like(token)

    lands = [_in_hbm(lax.empty((3,) + p.shape[1:], p.dtype)) for p in ps]
    res = pl.pallas_call(
        body, name=name, in_specs=[_HBM] * (2 * n),
        out_specs=[_SEMS, _SEMS] + [_HBM] * (2 * n) + [pl.BlockSpec(memory_space=pltpu.VMEM)],
        out_shape=[pltpu.SemaphoreType.DMA((3 * n,)), pltpu.SemaphoreType.DMA((3 * n,))]
        + [pltpu.HBM(p.shape, p.dtype) for p in ps] + [pltpu.HBM(l.shape, l.dtype) for l in lands]
        + [jax.ShapeDtypeStruct((8, 128), F32)],
        input_output_aliases={t: 2 + t for t in range(2 * n)},
        compiler_params=pltpu.CompilerParams(has_side_effects=_DATAFLOW),
    )(*[_in_hbm(p) for p in ps], *lands)
    return res[0], res[1], res[2:2 + n], res[2 + n:2 + 2 * n], res[-1]


def _rs_chips_wait(send, recv, ps, lands, after, *, name):
    n = len(ps)

    def body(*refs):
        p, r = refs[:n], refs[n:2 * n]
        for cp in _chips_copies(p, r, refs[2 * n], refs[2 * n + 1]):
            cp.wait_send()
            cp.wait_recv()

    res = pl.pallas_call(
        body, name=name, in_specs=[_HBM] * (2 * n) + [_SEMS, _SEMS] + [_ANY] * len(after), out_specs=[_HBM] * (2 * n),
        out_shape=[pltpu.HBM(p.shape, p.dtype) for p in ps] + [pltpu.HBM(l.shape, l.dtype) for l in lands],
        input_output_aliases={t: t for t in range(2 * n)},
        compiler_params=pltpu.CompilerParams(has_side_effects=_DATAFLOW),
    )(*ps, *lands, send, recv, *after)
    return res[:n], res[n:]


def _rs_add_chips(ps, rs, idx, *, name):
    n = len(ps)

    def body(idx_ref, *refs):
        for t in range(n):
            p_ref, r0, r1, r2 = refs[4 * t:4 * t + 4]
            refs[4 * n + t][...] = ((p_ref[...].astype(F32) + r0[...].astype(F32)) + r1[...].astype(F32)) + r2[...].astype(F32)

    in_specs, args = [], []
    for p, r in zip(ps, rs):
        _, rh, cols = p.shape
        blk = (None, rh // RS_ROW_SPLIT, cols)
        in_specs.append(pl.BlockSpec(blk, lambda i, idx_ref: (idx_ref[0], i, 0)))
        in_specs += [pl.BlockSpec(blk, lambda i, idx_ref, k=k: (k, i, 0)) for k in range(3)]
        args += [p, r, r, r]
    out_specs = [pl.BlockSpec((None, p.shape[1] // RS_ROW_SPLIT, p.shape[2]), lambda i, idx_ref: (idx_ref[1], i, 0))
                 for p in ps]
    return pl.pallas_call(
        body, name=name,
        grid_spec=pltpu.PrefetchScalarGridSpec(num_scalar_prefetch=1, grid=(RS_ROW_SPLIT,), in_specs=in_specs,
                                               out_specs=out_specs),
        out_shape=[jax.ShapeDtypeStruct((2,) + p.shape[1:], F32) for p in ps], compiler_params=_cp(1),
    )(idx, *args)


def _rs_join_halves(hs, *, name):
    n = len(hs)

    def body(*refs):
        o = refs[n:2 * n]
        send, recv = refs[2 * n:]
        x, y, c = _coords()
        cps = [_remote(o[t].at[c], o[t].at[c], send, recv, t, (x, y, 1 - c)) for t in range(n)]
        for cp in cps:
            cp.start()
        for t in range(n):
            _remote(o[t].at[1 - c], o[t].at[1 - c], send, recv, t, (x, y, 1 - c)).wait_recv()
        for cp in cps:
            cp.wait_send()

    return pl.pallas_call(
        body, name=name, in_specs=[_ANY] * n, out_specs=[_ANY] * n,
        input_output_aliases={t: t for t in range(n)},
        out_shape=[jax.ShapeDtypeStruct(h.shape, F32) for h in hs],
        scratch_shapes=[pltpu.SemaphoreType.DMA((n,)), pltpu.SemaphoreType.DMA((n,))],
    )(*hs)


def _adamw(w, gs, m, v, *, name, dep=None):
    L, Rr, C = w.shape
    tr, tc = _pick(Rr, (256, 128, 64)), C
    if tr == Rr and Rr * C > 512 * 1024:
        tc = 256
    bc1 = 1.0 - ADAM_B1 ** ADAM_STEP
    bc2 = 1.0 - ADAM_B2 ** ADAM_STEP
    nd = 0 if dep is None else 1

    def body(*refs):
        w_ref, m_ref, v_ref = refs[0], refs[1], refs[2]
        g_refs = refs[3:3 + L]
        d_ref, mo_ref, vo_ref, go_ref = refs[3 + L + nd:]
        layer = pl.program_id(0)
        gv = g_refs[0][...]
        for q in range(1, L):
            gv = jnp.where(layer == q, g_refs[q][...], gv)
        mn = ADAM_B1 * m_ref[...] + (1.0 - ADAM_B1) * gv
        vn = ADAM_B2 * v_ref[...] + (1.0 - ADAM_B2) * (gv * gv)
        go_ref[...] = gv
        mo_ref[...] = mn
        vo_ref[...] = vn
        d_ref[...] = -ADAM_LR * ((mn / bc1) / (jnp.sqrt(vn / bc2) + ADAM_EPS) + ADAM_WD * w_ref[...])

    blk = pl.BlockSpec((None, tr, tc), lambda l, i, j: (l, i, j))
    gblks = [pl.BlockSpec((tr, tc), lambda l, i, j, q=q: (jnp.where(l == q, i, 0), jnp.where(l == q, j, 0))) for q in range(L)]
    return pl.pallas_call(
        body, name=name, grid=(L, Rr // tr, C // tc), in_specs=[blk] * 3 + gblks + [_ANY] * nd, out_specs=[blk] * 4,
        out_shape=[jax.ShapeDtypeStruct((L, Rr, C), F32)] * 4, compiler_params=_cp(3),
    )(w, m, v, *gs, *([] if dep is None else [dep]))


def kernel(x, positions, a_norm, a_in_proj, a_conv_w, a_conv_b, a_dt_bias, a_A_log, a_D, a_gnorm, a_out_proj,
           kv_norm, w_kv, b_kv, k_norm, b_norm, w_q, b_q, q_norm, sinks, w_o, b_o, f_norm, f_w_in, f_conv_w,
           f_conv_b, f_w_down, loss_target, m_a_norm, m_a_in_proj, m_a_conv_w, m_a_conv_b, m_a_dt_bias, m_a_A_log,
           m_a_D, m_a_gnorm, m_a_out_proj, m_kv_norm, m_w_kv, m_b_kv, m_k_norm, m_b_norm, m_w_q, m_b_q, m_q_norm,
           m_sinks, m_w_o, m_b_o, m_f_norm, m_f_w_in, m_f_conv_w, m_f_conv_b, m_f_w_down, v_a_norm, v_a_in_proj,
           v_a_conv_w, v_a_conv_b, v_a_dt_bias, v_a_A_log, v_a_D, v_a_gnorm, v_a_out_proj, v_kv_norm, v_w_kv,
           v_b_kv, v_k_norm, v_b_norm, v_w_q, v_b_q, v_q_norm, v_sinks, v_w_o, v_b_o, v_f_norm, v_f_w_in,
           v_f_conv_w, v_f_conv_b, v_f_w_down):
    wl = dict(zip(WEIGHTS, (a_norm, a_in_proj, a_conv_w, a_conv_b, a_dt_bias, a_A_log, a_D, a_gnorm, a_out_proj,
                            kv_norm, w_kv, b_kv, k_norm, b_norm, w_q, b_q, q_norm, sinks, w_o, b_o, f_norm, f_w_in,
                            f_conv_w, f_conv_b, f_w_down)))
    ml = dict(zip(WEIGHTS, (m_a_norm, m_a_in_proj, m_a_conv_w, m_a_conv_b, m_a_dt_bias, m_a_A_log, m_a_D, m_a_gnorm,
                            m_a_out_proj, m_kv_norm, m_w_kv, m_b_kv, m_k_norm, m_b_norm, m_w_q, m_b_q, m_q_norm,
                            m_sinks, m_w_o, m_b_o, m_f_norm, m_f_w_in, m_f_conv_w, m_f_conv_b, m_f_w_down)))
    vl = dict(zip(WEIGHTS, (v_a_norm, v_a_in_proj, v_a_conv_w, v_a_conv_b, v_a_dt_bias, v_a_A_log, v_a_D, v_a_gnorm,
                            v_a_out_proj, v_kv_norm, v_w_kv, v_b_kv, v_k_norm, v_b_norm, v_w_q, v_b_q, v_q_norm,
                            v_sinks, v_w_o, v_b_o, v_f_norm, v_f_w_in, v_f_conv_w, v_f_conv_b, v_f_w_down)))
    xi, yi, ci = _coords()
    me = 2 * xi + yi
    S = x.shape[1]

    def block_of(n, layer):
        t = wl[n]
        return t if layer is None else t[layer]

    rows = lambda t: t.reshape(-1, t.shape[-1])
    c_idx = jnp.reshape(ci, (1,)).astype(jnp.int32)
    me_c = jnp.stack([me, ci]).astype(jnp.int32)
    early = ("in_proj", "out_proj")
    late = tuple(name for name, _, _ in MATS if name not in early)
    shards = {name: _halves(block_of(wn, layer).astype(BF16)) for name, wn, layer in MATS}

    sp = _pack([wl[n] for n, _ in SMALL_CUT], 8, 128, F32)
    gathered, gs = _gather_weights([shards[k] for k in early], sp)
    gt = {k: t.reshape(N_CHIPS, -1, t.shape[-1]) for k, t in zip(early, gathered)}
    started = _gather_start([shards[k] for k in late], gs, name="gather_late_start")
    full = {n: wl[n] for n in SMALL_REP}
    gs = gs.reshape(N_CHIPS, -1)
    pieces = [_unpack(gs[j], [wl[n].shape for n, _ in SMALL_CUT]) for j in range(N_CHIPS)]
    for q, (n, ax) in enumerate(SMALL_CUT):
        full[n] = jnp.concatenate([pieces[j][q] for j in range(N_CHIPS)], axis=ax)
    w = _prep_small(full, {})
    w["w_zx"], w["w_dt"] = _split_in_proj(gt["in_proj"].transpose(1, 0, 2).reshape(1024, -1))
    w["a_out_proj"] = rows(gt["out_proj"])
    w["dep"] = started[4]

    class Comm:
        pending = None
        token = None
        reduced = {}

        def late_weights(self, w, after):
            lands = _gather_wait(started[0], started[1], started[2], started[3], after, name="gather_late_wait")
            lands = _gather_forward(lands, name="gather_late_forward")
            lt = {k: t.reshape(N_CHIPS, -1, t.shape[-1]) for k, t in zip(late, lands)}
            w = dict(w)
            w["w_kv"], w["w_q"], w["w_o"] = (rows(lt[k]) for k in ("w_kv", "w_q", "w_o"))
            w["f_w_in"] = [lt["f_in0"], lt["f_in1"]]
            w["f_w_down"] = [rows(lt["f_down0"]), rows(lt["f_down1"])]
            return w

        def finish(self, after):
            names, send, recv, ps, lands, tag = self.pending
            ps, rs = _rs_chips_wait(send, recv, ps, lands, after, name=f"rs_chips_wait{tag}")
            halves = _rs_add_chips(ps, rs, me_c, name=f"rs_add_chips{tag}")
            joined = _rs_join_halves(halves, name=f"rs_join_halves{tag}")
            self.reduced.update({k: rows(t) for k, t in zip(names, joined)})
            self.pending = None

        def grads(self, group, tensors, after):
            if self.pending is not None:
                self.finish([after])
            names = list(tensors)
            glist = [tensors[k].reshape(N_CHIPS, 2, -1, tensors[k].shape[-1]) for k in names]
            from_sib = _rs_to_sibling(glist, name=f"rs_to_sibling{group}")
            pairs = _rs_add_pair(glist, from_sib, c_idx, name=f"rs_add_pair{group}")
            send, recv, ps, lands, token = _rs_chips_start(pairs, name=f"rs_chips_start{group}")
            self.pending = (names, send, recv, ps, lands, group)
            self.token = token
            return token

    comm = Comm()

    posf = positions.reshape(S, 1).astype(F32)
    loss_part, dx0, gr = _local_step(x[0], posf, loss_target[0], w, comm)
    g = _small_grads(gr)

    small_names = [n for n, _ in SMALL_CUT] + list(SMALL_REP)
    sv = _pack([g[n] for n in small_names] + [loss_part[0:1, 0:1]], 8, 128, F32)
    s_send, s_recv, sv, s_land, s_token = _small_start(sv, comm.token, name="small_start")

    grads, delta, new_m, new_v = {}, {}, {}, {}

    def update(wn, dep):
        gl = [comm.reduced[name] for name, n2, _ in MATS if n2 == wn]
        shp = wl[wn].shape
        three = (len(gl),) + gl[0].shape
        flip = shp[-1] % 128 != 0
        view = (lambda t: t.reshape(three).transpose(0, 2, 1)) if flip else (lambda t: t.reshape(three))
        back = (lambda t: t.transpose(0, 2, 1).reshape(shp)) if flip else (lambda t: t.reshape(shp))
        if flip:
            gl = [t.T for t in gl]
        d, mn, vn, go = _adamw(view(wl[wn]), gl, view(ml[wn]), view(vl[wn]), name="adamw_" + wn, dep=dep)
        grads[wn], delta[wn], new_m[wn], new_v[wn] = back(go), back(d), back(mn), back(vn)
        return d

    done = [update(wn, s_token) for wn in ("f_w_in", "f_w_down", "w_q", "w_o", "w_kv", "a_out_proj")]
    comm.finish(done)
    update("a_in_proj", None)

    sv, s_land = _small_wait(s_send, s_recv, sv, s_land, done, name="small_wait")
    sred = _small_sum(sv, s_land, jnp.reshape(2 * me + ci, (1,)).astype(jnp.int32)).reshape(-1)
    small_shapes = [g[n].shape for n in small_names] + [(1,)]
    sg = dict(zip(small_names + ["loss"], _unpack(sred, small_shapes)))
    loss = sg["loss"].reshape(())
    g_small = {}
    for n, ax in SMALL_CUT:
        size = wl[n].shape[ax]
        g_small[n] = lax.dynamic_slice_in_dim(sg[n], me * size, size, axis=ax)
    for n in SMALL_REP:
        g_small[n] = sg[n].reshape(wl[n].shape)

    pk = lambda d: _pack([d[n] for n in small_names], 8, 128, F32)[None]
    d, mn, vn, _ = _adamw(pk(wl), [pk(g_small)[0]], pk(ml), pk(vl), name="adamw_small")
    shapes = [wl[n].shape for n in small_names]
    for n, dd, mm, vv in zip(small_names, _unpack(d.reshape(-1), shapes), _unpack(mn.reshape(-1), shapes),
                             _unpack(vn.reshape(-1), shapes)):
        grads[n], delta[n], new_m[n], new_v[n] = g_small[n], dd, mm, vv

    return (loss, dx0[None], *[grads[n] for n in WEIGHTS], *[delta[n] for n in WEIGHTS],
            *[new_m[n] for n in WEIGHTS], *[new_v[n] for n in WEIGHTS])
```

```python
import math

import jax
import jax.numpy as jnp
from jax import lax
from jax.experimental import pallas as pl
from jax.experimental.pallas import tpu as pltpu

F32 = jnp.float32
BF16 = jnp.bfloat16

EPS = 1e-5
CHUNK = 256
WINDOW = 128
HEAD = 64
SSM_HEADS = 32
SSM_GROUPS = 8
SSM_STATE = 128
ATT_KV = 4
ATT_G = 4
ROPE_THETA = 10000.0
NEG = -1e30
N_CHIPS = 4
VMEM_LIMIT = 56 * 1024 * 1024

ADAM_LR, ADAM_B1, ADAM_B2, ADAM_EPS, ADAM_WD, ADAM_STEP = 0.001, 0.9, 0.999, 1e-08, 0.01, 10


def _cp(n_axes):
    return pltpu.CompilerParams(dimension_semantics=("arbitrary",) * n_axes, vmem_limit_bytes=VMEM_LIMIT)


def _pick(dim, prefs):
    for p in prefs:
        if dim % p == 0:
            return p
    return dim


def _iota(shape, dim):
    return lax.broadcasted_iota(jnp.int32, shape, dim)


def _dot(a, b, ca=1, cb=0):
    return lax.dot_general(a, b, (((ca,), (cb,)), ((), ())), preferred_element_type=F32)


def _dot3(x, ind):
    h = x.astype(BF16)
    r = x - h.astype(F32)
    m = r.astype(BF16)
    lo = (r - m.astype(F32)).astype(BF16)
    return _dot(h, ind) + _dot(m, ind) + _dot(lo, ind)


def _sigmoid(x):
    return jax.nn.sigmoid(x)


def _mm(a, b, *, name, ta=False, tb=False, bias=None, res=None, out_dtype=F32, b_koff=0, tm=None, tn=None, tk=None,
        dims=None, a_spec=None, b_spec=None, o_spec=None, o_shape=None, dep=None, more=()):
    if dims is not None:
        M, N, K = dims
    else:
        if ta:
            K, M = a.shape
        else:
            M, K = a.shape
        N = b.shape[0] if tb else b.shape[1]
    tm = tm or _pick(M, (1024, 1408, 512, 256, 128))
    tn = tn or _pick(N, (512, 1408, 256, 128))
    tk = tk or (K if K <= 2048 else _pick(K, (2048, 1408, 1024, 512)))
    assert M % tm == 0 and N % tn == 0 and K % tk == 0 and b_koff % tk == 0
    nk = K // tk
    kb0 = b_koff // tk
    has_bias, has_res = bias is not None, res is not None

    def body(*refs):
        a_ref, b_ref = refs[0], refs[1]
        pos = 2
        bias_ref = res_ref = acc_ref = None
        if has_bias:
            bias_ref = refs[pos]
            pos += 1
        if has_res:
            res_ref = refs[pos]
            pos += 1
        if dep is not None:
            pos += 1
        extra = refs[pos:pos + 2 * len(more)]
        pos += 2 * len(more)
        o_ref = refs[pos]
        if nk > 1:
            acc_ref = refs[pos + 1]
        part = _dot(a_ref[...].astype(BF16), b_ref[...].astype(BF16), 0 if ta else 1, 1 if tb else 0)
        for q in range(len(more)):
            part = part + _dot(extra[2 * q][...].astype(BF16), extra[2 * q + 1][...].astype(BF16),
                               0 if ta else 1, 1 if tb else 0)

        def finish(acc):
            if has_bias:
                acc = acc + bias_ref[...]
            if has_res:
                acc = acc + res_ref[...]
            o_ref[...] = acc.astype(out_dtype)

        if nk == 1:
            finish(part)
        else:
            k = pl.program_id(2)

            @pl.when(k == 0)
            def _():
                acc_ref[...] = part

            @pl.when(k > 0)
            def _():
                acc_ref[...] += part

            @pl.when(k == nk - 1)
            def _():
                finish(acc_ref[...])

    if a_spec is None:
        a_spec = pl.BlockSpec((tk, tm), lambda i, j, k: (k, i)) if ta else pl.BlockSpec((tm, tk), lambda i, j, k: (i, k))
    if b_spec is None:
        b_spec = (pl.BlockSpec((tn, tk), lambda i, j, k: (j, k + kb0)) if tb
                  else pl.BlockSpec((tk, tn), lambda i, j, k: (k + kb0, j)))
    if o_spec is None:
        o_spec = pl.BlockSpec((tm, tn), lambda i, j, k: (i, j))
    in_specs, args = [a_spec, b_spec], [a, b]
    if has_bias:
        in_specs.append(pl.BlockSpec((1, tn), lambda i, j, k: (0, j)))
        args.append(bias)
    if has_res:
        in_specs.append(pl.BlockSpec((tm, tn), lambda i, j, k: (i, j)))
        args.append(res)
    if dep is not None:
        in_specs.append(pl.BlockSpec(memory_space=pl.ANY))
        args.append(dep)
    for piece in more:
        a2, sa, b2, sb = piece if len(piece) == 4 else (a, piece[0], b, piece[1])
        in_specs += [sa, sb]
        args += [a2, b2]
    return pl.pallas_call(
        body, name=name, grid=(M // tm, N // tn, nk), in_specs=in_specs, out_specs=o_spec,
        out_shape=jax.ShapeDtypeStruct(o_shape or (M, N), out_dtype),
        scratch_shapes=[pltpu.VMEM((tm, tn), F32)] if nk > 1 else [],
        compiler_params=_cp(3),
    )(*args)


def _rms_fwd(x, gains, *, name, tr=256, dep=None):
    S, D = x.shape
    n = len(gains)
    nd = 0 if dep is None else 1

    def body(*refs):
        xv = refs[0][...]
        xh = xv * lax.rsqrt(jnp.mean(xv * xv, axis=-1, keepdims=True) + EPS)
        for q in range(n):
            refs[1 + n + nd + q][...] = (xh * refs[1 + q][...]).astype(BF16)

    row = pl.BlockSpec((tr, D), lambda i: (i, 0))
    vec = pl.BlockSpec((1, D), lambda i: (0, 0))
    return pl.pallas_call(
        body, name=name, grid=(S // tr,), in_specs=[row] + [vec] * n + [pl.BlockSpec(memory_space=pl.ANY)] * nd,
        out_specs=[row] * n, out_shape=[jax.ShapeDtypeStruct((S, D), BF16)] * n, compiler_params=_cp(1),
    )(x, *gains, *([] if dep is None else [dep]))


def _rms_bwd(x, gains, dhs, dres, *, name, tr=256, want_colsum=False):
    S, D = x.shape
    n = len(gains)
    steps = S // tr

    def body(*refs):
        x_ref = refs[0]
        g_refs = refs[1:1 + n]
        dh_refs = refs[1 + n:1 + 2 * n]
        dres_ref = refs[1 + 2 * n]
        dx_ref = refs[2 + 2 * n]
        dg_refs = refs[3 + 2 * n:3 + 3 * n]
        cs_ref = refs[3 + 3 * n] if want_colsum else None
        i = pl.program_id(0)
        xv = x_ref[...]
        r = lax.rsqrt(jnp.mean(xv * xv, axis=-1, keepdims=True) + EPS)
        xh = xv * r
        dx = dres_ref[...]
        for q in range(n):
            dh = dh_refs[q][...]
            dxh = dh * g_refs[q][...]
            dx = dx + r * (dxh - xh * jnp.mean(dxh * xh, axis=-1, keepdims=True))
            part = jnp.sum(dh * xh, axis=0, keepdims=True)

            @pl.when(i == 0)
            def _():
                dg_refs[q][...] = part

            @pl.when(i > 0)
            def _():
                dg_refs[q][...] += part

        dx_ref[...] = dx
        if want_colsum:
            cpart = jnp.sum(dx, axis=0, keepdims=True)

            @pl.when(i == 0)
            def _():
                cs_ref[...] = cpart

            @pl.when(i > 0)
            def _():
                cs_ref[...] += cpart

    row = pl.BlockSpec((tr, D), lambda i: (i, 0))
    vec = pl.BlockSpec((1, D), lambda i: (0, 0))
    n_vec_out = n + (1 if want_colsum else 0)
    outs = pl.pallas_call(
        body, name=name, grid=(steps,), in_specs=[row] + [vec] * n + [row] * n + [row],
        out_specs=[row] + [vec] * n_vec_out,
        out_shape=[jax.ShapeDtypeStruct((S, D), F32)] + [jax.ShapeDtypeStruct((1, D), F32)] * n_vec_out,
        compiler_params=_cp(1),
    )(x, *gains, *dhs, dres)
    return outs


def _colsum(x, *, name, tr=256):
    S, D = x.shape

    def body(x_ref, o_ref):
        i = pl.program_id(0)
        part = jnp.sum(x_ref[...].astype(F32), axis=0, keepdims=True)

        @pl.when(i == 0)
        def _():
            o_ref[...] = part

        @pl.when(i > 0)
        def _():
            o_ref[...] += part

    return pl.pallas_call(
        body, name=name, grid=(S // tr,), in_specs=[pl.BlockSpec((tr, D), lambda i: (i, 0))],
        out_specs=pl.BlockSpec((1, D), lambda i: (0, 0)), out_shape=jax.ShapeDtypeStruct((1, D), F32),
        compiler_params=_cp(1),
    )(x)


def _loss(y, t, *, name="loss", tr=256):
    S, D = y.shape
    steps = S // tr

    def body(y_ref, t_ref, dy_ref, l_ref, acc_ref):
        i = pl.program_id(0)
        e = y_ref[...] - t_ref[...]
        dy_ref[...] = e * (1.0 / D)
        part = jnp.sum(e * e, axis=0, keepdims=True)

        @pl.when(i == 0)
        def _():
            acc_ref[...] = part

        @pl.when(i > 0)
        def _():
            acc_ref[...] += part

        @pl.when(i == steps - 1)
        def _():
            tot = jnp.sum(acc_ref[...], axis=1, keepdims=True) * (0.5 / D)
            l_ref[...] = jnp.broadcast_to(tot, (8, 128))

    row = pl.BlockSpec((tr, D), lambda i: (i, 0))
    return pl.pallas_call(
        body, name=name, grid=(steps,), in_specs=[row, row],
        out_specs=[row, pl.BlockSpec((8, 128), lambda i: (0, 0))],
        out_shape=[jax.ShapeDtypeStruct((S, D), F32), jax.ShapeDtypeStruct((8, 128), F32)],
        scratch_shapes=[pltpu.VMEM((1, D), F32)], compiler_params=_cp(1),
    )(y, t)


STRIP = 64
HALO = 8


def _strips(S, tc):
    return [(r0, slice(l0, l0 + 128)) for l0 in range(0, tc, 128) for r0 in range(S - STRIP, -1, -STRIP)]


def _with_halo(ref, r0, ls):
    if r0 == 0:
        return jnp.concatenate([jnp.zeros((HALO, 128), F32), ref[0:STRIP, ls]], axis=0)
    return ref[r0 - HALO:r0 + STRIP, ls]


def _conv_strip(xw, w_ref, b_ref, ls, width):
    acc = b_ref[:, ls] + w_ref[pl.ds(width - 1, 1), ls] * xw[HALO:]
    shifted = []
    for s in range(1, width):
        xs = pltpu.roll(xw, s, axis=0)[HALO:]
        shifted.append(xs)
        acc = acc + w_ref[pl.ds(width - 1 - s, 1), ls] * xs
    return acc, shifted


def _conv_strip_back(dacc, after, xc, shifted, w_ref, ls, width):
    ext = jnp.concatenate([dacc, after], axis=0)
    dx = w_ref[pl.ds(width - 1, 1), ls] * dacc
    dws = [None] * width
    dws[width - 1] = jnp.sum(dacc * xc, axis=0, keepdims=True)
    for s in range(1, width):
        dx = dx + w_ref[pl.ds(width - 1 - s, 1), ls] * pltpu.roll(ext, STRIP + HALO - s, axis=0)[:STRIP]
        dws[width - 1 - s] = jnp.sum(dacc * shifted[s - 1], axis=0, keepdims=True)
    return dx, dws, jnp.sum(dacc, axis=0, keepdims=True)


def _conv_back_block(S, tc, width, w_ref, b_ref, x_ref, dacc_of, dx_store, dw_ref, db_ref):
    for l0 in range(0, tc, 128):
        ls = slice(l0, l0 + 128)
        after = jnp.zeros((HALO, 128), F32)
        tot = None
        for r0 in range(S - STRIP, -1, -STRIP):
            xw = _with_halo(x_ref, r0, ls)
            acc, shifted = _conv_strip(xw, w_ref, b_ref, ls, width)
            dacc = dacc_of(r0, ls, acc, _sigmoid(acc))
            dx, dws, db = _conv_strip_back(dacc, after, xw[HALO:], shifted, w_ref, ls, width)
            dx_store(r0, ls, dx)
            after = dacc[:HALO]
            part = dws + [db]
            tot = part if tot is None else [p + q for p, q in zip(tot, part)]
        for k in range(width):
            dw_ref[pl.ds(k, 1), ls] = tot[k]
        db_ref[:, ls] = tot[width]


def _conv_silu_fwd(xin, col0, C, w, b, *, name, tc=512):
    S = xin.shape[0]
    width = w.shape[0]
    off = col0 // tc

    def body(x_ref, w_ref, b_ref, o_ref):
        for r0, ls in _strips(S, tc):
            acc, _ = _conv_strip(_with_halo(x_ref, r0, ls), w_ref, b_ref, ls, width)
            o_ref[r0:r0 + STRIP, ls] = acc * _sigmoid(acc)

    return pl.pallas_call(
        body, name=name, grid=(C // tc,),
        in_specs=[pl.BlockSpec((S, tc), lambda j: (0, j + off)), pl.BlockSpec((width, tc), lambda j: (0, j)),
                  pl.BlockSpec((1, tc), lambda j: (0, j))],
        out_specs=pl.BlockSpec((S, tc), lambda j: (0, j)), out_shape=jax.ShapeDtypeStruct((S, C), F32),
        compiler_params=_cp(1),
    )(xin, w, b)


def _conv_silu_bwd(xin, col0, C, w, b, douts, *, name, tc=256):
    S = xin.shape[0]
    width = w.shape[0]
    off = col0 // tc
    nd = len(douts)
    ranges = [(o // tc, (o + d.shape[1]) // tc) for d, o in douts]

    def body(*refs):
        x_ref, w_ref, b_ref = refs[0], refs[1], refs[2]
        d_refs = refs[3:3 + nd]
        dx_ref, dw_ref, db_ref = refs[3 + nd], refs[4 + nd], refs[5 + nd]
        j = pl.program_id(0)

        def dacc_of(r0, ls, acc, sg):
            dout = jnp.zeros((STRIP, 128), F32)
            for q in range(nd):
                lo, hi = ranges[q]
                dout = dout + jnp.where((j >= lo) & (j < hi), d_refs[q][r0:r0 + STRIP, ls], 0.0)
            return dout * (sg * (1.0 + acc * (1.0 - sg)))

        def dx_store(r0, ls, dx):
            dx_ref[r0:r0 + STRIP, ls] = dx.astype(BF16)

        _conv_back_block(S, tc, width, w_ref, b_ref, x_ref, dacc_of, dx_store, dw_ref, db_ref)

    d_specs = [pl.BlockSpec((S, tc), (lambda j, lo=lo, hi=hi: (0, jnp.clip(j - lo, 0, hi - lo - 1)))) for lo, hi in ranges]
    return pl.pallas_call(
        body, name=name, grid=(C // tc,),
        in_specs=[pl.BlockSpec((S, tc), lambda j: (0, j + off)), pl.BlockSpec((width, tc), lambda j: (0, j)),
                  pl.BlockSpec((1, tc), lambda j: (0, j))] + d_specs,
        out_specs=[pl.BlockSpec((S, tc), lambda j: (0, j)), pl.BlockSpec((width, tc), lambda j: (0, j)),
                   pl.BlockSpec((1, tc), lambda j: (0, j))],
        out_shape=[jax.ShapeDtypeStruct((S, C), BF16), jax.ShapeDtypeStruct((width, C), F32),
                   jax.ShapeDtypeStruct((1, C), F32)],
        compiler_params=_cp(1),
    )(xin, w, b, *[d for d, _ in douts])


def _ffn_act_fwd(u, w, b, *, name, tc=256):
    S, F2 = u.shape
    Fd = F2 // 2
    width = w.shape[0]
    nb = Fd // tc

    def body(g_ref, v_ref, w_ref, b_ref, o_ref):
        for r0, ls in _strips(S, tc):
            acc, _ = _conv_strip(_with_halo(g_ref, r0, ls), w_ref, b_ref, ls, width)
            o_ref[r0:r0 + STRIP, ls] = (acc * _sigmoid(acc) * v_ref[r0:r0 + STRIP, ls]).astype(BF16)

    return pl.pallas_call(
        body, name=name, grid=(nb,),
        in_specs=[pl.BlockSpec((S, tc), lambda j: (0, j)), pl.BlockSpec((S, tc), lambda j: (0, j + nb)),
                  pl.BlockSpec((width, tc), lambda j: (0, j)), pl.BlockSpec((1, tc), lambda j: (0, j))],
        out_specs=pl.BlockSpec((S, tc), lambda j: (0, j)), out_shape=jax.ShapeDtypeStruct((S, Fd), BF16),
        compiler_params=_cp(1),
    )(u, u, w, b)


def _ffn_act_bwd(u, w, b, da, *, name, tc=256):
    S, F2 = u.shape
    Fd = F2 // 2
    width = w.shape[0]
    nb = Fd // tc

    def body(g_ref, v_ref, w_ref, b_ref, da_ref, du_ref, dw_ref, db_ref, a_ref):
        def dacc_of(r0, ls, acc, sg):
            rs = slice(r0, r0 + STRIP)
            dav, val, silu = da_ref[rs, ls], v_ref[rs, ls], acc * sg
            a_ref[rs, ls] = (silu * val).astype(BF16)
            du_ref[1, rs, ls] = (dav * silu).astype(BF16)
            return dav * val * (sg * (1.0 + acc * (1.0 - sg)))

        def dx_store(r0, ls, dx):
            du_ref[0, r0:r0 + STRIP, ls] = dx.astype(BF16)

        _conv_back_block(S, tc, width, w_ref, b_ref, g_ref, dacc_of, dx_store, dw_ref, db_ref)

    blk = pl.BlockSpec((S, tc), lambda j: (0, j))
    return pl.pallas_call(
        body, name=name, grid=(nb,),
        in_specs=[blk, pl.BlockSpec((S, tc), lambda j: (0, j + nb)), pl.BlockSpec((width, tc), lambda j: (0, j)),
                  pl.BlockSpec((1, tc), lambda j: (0, j)), blk],
        out_specs=[pl.BlockSpec((2, S, tc), lambda j: (0, 0, j)), pl.BlockSpec((width, tc), lambda j: (0, j)),
                   pl.BlockSpec((1, tc), lambda j: (0, j)), blk],
        out_shape=[jax.ShapeDtypeStruct((2, S, Fd), BF16),
                   jax.ShapeDtypeStruct((width, Fd), F32), jax.ShapeDtypeStruct((1, Fd), F32),
                   jax.ShapeDtypeStruct((S, Fd), BF16)],
        compiler_params=_cp(1),
    )(u, u, w, b, da)


def _ssd_prep(dtr, dt_bias, a_log, *, name="ssd_prep"):
    S = dtr.shape[0]

    def body(d_ref, b_ref, al_ref, dt_ref, ac_ref, sg_ref, act_ref):
        lane = _iota((CHUNK, 128), 1)
        valid = lane < SSM_HEADS
        z = d_ref[...] + b_ref[...]
        dt = jnp.where(valid, jnp.maximum(z, 0.0) + jnp.log(1.0 + jnp.exp(-jnp.abs(z))), 0.0)
        a = dt * (-jnp.exp(al_ref[...]))
        row = _iota((CHUNK, 128), 0)
        k = 1
        while k < CHUNK:
            a = a + jnp.where(row >= k, pltpu.roll(a, k, axis=0), 0.0)
            k *= 2
        sg = jnp.where(valid, _sigmoid(z), 0.0)
        for arr, ref in ((dt, dt_ref), (a, ac_ref), (sg, sg_ref)):
            for g in range(SSM_GROUPS):
                ref[g] = jnp.where(lane < 4, arr if g == 0 else pltpu.roll(arr, 128 - 4 * g, axis=1), 0.0)
        act_ref[...] = a.T[:SSM_HEADS, :]

    blk = pl.BlockSpec((CHUNK, 128), lambda i: (i, 0))
    vec = pl.BlockSpec((1, 128), lambda i: (0, 0))
    grp = pl.BlockSpec((SSM_GROUPS, CHUNK, 128), lambda i: (0, i, 0))
    return pl.pallas_call(
        body, name=name, grid=(S // CHUNK,), in_specs=[blk, vec, vec],
        out_specs=[grp, grp, grp, pl.BlockSpec((SSM_HEADS, CHUNK), lambda i: (0, i))],
        out_shape=[jax.ShapeDtypeStruct((SSM_GROUPS, S, 128), F32)] * 3 + [jax.ShapeDtypeStruct((SSM_HEADS, S), F32)],
        compiler_params=_cp(1),
    )(dtr, dt_bias, a_log)


SSD_GPS = 4


def _expand4(v, lanes):
    out = jnp.broadcast_to(v[:, 3:4], lanes.shape)
    for hh in (2, 1, 0):
        out = jnp.where(lanes < 64 * (hh + 1), v[:, hh:hh + 1], out)
    return out


def _ssd_fwd(xbc, dt_g, ac_g, ac_t, *, name="ssd_fwd"):
    S = xbc.shape[0]
    nc = S // CHUNK
    Lc = CHUNK

    def body(x_ref, b_ref, c_ref, dt_ref, ac_ref, act_ref, y_ref, st_out_ref, st_ref):
        g2 = pl.program_id(0)
        c = pl.program_id(1)

        @pl.when(c == 0)
        def _():
            st_ref[...] = jnp.zeros_like(st_ref)

        causal = _iota((Lc, Lc), 0) >= _iota((Lc, Lc), 1)
        lane256 = _iota((Lc, 256), 1)
        lane128 = _iota((Lc, 128), 1)
        row128 = _iota((128, 128), 0)
        for gg in range(SSD_GPS):
            g = SSD_GPS * g2 + gg
            bv = b_ref[:, 128 * gg:128 * (gg + 1)]
            cbf = c_ref[:, 128 * gg:128 * (gg + 1)].astype(BF16)
            cb = _dot(cbf, bv.astype(BF16), 1, 1)
            dtg, acg = dt_ref[gg], ac_ref[gg]
            ac_last = ac_ref[gg, pl.ds(Lc - 1, 1), :]
            dt4 = _expand4(dtg, lane256)
            ac4 = _expand4(acg, lane256)
            e4 = jnp.exp(ac4)
            xdb = (x_ref[:, 256 * gg:256 * (gg + 1)] * dt4).astype(BF16)
            st_out_ref[gg] = st_ref[gg]
            for p in range(2):
                xd_p = xdb[:, 128 * p:128 * (p + 1)]
                st_p = st_ref[gg, p]
                ys, sn, cds = [], [], []
                for q in range(2):
                    hh = 2 * p + q
                    a_col = acg[:, hh:hh + 1]
                    a_row = act_ref[pl.ds(4 * g + hh, 1), :]
                    dec = jnp.exp(jnp.where(causal, a_col - a_row, NEG))
                    w = (cb * dec).astype(BF16)
                    ys.append(_dot(w, xd_p))
                    al = ac_last[:, hh:hh + 1]
                    dte = jnp.exp(al - a_col)
                    sn.append(_dot(xd_p, (bv * dte).astype(BF16), 0, 0))
                    cds.append(jnp.exp(al))
                y_diag = jnp.where(lane128 < 64, ys[0], ys[1])
                y_off = _dot(cbf, st_p.astype(BF16), 1, 1) * e4[:, 128 * p:128 * (p + 1)]
                y_ref[:, 256 * gg + 128 * p:256 * gg + 128 * (p + 1)] = y_diag + y_off
                st_ref[gg, p] = jnp.where(row128 < 64, st_p * cds[0] + sn[0], st_p * cds[1] + sn[1])

    G = SSD_GPS
    per_g = lambda g, c: (g, c, 0)
    return pl.pallas_call(
        body, name=name, grid=(SSM_GROUPS // G, nc),
        in_specs=[pl.BlockSpec((Lc, 256 * G), lambda g, c: (c, g)),
                  pl.BlockSpec((Lc, 128 * G), lambda g, c: (c, 16 // G + g)),
                  pl.BlockSpec((Lc, 128 * G), lambda g, c: (c, 24 // G + g)),
                  pl.BlockSpec((G, Lc, 128), per_g), pl.BlockSpec((G, Lc, 128), per_g),
                  pl.BlockSpec((SSM_HEADS, Lc), lambda g, c: (0, c))],
        out_specs=[pl.BlockSpec((Lc, 256 * G), lambda g, c: (c, g)),
                   pl.BlockSpec((G, None, 2, 128, 128), lambda g, c: (g, c, 0, 0, 0))],
        out_shape=[jax.ShapeDtypeStruct((S, 2048), F32), jax.ShapeDtypeStruct((SSM_GROUPS, nc, 2, 128, 128), F32)],
        scratch_shapes=[pltpu.VMEM((G, 2, 128, 128), F32)], compiler_params=_cp(2),
    )(xbc, xbc, xbc, dt_g, ac_g, ac_t)


def _ssd_bwd(xbc, dt_g, ac_g, ac_t, states, dy, dexp, *, name="ssd_bwd"):
    S = xbc.shape[0]
    nc = S // CHUNK
    Lc = CHUNK

    def body(x_ref, b_ref, c_ref, dt_ref, ac_ref, act_ref, st_ref, dy_ref, d_ref, dx_ref, db_ref, dc_ref, dh_ref, ds_ref):
        g2 = pl.program_id(0)
        cc = pl.program_id(1)

        @pl.when(cc == 0)
        def _():
            ds_ref[...] = jnp.zeros_like(ds_ref)

        causal = _iota((Lc, Lc), 0) >= _iota((Lc, Lc), 1)
        lane256 = _iota((Lc, 256), 1)
        lane128 = _iota((Lc, 128), 1)
        row128 = _iota((128, 128), 0)
        ind_rows = _iota((256, 128), 0) >> 6
        ind_cols = _iota((256, 128), 1)
        ind_a = (ind_rows == ind_cols).astype(BF16)
        ind_b = (ind_rows + 4 == ind_cols).astype(BF16)
        for gg in range(SSD_GPS):
            g = SSD_GPS * g2 + gg
            bv = b_ref[:, 128 * gg:128 * (gg + 1)]
            cv = c_ref[:, 128 * gg:128 * (gg + 1)]
            bbf, cbf = bv.astype(BF16), cv.astype(BF16)
            cb = _dot(cbf, bbf, 1, 1)
            dtg, acg = dt_ref[gg], ac_ref[gg]
            ac_last = ac_ref[gg, pl.ds(Lc - 1, 1), :]
            dt4 = _expand4(dtg, lane256)
            ac4 = _expand4(acg, lane256)
            acl4 = _expand4(ac_last, _iota((1, 256), 1))
            e4 = jnp.exp(ac4)
            dte4 = jnp.exp(acl4 - ac4)
            xv = x_ref[:, 256 * gg:256 * (gg + 1)]
            xd = xv * dt4
            xdb = xd.astype(BF16)
            dyv = dy_ref[:, 256 * gg:256 * (gg + 1)]
            dcb = jnp.zeros((Lc, Lc), F32)
            dc_acc = jnp.zeros((Lc, 128), F32)
            db_acc = jnp.zeros((Lc, 128), F32)
            u_parts, dxd_parts, ends = [], [], []
            for p in range(2):
                sl = slice(128 * p, 128 * (p + 1))
                xd_p, xdb_p, dy_p = xd[:, sl], xdb[:, sl], dyv[:, sl]
                dyb_p = dy_p.astype(BF16)
                e_p, dte_p = e4[:, sl], dte4[:, sl]
                sp = st_ref[gg, p]
                spb = sp.astype(BF16)
                dsn = ds_ref[gg, p]
                dsnb = dsn.astype(BF16)
                yds, dxds, cds = [], [], []
                for q in range(2):
                    hh = 2 * p + q
                    a_col = acg[:, hh:hh + 1]
                    a_row = act_ref[pl.ds(4 * g + hh, 1), :]
                    dec = jnp.exp(jnp.where(causal, a_col - a_row, NEG))
                    w = (cb * dec).astype(BF16)
                    head = (lane128 < 64) if q == 0 else (lane128 >= 64)
                    dym = jnp.where(head, dyb_p, jnp.zeros_like(dyb_p))
                    dw = _dot(dym, xdb_p, 1, 1)
                    dcb = dcb + dw * dec
                    yds.append(_dot(w, xdb_p))
                    dxds.append(_dot(w, dyb_p, 0, 0))
                    cds.append(jnp.exp(ac_last[:, hh:hh + 1]))
                y_diag = jnp.where(lane128 < 64, yds[0], yds[1])
                dxd_diag = jnp.where(lane128 < 64, dxds[0], dxds[1])
                y_off = _dot(cbf, spb, 1, 1) * e_p
                dgp = dy_p * e_p
                dgb = dgp.astype(BF16)
                dc_acc = dc_acc + _dot(dgb, spb)
                dsp = _dot(dgb, cbf, 0, 0)
                cd_col = jnp.where(row128[:, 0:1] < 64, cds[0], cds[1])
                qm = _dot(bbf, dsnb, 1, 1)
                dxd_state = dte_p * qm
                db_acc = db_acc + _dot((xd_p * dte_p).astype(BF16), dsnb)
                t_p = xd_p * dxd_state
                prod = dsn * sp
                e0 = jnp.sum(jnp.sum(jnp.where(row128 < 64, prod, 0.0), axis=1, keepdims=True), axis=0, keepdims=True)
                e1 = jnp.sum(jnp.sum(jnp.where(row128 >= 64, prod, 0.0), axis=1, keepdims=True), axis=0, keepdims=True)
                tcol = jnp.sum(t_p, axis=0, keepdims=True)
                lane1 = _iota((1, 128), 1)
                t0 = jnp.sum(jnp.where(lane1 < 64, tcol, 0.0), axis=1, keepdims=True)
                t1 = jnp.sum(jnp.where(lane1 >= 64, tcol, 0.0), axis=1, keepdims=True)
                ends.append(e0 * cds[0] + t0)
                ends.append(e1 * cds[1] + t1)
                ds_ref[gg, p] = dsn * cd_col + dsp
                u_parts.append(dyb_p.astype(F32) * y_diag - xdb_p.astype(F32) * dxd_diag + dy_p * y_off - t_p)
                dxd_parts.append(dxd_diag + dxd_state)
            dxd = jnp.concatenate(dxd_parts, axis=1)
            u_all = jnp.concatenate(u_parts, axis=1)
            dx_ref[:, 256 * gg:256 * (gg + 1)] = dxd * dt4 + dyv * d_ref[:, 256 * gg:256 * (gg + 1)]
            dcbb = dcb.astype(BF16)
            dc_ref[:, 128 * gg:128 * (gg + 1)] = dc_acc + _dot(dcbb, bbf)
            db_ref[:, 128 * gg:128 * (gg + 1)] = db_acc + _dot(dcbb, cbf, 0, 0)
            lane = _iota((Lc, 128), 1)
            endv = jnp.zeros((Lc, 128), F32)
            for hh in range(4):
                endv = jnp.where(lane == 8 + hh, ends[hh], endv)
            dh_ref[gg] = _dot3(dxd * xv, ind_a) + _dot3(u_all, ind_b) + endv

    G = SSD_GPS
    rev = lambda c: nc - 1 - c
    per_g = lambda g, c: (g, rev(c), 0)
    return pl.pallas_call(
        body, name=name, grid=(SSM_GROUPS // G, nc),
        in_specs=[pl.BlockSpec((Lc, 256 * G), lambda g, c: (rev(c), g)),
                  pl.BlockSpec((Lc, 128 * G), lambda g, c: (rev(c), 16 // G + g)),
                  pl.BlockSpec((Lc, 128 * G), lambda g, c: (rev(c), 24 // G + g)),
                  pl.BlockSpec((G, Lc, 128), per_g), pl.BlockSpec((G, Lc, 128), per_g),
                  pl.BlockSpec((SSM_HEADS, Lc), lambda g, c: (0, rev(c))),
                  pl.BlockSpec((G, None, 2, 128, 128), lambda g, c: (g, rev(c), 0, 0, 0)),
                  pl.BlockSpec((Lc, 256 * G), lambda g, c: (rev(c), g)),
                  pl.BlockSpec((1, 256 * G), lambda g, c: (0, g))],
        out_specs=[pl.BlockSpec((Lc, 256 * G), lambda g, c: (rev(c), g)),
                   pl.BlockSpec((Lc, 128 * G), lambda g, c: (rev(c), g)),
                   pl.BlockSpec((Lc, 128 * G), lambda g, c: (rev(c), g)),
                   pl.BlockSpec((G, Lc, 128), per_g)],
        out_shape=[jax.ShapeDtypeStruct((S, 2048), F32), jax.ShapeDtypeStruct((S, 1024), F32),
                   jax.ShapeDtypeStruct((S, 1024), F32), jax.ShapeDtypeStruct((SSM_GROUPS, S, 128), F32)],
        scratch_shapes=[pltpu.VMEM((G, 2, 128, 128), F32)], compiler_params=_cp(2),
    )(xbc, xbc, xbc, dt_g, ac_g, ac_t, states, dy, dexp)


def _ssd_post(dhead, dt_g, sg_g, alog_g, *, name="ssd_post"):
    S = dhead.shape[1]
    nc = S // CHUNK
    Lc = CHUNK

    def body(dh_ref, dt_ref, sg_ref, al_ref, o_ref, s_ref):
        @pl.when(pl.program_id(0) == 0)
        def _():
            s_ref[...] = jnp.zeros_like(s_ref)

        lane = _iota((Lc, 128), 1)
        row = _iota((Lc, 128), 0)
        row8 = _iota((8, 128), 0)
        out = jnp.zeros((Lc, 128), F32)
        for g in range(SSM_GROUPS):
            dh = dh_ref[g]
            a_neg = -jnp.exp(al_ref[g])
            dac = jnp.where(lane < 4, pltpu.roll(dh, 124, axis=1), 0.0)
            end = jnp.where(lane < 4, pltpu.roll(dh, 120, axis=1), 0.0)
            k = 1
            while k < Lc:
                dac = dac + jnp.where(row < Lc - k, pltpu.roll(dac, Lc - k, axis=0), 0.0)
                k *= 2
            da = dac + end
            ddt = jnp.where(lane < 4, da * a_neg + dh, 0.0)
            ddtr = ddt * sg_ref[g]
            out = out + (ddtr if g == 0 else pltpu.roll(ddtr, 4 * g, axis=1))
            dal = jnp.sum(da * dt_ref[g], axis=0, keepdims=True) * a_neg
            dbias = jnp.sum(ddtr, axis=0, keepdims=True)
            part = jnp.where(row8 == 0, dal, jnp.where(row8 == 1, dbias, 0.0))
            s_ref[g] += part
        o_ref[...] = out.astype(BF16)

    grp = pl.BlockSpec((SSM_GROUPS, Lc, 128), lambda c: (0, c, 0))
    whole = lambda r: pl.BlockSpec((SSM_GROUPS, r, 128), lambda c: (0, 0, 0))
    return pl.pallas_call(
        body, name=name, grid=(nc,), in_specs=[grp, grp, grp, whole(1)],
        out_specs=[pl.BlockSpec((Lc, 128), lambda c: (c, 0)), whole(8)],
        out_shape=[jax.ShapeDtypeStruct((S, 128), BF16), jax.ShapeDtypeStruct((SSM_GROUPS, 8, 128), F32)],
        compiler_params=_cp(1),
    )(dhead, dt_g, sg_g, alog_g)


def _gate_fwd(y, xbc, zx, dexp, gn, *, name="gate_fwd", tr=256):
    S = y.shape[0]
    W = 2048
    gw = W // SSM_GROUPS

    def body(y_ref, x_ref, z_ref, d_ref, g_ref, o_ref):
        z = z_ref[...]
        u = (y_ref[...] + x_ref[...] * d_ref[...]) * (z * _sigmoid(z))
        gv = g_ref[...]
        for q in range(SSM_GROUPS):
            sl = slice(gw * q, gw * (q + 1))
            uq = u[:, sl]
            r = lax.rsqrt(jnp.mean(uq * uq, axis=-1, keepdims=True) + EPS)
            o_ref[:, sl] = (uq * r * gv[:, sl]).astype(BF16)

    row = pl.BlockSpec((tr, W), lambda i: (i, 0))
    vec = pl.BlockSpec((1, W), lambda i: (0, 0))
    return pl.pallas_call(
        body, name=name, grid=(S // tr,), in_specs=[row, row, row, vec, vec], out_specs=row,
        out_shape=jax.ShapeDtypeStruct((S, W), BF16), compiler_params=_cp(1),
    )(y, xbc, zx, dexp, gn)


def _gate_bwd(y, xbc, zx, dexp, gn, dout, *, name="gate_bwd", tr=256):
    S = y.shape[0]
    W = 2048
    gw = W // SSM_GROUPS
    steps = S // tr

    def body(y_ref, x_ref, z_ref, d_ref, g_ref, do_ref, dy_ref, dz_ref, dg_ref, dd_ref, acc_ref):
        i = pl.program_id(0)

        @pl.when(i == 0)
        def _():
            acc_ref[...] = jnp.zeros_like(acc_ref)

        z = z_ref[...]
        sg = _sigmoid(z)
        sz = z * sg
        xs = x_ref[...]
        yt = y_ref[...] + xs * d_ref[...]
        u = yt * sz
        gv = g_ref[...]
        do = do_ref[...]
        dgs = []
        for q in range(SSM_GROUPS):
            sl = slice(gw * q, gw * (q + 1))
            uq = u[:, sl]
            r = lax.rsqrt(jnp.mean(uq * uq, axis=-1, keepdims=True) + EPS)
            uh = uq * r
            dq = do[:, sl]
            duh = dq * gv[:, sl]
            duq = r * (duh - uh * jnp.mean(duh * uh, axis=-1, keepdims=True))
            dgs.append(jnp.sum(dq * uh, axis=0, keepdims=True))
            dyt = duq * sz[:, sl]
            dy_ref[:, sl] = dyt
            dz_ref[:, sl] = (duq * yt[:, sl] * (sg[:, sl] * (1.0 + z[:, sl] * (1.0 - sg[:, sl])))).astype(BF16)
            acc_ref[:, sl] += jnp.sum(dyt * xs[:, sl], axis=0, keepdims=True)
        dg = jnp.concatenate(dgs, axis=1)

        @pl.when(i == 0)
        def _():
            dg_ref[...] = dg

        @pl.when(i > 0)
        def _():
            dg_ref[...] += dg

        @pl.when(i == steps - 1)
        def _():
            ind = ((_iota((W, 128), 0) >> 6) == _iota((W, 128), 1)).astype(BF16)
            dd_ref[...] = _dot3(jnp.broadcast_to(acc_ref[...], (8, W)), ind)[0:1, :]

    row = pl.BlockSpec((tr, W), lambda i: (i, 0))
    vec = pl.BlockSpec((1, W), lambda i: (0, 0))
    return pl.pallas_call(
        body, name=name, grid=(steps,), in_specs=[row, row, row, vec, vec, row],
        out_specs=[row, row, vec, pl.BlockSpec((1, 128), lambda i: (0, 0))],
        out_shape=[jax.ShapeDtypeStruct((S, W), F32), jax.ShapeDtypeStruct((S, W), BF16),
                   jax.ShapeDtypeStruct((1, W), F32), jax.ShapeDtypeStruct((1, 128), F32)],
        scratch_shapes=[pltpu.VMEM((1, W), F32)], compiler_params=_cp(1),
    )(y, xbc, zx, dexp, gn, dout)


def _rope_cs(posf, *, name="rope_tables", tr=256):
    S = posf.shape[0]

    def body(p_ref, c_ref, s_ref):
        j = (_iota((tr, 128), 1) & 31).astype(F32)
        ang = p_ref[...] * jnp.exp(j * (-math.log(ROPE_THETA) / 32.0))
        c_ref[...] = jnp.cos(ang)
        s_ref[...] = jnp.sin(ang)

    blk = pl.BlockSpec((tr, 128), lambda i: (i, 0))
    return pl.pallas_call(
        body, name=name, grid=(S // tr,), in_specs=[pl.BlockSpec((tr, 1), lambda i: (i, 0))], out_specs=[blk, blk],
        out_shape=[jax.ShapeDtypeStruct((S, 128), F32)] * 2, compiler_params=_cp(1),
    )(posf)


def _rope_tables(c_ref, s_ref, shape):
    reps = shape[1] // 128
    return jnp.tile(c_ref[...], (1, reps)), jnp.tile(s_ref[...], (1, reps)), (_iota(shape, 1) & 63) < 32


def _hn_inds(W):
    ind = ((_iota((W, 128), 0) >> 6) == _iota((W, 128), 1)).astype(BF16)
    ind_t = ((_iota((128, W), 1) >> 6) == _iota((128, W), 0)).astype(BF16)
    return ind, ind_t


def _hnrope_fwd(xin, col0, W, gain_w, rope, *, name, tr=256):
    S = xin.shape[0]
    off = col0 // W
    nh = W // HEAD

    def body(x_ref, g_ref, c_ref, s_ref, o_ref):
        x = x_ref[...]
        ind, ind_t = _hn_inds(W)
        r = lax.rsqrt(_dot3(x * x, ind) * (1.0 / HEAD) + EPS)
        xn = x * _dot3(r, ind_t) * g_ref[...]
        cs, sn, half = _rope_tables(c_ref, s_ref, (tr, W))
        rot = jnp.where(half, -pltpu.roll(xn, W - 32, axis=1), pltpu.roll(xn, 32, axis=1))
        out = (xn * cs + rot * sn).astype(BF16)
        for h in range(nh):
            o_ref[h] = out[:, HEAD * h:HEAD * (h + 1)]

    tab = pl.BlockSpec((tr, 128), lambda i: (i, 0))
    return pl.pallas_call(
        body, name=name, grid=(S // tr,),
        in_specs=[pl.BlockSpec((tr, W), lambda i: (i, off)), pl.BlockSpec((1, W), lambda i: (0, 0)), tab, tab],
        out_specs=pl.BlockSpec((nh, tr, HEAD), lambda i: (0, i, 0)), out_shape=jax.ShapeDtypeStruct((nh, S, HEAD), BF16),
        compiler_params=_cp(1),
    )(xin, gain_w, *rope)


def _hnrope_bwd(xin, col0, W, gain_w, rope, dout, *, name, tr=256):
    S = xin.shape[0]
    off = col0 // W
    steps = S // tr
    nh = W // HEAD

    def body(x_ref, g_ref, c_ref, s_ref, do_ref, dx_ref, cs_ref, dg_ref, acc_ref):
        i = pl.program_id(0)
        x = x_ref[...]
        ind, ind_t = _hn_inds(W)
        r = lax.rsqrt(_dot3(x * x, ind) * (1.0 / HEAD) + EPS)
        rw = _dot3(r, ind_t)
        xh = x * rw
        cs, sn, half = _rope_tables(c_ref, s_ref, (tr, W))
        do = jnp.concatenate([do_ref[h] for h in range(nh)], axis=1).astype(F32)
        gs = do * sn
        g1 = do * cs + jnp.where(half, pltpu.roll(gs, W - 32, axis=1), -pltpu.roll(gs, 32, axis=1))
        dxh = g1 * g_ref[...]
        t = _dot3(dxh * xh, ind) * (1.0 / HEAD)
        dx = rw * (dxh - xh * _dot3(t, ind_t))
        dx_ref[...] = dx.astype(BF16)
        cpart = jnp.sum(dx, axis=0, keepdims=True)
        gpart = jnp.sum(g1 * xh, axis=0, keepdims=True)

        @pl.when(i == 0)
        def _():
            cs_ref[...] = cpart
            acc_ref[...] = gpart

        @pl.when(i > 0)
        def _():
            cs_ref[...] += cpart
            acc_ref[...] += gpart

        @pl.when(i == steps - 1)
        def _():
            fold = ((_iota((W, 128), 0) & 63) == _iota((W, 128), 1)).astype(BF16)
            dg_ref[...] = _dot3(jnp.broadcast_to(acc_ref[...], (8, W)), fold)[0:1, :]

    tab = pl.BlockSpec((tr, 128), lambda i: (i, 0))
    return pl.pallas_call(
        body, name=name, grid=(steps,),
        in_specs=[pl.BlockSpec((tr, W), lambda i: (i, off)), pl.BlockSpec((1, W), lambda i: (0, 0)), tab, tab,
                  pl.BlockSpec((nh, tr, HEAD), lambda i: (0, i, 0))],
        out_specs=[pl.BlockSpec((tr, W), lambda i: (i, 0)), pl.BlockSpec((1, W), lambda i: (0, 0)),
                   pl.BlockSpec((1, 128), lambda i: (0, 0))],
        out_shape=[jax.ShapeDtypeStruct((S, W), BF16), jax.ShapeDtypeStruct((1, W), F32),
                   jax.ShapeDtypeStruct((1, 128), F32)],
        scratch_shapes=[pltpu.VMEM((1, W), F32)], compiler_params=_cp(1),
    )(xin, gain_w, *rope, dout)


def _attn_band():
    qi = jnp.arange(ATT_G * WINDOW)[:, None] % WINDOW
    ki = jnp.arange(2 * WINDOW)[None, :]
    rel = qi + WINDOW - ki
    ok = (rel >= 0) & (rel < WINDOW)
    return jnp.stack([jnp.where(ok & (ki >= WINDOW), 0.0, NEG), jnp.where(ok, 0.0, NEG)]).astype(F32)


def _attn_probs(q, kb, sink_ref, band_ref, h, i):
    s = _dot(q, kb, 1, 1) * (HEAD ** -0.5) + band_ref[jnp.minimum(i, 1)]
    r1 = _iota((4 * WINDOW, 1), 0)
    sink = jnp.where(r1 < WINDOW, sink_ref[4 * h], jnp.where(r1 < 2 * WINDOW, sink_ref[4 * h + 1],
                     jnp.where(r1 < 3 * WINDOW, sink_ref[4 * h + 2], sink_ref[4 * h + 3])))
    m = jnp.maximum(jnp.max(s, axis=1, keepdims=True), sink)
    p = jnp.exp(s - m)
    ps = jnp.exp(sink - m)
    inv = 1.0 / (jnp.sum(p, axis=1, keepdims=True) + ps)
    return p * inv, ps * inv


ATT_HPS = 4
_BAND = pl.BlockSpec((2, ATT_G * WINDOW, 2 * WINDOW), lambda h, i: (0, 0, 0))


def _attn_specs(S):
    qspec = pl.BlockSpec((ATT_HPS, ATT_G, WINDOW, HEAD), lambda h, i: (h, 0, i, 0))
    cur = pl.BlockSpec((ATT_HPS, WINDOW, HEAD), lambda h, i: (h, i, 0))
    prev = pl.BlockSpec((ATT_HPS, WINDOW, HEAD), lambda h, i: (h, jnp.maximum(i - 1, 0), 0))
    tok = pl.BlockSpec((WINDOW, ATT_HPS * ATT_G * HEAD), lambda h, i: (i, h))
    return qspec, cur, prev, tok


def _attn_fwd(qh, kh, vh, sinks, *, name="attn_fwd"):
    S = kh.shape[1]
    nb = S // WINDOW

    def body(s_ref, band_ref, q_ref, kc_ref, kp_ref, vc_ref, vp_ref, o_ref):
        h2, i = pl.program_id(0), pl.program_id(1)
        outs = []
        for hh in range(ATT_HPS):
            q = q_ref[hh].reshape(ATT_G * WINDOW, HEAD)
            kb = jnp.concatenate([kp_ref[hh], kc_ref[hh]], axis=0)
            vb = jnp.concatenate([vp_ref[hh], vc_ref[hh]], axis=0)
            probs, _ = _attn_probs(q, kb, s_ref, band_ref, ATT_HPS * h2 + hh, i)
            o = _dot(probs.astype(BF16), vb).astype(BF16)
            outs += [o[WINDOW * g:WINDOW * (g + 1)] for g in range(ATT_G)]
        o_ref[...] = jnp.concatenate(outs, axis=1)

    qspec, cur, prev, tok = _attn_specs(S)
    return pl.pallas_call(
        body, name=name, grid=(ATT_KV // ATT_HPS, nb),
        in_specs=[pl.BlockSpec(memory_space=pltpu.SMEM), _BAND, qspec, cur, prev, cur, prev], out_specs=tok,
        out_shape=jax.ShapeDtypeStruct((S, ATT_KV * ATT_G * HEAD), BF16), compiler_params=_cp(2),
    )(sinks, _attn_band(), qh, kh, kh, vh, vh)


def _attn_bwd(qh, kh, vh, sinks, doh, *, name="attn_bwd"):
    S = kh.shape[1]
    nb = S // WINDOW

    def body(s_ref, band_ref, q_ref, kc_ref, kp_ref, vc_ref, vp_ref, do_ref, dq_ref, dk_ref, dv_ref, dsk_ref):
        h2, i = pl.program_id(0), pl.program_id(1)

        @pl.when(i == 0)
        def _():
            dk_ref[...] = jnp.zeros_like(dk_ref)
            dv_ref[...] = jnp.zeros_like(dv_ref)
            dsk_ref[...] = jnp.zeros_like(dsk_ref)

        dov = do_ref[...]
        cur = pl.multiple_of(i * WINDOW, WINDOW)
        lane = _iota((8, 128), 1)
        row = _iota((8, 128), 0)
        scale = HEAD ** -0.5
        for hh in range(ATT_HPS):
            q = q_ref[hh].reshape(ATT_G * WINDOW, HEAD)
            do = jnp.concatenate([dov[:, HEAD * (ATT_G * hh + g):HEAD * (ATT_G * hh + g + 1)] for g in range(ATT_G)], axis=0)
            kb = jnp.concatenate([kp_ref[hh], kc_ref[hh]], axis=0)
            vb = jnp.concatenate([vp_ref[hh], vc_ref[hh]], axis=0)
            probs, psink = _attn_probs(q, kb, s_ref, band_ref, ATT_HPS * h2 + hh, i)
            dp = _dot(do, vb, 1, 1)
            delta = jnp.sum(probs * dp, axis=1, keepdims=True)
            ds = (probs * (dp - delta)).astype(BF16)
            dq_ref[hh] = (_dot(ds, kb) * scale).reshape(ATT_G, WINDOW, HEAD)
            dkb = _dot(ds, q, 0, 0) * scale
            dvb = _dot(probs.astype(BF16), do, 0, 0)
            dk_ref[hh, pl.ds(cur, WINDOW), :] += dkb[WINDOW:, :]
            dv_ref[hh, pl.ds(cur, WINDOW), :] += dvb[WINDOW:, :]
            prv = pl.multiple_of(jnp.maximum(i - 1, 0) * WINDOW, WINDOW)
            dk_ref[hh, pl.ds(prv, WINDOW), :] += dkb[:WINDOW, :]
            dv_ref[hh, pl.ds(prv, WINDOW), :] += dvb[:WINDOW, :]

            dsr = -psink * delta
            upd = jnp.zeros((8, 128), F32)
            for gq in range(ATT_G):
                v = jnp.sum(dsr[gq * WINDOW:(gq + 1) * WINDOW, :], axis=0, keepdims=True)
                upd = jnp.where((lane == gq) & (row == 0), v, upd)
            dsk_ref[hh] += upd

    qspec, cur, prev, tok = _attn_specs(S)
    full = pl.BlockSpec((ATT_HPS, S, HEAD), lambda h, i: (h, 0, 0))
    return pl.pallas_call(
        body, name=name, grid=(ATT_KV // ATT_HPS, nb),
        in_specs=[pl.BlockSpec(memory_space=pltpu.SMEM), _BAND, qspec, cur, prev, cur, prev, tok],
        out_specs=[qspec, full, full, pl.BlockSpec((ATT_HPS, 8, 128), lambda h, i: (h, 0, 0))],
        out_shape=[jax.ShapeDtypeStruct((ATT_KV, ATT_G, S, HEAD), F32), jax.ShapeDtypeStruct((ATT_KV, S, HEAD), F32),
                   jax.ShapeDtypeStruct((ATT_KV, S, HEAD), F32), jax.ShapeDtypeStruct((ATT_KV, 8, 128), F32)],
        compiler_params=_cp(2),
    )(sinks, _attn_band(), qh, kh, kh, vh, vh, doh)


def _heads_major(t, nh):
    S = t.shape[0]
    return t.reshape(S, nh, HEAD).transpose(1, 0, 2)


def _tokens_major(t):
    nh, S, _ = t.shape
    return t.transpose(1, 0, 2).reshape(S, nh * HEAD)


class _NoComm:
    def late_weights(self, w, after):
        return w

    def grads(self, group, tensors, after):
        return None


def _local_step(x, posf, target, w, comm=None):
    S, D = x.shape
    gr = {}
    comm = comm or _NoComm()

    (h1,) = _rms_fwd(x, [w["a_norm"]], name="a_norm_f", dep=w.get("dep"))
    zx = _mm(h1, w["w_zx"], name="in_proj_zx")
    dtr = _mm(h1, w["w_dt"], name="in_proj_dt")
    xbc = _conv_silu_fwd(zx, 2048, 4096, w["a_conv_w"], w["a_conv_b"], name="a_conv_f")
    dt_g, ac_g, sg_g, ac_t = _ssd_prep(dtr, w["a_dt_bias"], w["a_A_log"])
    y_ssd, states = _ssd_fwd(xbc, dt_g, ac_g, ac_t)
    yg = _gate_fwd(y_ssd, xbc, zx, w["a_Dexp"], w["a_gnorm"])
    x1 = _mm(yg, w["a_out_proj"], res=x, name="out_proj")

    w = comm.late_weights(w, x1)
    FW = w["f_w_in"][0].shape[2]

    def ffn_fwd(xin, l):
        (h,) = _rms_fwd(xin, [w["f_norm"][l]], name=f"f_norm_f{l}")
        u = _mm(h, w["f_w_in"][l], name=f"f_in{l}", dims=(S, N_CHIPS * FW, D), tn=FW,
                b_spec=pl.BlockSpec((None, D, FW), lambda i, j, k: (j, 0, 0)))
        a = _ffn_act_fwd(u, w["f_conv_w"][l], w["f_conv_b"][l], name=f"f_act_f{l}")
        xo = _mm(a, w["f_w_down"][l], res=xin, tk=a.shape[1], name=f"f_down{l}")
        return xo, (h, u)

    x2, ffn0 = ffn_fwd(x1, 0)

    hk, hq = _rms_fwd(x2, [w["kv_norm"], w["b_norm"]], name="kvq_norm_f")
    kv = _mm(hk, w["w_kv"], bias=w["b_kv"], name="kv_proj")
    q = _mm(hq, w["w_q"], bias=w["b_q"], name="q_proj")
    rope = _rope_cs(posf)
    kr = _hnrope_fwd(kv, 0, 256, w["k_norm_w"], rope, name="k_rope_f")
    qr = _hnrope_fwd(q, 0, 1024, w["q_norm_w"], rope, name="q_rope_f")
    qh = qr.reshape(ATT_KV, ATT_G, S, HEAD)
    kh = kr
    vh = _heads_major(kv[:, 256:].astype(BF16), ATT_KV)
    att = _attn_fwd(qh, kh, vh, w["sinks"])
    x3 = _mm(att, w["w_o"], bias=w["b_o"], res=x2, name="o_proj")
    x4, ffn1 = ffn_fwd(x3, 1)

    dy, loss_part = _loss(x4, target)

    def ffn_bwd(xin, l, saved, dyo, want_colsum, dep=None):
        h, u = saved
        da = _mm(dyo, w["f_w_down"][l], tb=True, name=f"f_down_dx{l}", dep=dep)
        du, dcw, dcb, a = _ffn_act_bwd(u, w["f_conv_w"][l], w["f_conv_b"][l], da, name=f"f_act_b{l}")
        dw_down = _mm(a, dyo, ta=True, out_dtype=BF16, name=f"f_down_dw{l}")
        dw_in = _mm(h, du, ta=True, out_dtype=BF16, name=f"f_in_dw{l}", dims=(D, N_CHIPS * FW, S), tm=D, tn=FW, tk=S,
                    b_spec=pl.BlockSpec((None, S, FW), lambda i, j, k: (j // 2, 0, j % 2)),
                    o_spec=pl.BlockSpec((None, D, FW), lambda i, j, k: (j, i, 0)), o_shape=(N_CHIPS, D, FW))
        ts = _pick(S, (1024, 512, 256))
        pieces = [(pl.BlockSpec((None, ts, FW), lambda i, j, k, q=q: (q // 2, i, q % 2)),
                   pl.BlockSpec((None, 512, FW), lambda i, j, k, q=q: (q, j, 0))) for q in range(N_CHIPS)]
        dh = _mm(du, w["f_w_in"][l], tb=True, name=f"f_in_dx{l}", dims=(S, D, FW), tm=ts, tn=512, tk=FW,
                 a_spec=pieces[0][0], b_spec=pieces[0][1], more=pieces[1:])
        outs = _rms_bwd(xin, [w["f_norm"][l]], [dh], dyo, name=f"f_norm_b{l}", want_colsum=want_colsum)
        g = dict(f_norm=outs[1], f_w_in=dw_in, f_conv_w=dcw, f_conv_b=dcb, f_w_down=dw_down)
        return outs[0], g, (outs[2] if want_colsum else None)

    dx3, gr["ffn1"], db_o = ffn_bwd(x3, 1, ffn1, dy, True)
    gr["b_o"] = db_o
    gr["w_o"] = _mm(att, dx3, ta=True, out_dtype=BF16, name="o_proj_dw")
    datt = _mm(dx3, w["w_o"], tb=True, out_dtype=BF16, name="o_proj_dx")
    dqh, dkh, dvh, dsk = _attn_bwd(qh, kh, vh, w["sinks"], datt)
    gr["sinks"] = dsk[:, 0, :4].reshape(1, 16)
    dv = _tokens_major(dvh).astype(BF16)
    dq, db_q, dqn = _hnrope_bwd(q, 0, 1024, w["q_norm_w"], rope, dqh.reshape(16, S, HEAD), name="q_rope_b")
    dk, db_k, dkn = _hnrope_bwd(kv, 0, 256, w["k_norm_w"], rope, dkh, name="k_rope_b")
    gr["q_norm"], gr["k_norm"] = dqn[:, :HEAD], dkn[:, :HEAD]
    gr["b_q"] = db_q
    gr["b_kv"] = jnp.concatenate([db_k, _colsum(dv, name="dv_colsum")], axis=1)
    dkv = jnp.concatenate([dk, dv], axis=1)
    gr["w_q"] = _mm(hq, dq, ta=True, out_dtype=BF16, name="q_proj_dw")
    gr["w_kv"] = _mm(hk, dkv, ta=True, out_dtype=BF16, name="kv_proj_dw")
    tok = comm.grads(1, dict(f_down1=gr["ffn1"]["f_w_down"], f_in1=gr["ffn1"]["f_w_in"], w_o=gr["w_o"], w_q=gr["w_q"],
                             w_kv=gr["w_kv"]), None)
    dhq = _mm(dq, w["w_q"], tb=True, name="q_proj_dx", dep=tok)
    dhk = _mm(dkv, w["w_kv"], tb=True, name="kv_proj_dx")
    dx2, gr["kv_norm"], gr["b_norm"] = _rms_bwd(x2, [w["kv_norm"], w["b_norm"]], [dhk, dhq], dx3, name="kvq_norm_b")

    dx1, gr["ffn0"], _ = ffn_bwd(x1, 0, ffn0, dx2, False)

    gr["a_out_proj"] = _mm(yg, dx1, ta=True, out_dtype=BF16, name="out_proj_dw")
    tok = comm.grads(2, dict(f_down0=gr["ffn0"]["f_w_down"], f_in0=gr["ffn0"]["f_w_in"], out_proj=gr["a_out_proj"]), dx1)
    dyg = _mm(dx1, w["a_out_proj"], tb=True, name="out_proj_dx", dep=tok)
    dy_ssd, dz, gr["a_gnorm"], dD = _gate_bwd(y_ssd, xbc, zx, w["a_Dexp"], w["a_gnorm"], dyg)
    gr["a_D"] = dD[:, :SSM_HEADS]
    dxs, dB, dC, dhead = _ssd_bwd(xbc, dt_g, ac_g, ac_t, states, dy_ssd, w["a_Dexp"])
    ddtr, dsmall = _ssd_post(dhead, dt_g, sg_g, w["a_A_log_g"])
    gr["a_A_log"] = dsmall[:, 0, :4].reshape(1, SSM_HEADS)
    gr["a_dt_bias"] = dsmall[:, 1, :4].reshape(1, SSM_HEADS)
    dxbc, gr["a_conv_w"], gr["a_conv_b"] = _conv_silu_bwd(
        zx, 2048, 4096, w["a_conv_w"], w["a_conv_b"], [(dxs, 0), (dB, 2048), (dC, 3072)], name="a_conv_b")
    gr["w_z"] = _mm(h1, dz, ta=True, out_dtype=BF16, name="in_proj_dwz")
    gr["w_x"] = _mm(h1, dxbc, ta=True, out_dtype=BF16, name="in_proj_dwx")
    gr["w_dt"] = _mm(h1, ddtr, ta=True, out_dtype=BF16, name="in_proj_dwdt")
    ts = _pick(S, (1024, 512, 256))
    wblk = lambda q: pl.BlockSpec((512, 2048), lambda i, j, k: (j, q))
    dh1 = _mm(dz, w["w_zx"], tb=True, name="in_proj_dx", dims=(S, D, 2048), tm=ts, tn=512, tk=2048,
              a_spec=pl.BlockSpec((ts, 2048), lambda i, j, k: (i, 0)), b_spec=wblk(0),
              more=[(dxbc, pl.BlockSpec((ts, 2048), lambda i, j, k: (i, 0)), w["w_zx"], wblk(1)),
                    (dxbc, pl.BlockSpec((ts, 2048), lambda i, j, k: (i, 1)), w["w_zx"], wblk(2)),
                    (ddtr, pl.BlockSpec((ts, 128), lambda i, j, k: (i, 0)), w["w_dt"], pl.BlockSpec((512, 128), lambda i, j, k: (j, 0)))])
    dx0, gr["a_norm"] = _rms_bwd(x, [w["a_norm"]], [dh1], dx1, name="a_norm_b")
    comm.grads(3, dict(in_proj=_in_proj_grad(gr).reshape(D, N_CHIPS, -1).transpose(1, 0, 2)), dx0)
    return loss_part, dx0, gr


def _prep_small(full, w):
    w["a_norm"] = full["a_norm"]
    w["a_conv_w"] = full["a_conv_w"][0]
    w["a_conv_b"] = full["a_conv_b"]
    pad32 = lambda v: jnp.pad(v, ((0, 0), (0, 128 - SSM_HEADS)))
    w["a_dt_bias"] = pad32(full["a_dt_bias"])
    w["a_A_log"] = pad32(full["a_A_log"])
    w["a_A_log_g"] = jnp.pad(full["a_A_log"].reshape(SSM_GROUPS, 1, 4), ((0, 0), (0, 0), (0, 124)))
    w["a_Dexp"] = jnp.repeat(full["a_D"], HEAD, axis=1)
    w["a_gnorm"] = full["a_gnorm"]
    w["f_norm"] = [full["f_norm"][l:l + 1] for l in range(2)]
    w["f_conv_w"] = [full["f_conv_w"][l] for l in range(2)]
    w["f_conv_b"] = [full["f_conv_b"][l:l + 1] for l in range(2)]
    w["kv_norm"] = full["kv_norm"].reshape(1, -1)
    w["b_kv"] = full["b_kv"].reshape(1, -1)
    w["k_norm_w"] = jnp.tile(full["k_norm"].reshape(1, HEAD), (1, ATT_KV))
    w["b_norm"] = full["b_norm"]
    w["b_q"] = full["b_q"]
    w["q_norm_w"] = jnp.tile(full["q_norm"], (1, ATT_KV * ATT_G))
    w["sinks"] = full["sinks"].reshape(-1)
    w["b_o"] = full["b_o"]
    return w


def _split_in_proj(ip):
    return ip[:, :6144].astype(BF16), jnp.pad(ip[:, 6144:], ((0, 0), (0, 128 - SSM_HEADS))).astype(BF16)


def _prep_weights(full):
    w = _prep_small(full, {})
    w["w_zx"], w["w_dt"] = _split_in_proj(full["a_in_proj"][0])
    w["a_out_proj"] = full["a_out_proj"][0].astype(BF16)
    w["f_w_in"] = [full["f_w_in"][l].reshape(1024, N_CHIPS, -1).transpose(1, 0, 2).astype(BF16) for l in range(2)]
    w["f_w_down"] = [full["f_w_down"][l].astype(BF16) for l in range(2)]
    w["w_kv"] = full["w_kv"].astype(BF16)
    w["w_q"] = full["w_q"][0].astype(BF16)
    w["w_o"] = full["w_o"][0].astype(BF16)
    return w


def _small_grads(gr):
    g = {}
    g["a_norm"] = gr["a_norm"]
    g["a_conv_w"] = gr["a_conv_w"][None]
    g["a_conv_b"] = gr["a_conv_b"]
    g["a_dt_bias"], g["a_A_log"], g["a_D"] = gr["a_dt_bias"], gr["a_A_log"], gr["a_D"]
    g["a_gnorm"] = gr["a_gnorm"]
    g["kv_norm"] = gr["kv_norm"].reshape(-1)
    g["b_kv"] = gr["b_kv"].reshape(-1)
    g["k_norm"] = gr["k_norm"].reshape(-1)
    g["b_norm"] = gr["b_norm"]
    g["b_q"] = gr["b_q"]
    g["q_norm"] = gr["q_norm"]
    g["sinks"] = gr["sinks"]
    g["b_o"] = gr["b_o"]
    f = [gr["ffn0"], gr["ffn1"]]
    g["f_norm"] = jnp.concatenate([f[0]["f_norm"], f[1]["f_norm"]], axis=0)
    g["f_conv_w"] = jnp.stack([f[l]["f_conv_w"] for l in range(2)])
    g["f_conv_b"] = jnp.concatenate([f[l]["f_conv_b"] for l in range(2)], axis=0)
    return g


def _in_proj_grad(gr):
    return jnp.concatenate([gr["w_z"], gr["w_x"], gr["w_dt"][:, :SSM_HEADS]], axis=1)


def _full_grads(gr):
    g = _small_grads(gr)
    f32 = lambda t: t.astype(F32)
    g["a_in_proj"] = f32(_in_proj_grad(gr))[None]
    g["a_out_proj"] = f32(gr["a_out_proj"])[None]
    g["w_kv"] = f32(gr["w_kv"])
    g["w_q"] = f32(gr["w_q"])[None]
    g["w_o"] = f32(gr["w_o"])[None]
    f = [gr["ffn0"], gr["ffn1"]]
    g["f_w_in"] = jnp.stack([f32(f[l]["f_w_in"]).transpose(1, 0, 2).reshape(1024, -1) for l in range(2)])
    g["f_w_down"] = jnp.stack([f32(f[l]["f_w_down"]) for l in range(2)])
    return g


MESH = pl.DeviceIdType.MESH
WEIGHTS = ("a_norm", "a_in_proj", "a_conv_w", "a_conv_b", "a_dt_bias", "a_A_log", "a_D", "a_gnorm", "a_out_proj",
           "kv_norm", "w_kv", "b_kv", "k_norm", "b_norm", "w_q", "b_q", "q_norm", "sinks", "w_o", "b_o", "f_norm",
           "f_w_in", "f_conv_w", "f_conv_b", "f_w_down")
MATS = (("in_proj", "a_in_proj", 0), ("out_proj", "a_out_proj", 0), ("w_kv", "w_kv", None), ("w_q", "w_q", 0),
        ("w_o", "w_o", 0), ("f_in0", "f_w_in", 0), ("f_in1", "f_w_in", 1), ("f_down0", "f_w_down", 0),
        ("f_down1", "f_w_down", 1))
SMALL_CUT = (("a_norm", 1), ("a_conv_w", 2), ("a_conv_b", 1), ("a_gnorm", 1), ("f_conv_w", 2))
SMALL_REP = ("a_dt_bias", "a_A_log", "a_D", "kv_norm", "b_kv", "k_norm", "b_norm", "b_q", "q_norm", "sinks", "b_o",
             "f_norm", "f_conv_b")


def _coords():
    return lax.axis_index("x"), lax.axis_index("y"), lax.axis_index("c")


def _other_chips(x, y):
    return [(1 - x, y), (x, 1 - y), (1 - x, 1 - y)]


def _pack(arrs, rows_align, lanes, dtype):
    flat = jnp.concatenate([a.reshape(-1).astype(dtype) for a in arrs])
    per = rows_align * lanes
    total = -(-flat.shape[0] // per) * per
    return jnp.pad(flat, (0, total - flat.shape[0])).reshape(total // lanes, lanes)


def _unpack(flat, shapes):
    out, off = [], 0
    for s in shapes:
        n = math.prod(s)
        out.append(flat[off:off + n].reshape(s))
        off += n
    return out


def _remote(src, dst, send, recv, k, dev):
    return pltpu.make_async_remote_copy(src_ref=src, dst_ref=dst, send_sem=send.at[k], recv_sem=recv.at[k],
                                        device_id=dev, device_id_type=MESH)


_ANY = pl.BlockSpec(memory_space=pl.ANY)


def _halves(t):
    r, c = t.shape
    return t.reshape(2, r // 2, c)


def _gather_weights(shards, sp):
    n = len(shards)
    n_sem = 7 * n + 3

    def body(*refs):
        sh, sp_ref = refs[:n], refs[n]
        outs, sout = refs[n + 1:2 * n + 1], refs[2 * n + 1]
        send, recv, loc = refs[2 * n + 2:]
        x, y, c = _coords()
        me = 2 * x + y
        chips = _other_chips(x, y)
        sib = (x, y, 1 - c)
        l1 = pltpu.make_async_copy(sp_ref, sout.at[me], loc.at[0])
        l1.start()
        sends = []
        for j, (cx, cy) in enumerate(chips):
            sends.append(_remote(sp_ref, sout.at[me], send, recv, 7 * n + j, (cx, cy, c)))
            for t in range(n):
                sends.append(_remote(sh[t].at[c], outs[t].at[me, c], send, recv, 7 * t + j, (cx, cy, c)))
        for t in range(n):
            sends.append(_remote(sh[t], outs[t].at[me], send, recv, 7 * t + 6, sib))
        for cp in sends:
            cp.start()
        for j, (cx, cy) in enumerate(chips):
            src = 2 * cx + cy
            for t in range(n):
                _remote(sh[t].at[c], outs[t].at[src, c], send, recv, 7 * t + j, (cx, cy, c)).wait_recv()
                fwd = _remote(outs[t].at[src, c], outs[t].at[src, c], send, recv, 7 * t + 3 + j, sib)
                fwd.start()
                sends.append(fwd)
        for j, (cx, cy) in enumerate(chips):
            src = 2 * cx + cy
            _remote(sp_ref, sout.at[src], send, recv, 7 * n + j, (cx, cy, c)).wait_recv()
            for t in range(n):
                _remote(outs[t].at[src, 1 - c], outs[t].at[src, 1 - c], send, recv, 7 * t + 3 + j, sib).wait_recv()
        for t in range(n):
            _remote(sh[t], outs[t].at[me], send, recv, 7 * t + 6, sib).wait_recv()
        for cp in sends:
            cp.wait_send()
        l1.wait()

    res = pl.pallas_call(
        body, name="gather_weights", in_specs=[_ANY] * (n + 1), out_specs=[_ANY] * (n + 1),
        out_shape=[jax.ShapeDtypeStruct((N_CHIPS,) + t.shape, t.dtype) for t in shards]
        + [jax.ShapeDtypeStruct((N_CHIPS,) + sp.shape, sp.dtype)],
        scratch_shapes=[pltpu.SemaphoreType.DMA((n_sem,)), pltpu.SemaphoreType.DMA((n_sem,)),
                        pltpu.SemaphoreType.DMA((1,))],
    )(*shards, sp)
    return res[:n], res[n]


_HBM = pl.BlockSpec(memory_space=pltpu.HBM)
_SEMS = pl.BlockSpec(memory_space=pltpu.SEMAPHORE)
_DATAFLOW = pltpu.SideEffectType.DATAFLOW_SIDE_EFFECTING


def _in_hbm(a):
    return pltpu.with_memory_space_constraint(a, pltpu.HBM)


def _gather_copies(sh, land, send, recv):
    x, y, c = _coords()
    me = 2 * x + y
    out = []
    for t in range(len(sh)):
        for j, (cx, cy) in enumerate(_other_chips(x, y)):
            dev = (cx, cy, c)
            out.append((_remote(sh[t].at[c], land[t].at[me, c], send, recv, 4 * t + j, dev),
                        _remote(sh[t].at[c], land[t].at[2 * cx + cy, c], send, recv, 4 * t + j, dev)))
        sib = (x, y, 1 - c)
        out.append((_remote(sh[t], land[t].at[me], send, recv, 4 * t + 3, sib),
                    _remote(sh[t], land[t].at[me], send, recv, 4 * t + 3, sib)))
    return out


def _gather_start(shards, after, *, name):
    n = len(shards)

    def body(*refs):
        sh, land = refs[:n], refs[n:2 * n]
        send, recv = refs[2 * n + 1], refs[2 * n + 2]
        token = refs[-1]
        for mine, _ in _gather_copies(sh, land, send, recv):
            mine.start()
        token[...] = jnp.zeros_like(token)

    lands = [_in_hbm(lax.empty((N_CHIPS,) + s.shape, s.dtype)) for s in shards]
    res = pl.pallas_call(
        body, name=name, in_specs=[_HBM] * (2 * n) + [_ANY],
        out_specs=[_SEMS, _SEMS] + [_HBM] * (2 * n) + [pl.BlockSpec(memory_space=pltpu.VMEM)],
        out_shape=[pltpu.SemaphoreType.DMA((4 * n,)), pltpu.SemaphoreType.DMA((4 * n,))]
        + [pltpu.HBM(s.shape, s.dtype) for s in shards] + [pltpu.HBM(l.shape, l.dtype) for l in lands]
        + [jax.ShapeDtypeStruct((8, 128), F32)],
        input_output_aliases={t: 2 + t for t in range(2 * n)},
        compiler_params=pltpu.CompilerParams(has_side_effects=_DATAFLOW),
    )(*[_in_hbm(s) for s in shards], *lands, after)
    return res[0], res[1], res[2:2 + n], res[2 + n:2 + 2 * n], res[-1]


def _gather_wait(send, recv, shards, lands, after, *, name):
    n = len(shards)

    def body(*refs):
        sh, land = refs[:n], refs[n:2 * n]
        send_r, recv_r = refs[2 * n], refs[2 * n + 1]
        for mine, theirs in _gather_copies(sh, land, send_r, recv_r):
            mine.wait_send()
            theirs.wait_recv()

    res = pl.pallas_call(
        body, name=name, in_specs=[_HBM] * (2 * n) + [_SEMS, _SEMS, _ANY], out_specs=[_HBM] * (2 * n),
        out_shape=[pltpu.HBM(s.shape, s.dtype) for s in shards] + [pltpu.HBM(l.shape, l.dtype) for l in lands],
        input_output_aliases={t: t for t in range(2 * n)},
        compiler_params=pltpu.CompilerParams(has_side_effects=_DATAFLOW),
    )(*shards, *lands, send, recv, after)
    return res[n:]


def _gather_forward(lands, *, name):
    n = len(lands)

    def body(*refs):
        o = refs[n:2 * n]
        send, recv = refs[2 * n:]
        x, y, c = _coords()
        sib = (x, y, 1 - c)
        srcs = [2 * cx + cy for cx, cy in _other_chips(x, y)]
        cps = [_remote(o[t].at[s, c], o[t].at[s, c], send, recv, 3 * t + j, sib) for t in range(n) for j, s in enumerate(srcs)]
        for cp in cps:
            cp.start()
        for t in range(n):
            for j, s in enumerate(srcs):
                _remote(o[t].at[s, 1 - c], o[t].at[s, 1 - c], send, recv, 3 * t + j, sib).wait_recv()
        for cp in cps:
            cp.wait_send()

    return pl.pallas_call(
        body, name=name, in_specs=[_ANY] * n, out_specs=[_ANY] * n, input_output_aliases={t: t for t in range(n)},
        out_shape=[jax.ShapeDtypeStruct(l.shape, l.dtype) for l in lands],
        scratch_shapes=[pltpu.SemaphoreType.DMA((3 * n,)), pltpu.SemaphoreType.DMA((3 * n,))],
    )(*lands)


def _small_copies(v, land, send, recv):
    x, y, c = _coords()
    me = 4 * x + 2 * y + c
    out = []
    for k in range(1, 8):
        px = 1 - x if k & 4 else x
        py = 1 - y if k & 2 else y
        pc = 1 - c if k & 1 else c
        out.append((_remote(v, land.at[me], send, recv, k - 1, (px, py, pc)),
                    _remote(v, land.at[4 * px + 2 * py + pc], send, recv, k - 1, (px, py, pc))))
    return out


def _small_start(v, after, *, name):
    def body(v_ref, land_ref, after_ref, send, recv, v_thru, land_thru, token):
        for mine, _ in _small_copies(v_ref, land_ref, send, recv):
            mine.start()
        token[...] = jnp.zeros_like(token)

    land = _in_hbm(lax.empty((8,) + v.shape, v.dtype))
    return pl.pallas_call(
        body, name=name, in_specs=[_HBM, _HBM, _ANY],
        out_specs=[_SEMS, _SEMS, _HBM, _HBM, pl.BlockSpec(memory_space=pltpu.VMEM)],
        out_shape=[pltpu.SemaphoreType.DMA((7,)), pltpu.SemaphoreType.DMA((7,)), pltpu.HBM(v.shape, v.dtype),
                   pltpu.HBM(land.shape, land.dtype), jax.ShapeDtypeStruct((8, 128), F32)],
        input_output_aliases={0: 2, 1: 3}, compiler_params=pltpu.CompilerParams(has_side_effects=_DATAFLOW),
    )(_in_hbm(v), land, after)


def _small_wait(send, recv, v, land, after, *, name):
    def body(v_ref, land_ref, send_r, recv_r, *rest):
        for mine, theirs in _small_copies(v_ref, land_ref, send_r, recv_r):
            mine.wait_send()
            theirs.wait_recv()

    return pl.pallas_call(
        body, name=name, in_specs=[_HBM, _HBM, _SEMS, _SEMS] + [_ANY] * len(after), out_specs=[_HBM, _HBM],
        out_shape=[pltpu.HBM(v.shape, v.dtype), pltpu.HBM(land.shape, land.dtype)],
        input_output_aliases={0: 0, 1: 1}, compiler_params=pltpu.CompilerParams(has_side_effects=_DATAFLOW),
    )(v, land, send, recv, *after)


def _small_sum(v, land, me_idx, *, name="small_sum"):
    def body(me_ref, v_ref, land_ref, o_ref):
        acc = None
        for s in range(8):
            term = jnp.where(me_ref[0] == s, v_ref[...], land_ref[s])
            acc = term if acc is None else acc + term
        o_ref[...] = acc

    whole = lambda shape: pl.BlockSpec(shape, lambda i, me_ref: (0,) * len(shape))
    return pl.pallas_call(
        body, name=name,
        grid_spec=pltpu.PrefetchScalarGridSpec(num_scalar_prefetch=1, grid=(1,), in_specs=[whole(v.shape), whole(land.shape)],
                                               out_specs=whole(v.shape)),
        out_shape=jax.ShapeDtypeStruct(v.shape, F32), compiler_params=_cp(1),
    )(me_idx, v, land)


def _rs_to_sibling(gs, *, name):
    n = len(gs)

    def body(*refs):
        g, a = refs[:n], refs[n:2 * n]
        send, recv = refs[2 * n:]
        x, y, c = _coords()
        cps = [_remote(g[t].at[:, 1 - c], a[t], send, recv, t, (x, y, 1 - c)) for t in range(n)]
        for cp in cps:
            cp.start()
        for cp in cps:
            cp.wait()

    return pl.pallas_call(
        body, name=name, in_specs=[_ANY] * n, out_specs=[_ANY] * n,
        out_shape=[jax.ShapeDtypeStruct((N_CHIPS,) + g.shape[2:], g.dtype) for g in gs],
        scratch_shapes=[pltpu.SemaphoreType.DMA((n,)), pltpu.SemaphoreType.DMA((n,))],
    )(*gs)


RS_ROW_SPLIT = 2


def _rs_add_pair(gs, as_, c_idx, *, name):
    n = len(gs)

    def body(c_ref, *refs):
        for t in range(n):
            refs[2 * n + t][...] = (refs[t][...].astype(F32) + refs[n + t][...].astype(F32)).astype(BF16)

    def gspec(g):
        _, _, rh, cols = g.shape
        return pl.BlockSpec((None, None, rh // RS_ROW_SPLIT, cols), lambda j, i, c_ref: (j, c_ref[0], i, 0))

    def pspec(g):
        _, _, rh, cols = g.shape
        return pl.BlockSpec((None, rh // RS_ROW_SPLIT, cols), lambda j, i, c_ref: (j, i, 0))

    return pl.pallas_call(
        body, name=name,
        grid_spec=pltpu.PrefetchScalarGridSpec(
            num_scalar_prefetch=1, grid=(N_CHIPS, RS_ROW_SPLIT),
            in_specs=[gspec(g) for g in gs] + [pspec(g) for g in gs], out_specs=[pspec(g) for g in gs]),
        out_shape=[jax.ShapeDtypeStruct((N_CHIPS,) + g.shape[2:], BF16) for g in gs], compiler_params=_cp(2),
    )(c_idx, *gs, *as_)


def _chips_copies(p, r, send, recv):
    x, y, c = _coords()
    return [_remote(p[t].at[2 * cx + cy], r[t].at[k], send, recv, 3 * t + k, (cx, cy, c))
            for k, (cx, cy) in enumerate(_other_chips(x, y)) for t in range(len(p))]


def _rs_chips_start(ps, *, name):
    n = len(ps)

    def body(*refs):
        p, r = refs[:n], refs[n:2 * n]
        send, recv = refs[2 * n], refs[2 * n + 1]
        token = refs[-1]
        for cp in _chips_copies(p, r, send, recv):
            cp.start()
        token[...] = jnp.zeros_like(token)

    lands = [_in_hbm(lax.empty((3,) + p.shape[1:], p.dtype)) for p in ps]
    res = pl.pallas_call(
        body, name=name, in_specs=[_HBM] * (2 * n),
        out_specs=[_SEMS, _SEMS] + [_HBM] * (2 * n) + [pl.BlockSpec(memory_space=pltpu.VMEM)],
        out_shape=[pltpu.SemaphoreType.DMA((3 * n,)), pltpu.SemaphoreType.DMA((3 * n,))]
        + [pltpu.HBM(p.shape, p.dtype) for p in ps] + [pltpu.HBM(l.shape, l.dtype) for l in lands]
        + [jax.ShapeDtypeStruct((8, 128), F32)],
        input_output_aliases={t: 2 + t for t in range(2 * n)},
        compiler_params=pltpu.CompilerParams(has_side_effects=_DATAFLOW),
    )(*[_in_hbm(p) for p in ps], *lands)
    return res[0], res[1], res[2:2 + n], res[2 + n:2 + 2 * n], res[-1]


def _rs_chips_wait(send, recv, ps, lands, after, *, name):
    n = len(ps)

    def body(*refs):
        p, r = refs[:n], refs[n:2 * n]
        for cp in _chips_copies(p, r, refs[2 * n], refs[2 * n + 1]):
            cp.wait_send()
            cp.wait_recv()

    res = pl.pallas_call(
        body, name=name, in_specs=[_HBM] * (2 * n) + [_SEMS, _SEMS] + [_ANY] * len(after), out_specs=[_HBM] * (2 * n),
        out_shape=[pltpu.HBM(p.shape, p.dtype) for p in ps] + [pltpu.HBM(l.shape, l.dtype) for l in lands],
        input_output_aliases={t: t for t in range(2 * n)},
        compiler_params=pltpu.CompilerParams(has_side_effects=_DATAFLOW),
    )(*ps, *lands, send, recv, *after)
    return res[:n], res[n:]


def _rs_add_chips(ps, rs, idx, *, name):
    n = len(ps)

    def body(idx_ref, *refs):
        for t in range(n):
            p_ref, r0, r1, r2 = refs[4 * t:4 * t + 4]
            refs[4 * n + t][...] = ((p_ref[...].astype(F32) + r0[...].astype(F32)) + r1[...].astype(F32)) + r2[...].astype(F32)

    in_specs, args = [], []
    for p, r in zip(ps, rs):
        _, rh, cols = p.shape
        blk = (None, rh // RS_ROW_SPLIT, cols)
        in_specs.append(pl.BlockSpec(blk, lambda i, idx_ref: (idx_ref[0], i, 0)))
        in_specs += [pl.BlockSpec(blk, lambda i, idx_ref, k=k: (k, i, 0)) for k in range(3)]
        args += [p, r, r, r]
    out_specs = [pl.BlockSpec((None, p.shape[1] // RS_ROW_SPLIT, p.shape[2]), lambda i, idx_ref: (idx_ref[1], i, 0))
                 for p in ps]
    return pl.pallas_call(
        body, name=name,
        grid_spec=pltpu.PrefetchScalarGridSpec(num_scalar_prefetch=1, grid=(RS_ROW_SPLIT,), in_specs=in_specs,
                                               out_specs=out_specs),
        out_shape=[jax.ShapeDtypeStruct((2,) + p.shape[1:], F32) for p in ps], compiler_params=_cp(1),
    )(idx, *args)


def _rs_join_halves(hs, *, name):
    n = len(hs)

    def body(*refs):
        o = refs[n:2 * n]
        send, recv = refs[2 * n:]
        x, y, c = _coords()
        cps = [_remote(o[t].at[c], o[t].at[c], send, recv, t, (x, y, 1 - c)) for t in range(n)]
        for cp in cps:
            cp.start()
        for t in range(n):
            _remote(o[t].at[1 - c], o[t].at[1 - c], send, recv, t, (x, y, 1 - c)).wait_recv()
        for cp in cps:
            cp.wait_send()

    return pl.pallas_call(
        body, name=name, in_specs=[_ANY] * n, out_specs=[_ANY] * n,
        input_output_aliases={t: t for t in range(n)},
        out_shape=[jax.ShapeDtypeStruct(h.shape, F32) for h in hs],
        scratch_shapes=[pltpu.SemaphoreType.DMA((n,)), pltpu.SemaphoreType.DMA((n,))],
    )(*hs)


def _adamw(w, gs, m, v, *, name, dep=None):
    L, Rr, C = w.shape
    tr, tc = _pick(Rr, (256, 128, 64)), C
    if tr == Rr and Rr * C > 512 * 1024:
        tc = 256
    bc1 = 1.0 - ADAM_B1 ** ADAM_STEP
    bc2 = 1.0 - ADAM_B2 ** ADAM_STEP
    nd = 0 if dep is None else 1

    def body(*refs):
        w_ref, m_ref, v_ref = refs[0], refs[1], refs[2]
        g_refs = refs[3:3 + L]
        d_ref, mo_ref, vo_ref, go_ref = refs[3 + L + nd:]
        layer = pl.program_id(0)
        gv = g_refs[0][...]
        for q in range(1, L):
            gv = jnp.where(layer == q, g_refs[q][...], gv)
        mn = ADAM_B1 * m_ref[...] + (1.0 - ADAM_B1) * gv
        vn = ADAM_B2 * v_ref[...] + (1.0 - ADAM_B2) * (gv * gv)
        go_ref[...] = gv
        mo_ref[...] = mn
        vo_ref[...] = vn
        d_ref[...] = -ADAM_LR * ((mn / bc1) / (jnp.sqrt(vn / bc2) + ADAM_EPS) + ADAM_WD * w_ref[...])

    blk = pl.BlockSpec((None, tr, tc), lambda l, i, j: (l, i, j))
    gblks = [pl.BlockSpec((tr, tc), lambda l, i, j, q=q: (jnp.where(l == q, i, 0), jnp.where(l == q, j, 0))) for q in range(L)]
    return pl.pallas_call(
        body, name=name, grid=(L, Rr // tr, C // tc), in_specs=[blk] * 3 + gblks + [_ANY] * nd, out_specs=[blk] * 4,
        out_shape=[jax.ShapeDtypeStruct((L, Rr, C), F32)] * 4, compiler_params=_cp(3),
    )(w, m, v, *gs, *([] if dep is None else [dep]))


def kernel(x, positions, a_norm, a_in_proj, a_conv_w, a_conv_b, a_dt_bias, a_A_log, a_D, a_gnorm, a_out_proj,
           kv_norm, w_kv, b_kv, k_norm, b_norm, w_q, b_q, q_norm, sinks, w_o, b_o, f_norm, f_w_in, f_conv_w,
           f_conv_b, f_w_down, loss_target, m_a_norm, m_a_in_proj, m_a_conv_w, m_a_conv_b, m_a_dt_bias, m_a_A_log,
           m_a_D, m_a_gnorm, m_a_out_proj, m_kv_norm, m_w_kv, m_b_kv, m_k_norm, m_b_norm, m_w_q, m_b_q, m_q_norm,
           m_sinks, m_w_o, m_b_o, m_f_norm, m_f_w_in, m_f_conv_w, m_f_conv_b, m_f_w_down, v_a_norm, v_a_in_proj,
           v_a_conv_w, v_a_conv_b, v_a_dt_bias, v_a_A_log, v_a_D, v_a_gnorm, v_a_out_proj, v_kv_norm, v_w_kv,
           v_b_kv, v_k_norm, v_b_norm, v_w_q, v_b_q, v_q_norm, v_sinks, v_w_o, v_b_o, v_f_norm, v_f_w_in,
           v_f_conv_w, v_f_conv_b, v_f_w_down):
    wl = dict(zip(WEIGHTS, (a_norm, a_in_proj, a_conv_w, a_conv_b, a_dt_bias, a_A_log, a_D, a_gnorm, a_out_proj,
                            kv_norm, w_kv, b_kv, k_norm, b_norm, w_q, b_q, q_norm, sinks, w_o, b_o, f_norm, f_w_in,
                            f_conv_w, f_conv_b, f_w_down)))
    ml = dict(zip(WEIGHTS, (m_a_norm, m_a_in_proj, m_a_conv_w, m_a_conv_b, m_a_dt_bias, m_a_A_log, m_a_D, m_a_gnorm,
                            m_a_out_proj, m_kv_norm, m_w_kv, m_b_kv, m_k_norm, m_b_norm, m_w_q, m_b_q, m_q_norm,
                            m_sinks, m_w_o, m_b_o, m_f_norm, m_f_w_in, m_f_conv_w, m_f_conv_b, m_f_w_down)))
    vl = dict(zip(WEIGHTS, (v_a_norm, v_a_in_proj, v_a_conv_w, v_a_conv_b, v_a_dt_bias, v_a_A_log, v_a_D, v_a_gnorm,
                            v_a_out_proj, v_kv_norm, v_w_kv, v_b_kv, v_k_norm, v_b_norm, v_w_q, v_b_q, v_q_norm,
                            v_sinks, v_w_o, v_b_o, v_f_norm, v_f_w_in, v_f_conv_w, v_f_conv_b, v_f_w_down)))
    xi, yi, ci = _coords()
    me = 2 * xi + yi
    S = x.shape[1]

    def block_of(n, layer):
        t = wl[n]
        return t if layer is None else t[layer]

    rows = lambda t: t.reshape(-1, t.shape[-1])
    c_idx = jnp.reshape(ci, (1,)).astype(jnp.int32)
    me_c = jnp.stack([me, ci]).astype(jnp.int32)
    early = ("in_proj", "out_proj")
    late = tuple(name for name, _, _ in MATS if name not in early)
    shards = {name: _halves(block_of(wn, layer).astype(BF16)) for name, wn, layer in MATS}

    sp = _pack([wl[n] for n, _ in SMALL_CUT], 8, 128, F32)
    gathered, gs = _gather_weights([shards[k] for k in early], sp)
    gt = {k: t.reshape(N_CHIPS, -1, t.shape[-1]) for k, t in zip(early, gathered)}
    started = _gather_start([shards[k] for k in late], gs, name="gather_late_start")
    full = {n: wl[n] for n in SMALL_REP}
    gs = gs.reshape(N_CHIPS, -1)
    pieces = [_unpack(gs[j], [wl[n].shape for n, _ in SMALL_CUT]) for j in range(N_CHIPS)]
    for q, (n, ax) in enumerate(SMALL_CUT):
        full[n] = jnp.concatenate([pieces[j][q] for j in range(N_CHIPS)], axis=ax)
    w = _prep_small(full, {})
    w["w_zx"], w["w_dt"] = _split_in_proj(gt["in_proj"].transpose(1, 0, 2).reshape(1024, -1))
    w["a_out_proj"] = rows(gt["out_proj"])
    w["dep"] = started[4]

    class Comm:
        pending = None
        token = None
        reduced = {}

        def late_weights(self, w, after):
            lands = _gather_wait(started[0], started[1], started[2], started[3], after, name="gather_late_wait")
            lands = _gather_forward(lands, name="gather_late_forward")
            lt = {k: t.reshape(N_CHIPS, -1, t.shape[-1]) for k, t in zip(late, lands)}
            w = dict(w)
            w["w_kv"], w["w_q"], w["w_o"] = (rows(lt[k]) for k in ("w_kv", "w_q", "w_o"))
            w["f_w_in"] = [lt["f_in0"], lt["f_in1"]]
            w["f_w_down"] = [rows(lt["f_down0"]), rows(lt["f_down1"])]
            return w

        def finish(self, after):
            names, send, recv, ps, lands, tag = self.pending
            ps, rs = _rs_chips_wait(send, recv, ps, lands, after, name=f"rs_chips_wait{tag}")
            halves = _rs_add_chips(ps, rs, me_c, name=f"rs_add_chips{tag}")
            joined = _rs_join_halves(halves, name=f"rs_join_halves{tag}")
            self.reduced.update({k: rows(t) for k, t in zip(names, joined)})
            self.pending = None

        def grads(self, group, tensors, after):
            if self.pending is not None:
                self.finish([after])
            names = list(tensors)
            glist = [tensors[k].reshape(N_CHIPS, 2, -1, tensors[k].shape[-1]) for k in names]
            from_sib = _rs_to_sibling(glist, name=f"rs_to_sibling{group}")
            pairs = _rs_add_pair(glist, from_sib, c_idx, name=f"rs_add_pair{group}")
            send, recv, ps, lands, token = _rs_chips_start(pairs, name=f"rs_chips_start{group}")
            self.pending = (names, send, recv, ps, lands, group)
            self.token = token
            return token

    comm = Comm()

    posf = positions.reshape(S, 1).astype(F32)
    loss_part, dx0, gr = _local_step(x[0], posf, loss_target[0], w, comm)
    g = _small_grads(gr)

    small_names = [n for n, _ in SMALL_CUT] + list(SMALL_REP)
    sv = _pack([g[n] for n in small_names] + [loss_part[0:1, 0:1]], 8, 128, F32)
    s_send, s_recv, sv, s_land, s_token = _small_start(sv, comm.token, name="small_start")

    grads, delta, new_m, new_v = {}, {}, {}, {}

    def update(wn, dep):
        gl = [comm.reduced[name] for name, n2, _ in MATS if n2 == wn]
        shp = wl[wn].shape
        three = (len(gl),) + gl[0].shape
        flip = shp[-1] % 128 != 0
        view = (lambda t: t.reshape(three).transpose(0, 2, 1)) if flip else (lambda t: t.reshape(three))
        back = (lambda t: t.transpose(0, 2, 1).reshape(shp)) if flip else (lambda t: t.reshape(shp))
        if flip:
            gl = [t.T for t in gl]
        d, mn, vn, go = _adamw(view(wl[wn]), gl, view(ml[wn]), view(vl[wn]), name="adamw_" + wn, dep=dep)
        grads[wn], delta[wn], new_m[wn], new_v[wn] = back(go), back(d), back(mn), back(vn)
        return d

    done = [update(wn, s_token) for wn in ("f_w_in", "f_w_down", "w_q", "w_o", "w_kv", "a_out_proj")]
    comm.finish(done)
    update("a_in_proj", None)

    sv, s_land = _small_wait(s_send, s_recv, sv, s_land, done, name="small_wait")
    sred = _small_sum(sv, s_land, jnp.reshape(2 * me + ci, (1,)).astype(jnp.int32)).reshape(-1)
    small_shapes = [g[n].shape for n in small_names] + [(1,)]
    sg = dict(zip(small_names + ["loss"], _unpack(sred, small_shapes)))
    loss = sg["loss"].reshape(())
    g_small = {}
    for n, ax in SMALL_CUT:
        size = wl[n].shape[ax]
        g_small[n] = lax.dynamic_slice_in_dim(sg[n], me * size, size, axis=ax)
    for n in SMALL_REP:
        g_small[n] = sg[n].reshape(wl[n].shape)

    pk = lambda d: _pack([d[n] for n in small_names], 8, 128, F32)[None]
    d, mn, vn, _ = _adamw(pk(wl), [pk(g_small)[0]], pk(ml), pk(vl), name="adamw_small")
    shapes = [wl[n].shape for n in small_names]
    for n, dd, mm, vv in zip(small_names, _unpack(d.reshape(-1), shapes), _unpack(mn.reshape(-1), shapes),
                             _unpack(vn.reshape(-1), shapes)):
        grads[n], delta[n], new_m[n], new_v[n] = g_small[n], dd, mm, vv

    return (loss, dx0[None], *[grads[n] for n in WEIGHTS], *[delta[n] for n in WEIGHTS],
            *[new_m[n] for n in WEIGHTS], *[new_v[n] for n in WEIGHTS])
```

```python
import math

import jax
import jax.numpy as jnp
from jax import lax
from jax.experimental import pallas as pl
from jax.experimental.pallas import tpu as pltpu

F32 = jnp.float32
BF16 = jnp.bfloat16

EPS = 1e-5
CHUNK = 256
WINDOW = 128
HEAD = 64
SSM_HEADS = 32
SSM_GROUPS = 8
SSM_STATE = 128
ATT_KV = 4
ATT_G = 4
ROPE_THETA = 10000.0
NEG = -1e30
N_CHIPS = 4
VMEM_LIMIT = 56 * 1024 * 1024

ADAM_LR, ADAM_B1, ADAM_B2, ADAM_EPS, ADAM_WD, ADAM_STEP = 0.001, 0.9, 0.999, 1e-08, 0.01, 10


def _cp(n_axes):
    return pltpu.CompilerParams(dimension_semantics=("arbitrary",) * n_axes, vmem_limit_bytes=VMEM_LIMIT)


def _pick(dim, prefs):
    for p in prefs:
        if dim % p == 0:
            return p
    return dim


def _iota(shape, dim):
    return lax.broadcasted_iota(jnp.int32, shape, dim)


def _dot(a, b, ca=1, cb=0):
    return lax.dot_general(a, b, (((ca,), (cb,)), ((), ())), preferred_element_type=F32)


def _dot3(x, ind):
    h = x.astype(BF16)
    r = x - h.astype(F32)
    m = r.astype(BF16)
    lo = (r - m.astype(F32)).astype(BF16)
    return _dot(h, ind) + _dot(m, ind) + _dot(lo, ind)


def _sigmoid(x):
    return jax.nn.sigmoid(x)


def _mm(a, b, *, name, ta=False, tb=False, bias=None, res=None, out_dtype=F32, b_koff=0, tm=None, tn=None, tk=None,
        dims=None, a_spec=None, b_spec=None, o_spec=None, o_shape=None, dep=None, more=()):
    if dims is not None:
        M, N, K = dims
    else:
        if ta:
            K, M = a.shape
        else:
            M, K = a.shape
        N = b.shape[0] if tb else b.shape[1]
    tm = tm or _pick(M, (1024, 1408, 512, 256, 128))
    tn = tn or _pick(N, (512, 1408, 256, 128))
    tk = tk or (K if K <= 2048 else _pick(K, (2048, 1408, 1024, 512)))
    assert M % tm == 0 and N % tn == 0 and K % tk == 0 and b_koff % tk == 0
    nk = K // tk
    kb0 = b_koff // tk
    has_bias, has_res = bias is not None, res is not None

    def body(*refs):
        a_ref, b_ref = refs[0], refs[1]
        pos = 2
        bias_ref = res_ref = acc_ref = None
        if has_bias:
            bias_ref = refs[pos]
            pos += 1
        if has_res:
            res_ref = refs[pos]
            pos += 1
        if dep is not None:
            pos += 1
        extra = refs[pos:pos + 2 * len(more)]
        pos += 2 * len(more)
        o_ref = refs[pos]
        if nk > 1:
            acc_ref = refs[pos + 1]
        part = _dot(a_ref[...].astype(BF16), b_ref[...].astype(BF16), 0 if ta else 1, 1 if tb else 0)
        for q in range(len(more)):
            part = part + _dot(extra[2 * q][...].astype(BF16), extra[2 * q + 1][...].astype(BF16),
                               0 if ta else 1, 1 if tb else 0)

        def finish(acc):
            if has_bias:
                acc = acc + bias_ref[...]
            if has_res:
                acc = acc + res_ref[...]
            o_ref[...] = acc.astype(out_dtype)

        if nk == 1:
            finish(part)
        else:
            k = pl.program_id(2)

            @pl.when(k == 0)
            def _():
                acc_ref[...] = part

            @pl.when(k > 0)
            def _():
                acc_ref[...] += part

            @pl.when(k == nk - 1)
            def _():
                finish(acc_ref[...])

    if a_spec is None:
        a_spec = pl.BlockSpec((tk, tm), lambda i, j, k: (k, i)) if ta else pl.BlockSpec((tm, tk), lambda i, j, k: (i, k))
    if b_spec is None:
        b_spec = (pl.BlockSpec((tn, tk), lambda i, j, k: (j, k + kb0)) if tb
                  else pl.BlockSpec((tk, tn), lambda i, j, k: (k + kb0, j)))
    if o_spec is None:
        o_spec = pl.BlockSpec((tm, tn), lambda i, j, k: (i, j))
    in_specs, args = [a_spec, b_spec], [a, b]
    if has_bias:
        in_specs.append(pl.BlockSpec((1, tn), lambda i, j, k: (0, j)))
        args.append(bias)
    if has_res:
        in_specs.append(pl.BlockSpec((tm, tn), lambda i, j, k: (i, j)))
        args.append(res)
    if dep is not None:
        in_specs.append(pl.BlockSpec(memory_space=pl.ANY))
        args.append(dep)
    for piece in more:
        a2, sa, b2, sb = piece if len(piece) == 4 else (a, piece[0], b, piece[1])
        in_specs += [sa, sb]
        args += [a2, b2]
    return pl.pallas_call(
        body, name=name, grid=(M // tm, N // tn, nk), in_specs=in_specs, out_specs=o_spec,
        out_shape=jax.ShapeDtypeStruct(o_shape or (M, N), out_dtype),
        scratch_shapes=[pltpu.VMEM((tm, tn), F32)] if nk > 1 else [],
        compiler_params=_cp(3),
    )(*args)


def _rms_fwd(x, gains, *, name, tr=256, dep=None):
    S, D = x.shape
    n = len(gains)
    nd = 0 if dep is None else 1

    def body(*refs):
        xv = refs[0][...]
        xh = xv * lax.rsqrt(jnp.mean(xv * xv, axis=-1, keepdims=True) + EPS)
        for q in range(n):
            refs[1 + n + nd + q][...] = (xh * refs[1 + q][...]).astype(BF16)

    row = pl.BlockSpec((tr, D), lambda i: (i, 0))
    vec = pl.BlockSpec((1, D), lambda i: (0, 0))
    return pl.pallas_call(
        body, name=name, grid=(S // tr,), in_specs=[row] + [vec] * n + [pl.BlockSpec(memory_space=pl.ANY)] * nd,
        out_specs=[row] * n, out_shape=[jax.ShapeDtypeStruct((S, D), BF16)] * n, compiler_params=_cp(1),
    )(x, *gains, *([] if dep is None else [dep]))


def _rms_bwd(x, gains, dhs, dres, *, name, tr=256, want_colsum=False):
    S, D = x.shape
    n = len(gains)
    steps = S // tr

    def body(*refs):
        x_ref = refs[0]
        g_refs = refs[1:1 + n]
        dh_refs = refs[1 + n:1 + 2 * n]
        dres_ref = refs[1 + 2 * n]
        dx_ref = refs[2 + 2 * n]
        dg_refs = refs[3 + 2 * n:3 + 3 * n]
        cs_ref = refs[3 + 3 * n] if want_colsum else None
        i = pl.program_id(0)
        xv = x_ref[...]
        r = lax.rsqrt(jnp.mean(xv * xv, axis=-1, keepdims=True) + EPS)
        xh = xv * r
        dx = dres_ref[...]
        for q in range(n):
            dh = dh_refs[q][...]
            dxh = dh * g_refs[q][...]
            dx = dx + r * (dxh - xh * jnp.mean(dxh * xh, axis=-1, keepdims=True))
            part = jnp.sum(dh * xh, axis=0, keepdims=True)

            @pl.when(i == 0)
            def _():
                dg_refs[q][...] = part

            @pl.when(i > 0)
            def _():
                dg_refs[q][...] += part

        dx_ref[...] = dx
        if want_colsum:
            cpart = jnp.sum(dx, axis=0, keepdims=True)

            @pl.when(i == 0)
            def _():
                cs_ref[...] = cpart

            @pl.when(i > 0)
            def _():
                cs_ref[...] += cpart

    row = pl.BlockSpec((tr, D), lambda i: (i, 0))
    vec = pl.BlockSpec((1, D), lambda i: (0, 0))
    n_vec_out = n + (1 if want_colsum else 0)
    outs = pl.pallas_call(
        body, name=name, grid=(steps,), in_specs=[row] + [vec] * n + [row] * n + [row],
        out_specs=[row] + [vec] * n_vec_out,
        out_shape=[jax.ShapeDtypeStruct((S, D), F32)] + [jax.ShapeDtypeStruct((1, D), F32)] * n_vec_out,
        compiler_params=_cp(1),
    )(x, *gains, *dhs, dres)
    return outs


def _colsum(x, *, name, tr=256):
    S, D = x.shape

    def body(x_ref, o_ref):
        i = pl.program_id(0)
        part = jnp.sum(x_ref[...].astype(F32), axis=0, keepdims=True)

        @pl.when(i == 0)
        def _():
            o_ref[...] = part

        @pl.when(i > 0)
        def _():
            o_ref[...] += part

    return pl.pallas_call(
        body, name=name, grid=(S // tr,), in_specs=[pl.BlockSpec((tr, D), lambda i: (i, 0))],
        out_specs=pl.BlockSpec((1, D), lambda i: (0, 0)), out_shape=jax.ShapeDtypeStruct((1, D), F32),
        compiler_params=_cp(1),
    )(x)


def _loss(y, t, *, name="loss", tr=256):
    S, D = y.shape
    steps = S // tr

    def body(y_ref, t_ref, dy_ref, l_ref, acc_ref):
        i = pl.program_id(0)
        e = y_ref[...] - t_ref[...]
        dy_ref[...] = e * (1.0 / D)
        part = jnp.sum(e * e, axis=0, keepdims=True)

        @pl.when(i == 0)
        def _():
            acc_ref[...] = part

        @pl.when(i > 0)
        def _():
            acc_ref[...] += part

        @pl.when(i == steps - 1)
        def _():
            tot = jnp.sum(acc_ref[...], axis=1, keepdims=True) * (0.5 / D)
            l_ref[...] = jnp.broadcast_to(tot, (8, 128))

    row = pl.BlockSpec((tr, D), lambda i: (i, 0))
    return pl.pallas_call(
        body, name=name, grid=(steps,), in_specs=[row, row],
        out_specs=[row, pl.BlockSpec((8, 128), lambda i: (0, 0))],
        out_shape=[jax.ShapeDtypeStruct((S, D), F32), jax.ShapeDtypeStruct((8, 128), F32)],
        scratch_shapes=[pltpu.VMEM((1, D), F32)], compiler_params=_cp(1),
    )(y, t)


STRIP = 64
HALO = 8


def _strips(S, tc):
    return [(r0, slice(l0, l0 + 128)) for l0 in range(0, tc, 128) for r0 in range(S - STRIP, -1, -STRIP)]


def _with_halo(ref, r0, ls):
    if r0 == 0:
        return jnp.concatenate([jnp.zeros((HALO, 128), F32), ref[0:STRIP, ls]], axis=0)
    return ref[r0 - HALO:r0 + STRIP, ls]


def _conv_strip(xw, w_ref, b_ref, ls, width):
    acc = b_ref[:, ls] + w_ref[pl.ds(width - 1, 1), ls] * xw[HALO:]
    shifted = []
    for s in range(1, width):
        xs = pltpu.roll(xw, s, axis=0)[HALO:]
        shifted.append(xs)
        acc = acc + w_ref[pl.ds(width - 1 - s, 1), ls] * xs
    return acc, shifted


def _conv_strip_back(dacc, after, xc, shifted, w_ref, ls, width):
    ext = jnp.concatenate([dacc, after], axis=0)
    dx = w_ref[pl.ds(width - 1, 1), ls] * dacc
    dws = [None] * width
    dws[width - 1] = jnp.sum(dacc * xc, axis=0, keepdims=True)
    for s in range(1, width):
        dx = dx + w_ref[pl.ds(width - 1 - s, 1), ls] * pltpu.roll(ext, STRIP + HALO - s, axis=0)[:STRIP]
        dws[width - 1 - s] = jnp.sum(dacc * shifted[s - 1], axis=0, keepdims=True)
    return dx, dws, jnp.sum(dacc, axis=0, keepdims=True)


def _conv_back_block(S, tc, width, w_ref, b_ref, x_ref, dacc_of, dx_store, dw_ref, db_ref):
    for l0 in range(0, tc, 128):
        ls = slice(l0, l0 + 128)
        after = jnp.zeros((HALO, 128), F32)
        tot = None
        for r0 in range(S - STRIP, -1, -STRIP):
            xw = _with_halo(x_ref, r0, ls)
            acc, shifted = _conv_strip(xw, w_ref, b_ref, ls, width)
            dacc = dacc_of(r0, ls, acc, _sigmoid(acc))
            dx, dws, db = _conv_strip_back(dacc, after, xw[HALO:], shifted, w_ref, ls, width)
            dx_store(r0, ls, dx)
            after = dacc[:HALO]
            part = dws + [db]
            tot = part if tot is None else [p + q for p, q in zip(tot, part)]
        for k in range(width):
            dw_ref[pl.ds(k, 1), ls] = tot[k]
        db_ref[:, ls] = tot[width]


def _conv_silu_fwd(xin, col0, C, w, b, *, name, tc=512):
    S = xin.shape[0]
    width = w.shape[0]
    off = col0 // tc

    def body(x_ref, w_ref, b_ref, o_ref):
        for r0, ls in _strips(S, tc):
            acc, _ = _conv_strip(_with_halo(x_ref, r0, ls), w_ref, b_ref, ls, width)
            o_ref[r0:r0 + STRIP, ls] = acc * _sigmoid(acc)

    return pl.pallas_call(
        body, name=name, grid=(C // tc,),
        in_specs=[pl.BlockSpec((S, tc), lambda j: (0, j + off)), pl.BlockSpec((width, tc), lambda j: (0, j)),
                  pl.BlockSpec((1, tc), lambda j: (0, j))],
        out_specs=pl.BlockSpec((S, tc), lambda j: (0, j)), out_shape=jax.ShapeDtypeStruct((S, C), F32),
        compiler_params=_cp(1),
    )(xin, w, b)


def _conv_silu_bwd(xin, col0, C, w, b, douts, *, name, tc=256):
    S = xin.shape[0]
    width = w.shape[0]
    off = col0 // tc
    nd = len(douts)
    ranges = [(o // tc, (o + d.shape[1]) // tc) for d, o in douts]

    def body(*refs):
        x_ref, w_ref, b_ref = refs[0], refs[1], refs[2]
        d_refs = refs[3:3 + nd]
        dx_ref, dw_ref, db_ref = refs[3 + nd], refs[4 + nd], refs[5 + nd]
        j = pl.program_id(0)

        def dacc_of(r0, ls, acc, sg):
            dout = jnp.zeros((STRIP, 128), F32)
            for q in range(nd):
                lo, hi = ranges[q]
                dout = dout + jnp.where((j >= lo) & (j < hi), d_refs[q][r0:r0 + STRIP, ls], 0.0)
            return dout * (sg * (1.0 + acc * (1.0 - sg)))

        def dx_store(r0, ls, dx):
            dx_ref[r0:r0 + STRIP, ls] = dx.astype(BF16)

        _conv_back_block(S, tc, width, w_ref, b_ref, x_ref, dacc_of, dx_store, dw_ref, db_ref)

    d_specs = [pl.BlockSpec((S, tc), (lambda j, lo=lo, hi=hi: (0, jnp.clip(j - lo, 0, hi - lo - 1)))) for lo, hi in ranges]
    return pl.pallas_call(
        body, name=name, grid=(C // tc,),
        in_specs=[pl.BlockSpec((S, tc), lambda j: (0, j + off)), pl.BlockSpec((width, tc), lambda j: (0, j)),
                  pl.BlockSpec((1, tc), lambda j: (0, j))] + d_specs,
        out_specs=[pl.BlockSpec((S, tc), lambda j: (0, j)), pl.BlockSpec((width, tc), lambda j: (0, j)),
                   pl.BlockSpec((1, tc), lambda j: (0, j))],
        out_shape=[jax.ShapeDtypeStruct((S, C), BF16), jax.ShapeDtypeStruct((width, C), F32),
                   jax.ShapeDtypeStruct((1, C), F32)],
        compiler_params=_cp(1),
    )(xin, w, b, *[d for d, _ in douts])


def _ffn_act_fwd(u, w, b, *, name, tc=256):
    S, F2 = u.shape
    Fd = F2 // 2
    width = w.shape[0]
    nb = Fd // tc

    def body(g_ref, v_ref, w_ref, b_ref, o_ref):
        for r0, ls in _strips(S, tc):
            acc, _ = _conv_strip(_with_halo(g_ref, r0, ls), w_ref, b_ref, ls, width)
            o_ref[r0:r0 + STRIP, ls] = (acc * _sigmoid(acc) * v_ref[r0:r0 + STRIP, ls]).astype(BF16)

    return pl.pallas_call(
        body, name=name, grid=(nb,),
        in_specs=[pl.BlockSpec((S, tc), lambda j: (0, j)), pl.BlockSpec((S, tc), lambda j: (0, j + nb)),
                  pl.BlockSpec((width, tc), lambda j: (0, j)), pl.BlockSpec((1, tc), lambda j: (0, j))],
        out_specs=pl.BlockSpec((S, tc), lambda j: (0, j)), out_shape=jax.ShapeDtypeStruct((S, Fd), BF16),
        compiler_params=_cp(1),
    )(u, u, w, b)


def _ffn_act_bwd(u, w, b, da, *, name, tc=256):
    S, F2 = u.shape
    Fd = F2 // 2
    width = w.shape[0]
    nb = Fd // tc

    def body(g_ref, v_ref, w_ref, b_ref, da_ref, du_ref, dw_ref, db_ref, a_ref):
        def dacc_of(r0, ls, acc, sg):
            rs = slice(r0, r0 + STRIP)
            dav, val, silu = da_ref[rs, ls], v_ref[rs, ls], acc * sg
            a_ref[rs, ls] = (silu * val).astype(BF16)
            du_ref[1, rs, ls] = (dav * silu).astype(BF16)
            return dav * val * (sg * (1.0 + acc * (1.0 - sg)))

        def dx_store(r0, ls, dx):
            du_ref[0, r0:r0 + STRIP, ls] = dx.astype(BF16)

        _conv_back_block(S, tc, width, w_ref, b_ref, g_ref, dacc_of, dx_store, dw_ref, db_ref)

    blk = pl.BlockSpec((S, tc), lambda j: (0, j))
    return pl.pallas_call(
        body, name=name, grid=(nb,),
        in_specs=[blk, pl.BlockSpec((S, tc), lambda j: (0, j + nb)), pl.BlockSpec((width, tc), lambda j: (0, j)),
                  pl.BlockSpec((1, tc), lambda j: (0, j)), blk],
        out_specs=[pl.BlockSpec((2, S, tc), lambda j: (0, 0, j)), pl.BlockSpec((width, tc), lambda j: (0, j)),
                   pl.BlockSpec((1, tc), lambda j: (0, j)), blk],
        out_shape=[jax.ShapeDtypeStruct((2, S, Fd), BF16),
                   jax.ShapeDtypeStruct((width, Fd), F32), jax.ShapeDtypeStruct((1, Fd), F32),
                   jax.ShapeDtypeStruct((S, Fd), BF16)],
        compiler_params=_cp(1),
    )(u, u, w, b, da)


def _ssd_prep(dtr, dt_bias, a_log, *, name="ssd_prep"):
    S = dtr.shape[0]

    def body(d_ref, b_ref, al_ref, dt_ref, ac_ref, sg_ref, act_ref):
        lane = _iota((CHUNK, 128), 1)
        valid = lane < SSM_HEADS
        z = d_ref[...] + b_ref[...]
        dt = jnp.where(valid, jnp.maximum(z, 0.0) + jnp.log(1.0 + jnp.exp(-jnp.abs(z))), 0.0)
        a = dt * (-jnp.exp(al_ref[...]))
        row = _iota((CHUNK, 128), 0)
        k = 1
        while k < CHUNK:
            a = a + jnp.where(row >= k, pltpu.roll(a, k, axis=0), 0.0)
            k *= 2
        sg = jnp.where(valid, _sigmoid(z), 0.0)
        for arr, ref in ((dt, dt_ref), (a, ac_ref), (sg, sg_ref)):
            for g in range(SSM_GROUPS):
                ref[g] = jnp.where(lane < 4, arr if g == 0 else pltpu.roll(arr, 128 - 4 * g, axis=1), 0.0)
        act_ref[...] = a.T[:SSM_HEADS, :]

    blk = pl.BlockSpec((CHUNK, 128), lambda i: (i, 0))
    vec = pl.BlockSpec((1, 128), lambda i: (0, 0))
    grp = pl.BlockSpec((SSM_GROUPS, CHUNK, 128), lambda i: (0, i, 0))
    return pl.pallas_call(
        body, name=name, grid=(S // CHUNK,), in_specs=[blk, vec, vec],
        out_specs=[grp, grp, grp, pl.BlockSpec((SSM_HEADS, CHUNK), lambda i: (0, i))],
        out_shape=[jax.ShapeDtypeStruct((SSM_GROUPS, S, 128), F32)] * 3 + [jax.ShapeDtypeStruct((SSM_HEADS, S), F32)],
        compiler_params=_cp(1),
    )(dtr, dt_bias, a_log)


SSD_GPS = 4


def _expand4(v, lanes):
    out = jnp.broadcast_to(v[:, 3:4], lanes.shape)
    for hh in (2, 1, 0):
        out = jnp.where(lanes < 64 * (hh + 1), v[:, hh:hh + 1], out)
    return out


def _ssd_fwd(xbc, dt_g, ac_g, ac_t, *, name="ssd_fwd"):
    S = xbc.shape[0]
    nc = S // CHUNK
    Lc = CHUNK

    def body(x_ref, b_ref, c_ref, dt_ref, ac_ref, act_ref, y_ref, st_out_ref, st_ref):
        g2 = pl.program_id(0)
        c = pl.program_id(1)

        @pl.when(c == 0)
        def _():
            st_ref[...] = jnp.zeros_like(st_ref)

        causal = _iota((Lc, Lc), 0) >= _iota((Lc, Lc), 1)
        lane256 = _iota((Lc, 256), 1)
        lane128 = _iota((Lc, 128), 1)
        row128 = _iota((128, 128), 0)
        for gg in range(SSD_GPS):
            g = SSD_GPS * g2 + gg
            bv = b_ref[:, 128 * gg:128 * (gg + 1)]
            cbf = c_ref[:, 128 * gg:128 * (gg + 1)].astype(BF16)
            cb = _dot(cbf, bv.astype(BF16), 1, 1)
            dtg, acg = dt_ref[gg], ac_ref[gg]
            ac_last = ac_ref[gg, pl.ds(Lc - 1, 1), :]
            dt4 = _expand4(dtg, lane256)
            ac4 = _expand4(acg, lane256)
            e4 = jnp.exp(ac4)
            xdb = (x_ref[:, 256 * gg:256 * (gg + 1)] * dt4).astype(BF16)
            st_out_ref[gg] = st_ref[gg]
            for p in range(2):
                xd_p = xdb[:, 128 * p:128 * (p + 1)]
                st_p = st_ref[gg, p]
                ys, sn, cds = [], [], []
                for q in range(2):
                    hh = 2 * p + q
                    a_col = acg[:, hh:hh + 1]
                    a_row = act_ref[pl.ds(4 * g + hh, 1), :]
                    dec = jnp.exp(jnp.where(causal, a_col - a_row, NEG))
                    w = (cb * dec).astype(BF16)
                    ys.append(_dot(w, xd_p))
                    al = ac_last[:, hh:hh + 1]
                    dte = jnp.exp(al - a_col)
                    sn.append(_dot(xd_p, (bv * dte).astype(BF16), 0, 0))
                    cds.append(jnp.exp(al))
                y_diag = jnp.where(lane128 < 64, ys[0], ys[1])
                y_off = _dot(cbf, st_p.astype(BF16), 1, 1) * e4[:, 128 * p:128 * (p + 1)]
                y_ref[:, 256 * gg + 128 * p:256 * gg + 128 * (p + 1)] = y_diag + y_off
                st_ref[gg, p] = jnp.where(row128 < 64, st_p * cds[0] + sn[0], st_p * cds[1] + sn[1])

    G = SSD_GPS
    per_g = lambda g, c: (g, c, 0)
    return pl.pallas_call(
        body, name=name, grid=(SSM_GROUPS // G, nc),
        in_specs=[pl.BlockSpec((Lc, 256 * G), lambda g, c: (c, g)),
                  pl.BlockSpec((Lc, 128 * G), lambda g, c: (c, 16 // G + g)),
                  pl.BlockSpec((Lc, 128 * G), lambda g, c: (c, 24 // G + g)),
                  pl.BlockSpec((G, Lc, 128), per_g), pl.BlockSpec((G, Lc, 128), per_g),
                  pl.BlockSpec((SSM_HEADS, Lc), lambda g, c: (0, c))],
        out_specs=[pl.BlockSpec((Lc, 256 * G), lambda g, c: (c, g)),
                   pl.BlockSpec((G, None, 2, 128, 128), lambda g, c: (g, c, 0, 0, 0))],
        out_shape=[jax.ShapeDtypeStruct((S, 2048), F32), jax.ShapeDtypeStruct((SSM_GROUPS, nc, 2, 128, 128), F32)],
        scratch_shapes=[pltpu.VMEM((G, 2, 128, 128), F32)], compiler_params=_cp(2),
    )(xbc, xbc, xbc, dt_g, ac_g, ac_t)


def _ssd_bwd(xbc, dt_g, ac_g, ac_t, states, dy, dexp, *, name="ssd_bwd", dep=None):
    S = xbc.shape[0]
    nc = S // CHUNK
    Lc = CHUNK

    def body(x_ref, b_ref, c_ref, dt_ref, ac_ref, act_ref, st_ref, dy_ref, d_ref, *rest):
        dx_ref, db_ref, dc_ref, dh_ref, ds_ref = rest[-5:]
        g2 = pl.program_id(0)
        cc = pl.program_id(1)

        @pl.when(cc == 0)
        def _():
            ds_ref[...] = jnp.zeros_like(ds_ref)

        causal = _iota((Lc, Lc), 0) >= _iota((Lc, Lc), 1)
        lane256 = _iota((Lc, 256), 1)
        lane128 = _iota((Lc, 128), 1)
        row128 = _iota((128, 128), 0)
        ind_rows = _iota((256, 128), 0) >> 6
        ind_cols = _iota((256, 128), 1)
        ind_a = (ind_rows == ind_cols).astype(BF16)
        ind_b = (ind_rows + 4 == ind_cols).astype(BF16)
        for gg in range(SSD_GPS):
            g = SSD_GPS * g2 + gg
            bv = b_ref[:, 128 * gg:128 * (gg + 1)]
            cv = c_ref[:, 128 * gg:128 * (gg + 1)]
            bbf, cbf = bv.astype(BF16), cv.astype(BF16)
            cb = _dot(cbf, bbf, 1, 1)
            dtg, acg = dt_ref[gg], ac_ref[gg]
            ac_last = ac_ref[gg, pl.ds(Lc - 1, 1), :]
            dt4 = _expand4(dtg, lane256)
            ac4 = _expand4(acg, lane256)
            acl4 = _expand4(ac_last, _iota((1, 256), 1))
            e4 = jnp.exp(ac4)
            dte4 = jnp.exp(acl4 - ac4)
            xv = x_ref[:, 256 * gg:256 * (gg + 1)]
            xd = xv * dt4
            xdb = xd.astype(BF16)
            dyv = dy_ref[:, 256 * gg:256 * (gg + 1)]
            dcb = jnp.zeros((Lc, Lc), F32)
            dc_acc = jnp.zeros((Lc, 128), F32)
            db_acc = jnp.zeros((Lc, 128), F32)
            u_parts, dxd_parts, ends = [], [], []
            for p in range(2):
                sl = slice(128 * p, 128 * (p + 1))
                xd_p, xdb_p, dy_p = xd[:, sl], xdb[:, sl], dyv[:, sl]
                dyb_p = dy_p.astype(BF16)
                e_p, dte_p = e4[:, sl], dte4[:, sl]
                sp = st_ref[gg, p]
                spb = sp.astype(BF16)
                dsn = ds_ref[gg, p]
                dsnb = dsn.astype(BF16)
                yds, dxds, cds = [], [], []
                for q in range(2):
                    hh = 2 * p + q
                    a_col = acg[:, hh:hh + 1]
                    a_row = act_ref[pl.ds(4 * g + hh, 1), :]
                    dec = jnp.exp(jnp.where(causal, a_col - a_row, NEG))
                    w = (cb * dec).astype(BF16)
                    head = (lane128 < 64) if q == 0 else (lane128 >= 64)
                    dym = jnp.where(head, dyb_p, jnp.zeros_like(dyb_p))
                    dw = _dot(dym, xdb_p, 1, 1)
                    dcb = dcb + dw * dec
                    yds.append(_dot(w, xdb_p))
                    dxds.append(_dot(w, dyb_p, 0, 0))
                    cds.append(jnp.exp(ac_last[:, hh:hh + 1]))
                y_diag = jnp.where(lane128 < 64, yds[0], yds[1])
                dxd_diag = jnp.where(lane128 < 64, dxds[0], dxds[1])
                y_off = _dot(cbf, spb, 1, 1) * e_p
                dgp = dy_p * e_p
                dgb = dgp.astype(BF16)
                dc_acc = dc_acc + _dot(dgb, spb)
                dsp = _dot(dgb, cbf, 0, 0)
                cd_col = jnp.where(row128[:, 0:1] < 64, cds[0], cds[1])
                qm = _dot(bbf, dsnb, 1, 1)
                dxd_state = dte_p * qm
                db_acc = db_acc + _dot((xd_p * dte_p).astype(BF16), dsnb)
                t_p = xd_p * dxd_state
                prod = dsn * sp
                e0 = jnp.sum(jnp.sum(jnp.where(row128 < 64, prod, 0.0), axis=1, keepdims=True), axis=0, keepdims=True)
                e1 = jnp.sum(jnp.sum(jnp.where(row128 >= 64, prod, 0.0), axis=1, keepdims=True), axis=0, keepdims=True)
                tcol = jnp.sum(t_p, axis=0, keepdims=True)
                lane1 = _iota((1, 128), 1)
                t0 = jnp.sum(jnp.where(lane1 < 64, tcol, 0.0), axis=1, keepdims=True)
                t1 = jnp.sum(jnp.where(lane1 >= 64, tcol, 0.0), axis=1, keepdims=True)
                ends.append(e0 * cds[0] + t0)
                ends.append(e1 * cds[1] + t1)
                ds_ref[gg, p] = dsn * cd_col + dsp
                u_parts.append(dyb_p.astype(F32) * y_diag - xdb_p.astype(F32) * dxd_diag + dy_p * y_off - t_p)
                dxd_parts.append(dxd_diag + dxd_state)
            dxd = jnp.concatenate(dxd_parts, axis=1)
            u_all = jnp.concatenate(u_parts, axis=1)
            dx_ref[:, 256 * gg:256 * (gg + 1)] = dxd * dt4 + dyv * d_ref[:, 256 * gg:256 * (gg + 1)]
            dcbb = dcb.astype(BF16)
            dc_ref[:, 128 * gg:128 * (gg + 1)] = dc_acc + _dot(dcbb, bbf)
            db_ref[:, 128 * gg:128 * (gg + 1)] = db_acc + _dot(dcbb, cbf, 0, 0)
            lane = _iota((Lc, 128), 1)
            endv = jnp.zeros((Lc, 128), F32)
            for hh in range(4):
                endv = jnp.where(lane == 8 + hh, ends[hh], endv)
            dh_ref[gg] = _dot3(dxd * xv, ind_a) + _dot3(u_all, ind_b) + endv

    G = SSD_GPS
    rev = lambda c: nc - 1 - c
    per_g = lambda g, c: (g, rev(c), 0)
    return pl.pallas_call(
        body, name=name, grid=(SSM_GROUPS // G, nc),
        in_specs=[pl.BlockSpec((Lc, 256 * G), lambda g, c: (rev(c), g)),
                  pl.BlockSpec((Lc, 128 * G), lambda g, c: (rev(c), 16 // G + g)),
                  pl.BlockSpec((Lc, 128 * G), lambda g, c: (rev(c), 24 // G + g)),
                  pl.BlockSpec((G, Lc, 128), per_g), pl.BlockSpec((G, Lc, 128), per_g),
                  pl.BlockSpec((SSM_HEADS, Lc), lambda g, c: (0, rev(c))),
                  pl.BlockSpec((G, None, 2, 128, 128), lambda g, c: (g, rev(c), 0, 0, 0)),
                  pl.BlockSpec((Lc, 256 * G), lambda g, c: (rev(c), g)),
                  pl.BlockSpec((1, 256 * G), lambda g, c: (0, g))] + ([] if dep is None else [pl.BlockSpec(memory_space=pl.ANY)]),
        out_specs=[pl.BlockSpec((Lc, 256 * G), lambda g, c: (rev(c), g)),
                   pl.BlockSpec((Lc, 128 * G), lambda g, c: (rev(c), g)),
                   pl.BlockSpec((Lc, 128 * G), lambda g, c: (rev(c), g)),
                   pl.BlockSpec((G, Lc, 128), per_g)],
        out_shape=[jax.ShapeDtypeStruct((S, 2048), F32), jax.ShapeDtypeStruct((S, 1024), F32),
                   jax.ShapeDtypeStruct((S, 1024), F32), jax.ShapeDtypeStruct((SSM_GROUPS, S, 128), F32)],
        scratch_shapes=[pltpu.VMEM((G, 2, 128, 128), F32)], compiler_params=_cp(2),
    )(xbc, xbc, xbc, dt_g, ac_g, ac_t, states, dy, dexp, *([] if dep is None else [dep]))


def _ssd_post(dhead, dt_g, sg_g, alog_g, *, name="ssd_post"):
    S = dhead.shape[1]
    nc = S // CHUNK
    Lc = CHUNK

    def body(dh_ref, dt_ref, sg_ref, al_ref, o_ref, s_ref):
        @pl.when(pl.program_id(0) == 0)
        def _():
            s_ref[...] = jnp.zeros_like(s_ref)

        lane = _iota((Lc, 128), 1)
        row = _iota((Lc, 128), 0)
        row8 = _iota((8, 128), 0)
        out = jnp.zeros((Lc, 128), F32)
        for g in range(SSM_GROUPS):
            dh = dh_ref[g]
            a_neg = -jnp.exp(al_ref[g])
            dac = jnp.where(lane < 4, pltpu.roll(dh, 124, axis=1), 0.0)
            end = jnp.where(lane < 4, pltpu.roll(dh, 120, axis=1), 0.0)
            k = 1
            while k < Lc:
                dac = dac + jnp.where(row < Lc - k, pltpu.roll(dac, Lc - k, axis=0), 0.0)
                k *= 2
            da = dac + end
            ddt = jnp.where(lane < 4, da * a_neg + dh, 0.0)
            ddtr = ddt * sg_ref[g]
            out = out + (ddtr if g == 0 else pltpu.roll(ddtr, 4 * g, axis=1))
            dal = jnp.sum(da * dt_ref[g], axis=0, keepdims=True) * a_neg
            dbias = jnp.sum(ddtr, axis=0, keepdims=True)
            part = jnp.where(row8 == 0, dal, jnp.where(row8 == 1, dbias, 0.0))
            s_ref[g] += part
        o_ref[...] = out.astype(BF16)

    grp = pl.BlockSpec((SSM_GROUPS, Lc, 128), lambda c: (0, c, 0))
    whole = lambda r: pl.BlockSpec((SSM_GROUPS, r, 128), lambda c: (0, 0, 0))
    return pl.pallas_call(
        body, name=name, grid=(nc,), in_specs=[grp, grp, grp, whole(1)],
        out_specs=[pl.BlockSpec((Lc, 128), lambda c: (c, 0)), whole(8)],
        out_shape=[jax.ShapeDtypeStruct((S, 128), BF16), jax.ShapeDtypeStruct((SSM_GROUPS, 8, 128), F32)],
        compiler_params=_cp(1),
    )(dhead, dt_g, sg_g, alog_g)


def _gate_fwd(y, xbc, zx, dexp, gn, *, name="gate_fwd", tr=256):
    S = y.shape[0]
    W = 2048
    gw = W // SSM_GROUPS

    def body(y_ref, x_ref, z_ref, d_ref, g_ref, o_ref):
        z = z_ref[...]
        u = (y_ref[...] + x_ref[...] * d_ref[...]) * (z * _sigmoid(z))
        gv = g_ref[...]
        for q in range(SSM_GROUPS):
            sl = slice(gw * q, gw * (q + 1))
            uq = u[:, sl]
            r = lax.rsqrt(jnp.mean(uq * uq, axis=-1, keepdims=True) + EPS)
            o_ref[:, sl] = (uq * r * gv[:, sl]).astype(BF16)

    row = pl.BlockSpec((tr, W), lambda i: (i, 0))
    vec = pl.BlockSpec((1, W), lambda i: (0, 0))
    return pl.pallas_call(
        body, name=name, grid=(S // tr,), in_specs=[row, row, row, vec, vec], out_specs=row,
        out_shape=jax.ShapeDtypeStruct((S, W), BF16), compiler_params=_cp(1),
    )(y, xbc, zx, dexp, gn)


def _gate_bwd(y, xbc, zx, dexp, gn, dout, *, name="gate_bwd", tr=256):
    S = y.shape[0]
    W = 2048
    gw = W // SSM_GROUPS
    steps = S // tr

    def body(y_ref, x_ref, z_ref, d_ref, g_ref, do_ref, dy_ref, dz_ref, dg_ref, dd_ref, acc_ref):
        i = pl.program_id(0)

        @pl.when(i == 0)
        def _():
            acc_ref[...] = jnp.zeros_like(acc_ref)

        z = z_ref[...]
        sg = _sigmoid(z)
        sz = z * sg
        xs = x_ref[...]
        yt = y_ref[...] + xs * d_ref[...]
        u = yt * sz
        gv = g_ref[...]
        do = do_ref[...]
        dgs = []
        for q in range(SSM_GROUPS):
            sl = slice(gw * q, gw * (q + 1))
            uq = u[:, sl]
            r = lax.rsqrt(jnp.mean(uq * uq, axis=-1, keepdims=True) + EPS)
            uh = uq * r
            dq = do[:, sl]
            duh = dq * gv[:, sl]
            duq = r * (duh - uh * jnp.mean(duh * uh, axis=-1, keepdims=True))
            dgs.append(jnp.sum(dq * uh, axis=0, keepdims=True))
            dyt = duq * sz[:, sl]
            dy_ref[:, sl] = dyt
            dz_ref[:, sl] = (duq * yt[:, sl] * (sg[:, sl] * (1.0 + z[:, sl] * (1.0 - sg[:, sl])))).astype(BF16)
            acc_ref[:, sl] += jnp.sum(dyt * xs[:, sl], axis=0, keepdims=True)
        dg = jnp.concatenate(dgs, axis=1)

        @pl.when(i == 0)
        def _():
            dg_ref[...] = dg

        @pl.when(i > 0)
        def _():
            dg_ref[...] += dg

        @pl.when(i == steps - 1)
        def _():
            ind = ((_iota((W, 128), 0) >> 6) == _iota((W, 128), 1)).astype(BF16)
            dd_ref[...] = _dot3(jnp.broadcast_to(acc_ref[...], (8, W)), ind)[0:1, :]

    row = pl.BlockSpec((tr, W), lambda i: (i, 0))
    vec = pl.BlockSpec((1, W), lambda i: (0, 0))
    return pl.pallas_call(
        body, name=name, grid=(steps,), in_specs=[row, row, row, vec, vec, row],
        out_specs=[row, row, vec, pl.BlockSpec((1, 128), lambda i: (0, 0))],
        out_shape=[jax.ShapeDtypeStruct((S, W), F32), jax.ShapeDtypeStruct((S, W), BF16),
                   jax.ShapeDtypeStruct((1, W), F32), jax.ShapeDtypeStruct((1, 128), F32)],
        scratch_shapes=[pltpu.VMEM((1, W), F32)], compiler_params=_cp(1),
    )(y, xbc, zx, dexp, gn, dout)


def _rope_cs(posf, *, name="rope_tables", tr=256):
    S = posf.shape[0]

    def body(p_ref, c_ref, s_ref):
        j = (_iota((tr, 128), 1) & 31).astype(F32)
        ang = p_ref[...] * jnp.exp(j * (-math.log(ROPE_THETA) / 32.0))
        c_ref[...] = jnp.cos(ang)
        s_ref[...] = jnp.sin(ang)

    blk = pl.BlockSpec((tr, 128), lambda i: (i, 0))
    return pl.pallas_call(
        body, name=name, grid=(S // tr,), in_specs=[pl.BlockSpec((tr, 1), lambda i: (i, 0))], out_specs=[blk, blk],
        out_shape=[jax.ShapeDtypeStruct((S, 128), F32)] * 2, compiler_params=_cp(1),
    )(posf)


def _rope_tables(c_ref, s_ref, shape):
    reps = shape[1] // 128
    return jnp.tile(c_ref[...], (1, reps)), jnp.tile(s_ref[...], (1, reps)), (_iota(shape, 1) & 63) < 32


def _hn_inds(W):
    ind = ((_iota((W, 128), 0) >> 6) == _iota((W, 128), 1)).astype(BF16)
    ind_t = ((_iota((128, W), 1) >> 6) == _iota((128, W), 0)).astype(BF16)
    return ind, ind_t


def _hnrope_fwd(xin, col0, W, gain_w, rope, *, name, tr=256):
    S = xin.shape[0]
    off = col0 // W
    nh = W // HEAD

    def body(x_ref, g_ref, c_ref, s_ref, o_ref):
        x = x_ref[...]
        ind, ind_t = _hn_inds(W)
        r = lax.rsqrt(_dot3(x * x, ind) * (1.0 / HEAD) + EPS)
        xn = x * _dot3(r, ind_t) * g_ref[...]
        cs, sn, half = _rope_tables(c_ref, s_ref, (tr, W))
        rot = jnp.where(half, -pltpu.roll(xn, W - 32, axis=1), pltpu.roll(xn, 32, axis=1))
        out = (xn * cs + rot * sn).astype(BF16)
        for h in range(nh):
            o_ref[h] = out[:, HEAD * h:HEAD * (h + 1)]

    tab = pl.BlockSpec((tr, 128), lambda i: (i, 0))
    return pl.pallas_call(
        body, name=name, grid=(S // tr,),
        in_specs=[pl.BlockSpec((tr, W), lambda i: (i, off)), pl.BlockSpec((1, W), lambda i: (0, 0)), tab, tab],
        out_specs=pl.BlockSpec((nh, tr, HEAD), lambda i: (0, i, 0)), out_shape=jax.ShapeDtypeStruct((nh, S, HEAD), BF16),
        compiler_params=_cp(1),
    )(xin, gain_w, *rope)


def _hnrope_bwd(xin, col0, W, gain_w, rope, dout, *, name, tr=256):
    S = xin.shape[0]
    off = col0 // W
    steps = S // tr
    nh = W // HEAD

    def body(x_ref, g_ref, c_ref, s_ref, do_ref, dx_ref, cs_ref, dg_ref, acc_ref):
        i = pl.program_id(0)
        x = x_ref[...]
        ind, ind_t = _hn_inds(W)
        r = lax.rsqrt(_dot3(x * x, ind) * (1.0 / HEAD) + EPS)
        rw = _dot3(r, ind_t)
        xh = x * rw
        cs, sn, half = _rope_tables(c_ref, s_ref, (tr, W))
        do = jnp.concatenate([do_ref[h] for h in range(nh)], axis=1).astype(F32)
        gs = do * sn
        g1 = do * cs + jnp.where(half, pltpu.roll(gs, W - 32, axis=1), -pltpu.roll(gs, 32, axis=1))
        dxh = g1 * g_ref[...]
        t = _dot3(dxh * xh, ind) * (1.0 / HEAD)
        dx = rw * (dxh - xh * _dot3(t, ind_t))
        dx_ref[...] = dx.astype(BF16)
        cpart = jnp.sum(dx, axis=0, keepdims=True)
        gpart = jnp.sum(g1 * xh, axis=0, keepdims=True)

        @pl.when(i == 0)
        def _():
            cs_ref[...] = cpart
            acc_ref[...] = gpart

        @pl.when(i > 0)
        def _():
            cs_ref[...] += cpart
            acc_ref[...] += gpart

        @pl.when(i == steps - 1)
        def _():
            fold = ((_iota((W, 128), 0) & 63) == _iota((W, 128), 1)).astype(BF16)
            dg_ref[...] = _dot3(jnp.broadcast_to(acc_ref[...], (8, W)), fold)[0:1, :]

    tab = pl.BlockSpec((tr, 128), lambda i: (i, 0))
    return pl.pallas_call(
        body, name=name, grid=(steps,),
        in_specs=[pl.BlockSpec((tr, W), lambda i: (i, off)), pl.BlockSpec((1, W), lambda i: (0, 0)), tab, tab,
                  pl.BlockSpec((nh, tr, HEAD), lambda i: (0, i, 0))],
        out_specs=[pl.BlockSpec((tr, W), lambda i: (i, 0)), pl.BlockSpec((1, W), lambda i: (0, 0)),
                   pl.BlockSpec((1, 128), lambda i: (0, 0))],
        out_shape=[jax.ShapeDtypeStruct((S, W), BF16), jax.ShapeDtypeStruct((1, W), F32),
                   jax.ShapeDtypeStruct((1, 128), F32)],
        scratch_shapes=[pltpu.VMEM((1, W), F32)], compiler_params=_cp(1),
    )(xin, gain_w, *rope, dout)


def _attn_band():
    qi = jnp.arange(ATT_G * WINDOW)[:, None] % WINDOW
    ki = jnp.arange(2 * WINDOW)[None, :]
    rel = qi + WINDOW - ki
    ok = (rel >= 0) & (rel < WINDOW)
    return jnp.stack([jnp.where(ok & (ki >= WINDOW), 0.0, NEG), jnp.where(ok, 0.0, NEG)]).astype(F32)


def _attn_probs(q, kb, sink_ref, band_ref, h, i):
    s = _dot(q, kb, 1, 1) * (HEAD ** -0.5) + band_ref[jnp.minimum(i, 1)]
    r1 = _iota((4 * WINDOW, 1), 0)
    sink = jnp.where(r1 < WINDOW, sink_ref[4 * h], jnp.where(r1 < 2 * WINDOW, sink_ref[4 * h + 1],
                     jnp.where(r1 < 3 * WINDOW, sink_ref[4 * h + 2], sink_ref[4 * h + 3])))
    m = jnp.maximum(jnp.max(s, axis=1, keepdims=True), sink)
    p = jnp.exp(s - m)
    ps = jnp.exp(sink - m)
    inv = 1.0 / (jnp.sum(p, axis=1, keepdims=True) + ps)
    return p * inv, ps * inv


ATT_HPS = 4
_BAND = pl.BlockSpec((2, ATT_G * WINDOW, 2 * WINDOW), lambda h, i: (0, 0, 0))


def _attn_specs(S):
    qspec = pl.BlockSpec((ATT_HPS, ATT_G, WINDOW, HEAD), lambda h, i: (h, 0, i, 0))
    cur = pl.BlockSpec((ATT_HPS, WINDOW, HEAD), lambda h, i: (h, i, 0))
    prev = pl.BlockSpec((ATT_HPS, WINDOW, HEAD), lambda h, i: (h, jnp.maximum(i - 1, 0), 0))
    tok = pl.BlockSpec((WINDOW, ATT_HPS * ATT_G * HEAD), lambda h, i: (i, h))
    return qspec, cur, prev, tok


def _attn_fwd(qh, kh, vh, sinks, *, name="attn_fwd"):
    S = kh.shape[1]
    nb = S // WINDOW

    def body(s_ref, band_ref, q_ref, kc_ref, kp_ref, vc_ref, vp_ref, o_ref):
        h2, i = pl.program_id(0), pl.program_id(1)
        outs = []
        for hh in range(ATT_HPS):
            q = q_ref[hh].reshape(ATT_G * WINDOW, HEAD)
            kb = jnp.concatenate([kp_ref[hh], kc_ref[hh]], axis=0)
            vb = jnp.concatenate([vp_ref[hh], vc_ref[hh]], axis=0)
            probs, _ = _attn_probs(q, kb, s_ref, band_ref, ATT_HPS * h2 + hh, i)
            o = _dot(probs.astype(BF16), vb).astype(BF16)
            outs += [o[WINDOW * g:WINDOW * (g + 1)] for g in range(ATT_G)]
        o_ref[...] = jnp.concatenate(outs, axis=1)

    qspec, cur, prev, tok = _attn_specs(S)
    return pl.pallas_call(
        body, name=name, grid=(ATT_KV // ATT_HPS, nb),
        in_specs=[pl.BlockSpec(memory_space=pltpu.SMEM), _BAND, qspec, cur, prev, cur, prev], out_specs=tok,
        out_shape=jax.ShapeDtypeStruct((S, ATT_KV * ATT_G * HEAD), BF16), compiler_params=_cp(2),
    )(sinks, _attn_band(), qh, kh, kh, vh, vh)


def _attn_bwd(qh, kh, vh, sinks, doh, *, name="attn_bwd"):
    S = kh.shape[1]
    nb = S // WINDOW

    def body(s_ref, band_ref, q_ref, kc_ref, kp_ref, vc_ref, vp_ref, do_ref, dq_ref, dk_ref, dv_ref, dsk_ref):
        h2, i = pl.program_id(0), pl.program_id(1)

        @pl.when(i == 0)
        def _():
            dk_ref[...] = jnp.zeros_like(dk_ref)
            dv_ref[...] = jnp.zeros_like(dv_ref)
            dsk_ref[...] = jnp.zeros_like(dsk_ref)

        dov = do_ref[...]
        cur = pl.multiple_of(i * WINDOW, WINDOW)
        lane = _iota((8, 128), 1)
        row = _iota((8, 128), 0)
        scale = HEAD ** -0.5
        for hh in range(ATT_HPS):
            q = q_ref[hh].reshape(ATT_G * WINDOW, HEAD)
            do = jnp.concatenate([dov[:, HEAD * (ATT_G * hh + g):HEAD * (ATT_G * hh + g + 1)] for g in range(ATT_G)], axis=0)
            kb = jnp.concatenate([kp_ref[hh], kc_ref[hh]], axis=0)
            vb = jnp.concatenate([vp_ref[hh], vc_ref[hh]], axis=0)
            probs, psink = _attn_probs(q, kb, s_ref, band_ref, ATT_HPS * h2 + hh, i)
            dp = _dot(do, vb, 1, 1)
            delta = jnp.sum(probs * dp, axis=1, keepdims=True)
            ds = (probs * (dp - delta)).astype(BF16)
            dq_ref[hh] = (_dot(ds, kb) * scale).reshape(ATT_G, WINDOW, HEAD)
            dkb = _dot(ds, q, 0, 0) * scale
            dvb = _dot(probs.astype(BF16), do, 0, 0)
            dk_ref[hh, pl.ds(cur, WINDOW), :] += dkb[WINDOW:, :]
            dv_ref[hh, pl.ds(cur, WINDOW), :] += dvb[WINDOW:, :]
            prv = pl.multiple_of(jnp.maximum(i - 1, 0) * WINDOW, WINDOW)
            dk_ref[hh, pl.ds(prv, WINDOW), :] += dkb[:WINDOW, :]
            dv_ref[hh, pl.ds(prv, WINDOW), :] += dvb[:WINDOW, :]

            dsr = -psink * delta
            upd = jnp.zeros((8, 128), F32)
            for gq in range(ATT_G):
                v = jnp.sum(dsr[gq * WINDOW:(gq + 1) * WINDOW, :], axis=0, keepdims=True)
                upd = jnp.where((lane == gq) & (row == 0), v, upd)
            dsk_ref[hh] += upd

    qspec, cur, prev, tok = _attn_specs(S)
    full = pl.BlockSpec((ATT_HPS, S, HEAD), lambda h, i: (h, 0, 0))
    return pl.pallas_call(
        body, name=name, grid=(ATT_KV // ATT_HPS, nb),
        in_specs=[pl.BlockSpec(memory_space=pltpu.SMEM), _BAND, qspec, cur, prev, cur, prev, tok],
        out_specs=[qspec, full, full, pl.BlockSpec((ATT_HPS, 8, 128), lambda h, i: (h, 0, 0))],
        out_shape=[jax.ShapeDtypeStruct((ATT_KV, ATT_G, S, HEAD), F32), jax.ShapeDtypeStruct((ATT_KV, S, HEAD), F32),
                   jax.ShapeDtypeStruct((ATT_KV, S, HEAD), F32), jax.ShapeDtypeStruct((ATT_KV, 8, 128), F32)],
        compiler_params=_cp(2),
    )(sinks, _attn_band(), qh, kh, kh, vh, vh, doh)


def _heads_major(t, nh):
    S = t.shape[0]
    return t.reshape(S, nh, HEAD).transpose(1, 0, 2)


def _tokens_major(t):
    nh, S, _ = t.shape
    return t.transpose(1, 0, 2).reshape(S, nh * HEAD)


class _NoComm:
    def late_weights(self, w, after):
        return w

    def advance(self, after, group=None, tensors=None):
        return None


def _local_step(x, posf, target, w, comm=None):
    S, D = x.shape
    gr = {}
    comm = comm or _NoComm()

    (h1,) = _rms_fwd(x, [w["a_norm"]], name="a_norm_f", dep=w.get("dep"))
    zx = _mm(h1, w["w_zx"], name="in_proj_zx")
    dtr = _mm(h1, w["w_dt"], name="in_proj_dt")
    xbc = _conv_silu_fwd(zx, 2048, 4096, w["a_conv_w"], w["a_conv_b"], name="a_conv_f")
    dt_g, ac_g, sg_g, ac_t = _ssd_prep(dtr, w["a_dt_bias"], w["a_A_log"])
    y_ssd, states = _ssd_fwd(xbc, dt_g, ac_g, ac_t)
    yg = _gate_fwd(y_ssd, xbc, zx, w["a_Dexp"], w["a_gnorm"])
    x1 = _mm(yg, w["a_out_proj"], res=x, name="out_proj")

    w = comm.late_weights(w, x1)
    FW = w["f_w_in"][0].shape[2]

    def ffn_fwd(xin, l):
        (h,) = _rms_fwd(xin, [w["f_norm"][l]], name=f"f_norm_f{l}")
        u = _mm(h, w["f_w_in"][l], name=f"f_in{l}", dims=(S, N_CHIPS * FW, D), tn=FW,
                b_spec=pl.BlockSpec((None, D, FW), lambda i, j, k: (j, 0, 0)))
        a = _ffn_act_fwd(u, w["f_conv_w"][l], w["f_conv_b"][l], name=f"f_act_f{l}")
        xo = _mm(a, w["f_w_down"][l], res=xin, tk=a.shape[1], name=f"f_down{l}")
        return xo, (h, u)

    x2, ffn0 = ffn_fwd(x1, 0)

    hk, hq = _rms_fwd(x2, [w["kv_norm"], w["b_norm"]], name="kvq_norm_f")
    kv = _mm(hk, w["w_kv"], bias=w["b_kv"], name="kv_proj")
    q = _mm(hq, w["w_q"], bias=w["b_q"], name="q_proj")
    rope = _rope_cs(posf)
    kr = _hnrope_fwd(kv, 0, 256, w["k_norm_w"], rope, name="k_rope_f")
    qr = _hnrope_fwd(q, 0, 1024, w["q_norm_w"], rope, name="q_rope_f")
    qh = qr.reshape(ATT_KV, ATT_G, S, HEAD)
    kh = kr
    vh = _heads_major(kv[:, 256:].astype(BF16), ATT_KV)
    att = _attn_fwd(qh, kh, vh, w["sinks"])
    x3 = _mm(att, w["w_o"], bias=w["b_o"], res=x2, name="o_proj")
    x4, ffn1 = ffn_fwd(x3, 1)

    dy, loss_part = _loss(x4, target)

    def ffn_bwd(xin, l, saved, dyo, want_colsum, dep=None):
        h, u = saved
        da = _mm(dyo, w["f_w_down"][l], tb=True, name=f"f_down_dx{l}", dep=dep)
        du, dcw, dcb, a = _ffn_act_bwd(u, w["f_conv_w"][l], w["f_conv_b"][l], da, name=f"f_act_b{l}")
        dw_down = _mm(a, dyo, ta=True, out_dtype=BF16, name=f"f_down_dw{l}")
        dw_in = _mm(h, du, ta=True, out_dtype=BF16, name=f"f_in_dw{l}", dims=(D, N_CHIPS * FW, S), tm=D, tn=FW, tk=S,
                    b_spec=pl.BlockSpec((None, S, FW), lambda i, j, k: (j // 2, 0, j % 2)),
                    o_spec=pl.BlockSpec((None, D, FW), lambda i, j, k: (j, i, 0)), o_shape=(N_CHIPS, D, FW))
        ts = _pick(S, (1024, 512, 256))
        pieces = [(pl.BlockSpec((None, ts, FW), lambda i, j, k, q=q: (q // 2, i, q % 2)),
                   pl.BlockSpec((None, 512, FW), lambda i, j, k, q=q: (q, j, 0))) for q in range(N_CHIPS)]
        dh = _mm(du, w["f_w_in"][l], tb=True, name=f"f_in_dx{l}", dims=(S, D, FW), tm=ts, tn=512, tk=FW,
                 a_spec=pieces[0][0], b_spec=pieces[0][1], more=pieces[1:])
        outs = _rms_bwd(xin, [w["f_norm"][l]], [dh], dyo, name=f"f_norm_b{l}", want_colsum=want_colsum)
        g = dict(f_norm=outs[1], f_w_in=dw_in, f_conv_w=dcw, f_conv_b=dcb, f_w_down=dw_down)
        return outs[0], g, (outs[2] if want_colsum else None)

    dx3, gr["ffn1"], db_o = ffn_bwd(x3, 1, ffn1, dy, True)
    gr["b_o"] = db_o
    gr["w_o"] = _mm(att, dx3, ta=True, out_dtype=BF16, name="o_proj_dw")
    datt = _mm(dx3, w["w_o"], tb=True, out_dtype=BF16, name="o_proj_dx")
    dqh, dkh, dvh, dsk = _attn_bwd(qh, kh, vh, w["sinks"], datt)
    gr["sinks"] = dsk[:, 0, :4].reshape(1, 16)
    dv = _tokens_major(dvh).astype(BF16)
    dq, db_q, dqn = _hnrope_bwd(q, 0, 1024, w["q_norm_w"], rope, dqh.reshape(16, S, HEAD), name="q_rope_b")
    dk, db_k, dkn = _hnrope_bwd(kv, 0, 256, w["k_norm_w"], rope, dkh, name="k_rope_b")
    gr["q_norm"], gr["k_norm"] = dqn[:, :HEAD], dkn[:, :HEAD]
    gr["b_q"] = db_q
    gr["b_kv"] = jnp.concatenate([db_k, _colsum(dv, name="dv_colsum")], axis=1)
    dkv = jnp.concatenate([dk, dv], axis=1)
    gr["w_q"] = _mm(hq, dq, ta=True, out_dtype=BF16, name="q_proj_dw")
    gr["w_kv"] = _mm(hk, dkv, ta=True, out_dtype=BF16, name="kv_proj_dw")
    tok = comm.advance([gr["w_kv"]], 1, dict(f_down1=gr["ffn1"]["f_w_down"], f_in1=gr["ffn1"]["f_w_in"], w_o=gr["w_o"],
                                             w_q=gr["w_q"], w_kv=gr["w_kv"]))
    dhq = _mm(dq, w["w_q"], tb=True, name="q_proj_dx", dep=tok)
    dhk = _mm(dkv, w["w_kv"], tb=True, name="kv_proj_dx")
    dx2, gr["kv_norm"], gr["b_norm"] = _rms_bwd(x2, [w["kv_norm"], w["b_norm"]], [dhk, dhq], dx3, name="kvq_norm_b")

    dx1, gr["ffn0"], _ = ffn_bwd(x1, 0, ffn0, dx2, False, dep=comm.advance([dx2]))

    gr["a_out_proj"] = _mm(yg, dx1, ta=True, out_dtype=BF16, name="out_proj_dw")
    tok = comm.advance([dx1, gr["a_out_proj"]], 2,
                       dict(f_down0=gr["ffn0"]["f_w_down"], f_in0=gr["ffn0"]["f_w_in"], out_proj=gr["a_out_proj"]))
    dyg = _mm(dx1, w["a_out_proj"], tb=True, name="out_proj_dx", dep=tok)
    dy_ssd, dz, gr["a_gnorm"], dD = _gate_bwd(y_ssd, xbc, zx, w["a_Dexp"], w["a_gnorm"], dyg)
    gr["a_D"] = dD[:, :SSM_HEADS]
    dxs, dB, dC, dhead = _ssd_bwd(xbc, dt_g, ac_g, ac_t, states, dy_ssd, w["a_Dexp"], dep=comm.advance([dy_ssd]))
    ddtr, dsmall = _ssd_post(dhead, dt_g, sg_g, w["a_A_log_g"])
    gr["a_A_log"] = dsmall[:, 0, :4].reshape(1, SSM_HEADS)
    gr["a_dt_bias"] = dsmall[:, 1, :4].reshape(1, SSM_HEADS)
    dxbc, gr["a_conv_w"], gr["a_conv_b"] = _conv_silu_bwd(
        zx, 2048, 4096, w["a_conv_w"], w["a_conv_b"], [(dxs, 0), (dB, 2048), (dC, 3072)], name="a_conv_b")
    gr["w_z"] = _mm(h1, dz, ta=True, out_dtype=BF16, name="in_proj_dwz")
    gr["w_x"] = _mm(h1, dxbc, ta=True, out_dtype=BF16, name="in_proj_dwx")
    gr["w_dt"] = _mm(h1, ddtr, ta=True, out_dtype=BF16, name="in_proj_dwdt")
    ts = _pick(S, (1024, 512, 256))
    wblk = lambda q: pl.BlockSpec((512, 2048), lambda i, j, k: (j, q))
    dh1 = _mm(dz, w["w_zx"], tb=True, name="in_proj_dx", dims=(S, D, 2048), tm=ts, tn=512, tk=2048,
              a_spec=pl.BlockSpec((ts, 2048), lambda i, j, k: (i, 0)), b_spec=wblk(0),
              more=[(dxbc, pl.BlockSpec((ts, 2048), lambda i, j, k: (i, 0)), w["w_zx"], wblk(1)),
                    (dxbc, pl.BlockSpec((ts, 2048), lambda i, j, k: (i, 1)), w["w_zx"], wblk(2)),
                    (ddtr, pl.BlockSpec((ts, 128), lambda i, j, k: (i, 0)), w["w_dt"], pl.BlockSpec((512, 128), lambda i, j, k: (j, 0)))])
    dx0, gr["a_norm"] = _rms_bwd(x, [w["a_norm"]], [dh1], dx1, name="a_norm_b")
    tok = comm.advance([dx0], 3, dict(in_proj=_in_proj_grad(gr).reshape(D, N_CHIPS, -1).transpose(1, 0, 2)))
    return loss_part, dx0, gr, tok


def _prep_small(full, w):
    w["a_norm"] = full["a_norm"]
    w["a_conv_w"] = full["a_conv_w"][0]
    w["a_conv_b"] = full["a_conv_b"]
    pad32 = lambda v: jnp.pad(v, ((0, 0), (0, 128 - SSM_HEADS)))
    w["a_dt_bias"] = pad32(full["a_dt_bias"])
    w["a_A_log"] = pad32(full["a_A_log"])
    w["a_A_log_g"] = jnp.pad(full["a_A_log"].reshape(SSM_GROUPS, 1, 4), ((0, 0), (0, 0), (0, 124)))
    w["a_Dexp"] = jnp.repeat(full["a_D"], HEAD, axis=1)
    w["a_gnorm"] = full["a_gnorm"]
    w["f_norm"] = [full["f_norm"][l:l + 1] for l in range(2)]
    w["f_conv_w"] = [full["f_conv_w"][l] for l in range(2)]
    w["f_conv_b"] = [full["f_conv_b"][l:l + 1] for l in range(2)]
    w["kv_norm"] = full["kv_norm"].reshape(1, -1)
    w["b_kv"] = full["b_kv"].reshape(1, -1)
    w["k_norm_w"] = jnp.tile(full["k_norm"].reshape(1, HEAD), (1, ATT_KV))
    w["b_norm"] = full["b_norm"]
    w["b_q"] = full["b_q"]
    w["q_norm_w"] = jnp.tile(full["q_norm"], (1, ATT_KV * ATT_G))
    w["sinks"] = full["sinks"].reshape(-1)
    w["b_o"] = full["b_o"]
    return w


def _split_in_proj(ip):
    return ip[:, :6144].astype(BF16), jnp.pad(ip[:, 6144:], ((0, 0), (0, 128 - SSM_HEADS))).astype(BF16)


def _prep_weights(full):
    w = _prep_small(full, {})
    w["w_zx"], w["w_dt"] = _split_in_proj(full["a_in_proj"][0])
    w["a_out_proj"] = full["a_out_proj"][0].astype(BF16)
    w["f_w_in"] = [full["f_w_in"][l].reshape(1024, N_CHIPS, -1).transpose(1, 0, 2).astype(BF16) for l in range(2)]
    w["f_w_down"] = [full["f_w_down"][l].astype(BF16) for l in range(2)]
    w["w_kv"] = full["w_kv"].astype(BF16)
    w["w_q"] = full["w_q"][0].astype(BF16)
    w["w_o"] = full["w_o"][0].astype(BF16)
    return w


def _small_grads(gr):
    g = {}
    g["a_norm"] = gr["a_norm"]
    g["a_conv_w"] = gr["a_conv_w"][None]
    g["a_conv_b"] = gr["a_conv_b"]
    g["a_dt_bias"], g["a_A_log"], g["a_D"] = gr["a_dt_bias"], gr["a_A_log"], gr["a_D"]
    g["a_gnorm"] = gr["a_gnorm"]
    g["kv_norm"] = gr["kv_norm"].reshape(-1)
    g["b_kv"] = gr["b_kv"].reshape(-1)
    g["k_norm"] = gr["k_norm"].reshape(-1)
    g["b_norm"] = gr["b_norm"]
    g["b_q"] = gr["b_q"]
    g["q_norm"] = gr["q_norm"]
    g["sinks"] = gr["sinks"]
    g["b_o"] = gr["b_o"]
    f = [gr["ffn0"], gr["ffn1"]]
    g["f_norm"] = jnp.concatenate([f[0]["f_norm"], f[1]["f_norm"]], axis=0)
    g["f_conv_w"] = jnp.stack([f[l]["f_conv_w"] for l in range(2)])
    g["f_conv_b"] = jnp.concatenate([f[l]["f_conv_b"] for l in range(2)], axis=0)
    return g


def _in_proj_grad(gr):
    return jnp.concatenate([gr["w_z"], gr["w_x"], gr["w_dt"][:, :SSM_HEADS]], axis=1)


def _full_grads(gr):
    g = _small_grads(gr)
    f32 = lambda t: t.astype(F32)
    g["a_in_proj"] = f32(_in_proj_grad(gr))[None]
    g["a_out_proj"] = f32(gr["a_out_proj"])[None]
    g["w_kv"] = f32(gr["w_kv"])
    g["w_q"] = f32(gr["w_q"])[None]
    g["w_o"] = f32(gr["w_o"])[None]
    f = [gr["ffn0"], gr["ffn1"]]
    g["f_w_in"] = jnp.stack([f32(f[l]["f_w_in"]).transpose(1, 0, 2).reshape(1024, -1) for l in range(2)])
    g["f_w_down"] = jnp.stack([f32(f[l]["f_w_down"]) for l in range(2)])
    return g


MESH = pl.DeviceIdType.MESH
WEIGHTS = ("a_norm", "a_in_proj", "a_conv_w", "a_conv_b", "a_dt_bias", "a_A_log", "a_D", "a_gnorm", "a_out_proj",
           "kv_norm", "w_kv", "b_kv", "k_norm", "b_norm", "w_q", "b_q", "q_norm", "sinks", "w_o", "b_o", "f_norm",
           "f_w_in", "f_conv_w", "f_conv_b", "f_w_down")
MATS = (("in_proj", "a_in_proj", 0), ("out_proj", "a_out_proj", 0), ("w_kv", "w_kv", None), ("w_q", "w_q", 0),
        ("w_o", "w_o", 0), ("f_in0", "f_w_in", 0), ("f_in1", "f_w_in", 1), ("f_down0", "f_w_down", 0),
        ("f_down1", "f_w_down", 1))
SMALL_CUT = (("a_norm", 1), ("a_conv_w", 2), ("a_conv_b", 1), ("a_gnorm", 1), ("f_conv_w", 2))
SMALL_REP = ("a_dt_bias", "a_A_log", "a_D", "kv_norm", "b_kv", "k_norm", "b_norm", "b_q", "q_norm", "sinks", "b_o",
             "f_norm", "f_conv_b")


def _coords():
    return lax.axis_index("x"), lax.axis_index("y"), lax.axis_index("c")


def _other_chips(x, y):
    return [(1 - x, y), (x, 1 - y), (1 - x, 1 - y)]


def _pack(arrs, rows_align, lanes, dtype):
    flat = jnp.concatenate([a.reshape(-1).astype(dtype) for a in arrs])
    per = rows_align * lanes
    total = -(-flat.shape[0] // per) * per
    return jnp.pad(flat, (0, total - flat.shape[0])).reshape(total // lanes, lanes)


def _unpack(flat, shapes):
    out, off = [], 0
    for s in shapes:
        n = math.prod(s)
        out.append(flat[off:off + n].reshape(s))
        off += n
    return out


def _remote(src, dst, send, recv, k, dev):
    return pltpu.make_async_remote_copy(src_ref=src, dst_ref=dst, send_sem=send.at[k], recv_sem=recv.at[k],
                                        device_id=dev, device_id_type=MESH)


_ANY = pl.BlockSpec(memory_space=pl.ANY)


def _halves(t):
    r, c = t.shape
    return t.reshape(2, r // 2, c)


def _gather_weights(shards, sp):
    n = len(shards)
    n_sem = 7 * n + 3

    def body(*refs):
        sh, sp_ref = refs[:n], refs[n]
        outs, sout = refs[n + 1:2 * n + 1], refs[2 * n + 1]
        send, recv, loc = refs[2 * n + 2:]
        x, y, c = _coords()
        me = 2 * x + y
        chips = _other_chips(x, y)
        sib = (x, y, 1 - c)
        l1 = pltpu.make_async_copy(sp_ref, sout.at[me], loc.at[0])
        l1.start()
        sends = []
        for j, (cx, cy) in enumerate(chips):
            sends.append(_remote(sp_ref, sout.at[me], send, recv, 7 * n + j, (cx, cy, c)))
            for t in range(n):
                sends.append(_remote(sh[t].at[c], outs[t].at[me, c], send, recv, 7 * t + j, (cx, cy, c)))
        for t in range(n):
            sends.append(_remote(sh[t], outs[t].at[me], send, recv, 7 * t + 6, sib))
        for cp in sends:
            cp.start()
        for j, (cx, cy) in enumerate(chips):
            src = 2 * cx + cy
            for t in range(n):
                _remote(sh[t].at[c], outs[t].at[src, c], send, recv, 7 * t + j, (cx, cy, c)).wait_recv()
                fwd = _remote(outs[t].at[src, c], outs[t].at[src, c], send, recv, 7 * t + 3 + j, sib)
                fwd.start()
                sends.append(fwd)
        for j, (cx, cy) in enumerate(chips):
            src = 2 * cx + cy
            _remote(sp_ref, sout.at[src], send, recv, 7 * n + j, (cx, cy, c)).wait_recv()
            for t in range(n):
                _remote(outs[t].at[src, 1 - c], outs[t].at[src, 1 - c], send, recv, 7 * t + 3 + j, sib).wait_recv()
        for t in range(n):
            _remote(sh[t], outs[t].at[me], send, recv, 7 * t + 6, sib).wait_recv()
        for cp in sends:
            cp.wait_send()
        l1.wait()

    res = pl.pallas_call(
        body, name="gather_weights", in_specs=[_ANY] * (n + 1), out_specs=[_ANY] * (n + 1),
        out_shape=[jax.ShapeDtypeStruct((N_CHIPS,) + t.shape, t.dtype) for t in shards]
        + [jax.ShapeDtypeStruct((N_CHIPS,) + sp.shape, sp.dtype)],
        scratch_shapes=[pltpu.SemaphoreType.DMA((n_sem,)), pltpu.SemaphoreType.DMA((n_sem,)),
                        pltpu.SemaphoreType.DMA((1,))],
    )(*shards, sp)
    return res[:n], res[n]


_HBM = pl.BlockSpec(memory_space=pltpu.HBM)
_SEMS = pl.BlockSpec(memory_space=pltpu.SEMAPHORE)
_DATAFLOW = pltpu.SideEffectType.DATAFLOW_SIDE_EFFECTING


def _in_hbm(a):
    return pltpu.with_memory_space_constraint(a, pltpu.HBM)


def _start_copies(copies, arrays, n_sem, after, *, name):
    n, na = len(arrays), len(after)

    def body(*refs):
        for mine, _ in copies(refs[:n], refs[n + na], refs[n + na + 1]):
            mine.start()
        refs[-1][...] = jnp.zeros_like(refs[-1])

    res = pl.pallas_call(
        body, name=name, in_specs=[_HBM] * n + [_ANY] * na,
        out_specs=[_SEMS, _SEMS] + [_HBM] * n + [pl.BlockSpec(memory_space=pltpu.VMEM)],
        out_shape=[pltpu.SemaphoreType.DMA((n_sem,)), pltpu.SemaphoreType.DMA((n_sem,))]
        + [pltpu.HBM(a.shape, a.dtype) for a in arrays] + [jax.ShapeDtypeStruct((8, 128), F32)],
        input_output_aliases={t: 2 + t for t in range(n)},
        compiler_params=pltpu.CompilerParams(has_side_effects=_DATAFLOW),
    )(*[_in_hbm(a) for a in arrays], *after)
    return res[0], res[1], list(res[2:2 + n]), res[-1]


def _wait_copies(copies, send, recv, arrays, after, *, name):
    n = len(arrays)

    def body(*refs):
        for mine, theirs in copies(refs[:n], refs[n], refs[n + 1]):
            mine.wait_send()
            theirs.wait_recv()

    return list(pl.pallas_call(
        body, name=name, in_specs=[_HBM] * n + [_SEMS, _SEMS] + [_ANY] * len(after), out_specs=[_HBM] * n,
        out_shape=[pltpu.HBM(a.shape, a.dtype) for a in arrays], input_output_aliases={t: t for t in range(n)},
        compiler_params=pltpu.CompilerParams(has_side_effects=_DATAFLOW),
    )(*arrays, send, recv, *after))


def _sibling_copies(refs, send, recv):
    n = len(refs) // 2
    x, y, c = _coords()
    cps = [_remote(refs[t].at[:, 1 - c], refs[n + t], send, recv, t, (x, y, 1 - c)) for t in range(n)]
    return [(cp, cp) for cp in cps]


def _join_copies(refs, send, recv):
    x, y, c = _coords()
    sib = (x, y, 1 - c)
    return [(_remote(o.at[c], o.at[c], send, recv, t, sib), _remote(o.at[1 - c], o.at[1 - c], send, recv, t, sib))
            for t, o in enumerate(refs)]


def _gather_copies(sh, land, send, recv):
    x, y, c = _coords()
    me = 2 * x + y
    out = []
    for t in range(len(sh)):
        for j, (cx, cy) in enumerate(_other_chips(x, y)):
            dev = (cx, cy, c)
            out.append((_remote(sh[t].at[c], land[t].at[me, c], send, recv, 4 * t + j, dev),
                        _remote(sh[t].at[c], land[t].at[2 * cx + cy, c], send, recv, 4 * t + j, dev)))
        sib = (x, y, 1 - c)
        out.append((_remote(sh[t], land[t].at[me], send, recv, 4 * t + 3, sib),
                    _remote(sh[t], land[t].at[me], send, recv, 4 * t + 3, sib)))
    return out


def _gather_start(shards, after, *, name):
    n = len(shards)

    def body(*refs):
        sh, land = refs[:n], refs[n:2 * n]
        send, recv = refs[2 * n + 1], refs[2 * n + 2]
        token = refs[-1]
        for mine, _ in _gather_copies(sh, land, send, recv):
            mine.start()
        token[...] = jnp.zeros_like(token)

    lands = [_in_hbm(lax.empty((N_CHIPS,) + s.shape, s.dtype)) for s in shards]
    res = pl.pallas_call(
        body, name=name, in_specs=[_HBM] * (2 * n) + [_ANY],
        out_specs=[_SEMS, _SEMS] + [_HBM] * (2 * n) + [pl.BlockSpec(memory_space=pltpu.VMEM)],
        out_shape=[pltpu.SemaphoreType.DMA((4 * n,)), pltpu.SemaphoreType.DMA((4 * n,))]
        + [pltpu.HBM(s.shape, s.dtype) for s in shards] + [pltpu.HBM(l.shape, l.dtype) for l in lands]
        + [jax.ShapeDtypeStruct((8, 128), F32)],
        input_output_aliases={t: 2 + t for t in range(2 * n)},
        compiler_params=pltpu.CompilerParams(has_side_effects=_DATAFLOW),
    )(*[_in_hbm(s) for s in shards], *lands, after)
    return res[0], res[1], res[2:2 + n], res[2 + n:2 + 2 * n], res[-1]


def _gather_wait(send, recv, shards, lands, after, *, name):
    n = len(shards)

    def body(*refs):
        sh, land = refs[:n], refs[n:2 * n]
        send_r, recv_r = refs[2 * n], refs[2 * n + 1]
        for mine, theirs in _gather_copies(sh, land, send_r, recv_r):
            mine.wait_send()
            theirs.wait_recv()

    res = pl.pallas_call(
        body, name=name, in_specs=[_HBM] * (2 * n) + [_SEMS, _SEMS, _ANY], out_specs=[_HBM] * (2 * n),
        out_shape=[pltpu.HBM(s.shape, s.dtype) for s in shards] + [pltpu.HBM(l.shape, l.dtype) for l in lands],
        input_output_aliases={t: t for t in range(2 * n)},
        compiler_params=pltpu.CompilerParams(has_side_effects=_DATAFLOW),
    )(*shards, *lands, send, recv, after)
    return res[n:]


def _gather_forward(lands, *, name):
    n = len(lands)

    def body(*refs):
        o = refs[n:2 * n]
        send, recv = refs[2 * n:]
        x, y, c = _coords()
        sib = (x, y, 1 - c)
        srcs = [2 * cx + cy for cx, cy in _other_chips(x, y)]
        cps = [_remote(o[t].at[s, c], o[t].at[s, c], send, recv, 3 * t + j, sib) for t in range(n) for j, s in enumerate(srcs)]
        for cp in cps:
            cp.start()
        for t in range(n):
            for j, s in enumerate(srcs):
                _remote(o[t].at[s, 1 - c], o[t].at[s, 1 - c], send, recv, 3 * t + j, sib).wait_recv()
        for cp in cps:
            cp.wait_send()

    return pl.pallas_call(
        body, name=name, in_specs=[_ANY] * n, out_specs=[_ANY] * n, input_output_aliases={t: t for t in range(n)},
        out_shape=[jax.ShapeDtypeStruct(l.shape, l.dtype) for l in lands],
        scratch_shapes=[pltpu.SemaphoreType.DMA((3 * n,)), pltpu.SemaphoreType.DMA((3 * n,))],
    )(*lands)


def _small_copies(v, land, send, recv):
    x, y, c = _coords()
    me = 4 * x + 2 * y + c
    out = []
    for k in range(1, 8):
        px = 1 - x if k & 4 else x
        py = 1 - y if k & 2 else y
        pc = 1 - c if k & 1 else c
        out.append((_remote(v, land.at[me], send, recv, k - 1, (px, py, pc)),
                    _remote(v, land.at[4 * px + 2 * py + pc], send, recv, k - 1, (px, py, pc))))
    return out


def _small_start(v, after, *, name):
    def body(v_ref, land_ref, after_ref, send, recv, v_thru, land_thru, token):
        for mine, _ in _small_copies(v_ref, land_ref, send, recv):
            mine.start()
        token[...] = jnp.zeros_like(token)

    land = _in_hbm(lax.empty((8,) + v.shape, v.dtype))
    return pl.pallas_call(
        body, name=name, in_specs=[_HBM, _HBM, _ANY],
        out_specs=[_SEMS, _SEMS, _HBM, _HBM, pl.BlockSpec(memory_space=pltpu.VMEM)],
        out_shape=[pltpu.SemaphoreType.DMA((7,)), pltpu.SemaphoreType.DMA((7,)), pltpu.HBM(v.shape, v.dtype),
                   pltpu.HBM(land.shape, land.dtype), jax.ShapeDtypeStruct((8, 128), F32)],
        input_output_aliases={0: 2, 1: 3}, compiler_params=pltpu.CompilerParams(has_side_effects=_DATAFLOW),
    )(_in_hbm(v), land, after)


def _small_wait(send, recv, v, land, after, *, name):
    def body(v_ref, land_ref, send_r, recv_r, *rest):
        for mine, theirs in _small_copies(v_ref, land_ref, send_r, recv_r):
            mine.wait_send()
            theirs.wait_recv()

    return pl.pallas_call(
        body, name=name, in_specs=[_HBM, _HBM, _SEMS, _SEMS] + [_ANY] * len(after), out_specs=[_HBM, _HBM],
        out_shape=[pltpu.HBM(v.shape, v.dtype), pltpu.HBM(land.shape, land.dtype)],
        input_output_aliases={0: 0, 1: 1}, compiler_params=pltpu.CompilerParams(has_side_effects=_DATAFLOW),
    )(v, land, send, recv, *after)


def _small_sum(v, land, me_idx, *, name="small_sum"):
    def body(me_ref, v_ref, land_ref, o_ref):
        acc = None
        for s in range(8):
            term = jnp.where(me_ref[0] == s, v_ref[...], land_ref[s])
            acc = term if acc is None else acc + term
        o_ref[...] = acc

    whole = lambda shape: pl.BlockSpec(shape, lambda i, me_ref: (0,) * len(shape))
    return pl.pallas_call(
        body, name=name,
        grid_spec=pltpu.PrefetchScalarGridSpec(num_scalar_prefetch=1, grid=(1,), in_specs=[whole(v.shape), whole(land.shape)],
                                               out_specs=whole(v.shape)),
        out_shape=jax.ShapeDtypeStruct(v.shape, F32), compiler_params=_cp(1),
    )(me_idx, v, land)


def _rs_to_sibling(gs, *, name):
    n = len(gs)

    def body(*refs):
        g, a = refs[:n], refs[n:2 * n]
        send, recv = refs[2 * n:]
        x, y, c = _coords()
        cps = [_remote(g[t].at[:, 1 - c], a[t], send, recv, t, (x, y, 1 - c)) for t in range(n)]
        for cp in cps:
            cp.start()
        for cp in cps:
            cp.wait()

    return pl.pallas_call(
        body, name=name, in_specs=[_ANY] * n, out_specs=[_ANY] * n,
        out_shape=[jax.ShapeDtypeStruct((N_CHIPS,) + g.shape[2:], g.dtype) for g in gs],
        scratch_shapes=[pltpu.SemaphoreType.DMA((n,)), pltpu.SemaphoreType.DMA((n,))],
    )(*gs)


RS_ROW_SPLIT = 2


def _rs_add_pair(gs, as_, c_idx, *, name):
    n = len(gs)

    def body(c_ref, *refs):
        for t in range(n):
            refs[2 * n + t][...] = (refs[t][...].astype(F32) + refs[n + t][...].astype(F32)).astype(BF16)

    def gspec(g):
        _, _, rh, cols = g.shape
        return pl.BlockSpec((None, None, rh // RS_ROW_SPLIT, cols), lambda j, i, c_ref: (j, c_ref[0], i, 0))

    def pspec(g):
        _, _, rh, cols = g.shape
        return pl.BlockSpec((None, rh // RS_ROW_SPLIT, cols), lambda j, i, c_ref: (j, i, 0))

    return pl.pallas_call(
        body, name=name,
        grid_spec=pltpu.PrefetchScalarGridSpec(
            num_scalar_prefetch=1, grid=(N_CHIPS, RS_ROW_SPLIT),
            in_specs=[gspec(g) for g in gs] + [pspec(g) for g in gs], out_specs=[pspec(g) for g in gs]),
        out_shape=[jax.ShapeDtypeStruct((N_CHIPS,) + g.shape[2:], BF16) for g in gs], compiler_params=_cp(2),
    )(c_idx, *gs, *as_)


def _chips_copies(p, r, send, recv):
    x, y, c = _coords()
    return [_remote(p[t].at[2 * cx + cy], r[t].at[k], send, recv, 3 * t + k, (cx, cy, c))
            for k, (cx, cy) in enumerate(_other_chips(x, y)) for t in range(len(p))]


def _rs_chips_start(ps, after, *, name):
    n, na = len(ps), len(after)

    def body(*refs):
        p, r = refs[:n], refs[n:2 * n]
        send, recv = refs[2 * n + na], refs[2 * n + na + 1]
        token = refs[-1]
        for cp in _chips_copies(p, r, send, recv):
            cp.start()
        token[...] = jnp.zeros_like(token)

    lands = [_in_hbm(lax.empty((3,) + p.shape[1:], p.dtype)) for p in ps]
    res = pl.pallas_call(
        body, name=name, in_specs=[_HBM] * (2 * n) + [_ANY] * na,
        out_specs=[_SEMS, _SEMS] + [_HBM] * (2 * n) + [pl.BlockSpec(memory_space=pltpu.VMEM)],
        out_shape=[pltpu.SemaphoreType.DMA((3 * n,)), pltpu.SemaphoreType.DMA((3 * n,))]
        + [pltpu.HBM(p.shape, p.dtype) for p in ps] + [pltpu.HBM(l.shape, l.dtype) for l in lands]
        + [jax.ShapeDtypeStruct((8, 128), F32)],
        input_output_aliases={t: 2 + t for t in range(2 * n)},
        compiler_params=pltpu.CompilerParams(has_side_effects=_DATAFLOW),
    )(*[_in_hbm(p) for p in ps], *lands, *after)
    return res[0], res[1], res[2:2 + n], res[2 + n:2 + 2 * n], res[-1]


def _rs_chips_wait(send, recv, ps, lands, after, *, name):
    n = len(ps)

    def body(*refs):
        p, r = refs[:n], refs[n:2 * n]
        for cp in _chips_copies(p, r, refs[2 * n], refs[2 * n + 1]):
            cp.wait_send()
            cp.wait_recv()

    res = pl.pallas_call(
        body, name=name, in_specs=[_HBM] * (2 * n) + [_SEMS, _SEMS] + [_ANY] * len(after), out_specs=[_HBM] * (2 * n),
        out_shape=[pltpu.HBM(p.shape, p.dtype) for p in ps] + [pltpu.HBM(l.shape, l.dtype) for l in lands],
        input_output_aliases={t: t for t in range(2 * n)},
        compiler_params=pltpu.CompilerParams(has_side_effects=_DATAFLOW),
    )(*ps, *lands, send, recv, *after)
    return res[:n], res[n:]


def _rs_add_chips(ps, rs, idx, *, name):
    n = len(ps)

    def body(idx_ref, *refs):
        for t in range(n):
            p_ref, r0, r1, r2 = refs[4 * t:4 * t + 4]
            refs[4 * n + t][...] = ((p_ref[...].astype(F32) + r0[...].astype(F32)) + r1[...].astype(F32)) + r2[...].astype(F32)

    in_specs, args = [], []
    for p, r in zip(ps, rs):
        _, rh, cols = p.shape
        blk = (None, rh // RS_ROW_SPLIT, cols)
        in_specs.append(pl.BlockSpec(blk, lambda i, idx_ref: (idx_ref[0], i, 0)))
        in_specs += [pl.BlockSpec(blk, lambda i, idx_ref, k=k: (k, i, 0)) for k in range(3)]
        args += [p, r, r, r]
    out_specs = [pl.BlockSpec((None, p.shape[1] // RS_ROW_SPLIT, p.shape[2]), lambda i, idx_ref: (idx_ref[1], i, 0))
                 for p in ps]
    return pl.pallas_call(
        body, name=name,
        grid_spec=pltpu.PrefetchScalarGridSpec(num_scalar_prefetch=1, grid=(RS_ROW_SPLIT,), in_specs=in_specs,
                                               out_specs=out_specs),
        out_shape=[jax.ShapeDtypeStruct((2,) + p.shape[1:], F32) for p in ps], compiler_params=_cp(1),
    )(idx, *args)


def _rs_join_halves(hs, *, name):
    n = len(hs)

    def body(*refs):
        o = refs[n:2 * n]
        send, recv = refs[2 * n:]
        x, y, c = _coords()
        cps = [_remote(o[t].at[c], o[t].at[c], send, recv, t, (x, y, 1 - c)) for t in range(n)]
        for cp in cps:
            cp.start()
        for t in range(n):
            _remote(o[t].at[1 - c], o[t].at[1 - c], send, recv, t, (x, y, 1 - c)).wait_recv()
        for cp in cps:
            cp.wait_send()

    return pl.pallas_call(
        body, name=name, in_specs=[_ANY] * n, out_specs=[_ANY] * n,
        input_output_aliases={t: t for t in range(n)},
        out_shape=[jax.ShapeDtypeStruct(h.shape, F32) for h in hs],
        scratch_shapes=[pltpu.SemaphoreType.DMA((n,)), pltpu.SemaphoreType.DMA((n,))],
    )(*hs)


def _adamw(w, gs, m, v, *, name, dep=None):
    L, Rr, C = w.shape
    tr, tc = _pick(Rr, (256, 128, 64)), C
    if tr == Rr and Rr * C > 512 * 1024:
        tc = 256
    bc1 = 1.0 - ADAM_B1 ** ADAM_STEP
    bc2 = 1.0 - ADAM_B2 ** ADAM_STEP
    nd = 0 if dep is None else 1

    def body(*refs):
        w_ref, m_ref, v_ref = refs[0], refs[1], refs[2]
        g_refs = refs[3:3 + L]
        d_ref, mo_ref, vo_ref, go_ref = refs[3 + L + nd:]
        layer = pl.program_id(0)
        gv = g_refs[0][...]
        for q in range(1, L):
            gv = jnp.where(layer == q, g_refs[q][...], gv)
        mn = ADAM_B1 * m_ref[...] + (1.0 - ADAM_B1) * gv
        vn = ADAM_B2 * v_ref[...] + (1.0 - ADAM_B2) * (gv * gv)
        go_ref[...] = gv
        mo_ref[...] = mn
        vo_ref[...] = vn
        d_ref[...] = -ADAM_LR * ((mn / bc1) / (jnp.sqrt(vn / bc2) + ADAM_EPS) + ADAM_WD * w_ref[...])

    blk = pl.BlockSpec((None, tr, tc), lambda l, i, j: (l, i, j))
    gblks = [pl.BlockSpec((tr, tc), lambda l, i, j, q=q: (jnp.where(l == q, i, 0), jnp.where(l == q, j, 0))) for q in range(L)]
    return pl.pallas_call(
        body, name=name, grid=(L, Rr // tr, C // tc), in_specs=[blk] * 3 + gblks + [_ANY] * nd, out_specs=[blk] * 4,
        out_shape=[jax.ShapeDtypeStruct((L, Rr, C), F32)] * 4, compiler_params=_cp(3),
    )(w, m, v, *gs, *([] if dep is None else [dep]))


def kernel(x, positions, a_norm, a_in_proj, a_conv_w, a_conv_b, a_dt_bias, a_A_log, a_D, a_gnorm, a_out_proj,
           kv_norm, w_kv, b_kv, k_norm, b_norm, w_q, b_q, q_norm, sinks, w_o, b_o, f_norm, f_w_in, f_conv_w,
           f_conv_b, f_w_down, loss_target, m_a_norm, m_a_in_proj, m_a_conv_w, m_a_conv_b, m_a_dt_bias, m_a_A_log,
           m_a_D, m_a_gnorm, m_a_out_proj, m_kv_norm, m_w_kv, m_b_kv, m_k_norm, m_b_norm, m_w_q, m_b_q, m_q_norm,
           m_sinks, m_w_o, m_b_o, m_f_norm, m_f_w_in, m_f_conv_w, m_f_conv_b, m_f_w_down, v_a_norm, v_a_in_proj,
           v_a_conv_w, v_a_conv_b, v_a_dt_bias, v_a_A_log, v_a_D, v_a_gnorm, v_a_out_proj, v_kv_norm, v_w_kv,
           v_b_kv, v_k_norm, v_b_norm, v_w_q, v_b_q, v_q_norm, v_sinks, v_w_o, v_b_o, v_f_norm, v_f_w_in,
           v_f_conv_w, v_f_conv_b, v_f_w_down):
    wl = dict(zip(WEIGHTS, (a_norm, a_in_proj, a_conv_w, a_conv_b, a_dt_bias, a_A_log, a_D, a_gnorm, a_out_proj,
                            kv_norm, w_kv, b_kv, k_norm, b_norm, w_q, b_q, q_norm, sinks, w_o, b_o, f_norm, f_w_in,
                            f_conv_w, f_conv_b, f_w_down)))
    ml = dict(zip(WEIGHTS, (m_a_norm, m_a_in_proj, m_a_conv_w, m_a_conv_b, m_a_dt_bias, m_a_A_log, m_a_D, m_a_gnorm,
                            m_a_out_proj, m_kv_norm, m_w_kv, m_b_kv, m_k_norm, m_b_norm, m_w_q, m_b_q, m_q_norm,
                            m_sinks, m_w_o, m_b_o, m_f_norm, m_f_w_in, m_f_conv_w, m_f_conv_b, m_f_w_down)))
    vl = dict(zip(WEIGHTS, (v_a_norm, v_a_in_proj, v_a_conv_w, v_a_conv_b, v_a_dt_bias, v_a_A_log, v_a_D, v_a_gnorm,
                            v_a_out_proj, v_kv_norm, v_w_kv, v_b_kv, v_k_norm, v_b_norm, v_w_q, v_b_q, v_q_norm,
                            v_sinks, v_w_o, v_b_o, v_f_norm, v_f_w_in, v_f_conv_w, v_f_conv_b, v_f_w_down)))
    xi, yi, ci = _coords()
    me = 2 * xi + yi
    S = x.shape[1]

    def block_of(n, layer):
        t = wl[n]
        return t if layer is None else t[layer]

    rows = lambda t: t.reshape(-1, t.shape[-1])
    c_idx = jnp.reshape(ci, (1,)).astype(jnp.int32)
    me_c = jnp.stack([me, ci]).astype(jnp.int32)
    early = ("in_proj", "out_proj")
    late = tuple(name for name, _, _ in MATS if name not in early)
    shards = {name: _halves(block_of(wn, layer).astype(BF16)) for name, wn, layer in MATS}

    sp = _pack([wl[n] for n, _ in SMALL_CUT], 8, 128, F32)
    gathered, gs = _gather_weights([shards[k] for k in early], sp)
    gt = {k: t.reshape(N_CHIPS, -1, t.shape[-1]) for k, t in zip(early, gathered)}
    started = _gather_start([shards[k] for k in late], gs, name="gather_late_start")
    full = {n: wl[n] for n in SMALL_REP}
    gs = gs.reshape(N_CHIPS, -1)
    pieces = [_unpack(gs[j], [wl[n].shape for n, _ in SMALL_CUT]) for j in range(N_CHIPS)]
    for q, (n, ax) in enumerate(SMALL_CUT):
        full[n] = jnp.concatenate([pieces[j][q] for j in range(N_CHIPS)], axis=ax)
    w = _prep_small(full, {})
    w["w_zx"], w["w_dt"] = _split_in_proj(gt["in_proj"].transpose(1, 0, 2).reshape(1024, -1))
    w["a_out_proj"] = rows(gt["out_proj"])
    w["dep"] = started[4]

    class Comm:
        flight = []
        reduced = {}

        def late_weights(self, w, after):
            lands = _gather_wait(started[0], started[1], started[2], started[3], after, name="gather_late_wait")
            lands = _gather_forward(lands, name="gather_late_forward")
            lt = {k: t.reshape(N_CHIPS, -1, t.shape[-1]) for k, t in zip(late, lands)}
            w = dict(w)
            w["w_kv"], w["w_q"], w["w_o"] = (rows(lt[k]) for k in ("w_kv", "w_q", "w_o"))
            w["f_w_in"] = [lt["f_in0"], lt["f_in1"]]
            w["f_w_down"] = [rows(lt["f_down0"]), rows(lt["f_down1"])]
            return w

        def advance(self, after, group=None, tensors=None):
            token = None
            for grp in list(self.flight):
                tag, n = grp["tag"], len(grp["names"])
                dep = list(after) + ([] if token is None else [token])
                if grp["stage"] == "sibling":
                    arrs = _wait_copies(_sibling_copies, grp["send"], grp["recv"], grp["arrays"], dep, name=f"rs_sibling_wait{tag}")
                    pairs = _rs_add_pair(arrs[:n], arrs[n:], c_idx, name=f"rs_add_pair{tag}")
                    send, recv, ps, lands, token = _rs_chips_start(pairs, dep, name=f"rs_chips_start{tag}")
                    grp.update(stage="chips", send=send, recv=recv, ps=ps, lands=lands)
                elif grp["stage"] == "chips":
                    ps, rs = _rs_chips_wait(grp["send"], grp["recv"], grp["ps"], grp["lands"], dep, name=f"rs_chips_wait{tag}")
                    halves = _rs_add_chips(ps, rs, me_c, name=f"rs_add_chips{tag}")
                    send, recv, arrs, token = _start_copies(_join_copies, halves, n, dep, name=f"rs_join_start{tag}")
                    grp.update(stage="join", send=send, recv=recv, arrays=arrs)
                else:
                    joined = _wait_copies(_join_copies, grp["send"], grp["recv"], grp["arrays"], dep, name=f"rs_join_wait{tag}")
                    self.reduced.update({k: rows(t) for k, t in zip(grp["names"], joined)})
                    self.flight.remove(grp)
            if group is not None:
                names = list(tensors)
                glist = [tensors[k].reshape(N_CHIPS, 2, -1, tensors[k].shape[-1]) for k in names]
                lands = [lax.empty((N_CHIPS,) + gq.shape[2:], gq.dtype) for gq in glist]
                dep = list(after) + ([] if token is None else [token])
                send, recv, arrs, token = _start_copies(_sibling_copies, glist + lands, len(names), dep,
                                                        name=f"rs_sibling_start{group}")
                self.flight.append(dict(tag=group, names=names, stage="sibling", send=send, recv=recv, arrays=arrs))
            return token

    comm = Comm()

    posf = positions.reshape(S, 1).astype(F32)
    loss_part, dx0, gr, tok = _local_step(x[0], posf, loss_target[0], w, comm)
    g = _small_grads(gr)

    small_names = [n for n, _ in SMALL_CUT] + list(SMALL_REP)
    sv = _pack([g[n] for n in small_names] + [loss_part[0:1, 0:1]], 8, 128, F32)
    s_send, s_recv, sv, s_land, s_token = _small_start(sv, tok, name="small_start")

    grads, delta, new_m, new_v = {}, {}, {}, {}

    def update(wn, dep):
        gl = [comm.reduced[name] for name, n2, _ in MATS if n2 == wn]
        shp = wl[wn].shape
        three = (len(gl),) + gl[0].shape
        flip = shp[-1] % 128 != 0
        view = (lambda t: t.reshape(three).transpose(0, 2, 1)) if flip else (lambda t: t.reshape(three))
        back = (lambda t: t.transpose(0, 2, 1).reshape(shp)) if flip else (lambda t: t.reshape(shp))
        if flip:
            gl = [t.T for t in gl]
        d, mn, vn, go = _adamw(view(wl[wn]), gl, view(ml[wn]), view(vl[wn]), name="adamw_" + wn, dep=dep)
        grads[wn], delta[wn], new_m[wn], new_v[wn] = back(go), back(d), back(mn), back(vn)
        return d

    first = [update(wn, s_token) for wn in ("w_q", "w_o", "w_kv")]
    tok = comm.advance(first)
    second = [update(wn, tok) for wn in ("f_w_in", "f_w_down", "a_out_proj")]
    comm.advance(second)
    comm.advance(second)
    update("a_in_proj", None)
    done = first + second

    sv, s_land = _small_wait(s_send, s_recv, sv, s_land, done, name="small_wait")
    sred = _small_sum(sv, s_land, jnp.reshape(2 * me + ci, (1,)).astype(jnp.int32)).reshape(-1)
    small_shapes = [g[n].shape for n in small_names] + [(1,)]
    sg = dict(zip(small_names + ["loss"], _unpack(sred, small_shapes)))
    loss = sg["loss"].reshape(())
    g_small = {}
    for n, ax in SMALL_CUT:
        size = wl[n].shape[ax]
        g_small[n] = lax.dynamic_slice_in_dim(sg[n], me * size, size, axis=ax)
    for n in SMALL_REP:
        g_small[n] = sg[n].reshape(wl[n].shape)

    pk = lambda d: _pack([d[n] for n in small_names], 8, 128, F32)[None]
    d, mn, vn, _ = _adamw(pk(wl), [pk(g_small)[0]], pk(ml), pk(vl), name="adamw_small")
    shapes = [wl[n].shape for n in small_names]
    for n, dd, mm, vv in zip(small_names, _unpack(d.reshape(-1), shapes), _unpack(mn.reshape(-1), shapes),
                             _unpack(vn.reshape(-1), shapes)):
        grads[n], delta[n], new_m[n], new_v[n] = g_small[n], dd, mm, vv

    return (loss, dx0[None], *[grads[n] for n in WEIGHTS], *[delta[n] for n in WEIGHTS],
            *[new_m[n] for n in WEIGHTS], *[new_v[n] for n in WEIGHTS])
```

```python
import math

import jax
import jax.numpy as jnp
from jax import lax
from jax.experimental import pallas as pl
from jax.experimental.pallas import tpu as pltpu

F32 = jnp.float32
BF16 = jnp.bfloat16

EPS = 1e-5
CHUNK = 256
WINDOW = 128
HEAD = 64
SSM_HEADS = 32
SSM_GROUPS = 8
SSM_STATE = 128
ATT_KV = 4
ATT_G = 4
ROPE_THETA = 10000.0
NEG = -1e30
N_CHIPS = 4
VMEM_LIMIT = 56 * 1024 * 1024

ADAM_LR, ADAM_B1, ADAM_B2, ADAM_EPS, ADAM_WD, ADAM_STEP = 0.001, 0.9, 0.999, 1e-08, 0.01, 10


def _cp(n_axes):
    return pltpu.CompilerParams(dimension_semantics=("arbitrary",) * n_axes, vmem_limit_bytes=VMEM_LIMIT)


def _pick(dim, prefs):
    for p in prefs:
        if dim % p == 0:
            return p
    return dim


def _iota(shape, dim):
    return lax.broadcasted_iota(jnp.int32, shape, dim)


def _dot(a, b, ca=1, cb=0):
    return lax.dot_general(a, b, (((ca,), (cb,)), ((), ())), preferred_element_type=F32)


def _dot3(x, ind):
    h = x.astype(BF16)
    r = x - h.astype(F32)
    m = r.astype(BF16)
    lo = (r - m.astype(F32)).astype(BF16)
    return _dot(h, ind) + _dot(m, ind) + _dot(lo, ind)


def _sigmoid(x):
    return jax.nn.sigmoid(x)


def _mm(a, b, *, name, ta=False, tb=False, bias=None, res=None, out_dtype=F32, b_koff=0, tm=None, tn=None, tk=None,
        dims=None, a_spec=None, b_spec=None, o_spec=None, o_shape=None, dep=None, more=(), target=None):
    if dims is not None:
        M, N, K = dims
    else:
        if ta:
            K, M = a.shape
        else:
            M, K = a.shape
        N = b.shape[0] if tb else b.shape[1]
    tm = tm or _pick(M, (1024, 1408, 512, 256, 128))
    tn = tn or _pick(N, (512, 1408, 256, 128))
    tk = tk or (K if K <= 2048 else _pick(K, (2048, 1408, 1024, 512)))
    assert M % tm == 0 and N % tn == 0 and K % tk == 0 and b_koff % tk == 0
    nk = K // tk
    kb0 = b_koff // tk
    has_bias, has_res = bias is not None, res is not None

    def body(*refs):
        a_ref, b_ref = refs[0], refs[1]
        pos = 2
        bias_ref = res_ref = acc_ref = None
        if has_bias:
            bias_ref = refs[pos]
            pos += 1
        if has_res:
            res_ref = refs[pos]
            pos += 1
        if dep is not None:
            pos += 1
        extra = refs[pos:pos + 2 * len(more)]
        pos += 2 * len(more)
        tgt_ref = lp_ref = None
        if target is not None:
            tgt_ref, o_ref, lp_ref = refs[pos], refs[pos + 1], refs[pos + 2]
            pos += 2
        else:
            o_ref = refs[pos]
        if nk > 1:
            acc_ref = refs[pos + 1]
        part = _dot(a_ref[...].astype(BF16), b_ref[...].astype(BF16), 0 if ta else 1, 1 if tb else 0)
        for q in range(len(more)):
            part = part + _dot(extra[2 * q][...].astype(BF16), extra[2 * q + 1][...].astype(BF16),
                               0 if ta else 1, 1 if tb else 0)

        def finish(acc):
            if has_bias:
                acc = acc + bias_ref[...]
            if has_res:
                acc = acc + res_ref[...]
            if target is not None:
                err = acc - tgt_ref[...]
                acc = err * (1.0 / N)
                part_loss = jnp.sum(jnp.sum(err * err, axis=1, keepdims=True), axis=0, keepdims=True) * (0.5 / N)
                first = (pl.program_id(0) == 0) & (pl.program_id(1) == 0)

                @pl.when(first)
                def _():
                    lp_ref[...] = jnp.broadcast_to(part_loss, lp_ref.shape)

                @pl.when(jnp.logical_not(first))
                def _():
                    lp_ref[...] += jnp.broadcast_to(part_loss, lp_ref.shape)

            o_ref[...] = acc.astype(out_dtype)

        if nk == 1:
            finish(part)
        else:
            k = pl.program_id(2)

            @pl.when(k == 0)
            def _():
                acc_ref[...] = part

            @pl.when(k > 0)
            def _():
                acc_ref[...] += part

            @pl.when(k == nk - 1)
            def _():
                finish(acc_ref[...])

    if a_spec is None:
        a_spec = pl.BlockSpec((tk, tm), lambda i, j, k: (k, i)) if ta else pl.BlockSpec((tm, tk), lambda i, j, k: (i, k))
    if b_spec is None:
        b_spec = (pl.BlockSpec((tn, tk), lambda i, j, k: (j, k + kb0)) if tb
                  else pl.BlockSpec((tk, tn), lambda i, j, k: (k + kb0, j)))
    if o_spec is None:
        o_spec = pl.BlockSpec((tm, tn), lambda i, j, k: (i, j))
    in_specs, args = [a_spec, b_spec], [a, b]
    if has_bias:
        in_specs.append(pl.BlockSpec((1, tn), lambda i, j, k: (0, j)))
        args.append(bias)
    if has_res:
        in_specs.append(pl.BlockSpec((tm, tn), lambda i, j, k: (i, j)))
        args.append(res)
    if dep is not None:
        in_specs.append(pl.BlockSpec(memory_space=pl.ANY))
        args.append(dep)
    for piece in more:
        a2, sa, b2, sb = piece if len(piece) == 4 else (a, piece[0], b, piece[1])
        in_specs += [sa, sb]
        args += [a2, b2]
    out_specs, out_shape = o_spec, jax.ShapeDtypeStruct(o_shape or (M, N), out_dtype)
    if target is not None:
        in_specs.append(pl.BlockSpec((tm, tn), lambda i, j, k: (i, j)))
        args.append(target)
        out_specs = [o_spec, pl.BlockSpec((8, 128), lambda i, j, k: (0, 0))]
        out_shape = [out_shape, jax.ShapeDtypeStruct((8, 128), F32)]
    return pl.pallas_call(
        body, name=name, grid=(M // tm, N // tn, nk), in_specs=in_specs, out_specs=out_specs, out_shape=out_shape,
        scratch_shapes=[pltpu.VMEM((tm, tn), F32)] if nk > 1 else [],
        compiler_params=_cp(3),
    )(*args)


def _rms_fwd(x, gains, *, name, tr=256, dep=None):
    S, D = x.shape
    n = len(gains)
    nd = 0 if dep is None else 1

    def body(*refs):
        xv = refs[0][...]
        xh = xv * lax.rsqrt(jnp.mean(xv * xv, axis=-1, keepdims=True) + EPS)
        for q in range(n):
            refs[1 + n + nd + q][...] = (xh * refs[1 + q][...]).astype(BF16)

    row = pl.BlockSpec((tr, D), lambda i: (i, 0))
    vec = pl.BlockSpec((1, D), lambda i: (0, 0))
    return pl.pallas_call(
        body, name=name, grid=(S // tr,), in_specs=[row] + [vec] * n + [pl.BlockSpec(memory_space=pl.ANY)] * nd,
        out_specs=[row] * n, out_shape=[jax.ShapeDtypeStruct((S, D), BF16)] * n, compiler_params=_cp(1),
    )(x, *gains, *([] if dep is None else [dep]))


def _rms_bwd(x, gains, dhs, dres, *, name, tr=256, want_colsum=False):
    S, D = x.shape
    n = len(gains)
    steps = S // tr

    def body(*refs):
        x_ref = refs[0]
        g_refs = refs[1:1 + n]
        dh_refs = refs[1 + n:1 + 2 * n]
        dres_ref = refs[1 + 2 * n]
        dx_ref = refs[2 + 2 * n]
        dg_refs = refs[3 + 2 * n:3 + 3 * n]
        cs_ref = refs[3 + 3 * n] if want_colsum else None
        i = pl.program_id(0)
        xv = x_ref[...]
        r = lax.rsqrt(jnp.mean(xv * xv, axis=-1, keepdims=True) + EPS)
        xh = xv * r
        dx = dres_ref[...]
        for q in range(n):
            dh = dh_refs[q][...]
            dxh = dh * g_refs[q][...]
            dx = dx + r * (dxh - xh * jnp.mean(dxh * xh, axis=-1, keepdims=True))
            part = jnp.sum(dh * xh, axis=0, keepdims=True)

            @pl.when(i == 0)
            def _():
                dg_refs[q][...] = part

            @pl.when(i > 0)
            def _():
                dg_refs[q][...] += part

        dx_ref[...] = dx
        if want_colsum:
            cpart = jnp.sum(dx, axis=0, keepdims=True)

            @pl.when(i == 0)
            def _():
                cs_ref[...] = cpart

            @pl.when(i > 0)
            def _():
                cs_ref[...] += cpart

    row = pl.BlockSpec((tr, D), lambda i: (i, 0))
    vec = pl.BlockSpec((1, D), lambda i: (0, 0))
    n_vec_out = n + (1 if want_colsum else 0)
    outs = pl.pallas_call(
        body, name=name, grid=(steps,), in_specs=[row] + [vec] * n + [row] * n + [row],
        out_specs=[row] + [vec] * n_vec_out,
        out_shape=[jax.ShapeDtypeStruct((S, D), F32)] + [jax.ShapeDtypeStruct((1, D), F32)] * n_vec_out,
        compiler_params=_cp(1),
    )(x, *gains, *dhs, dres)
    return outs


def _colsum(x, *, name, tr=256):
    S, D = x.shape

    def body(x_ref, o_ref):
        i = pl.program_id(0)
        part = jnp.sum(x_ref[...].astype(F32), axis=0, keepdims=True)

        @pl.when(i == 0)
        def _():
            o_ref[...] = part

        @pl.when(i > 0)
        def _():
            o_ref[...] += part

    return pl.pallas_call(
        body, name=name, grid=(S // tr,), in_specs=[pl.BlockSpec((tr, D), lambda i: (i, 0))],
        out_specs=pl.BlockSpec((1, D), lambda i: (0, 0)), out_shape=jax.ShapeDtypeStruct((1, D), F32),
        compiler_params=_cp(1),
    )(x)


STRIP = 64
HALO = 8


def _strips(S, tc):
    return [(r0, slice(l0, l0 + 128)) for l0 in range(0, tc, 128) for r0 in range(S - STRIP, -1, -STRIP)]


def _with_halo(ref, r0, ls):
    if r0 == 0:
        return jnp.concatenate([jnp.zeros((HALO, 128), F32), ref[0:STRIP, ls]], axis=0)
    return ref[r0 - HALO:r0 + STRIP, ls]


def _conv_strip(xw, w_ref, b_ref, ls, width):
    acc = b_ref[:, ls] + w_ref[pl.ds(width - 1, 1), ls] * xw[HALO:]
    shifted = []
    for s in range(1, width):
        xs = pltpu.roll(xw, s, axis=0)[HALO:]
        shifted.append(xs)
        acc = acc + w_ref[pl.ds(width - 1 - s, 1), ls] * xs
    return acc, shifted


def _conv_strip_back(dacc, after, xc, shifted, w_ref, ls, width):
    ext = jnp.concatenate([dacc, after], axis=0)
    dx = w_ref[pl.ds(width - 1, 1), ls] * dacc
    dws = [None] * width
    dws[width - 1] = jnp.sum(dacc * xc, axis=0, keepdims=True)
    for s in range(1, width):
        dx = dx + w_ref[pl.ds(width - 1 - s, 1), ls] * pltpu.roll(ext, STRIP + HALO - s, axis=0)[:STRIP]
        dws[width - 1 - s] = jnp.sum(dacc * shifted[s - 1], axis=0, keepdims=True)
    return dx, dws, jnp.sum(dacc, axis=0, keepdims=True)


def _conv_back_block(S, tc, width, w_ref, b_ref, x_ref, dacc_of, dx_store, dw_ref, db_ref):
    for l0 in range(0, tc, 128):
        ls = slice(l0, l0 + 128)
        after = jnp.zeros((HALO, 128), F32)
        tot = None
        for r0 in range(S - STRIP, -1, -STRIP):
            xw = _with_halo(x_ref, r0, ls)
            acc, shifted = _conv_strip(xw, w_ref, b_ref, ls, width)
            dacc = dacc_of(r0, ls, acc, _sigmoid(acc))
            dx, dws, db = _conv_strip_back(dacc, after, xw[HALO:], shifted, w_ref, ls, width)
            dx_store(r0, ls, dx)
            after = dacc[:HALO]
            part = dws + [db]
            tot = part if tot is None else [p + q for p, q in zip(tot, part)]
        for k in range(width):
            dw_ref[pl.ds(k, 1), ls] = tot[k]
        db_ref[:, ls] = tot[width]


def _conv_silu_fwd(xin, col0, C, w, b, *, name, tc=512):
    S = xin.shape[0]
    width = w.shape[0]
    off = col0 // tc

    def body(x_ref, w_ref, b_ref, o_ref):
        for r0, ls in _strips(S, tc):
            acc, _ = _conv_strip(_with_halo(x_ref, r0, ls), w_ref, b_ref, ls, width)
            o_ref[r0:r0 + STRIP, ls] = acc * _sigmoid(acc)

    return pl.pallas_call(
        body, name=name, grid=(C // tc,),
        in_specs=[pl.BlockSpec((S, tc), lambda j: (0, j + off)), pl.BlockSpec((width, tc), lambda j: (0, j)),
                  pl.BlockSpec((1, tc), lambda j: (0, j))],
        out_specs=pl.BlockSpec((S, tc), lambda j: (0, j)), out_shape=jax.ShapeDtypeStruct((S, C), F32),
        compiler_params=_cp(1),
    )(xin, w, b)


def _conv_silu_bwd(xin, col0, C, w, b, douts, *, name, tc=256):
    S = xin.shape[0]
    width = w.shape[0]
    off = col0 // tc
    nd = len(douts)
    ranges = [(o // tc, (o + d.shape[1]) // tc) for d, o in douts]

    def body(*refs):
        x_ref, w_ref, b_ref = refs[0], refs[1], refs[2]
        d_refs = refs[3:3 + nd]
        dx_ref, dw_ref, db_ref = refs[3 + nd], refs[4 + nd], refs[5 + nd]
        j = pl.program_id(0)

        def dacc_of(r0, ls, acc, sg):
            dout = jnp.zeros((STRIP, 128), F32)
            for q in range(nd):
                lo, hi = ranges[q]
                dout = dout + jnp.where((j >= lo) & (j < hi), d_refs[q][r0:r0 + STRIP, ls], 0.0)
            return dout * (sg * (1.0 + acc * (1.0 - sg)))

        def dx_store(r0, ls, dx):
            dx_ref[r0:r0 + STRIP, ls] = dx.astype(BF16)

        _conv_back_block(S, tc, width, w_ref, b_ref, x_ref, dacc_of, dx_store, dw_ref, db_ref)

    d_specs = [pl.BlockSpec((S, tc), (lambda j, lo=lo, hi=hi: (0, jnp.clip(j - lo, 0, hi - lo - 1)))) for lo, hi in ranges]
    return pl.pallas_call(
        body, name=name, grid=(C // tc,),
        in_specs=[pl.BlockSpec((S, tc), lambda j: (0, j + off)), pl.BlockSpec((width, tc), lambda j: (0, j)),
                  pl.BlockSpec((1, tc), lambda j: (0, j))] + d_specs,
        out_specs=[pl.BlockSpec((S, tc), lambda j: (0, j)), pl.BlockSpec((width, tc), lambda j: (0, j)),
                   pl.BlockSpec((1, tc), lambda j: (0, j))],
        out_shape=[jax.ShapeDtypeStruct((S, C), BF16), jax.ShapeDtypeStruct((width, C), F32),
                   jax.ShapeDtypeStruct((1, C), F32)],
        compiler_params=_cp(1),
    )(xin, w, b, *[d for d, _ in douts])


def _ffn_act_fwd(u, w, b, *, name, tc=256):
    S, F2 = u.shape
    Fd = F2 // 2
    width = w.shape[0]
    nb = Fd // tc

    def body(g_ref, v_ref, w_ref, b_ref, o_ref):
        for r0, ls in _strips(S, tc):
            acc, _ = _conv_strip(_with_halo(g_ref, r0, ls), w_ref, b_ref, ls, width)
            o_ref[r0:r0 + STRIP, ls] = (acc * _sigmoid(acc) * v_ref[r0:r0 + STRIP, ls]).astype(BF16)

    return pl.pallas_call(
        body, name=name, grid=(nb,),
        in_specs=[pl.BlockSpec((S, tc), lambda j: (0, j)), pl.BlockSpec((S, tc), lambda j: (0, j + nb)),
                  pl.BlockSpec((width, tc), lambda j: (0, j)), pl.BlockSpec((1, tc), lambda j: (0, j))],
        out_specs=pl.BlockSpec((S, tc), lambda j: (0, j)), out_shape=jax.ShapeDtypeStruct((S, Fd), BF16),
        compiler_params=_cp(1),
    )(u, u, w, b)


def _ffn_act_bwd(u, w, b, da, *, name, tc=256):
    S, F2 = u.shape
    Fd = F2 // 2
    width = w.shape[0]
    nb = Fd // tc

    def body(g_ref, v_ref, w_ref, b_ref, da_ref, du_ref, dw_ref, db_ref, a_ref):
        def dacc_of(r0, ls, acc, sg):
            rs = slice(r0, r0 + STRIP)
            dav, val, silu = da_ref[rs, ls], v_ref[rs, ls], acc * sg
            a_ref[rs, ls] = (silu * val).astype(BF16)
            du_ref[1, rs, ls] = (dav * silu).astype(BF16)
            return dav * val * (sg * (1.0 + acc * (1.0 - sg)))

        def dx_store(r0, ls, dx):
            du_ref[0, r0:r0 + STRIP, ls] = dx.astype(BF16)

        _conv_back_block(S, tc, width, w_ref, b_ref, g_ref, dacc_of, dx_store, dw_ref, db_ref)

    blk = pl.BlockSpec((S, tc), lambda j: (0, j))
    return pl.pallas_call(
        body, name=name, grid=(nb,),
        in_specs=[blk, pl.BlockSpec((S, tc), lambda j: (0, j + nb)), pl.BlockSpec((width, tc), lambda j: (0, j)),
                  pl.BlockSpec((1, tc), lambda j: (0, j)), blk],
        out_specs=[pl.BlockSpec((2, S, tc), lambda j: (0, 0, j)), pl.BlockSpec((width, tc), lambda j: (0, j)),
                   pl.BlockSpec((1, tc), lambda j: (0, j)), blk],
        out_shape=[jax.ShapeDtypeStruct((2, S, Fd), BF16),
                   jax.ShapeDtypeStruct((width, Fd), F32), jax.ShapeDtypeStruct((1, Fd), F32),
                   jax.ShapeDtypeStruct((S, Fd), BF16)],
        compiler_params=_cp(1),
    )(u, u, w, b, da)


def _ssd_prep(dtr, dt_bias, a_log, *, name="ssd_prep"):
    S = dtr.shape[0]

    def body(d_ref, b_ref, al_ref, dt_ref, ac_ref, sg_ref, act_ref):
        lane = _iota((CHUNK, 128), 1)
        valid = lane < SSM_HEADS
        z = d_ref[...] + b_ref[...]
        dt = jnp.where(valid, jnp.maximum(z, 0.0) + jnp.log(1.0 + jnp.exp(-jnp.abs(z))), 0.0)
        a = dt * (-jnp.exp(al_ref[...]))
        row = _iota((CHUNK, 128), 0)
        k = 1
        while k < CHUNK:
            a = a + jnp.where(row >= k, pltpu.roll(a, k, axis=0), 0.0)
            k *= 2
        sg = jnp.where(valid, _sigmoid(z), 0.0)
        for arr, ref in ((dt, dt_ref), (a, ac_ref), (sg, sg_ref)):
            for g in range(SSM_GROUPS):
                ref[g] = jnp.where(lane < 4, arr if g == 0 else pltpu.roll(arr, 128 - 4 * g, axis=1), 0.0)
        act_ref[...] = a.T[:SSM_HEADS, :]

    blk = pl.BlockSpec((CHUNK, 128), lambda i: (i, 0))
    vec = pl.BlockSpec((1, 128), lambda i: (0, 0))
    grp = pl.BlockSpec((SSM_GROUPS, CHUNK, 128), lambda i: (0, i, 0))
    return pl.pallas_call(
        body, name=name, grid=(S // CHUNK,), in_specs=[blk, vec, vec],
        out_specs=[grp, grp, grp, pl.BlockSpec((SSM_HEADS, CHUNK), lambda i: (0, i))],
        out_shape=[jax.ShapeDtypeStruct((SSM_GROUPS, S, 128), F32)] * 3 + [jax.ShapeDtypeStruct((SSM_HEADS, S), F32)],
        compiler_params=_cp(1),
    )(dtr, dt_bias, a_log)


SSD_GPS = 4


def _expand4(v, lanes):
    out = jnp.broadcast_to(v[:, 3:4], lanes.shape)
    for hh in (2, 1, 0):
        out = jnp.where(lanes < 64 * (hh + 1), v[:, hh:hh + 1], out)
    return out


def _ssd_fwd(xbc, dt_g, ac_g, ac_t, *, name="ssd_fwd"):
    S = xbc.shape[0]
    nc = S // CHUNK
    Lc = CHUNK

    def body(x_ref, b_ref, c_ref, dt_ref, ac_ref, act_ref, y_ref, st_out_ref, st_ref):
        g2 = pl.program_id(0)
        c = pl.program_id(1)

        @pl.when(c == 0)
        def _():
            st_ref[...] = jnp.zeros_like(st_ref)

        causal = _iota((Lc, Lc), 0) >= _iota((Lc, Lc), 1)
        lane256 = _iota((Lc, 256), 1)
        lane128 = _iota((Lc, 128), 1)
        row128 = _iota((128, 128), 0)
        for gg in range(SSD_GPS):
            g = SSD_GPS * g2 + gg
            bv = b_ref[:, 128 * gg:128 * (gg + 1)]
            cbf = c_ref[:, 128 * gg:128 * (gg + 1)].astype(BF16)
            cb = _dot(cbf, bv.astype(BF16), 1, 1)
            dtg, acg = dt_ref[gg], ac_ref[gg]
            ac_last = ac_ref[gg, pl.ds(Lc - 1, 1), :]
            dt4 = _expand4(dtg, lane256)
            ac4 = _expand4(acg, lane256)
            e4 = jnp.exp(ac4)
            xdb = (x_ref[:, 256 * gg:256 * (gg + 1)] * dt4).astype(BF16)
            st_out_ref[gg] = st_ref[gg]
            for p in range(2):
                xd_p = xdb[:, 128 * p:128 * (p + 1)]
                st_p = st_ref[gg, p]
                ys, sn, cds = [], [], []
                for q in range(2):
                    hh = 2 * p + q
                    a_col = acg[:, hh:hh + 1]
                    a_row = act_ref[pl.ds(4 * g + hh, 1), :]
                    dec = jnp.exp(jnp.where(causal, a_col - a_row, NEG))
                    w = (cb * dec).astype(BF16)
                    ys.append(_dot(w, xd_p))
                    al = ac_last[:, hh:hh + 1]
                    dte = jnp.exp(al - a_col)
                    sn.append(_dot(xd_p, (bv * dte).astype(BF16), 0, 0))
                    cds.append(jnp.exp(al))
                y_diag = jnp.where(lane128 < 64, ys[0], ys[1])
                y_off = _dot(cbf, st_p.astype(BF16), 1, 1) * e4[:, 128 * p:128 * (p + 1)]
                y_ref[:, 256 * gg + 128 * p:256 * gg + 128 * (p + 1)] = y_diag + y_off
                st_ref[gg, p] = jnp.where(row128 < 64, st_p * cds[0] + sn[0], st_p * cds[1] + sn[1])

    G = SSD_GPS
    per_g = lambda g, c: (g, c, 0)
    return pl.pallas_call(
        body, name=name, grid=(SSM_GROUPS // G, nc),
        in_specs=[pl.BlockSpec((Lc, 256 * G), lambda g, c: (c, g)),
                  pl.BlockSpec((Lc, 128 * G), lambda g, c: (c, 16 // G + g)),
                  pl.BlockSpec((Lc, 128 * G), lambda g, c: (c, 24 // G + g)),
                  pl.BlockSpec((G, Lc, 128), per_g), pl.BlockSpec((G, Lc, 128), per_g),
                  pl.BlockSpec((SSM_HEADS, Lc), lambda g, c: (0, c))],
        out_specs=[pl.BlockSpec((Lc, 256 * G), lambda g, c: (c, g)),
                   pl.BlockSpec((G, None, 2, 128, 128), lambda g, c: (g, c, 0, 0, 0))],
        out_shape=[jax.ShapeDtypeStruct((S, 2048), F32), jax.ShapeDtypeStruct((SSM_GROUPS, nc, 2, 128, 128), F32)],
        scratch_shapes=[pltpu.VMEM((G, 2, 128, 128), F32)], compiler_params=_cp(2),
    )(xbc, xbc, xbc, dt_g, ac_g, ac_t)


def _ssd_bwd(xbc, dt_g, ac_g, ac_t, states, dy, dexp, *, name="ssd_bwd", dep=None):
    S = xbc.shape[0]
    nc = S // CHUNK
    Lc = CHUNK

    def body(x_ref, b_ref, c_ref, dt_ref, ac_ref, act_ref, st_ref, dy_ref, d_ref, *rest):
        dx_ref, db_ref, dc_ref, dh_ref, ds_ref = rest[-5:]
        g2 = pl.program_id(0)
        cc = pl.program_id(1)

        @pl.when(cc == 0)
        def _():
            ds_ref[...] = jnp.zeros_like(ds_ref)

        causal = _iota((Lc, Lc), 0) >= _iota((Lc, Lc), 1)
        lane256 = _iota((Lc, 256), 1)
        lane128 = _iota((Lc, 128), 1)
        row128 = _iota((128, 128), 0)
        ind_rows = _iota((256, 128), 0) >> 6
        ind_cols = _iota((256, 128), 1)
        ind_a = (ind_rows == ind_cols).astype(BF16)
        ind_b = (ind_rows + 4 == ind_cols).astype(BF16)
        for gg in range(SSD_GPS):
            g = SSD_GPS * g2 + gg
            bv = b_ref[:, 128 * gg:128 * (gg + 1)]
            cv = c_ref[:, 128 * gg:128 * (gg + 1)]
            bbf, cbf = bv.astype(BF16), cv.astype(BF16)
            cb = _dot(cbf, bbf, 1, 1)
            dtg, acg = dt_ref[gg], ac_ref[gg]
            ac_last = ac_ref[gg, pl.ds(Lc - 1, 1), :]
            dt4 = _expand4(dtg, lane256)
            ac4 = _expand4(acg, lane256)
            acl4 = _expand4(ac_last, _iota((1, 256), 1))
            e4 = jnp.exp(ac4)
            dte4 = jnp.exp(acl4 - ac4)
            xv = x_ref[:, 256 * gg:256 * (gg + 1)]
            xd = xv * dt4
            xdb = xd.astype(BF16)
            dyv = dy_ref[:, 256 * gg:256 * (gg + 1)]
            dcb = jnp.zeros((Lc, Lc), F32)
            dc_acc = jnp.zeros((Lc, 128), F32)
            db_acc = jnp.zeros((Lc, 128), F32)
            u_parts, dxd_parts, ends = [], [], []
            for p in range(2):
                sl = slice(128 * p, 128 * (p + 1))
                xd_p, xdb_p, dy_p = xd[:, sl], xdb[:, sl], dyv[:, sl]
                dyb_p = dy_p.astype(BF16)
                e_p, dte_p = e4[:, sl], dte4[:, sl]
                sp = st_ref[gg, p]
                spb = sp.astype(BF16)
                dsn = ds_ref[gg, p]
                dsnb = dsn.astype(BF16)
                yds, dxds, cds = [], [], []
                for q in range(2):
                    hh = 2 * p + q
                    a_col = acg[:, hh:hh + 1]
                    a_row = act_ref[pl.ds(4 * g + hh, 1), :]
                    dec = jnp.exp(jnp.where(causal, a_col - a_row, NEG))
                    w = (cb * dec).astype(BF16)
                    head = (lane128 < 64) if q == 0 else (lane128 >= 64)
                    dym = jnp.where(head, dyb_p, jnp.zeros_like(dyb_p))
                    dw = _dot(dym, xdb_p, 1, 1)
                    dcb = dcb + dw * dec
                    yds.append(_dot(w, xdb_p))
                    dxds.append(_dot(w, dyb_p, 0, 0))
                    cds.append(jnp.exp(ac_last[:, hh:hh + 1]))
                y_diag = jnp.where(lane128 < 64, yds[0], yds[1])
                dxd_diag = jnp.where(lane128 < 64, dxds[0], dxds[1])
                y_off = _dot(cbf, spb, 1, 1) * e_p
                dgp = dy_p * e_p
                dgb = dgp.astype(BF16)
                dc_acc = dc_acc + _dot(dgb, spb)
                dsp = _dot(dgb, cbf, 0, 0)
                cd_col = jnp.where(row128[:, 0:1] < 64, cds[0], cds[1])
                qm = _dot(bbf, dsnb, 1, 1)
                dxd_state = dte_p * qm
                db_acc = db_acc + _dot((xd_p * dte_p).astype(BF16), dsnb)
                t_p = xd_p * dxd_state
                prod = dsn * sp
                e0 = jnp.sum(jnp.sum(jnp.where(row128 < 64, prod, 0.0), axis=1, keepdims=True), axis=0, keepdims=True)
                e1 = jnp.sum(jnp.sum(jnp.where(row128 >= 64, prod, 0.0), axis=1, keepdims=True), axis=0, keepdims=True)
                tcol = jnp.sum(t_p, axis=0, keepdims=True)
                lane1 = _iota((1, 128), 1)
                t0 = jnp.sum(jnp.where(lane1 < 64, tcol, 0.0), axis=1, keepdims=True)
                t1 = jnp.sum(jnp.where(lane1 >= 64, tcol, 0.0), axis=1, keepdims=True)
                ends.append(e0 * cds[0] + t0)
                ends.append(e1 * cds[1] + t1)
                ds_ref[gg, p] = dsn * cd_col + dsp
                u_parts.append(dyb_p.astype(F32) * y_diag - xdb_p.astype(F32) * dxd_diag + dy_p * y_off - t_p)
                dxd_parts.append(dxd_diag + dxd_state)
            dxd = jnp.concatenate(dxd_parts, axis=1)
            u_all = jnp.concatenate(u_parts, axis=1)
            dx_ref[:, 256 * gg:256 * (gg + 1)] = dxd * dt4 + dyv * d_ref[:, 256 * gg:256 * (gg + 1)]
            dcbb = dcb.astype(BF16)
            dc_ref[:, 128 * gg:128 * (gg + 1)] = dc_acc + _dot(dcbb, bbf)
            db_ref[:, 128 * gg:128 * (gg + 1)] = db_acc + _dot(dcbb, cbf, 0, 0)
            lane = _iota((Lc, 128), 1)
            endv = jnp.zeros((Lc, 128), F32)
            for hh in range(4):
                endv = jnp.where(lane == 8 + hh, ends[hh], endv)
            dh_ref[gg] = _dot3(dxd * xv, ind_a) + _dot3(u_all, ind_b) + endv

    G = SSD_GPS
    rev = lambda c: nc - 1 - c
    per_g = lambda g, c: (g, rev(c), 0)
    return pl.pallas_call(
        body, name=name, grid=(SSM_GROUPS // G, nc),
        in_specs=[pl.BlockSpec((Lc, 256 * G), lambda g, c: (rev(c), g)),
                  pl.BlockSpec((Lc, 128 * G), lambda g, c: (rev(c), 16 // G + g)),
                  pl.BlockSpec((Lc, 128 * G), lambda g, c: (rev(c), 24 // G + g)),
                  pl.BlockSpec((G, Lc, 128), per_g), pl.BlockSpec((G, Lc, 128), per_g),
                  pl.BlockSpec((SSM_HEADS, Lc), lambda g, c: (0, rev(c))),
                  pl.BlockSpec((G, None, 2, 128, 128), lambda g, c: (g, rev(c), 0, 0, 0)),
                  pl.BlockSpec((Lc, 256 * G), lambda g, c: (rev(c), g)),
                  pl.BlockSpec((1, 256 * G), lambda g, c: (0, g))] + ([] if dep is None else [pl.BlockSpec(memory_space=pl.ANY)]),
        out_specs=[pl.BlockSpec((Lc, 256 * G), lambda g, c: (rev(c), g)),
                   pl.BlockSpec((Lc, 128 * G), lambda g, c: (rev(c), g)),
                   pl.BlockSpec((Lc, 128 * G), lambda g, c: (rev(c), g)),
                   pl.BlockSpec((G, Lc, 128), per_g)],
        out_shape=[jax.ShapeDtypeStruct((S, 2048), F32), jax.ShapeDtypeStruct((S, 1024), F32),
                   jax.ShapeDtypeStruct((S, 1024), F32), jax.ShapeDtypeStruct((SSM_GROUPS, S, 128), F32)],
        scratch_shapes=[pltpu.VMEM((G, 2, 128, 128), F32)], compiler_params=_cp(2),
    )(xbc, xbc, xbc, dt_g, ac_g, ac_t, states, dy, dexp, *([] if dep is None else [dep]))


def _ssd_post(dhead, dt_g, sg_g, alog_g, *, name="ssd_post"):
    S = dhead.shape[1]
    nc = S // CHUNK
    Lc = CHUNK

    def body(dh_ref, dt_ref, sg_ref, al_ref, o_ref, s_ref):
        @pl.when(pl.program_id(0) == 0)
        def _():
            s_ref[...] = jnp.zeros_like(s_ref)

        lane = _iota((Lc, 128), 1)
        row = _iota((Lc, 128), 0)
        row8 = _iota((8, 128), 0)
        out = jnp.zeros((Lc, 128), F32)
        for g in range(SSM_GROUPS):
            dh = dh_ref[g]
            a_neg = -jnp.exp(al_ref[g])
            dac = jnp.where(lane < 4, pltpu.roll(dh, 124, axis=1), 0.0)
            end = jnp.where(lane < 4, pltpu.roll(dh, 120, axis=1), 0.0)
            k = 1
            while k < Lc:
                dac = dac + jnp.where(row < Lc - k, pltpu.roll(dac, Lc - k, axis=0), 0.0)
                k *= 2
            da = dac + end
            ddt = jnp.where(lane < 4, da * a_neg + dh, 0.0)
            ddtr = ddt * sg_ref[g]
            out = out + (ddtr if g == 0 else pltpu.roll(ddtr, 4 * g, axis=1))
            dal = jnp.sum(da * dt_ref[g], axis=0, keepdims=True) * a_neg
            dbias = jnp.sum(ddtr, axis=0, keepdims=True)
            part = jnp.where(row8 == 0, dal, jnp.where(row8 == 1, dbias, 0.0))
            s_ref[g] += part
        o_ref[...] = out.astype(BF16)

    grp = pl.BlockSpec((SSM_GROUPS, Lc, 128), lambda c: (0, c, 0))
    whole = lambda r: pl.BlockSpec((SSM_GROUPS, r, 128), lambda c: (0, 0, 0))
    return pl.pallas_call(
        body, name=name, grid=(nc,), in_specs=[grp, grp, grp, whole(1)],
        out_specs=[pl.BlockSpec((Lc, 128), lambda c: (c, 0)), whole(8)],
        out_shape=[jax.ShapeDtypeStruct((S, 128), BF16), jax.ShapeDtypeStruct((SSM_GROUPS, 8, 128), F32)],
        compiler_params=_cp(1),
    )(dhead, dt_g, sg_g, alog_g)


def _gate_fwd(y, xbc, zx, dexp, gn, *, name="gate_fwd", tr=256):
    S = y.shape[0]
    W = 2048
    gw = W // SSM_GROUPS

    def body(y_ref, x_ref, z_ref, d_ref, g_ref, o_ref):
        z = z_ref[...]
        u = (y_ref[...] + x_ref[...] * d_ref[...]) * (z * _sigmoid(z))
        gv = g_ref[...]
        for q in range(SSM_GROUPS):
            sl = slice(gw * q, gw * (q + 1))
            uq = u[:, sl]
            r = lax.rsqrt(jnp.mean(uq * uq, axis=-1, keepdims=True) + EPS)
            o_ref[:, sl] = (uq * r * gv[:, sl]).astype(BF16)

    row = pl.BlockSpec((tr, W), lambda i: (i, 0))
    vec = pl.BlockSpec((1, W), lambda i: (0, 0))
    return pl.pallas_call(
        body, name=name, grid=(S // tr,), in_specs=[row, row, row, vec, vec], out_specs=row,
        out_shape=jax.ShapeDtypeStruct((S, W), BF16), compiler_params=_cp(1),
    )(y, xbc, zx, dexp, gn)


def _gate_bwd(y, xbc, zx, dexp, gn, dout, *, name="gate_bwd", tr=256):
    S = y.shape[0]
    W = 2048
    gw = W // SSM_GROUPS
    steps = S // tr

    def body(y_ref, x_ref, z_ref, d_ref, g_ref, do_ref, dy_ref, dz_ref, dg_ref, dd_ref, acc_ref):
        i = pl.program_id(0)

        @pl.when(i == 0)
        def _():
            acc_ref[...] = jnp.zeros_like(acc_ref)

        z = z_ref[...]
        sg = _sigmoid(z)
        sz = z * sg
        xs = x_ref[...]
        yt = y_ref[...] + xs * d_ref[...]
        u = yt * sz
        gv = g_ref[...]
        do = do_ref[...]
        dgs = []
        for q in range(SSM_GROUPS):
            sl = slice(gw * q, gw * (q + 1))
            uq = u[:, sl]
            r = lax.rsqrt(jnp.mean(uq * uq, axis=-1, keepdims=True) + EPS)
            uh = uq * r
            dq = do[:, sl]
            duh = dq * gv[:, sl]
            duq = r * (duh - uh * jnp.mean(duh * uh, axis=-1, keepdims=True))
            dgs.append(jnp.sum(dq * uh, axis=0, keepdims=True))
            dyt = duq * sz[:, sl]
            dy_ref[:, sl] = dyt
            dz_ref[:, sl] = (duq * yt[:, sl] * (sg[:, sl] * (1.0 + z[:, sl] * (1.0 - sg[:, sl])))).astype(BF16)
            acc_ref[:, sl] += jnp.sum(dyt * xs[:, sl], axis=0, keepdims=True)
        dg = jnp.concatenate(dgs, axis=1)

        @pl.when(i == 0)
        def _():
            dg_ref[...] = dg

        @pl.when(i > 0)
        def _():
            dg_ref[...] += dg

        @pl.when(i == steps - 1)
        def _():
            ind = ((_iota((W, 128), 0) >> 6) == _iota((W, 128), 1)).astype(BF16)
            dd_ref[...] = _dot3(jnp.broadcast_to(acc_ref[...], (8, W)), ind)[0:1, :]

    row = pl.BlockSpec((tr, W), lambda i: (i, 0))
    vec = pl.BlockSpec((1, W), lambda i: (0, 0))
    return pl.pallas_call(
        body, name=name, grid=(steps,), in_specs=[row, row, row, vec, vec, row],
        out_specs=[row, row, vec, pl.BlockSpec((1, 128), lambda i: (0, 0))],
        out_shape=[jax.ShapeDtypeStruct((S, W), F32), jax.ShapeDtypeStruct((S, W), BF16),
                   jax.ShapeDtypeStruct((1, W), F32), jax.ShapeDtypeStruct((1, 128), F32)],
        scratch_shapes=[pltpu.VMEM((1, W), F32)], compiler_params=_cp(1),
    )(y, xbc, zx, dexp, gn, dout)


def _rope_cs(posf, *, name="rope_tables", tr=256):
    S = posf.shape[0]

    def body(p_ref, c_ref, s_ref):
        j = (_iota((tr, 128), 1) & 31).astype(F32)
        ang = p_ref[...] * jnp.exp(j * (-math.log(ROPE_THETA) / 32.0))
        c_ref[...] = jnp.cos(ang)
        s_ref[...] = jnp.sin(ang)

    blk = pl.BlockSpec((tr, 128), lambda i: (i, 0))
    return pl.pallas_call(
        body, name=name, grid=(S // tr,), in_specs=[pl.BlockSpec((tr, 1), lambda i: (i, 0))], out_specs=[blk, blk],
        out_shape=[jax.ShapeDtypeStruct((S, 128), F32)] * 2, compiler_params=_cp(1),
    )(posf)


def _rope_tables(c_ref, s_ref, shape):
    reps = shape[1] // 128
    return jnp.tile(c_ref[...], (1, reps)), jnp.tile(s_ref[...], (1, reps)), (_iota(shape, 1) & 63) < 32


def _hn_inds(W):
    ind = ((_iota((W, 128), 0) >> 6) == _iota((W, 128), 1)).astype(BF16)
    ind_t = ((_iota((128, W), 1) >> 6) == _iota((128, W), 0)).astype(BF16)
    return ind, ind_t


def _hnrope_fwd(xin, col0, W, gain_w, rope, *, name, tr=256):
    S = xin.shape[0]
    off = col0 // W
    nh = W // HEAD

    def body(x_ref, g_ref, c_ref, s_ref, o_ref):
        x = x_ref[...]
        ind, ind_t = _hn_inds(W)
        r = lax.rsqrt(_dot3(x * x, ind) * (1.0 / HEAD) + EPS)
        xn = x * _dot3(r, ind_t) * g_ref[...]
        cs, sn, half = _rope_tables(c_ref, s_ref, (tr, W))
        rot = jnp.where(half, -pltpu.roll(xn, W - 32, axis=1), pltpu.roll(xn, 32, axis=1))
        out = (xn * cs + rot * sn).astype(BF16)
        for h in range(nh):
            o_ref[h] = out[:, HEAD * h:HEAD * (h + 1)]

    tab = pl.BlockSpec((tr, 128), lambda i: (i, 0))
    return pl.pallas_call(
        body, name=name, grid=(S // tr,),
        in_specs=[pl.BlockSpec((tr, W), lambda i: (i, off)), pl.BlockSpec((1, W), lambda i: (0, 0)), tab, tab],
        out_specs=pl.BlockSpec((nh, tr, HEAD), lambda i: (0, i, 0)), out_shape=jax.ShapeDtypeStruct((nh, S, HEAD), BF16),
        compiler_params=_cp(1),
    )(xin, gain_w, *rope)


def _hnrope_bwd(xin, col0, W, gain_w, rope, dout, *, name, tr=256):
    S = xin.shape[0]
    off = col0 // W
    steps = S // tr
    nh = W // HEAD

    def body(x_ref, g_ref, c_ref, s_ref, do_ref, dx_ref, cs_ref, dg_ref, acc_ref):
        i = pl.program_id(0)
        x = x_ref[...]
        ind, ind_t = _hn_inds(W)
        r = lax.rsqrt(_dot3(x * x, ind) * (1.0 / HEAD) + EPS)
        rw = _dot3(r, ind_t)
        xh = x * rw
        cs, sn, half = _rope_tables(c_ref, s_ref, (tr, W))
        do = jnp.concatenate([do_ref[h] for h in range(nh)], axis=1).astype(F32)
        gs = do * sn
        g1 = do * cs + jnp.where(half, pltpu.roll(gs, W - 32, axis=1), -pltpu.roll(gs, 32, axis=1))
        dxh = g1 * g_ref[...]
        t = _dot3(dxh * xh, ind) * (1.0 / HEAD)
        dx = rw * (dxh - xh * _dot3(t, ind_t))
        dx_ref[...] = dx.astype(BF16)
        cpart = jnp.sum(dx, axis=0, keepdims=True)
        gpart = jnp.sum(g1 * xh, axis=0, keepdims=True)

        @pl.when(i == 0)
        def _():
            cs_ref[...] = cpart
            acc_ref[...] = gpart

        @pl.when(i > 0)
        def _():
            cs_ref[...] += cpart
            acc_ref[...] += gpart

        @pl.when(i == steps - 1)
        def _():
            fold = ((_iota((W, 128), 0) & 63) == _iota((W, 128), 1)).astype(BF16)
            dg_ref[...] = _dot3(jnp.broadcast_to(acc_ref[...], (8, W)), fold)[0:1, :]

    tab = pl.BlockSpec((tr, 128), lambda i: (i, 0))
    return pl.pallas_call(
        body, name=name, grid=(steps,),
        in_specs=[pl.BlockSpec((tr, W), lambda i: (i, off)), pl.BlockSpec((1, W), lambda i: (0, 0)), tab, tab,
                  pl.BlockSpec((nh, tr, HEAD), lambda i: (0, i, 0))],
        out_specs=[pl.BlockSpec((tr, W), lambda i: (i, 0)), pl.BlockSpec((1, W), lambda i: (0, 0)),
                   pl.BlockSpec((1, 128), lambda i: (0, 0))],
        out_shape=[jax.ShapeDtypeStruct((S, W), BF16), jax.ShapeDtypeStruct((1, W), F32),
                   jax.ShapeDtypeStruct((1, 128), F32)],
        scratch_shapes=[pltpu.VMEM((1, W), F32)], compiler_params=_cp(1),
    )(xin, gain_w, *rope, dout)


def _attn_band():
    qi = jnp.arange(ATT_G * WINDOW)[:, None] % WINDOW
    ki = jnp.arange(2 * WINDOW)[None, :]
    rel = qi + WINDOW - ki
    ok = (rel >= 0) & (rel < WINDOW)
    return jnp.stack([jnp.where(ok & (ki >= WINDOW), 0.0, NEG), jnp.where(ok, 0.0, NEG)]).astype(F32)


def _attn_probs(q, kb, sink_ref, band_ref, h, i):
    s = _dot(q, kb, 1, 1) * (HEAD ** -0.5) + band_ref[jnp.minimum(i, 1)]
    r1 = _iota((4 * WINDOW, 1), 0)
    sink = jnp.where(r1 < WINDOW, sink_ref[4 * h], jnp.where(r1 < 2 * WINDOW, sink_ref[4 * h + 1],
                     jnp.where(r1 < 3 * WINDOW, sink_ref[4 * h + 2], sink_ref[4 * h + 3])))
    m = jnp.maximum(jnp.max(s, axis=1, keepdims=True), sink)
    p = jnp.exp(s - m)
    ps = jnp.exp(sink - m)
    inv = 1.0 / (jnp.sum(p, axis=1, keepdims=True) + ps)
    return p * inv, ps * inv


ATT_HPS = 4
_BAND = pl.BlockSpec((2, ATT_G * WINDOW, 2 * WINDOW), lambda h, i: (0, 0, 0))


def _attn_specs(S):
    qspec = pl.BlockSpec((ATT_HPS, ATT_G, WINDOW, HEAD), lambda h, i: (h, 0, i, 0))
    cur = pl.BlockSpec((ATT_HPS, WINDOW, HEAD), lambda h, i: (h, i, 0))
    prev = pl.BlockSpec((ATT_HPS, WINDOW, HEAD), lambda h, i: (h, jnp.maximum(i - 1, 0), 0))
    tok = pl.BlockSpec((WINDOW, ATT_HPS * ATT_G * HEAD), lambda h, i: (i, h))
    return qspec, cur, prev, tok


def _attn_fwd(qh, kh, vh, sinks, *, name="attn_fwd"):
    S = kh.shape[1]
    nb = S // WINDOW

    def body(s_ref, band_ref, q_ref, kc_ref, kp_ref, vc_ref, vp_ref, o_ref):
        h2, i = pl.program_id(0), pl.program_id(1)
        outs = []
        for hh in range(ATT_HPS):
            q = q_ref[hh].reshape(ATT_G * WINDOW, HEAD)
            kb = jnp.concatenate([kp_ref[hh], kc_ref[hh]], axis=0)
            vb = jnp.concatenate([vp_ref[hh], vc_ref[hh]], axis=0)
            probs, _ = _attn_probs(q, kb, s_ref, band_ref, ATT_HPS * h2 + hh, i)
            o = _dot(probs.astype(BF16), vb).astype(BF16)
            outs += [o[WINDOW * g:WINDOW * (g + 1)] for g in range(ATT_G)]
        o_ref[...] = jnp.concatenate(outs, axis=1)

    qspec, cur, prev, tok = _attn_specs(S)
    return pl.pallas_call(
        body, name=name, grid=(ATT_KV // ATT_HPS, nb),
        in_specs=[pl.BlockSpec(memory_space=pltpu.SMEM), _BAND, qspec, cur, prev, cur, prev], out_specs=tok,
        out_shape=jax.ShapeDtypeStruct((S, ATT_KV * ATT_G * HEAD), BF16), compiler_params=_cp(2),
    )(sinks, _attn_band(), qh, kh, kh, vh, vh)


def _attn_bwd(qh, kh, vh, sinks, doh, *, name="attn_bwd"):
    S = kh.shape[1]
    nb = S // WINDOW

    def body(s_ref, band_ref, q_ref, kc_ref, kp_ref, vc_ref, vp_ref, do_ref, dq_ref, dk_ref, dv_ref, dsk_ref):
        h2, i = pl.program_id(0), pl.program_id(1)

        @pl.when(i == 0)
        def _():
            dk_ref[...] = jnp.zeros_like(dk_ref)
            dv_ref[...] = jnp.zeros_like(dv_ref)
            dsk_ref[...] = jnp.zeros_like(dsk_ref)

        dov = do_ref[...]
        cur = pl.multiple_of(i * WINDOW, WINDOW)
        lane = _iota((8, 128), 1)
        row = _iota((8, 128), 0)
        scale = HEAD ** -0.5
        for hh in range(ATT_HPS):
            q = q_ref[hh].reshape(ATT_G * WINDOW, HEAD)
            do = jnp.concatenate([dov[:, HEAD * (ATT_G * hh + g):HEAD * (ATT_G * hh + g + 1)] for g in range(ATT_G)], axis=0)
            kb = jnp.concatenate([kp_ref[hh], kc_ref[hh]], axis=0)
            vb = jnp.concatenate([vp_ref[hh], vc_ref[hh]], axis=0)
            probs, psink = _attn_probs(q, kb, s_ref, band_ref, ATT_HPS * h2 + hh, i)
            dp = _dot(do, vb, 1, 1)
            delta = jnp.sum(probs * dp, axis=1, keepdims=True)
            ds = (probs * (dp - delta)).astype(BF16)
            dq_ref[hh] = (_dot(ds, kb) * scale).reshape(ATT_G, WINDOW, HEAD)
            dkb = _dot(ds, q, 0, 0) * scale
            dvb = _dot(probs.astype(BF16), do, 0, 0)
            dk_ref[hh, pl.ds(cur, WINDOW), :] += dkb[WINDOW:, :]
            dv_ref[hh, pl.ds(cur, WINDOW), :] += dvb[WINDOW:, :]
            prv = pl.multiple_of(jnp.maximum(i - 1, 0) * WINDOW, WINDOW)
            dk_ref[hh, pl.ds(prv, WINDOW), :] += dkb[:WINDOW, :]
            dv_ref[hh, pl.ds(prv, WINDOW), :] += dvb[:WINDOW, :]

            dsr = -psink * delta
            upd = jnp.zeros((8, 128), F32)
            for gq in range(ATT_G):
                v = jnp.sum(dsr[gq * WINDOW:(gq + 1) * WINDOW, :], axis=0, keepdims=True)
                upd = jnp.where((lane == gq) & (row == 0), v, upd)
            dsk_ref[hh] += upd

    qspec, cur, prev, tok = _attn_specs(S)
    full = pl.BlockSpec((ATT_HPS, S, HEAD), lambda h, i: (h, 0, 0))
    return pl.pallas_call(
        body, name=name, grid=(ATT_KV // ATT_HPS, nb),
        in_specs=[pl.BlockSpec(memory_space=pltpu.SMEM), _BAND, qspec, cur, prev, cur, prev, tok],
        out_specs=[qspec, full, full, pl.BlockSpec((ATT_HPS, 8, 128), lambda h, i: (h, 0, 0))],
        out_shape=[jax.ShapeDtypeStruct((ATT_KV, ATT_G, S, HEAD), F32), jax.ShapeDtypeStruct((ATT_KV, S, HEAD), F32),
                   jax.ShapeDtypeStruct((ATT_KV, S, HEAD), F32), jax.ShapeDtypeStruct((ATT_KV, 8, 128), F32)],
        compiler_params=_cp(2),
    )(sinks, _attn_band(), qh, kh, kh, vh, vh, doh)


def _heads_major(t, nh):
    S = t.shape[0]
    return t.reshape(S, nh, HEAD).transpose(1, 0, 2)


def _tokens_major(t):
    nh, S, _ = t.shape
    return t.transpose(1, 0, 2).reshape(S, nh * HEAD)


class _NoComm:
    def late_weights(self, w, after):
        return w

    def advance(self, after, group=None, tensors=None):
        return None


def _local_step(x, posf, target, w, comm=None):
    S, D = x.shape
    gr = {}
    comm = comm or _NoComm()

    (h1,) = _rms_fwd(x, [w["a_norm"]], name="a_norm_f", dep=w.get("dep"))
    zx = _mm(h1, w["w_zx"], name="in_proj_zx")
    dtr = _mm(h1, w["w_dt"], name="in_proj_dt")
    xbc = _conv_silu_fwd(zx, 2048, 4096, w["a_conv_w"], w["a_conv_b"], name="a_conv_f")
    dt_g, ac_g, sg_g, ac_t = _ssd_prep(dtr, w["a_dt_bias"], w["a_A_log"])
    y_ssd, states = _ssd_fwd(xbc, dt_g, ac_g, ac_t)
    yg = _gate_fwd(y_ssd, xbc, zx, w["a_Dexp"], w["a_gnorm"])
    x1 = _mm(yg, w["a_out_proj"], res=x, name="out_proj")

    w = comm.late_weights(w, x1)
    FW = w["f_w_in"][0].shape[2]

    def ffn_fwd(xin, l, loss_target=None):
        (h,) = _rms_fwd(xin, [w["f_norm"][l]], name=f"f_norm_f{l}")
        u = _mm(h, w["f_w_in"][l], name=f"f_in{l}", dims=(S, N_CHIPS * FW, D), tn=FW,
                b_spec=pl.BlockSpec((None, D, FW), lambda i, j, k: (j, 0, 0)))
        a = _ffn_act_fwd(u, w["f_conv_w"][l], w["f_conv_b"][l], name=f"f_act_f{l}")
        xo = _mm(a, w["f_w_down"][l], res=xin, tk=a.shape[1], name=f"f_down{l}", target=loss_target)
        return xo, (h, u)

    x2, ffn0 = ffn_fwd(x1, 0)

    hk, hq = _rms_fwd(x2, [w["kv_norm"], w["b_norm"]], name="kvq_norm_f")
    kv = _mm(hk, w["w_kv"], bias=w["b_kv"], name="kv_proj")
    q = _mm(hq, w["w_q"], bias=w["b_q"], name="q_proj")
    rope = _rope_cs(posf)
    kr = _hnrope_fwd(kv, 0, 256, w["k_norm_w"], rope, name="k_rope_f")
    qr = _hnrope_fwd(q, 0, 1024, w["q_norm_w"], rope, name="q_rope_f")
    qh = qr.reshape(ATT_KV, ATT_G, S, HEAD)
    kh = kr
    vh = _heads_major(kv[:, 256:].astype(BF16), ATT_KV)
    att = _attn_fwd(qh, kh, vh, w["sinks"])
    x3 = _mm(att, w["w_o"], bias=w["b_o"], res=x2, name="o_proj")
    (dy, loss_part), ffn1 = ffn_fwd(x3, 1, target)

    def ffn_bwd(xin, l, saved, dyo, want_colsum, dep=None):
        h, u = saved
        da = _mm(dyo, w["f_w_down"][l], tb=True, name=f"f_down_dx{l}", dep=dep)
        du, dcw, dcb, a = _ffn_act_bwd(u, w["f_conv_w"][l], w["f_conv_b"][l], da, name=f"f_act_b{l}")
        dw_down = _mm(a, dyo, ta=True, out_dtype=BF16, name=f"f_down_dw{l}")
        dw_in = _mm(h, du, ta=True, out_dtype=BF16, name=f"f_in_dw{l}", dims=(D, N_CHIPS * FW, S), tm=D, tn=FW, tk=S,
                    b_spec=pl.BlockSpec((None, S, FW), lambda i, j, k: (j // 2, 0, j % 2)),
                    o_spec=pl.BlockSpec((None, D, FW), lambda i, j, k: (j, i, 0)), o_shape=(N_CHIPS, D, FW))
        ts = _pick(S, (1024, 512, 256))
        pieces = [(pl.BlockSpec((None, ts, FW), lambda i, j, k, q=q: (q // 2, i, q % 2)),
                   pl.BlockSpec((None, 512, FW), lambda i, j, k, q=q: (q, j, 0))) for q in range(N_CHIPS)]
        dh = _mm(du, w["f_w_in"][l], tb=True, name=f"f_in_dx{l}", dims=(S, D, FW), tm=ts, tn=512, tk=FW,
                 a_spec=pieces[0][0], b_spec=pieces[0][1], more=pieces[1:])
        outs = _rms_bwd(xin, [w["f_norm"][l]], [dh], dyo, name=f"f_norm_b{l}", want_colsum=want_colsum)
        g = dict(f_norm=outs[1], f_w_in=dw_in, f_conv_w=dcw, f_conv_b=dcb, f_w_down=dw_down)
        return outs[0], g, (outs[2] if want_colsum else None)

    dx3, gr["ffn1"], db_o = ffn_bwd(x3, 1, ffn1, dy, True)
    gr["b_o"] = db_o
    gr["w_o"] = _mm(att, dx3, ta=True, out_dtype=BF16, name="o_proj_dw")
    datt = _mm(dx3, w["w_o"], tb=True, out_dtype=BF16, name="o_proj_dx")
    dqh, dkh, dvh, dsk = _attn_bwd(qh, kh, vh, w["sinks"], datt)
    gr["sinks"] = dsk[:, 0, :4].reshape(1, 16)
    dv = _tokens_major(dvh).astype(BF16)
    dq, db_q, dqn = _hnrope_bwd(q, 0, 1024, w["q_norm_w"], rope, dqh.reshape(16, S, HEAD), name="q_rope_b")
    dk, db_k, dkn = _hnrope_bwd(kv, 0, 256, w["k_norm_w"], rope, dkh, name="k_rope_b")
    gr["q_norm"], gr["k_norm"] = dqn[:, :HEAD], dkn[:, :HEAD]
    gr["b_q"] = db_q
    gr["b_kv"] = jnp.concatenate([db_k, _colsum(dv, name="dv_colsum")], axis=1)
    dkv = jnp.concatenate([dk, dv], axis=1)
    gr["w_q"] = _mm(hq, dq, ta=True, out_dtype=BF16, name="q_proj_dw")
    gr["w_kv"] = _mm(hk, dkv, ta=True, out_dtype=BF16, name="kv_proj_dw")
    tok = comm.advance([gr["w_kv"]], 1, dict(f_down1=gr["ffn1"]["f_w_down"], f_in1=gr["ffn1"]["f_w_in"], w_o=gr["w_o"],
                                             w_q=gr["w_q"], w_kv=gr["w_kv"]))
    dhq = _mm(dq, w["w_q"], tb=True, name="q_proj_dx", dep=tok)
    dhk = _mm(dkv, w["w_kv"], tb=True, name="kv_proj_dx")
    dx2, gr["kv_norm"], gr["b_norm"] = _rms_bwd(x2, [w["kv_norm"], w["b_norm"]], [dhk, dhq], dx3, name="kvq_norm_b")

    dx1, gr["ffn0"], _ = ffn_bwd(x1, 0, ffn0, dx2, False, dep=comm.advance([dx2]))

    gr["a_out_proj"] = _mm(yg, dx1, ta=True, out_dtype=BF16, name="out_proj_dw")
    tok = comm.advance([dx1, gr["a_out_proj"]], 2,
                       dict(f_down0=gr["ffn0"]["f_w_down"], f_in0=gr["ffn0"]["f_w_in"], out_proj=gr["a_out_proj"]))
    dyg = _mm(dx1, w["a_out_proj"], tb=True, name="out_proj_dx", dep=tok)
    dy_ssd, dz, gr["a_gnorm"], dD = _gate_bwd(y_ssd, xbc, zx, w["a_Dexp"], w["a_gnorm"], dyg)
    gr["a_D"] = dD[:, :SSM_HEADS]
    dxs, dB, dC, dhead = _ssd_bwd(xbc, dt_g, ac_g, ac_t, states, dy_ssd, w["a_Dexp"], dep=comm.advance([dy_ssd]))
    ddtr, dsmall = _ssd_post(dhead, dt_g, sg_g, w["a_A_log_g"])
    gr["a_A_log"] = dsmall[:, 0, :4].reshape(1, SSM_HEADS)
    gr["a_dt_bias"] = dsmall[:, 1, :4].reshape(1, SSM_HEADS)
    dxbc, gr["a_conv_w"], gr["a_conv_b"] = _conv_silu_bwd(
        zx, 2048, 4096, w["a_conv_w"], w["a_conv_b"], [(dxs, 0), (dB, 2048), (dC, 3072)], name="a_conv_b")
    gr["w_z"] = _mm(h1, dz, ta=True, out_dtype=BF16, name="in_proj_dwz")
    gr["w_x"] = _mm(h1, dxbc, ta=True, out_dtype=BF16, name="in_proj_dwx")
    gr["w_dt"] = _mm(h1, ddtr, ta=True, out_dtype=BF16, name="in_proj_dwdt")
    ts = _pick(S, (1024, 512, 256))
    wblk = lambda q: pl.BlockSpec((512, 2048), lambda i, j, k: (j, q))
    dh1 = _mm(dz, w["w_zx"], tb=True, name="in_proj_dx", dims=(S, D, 2048), tm=ts, tn=512, tk=2048,
              a_spec=pl.BlockSpec((ts, 2048), lambda i, j, k: (i, 0)), b_spec=wblk(0),
              more=[(dxbc, pl.BlockSpec((ts, 2048), lambda i, j, k: (i, 0)), w["w_zx"], wblk(1)),
                    (dxbc, pl.BlockSpec((ts, 2048), lambda i, j, k: (i, 1)), w["w_zx"], wblk(2)),
                    (ddtr, pl.BlockSpec((ts, 128), lambda i, j, k: (i, 0)), w["w_dt"], pl.BlockSpec((512, 128), lambda i, j, k: (j, 0)))])
    dx0, gr["a_norm"] = _rms_bwd(x, [w["a_norm"]], [dh1], dx1, name="a_norm_b")
    tok = comm.advance([dx0], 3, dict(in_proj=_in_proj_grad(gr).reshape(D, N_CHIPS, -1).transpose(1, 0, 2)))
    return loss_part, dx0, gr, tok


def _prep_small(full, w):
    w["a_norm"] = full["a_norm"]
    w["a_conv_w"] = full["a_conv_w"][0]
    w["a_conv_b"] = full["a_conv_b"]
    pad32 = lambda v: jnp.pad(v, ((0, 0), (0, 128 - SSM_HEADS)))
    w["a_dt_bias"] = pad32(full["a_dt_bias"])
    w["a_A_log"] = pad32(full["a_A_log"])
    w["a_A_log_g"] = jnp.pad(full["a_A_log"].reshape(SSM_GROUPS, 1, 4), ((0, 0), (0, 0), (0, 124)))
    w["a_Dexp"] = jnp.repeat(full["a_D"], HEAD, axis=1)
    w["a_gnorm"] = full["a_gnorm"]
    w["f_norm"] = [full["f_norm"][l:l + 1] for l in range(2)]
    w["f_conv_w"] = [full["f_conv_w"][l] for l in range(2)]
    w["f_conv_b"] = [full["f_conv_b"][l:l + 1] for l in range(2)]
    w["kv_norm"] = full["kv_norm"].reshape(1, -1)
    w["b_kv"] = full["b_kv"].reshape(1, -1)
    w["k_norm_w"] = jnp.tile(full["k_norm"].reshape(1, HEAD), (1, ATT_KV))
    w["b_norm"] = full["b_norm"]
    w["b_q"] = full["b_q"]
    w["q_norm_w"] = jnp.tile(full["q_norm"], (1, ATT_KV * ATT_G))
    w["sinks"] = full["sinks"].reshape(-1)
    w["b_o"] = full["b_o"]
    return w


def _split_in_proj(ip):
    return ip[:, :6144].astype(BF16), jnp.pad(ip[:, 6144:], ((0, 0), (0, 128 - SSM_HEADS))).astype(BF16)


def _join_in_proj(blocks, *, name="in_proj_join", tr=256):
    _, R, cw = blocks.shape
    zx_cols = 3 * 2048
    rest = N_CHIPS * cw - zx_cols

    def body(b_ref, zx_ref, dt_ref):
        whole = jnp.concatenate([b_ref[j] for j in range(N_CHIPS)], axis=1)
        zx_ref[...] = whole[:, :zx_cols]
        dt_ref[...] = jnp.concatenate([whole[:, zx_cols:], jnp.zeros((tr, 128 - rest), BF16)], axis=1)

    return pl.pallas_call(
        body, name=name, grid=(R // tr,), in_specs=[pl.BlockSpec((N_CHIPS, tr, cw), lambda i: (0, i, 0))],
        out_specs=[pl.BlockSpec((tr, zx_cols), lambda i: (i, 0)), pl.BlockSpec((tr, 128), lambda i: (i, 0))],
        out_shape=[jax.ShapeDtypeStruct((R, zx_cols), BF16), jax.ShapeDtypeStruct((R, 128), BF16)],
        compiler_params=_cp(1),
    )(blocks)


def _prep_weights(full):
    w = _prep_small(full, {})
    w["w_zx"], w["w_dt"] = _split_in_proj(full["a_in_proj"][0])
    w["a_out_proj"] = full["a_out_proj"][0].astype(BF16)
    w["f_w_in"] = [full["f_w_in"][l].reshape(1024, N_CHIPS, -1).transpose(1, 0, 2).astype(BF16) for l in range(2)]
    w["f_w_down"] = [full["f_w_down"][l].astype(BF16) for l in range(2)]
    w["w_kv"] = full["w_kv"].astype(BF16)
    w["w_q"] = full["w_q"][0].astype(BF16)
    w["w_o"] = full["w_o"][0].astype(BF16)
    return w


def _small_grads(gr):
    g = {}
    g["a_norm"] = gr["a_norm"]
    g["a_conv_w"] = gr["a_conv_w"][None]
    g["a_conv_b"] = gr["a_conv_b"]
    g["a_dt_bias"], g["a_A_log"], g["a_D"] = gr["a_dt_bias"], gr["a_A_log"], gr["a_D"]
    g["a_gnorm"] = gr["a_gnorm"]
    g["kv_norm"] = gr["kv_norm"].reshape(-1)
    g["b_kv"] = gr["b_kv"].reshape(-1)
    g["k_norm"] = gr["k_norm"].reshape(-1)
    g["b_norm"] = gr["b_norm"]
    g["b_q"] = gr["b_q"]
    g["q_norm"] = gr["q_norm"]
    g["sinks"] = gr["sinks"]
    g["b_o"] = gr["b_o"]
    f = [gr["ffn0"], gr["ffn1"]]
    g["f_norm"] = jnp.concatenate([f[0]["f_norm"], f[1]["f_norm"]], axis=0)
    g["f_conv_w"] = jnp.stack([f[l]["f_conv_w"] for l in range(2)])
    g["f_conv_b"] = jnp.concatenate([f[l]["f_conv_b"] for l in range(2)], axis=0)
    return g


def _in_proj_grad(gr):
    return jnp.concatenate([gr["w_z"], gr["w_x"], gr["w_dt"][:, :SSM_HEADS]], axis=1)


def _full_grads(gr):
    g = _small_grads(gr)
    f32 = lambda t: t.astype(F32)
    g["a_in_proj"] = f32(_in_proj_grad(gr))[None]
    g["a_out_proj"] = f32(gr["a_out_proj"])[None]
    g["w_kv"] = f32(gr["w_kv"])
    g["w_q"] = f32(gr["w_q"])[None]
    g["w_o"] = f32(gr["w_o"])[None]
    f = [gr["ffn0"], gr["ffn1"]]
    g["f_w_in"] = jnp.stack([f32(f[l]["f_w_in"]).transpose(1, 0, 2).reshape(1024, -1) for l in range(2)])
    g["f_w_down"] = jnp.stack([f32(f[l]["f_w_down"]) for l in range(2)])
    return g


MESH = pl.DeviceIdType.MESH
WEIGHTS = ("a_norm", "a_in_proj", "a_conv_w", "a_conv_b", "a_dt_bias", "a_A_log", "a_D", "a_gnorm", "a_out_proj",
           "kv_norm", "w_kv", "b_kv", "k_norm", "b_norm", "w_q", "b_q", "q_norm", "sinks", "w_o", "b_o", "f_norm",
           "f_w_in", "f_conv_w", "f_conv_b", "f_w_down")
MATS = (("in_proj", "a_in_proj", 0), ("out_proj", "a_out_proj", 0), ("w_kv", "w_kv", None), ("w_q", "w_q", 0),
        ("w_o", "w_o", 0), ("f_in0", "f_w_in", 0), ("f_in1", "f_w_in", 1), ("f_down0", "f_w_down", 0),
        ("f_down1", "f_w_down", 1))
SMALL_CUT = (("a_norm", 1), ("a_conv_w", 2), ("a_conv_b", 1), ("a_gnorm", 1), ("f_conv_w", 2))
SMALL_REP = ("a_dt_bias", "a_A_log", "a_D", "kv_norm", "b_kv", "k_norm", "b_norm", "b_q", "q_norm", "sinks", "b_o",
             "f_norm", "f_conv_b")


def _coords():
    return lax.axis_index("x"), lax.axis_index("y"), lax.axis_index("c")


def _other_chips(x, y):
    return [(1 - x, y), (x, 1 - y), (1 - x, 1 - y)]


def _pack(arrs, rows_align, lanes, dtype):
    flat = jnp.concatenate([a.reshape(-1).astype(dtype) for a in arrs])
    per = rows_align * lanes
    total = -(-flat.shape[0] // per) * per
    return jnp.pad(flat, (0, total - flat.shape[0])).reshape(total // lanes, lanes)


def _unpack(flat, shapes):
    out, off = [], 0
    for s in shapes:
        n = math.prod(s)
        out.append(flat[off:off + n].reshape(s))
        off += n
    return out


def _remote(src, dst, send, recv, k, dev):
    return pltpu.make_async_remote_copy(src_ref=src, dst_ref=dst, send_sem=send.at[k], recv_sem=recv.at[k],
                                        device_id=dev, device_id_type=MESH)


_ANY = pl.BlockSpec(memory_space=pl.ANY)


def _halves(t):
    r, c = t.shape
    return t.reshape(2, r // 2, c)


def _gather_weights(shards, sp):
    n = len(shards)
    n_sem = 7 * n + 3

    def body(*refs):
        sh, sp_ref = refs[:n], refs[n]
        outs, sout = refs[n + 1:2 * n + 1], refs[2 * n + 1]
        send, recv, loc = refs[2 * n + 2:]
        x, y, c = _coords()
        me = 2 * x + y
        chips = _other_chips(x, y)
        sib = (x, y, 1 - c)
        l1 = pltpu.make_async_copy(sp_ref, sout.at[me], loc.at[0])
        l1.start()
        sends = []
        for j, (cx, cy) in enumerate(chips):
            sends.append(_remote(sp_ref, sout.at[me], send, recv, 7 * n + j, (cx, cy, c)))
            for t in range(n):
                sends.append(_remote(sh[t].at[c], outs[t].at[me, c], send, recv, 7 * t + j, (cx, cy, c)))
        for t in range(n):
            sends.append(_remote(sh[t], outs[t].at[me], send, recv, 7 * t + 6, sib))
        for cp in sends:
            cp.start()
        for j, (cx, cy) in enumerate(chips):
            src = 2 * cx + cy
            for t in range(n):
                _remote(sh[t].at[c], outs[t].at[src, c], send, recv, 7 * t + j, (cx, cy, c)).wait_recv()
                fwd = _remote(outs[t].at[src, c], outs[t].at[src, c], send, recv, 7 * t + 3 + j, sib)
                fwd.start()
                sends.append(fwd)
        for j, (cx, cy) in enumerate(chips):
            src = 2 * cx + cy
            _remote(sp_ref, sout.at[src], send, recv, 7 * n + j, (cx, cy, c)).wait_recv()
            for t in range(n):
                _remote(outs[t].at[src, 1 - c], outs[t].at[src, 1 - c], send, recv, 7 * t + 3 + j, sib).wait_recv()
        for t in range(n):
            _remote(sh[t], outs[t].at[me], send, recv, 7 * t + 6, sib).wait_recv()
        for cp in sends:
            cp.wait_send()
        l1.wait()

    res = pl.pallas_call(
        body, name="gather_weights", in_specs=[_ANY] * (n + 1), out_specs=[_ANY] * (n + 1),
        out_shape=[jax.ShapeDtypeStruct((N_CHIPS,) + t.shape, t.dtype) for t in shards]
        + [jax.ShapeDtypeStruct((N_CHIPS,) + sp.shape, sp.dtype)],
        scratch_shapes=[pltpu.SemaphoreType.DMA((n_sem,)), pltpu.SemaphoreType.DMA((n_sem,)),
                        pltpu.SemaphoreType.DMA((1,))],
    )(*shards, sp)
    return res[:n], res[n]


_HBM = pl.BlockSpec(memory_space=pltpu.HBM)
_SEMS = pl.BlockSpec(memory_space=pltpu.SEMAPHORE)
_DATAFLOW = pltpu.SideEffectType.DATAFLOW_SIDE_EFFECTING


def _in_hbm(a):
    return pltpu.with_memory_space_constraint(a, pltpu.HBM)


def _start_copies(copies, arrays, n_sem, after, *, name):
    n, na = len(arrays), len(after)

    def body(*refs):
        for mine, _ in copies(refs[:n], refs[n + na], refs[n + na + 1]):
            mine.start()
        refs[-1][...] = jnp.zeros_like(refs[-1])

    res = pl.pallas_call(
        body, name=name, in_specs=[_HBM] * n + [_ANY] * na,
        out_specs=[_SEMS, _SEMS] + [_HBM] * n + [pl.BlockSpec(memory_space=pltpu.VMEM)],
        out_shape=[pltpu.SemaphoreType.DMA((n_sem,)), pltpu.SemaphoreType.DMA((n_sem,))]
        + [pltpu.HBM(a.shape, a.dtype) for a in arrays] + [jax.ShapeDtypeStruct((8, 128), F32)],
        input_output_aliases={t: 2 + t for t in range(n)},
        compiler_params=pltpu.CompilerParams(has_side_effects=_DATAFLOW),
    )(*[_in_hbm(a) for a in arrays], *after)
    return res[0], res[1], list(res[2:2 + n]), res[-1]


def _wait_copies(copies, send, recv, arrays, after, *, name):
    n = len(arrays)

    def body(*refs):
        for mine, theirs in copies(refs[:n], refs[n], refs[n + 1]):
            mine.wait_send()
            theirs.wait_recv()

    return list(pl.pallas_call(
        body, name=name, in_specs=[_HBM] * n + [_SEMS, _SEMS] + [_ANY] * len(after), out_specs=[_HBM] * n,
        out_shape=[pltpu.HBM(a.shape, a.dtype) for a in arrays], input_output_aliases={t: t for t in range(n)},
        compiler_params=pltpu.CompilerParams(has_side_effects=_DATAFLOW),
    )(*arrays, send, recv, *after))


def _sibling_copies(refs, send, recv):
    n = len(refs) // 2
    x, y, c = _coords()
    cps = [_remote(refs[t].at[:, 1 - c], refs[n + t], send, recv, t, (x, y, 1 - c)) for t in range(n)]
    return [(cp, cp) for cp in cps]


def _join_copies(refs, send, recv):
    x, y, c = _coords()
    sib = (x, y, 1 - c)
    return [(_remote(o.at[c], o.at[c], send, recv, t, sib), _remote(o.at[1 - c], o.at[1 - c], send, recv, t, sib))
            for t, o in enumerate(refs)]


def _gather_copies(sh, land, send, recv):
    x, y, c = _coords()
    me = 2 * x + y
    out = []
    for t in range(len(sh)):
        for j, (cx, cy) in enumerate(_other_chips(x, y)):
            dev = (cx, cy, c)
            out.append((_remote(sh[t].at[c], land[t].at[me, c], send, recv, 4 * t + j, dev),
                        _remote(sh[t].at[c], land[t].at[2 * cx + cy, c], send, recv, 4 * t + j, dev)))
        sib = (x, y, 1 - c)
        out.append((_remote(sh[t], land[t].at[me], send, recv, 4 * t + 3, sib),
                    _remote(sh[t], land[t].at[me], send, recv, 4 * t + 3, sib)))
    return out


def _gather_start(shards, after, *, name):
    n = len(shards)

    def body(*refs):
        sh, land = refs[:n], refs[n:2 * n]
        send, recv = refs[2 * n + 1], refs[2 * n + 2]
        token = refs[-1]
        for mine, _ in _gather_copies(sh, land, send, recv):
            mine.start()
        token[...] = jnp.zeros_like(token)

    lands = [_in_hbm(lax.empty((N_CHIPS,) + s.shape, s.dtype)) for s in shards]
    res = pl.pallas_call(
        body, name=name, in_specs=[_HBM] * (2 * n) + [_ANY],
        out_specs=[_SEMS, _SEMS] + [_HBM] * (2 * n) + [pl.BlockSpec(memory_space=pltpu.VMEM)],
        out_shape=[pltpu.SemaphoreType.DMA((4 * n,)), pltpu.SemaphoreType.DMA((4 * n,))]
        + [pltpu.HBM(s.shape, s.dtype) for s in shards] + [pltpu.HBM(l.shape, l.dtype) for l in lands]
        + [jax.ShapeDtypeStruct((8, 128), F32)],
        input_output_aliases={t: 2 + t for t in range(2 * n)},
        compiler_params=pltpu.CompilerParams(has_side_effects=_DATAFLOW),
    )(*[_in_hbm(s) for s in shards], *lands, after)
    return res[0], res[1], res[2:2 + n], res[2 + n:2 + 2 * n], res[-1]


def _gather_wait(send, recv, shards, lands, after, *, name):
    n = len(shards)

    def body(*refs):
        sh, land = refs[:n], refs[n:2 * n]
        send_r, recv_r = refs[2 * n], refs[2 * n + 1]
        for mine, theirs in _gather_copies(sh, land, send_r, recv_r):
            mine.wait_send()
            theirs.wait_recv()

    res = pl.pallas_call(
        body, name=name, in_specs=[_HBM] * (2 * n) + [_SEMS, _SEMS, _ANY], out_specs=[_HBM] * (2 * n),
        out_shape=[pltpu.HBM(s.shape, s.dtype) for s in shards] + [pltpu.HBM(l.shape, l.dtype) for l in lands],
        input_output_aliases={t: t for t in range(2 * n)},
        compiler_params=pltpu.CompilerParams(has_side_effects=_DATAFLOW),
    )(*shards, *lands, send, recv, after)
    return res[n:]


def _gather_forward(lands, *, name):
    n = len(lands)

    def body(*refs):
        o = refs[n:2 * n]
        send, recv = refs[2 * n:]
        x, y, c = _coords()
        sib = (x, y, 1 - c)
        srcs = [2 * cx + cy for cx, cy in _other_chips(x, y)]
        cps = [_remote(o[t].at[s, c], o[t].at[s, c], send, recv, 3 * t + j, sib) for t in range(n) for j, s in enumerate(srcs)]
        for cp in cps:
            cp.start()
        for t in range(n):
            for j, s in enumerate(srcs):
                _remote(o[t].at[s, 1 - c], o[t].at[s, 1 - c], send, recv, 3 * t + j, sib).wait_recv()
        for cp in cps:
            cp.wait_send()

    return pl.pallas_call(
        body, name=name, in_specs=[_ANY] * n, out_specs=[_ANY] * n, input_output_aliases={t: t for t in range(n)},
        out_shape=[jax.ShapeDtypeStruct(l.shape, l.dtype) for l in lands],
        scratch_shapes=[pltpu.SemaphoreType.DMA((3 * n,)), pltpu.SemaphoreType.DMA((3 * n,))],
    )(*lands)


def _small_copies(v, land, send, recv):
    x, y, c = _coords()
    me = 4 * x + 2 * y + c
    out = []
    for k in range(1, 8):
        px = 1 - x if k & 4 else x
        py = 1 - y if k & 2 else y
        pc = 1 - c if k & 1 else c
        out.append((_remote(v, land.at[me], send, recv, k - 1, (px, py, pc)),
                    _remote(v, land.at[4 * px + 2 * py + pc], send, recv, k - 1, (px, py, pc))))
    return out


def _small_start(v, after, *, name):
    def body(v_ref, land_ref, after_ref, send, recv, v_thru, land_thru, token):
        for mine, _ in _small_copies(v_ref, land_ref, send, recv):
            mine.start()
        token[...] = jnp.zeros_like(token)

    land = _in_hbm(lax.empty((8,) + v.shape, v.dtype))
    return pl.pallas_call(
        body, name=name, in_specs=[_HBM, _HBM, _ANY],
        out_specs=[_SEMS, _SEMS, _HBM, _HBM, pl.BlockSpec(memory_space=pltpu.VMEM)],
        out_shape=[pltpu.SemaphoreType.DMA((7,)), pltpu.SemaphoreType.DMA((7,)), pltpu.HBM(v.shape, v.dtype),
                   pltpu.HBM(land.shape, land.dtype), jax.ShapeDtypeStruct((8, 128), F32)],
        input_output_aliases={0: 2, 1: 3}, compiler_params=pltpu.CompilerParams(has_side_effects=_DATAFLOW),
    )(_in_hbm(v), land, after)


def _small_wait(send, recv, v, land, after, *, name):
    def body(v_ref, land_ref, send_r, recv_r, *rest):
        for mine, theirs in _small_copies(v_ref, land_ref, send_r, recv_r):
            mine.wait_send()
            theirs.wait_recv()

    return pl.pallas_call(
        body, name=name, in_specs=[_HBM, _HBM, _SEMS, _SEMS] + [_ANY] * len(after), out_specs=[_HBM, _HBM],
        out_shape=[pltpu.HBM(v.shape, v.dtype), pltpu.HBM(land.shape, land.dtype)],
        input_output_aliases={0: 0, 1: 1}, compiler_params=pltpu.CompilerParams(has_side_effects=_DATAFLOW),
    )(v, land, send, recv, *after)


def _small_sum(v, land, me_idx, *, name="small_sum"):
    def body(me_ref, v_ref, land_ref, o_ref):
        acc = None
        for s in range(8):
            term = jnp.where(me_ref[0] == s, v_ref[...], land_ref[s])
            acc = term if acc is None else acc + term
        o_ref[...] = acc

    whole = lambda shape: pl.BlockSpec(shape, lambda i, me_ref: (0,) * len(shape))
    return pl.pallas_call(
        body, name=name,
        grid_spec=pltpu.PrefetchScalarGridSpec(num_scalar_prefetch=1, grid=(1,), in_specs=[whole(v.shape), whole(land.shape)],
                                               out_specs=whole(v.shape)),
        out_shape=jax.ShapeDtypeStruct(v.shape, F32), compiler_params=_cp(1),
    )(me_idx, v, land)


RS_ROW_SPLIT = 2


def _rs_add_pair(gs, as_, c_idx, *, name):
    n = len(gs)

    def body(c_ref, *refs):
        for t in range(n):
            refs[2 * n + t][...] = (refs[t][...].astype(F32) + refs[n + t][...].astype(F32)).astype(BF16)

    def gspec(g):
        _, _, rh, cols = g.shape
        return pl.BlockSpec((None, None, rh // RS_ROW_SPLIT, cols), lambda j, i, c_ref: (j, c_ref[0], i, 0))

    def pspec(g):
        _, _, rh, cols = g.shape
        return pl.BlockSpec((None, rh // RS_ROW_SPLIT, cols), lambda j, i, c_ref: (j, i, 0))

    return pl.pallas_call(
        body, name=name,
        grid_spec=pltpu.PrefetchScalarGridSpec(
            num_scalar_prefetch=1, grid=(N_CHIPS, RS_ROW_SPLIT),
            in_specs=[gspec(g) for g in gs] + [pspec(g) for g in gs], out_specs=[pspec(g) for g in gs]),
        out_shape=[jax.ShapeDtypeStruct((N_CHIPS,) + g.shape[2:], BF16) for g in gs], compiler_params=_cp(2),
    )(c_idx, *gs, *as_)


def _chips_copies(p, r, send, recv):
    x, y, c = _coords()
    return [_remote(p[t].at[2 * cx + cy], r[t].at[k], send, recv, 3 * t + k, (cx, cy, c))
            for k, (cx, cy) in enumerate(_other_chips(x, y)) for t in range(len(p))]


def _rs_chips_start(ps, after, *, name):
    n, na = len(ps), len(after)

    def body(*refs):
        p, r = refs[:n], refs[n:2 * n]
        send, recv = refs[2 * n + na], refs[2 * n + na + 1]
        token = refs[-1]
        for cp in _chips_copies(p, r, send, recv):
            cp.start()
        token[...] = jnp.zeros_like(token)

    lands = [_in_hbm(lax.empty((3,) + p.shape[1:], p.dtype)) for p in ps]
    res = pl.pallas_call(
        body, name=name, in_specs=[_HBM] * (2 * n) + [_ANY] * na,
        out_specs=[_SEMS, _SEMS] + [_HBM] * (2 * n) + [pl.BlockSpec(memory_space=pltpu.VMEM)],
        out_shape=[pltpu.SemaphoreType.DMA((3 * n,)), pltpu.SemaphoreType.DMA((3 * n,))]
        + [pltpu.HBM(p.shape, p.dtype) for p in ps] + [pltpu.HBM(l.shape, l.dtype) for l in lands]
        + [jax.ShapeDtypeStruct((8, 128), F32)],
        input_output_aliases={t: 2 + t for t in range(2 * n)},
        compiler_params=pltpu.CompilerParams(has_side_effects=_DATAFLOW),
    )(*[_in_hbm(p) for p in ps], *lands, *after)
    return res[0], res[1], res[2:2 + n], res[2 + n:2 + 2 * n], res[-1]


def _rs_chips_wait(send, recv, ps, lands, after, *, name):
    n = len(ps)

    def body(*refs):
        p, r = refs[:n], refs[n:2 * n]
        for cp in _chips_copies(p, r, refs[2 * n], refs[2 * n + 1]):
            cp.wait_send()
            cp.wait_recv()

    res = pl.pallas_call(
        body, name=name, in_specs=[_HBM] * (2 * n) + [_SEMS, _SEMS] + [_ANY] * len(after), out_specs=[_HBM] * (2 * n),
        out_shape=[pltpu.HBM(p.shape, p.dtype) for p in ps] + [pltpu.HBM(l.shape, l.dtype) for l in lands],
        input_output_aliases={t: t for t in range(2 * n)},
        compiler_params=pltpu.CompilerParams(has_side_effects=_DATAFLOW),
    )(*ps, *lands, send, recv, *after)
    return res[:n], res[n:]


def _rs_add_chips(ps, rs, idx, *, name):
    n = len(ps)

    def body(idx_ref, *refs):
        for t in range(n):
            p_ref, r0, r1, r2 = refs[4 * t:4 * t + 4]
            refs[4 * n + t][...] = ((p_ref[...].astype(F32) + r0[...].astype(F32)) + r1[...].astype(F32)) + r2[...].astype(F32)

    in_specs, args = [], []
    for p, r in zip(ps, rs):
        _, rh, cols = p.shape
        blk = (None, rh // RS_ROW_SPLIT, cols)
        in_specs.append(pl.BlockSpec(blk, lambda i, idx_ref: (idx_ref[0], i, 0)))
        in_specs += [pl.BlockSpec(blk, lambda i, idx_ref, k=k: (k, i, 0)) for k in range(3)]
        args += [p, r, r, r]
    out_specs = [pl.BlockSpec((None, p.shape[1] // RS_ROW_SPLIT, p.shape[2]), lambda i, idx_ref: (idx_ref[1], i, 0))
                 for p in ps]
    return pl.pallas_call(
        body, name=name,
        grid_spec=pltpu.PrefetchScalarGridSpec(num_scalar_prefetch=1, grid=(RS_ROW_SPLIT,), in_specs=in_specs,
                                               out_specs=out_specs),
        out_shape=[jax.ShapeDtypeStruct((2,) + p.shape[1:], F32) for p in ps], compiler_params=_cp(1),
    )(idx, *args)


def _adamw(w, gs, m, v, *, name, dep=None):
    L, Rr, C = w.shape
    tr, tc = _pick(Rr, (256, 128, 64)), C
    if tr == Rr and Rr * C > 512 * 1024:
        tc = 256
    bc1 = 1.0 - ADAM_B1 ** ADAM_STEP
    bc2 = 1.0 - ADAM_B2 ** ADAM_STEP
    nd = 0 if dep is None else 1

    def body(*refs):
        w_ref, m_ref, v_ref = refs[0], refs[1], refs[2]
        g_refs = refs[3:3 + L]
        d_ref, mo_ref, vo_ref, go_ref = refs[3 + L + nd:]
        layer = pl.program_id(0)
        gv = g_refs[0][...]
        for q in range(1, L):
            gv = jnp.where(layer == q, g_refs[q][...], gv)
        mn = ADAM_B1 * m_ref[...] + (1.0 - ADAM_B1) * gv
        vn = ADAM_B2 * v_ref[...] + (1.0 - ADAM_B2) * (gv * gv)
        go_ref[...] = gv
        mo_ref[...] = mn
        vo_ref[...] = vn
        d_ref[...] = -ADAM_LR * ((mn / bc1) / (jnp.sqrt(vn / bc2) + ADAM_EPS) + ADAM_WD * w_ref[...])

    blk = pl.BlockSpec((None, tr, tc), lambda l, i, j: (l, i, j))
    gblks = [pl.BlockSpec((tr, tc), lambda l, i, j, q=q: (jnp.where(l == q, i, 0), jnp.where(l == q, j, 0))) for q in range(L)]
    return pl.pallas_call(
        body, name=name, grid=(L, Rr // tr, C // tc), in_specs=[blk] * 3 + gblks + [_ANY] * nd, out_specs=[blk] * 4,
        out_shape=[jax.ShapeDtypeStruct((L, Rr, C), F32)] * 4, compiler_params=_cp(3),
    )(w, m, v, *gs, *([] if dep is None else [dep]))


def kernel(x, positions, a_norm, a_in_proj, a_conv_w, a_conv_b, a_dt_bias, a_A_log, a_D, a_gnorm, a_out_proj,
           kv_norm, w_kv, b_kv, k_norm, b_norm, w_q, b_q, q_norm, sinks, w_o, b_o, f_norm, f_w_in, f_conv_w,
           f_conv_b, f_w_down, loss_target, m_a_norm, m_a_in_proj, m_a_conv_w, m_a_conv_b, m_a_dt_bias, m_a_A_log,
           m_a_D, m_a_gnorm, m_a_out_proj, m_kv_norm, m_w_kv, m_b_kv, m_k_norm, m_b_norm, m_w_q, m_b_q, m_q_norm,
           m_sinks, m_w_o, m_b_o, m_f_norm, m_f_w_in, m_f_conv_w, m_f_conv_b, m_f_w_down, v_a_norm, v_a_in_proj,
           v_a_conv_w, v_a_conv_b, v_a_dt_bias, v_a_A_log, v_a_D, v_a_gnorm, v_a_out_proj, v_kv_norm, v_w_kv,
           v_b_kv, v_k_norm, v_b_norm, v_w_q, v_b_q, v_q_norm, v_sinks, v_w_o, v_b_o, v_f_norm, v_f_w_in,
           v_f_conv_w, v_f_conv_b, v_f_w_down):
    wl = dict(zip(WEIGHTS, (a_norm, a_in_proj, a_conv_w, a_conv_b, a_dt_bias, a_A_log, a_D, a_gnorm, a_out_proj,
                            kv_norm, w_kv, b_kv, k_norm, b_norm, w_q, b_q, q_norm, sinks, w_o, b_o, f_norm, f_w_in,
                            f_conv_w, f_conv_b, f_w_down)))
    ml = dict(zip(WEIGHTS, (m_a_norm, m_a_in_proj, m_a_conv_w, m_a_conv_b, m_a_dt_bias, m_a_A_log, m_a_D, m_a_gnorm,
                            m_a_out_proj, m_kv_norm, m_w_kv, m_b_kv, m_k_norm, m_b_norm, m_w_q, m_b_q, m_q_norm,
                            m_sinks, m_w_o, m_b_o, m_f_norm, m_f_w_in, m_f_conv_w, m_f_conv_b, m_f_w_down)))
    vl = dict(zip(WEIGHTS, (v_a_norm, v_a_in_proj, v_a_conv_w, v_a_conv_b, v_a_dt_bias, v_a_A_log, v_a_D, v_a_gnorm,
                            v_a_out_proj, v_kv_norm, v_w_kv, v_b_kv, v_k_norm, v_b_norm, v_w_q, v_b_q, v_q_norm,
                            v_sinks, v_w_o, v_b_o, v_f_norm, v_f_w_in, v_f_conv_w, v_f_conv_b, v_f_w_down)))
    xi, yi, ci = _coords()
    me = 2 * xi + yi
    S = x.shape[1]

    def block_of(n, layer):
        t = wl[n]
        return t if layer is None else t[layer]

    rows = lambda t: t.reshape(-1, t.shape[-1])
    c_idx = jnp.reshape(ci, (1,)).astype(jnp.int32)
    me_c = jnp.stack([me, ci]).astype(jnp.int32)
    early = ("in_proj", "out_proj")
    late = tuple(name for name, _, _ in MATS if name not in early)
    shards = {name: _halves(block_of(wn, layer).astype(BF16)) for name, wn, layer in MATS}

    sp = _pack([wl[n] for n, _ in SMALL_CUT], 8, 128, F32)
    gathered, gs = _gather_weights([shards[k] for k in early], sp)
    gt = {k: t.reshape(N_CHIPS, -1, t.shape[-1]) for k, t in zip(early, gathered)}
    started = _gather_start([shards[k] for k in late], gs, name="gather_late_start")
    full = {n: wl[n] for n in SMALL_REP}
    gs = gs.reshape(N_CHIPS, -1)
    pieces = [_unpack(gs[j], [wl[n].shape for n, _ in SMALL_CUT]) for j in range(N_CHIPS)]
    for q, (n, ax) in enumerate(SMALL_CUT):
        full[n] = jnp.concatenate([pieces[j][q] for j in range(N_CHIPS)], axis=ax)
    w = _prep_small(full, {})
    w["w_zx"], w["w_dt"] = _join_in_proj(gt["in_proj"])
    w["a_out_proj"] = rows(gt["out_proj"])
    w["dep"] = started[4]

    class Comm:
        flight = []
        reduced = {}

        def late_weights(self, w, after):
            lands = _gather_wait(started[0], started[1], started[2], started[3], after, name="gather_late_wait")
            lands = _gather_forward(lands, name="gather_late_forward")
            lt = {k: t.reshape(N_CHIPS, -1, t.shape[-1]) for k, t in zip(late, lands)}
            w = dict(w)
            w["w_kv"], w["w_q"], w["w_o"] = (rows(lt[k]) for k in ("w_kv", "w_q", "w_o"))
            w["f_w_in"] = [lt["f_in0"], lt["f_in1"]]
            w["f_w_down"] = [rows(lt["f_down0"]), rows(lt["f_down1"])]
            return w

        def advance(self, after, group=None, tensors=None):
            token = None
            for grp in list(self.flight):
                tag, n = grp["tag"], len(grp["names"])
                dep = list(after) + ([] if token is None else [token])
                if grp["stage"] == "sibling":
                    arrs = _wait_copies(_sibling_copies, grp["send"], grp["recv"], grp["arrays"], dep, name=f"rs_sibling_wait{tag}")
                    pairs = _rs_add_pair(arrs[:n], arrs[n:], c_idx, name=f"rs_add_pair{tag}")
                    send, recv, ps, lands, token = _rs_chips_start(pairs, dep, name=f"rs_chips_start{tag}")
                    grp.update(stage="chips", send=send, recv=recv, ps=ps, lands=lands)
                elif grp["stage"] == "chips":
                    ps, rs = _rs_chips_wait(grp["send"], grp["recv"], grp["ps"], grp["lands"], dep, name=f"rs_chips_wait{tag}")
                    halves = _rs_add_chips(ps, rs, me_c, name=f"rs_add_chips{tag}")
                    send, recv, arrs, token = _start_copies(_join_copies, halves, n, dep, name=f"rs_join_start{tag}")
                    grp.update(stage="join", send=send, recv=recv, arrays=arrs)
                else:
                    joined = _wait_copies(_join_copies, grp["send"], grp["recv"], grp["arrays"], dep, name=f"rs_join_wait{tag}")
                    self.reduced.update({k: rows(t) for k, t in zip(grp["names"], joined)})
                    self.flight.remove(grp)
            if group is not None:
                names = list(tensors)
                glist = [tensors[k].reshape(N_CHIPS, 2, -1, tensors[k].shape[-1]) for k in names]
                lands = [lax.empty((N_CHIPS,) + gq.shape[2:], gq.dtype) for gq in glist]
                dep = list(after) + ([] if token is None else [token])
                send, recv, arrs, token = _start_copies(_sibling_copies, glist + lands, len(names), dep,
                                                        name=f"rs_sibling_start{group}")
                self.flight.append(dict(tag=group, names=names, stage="sibling", send=send, recv=recv, arrays=arrs))
            return token

    comm = Comm()

    posf = positions.reshape(S, 1).astype(F32)
    loss_part, dx0, gr, tok = _local_step(x[0], posf, loss_target[0], w, comm)
    g = _small_grads(gr)

    small_names = [n for n, _ in SMALL_CUT] + list(SMALL_REP)
    sv = _pack([g[n] for n in small_names] + [loss_part[0:1, 0:1]], 8, 128, F32)
    s_send, s_recv, sv, s_land, s_token = _small_start(sv, tok, name="small_start")

    grads, delta, new_m, new_v = {}, {}, {}, {}

    def update(wn, dep):
        gl = [comm.reduced[name] for name, n2, _ in MATS if n2 == wn]
        shp = wl[wn].shape
        three = (len(gl),) + gl[0].shape
        flip = shp[-1] % 128 != 0
        view = (lambda t: t.reshape(three).transpose(0, 2, 1)) if flip else (lambda t: t.reshape(three))
        back = (lambda t: t.transpose(0, 2, 1).reshape(shp)) if flip else (lambda t: t.reshape(shp))
        if flip:
            gl = [t.T for t in gl]
        d, mn, vn, go = _adamw(view(wl[wn]), gl, view(ml[wn]), view(vl[wn]), name="adamw_" + wn, dep=dep)
        grads[wn], delta[wn], new_m[wn], new_v[wn] = back(go), back(d), back(mn), back(vn)
        return d

    first = [update(wn, s_token) for wn in ("w_q", "w_o", "w_kv")]
    tok = comm.advance(first)
    second = [update(wn, tok) for wn in ("f_w_in", "f_w_down", "a_out_proj")]
    comm.advance(second)
    comm.advance(second)
    update("a_in_proj", None)
    done = first + second

    sv, s_land = _small_wait(s_send, s_recv, sv, s_land, done, name="small_wait")
    sred = _small_sum(sv, s_land, jnp.reshape(2 * me + ci, (1,)).astype(jnp.int32)).reshape(-1)
    small_shapes = [g[n].shape for n in small_names] + [(1,)]
    sg = dict(zip(small_names + ["loss"], _unpack(sred, small_shapes)))
    loss = sg["loss"].reshape(())
    g_small = {}
    for n, ax in SMALL_CUT:
        size = wl[n].shape[ax]
        g_small[n] = lax.dynamic_slice_in_dim(sg[n], me * size, size, axis=ax)
    for n in SMALL_REP:
        g_small[n] = sg[n].reshape(wl[n].shape)

    pk = lambda d: _pack([d[n] for n in small_names], 8, 128, F32)[None]
    d, mn, vn, _ = _adamw(pk(wl), [pk(g_small)[0]], pk(ml), pk(vl), name="adamw_small")
    shapes = [wl[n].shape for n in small_names]
    for n, dd, mm, vv in zip(small_names, _unpack(d.reshape(-1), shapes), _unpack(mn.reshape(-1), shapes),
                             _unpack(vn.reshape(-1), shapes)):
        grads[n], delta[n], new_m[n], new_v[n] = g_small[n], dd, mm, vv

    return (loss, dx0[None], *[grads[n] for n in WEIGHTS], *[delta[n] for n in WEIGHTS],
            *[new_m[n] for n in WEIGHTS], *[new_v[n] for n in WEIGHTS])
```

```python
import math

import jax
import jax.numpy as jnp
from jax import lax
from jax.experimental import pallas as pl
from jax.experimental.pallas import tpu as pltpu

F32 = jnp.float32
BF16 = jnp.bfloat16

EPS = 1e-5
CHUNK = 256
WINDOW = 128
HEAD = 64
SSM_HEADS = 32
SSM_GROUPS = 8
SSM_STATE = 128
ATT_KV = 4
ATT_G = 4
ROPE_THETA = 10000.0
NEG = -1e30
N_CHIPS = 4
VMEM_LIMIT = 56 * 1024 * 1024

ADAM_LR, ADAM_B1, ADAM_B2, ADAM_EPS, ADAM_WD, ADAM_STEP = 0.001, 0.9, 0.999, 1e-08, 0.01, 10


def _cp(n_axes):
    return pltpu.CompilerParams(dimension_semantics=("arbitrary",) * n_axes, vmem_limit_bytes=VMEM_LIMIT)


def _pick(dim, prefs):
    for p in prefs:
        if dim % p == 0:
            return p
    return dim


def _iota(shape, dim):
    return lax.broadcasted_iota(jnp.int32, shape, dim)


def _dot(a, b, ca=1, cb=0):
    return lax.dot_general(a, b, (((ca,), (cb,)), ((), ())), preferred_element_type=F32)


def _dot3(x, ind):
    h = x.astype(BF16)
    r = x - h.astype(F32)
    m = r.astype(BF16)
    lo = (r - m.astype(F32)).astype(BF16)
    return _dot(h, ind) + _dot(m, ind) + _dot(lo, ind)


def _sigmoid(x):
    return jax.nn.sigmoid(x)


def _mm(a, b, *, name, ta=False, tb=False, bias=None, res=None, out_dtype=F32, b_koff=0, tm=None, tn=None, tk=None,
        dims=None, a_spec=None, b_spec=None, o_spec=None, o_shape=None, dep=None, more=(), target=None):
    if dims is not None:
        M, N, K = dims
    else:
        if ta:
            K, M = a.shape
        else:
            M, K = a.shape
        N = b.shape[0] if tb else b.shape[1]
    tm = tm or _pick(M, (1024, 1408, 512, 256, 128))
    tn = tn or _pick(N, (512, 1408, 256, 128))
    tk = tk or (K if K <= 2048 else _pick(K, (2048, 1408, 1024, 512)))
    assert M % tm == 0 and N % tn == 0 and K % tk == 0 and b_koff % tk == 0
    nk = K // tk
    kb0 = b_koff // tk
    has_bias, has_res = bias is not None, res is not None

    def body(*refs):
        a_ref, b_ref = refs[0], refs[1]
        pos = 2
        bias_ref = res_ref = acc_ref = None
        if has_bias:
            bias_ref = refs[pos]
            pos += 1
        if has_res:
            res_ref = refs[pos]
            pos += 1
        if dep is not None:
            pos += 1
        extra = refs[pos:pos + 2 * len(more)]
        pos += 2 * len(more)
        tgt_ref = lp_ref = None
        if target is not None:
            tgt_ref, o_ref, lp_ref = refs[pos], refs[pos + 1], refs[pos + 2]
            pos += 2
        else:
            o_ref = refs[pos]
        if nk > 1:
            acc_ref = refs[pos + 1]
        part = _dot(a_ref[...].astype(BF16), b_ref[...].astype(BF16), 0 if ta else 1, 1 if tb else 0)
        for q in range(len(more)):
            part = part + _dot(extra[2 * q][...].astype(BF16), extra[2 * q + 1][...].astype(BF16),
                               0 if ta else 1, 1 if tb else 0)

        def finish(acc):
            if has_bias:
                acc = acc + bias_ref[...]
            if has_res:
                acc = acc + res_ref[...]
            if target is not None:
                err = acc - tgt_ref[...]
                acc = err * (1.0 / N)
                part_loss = jnp.sum(jnp.sum(err * err, axis=1, keepdims=True), axis=0, keepdims=True) * (0.5 / N)
                first = (pl.program_id(0) == 0) & (pl.program_id(1) == 0)

                @pl.when(first)
                def _():
                    lp_ref[...] = jnp.broadcast_to(part_loss, lp_ref.shape)

                @pl.when(jnp.logical_not(first))
                def _():
                    lp_ref[...] += jnp.broadcast_to(part_loss, lp_ref.shape)

            o_ref[...] = acc.astype(out_dtype)

        if nk == 1:
            finish(part)
        else:
            k = pl.program_id(2)

            @pl.when(k == 0)
            def _():
                acc_ref[...] = part

            @pl.when(k > 0)
            def _():
                acc_ref[...] += part

            @pl.when(k == nk - 1)
            def _():
                finish(acc_ref[...])

    if a_spec is None:
        a_spec = pl.BlockSpec((tk, tm), lambda i, j, k: (k, i)) if ta else pl.BlockSpec((tm, tk), lambda i, j, k: (i, k))
    if b_spec is None:
        b_spec = (pl.BlockSpec((tn, tk), lambda i, j, k: (j, k + kb0)) if tb
                  else pl.BlockSpec((tk, tn), lambda i, j, k: (k + kb0, j)))
    if o_spec is None:
        o_spec = pl.BlockSpec((tm, tn), lambda i, j, k: (i, j))
    in_specs, args = [a_spec, b_spec], [a, b]
    if has_bias:
        in_specs.append(pl.BlockSpec((1, tn), lambda i, j, k: (0, j)))
        args.append(bias)
    if has_res:
        in_specs.append(pl.BlockSpec((tm, tn), lambda i, j, k: (i, j)))
        args.append(res)
    if dep is not None:
        in_specs.append(pl.BlockSpec(memory_space=pl.ANY))
        args.append(dep)
    for piece in more:
        a2, sa, b2, sb = piece if len(piece) == 4 else (a, piece[0], b, piece[1])
        in_specs += [sa, sb]
        args += [a2, b2]
    out_specs, out_shape = o_spec, jax.ShapeDtypeStruct(o_shape or (M, N), out_dtype)
    if target is not None:
        in_specs.append(pl.BlockSpec((tm, tn), lambda i, j, k: (i, j)))
        args.append(target)
        out_specs = [o_spec, pl.BlockSpec((8, 128), lambda i, j, k: (0, 0))]
        out_shape = [out_shape, jax.ShapeDtypeStruct((8, 128), F32)]
    return pl.pallas_call(
        body, name=name, grid=(M // tm, N // tn, nk), in_specs=in_specs, out_specs=out_specs, out_shape=out_shape,
        scratch_shapes=[pltpu.VMEM((tm, tn), F32)] if nk > 1 else [],
        compiler_params=_cp(3),
    )(*args)


def _rms_fwd(x, gains, *, name, tr=256, dep=None):
    S, D = x.shape
    n = len(gains)
    nd = 0 if dep is None else 1

    def body(*refs):
        xv = refs[0][...]
        xh = xv * lax.rsqrt(jnp.mean(xv * xv, axis=-1, keepdims=True) + EPS)
        for q in range(n):
            refs[1 + n + nd + q][...] = (xh * refs[1 + q][...]).astype(BF16)

    row = pl.BlockSpec((tr, D), lambda i: (i, 0))
    vec = pl.BlockSpec((1, D), lambda i: (0, 0))
    return pl.pallas_call(
        body, name=name, grid=(S // tr,), in_specs=[row] + [vec] * n + [pl.BlockSpec(memory_space=pl.ANY)] * nd,
        out_specs=[row] * n, out_shape=[jax.ShapeDtypeStruct((S, D), BF16)] * n, compiler_params=_cp(1),
    )(x, *gains, *([] if dep is None else [dep]))


def _rms_bwd(x, gains, dhs, dres, *, name, tr=256, want_colsum=False):
    S, D = x.shape
    n = len(gains)
    steps = S // tr

    def body(*refs):
        x_ref = refs[0]
        g_refs = refs[1:1 + n]
        dh_refs = refs[1 + n:1 + 2 * n]
        dres_ref = refs[1 + 2 * n]
        dx_ref = refs[2 + 2 * n]
        dg_refs = refs[3 + 2 * n:3 + 3 * n]
        cs_ref = refs[3 + 3 * n] if want_colsum else None
        i = pl.program_id(0)
        xv = x_ref[...]
        r = lax.rsqrt(jnp.mean(xv * xv, axis=-1, keepdims=True) + EPS)
        xh = xv * r
        dx = dres_ref[...]
        for q in range(n):
            dh = dh_refs[q][...]
            dxh = dh * g_refs[q][...]
            dx = dx + r * (dxh - xh * jnp.mean(dxh * xh, axis=-1, keepdims=True))
            part = jnp.sum(dh * xh, axis=0, keepdims=True)

            @pl.when(i == 0)
            def _():
                dg_refs[q][...] = part

            @pl.when(i > 0)
            def _():
                dg_refs[q][...] += part

        dx_ref[...] = dx
        if want_colsum:
            cpart = jnp.sum(dx, axis=0, keepdims=True)

            @pl.when(i == 0)
            def _():
                cs_ref[...] = cpart

            @pl.when(i > 0)
            def _():
                cs_ref[...] += cpart

    row = pl.BlockSpec((tr, D), lambda i: (i, 0))
    vec = pl.BlockSpec((1, D), lambda i: (0, 0))
    n_vec_out = n + (1 if want_colsum else 0)
    outs = pl.pallas_call(
        body, name=name, grid=(steps,), in_specs=[row] + [vec] * n + [row] * n + [row],
        out_specs=[row] + [vec] * n_vec_out,
        out_shape=[jax.ShapeDtypeStruct((S, D), F32)] + [jax.ShapeDtypeStruct((1, D), F32)] * n_vec_out,
        compiler_params=_cp(1),
    )(x, *gains, *dhs, dres)
    return outs


def _colsum(x, *, name, tr=256):
    S, D = x.shape

    def body(x_ref, o_ref):
        i = pl.program_id(0)
        part = jnp.sum(x_ref[...].astype(F32), axis=0, keepdims=True)

        @pl.when(i == 0)
        def _():
            o_ref[...] = part

        @pl.when(i > 0)
        def _():
            o_ref[...] += part

    return pl.pallas_call(
        body, name=name, grid=(S // tr,), in_specs=[pl.BlockSpec((tr, D), lambda i: (i, 0))],
        out_specs=pl.BlockSpec((1, D), lambda i: (0, 0)), out_shape=jax.ShapeDtypeStruct((1, D), F32),
        compiler_params=_cp(1),
    )(x)


STRIP = 64
HALO = 8


def _strips(S, tc):
    return [(r0, slice(l0, l0 + 128)) for l0 in range(0, tc, 128) for r0 in range(S - STRIP, -1, -STRIP)]


def _with_halo(ref, r0, ls):
    if r0 == 0:
        return jnp.concatenate([jnp.zeros((HALO, 128), F32), ref[0:STRIP, ls]], axis=0)
    return ref[r0 - HALO:r0 + STRIP, ls]


def _conv_strip(xw, w_ref, b_ref, ls, width):
    acc = b_ref[:, ls] + w_ref[pl.ds(width - 1, 1), ls] * xw[HALO:]
    shifted = []
    for s in range(1, width):
        xs = pltpu.roll(xw, s, axis=0)[HALO:]
        shifted.append(xs)
        acc = acc + w_ref[pl.ds(width - 1 - s, 1), ls] * xs
    return acc, shifted


def _conv_strip_back(dacc, after, xc, shifted, w_ref, ls, width):
    ext = jnp.concatenate([dacc, after], axis=0)
    dx = w_ref[pl.ds(width - 1, 1), ls] * dacc
    dws = [None] * width
    dws[width - 1] = jnp.sum(dacc * xc, axis=0, keepdims=True)
    for s in range(1, width):
        dx = dx + w_ref[pl.ds(width - 1 - s, 1), ls] * pltpu.roll(ext, STRIP + HALO - s, axis=0)[:STRIP]
        dws[width - 1 - s] = jnp.sum(dacc * shifted[s - 1], axis=0, keepdims=True)
    return dx, dws, jnp.sum(dacc, axis=0, keepdims=True)


def _conv_back_block(S, tc, width, w_ref, b_ref, x_ref, dacc_of, dx_store, dw_ref, db_ref):
    for l0 in range(0, tc, 128):
        ls = slice(l0, l0 + 128)
        after = jnp.zeros((HALO, 128), F32)
        tot = None
        for r0 in range(S - STRIP, -1, -STRIP):
            xw = _with_halo(x_ref, r0, ls)
            acc, shifted = _conv_strip(xw, w_ref, b_ref, ls, width)
            dacc = dacc_of(r0, ls, acc, _sigmoid(acc))
            dx, dws, db = _conv_strip_back(dacc, after, xw[HALO:], shifted, w_ref, ls, width)
            dx_store(r0, ls, dx)
            after = dacc[:HALO]
            part = dws + [db]
            tot = part if tot is None else [p + q for p, q in zip(tot, part)]
        for k in range(width):
            dw_ref[pl.ds(k, 1), ls] = tot[k]
        db_ref[:, ls] = tot[width]


def _conv_silu_fwd(xin, col0, C, w, b, *, name, tc=512):
    S = xin.shape[0]
    width = w.shape[0]
    off = col0 // tc

    def body(x_ref, w_ref, b_ref, o_ref):
        for r0, ls in _strips(S, tc):
            acc, _ = _conv_strip(_with_halo(x_ref, r0, ls), w_ref, b_ref, ls, width)
            o_ref[r0:r0 + STRIP, ls] = acc * _sigmoid(acc)

    return pl.pallas_call(
        body, name=name, grid=(C // tc,),
        in_specs=[pl.BlockSpec((S, tc), lambda j: (0, j + off)), pl.BlockSpec((width, tc), lambda j: (0, j)),
                  pl.BlockSpec((1, tc), lambda j: (0, j))],
        out_specs=pl.BlockSpec((S, tc), lambda j: (0, j)), out_shape=jax.ShapeDtypeStruct((S, C), F32),
        compiler_params=_cp(1),
    )(xin, w, b)


def _conv_silu_bwd(xin, col0, C, w, b, douts, *, name, tc=256):
    S = xin.shape[0]
    width = w.shape[0]
    off = col0 // tc
    nd = len(douts)
    ranges = [(o // tc, (o + d.shape[1]) // tc) for d, o in douts]

    def body(*refs):
        x_ref, w_ref, b_ref = refs[0], refs[1], refs[2]
        d_refs = refs[3:3 + nd]
        dx_ref, dw_ref, db_ref = refs[3 + nd], refs[4 + nd], refs[5 + nd]
        j = pl.program_id(0)

        def dacc_of(r0, ls, acc, sg):
            dout = jnp.zeros((STRIP, 128), F32)
            for q in range(nd):
                lo, hi = ranges[q]
                dout = dout + jnp.where((j >= lo) & (j < hi), d_refs[q][r0:r0 + STRIP, ls], 0.0)
            return dout * (sg * (1.0 + acc * (1.0 - sg)))

        def dx_store(r0, ls, dx):
            dx_ref[r0:r0 + STRIP, ls] = dx.astype(BF16)

        _conv_back_block(S, tc, width, w_ref, b_ref, x_ref, dacc_of, dx_store, dw_ref, db_ref)

    d_specs = [pl.BlockSpec((S, tc), (lambda j, lo=lo, hi=hi: (0, jnp.clip(j - lo, 0, hi - lo - 1)))) for lo, hi in ranges]
    return pl.pallas_call(
        body, name=name, grid=(C // tc,),
        in_specs=[pl.BlockSpec((S, tc), lambda j: (0, j + off)), pl.BlockSpec((width, tc), lambda j: (0, j)),
                  pl.BlockSpec((1, tc), lambda j: (0, j))] + d_specs,
        out_specs=[pl.BlockSpec((S, tc), lambda j: (0, j)), pl.BlockSpec((width, tc), lambda j: (0, j)),
                   pl.BlockSpec((1, tc), lambda j: (0, j))],
        out_shape=[jax.ShapeDtypeStruct((S, C), BF16), jax.ShapeDtypeStruct((width, C), F32),
                   jax.ShapeDtypeStruct((1, C), F32)],
        compiler_params=_cp(1),
    )(xin, w, b, *[d for d, _ in douts])


def _ffn_act_fwd(u, w, b, *, name, tc=256):
    S, F2 = u.shape
    Fd = F2 // 2
    width = w.shape[0]
    nb = Fd // tc

    def body(g_ref, v_ref, w_ref, b_ref, o_ref):
        for r0, ls in _strips(S, tc):
            acc, _ = _conv_strip(_with_halo(g_ref, r0, ls), w_ref, b_ref, ls, width)
            o_ref[r0:r0 + STRIP, ls] = (acc * _sigmoid(acc) * v_ref[r0:r0 + STRIP, ls]).astype(BF16)

    return pl.pallas_call(
        body, name=name, grid=(nb,),
        in_specs=[pl.BlockSpec((S, tc), lambda j: (0, j)), pl.BlockSpec((S, tc), lambda j: (0, j + nb)),
                  pl.BlockSpec((width, tc), lambda j: (0, j)), pl.BlockSpec((1, tc), lambda j: (0, j))],
        out_specs=pl.BlockSpec((S, tc), lambda j: (0, j)), out_shape=jax.ShapeDtypeStruct((S, Fd), BF16),
        compiler_params=_cp(1),
    )(u, u, w, b)


def _ffn_act_bwd(u, w, b, da, *, name, tc=256):
    S, F2 = u.shape
    Fd = F2 // 2
    width = w.shape[0]
    nb = Fd // tc

    def body(g_ref, v_ref, w_ref, b_ref, da_ref, du_ref, dw_ref, db_ref, a_ref):
        def dacc_of(r0, ls, acc, sg):
            rs = slice(r0, r0 + STRIP)
            dav, val, silu = da_ref[rs, ls], v_ref[rs, ls], acc * sg
            a_ref[rs, ls] = (silu * val).astype(BF16)
            du_ref[1, rs, ls] = (dav * silu).astype(BF16)
            return dav * val * (sg * (1.0 + acc * (1.0 - sg)))

        def dx_store(r0, ls, dx):
            du_ref[0, r0:r0 + STRIP, ls] = dx.astype(BF16)

        _conv_back_block(S, tc, width, w_ref, b_ref, g_ref, dacc_of, dx_store, dw_ref, db_ref)

    blk = pl.BlockSpec((S, tc), lambda j: (0, j))
    return pl.pallas_call(
        body, name=name, grid=(nb,),
        in_specs=[blk, pl.BlockSpec((S, tc), lambda j: (0, j + nb)), pl.BlockSpec((width, tc), lambda j: (0, j)),
                  pl.BlockSpec((1, tc), lambda j: (0, j)), blk],
        out_specs=[pl.BlockSpec((2, S, tc), lambda j: (0, 0, j)), pl.BlockSpec((width, tc), lambda j: (0, j)),
                   pl.BlockSpec((1, tc), lambda j: (0, j)), blk],
        out_shape=[jax.ShapeDtypeStruct((2, S, Fd), BF16),
                   jax.ShapeDtypeStruct((width, Fd), F32), jax.ShapeDtypeStruct((1, Fd), F32),
                   jax.ShapeDtypeStruct((S, Fd), BF16)],
        compiler_params=_cp(1),
    )(u, u, w, b, da)


def _ssd_prep(dtr, dt_bias, a_log, *, name="ssd_prep"):
    S = dtr.shape[0]

    def body(d_ref, b_ref, al_ref, dt_ref, ac_ref, sg_ref, act_ref):
        lane = _iota((CHUNK, 128), 1)
        valid = lane < SSM_HEADS
        z = d_ref[...] + b_ref[...]
        dt = jnp.where(valid, jnp.maximum(z, 0.0) + jnp.log(1.0 + jnp.exp(-jnp.abs(z))), 0.0)
        a = dt * (-jnp.exp(al_ref[...]))
        row = _iota((CHUNK, 128), 0)
        k = 1
        while k < CHUNK:
            a = a + jnp.where(row >= k, pltpu.roll(a, k, axis=0), 0.0)
            k *= 2
        sg = jnp.where(valid, _sigmoid(z), 0.0)
        for arr, ref in ((dt, dt_ref), (a, ac_ref), (sg, sg_ref)):
            for g in range(SSM_GROUPS):
                ref[g] = jnp.where(lane < 4, arr if g == 0 else pltpu.roll(arr, 128 - 4 * g, axis=1), 0.0)
        act_ref[...] = a.T[:SSM_HEADS, :]

    blk = pl.BlockSpec((CHUNK, 128), lambda i: (i, 0))
    vec = pl.BlockSpec((1, 128), lambda i: (0, 0))
    grp = pl.BlockSpec((SSM_GROUPS, CHUNK, 128), lambda i: (0, i, 0))
    return pl.pallas_call(
        body, name=name, grid=(S // CHUNK,), in_specs=[blk, vec, vec],
        out_specs=[grp, grp, grp, pl.BlockSpec((SSM_HEADS, CHUNK), lambda i: (0, i))],
        out_shape=[jax.ShapeDtypeStruct((SSM_GROUPS, S, 128), F32)] * 3 + [jax.ShapeDtypeStruct((SSM_HEADS, S), F32)],
        compiler_params=_cp(1),
    )(dtr, dt_bias, a_log)


SSD_GPS = 4


def _expand4(v, lanes):
    out = jnp.broadcast_to(v[:, 3:4], lanes.shape)
    for hh in (2, 1, 0):
        out = jnp.where(lanes < 64 * (hh + 1), v[:, hh:hh + 1], out)
    return out


def _ssd_fwd(xbc, dt_g, ac_g, ac_t, *, name="ssd_fwd"):
    S = xbc.shape[0]
    nc = S // CHUNK
    Lc = CHUNK

    def body(x_ref, b_ref, c_ref, dt_ref, ac_ref, act_ref, y_ref, st_out_ref, st_ref):
        g2 = pl.program_id(0)
        c = pl.program_id(1)

        @pl.when(c == 0)
        def _():
            st_ref[...] = jnp.zeros_like(st_ref)

        causal = _iota((Lc, Lc), 0) >= _iota((Lc, Lc), 1)
        lane256 = _iota((Lc, 256), 1)
        lane128 = _iota((Lc, 128), 1)
        row128 = _iota((128, 128), 0)
        for gg in range(SSD_GPS):
            g = SSD_GPS * g2 + gg
            bv = b_ref[:, 128 * gg:128 * (gg + 1)]
            cbf = c_ref[:, 128 * gg:128 * (gg + 1)].astype(BF16)
            cb = _dot(cbf, bv.astype(BF16), 1, 1)
            dtg, acg = dt_ref[gg], ac_ref[gg]
            ac_last = ac_ref[gg, pl.ds(Lc - 1, 1), :]
            dt4 = _expand4(dtg, lane256)
            ac4 = _expand4(acg, lane256)
            e4 = jnp.exp(ac4)
            xdb = (x_ref[:, 256 * gg:256 * (gg + 1)] * dt4).astype(BF16)
            st_out_ref[gg] = st_ref[gg]
            for p in range(2):
                xd_p = xdb[:, 128 * p:128 * (p + 1)]
                st_p = st_ref[gg, p]
                ys, sn, cds = [], [], []
                for q in range(2):
                    hh = 2 * p + q
                    a_col = acg[:, hh:hh + 1]
                    a_row = act_ref[pl.ds(4 * g + hh, 1), :]
                    dec = jnp.exp(jnp.where(causal, a_col - a_row, NEG))
                    w = (cb * dec).astype(BF16)
                    ys.append(_dot(w, xd_p))
                    al = ac_last[:, hh:hh + 1]
                    dte = jnp.exp(al - a_col)
                    sn.append(_dot(xd_p, (bv * dte).astype(BF16), 0, 0))
                    cds.append(jnp.exp(al))
                y_diag = jnp.where(lane128 < 64, ys[0], ys[1])
                y_off = _dot(cbf, st_p.astype(BF16), 1, 1) * e4[:, 128 * p:128 * (p + 1)]
                y_ref[:, 256 * gg + 128 * p:256 * gg + 128 * (p + 1)] = y_diag + y_off
                st_ref[gg, p] = jnp.where(row128 < 64, st_p * cds[0] + sn[0], st_p * cds[1] + sn[1])

    G = SSD_GPS
    per_g = lambda g, c: (g, c, 0)
    return pl.pallas_call(
        body, name=name, grid=(SSM_GROUPS // G, nc),
        in_specs=[pl.BlockSpec((Lc, 256 * G), lambda g, c: (c, g)),
                  pl.BlockSpec((Lc, 128 * G), lambda g, c: (c, 16 // G + g)),
                  pl.BlockSpec((Lc, 128 * G), lambda g, c: (c, 24 // G + g)),
                  pl.BlockSpec((G, Lc, 128), per_g), pl.BlockSpec((G, Lc, 128), per_g),
                  pl.BlockSpec((SSM_HEADS, Lc), lambda g, c: (0, c))],
        out_specs=[pl.BlockSpec((Lc, 256 * G), lambda g, c: (c, g)),
                   pl.BlockSpec((G, None, 2, 128, 128), lambda g, c: (g, c, 0, 0, 0))],
        out_shape=[jax.ShapeDtypeStruct((S, 2048), F32), jax.ShapeDtypeStruct((SSM_GROUPS, nc, 2, 128, 128), F32)],
        scratch_shapes=[pltpu.VMEM((G, 2, 128, 128), F32)], compiler_params=_cp(2),
    )(xbc, xbc, xbc, dt_g, ac_g, ac_t)


def _ssd_bwd(xbc, dt_g, ac_g, ac_t, states, dy, dexp, *, name="ssd_bwd", dep=None):
    S = xbc.shape[0]
    nc = S // CHUNK
    Lc = CHUNK

    def body(x_ref, b_ref, c_ref, dt_ref, ac_ref, act_ref, st_ref, dy_ref, d_ref, *rest):
        dx_ref, db_ref, dc_ref, dh_ref, ds_ref = rest[-5:]
        g2 = pl.program_id(0)
        cc = pl.program_id(1)

        @pl.when(cc == 0)
        def _():
            ds_ref[...] = jnp.zeros_like(ds_ref)

        causal = _iota((Lc, Lc), 0) >= _iota((Lc, Lc), 1)
        lane256 = _iota((Lc, 256), 1)
        lane128 = _iota((Lc, 128), 1)
        row128 = _iota((128, 128), 0)
        ind_rows = _iota((256, 128), 0) >> 6
        ind_cols = _iota((256, 128), 1)
        ind_a = (ind_rows == ind_cols).astype(BF16)
        ind_b = (ind_rows + 4 == ind_cols).astype(BF16)
        for gg in range(SSD_GPS):
            g = SSD_GPS * g2 + gg
            bv = b_ref[:, 128 * gg:128 * (gg + 1)]
            cv = c_ref[:, 128 * gg:128 * (gg + 1)]
            bbf, cbf = bv.astype(BF16), cv.astype(BF16)
            cb = _dot(cbf, bbf, 1, 1)
            dtg, acg = dt_ref[gg], ac_ref[gg]
            ac_last = ac_ref[gg, pl.ds(Lc - 1, 1), :]
            dt4 = _expand4(dtg, lane256)
            ac4 = _expand4(acg, lane256)
            acl4 = _expand4(ac_last, _iota((1, 256), 1))
            e4 = jnp.exp(ac4)
            dte4 = jnp.exp(acl4 - ac4)
            xv = x_ref[:, 256 * gg:256 * (gg + 1)]
            xd = xv * dt4
            xdb = xd.astype(BF16)
            dyv = dy_ref[:, 256 * gg:256 * (gg + 1)]
            dcb = jnp.zeros((Lc, Lc), F32)
            dc_acc = jnp.zeros((Lc, 128), F32)
            db_acc = jnp.zeros((Lc, 128), F32)
            u_parts, dxd_parts, ends = [], [], []
            for p in range(2):
                sl = slice(128 * p, 128 * (p + 1))
                xd_p, xdb_p, dy_p = xd[:, sl], xdb[:, sl], dyv[:, sl]
                dyb_p = dy_p.astype(BF16)
                e_p, dte_p = e4[:, sl], dte4[:, sl]
                sp = st_ref[gg, p]
                spb = sp.astype(BF16)
                dsn = ds_ref[gg, p]
                dsnb = dsn.astype(BF16)
                yds, dxds, cds = [], [], []
                for q in range(2):
                    hh = 2 * p + q
                    a_col = acg[:, hh:hh + 1]
                    a_row = act_ref[pl.ds(4 * g + hh, 1), :]
                    dec = jnp.exp(jnp.where(causal, a_col - a_row, NEG))
                    w = (cb * dec).astype(BF16)
                    head = (lane128 < 64) if q == 0 else (lane128 >= 64)
                    dym = jnp.where(head, dyb_p, jnp.zeros_like(dyb_p))
                    dw = _dot(dym, xdb_p, 1, 1)
                    dcb = dcb + dw * dec
                    yds.append(_dot(w, xdb_p))
                    dxds.append(_dot(w, dyb_p, 0, 0))
                    cds.append(jnp.exp(ac_last[:, hh:hh + 1]))
                y_diag = jnp.where(lane128 < 64, yds[0], yds[1])
                dxd_diag = jnp.where(lane128 < 64, dxds[0], dxds[1])
                y_off = _dot(cbf, spb, 1, 1) * e_p
                dgp = dy_p * e_p
                dgb = dgp.astype(BF16)
                dc_acc = dc_acc + _dot(dgb, spb)
                dsp = _dot(dgb, cbf, 0, 0)
                cd_col = jnp.where(row128[:, 0:1] < 64, cds[0], cds[1])
                qm = _dot(bbf, dsnb, 1, 1)
                dxd_state = dte_p * qm
                db_acc = db_acc + _dot((xd_p * dte_p).astype(BF16), dsnb)
                t_p = xd_p * dxd_state
                prod = dsn * sp
                e0 = jnp.sum(jnp.sum(jnp.where(row128 < 64, prod, 0.0), axis=1, keepdims=True), axis=0, keepdims=True)
                e1 = jnp.sum(jnp.sum(jnp.where(row128 >= 64, prod, 0.0), axis=1, keepdims=True), axis=0, keepdims=True)
                tcol = jnp.sum(t_p, axis=0, keepdims=True)
                lane1 = _iota((1, 128), 1)
                t0 = jnp.sum(jnp.where(lane1 < 64, tcol, 0.0), axis=1, keepdims=True)
                t1 = jnp.sum(jnp.where(lane1 >= 64, tcol, 0.0), axis=1, keepdims=True)
                ends.append(e0 * cds[0] + t0)
                ends.append(e1 * cds[1] + t1)
                ds_ref[gg, p] = dsn * cd_col + dsp
                u_parts.append(dyb_p.astype(F32) * y_diag - xdb_p.astype(F32) * dxd_diag + dy_p * y_off - t_p)
                dxd_parts.append(dxd_diag + dxd_state)
            dxd = jnp.concatenate(dxd_parts, axis=1)
            u_all = jnp.concatenate(u_parts, axis=1)
            dx_ref[:, 256 * gg:256 * (gg + 1)] = dxd * dt4 + dyv * d_ref[:, 256 * gg:256 * (gg + 1)]
            dcbb = dcb.astype(BF16)
            dc_ref[:, 128 * gg:128 * (gg + 1)] = dc_acc + _dot(dcbb, bbf)
            db_ref[:, 128 * gg:128 * (gg + 1)] = db_acc + _dot(dcbb, cbf, 0, 0)
            lane = _iota((Lc, 128), 1)
            endv = jnp.zeros((Lc, 128), F32)
            for hh in range(4):
                endv = jnp.where(lane == 8 + hh, ends[hh], endv)
            dh_ref[gg] = _dot3(dxd * xv, ind_a) + _dot3(u_all, ind_b) + endv

    G = SSD_GPS
    rev = lambda c: nc - 1 - c
    per_g = lambda g, c: (g, rev(c), 0)
    return pl.pallas_call(
        body, name=name, grid=(SSM_GROUPS // G, nc),
        in_specs=[pl.BlockSpec((Lc, 256 * G), lambda g, c: (rev(c), g)),
                  pl.BlockSpec((Lc, 128 * G), lambda g, c: (rev(c), 16 // G + g)),
                  pl.BlockSpec((Lc, 128 * G), lambda g, c: (rev(c), 24 // G + g)),
                  pl.BlockSpec((G, Lc, 128), per_g), pl.BlockSpec((G, Lc, 128), per_g),
                  pl.BlockSpec((SSM_HEADS, Lc), lambda g, c: (0, rev(c))),
                  pl.BlockSpec((G, None, 2, 128, 128), lambda g, c: (g, rev(c), 0, 0, 0)),
                  pl.BlockSpec((Lc, 256 * G), lambda g, c: (rev(c), g)),
                  pl.BlockSpec((1, 256 * G), lambda g, c: (0, g))] + ([] if dep is None else [pl.BlockSpec(memory_space=pl.ANY)]),
        out_specs=[pl.BlockSpec((Lc, 256 * G), lambda g, c: (rev(c), g)),
                   pl.BlockSpec((Lc, 128 * G), lambda g, c: (rev(c), g)),
                   pl.BlockSpec((Lc, 128 * G), lambda g, c: (rev(c), g)),
                   pl.BlockSpec((G, Lc, 128), per_g)],
        out_shape=[jax.ShapeDtypeStruct((S, 2048), F32), jax.ShapeDtypeStruct((S, 1024), F32),
                   jax.ShapeDtypeStruct((S, 1024), F32), jax.ShapeDtypeStruct((SSM_GROUPS, S, 128), F32)],
        scratch_shapes=[pltpu.VMEM((G, 2, 128, 128), F32)], compiler_params=_cp(2),
    )(xbc, xbc, xbc, dt_g, ac_g, ac_t, states, dy, dexp, *([] if dep is None else [dep]))


def _ssd_post(dhead, dt_g, sg_g, alog_g, *, name="ssd_post"):
    S = dhead.shape[1]
    nc = S // CHUNK
    Lc = CHUNK

    def body(dh_ref, dt_ref, sg_ref, al_ref, o_ref, s_ref):
        @pl.when(pl.program_id(0) == 0)
        def _():
            s_ref[...] = jnp.zeros_like(s_ref)

        lane = _iota((Lc, 128), 1)
        row = _iota((Lc, 128), 0)
        row8 = _iota((8, 128), 0)
        out = jnp.zeros((Lc, 128), F32)
        for g in range(SSM_GROUPS):
            dh = dh_ref[g]
            a_neg = -jnp.exp(al_ref[g])
            dac = jnp.where(lane < 4, pltpu.roll(dh, 124, axis=1), 0.0)
            end = jnp.where(lane < 4, pltpu.roll(dh, 120, axis=1), 0.0)
            k = 1
            while k < Lc:
                dac = dac + jnp.where(row < Lc - k, pltpu.roll(dac, Lc - k, axis=0), 0.0)
                k *= 2
            da = dac + end
            ddt = jnp.where(lane < 4, da * a_neg + dh, 0.0)
            ddtr = ddt * sg_ref[g]
            out = out + (ddtr if g == 0 else pltpu.roll(ddtr, 4 * g, axis=1))
            dal = jnp.sum(da * dt_ref[g], axis=0, keepdims=True) * a_neg
            dbias = jnp.sum(ddtr, axis=0, keepdims=True)
            part = jnp.where(row8 == 0, dal, jnp.where(row8 == 1, dbias, 0.0))
            s_ref[g] += part
        o_ref[...] = out.astype(BF16)

    grp = pl.BlockSpec((SSM_GROUPS, Lc, 128), lambda c: (0, c, 0))
    whole = lambda r: pl.BlockSpec((SSM_GROUPS, r, 128), lambda c: (0, 0, 0))
    return pl.pallas_call(
        body, name=name, grid=(nc,), in_specs=[grp, grp, grp, whole(1)],
        out_specs=[pl.BlockSpec((Lc, 128), lambda c: (c, 0)), whole(8)],
        out_shape=[jax.ShapeDtypeStruct((S, 128), BF16), jax.ShapeDtypeStruct((SSM_GROUPS, 8, 128), F32)],
        compiler_params=_cp(1),
    )(dhead, dt_g, sg_g, alog_g)


def _gate_fwd(y, xbc, zx, dexp, gn, *, name="gate_fwd", tr=256):
    S = y.shape[0]
    W = 2048
    gw = W // SSM_GROUPS

    def body(y_ref, x_ref, z_ref, d_ref, g_ref, o_ref):
        z = z_ref[...]
        u = (y_ref[...] + x_ref[...] * d_ref[...]) * (z * _sigmoid(z))
        gv = g_ref[...]
        for q in range(SSM_GROUPS):
            sl = slice(gw * q, gw * (q + 1))
            uq = u[:, sl]
            r = lax.rsqrt(jnp.mean(uq * uq, axis=-1, keepdims=True) + EPS)
            o_ref[:, sl] = (uq * r * gv[:, sl]).astype(BF16)

    row = pl.BlockSpec((tr, W), lambda i: (i, 0))
    vec = pl.BlockSpec((1, W), lambda i: (0, 0))
    return pl.pallas_call(
        body, name=name, grid=(S // tr,), in_specs=[row, row, row, vec, vec], out_specs=row,
        out_shape=jax.ShapeDtypeStruct((S, W), BF16), compiler_params=_cp(1),
    )(y, xbc, zx, dexp, gn)


def _gate_bwd(y, xbc, zx, dexp, gn, dout, *, name="gate_bwd", tr=256):
    S = y.shape[0]
    W = 2048
    gw = W // SSM_GROUPS
    steps = S // tr

    def body(y_ref, x_ref, z_ref, d_ref, g_ref, do_ref, dy_ref, dz_ref, dg_ref, dd_ref, acc_ref):
        i = pl.program_id(0)

        @pl.when(i == 0)
        def _():
            acc_ref[...] = jnp.zeros_like(acc_ref)

        z = z_ref[...]
        sg = _sigmoid(z)
        sz = z * sg
        xs = x_ref[...]
        yt = y_ref[...] + xs * d_ref[...]
        u = yt * sz
        gv = g_ref[...]
        do = do_ref[...]
        dgs = []
        for q in range(SSM_GROUPS):
            sl = slice(gw * q, gw * (q + 1))
            uq = u[:, sl]
            r = lax.rsqrt(jnp.mean(uq * uq, axis=-1, keepdims=True) + EPS)
            uh = uq * r
            dq = do[:, sl]
            duh = dq * gv[:, sl]
            duq = r * (duh - uh * jnp.mean(duh * uh, axis=-1, keepdims=True))
            dgs.append(jnp.sum(dq * uh, axis=0, keepdims=True))
            dyt = duq * sz[:, sl]
            dy_ref[:, sl] = dyt
            dz_ref[:, sl] = (duq * yt[:, sl] * (sg[:, sl] * (1.0 + z[:, sl] * (1.0 - sg[:, sl])))).astype(BF16)
            acc_ref[:, sl] += jnp.sum(dyt * xs[:, sl], axis=0, keepdims=True)
        dg = jnp.concatenate(dgs, axis=1)

        @pl.when(i == 0)
        def _():
            dg_ref[...] = dg

        @pl.when(i > 0)
        def _():
            dg_ref[...] += dg

        @pl.when(i == steps - 1)
        def _():
            ind = ((_iota((W, 128), 0) >> 6) == _iota((W, 128), 1)).astype(BF16)
            dd_ref[...] = _dot3(jnp.broadcast_to(acc_ref[...], (8, W)), ind)[0:1, :]

    row = pl.BlockSpec((tr, W), lambda i: (i, 0))
    vec = pl.BlockSpec((1, W), lambda i: (0, 0))
    return pl.pallas_call(
        body, name=name, grid=(steps,), in_specs=[row, row, row, vec, vec, row],
        out_specs=[row, row, vec, pl.BlockSpec((1, 128), lambda i: (0, 0))],
        out_shape=[jax.ShapeDtypeStruct((S, W), F32), jax.ShapeDtypeStruct((S, W), BF16),
                   jax.ShapeDtypeStruct((1, W), F32), jax.ShapeDtypeStruct((1, 128), F32)],
        scratch_shapes=[pltpu.VMEM((1, W), F32)], compiler_params=_cp(1),
    )(y, xbc, zx, dexp, gn, dout)


def _rope_cs(posf, *, name="rope_tables", tr=256):
    S = posf.shape[0]

    def body(p_ref, c_ref, s_ref):
        j = (_iota((tr, 128), 1) & 31).astype(F32)
        ang = p_ref[...] * jnp.exp(j * (-math.log(ROPE_THETA) / 32.0))
        c_ref[...] = jnp.cos(ang)
        s_ref[...] = jnp.sin(ang)

    blk = pl.BlockSpec((tr, 128), lambda i: (i, 0))
    return pl.pallas_call(
        body, name=name, grid=(S // tr,), in_specs=[pl.BlockSpec((tr, 1), lambda i: (i, 0))], out_specs=[blk, blk],
        out_shape=[jax.ShapeDtypeStruct((S, 128), F32)] * 2, compiler_params=_cp(1),
    )(posf)


def _rope_tables(c_ref, s_ref, shape):
    reps = shape[1] // 128
    return jnp.tile(c_ref[...], (1, reps)), jnp.tile(s_ref[...], (1, reps)), (_iota(shape, 1) & 63) < 32


def _hn_inds(W):
    ind = ((_iota((W, 128), 0) >> 6) == _iota((W, 128), 1)).astype(BF16)
    ind_t = ((_iota((128, W), 1) >> 6) == _iota((128, W), 0)).astype(BF16)
    return ind, ind_t


def _hnrope_fwd(xin, col0, W, gain_w, rope, *, name, tr=256):
    S = xin.shape[0]
    off = col0 // W
    nh = W // HEAD

    def body(x_ref, g_ref, c_ref, s_ref, o_ref):
        x = x_ref[...]
        ind, ind_t = _hn_inds(W)
        r = lax.rsqrt(_dot3(x * x, ind) * (1.0 / HEAD) + EPS)
        xn = x * _dot3(r, ind_t) * g_ref[...]
        cs, sn, half = _rope_tables(c_ref, s_ref, (tr, W))
        rot = jnp.where(half, -pltpu.roll(xn, W - 32, axis=1), pltpu.roll(xn, 32, axis=1))
        out = (xn * cs + rot * sn).astype(BF16)
        for h in range(nh):
            o_ref[h] = out[:, HEAD * h:HEAD * (h + 1)]

    tab = pl.BlockSpec((tr, 128), lambda i: (i, 0))
    return pl.pallas_call(
        body, name=name, grid=(S // tr,),
        in_specs=[pl.BlockSpec((tr, W), lambda i: (i, off)), pl.BlockSpec((1, W), lambda i: (0, 0)), tab, tab],
        out_specs=pl.BlockSpec((nh, tr, HEAD), lambda i: (0, i, 0)), out_shape=jax.ShapeDtypeStruct((nh, S, HEAD), BF16),
        compiler_params=_cp(1),
    )(xin, gain_w, *rope)


def _hnrope_bwd(xin, col0, W, gain_w, rope, dout, *, name, tr=256):
    S = xin.shape[0]
    off = col0 // W
    steps = S // tr
    nh = W // HEAD

    def body(x_ref, g_ref, c_ref, s_ref, do_ref, dx_ref, cs_ref, dg_ref, acc_ref):
        i = pl.program_id(0)
        x = x_ref[...]
        ind, ind_t = _hn_inds(W)
        r = lax.rsqrt(_dot3(x * x, ind) * (1.0 / HEAD) + EPS)
        rw = _dot3(r, ind_t)
        xh = x * rw
        cs, sn, half = _rope_tables(c_ref, s_ref, (tr, W))
        do = jnp.concatenate([do_ref[h] for h in range(nh)], axis=1).astype(F32)
        gs = do * sn
        g1 = do * cs + jnp.where(half, pltpu.roll(gs, W - 32, axis=1), -pltpu.roll(gs, 32, axis=1))
        dxh = g1 * g_ref[...]
        t = _dot3(dxh * xh, ind) * (1.0 / HEAD)
        dx = rw * (dxh - xh * _dot3(t, ind_t))
        dx_ref[...] = dx.astype(BF16)
        cpart = jnp.sum(dx, axis=0, keepdims=True)
        gpart = jnp.sum(g1 * xh, axis=0, keepdims=True)

        @pl.when(i == 0)
        def _():
            cs_ref[...] = cpart
            acc_ref[...] = gpart

        @pl.when(i > 0)
        def _():
            cs_ref[...] += cpart
            acc_ref[...] += gpart

        @pl.when(i == steps - 1)
        def _():
            fold = ((_iota((W, 128), 0) & 63) == _iota((W, 128), 1)).astype(BF16)
            dg_ref[...] = _dot3(jnp.broadcast_to(acc_ref[...], (8, W)), fold)[0:1, :]

    tab = pl.BlockSpec((tr, 128), lambda i: (i, 0))
    return pl.pallas_call(
        body, name=name, grid=(steps,),
        in_specs=[pl.BlockSpec((tr, W), lambda i: (i, off)), pl.BlockSpec((1, W), lambda i: (0, 0)), tab, tab,
                  pl.BlockSpec((nh, tr, HEAD), lambda i: (0, i, 0))],
        out_specs=[pl.BlockSpec((tr, W), lambda i: (i, 0)), pl.BlockSpec((1, W), lambda i: (0, 0)),
                   pl.BlockSpec((1, 128), lambda i: (0, 0))],
        out_shape=[jax.ShapeDtypeStruct((S, W), BF16), jax.ShapeDtypeStruct((1, W), F32),
                   jax.ShapeDtypeStruct((1, 128), F32)],
        scratch_shapes=[pltpu.VMEM((1, W), F32)], compiler_params=_cp(1),
    )(xin, gain_w, *rope, dout)


def _attn_band():
    qi = jnp.arange(ATT_G * WINDOW)[:, None] % WINDOW
    ki = jnp.arange(2 * WINDOW)[None, :]
    rel = qi + WINDOW - ki
    ok = (rel >= 0) & (rel < WINDOW)
    return jnp.stack([jnp.where(ok & (ki >= WINDOW), 0.0, NEG), jnp.where(ok, 0.0, NEG)]).astype(F32)


def _attn_probs(q, kb, sink_ref, band_ref, h, i):
    s = _dot(q, kb, 1, 1) * (HEAD ** -0.5) + band_ref[jnp.minimum(i, 1)]
    r1 = _iota((4 * WINDOW, 1), 0)
    sink = jnp.where(r1 < WINDOW, sink_ref[4 * h], jnp.where(r1 < 2 * WINDOW, sink_ref[4 * h + 1],
                     jnp.where(r1 < 3 * WINDOW, sink_ref[4 * h + 2], sink_ref[4 * h + 3])))
    m = jnp.maximum(jnp.max(s, axis=1, keepdims=True), sink)
    p = jnp.exp(s - m)
    ps = jnp.exp(sink - m)
    inv = 1.0 / (jnp.sum(p, axis=1, keepdims=True) + ps)
    return p * inv, ps * inv


ATT_HPS = 4
_BAND = pl.BlockSpec((2, ATT_G * WINDOW, 2 * WINDOW), lambda h, i: (0, 0, 0))


def _attn_specs(S):
    qspec = pl.BlockSpec((ATT_HPS, ATT_G, WINDOW, HEAD), lambda h, i: (h, 0, i, 0))
    cur = pl.BlockSpec((ATT_HPS, WINDOW, HEAD), lambda h, i: (h, i, 0))
    prev = pl.BlockSpec((ATT_HPS, WINDOW, HEAD), lambda h, i: (h, jnp.maximum(i - 1, 0), 0))
    tok = pl.BlockSpec((WINDOW, ATT_HPS * ATT_G * HEAD), lambda h, i: (i, h))
    return qspec, cur, prev, tok


def _attn_fwd(qh, kh, vh, sinks, *, name="attn_fwd"):
    S = kh.shape[1]
    nb = S // WINDOW

    def body(s_ref, band_ref, q_ref, kc_ref, kp_ref, vc_ref, vp_ref, o_ref):
        h2, i = pl.program_id(0), pl.program_id(1)
        outs = []
        for hh in range(ATT_HPS):
            q = q_ref[hh].reshape(ATT_G * WINDOW, HEAD)
            kb = jnp.concatenate([kp_ref[hh], kc_ref[hh]], axis=0)
            vb = jnp.concatenate([vp_ref[hh], vc_ref[hh]], axis=0)
            probs, _ = _attn_probs(q, kb, s_ref, band_ref, ATT_HPS * h2 + hh, i)
            o = _dot(probs.astype(BF16), vb).astype(BF16)
            outs += [o[WINDOW * g:WINDOW * (g + 1)] for g in range(ATT_G)]
        o_ref[...] = jnp.concatenate(outs, axis=1)

    qspec, cur, prev, tok = _attn_specs(S)
    return pl.pallas_call(
        body, name=name, grid=(ATT_KV // ATT_HPS, nb),
        in_specs=[pl.BlockSpec(memory_space=pltpu.SMEM), _BAND, qspec, cur, prev, cur, prev], out_specs=tok,
        out_shape=jax.ShapeDtypeStruct((S, ATT_KV * ATT_G * HEAD), BF16), compiler_params=_cp(2),
    )(sinks, _attn_band(), qh, kh, kh, vh, vh)


def _attn_bwd(qh, kh, vh, sinks, doh, *, name="attn_bwd"):
    S = kh.shape[1]
    nb = S // WINDOW

    def body(s_ref, band_ref, q_ref, kc_ref, kp_ref, vc_ref, vp_ref, do_ref, dq_ref, dk_ref, dv_ref, dsk_ref):
        h2, i = pl.program_id(0), pl.program_id(1)

        @pl.when(i == 0)
        def _():
            dk_ref[...] = jnp.zeros_like(dk_ref)
            dv_ref[...] = jnp.zeros_like(dv_ref)
            dsk_ref[...] = jnp.zeros_like(dsk_ref)

        dov = do_ref[...]
        cur = pl.multiple_of(i * WINDOW, WINDOW)
        lane = _iota((8, 128), 1)
        row = _iota((8, 128), 0)
        scale = HEAD ** -0.5
        for hh in range(ATT_HPS):
            q = q_ref[hh].reshape(ATT_G * WINDOW, HEAD)
            do = jnp.concatenate([dov[:, HEAD * (ATT_G * hh + g):HEAD * (ATT_G * hh + g + 1)] for g in range(ATT_G)], axis=0)
            kb = jnp.concatenate([kp_ref[hh], kc_ref[hh]], axis=0)
            vb = jnp.concatenate([vp_ref[hh], vc_ref[hh]], axis=0)
            probs, psink = _attn_probs(q, kb, s_ref, band_ref, ATT_HPS * h2 + hh, i)
            dp = _dot(do, vb, 1, 1)
            delta = jnp.sum(probs * dp, axis=1, keepdims=True)
            ds = (probs * (dp - delta)).astype(BF16)
            dq_ref[hh] = (_dot(ds, kb) * scale).reshape(ATT_G, WINDOW, HEAD)
            dkb = _dot(ds, q, 0, 0) * scale
            dvb = _dot(probs.astype(BF16), do, 0, 0)
            dk_ref[hh, pl.ds(cur, WINDOW), :] += dkb[WINDOW:, :]
            dv_ref[hh, pl.ds(cur, WINDOW), :] += dvb[WINDOW:, :]
            prv = pl.multiple_of(jnp.maximum(i - 1, 0) * WINDOW, WINDOW)
            dk_ref[hh, pl.ds(prv, WINDOW), :] += dkb[:WINDOW, :]
            dv_ref[hh, pl.ds(prv, WINDOW), :] += dvb[:WINDOW, :]

            dsr = -psink * delta
            upd = jnp.zeros((8, 128), F32)
            for gq in range(ATT_G):
                v = jnp.sum(dsr[gq * WINDOW:(gq + 1) * WINDOW, :], axis=0, keepdims=True)
                upd = jnp.where((lane == gq) & (row == 0), v, upd)
            dsk_ref[hh] += upd

    qspec, cur, prev, tok = _attn_specs(S)
    full = pl.BlockSpec((ATT_HPS, S, HEAD), lambda h, i: (h, 0, 0))
    return pl.pallas_call(
        body, name=name, grid=(ATT_KV // ATT_HPS, nb),
        in_specs=[pl.BlockSpec(memory_space=pltpu.SMEM), _BAND, qspec, cur, prev, cur, prev, tok],
        out_specs=[qspec, full, full, pl.BlockSpec((ATT_HPS, 8, 128), lambda h, i: (h, 0, 0))],
        out_shape=[jax.ShapeDtypeStruct((ATT_KV, ATT_G, S, HEAD), F32), jax.ShapeDtypeStruct((ATT_KV, S, HEAD), F32),
                   jax.ShapeDtypeStruct((ATT_KV, S, HEAD), F32), jax.ShapeDtypeStruct((ATT_KV, 8, 128), F32)],
        compiler_params=_cp(2),
    )(sinks, _attn_band(), qh, kh, kh, vh, vh, doh)


def _heads_major(t, nh):
    S = t.shape[0]
    return t.reshape(S, nh, HEAD).transpose(1, 0, 2)


def _tokens_major(t):
    nh, S, _ = t.shape
    return t.transpose(1, 0, 2).reshape(S, nh * HEAD)


class _NoComm:
    def late_weights(self, w, after):
        return w

    def advance(self, after, group=None, tensors=None):
        return None


def _local_step(x, posf, target, w, comm=None):
    S, D = x.shape
    gr = {}
    comm = comm or _NoComm()

    (h1,) = _rms_fwd(x, [w["a_norm"]], name="a_norm_f", dep=w.get("dep"))
    zx = _mm(h1, w["w_zx"], name="in_proj_zx")
    dtr = _mm(h1, w["w_dt"], name="in_proj_dt")
    xbc = _conv_silu_fwd(zx, 2048, 4096, w["a_conv_w"], w["a_conv_b"], name="a_conv_f")
    dt_g, ac_g, sg_g, ac_t = _ssd_prep(dtr, w["a_dt_bias"], w["a_A_log"])
    y_ssd, states = _ssd_fwd(xbc, dt_g, ac_g, ac_t)
    yg = _gate_fwd(y_ssd, xbc, zx, w["a_Dexp"], w["a_gnorm"])
    x1 = _mm(yg, w["a_out_proj"], res=x, name="out_proj")

    w = comm.late_weights(w, x1)
    FW = w["f_w_in"][0].shape[2]

    def ffn_fwd(xin, l, loss_target=None):
        (h,) = _rms_fwd(xin, [w["f_norm"][l]], name=f"f_norm_f{l}")
        u = _mm(h, w["f_w_in"][l], name=f"f_in{l}", dims=(S, N_CHIPS * FW, D), tn=FW,
                b_spec=pl.BlockSpec((None, D, FW), lambda i, j, k: (j, 0, 0)))
        a = _ffn_act_fwd(u, w["f_conv_w"][l], w["f_conv_b"][l], name=f"f_act_f{l}")
        xo = _mm(a, w["f_w_down"][l], res=xin, tk=a.shape[1], name=f"f_down{l}", target=loss_target)
        return xo, (h, u)

    x2, ffn0 = ffn_fwd(x1, 0)

    hk, hq = _rms_fwd(x2, [w["kv_norm"], w["b_norm"]], name="kvq_norm_f")
    kv = _mm(hk, w["w_kv"], bias=w["b_kv"], name="kv_proj")
    q = _mm(hq, w["w_q"], bias=w["b_q"], name="q_proj")
    rope = _rope_cs(posf)
    kr = _hnrope_fwd(kv, 0, 256, w["k_norm_w"], rope, name="k_rope_f")
    qr = _hnrope_fwd(q, 0, 1024, w["q_norm_w"], rope, name="q_rope_f")
    qh = qr.reshape(ATT_KV, ATT_G, S, HEAD)
    kh = kr
    vh = _heads_major(kv[:, 256:].astype(BF16), ATT_KV)
    att = _attn_fwd(qh, kh, vh, w["sinks"])
    x3 = _mm(att, w["w_o"], bias=w["b_o"], res=x2, name="o_proj")
    (dy, loss_part), ffn1 = ffn_fwd(x3, 1, target)

    def ffn_bwd(xin, l, saved, dyo, want_colsum, dep=None):
        h, u = saved
        da = _mm(dyo, w["f_w_down"][l], tb=True, name=f"f_down_dx{l}", dep=dep)
        du, dcw, dcb, a = _ffn_act_bwd(u, w["f_conv_w"][l], w["f_conv_b"][l], da, name=f"f_act_b{l}")
        dw_down = _mm(a, dyo, ta=True, out_dtype=BF16, name=f"f_down_dw{l}")
        dw_in = _mm(h, du, ta=True, out_dtype=BF16, name=f"f_in_dw{l}", dims=(D, N_CHIPS * FW, S), tm=D, tn=FW, tk=S,
                    b_spec=pl.BlockSpec((None, S, FW), lambda i, j, k: (j // 2, 0, j % 2)),
                    o_spec=pl.BlockSpec((None, D, FW), lambda i, j, k: (j, i, 0)), o_shape=(N_CHIPS, D, FW))
        ts = _pick(S, (1024, 512, 256))
        pieces = [(pl.BlockSpec((None, ts, FW), lambda i, j, k, q=q: (q // 2, i, q % 2)),
                   pl.BlockSpec((None, 512, FW), lambda i, j, k, q=q: (q, j, 0))) for q in range(N_CHIPS)]
        dh = _mm(du, w["f_w_in"][l], tb=True, name=f"f_in_dx{l}", dims=(S, D, FW), tm=ts, tn=512, tk=FW,
                 a_spec=pieces[0][0], b_spec=pieces[0][1], more=pieces[1:])
        outs = _rms_bwd(xin, [w["f_norm"][l]], [dh], dyo, name=f"f_norm_b{l}", want_colsum=want_colsum)
        g = dict(f_norm=outs[1], f_w_in=dw_in, f_conv_w=dcw, f_conv_b=dcb, f_w_down=dw_down)
        return outs[0], g, (outs[2] if want_colsum else None)

    dx3, gr["ffn1"], db_o = ffn_bwd(x3, 1, ffn1, dy, True)
    gr["b_o"] = db_o
    gr["w_o"] = _mm(att, dx3, ta=True, out_dtype=BF16, name="o_proj_dw")
    datt = _mm(dx3, w["w_o"], tb=True, out_dtype=BF16, name="o_proj_dx")
    dqh, dkh, dvh, dsk = _attn_bwd(qh, kh, vh, w["sinks"], datt)
    gr["sinks"] = dsk[:, 0, :4].reshape(1, 16)
    dv = _tokens_major(dvh).astype(BF16)
    dq, db_q, dqn = _hnrope_bwd(q, 0, 1024, w["q_norm_w"], rope, dqh.reshape(16, S, HEAD), name="q_rope_b")
    dk, db_k, dkn = _hnrope_bwd(kv, 0, 256, w["k_norm_w"], rope, dkh, name="k_rope_b")
    gr["q_norm"], gr["k_norm"] = dqn[:, :HEAD], dkn[:, :HEAD]
    gr["b_q"] = db_q
    gr["b_kv"] = jnp.concatenate([db_k, _colsum(dv, name="dv_colsum")], axis=1)
    dkv = jnp.concatenate([dk, dv], axis=1)
    gr["w_q"] = _mm(hq, dq, ta=True, out_dtype=BF16, name="q_proj_dw")
    gr["w_kv"] = _mm(hk, dkv, ta=True, out_dtype=BF16, name="kv_proj_dw")
    tok = comm.advance([gr["w_kv"]], 1, dict(f_down1=gr["ffn1"]["f_w_down"], f_in1=gr["ffn1"]["f_w_in"], w_o=gr["w_o"],
                                             w_q=gr["w_q"], w_kv=gr["w_kv"]))
    dhq = _mm(dq, w["w_q"], tb=True, name="q_proj_dx", dep=tok)
    dhk = _mm(dkv, w["w_kv"], tb=True, name="kv_proj_dx")
    dx2, gr["kv_norm"], gr["b_norm"] = _rms_bwd(x2, [w["kv_norm"], w["b_norm"]], [dhk, dhq], dx3, name="kvq_norm_b")

    dx1, gr["ffn0"], _ = ffn_bwd(x1, 0, ffn0, dx2, False, dep=comm.advance([dx2]))

    gr["a_out_proj"] = _mm(yg, dx1, ta=True, out_dtype=BF16, name="out_proj_dw")
    tok = comm.advance([dx1, gr["a_out_proj"]], 2,
                       dict(f_down0=gr["ffn0"]["f_w_down"], f_in0=gr["ffn0"]["f_w_in"], out_proj=gr["a_out_proj"]))
    dyg = _mm(dx1, w["a_out_proj"], tb=True, name="out_proj_dx", dep=tok)
    dy_ssd, dz, gr["a_gnorm"], dD = _gate_bwd(y_ssd, xbc, zx, w["a_Dexp"], w["a_gnorm"], dyg)
    gr["a_D"] = dD[:, :SSM_HEADS]
    dxs, dB, dC, dhead = _ssd_bwd(xbc, dt_g, ac_g, ac_t, states, dy_ssd, w["a_Dexp"], dep=comm.advance([dy_ssd]))
    ddtr, dsmall = _ssd_post(dhead, dt_g, sg_g, w["a_A_log_g"])
    gr["a_A_log"] = dsmall[:, 0, :4].reshape(1, SSM_HEADS)
    gr["a_dt_bias"] = dsmall[:, 1, :4].reshape(1, SSM_HEADS)
    dxbc, gr["a_conv_w"], gr["a_conv_b"] = _conv_silu_bwd(
        zx, 2048, 4096, w["a_conv_w"], w["a_conv_b"], [(dxs, 0), (dB, 2048), (dC, 3072)], name="a_conv_b")
    gr["w_z"] = _mm(h1, dz, ta=True, out_dtype=BF16, name="in_proj_dwz")
    gr["w_x"] = _mm(h1, dxbc, ta=True, out_dtype=BF16, name="in_proj_dwx")
    gr["w_dt"] = _mm(h1, ddtr, ta=True, out_dtype=BF16, name="in_proj_dwdt")
    ts = _pick(S, (1024, 512, 256))
    wblk = lambda q: pl.BlockSpec((512, 2048), lambda i, j, k: (j, q))
    dh1 = _mm(dz, w["w_zx"], tb=True, name="in_proj_dx", dims=(S, D, 2048), tm=ts, tn=512, tk=2048,
              a_spec=pl.BlockSpec((ts, 2048), lambda i, j, k: (i, 0)), b_spec=wblk(0),
              more=[(dxbc, pl.BlockSpec((ts, 2048), lambda i, j, k: (i, 0)), w["w_zx"], wblk(1)),
                    (dxbc, pl.BlockSpec((ts, 2048), lambda i, j, k: (i, 1)), w["w_zx"], wblk(2)),
                    (ddtr, pl.BlockSpec((ts, 128), lambda i, j, k: (i, 0)), w["w_dt"], pl.BlockSpec((512, 128), lambda i, j, k: (j, 0)))])
    dx0, gr["a_norm"] = _rms_bwd(x, [w["a_norm"]], [dh1], dx1, name="a_norm_b")
    tok = comm.advance([dx0], 3, dict(in_proj=_in_proj_grad(gr).reshape(D, N_CHIPS, -1).transpose(1, 0, 2)))
    return loss_part, dx0, gr, tok


def _prep_small(full, w):
    w["a_norm"] = full["a_norm"]
    w["a_conv_w"] = full["a_conv_w"][0]
    w["a_conv_b"] = full["a_conv_b"]
    pad32 = lambda v: jnp.pad(v, ((0, 0), (0, 128 - SSM_HEADS)))
    w["a_dt_bias"] = pad32(full["a_dt_bias"])
    w["a_A_log"] = pad32(full["a_A_log"])
    w["a_A_log_g"] = jnp.pad(full["a_A_log"].reshape(SSM_GROUPS, 1, 4), ((0, 0), (0, 0), (0, 124)))
    w["a_Dexp"] = jnp.repeat(full["a_D"], HEAD, axis=1)
    w["a_gnorm"] = full["a_gnorm"]
    w["f_norm"] = [full["f_norm"][l:l + 1] for l in range(2)]
    w["f_conv_w"] = [full["f_conv_w"][l] for l in range(2)]
    w["f_conv_b"] = [full["f_conv_b"][l:l + 1] for l in range(2)]
    w["kv_norm"] = full["kv_norm"].reshape(1, -1)
    w["b_kv"] = full["b_kv"].reshape(1, -1)
    w["k_norm_w"] = jnp.tile(full["k_norm"].reshape(1, HEAD), (1, ATT_KV))
    w["b_norm"] = full["b_norm"]
    w["b_q"] = full["b_q"]
    w["q_norm_w"] = jnp.tile(full["q_norm"], (1, ATT_KV * ATT_G))
    w["sinks"] = full["sinks"].reshape(-1)
    w["b_o"] = full["b_o"]
    return w


def _split_in_proj(ip):
    return ip[:, :6144].astype(BF16), jnp.pad(ip[:, 6144:], ((0, 0), (0, 128 - SSM_HEADS))).astype(BF16)


def _join_in_proj(blocks, *, name="in_proj_join", tr=256):
    _, R, cw = blocks.shape
    zx_cols = 3 * 2048
    rest = N_CHIPS * cw - zx_cols

    def body(b_ref, zx_ref, dt_ref):
        whole = jnp.concatenate([b_ref[j] for j in range(N_CHIPS)], axis=1)
        zx_ref[...] = whole[:, :zx_cols]
        dt_ref[...] = jnp.concatenate([whole[:, zx_cols:], jnp.zeros((tr, 128 - rest), BF16)], axis=1)

    return pl.pallas_call(
        body, name=name, grid=(R // tr,), in_specs=[pl.BlockSpec((N_CHIPS, tr, cw), lambda i: (0, i, 0))],
        out_specs=[pl.BlockSpec((tr, zx_cols), lambda i: (i, 0)), pl.BlockSpec((tr, 128), lambda i: (i, 0))],
        out_shape=[jax.ShapeDtypeStruct((R, zx_cols), BF16), jax.ShapeDtypeStruct((R, 128), BF16)],
        compiler_params=_cp(1),
    )(blocks)


def _prep_weights(full):
    w = _prep_small(full, {})
    w["w_zx"], w["w_dt"] = _split_in_proj(full["a_in_proj"][0])
    w["a_out_proj"] = full["a_out_proj"][0].astype(BF16)
    w["f_w_in"] = [full["f_w_in"][l].reshape(1024, N_CHIPS, -1).transpose(1, 0, 2).astype(BF16) for l in range(2)]
    w["f_w_down"] = [full["f_w_down"][l].astype(BF16) for l in range(2)]
    w["w_kv"] = full["w_kv"].astype(BF16)
    w["w_q"] = full["w_q"][0].astype(BF16)
    w["w_o"] = full["w_o"][0].astype(BF16)
    return w


def _small_grads(gr):
    g = {}
    g["a_norm"] = gr["a_norm"]
    g["a_conv_w"] = gr["a_conv_w"][None]
    g["a_conv_b"] = gr["a_conv_b"]
    g["a_dt_bias"], g["a_A_log"], g["a_D"] = gr["a_dt_bias"], gr["a_A_log"], gr["a_D"]
    g["a_gnorm"] = gr["a_gnorm"]
    g["kv_norm"] = gr["kv_norm"].reshape(-1)
    g["b_kv"] = gr["b_kv"].reshape(-1)
    g["k_norm"] = gr["k_norm"].reshape(-1)
    g["b_norm"] = gr["b_norm"]
    g["b_q"] = gr["b_q"]
    g["q_norm"] = gr["q_norm"]
    g["sinks"] = gr["sinks"]
    g["b_o"] = gr["b_o"]
    f = [gr["ffn0"], gr["ffn1"]]
    g["f_norm"] = jnp.concatenate([f[0]["f_norm"], f[1]["f_norm"]], axis=0)
    g["f_conv_w"] = jnp.stack([f[l]["f_conv_w"] for l in range(2)])
    g["f_conv_b"] = jnp.concatenate([f[l]["f_conv_b"] for l in range(2)], axis=0)
    return g


def _in_proj_grad(gr):
    return jnp.concatenate([gr["w_z"], gr["w_x"], gr["w_dt"][:, :SSM_HEADS]], axis=1)


def _full_grads(gr):
    g = _small_grads(gr)
    f32 = lambda t: t.astype(F32)
    g["a_in_proj"] = f32(_in_proj_grad(gr))[None]
    g["a_out_proj"] = f32(gr["a_out_proj"])[None]
    g["w_kv"] = f32(gr["w_kv"])
    g["w_q"] = f32(gr["w_q"])[None]
    g["w_o"] = f32(gr["w_o"])[None]
    f = [gr["ffn0"], gr["ffn1"]]
    g["f_w_in"] = jnp.stack([f32(f[l]["f_w_in"]).transpose(1, 0, 2).reshape(1024, -1) for l in range(2)])
    g["f_w_down"] = jnp.stack([f32(f[l]["f_w_down"]) for l in range(2)])
    return g


MESH = pl.DeviceIdType.MESH
WEIGHTS = ("a_norm", "a_in_proj", "a_conv_w", "a_conv_b", "a_dt_bias", "a_A_log", "a_D", "a_gnorm", "a_out_proj",
           "kv_norm", "w_kv", "b_kv", "k_norm", "b_norm", "w_q", "b_q", "q_norm", "sinks", "w_o", "b_o", "f_norm",
           "f_w_in", "f_conv_w", "f_conv_b", "f_w_down")
MATS = (("in_proj", "a_in_proj", 0), ("out_proj", "a_out_proj", 0), ("w_kv", "w_kv", None), ("w_q", "w_q", 0),
        ("w_o", "w_o", 0), ("f_in0", "f_w_in", 0), ("f_in1", "f_w_in", 1), ("f_down0", "f_w_down", 0),
        ("f_down1", "f_w_down", 1))
SMALL_CUT = (("a_norm", 1), ("a_conv_w", 2), ("a_conv_b", 1), ("a_gnorm", 1), ("f_conv_w", 2))
SMALL_REP = ("a_dt_bias", "a_A_log", "a_D", "kv_norm", "b_kv", "k_norm", "b_norm", "b_q", "q_norm", "sinks", "b_o",
             "f_norm", "f_conv_b")


def _coords():
    return lax.axis_index("x"), lax.axis_index("y"), lax.axis_index("c")


def _other_chips(x, y):
    return [(1 - x, y), (x, 1 - y), (1 - x, 1 - y)]


def _pack(arrs, rows_align, lanes, dtype):
    flat = jnp.concatenate([a.reshape(-1).astype(dtype) for a in arrs])
    per = rows_align * lanes
    total = -(-flat.shape[0] // per) * per
    return jnp.pad(flat, (0, total - flat.shape[0])).reshape(total // lanes, lanes)


def _unpack(flat, shapes):
    out, off = [], 0
    for s in shapes:
        n = math.prod(s)
        out.append(flat[off:off + n].reshape(s))
        off += n
    return out


def _remote(src, dst, send, recv, k, dev):
    return pltpu.make_async_remote_copy(src_ref=src, dst_ref=dst, send_sem=send.at[k], recv_sem=recv.at[k],
                                        device_id=dev, device_id_type=MESH)


_ANY = pl.BlockSpec(memory_space=pl.ANY)


def _halves(t):
    r, c = t.shape
    return t.reshape(2, r // 2, c)


def _gather_weights(shards, sp):
    n = len(shards)
    per = 9
    n_sem = per * n + 3

    def body(*refs):
        sh, sp_ref = refs[:n], refs[n]
        outs, sout = refs[n + 1:2 * n + 1], refs[2 * n + 1]
        send, recv, loc = refs[2 * n + 2:]
        x, y, c = _coords()
        me = 2 * x + y
        cx_, cy_, cd_ = _other_chips(x, y)
        ix, iy, idg = (2 * p[0] + p[1] for p in (cx_, cy_, cd_))
        to_x, to_y, sib = (*cx_, c), (*cy_, c), (x, y, 1 - c)
        l1 = pltpu.make_async_copy(sp_ref, sout.at[me], loc.at[0])
        l1.start()
        sends = [_remote(sp_ref, sout.at[me], send, recv, per * n + j, (*p, c)) for j, p in enumerate((cx_, cy_, cd_))]
        for t in range(n):
            sends.append(_remote(sh[t].at[c], outs[t].at[me, c], send, recv, per * t + 0, to_x))
            sends.append(_remote(sh[t].at[c], outs[t].at[me, c], send, recv, per * t + 1, to_y))
            sends.append(_remote(sh[t], outs[t].at[me], send, recv, per * t + 8, sib))
        for cp in sends:
            cp.start()

        def go(src, dst, k, dev):
            cp = _remote(src, dst, send, recv, k, dev)
            cp.start()
            sends.append(cp)

        def piece(t, owner, first):
            q = sh[t].shape[1] // 2
            return outs[t].at[owner, c, pl.ds(0 if first else q, q)]

        for t in range(n):
            _remote(sh[t].at[c], outs[t].at[ix, c], send, recv, per * t + 0, to_x).wait_recv()
            go(piece(t, ix, False), piece(t, ix, False), per * t + 3, to_y)
            go(outs[t].at[ix, c], outs[t].at[ix, c], per * t + 4, sib)
        for t in range(n):
            _remote(sh[t].at[c], outs[t].at[iy, c], send, recv, per * t + 1, to_y).wait_recv()
            go(piece(t, iy, True), piece(t, iy, True), per * t + 2, to_x)
            go(outs[t].at[iy, c], outs[t].at[iy, c], per * t + 5, sib)
        for t in range(n):
            _remote(piece(t, idg, True), piece(t, idg, True), send, recv, per * t + 2, to_x).wait_recv()
            go(piece(t, idg, True), piece(t, idg, True), per * t + 6, sib)
            _remote(piece(t, idg, False), piece(t, idg, False), send, recv, per * t + 3, to_y).wait_recv()
            go(piece(t, idg, False), piece(t, idg, False), per * t + 7, sib)
        for j, p in enumerate((cx_, cy_, cd_)):
            _remote(sp_ref, sout.at[2 * p[0] + p[1]], send, recv, per * n + j, (*p, c)).wait_recv()
        for t in range(n):
            q = sh[t].shape[1] // 2
            other = lambda owner, lo=None: outs[t].at[owner, 1 - c] if lo is None else outs[t].at[owner, 1 - c, pl.ds(lo, q)]
            _remote(other(ix), other(ix), send, recv, per * t + 4, sib).wait_recv()
            _remote(other(iy), other(iy), send, recv, per * t + 5, sib).wait_recv()
            _remote(other(idg, 0), other(idg, 0), send, recv, per * t + 6, sib).wait_recv()
            _remote(other(idg, q), other(idg, q), send, recv, per * t + 7, sib).wait_recv()
            _remote(sh[t], outs[t].at[me], send, recv, per * t + 8, sib).wait_recv()
        for cp in sends:
            cp.wait_send()
        l1.wait()

    res = pl.pallas_call(
        body, name="gather_weights", in_specs=[_ANY] * (n + 1), out_specs=[_ANY] * (n + 1),
        out_shape=[jax.ShapeDtypeStruct((N_CHIPS,) + t.shape, t.dtype) for t in shards]
        + [jax.ShapeDtypeStruct((N_CHIPS,) + sp.shape, sp.dtype)],
        scratch_shapes=[pltpu.SemaphoreType.DMA((n_sem,)), pltpu.SemaphoreType.DMA((n_sem,)),
                        pltpu.SemaphoreType.DMA((1,))],
    )(*shards, sp)
    return res[:n], res[n]


_HBM = pl.BlockSpec(memory_space=pltpu.HBM)
_SEMS = pl.BlockSpec(memory_space=pltpu.SEMAPHORE)
_DATAFLOW = pltpu.SideEffectType.DATAFLOW_SIDE_EFFECTING


def _in_hbm(a):
    return pltpu.with_memory_space_constraint(a, pltpu.HBM)


def _start_copies(copies, arrays, n_sem, after, *, name):
    n, na = len(arrays), len(after)

    def body(*refs):
        for mine, _ in copies(refs[:n], refs[n + na], refs[n + na + 1]):
            mine.start()
        refs[-1][...] = jnp.zeros_like(refs[-1])

    res = pl.pallas_call(
        body, name=name, in_specs=[_HBM] * n + [_ANY] * na,
        out_specs=[_SEMS, _SEMS] + [_HBM] * n + [pl.BlockSpec(memory_space=pltpu.VMEM)],
        out_shape=[pltpu.SemaphoreType.DMA((n_sem,)), pltpu.SemaphoreType.DMA((n_sem,))]
        + [pltpu.HBM(a.shape, a.dtype) for a in arrays] + [jax.ShapeDtypeStruct((8, 128), F32)],
        input_output_aliases={t: 2 + t for t in range(n)},
        compiler_params=pltpu.CompilerParams(has_side_effects=_DATAFLOW),
    )(*[_in_hbm(a) for a in arrays], *after)
    return res[0], res[1], list(res[2:2 + n]), res[-1]


def _wait_copies(copies, send, recv, arrays, after, *, name):
    n = len(arrays)

    def body(*refs):
        for mine, theirs in copies(refs[:n], refs[n], refs[n + 1]):
            mine.wait_send()
            theirs.wait_recv()

    return list(pl.pallas_call(
        body, name=name, in_specs=[_HBM] * n + [_SEMS, _SEMS] + [_ANY] * len(after), out_specs=[_HBM] * n,
        out_shape=[pltpu.HBM(a.shape, a.dtype) for a in arrays], input_output_aliases={t: t for t in range(n)},
        compiler_params=pltpu.CompilerParams(has_side_effects=_DATAFLOW),
    )(*arrays, send, recv, *after))


def _sibling_copies(refs, send, recv):
    n = len(refs) // 2
    x, y, c = _coords()
    cps = [_remote(refs[t].at[:, 1 - c], refs[n + t], send, recv, t, (x, y, 1 - c)) for t in range(n)]
    return [(cp, cp) for cp in cps]


def _join_copies(refs, send, recv):
    x, y, c = _coords()
    sib = (x, y, 1 - c)
    return [(_remote(o.at[c], o.at[c], send, recv, t, sib), _remote(o.at[1 - c], o.at[1 - c], send, recv, t, sib))
            for t, o in enumerate(refs)]


def _gather_copies(sh, land, send, recv):
    x, y, c = _coords()
    me = 2 * x + y
    out = []
    for t in range(len(sh)):
        for j, (cx, cy) in enumerate(_other_chips(x, y)):
            dev = (cx, cy, c)
            out.append((_remote(sh[t].at[c], land[t].at[me, c], send, recv, 4 * t + j, dev),
                        _remote(sh[t].at[c], land[t].at[2 * cx + cy, c], send, recv, 4 * t + j, dev)))
        sib = (x, y, 1 - c)
        out.append((_remote(sh[t], land[t].at[me], send, recv, 4 * t + 3, sib),
                    _remote(sh[t], land[t].at[me], send, recv, 4 * t + 3, sib)))
    return out


def _gather_start(shards, after, *, name):
    n = len(shards)

    def body(*refs):
        sh, land = refs[:n], refs[n:2 * n]
        send, recv = refs[2 * n + 1], refs[2 * n + 2]
        token = refs[-1]
        for mine, _ in _gather_copies(sh, land, send, recv):
            mine.start()
        token[...] = jnp.zeros_like(token)

    lands = [_in_hbm(lax.empty((N_CHIPS,) + s.shape, s.dtype)) for s in shards]
    res = pl.pallas_call(
        body, name=name, in_specs=[_HBM] * (2 * n) + [_ANY],
        out_specs=[_SEMS, _SEMS] + [_HBM] * (2 * n) + [pl.BlockSpec(memory_space=pltpu.VMEM)],
        out_shape=[pltpu.SemaphoreType.DMA((4 * n,)), pltpu.SemaphoreType.DMA((4 * n,))]
        + [pltpu.HBM(s.shape, s.dtype) for s in shards] + [pltpu.HBM(l.shape, l.dtype) for l in lands]
        + [jax.ShapeDtypeStruct((8, 128), F32)],
        input_output_aliases={t: 2 + t for t in range(2 * n)},
        compiler_params=pltpu.CompilerParams(has_side_effects=_DATAFLOW),
    )(*[_in_hbm(s) for s in shards], *lands, after)
    return res[0], res[1], res[2:2 + n], res[2 + n:2 + 2 * n], res[-1]


def _gather_wait(send, recv, shards, lands, after, *, name):
    n = len(shards)

    def body(*refs):
        sh, land = refs[:n], refs[n:2 * n]
        send_r, recv_r = refs[2 * n], refs[2 * n + 1]
        for mine, theirs in _gather_copies(sh, land, send_r, recv_r):
            mine.wait_send()
            theirs.wait_recv()

    res = pl.pallas_call(
        body, name=name, in_specs=[_HBM] * (2 * n) + [_SEMS, _SEMS, _ANY], out_specs=[_HBM] * (2 * n),
        out_shape=[pltpu.HBM(s.shape, s.dtype) for s in shards] + [pltpu.HBM(l.shape, l.dtype) for l in lands],
        input_output_aliases={t: t for t in range(2 * n)},
        compiler_params=pltpu.CompilerParams(has_side_effects=_DATAFLOW),
    )(*shards, *lands, send, recv, after)
    return res[n:]


def _gather_forward(lands, *, name):
    n = len(lands)

    def body(*refs):
        o = refs[n:2 * n]
        send, recv = refs[2 * n:]
        x, y, c = _coords()
        sib = (x, y, 1 - c)
        srcs = [2 * cx + cy for cx, cy in _other_chips(x, y)]
        cps = [_remote(o[t].at[s, c], o[t].at[s, c], send, recv, 3 * t + j, sib) for t in range(n) for j, s in enumerate(srcs)]
        for cp in cps:
            cp.start()
        for t in range(n):
            for j, s in enumerate(srcs):
                _remote(o[t].at[s, 1 - c], o[t].at[s, 1 - c], send, recv, 3 * t + j, sib).wait_recv()
        for cp in cps:
            cp.wait_send()

    return pl.pallas_call(
        body, name=name, in_specs=[_ANY] * n, out_specs=[_ANY] * n, input_output_aliases={t: t for t in range(n)},
        out_shape=[jax.ShapeDtypeStruct(l.shape, l.dtype) for l in lands],
        scratch_shapes=[pltpu.SemaphoreType.DMA((3 * n,)), pltpu.SemaphoreType.DMA((3 * n,))],
    )(*lands)


def _small_copies(v, land, send, recv):
    x, y, c = _coords()
    me = 4 * x + 2 * y + c
    out = []
    for k in range(1, 8):
        px = 1 - x if k & 4 else x
        py = 1 - y if k & 2 else y
        pc = 1 - c if k & 1 else c
        out.append((_remote(v, land.at[me], send, recv, k - 1, (px, py, pc)),
                    _remote(v, land.at[4 * px + 2 * py + pc], send, recv, k - 1, (px, py, pc))))
    return out


def _small_start(v, after, *, name):
    def body(v_ref, land_ref, after_ref, send, recv, v_thru, land_thru, token):
        for mine, _ in _small_copies(v_ref, land_ref, send, recv):
            mine.start()
        token[...] = jnp.zeros_like(token)

    land = _in_hbm(lax.empty((8,) + v.shape, v.dtype))
    return pl.pallas_call(
        body, name=name, in_specs=[_HBM, _HBM, _ANY],
        out_specs=[_SEMS, _SEMS, _HBM, _HBM, pl.BlockSpec(memory_space=pltpu.VMEM)],
        out_shape=[pltpu.SemaphoreType.DMA((7,)), pltpu.SemaphoreType.DMA((7,)), pltpu.HBM(v.shape, v.dtype),
                   pltpu.HBM(land.shape, land.dtype), jax.ShapeDtypeStruct((8, 128), F32)],
        input_output_aliases={0: 2, 1: 3}, compiler_params=pltpu.CompilerParams(has_side_effects=_DATAFLOW),
    )(_in_hbm(v), land, after)


def _small_wait(send, recv, v, land, after, *, name):
    def body(v_ref, land_ref, send_r, recv_r, *rest):
        for mine, theirs in _small_copies(v_ref, land_ref, send_r, recv_r):
            mine.wait_send()
            theirs.wait_recv()

    return pl.pallas_call(
        body, name=name, in_specs=[_HBM, _HBM, _SEMS, _SEMS] + [_ANY] * len(after), out_specs=[_HBM, _HBM],
        out_shape=[pltpu.HBM(v.shape, v.dtype), pltpu.HBM(land.shape, land.dtype)],
        input_output_aliases={0: 0, 1: 1}, compiler_params=pltpu.CompilerParams(has_side_effects=_DATAFLOW),
    )(v, land, send, recv, *after)


def _small_sum(v, land, me_idx, *, name="small_sum"):
    def body(me_ref, v_ref, land_ref, o_ref):
        acc = None
        for s in range(8):
            term = jnp.where(me_ref[0] == s, v_ref[...], land_ref[s])
            acc = term if acc is None else acc + term
        o_ref[...] = acc

    whole = lambda shape: pl.BlockSpec(shape, lambda i, me_ref: (0,) * len(shape))
    return pl.pallas_call(
        body, name=name,
        grid_spec=pltpu.PrefetchScalarGridSpec(num_scalar_prefetch=1, grid=(1,), in_specs=[whole(v.shape), whole(land.shape)],
                                               out_specs=whole(v.shape)),
        out_shape=jax.ShapeDtypeStruct(v.shape, F32), compiler_params=_cp(1),
    )(me_idx, v, land)


RS_ROW_SPLIT = 2


def _rs_add_pair(gs, as_, c_idx, *, name):
    n = len(gs)

    def body(c_ref, *refs):
        for t in range(n):
            refs[2 * n + t][...] = (refs[t][...].astype(F32) + refs[n + t][...].astype(F32)).astype(BF16)

    def gspec(g):
        _, _, rh, cols = g.shape
        return pl.BlockSpec((None, None, rh // RS_ROW_SPLIT, cols), lambda j, i, c_ref: (j, c_ref[0], i, 0))

    def pspec(g):
        _, _, rh, cols = g.shape
        return pl.BlockSpec((None, rh // RS_ROW_SPLIT, cols), lambda j, i, c_ref: (j, i, 0))

    return pl.pallas_call(
        body, name=name,
        grid_spec=pltpu.PrefetchScalarGridSpec(
            num_scalar_prefetch=1, grid=(N_CHIPS, RS_ROW_SPLIT),
            in_specs=[gspec(g) for g in gs] + [pspec(g) for g in gs], out_specs=[pspec(g) for g in gs]),
        out_shape=[jax.ShapeDtypeStruct((N_CHIPS,) + g.shape[2:], BF16) for g in gs], compiler_params=_cp(2),
    )(c_idx, *gs, *as_)


def _chips_copies(p, r, send, recv):
    x, y, c = _coords()
    return [_remote(p[t].at[2 * cx + cy], r[t].at[k], send, recv, 3 * t + k, (cx, cy, c))
            for k, (cx, cy) in enumerate(_other_chips(x, y)) for t in range(len(p))]


def _rs_chips_start(ps, after, *, name):
    n, na = len(ps), len(after)

    def body(*refs):
        p, r = refs[:n], refs[n:2 * n]
        send, recv = refs[2 * n + na], refs[2 * n + na + 1]
        token = refs[-1]
        for cp in _chips_copies(p, r, send, recv):
            cp.start()
        token[...] = jnp.zeros_like(token)

    lands = [_in_hbm(lax.empty((3,) + p.shape[1:], p.dtype)) for p in ps]
    res = pl.pallas_call(
        body, name=name, in_specs=[_HBM] * (2 * n) + [_ANY] * na,
        out_specs=[_SEMS, _SEMS] + [_HBM] * (2 * n) + [pl.BlockSpec(memory_space=pltpu.VMEM)],
        out_shape=[pltpu.SemaphoreType.DMA((3 * n,)), pltpu.SemaphoreType.DMA((3 * n,))]
        + [pltpu.HBM(p.shape, p.dtype) for p in ps] + [pltpu.HBM(l.shape, l.dtype) for l in lands]
        + [jax.ShapeDtypeStruct((8, 128), F32)],
        input_output_aliases={t: 2 + t for t in range(2 * n)},
        compiler_params=pltpu.CompilerParams(has_side_effects=_DATAFLOW),
    )(*[_in_hbm(p) for p in ps], *lands, *after)
    return res[0], res[1], res[2:2 + n], res[2 + n:2 + 2 * n], res[-1]


def _rs_chips_wait(send, recv, ps, lands, after, *, name):
    n = len(ps)

    def body(*refs):
        p, r = refs[:n], refs[n:2 * n]
        for cp in _chips_copies(p, r, refs[2 * n], refs[2 * n + 1]):
            cp.wait_send()
            cp.wait_recv()

    res = pl.pallas_call(
        body, name=name, in_specs=[_HBM] * (2 * n) + [_SEMS, _SEMS] + [_ANY] * len(after), out_specs=[_HBM] * (2 * n),
        out_shape=[pltpu.HBM(p.shape, p.dtype) for p in ps] + [pltpu.HBM(l.shape, l.dtype) for l in lands],
        input_output_aliases={t: t for t in range(2 * n)},
        compiler_params=pltpu.CompilerParams(has_side_effects=_DATAFLOW),
    )(*ps, *lands, send, recv, *after)
    return res[:n], res[n:]


def _rs_add_chips(ps, rs, idx, *, name):
    n = len(ps)

    def body(idx_ref, *refs):
        for t in range(n):
            p_ref, r0, r1, r2 = refs[4 * t:4 * t + 4]
            refs[4 * n + t][...] = ((p_ref[...].astype(F32) + r0[...].astype(F32)) + r1[...].astype(F32)) + r2[...].astype(F32)

    in_specs, args = [], []
    for p, r in zip(ps, rs):
        _, rh, cols = p.shape
        blk = (None, rh // RS_ROW_SPLIT, cols)
        in_specs.append(pl.BlockSpec(blk, lambda i, idx_ref: (idx_ref[0], i, 0)))
        in_specs += [pl.BlockSpec(blk, lambda i, idx_ref, k=k: (k, i, 0)) for k in range(3)]
        args += [p, r, r, r]
    out_specs = [pl.BlockSpec((None, p.shape[1] // RS_ROW_SPLIT, p.shape[2]), lambda i, idx_ref: (idx_ref[1], i, 0))
                 for p in ps]
    return pl.pallas_call(
        body, name=name,
        grid_spec=pltpu.PrefetchScalarGridSpec(num_scalar_prefetch=1, grid=(RS_ROW_SPLIT,), in_specs=in_specs,
                                               out_specs=out_specs),
        out_shape=[jax.ShapeDtypeStruct((2,) + p.shape[1:], F32) for p in ps], compiler_params=_cp(1),
    )(idx, *args)


def _adamw(w, gs, m, v, *, name, dep=None):
    L, Rr, C = w.shape
    tr, tc = _pick(Rr, (256, 128, 64)), C
    if tr == Rr and Rr * C > 512 * 1024:
        tc = 256
    bc1 = 1.0 - ADAM_B1 ** ADAM_STEP
    bc2 = 1.0 - ADAM_B2 ** ADAM_STEP
    nd = 0 if dep is None else 1

    def body(*refs):
        w_ref, m_ref, v_ref = refs[0], refs[1], refs[2]
        g_refs = refs[3:3 + L]
        d_ref, mo_ref, vo_ref, go_ref = refs[3 + L + nd:]
        layer = pl.program_id(0)
        gv = g_refs[0][...]
        for q in range(1, L):
            gv = jnp.where(layer == q, g_refs[q][...], gv)
        mn = ADAM_B1 * m_ref[...] + (1.0 - ADAM_B1) * gv
        vn = ADAM_B2 * v_ref[...] + (1.0 - ADAM_B2) * (gv * gv)
        go_ref[...] = gv
        mo_ref[...] = mn
        vo_ref[...] = vn
        d_ref[...] = -ADAM_LR * ((mn / bc1) / (jnp.sqrt(vn / bc2) + ADAM_EPS) + ADAM_WD * w_ref[...])

    blk = pl.BlockSpec((None, tr, tc), lambda l, i, j: (l, i, j))
    gblks = [pl.BlockSpec((tr, tc), lambda l, i, j, q=q: (jnp.where(l == q, i, 0), jnp.where(l == q, j, 0))) for q in range(L)]
    return pl.pallas_call(
        body, name=name, grid=(L, Rr // tr, C // tc), in_specs=[blk] * 3 + gblks + [_ANY] * nd, out_specs=[blk] * 4,
        out_shape=[jax.ShapeDtypeStruct((L, Rr, C), F32)] * 4, compiler_params=_cp(3),
    )(w, m, v, *gs, *([] if dep is None else [dep]))


def kernel(x, positions, a_norm, a_in_proj, a_conv_w, a_conv_b, a_dt_bias, a_A_log, a_D, a_gnorm, a_out_proj,
           kv_norm, w_kv, b_kv, k_norm, b_norm, w_q, b_q, q_norm, sinks, w_o, b_o, f_norm, f_w_in, f_conv_w,
           f_conv_b, f_w_down, loss_target, m_a_norm, m_a_in_proj, m_a_conv_w, m_a_conv_b, m_a_dt_bias, m_a_A_log,
           m_a_D, m_a_gnorm, m_a_out_proj, m_kv_norm, m_w_kv, m_b_kv, m_k_norm, m_b_norm, m_w_q, m_b_q, m_q_norm,
           m_sinks, m_w_o, m_b_o, m_f_norm, m_f_w_in, m_f_conv_w, m_f_conv_b, m_f_w_down, v_a_norm, v_a_in_proj,
           v_a_conv_w, v_a_conv_b, v_a_dt_bias, v_a_A_log, v_a_D, v_a_gnorm, v_a_out_proj, v_kv_norm, v_w_kv,
           v_b_kv, v_k_norm, v_b_norm, v_w_q, v_b_q, v_q_norm, v_sinks, v_w_o, v_b_o, v_f_norm, v_f_w_in,
           v_f_conv_w, v_f_conv_b, v_f_w_down):
    wl = dict(zip(WEIGHTS, (a_norm, a_in_proj, a_conv_w, a_conv_b, a_dt_bias, a_A_log, a_D, a_gnorm, a_out_proj,
                            kv_norm, w_kv, b_kv, k_norm, b_norm, w_q, b_q, q_norm, sinks, w_o, b_o, f_norm, f_w_in,
                            f_conv_w, f_conv_b, f_w_down)))
    ml = dict(zip(WEIGHTS, (m_a_norm, m_a_in_proj, m_a_conv_w, m_a_conv_b, m_a_dt_bias, m_a_A_log, m_a_D, m_a_gnorm,
                            m_a_out_proj, m_kv_norm, m_w_kv, m_b_kv, m_k_norm, m_b_norm, m_w_q, m_b_q, m_q_norm,
                            m_sinks, m_w_o, m_b_o, m_f_norm, m_f_w_in, m_f_conv_w, m_f_conv_b, m_f_w_down)))
    vl = dict(zip(WEIGHTS, (v_a_norm, v_a_in_proj, v_a_conv_w, v_a_conv_b, v_a_dt_bias, v_a_A_log, v_a_D, v_a_gnorm,
                            v_a_out_proj, v_kv_norm, v_w_kv, v_b_kv, v_k_norm, v_b_norm, v_w_q, v_b_q, v_q_norm,
                            v_sinks, v_w_o, v_b_o, v_f_norm, v_f_w_in, v_f_conv_w, v_f_conv_b, v_f_w_down)))
    xi, yi, ci = _coords()
    me = 2 * xi + yi
    S = x.shape[1]

    def block_of(n, layer):
        t = wl[n]
        return t if layer is None else t[layer]

    rows = lambda t: t.reshape(-1, t.shape[-1])
    c_idx = jnp.reshape(ci, (1,)).astype(jnp.int32)
    me_c = jnp.stack([me, ci]).astype(jnp.int32)
    early = ("in_proj", "out_proj")
    late = tuple(name for name, _, _ in MATS if name not in early)
    shards = {name: _halves(block_of(wn, layer).astype(BF16)) for name, wn, layer in MATS}

    sp = _pack([wl[n] for n, _ in SMALL_CUT], 8, 128, F32)
    gathered, gs = _gather_weights([shards[k] for k in early], sp)
    gt = {k: t.reshape(N_CHIPS, -1, t.shape[-1]) for k, t in zip(early, gathered)}
    started = _gather_start([shards[k] for k in late], gs, name="gather_late_start")
    full = {n: wl[n] for n in SMALL_REP}
    gs = gs.reshape(N_CHIPS, -1)
    pieces = [_unpack(gs[j], [wl[n].shape for n, _ in SMALL_CUT]) for j in range(N_CHIPS)]
    for q, (n, ax) in enumerate(SMALL_CUT):
        full[n] = jnp.concatenate([pieces[j][q] for j in range(N_CHIPS)], axis=ax)
    w = _prep_small(full, {})
    w["w_zx"], w["w_dt"] = _join_in_proj(gt["in_proj"])
    w["a_out_proj"] = rows(gt["out_proj"])
    w["dep"] = started[4]

    class Comm:
        flight = []
        reduced = {}

        def late_weights(self, w, after):
            lands = _gather_wait(started[0], started[1], started[2], started[3], after, name="gather_late_wait")
            lands = _gather_forward(lands, name="gather_late_forward")
            lt = {k: t.reshape(N_CHIPS, -1, t.shape[-1]) for k, t in zip(late, lands)}
            w = dict(w)
            w["w_kv"], w["w_q"], w["w_o"] = (rows(lt[k]) for k in ("w_kv", "w_q", "w_o"))
            w["f_w_in"] = [lt["f_in0"], lt["f_in1"]]
            w["f_w_down"] = [rows(lt["f_down0"]), rows(lt["f_down1"])]
            return w

        def advance(self, after, group=None, tensors=None):
            token = None
            for grp in list(self.flight):
                tag, n = grp["tag"], len(grp["names"])
                dep = list(after) + ([] if token is None else [token])
                if grp["stage"] == "sibling":
                    arrs = _wait_copies(_sibling_copies, grp["send"], grp["recv"], grp["arrays"], dep, name=f"rs_sibling_wait{tag}")
                    pairs = _rs_add_pair(arrs[:n], arrs[n:], c_idx, name=f"rs_add_pair{tag}")
                    send, recv, ps, lands, token = _rs_chips_start(pairs, dep, name=f"rs_chips_start{tag}")
                    grp.update(stage="chips", send=send, recv=recv, ps=ps, lands=lands)
                elif grp["stage"] == "chips":
                    ps, rs = _rs_chips_wait(grp["send"], grp["recv"], grp["ps"], grp["lands"], dep, name=f"rs_chips_wait{tag}")
                    halves = _rs_add_chips(ps, rs, me_c, name=f"rs_add_chips{tag}")
                    send, recv, arrs, token = _start_copies(_join_copies, halves, n, dep, name=f"rs_join_start{tag}")
                    grp.update(stage="join", send=send, recv=recv, arrays=arrs)
                else:
                    joined = _wait_copies(_join_copies, grp["send"], grp["recv"], grp["arrays"], dep, name=f"rs_join_wait{tag}")
                    self.reduced.update({k: rows(t) for k, t in zip(grp["names"], joined)})
                    self.flight.remove(grp)
            if group is not None:
                names = list(tensors)
                glist = [tensors[k].reshape(N_CHIPS, 2, -1, tensors[k].shape[-1]) for k in names]
                lands = [lax.empty((N_CHIPS,) + gq.shape[2:], gq.dtype) for gq in glist]
                dep = list(after) + ([] if token is None else [token])
                send, recv, arrs, token = _start_copies(_sibling_copies, glist + lands, len(names), dep,
                                                        name=f"rs_sibling_start{group}")
                self.flight.append(dict(tag=group, names=names, stage="sibling", send=send, recv=recv, arrays=arrs))
            return token

    comm = Comm()

    posf = positions.reshape(S, 1).astype(F32)
    loss_part, dx0, gr, tok = _local_step(x[0], posf, loss_target[0], w, comm)
    g = _small_grads(gr)

    small_names = [n for n, _ in SMALL_CUT] + list(SMALL_REP)
    sv = _pack([g[n] for n in small_names] + [loss_part[0:1, 0:1]], 8, 128, F32)
    s_send, s_recv, sv, s_land, s_token = _small_start(sv, tok, name="small_start")

    grads, delta, new_m, new_v = {}, {}, {}, {}

    def update(wn, dep):
        gl = [comm.reduced[name] for name, n2, _ in MATS if n2 == wn]
        shp = wl[wn].shape
        three = (len(gl),) + gl[0].shape
        flip = shp[-1] % 128 != 0
        view = (lambda t: t.reshape(three).transpose(0, 2, 1)) if flip else (lambda t: t.reshape(three))
        back = (lambda t: t.transpose(0, 2, 1).reshape(shp)) if flip else (lambda t: t.reshape(shp))
        if flip:
            gl = [t.T for t in gl]
        d, mn, vn, go = _adamw(view(wl[wn]), gl, view(ml[wn]), view(vl[wn]), name="adamw_" + wn, dep=dep)
        grads[wn], delta[wn], new_m[wn], new_v[wn] = back(go), back(d), back(mn), back(vn)
        return d

    first = [update(wn, s_token) for wn in ("w_q", "w_o", "w_kv")]
    tok = comm.advance(first)
    second = [update(wn, tok) for wn in ("f_w_in", "f_w_down", "a_out_proj")]
    comm.advance(second)
    comm.advance(second)
    update("a_in_proj", None)
    done = first + second

    sv, s_land = _small_wait(s_send, s_recv, sv, s_land, done, name="small_wait")
    sred = _small_sum(sv, s_land, jnp.reshape(2 * me + ci, (1,)).astype(jnp.int32)).reshape(-1)
    small_shapes = [g[n].shape for n in small_names] + [(1,)]
    sg = dict(zip(small_names + ["loss"], _unpack(sred, small_shapes)))
    loss = sg["loss"].reshape(())
    g_small = {}
    for n, ax in SMALL_CUT:
        size = wl[n].shape[ax]
        g_small[n] = lax.dynamic_slice_in_dim(sg[n], me * size, size, axis=ax)
    for n in SMALL_REP:
        g_small[n] = sg[n].reshape(wl[n].shape)

    pk = lambda d: _pack([d[n] for n in small_names], 8, 128, F32)[None]
    d, mn, vn, _ = _adamw(pk(wl), [pk(g_small)[0]], pk(ml), pk(vl), name="adamw_small")
    shapes = [wl[n].shape for n in small_names]
    for n, dd, mm, vv in zip(small_names, _unpack(d.reshape(-1), shapes), _unpack(mn.reshape(-1), shapes),
                             _unpack(vn.reshape(-1), shapes)):
        grads[n], delta[n], new_m[n], new_v[n] = g_small[n], dd, mm, vv

    return (loss, dx0[None], *[grads[n] for n in WEIGHTS], *[delta[n] for n in WEIGHTS],
            *[new_m[n] for n in WEIGHTS], *[new_v[n] for n in WEIGHTS])
```

```python
import math

import jax
import jax.numpy as jnp
from jax import lax
from jax.experimental import pallas as pl
from jax.experimental.pallas import tpu as pltpu

F32 = jnp.float32
BF16 = jnp.bfloat16

EPS = 1e-5
CHUNK = 256
WINDOW = 128
HEAD = 64
SSM_HEADS = 32
SSM_GROUPS = 8
SSM_STATE = 128
ATT_KV = 4
ATT_G = 4
ROPE_THETA = 10000.0
NEG = -1e30
N_CHIPS = 4
VMEM_LIMIT = 56 * 1024 * 1024

ADAM_LR, ADAM_B1, ADAM_B2, ADAM_EPS, ADAM_WD, ADAM_STEP = 0.001, 0.9, 0.999, 1e-08, 0.01, 10


def _cp(n_axes):
    return pltpu.CompilerParams(dimension_semantics=("arbitrary",) * n_axes, vmem_limit_bytes=VMEM_LIMIT)


def _pick(dim, prefs):
    for p in prefs:
        if dim % p == 0:
            return p
    return dim


def _iota(shape, dim):
    return lax.broadcasted_iota(jnp.int32, shape, dim)


def _dot(a, b, ca=1, cb=0):
    return lax.dot_general(a, b, (((ca,), (cb,)), ((), ())), preferred_element_type=F32)


def _dot3(x, ind):
    h = x.astype(BF16)
    r = x - h.astype(F32)
    m = r.astype(BF16)
    lo = (r - m.astype(F32)).astype(BF16)
    return _dot(h, ind) + _dot(m, ind) + _dot(lo, ind)


def _sigmoid(x):
    return jax.nn.sigmoid(x)


def _mm(a, b, *, name, ta=False, tb=False, bias=None, res=None, out_dtype=F32, b_koff=0, tm=None, tn=None, tk=None,
        dims=None, a_spec=None, b_spec=None, o_spec=None, o_shape=None, dep=None, more=(), target=None):
    if dims is not None:
        M, N, K = dims
    else:
        if ta:
            K, M = a.shape
        else:
            M, K = a.shape
        N = b.shape[0] if tb else b.shape[1]
    tm = tm or _pick(M, (1024, 1408, 512, 256, 128))
    tn = tn or _pick(N, (512, 1408, 256, 128))
    tk = tk or (K if K <= 2048 else _pick(K, (2048, 1408, 1024, 512)))
    assert M % tm == 0 and N % tn == 0 and K % tk == 0 and b_koff % tk == 0
    nk = K // tk
    kb0 = b_koff // tk
    has_bias, has_res = bias is not None, res is not None

    def body(*refs):
        a_ref, b_ref = refs[0], refs[1]
        pos = 2
        bias_ref = res_ref = acc_ref = None
        if has_bias:
            bias_ref = refs[pos]
            pos += 1
        if has_res:
            res_ref = refs[pos]
            pos += 1
        if dep is not None:
            pos += 1
        extra = refs[pos:pos + 2 * len(more)]
        pos += 2 * len(more)
        tgt_ref = lp_ref = None
        if target is not None:
            tgt_ref, o_ref, lp_ref = refs[pos], refs[pos + 1], refs[pos + 2]
            pos += 2
        else:
            o_ref = refs[pos]
        if nk > 1:
            acc_ref = refs[pos + 1]
        part = _dot(a_ref[...].astype(BF16), b_ref[...].astype(BF16), 0 if ta else 1, 1 if tb else 0)
        for q in range(len(more)):
            part = part + _dot(extra[2 * q][...].astype(BF16), extra[2 * q + 1][...].astype(BF16),
                               0 if ta else 1, 1 if tb else 0)

        def finish(acc):
            if has_bias:
                acc = acc + bias_ref[...]
            if has_res:
                acc = acc + res_ref[...]
            if target is not None:
                err = acc - tgt_ref[...]
                acc = err * (1.0 / N)
                part_loss = jnp.sum(jnp.sum(err * err, axis=1, keepdims=True), axis=0, keepdims=True) * (0.5 / N)
                first = (pl.program_id(0) == 0) & (pl.program_id(1) == 0)

                @pl.when(first)
                def _():
                    lp_ref[...] = jnp.broadcast_to(part_loss, lp_ref.shape)

                @pl.when(jnp.logical_not(first))
                def _():
                    lp_ref[...] += jnp.broadcast_to(part_loss, lp_ref.shape)

            o_ref[...] = acc.astype(out_dtype)

        if nk == 1:
            finish(part)
        else:
            k = pl.program_id(2)

            @pl.when(k == 0)
            def _():
                acc_ref[...] = part

            @pl.when(k > 0)
            def _():
                acc_ref[...] += part

            @pl.when(k == nk - 1)
            def _():
                finish(acc_ref[...])

    if a_spec is None:
        a_spec = pl.BlockSpec((tk, tm), lambda i, j, k: (k, i)) if ta else pl.BlockSpec((tm, tk), lambda i, j, k: (i, k))
    if b_spec is None:
        b_spec = (pl.BlockSpec((tn, tk), lambda i, j, k: (j, k + kb0)) if tb
                  else pl.BlockSpec((tk, tn), lambda i, j, k: (k + kb0, j)))
    if o_spec is None:
        o_spec = pl.BlockSpec((tm, tn), lambda i, j, k: (i, j))
    in_specs, args = [a_spec, b_spec], [a, b]
    if has_bias:
        in_specs.append(pl.BlockSpec((1, tn), lambda i, j, k: (0, j)))
        args.append(bias)
    if has_res:
        in_specs.append(pl.BlockSpec((tm, tn), lambda i, j, k: (i, j)))
        args.append(res)
    if dep is not None:
        in_specs.append(pl.BlockSpec(memory_space=pl.ANY))
        args.append(dep)
    for piece in more:
        a2, sa, b2, sb = piece if len(piece) == 4 else (a, piece[0], b, piece[1])
        in_specs += [sa, sb]
        args += [a2, b2]
    out_specs, out_shape = o_spec, jax.ShapeDtypeStruct(o_shape or (M, N), out_dtype)
    if target is not None:
        in_specs.append(pl.BlockSpec((tm, tn), lambda i, j, k: (i, j)))
        args.append(target)
        out_specs = [o_spec, pl.BlockSpec((8, 128), lambda i, j, k: (0, 0))]
        out_shape = [out_shape, jax.ShapeDtypeStruct((8, 128), F32)]
    return pl.pallas_call(
        body, name=name, grid=(M // tm, N // tn, nk), in_specs=in_specs, out_specs=out_specs, out_shape=out_shape,
        scratch_shapes=[pltpu.VMEM((tm, tn), F32)] if nk > 1 else [],
        compiler_params=_cp(3),
    )(*args)


def _norm_mm(x, gain, b, *, name, bias=None, N=None, tn=None, b_spec=None, dep=None):
    M, K = x.shape
    N = N or b.shape[1]
    tm = _pick(M, (1024, 512, 256))
    tn = tn or _pick(N, (512, 1408, 256, 128))
    has_bias = bias is not None

    def body(*refs):
        x_ref, g_ref, b_ref = refs[:3]
        pos = 3 + (1 if has_bias else 0) + (0 if dep is None else 1)
        o_ref, h_ref = refs[pos], refs[pos + 1]

        @pl.when(pl.program_id(1) == 0)
        def _():
            xv = x_ref[...]
            h_ref[...] = (xv * lax.rsqrt(jnp.mean(xv * xv, axis=-1, keepdims=True) + EPS) * g_ref[...]).astype(BF16)

        acc = _dot(h_ref[...], b_ref[...].astype(BF16))
        if has_bias:
            acc = acc + refs[3][...]
        o_ref[...] = acc

    in_specs = [pl.BlockSpec((tm, K), lambda i, j: (i, 0)), pl.BlockSpec((1, K), lambda i, j: (0, 0)),
                b_spec or pl.BlockSpec((K, tn), lambda i, j: (0, j))]
    args = [x, gain, b]
    if has_bias:
        in_specs.append(pl.BlockSpec((1, tn), lambda i, j: (0, j)))
        args.append(bias)
    if dep is not None:
        in_specs.append(pl.BlockSpec(memory_space=pl.ANY))
        args.append(dep)
    return pl.pallas_call(
        body, name=name, grid=(M // tm, N // tn), in_specs=in_specs,
        out_specs=[pl.BlockSpec((tm, tn), lambda i, j: (i, j)), pl.BlockSpec((tm, K), lambda i, j: (i, 0))],
        out_shape=[jax.ShapeDtypeStruct((M, N), F32), jax.ShapeDtypeStruct((M, K), BF16)], compiler_params=_cp(2),
    )(*args)


def _rms_bwd(x, gains, dhs, dres, *, name, tr=256, want_colsum=False):
    S, D = x.shape
    n = len(gains)
    steps = S // tr

    def body(*refs):
        x_ref = refs[0]
        g_refs = refs[1:1 + n]
        dh_refs = refs[1 + n:1 + 2 * n]
        dres_ref = refs[1 + 2 * n]
        dx_ref = refs[2 + 2 * n]
        dg_refs = refs[3 + 2 * n:3 + 3 * n]
        cs_ref = refs[3 + 3 * n] if want_colsum else None
        i = pl.program_id(0)
        xv = x_ref[...]
        r = lax.rsqrt(jnp.mean(xv * xv, axis=-1, keepdims=True) + EPS)
        xh = xv * r
        dx = dres_ref[...]
        for q in range(n):
            dh = dh_refs[q][...]
            dxh = dh * g_refs[q][...]
            dx = dx + r * (dxh - xh * jnp.mean(dxh * xh, axis=-1, keepdims=True))
            part = jnp.sum(dh * xh, axis=0, keepdims=True)

            @pl.when(i == 0)
            def _():
                dg_refs[q][...] = part

            @pl.when(i > 0)
            def _():
                dg_refs[q][...] += part

        dx_ref[...] = dx
        if want_colsum:
            cpart = jnp.sum(dx, axis=0, keepdims=True)

            @pl.when(i == 0)
            def _():
                cs_ref[...] = cpart

            @pl.when(i > 0)
            def _():
                cs_ref[...] += cpart

    row = pl.BlockSpec((tr, D), lambda i: (i, 0))
    vec = pl.BlockSpec((1, D), lambda i: (0, 0))
    n_vec_out = n + (1 if want_colsum else 0)
    outs = pl.pallas_call(
        body, name=name, grid=(steps,), in_specs=[row] + [vec] * n + [row] * n + [row],
        out_specs=[row] + [vec] * n_vec_out,
        out_shape=[jax.ShapeDtypeStruct((S, D), F32)] + [jax.ShapeDtypeStruct((1, D), F32)] * n_vec_out,
        compiler_params=_cp(1),
    )(x, *gains, *dhs, dres)
    return outs


def _colsum(x, *, name, tr=256):
    S, D = x.shape

    def body(x_ref, o_ref):
        i = pl.program_id(0)
        part = jnp.sum(x_ref[...].astype(F32), axis=0, keepdims=True)

        @pl.when(i == 0)
        def _():
            o_ref[...] = part

        @pl.when(i > 0)
        def _():
            o_ref[...] += part

    return pl.pallas_call(
        body, name=name, grid=(S // tr,), in_specs=[pl.BlockSpec((tr, D), lambda i: (i, 0))],
        out_specs=pl.BlockSpec((1, D), lambda i: (0, 0)), out_shape=jax.ShapeDtypeStruct((1, D), F32),
        compiler_params=_cp(1),
    )(x)


STRIP = 64
HALO = 8


def _strips(S, tc):
    return [(r0, slice(l0, l0 + 128)) for l0 in range(0, tc, 128) for r0 in range(S - STRIP, -1, -STRIP)]


def _with_halo(ref, r0, ls):
    if r0 == 0:
        return jnp.concatenate([jnp.zeros((HALO, 128), F32), ref[0:STRIP, ls]], axis=0)
    return ref[r0 - HALO:r0 + STRIP, ls]


def _conv_strip(xw, w_ref, b_ref, ls, width):
    acc = b_ref[:, ls] + w_ref[pl.ds(width - 1, 1), ls] * xw[HALO:]
    shifted = []
    for s in range(1, width):
        xs = pltpu.roll(xw, s, axis=0)[HALO:]
        shifted.append(xs)
        acc = acc + w_ref[pl.ds(width - 1 - s, 1), ls] * xs
    return acc, shifted


def _conv_strip_back(dacc, after, xc, shifted, w_ref, ls, width):
    ext = jnp.concatenate([dacc, after], axis=0)
    dx = w_ref[pl.ds(width - 1, 1), ls] * dacc
    dws = [None] * width
    dws[width - 1] = jnp.sum(dacc * xc, axis=0, keepdims=True)
    for s in range(1, width):
        dx = dx + w_ref[pl.ds(width - 1 - s, 1), ls] * pltpu.roll(ext, STRIP + HALO - s, axis=0)[:STRIP]
        dws[width - 1 - s] = jnp.sum(dacc * shifted[s - 1], axis=0, keepdims=True)
    return dx, dws, jnp.sum(dacc, axis=0, keepdims=True)


def _conv_back_block(S, tc, width, w_ref, b_ref, x_ref, dacc_of, dx_store, dw_ref, db_ref):
    for l0 in range(0, tc, 128):
        ls = slice(l0, l0 + 128)
        after = jnp.zeros((HALO, 128), F32)
        tot = None
        for r0 in range(S - STRIP, -1, -STRIP):
            xw = _with_halo(x_ref, r0, ls)
            acc, shifted = _conv_strip(xw, w_ref, b_ref, ls, width)
            dacc = dacc_of(r0, ls, acc, _sigmoid(acc))
            dx, dws, db = _conv_strip_back(dacc, after, xw[HALO:], shifted, w_ref, ls, width)
            dx_store(r0, ls, dx)
            after = dacc[:HALO]
            part = dws + [db]
            tot = part if tot is None else [p + q for p, q in zip(tot, part)]
        for k in range(width):
            dw_ref[pl.ds(k, 1), ls] = tot[k]
        db_ref[:, ls] = tot[width]


def _conv_silu_fwd(xin, col0, C, w, b, *, name, tc=512):
    S = xin.shape[0]
    width = w.shape[0]
    off = col0 // tc

    def body(x_ref, w_ref, b_ref, o_ref):
        for r0, ls in _strips(S, tc):
            acc, _ = _conv_strip(_with_halo(x_ref, r0, ls), w_ref, b_ref, ls, width)
            o_ref[r0:r0 + STRIP, ls] = acc * _sigmoid(acc)

    return pl.pallas_call(
        body, name=name, grid=(C // tc,),
        in_specs=[pl.BlockSpec((S, tc), lambda j: (0, j + off)), pl.BlockSpec((width, tc), lambda j: (0, j)),
                  pl.BlockSpec((1, tc), lambda j: (0, j))],
        out_specs=pl.BlockSpec((S, tc), lambda j: (0, j)), out_shape=jax.ShapeDtypeStruct((S, C), F32),
        compiler_params=_cp(1),
    )(xin, w, b)


def _conv_silu_bwd(xin, col0, C, w, b, douts, *, name, tc=256):
    S = xin.shape[0]
    width = w.shape[0]
    off = col0 // tc
    nd = len(douts)
    ranges = [(o // tc, (o + d.shape[1]) // tc) for d, o in douts]

    def body(*refs):
        x_ref, w_ref, b_ref = refs[0], refs[1], refs[2]
        d_refs = refs[3:3 + nd]
        dx_ref, dw_ref, db_ref = refs[3 + nd], refs[4 + nd], refs[5 + nd]
        j = pl.program_id(0)

        def dacc_of(r0, ls, acc, sg):
            dout = jnp.zeros((STRIP, 128), F32)
            for q in range(nd):
                lo, hi = ranges[q]
                dout = dout + jnp.where((j >= lo) & (j < hi), d_refs[q][r0:r0 + STRIP, ls], 0.0)
            return dout * (sg * (1.0 + acc * (1.0 - sg)))

        def dx_store(r0, ls, dx):
            dx_ref[r0:r0 + STRIP, ls] = dx.astype(BF16)

        _conv_back_block(S, tc, width, w_ref, b_ref, x_ref, dacc_of, dx_store, dw_ref, db_ref)

    d_specs = [pl.BlockSpec((S, tc), (lambda j, lo=lo, hi=hi: (0, jnp.clip(j - lo, 0, hi - lo - 1)))) for lo, hi in ranges]
    return pl.pallas_call(
        body, name=name, grid=(C // tc,),
        in_specs=[pl.BlockSpec((S, tc), lambda j: (0, j + off)), pl.BlockSpec((width, tc), lambda j: (0, j)),
                  pl.BlockSpec((1, tc), lambda j: (0, j))] + d_specs,
        out_specs=[pl.BlockSpec((S, tc), lambda j: (0, j)), pl.BlockSpec((width, tc), lambda j: (0, j)),
                   pl.BlockSpec((1, tc), lambda j: (0, j))],
        out_shape=[jax.ShapeDtypeStruct((S, C), BF16), jax.ShapeDtypeStruct((width, C), F32),
                   jax.ShapeDtypeStruct((1, C), F32)],
        compiler_params=_cp(1),
    )(xin, w, b, *[d for d, _ in douts])


def _ffn_act_fwd(u, w, b, *, name, tc=256):
    S, F2 = u.shape
    Fd = F2 // 2
    width = w.shape[0]
    nb = Fd // tc

    def body(g_ref, v_ref, w_ref, b_ref, o_ref):
        for r0, ls in _strips(S, tc):
            acc, _ = _conv_strip(_with_halo(g_ref, r0, ls), w_ref, b_ref, ls, width)
            o_ref[r0:r0 + STRIP, ls] = (acc * _sigmoid(acc) * v_ref[r0:r0 + STRIP, ls]).astype(BF16)

    return pl.pallas_call(
        body, name=name, grid=(nb,),
        in_specs=[pl.BlockSpec((S, tc), lambda j: (0, j)), pl.BlockSpec((S, tc), lambda j: (0, j + nb)),
                  pl.BlockSpec((width, tc), lambda j: (0, j)), pl.BlockSpec((1, tc), lambda j: (0, j))],
        out_specs=pl.BlockSpec((S, tc), lambda j: (0, j)), out_shape=jax.ShapeDtypeStruct((S, Fd), BF16),
        compiler_params=_cp(1),
    )(u, u, w, b)


def _ffn_act_bwd(u, w, b, da, *, name, tc=256):
    S, F2 = u.shape
    Fd = F2 // 2
    width = w.shape[0]
    nb = Fd // tc

    def body(g_ref, v_ref, w_ref, b_ref, da_ref, du_ref, dw_ref, db_ref, a_ref):
        def dacc_of(r0, ls, acc, sg):
            rs = slice(r0, r0 + STRIP)
            dav, val, silu = da_ref[rs, ls], v_ref[rs, ls], acc * sg
            a_ref[rs, ls] = (silu * val).astype(BF16)
            du_ref[1, rs, ls] = (dav * silu).astype(BF16)
            return dav * val * (sg * (1.0 + acc * (1.0 - sg)))

        def dx_store(r0, ls, dx):
            du_ref[0, r0:r0 + STRIP, ls] = dx.astype(BF16)

        _conv_back_block(S, tc, width, w_ref, b_ref, g_ref, dacc_of, dx_store, dw_ref, db_ref)

    blk = pl.BlockSpec((S, tc), lambda j: (0, j))
    return pl.pallas_call(
        body, name=name, grid=(nb,),
        in_specs=[blk, pl.BlockSpec((S, tc), lambda j: (0, j + nb)), pl.BlockSpec((width, tc), lambda j: (0, j)),
                  pl.BlockSpec((1, tc), lambda j: (0, j)), blk],
        out_specs=[pl.BlockSpec((2, S, tc), lambda j: (0, 0, j)), pl.BlockSpec((width, tc), lambda j: (0, j)),
                   pl.BlockSpec((1, tc), lambda j: (0, j)), blk],
        out_shape=[jax.ShapeDtypeStruct((2, S, Fd), BF16),
                   jax.ShapeDtypeStruct((width, Fd), F32), jax.ShapeDtypeStruct((1, Fd), F32),
                   jax.ShapeDtypeStruct((S, Fd), BF16)],
        compiler_params=_cp(1),
    )(u, u, w, b, da)


def _ssd_prep(dtr, dt_bias, a_log, *, name="ssd_prep"):
    S = dtr.shape[0]

    def body(d_ref, b_ref, al_ref, dt_ref, ac_ref, sg_ref, act_ref):
        lane = _iota((CHUNK, 128), 1)
        valid = lane < SSM_HEADS
        z = d_ref[...] + b_ref[...]
        dt = jnp.where(valid, jnp.maximum(z, 0.0) + jnp.log(1.0 + jnp.exp(-jnp.abs(z))), 0.0)
        a = dt * (-jnp.exp(al_ref[...]))
        row = _iota((CHUNK, 128), 0)
        k = 1
        while k < CHUNK:
            a = a + jnp.where(row >= k, pltpu.roll(a, k, axis=0), 0.0)
            k *= 2
        sg = jnp.where(valid, _sigmoid(z), 0.0)
        for arr, ref in ((dt, dt_ref), (a, ac_ref), (sg, sg_ref)):
            for g in range(SSM_GROUPS):
                ref[g] = jnp.where(lane < 4, arr if g == 0 else pltpu.roll(arr, 128 - 4 * g, axis=1), 0.0)
        act_ref[...] = a.T[:SSM_HEADS, :]

    blk = pl.BlockSpec((CHUNK, 128), lambda i: (i, 0))
    vec = pl.BlockSpec((1, 128), lambda i: (0, 0))
    grp = pl.BlockSpec((SSM_GROUPS, CHUNK, 128), lambda i: (0, i, 0))
    return pl.pallas_call(
        body, name=name, grid=(S // CHUNK,), in_specs=[blk, vec, vec],
        out_specs=[grp, grp, grp, pl.BlockSpec((SSM_HEADS, CHUNK), lambda i: (0, i))],
        out_shape=[jax.ShapeDtypeStruct((SSM_GROUPS, S, 128), F32)] * 3 + [jax.ShapeDtypeStruct((SSM_HEADS, S), F32)],
        compiler_params=_cp(1),
    )(dtr, dt_bias, a_log)


SSD_GPS = 4


def _expand4(v, lanes):
    out = jnp.broadcast_to(v[:, 3:4], lanes.shape)
    for hh in (2, 1, 0):
        out = jnp.where(lanes < 64 * (hh + 1), v[:, hh:hh + 1], out)
    return out


def _ssd_fwd(xbc, dt_g, ac_g, ac_t, *, name="ssd_fwd"):
    S = xbc.shape[0]
    nc = S // CHUNK
    Lc = CHUNK

    def body(x_ref, b_ref, c_ref, dt_ref, ac_ref, act_ref, y_ref, st_out_ref, st_ref):
        g2 = pl.program_id(0)
        c = pl.program_id(1)

        @pl.when(c == 0)
        def _():
            st_ref[...] = jnp.zeros_like(st_ref)

        causal = _iota((Lc, Lc), 0) >= _iota((Lc, Lc), 1)
        lane256 = _iota((Lc, 256), 1)
        lane128 = _iota((Lc, 128), 1)
        row128 = _iota((128, 128), 0)
        for gg in range(SSD_GPS):
            g = SSD_GPS * g2 + gg
            bv = b_ref[:, 128 * gg:128 * (gg + 1)]
            cbf = c_ref[:, 128 * gg:128 * (gg + 1)].astype(BF16)
            cb = _dot(cbf, bv.astype(BF16), 1, 1)
            dtg, acg = dt_ref[gg], ac_ref[gg]
            ac_last = ac_ref[gg, pl.ds(Lc - 1, 1), :]
            dt4 = _expand4(dtg, lane256)
            ac4 = _expand4(acg, lane256)
            e4 = jnp.exp(ac4)
            xdb = (x_ref[:, 256 * gg:256 * (gg + 1)] * dt4).astype(BF16)
            st_out_ref[gg] = st_ref[gg]
            for p in range(2):
                xd_p = xdb[:, 128 * p:128 * (p + 1)]
                st_p = st_ref[gg, p]
                ys, sn, cds = [], [], []
                for q in range(2):
                    hh = 2 * p + q
                    a_col = acg[:, hh:hh + 1]
                    a_row = act_ref[pl.ds(4 * g + hh, 1), :]
                    dec = jnp.exp(jnp.where(causal, a_col - a_row, NEG))
                    w = (cb * dec).astype(BF16)
                    ys.append(_dot(w, xd_p))
                    al = ac_last[:, hh:hh + 1]
                    dte = jnp.exp(al - a_col)
                    sn.append(_dot(xd_p, (bv * dte).astype(BF16), 0, 0))
                    cds.append(jnp.exp(al))
                y_diag = jnp.where(lane128 < 64, ys[0], ys[1])
                y_off = _dot(cbf, st_p.astype(BF16), 1, 1) * e4[:, 128 * p:128 * (p + 1)]
                y_ref[:, 256 * gg + 128 * p:256 * gg + 128 * (p + 1)] = y_diag + y_off
                st_ref[gg, p] = jnp.where(row128 < 64, st_p * cds[0] + sn[0], st_p * cds[1] + sn[1])

    G = SSD_GPS
    per_g = lambda g, c: (g, c, 0)
    return pl.pallas_call(
        body, name=name, grid=(SSM_GROUPS // G, nc),
        in_specs=[pl.BlockSpec((Lc, 256 * G), lambda g, c: (c, g)),
                  pl.BlockSpec((Lc, 128 * G), lambda g, c: (c, 16 // G + g)),
                  pl.BlockSpec((Lc, 128 * G), lambda g, c: (c, 24 // G + g)),
                  pl.BlockSpec((G, Lc, 128), per_g), pl.BlockSpec((G, Lc, 128), per_g),
                  pl.BlockSpec((SSM_HEADS, Lc), lambda g, c: (0, c))],
        out_specs=[pl.BlockSpec((Lc, 256 * G), lambda g, c: (c, g)),
                   pl.BlockSpec((G, None, 2, 128, 128), lambda g, c: (g, c, 0, 0, 0))],
        out_shape=[jax.ShapeDtypeStruct((S, 2048), F32), jax.ShapeDtypeStruct((SSM_GROUPS, nc, 2, 128, 128), F32)],
        scratch_shapes=[pltpu.VMEM((G, 2, 128, 128), F32)], compiler_params=_cp(2),
    )(xbc, xbc, xbc, dt_g, ac_g, ac_t)


def _ssd_bwd(xbc, dt_g, ac_g, ac_t, states, dy, dexp, *, name="ssd_bwd", dep=None):
    S = xbc.shape[0]
    nc = S // CHUNK
    Lc = CHUNK

    def body(x_ref, b_ref, c_ref, dt_ref, ac_ref, act_ref, st_ref, dy_ref, d_ref, *rest):
        dx_ref, db_ref, dc_ref, dh_ref, ds_ref = rest[-5:]
        g2 = pl.program_id(0)
        cc = pl.program_id(1)

        @pl.when(cc == 0)
        def _():
            ds_ref[...] = jnp.zeros_like(ds_ref)

        causal = _iota((Lc, Lc), 0) >= _iota((Lc, Lc), 1)
        lane256 = _iota((Lc, 256), 1)
        lane128 = _iota((Lc, 128), 1)
        row128 = _iota((128, 128), 0)
        ind_rows = _iota((256, 128), 0) >> 6
        ind_cols = _iota((256, 128), 1)
        ind_a = (ind_rows == ind_cols).astype(BF16)
        ind_b = (ind_rows + 4 == ind_cols).astype(BF16)
        for gg in range(SSD_GPS):
            g = SSD_GPS * g2 + gg
            bv = b_ref[:, 128 * gg:128 * (gg + 1)]
            cv = c_ref[:, 128 * gg:128 * (gg + 1)]
            bbf, cbf = bv.astype(BF16), cv.astype(BF16)
            cb = _dot(cbf, bbf, 1, 1)
            dtg, acg = dt_ref[gg], ac_ref[gg]
            ac_last = ac_ref[gg, pl.ds(Lc - 1, 1), :]
            dt4 = _expand4(dtg, lane256)
            ac4 = _expand4(acg, lane256)
            acl4 = _expand4(ac_last, _iota((1, 256), 1))
            e4 = jnp.exp(ac4)
            dte4 = jnp.exp(acl4 - ac4)
            xv = x_ref[:, 256 * gg:256 * (gg + 1)]
            xd = xv * dt4
            xdb = xd.astype(BF16)
            dyv = dy_ref[:, 256 * gg:256 * (gg + 1)]
            dcb = jnp.zeros((Lc, Lc), F32)
            dc_acc = jnp.zeros((Lc, 128), F32)
            db_acc = jnp.zeros((Lc, 128), F32)
            u_parts, dxd_parts, ends = [], [], []
            for p in range(2):
                sl = slice(128 * p, 128 * (p + 1))
                xd_p, xdb_p, dy_p = xd[:, sl], xdb[:, sl], dyv[:, sl]
                dyb_p = dy_p.astype(BF16)
                e_p, dte_p = e4[:, sl], dte4[:, sl]
                sp = st_ref[gg, p]
                spb = sp.astype(BF16)
                dsn = ds_ref[gg, p]
                dsnb = dsn.astype(BF16)
                yds, dxds, cds = [], [], []
                for q in range(2):
                    hh = 2 * p + q
                    a_col = acg[:, hh:hh + 1]
                    a_row = act_ref[pl.ds(4 * g + hh, 1), :]
                    dec = jnp.exp(jnp.where(causal, a_col - a_row, NEG))
                    w = (cb * dec).astype(BF16)
                    head = (lane128 < 64) if q == 0 else (lane128 >= 64)
                    dym = jnp.where(head, dyb_p, jnp.zeros_like(dyb_p))
                    dw = _dot(dym, xdb_p, 1, 1)
                    dcb = dcb + dw * dec
                    yds.append(_dot(w, xdb_p))
                    dxds.append(_dot(w, dyb_p, 0, 0))
                    cds.append(jnp.exp(ac_last[:, hh:hh + 1]))
                y_diag = jnp.where(lane128 < 64, yds[0], yds[1])
                dxd_diag = jnp.where(lane128 < 64, dxds[0], dxds[1])
                y_off = _dot(cbf, spb, 1, 1) * e_p
                dgp = dy_p * e_p
                dgb = dgp.astype(BF16)
                dc_acc = dc_acc + _dot(dgb, spb)
                dsp = _dot(dgb, cbf, 0, 0)
                cd_col = jnp.where(row128[:, 0:1] < 64, cds[0], cds[1])
                qm = _dot(bbf, dsnb, 1, 1)
                dxd_state = dte_p * qm
                db_acc = db_acc + _dot((xd_p * dte_p).astype(BF16), dsnb)
                t_p = xd_p * dxd_state
                prod = dsn * sp
                e0 = jnp.sum(jnp.sum(jnp.where(row128 < 64, prod, 0.0), axis=1, keepdims=True), axis=0, keepdims=True)
                e1 = jnp.sum(jnp.sum(jnp.where(row128 >= 64, prod, 0.0), axis=1, keepdims=True), axis=0, keepdims=True)
                tcol = jnp.sum(t_p, axis=0, keepdims=True)
                lane1 = _iota((1, 128), 1)
                t0 = jnp.sum(jnp.where(lane1 < 64, tcol, 0.0), axis=1, keepdims=True)
                t1 = jnp.sum(jnp.where(lane1 >= 64, tcol, 0.0), axis=1, keepdims=True)
                ends.append(e0 * cds[0] + t0)
                ends.append(e1 * cds[1] + t1)
                ds_ref[gg, p] = dsn * cd_col + dsp
                u_parts.append(dyb_p.astype(F32) * y_diag - xdb_p.astype(F32) * dxd_diag + dy_p * y_off - t_p)
                dxd_parts.append(dxd_diag + dxd_state)
            dxd = jnp.concatenate(dxd_parts, axis=1)
            u_all = jnp.concatenate(u_parts, axis=1)
            dx_ref[:, 256 * gg:256 * (gg + 1)] = dxd * dt4 + dyv * d_ref[:, 256 * gg:256 * (gg + 1)]
            dcbb = dcb.astype(BF16)
            dc_ref[:, 128 * gg:128 * (gg + 1)] = dc_acc + _dot(dcbb, bbf)
            db_ref[:, 128 * gg:128 * (gg + 1)] = db_acc + _dot(dcbb, cbf, 0, 0)
            lane = _iota((Lc, 128), 1)
            endv = jnp.zeros((Lc, 128), F32)
            for hh in range(4):
                endv = jnp.where(lane == 8 + hh, ends[hh], endv)
            dh_ref[gg] = _dot3(dxd * xv, ind_a) + _dot3(u_all, ind_b) + endv

    G = SSD_GPS
    rev = lambda c: nc - 1 - c
    per_g = lambda g, c: (g, rev(c), 0)
    return pl.pallas_call(
        body, name=name, grid=(SSM_GROUPS // G, nc),
        in_specs=[pl.BlockSpec((Lc, 256 * G), lambda g, c: (rev(c), g)),
                  pl.BlockSpec((Lc, 128 * G), lambda g, c: (rev(c), 16 // G + g)),
                  pl.BlockSpec((Lc, 128 * G), lambda g, c: (rev(c), 24 // G + g)),
                  pl.BlockSpec((G, Lc, 128), per_g), pl.BlockSpec((G, Lc, 128), per_g),
                  pl.BlockSpec((SSM_HEADS, Lc), lambda g, c: (0, rev(c))),
                  pl.BlockSpec((G, None, 2, 128, 128), lambda g, c: (g, rev(c), 0, 0, 0)),
                  pl.BlockSpec((Lc, 256 * G), lambda g, c: (rev(c), g)),
                  pl.BlockSpec((1, 256 * G), lambda g, c: (0, g))] + ([] if dep is None else [pl.BlockSpec(memory_space=pl.ANY)]),
        out_specs=[pl.BlockSpec((Lc, 256 * G), lambda g, c: (rev(c), g)),
                   pl.BlockSpec((Lc, 128 * G), lambda g, c: (rev(c), g)),
                   pl.BlockSpec((Lc, 128 * G), lambda g, c: (rev(c), g)),
                   pl.BlockSpec((G, Lc, 128), per_g)],
        out_shape=[jax.ShapeDtypeStruct((S, 2048), F32), jax.ShapeDtypeStruct((S, 1024), F32),
                   jax.ShapeDtypeStruct((S, 1024), F32), jax.ShapeDtypeStruct((SSM_GROUPS, S, 128), F32)],
        scratch_shapes=[pltpu.VMEM((G, 2, 128, 128), F32)], compiler_params=_cp(2),
    )(xbc, xbc, xbc, dt_g, ac_g, ac_t, states, dy, dexp, *([] if dep is None else [dep]))


def _ssd_post(dhead, dt_g, sg_g, alog_g, *, name="ssd_post"):
    S = dhead.shape[1]
    nc = S // CHUNK
    Lc = CHUNK

    def body(dh_ref, dt_ref, sg_ref, al_ref, o_ref, s_ref):
        @pl.when(pl.program_id(0) == 0)
        def _():
            s_ref[...] = jnp.zeros_like(s_ref)

        lane = _iota((Lc, 128), 1)
        row = _iota((Lc, 128), 0)
        row8 = _iota((8, 128), 0)
        out = jnp.zeros((Lc, 128), F32)
        for g in range(SSM_GROUPS):
            dh = dh_ref[g]
            a_neg = -jnp.exp(al_ref[g])
            dac = jnp.where(lane < 4, pltpu.roll(dh, 124, axis=1), 0.0)
            end = jnp.where(lane < 4, pltpu.roll(dh, 120, axis=1), 0.0)
            k = 1
            while k < Lc:
                dac = dac + jnp.where(row < Lc - k, pltpu.roll(dac, Lc - k, axis=0), 0.0)
                k *= 2
            da = dac + end
            ddt = jnp.where(lane < 4, da * a_neg + dh, 0.0)
            ddtr = ddt * sg_ref[g]
            out = out + (ddtr if g == 0 else pltpu.roll(ddtr, 4 * g, axis=1))
            dal = jnp.sum(da * dt_ref[g], axis=0, keepdims=True) * a_neg
            dbias = jnp.sum(ddtr, axis=0, keepdims=True)
            part = jnp.where(row8 == 0, dal, jnp.where(row8 == 1, dbias, 0.0))
            s_ref[g] += part
        o_ref[...] = out.astype(BF16)

    grp = pl.BlockSpec((SSM_GROUPS, Lc, 128), lambda c: (0, c, 0))
    whole = lambda r: pl.BlockSpec((SSM_GROUPS, r, 128), lambda c: (0, 0, 0))
    return pl.pallas_call(
        body, name=name, grid=(nc,), in_specs=[grp, grp, grp, whole(1)],
        out_specs=[pl.BlockSpec((Lc, 128), lambda c: (c, 0)), whole(8)],
        out_shape=[jax.ShapeDtypeStruct((S, 128), BF16), jax.ShapeDtypeStruct((SSM_GROUPS, 8, 128), F32)],
        compiler_params=_cp(1),
    )(dhead, dt_g, sg_g, alog_g)


def _gate_fwd(y, xbc, zx, dexp, gn, *, name="gate_fwd", tr=256):
    S = y.shape[0]
    W = 2048
    gw = W // SSM_GROUPS

    def body(y_ref, x_ref, z_ref, d_ref, g_ref, o_ref):
        z = z_ref[...]
        u = (y_ref[...] + x_ref[...] * d_ref[...]) * (z * _sigmoid(z))
        gv = g_ref[...]
        for q in range(SSM_GROUPS):
            sl = slice(gw * q, gw * (q + 1))
            uq = u[:, sl]
            r = lax.rsqrt(jnp.mean(uq * uq, axis=-1, keepdims=True) + EPS)
            o_ref[:, sl] = (uq * r * gv[:, sl]).astype(BF16)

    row = pl.BlockSpec((tr, W), lambda i: (i, 0))
    vec = pl.BlockSpec((1, W), lambda i: (0, 0))
    return pl.pallas_call(
        body, name=name, grid=(S // tr,), in_specs=[row, row, row, vec, vec], out_specs=row,
        out_shape=jax.ShapeDtypeStruct((S, W), BF16), compiler_params=_cp(1),
    )(y, xbc, zx, dexp, gn)


def _gate_bwd(y, xbc, zx, dexp, gn, dout, *, name="gate_bwd", tr=256):
    S = y.shape[0]
    W = 2048
    gw = W // SSM_GROUPS
    steps = S // tr

    def body(y_ref, x_ref, z_ref, d_ref, g_ref, do_ref, dy_ref, dz_ref, dg_ref, dd_ref, acc_ref):
        i = pl.program_id(0)

        @pl.when(i == 0)
        def _():
            acc_ref[...] = jnp.zeros_like(acc_ref)

        z = z_ref[...]
        sg = _sigmoid(z)
        sz = z * sg
        xs = x_ref[...]
        yt = y_ref[...] + xs * d_ref[...]
        u = yt * sz
        gv = g_ref[...]
        do = do_ref[...]
        dgs = []
        for q in range(SSM_GROUPS):
            sl = slice(gw * q, gw * (q + 1))
            uq = u[:, sl]
            r = lax.rsqrt(jnp.mean(uq * uq, axis=-1, keepdims=True) + EPS)
            uh = uq * r
            dq = do[:, sl]
            duh = dq * gv[:, sl]
            duq = r * (duh - uh * jnp.mean(duh * uh, axis=-1, keepdims=True))
            dgs.append(jnp.sum(dq * uh, axis=0, keepdims=True))
            dyt = duq * sz[:, sl]
            dy_ref[:, sl] = dyt
            dz_ref[:, sl] = (duq * yt[:, sl] * (sg[:, sl] * (1.0 + z[:, sl] * (1.0 - sg[:, sl])))).astype(BF16)
            acc_ref[:, sl] += jnp.sum(dyt * xs[:, sl], axis=0, keepdims=True)
        dg = jnp.concatenate(dgs, axis=1)

        @pl.when(i == 0)
        def _():
            dg_ref[...] = dg

        @pl.when(i > 0)
        def _():
            dg_ref[...] += dg

        @pl.when(i == steps - 1)
        def _():
            ind = ((_iota((W, 128), 0) >> 6) == _iota((W, 128), 1)).astype(BF16)
            dd_ref[...] = _dot3(jnp.broadcast_to(acc_ref[...], (8, W)), ind)[0:1, :]

    row = pl.BlockSpec((tr, W), lambda i: (i, 0))
    vec = pl.BlockSpec((1, W), lambda i: (0, 0))
    return pl.pallas_call(
        body, name=name, grid=(steps,), in_specs=[row, row, row, vec, vec, row],
        out_specs=[row, row, vec, pl.BlockSpec((1, 128), lambda i: (0, 0))],
        out_shape=[jax.ShapeDtypeStruct((S, W), F32), jax.ShapeDtypeStruct((S, W), BF16),
                   jax.ShapeDtypeStruct((1, W), F32), jax.ShapeDtypeStruct((1, 128), F32)],
        scratch_shapes=[pltpu.VMEM((1, W), F32)], compiler_params=_cp(1),
    )(y, xbc, zx, dexp, gn, dout)


def _rope_cs(posf, *, name="rope_tables", tr=256):
    S = posf.shape[0]

    def body(p_ref, c_ref, s_ref):
        j = (_iota((tr, 128), 1) & 31).astype(F32)
        ang = p_ref[...] * jnp.exp(j * (-math.log(ROPE_THETA) / 32.0))
        c_ref[...] = jnp.cos(ang)
        s_ref[...] = jnp.sin(ang)

    blk = pl.BlockSpec((tr, 128), lambda i: (i, 0))
    return pl.pallas_call(
        body, name=name, grid=(S // tr,), in_specs=[pl.BlockSpec((tr, 1), lambda i: (i, 0))], out_specs=[blk, blk],
        out_shape=[jax.ShapeDtypeStruct((S, 128), F32)] * 2, compiler_params=_cp(1),
    )(posf)


def _rope_tables(c_ref, s_ref, shape):
    reps = shape[1] // 128
    return jnp.tile(c_ref[...], (1, reps)), jnp.tile(s_ref[...], (1, reps)), (_iota(shape, 1) & 63) < 32


def _hn_inds(W):
    ind = ((_iota((W, 128), 0) >> 6) == _iota((W, 128), 1)).astype(BF16)
    ind_t = ((_iota((128, W), 1) >> 6) == _iota((128, W), 0)).astype(BF16)
    return ind, ind_t


def _hnrope_fwd(xin, col0, W, gain_w, rope, *, name, tr=256):
    S = xin.shape[0]
    off = col0 // W
    nh = W // HEAD

    def body(x_ref, g_ref, c_ref, s_ref, o_ref):
        x = x_ref[...]
        ind, ind_t = _hn_inds(W)
        r = lax.rsqrt(_dot3(x * x, ind) * (1.0 / HEAD) + EPS)
        xn = x * _dot3(r, ind_t) * g_ref[...]
        cs, sn, half = _rope_tables(c_ref, s_ref, (tr, W))
        rot = jnp.where(half, -pltpu.roll(xn, W - 32, axis=1), pltpu.roll(xn, 32, axis=1))
        out = (xn * cs + rot * sn).astype(BF16)
        for h in range(nh):
            o_ref[h] = out[:, HEAD * h:HEAD * (h + 1)]

    tab = pl.BlockSpec((tr, 128), lambda i: (i, 0))
    return pl.pallas_call(
        body, name=name, grid=(S // tr,),
        in_specs=[pl.BlockSpec((tr, W), lambda i: (i, off)), pl.BlockSpec((1, W), lambda i: (0, 0)), tab, tab],
        out_specs=pl.BlockSpec((nh, tr, HEAD), lambda i: (0, i, 0)), out_shape=jax.ShapeDtypeStruct((nh, S, HEAD), BF16),
        compiler_params=_cp(1),
    )(xin, gain_w, *rope)


def _hnrope_bwd(xin, col0, W, gain_w, rope, dout, *, name, tr=256):
    S = xin.shape[0]
    off = col0 // W
    steps = S // tr
    nh = W // HEAD

    def body(x_ref, g_ref, c_ref, s_ref, do_ref, dx_ref, cs_ref, dg_ref, acc_ref):
        i = pl.program_id(0)
        x = x_ref[...]
        ind, ind_t = _hn_inds(W)
        r = lax.rsqrt(_dot3(x * x, ind) * (1.0 / HEAD) + EPS)
        rw = _dot3(r, ind_t)
        xh = x * rw
        cs, sn, half = _rope_tables(c_ref, s_ref, (tr, W))
        do = jnp.concatenate([do_ref[h] for h in range(nh)], axis=1).astype(F32)
        gs = do * sn
        g1 = do * cs + jnp.where(half, pltpu.roll(gs, W - 32, axis=1), -pltpu.roll(gs, 32, axis=1))
        dxh = g1 * g_ref[...]
        t = _dot3(dxh * xh, ind) * (1.0 / HEAD)
        dx = rw * (dxh - xh * _dot3(t, ind_t))
        dx_ref[...] = dx.astype(BF16)
        cpart = jnp.sum(dx, axis=0, keepdims=True)
        gpart = jnp.sum(g1 * xh, axis=0, keepdims=True)

        @pl.when(i == 0)
        def _():
            cs_ref[...] = cpart
            acc_ref[...] = gpart

        @pl.when(i > 0)
        def _():
            cs_ref[...] += cpart
            acc_ref[...] += gpart

        @pl.when(i == steps - 1)
        def _():
            fold = ((_iota((W, 128), 0) & 63) == _iota((W, 128), 1)).astype(BF16)
            dg_ref[...] = _dot3(jnp.broadcast_to(acc_ref[...], (8, W)), fold)[0:1, :]

    tab = pl.BlockSpec((tr, 128), lambda i: (i, 0))
    return pl.pallas_call(
        body, name=name, grid=(steps,),
        in_specs=[pl.BlockSpec((tr, W), lambda i: (i, off)), pl.BlockSpec((1, W), lambda i: (0, 0)), tab, tab,
                  pl.BlockSpec((nh, tr, HEAD), lambda i: (0, i, 0))],
        out_specs=[pl.BlockSpec((tr, W), lambda i: (i, 0)), pl.BlockSpec((1, W), lambda i: (0, 0)),
                   pl.BlockSpec((1, 128), lambda i: (0, 0))],
        out_shape=[jax.ShapeDtypeStruct((S, W), BF16), jax.ShapeDtypeStruct((1, W), F32),
                   jax.ShapeDtypeStruct((1, 128), F32)],
        scratch_shapes=[pltpu.VMEM((1, W), F32)], compiler_params=_cp(1),
    )(xin, gain_w, *rope, dout)


def _attn_band():
    qi = jnp.arange(ATT_G * WINDOW)[:, None] % WINDOW
    ki = jnp.arange(2 * WINDOW)[None, :]
    rel = qi + WINDOW - ki
    ok = (rel >= 0) & (rel < WINDOW)
    return jnp.stack([jnp.where(ok & (ki >= WINDOW), 0.0, NEG), jnp.where(ok, 0.0, NEG)]).astype(F32)


def _attn_probs(q, kb, sink_ref, band_ref, h, i):
    s = _dot(q, kb, 1, 1) * (HEAD ** -0.5) + band_ref[jnp.minimum(i, 1)]
    r1 = _iota((4 * WINDOW, 1), 0)
    sink = jnp.where(r1 < WINDOW, sink_ref[4 * h], jnp.where(r1 < 2 * WINDOW, sink_ref[4 * h + 1],
                     jnp.where(r1 < 3 * WINDOW, sink_ref[4 * h + 2], sink_ref[4 * h + 3])))
    m = jnp.maximum(jnp.max(s, axis=1, keepdims=True), sink)
    p = jnp.exp(s - m)
    ps = jnp.exp(sink - m)
    inv = 1.0 / (jnp.sum(p, axis=1, keepdims=True) + ps)
    return p * inv, ps * inv


ATT_HPS = 4
_BAND = pl.BlockSpec((2, ATT_G * WINDOW, 2 * WINDOW), lambda h, i: (0, 0, 0))


def _attn_specs(S):
    qspec = pl.BlockSpec((ATT_HPS, ATT_G, WINDOW, HEAD), lambda h, i: (h, 0, i, 0))
    cur = pl.BlockSpec((ATT_HPS, WINDOW, HEAD), lambda h, i: (h, i, 0))
    prev = pl.BlockSpec((ATT_HPS, WINDOW, HEAD), lambda h, i: (h, jnp.maximum(i - 1, 0), 0))
    tok = pl.BlockSpec((WINDOW, ATT_HPS * ATT_G * HEAD), lambda h, i: (i, h))
    return qspec, cur, prev, tok


def _attn_fwd(qh, kh, vh, sinks, *, name="attn_fwd"):
    S = kh.shape[1]
    nb = S // WINDOW

    def body(s_ref, band_ref, q_ref, kc_ref, kp_ref, vc_ref, vp_ref, o_ref):
        h2, i = pl.program_id(0), pl.program_id(1)
        outs = []
        for hh in range(ATT_HPS):
            q = q_ref[hh].reshape(ATT_G * WINDOW, HEAD)
            kb = jnp.concatenate([kp_ref[hh], kc_ref[hh]], axis=0)
            vb = jnp.concatenate([vp_ref[hh], vc_ref[hh]], axis=0)
            probs, _ = _attn_probs(q, kb, s_ref, band_ref, ATT_HPS * h2 + hh, i)
            o = _dot(probs.astype(BF16), vb).astype(BF16)
            outs += [o[WINDOW * g:WINDOW * (g + 1)] for g in range(ATT_G)]
        o_ref[...] = jnp.concatenate(outs, axis=1)

    qspec, cur, prev, tok = _attn_specs(S)
    return pl.pallas_call(
        body, name=name, grid=(ATT_KV // ATT_HPS, nb),
        in_specs=[pl.BlockSpec(memory_space=pltpu.SMEM), _BAND, qspec, cur, prev, cur, prev], out_specs=tok,
        out_shape=jax.ShapeDtypeStruct((S, ATT_KV * ATT_G * HEAD), BF16), compiler_params=_cp(2),
    )(sinks, _attn_band(), qh, kh, kh, vh, vh)


def _attn_bwd(qh, kh, vh, sinks, doh, *, name="attn_bwd"):
    S = kh.shape[1]
    nb = S // WINDOW

    def body(s_ref, band_ref, q_ref, kc_ref, kp_ref, vc_ref, vp_ref, do_ref, dq_ref, dk_ref, dv_ref, dsk_ref):
        h2, i = pl.program_id(0), pl.program_id(1)

        @pl.when(i == 0)
        def _():
            dk_ref[...] = jnp.zeros_like(dk_ref)
            dv_ref[...] = jnp.zeros_like(dv_ref)
            dsk_ref[...] = jnp.zeros_like(dsk_ref)

        dov = do_ref[...]
        cur = pl.multiple_of(i * WINDOW, WINDOW)
        lane = _iota((8, 128), 1)
        row = _iota((8, 128), 0)
        scale = HEAD ** -0.5
        for hh in range(ATT_HPS):
            q = q_ref[hh].reshape(ATT_G * WINDOW, HEAD)
            do = jnp.concatenate([dov[:, HEAD * (ATT_G * hh + g):HEAD * (ATT_G * hh + g + 1)] for g in range(ATT_G)], axis=0)
            kb = jnp.concatenate([kp_ref[hh], kc_ref[hh]], axis=0)
            vb = jnp.concatenate([vp_ref[hh], vc_ref[hh]], axis=0)
            probs, psink = _attn_probs(q, kb, s_ref, band_ref, ATT_HPS * h2 + hh, i)
            dp = _dot(do, vb, 1, 1)
            delta = jnp.sum(probs * dp, axis=1, keepdims=True)
            ds = (probs * (dp - delta)).astype(BF16)
            dq_ref[hh] = (_dot(ds, kb) * scale).reshape(ATT_G, WINDOW, HEAD)
            dkb = _dot(ds, q, 0, 0) * scale
            dvb = _dot(probs.astype(BF16), do, 0, 0)
            dk_ref[hh, pl.ds(cur, WINDOW), :] += dkb[WINDOW:, :]
            dv_ref[hh, pl.ds(cur, WINDOW), :] += dvb[WINDOW:, :]
            prv = pl.multiple_of(jnp.maximum(i - 1, 0) * WINDOW, WINDOW)
            dk_ref[hh, pl.ds(prv, WINDOW), :] += dkb[:WINDOW, :]
            dv_ref[hh, pl.ds(prv, WINDOW), :] += dvb[:WINDOW, :]

            dsr = -psink * delta
            upd = jnp.zeros((8, 128), F32)
            for gq in range(ATT_G):
                v = jnp.sum(dsr[gq * WINDOW:(gq + 1) * WINDOW, :], axis=0, keepdims=True)
                upd = jnp.where((lane == gq) & (row == 0), v, upd)
            dsk_ref[hh] += upd

    qspec, cur, prev, tok = _attn_specs(S)
    full = pl.BlockSpec((ATT_HPS, S, HEAD), lambda h, i: (h, 0, 0))
    return pl.pallas_call(
        body, name=name, grid=(ATT_KV // ATT_HPS, nb),
        in_specs=[pl.BlockSpec(memory_space=pltpu.SMEM), _BAND, qspec, cur, prev, cur, prev, tok],
        out_specs=[qspec, full, full, pl.BlockSpec((ATT_HPS, 8, 128), lambda h, i: (h, 0, 0))],
        out_shape=[jax.ShapeDtypeStruct((ATT_KV, ATT_G, S, HEAD), F32), jax.ShapeDtypeStruct((ATT_KV, S, HEAD), F32),
                   jax.ShapeDtypeStruct((ATT_KV, S, HEAD), F32), jax.ShapeDtypeStruct((ATT_KV, 8, 128), F32)],
        compiler_params=_cp(2),
    )(sinks, _attn_band(), qh, kh, kh, vh, vh, doh)


def _heads_major(t, nh):
    S = t.shape[0]
    return t.reshape(S, nh, HEAD).transpose(1, 0, 2)


def _tokens_major(t):
    nh, S, _ = t.shape
    return t.transpose(1, 0, 2).reshape(S, nh * HEAD)


class _NoComm:
    def late_weights(self, w, after):
        return w

    def advance(self, after, group=None, tensors=None):
        return None


def _local_step(x, posf, target, w, comm=None):
    S, D = x.shape
    gr = {}
    comm = comm or _NoComm()

    zx, h1 = _norm_mm(x, w["a_norm"], w["w_zx"], name="in_proj_zx", dep=w.get("dep"))
    dtr = _mm(h1, w["w_dt"], name="in_proj_dt")
    xbc = _conv_silu_fwd(zx, 2048, 4096, w["a_conv_w"], w["a_conv_b"], name="a_conv_f")
    dt_g, ac_g, sg_g, ac_t = _ssd_prep(dtr, w["a_dt_bias"], w["a_A_log"])
    y_ssd, states = _ssd_fwd(xbc, dt_g, ac_g, ac_t)
    yg = _gate_fwd(y_ssd, xbc, zx, w["a_Dexp"], w["a_gnorm"])
    x1 = _mm(yg, w["a_out_proj"], res=x, name="out_proj")

    w = comm.late_weights(w, x1)
    FW = w["f_w_in"][0].shape[2]

    def ffn_fwd(xin, l, loss_target=None):
        u, h = _norm_mm(xin, w["f_norm"][l], w["f_w_in"][l], name=f"f_in{l}", N=N_CHIPS * FW, tn=FW,
                        b_spec=pl.BlockSpec((None, D, FW), lambda i, j: (j, 0, 0)))
        a = _ffn_act_fwd(u, w["f_conv_w"][l], w["f_conv_b"][l], name=f"f_act_f{l}")
        xo = _mm(a, w["f_w_down"][l], res=xin, tk=a.shape[1], name=f"f_down{l}", target=loss_target)
        return xo, (h, u)

    x2, ffn0 = ffn_fwd(x1, 0)

    kv, hk = _norm_mm(x2, w["kv_norm"], w["w_kv"], bias=w["b_kv"], name="kv_proj")
    q, hq = _norm_mm(x2, w["b_norm"], w["w_q"], bias=w["b_q"], name="q_proj")
    rope = _rope_cs(posf)
    kr = _hnrope_fwd(kv, 0, 256, w["k_norm_w"], rope, name="k_rope_f")
    qr = _hnrope_fwd(q, 0, 1024, w["q_norm_w"], rope, name="q_rope_f")
    qh = qr.reshape(ATT_KV, ATT_G, S, HEAD)
    kh = kr
    vh = _heads_major(kv[:, 256:].astype(BF16), ATT_KV)
    att = _attn_fwd(qh, kh, vh, w["sinks"])
    x3 = _mm(att, w["w_o"], bias=w["b_o"], res=x2, name="o_proj")
    (dy, loss_part), ffn1 = ffn_fwd(x3, 1, target)

    def ffn_bwd(xin, l, saved, dyo, want_colsum, dep=None):
        h, u = saved
        da = _mm(dyo, w["f_w_down"][l], tb=True, name=f"f_down_dx{l}", dep=dep)
        du, dcw, dcb, a = _ffn_act_bwd(u, w["f_conv_w"][l], w["f_conv_b"][l], da, name=f"f_act_b{l}")
        dw_down = _mm(a, dyo, ta=True, out_dtype=BF16, name=f"f_down_dw{l}")
        dw_in = _mm(h, du, ta=True, out_dtype=BF16, name=f"f_in_dw{l}", dims=(D, N_CHIPS * FW, S), tm=D, tn=FW, tk=S,
                    b_spec=pl.BlockSpec((None, S, FW), lambda i, j, k: (j // 2, 0, j % 2)),
                    o_spec=pl.BlockSpec((None, D, FW), lambda i, j, k: (j, i, 0)), o_shape=(N_CHIPS, D, FW))
        ts = _pick(S, (1024, 512, 256))
        pieces = [(pl.BlockSpec((None, ts, FW), lambda i, j, k, q=q: (q // 2, i, q % 2)),
                   pl.BlockSpec((None, 512, FW), lambda i, j, k, q=q: (q, j, 0))) for q in range(N_CHIPS)]
        dh = _mm(du, w["f_w_in"][l], tb=True, name=f"f_in_dx{l}", dims=(S, D, FW), tm=ts, tn=512, tk=FW,
                 a_spec=pieces[0][0], b_spec=pieces[0][1], more=pieces[1:])
        outs = _rms_bwd(xin, [w["f_norm"][l]], [dh], dyo, name=f"f_norm_b{l}", want_colsum=want_colsum)
        g = dict(f_norm=outs[1], f_w_in=dw_in, f_conv_w=dcw, f_conv_b=dcb, f_w_down=dw_down)
        return outs[0], g, (outs[2] if want_colsum else None)

    dx3, gr["ffn1"], db_o = ffn_bwd(x3, 1, ffn1, dy, True)
    gr["b_o"] = db_o
    gr["w_o"] = _mm(att, dx3, ta=True, out_dtype=BF16, name="o_proj_dw")
    datt = _mm(dx3, w["w_o"], tb=True, out_dtype=BF16, name="o_proj_dx")
    dqh, dkh, dvh, dsk = _attn_bwd(qh, kh, vh, w["sinks"], datt)
    gr["sinks"] = dsk[:, 0, :4].reshape(1, 16)
    dv = _tokens_major(dvh).astype(BF16)
    dq, db_q, dqn = _hnrope_bwd(q, 0, 1024, w["q_norm_w"], rope, dqh.reshape(16, S, HEAD), name="q_rope_b")
    dk, db_k, dkn = _hnrope_bwd(kv, 0, 256, w["k_norm_w"], rope, dkh, name="k_rope_b")
    gr["q_norm"], gr["k_norm"] = dqn[:, :HEAD], dkn[:, :HEAD]
    gr["b_q"] = db_q
    gr["b_kv"] = jnp.concatenate([db_k, _colsum(dv, name="dv_colsum")], axis=1)
    dkv = jnp.concatenate([dk, dv], axis=1)
    gr["w_q"] = _mm(hq, dq, ta=True, out_dtype=BF16, name="q_proj_dw")
    gr["w_kv"] = _mm(hk, dkv, ta=True, out_dtype=BF16, name="kv_proj_dw")
    tok = comm.advance([gr["w_kv"]], 1, dict(f_down1=gr["ffn1"]["f_w_down"], f_in1=gr["ffn1"]["f_w_in"], w_o=gr["w_o"],
                                             w_q=gr["w_q"], w_kv=gr["w_kv"]))
    dhq = _mm(dq, w["w_q"], tb=True, name="q_proj_dx", dep=tok)
    dhk = _mm(dkv, w["w_kv"], tb=True, name="kv_proj_dx")
    dx2, gr["kv_norm"], gr["b_norm"] = _rms_bwd(x2, [w["kv_norm"], w["b_norm"]], [dhk, dhq], dx3, name="kvq_norm_b")

    dx1, gr["ffn0"], _ = ffn_bwd(x1, 0, ffn0, dx2, False, dep=comm.advance([dx2]))

    gr["a_out_proj"] = _mm(yg, dx1, ta=True, out_dtype=BF16, name="out_proj_dw")
    tok = comm.advance([dx1, gr["a_out_proj"]], 2,
                       dict(f_down0=gr["ffn0"]["f_w_down"], f_in0=gr["ffn0"]["f_w_in"], out_proj=gr["a_out_proj"]))
    dyg = _mm(dx1, w["a_out_proj"], tb=True, name="out_proj_dx", dep=tok)
    dy_ssd, dz, gr["a_gnorm"], dD = _gate_bwd(y_ssd, xbc, zx, w["a_Dexp"], w["a_gnorm"], dyg)
    gr["a_D"] = dD[:, :SSM_HEADS]
    dxs, dB, dC, dhead = _ssd_bwd(xbc, dt_g, ac_g, ac_t, states, dy_ssd, w["a_Dexp"], dep=comm.advance([dy_ssd]))
    ddtr, dsmall = _ssd_post(dhead, dt_g, sg_g, w["a_A_log_g"])
    gr["a_A_log"] = dsmall[:, 0, :4].reshape(1, SSM_HEADS)
    gr["a_dt_bias"] = dsmall[:, 1, :4].reshape(1, SSM_HEADS)
    dxbc, gr["a_conv_w"], gr["a_conv_b"] = _conv_silu_bwd(
        zx, 2048, 4096, w["a_conv_w"], w["a_conv_b"], [(dxs, 0), (dB, 2048), (dC, 3072)], name="a_conv_b")
    gr["w_z"] = _mm(h1, dz, ta=True, out_dtype=BF16, name="in_proj_dwz")
    gr["w_x"] = _mm(h1, dxbc, ta=True, out_dtype=BF16, name="in_proj_dwx")
    gr["w_dt"] = _mm(h1, ddtr, ta=True, out_dtype=BF16, name="in_proj_dwdt")
    ts = _pick(S, (1024, 512, 256))
    wblk = lambda q: pl.BlockSpec((512, 2048), lambda i, j, k: (j, q))
    dh1 = _mm(dz, w["w_zx"], tb=True, name="in_proj_dx", dims=(S, D, 2048), tm=ts, tn=512, tk=2048,
              a_spec=pl.BlockSpec((ts, 2048), lambda i, j, k: (i, 0)), b_spec=wblk(0),
              more=[(dxbc, pl.BlockSpec((ts, 2048), lambda i, j, k: (i, 0)), w["w_zx"], wblk(1)),
                    (dxbc, pl.BlockSpec((ts, 2048), lambda i, j, k: (i, 1)), w["w_zx"], wblk(2)),
                    (ddtr, pl.BlockSpec((ts, 128), lambda i, j, k: (i, 0)), w["w_dt"], pl.BlockSpec((512, 128), lambda i, j, k: (j, 0)))])
    dx0, gr["a_norm"] = _rms_bwd(x, [w["a_norm"]], [dh1], dx1, name="a_norm_b")
    tok = comm.advance([dx0], 3, dict(in_proj=_in_proj_grad(gr).reshape(D, N_CHIPS, -1).transpose(1, 0, 2)))
    return loss_part, dx0, gr, tok


def _prep_small(full, w):
    w["a_norm"] = full["a_norm"]
    w["a_conv_w"] = full["a_conv_w"][0]
    w["a_conv_b"] = full["a_conv_b"]
    pad32 = lambda v: jnp.pad(v, ((0, 0), (0, 128 - SSM_HEADS)))
    w["a_dt_bias"] = pad32(full["a_dt_bias"])
    w["a_A_log"] = pad32(full["a_A_log"])
    w["a_A_log_g"] = jnp.pad(full["a_A_log"].reshape(SSM_GROUPS, 1, 4), ((0, 0), (0, 0), (0, 124)))
    w["a_Dexp"] = jnp.repeat(full["a_D"], HEAD, axis=1)
    w["a_gnorm"] = full["a_gnorm"]
    w["f_norm"] = [full["f_norm"][l:l + 1] for l in range(2)]
    w["f_conv_w"] = [full["f_conv_w"][l] for l in range(2)]
    w["f_conv_b"] = [full["f_conv_b"][l:l + 1] for l in range(2)]
    w["kv_norm"] = full["kv_norm"].reshape(1, -1)
    w["b_kv"] = full["b_kv"].reshape(1, -1)
    w["k_norm_w"] = jnp.tile(full["k_norm"].reshape(1, HEAD), (1, ATT_KV))
    w["b_norm"] = full["b_norm"]
    w["b_q"] = full["b_q"]
    w["q_norm_w"] = jnp.tile(full["q_norm"], (1, ATT_KV * ATT_G))
    w["sinks"] = full["sinks"].reshape(-1)
    w["b_o"] = full["b_o"]
    return w


def _split_in_proj(ip):
    return ip[:, :6144].astype(BF16), jnp.pad(ip[:, 6144:], ((0, 0), (0, 128 - SSM_HEADS))).astype(BF16)


def _join_in_proj(blocks, *, name="in_proj_join", tr=256):
    _, R, cw = blocks.shape
    zx_cols = 3 * 2048
    rest = N_CHIPS * cw - zx_cols

    def body(b_ref, zx_ref, dt_ref):
        whole = jnp.concatenate([b_ref[j] for j in range(N_CHIPS)], axis=1)
        zx_ref[...] = whole[:, :zx_cols]
        dt_ref[...] = jnp.concatenate([whole[:, zx_cols:], jnp.zeros((tr, 128 - rest), BF16)], axis=1)

    return pl.pallas_call(
        body, name=name, grid=(R // tr,), in_specs=[pl.BlockSpec((N_CHIPS, tr, cw), lambda i: (0, i, 0))],
        out_specs=[pl.BlockSpec((tr, zx_cols), lambda i: (i, 0)), pl.BlockSpec((tr, 128), lambda i: (i, 0))],
        out_shape=[jax.ShapeDtypeStruct((R, zx_cols), BF16), jax.ShapeDtypeStruct((R, 128), BF16)],
        compiler_params=_cp(1),
    )(blocks)


def _prep_weights(full):
    w = _prep_small(full, {})
    w["w_zx"], w["w_dt"] = _split_in_proj(full["a_in_proj"][0])
    w["a_out_proj"] = full["a_out_proj"][0].astype(BF16)
    w["f_w_in"] = [full["f_w_in"][l].reshape(1024, N_CHIPS, -1).transpose(1, 0, 2).astype(BF16) for l in range(2)]
    w["f_w_down"] = [full["f_w_down"][l].astype(BF16) for l in range(2)]
    w["w_kv"] = full["w_kv"].astype(BF16)
    w["w_q"] = full["w_q"][0].astype(BF16)
    w["w_o"] = full["w_o"][0].astype(BF16)
    return w


def _small_grads(gr):
    g = {}
    g["a_norm"] = gr["a_norm"]
    g["a_conv_w"] = gr["a_conv_w"][None]
    g["a_conv_b"] = gr["a_conv_b"]
    g["a_dt_bias"], g["a_A_log"], g["a_D"] = gr["a_dt_bias"], gr["a_A_log"], gr["a_D"]
    g["a_gnorm"] = gr["a_gnorm"]
    g["kv_norm"] = gr["kv_norm"].reshape(-1)
    g["b_kv"] = gr["b_kv"].reshape(-1)
    g["k_norm"] = gr["k_norm"].reshape(-1)
    g["b_norm"] = gr["b_norm"]
    g["b_q"] = gr["b_q"]
    g["q_norm"] = gr["q_norm"]
    g["sinks"] = gr["sinks"]
    g["b_o"] = gr["b_o"]
    f = [gr["ffn0"], gr["ffn1"]]
    g["f_norm"] = jnp.concatenate([f[0]["f_norm"], f[1]["f_norm"]], axis=0)
    g["f_conv_w"] = jnp.stack([f[l]["f_conv_w"] for l in range(2)])
    g["f_conv_b"] = jnp.concatenate([f[l]["f_conv_b"] for l in range(2)], axis=0)
    return g


def _in_proj_grad(gr):
    return jnp.concatenate([gr["w_z"], gr["w_x"], gr["w_dt"][:, :SSM_HEADS]], axis=1)


def _full_grads(gr):
    g = _small_grads(gr)
    f32 = lambda t: t.astype(F32)
    g["a_in_proj"] = f32(_in_proj_grad(gr))[None]
    g["a_out_proj"] = f32(gr["a_out_proj"])[None]
    g["w_kv"] = f32(gr["w_kv"])
    g["w_q"] = f32(gr["w_q"])[None]
    g["w_o"] = f32(gr["w_o"])[None]
    f = [gr["ffn0"], gr["ffn1"]]
    g["f_w_in"] = jnp.stack([f32(f[l]["f_w_in"]).transpose(1, 0, 2).reshape(1024, -1) for l in range(2)])
    g["f_w_down"] = jnp.stack([f32(f[l]["f_w_down"]) for l in range(2)])
    return g


MESH = pl.DeviceIdType.MESH
WEIGHTS = ("a_norm", "a_in_proj", "a_conv_w", "a_conv_b", "a_dt_bias", "a_A_log", "a_D", "a_gnorm", "a_out_proj",
           "kv_norm", "w_kv", "b_kv", "k_norm", "b_norm", "w_q", "b_q", "q_norm", "sinks", "w_o", "b_o", "f_norm",
           "f_w_in", "f_conv_w", "f_conv_b", "f_w_down")
MATS = (("in_proj", "a_in_proj", 0), ("out_proj", "a_out_proj", 0), ("w_kv", "w_kv", None), ("w_q", "w_q", 0),
        ("w_o", "w_o", 0), ("f_in0", "f_w_in", 0), ("f_in1", "f_w_in", 1), ("f_down0", "f_w_down", 0),
        ("f_down1", "f_w_down", 1))
SMALL_CUT = (("a_norm", 1), ("a_conv_w", 2), ("a_conv_b", 1), ("a_gnorm", 1), ("f_conv_w", 2))
SMALL_REP = ("a_dt_bias", "a_A_log", "a_D", "kv_norm", "b_kv", "k_norm", "b_norm", "b_q", "q_norm", "sinks", "b_o",
             "f_norm", "f_conv_b")


def _coords():
    return lax.axis_index("x"), lax.axis_index("y"), lax.axis_index("c")


def _other_chips(x, y):
    return [(1 - x, y), (x, 1 - y), (1 - x, 1 - y)]


def _pack(arrs, rows_align, lanes, dtype):
    flat = jnp.concatenate([a.reshape(-1).astype(dtype) for a in arrs])
    per = rows_align * lanes
    total = -(-flat.shape[0] // per) * per
    return jnp.pad(flat, (0, total - flat.shape[0])).reshape(total // lanes, lanes)


def _unpack(flat, shapes):
    out, off = [], 0
    for s in shapes:
        n = math.prod(s)
        out.append(flat[off:off + n].reshape(s))
        off += n
    return out


def _remote(src, dst, send, recv, k, dev):
    return pltpu.make_async_remote_copy(src_ref=src, dst_ref=dst, send_sem=send.at[k], recv_sem=recv.at[k],
                                        device_id=dev, device_id_type=MESH)


_ANY = pl.BlockSpec(memory_space=pl.ANY)


def _halves(t):
    r, c = t.shape
    return t.reshape(2, r // 2, c)


def _gather_weights(shards, sp):
    n = len(shards)
    per = 9
    n_sem = per * n + 3

    def body(*refs):
        sh, sp_ref = refs[:n], refs[n]
        outs, sout = refs[n + 1:2 * n + 1], refs[2 * n + 1]
        send, recv, loc = refs[2 * n + 2:]
        x, y, c = _coords()
        me = 2 * x + y
        cx_, cy_, cd_ = _other_chips(x, y)
        ix, iy, idg = (2 * p[0] + p[1] for p in (cx_, cy_, cd_))
        to_x, to_y, sib = (*cx_, c), (*cy_, c), (x, y, 1 - c)
        l1 = pltpu.make_async_copy(sp_ref, sout.at[me], loc.at[0])
        l1.start()
        sends = [_remote(sp_ref, sout.at[me], send, recv, per * n + j, (*p, c)) for j, p in enumerate((cx_, cy_, cd_))]
        for t in range(n):
            sends.append(_remote(sh[t].at[c], outs[t].at[me, c], send, recv, per * t + 0, to_x))
            sends.append(_remote(sh[t].at[c], outs[t].at[me, c], send, recv, per * t + 1, to_y))
            sends.append(_remote(sh[t], outs[t].at[me], send, recv, per * t + 8, sib))
        for cp in sends:
            cp.start()

        def go(src, dst, k, dev):
            cp = _remote(src, dst, send, recv, k, dev)
            cp.start()
            sends.append(cp)

        def piece(t, owner, first):
            q = sh[t].shape[1] // 2
            return outs[t].at[owner, c, pl.ds(0 if first else q, q)]

        for t in range(n):
            _remote(sh[t].at[c], outs[t].at[ix, c], send, recv, per * t + 0, to_x).wait_recv()
            go(piece(t, ix, False), piece(t, ix, False), per * t + 3, to_y)
            go(outs[t].at[ix, c], outs[t].at[ix, c], per * t + 4, sib)
        for t in range(n):
            _remote(sh[t].at[c], outs[t].at[iy, c], send, recv, per * t + 1, to_y).wait_recv()
            go(piece(t, iy, True), piece(t, iy, True), per * t + 2, to_x)
            go(outs[t].at[iy, c], outs[t].at[iy, c], per * t + 5, sib)
        for t in range(n):
            _remote(piece(t, idg, True), piece(t, idg, True), send, recv, per * t + 2, to_x).wait_recv()
            go(piece(t, idg, True), piece(t, idg, True), per * t + 6, sib)
            _remote(piece(t, idg, False), piece(t, idg, False), send, recv, per * t + 3, to_y).wait_recv()
            go(piece(t, idg, False), piece(t, idg, False), per * t + 7, sib)
        for j, p in enumerate((cx_, cy_, cd_)):
            _remote(sp_ref, sout.at[2 * p[0] + p[1]], send, recv, per * n + j, (*p, c)).wait_recv()
        for t in range(n):
            q = sh[t].shape[1] // 2
            other = lambda owner, lo=None: outs[t].at[owner, 1 - c] if lo is None else outs[t].at[owner, 1 - c, pl.ds(lo, q)]
            _remote(other(ix), other(ix), send, recv, per * t + 4, sib).wait_recv()
            _remote(other(iy), other(iy), send, recv, per * t + 5, sib).wait_recv()
            _remote(other(idg, 0), other(idg, 0), send, recv, per * t + 6, sib).wait_recv()
            _remote(other(idg, q), other(idg, q), send, recv, per * t + 7, sib).wait_recv()
            _remote(sh[t], outs[t].at[me], send, recv, per * t + 8, sib).wait_recv()
        for cp in sends:
            cp.wait_send()
        l1.wait()

    res = pl.pallas_call(
        body, name="gather_weights", in_specs=[_ANY] * (n + 1), out_specs=[_ANY] * (n + 1),
        out_shape=[jax.ShapeDtypeStruct((N_CHIPS,) + t.shape, t.dtype) for t in shards]
        + [jax.ShapeDtypeStruct((N_CHIPS,) + sp.shape, sp.dtype)],
        scratch_shapes=[pltpu.SemaphoreType.DMA((n_sem,)), pltpu.SemaphoreType.DMA((n_sem,)),
                        pltpu.SemaphoreType.DMA((1,))],
    )(*shards, sp)
    return res[:n], res[n]


_HBM = pl.BlockSpec(memory_space=pltpu.HBM)
_SEMS = pl.BlockSpec(memory_space=pltpu.SEMAPHORE)
_DATAFLOW = pltpu.SideEffectType.DATAFLOW_SIDE_EFFECTING


def _in_hbm(a):
    return pltpu.with_memory_space_constraint(a, pltpu.HBM)


def _start_copies(copies, arrays, n_sem, after, *, name):
    n, na = len(arrays), len(after)

    def body(*refs):
        for mine, _ in copies(refs[:n], refs[n + na], refs[n + na + 1]):
            mine.start()
        refs[-1][...] = jnp.zeros_like(refs[-1])

    res = pl.pallas_call(
        body, name=name, in_specs=[_HBM] * n + [_ANY] * na,
        out_specs=[_SEMS, _SEMS] + [_HBM] * n + [pl.BlockSpec(memory_space=pltpu.VMEM)],
        out_shape=[pltpu.SemaphoreType.DMA((n_sem,)), pltpu.SemaphoreType.DMA((n_sem,))]
        + [pltpu.HBM(a.shape, a.dtype) for a in arrays] + [jax.ShapeDtypeStruct((8, 128), F32)],
        input_output_aliases={t: 2 + t for t in range(n)},
        compiler_params=pltpu.CompilerParams(has_side_effects=_DATAFLOW),
    )(*[_in_hbm(a) for a in arrays], *after)
    return res[0], res[1], list(res[2:2 + n]), res[-1]


def _wait_copies(copies, send, recv, arrays, after, *, name):
    n = len(arrays)

    def body(*refs):
        for mine, theirs in copies(refs[:n], refs[n], refs[n + 1]):
            mine.wait_send()
            theirs.wait_recv()

    return list(pl.pallas_call(
        body, name=name, in_specs=[_HBM] * n + [_SEMS, _SEMS] + [_ANY] * len(after), out_specs=[_HBM] * n,
        out_shape=[pltpu.HBM(a.shape, a.dtype) for a in arrays], input_output_aliases={t: t for t in range(n)},
        compiler_params=pltpu.CompilerParams(has_side_effects=_DATAFLOW),
    )(*arrays, send, recv, *after))


def _sibling_copies(refs, send, recv):
    n = len(refs) // 2
    x, y, c = _coords()
    cps = [_remote(refs[t].at[:, 1 - c], refs[n + t], send, recv, t, (x, y, 1 - c)) for t in range(n)]
    return [(cp, cp) for cp in cps]


def _join_copies(refs, send, recv):
    x, y, c = _coords()
    sib = (x, y, 1 - c)
    return [(_remote(o.at[c], o.at[c], send, recv, t, sib), _remote(o.at[1 - c], o.at[1 - c], send, recv, t, sib))
            for t, o in enumerate(refs)]


def _gather_copies(sh, land, send, recv):
    x, y, c = _coords()
    me = 2 * x + y
    out = []
    for t in range(len(sh)):
        for j, (cx, cy) in enumerate(_other_chips(x, y)):
            dev = (cx, cy, c)
            out.append((_remote(sh[t].at[c], land[t].at[me, c], send, recv, 4 * t + j, dev),
                        _remote(sh[t].at[c], land[t].at[2 * cx + cy, c], send, recv, 4 * t + j, dev)))
        sib = (x, y, 1 - c)
        out.append((_remote(sh[t], land[t].at[me], send, recv, 4 * t + 3, sib),
                    _remote(sh[t], land[t].at[me], send, recv, 4 * t + 3, sib)))
    return out


def _gather_start(shards, after, *, name):
    n = len(shards)

    def body(*refs):
        sh, land = refs[:n], refs[n:2 * n]
        send, recv = refs[2 * n + 1], refs[2 * n + 2]
        token = refs[-1]
        for mine, _ in _gather_copies(sh, land, send, recv):
            mine.start()
        token[...] = jnp.zeros_like(token)

    lands = [_in_hbm(lax.empty((N_CHIPS,) + s.shape, s.dtype)) for s in shards]
    res = pl.pallas_call(
        body, name=name, in_specs=[_HBM] * (2 * n) + [_ANY],
        out_specs=[_SEMS, _SEMS] + [_HBM] * (2 * n) + [pl.BlockSpec(memory_space=pltpu.VMEM)],
        out_shape=[pltpu.SemaphoreType.DMA((4 * n,)), pltpu.SemaphoreType.DMA((4 * n,))]
        + [pltpu.HBM(s.shape, s.dtype) for s in shards] + [pltpu.HBM(l.shape, l.dtype) for l in lands]
        + [jax.ShapeDtypeStruct((8, 128), F32)],
        input_output_aliases={t: 2 + t for t in range(2 * n)},
        compiler_params=pltpu.CompilerParams(has_side_effects=_DATAFLOW),
    )(*[_in_hbm(s) for s in shards], *lands, after)
    return res[0], res[1], res[2:2 + n], res[2 + n:2 + 2 * n], res[-1]


def _gather_wait(send, recv, shards, lands, after, *, name):
    n = len(shards)

    def body(*refs):
        sh, land = refs[:n], refs[n:2 * n]
        send_r, recv_r = refs[2 * n], refs[2 * n + 1]
        for mine, theirs in _gather_copies(sh, land, send_r, recv_r):
            mine.wait_send()
            theirs.wait_recv()

    res = pl.pallas_call(
        body, name=name, in_specs=[_HBM] * (2 * n) + [_SEMS, _SEMS, _ANY], out_specs=[_HBM] * (2 * n),
        out_shape=[pltpu.HBM(s.shape, s.dtype) for s in shards] + [pltpu.HBM(l.shape, l.dtype) for l in lands],
        input_output_aliases={t: t for t in range(2 * n)},
        compiler_params=pltpu.CompilerParams(has_side_effects=_DATAFLOW),
    )(*shards, *lands, send, recv, after)
    return res[n:]


def _gather_forward(lands, *, name):
    n = len(lands)

    def body(*refs):
        o = refs[n:2 * n]
        send, recv = refs[2 * n:]
        x, y, c = _coords()
        sib = (x, y, 1 - c)
        srcs = [2 * cx + cy for cx, cy in _other_chips(x, y)]
        cps = [_remote(o[t].at[s, c], o[t].at[s, c], send, recv, 3 * t + j, sib) for t in range(n) for j, s in enumerate(srcs)]
        for cp in cps:
            cp.start()
        for t in range(n):
            for j, s in enumerate(srcs):
                _remote(o[t].at[s, 1 - c], o[t].at[s, 1 - c], send, recv, 3 * t + j, sib).wait_recv()
        for cp in cps:
            cp.wait_send()

    return pl.pallas_call(
        body, name=name, in_specs=[_ANY] * n, out_specs=[_ANY] * n, input_output_aliases={t: t for t in range(n)},
        out_shape=[jax.ShapeDtypeStruct(l.shape, l.dtype) for l in lands],
        scratch_shapes=[pltpu.SemaphoreType.DMA((3 * n,)), pltpu.SemaphoreType.DMA((3 * n,))],
    )(*lands)


def _small_copies(v, land, send, recv):
    x, y, c = _coords()
    me = 4 * x + 2 * y + c
    out = []
    for k in range(1, 8):
        px = 1 - x if k & 4 else x
        py = 1 - y if k & 2 else y
        pc = 1 - c if k & 1 else c
        out.append((_remote(v, land.at[me], send, recv, k - 1, (px, py, pc)),
                    _remote(v, land.at[4 * px + 2 * py + pc], send, recv, k - 1, (px, py, pc))))
    return out


def _small_start(v, after, *, name):
    def body(v_ref, land_ref, after_ref, send, recv, v_thru, land_thru, token):
        for mine, _ in _small_copies(v_ref, land_ref, send, recv):
            mine.start()
        token[...] = jnp.zeros_like(token)

    land = _in_hbm(lax.empty((8,) + v.shape, v.dtype))
    return pl.pallas_call(
        body, name=name, in_specs=[_HBM, _HBM, _ANY],
        out_specs=[_SEMS, _SEMS, _HBM, _HBM, pl.BlockSpec(memory_space=pltpu.VMEM)],
        out_shape=[pltpu.SemaphoreType.DMA((7,)), pltpu.SemaphoreType.DMA((7,)), pltpu.HBM(v.shape, v.dtype),
                   pltpu.HBM(land.shape, land.dtype), jax.ShapeDtypeStruct((8, 128), F32)],
        input_output_aliases={0: 2, 1: 3}, compiler_params=pltpu.CompilerParams(has_side_effects=_DATAFLOW),
    )(_in_hbm(v), land, after)


def _small_wait(send, recv, v, land, after, *, name):
    def body(v_ref, land_ref, send_r, recv_r, *rest):
        for mine, theirs in _small_copies(v_ref, land_ref, send_r, recv_r):
            mine.wait_send()
            theirs.wait_recv()

    return pl.pallas_call(
        body, name=name, in_specs=[_HBM, _HBM, _SEMS, _SEMS] + [_ANY] * len(after), out_specs=[_HBM, _HBM],
        out_shape=[pltpu.HBM(v.shape, v.dtype), pltpu.HBM(land.shape, land.dtype)],
        input_output_aliases={0: 0, 1: 1}, compiler_params=pltpu.CompilerParams(has_side_effects=_DATAFLOW),
    )(v, land, send, recv, *after)


def _small_sum(v, land, me_idx, *, name="small_sum"):
    def body(me_ref, v_ref, land_ref, o_ref):
        acc = None
        for s in range(8):
            term = jnp.where(me_ref[0] == s, v_ref[...], land_ref[s])
            acc = term if acc is None else acc + term
        o_ref[...] = acc

    whole = lambda shape: pl.BlockSpec(shape, lambda i, me_ref: (0,) * len(shape))
    return pl.pallas_call(
        body, name=name,
        grid_spec=pltpu.PrefetchScalarGridSpec(num_scalar_prefetch=1, grid=(1,), in_specs=[whole(v.shape), whole(land.shape)],
                                               out_specs=whole(v.shape)),
        out_shape=jax.ShapeDtypeStruct(v.shape, F32), compiler_params=_cp(1),
    )(me_idx, v, land)


RS_ROW_SPLIT = 2


def _rs_add_pair(gs, as_, c_idx, *, name):
    n = len(gs)

    def body(c_ref, *refs):
        for t in range(n):
            refs[2 * n + t][...] = (refs[t][...].astype(F32) + refs[n + t][...].astype(F32)).astype(BF16)

    def gspec(g):
        _, _, rh, cols = g.shape
        return pl.BlockSpec((None, None, rh // RS_ROW_SPLIT, cols), lambda j, i, c_ref: (j, c_ref[0], i, 0))

    def pspec(g):
        _, _, rh, cols = g.shape
        return pl.BlockSpec((None, rh // RS_ROW_SPLIT, cols), lambda j, i, c_ref: (j, i, 0))

    return pl.pallas_call(
        body, name=name,
        grid_spec=pltpu.PrefetchScalarGridSpec(
            num_scalar_prefetch=1, grid=(N_CHIPS, RS_ROW_SPLIT),
            in_specs=[gspec(g) for g in gs] + [pspec(g) for g in gs], out_specs=[pspec(g) for g in gs]),
        out_shape=[jax.ShapeDtypeStruct((N_CHIPS,) + g.shape[2:], BF16) for g in gs], compiler_params=_cp(2),
    )(c_idx, *gs, *as_)


def _chips_copies(p, r, send, recv):
    x, y, c = _coords()
    return [_remote(p[t].at[2 * cx + cy], r[t].at[k], send, recv, 3 * t + k, (cx, cy, c))
            for k, (cx, cy) in enumerate(_other_chips(x, y)) for t in range(len(p))]


def _rs_chips_start(ps, after, *, name):
    n, na = len(ps), len(after)

    def body(*refs):
        p, r = refs[:n], refs[n:2 * n]
        send, recv = refs[2 * n + na], refs[2 * n + na + 1]
        token = refs[-1]
        for cp in _chips_copies(p, r, send, recv):
            cp.start()
        token[...] = jnp.zeros_like(token)

    lands = [_in_hbm(lax.empty((3,) + p.shape[1:], p.dtype)) for p in ps]
    res = pl.pallas_call(
        body, name=name, in_specs=[_HBM] * (2 * n) + [_ANY] * na,
        out_specs=[_SEMS, _SEMS] + [_HBM] * (2 * n) + [pl.BlockSpec(memory_space=pltpu.VMEM)],
        out_shape=[pltpu.SemaphoreType.DMA((3 * n,)), pltpu.SemaphoreType.DMA((3 * n,))]
        + [pltpu.HBM(p.shape, p.dtype) for p in ps] + [pltpu.HBM(l.shape, l.dtype) for l in lands]
        + [jax.ShapeDtypeStruct((8, 128), F32)],
        input_output_aliases={t: 2 + t for t in range(2 * n)},
        compiler_params=pltpu.CompilerParams(has_side_effects=_DATAFLOW),
    )(*[_in_hbm(p) for p in ps], *lands, *after)
    return res[0], res[1], res[2:2 + n], res[2 + n:2 + 2 * n], res[-1]


def _rs_chips_wait(send, recv, ps, lands, after, *, name):
    n = len(ps)

    def body(*refs):
        p, r = refs[:n], refs[n:2 * n]
        for cp in _chips_copies(p, r, refs[2 * n], refs[2 * n + 1]):
            cp.wait_send()
            cp.wait_recv()

    res = pl.pallas_call(
        body, name=name, in_specs=[_HBM] * (2 * n) + [_SEMS, _SEMS] + [_ANY] * len(after), out_specs=[_HBM] * (2 * n),
        out_shape=[pltpu.HBM(p.shape, p.dtype) for p in ps] + [pltpu.HBM(l.shape, l.dtype) for l in lands],
        input_output_aliases={t: t for t in range(2 * n)},
        compiler_params=pltpu.CompilerParams(has_side_effects=_DATAFLOW),
    )(*ps, *lands, send, recv, *after)
    return res[:n], res[n:]


def _rs_add_chips(ps, rs, idx, *, name):
    n = len(ps)

    def body(idx_ref, *refs):
        for t in range(n):
            p_ref, r0, r1, r2 = refs[4 * t:4 * t + 4]
            refs[4 * n + t][...] = ((p_ref[...].astype(F32) + r0[...].astype(F32)) + r1[...].astype(F32)) + r2[...].astype(F32)

    in_specs, args = [], []
    for p, r in zip(ps, rs):
        _, rh, cols = p.shape
        blk = (None, rh // RS_ROW_SPLIT, cols)
        in_specs.append(pl.BlockSpec(blk, lambda i, idx_ref: (idx_ref[0], i, 0)))
        in_specs += [pl.BlockSpec(blk, lambda i, idx_ref, k=k: (k, i, 0)) for k in range(3)]
        args += [p, r, r, r]
    out_specs = [pl.BlockSpec((None, p.shape[1] // RS_ROW_SPLIT, p.shape[2]), lambda i, idx_ref: (idx_ref[1], i, 0))
                 for p in ps]
    return pl.pallas_call(
        body, name=name,
        grid_spec=pltpu.PrefetchScalarGridSpec(num_scalar_prefetch=1, grid=(RS_ROW_SPLIT,), in_specs=in_specs,
                                               out_specs=out_specs),
        out_shape=[jax.ShapeDtypeStruct((2,) + p.shape[1:], F32) for p in ps], compiler_params=_cp(1),
    )(idx, *args)


def _adamw(w, gs, m, v, *, name, dep=None):
    L, Rr, C = w.shape
    tr, tc = _pick(Rr, (256, 128, 64)), C
    if tr == Rr and Rr * C > 512 * 1024:
        tc = 256
    bc1 = 1.0 - ADAM_B1 ** ADAM_STEP
    bc2 = 1.0 - ADAM_B2 ** ADAM_STEP
    nd = 0 if dep is None else 1

    def body(*refs):
        w_ref, m_ref, v_ref = refs[0], refs[1], refs[2]
        g_refs = refs[3:3 + L]
        d_ref, mo_ref, vo_ref, go_ref = refs[3 + L + nd:]
        layer = pl.program_id(0)
        gv = g_refs[0][...]
        for q in range(1, L):
            gv = jnp.where(layer == q, g_refs[q][...], gv)
        mn = ADAM_B1 * m_ref[...] + (1.0 - ADAM_B1) * gv
        vn = ADAM_B2 * v_ref[...] + (1.0 - ADAM_B2) * (gv * gv)
        go_ref[...] = gv
        mo_ref[...] = mn
        vo_ref[...] = vn
        d_ref[...] = -ADAM_LR * ((mn / bc1) / (jnp.sqrt(vn / bc2) + ADAM_EPS) + ADAM_WD * w_ref[...])

    blk = pl.BlockSpec((None, tr, tc), lambda l, i, j: (l, i, j))
    gblks = [pl.BlockSpec((tr, tc), lambda l, i, j, q=q: (jnp.where(l == q, i, 0), jnp.where(l == q, j, 0))) for q in range(L)]
    return pl.pallas_call(
        body, name=name, grid=(L, Rr // tr, C // tc), in_specs=[blk] * 3 + gblks + [_ANY] * nd, out_specs=[blk] * 4,
        out_shape=[jax.ShapeDtypeStruct((L, Rr, C), F32)] * 4, compiler_params=_cp(3),
    )(w, m, v, *gs, *([] if dep is None else [dep]))


def kernel(x, positions, a_norm, a_in_proj, a_conv_w, a_conv_b, a_dt_bias, a_A_log, a_D, a_gnorm, a_out_proj,
           kv_norm, w_kv, b_kv, k_norm, b_norm, w_q, b_q, q_norm, sinks, w_o, b_o, f_norm, f_w_in, f_conv_w,
           f_conv_b, f_w_down, loss_target, m_a_norm, m_a_in_proj, m_a_conv_w, m_a_conv_b, m_a_dt_bias, m_a_A_log,
           m_a_D, m_a_gnorm, m_a_out_proj, m_kv_norm, m_w_kv, m_b_kv, m_k_norm, m_b_norm, m_w_q, m_b_q, m_q_norm,
           m_sinks, m_w_o, m_b_o, m_f_norm, m_f_w_in, m_f_conv_w, m_f_conv_b, m_f_w_down, v_a_norm, v_a_in_proj,
           v_a_conv_w, v_a_conv_b, v_a_dt_bias, v_a_A_log, v_a_D, v_a_gnorm, v_a_out_proj, v_kv_norm, v_w_kv,
           v_b_kv, v_k_norm, v_b_norm, v_w_q, v_b_q, v_q_norm, v_sinks, v_w_o, v_b_o, v_f_norm, v_f_w_in,
           v_f_conv_w, v_f_conv_b, v_f_w_down):
    wl = dict(zip(WEIGHTS, (a_norm, a_in_proj, a_conv_w, a_conv_b, a_dt_bias, a_A_log, a_D, a_gnorm, a_out_proj,
                            kv_norm, w_kv, b_kv, k_norm, b_norm, w_q, b_q, q_norm, sinks, w_o, b_o, f_norm, f_w_in,
                            f_conv_w, f_conv_b, f_w_down)))
    ml = dict(zip(WEIGHTS, (m_a_norm, m_a_in_proj, m_a_conv_w, m_a_conv_b, m_a_dt_bias, m_a_A_log, m_a_D, m_a_gnorm,
                            m_a_out_proj, m_kv_norm, m_w_kv, m_b_kv, m_k_norm, m_b_norm, m_w_q, m_b_q, m_q_norm,
                            m_sinks, m_w_o, m_b_o, m_f_norm, m_f_w_in, m_f_conv_w, m_f_conv_b, m_f_w_down)))
    vl = dict(zip(WEIGHTS, (v_a_norm, v_a_in_proj, v_a_conv_w, v_a_conv_b, v_a_dt_bias, v_a_A_log, v_a_D, v_a_gnorm,
                            v_a_out_proj, v_kv_norm, v_w_kv, v_b_kv, v_k_norm, v_b_norm, v_w_q, v_b_q, v_q_norm,
                            v_sinks, v_w_o, v_b_o, v_f_norm, v_f_w_in, v_f_conv_w, v_f_conv_b, v_f_w_down)))
    xi, yi, ci = _coords()
    me = 2 * xi + yi
    S = x.shape[1]

    def block_of(n, layer):
        t = wl[n]
        return t if layer is None else t[layer]

    rows = lambda t: t.reshape(-1, t.shape[-1])
    c_idx = jnp.reshape(ci, (1,)).astype(jnp.int32)
    me_c = jnp.stack([me, ci]).astype(jnp.int32)
    early = ("in_proj", "out_proj")
    late = tuple(name for name, _, _ in MATS if name not in early)
    shards = {name: _halves(block_of(wn, layer).astype(BF16)) for name, wn, layer in MATS}

    sp = _pack([wl[n] for n, _ in SMALL_CUT], 8, 128, F32)
    gathered, gs = _gather_weights([shards[k] for k in early], sp)
    gt = {k: t.reshape(N_CHIPS, -1, t.shape[-1]) for k, t in zip(early, gathered)}
    started = _gather_start([shards[k] for k in late], gs, name="gather_late_start")
    full = {n: wl[n] for n in SMALL_REP}
    gs = gs.reshape(N_CHIPS, -1)
    pieces = [_unpack(gs[j], [wl[n].shape for n, _ in SMALL_CUT]) for j in range(N_CHIPS)]
    for q, (n, ax) in enumerate(SMALL_CUT):
        full[n] = jnp.concatenate([pieces[j][q] for j in range(N_CHIPS)], axis=ax)
    w = _prep_small(full, {})
    w["w_zx"], w["w_dt"] = _join_in_proj(gt["in_proj"])
    w["a_out_proj"] = rows(gt["out_proj"])
    w["dep"] = started[4]

    class Comm:
        flight = []
        reduced = {}

        def late_weights(self, w, after):
            lands = _gather_wait(started[0], started[1], started[2], started[3], after, name="gather_late_wait")
            lands = _gather_forward(lands, name="gather_late_forward")
            lt = {k: t.reshape(N_CHIPS, -1, t.shape[-1]) for k, t in zip(late, lands)}
            w = dict(w)
            w["w_kv"], w["w_q"], w["w_o"] = (rows(lt[k]) for k in ("w_kv", "w_q", "w_o"))
            w["f_w_in"] = [lt["f_in0"], lt["f_in1"]]
            w["f_w_down"] = [rows(lt["f_down0"]), rows(lt["f_down1"])]
            return w

        def advance(self, after, group=None, tensors=None):
            token = None
            for grp in list(self.flight):
                tag, n = grp["tag"], len(grp["names"])
                dep = list(after) + ([] if token is None else [token])
                if grp["stage"] == "sibling":
                    arrs = _wait_copies(_sibling_copies, grp["send"], grp["recv"], grp["arrays"], dep, name=f"rs_sibling_wait{tag}")
                    pairs = _rs_add_pair(arrs[:n], arrs[n:], c_idx, name=f"rs_add_pair{tag}")
                    send, recv, ps, lands, token = _rs_chips_start(pairs, dep, name=f"rs_chips_start{tag}")
                    grp.update(stage="chips", send=send, recv=recv, ps=ps, lands=lands)
                elif grp["stage"] == "chips":
                    ps, rs = _rs_chips_wait(grp["send"], grp["recv"], grp["ps"], grp["lands"], dep, name=f"rs_chips_wait{tag}")
                    halves = _rs_add_chips(ps, rs, me_c, name=f"rs_add_chips{tag}")
                    send, recv, arrs, token = _start_copies(_join_copies, halves, n, dep, name=f"rs_join_start{tag}")
                    grp.update(stage="join", send=send, recv=recv, arrays=arrs)
                else:
                    joined = _wait_copies(_join_copies, grp["send"], grp["recv"], grp["arrays"], dep, name=f"rs_join_wait{tag}")
                    self.reduced.update({k: rows(t) for k, t in zip(grp["names"], joined)})
                    self.flight.remove(grp)
            if group is not None:
                names = list(tensors)
                glist = [tensors[k].reshape(N_CHIPS, 2, -1, tensors[k].shape[-1]) for k in names]
                lands = [lax.empty((N_CHIPS,) + gq.shape[2:], gq.dtype) for gq in glist]
                dep = list(after) + ([] if token is None else [token])
                send, recv, arrs, token = _start_copies(_sibling_copies, glist + lands, len(names), dep,
                                                        name=f"rs_sibling_start{group}")
                self.flight.append(dict(tag=group, names=names, stage="sibling", send=send, recv=recv, arrays=arrs))
            return token

    comm = Comm()

    posf = positions.reshape(S, 1).astype(F32)
    loss_part, dx0, gr, tok = _local_step(x[0], posf, loss_target[0], w, comm)
    g = _small_grads(gr)

    small_names = [n for n, _ in SMALL_CUT] + list(SMALL_REP)
    sv = _pack([g[n] for n in small_names] + [loss_part[0:1, 0:1]], 8, 128, F32)
    s_send, s_recv, sv, s_land, s_token = _small_start(sv, tok, name="small_start")

    grads, delta, new_m, new_v = {}, {}, {}, {}

    def update(wn, dep):
        gl = [comm.reduced[name] for name, n2, _ in MATS if n2 == wn]
        shp = wl[wn].shape
        three = (len(gl),) + gl[0].shape
        flip = shp[-1] % 128 != 0
        view = (lambda t: t.reshape(three).transpose(0, 2, 1)) if flip else (lambda t: t.reshape(three))
        back = (lambda t: t.transpose(0, 2, 1).reshape(shp)) if flip else (lambda t: t.reshape(shp))
        if flip:
            gl = [t.T for t in gl]
        d, mn, vn, go = _adamw(view(wl[wn]), gl, view(ml[wn]), view(vl[wn]), name="adamw_" + wn, dep=dep)
        grads[wn], delta[wn], new_m[wn], new_v[wn] = back(go), back(d), back(mn), back(vn)
        return d

    first = [update(wn, s_token) for wn in ("w_q", "w_o", "w_kv")]
    tok = comm.advance(first)
    second = [update(wn, tok) for wn in ("f_w_in", "f_w_down", "a_out_proj")]
    comm.advance(second)
    comm.advance(second)
    update("a_in_proj", None)
    done = first + second

    sv, s_land = _small_wait(s_send, s_recv, sv, s_land, done, name="small_wait")
    sred = _small_sum(sv, s_land, jnp.reshape(2 * me + ci, (1,)).astype(jnp.int32)).reshape(-1)
    small_shapes = [g[n].shape for n in small_names] + [(1,)]
    sg = dict(zip(small_names + ["loss"], _unpack(sred, small_shapes)))
    loss = sg["loss"].reshape(())
    g_small = {}
    for n, ax in SMALL_CUT:
        size = wl[n].shape[ax]
        g_small[n] = lax.dynamic_slice_in_dim(sg[n], me * size, size, axis=ax)
    for n in SMALL_REP:
        g_small[n] = sg[n].reshape(wl[n].shape)

    pk = lambda d: _pack([d[n] for n in small_names], 8, 128, F32)[None]
    d, mn, vn, _ = _adamw(pk(wl), [pk(g_small)[0]], pk(ml), pk(vl), name="adamw_small")
    shapes = [wl[n].shape for n in small_names]
    for n, dd, mm, vv in zip(small_names, _unpack(d.reshape(-1), shapes), _unpack(mn.reshape(-1), shapes),
                             _unpack(vn.reshape(-1), shapes)):
        grads[n], delta[n], new_m[n], new_v[n] = g_small[n], dd, mm, vv

    return (loss, dx0[None], *[grads[n] for n in WEIGHTS], *[delta[n] for n in WEIGHTS],
            *[new_m[n] for n in WEIGHTS], *[new_v[n] for n in WEIGHTS])
```

```python
import math

import jax
import jax.numpy as jnp
from jax import lax
from jax.experimental import pallas as pl
from jax.experimental.pallas import tpu as pltpu

F32 = jnp.float32
BF16 = jnp.bfloat16

EPS = 1e-5
CHUNK = 256
WINDOW = 128
HEAD = 64
SSM_HEADS = 32
SSM_GROUPS = 8
SSM_STATE = 128
ATT_KV = 4
ATT_G = 4
ROPE_THETA = 10000.0
NEG = -1e30
N_CHIPS = 4
VMEM_LIMIT = 56 * 1024 * 1024

ADAM_LR, ADAM_B1, ADAM_B2, ADAM_EPS, ADAM_WD, ADAM_STEP = 0.001, 0.9, 0.999, 1e-08, 0.01, 10


def _cp(n_axes):
    return pltpu.CompilerParams(dimension_semantics=("arbitrary",) * n_axes, vmem_limit_bytes=VMEM_LIMIT)


def _pick(dim, prefs):
    for p in prefs:
        if dim % p == 0:
            return p
    return dim


def _iota(shape, dim):
    return lax.broadcasted_iota(jnp.int32, shape, dim)


def _dot(a, b, ca=1, cb=0):
    return lax.dot_general(a, b, (((ca,), (cb,)), ((), ())), preferred_element_type=F32)


def _dot3(x, ind):
    h = x.astype(BF16)
    r = x - h.astype(F32)
    m = r.astype(BF16)
    lo = (r - m.astype(F32)).astype(BF16)
    return _dot(h, ind) + _dot(m, ind) + _dot(lo, ind)


def _sigmoid(x):
    return jax.nn.sigmoid(x)


def _mm(a, b, *, name, ta=False, tb=False, bias=None, res=None, out_dtype=F32, b_koff=0, tm=None, tn=None, tk=None,
        dims=None, a_spec=None, b_spec=None, o_spec=None, o_shape=None, dep=None, more=(), target=None):
    if dims is not None:
        M, N, K = dims
    else:
        if ta:
            K, M = a.shape
        else:
            M, K = a.shape
        N = b.shape[0] if tb else b.shape[1]
    tm = tm or _pick(M, (1024, 1408, 512, 256, 128))
    tn = tn or _pick(N, (512, 1408, 256, 128))
    tk = tk or (K if K <= 2048 else _pick(K, (2048, 1408, 1024, 512)))
    assert M % tm == 0 and N % tn == 0 and K % tk == 0 and b_koff % tk == 0
    nk = K // tk
    kb0 = b_koff // tk
    has_bias, has_res = bias is not None, res is not None

    def body(*refs):
        a_ref, b_ref = refs[0], refs[1]
        pos = 2
        bias_ref = res_ref = acc_ref = None
        if has_bias:
            bias_ref = refs[pos]
            pos += 1
        if has_res:
            res_ref = refs[pos]
            pos += 1
        if dep is not None:
            pos += 1
        extra = refs[pos:pos + 2 * len(more)]
        pos += 2 * len(more)
        tgt_ref = lp_ref = None
        if target is not None:
            tgt_ref, o_ref, lp_ref = refs[pos], refs[pos + 1], refs[pos + 2]
            pos += 2
        else:
            o_ref = refs[pos]
        if nk > 1:
            acc_ref = refs[pos + 1]
        part = _dot(a_ref[...].astype(BF16), b_ref[...].astype(BF16), 0 if ta else 1, 1 if tb else 0)
        for q in range(len(more)):
            part = part + _dot(extra[2 * q][...].astype(BF16), extra[2 * q + 1][...].astype(BF16),
                               0 if ta else 1, 1 if tb else 0)

        def finish(acc):
            if has_bias:
                acc = acc + bias_ref[...]
            if has_res:
                acc = acc + res_ref[...]
            if target is not None:
                err = acc - tgt_ref[...]
                acc = err * (1.0 / N)
                part_loss = jnp.sum(jnp.sum(err * err, axis=1, keepdims=True), axis=0, keepdims=True) * (0.5 / N)
                first = (pl.program_id(0) == 0) & (pl.program_id(1) == 0)

                @pl.when(first)
                def _():
                    lp_ref[...] = jnp.broadcast_to(part_loss, lp_ref.shape)

                @pl.when(jnp.logical_not(first))
                def _():
                    lp_ref[...] += jnp.broadcast_to(part_loss, lp_ref.shape)

            o_ref[...] = acc.astype(out_dtype)

        if nk == 1:
            finish(part)
        else:
            k = pl.program_id(2)

            @pl.when(k == 0)
            def _():
                acc_ref[...] = part

            @pl.when(k > 0)
            def _():
                acc_ref[...] += part

            @pl.when(k == nk - 1)
            def _():
                finish(acc_ref[...])

    if a_spec is None:
        a_spec = pl.BlockSpec((tk, tm), lambda i, j, k: (k, i)) if ta else pl.BlockSpec((tm, tk), lambda i, j, k: (i, k))
    if b_spec is None:
        b_spec = (pl.BlockSpec((tn, tk), lambda i, j, k: (j, k + kb0)) if tb
                  else pl.BlockSpec((tk, tn), lambda i, j, k: (k + kb0, j)))
    if o_spec is None:
        o_spec = pl.BlockSpec((tm, tn), lambda i, j, k: (i, j))
    in_specs, args = [a_spec, b_spec], [a, b]
    if has_bias:
        in_specs.append(pl.BlockSpec((1, tn), lambda i, j, k: (0, j)))
        args.append(bias)
    if has_res:
        in_specs.append(pl.BlockSpec((tm, tn), lambda i, j, k: (i, j)))
        args.append(res)
    if dep is not None:
        in_specs.append(pl.BlockSpec(memory_space=pl.ANY))
        args.append(dep)
    for piece in more:
        a2, sa, b2, sb = piece if len(piece) == 4 else (a, piece[0], b, piece[1])
        in_specs += [sa, sb]
        args += [a2, b2]
    out_specs, out_shape = o_spec, jax.ShapeDtypeStruct(o_shape or (M, N), out_dtype)
    if target is not None:
        in_specs.append(pl.BlockSpec((tm, tn), lambda i, j, k: (i, j)))
        args.append(target)
        out_specs = [o_spec, pl.BlockSpec((8, 128), lambda i, j, k: (0, 0))]
        out_shape = [out_shape, jax.ShapeDtypeStruct((8, 128), F32)]
    return pl.pallas_call(
        body, name=name, grid=(M // tm, N // tn, nk), in_specs=in_specs, out_specs=out_specs, out_shape=out_shape,
        scratch_shapes=[pltpu.VMEM((tm, tn), F32)] if nk > 1 else [],
        compiler_params=_cp(3),
    )(*args)


def _norm_mm(x, gain, b, *, name, bias=None, N=None, tn=None, b_spec=None, dep=None):
    M, K = x.shape
    N = N or b.shape[1]
    tm = _pick(M, (1024, 512, 256))
    tn = tn or _pick(N, (512, 1408, 256, 128))
    has_bias = bias is not None

    def body(*refs):
        x_ref, g_ref, b_ref = refs[:3]
        pos = 3 + (1 if has_bias else 0) + (0 if dep is None else 1)
        o_ref, h_ref = refs[pos], refs[pos + 1]

        @pl.when(pl.program_id(1) == 0)
        def _():
            xv = x_ref[...]
            h_ref[...] = (xv * lax.rsqrt(jnp.mean(xv * xv, axis=-1, keepdims=True) + EPS) * g_ref[...]).astype(BF16)

        acc = _dot(h_ref[...], b_ref[...].astype(BF16))
        if has_bias:
            acc = acc + refs[3][...]
        o_ref[...] = acc

    in_specs = [pl.BlockSpec((tm, K), lambda i, j: (i, 0)), pl.BlockSpec((1, K), lambda i, j: (0, 0)),
                b_spec or pl.BlockSpec((K, tn), lambda i, j: (0, j))]
    args = [x, gain, b]
    if has_bias:
        in_specs.append(pl.BlockSpec((1, tn), lambda i, j: (0, j)))
        args.append(bias)
    if dep is not None:
        in_specs.append(pl.BlockSpec(memory_space=pl.ANY))
        args.append(dep)
    return pl.pallas_call(
        body, name=name, grid=(M // tm, N // tn), in_specs=in_specs,
        out_specs=[pl.BlockSpec((tm, tn), lambda i, j: (i, j)), pl.BlockSpec((tm, K), lambda i, j: (i, 0))],
        out_shape=[jax.ShapeDtypeStruct((M, N), F32), jax.ShapeDtypeStruct((M, K), BF16)], compiler_params=_cp(2),
    )(*args)


def _rms_bwd(x, gains, dhs, dres, *, name, tr=256, want_colsum=False):
    S, D = x.shape
    n = len(gains)
    steps = S // tr

    def body(*refs):
        x_ref = refs[0]
        g_refs = refs[1:1 + n]
        dh_refs = refs[1 + n:1 + 2 * n]
        dres_ref = refs[1 + 2 * n]
        dx_ref = refs[2 + 2 * n]
        dg_refs = refs[3 + 2 * n:3 + 3 * n]
        cs_ref = refs[3 + 3 * n] if want_colsum else None
        i = pl.program_id(0)
        xv = x_ref[...]
        r = lax.rsqrt(jnp.mean(xv * xv, axis=-1, keepdims=True) + EPS)
        xh = xv * r
        dx = dres_ref[...]
        for q in range(n):
            dh = dh_refs[q][...]
            dxh = dh * g_refs[q][...]
            dx = dx + r * (dxh - xh * jnp.mean(dxh * xh, axis=-1, keepdims=True))
            part = jnp.sum(dh * xh, axis=0, keepdims=True)

            @pl.when(i == 0)
            def _():
                dg_refs[q][...] = part

            @pl.when(i > 0)
            def _():
                dg_refs[q][...] += part

        dx_ref[...] = dx
        if want_colsum:
            cpart = jnp.sum(dx, axis=0, keepdims=True)

            @pl.when(i == 0)
            def _():
                cs_ref[...] = cpart

            @pl.when(i > 0)
            def _():
                cs_ref[...] += cpart

    row = pl.BlockSpec((tr, D), lambda i: (i, 0))
    vec = pl.BlockSpec((1, D), lambda i: (0, 0))
    n_vec_out = n + (1 if want_colsum else 0)
    outs = pl.pallas_call(
        body, name=name, grid=(steps,), in_specs=[row] + [vec] * n + [row] * n + [row],
        out_specs=[row] + [vec] * n_vec_out,
        out_shape=[jax.ShapeDtypeStruct((S, D), F32)] + [jax.ShapeDtypeStruct((1, D), F32)] * n_vec_out,
        compiler_params=_cp(1),
    )(x, *gains, *dhs, dres)
    return outs


def _colsum(x, *, name, tr=256):
    S, D = x.shape

    def body(x_ref, o_ref):
        i = pl.program_id(0)
        part = jnp.sum(x_ref[...].astype(F32), axis=0, keepdims=True)

        @pl.when(i == 0)
        def _():
            o_ref[...] = part

        @pl.when(i > 0)
        def _():
            o_ref[...] += part

    return pl.pallas_call(
        body, name=name, grid=(S // tr,), in_specs=[pl.BlockSpec((tr, D), lambda i: (i, 0))],
        out_specs=pl.BlockSpec((1, D), lambda i: (0, 0)), out_shape=jax.ShapeDtypeStruct((1, D), F32),
        compiler_params=_cp(1),
    )(x)


STRIP = 64
HALO = 8


def _strips(S, tc):
    return [(r0, slice(l0, l0 + 128)) for l0 in range(0, tc, 128) for r0 in range(S - STRIP, -1, -STRIP)]


def _with_halo(ref, r0, ls):
    if r0 == 0:
        return jnp.concatenate([jnp.zeros((HALO, 128), F32), ref[0:STRIP, ls]], axis=0)
    return ref[r0 - HALO:r0 + STRIP, ls]


def _conv_strip(xw, w_ref, b_ref, ls, width):
    acc = b_ref[:, ls] + w_ref[pl.ds(width - 1, 1), ls] * xw[HALO:]
    shifted = []
    for s in range(1, width):
        xs = pltpu.roll(xw, s, axis=0)[HALO:]
        shifted.append(xs)
        acc = acc + w_ref[pl.ds(width - 1 - s, 1), ls] * xs
    return acc, shifted


def _conv_strip_back(dacc, after, xc, shifted, w_ref, ls, width):
    ext = jnp.concatenate([dacc, after], axis=0)
    dx = w_ref[pl.ds(width - 1, 1), ls] * dacc
    dws = [None] * width
    dws[width - 1] = jnp.sum(dacc * xc, axis=0, keepdims=True)
    for s in range(1, width):
        dx = dx + w_ref[pl.ds(width - 1 - s, 1), ls] * pltpu.roll(ext, STRIP + HALO - s, axis=0)[:STRIP]
        dws[width - 1 - s] = jnp.sum(dacc * shifted[s - 1], axis=0, keepdims=True)
    return dx, dws, jnp.sum(dacc, axis=0, keepdims=True)


def _conv_back_block(S, tc, width, w_ref, b_ref, x_ref, dacc_of, dx_store, dw_ref, db_ref):
    for l0 in range(0, tc, 128):
        ls = slice(l0, l0 + 128)
        after = jnp.zeros((HALO, 128), F32)
        tot = None
        for r0 in range(S - STRIP, -1, -STRIP):
            xw = _with_halo(x_ref, r0, ls)
            acc, shifted = _conv_strip(xw, w_ref, b_ref, ls, width)
            dacc = dacc_of(r0, ls, acc, _sigmoid(acc))
            dx, dws, db = _conv_strip_back(dacc, after, xw[HALO:], shifted, w_ref, ls, width)
            dx_store(r0, ls, dx)
            after = dacc[:HALO]
            part = dws + [db]
            tot = part if tot is None else [p + q for p, q in zip(tot, part)]
        for k in range(width):
            dw_ref[pl.ds(k, 1), ls] = tot[k]
        db_ref[:, ls] = tot[width]


def _conv_silu_fwd(xin, col0, C, w, b, *, name, tc=512):
    S = xin.shape[0]
    width = w.shape[0]
    off = col0 // tc

    def body(x_ref, w_ref, b_ref, o_ref):
        for r0, ls in _strips(S, tc):
            acc, _ = _conv_strip(_with_halo(x_ref, r0, ls), w_ref, b_ref, ls, width)
            o_ref[r0:r0 + STRIP, ls] = acc * _sigmoid(acc)

    return pl.pallas_call(
        body, name=name, grid=(C // tc,),
        in_specs=[pl.BlockSpec((S, tc), lambda j: (0, j + off)), pl.BlockSpec((width, tc), lambda j: (0, j)),
                  pl.BlockSpec((1, tc), lambda j: (0, j))],
        out_specs=pl.BlockSpec((S, tc), lambda j: (0, j)), out_shape=jax.ShapeDtypeStruct((S, C), F32),
        compiler_params=_cp(1),
    )(xin, w, b)


def _conv_silu_bwd(xin, col0, C, w, b, douts, *, name, tc=256):
    S = xin.shape[0]
    width = w.shape[0]
    off = col0 // tc
    nd = len(douts)
    ranges = [(o // tc, (o + d.shape[1]) // tc) for d, o in douts]

    def body(*refs):
        x_ref, w_ref, b_ref = refs[0], refs[1], refs[2]
        d_refs = refs[3:3 + nd]
        dx_ref, dw_ref, db_ref = refs[3 + nd], refs[4 + nd], refs[5 + nd]
        j = pl.program_id(0)

        def dacc_of(r0, ls, acc, sg):
            dout = jnp.zeros((STRIP, 128), F32)
            for q in range(nd):
                lo, hi = ranges[q]
                dout = dout + jnp.where((j >= lo) & (j < hi), d_refs[q][r0:r0 + STRIP, ls], 0.0)
            return dout * (sg * (1.0 + acc * (1.0 - sg)))

        def dx_store(r0, ls, dx):
            dx_ref[r0:r0 + STRIP, ls] = dx.astype(BF16)

        _conv_back_block(S, tc, width, w_ref, b_ref, x_ref, dacc_of, dx_store, dw_ref, db_ref)

    d_specs = [pl.BlockSpec((S, tc), (lambda j, lo=lo, hi=hi: (0, jnp.clip(j - lo, 0, hi - lo - 1)))) for lo, hi in ranges]
    return pl.pallas_call(
        body, name=name, grid=(C // tc,),
        in_specs=[pl.BlockSpec((S, tc), lambda j: (0, j + off)), pl.BlockSpec((width, tc), lambda j: (0, j)),
                  pl.BlockSpec((1, tc), lambda j: (0, j))] + d_specs,
        out_specs=[pl.BlockSpec((S, tc), lambda j: (0, j)), pl.BlockSpec((width, tc), lambda j: (0, j)),
                   pl.BlockSpec((1, tc), lambda j: (0, j))],
        out_shape=[jax.ShapeDtypeStruct((S, C), BF16), jax.ShapeDtypeStruct((width, C), F32),
                   jax.ShapeDtypeStruct((1, C), F32)],
        compiler_params=_cp(1),
    )(xin, w, b, *[d for d, _ in douts])


def _ffn_act_fwd(u, w, b, *, name, tc=256):
    S, F2 = u.shape
    Fd = F2 // 2
    width = w.shape[0]
    nb = Fd // tc

    def body(g_ref, v_ref, w_ref, b_ref, o_ref):
        for r0, ls in _strips(S, tc):
            acc, _ = _conv_strip(_with_halo(g_ref, r0, ls), w_ref, b_ref, ls, width)
            o_ref[r0:r0 + STRIP, ls] = (acc * _sigmoid(acc) * v_ref[r0:r0 + STRIP, ls]).astype(BF16)

    return pl.pallas_call(
        body, name=name, grid=(nb,),
        in_specs=[pl.BlockSpec((S, tc), lambda j: (0, j)), pl.BlockSpec((S, tc), lambda j: (0, j + nb)),
                  pl.BlockSpec((width, tc), lambda j: (0, j)), pl.BlockSpec((1, tc), lambda j: (0, j))],
        out_specs=pl.BlockSpec((S, tc), lambda j: (0, j)), out_shape=jax.ShapeDtypeStruct((S, Fd), BF16),
        compiler_params=_cp(1),
    )(u, u, w, b)


def _ffn_act_bwd(u, w, b, da, *, name, tc=256):
    S, F2 = u.shape
    Fd = F2 // 2
    width = w.shape[0]
    nb = Fd // tc

    def body(g_ref, v_ref, w_ref, b_ref, da_ref, du_ref, dw_ref, db_ref, a_ref):
        def dacc_of(r0, ls, acc, sg):
            rs = slice(r0, r0 + STRIP)
            dav, val, silu = da_ref[rs, ls], v_ref[rs, ls], acc * sg
            a_ref[rs, ls] = (silu * val).astype(BF16)
            du_ref[1, rs, ls] = (dav * silu).astype(BF16)
            return dav * val * (sg * (1.0 + acc * (1.0 - sg)))

        def dx_store(r0, ls, dx):
            du_ref[0, r0:r0 + STRIP, ls] = dx.astype(BF16)

        _conv_back_block(S, tc, width, w_ref, b_ref, g_ref, dacc_of, dx_store, dw_ref, db_ref)

    blk = pl.BlockSpec((S, tc), lambda j: (0, j))
    return pl.pallas_call(
        body, name=name, grid=(nb,),
        in_specs=[blk, pl.BlockSpec((S, tc), lambda j: (0, j + nb)), pl.BlockSpec((width, tc), lambda j: (0, j)),
                  pl.BlockSpec((1, tc), lambda j: (0, j)), blk],
        out_specs=[pl.BlockSpec((2, S, tc), lambda j: (0, 0, j)), pl.BlockSpec((width, tc), lambda j: (0, j)),
                   pl.BlockSpec((1, tc), lambda j: (0, j)), blk],
        out_shape=[jax.ShapeDtypeStruct((2, S, Fd), BF16),
                   jax.ShapeDtypeStruct((width, Fd), F32), jax.ShapeDtypeStruct((1, Fd), F32),
                   jax.ShapeDtypeStruct((S, Fd), BF16)],
        compiler_params=_cp(1),
    )(u, u, w, b, da)


def _ssd_prep(dtr, dt_bias, a_log, *, name="ssd_prep"):
    S = dtr.shape[0]

    def body(d_ref, b_ref, al_ref, dt_ref, ac_ref, sg_ref, act_ref):
        lane = _iota((CHUNK, 128), 1)
        valid = lane < SSM_HEADS
        z = d_ref[...] + b_ref[...]
        dt = jnp.where(valid, jnp.maximum(z, 0.0) + jnp.log(1.0 + jnp.exp(-jnp.abs(z))), 0.0)
        a = dt * (-jnp.exp(al_ref[...]))
        row = _iota((CHUNK, 128), 0)
        k = 1
        while k < CHUNK:
            a = a + jnp.where(row >= k, pltpu.roll(a, k, axis=0), 0.0)
            k *= 2
        sg = jnp.where(valid, _sigmoid(z), 0.0)
        for arr, ref in ((dt, dt_ref), (a, ac_ref), (sg, sg_ref)):
            for g in range(SSM_GROUPS):
                ref[g] = jnp.where(lane < 4, arr if g == 0 else pltpu.roll(arr, 128 - 4 * g, axis=1), 0.0)
        act_ref[...] = a.T[:SSM_HEADS, :]

    blk = pl.BlockSpec((CHUNK, 128), lambda i: (i, 0))
    vec = pl.BlockSpec((1, 128), lambda i: (0, 0))
    grp = pl.BlockSpec((SSM_GROUPS, CHUNK, 128), lambda i: (0, i, 0))
    return pl.pallas_call(
        body, name=name, grid=(S // CHUNK,), in_specs=[blk, vec, vec],
        out_specs=[grp, grp, grp, pl.BlockSpec((SSM_HEADS, CHUNK), lambda i: (0, i))],
        out_shape=[jax.ShapeDtypeStruct((SSM_GROUPS, S, 128), F32)] * 3 + [jax.ShapeDtypeStruct((SSM_HEADS, S), F32)],
        compiler_params=_cp(1),
    )(dtr, dt_bias, a_log)


SSD_GPS = 4


def _expand4(v, lanes):
    out = jnp.broadcast_to(v[:, 3:4], lanes.shape)
    for hh in (2, 1, 0):
        out = jnp.where(lanes < 64 * (hh + 1), v[:, hh:hh + 1], out)
    return out


def _ssd_fwd(xbc, dt_g, ac_g, ac_t, *, name="ssd_fwd"):
    S = xbc.shape[0]
    nc = S // CHUNK
    Lc = CHUNK

    def body(x_ref, b_ref, c_ref, dt_ref, ac_ref, act_ref, y_ref, st_out_ref, st_ref):
        g2 = pl.program_id(0)
        c = pl.program_id(1)

        @pl.when(c == 0)
        def _():
            st_ref[...] = jnp.zeros_like(st_ref)

        causal = _iota((Lc, Lc), 0) >= _iota((Lc, Lc), 1)
        lane256 = _iota((Lc, 256), 1)
        lane128 = _iota((Lc, 128), 1)
        row128 = _iota((128, 128), 0)
        for gg in range(SSD_GPS):
            g = SSD_GPS * g2 + gg
            bv = b_ref[:, 128 * gg:128 * (gg + 1)]
            cbf = c_ref[:, 128 * gg:128 * (gg + 1)].astype(BF16)
            cb = _dot(cbf, bv.astype(BF16), 1, 1)
            dtg, acg = dt_ref[gg], ac_ref[gg]
            ac_last = ac_ref[gg, pl.ds(Lc - 1, 1), :]
            dt4 = _expand4(dtg, lane256)
            ac4 = _expand4(acg, lane256)
            e4 = jnp.exp(ac4)
            xdb = (x_ref[:, 256 * gg:256 * (gg + 1)] * dt4).astype(BF16)
            st_out_ref[gg] = st_ref[gg]
            for p in range(2):
                xd_p = xdb[:, 128 * p:128 * (p + 1)]
                st_p = st_ref[gg, p]
                ys, sn, cds = [], [], []
                for q in range(2):
                    hh = 2 * p + q
                    a_col = acg[:, hh:hh + 1]
                    a_row = act_ref[pl.ds(4 * g + hh, 1), :]
                    dec = jnp.exp(jnp.where(causal, a_col - a_row, NEG))
                    w = (cb * dec).astype(BF16)
                    ys.append(_dot(w, xd_p))
                    al = ac_last[:, hh:hh + 1]
                    dte = jnp.exp(al - a_col)
                    sn.append(_dot(xd_p, (bv * dte).astype(BF16), 0, 0))
                    cds.append(jnp.exp(al))
                y_diag = jnp.where(lane128 < 64, ys[0], ys[1])
                y_off = _dot(cbf, st_p.astype(BF16), 1, 1) * e4[:, 128 * p:128 * (p + 1)]
                y_ref[:, 256 * gg + 128 * p:256 * gg + 128 * (p + 1)] = y_diag + y_off
                st_ref[gg, p] = jnp.where(row128 < 64, st_p * cds[0] + sn[0], st_p * cds[1] + sn[1])

    G = SSD_GPS
    per_g = lambda g, c: (g, c, 0)
    return pl.pallas_call(
        body, name=name, grid=(SSM_GROUPS // G, nc),
        in_specs=[pl.BlockSpec((Lc, 256 * G), lambda g, c: (c, g)),
                  pl.BlockSpec((Lc, 128 * G), lambda g, c: (c, 16 // G + g)),
                  pl.BlockSpec((Lc, 128 * G), lambda g, c: (c, 24 // G + g)),
                  pl.BlockSpec((G, Lc, 128), per_g), pl.BlockSpec((G, Lc, 128), per_g),
                  pl.BlockSpec((SSM_HEADS, Lc), lambda g, c: (0, c))],
        out_specs=[pl.BlockSpec((Lc, 256 * G), lambda g, c: (c, g)),
                   pl.BlockSpec((G, None, 2, 128, 128), lambda g, c: (g, c, 0, 0, 0))],
        out_shape=[jax.ShapeDtypeStruct((S, 2048), F32), jax.ShapeDtypeStruct((SSM_GROUPS, nc, 2, 128, 128), F32)],
        scratch_shapes=[pltpu.VMEM((G, 2, 128, 128), F32)], compiler_params=_cp(2),
    )(xbc, xbc, xbc, dt_g, ac_g, ac_t)


def _ssd_bwd(xbc, dt_g, ac_g, ac_t, states, dy, dexp, *, name="ssd_bwd", dep=None):
    S = xbc.shape[0]
    nc = S // CHUNK
    Lc = CHUNK

    def body(x_ref, b_ref, c_ref, dt_ref, ac_ref, act_ref, st_ref, dy_ref, d_ref, *rest):
        dx_ref, db_ref, dc_ref, dh_ref, ds_ref = rest[-5:]
        g2 = pl.program_id(0)
        cc = pl.program_id(1)

        @pl.when(cc == 0)
        def _():
            ds_ref[...] = jnp.zeros_like(ds_ref)

        causal = _iota((Lc, Lc), 0) >= _iota((Lc, Lc), 1)
        lane256 = _iota((Lc, 256), 1)
        lane128 = _iota((Lc, 128), 1)
        row128 = _iota((128, 128), 0)
        ind_rows = _iota((256, 128), 0) >> 6
        ind_cols = _iota((256, 128), 1)
        ind_a = (ind_rows == ind_cols).astype(BF16)
        ind_b = (ind_rows + 4 == ind_cols).astype(BF16)
        for gg in range(SSD_GPS):
            g = SSD_GPS * g2 + gg
            bv = b_ref[:, 128 * gg:128 * (gg + 1)]
            cv = c_ref[:, 128 * gg:128 * (gg + 1)]
            bbf, cbf = bv.astype(BF16), cv.astype(BF16)
            cb = _dot(cbf, bbf, 1, 1)
            dtg, acg = dt_ref[gg], ac_ref[gg]
            ac_last = ac_ref[gg, pl.ds(Lc - 1, 1), :]
            dt4 = _expand4(dtg, lane256)
            ac4 = _expand4(acg, lane256)
            acl4 = _expand4(ac_last, _iota((1, 256), 1))
            e4 = jnp.exp(ac4)
            dte4 = jnp.exp(acl4 - ac4)
            xv = x_ref[:, 256 * gg:256 * (gg + 1)]
            xd = xv * dt4
            xdb = xd.astype(BF16)
            dyv = dy_ref[:, 256 * gg:256 * (gg + 1)]
            dcb = jnp.zeros((Lc, Lc), F32)
            dc_acc = jnp.zeros((Lc, 128), F32)
            db_acc = jnp.zeros((Lc, 128), F32)
            u_parts, dxd_parts, ends = [], [], []
            for p in range(2):
                sl = slice(128 * p, 128 * (p + 1))
                xd_p, xdb_p, dy_p = xd[:, sl], xdb[:, sl], dyv[:, sl]
                dyb_p = dy_p.astype(BF16)
                e_p, dte_p = e4[:, sl], dte4[:, sl]
                sp = st_ref[gg, p]
                spb = sp.astype(BF16)
                dsn = ds_ref[gg, p]
                dsnb = dsn.astype(BF16)
                yds, dxds, cds = [], [], []
                for q in range(2):
                    hh = 2 * p + q
                    a_col = acg[:, hh:hh + 1]
                    a_row = act_ref[pl.ds(4 * g + hh, 1), :]
                    dec = jnp.exp(jnp.where(causal, a_col - a_row, NEG))
                    w = (cb * dec).astype(BF16)
                    head = (lane128 < 64) if q == 0 else (lane128 >= 64)
                    dym = jnp.where(head, dyb_p, jnp.zeros_like(dyb_p))
                    dw = _dot(dym, xdb_p, 1, 1)
                    dcb = dcb + dw * dec
                    yds.append(_dot(w, xdb_p))
                    dxds.append(_dot(w, dyb_p, 0, 0))
                    cds.append(jnp.exp(ac_last[:, hh:hh + 1]))
                y_diag = jnp.where(lane128 < 64, yds[0], yds[1])
                dxd_diag = jnp.where(lane128 < 64, dxds[0], dxds[1])
                y_off = _dot(cbf, spb, 1, 1) * e_p
                dgp = dy_p * e_p
                dgb = dgp.astype(BF16)
                dc_acc = dc_acc + _dot(dgb, spb)
                dsp = _dot(dgb, cbf, 0, 0)
                cd_col = jnp.where(row128[:, 0:1] < 64, cds[0], cds[1])
                qm = _dot(bbf, dsnb, 1, 1)
                dxd_state = dte_p * qm
                db_acc = db_acc + _dot((xd_p * dte_p).astype(BF16), dsnb)
                t_p = xd_p * dxd_state
                prod = dsn * sp
                e0 = jnp.sum(jnp.sum(jnp.where(row128 < 64, prod, 0.0), axis=1, keepdims=True), axis=0, keepdims=True)
                e1 = jnp.sum(jnp.sum(jnp.where(row128 >= 64, prod, 0.0), axis=1, keepdims=True), axis=0, keepdims=True)
                tcol = jnp.sum(t_p, axis=0, keepdims=True)
                lane1 = _iota((1, 128), 1)
                t0 = jnp.sum(jnp.where(lane1 < 64, tcol, 0.0), axis=1, keepdims=True)
                t1 = jnp.sum(jnp.where(lane1 >= 64, tcol, 0.0), axis=1, keepdims=True)
                ends.append(e0 * cds[0] + t0)
                ends.append(e1 * cds[1] + t1)
                ds_ref[gg, p] = dsn * cd_col + dsp
                u_parts.append(dyb_p.astype(F32) * y_diag - xdb_p.astype(F32) * dxd_diag + dy_p * y_off - t_p)
                dxd_parts.append(dxd_diag + dxd_state)
            dxd = jnp.concatenate(dxd_parts, axis=1)
            u_all = jnp.concatenate(u_parts, axis=1)
            dx_ref[:, 256 * gg:256 * (gg + 1)] = dxd * dt4 + dyv * d_ref[:, 256 * gg:256 * (gg + 1)]
            dcbb = dcb.astype(BF16)
            dc_ref[:, 128 * gg:128 * (gg + 1)] = dc_acc + _dot(dcbb, bbf)
            db_ref[:, 128 * gg:128 * (gg + 1)] = db_acc + _dot(dcbb, cbf, 0, 0)
            lane = _iota((Lc, 128), 1)
            endv = jnp.zeros((Lc, 128), F32)
            for hh in range(4):
                endv = jnp.where(lane == 8 + hh, ends[hh], endv)
            dh_ref[gg] = _dot3(dxd * xv, ind_a) + _dot3(u_all, ind_b) + endv

    G = SSD_GPS
    rev = lambda c: nc - 1 - c
    per_g = lambda g, c: (g, rev(c), 0)
    return pl.pallas_call(
        body, name=name, grid=(SSM_GROUPS // G, nc),
        in_specs=[pl.BlockSpec((Lc, 256 * G), lambda g, c: (rev(c), g)),
                  pl.BlockSpec((Lc, 128 * G), lambda g, c: (rev(c), 16 // G + g)),
                  pl.BlockSpec((Lc, 128 * G), lambda g, c: (rev(c), 24 // G + g)),
                  pl.BlockSpec((G, Lc, 128), per_g), pl.BlockSpec((G, Lc, 128), per_g),
                  pl.BlockSpec((SSM_HEADS, Lc), lambda g, c: (0, rev(c))),
                  pl.BlockSpec((G, None, 2, 128, 128), lambda g, c: (g, rev(c), 0, 0, 0)),
                  pl.BlockSpec((Lc, 256 * G), lambda g, c: (rev(c), g)),
                  pl.BlockSpec((1, 256 * G), lambda g, c: (0, g))] + ([] if dep is None else [pl.BlockSpec(memory_space=pl.ANY)]),
        out_specs=[pl.BlockSpec((Lc, 256 * G), lambda g, c: (rev(c), g)),
                   pl.BlockSpec((Lc, 128 * G), lambda g, c: (rev(c), g)),
                   pl.BlockSpec((Lc, 128 * G), lambda g, c: (rev(c), g)),
                   pl.BlockSpec((G, Lc, 128), per_g)],
        out_shape=[jax.ShapeDtypeStruct((S, 2048), F32), jax.ShapeDtypeStruct((S, 1024), F32),
                   jax.ShapeDtypeStruct((S, 1024), F32), jax.ShapeDtypeStruct((SSM_GROUPS, S, 128), F32)],
        scratch_shapes=[pltpu.VMEM((G, 2, 128, 128), F32)], compiler_params=_cp(2),
    )(xbc, xbc, xbc, dt_g, ac_g, ac_t, states, dy, dexp, *([] if dep is None else [dep]))


def _ssd_post(dhead, dt_g, sg_g, alog_g, *, name="ssd_post"):
    S = dhead.shape[1]
    nc = S // CHUNK
    Lc = CHUNK

    def body(dh_ref, dt_ref, sg_ref, al_ref, o_ref, s_ref):
        @pl.when(pl.program_id(0) == 0)
        def _():
            s_ref[...] = jnp.zeros_like(s_ref)

        lane = _iota((Lc, 128), 1)
        row = _iota((Lc, 128), 0)
        row8 = _iota((8, 128), 0)
        out = jnp.zeros((Lc, 128), F32)
        for g in range(SSM_GROUPS):
            dh = dh_ref[g]
            a_neg = -jnp.exp(al_ref[g])
            dac = jnp.where(lane < 4, pltpu.roll(dh, 124, axis=1), 0.0)
            end = jnp.where(lane < 4, pltpu.roll(dh, 120, axis=1), 0.0)
            k = 1
            while k < Lc:
                dac = dac + jnp.where(row < Lc - k, pltpu.roll(dac, Lc - k, axis=0), 0.0)
                k *= 2
            da = dac + end
            ddt = jnp.where(lane < 4, da * a_neg + dh, 0.0)
            ddtr = ddt * sg_ref[g]
            out = out + (ddtr if g == 0 else pltpu.roll(ddtr, 4 * g, axis=1))
            dal = jnp.sum(da * dt_ref[g], axis=0, keepdims=True) * a_neg
            dbias = jnp.sum(ddtr, axis=0, keepdims=True)
            part = jnp.where(row8 == 0, dal, jnp.where(row8 == 1, dbias, 0.0))
            s_ref[g] += part
        o_ref[...] = out.astype(BF16)

    grp = pl.BlockSpec((SSM_GROUPS, Lc, 128), lambda c: (0, c, 0))
    whole = lambda r: pl.BlockSpec((SSM_GROUPS, r, 128), lambda c: (0, 0, 0))
    return pl.pallas_call(
        body, name=name, grid=(nc,), in_specs=[grp, grp, grp, whole(1)],
        out_specs=[pl.BlockSpec((Lc, 128), lambda c: (c, 0)), whole(8)],
        out_shape=[jax.ShapeDtypeStruct((S, 128), BF16), jax.ShapeDtypeStruct((SSM_GROUPS, 8, 128), F32)],
        compiler_params=_cp(1),
    )(dhead, dt_g, sg_g, alog_g)


def _gate_fwd(y, xbc, zx, dexp, gn, *, name="gate_fwd", tr=256):
    S = y.shape[0]
    W = 2048
    gw = W // SSM_GROUPS

    def body(y_ref, x_ref, z_ref, d_ref, g_ref, o_ref):
        z = z_ref[...]
        u = (y_ref[...] + x_ref[...] * d_ref[...]) * (z * _sigmoid(z))
        gv = g_ref[...]
        for q in range(SSM_GROUPS):
            sl = slice(gw * q, gw * (q + 1))
            uq = u[:, sl]
            r = lax.rsqrt(jnp.mean(uq * uq, axis=-1, keepdims=True) + EPS)
            o_ref[:, sl] = (uq * r * gv[:, sl]).astype(BF16)

    row = pl.BlockSpec((tr, W), lambda i: (i, 0))
    vec = pl.BlockSpec((1, W), lambda i: (0, 0))
    return pl.pallas_call(
        body, name=name, grid=(S // tr,), in_specs=[row, row, row, vec, vec], out_specs=row,
        out_shape=jax.ShapeDtypeStruct((S, W), BF16), compiler_params=_cp(1),
    )(y, xbc, zx, dexp, gn)


def _gate_bwd(y, xbc, zx, dexp, gn, dout, *, name="gate_bwd", tr=256):
    S = y.shape[0]
    W = 2048
    gw = W // SSM_GROUPS
    steps = S // tr

    def body(y_ref, x_ref, z_ref, d_ref, g_ref, do_ref, dy_ref, dz_ref, dg_ref, dd_ref, acc_ref):
        i = pl.program_id(0)

        @pl.when(i == 0)
        def _():
            acc_ref[...] = jnp.zeros_like(acc_ref)

        z = z_ref[...]
        sg = _sigmoid(z)
        sz = z * sg
        xs = x_ref[...]
        yt = y_ref[...] + xs * d_ref[...]
        u = yt * sz
        gv = g_ref[...]
        do = do_ref[...]
        dgs = []
        for q in range(SSM_GROUPS):
            sl = slice(gw * q, gw * (q + 1))
            uq = u[:, sl]
            r = lax.rsqrt(jnp.mean(uq * uq, axis=-1, keepdims=True) + EPS)
            uh = uq * r
            dq = do[:, sl]
            duh = dq * gv[:, sl]
            duq = r * (duh - uh * jnp.mean(duh * uh, axis=-1, keepdims=True))
            dgs.append(jnp.sum(dq * uh, axis=0, keepdims=True))
            dyt = duq * sz[:, sl]
            dy_ref[:, sl] = dyt
            dz_ref[:, sl] = (duq * yt[:, sl] * (sg[:, sl] * (1.0 + z[:, sl] * (1.0 - sg[:, sl])))).astype(BF16)
            acc_ref[:, sl] += jnp.sum(dyt * xs[:, sl], axis=0, keepdims=True)
        dg = jnp.concatenate(dgs, axis=1)

        @pl.when(i == 0)
        def _():
            dg_ref[...] = dg

        @pl.when(i > 0)
        def _():
            dg_ref[...] += dg

        @pl.when(i == steps - 1)
        def _():
            ind = ((_iota((W, 128), 0) >> 6) == _iota((W, 128), 1)).astype(BF16)
            dd_ref[...] = _dot3(jnp.broadcast_to(acc_ref[...], (8, W)), ind)[0:1, :]

    row = pl.BlockSpec((tr, W), lambda i: (i, 0))
    vec = pl.BlockSpec((1, W), lambda i: (0, 0))
    return pl.pallas_call(
        body, name=name, grid=(steps,), in_specs=[row, row, row, vec, vec, row],
        out_specs=[row, row, vec, pl.BlockSpec((1, 128), lambda i: (0, 0))],
        out_shape=[jax.ShapeDtypeStruct((S, W), F32), jax.ShapeDtypeStruct((S, W), BF16),
                   jax.ShapeDtypeStruct((1, W), F32), jax.ShapeDtypeStruct((1, 128), F32)],
        scratch_shapes=[pltpu.VMEM((1, W), F32)], compiler_params=_cp(1),
    )(y, xbc, zx, dexp, gn, dout)


def _rope_cs(posf, *, name="rope_tables", tr=256):
    S = posf.shape[0]

    def body(p_ref, c_ref, s_ref):
        j = (_iota((tr, 128), 1) & 31).astype(F32)
        ang = p_ref[...] * jnp.exp(j * (-math.log(ROPE_THETA) / 32.0))
        c_ref[...] = jnp.cos(ang)
        s_ref[...] = jnp.sin(ang)

    blk = pl.BlockSpec((tr, 128), lambda i: (i, 0))
    return pl.pallas_call(
        body, name=name, grid=(S // tr,), in_specs=[pl.BlockSpec((tr, 1), lambda i: (i, 0))], out_specs=[blk, blk],
        out_shape=[jax.ShapeDtypeStruct((S, 128), F32)] * 2, compiler_params=_cp(1),
    )(posf)


def _rope_tables(c_ref, s_ref, shape):
    reps = shape[1] // 128
    return jnp.tile(c_ref[...], (1, reps)), jnp.tile(s_ref[...], (1, reps)), (_iota(shape, 1) & 63) < 32


def _hn_inds(W):
    ind = ((_iota((W, 128), 0) >> 6) == _iota((W, 128), 1)).astype(BF16)
    ind_t = ((_iota((128, W), 1) >> 6) == _iota((128, W), 0)).astype(BF16)
    return ind, ind_t


def _hnrope_fwd(xin, col0, W, gain_w, rope, *, name, tr=256):
    S = xin.shape[0]
    off = col0 // W
    nh = W // HEAD

    def body(x_ref, g_ref, c_ref, s_ref, o_ref):
        x = x_ref[...]
        ind, ind_t = _hn_inds(W)
        r = lax.rsqrt(_dot3(x * x, ind) * (1.0 / HEAD) + EPS)
        xn = x * _dot3(r, ind_t) * g_ref[...]
        cs, sn, half = _rope_tables(c_ref, s_ref, (tr, W))
        rot = jnp.where(half, -pltpu.roll(xn, W - 32, axis=1), pltpu.roll(xn, 32, axis=1))
        out = (xn * cs + rot * sn).astype(BF16)
        for h in range(nh):
            o_ref[h] = out[:, HEAD * h:HEAD * (h + 1)]

    tab = pl.BlockSpec((tr, 128), lambda i: (i, 0))
    return pl.pallas_call(
        body, name=name, grid=(S // tr,),
        in_specs=[pl.BlockSpec((tr, W), lambda i: (i, off)), pl.BlockSpec((1, W), lambda i: (0, 0)), tab, tab],
        out_specs=pl.BlockSpec((nh, tr, HEAD), lambda i: (0, i, 0)), out_shape=jax.ShapeDtypeStruct((nh, S, HEAD), BF16),
        compiler_params=_cp(1),
    )(xin, gain_w, *rope)


def _hnrope_bwd(xin, col0, W, gain_w, rope, dout, *, name, tr=256):
    S = xin.shape[0]
    off = col0 // W
    steps = S // tr
    nh = W // HEAD

    def body(x_ref, g_ref, c_ref, s_ref, do_ref, dx_ref, cs_ref, dg_ref, acc_ref):
        i = pl.program_id(0)
        x = x_ref[...]
        ind, ind_t = _hn_inds(W)
        r = lax.rsqrt(_dot3(x * x, ind) * (1.0 / HEAD) + EPS)
        rw = _dot3(r, ind_t)
        xh = x * rw
        cs, sn, half = _rope_tables(c_ref, s_ref, (tr, W))
        do = jnp.concatenate([do_ref[h] for h in range(nh)], axis=1).astype(F32)
        gs = do * sn
        g1 = do * cs + jnp.where(half, pltpu.roll(gs, W - 32, axis=1), -pltpu.roll(gs, 32, axis=1))
        dxh = g1 * g_ref[...]
        t = _dot3(dxh * xh, ind) * (1.0 / HEAD)
        dx = rw * (dxh - xh * _dot3(t, ind_t))
        dx_ref[...] = dx.astype(BF16)
        cpart = jnp.sum(dx, axis=0, keepdims=True)
        gpart = jnp.sum(g1 * xh, axis=0, keepdims=True)

        @pl.when(i == 0)
        def _():
            cs_ref[...] = cpart
            acc_ref[...] = gpart

        @pl.when(i > 0)
        def _():
            cs_ref[...] += cpart
            acc_ref[...] += gpart

        @pl.when(i == steps - 1)
        def _():
            fold = ((_iota((W, 128), 0) & 63) == _iota((W, 128), 1)).astype(BF16)
            dg_ref[...] = _dot3(jnp.broadcast_to(acc_ref[...], (8, W)), fold)[0:1, :]

    tab = pl.BlockSpec((tr, 128), lambda i: (i, 0))
    return pl.pallas_call(
        body, name=name, grid=(steps,),
        in_specs=[pl.BlockSpec((tr, W), lambda i: (i, off)), pl.BlockSpec((1, W), lambda i: (0, 0)), tab, tab,
                  pl.BlockSpec((nh, tr, HEAD), lambda i: (0, i, 0))],
        out_specs=[pl.BlockSpec((tr, W), lambda i: (i, 0)), pl.BlockSpec((1, W), lambda i: (0, 0)),
                   pl.BlockSpec((1, 128), lambda i: (0, 0))],
        out_shape=[jax.ShapeDtypeStruct((S, W), BF16), jax.ShapeDtypeStruct((1, W), F32),
                   jax.ShapeDtypeStruct((1, 128), F32)],
        scratch_shapes=[pltpu.VMEM((1, W), F32)], compiler_params=_cp(1),
    )(xin, gain_w, *rope, dout)


def _attn_band():
    qi = jnp.arange(ATT_G * WINDOW)[:, None] % WINDOW
    ki = jnp.arange(2 * WINDOW)[None, :]
    rel = qi + WINDOW - ki
    ok = (rel >= 0) & (rel < WINDOW)
    return jnp.stack([jnp.where(ok & (ki >= WINDOW), 0.0, NEG), jnp.where(ok, 0.0, NEG)]).astype(F32)


def _attn_probs(q, kb, sink_ref, band_ref, h, i):
    s = _dot(q, kb, 1, 1) * (HEAD ** -0.5) + band_ref[jnp.minimum(i, 1)]
    r1 = _iota((4 * WINDOW, 1), 0)
    sink = jnp.where(r1 < WINDOW, sink_ref[4 * h], jnp.where(r1 < 2 * WINDOW, sink_ref[4 * h + 1],
                     jnp.where(r1 < 3 * WINDOW, sink_ref[4 * h + 2], sink_ref[4 * h + 3])))
    m = jnp.maximum(jnp.max(s, axis=1, keepdims=True), sink)
    p = jnp.exp(s - m)
    ps = jnp.exp(sink - m)
    inv = 1.0 / (jnp.sum(p, axis=1, keepdims=True) + ps)
    return p * inv, ps * inv


ATT_HPS = 4
_BAND = pl.BlockSpec((2, ATT_G * WINDOW, 2 * WINDOW), lambda h, i: (0, 0, 0))


def _attn_specs(S):
    qspec = pl.BlockSpec((ATT_HPS, ATT_G, WINDOW, HEAD), lambda h, i: (h, 0, i, 0))
    cur = pl.BlockSpec((ATT_HPS, WINDOW, HEAD), lambda h, i: (h, i, 0))
    prev = pl.BlockSpec((ATT_HPS, WINDOW, HEAD), lambda h, i: (h, jnp.maximum(i - 1, 0), 0))
    tok = pl.BlockSpec((WINDOW, ATT_HPS * ATT_G * HEAD), lambda h, i: (i, h))
    return qspec, cur, prev, tok


def _attn_fwd(qh, kh, vh, sinks, *, name="attn_fwd"):
    S = kh.shape[1]
    nb = S // WINDOW

    def body(s_ref, band_ref, q_ref, kc_ref, kp_ref, vc_ref, vp_ref, o_ref):
        h2, i = pl.program_id(0), pl.program_id(1)
        outs = []
        for hh in range(ATT_HPS):
            q = q_ref[hh].reshape(ATT_G * WINDOW, HEAD)
            kb = jnp.concatenate([kp_ref[hh], kc_ref[hh]], axis=0)
            vb = jnp.concatenate([vp_ref[hh], vc_ref[hh]], axis=0)
            probs, _ = _attn_probs(q, kb, s_ref, band_ref, ATT_HPS * h2 + hh, i)
            o = _dot(probs.astype(BF16), vb).astype(BF16)
            outs += [o[WINDOW * g:WINDOW * (g + 1)] for g in range(ATT_G)]
        o_ref[...] = jnp.concatenate(outs, axis=1)

    qspec, cur, prev, tok = _attn_specs(S)
    return pl.pallas_call(
        body, name=name, grid=(ATT_KV // ATT_HPS, nb),
        in_specs=[pl.BlockSpec(memory_space=pltpu.SMEM), _BAND, qspec, cur, prev, cur, prev], out_specs=tok,
        out_shape=jax.ShapeDtypeStruct((S, ATT_KV * ATT_G * HEAD), BF16), compiler_params=_cp(2),
    )(sinks, _attn_band(), qh, kh, kh, vh, vh)


def _attn_bwd(qh, kh, vh, sinks, doh, *, name="attn_bwd"):
    S = kh.shape[1]
    nb = S // WINDOW

    def body(s_ref, band_ref, q_ref, kc_ref, kp_ref, vc_ref, vp_ref, do_ref, dq_ref, dk_ref, dv_ref, dsk_ref):
        h2, i = pl.program_id(0), pl.program_id(1)

        @pl.when(i == 0)
        def _():
            dk_ref[...] = jnp.zeros_like(dk_ref)
            dv_ref[...] = jnp.zeros_like(dv_ref)
            dsk_ref[...] = jnp.zeros_like(dsk_ref)

        dov = do_ref[...]
        cur = pl.multiple_of(i * WINDOW, WINDOW)
        lane = _iota((8, 128), 1)
        row = _iota((8, 128), 0)
        scale = HEAD ** -0.5
        for hh in range(ATT_HPS):
            q = q_ref[hh].reshape(ATT_G * WINDOW, HEAD)
            do = jnp.concatenate([dov[:, HEAD * (ATT_G * hh + g):HEAD * (ATT_G * hh + g + 1)] for g in range(ATT_G)], axis=0)
            kb = jnp.concatenate([kp_ref[hh], kc_ref[hh]], axis=0)
            vb = jnp.concatenate([vp_ref[hh], vc_ref[hh]], axis=0)
            probs, psink = _attn_probs(q, kb, s_ref, band_ref, ATT_HPS * h2 + hh, i)
            dp = _dot(do, vb, 1, 1)
            delta = jnp.sum(probs * dp, axis=1, keepdims=True)
            ds = (probs * (dp - delta)).astype(BF16)
            dq_ref[hh] = (_dot(ds, kb) * scale).reshape(ATT_G, WINDOW, HEAD)
            dkb = _dot(ds, q, 0, 0) * scale
            dvb = _dot(probs.astype(BF16), do, 0, 0)
            dk_ref[hh, pl.ds(cur, WINDOW), :] += dkb[WINDOW:, :]
            dv_ref[hh, pl.ds(cur, WINDOW), :] += dvb[WINDOW:, :]
            prv = pl.multiple_of(jnp.maximum(i - 1, 0) * WINDOW, WINDOW)
            dk_ref[hh, pl.ds(prv, WINDOW), :] += dkb[:WINDOW, :]
            dv_ref[hh, pl.ds(prv, WINDOW), :] += dvb[:WINDOW, :]

            dsr = -psink * delta
            upd = jnp.zeros((8, 128), F32)
            for gq in range(ATT_G):
                v = jnp.sum(dsr[gq * WINDOW:(gq + 1) * WINDOW, :], axis=0, keepdims=True)
                upd = jnp.where((lane == gq) & (row == 0), v, upd)
            dsk_ref[hh] += upd

    qspec, cur, prev, tok = _attn_specs(S)
    full = pl.BlockSpec((ATT_HPS, S, HEAD), lambda h, i: (h, 0, 0))
    return pl.pallas_call(
        body, name=name, grid=(ATT_KV // ATT_HPS, nb),
        in_specs=[pl.BlockSpec(memory_space=pltpu.SMEM), _BAND, qspec, cur, prev, cur, prev, tok],
        out_specs=[qspec, full, full, pl.BlockSpec((ATT_HPS, 8, 128), lambda h, i: (h, 0, 0))],
        out_shape=[jax.ShapeDtypeStruct((ATT_KV, ATT_G, S, HEAD), F32), jax.ShapeDtypeStruct((ATT_KV, S, HEAD), F32),
                   jax.ShapeDtypeStruct((ATT_KV, S, HEAD), F32), jax.ShapeDtypeStruct((ATT_KV, 8, 128), F32)],
        compiler_params=_cp(2),
    )(sinks, _attn_band(), qh, kh, kh, vh, vh, doh)


def _heads_major(t, nh):
    S = t.shape[0]
    return t.reshape(S, nh, HEAD).transpose(1, 0, 2)


def _tokens_major(t):
    nh, S, _ = t.shape
    return t.transpose(1, 0, 2).reshape(S, nh * HEAD)


class _NoComm:
    def late_weights(self, w, after, part):
        return w

    def advance(self, after, group=None, tensors=None):
        return None


def _local_step(x, posf, target, w, comm=None):
    S, D = x.shape
    gr = {}
    comm = comm or _NoComm()

    zx, h1 = _norm_mm(x, w["a_norm"], w["w_zx"], name="in_proj_zx", dep=w.get("dep"))
    dtr = _mm(h1, w["w_dt"], name="in_proj_dt")
    xbc = _conv_silu_fwd(zx, 2048, 4096, w["a_conv_w"], w["a_conv_b"], name="a_conv_f")
    dt_g, ac_g, sg_g, ac_t = _ssd_prep(dtr, w["a_dt_bias"], w["a_A_log"])
    y_ssd, states = _ssd_fwd(xbc, dt_g, ac_g, ac_t)
    yg = _gate_fwd(y_ssd, xbc, zx, w["a_Dexp"], w["a_gnorm"])
    x1 = _mm(yg, w["a_out_proj"], res=x, name="out_proj")

    w = comm.late_weights(w, x1, 0)
    FW = w["f_w_in"][0].shape[2]

    def ffn_fwd(xin, l, loss_target=None):
        u, h = _norm_mm(xin, w["f_norm"][l], w["f_w_in"][l], name=f"f_in{l}", N=N_CHIPS * FW, tn=FW,
                        b_spec=pl.BlockSpec((None, D, FW), lambda i, j: (j, 0, 0)))
        a = _ffn_act_fwd(u, w["f_conv_w"][l], w["f_conv_b"][l], name=f"f_act_f{l}")
        xo = _mm(a, w["f_w_down"][l], res=xin, tk=a.shape[1], name=f"f_down{l}", target=loss_target)
        return xo, (h, u)

    x2, ffn0 = ffn_fwd(x1, 0)
    w = comm.late_weights(w, x2, 1)

    kv, hk = _norm_mm(x2, w["kv_norm"], w["w_kv"], bias=w["b_kv"], name="kv_proj")
    q, hq = _norm_mm(x2, w["b_norm"], w["w_q"], bias=w["b_q"], name="q_proj")
    rope = _rope_cs(posf)
    kr = _hnrope_fwd(kv, 0, 256, w["k_norm_w"], rope, name="k_rope_f")
    qr = _hnrope_fwd(q, 0, 1024, w["q_norm_w"], rope, name="q_rope_f")
    qh = qr.reshape(ATT_KV, ATT_G, S, HEAD)
    kh = kr
    vh = _heads_major(kv[:, 256:].astype(BF16), ATT_KV)
    att = _attn_fwd(qh, kh, vh, w["sinks"])
    x3 = _mm(att, w["w_o"], bias=w["b_o"], res=x2, name="o_proj")
    (dy, loss_part), ffn1 = ffn_fwd(x3, 1, target)

    def ffn_bwd(xin, l, saved, dyo, want_colsum, dep=None):
        h, u = saved
        da = _mm(dyo, w["f_w_down"][l], tb=True, name=f"f_down_dx{l}", dep=dep)
        du, dcw, dcb, a = _ffn_act_bwd(u, w["f_conv_w"][l], w["f_conv_b"][l], da, name=f"f_act_b{l}")
        dw_down = _mm(a, dyo, ta=True, out_dtype=BF16, name=f"f_down_dw{l}")
        dw_in = _mm(h, du, ta=True, out_dtype=BF16, name=f"f_in_dw{l}", dims=(D, N_CHIPS * FW, S), tm=D, tn=FW, tk=S,
                    b_spec=pl.BlockSpec((None, S, FW), lambda i, j, k: (j // 2, 0, j % 2)),
                    o_spec=pl.BlockSpec((None, D, FW), lambda i, j, k: (j, i, 0)), o_shape=(N_CHIPS, D, FW))
        ts = _pick(S, (1024, 512, 256))
        pieces = [(pl.BlockSpec((None, ts, FW), lambda i, j, k, q=q: (q // 2, i, q % 2)),
                   pl.BlockSpec((None, 512, FW), lambda i, j, k, q=q: (q, j, 0))) for q in range(N_CHIPS)]
        dh = _mm(du, w["f_w_in"][l], tb=True, name=f"f_in_dx{l}", dims=(S, D, FW), tm=ts, tn=512, tk=FW,
                 a_spec=pieces[0][0], b_spec=pieces[0][1], more=pieces[1:])
        outs = _rms_bwd(xin, [w["f_norm"][l]], [dh], dyo, name=f"f_norm_b{l}", want_colsum=want_colsum)
        g = dict(f_norm=outs[1], f_w_in=dw_in, f_conv_w=dcw, f_conv_b=dcb, f_w_down=dw_down)
        return outs[0], g, (outs[2] if want_colsum else None)

    dx3, gr["ffn1"], db_o = ffn_bwd(x3, 1, ffn1, dy, True)
    gr["b_o"] = db_o
    gr["w_o"] = _mm(att, dx3, ta=True, out_dtype=BF16, name="o_proj_dw")
    datt = _mm(dx3, w["w_o"], tb=True, out_dtype=BF16, name="o_proj_dx")
    dqh, dkh, dvh, dsk = _attn_bwd(qh, kh, vh, w["sinks"], datt)
    gr["sinks"] = dsk[:, 0, :4].reshape(1, 16)
    dv = _tokens_major(dvh).astype(BF16)
    dq, db_q, dqn = _hnrope_bwd(q, 0, 1024, w["q_norm_w"], rope, dqh.reshape(16, S, HEAD), name="q_rope_b")
    dk, db_k, dkn = _hnrope_bwd(kv, 0, 256, w["k_norm_w"], rope, dkh, name="k_rope_b")
    gr["q_norm"], gr["k_norm"] = dqn[:, :HEAD], dkn[:, :HEAD]
    gr["b_q"] = db_q
    gr["b_kv"] = jnp.concatenate([db_k, _colsum(dv, name="dv_colsum")], axis=1)
    dkv = jnp.concatenate([dk, dv], axis=1)
    gr["w_q"] = _mm(hq, dq, ta=True, out_dtype=BF16, name="q_proj_dw")
    gr["w_kv"] = _mm(hk, dkv, ta=True, out_dtype=BF16, name="kv_proj_dw")
    tok = comm.advance([gr["w_kv"]], 1, dict(f_down1=gr["ffn1"]["f_w_down"], f_in1=gr["ffn1"]["f_w_in"], w_o=gr["w_o"],
                                             w_q=gr["w_q"], w_kv=gr["w_kv"]))
    dhq = _mm(dq, w["w_q"], tb=True, name="q_proj_dx", dep=tok)
    dhk = _mm(dkv, w["w_kv"], tb=True, name="kv_proj_dx")
    dx2, gr["kv_norm"], gr["b_norm"] = _rms_bwd(x2, [w["kv_norm"], w["b_norm"]], [dhk, dhq], dx3, name="kvq_norm_b")

    dx1, gr["ffn0"], _ = ffn_bwd(x1, 0, ffn0, dx2, False, dep=comm.advance([dx2]))

    gr["a_out_proj"] = _mm(yg, dx1, ta=True, out_dtype=BF16, name="out_proj_dw")
    tok = comm.advance([dx1, gr["a_out_proj"]], 2,
                       dict(f_down0=gr["ffn0"]["f_w_down"], f_in0=gr["ffn0"]["f_w_in"], out_proj=gr["a_out_proj"]))
    dyg = _mm(dx1, w["a_out_proj"], tb=True, name="out_proj_dx", dep=tok)
    dy_ssd, dz, gr["a_gnorm"], dD = _gate_bwd(y_ssd, xbc, zx, w["a_Dexp"], w["a_gnorm"], dyg)
    gr["a_D"] = dD[:, :SSM_HEADS]
    dxs, dB, dC, dhead = _ssd_bwd(xbc, dt_g, ac_g, ac_t, states, dy_ssd, w["a_Dexp"], dep=comm.advance([dy_ssd]))
    ddtr, dsmall = _ssd_post(dhead, dt_g, sg_g, w["a_A_log_g"])
    gr["a_A_log"] = dsmall[:, 0, :4].reshape(1, SSM_HEADS)
    gr["a_dt_bias"] = dsmall[:, 1, :4].reshape(1, SSM_HEADS)
    dxbc, gr["a_conv_w"], gr["a_conv_b"] = _conv_silu_bwd(
        zx, 2048, 4096, w["a_conv_w"], w["a_conv_b"], [(dxs, 0), (dB, 2048), (dC, 3072)], name="a_conv_b")
    gr["w_z"] = _mm(h1, dz, ta=True, out_dtype=BF16, name="in_proj_dwz")
    gr["w_x"] = _mm(h1, dxbc, ta=True, out_dtype=BF16, name="in_proj_dwx")
    gr["w_dt"] = _mm(h1, ddtr, ta=True, out_dtype=BF16, name="in_proj_dwdt")
    ts = _pick(S, (1024, 512, 256))
    wblk = lambda q: pl.BlockSpec((512, 2048), lambda i, j, k: (j, q))
    dh1 = _mm(dz, w["w_zx"], tb=True, name="in_proj_dx", dims=(S, D, 2048), tm=ts, tn=512, tk=2048,
              a_spec=pl.BlockSpec((ts, 2048), lambda i, j, k: (i, 0)), b_spec=wblk(0),
              more=[(dxbc, pl.BlockSpec((ts, 2048), lambda i, j, k: (i, 0)), w["w_zx"], wblk(1)),
                    (dxbc, pl.BlockSpec((ts, 2048), lambda i, j, k: (i, 1)), w["w_zx"], wblk(2)),
                    (ddtr, pl.BlockSpec((ts, 128), lambda i, j, k: (i, 0)), w["w_dt"], pl.BlockSpec((512, 128), lambda i, j, k: (j, 0)))])
    dx0, gr["a_norm"] = _rms_bwd(x, [w["a_norm"]], [dh1], dx1, name="a_norm_b")
    tok = comm.advance([dx0], 3, dict(in_proj=_in_proj_grad(gr).reshape(D, N_CHIPS, -1).transpose(1, 0, 2)))
    return loss_part, dx0, gr, tok


def _prep_small(full, w):
    w["a_norm"] = full["a_norm"]
    w["a_conv_w"] = full["a_conv_w"][0]
    w["a_conv_b"] = full["a_conv_b"]
    pad32 = lambda v: jnp.pad(v, ((0, 0), (0, 128 - SSM_HEADS)))
    w["a_dt_bias"] = pad32(full["a_dt_bias"])
    w["a_A_log"] = pad32(full["a_A_log"])
    w["a_A_log_g"] = jnp.pad(full["a_A_log"].reshape(SSM_GROUPS, 1, 4), ((0, 0), (0, 0), (0, 124)))
    w["a_Dexp"] = jnp.repeat(full["a_D"], HEAD, axis=1)
    w["a_gnorm"] = full["a_gnorm"]
    w["f_norm"] = [full["f_norm"][l:l + 1] for l in range(2)]
    w["f_conv_w"] = [full["f_conv_w"][l] for l in range(2)]
    w["f_conv_b"] = [full["f_conv_b"][l:l + 1] for l in range(2)]
    w["kv_norm"] = full["kv_norm"].reshape(1, -1)
    w["b_kv"] = full["b_kv"].reshape(1, -1)
    w["k_norm_w"] = jnp.tile(full["k_norm"].reshape(1, HEAD), (1, ATT_KV))
    w["b_norm"] = full["b_norm"]
    w["b_q"] = full["b_q"]
    w["q_norm_w"] = jnp.tile(full["q_norm"], (1, ATT_KV * ATT_G))
    w["sinks"] = full["sinks"].reshape(-1)
    w["b_o"] = full["b_o"]
    return w


def _split_in_proj(ip):
    return ip[:, :6144].astype(BF16), jnp.pad(ip[:, 6144:], ((0, 0), (0, 128 - SSM_HEADS))).astype(BF16)


def _join_in_proj(blocks, *, name="in_proj_join", tr=256):
    _, R, cw = blocks.shape
    zx_cols = 3 * 2048
    rest = N_CHIPS * cw - zx_cols

    def body(b_ref, zx_ref, dt_ref):
        whole = jnp.concatenate([b_ref[j] for j in range(N_CHIPS)], axis=1)
        zx_ref[...] = whole[:, :zx_cols]
        dt_ref[...] = jnp.concatenate([whole[:, zx_cols:], jnp.zeros((tr, 128 - rest), BF16)], axis=1)

    return pl.pallas_call(
        body, name=name, grid=(R // tr,), in_specs=[pl.BlockSpec((N_CHIPS, tr, cw), lambda i: (0, i, 0))],
        out_specs=[pl.BlockSpec((tr, zx_cols), lambda i: (i, 0)), pl.BlockSpec((tr, 128), lambda i: (i, 0))],
        out_shape=[jax.ShapeDtypeStruct((R, zx_cols), BF16), jax.ShapeDtypeStruct((R, 128), BF16)],
        compiler_params=_cp(1),
    )(blocks)


def _prep_weights(full):
    w = _prep_small(full, {})
    w["w_zx"], w["w_dt"] = _split_in_proj(full["a_in_proj"][0])
    w["a_out_proj"] = full["a_out_proj"][0].astype(BF16)
    w["f_w_in"] = [full["f_w_in"][l].reshape(1024, N_CHIPS, -1).transpose(1, 0, 2).astype(BF16) for l in range(2)]
    w["f_w_down"] = [full["f_w_down"][l].astype(BF16) for l in range(2)]
    w["w_kv"] = full["w_kv"].astype(BF16)
    w["w_q"] = full["w_q"][0].astype(BF16)
    w["w_o"] = full["w_o"][0].astype(BF16)
    return w


def _small_grads(gr):
    g = {}
    g["a_norm"] = gr["a_norm"]
    g["a_conv_w"] = gr["a_conv_w"][None]
    g["a_conv_b"] = gr["a_conv_b"]
    g["a_dt_bias"], g["a_A_log"], g["a_D"] = gr["a_dt_bias"], gr["a_A_log"], gr["a_D"]
    g["a_gnorm"] = gr["a_gnorm"]
    g["kv_norm"] = gr["kv_norm"].reshape(-1)
    g["b_kv"] = gr["b_kv"].reshape(-1)
    g["k_norm"] = gr["k_norm"].reshape(-1)
    g["b_norm"] = gr["b_norm"]
    g["b_q"] = gr["b_q"]
    g["q_norm"] = gr["q_norm"]
    g["sinks"] = gr["sinks"]
    g["b_o"] = gr["b_o"]
    f = [gr["ffn0"], gr["ffn1"]]
    g["f_norm"] = jnp.concatenate([f[0]["f_norm"], f[1]["f_norm"]], axis=0)
    g["f_conv_w"] = jnp.stack([f[l]["f_conv_w"] for l in range(2)])
    g["f_conv_b"] = jnp.concatenate([f[l]["f_conv_b"] for l in range(2)], axis=0)
    return g


def _in_proj_grad(gr):
    return jnp.concatenate([gr["w_z"], gr["w_x"], gr["w_dt"][:, :SSM_HEADS]], axis=1)


def _full_grads(gr):
    g = _small_grads(gr)
    f32 = lambda t: t.astype(F32)
    g["a_in_proj"] = f32(_in_proj_grad(gr))[None]
    g["a_out_proj"] = f32(gr["a_out_proj"])[None]
    g["w_kv"] = f32(gr["w_kv"])
    g["w_q"] = f32(gr["w_q"])[None]
    g["w_o"] = f32(gr["w_o"])[None]
    f = [gr["ffn0"], gr["ffn1"]]
    g["f_w_in"] = jnp.stack([f32(f[l]["f_w_in"]).transpose(1, 0, 2).reshape(1024, -1) for l in range(2)])
    g["f_w_down"] = jnp.stack([f32(f[l]["f_w_down"]) for l in range(2)])
    return g


MESH = pl.DeviceIdType.MESH
WEIGHTS = ("a_norm", "a_in_proj", "a_conv_w", "a_conv_b", "a_dt_bias", "a_A_log", "a_D", "a_gnorm", "a_out_proj",
           "kv_norm", "w_kv", "b_kv", "k_norm", "b_norm", "w_q", "b_q", "q_norm", "sinks", "w_o", "b_o", "f_norm",
           "f_w_in", "f_conv_w", "f_conv_b", "f_w_down")
MATS = (("in_proj", "a_in_proj", 0), ("out_proj", "a_out_proj", 0), ("w_kv", "w_kv", None), ("w_q", "w_q", 0),
        ("w_o", "w_o", 0), ("f_in0", "f_w_in", 0), ("f_in1", "f_w_in", 1), ("f_down0", "f_w_down", 0),
        ("f_down1", "f_w_down", 1))
SMALL_CUT = (("a_norm", 1), ("a_conv_w", 2), ("a_conv_b", 1), ("a_gnorm", 1), ("f_conv_w", 2))
SMALL_REP = ("a_dt_bias", "a_A_log", "a_D", "kv_norm", "b_kv", "k_norm", "b_norm", "b_q", "q_norm", "sinks", "b_o",
             "f_norm", "f_conv_b")


def _coords():
    return lax.axis_index("x"), lax.axis_index("y"), lax.axis_index("c")


def _other_chips(x, y):
    return [(1 - x, y), (x, 1 - y), (1 - x, 1 - y)]


def _pack(arrs, rows_align, lanes, dtype):
    flat = jnp.concatenate([a.reshape(-1).astype(dtype) for a in arrs])
    per = rows_align * lanes
    total = -(-flat.shape[0] // per) * per
    return jnp.pad(flat, (0, total - flat.shape[0])).reshape(total // lanes, lanes)


def _unpack(flat, shapes):
    out, off = [], 0
    for s in shapes:
        n = math.prod(s)
        out.append(flat[off:off + n].reshape(s))
        off += n
    return out


def _remote(src, dst, send, recv, k, dev):
    return pltpu.make_async_remote_copy(src_ref=src, dst_ref=dst, send_sem=send.at[k], recv_sem=recv.at[k],
                                        device_id=dev, device_id_type=MESH)


_ANY = pl.BlockSpec(memory_space=pl.ANY)


def _halves(t):
    r, c = t.shape
    return t.reshape(2, r // 2, c)


def _gather_weights(shards, sp):
    n = len(shards)
    per = 9
    n_sem = per * n + 3

    def body(*refs):
        sh, sp_ref = refs[:n], refs[n]
        outs, sout = refs[n + 1:2 * n + 1], refs[2 * n + 1]
        send, recv, loc = refs[2 * n + 2:]
        x, y, c = _coords()
        me = 2 * x + y
        cx_, cy_, cd_ = _other_chips(x, y)
        ix, iy, idg = (2 * p[0] + p[1] for p in (cx_, cy_, cd_))
        to_x, to_y, sib = (*cx_, c), (*cy_, c), (x, y, 1 - c)
        l1 = pltpu.make_async_copy(sp_ref, sout.at[me], loc.at[0])
        l1.start()
        sends = [_remote(sp_ref, sout.at[me], send, recv, per * n + j, (*p, c)) for j, p in enumerate((cx_, cy_, cd_))]
        for t in range(n):
            sends.append(_remote(sh[t].at[c], outs[t].at[me, c], send, recv, per * t + 0, to_x))
            sends.append(_remote(sh[t].at[c], outs[t].at[me, c], send, recv, per * t + 1, to_y))
            sends.append(_remote(sh[t], outs[t].at[me], send, recv, per * t + 8, sib))
        for cp in sends:
            cp.start()

        def go(src, dst, k, dev):
            cp = _remote(src, dst, send, recv, k, dev)
            cp.start()
            sends.append(cp)

        def piece(t, owner, first):
            q = sh[t].shape[1] // 2
            return outs[t].at[owner, c, pl.ds(0 if first else q, q)]

        for t in range(n):
            _remote(sh[t].at[c], outs[t].at[ix, c], send, recv, per * t + 0, to_x).wait_recv()
            go(piece(t, ix, False), piece(t, ix, False), per * t + 3, to_y)
            go(outs[t].at[ix, c], outs[t].at[ix, c], per * t + 4, sib)
        for t in range(n):
            _remote(sh[t].at[c], outs[t].at[iy, c], send, recv, per * t + 1, to_y).wait_recv()
            go(piece(t, iy, True), piece(t, iy, True), per * t + 2, to_x)
            go(outs[t].at[iy, c], outs[t].at[iy, c], per * t + 5, sib)
        for t in range(n):
            _remote(piece(t, idg, True), piece(t, idg, True), send, recv, per * t + 2, to_x).wait_recv()
            go(piece(t, idg, True), piece(t, idg, True), per * t + 6, sib)
            _remote(piece(t, idg, False), piece(t, idg, False), send, recv, per * t + 3, to_y).wait_recv()
            go(piece(t, idg, False), piece(t, idg, False), per * t + 7, sib)
        for j, p in enumerate((cx_, cy_, cd_)):
            _remote(sp_ref, sout.at[2 * p[0] + p[1]], send, recv, per * n + j, (*p, c)).wait_recv()
        for t in range(n):
            q = sh[t].shape[1] // 2
            other = lambda owner, lo=None: outs[t].at[owner, 1 - c] if lo is None else outs[t].at[owner, 1 - c, pl.ds(lo, q)]
            _remote(other(ix), other(ix), send, recv, per * t + 4, sib).wait_recv()
            _remote(other(iy), other(iy), send, recv, per * t + 5, sib).wait_recv()
            _remote(other(idg, 0), other(idg, 0), send, recv, per * t + 6, sib).wait_recv()
            _remote(other(idg, q), other(idg, q), send, recv, per * t + 7, sib).wait_recv()
            _remote(sh[t], outs[t].at[me], send, recv, per * t + 8, sib).wait_recv()
        for cp in sends:
            cp.wait_send()
        l1.wait()

    res = pl.pallas_call(
        body, name="gather_weights", in_specs=[_ANY] * (n + 1), out_specs=[_ANY] * (n + 1),
        out_shape=[jax.ShapeDtypeStruct((N_CHIPS,) + t.shape, t.dtype) for t in shards]
        + [jax.ShapeDtypeStruct((N_CHIPS,) + sp.shape, sp.dtype)],
        scratch_shapes=[pltpu.SemaphoreType.DMA((n_sem,)), pltpu.SemaphoreType.DMA((n_sem,)),
                        pltpu.SemaphoreType.DMA((1,))],
    )(*shards, sp)
    return res[:n], res[n]


_HBM = pl.BlockSpec(memory_space=pltpu.HBM)
_SEMS = pl.BlockSpec(memory_space=pltpu.SEMAPHORE)
_DATAFLOW = pltpu.SideEffectType.DATAFLOW_SIDE_EFFECTING


def _in_hbm(a):
    return pltpu.with_memory_space_constraint(a, pltpu.HBM)


def _start_copies(copies, arrays, n_sem, after, *, name):
    n, na = len(arrays), len(after)

    def body(*refs):
        for mine, _ in copies(refs[:n], refs[n + na], refs[n + na + 1]):
            mine.start()
        refs[-1][...] = jnp.zeros_like(refs[-1])

    res = pl.pallas_call(
        body, name=name, in_specs=[_HBM] * n + [_ANY] * na,
        out_specs=[_SEMS, _SEMS] + [_HBM] * n + [pl.BlockSpec(memory_space=pltpu.VMEM)],
        out_shape=[pltpu.SemaphoreType.DMA((n_sem,)), pltpu.SemaphoreType.DMA((n_sem,))]
        + [pltpu.HBM(a.shape, a.dtype) for a in arrays] + [jax.ShapeDtypeStruct((8, 128), F32)],
        input_output_aliases={t: 2 + t for t in range(n)},
        compiler_params=pltpu.CompilerParams(has_side_effects=_DATAFLOW),
    )(*[_in_hbm(a) for a in arrays], *after)
    return res[0], res[1], list(res[2:2 + n]), res[-1]


def _wait_copies(copies, send, recv, arrays, after, *, name):
    n = len(arrays)

    def body(*refs):
        for mine, theirs in copies(refs[:n], refs[n], refs[n + 1]):
            mine.wait_send()
            theirs.wait_recv()

    return list(pl.pallas_call(
        body, name=name, in_specs=[_HBM] * n + [_SEMS, _SEMS] + [_ANY] * len(after), out_specs=[_HBM] * n,
        out_shape=[pltpu.HBM(a.shape, a.dtype) for a in arrays], input_output_aliases={t: t for t in range(n)},
        compiler_params=pltpu.CompilerParams(has_side_effects=_DATAFLOW),
    )(*arrays, send, recv, *after))


def _sibling_copies(refs, send, recv):
    n = len(refs) // 2
    x, y, c = _coords()
    cps = [_remote(refs[t].at[:, 1 - c], refs[n + t], send, recv, t, (x, y, 1 - c)) for t in range(n)]
    return [(cp, cp) for cp in cps]


def _join_copies(refs, send, recv):
    x, y, c = _coords()
    sib = (x, y, 1 - c)
    return [(_remote(o.at[c], o.at[c], send, recv, t, sib), _remote(o.at[1 - c], o.at[1 - c], send, recv, t, sib))
            for t, o in enumerate(refs)]


def _gather_copies(sh, land, send, recv):
    x, y, c = _coords()
    me = 2 * x + y
    out = []
    for t in range(len(sh)):
        for j, (cx, cy) in enumerate(_other_chips(x, y)):
            dev = (cx, cy, c)
            out.append((_remote(sh[t].at[c], land[t].at[me, c], send, recv, 4 * t + j, dev),
                        _remote(sh[t].at[c], land[t].at[2 * cx + cy, c], send, recv, 4 * t + j, dev)))
        sib = (x, y, 1 - c)
        out.append((_remote(sh[t], land[t].at[me], send, recv, 4 * t + 3, sib),
                    _remote(sh[t], land[t].at[me], send, recv, 4 * t + 3, sib)))
    return out


def _gather_start(shards, after, *, name):
    n = len(shards)

    def body(*refs):
        sh, land = refs[:n], refs[n:2 * n]
        send, recv = refs[2 * n + 1], refs[2 * n + 2]
        token = refs[-1]
        for mine, _ in _gather_copies(sh, land, send, recv):
            mine.start()
        token[...] = jnp.zeros_like(token)

    lands = [_in_hbm(lax.empty((N_CHIPS,) + s.shape, s.dtype)) for s in shards]
    res = pl.pallas_call(
        body, name=name, in_specs=[_HBM] * (2 * n) + [_ANY],
        out_specs=[_SEMS, _SEMS] + [_HBM] * (2 * n) + [pl.BlockSpec(memory_space=pltpu.VMEM)],
        out_shape=[pltpu.SemaphoreType.DMA((4 * n,)), pltpu.SemaphoreType.DMA((4 * n,))]
        + [pltpu.HBM(s.shape, s.dtype) for s in shards] + [pltpu.HBM(l.shape, l.dtype) for l in lands]
        + [jax.ShapeDtypeStruct((8, 128), F32)],
        input_output_aliases={t: 2 + t for t in range(2 * n)},
        compiler_params=pltpu.CompilerParams(has_side_effects=_DATAFLOW),
    )(*[_in_hbm(s) for s in shards], *lands, after)
    return res[0], res[1], res[2:2 + n], res[2 + n:2 + 2 * n], res[-1]


def _gather_wait(send, recv, shards, lands, after, *, name):
    n = len(shards)

    def body(*refs):
        sh, land = refs[:n], refs[n:2 * n]
        send_r, recv_r = refs[2 * n], refs[2 * n + 1]
        for mine, theirs in _gather_copies(sh, land, send_r, recv_r):
            mine.wait_send()
            theirs.wait_recv()

    res = pl.pallas_call(
        body, name=name, in_specs=[_HBM] * (2 * n) + [_SEMS, _SEMS, _ANY], out_specs=[_HBM] * (2 * n),
        out_shape=[pltpu.HBM(s.shape, s.dtype) for s in shards] + [pltpu.HBM(l.shape, l.dtype) for l in lands],
        input_output_aliases={t: t for t in range(2 * n)},
        compiler_params=pltpu.CompilerParams(has_side_effects=_DATAFLOW),
    )(*shards, *lands, send, recv, after)
    return res[n:]


def _gather_forward(lands, *, name):
    n = len(lands)

    def body(*refs):
        o = refs[n:2 * n]
        send, recv = refs[2 * n:]
        x, y, c = _coords()
        sib = (x, y, 1 - c)
        srcs = [2 * cx + cy for cx, cy in _other_chips(x, y)]
        cps = [_remote(o[t].at[s, c], o[t].at[s, c], send, recv, 3 * t + j, sib) for t in range(n) for j, s in enumerate(srcs)]
        for cp in cps:
            cp.start()
        for t in range(n):
            for j, s in enumerate(srcs):
                _remote(o[t].at[s, 1 - c], o[t].at[s, 1 - c], send, recv, 3 * t + j, sib).wait_recv()
        for cp in cps:
            cp.wait_send()

    return pl.pallas_call(
        body, name=name, in_specs=[_ANY] * n, out_specs=[_ANY] * n, input_output_aliases={t: t for t in range(n)},
        out_shape=[jax.ShapeDtypeStruct(l.shape, l.dtype) for l in lands],
        scratch_shapes=[pltpu.SemaphoreType.DMA((3 * n,)), pltpu.SemaphoreType.DMA((3 * n,))],
    )(*lands)


def _small_copies(v, land, send, recv):
    x, y, c = _coords()
    me = 4 * x + 2 * y + c
    out = []
    for k in range(1, 8):
        px = 1 - x if k & 4 else x
        py = 1 - y if k & 2 else y
        pc = 1 - c if k & 1 else c
        out.append((_remote(v, land.at[me], send, recv, k - 1, (px, py, pc)),
                    _remote(v, land.at[4 * px + 2 * py + pc], send, recv, k - 1, (px, py, pc))))
    return out


def _small_start(v, after, *, name):
    def body(v_ref, land_ref, after_ref, send, recv, v_thru, land_thru, token):
        for mine, _ in _small_copies(v_ref, land_ref, send, recv):
            mine.start()
        token[...] = jnp.zeros_like(token)

    land = _in_hbm(lax.empty((8,) + v.shape, v.dtype))
    return pl.pallas_call(
        body, name=name, in_specs=[_HBM, _HBM, _ANY],
        out_specs=[_SEMS, _SEMS, _HBM, _HBM, pl.BlockSpec(memory_space=pltpu.VMEM)],
        out_shape=[pltpu.SemaphoreType.DMA((7,)), pltpu.SemaphoreType.DMA((7,)), pltpu.HBM(v.shape, v.dtype),
                   pltpu.HBM(land.shape, land.dtype), jax.ShapeDtypeStruct((8, 128), F32)],
        input_output_aliases={0: 2, 1: 3}, compiler_params=pltpu.CompilerParams(has_side_effects=_DATAFLOW),
    )(_in_hbm(v), land, after)


def _small_wait(send, recv, v, land, after, *, name):
    def body(v_ref, land_ref, send_r, recv_r, *rest):
        for mine, theirs in _small_copies(v_ref, land_ref, send_r, recv_r):
            mine.wait_send()
            theirs.wait_recv()

    return pl.pallas_call(
        body, name=name, in_specs=[_HBM, _HBM, _SEMS, _SEMS] + [_ANY] * len(after), out_specs=[_HBM, _HBM],
        out_shape=[pltpu.HBM(v.shape, v.dtype), pltpu.HBM(land.shape, land.dtype)],
        input_output_aliases={0: 0, 1: 1}, compiler_params=pltpu.CompilerParams(has_side_effects=_DATAFLOW),
    )(v, land, send, recv, *after)


def _small_sum(v, land, me_idx, *, name="small_sum"):
    def body(me_ref, v_ref, land_ref, o_ref):
        acc = None
        for s in range(8):
            term = jnp.where(me_ref[0] == s, v_ref[...], land_ref[s])
            acc = term if acc is None else acc + term
        o_ref[...] = acc

    whole = lambda shape: pl.BlockSpec(shape, lambda i, me_ref: (0,) * len(shape))
    return pl.pallas_call(
        body, name=name,
        grid_spec=pltpu.PrefetchScalarGridSpec(num_scalar_prefetch=1, grid=(1,), in_specs=[whole(v.shape), whole(land.shape)],
                                               out_specs=whole(v.shape)),
        out_shape=jax.ShapeDtypeStruct(v.shape, F32), compiler_params=_cp(1),
    )(me_idx, v, land)


RS_ROW_SPLIT = 2


def _rs_add_pair(gs, as_, c_idx, *, name):
    n = len(gs)

    def body(c_ref, *refs):
        for t in range(n):
            refs[2 * n + t][...] = (refs[t][...].astype(F32) + refs[n + t][...].astype(F32)).astype(BF16)

    def gspec(g):
        _, _, rh, cols = g.shape
        return pl.BlockSpec((None, None, rh // RS_ROW_SPLIT, cols), lambda j, i, c_ref: (j, c_ref[0], i, 0))

    def pspec(g):
        _, _, rh, cols = g.shape
        return pl.BlockSpec((None, rh // RS_ROW_SPLIT, cols), lambda j, i, c_ref: (j, i, 0))

    return pl.pallas_call(
        body, name=name,
        grid_spec=pltpu.PrefetchScalarGridSpec(
            num_scalar_prefetch=1, grid=(N_CHIPS, RS_ROW_SPLIT),
            in_specs=[gspec(g) for g in gs] + [pspec(g) for g in gs], out_specs=[pspec(g) for g in gs]),
        out_shape=[jax.ShapeDtypeStruct((N_CHIPS,) + g.shape[2:], BF16) for g in gs], compiler_params=_cp(2),
    )(c_idx, *gs, *as_)


def _chips_copies(p, r, send, recv):
    x, y, c = _coords()
    return [_remote(p[t].at[2 * cx + cy], r[t].at[k], send, recv, 3 * t + k, (cx, cy, c))
            for k, (cx, cy) in enumerate(_other_chips(x, y)) for t in range(len(p))]


def _rs_chips_start(ps, after, *, name):
    n, na = len(ps), len(after)

    def body(*refs):
        p, r = refs[:n], refs[n:2 * n]
        send, recv = refs[2 * n + na], refs[2 * n + na + 1]
        token = refs[-1]
        for cp in _chips_copies(p, r, send, recv):
            cp.start()
        token[...] = jnp.zeros_like(token)

    lands = [_in_hbm(lax.empty((3,) + p.shape[1:], p.dtype)) for p in ps]
    res = pl.pallas_call(
        body, name=name, in_specs=[_HBM] * (2 * n) + [_ANY] * na,
        out_specs=[_SEMS, _SEMS] + [_HBM] * (2 * n) + [pl.BlockSpec(memory_space=pltpu.VMEM)],
        out_shape=[pltpu.SemaphoreType.DMA((3 * n,)), pltpu.SemaphoreType.DMA((3 * n,))]
        + [pltpu.HBM(p.shape, p.dtype) for p in ps] + [pltpu.HBM(l.shape, l.dtype) for l in lands]
        + [jax.ShapeDtypeStruct((8, 128), F32)],
        input_output_aliases={t: 2 + t for t in range(2 * n)},
        compiler_params=pltpu.CompilerParams(has_side_effects=_DATAFLOW),
    )(*[_in_hbm(p) for p in ps], *lands, *after)
    return res[0], res[1], res[2:2 + n], res[2 + n:2 + 2 * n], res[-1]


def _rs_chips_wait(send, recv, ps, lands, after, *, name):
    n = len(ps)

    def body(*refs):
        p, r = refs[:n], refs[n:2 * n]
        for cp in _chips_copies(p, r, refs[2 * n], refs[2 * n + 1]):
            cp.wait_send()
            cp.wait_recv()

    res = pl.pallas_call(
        body, name=name, in_specs=[_HBM] * (2 * n) + [_SEMS, _SEMS] + [_ANY] * len(after), out_specs=[_HBM] * (2 * n),
        out_shape=[pltpu.HBM(p.shape, p.dtype) for p in ps] + [pltpu.HBM(l.shape, l.dtype) for l in lands],
        input_output_aliases={t: t for t in range(2 * n)},
        compiler_params=pltpu.CompilerParams(has_side_effects=_DATAFLOW),
    )(*ps, *lands, send, recv, *after)
    return res[:n], res[n:]


def _rs_add_chips(ps, rs, idx, *, name):
    n = len(ps)

    def body(idx_ref, *refs):
        for t in range(n):
            p_ref, r0, r1, r2 = refs[4 * t:4 * t + 4]
            refs[4 * n + t][...] = ((p_ref[...].astype(F32) + r0[...].astype(F32)) + r1[...].astype(F32)) + r2[...].astype(F32)

    in_specs, args = [], []
    for p, r in zip(ps, rs):
        _, rh, cols = p.shape
        blk = (None, rh // RS_ROW_SPLIT, cols)
        in_specs.append(pl.BlockSpec(blk, lambda i, idx_ref: (idx_ref[0], i, 0)))
        in_specs += [pl.BlockSpec(blk, lambda i, idx_ref, k=k: (k, i, 0)) for k in range(3)]
        args += [p, r, r, r]
    out_specs = [pl.BlockSpec((None, p.shape[1] // RS_ROW_SPLIT, p.shape[2]), lambda i, idx_ref: (idx_ref[1], i, 0))
                 for p in ps]
    return pl.pallas_call(
        body, name=name,
        grid_spec=pltpu.PrefetchScalarGridSpec(num_scalar_prefetch=1, grid=(RS_ROW_SPLIT,), in_specs=in_specs,
                                               out_specs=out_specs),
        out_shape=[jax.ShapeDtypeStruct((2,) + p.shape[1:], F32) for p in ps], compiler_params=_cp(1),
    )(idx, *args)


def _adamw(w, gs, m, v, *, name, dep=None):
    L, Rr, C = w.shape
    tr, tc = _pick(Rr, (256, 128, 64)), C
    if tr == Rr and Rr * C > 512 * 1024:
        tc = 256
    bc1 = 1.0 - ADAM_B1 ** ADAM_STEP
    bc2 = 1.0 - ADAM_B2 ** ADAM_STEP
    nd = 0 if dep is None else 1

    def body(*refs):
        w_ref, m_ref, v_ref = refs[0], refs[1], refs[2]
        g_refs = refs[3:3 + L]
        d_ref, mo_ref, vo_ref, go_ref = refs[3 + L + nd:]
        layer = pl.program_id(0)
        gv = g_refs[0][...]
        for q in range(1, L):
            gv = jnp.where(layer == q, g_refs[q][...], gv)
        mn = ADAM_B1 * m_ref[...] + (1.0 - ADAM_B1) * gv
        vn = ADAM_B2 * v_ref[...] + (1.0 - ADAM_B2) * (gv * gv)
        go_ref[...] = gv
        mo_ref[...] = mn
        vo_ref[...] = vn
        d_ref[...] = -ADAM_LR * ((mn / bc1) / (jnp.sqrt(vn / bc2) + ADAM_EPS) + ADAM_WD * w_ref[...])

    blk = pl.BlockSpec((None, tr, tc), lambda l, i, j: (l, i, j))
    gblks = [pl.BlockSpec((tr, tc), lambda l, i, j, q=q: (jnp.where(l == q, i, 0), jnp.where(l == q, j, 0))) for q in range(L)]
    return pl.pallas_call(
        body, name=name, grid=(L, Rr // tr, C // tc), in_specs=[blk] * 3 + gblks + [_ANY] * nd, out_specs=[blk] * 4,
        out_shape=[jax.ShapeDtypeStruct((L, Rr, C), F32)] * 4, compiler_params=_cp(3),
    )(w, m, v, *gs, *([] if dep is None else [dep]))


def kernel(x, positions, a_norm, a_in_proj, a_conv_w, a_conv_b, a_dt_bias, a_A_log, a_D, a_gnorm, a_out_proj,
           kv_norm, w_kv, b_kv, k_norm, b_norm, w_q, b_q, q_norm, sinks, w_o, b_o, f_norm, f_w_in, f_conv_w,
           f_conv_b, f_w_down, loss_target, m_a_norm, m_a_in_proj, m_a_conv_w, m_a_conv_b, m_a_dt_bias, m_a_A_log,
           m_a_D, m_a_gnorm, m_a_out_proj, m_kv_norm, m_w_kv, m_b_kv, m_k_norm, m_b_norm, m_w_q, m_b_q, m_q_norm,
           m_sinks, m_w_o, m_b_o, m_f_norm, m_f_w_in, m_f_conv_w, m_f_conv_b, m_f_w_down, v_a_norm, v_a_in_proj,
           v_a_conv_w, v_a_conv_b, v_a_dt_bias, v_a_A_log, v_a_D, v_a_gnorm, v_a_out_proj, v_kv_norm, v_w_kv,
           v_b_kv, v_k_norm, v_b_norm, v_w_q, v_b_q, v_q_norm, v_sinks, v_w_o, v_b_o, v_f_norm, v_f_w_in,
           v_f_conv_w, v_f_conv_b, v_f_w_down):
    wl = dict(zip(WEIGHTS, (a_norm, a_in_proj, a_conv_w, a_conv_b, a_dt_bias, a_A_log, a_D, a_gnorm, a_out_proj,
                            kv_norm, w_kv, b_kv, k_norm, b_norm, w_q, b_q, q_norm, sinks, w_o, b_o, f_norm, f_w_in,
                            f_conv_w, f_conv_b, f_w_down)))
    ml = dict(zip(WEIGHTS, (m_a_norm, m_a_in_proj, m_a_conv_w, m_a_conv_b, m_a_dt_bias, m_a_A_log, m_a_D, m_a_gnorm,
                            m_a_out_proj, m_kv_norm, m_w_kv, m_b_kv, m_k_norm, m_b_norm, m_w_q, m_b_q, m_q_norm,
                            m_sinks, m_w_o, m_b_o, m_f_norm, m_f_w_in, m_f_conv_w, m_f_conv_b, m_f_w_down)))
    vl = dict(zip(WEIGHTS, (v_a_norm, v_a_in_proj, v_a_conv_w, v_a_conv_b, v_a_dt_bias, v_a_A_log, v_a_D, v_a_gnorm,
                            v_a_out_proj, v_kv_norm, v_w_kv, v_b_kv, v_k_norm, v_b_norm, v_w_q, v_b_q, v_q_norm,
                            v_sinks, v_w_o, v_b_o, v_f_norm, v_f_w_in, v_f_conv_w, v_f_conv_b, v_f_w_down)))
    xi, yi, ci = _coords()
    me = 2 * xi + yi
    S = x.shape[1]

    def block_of(n, layer):
        t = wl[n]
        return t if layer is None else t[layer]

    rows = lambda t: t.reshape(-1, t.shape[-1])
    c_idx = jnp.reshape(ci, (1,)).astype(jnp.int32)
    me_c = jnp.stack([me, ci]).astype(jnp.int32)
    early = ("in_proj", "out_proj")
    late = (("f_in0", "f_down0"), ("w_kv", "w_q", "w_o", "f_in1", "f_down1"))
    shards = {name: _halves(block_of(wn, layer).astype(BF16)) for name, wn, layer in MATS}

    sp = _pack([wl[n] for n, _ in SMALL_CUT], 8, 128, F32)
    gathered, gs = _gather_weights([shards[k] for k in early], sp)
    gt = {k: t.reshape(N_CHIPS, -1, t.shape[-1]) for k, t in zip(early, gathered)}
    started = [_gather_start([shards[k] for k in late[0]], gs, name="gather_late_start0")]
    started.append(_gather_start([shards[k] for k in late[1]], started[0][4], name="gather_late_start1"))
    full = {n: wl[n] for n in SMALL_REP}
    gs = gs.reshape(N_CHIPS, -1)
    pieces = [_unpack(gs[j], [wl[n].shape for n, _ in SMALL_CUT]) for j in range(N_CHIPS)]
    for q, (n, ax) in enumerate(SMALL_CUT):
        full[n] = jnp.concatenate([pieces[j][q] for j in range(N_CHIPS)], axis=ax)
    w = _prep_small(full, {})
    w["w_zx"], w["w_dt"] = _join_in_proj(gt["in_proj"])
    w["a_out_proj"] = rows(gt["out_proj"])
    w["dep"] = started[1][4]

    class Comm:
        flight = []
        reduced = {}

        def late_weights(self, w, after, part):
            send, recv, shs, lands, _ = started[part]
            lands = _gather_wait(send, recv, shs, lands, after, name=f"gather_late_wait{part}")
            lands = _gather_forward(lands, name=f"gather_late_forward{part}")
            lt = {k: t.reshape(N_CHIPS, -1, t.shape[-1]) for k, t in zip(late[part], lands)}
            w = dict(w)
            if part == 0:
                w["f_w_in"], w["f_w_down"] = [lt["f_in0"]], [rows(lt["f_down0"])]
            else:
                w["w_kv"], w["w_q"], w["w_o"] = (rows(lt[k]) for k in ("w_kv", "w_q", "w_o"))
                w["f_w_in"], w["f_w_down"] = w["f_w_in"] + [lt["f_in1"]], w["f_w_down"] + [rows(lt["f_down1"])]
            return w

        def advance(self, after, group=None, tensors=None):
            token = None
            for grp in list(self.flight):
                tag, n = grp["tag"], len(grp["names"])
                dep = list(after) + ([] if token is None else [token])
                if grp["stage"] == "sibling":
                    arrs = _wait_copies(_sibling_copies, grp["send"], grp["recv"], grp["arrays"], dep, name=f"rs_sibling_wait{tag}")
                    pairs = _rs_add_pair(arrs[:n], arrs[n:], c_idx, name=f"rs_add_pair{tag}")
                    send, recv, ps, lands, token = _rs_chips_start(pairs, dep, name=f"rs_chips_start{tag}")
                    grp.update(stage="chips", send=send, recv=recv, ps=ps, lands=lands)
                elif grp["stage"] == "chips":
                    ps, rs = _rs_chips_wait(grp["send"], grp["recv"], grp["ps"], grp["lands"], dep, name=f"rs_chips_wait{tag}")
                    halves = _rs_add_chips(ps, rs, me_c, name=f"rs_add_chips{tag}")
                    send, recv, arrs, token = _start_copies(_join_copies, halves, n, dep, name=f"rs_join_start{tag}")
                    grp.update(stage="join", send=send, recv=recv, arrays=arrs)
                else:
                    joined = _wait_copies(_join_copies, grp["send"], grp["recv"], grp["arrays"], dep, name=f"rs_join_wait{tag}")
                    self.reduced.update({k: rows(t) for k, t in zip(grp["names"], joined)})
                    self.flight.remove(grp)
            if group is not None:
                names = list(tensors)
                glist = [tensors[k].reshape(N_CHIPS, 2, -1, tensors[k].shape[-1]) for k in names]
                lands = [lax.empty((N_CHIPS,) + gq.shape[2:], gq.dtype) for gq in glist]
                dep = list(after) + ([] if token is None else [token])
                send, recv, arrs, token = _start_copies(_sibling_copies, glist + lands, len(names), dep,
                                                        name=f"rs_sibling_start{group}")
                self.flight.append(dict(tag=group, names=names, stage="sibling", send=send, recv=recv, arrays=arrs))
            return token

    comm = Comm()

    posf = positions.reshape(S, 1).astype(F32)
    loss_part, dx0, gr, tok = _local_step(x[0], posf, loss_target[0], w, comm)
    g = _small_grads(gr)

    small_names = [n for n, _ in SMALL_CUT] + list(SMALL_REP)
    sv = _pack([g[n] for n in small_names] + [loss_part[0:1, 0:1]], 8, 128, F32)
    s_send, s_recv, sv, s_land, s_token = _small_start(sv, tok, name="small_start")

    grads, delta, new_m, new_v = {}, {}, {}, {}

    def update(wn, dep):
        gl = [comm.reduced[name] for name, n2, _ in MATS if n2 == wn]
        shp = wl[wn].shape
        three = (len(gl),) + gl[0].shape
        flip = shp[-1] % 128 != 0
        view = (lambda t: t.reshape(three).transpose(0, 2, 1)) if flip else (lambda t: t.reshape(three))
        back = (lambda t: t.transpose(0, 2, 1).reshape(shp)) if flip else (lambda t: t.reshape(shp))
        if flip:
            gl = [t.T for t in gl]
        d, mn, vn, go = _adamw(view(wl[wn]), gl, view(ml[wn]), view(vl[wn]), name="adamw_" + wn, dep=dep)
        grads[wn], delta[wn], new_m[wn], new_v[wn] = back(go), back(d), back(mn), back(vn)
        return d

    first = [update(wn, s_token) for wn in ("w_q", "w_o", "w_kv")]
    tok = comm.advance(first)
    second = [update(wn, tok) for wn in ("f_w_in", "f_w_down", "a_out_proj")]
    comm.advance(second)
    comm.advance(second)
    update("a_in_proj", None)
    done = first + second

    sv, s_land = _small_wait(s_send, s_recv, sv, s_land, done, name="small_wait")
    sred = _small_sum(sv, s_land, jnp.reshape(2 * me + ci, (1,)).astype(jnp.int32)).reshape(-1)
    small_shapes = [g[n].shape for n in small_names] + [(1,)]
    sg = dict(zip(small_names + ["loss"], _unpack(sred, small_shapes)))
    loss = sg["loss"].reshape(())
    g_small = {}
    for n, ax in SMALL_CUT:
        size = wl[n].shape[ax]
        g_small[n] = lax.dynamic_slice_in_dim(sg[n], me * size, size, axis=ax)
    for n in SMALL_REP:
        g_small[n] = sg[n].reshape(wl[n].shape)

    pk = lambda d: _pack([d[n] for n in small_names], 8, 128, F32)[None]
    d, mn, vn, _ = _adamw(pk(wl), [pk(g_small)[0]], pk(ml), pk(vl), name="adamw_small")
    shapes = [wl[n].shape for n in small_names]
    for n, dd, mm, vv in zip(small_names, _unpack(d.reshape(-1), shapes), _unpack(mn.reshape(-1), shapes),
                             _unpack(vn.reshape(-1), shapes)):
        grads[n], delta[n], new_m[n], new_v[n] = g_small[n], dd, mm, vv

    return (loss, dx0[None], *[grads[n] for n in WEIGHTS], *[delta[n] for n in WEIGHTS],
            *[new_m[n] for n in WEIGHTS], *[new_v[n] for n in WEIGHTS])
```

```python
import math

import jax
import jax.numpy as jnp
from jax import lax
from jax.experimental import pallas as pl
from jax.experimental.pallas import tpu as pltpu

F32 = jnp.float32
BF16 = jnp.bfloat16

EPS = 1e-5
CHUNK = 256
WINDOW = 128
HEAD = 64
SSM_HEADS = 32
SSM_GROUPS = 8
SSM_STATE = 128
ATT_KV = 4
ATT_G = 4
ROPE_THETA = 10000.0
NEG = -1e30
N_CHIPS = 4
VMEM_LIMIT = 56 * 1024 * 1024

ADAM_LR, ADAM_B1, ADAM_B2, ADAM_EPS, ADAM_WD, ADAM_STEP = 0.001, 0.9, 0.999, 1e-08, 0.01, 10


def _cp(n_axes):
    return pltpu.CompilerParams(dimension_semantics=("arbitrary",) * n_axes, vmem_limit_bytes=VMEM_LIMIT)


def _pick(dim, prefs):
    for p in prefs:
        if dim % p == 0:
            return p
    return dim


def _iota(shape, dim):
    return lax.broadcasted_iota(jnp.int32, shape, dim)


def _dot(a, b, ca=1, cb=0):
    return lax.dot_general(a, b, (((ca,), (cb,)), ((), ())), preferred_element_type=F32)


def _dot3(x, ind):
    h = x.astype(BF16)
    r = x - h.astype(F32)
    m = r.astype(BF16)
    lo = (r - m.astype(F32)).astype(BF16)
    return _dot(h, ind) + _dot(m, ind) + _dot(lo, ind)


def _sigmoid(x):
    return jax.nn.sigmoid(x)


def _mm(a, b, *, name, ta=False, tb=False, bias=None, res=None, out_dtype=F32, b_koff=0, tm=None, tn=None, tk=None,
        dims=None, a_spec=None, b_spec=None, o_spec=None, o_shape=None, dep=None, more=(), target=None):
    if dims is not None:
        M, N, K = dims
    else:
        if ta:
            K, M = a.shape
        else:
            M, K = a.shape
        N = b.shape[0] if tb else b.shape[1]
    tm = tm or _pick(M, (1024, 1408, 512, 256, 128))
    tn = tn or _pick(N, (512, 1408, 256, 128))
    tk = tk or (K if K <= 2048 else _pick(K, (2048, 1408, 1024, 512)))
    assert M % tm == 0 and N % tn == 0 and K % tk == 0 and b_koff % tk == 0
    nk = K // tk
    kb0 = b_koff // tk
    has_bias, has_res = bias is not None, res is not None

    def body(*refs):
        a_ref, b_ref = refs[0], refs[1]
        pos = 2
        bias_ref = res_ref = acc_ref = None
        if has_bias:
            bias_ref = refs[pos]
            pos += 1
        if has_res:
            res_ref = refs[pos]
            pos += 1
        if dep is not None:
            pos += 1
        extra = refs[pos:pos + 2 * len(more)]
        pos += 2 * len(more)
        tgt_ref = lp_ref = None
        if target is not None:
            tgt_ref, o_ref, lp_ref = refs[pos], refs[pos + 1], refs[pos + 2]
            pos += 2
        else:
            o_ref = refs[pos]
        if nk > 1:
            acc_ref = refs[pos + 1]
        part = _dot(a_ref[...].astype(BF16), b_ref[...].astype(BF16), 0 if ta else 1, 1 if tb else 0)
        for q in range(len(more)):
            part = part + _dot(extra[2 * q][...].astype(BF16), extra[2 * q + 1][...].astype(BF16),
                               0 if ta else 1, 1 if tb else 0)

        def finish(acc):
            if has_bias:
                acc = acc + bias_ref[...]
            if has_res:
                acc = acc + res_ref[...]
            if target is not None:
                err = acc - tgt_ref[...]
                acc = err * (1.0 / N)
                part_loss = jnp.sum(jnp.sum(err * err, axis=1, keepdims=True), axis=0, keepdims=True) * (0.5 / N)
                first = (pl.program_id(0) == 0) & (pl.program_id(1) == 0)

                @pl.when(first)
                def _():
                    lp_ref[...] = jnp.broadcast_to(part_loss, lp_ref.shape)

                @pl.when(jnp.logical_not(first))
                def _():
                    lp_ref[...] += jnp.broadcast_to(part_loss, lp_ref.shape)

            o_ref[...] = acc.astype(out_dtype)

        if nk == 1:
            finish(part)
        else:
            k = pl.program_id(2)

            @pl.when(k == 0)
            def _():
                acc_ref[...] = part

            @pl.when(k > 0)
            def _():
                acc_ref[...] += part

            @pl.when(k == nk - 1)
            def _():
                finish(acc_ref[...])

    if a_spec is None:
        a_spec = pl.BlockSpec((tk, tm), lambda i, j, k: (k, i)) if ta else pl.BlockSpec((tm, tk), lambda i, j, k: (i, k))
    if b_spec is None:
        b_spec = (pl.BlockSpec((tn, tk), lambda i, j, k: (j, k + kb0)) if tb
                  else pl.BlockSpec((tk, tn), lambda i, j, k: (k + kb0, j)))
    if o_spec is None:
        o_spec = pl.BlockSpec((tm, tn), lambda i, j, k: (i, j))
    in_specs, args = [a_spec, b_spec], [a, b]
    if has_bias:
        in_specs.append(pl.BlockSpec((1, tn), lambda i, j, k: (0, j)))
        args.append(bias)
    if has_res:
        in_specs.append(pl.BlockSpec((tm, tn), lambda i, j, k: (i, j)))
        args.append(res)
    if dep is not None:
        in_specs.append(pl.BlockSpec(memory_space=pl.ANY))
        args.append(dep)
    for piece in more:
        a2, sa, b2, sb = piece if len(piece) == 4 else (a, piece[0], b, piece[1])
        in_specs += [sa, sb]
        args += [a2, b2]
    out_specs, out_shape = o_spec, jax.ShapeDtypeStruct(o_shape or (M, N), out_dtype)
    if target is not None:
        in_specs.append(pl.BlockSpec((tm, tn), lambda i, j, k: (i, j)))
        args.append(target)
        out_specs = [o_spec, pl.BlockSpec((8, 128), lambda i, j, k: (0, 0))]
        out_shape = [out_shape, jax.ShapeDtypeStruct((8, 128), F32)]
    return pl.pallas_call(
        body, name=name, grid=(M // tm, N // tn, nk), in_specs=in_specs, out_specs=out_specs, out_shape=out_shape,
        scratch_shapes=[pltpu.VMEM((tm, tn), F32)] if nk > 1 else [],
        compiler_params=_cp(3),
    )(*args)


def _norm_mm(x, gain, b, *, name, bias=None, N=None, tn=None, b_spec=None, dep=None):
    M, K = x.shape
    N = N or b.shape[1]
    tm = _pick(M, (1024, 512, 256))
    tn = tn or _pick(N, (512, 1408, 256, 128))
    has_bias = bias is not None

    def body(*refs):
        x_ref, g_ref, b_ref = refs[:3]
        pos = 3 + (1 if has_bias else 0) + (0 if dep is None else 1)
        o_ref, h_ref = refs[pos], refs[pos + 1]

        @pl.when(pl.program_id(1) == 0)
        def _():
            xv = x_ref[...]
            h_ref[...] = (xv * lax.rsqrt(jnp.mean(xv * xv, axis=-1, keepdims=True) + EPS) * g_ref[...]).astype(BF16)

        acc = _dot(h_ref[...], b_ref[...].astype(BF16))
        if has_bias:
            acc = acc + refs[3][...]
        o_ref[...] = acc

    in_specs = [pl.BlockSpec((tm, K), lambda i, j: (i, 0)), pl.BlockSpec((1, K), lambda i, j: (0, 0)),
                b_spec or pl.BlockSpec((K, tn), lambda i, j: (0, j))]
    args = [x, gain, b]
    if has_bias:
        in_specs.append(pl.BlockSpec((1, tn), lambda i, j: (0, j)))
        args.append(bias)
    if dep is not None:
        in_specs.append(pl.BlockSpec(memory_space=pl.ANY))
        args.append(dep)
    return pl.pallas_call(
        body, name=name, grid=(M // tm, N // tn), in_specs=in_specs,
        out_specs=[pl.BlockSpec((tm, tn), lambda i, j: (i, j)), pl.BlockSpec((tm, K), lambda i, j: (i, 0))],
        out_shape=[jax.ShapeDtypeStruct((M, N), F32), jax.ShapeDtypeStruct((M, K), BF16)], compiler_params=_cp(2),
    )(*args)


def _rms_bwd(x, gains, dhs, dres, *, name, tr=256, want_colsum=False):
    S, D = x.shape
    n = len(gains)
    steps = S // tr

    def body(*refs):
        x_ref = refs[0]
        g_refs = refs[1:1 + n]
        dh_refs = refs[1 + n:1 + 2 * n]
        dres_ref = refs[1 + 2 * n]
        dx_ref = refs[2 + 2 * n]
        dg_refs = refs[3 + 2 * n:3 + 3 * n]
        cs_ref = refs[3 + 3 * n] if want_colsum else None
        i = pl.program_id(0)
        xv = x_ref[...]
        r = lax.rsqrt(jnp.mean(xv * xv, axis=-1, keepdims=True) + EPS)
        xh = xv * r
        dx = dres_ref[...]
        for q in range(n):
            dh = dh_refs[q][...]
            dxh = dh * g_refs[q][...]
            dx = dx + r * (dxh - xh * jnp.mean(dxh * xh, axis=-1, keepdims=True))
            part = jnp.sum(dh * xh, axis=0, keepdims=True)

            @pl.when(i == 0)
            def _():
                dg_refs[q][...] = part

            @pl.when(i > 0)
            def _():
                dg_refs[q][...] += part

        dx_ref[...] = dx
        if want_colsum:
            cpart = jnp.sum(dx, axis=0, keepdims=True)

            @pl.when(i == 0)
            def _():
                cs_ref[...] = cpart

            @pl.when(i > 0)
            def _():
                cs_ref[...] += cpart

    row = pl.BlockSpec((tr, D), lambda i: (i, 0))
    vec = pl.BlockSpec((1, D), lambda i: (0, 0))
    n_vec_out = n + (1 if want_colsum else 0)
    outs = pl.pallas_call(
        body, name=name, grid=(steps,), in_specs=[row] + [vec] * n + [row] * n + [row],
        out_specs=[row] + [vec] * n_vec_out,
        out_shape=[jax.ShapeDtypeStruct((S, D), F32)] + [jax.ShapeDtypeStruct((1, D), F32)] * n_vec_out,
        compiler_params=_cp(1),
    )(x, *gains, *dhs, dres)
    return outs


def _colsum(x, *, name, tr=256):
    S, D = x.shape

    def body(x_ref, o_ref):
        i = pl.program_id(0)
        part = jnp.sum(x_ref[...].astype(F32), axis=0, keepdims=True)

        @pl.when(i == 0)
        def _():
            o_ref[...] = part

        @pl.when(i > 0)
        def _():
            o_ref[...] += part

    return pl.pallas_call(
        body, name=name, grid=(S // tr,), in_specs=[pl.BlockSpec((tr, D), lambda i: (i, 0))],
        out_specs=pl.BlockSpec((1, D), lambda i: (0, 0)), out_shape=jax.ShapeDtypeStruct((1, D), F32),
        compiler_params=_cp(1),
    )(x)


STRIP = 64
HALO = 8


def _strips(S, tc):
    return [(r0, slice(l0, l0 + 128)) for l0 in range(0, tc, 128) for r0 in range(S - STRIP, -1, -STRIP)]


def _with_halo(ref, r0, ls):
    if r0 == 0:
        return jnp.concatenate([jnp.zeros((HALO, 128), F32), ref[0:STRIP, ls]], axis=0)
    return ref[r0 - HALO:r0 + STRIP, ls]


def _conv_strip(xw, w_ref, b_ref, ls, width):
    acc = b_ref[:, ls] + w_ref[pl.ds(width - 1, 1), ls] * xw[HALO:]
    shifted = []
    for s in range(1, width):
        xs = pltpu.roll(xw, s, axis=0)[HALO:]
        shifted.append(xs)
        acc = acc + w_ref[pl.ds(width - 1 - s, 1), ls] * xs
    return acc, shifted


def _conv_strip_back(dacc, after, xc, shifted, w_ref, ls, width):
    ext = jnp.concatenate([dacc, after], axis=0)
    dx = w_ref[pl.ds(width - 1, 1), ls] * dacc
    dws = [None] * width
    dws[width - 1] = jnp.sum(dacc * xc, axis=0, keepdims=True)
    for s in range(1, width):
        dx = dx + w_ref[pl.ds(width - 1 - s, 1), ls] * pltpu.roll(ext, STRIP + HALO - s, axis=0)[:STRIP]
        dws[width - 1 - s] = jnp.sum(dacc * shifted[s - 1], axis=0, keepdims=True)
    return dx, dws, jnp.sum(dacc, axis=0, keepdims=True)


def _conv_back_block(S, tc, width, w_ref, b_ref, x_ref, dacc_of, dx_store, dw_ref, db_ref):
    for l0 in range(0, tc, 128):
        ls = slice(l0, l0 + 128)
        after = jnp.zeros((HALO, 128), F32)
        tot = None
        for r0 in range(S - STRIP, -1, -STRIP):
            xw = _with_halo(x_ref, r0, ls)
            acc, shifted = _conv_strip(xw, w_ref, b_ref, ls, width)
            dacc = dacc_of(r0, ls, acc, _sigmoid(acc))
            dx, dws, db = _conv_strip_back(dacc, after, xw[HALO:], shifted, w_ref, ls, width)
            dx_store(r0, ls, dx)
            after = dacc[:HALO]
            part = dws + [db]
            tot = part if tot is None else [p + q for p, q in zip(tot, part)]
        for k in range(width):
            dw_ref[pl.ds(k, 1), ls] = tot[k]
        db_ref[:, ls] = tot[width]


def _conv_silu_fwd(xin, col0, C, w, b, *, name, tc=512):
    S = xin.shape[0]
    width = w.shape[0]
    off = col0 // tc

    def body(x_ref, w_ref, b_ref, o_ref):
        for r0, ls in _strips(S, tc):
            acc, _ = _conv_strip(_with_halo(x_ref, r0, ls), w_ref, b_ref, ls, width)
            o_ref[r0:r0 + STRIP, ls] = acc * _sigmoid(acc)

    return pl.pallas_call(
        body, name=name, grid=(C // tc,),
        in_specs=[pl.BlockSpec((S, tc), lambda j: (0, j + off)), pl.BlockSpec((width, tc), lambda j: (0, j)),
                  pl.BlockSpec((1, tc), lambda j: (0, j))],
        out_specs=pl.BlockSpec((S, tc), lambda j: (0, j)), out_shape=jax.ShapeDtypeStruct((S, C), F32),
        compiler_params=_cp(1),
    )(xin, w, b)


def _conv_silu_bwd(xin, col0, C, w, b, douts, *, name, tc=256):
    S = xin.shape[0]
    width = w.shape[0]
    off = col0 // tc
    nd = len(douts)
    ranges = [(o // tc, (o + d.shape[1]) // tc) for d, o in douts]

    def body(*refs):
        x_ref, w_ref, b_ref = refs[0], refs[1], refs[2]
        d_refs = refs[3:3 + nd]
        dx_ref, dw_ref, db_ref = refs[3 + nd], refs[4 + nd], refs[5 + nd]
        j = pl.program_id(0)

        def dacc_of(r0, ls, acc, sg):
            dout = jnp.zeros((STRIP, 128), F32)
            for q in range(nd):
                lo, hi = ranges[q]
                dout = dout + jnp.where((j >= lo) & (j < hi), d_refs[q][r0:r0 + STRIP, ls], 0.0)
            return dout * (sg * (1.0 + acc * (1.0 - sg)))

        def dx_store(r0, ls, dx):
            dx_ref[r0:r0 + STRIP, ls] = dx.astype(BF16)

        _conv_back_block(S, tc, width, w_ref, b_ref, x_ref, dacc_of, dx_store, dw_ref, db_ref)

    d_specs = [pl.BlockSpec((S, tc), (lambda j, lo=lo, hi=hi: (0, jnp.clip(j - lo, 0, hi - lo - 1)))) for lo, hi in ranges]
    return pl.pallas_call(
        body, name=name, grid=(C // tc,),
        in_specs=[pl.BlockSpec((S, tc), lambda j: (0, j + off)), pl.BlockSpec((width, tc), lambda j: (0, j)),
                  pl.BlockSpec((1, tc), lambda j: (0, j))] + d_specs,
        out_specs=[pl.BlockSpec((S, tc), lambda j: (0, j)), pl.BlockSpec((width, tc), lambda j: (0, j)),
                   pl.BlockSpec((1, tc), lambda j: (0, j))],
        out_shape=[jax.ShapeDtypeStruct((S, C), BF16), jax.ShapeDtypeStruct((width, C), F32),
                   jax.ShapeDtypeStruct((1, C), F32)],
        compiler_params=_cp(1),
    )(xin, w, b, *[d for d, _ in douts])


def _ffn_act_fwd(u, w, b, *, name, tc=256):
    S, F2 = u.shape
    Fd = F2 // 2
    width = w.shape[0]
    nb = Fd // tc

    def body(g_ref, v_ref, w_ref, b_ref, o_ref):
        for r0, ls in _strips(S, tc):
            acc, _ = _conv_strip(_with_halo(g_ref, r0, ls), w_ref, b_ref, ls, width)
            o_ref[r0:r0 + STRIP, ls] = (acc * _sigmoid(acc) * v_ref[r0:r0 + STRIP, ls]).astype(BF16)

    return pl.pallas_call(
        body, name=name, grid=(nb,),
        in_specs=[pl.BlockSpec((S, tc), lambda j: (0, j)), pl.BlockSpec((S, tc), lambda j: (0, j + nb)),
                  pl.BlockSpec((width, tc), lambda j: (0, j)), pl.BlockSpec((1, tc), lambda j: (0, j))],
        out_specs=pl.BlockSpec((S, tc), lambda j: (0, j)), out_shape=jax.ShapeDtypeStruct((S, Fd), BF16),
        compiler_params=_cp(1),
    )(u, u, w, b)


def _ffn_act_bwd(u, w, b, da, *, name, tc=256):
    S, F2 = u.shape
    Fd = F2 // 2
    width = w.shape[0]
    nb = Fd // tc

    def body(g_ref, v_ref, w_ref, b_ref, da_ref, du_ref, dw_ref, db_ref, a_ref):
        def dacc_of(r0, ls, acc, sg):
            rs = slice(r0, r0 + STRIP)
            dav, val, silu = da_ref[rs, ls], v_ref[rs, ls], acc * sg
            a_ref[rs, ls] = (silu * val).astype(BF16)
            du_ref[1, rs, ls] = (dav * silu).astype(BF16)
            return dav * val * (sg * (1.0 + acc * (1.0 - sg)))

        def dx_store(r0, ls, dx):
            du_ref[0, r0:r0 + STRIP, ls] = dx.astype(BF16)

        _conv_back_block(S, tc, width, w_ref, b_ref, g_ref, dacc_of, dx_store, dw_ref, db_ref)

    blk = pl.BlockSpec((S, tc), lambda j: (0, j))
    return pl.pallas_call(
        body, name=name, grid=(nb,),
        in_specs=[blk, pl.BlockSpec((S, tc), lambda j: (0, j + nb)), pl.BlockSpec((width, tc), lambda j: (0, j)),
                  pl.BlockSpec((1, tc), lambda j: (0, j)), blk],
        out_specs=[pl.BlockSpec((2, S, tc), lambda j: (0, 0, j)), pl.BlockSpec((width, tc), lambda j: (0, j)),
                   pl.BlockSpec((1, tc), lambda j: (0, j)), blk],
        out_shape=[jax.ShapeDtypeStruct((2, S, Fd), BF16),
                   jax.ShapeDtypeStruct((width, Fd), F32), jax.ShapeDtypeStruct((1, Fd), F32),
                   jax.ShapeDtypeStruct((S, Fd), BF16)],
        compiler_params=_cp(1),
    )(u, u, w, b, da)


def _ssd_prep(dtr, dt_bias, a_log, *, name="ssd_prep"):
    S = dtr.shape[0]

    def body(d_ref, b_ref, al_ref, dt_ref, ac_ref, sg_ref, act_ref):
        lane = _iota((CHUNK, 128), 1)
        valid = lane < SSM_HEADS
        z = d_ref[...] + b_ref[...]
        dt = jnp.where(valid, jnp.maximum(z, 0.0) + jnp.log(1.0 + jnp.exp(-jnp.abs(z))), 0.0)
        a = dt * (-jnp.exp(al_ref[...]))
        row = _iota((CHUNK, 128), 0)
        k = 1
        while k < CHUNK:
            a = a + jnp.where(row >= k, pltpu.roll(a, k, axis=0), 0.0)
            k *= 2
        sg = jnp.where(valid, _sigmoid(z), 0.0)
        for arr, ref in ((dt, dt_ref), (a, ac_ref), (sg, sg_ref)):
            for g in range(SSM_GROUPS):
                ref[g] = jnp.where(lane < 4, arr if g == 0 else pltpu.roll(arr, 128 - 4 * g, axis=1), 0.0)
        act_ref[...] = a.T[:SSM_HEADS, :]

    blk = pl.BlockSpec((CHUNK, 128), lambda i: (i, 0))
    vec = pl.BlockSpec((1, 128), lambda i: (0, 0))
    grp = pl.BlockSpec((SSM_GROUPS, CHUNK, 128), lambda i: (0, i, 0))
    return pl.pallas_call(
        body, name=name, grid=(S // CHUNK,), in_specs=[blk, vec, vec],
        out_specs=[grp, grp, grp, pl.BlockSpec((SSM_HEADS, CHUNK), lambda i: (0, i))],
        out_shape=[jax.ShapeDtypeStruct((SSM_GROUPS, S, 128), F32)] * 3 + [jax.ShapeDtypeStruct((SSM_HEADS, S), F32)],
        compiler_params=_cp(1),
    )(dtr, dt_bias, a_log)


SSD_GPS = 4


def _expand4(v, lanes):
    out = jnp.broadcast_to(v[:, 3:4], lanes.shape)
    for hh in (2, 1, 0):
        out = jnp.where(lanes < 64 * (hh + 1), v[:, hh:hh + 1], out)
    return out


def _ssd_fwd(xbc, dt_g, ac_g, ac_t, *, name="ssd_fwd", dep=None):
    S = xbc.shape[0]
    nc = S // CHUNK
    Lc = CHUNK

    def body(x_ref, b_ref, c_ref, dt_ref, ac_ref, act_ref, *rest):
        y_ref, st_out_ref, st_ref = rest[-3:]
        g2 = pl.program_id(0)
        c = pl.program_id(1)

        @pl.when(c == 0)
        def _():
            st_ref[...] = jnp.zeros_like(st_ref)

        causal = _iota((Lc, Lc), 0) >= _iota((Lc, Lc), 1)
        lane256 = _iota((Lc, 256), 1)
        lane128 = _iota((Lc, 128), 1)
        row128 = _iota((128, 128), 0)
        for gg in range(SSD_GPS):
            g = SSD_GPS * g2 + gg
            bv = b_ref[:, 128 * gg:128 * (gg + 1)]
            cbf = c_ref[:, 128 * gg:128 * (gg + 1)].astype(BF16)
            cb = _dot(cbf, bv.astype(BF16), 1, 1)
            dtg, acg = dt_ref[gg], ac_ref[gg]
            ac_last = ac_ref[gg, pl.ds(Lc - 1, 1), :]
            dt4 = _expand4(dtg, lane256)
            ac4 = _expand4(acg, lane256)
            e4 = jnp.exp(ac4)
            xdb = (x_ref[:, 256 * gg:256 * (gg + 1)] * dt4).astype(BF16)
            st_out_ref[gg] = st_ref[gg]
            for p in range(2):
                xd_p = xdb[:, 128 * p:128 * (p + 1)]
                st_p = st_ref[gg, p]
                ys, sn, cds = [], [], []
                for q in range(2):
                    hh = 2 * p + q
                    a_col = acg[:, hh:hh + 1]
                    a_row = act_ref[pl.ds(4 * g + hh, 1), :]
                    dec = jnp.exp(jnp.where(causal, a_col - a_row, NEG))
                    w = (cb * dec).astype(BF16)
                    ys.append(_dot(w, xd_p))
                    al = ac_last[:, hh:hh + 1]
                    dte = jnp.exp(al - a_col)
                    sn.append(_dot(xd_p, (bv * dte).astype(BF16), 0, 0))
                    cds.append(jnp.exp(al))
                y_diag = jnp.where(lane128 < 64, ys[0], ys[1])
                y_off = _dot(cbf, st_p.astype(BF16), 1, 1) * e4[:, 128 * p:128 * (p + 1)]
                y_ref[:, 256 * gg + 128 * p:256 * gg + 128 * (p + 1)] = y_diag + y_off
                st_ref[gg, p] = jnp.where(row128 < 64, st_p * cds[0] + sn[0], st_p * cds[1] + sn[1])

    G = SSD_GPS
    per_g = lambda g, c: (g, c, 0)
    return pl.pallas_call(
        body, name=name, grid=(SSM_GROUPS // G, nc),
        in_specs=[pl.BlockSpec((Lc, 256 * G), lambda g, c: (c, g)),
                  pl.BlockSpec((Lc, 128 * G), lambda g, c: (c, 16 // G + g)),
                  pl.BlockSpec((Lc, 128 * G), lambda g, c: (c, 24 // G + g)),
                  pl.BlockSpec((G, Lc, 128), per_g), pl.BlockSpec((G, Lc, 128), per_g),
                  pl.BlockSpec((SSM_HEADS, Lc), lambda g, c: (0, c))] + ([] if dep is None else [pl.BlockSpec(memory_space=pl.ANY)]),
        out_specs=[pl.BlockSpec((Lc, 256 * G), lambda g, c: (c, g)),
                   pl.BlockSpec((G, None, 2, 128, 128), lambda g, c: (g, c, 0, 0, 0))],
        out_shape=[jax.ShapeDtypeStruct((S, 2048), F32), jax.ShapeDtypeStruct((SSM_GROUPS, nc, 2, 128, 128), F32)],
        scratch_shapes=[pltpu.VMEM((G, 2, 128, 128), F32)], compiler_params=_cp(2),
    )(xbc, xbc, xbc, dt_g, ac_g, ac_t, *([] if dep is None else [dep]))


def _ssd_bwd(xbc, dt_g, ac_g, ac_t, states, dy, dexp, *, name="ssd_bwd", dep=None):
    S = xbc.shape[0]
    nc = S // CHUNK
    Lc = CHUNK

    def body(x_ref, b_ref, c_ref, dt_ref, ac_ref, act_ref, st_ref, dy_ref, d_ref, *rest):
        dx_ref, db_ref, dc_ref, dh_ref, ds_ref = rest[-5:]
        g2 = pl.program_id(0)
        cc = pl.program_id(1)

        @pl.when(cc == 0)
        def _():
            ds_ref[...] = jnp.zeros_like(ds_ref)

        causal = _iota((Lc, Lc), 0) >= _iota((Lc, Lc), 1)
        lane256 = _iota((Lc, 256), 1)
        lane128 = _iota((Lc, 128), 1)
        row128 = _iota((128, 128), 0)
        ind_rows = _iota((256, 128), 0) >> 6
        ind_cols = _iota((256, 128), 1)
        ind_a = (ind_rows == ind_cols).astype(BF16)
        ind_b = (ind_rows + 4 == ind_cols).astype(BF16)
        for gg in range(SSD_GPS):
            g = SSD_GPS * g2 + gg
            bv = b_ref[:, 128 * gg:128 * (gg + 1)]
            cv = c_ref[:, 128 * gg:128 * (gg + 1)]
            bbf, cbf = bv.astype(BF16), cv.astype(BF16)
            cb = _dot(cbf, bbf, 1, 1)
            dtg, acg = dt_ref[gg], ac_ref[gg]
            ac_last = ac_ref[gg, pl.ds(Lc - 1, 1), :]
            dt4 = _expand4(dtg, lane256)
            ac4 = _expand4(acg, lane256)
            acl4 = _expand4(ac_last, _iota((1, 256), 1))
            e4 = jnp.exp(ac4)
            dte4 = jnp.exp(acl4 - ac4)
            xv = x_ref[:, 256 * gg:256 * (gg + 1)]
            xd = xv * dt4
            xdb = xd.astype(BF16)
            dyv = dy_ref[:, 256 * gg:256 * (gg + 1)]
            dcb = jnp.zeros((Lc, Lc), F32)
            dc_acc = jnp.zeros((Lc, 128), F32)
            db_acc = jnp.zeros((Lc, 128), F32)
            u_parts, dxd_parts, ends = [], [], []
            for p in range(2):
                sl = slice(128 * p, 128 * (p + 1))
                xd_p, xdb_p, dy_p = xd[:, sl], xdb[:, sl], dyv[:, sl]
                dyb_p = dy_p.astype(BF16)
                e_p, dte_p = e4[:, sl], dte4[:, sl]
                sp = st_ref[gg, p]
                spb = sp.astype(BF16)
                dsn = ds_ref[gg, p]
                dsnb = dsn.astype(BF16)
                yds, dxds, cds = [], [], []
                for q in range(2):
                    hh = 2 * p + q
                    a_col = acg[:, hh:hh + 1]
                    a_row = act_ref[pl.ds(4 * g + hh, 1), :]
                    dec = jnp.exp(jnp.where(causal, a_col - a_row, NEG))
                    w = (cb * dec).astype(BF16)
                    head = (lane128 < 64) if q == 0 else (lane128 >= 64)
                    dym = jnp.where(head, dyb_p, jnp.zeros_like(dyb_p))
                    dw = _dot(dym, xdb_p, 1, 1)
                    dcb = dcb + dw * dec
                    yds.append(_dot(w, xdb_p))
                    dxds.append(_dot(w, dyb_p, 0, 0))
                    cds.append(jnp.exp(ac_last[:, hh:hh + 1]))
                y_diag = jnp.where(lane128 < 64, yds[0], yds[1])
                dxd_diag = jnp.where(lane128 < 64, dxds[0], dxds[1])
                y_off = _dot(cbf, spb, 1, 1) * e_p
                dgp = dy_p * e_p
                dgb = dgp.astype(BF16)
                dc_acc = dc_acc + _dot(dgb, spb)
                dsp = _dot(dgb, cbf, 0, 0)
                cd_col = jnp.where(row128[:, 0:1] < 64, cds[0], cds[1])
                qm = _dot(bbf, dsnb, 1, 1)
                dxd_state = dte_p * qm
                db_acc = db_acc + _dot((xd_p * dte_p).astype(BF16), dsnb)
                t_p = xd_p * dxd_state
                prod = dsn * sp
                e0 = jnp.sum(jnp.sum(jnp.where(row128 < 64, prod, 0.0), axis=1, keepdims=True), axis=0, keepdims=True)
                e1 = jnp.sum(jnp.sum(jnp.where(row128 >= 64, prod, 0.0), axis=1, keepdims=True), axis=0, keepdims=True)
                tcol = jnp.sum(t_p, axis=0, keepdims=True)
                lane1 = _iota((1, 128), 1)
                t0 = jnp.sum(jnp.where(lane1 < 64, tcol, 0.0), axis=1, keepdims=True)
                t1 = jnp.sum(jnp.where(lane1 >= 64, tcol, 0.0), axis=1, keepdims=True)
                ends.append(e0 * cds[0] + t0)
                ends.append(e1 * cds[1] + t1)
                ds_ref[gg, p] = dsn * cd_col + dsp
                u_parts.append(dyb_p.astype(F32) * y_diag - xdb_p.astype(F32) * dxd_diag + dy_p * y_off - t_p)
                dxd_parts.append(dxd_diag + dxd_state)
            dxd = jnp.concatenate(dxd_parts, axis=1)
            u_all = jnp.concatenate(u_parts, axis=1)
            dx_ref[:, 256 * gg:256 * (gg + 1)] = dxd * dt4 + dyv * d_ref[:, 256 * gg:256 * (gg + 1)]
            dcbb = dcb.astype(BF16)
            dc_ref[:, 128 * gg:128 * (gg + 1)] = dc_acc + _dot(dcbb, bbf)
            db_ref[:, 128 * gg:128 * (gg + 1)] = db_acc + _dot(dcbb, cbf, 0, 0)
            lane = _iota((Lc, 128), 1)
            endv = jnp.zeros((Lc, 128), F32)
            for hh in range(4):
                endv = jnp.where(lane == 8 + hh, ends[hh], endv)
            dh_ref[gg] = _dot3(dxd * xv, ind_a) + _dot3(u_all, ind_b) + endv

    G = SSD_GPS
    rev = lambda c: nc - 1 - c
    per_g = lambda g, c: (g, rev(c), 0)
    return pl.pallas_call(
        body, name=name, grid=(SSM_GROUPS // G, nc),
        in_specs=[pl.BlockSpec((Lc, 256 * G), lambda g, c: (rev(c), g)),
                  pl.BlockSpec((Lc, 128 * G), lambda g, c: (rev(c), 16 // G + g)),
                  pl.BlockSpec((Lc, 128 * G), lambda g, c: (rev(c), 24 // G + g)),
                  pl.BlockSpec((G, Lc, 128), per_g), pl.BlockSpec((G, Lc, 128), per_g),
                  pl.BlockSpec((SSM_HEADS, Lc), lambda g, c: (0, rev(c))),
                  pl.BlockSpec((G, None, 2, 128, 128), lambda g, c: (g, rev(c), 0, 0, 0)),
                  pl.BlockSpec((Lc, 256 * G), lambda g, c: (rev(c), g)),
                  pl.BlockSpec((1, 256 * G), lambda g, c: (0, g))] + ([] if dep is None else [pl.BlockSpec(memory_space=pl.ANY)]),
        out_specs=[pl.BlockSpec((Lc, 256 * G), lambda g, c: (rev(c), g)),
                   pl.BlockSpec((Lc, 128 * G), lambda g, c: (rev(c), g)),
                   pl.BlockSpec((Lc, 128 * G), lambda g, c: (rev(c), g)),
                   pl.BlockSpec((G, Lc, 128), per_g)],
        out_shape=[jax.ShapeDtypeStruct((S, 2048), F32), jax.ShapeDtypeStruct((S, 1024), F32),
                   jax.ShapeDtypeStruct((S, 1024), F32), jax.ShapeDtypeStruct((SSM_GROUPS, S, 128), F32)],
        scratch_shapes=[pltpu.VMEM((G, 2, 128, 128), F32)], compiler_params=_cp(2),
    )(xbc, xbc, xbc, dt_g, ac_g, ac_t, states, dy, dexp, *([] if dep is None else [dep]))


def _ssd_post(dhead, dt_g, sg_g, alog_g, *, name="ssd_post"):
    S = dhead.shape[1]
    nc = S // CHUNK
    Lc = CHUNK

    def body(dh_ref, dt_ref, sg_ref, al_ref, o_ref, s_ref):
        @pl.when(pl.program_id(0) == 0)
        def _():
            s_ref[...] = jnp.zeros_like(s_ref)

        lane = _iota((Lc, 128), 1)
        row = _iota((Lc, 128), 0)
        row8 = _iota((8, 128), 0)
        out = jnp.zeros((Lc, 128), F32)
        for g in range(SSM_GROUPS):
            dh = dh_ref[g]
            a_neg = -jnp.exp(al_ref[g])
            dac = jnp.where(lane < 4, pltpu.roll(dh, 124, axis=1), 0.0)
            end = jnp.where(lane < 4, pltpu.roll(dh, 120, axis=1), 0.0)
            k = 1
            while k < Lc:
                dac = dac + jnp.where(row < Lc - k, pltpu.roll(dac, Lc - k, axis=0), 0.0)
                k *= 2
            da = dac + end
            ddt = jnp.where(lane < 4, da * a_neg + dh, 0.0)
            ddtr = ddt * sg_ref[g]
            out = out + (ddtr if g == 0 else pltpu.roll(ddtr, 4 * g, axis=1))
            dal = jnp.sum(da * dt_ref[g], axis=0, keepdims=True) * a_neg
            dbias = jnp.sum(ddtr, axis=0, keepdims=True)
            part = jnp.where(row8 == 0, dal, jnp.where(row8 == 1, dbias, 0.0))
            s_ref[g] += part
        o_ref[...] = out.astype(BF16)

    grp = pl.BlockSpec((SSM_GROUPS, Lc, 128), lambda c: (0, c, 0))
    whole = lambda r: pl.BlockSpec((SSM_GROUPS, r, 128), lambda c: (0, 0, 0))
    return pl.pallas_call(
        body, name=name, grid=(nc,), in_specs=[grp, grp, grp, whole(1)],
        out_specs=[pl.BlockSpec((Lc, 128), lambda c: (c, 0)), whole(8)],
        out_shape=[jax.ShapeDtypeStruct((S, 128), BF16), jax.ShapeDtypeStruct((SSM_GROUPS, 8, 128), F32)],
        compiler_params=_cp(1),
    )(dhead, dt_g, sg_g, alog_g)


def _gate_fwd(y, xbc, zx, dexp, gn, *, name="gate_fwd", tr=256):
    S = y.shape[0]
    W = 2048
    gw = W // SSM_GROUPS

    def body(y_ref, x_ref, z_ref, d_ref, g_ref, o_ref):
        z = z_ref[...]
        u = (y_ref[...] + x_ref[...] * d_ref[...]) * (z * _sigmoid(z))
        gv = g_ref[...]
        for q in range(SSM_GROUPS):
            sl = slice(gw * q, gw * (q + 1))
            uq = u[:, sl]
            r = lax.rsqrt(jnp.mean(uq * uq, axis=-1, keepdims=True) + EPS)
            o_ref[:, sl] = (uq * r * gv[:, sl]).astype(BF16)

    row = pl.BlockSpec((tr, W), lambda i: (i, 0))
    vec = pl.BlockSpec((1, W), lambda i: (0, 0))
    return pl.pallas_call(
        body, name=name, grid=(S // tr,), in_specs=[row, row, row, vec, vec], out_specs=row,
        out_shape=jax.ShapeDtypeStruct((S, W), BF16), compiler_params=_cp(1),
    )(y, xbc, zx, dexp, gn)


def _gate_bwd(y, xbc, zx, dexp, gn, dout, *, name="gate_bwd", tr=256):
    S = y.shape[0]
    W = 2048
    gw = W // SSM_GROUPS
    steps = S // tr

    def body(y_ref, x_ref, z_ref, d_ref, g_ref, do_ref, dy_ref, dz_ref, dg_ref, dd_ref, acc_ref):
        i = pl.program_id(0)

        @pl.when(i == 0)
        def _():
            acc_ref[...] = jnp.zeros_like(acc_ref)

        z = z_ref[...]
        sg = _sigmoid(z)
        sz = z * sg
        xs = x_ref[...]
        yt = y_ref[...] + xs * d_ref[...]
        u = yt * sz
        gv = g_ref[...]
        do = do_ref[...]
        dgs = []
        for q in range(SSM_GROUPS):
            sl = slice(gw * q, gw * (q + 1))
            uq = u[:, sl]
            r = lax.rsqrt(jnp.mean(uq * uq, axis=-1, keepdims=True) + EPS)
            uh = uq * r
            dq = do[:, sl]
            duh = dq * gv[:, sl]
            duq = r * (duh - uh * jnp.mean(duh * uh, axis=-1, keepdims=True))
            dgs.append(jnp.sum(dq * uh, axis=0, keepdims=True))
            dyt = duq * sz[:, sl]
            dy_ref[:, sl] = dyt
            dz_ref[:, sl] = (duq * yt[:, sl] * (sg[:, sl] * (1.0 + z[:, sl] * (1.0 - sg[:, sl])))).astype(BF16)
            acc_ref[:, sl] += jnp.sum(dyt * xs[:, sl], axis=0, keepdims=True)
        dg = jnp.concatenate(dgs, axis=1)

        @pl.when(i == 0)
        def _():
            dg_ref[...] = dg

        @pl.when(i > 0)
        def _():
            dg_ref[...] += dg

        @pl.when(i == steps - 1)
        def _():
            ind = ((_iota((W, 128), 0) >> 6) == _iota((W, 128), 1)).astype(BF16)
            dd_ref[...] = _dot3(jnp.broadcast_to(acc_ref[...], (8, W)), ind)[0:1, :]

    row = pl.BlockSpec((tr, W), lambda i: (i, 0))
    vec = pl.BlockSpec((1, W), lambda i: (0, 0))
    return pl.pallas_call(
        body, name=name, grid=(steps,), in_specs=[row, row, row, vec, vec, row],
        out_specs=[row, row, vec, pl.BlockSpec((1, 128), lambda i: (0, 0))],
        out_shape=[jax.ShapeDtypeStruct((S, W), F32), jax.ShapeDtypeStruct((S, W), BF16),
                   jax.ShapeDtypeStruct((1, W), F32), jax.ShapeDtypeStruct((1, 128), F32)],
        scratch_shapes=[pltpu.VMEM((1, W), F32)], compiler_params=_cp(1),
    )(y, xbc, zx, dexp, gn, dout)


def _rope_cs(posf, *, name="rope_tables", tr=256):
    S = posf.shape[0]

    def body(p_ref, c_ref, s_ref):
        j = (_iota((tr, 128), 1) & 31).astype(F32)
        ang = p_ref[...] * jnp.exp(j * (-math.log(ROPE_THETA) / 32.0))
        c_ref[...] = jnp.cos(ang)
        s_ref[...] = jnp.sin(ang)

    blk = pl.BlockSpec((tr, 128), lambda i: (i, 0))
    return pl.pallas_call(
        body, name=name, grid=(S // tr,), in_specs=[pl.BlockSpec((tr, 1), lambda i: (i, 0))], out_specs=[blk, blk],
        out_shape=[jax.ShapeDtypeStruct((S, 128), F32)] * 2, compiler_params=_cp(1),
    )(posf)


def _rope_tables(c_ref, s_ref, shape):
    reps = shape[1] // 128
    return jnp.tile(c_ref[...], (1, reps)), jnp.tile(s_ref[...], (1, reps)), (_iota(shape, 1) & 63) < 32


def _hn_inds(W):
    ind = ((_iota((W, 128), 0) >> 6) == _iota((W, 128), 1)).astype(BF16)
    ind_t = ((_iota((128, W), 1) >> 6) == _iota((128, W), 0)).astype(BF16)
    return ind, ind_t


def _hnrope_fwd(xin, col0, W, gain_w, rope, *, name, tr=256):
    S = xin.shape[0]
    off = col0 // W
    nh = W // HEAD

    def body(x_ref, g_ref, c_ref, s_ref, o_ref):
        x = x_ref[...]
        ind, ind_t = _hn_inds(W)
        r = lax.rsqrt(_dot3(x * x, ind) * (1.0 / HEAD) + EPS)
        xn = x * _dot3(r, ind_t) * g_ref[...]
        cs, sn, half = _rope_tables(c_ref, s_ref, (tr, W))
        rot = jnp.where(half, -pltpu.roll(xn, W - 32, axis=1), pltpu.roll(xn, 32, axis=1))
        out = (xn * cs + rot * sn).astype(BF16)
        for h in range(nh):
            o_ref[h] = out[:, HEAD * h:HEAD * (h + 1)]

    tab = pl.BlockSpec((tr, 128), lambda i: (i, 0))
    return pl.pallas_call(
        body, name=name, grid=(S // tr,),
        in_specs=[pl.BlockSpec((tr, W), lambda i: (i, off)), pl.BlockSpec((1, W), lambda i: (0, 0)), tab, tab],
        out_specs=pl.BlockSpec((nh, tr, HEAD), lambda i: (0, i, 0)), out_shape=jax.ShapeDtypeStruct((nh, S, HEAD), BF16),
        compiler_params=_cp(1),
    )(xin, gain_w, *rope)


def _hnrope_bwd(xin, col0, W, gain_w, rope, dout, *, name, tr=256):
    S = xin.shape[0]
    off = col0 // W
    steps = S // tr
    nh = W // HEAD

    def body(x_ref, g_ref, c_ref, s_ref, do_ref, dx_ref, cs_ref, dg_ref, acc_ref):
        i = pl.program_id(0)
        x = x_ref[...]
        ind, ind_t = _hn_inds(W)
        r = lax.rsqrt(_dot3(x * x, ind) * (1.0 / HEAD) + EPS)
        rw = _dot3(r, ind_t)
        xh = x * rw
        cs, sn, half = _rope_tables(c_ref, s_ref, (tr, W))
        do = jnp.concatenate([do_ref[h] for h in range(nh)], axis=1).astype(F32)
        gs = do * sn
        g1 = do * cs + jnp.where(half, pltpu.roll(gs, W - 32, axis=1), -pltpu.roll(gs, 32, axis=1))
        dxh = g1 * g_ref[...]
        t = _dot3(dxh * xh, ind) * (1.0 / HEAD)
        dx = rw * (dxh - xh * _dot3(t, ind_t))
        dx_ref[...] = dx.astype(BF16)
        cpart = jnp.sum(dx, axis=0, keepdims=True)
        gpart = jnp.sum(g1 * xh, axis=0, keepdims=True)

        @pl.when(i == 0)
        def _():
            cs_ref[...] = cpart
            acc_ref[...] = gpart

        @pl.when(i > 0)
        def _():
            cs_ref[...] += cpart
            acc_ref[...] += gpart

        @pl.when(i == steps - 1)
        def _():
            fold = ((_iota((W, 128), 0) & 63) == _iota((W, 128), 1)).astype(BF16)
            dg_ref[...] = _dot3(jnp.broadcast_to(acc_ref[...], (8, W)), fold)[0:1, :]

    tab = pl.BlockSpec((tr, 128), lambda i: (i, 0))
    return pl.pallas_call(
        body, name=name, grid=(steps,),
        in_specs=[pl.BlockSpec((tr, W), lambda i: (i, off)), pl.BlockSpec((1, W), lambda i: (0, 0)), tab, tab,
                  pl.BlockSpec((nh, tr, HEAD), lambda i: (0, i, 0))],
        out_specs=[pl.BlockSpec((tr, W), lambda i: (i, 0)), pl.BlockSpec((1, W), lambda i: (0, 0)),
                   pl.BlockSpec((1, 128), lambda i: (0, 0))],
        out_shape=[jax.ShapeDtypeStruct((S, W), BF16), jax.ShapeDtypeStruct((1, W), F32),
                   jax.ShapeDtypeStruct((1, 128), F32)],
        scratch_shapes=[pltpu.VMEM((1, W), F32)], compiler_params=_cp(1),
    )(xin, gain_w, *rope, dout)


def _attn_band():
    qi = jnp.arange(ATT_G * WINDOW)[:, None] % WINDOW
    ki = jnp.arange(2 * WINDOW)[None, :]
    rel = qi + WINDOW - ki
    ok = (rel >= 0) & (rel < WINDOW)
    return jnp.stack([jnp.where(ok & (ki >= WINDOW), 0.0, NEG), jnp.where(ok, 0.0, NEG)]).astype(F32)


def _attn_probs(q, kb, sink_ref, band_ref, h, i):
    s = _dot(q, kb, 1, 1) * (HEAD ** -0.5) + band_ref[jnp.minimum(i, 1)]
    r1 = _iota((4 * WINDOW, 1), 0)
    sink = jnp.where(r1 < WINDOW, sink_ref[4 * h], jnp.where(r1 < 2 * WINDOW, sink_ref[4 * h + 1],
                     jnp.where(r1 < 3 * WINDOW, sink_ref[4 * h + 2], sink_ref[4 * h + 3])))
    m = jnp.maximum(jnp.max(s, axis=1, keepdims=True), sink)
    p = jnp.exp(s - m)
    ps = jnp.exp(sink - m)
    inv = 1.0 / (jnp.sum(p, axis=1, keepdims=True) + ps)
    return p * inv, ps * inv


ATT_HPS = 4
_BAND = pl.BlockSpec((2, ATT_G * WINDOW, 2 * WINDOW), lambda h, i: (0, 0, 0))


def _attn_specs(S):
    qspec = pl.BlockSpec((ATT_HPS, ATT_G, WINDOW, HEAD), lambda h, i: (h, 0, i, 0))
    cur = pl.BlockSpec((ATT_HPS, WINDOW, HEAD), lambda h, i: (h, i, 0))
    prev = pl.BlockSpec((ATT_HPS, WINDOW, HEAD), lambda h, i: (h, jnp.maximum(i - 1, 0), 0))
    tok = pl.BlockSpec((WINDOW, ATT_HPS * ATT_G * HEAD), lambda h, i: (i, h))
    return qspec, cur, prev, tok


def _attn_fwd(qh, kh, vh, sinks, *, name="attn_fwd"):
    S = kh.shape[1]
    nb = S // WINDOW

    def body(s_ref, band_ref, q_ref, kc_ref, kp_ref, vc_ref, vp_ref, o_ref):
        h2, i = pl.program_id(0), pl.program_id(1)
        outs = []
        for hh in range(ATT_HPS):
            q = q_ref[hh].reshape(ATT_G * WINDOW, HEAD)
            kb = jnp.concatenate([kp_ref[hh], kc_ref[hh]], axis=0)
            vb = jnp.concatenate([vp_ref[hh], vc_ref[hh]], axis=0)
            probs, _ = _attn_probs(q, kb, s_ref, band_ref, ATT_HPS * h2 + hh, i)
            o = _dot(probs.astype(BF16), vb).astype(BF16)
            outs += [o[WINDOW * g:WINDOW * (g + 1)] for g in range(ATT_G)]
        o_ref[...] = jnp.concatenate(outs, axis=1)

    qspec, cur, prev, tok = _attn_specs(S)
    return pl.pallas_call(
        body, name=name, grid=(ATT_KV // ATT_HPS, nb),
        in_specs=[pl.BlockSpec(memory_space=pltpu.SMEM), _BAND, qspec, cur, prev, cur, prev], out_specs=tok,
        out_shape=jax.ShapeDtypeStruct((S, ATT_KV * ATT_G * HEAD), BF16), compiler_params=_cp(2),
    )(sinks, _attn_band(), qh, kh, kh, vh, vh)


def _attn_bwd(qh, kh, vh, sinks, doh, *, name="attn_bwd"):
    S = kh.shape[1]
    nb = S // WINDOW

    def body(s_ref, band_ref, q_ref, kc_ref, kp_ref, vc_ref, vp_ref, do_ref, dq_ref, dk_ref, dv_ref, dsk_ref):
        h2, i = pl.program_id(0), pl.program_id(1)

        @pl.when(i == 0)
        def _():
            dk_ref[...] = jnp.zeros_like(dk_ref)
            dv_ref[...] = jnp.zeros_like(dv_ref)
            dsk_ref[...] = jnp.zeros_like(dsk_ref)

        dov = do_ref[...]
        cur = pl.multiple_of(i * WINDOW, WINDOW)
        lane = _iota((8, 128), 1)
        row = _iota((8, 128), 0)
        scale = HEAD ** -0.5
        for hh in range(ATT_HPS):
            q = q_ref[hh].reshape(ATT_G * WINDOW, HEAD)
            do = jnp.concatenate([dov[:, HEAD * (ATT_G * hh + g):HEAD * (ATT_G * hh + g + 1)] for g in range(ATT_G)], axis=0)
            kb = jnp.concatenate([kp_ref[hh], kc_ref[hh]], axis=0)
            vb = jnp.concatenate([vp_ref[hh], vc_ref[hh]], axis=0)
            probs, psink = _attn_probs(q, kb, s_ref, band_ref, ATT_HPS * h2 + hh, i)
            dp = _dot(do, vb, 1, 1)
            delta = jnp.sum(probs * dp, axis=1, keepdims=True)
            ds = (probs * (dp - delta)).astype(BF16)
            dq_ref[hh] = (_dot(ds, kb) * scale).reshape(ATT_G, WINDOW, HEAD)
            dkb = _dot(ds, q, 0, 0) * scale
            dvb = _dot(probs.astype(BF16), do, 0, 0)
            dk_ref[hh, pl.ds(cur, WINDOW), :] += dkb[WINDOW:, :]
            dv_ref[hh, pl.ds(cur, WINDOW), :] += dvb[WINDOW:, :]
            prv = pl.multiple_of(jnp.maximum(i - 1, 0) * WINDOW, WINDOW)
            dk_ref[hh, pl.ds(prv, WINDOW), :] += dkb[:WINDOW, :]
            dv_ref[hh, pl.ds(prv, WINDOW), :] += dvb[:WINDOW, :]

            dsr = -psink * delta
            upd = jnp.zeros((8, 128), F32)
            for gq in range(ATT_G):
                v = jnp.sum(dsr[gq * WINDOW:(gq + 1) * WINDOW, :], axis=0, keepdims=True)
                upd = jnp.where((lane == gq) & (row == 0), v, upd)
            dsk_ref[hh] += upd

    qspec, cur, prev, tok = _attn_specs(S)
    full = pl.BlockSpec((ATT_HPS, S, HEAD), lambda h, i: (h, 0, 0))
    return pl.pallas_call(
        body, name=name, grid=(ATT_KV // ATT_HPS, nb),
        in_specs=[pl.BlockSpec(memory_space=pltpu.SMEM), _BAND, qspec, cur, prev, cur, prev, tok],
        out_specs=[qspec, full, full, pl.BlockSpec((ATT_HPS, 8, 128), lambda h, i: (h, 0, 0))],
        out_shape=[jax.ShapeDtypeStruct((ATT_KV, ATT_G, S, HEAD), F32), jax.ShapeDtypeStruct((ATT_KV, S, HEAD), F32),
                   jax.ShapeDtypeStruct((ATT_KV, S, HEAD), F32), jax.ShapeDtypeStruct((ATT_KV, 8, 128), F32)],
        compiler_params=_cp(2),
    )(sinks, _attn_band(), qh, kh, kh, vh, vh, doh)


def _heads_major(t, nh):
    S = t.shape[0]
    return t.reshape(S, nh, HEAD).transpose(1, 0, 2)


def _tokens_major(t):
    nh, S, _ = t.shape
    return t.transpose(1, 0, 2).reshape(S, nh * HEAD)


class _NoComm:
    def late_arrived(self, part, after):
        return None

    def late_weights(self, w, after, part):
        return w

    def advance(self, after, group=None, tensors=None):
        return None


def _local_step(x, posf, target, w, comm=None):
    S, D = x.shape
    gr = {}
    comm = comm or _NoComm()

    zx, h1 = _norm_mm(x, w["a_norm"], w["w_zx"], name="in_proj_zx", dep=w.get("dep"))
    dtr = _mm(h1, w["w_dt"], name="in_proj_dt")
    xbc = _conv_silu_fwd(zx, 2048, 4096, w["a_conv_w"], w["a_conv_b"], name="a_conv_f")
    dt_g, ac_g, sg_g, ac_t = _ssd_prep(dtr, w["a_dt_bias"], w["a_A_log"])
    y_ssd, states = _ssd_fwd(xbc, dt_g, ac_g, ac_t, dep=comm.late_arrived(0, [dt_g]))
    yg = _gate_fwd(y_ssd, xbc, zx, w["a_Dexp"], w["a_gnorm"])
    x1 = _mm(yg, w["a_out_proj"], res=x, name="out_proj")

    w = comm.late_weights(w, x1, 0)
    FW = w["f_w_in"][0].shape[2]

    def ffn_fwd(xin, l, loss_target=None):
        u, h = _norm_mm(xin, w["f_norm"][l], w["f_w_in"][l], name=f"f_in{l}", N=N_CHIPS * FW, tn=FW,
                        b_spec=pl.BlockSpec((None, D, FW), lambda i, j: (j, 0, 0)))
        a = _ffn_act_fwd(u, w["f_conv_w"][l], w["f_conv_b"][l], name=f"f_act_f{l}")
        dep = comm.late_arrived(1, [u]) if l == 0 else None
        xo = _mm(a, w["f_w_down"][l], res=xin, tk=a.shape[1], name=f"f_down{l}", target=loss_target, dep=dep)
        return xo, (h, u)

    x2, ffn0 = ffn_fwd(x1, 0)
    w = comm.late_weights(w, x2, 1)

    kv, hk = _norm_mm(x2, w["kv_norm"], w["w_kv"], bias=w["b_kv"], name="kv_proj")
    q, hq = _norm_mm(x2, w["b_norm"], w["w_q"], bias=w["b_q"], name="q_proj")
    rope = _rope_cs(posf)
    kr = _hnrope_fwd(kv, 0, 256, w["k_norm_w"], rope, name="k_rope_f")
    qr = _hnrope_fwd(q, 0, 1024, w["q_norm_w"], rope, name="q_rope_f")
    qh = qr.reshape(ATT_KV, ATT_G, S, HEAD)
    kh = kr
    vh = _heads_major(kv[:, 256:].astype(BF16), ATT_KV)
    att = _attn_fwd(qh, kh, vh, w["sinks"])
    x3 = _mm(att, w["w_o"], bias=w["b_o"], res=x2, name="o_proj")
    (dy, loss_part), ffn1 = ffn_fwd(x3, 1, target)

    def ffn_bwd(xin, l, saved, dyo, want_colsum, dep=None):
        h, u = saved
        da = _mm(dyo, w["f_w_down"][l], tb=True, name=f"f_down_dx{l}", dep=dep)
        du, dcw, dcb, a = _ffn_act_bwd(u, w["f_conv_w"][l], w["f_conv_b"][l], da, name=f"f_act_b{l}")
        dw_down = _mm(a, dyo, ta=True, out_dtype=BF16, name=f"f_down_dw{l}")
        dw_in = _mm(h, du, ta=True, out_dtype=BF16, name=f"f_in_dw{l}", dims=(D, N_CHIPS * FW, S), tm=D, tn=FW, tk=S,
                    b_spec=pl.BlockSpec((None, S, FW), lambda i, j, k: (j // 2, 0, j % 2)),
                    o_spec=pl.BlockSpec((None, D, FW), lambda i, j, k: (j, i, 0)), o_shape=(N_CHIPS, D, FW))
        ts = _pick(S, (1024, 512, 256))
        pieces = [(pl.BlockSpec((None, ts, FW), lambda i, j, k, q=q: (q // 2, i, q % 2)),
                   pl.BlockSpec((None, 512, FW), lambda i, j, k, q=q: (q, j, 0))) for q in range(N_CHIPS)]
        dh = _mm(du, w["f_w_in"][l], tb=True, name=f"f_in_dx{l}", dims=(S, D, FW), tm=ts, tn=512, tk=FW,
                 a_spec=pieces[0][0], b_spec=pieces[0][1], more=pieces[1:])
        outs = _rms_bwd(xin, [w["f_norm"][l]], [dh], dyo, name=f"f_norm_b{l}", want_colsum=want_colsum)
        g = dict(f_norm=outs[1], f_w_in=dw_in, f_conv_w=dcw, f_conv_b=dcb, f_w_down=dw_down)
        return outs[0], g, (outs[2] if want_colsum else None)

    dx3, gr["ffn1"], db_o = ffn_bwd(x3, 1, ffn1, dy, True)
    gr["b_o"] = db_o
    gr["w_o"] = _mm(att, dx3, ta=True, out_dtype=BF16, name="o_proj_dw")
    datt = _mm(dx3, w["w_o"], tb=True, out_dtype=BF16, name="o_proj_dx")
    dqh, dkh, dvh, dsk = _attn_bwd(qh, kh, vh, w["sinks"], datt)
    gr["sinks"] = dsk[:, 0, :4].reshape(1, 16)
    dv = _tokens_major(dvh).astype(BF16)
    dq, db_q, dqn = _hnrope_bwd(q, 0, 1024, w["q_norm_w"], rope, dqh.reshape(16, S, HEAD), name="q_rope_b")
    dk, db_k, dkn = _hnrope_bwd(kv, 0, 256, w["k_norm_w"], rope, dkh, name="k_rope_b")
    gr["q_norm"], gr["k_norm"] = dqn[:, :HEAD], dkn[:, :HEAD]
    gr["b_q"] = db_q
    gr["b_kv"] = jnp.concatenate([db_k, _colsum(dv, name="dv_colsum")], axis=1)
    dkv = jnp.concatenate([dk, dv], axis=1)
    gr["w_q"] = _mm(hq, dq, ta=True, out_dtype=BF16, name="q_proj_dw")
    gr["w_kv"] = _mm(hk, dkv, ta=True, out_dtype=BF16, name="kv_proj_dw")
    tok = comm.advance([gr["w_kv"]], 1, dict(f_down1=gr["ffn1"]["f_w_down"], f_in1=gr["ffn1"]["f_w_in"], w_o=gr["w_o"],
                                             w_q=gr["w_q"], w_kv=gr["w_kv"]))
    dhq = _mm(dq, w["w_q"], tb=True, name="q_proj_dx", dep=tok)
    dhk = _mm(dkv, w["w_kv"], tb=True, name="kv_proj_dx")
    dx2, gr["kv_norm"], gr["b_norm"] = _rms_bwd(x2, [w["kv_norm"], w["b_norm"]], [dhk, dhq], dx3, name="kvq_norm_b")

    dx1, gr["ffn0"], _ = ffn_bwd(x1, 0, ffn0, dx2, False, dep=comm.advance([dx2]))

    gr["a_out_proj"] = _mm(yg, dx1, ta=True, out_dtype=BF16, name="out_proj_dw")
    tok = comm.advance([dx1, gr["a_out_proj"]], 2,
                       dict(f_down0=gr["ffn0"]["f_w_down"], f_in0=gr["ffn0"]["f_w_in"], out_proj=gr["a_out_proj"]))
    dyg = _mm(dx1, w["a_out_proj"], tb=True, name="out_proj_dx", dep=tok)
    dy_ssd, dz, gr["a_gnorm"], dD = _gate_bwd(y_ssd, xbc, zx, w["a_Dexp"], w["a_gnorm"], dyg)
    gr["a_D"] = dD[:, :SSM_HEADS]
    dxs, dB, dC, dhead = _ssd_bwd(xbc, dt_g, ac_g, ac_t, states, dy_ssd, w["a_Dexp"], dep=comm.advance([dy_ssd]))
    ddtr, dsmall = _ssd_post(dhead, dt_g, sg_g, w["a_A_log_g"])
    gr["a_A_log"] = dsmall[:, 0, :4].reshape(1, SSM_HEADS)
    gr["a_dt_bias"] = dsmall[:, 1, :4].reshape(1, SSM_HEADS)
    dxbc, gr["a_conv_w"], gr["a_conv_b"] = _conv_silu_bwd(
        zx, 2048, 4096, w["a_conv_w"], w["a_conv_b"], [(dxs, 0), (dB, 2048), (dC, 3072)], name="a_conv_b")
    gr["w_z"] = _mm(h1, dz, ta=True, out_dtype=BF16, name="in_proj_dwz")
    gr["w_x"] = _mm(h1, dxbc, ta=True, out_dtype=BF16, name="in_proj_dwx")
    gr["w_dt"] = _mm(h1, ddtr, ta=True, out_dtype=BF16, name="in_proj_dwdt")
    ts = _pick(S, (1024, 512, 256))
    wblk = lambda q: pl.BlockSpec((512, 2048), lambda i, j, k: (j, q))
    dh1 = _mm(dz, w["w_zx"], tb=True, name="in_proj_dx", dims=(S, D, 2048), tm=ts, tn=512, tk=2048,
              a_spec=pl.BlockSpec((ts, 2048), lambda i, j, k: (i, 0)), b_spec=wblk(0),
              more=[(dxbc, pl.BlockSpec((ts, 2048), lambda i, j, k: (i, 0)), w["w_zx"], wblk(1)),
                    (dxbc, pl.BlockSpec((ts, 2048), lambda i, j, k: (i, 1)), w["w_zx"], wblk(2)),
                    (ddtr, pl.BlockSpec((ts, 128), lambda i, j, k: (i, 0)), w["w_dt"], pl.BlockSpec((512, 128), lambda i, j, k: (j, 0)))])
    dx0, gr["a_norm"] = _rms_bwd(x, [w["a_norm"]], [dh1], dx1, name="a_norm_b")
    tok = comm.advance([dx0], 3, dict(in_proj=_in_proj_grad(gr).reshape(D, N_CHIPS, -1).transpose(1, 0, 2)))
    return loss_part, dx0, gr, tok


def _prep_small(full, w):
    w["a_norm"] = full["a_norm"]
    w["a_conv_w"] = full["a_conv_w"][0]
    w["a_conv_b"] = full["a_conv_b"]
    pad32 = lambda v: jnp.pad(v, ((0, 0), (0, 128 - SSM_HEADS)))
    w["a_dt_bias"] = pad32(full["a_dt_bias"])
    w["a_A_log"] = pad32(full["a_A_log"])
    w["a_A_log_g"] = jnp.pad(full["a_A_log"].reshape(SSM_GROUPS, 1, 4), ((0, 0), (0, 0), (0, 124)))
    w["a_Dexp"] = jnp.repeat(full["a_D"], HEAD, axis=1)
    w["a_gnorm"] = full["a_gnorm"]
    w["f_norm"] = [full["f_norm"][l:l + 1] for l in range(2)]
    w["f_conv_w"] = [full["f_conv_w"][l] for l in range(2)]
    w["f_conv_b"] = [full["f_conv_b"][l:l + 1] for l in range(2)]
    w["kv_norm"] = full["kv_norm"].reshape(1, -1)
    w["b_kv"] = full["b_kv"].reshape(1, -1)
    w["k_norm_w"] = jnp.tile(full["k_norm"].reshape(1, HEAD), (1, ATT_KV))
    w["b_norm"] = full["b_norm"]
    w["b_q"] = full["b_q"]
    w["q_norm_w"] = jnp.tile(full["q_norm"], (1, ATT_KV * ATT_G))
    w["sinks"] = full["sinks"].reshape(-1)
    w["b_o"] = full["b_o"]
    return w


def _split_in_proj(ip):
    return ip[:, :6144].astype(BF16), jnp.pad(ip[:, 6144:], ((0, 0), (0, 128 - SSM_HEADS))).astype(BF16)


def _join_in_proj(blocks, *, name="in_proj_join", tr=256):
    _, R, cw = blocks.shape
    zx_cols = 3 * 2048
    rest = N_CHIPS * cw - zx_cols

    def body(b_ref, zx_ref, dt_ref):
        whole = jnp.concatenate([b_ref[j] for j in range(N_CHIPS)], axis=1)
        zx_ref[...] = whole[:, :zx_cols]
        dt_ref[...] = jnp.concatenate([whole[:, zx_cols:], jnp.zeros((tr, 128 - rest), BF16)], axis=1)

    return pl.pallas_call(
        body, name=name, grid=(R // tr,), in_specs=[pl.BlockSpec((N_CHIPS, tr, cw), lambda i: (0, i, 0))],
        out_specs=[pl.BlockSpec((tr, zx_cols), lambda i: (i, 0)), pl.BlockSpec((tr, 128), lambda i: (i, 0))],
        out_shape=[jax.ShapeDtypeStruct((R, zx_cols), BF16), jax.ShapeDtypeStruct((R, 128), BF16)],
        compiler_params=_cp(1),
    )(blocks)


def _prep_weights(full):
    w = _prep_small(full, {})
    w["w_zx"], w["w_dt"] = _split_in_proj(full["a_in_proj"][0])
    w["a_out_proj"] = full["a_out_proj"][0].astype(BF16)
    w["f_w_in"] = [full["f_w_in"][l].reshape(1024, N_CHIPS, -1).transpose(1, 0, 2).astype(BF16) for l in range(2)]
    w["f_w_down"] = [full["f_w_down"][l].astype(BF16) for l in range(2)]
    w["w_kv"] = full["w_kv"].astype(BF16)
    w["w_q"] = full["w_q"][0].astype(BF16)
    w["w_o"] = full["w_o"][0].astype(BF16)
    return w


def _small_grads(gr):
    g = {}
    g["a_norm"] = gr["a_norm"]
    g["a_conv_w"] = gr["a_conv_w"][None]
    g["a_conv_b"] = gr["a_conv_b"]
    g["a_dt_bias"], g["a_A_log"], g["a_D"] = gr["a_dt_bias"], gr["a_A_log"], gr["a_D"]
    g["a_gnorm"] = gr["a_gnorm"]
    g["kv_norm"] = gr["kv_norm"].reshape(-1)
    g["b_kv"] = gr["b_kv"].reshape(-1)
    g["k_norm"] = gr["k_norm"].reshape(-1)
    g["b_norm"] = gr["b_norm"]
    g["b_q"] = gr["b_q"]
    g["q_norm"] = gr["q_norm"]
    g["sinks"] = gr["sinks"]
    g["b_o"] = gr["b_o"]
    f = [gr["ffn0"], gr["ffn1"]]
    g["f_norm"] = jnp.concatenate([f[0]["f_norm"], f[1]["f_norm"]], axis=0)
    g["f_conv_w"] = jnp.stack([f[l]["f_conv_w"] for l in range(2)])
    g["f_conv_b"] = jnp.concatenate([f[l]["f_conv_b"] for l in range(2)], axis=0)
    return g


def _in_proj_grad(gr):
    return jnp.concatenate([gr["w_z"], gr["w_x"], gr["w_dt"][:, :SSM_HEADS]], axis=1)


def _full_grads(gr):
    g = _small_grads(gr)
    f32 = lambda t: t.astype(F32)
    g["a_in_proj"] = f32(_in_proj_grad(gr))[None]
    g["a_out_proj"] = f32(gr["a_out_proj"])[None]
    g["w_kv"] = f32(gr["w_kv"])
    g["w_q"] = f32(gr["w_q"])[None]
    g["w_o"] = f32(gr["w_o"])[None]
    f = [gr["ffn0"], gr["ffn1"]]
    g["f_w_in"] = jnp.stack([f32(f[l]["f_w_in"]).transpose(1, 0, 2).reshape(1024, -1) for l in range(2)])
    g["f_w_down"] = jnp.stack([f32(f[l]["f_w_down"]) for l in range(2)])
    return g


MESH = pl.DeviceIdType.MESH
WEIGHTS = ("a_norm", "a_in_proj", "a_conv_w", "a_conv_b", "a_dt_bias", "a_A_log", "a_D", "a_gnorm", "a_out_proj",
           "kv_norm", "w_kv", "b_kv", "k_norm", "b_norm", "w_q", "b_q", "q_norm", "sinks", "w_o", "b_o", "f_norm",
           "f_w_in", "f_conv_w", "f_conv_b", "f_w_down")
MATS = (("in_proj", "a_in_proj", 0), ("out_proj", "a_out_proj", 0), ("w_kv", "w_kv", None), ("w_q", "w_q", 0),
        ("w_o", "w_o", 0), ("f_in0", "f_w_in", 0), ("f_in1", "f_w_in", 1), ("f_down0", "f_w_down", 0),
        ("f_down1", "f_w_down", 1))
SMALL_CUT = (("a_norm", 1), ("a_conv_w", 2), ("a_conv_b", 1), ("a_gnorm", 1), ("f_conv_w", 2))
SMALL_REP = ("a_dt_bias", "a_A_log", "a_D", "kv_norm", "b_kv", "k_norm", "b_norm", "b_q", "q_norm", "sinks", "b_o",
             "f_norm", "f_conv_b")


def _coords():
    return lax.axis_index("x"), lax.axis_index("y"), lax.axis_index("c")


def _other_chips(x, y):
    return [(1 - x, y), (x, 1 - y), (1 - x, 1 - y)]


def _pack(arrs, rows_align, lanes, dtype):
    flat = jnp.concatenate([a.reshape(-1).astype(dtype) for a in arrs])
    per = rows_align * lanes
    total = -(-flat.shape[0] // per) * per
    return jnp.pad(flat, (0, total - flat.shape[0])).reshape(total // lanes, lanes)


def _unpack(flat, shapes):
    out, off = [], 0
    for s in shapes:
        n = math.prod(s)
        out.append(flat[off:off + n].reshape(s))
        off += n
    return out


def _remote(src, dst, send, recv, k, dev):
    return pltpu.make_async_remote_copy(src_ref=src, dst_ref=dst, send_sem=send.at[k], recv_sem=recv.at[k],
                                        device_id=dev, device_id_type=MESH)


_ANY = pl.BlockSpec(memory_space=pl.ANY)


def _halves(t):
    r, c = t.shape
    return t.reshape(2, r // 2, c)


def _gather_weights(shards, sp):
    n = len(shards)
    per = 9
    n_sem = per * n + 3

    def body(*refs):
        sh, sp_ref = refs[:n], refs[n]
        outs, sout = refs[n + 1:2 * n + 1], refs[2 * n + 1]
        send, recv, loc = refs[2 * n + 2:]
        x, y, c = _coords()
        me = 2 * x + y
        cx_, cy_, cd_ = _other_chips(x, y)
        ix, iy, idg = (2 * p[0] + p[1] for p in (cx_, cy_, cd_))
        to_x, to_y, sib = (*cx_, c), (*cy_, c), (x, y, 1 - c)
        l1 = pltpu.make_async_copy(sp_ref, sout.at[me], loc.at[0])
        l1.start()
        sends = [_remote(sp_ref, sout.at[me], send, recv, per * n + j, (*p, c)) for j, p in enumerate((cx_, cy_, cd_))]
        for t in range(n):
            sends.append(_remote(sh[t].at[c], outs[t].at[me, c], send, recv, per * t + 0, to_x))
            sends.append(_remote(sh[t].at[c], outs[t].at[me, c], send, recv, per * t + 1, to_y))
            sends.append(_remote(sh[t], outs[t].at[me], send, recv, per * t + 8, sib))
        for cp in sends:
            cp.start()

        def go(src, dst, k, dev):
            cp = _remote(src, dst, send, recv, k, dev)
            cp.start()
            sends.append(cp)

        def piece(t, owner, first):
            q = sh[t].shape[1] // 2
            return outs[t].at[owner, c, pl.ds(0 if first else q, q)]

        for t in range(n):
            _remote(sh[t].at[c], outs[t].at[ix, c], send, recv, per * t + 0, to_x).wait_recv()
            go(piece(t, ix, False), piece(t, ix, False), per * t + 3, to_y)
            go(outs[t].at[ix, c], outs[t].at[ix, c], per * t + 4, sib)
        for t in range(n):
            _remote(sh[t].at[c], outs[t].at[iy, c], send, recv, per * t + 1, to_y).wait_recv()
            go(piece(t, iy, True), piece(t, iy, True), per * t + 2, to_x)
            go(outs[t].at[iy, c], outs[t].at[iy, c], per * t + 5, sib)
        for t in range(n):
            _remote(piece(t, idg, True), piece(t, idg, True), send, recv, per * t + 2, to_x).wait_recv()
            go(piece(t, idg, True), piece(t, idg, True), per * t + 6, sib)
            _remote(piece(t, idg, False), piece(t, idg, False), send, recv, per * t + 3, to_y).wait_recv()
            go(piece(t, idg, False), piece(t, idg, False), per * t + 7, sib)
        for j, p in enumerate((cx_, cy_, cd_)):
            _remote(sp_ref, sout.at[2 * p[0] + p[1]], send, recv, per * n + j, (*p, c)).wait_recv()
        for t in range(n):
            q = sh[t].shape[1] // 2
            other = lambda owner, lo=None: outs[t].at[owner, 1 - c] if lo is None else outs[t].at[owner, 1 - c, pl.ds(lo, q)]
            _remote(other(ix), other(ix), send, recv, per * t + 4, sib).wait_recv()
            _remote(other(iy), other(iy), send, recv, per * t + 5, sib).wait_recv()
            _remote(other(idg, 0), other(idg, 0), send, recv, per * t + 6, sib).wait_recv()
            _remote(other(idg, q), other(idg, q), send, recv, per * t + 7, sib).wait_recv()
            _remote(sh[t], outs[t].at[me], send, recv, per * t + 8, sib).wait_recv()
        for cp in sends:
            cp.wait_send()
        l1.wait()

    res = pl.pallas_call(
        body, name="gather_weights", in_specs=[_ANY] * (n + 1), out_specs=[_ANY] * (n + 1),
        out_shape=[jax.ShapeDtypeStruct((N_CHIPS,) + t.shape, t.dtype) for t in shards]
        + [jax.ShapeDtypeStruct((N_CHIPS,) + sp.shape, sp.dtype)],
        scratch_shapes=[pltpu.SemaphoreType.DMA((n_sem,)), pltpu.SemaphoreType.DMA((n_sem,)),
                        pltpu.SemaphoreType.DMA((1,))],
    )(*shards, sp)
    return res[:n], res[n]


_HBM = pl.BlockSpec(memory_space=pltpu.HBM)
_SEMS = pl.BlockSpec(memory_space=pltpu.SEMAPHORE)
_DATAFLOW = pltpu.SideEffectType.DATAFLOW_SIDE_EFFECTING


def _in_hbm(a):
    return pltpu.with_memory_space_constraint(a, pltpu.HBM)


def _start_copies(copies, arrays, n_sem, after, *, name):
    n, na = len(arrays), len(after)

    def body(*refs):
        for mine, _ in copies(refs[:n], refs[n + na], refs[n + na + 1]):
            mine.start()
        refs[-1][...] = jnp.zeros_like(refs[-1])

    res = pl.pallas_call(
        body, name=name, in_specs=[_HBM] * n + [_ANY] * na,
        out_specs=[_SEMS, _SEMS] + [_HBM] * n + [pl.BlockSpec(memory_space=pltpu.VMEM)],
        out_shape=[pltpu.SemaphoreType.DMA((n_sem,)), pltpu.SemaphoreType.DMA((n_sem,))]
        + [pltpu.HBM(a.shape, a.dtype) for a in arrays] + [jax.ShapeDtypeStruct((8, 128), F32)],
        input_output_aliases={t: 2 + t for t in range(n)},
        compiler_params=pltpu.CompilerParams(has_side_effects=_DATAFLOW),
    )(*[_in_hbm(a) for a in arrays], *after)
    return res[0], res[1], list(res[2:2 + n]), res[-1]


def _wait_copies(copies, send, recv, arrays, after, *, name):
    n = len(arrays)

    def body(*refs):
        for mine, theirs in copies(refs[:n], refs[n], refs[n + 1]):
            mine.wait_send()
            theirs.wait_recv()

    return list(pl.pallas_call(
        body, name=name, in_specs=[_HBM] * n + [_SEMS, _SEMS] + [_ANY] * len(after), out_specs=[_HBM] * n,
        out_shape=[pltpu.HBM(a.shape, a.dtype) for a in arrays], input_output_aliases={t: t for t in range(n)},
        compiler_params=pltpu.CompilerParams(has_side_effects=_DATAFLOW),
    )(*arrays, send, recv, *after))


def _sibling_copies(refs, send, recv):
    n = len(refs) // 2
    x, y, c = _coords()
    cps = [_remote(refs[t].at[:, 1 - c], refs[n + t], send, recv, t, (x, y, 1 - c)) for t in range(n)]
    return [(cp, cp) for cp in cps]


def _join_copies(refs, send, recv):
    x, y, c = _coords()
    sib = (x, y, 1 - c)
    return [(_remote(o.at[c], o.at[c], send, recv, t, sib), _remote(o.at[1 - c], o.at[1 - c], send, recv, t, sib))
            for t, o in enumerate(refs)]


def _gather_copies(sh, land, send, recv):
    x, y, c = _coords()
    me = 2 * x + y
    out = []
    for t in range(len(sh)):
        for j, (cx, cy) in enumerate(_other_chips(x, y)):
            dev = (cx, cy, c)
            out.append((_remote(sh[t].at[c], land[t].at[me, c], send, recv, 4 * t + j, dev),
                        _remote(sh[t].at[c], land[t].at[2 * cx + cy, c], send, recv, 4 * t + j, dev)))
        sib = (x, y, 1 - c)
        out.append((_remote(sh[t], land[t].at[me], send, recv, 4 * t + 3, sib),
                    _remote(sh[t], land[t].at[me], send, recv, 4 * t + 3, sib)))
    return out


def _gather_start(shards, after, *, name):
    n = len(shards)

    def body(*refs):
        sh, land = refs[:n], refs[n:2 * n]
        send, recv = refs[2 * n + 1], refs[2 * n + 2]
        token = refs[-1]
        for mine, _ in _gather_copies(sh, land, send, recv):
            mine.start()
        token[...] = jnp.zeros_like(token)

    lands = [_in_hbm(lax.empty((N_CHIPS,) + s.shape, s.dtype)) for s in shards]
    res = pl.pallas_call(
        body, name=name, in_specs=[_HBM] * (2 * n) + [_ANY],
        out_specs=[_SEMS, _SEMS] + [_HBM] * (2 * n) + [pl.BlockSpec(memory_space=pltpu.VMEM)],
        out_shape=[pltpu.SemaphoreType.DMA((4 * n,)), pltpu.SemaphoreType.DMA((4 * n,))]
        + [pltpu.HBM(s.shape, s.dtype) for s in shards] + [pltpu.HBM(l.shape, l.dtype) for l in lands]
        + [jax.ShapeDtypeStruct((8, 128), F32)],
        input_output_aliases={t: 2 + t for t in range(2 * n)},
        compiler_params=pltpu.CompilerParams(has_side_effects=_DATAFLOW),
    )(*[_in_hbm(s) for s in shards], *lands, after)
    return res[0], res[1], res[2:2 + n], res[2 + n:2 + 2 * n], res[-1]


def _gather_wait(send, recv, shards, lands, after, *, name):
    n = len(shards)

    def body(*refs):
        sh, land = refs[:n], refs[n:2 * n]
        send_r, recv_r = refs[2 * n], refs[2 * n + 1]
        for mine, theirs in _gather_copies(sh, land, send_r, recv_r):
            mine.wait_send()
            theirs.wait_recv()

    res = pl.pallas_call(
        body, name=name, in_specs=[_HBM] * (2 * n) + [_SEMS, _SEMS, _ANY], out_specs=[_HBM] * (2 * n),
        out_shape=[pltpu.HBM(s.shape, s.dtype) for s in shards] + [pltpu.HBM(l.shape, l.dtype) for l in lands],
        input_output_aliases={t: t for t in range(2 * n)},
        compiler_params=pltpu.CompilerParams(has_side_effects=_DATAFLOW),
    )(*shards, *lands, send, recv, after)
    return res[n:]


def _forward_copies(refs, send, recv):
    x, y, c = _coords()
    sib = (x, y, 1 - c)
    srcs = [2 * cx + cy for cx, cy in _other_chips(x, y)]
    return [(_remote(o.at[s, c], o.at[s, c], send, recv, 3 * t + j, sib),
             _remote(o.at[s, 1 - c], o.at[s, 1 - c], send, recv, 3 * t + j, sib))
            for t, o in enumerate(refs) for j, s in enumerate(srcs)]


def _small_copies(v, land, send, recv):
    x, y, c = _coords()
    me = 4 * x + 2 * y + c
    out = []
    for k in range(1, 8):
        px = 1 - x if k & 4 else x
        py = 1 - y if k & 2 else y
        pc = 1 - c if k & 1 else c
        out.append((_remote(v, land.at[me], send, recv, k - 1, (px, py, pc)),
                    _remote(v, land.at[4 * px + 2 * py + pc], send, recv, k - 1, (px, py, pc))))
    return out


def _small_start(v, after, *, name):
    def body(v_ref, land_ref, after_ref, send, recv, v_thru, land_thru, token):
        for mine, _ in _small_copies(v_ref, land_ref, send, recv):
            mine.start()
        token[...] = jnp.zeros_like(token)

    land = _in_hbm(lax.empty((8,) + v.shape, v.dtype))
    return pl.pallas_call(
        body, name=name, in_specs=[_HBM, _HBM, _ANY],
        out_specs=[_SEMS, _SEMS, _HBM, _HBM, pl.BlockSpec(memory_space=pltpu.VMEM)],
        out_shape=[pltpu.SemaphoreType.DMA((7,)), pltpu.SemaphoreType.DMA((7,)), pltpu.HBM(v.shape, v.dtype),
                   pltpu.HBM(land.shape, land.dtype), jax.ShapeDtypeStruct((8, 128), F32)],
        input_output_aliases={0: 2, 1: 3}, compiler_params=pltpu.CompilerParams(has_side_effects=_DATAFLOW),
    )(_in_hbm(v), land, after)


def _small_wait(send, recv, v, land, after, *, name):
    def body(v_ref, land_ref, send_r, recv_r, *rest):
        for mine, theirs in _small_copies(v_ref, land_ref, send_r, recv_r):
            mine.wait_send()
            theirs.wait_recv()

    return pl.pallas_call(
        body, name=name, in_specs=[_HBM, _HBM, _SEMS, _SEMS] + [_ANY] * len(after), out_specs=[_HBM, _HBM],
        out_shape=[pltpu.HBM(v.shape, v.dtype), pltpu.HBM(land.shape, land.dtype)],
        input_output_aliases={0: 0, 1: 1}, compiler_params=pltpu.CompilerParams(has_side_effects=_DATAFLOW),
    )(v, land, send, recv, *after)


def _small_sum(v, land, me_idx, *, name="small_sum"):
    def body(me_ref, v_ref, land_ref, o_ref):
        acc = None
        for s in range(8):
            term = jnp.where(me_ref[0] == s, v_ref[...], land_ref[s])
            acc = term if acc is None else acc + term
        o_ref[...] = acc

    whole = lambda shape: pl.BlockSpec(shape, lambda i, me_ref: (0,) * len(shape))
    return pl.pallas_call(
        body, name=name,
        grid_spec=pltpu.PrefetchScalarGridSpec(num_scalar_prefetch=1, grid=(1,), in_specs=[whole(v.shape), whole(land.shape)],
                                               out_specs=whole(v.shape)),
        out_shape=jax.ShapeDtypeStruct(v.shape, F32), compiler_params=_cp(1),
    )(me_idx, v, land)


RS_ROW_SPLIT = 2


def _rs_add_pair(gs, as_, c_idx, *, name):
    n = len(gs)

    def body(c_ref, *refs):
        for t in range(n):
            refs[2 * n + t][...] = (refs[t][...].astype(F32) + refs[n + t][...].astype(F32)).astype(BF16)

    def gspec(g):
        _, _, rh, cols = g.shape
        return pl.BlockSpec((None, None, rh // RS_ROW_SPLIT, cols), lambda j, i, c_ref: (j, c_ref[0], i, 0))

    def pspec(g):
        _, _, rh, cols = g.shape
        return pl.BlockSpec((None, rh // RS_ROW_SPLIT, cols), lambda j, i, c_ref: (j, i, 0))

    return pl.pallas_call(
        body, name=name,
        grid_spec=pltpu.PrefetchScalarGridSpec(
            num_scalar_prefetch=1, grid=(N_CHIPS, RS_ROW_SPLIT),
            in_specs=[gspec(g) for g in gs] + [pspec(g) for g in gs], out_specs=[pspec(g) for g in gs]),
        out_shape=[jax.ShapeDtypeStruct((N_CHIPS,) + g.shape[2:], BF16) for g in gs], compiler_params=_cp(2),
    )(c_idx, *gs, *as_)


def _chips_copies(p, r, send, recv):
    x, y, c = _coords()
    return [_remote(p[t].at[2 * cx + cy], r[t].at[k], send, recv, 3 * t + k, (cx, cy, c))
            for k, (cx, cy) in enumerate(_other_chips(x, y)) for t in range(len(p))]


def _rs_chips_start(ps, after, *, name):
    n, na = len(ps), len(after)

    def body(*refs):
        p, r = refs[:n], refs[n:2 * n]
        send, recv = refs[2 * n + na], refs[2 * n + na + 1]
        token = refs[-1]
        for cp in _chips_copies(p, r, send, recv):
            cp.start()
        token[...] = jnp.zeros_like(token)

    lands = [_in_hbm(lax.empty((3,) + p.shape[1:], p.dtype)) for p in ps]
    res = pl.pallas_call(
        body, name=name, in_specs=[_HBM] * (2 * n) + [_ANY] * na,
        out_specs=[_SEMS, _SEMS] + [_HBM] * (2 * n) + [pl.BlockSpec(memory_space=pltpu.VMEM)],
        out_shape=[pltpu.SemaphoreType.DMA((3 * n,)), pltpu.SemaphoreType.DMA((3 * n,))]
        + [pltpu.HBM(p.shape, p.dtype) for p in ps] + [pltpu.HBM(l.shape, l.dtype) for l in lands]
        + [jax.ShapeDtypeStruct((8, 128), F32)],
        input_output_aliases={t: 2 + t for t in range(2 * n)},
        compiler_params=pltpu.CompilerParams(has_side_effects=_DATAFLOW),
    )(*[_in_hbm(p) for p in ps], *lands, *after)
    return res[0], res[1], res[2:2 + n], res[2 + n:2 + 2 * n], res[-1]


def _rs_chips_wait(send, recv, ps, lands, after, *, name):
    n = len(ps)

    def body(*refs):
        p, r = refs[:n], refs[n:2 * n]
        for cp in _chips_copies(p, r, refs[2 * n], refs[2 * n + 1]):
            cp.wait_send()
            cp.wait_recv()

    res = pl.pallas_call(
        body, name=name, in_specs=[_HBM] * (2 * n) + [_SEMS, _SEMS] + [_ANY] * len(after), out_specs=[_HBM] * (2 * n),
        out_shape=[pltpu.HBM(p.shape, p.dtype) for p in ps] + [pltpu.HBM(l.shape, l.dtype) for l in lands],
        input_output_aliases={t: t for t in range(2 * n)},
        compiler_params=pltpu.CompilerParams(has_side_effects=_DATAFLOW),
    )(*ps, *lands, send, recv, *after)
    return res[:n], res[n:]


def _rs_add_chips(ps, rs, idx, *, name):
    n = len(ps)

    def body(idx_ref, *refs):
        for t in range(n):
            p_ref, r0, r1, r2 = refs[4 * t:4 * t + 4]
            refs[4 * n + t][...] = ((p_ref[...].astype(F32) + r0[...].astype(F32)) + r1[...].astype(F32)) + r2[...].astype(F32)

    in_specs, args = [], []
    for p, r in zip(ps, rs):
        _, rh, cols = p.shape
        blk = (None, rh // RS_ROW_SPLIT, cols)
        in_specs.append(pl.BlockSpec(blk, lambda i, idx_ref: (idx_ref[0], i, 0)))
        in_specs += [pl.BlockSpec(blk, lambda i, idx_ref, k=k: (k, i, 0)) for k in range(3)]
        args += [p, r, r, r]
    out_specs = [pl.BlockSpec((None, p.shape[1] // RS_ROW_SPLIT, p.shape[2]), lambda i, idx_ref: (idx_ref[1], i, 0))
                 for p in ps]
    return pl.pallas_call(
        body, name=name,
        grid_spec=pltpu.PrefetchScalarGridSpec(num_scalar_prefetch=1, grid=(RS_ROW_SPLIT,), in_specs=in_specs,
                                               out_specs=out_specs),
        out_shape=[jax.ShapeDtypeStruct((2,) + p.shape[1:], F32) for p in ps], compiler_params=_cp(1),
    )(idx, *args)


def _adamw(w, gs, m, v, *, name, dep=None):
    L, Rr, C = w.shape
    tr, tc = _pick(Rr, (256, 128, 64)), C
    if tr == Rr and Rr * C > 512 * 1024:
        tc = 256
    bc1 = 1.0 - ADAM_B1 ** ADAM_STEP
    bc2 = 1.0 - ADAM_B2 ** ADAM_STEP
    nd = 0 if dep is None else 1

    def body(*refs):
        w_ref, m_ref, v_ref = refs[0], refs[1], refs[2]
        g_refs = refs[3:3 + L]
        d_ref, mo_ref, vo_ref, go_ref = refs[3 + L + nd:]
        layer = pl.program_id(0)
        gv = g_refs[0][...]
        for q in range(1, L):
            gv = jnp.where(layer == q, g_refs[q][...], gv)
        mn = ADAM_B1 * m_ref[...] + (1.0 - ADAM_B1) * gv
        vn = ADAM_B2 * v_ref[...] + (1.0 - ADAM_B2) * (gv * gv)
        go_ref[...] = gv
        mo_ref[...] = mn
        vo_ref[...] = vn
        d_ref[...] = -ADAM_LR * ((mn / bc1) / (jnp.sqrt(vn / bc2) + ADAM_EPS) + ADAM_WD * w_ref[...])

    blk = pl.BlockSpec((None, tr, tc), lambda l, i, j: (l, i, j))
    gblks = [pl.BlockSpec((tr, tc), lambda l, i, j, q=q: (jnp.where(l == q, i, 0), jnp.where(l == q, j, 0))) for q in range(L)]
    return pl.pallas_call(
        body, name=name, grid=(L, Rr // tr, C // tc), in_specs=[blk] * 3 + gblks + [_ANY] * nd, out_specs=[blk] * 4,
        out_shape=[jax.ShapeDtypeStruct((L, Rr, C), F32)] * 4, compiler_params=_cp(3),
    )(w, m, v, *gs, *([] if dep is None else [dep]))


def kernel(x, positions, a_norm, a_in_proj, a_conv_w, a_conv_b, a_dt_bias, a_A_log, a_D, a_gnorm, a_out_proj,
           kv_norm, w_kv, b_kv, k_norm, b_norm, w_q, b_q, q_norm, sinks, w_o, b_o, f_norm, f_w_in, f_conv_w,
           f_conv_b, f_w_down, loss_target, m_a_norm, m_a_in_proj, m_a_conv_w, m_a_conv_b, m_a_dt_bias, m_a_A_log,
           m_a_D, m_a_gnorm, m_a_out_proj, m_kv_norm, m_w_kv, m_b_kv, m_k_norm, m_b_norm, m_w_q, m_b_q, m_q_norm,
           m_sinks, m_w_o, m_b_o, m_f_norm, m_f_w_in, m_f_conv_w, m_f_conv_b, m_f_w_down, v_a_norm, v_a_in_proj,
           v_a_conv_w, v_a_conv_b, v_a_dt_bias, v_a_A_log, v_a_D, v_a_gnorm, v_a_out_proj, v_kv_norm, v_w_kv,
           v_b_kv, v_k_norm, v_b_norm, v_w_q, v_b_q, v_q_norm, v_sinks, v_w_o, v_b_o, v_f_norm, v_f_w_in,
           v_f_conv_w, v_f_conv_b, v_f_w_down):
    wl = dict(zip(WEIGHTS, (a_norm, a_in_proj, a_conv_w, a_conv_b, a_dt_bias, a_A_log, a_D, a_gnorm, a_out_proj,
                            kv_norm, w_kv, b_kv, k_norm, b_norm, w_q, b_q, q_norm, sinks, w_o, b_o, f_norm, f_w_in,
                            f_conv_w, f_conv_b, f_w_down)))
    ml = dict(zip(WEIGHTS, (m_a_norm, m_a_in_proj, m_a_conv_w, m_a_conv_b, m_a_dt_bias, m_a_A_log, m_a_D, m_a_gnorm,
                            m_a_out_proj, m_kv_norm, m_w_kv, m_b_kv, m_k_norm, m_b_norm, m_w_q, m_b_q, m_q_norm,
                            m_sinks, m_w_o, m_b_o, m_f_norm, m_f_w_in, m_f_conv_w, m_f_conv_b, m_f_w_down)))
    vl = dict(zip(WEIGHTS, (v_a_norm, v_a_in_proj, v_a_conv_w, v_a_conv_b, v_a_dt_bias, v_a_A_log, v_a_D, v_a_gnorm,
                            v_a_out_proj, v_kv_norm, v_w_kv, v_b_kv, v_k_norm, v_b_norm, v_w_q, v_b_q, v_q_norm,
                            v_sinks, v_w_o, v_b_o, v_f_norm, v_f_w_in, v_f_conv_w, v_f_conv_b, v_f_w_down)))
    xi, yi, ci = _coords()
    me = 2 * xi + yi
    S = x.shape[1]

    def block_of(n, layer):
        t = wl[n]
        return t if layer is None else t[layer]

    rows = lambda t: t.reshape(-1, t.shape[-1])
    c_idx = jnp.reshape(ci, (1,)).astype(jnp.int32)
    me_c = jnp.stack([me, ci]).astype(jnp.int32)
    early = ("in_proj", "out_proj")
    late = (("f_in0", "f_down0"), ("w_kv", "w_q", "w_o", "f_in1", "f_down1"))
    shards = {name: _halves(block_of(wn, layer).astype(BF16)) for name, wn, layer in MATS}

    sp = _pack([wl[n] for n, _ in SMALL_CUT], 8, 128, F32)
    gathered, gs = _gather_weights([shards[k] for k in early], sp)
    gt = {k: t.reshape(N_CHIPS, -1, t.shape[-1]) for k, t in zip(early, gathered)}
    started = [_gather_start([shards[k] for k in late[0]], gs, name="gather_late_start0")]
    started.append(_gather_start([shards[k] for k in late[1]], started[0][4], name="gather_late_start1"))
    full = {n: wl[n] for n in SMALL_REP}
    gs = gs.reshape(N_CHIPS, -1)
    pieces = [_unpack(gs[j], [wl[n].shape for n, _ in SMALL_CUT]) for j in range(N_CHIPS)]
    for q, (n, ax) in enumerate(SMALL_CUT):
        full[n] = jnp.concatenate([pieces[j][q] for j in range(N_CHIPS)], axis=ax)
    w = _prep_small(full, {})
    w["w_zx"], w["w_dt"] = _join_in_proj(gt["in_proj"])
    w["a_out_proj"] = rows(gt["out_proj"])
    w["dep"] = started[1][4]

    class Comm:
        flight = []
        reduced = {}

        forwarding = {}

        def late_arrived(self, part, after):
            send, recv, shs, lands, _ = started[part]
            lands = _gather_wait(send, recv, shs, lands, after[0], name=f"gather_late_wait{part}")
            send, recv, lands, token = _start_copies(_forward_copies, list(lands), 3 * len(lands), after,
                                                     name=f"gather_late_forward_start{part}")
            self.forwarding[part] = (send, recv, lands)
            return token

        def late_weights(self, w, after, part):
            send, recv, lands = self.forwarding[part]
            lands = _wait_copies(_forward_copies, send, recv, lands, [after], name=f"gather_late_forward_wait{part}")
            lt = {k: t.reshape(N_CHIPS, -1, t.shape[-1]) for k, t in zip(late[part], lands)}
            w = dict(w)
            if part == 0:
                w["f_w_in"], w["f_w_down"] = [lt["f_in0"]], [rows(lt["f_down0"])]
            else:
                w["w_kv"], w["w_q"], w["w_o"] = (rows(lt[k]) for k in ("w_kv", "w_q", "w_o"))
                w["f_w_in"], w["f_w_down"] = w["f_w_in"] + [lt["f_in1"]], w["f_w_down"] + [rows(lt["f_down1"])]
            return w

        def advance(self, after, group=None, tensors=None):
            token = None
            for grp in list(self.flight):
                tag, n = grp["tag"], len(grp["names"])
                dep = list(after) + ([] if token is None else [token])
                if grp["stage"] == "sibling":
                    arrs = _wait_copies(_sibling_copies, grp["send"], grp["recv"], grp["arrays"], dep, name=f"rs_sibling_wait{tag}")
                    pairs = _rs_add_pair(arrs[:n], arrs[n:], c_idx, name=f"rs_add_pair{tag}")
                    send, recv, ps, lands, token = _rs_chips_start(pairs, dep, name=f"rs_chips_start{tag}")
                    grp.update(stage="chips", send=send, recv=recv, ps=ps, lands=lands)
                elif grp["stage"] == "chips":
                    ps, rs = _rs_chips_wait(grp["send"], grp["recv"], grp["ps"], grp["lands"], dep, name=f"rs_chips_wait{tag}")
                    halves = _rs_add_chips(ps, rs, me_c, name=f"rs_add_chips{tag}")
                    send, recv, arrs, token = _start_copies(_join_copies, halves, n, dep, name=f"rs_join_start{tag}")
                    grp.update(stage="join", send=send, recv=recv, arrays=arrs)
                else:
                    joined = _wait_copies(_join_copies, grp["send"], grp["recv"], grp["arrays"], dep, name=f"rs_join_wait{tag}")
                    self.reduced.update({k: rows(t) for k, t in zip(grp["names"], joined)})
                    self.flight.remove(grp)
            if group is not None:
                names = list(tensors)
                glist = [tensors[k].reshape(N_CHIPS, 2, -1, tensors[k].shape[-1]) for k in names]
                lands = [lax.empty((N_CHIPS,) + gq.shape[2:], gq.dtype) for gq in glist]
                dep = list(after) + ([] if token is None else [token])
                send, recv, arrs, token = _start_copies(_sibling_copies, glist + lands, len(names), dep,
                                                        name=f"rs_sibling_start{group}")
                self.flight.append(dict(tag=group, names=names, stage="sibling", send=send, recv=recv, arrays=arrs))
            return token

    comm = Comm()

    posf = positions.reshape(S, 1).astype(F32)
    loss_part, dx0, gr, tok = _local_step(x[0], posf, loss_target[0], w, comm)
    g = _small_grads(gr)

    small_names = [n for n, _ in SMALL_CUT] + list(SMALL_REP)
    sv = _pack([g[n] for n in small_names] + [loss_part[0:1, 0:1]], 8, 128, F32)
    s_send, s_recv, sv, s_land, s_token = _small_start(sv, tok, name="small_start")

    grads, delta, new_m, new_v = {}, {}, {}, {}

    def update(wn, dep):
        gl = [comm.reduced[name] for name, n2, _ in MATS if n2 == wn]
        shp = wl[wn].shape
        three = (len(gl),) + gl[0].shape
        flip = shp[-1] % 128 != 0
        view = (lambda t: t.reshape(three).transpose(0, 2, 1)) if flip else (lambda t: t.reshape(three))
        back = (lambda t: t.transpose(0, 2, 1).reshape(shp)) if flip else (lambda t: t.reshape(shp))
        if flip:
            gl = [t.T for t in gl]
        d, mn, vn, go = _adamw(view(wl[wn]), gl, view(ml[wn]), view(vl[wn]), name="adamw_" + wn, dep=dep)
        grads[wn], delta[wn], new_m[wn], new_v[wn] = back(go), back(d), back(mn), back(vn)
        return d

    first = [update(wn, s_token) for wn in ("w_q", "w_o", "w_kv")]
    tok = comm.advance(first)
    second = [update(wn, tok) for wn in ("f_w_in", "f_w_down", "a_out_proj")]
    comm.advance(second)
    comm.advance(second)
    update("a_in_proj", None)
    done = first + second

    sv, s_land = _small_wait(s_send, s_recv, sv, s_land, done, name="small_wait")
    sred = _small_sum(sv, s_land, jnp.reshape(2 * me + ci, (1,)).astype(jnp.int32)).reshape(-1)
    small_shapes = [g[n].shape for n in small_names] + [(1,)]
    sg = dict(zip(small_names + ["loss"], _unpack(sred, small_shapes)))
    loss = sg["loss"].reshape(())
    g_small = {}
    for n, ax in SMALL_CUT:
        size = wl[n].shape[ax]
        g_small[n] = lax.dynamic_slice_in_dim(sg[n], me * size, size, axis=ax)
    for n in SMALL_REP:
        g_small[n] = sg[n].reshape(wl[n].shape)

    pk = lambda d: _pack([d[n] for n in small_names], 8, 128, F32)[None]
    d, mn, vn, _ = _adamw(pk(wl), [pk(g_small)[0]], pk(ml), pk(vl), name="adamw_small")
    shapes = [wl[n].shape for n in small_names]
    for n, dd, mm, vv in zip(small_names, _unpack(d.reshape(-1), shapes), _unpack(mn.reshape(-1), shapes),
                             _unpack(vn.reshape(-1), shapes)):
        grads[n], delta[n], new_m[n], new_v[n] = g_small[n], dd, mm, vv

    return (loss, dx0[None], *[grads[n] for n in WEIGHTS], *[delta[n] for n in WEIGHTS],
            *[new_m[n] for n in WEIGHTS], *[new_v[n] for n in WEIGHTS])
```

```python
import math

import jax
import jax.numpy as jnp
from jax import lax
from jax.experimental import pallas as pl
from jax.experimental.pallas import tpu as pltpu

F32 = jnp.float32
BF16 = jnp.bfloat16

EPS = 1e-5
CHUNK = 256
WINDOW = 128
HEAD = 64
SSM_HEADS = 32
SSM_GROUPS = 8
SSM_STATE = 128
ATT_KV = 4
ATT_G = 4
ROPE_THETA = 10000.0
NEG = -1e30
N_CHIPS = 4
VMEM_LIMIT = 56 * 1024 * 1024

ADAM_LR, ADAM_B1, ADAM_B2, ADAM_EPS, ADAM_WD, ADAM_STEP = 0.001, 0.9, 0.999, 1e-08, 0.01, 10


def _cp(n_axes):
    return pltpu.CompilerParams(dimension_semantics=("arbitrary",) * n_axes, vmem_limit_bytes=VMEM_LIMIT)


def _pick(dim, prefs):
    for p in prefs:
        if dim % p == 0:
            return p
    return dim


def _iota(shape, dim):
    return lax.broadcasted_iota(jnp.int32, shape, dim)


def _dot(a, b, ca=1, cb=0):
    return lax.dot_general(a, b, (((ca,), (cb,)), ((), ())), preferred_element_type=F32)


def _dot3(x, ind):
    h = x.astype(BF16)
    r = x - h.astype(F32)
    m = r.astype(BF16)
    lo = (r - m.astype(F32)).astype(BF16)
    return _dot(h, ind) + _dot(m, ind) + _dot(lo, ind)


def _sigmoid(x):
    return jax.nn.sigmoid(x)


def _mm(a, b, *, name, ta=False, tb=False, bias=None, res=None, out_dtype=F32, b_koff=0, tm=None, tn=None, tk=None,
        dims=None, a_spec=None, b_spec=None, o_spec=None, o_shape=None, dep=None, more=(), target=None):
    if dims is not None:
        M, N, K = dims
    else:
        if ta:
            K, M = a.shape
        else:
            M, K = a.shape
        N = b.shape[0] if tb else b.shape[1]
    tm = tm or _pick(M, (1024, 1408, 512, 256, 128))
    tn = tn or _pick(N, (512, 1408, 256, 128))
    tk = tk or (K if K <= 2048 else _pick(K, (2048, 1408, 1024, 512)))
    assert M % tm == 0 and N % tn == 0 and K % tk == 0 and b_koff % tk == 0
    nk = K // tk
    kb0 = b_koff // tk
    has_bias, has_res = bias is not None, res is not None

    def body(*refs):
        a_ref, b_ref = refs[0], refs[1]
        pos = 2
        bias_ref = res_ref = acc_ref = None
        if has_bias:
            bias_ref = refs[pos]
            pos += 1
        if has_res:
            res_ref = refs[pos]
            pos += 1
        if dep is not None:
            pos += 1
        extra = refs[pos:pos + 2 * len(more)]
        pos += 2 * len(more)
        tgt_ref = lp_ref = None
        if target is not None:
            tgt_ref, o_ref, lp_ref = refs[pos], refs[pos + 1], refs[pos + 2]
            pos += 2
        else:
            o_ref = refs[pos]
        if nk > 1:
            acc_ref = refs[pos + 1]
        part = _dot(a_ref[...].astype(BF16), b_ref[...].astype(BF16), 0 if ta else 1, 1 if tb else 0)
        for q in range(len(more)):
            part = part + _dot(extra[2 * q][...].astype(BF16), extra[2 * q + 1][...].astype(BF16),
                               0 if ta else 1, 1 if tb else 0)

        def finish(acc):
            if has_bias:
                acc = acc + bias_ref[...]
            if has_res:
                acc = acc + res_ref[...]
            if target is not None:
                err = acc - tgt_ref[...]
                acc = err * (1.0 / N)
                part_loss = jnp.sum(jnp.sum(err * err, axis=1, keepdims=True), axis=0, keepdims=True) * (0.5 / N)
                first = (pl.program_id(0) == 0) & (pl.program_id(1) == 0)

                @pl.when(first)
                def _():
                    lp_ref[...] = jnp.broadcast_to(part_loss, lp_ref.shape)

                @pl.when(jnp.logical_not(first))
                def _():
                    lp_ref[...] += jnp.broadcast_to(part_loss, lp_ref.shape)

            o_ref[...] = acc.astype(out_dtype)

        if nk == 1:
            finish(part)
        else:
            k = pl.program_id(2)

            @pl.when(k == 0)
            def _():
                acc_ref[...] = part

            @pl.when(k > 0)
            def _():
                acc_ref[...] += part

            @pl.when(k == nk - 1)
            def _():
                finish(acc_ref[...])

    if a_spec is None:
        a_spec = pl.BlockSpec((tk, tm), lambda i, j, k: (k, i)) if ta else pl.BlockSpec((tm, tk), lambda i, j, k: (i, k))
    if b_spec is None:
        b_spec = (pl.BlockSpec((tn, tk), lambda i, j, k: (j, k + kb0)) if tb
                  else pl.BlockSpec((tk, tn), lambda i, j, k: (k + kb0, j)))
    if o_spec is None:
        o_spec = pl.BlockSpec((tm, tn), lambda i, j, k: (i, j))
    in_specs, args = [a_spec, b_spec], [a, b]
    if has_bias:
        in_specs.append(pl.BlockSpec((1, tn), lambda i, j, k: (0, j)))
        args.append(bias)
    if has_res:
        in_specs.append(pl.BlockSpec((tm, tn), lambda i, j, k: (i, j)))
        args.append(res)
    if dep is not None:
        in_specs.append(pl.BlockSpec(memory_space=pl.ANY))
        args.append(dep)
    for piece in more:
        a2, sa, b2, sb = piece if len(piece) == 4 else (a, piece[0], b, piece[1])
        in_specs += [sa, sb]
        args += [a2, b2]
    out_specs, out_shape = o_spec, jax.ShapeDtypeStruct(o_shape or (M, N), out_dtype)
    if target is not None:
        in_specs.append(pl.BlockSpec((tm, tn), lambda i, j, k: (i, j)))
        args.append(target)
        out_specs = [o_spec, pl.BlockSpec((8, 128), lambda i, j, k: (0, 0))]
        out_shape = [out_shape, jax.ShapeDtypeStruct((8, 128), F32)]
    return pl.pallas_call(
        body, name=name, grid=(M // tm, N // tn, nk), in_specs=in_specs, out_specs=out_specs, out_shape=out_shape,
        scratch_shapes=[pltpu.VMEM((tm, tn), F32)] if nk > 1 else [],
        compiler_params=_cp(3),
    )(*args)


def _norm_mm(x, gain, b, *, name, bias=None, N=None, tn=None, b_spec=None, dep=None):
    M, K = x.shape
    N = N or b.shape[1]
    tm = _pick(M, (1024, 512, 256))
    tn = tn or _pick(N, (512, 1408, 256, 128))
    has_bias = bias is not None

    def body(*refs):
        x_ref, g_ref, b_ref = refs[:3]
        pos = 3 + (1 if has_bias else 0) + (0 if dep is None else 1)
        o_ref, h_ref = refs[pos], refs[pos + 1]

        @pl.when(pl.program_id(1) == 0)
        def _():
            xv = x_ref[...]
            h_ref[...] = (xv * lax.rsqrt(jnp.mean(xv * xv, axis=-1, keepdims=True) + EPS) * g_ref[...]).astype(BF16)

        acc = _dot(h_ref[...], b_ref[...].astype(BF16))
        if has_bias:
            acc = acc + refs[3][...]
        o_ref[...] = acc

    in_specs = [pl.BlockSpec((tm, K), lambda i, j: (i, 0)), pl.BlockSpec((1, K), lambda i, j: (0, 0)),
                b_spec or pl.BlockSpec((K, tn), lambda i, j: (0, j))]
    args = [x, gain, b]
    if has_bias:
        in_specs.append(pl.BlockSpec((1, tn), lambda i, j: (0, j)))
        args.append(bias)
    if dep is not None:
        in_specs.append(pl.BlockSpec(memory_space=pl.ANY))
        args.append(dep)
    return pl.pallas_call(
        body, name=name, grid=(M // tm, N // tn), in_specs=in_specs,
        out_specs=[pl.BlockSpec((tm, tn), lambda i, j: (i, j)), pl.BlockSpec((tm, K), lambda i, j: (i, 0))],
        out_shape=[jax.ShapeDtypeStruct((M, N), F32), jax.ShapeDtypeStruct((M, K), BF16)], compiler_params=_cp(2),
    )(*args)


def _rms_bwd(x, gains, dhs, dres, *, name, tr=256, want_colsum=False):
    S, D = x.shape
    n = len(gains)
    steps = S // tr

    def body(*refs):
        x_ref = refs[0]
        g_refs = refs[1:1 + n]
        dh_refs = refs[1 + n:1 + 2 * n]
        dres_ref = refs[1 + 2 * n]
        dx_ref = refs[2 + 2 * n]
        dg_refs = refs[3 + 2 * n:3 + 3 * n]
        cs_ref = refs[3 + 3 * n] if want_colsum else None
        i = pl.program_id(0)
        xv = x_ref[...]
        r = lax.rsqrt(jnp.mean(xv * xv, axis=-1, keepdims=True) + EPS)
        xh = xv * r
        dx = dres_ref[...]
        for q in range(n):
            dh = dh_refs[q][...]
            dxh = dh * g_refs[q][...]
            dx = dx + r * (dxh - xh * jnp.mean(dxh * xh, axis=-1, keepdims=True))
            part = jnp.sum(dh * xh, axis=0, keepdims=True)

            @pl.when(i == 0)
            def _():
                dg_refs[q][...] = part

            @pl.when(i > 0)
            def _():
                dg_refs[q][...] += part

        dx_ref[...] = dx
        if want_colsum:
            cpart = jnp.sum(dx, axis=0, keepdims=True)

            @pl.when(i == 0)
            def _():
                cs_ref[...] = cpart

            @pl.when(i > 0)
            def _():
                cs_ref[...] += cpart

    row = pl.BlockSpec((tr, D), lambda i: (i, 0))
    vec = pl.BlockSpec((1, D), lambda i: (0, 0))
    n_vec_out = n + (1 if want_colsum else 0)
    outs = pl.pallas_call(
        body, name=name, grid=(steps,), in_specs=[row] + [vec] * n + [row] * n + [row],
        out_specs=[row] + [vec] * n_vec_out,
        out_shape=[jax.ShapeDtypeStruct((S, D), F32)] + [jax.ShapeDtypeStruct((1, D), F32)] * n_vec_out,
        compiler_params=_cp(1),
    )(x, *gains, *dhs, dres)
    return outs


def _colsum(x, *, name, tr=256):
    S, D = x.shape

    def body(x_ref, o_ref):
        i = pl.program_id(0)
        part = jnp.sum(x_ref[...].astype(F32), axis=0, keepdims=True)

        @pl.when(i == 0)
        def _():
            o_ref[...] = part

        @pl.when(i > 0)
        def _():
            o_ref[...] += part

    return pl.pallas_call(
        body, name=name, grid=(S // tr,), in_specs=[pl.BlockSpec((tr, D), lambda i: (i, 0))],
        out_specs=pl.BlockSpec((1, D), lambda i: (0, 0)), out_shape=jax.ShapeDtypeStruct((1, D), F32),
        compiler_params=_cp(1),
    )(x)


STRIP = 64
HALO = 8


def _strips(S, tc):
    return [(r0, slice(l0, l0 + 128)) for l0 in range(0, tc, 128) for r0 in range(S - STRIP, -1, -STRIP)]


def _with_halo(ref, r0, ls):
    if r0 == 0:
        return jnp.concatenate([jnp.zeros((HALO, 128), F32), ref[0:STRIP, ls]], axis=0)
    return ref[r0 - HALO:r0 + STRIP, ls]


def _conv_strip(xw, w_ref, b_ref, ls, width):
    acc = b_ref[:, ls] + w_ref[pl.ds(width - 1, 1), ls] * xw[HALO:]
    shifted = []
    for s in range(1, width):
        xs = pltpu.roll(xw, s, axis=0)[HALO:]
        shifted.append(xs)
        acc = acc + w_ref[pl.ds(width - 1 - s, 1), ls] * xs
    return acc, shifted


def _conv_strip_back(dacc, after, xc, shifted, w_ref, ls, width):
    ext = jnp.concatenate([dacc, after], axis=0)
    dx = w_ref[pl.ds(width - 1, 1), ls] * dacc
    dws = [None] * width
    dws[width - 1] = jnp.sum(dacc * xc, axis=0, keepdims=True)
    for s in range(1, width):
        dx = dx + w_ref[pl.ds(width - 1 - s, 1), ls] * pltpu.roll(ext, STRIP + HALO - s, axis=0)[:STRIP]
        dws[width - 1 - s] = jnp.sum(dacc * shifted[s - 1], axis=0, keepdims=True)
    return dx, dws, jnp.sum(dacc, axis=0, keepdims=True)


def _conv_back_block(S, tc, width, w_ref, b_ref, x_ref, dacc_of, dx_store, dw_ref, db_ref):
    for l0 in range(0, tc, 128):
        ls = slice(l0, l0 + 128)
        after = jnp.zeros((HALO, 128), F32)
        tot = None
        for r0 in range(S - STRIP, -1, -STRIP):
            xw = _with_halo(x_ref, r0, ls)
            acc, shifted = _conv_strip(xw, w_ref, b_ref, ls, width)
            dacc = dacc_of(r0, ls, acc, _sigmoid(acc))
            dx, dws, db = _conv_strip_back(dacc, after, xw[HALO:], shifted, w_ref, ls, width)
            dx_store(r0, ls, dx)
            after = dacc[:HALO]
            part = dws + [db]
            tot = part if tot is None else [p + q for p, q in zip(tot, part)]
        for k in range(width):
            dw_ref[pl.ds(k, 1), ls] = tot[k]
        db_ref[:, ls] = tot[width]


def _conv_silu_fwd(xin, col0, C, w, b, *, name, tc=512):
    S = xin.shape[0]
    width = w.shape[0]
    off = col0 // tc

    def body(x_ref, w_ref, b_ref, o_ref):
        for r0, ls in _strips(S, tc):
            acc, _ = _conv_strip(_with_halo(x_ref, r0, ls), w_ref, b_ref, ls, width)
            o_ref[r0:r0 + STRIP, ls] = acc * _sigmoid(acc)

    return pl.pallas_call(
        body, name=name, grid=(C // tc,),
        in_specs=[pl.BlockSpec((S, tc), lambda j: (0, j + off)), pl.BlockSpec((width, tc), lambda j: (0, j)),
                  pl.BlockSpec((1, tc), lambda j: (0, j))],
        out_specs=pl.BlockSpec((S, tc), lambda j: (0, j)), out_shape=jax.ShapeDtypeStruct((S, C), F32),
        compiler_params=_cp(1),
    )(xin, w, b)


def _conv_silu_bwd(xin, col0, C, w, b, douts, *, name, tc=256):
    S = xin.shape[0]
    width = w.shape[0]
    off = col0 // tc
    nd = len(douts)
    ranges = [(o // tc, (o + d.shape[1]) // tc) for d, o in douts]

    def body(*refs):
        x_ref, w_ref, b_ref = refs[0], refs[1], refs[2]
        d_refs = refs[3:3 + nd]
        dx_ref, dw_ref, db_ref = refs[3 + nd], refs[4 + nd], refs[5 + nd]
        j = pl.program_id(0)

        def dacc_of(r0, ls, acc, sg):
            dout = jnp.zeros((STRIP, 128), F32)
            for q in range(nd):
                lo, hi = ranges[q]
                dout = dout + jnp.where((j >= lo) & (j < hi), d_refs[q][r0:r0 + STRIP, ls], 0.0)
            return dout * (sg * (1.0 + acc * (1.0 - sg)))

        def dx_store(r0, ls, dx):
            dx_ref[r0:r0 + STRIP, ls] = dx.astype(BF16)

        _conv_back_block(S, tc, width, w_ref, b_ref, x_ref, dacc_of, dx_store, dw_ref, db_ref)

    d_specs = [pl.BlockSpec((S, tc), (lambda j, lo=lo, hi=hi: (0, jnp.clip(j - lo, 0, hi - lo - 1)))) for lo, hi in ranges]
    return pl.pallas_call(
        body, name=name, grid=(C // tc,),
        in_specs=[pl.BlockSpec((S, tc), lambda j: (0, j + off)), pl.BlockSpec((width, tc), lambda j: (0, j)),
                  pl.BlockSpec((1, tc), lambda j: (0, j))] + d_specs,
        out_specs=[pl.BlockSpec((S, tc), lambda j: (0, j)), pl.BlockSpec((width, tc), lambda j: (0, j)),
                   pl.BlockSpec((1, tc), lambda j: (0, j))],
        out_shape=[jax.ShapeDtypeStruct((S, C), BF16), jax.ShapeDtypeStruct((width, C), F32),
                   jax.ShapeDtypeStruct((1, C), F32)],
        compiler_params=_cp(1),
    )(xin, w, b, *[d for d, _ in douts])


def _ffn_act_fwd(u, w, b, *, name, tc=256):
    S, F2 = u.shape
    Fd = F2 // 2
    width = w.shape[0]
    nb = Fd // tc

    def body(g_ref, v_ref, w_ref, b_ref, o_ref):
        for r0, ls in _strips(S, tc):
            acc, _ = _conv_strip(_with_halo(g_ref, r0, ls), w_ref, b_ref, ls, width)
            o_ref[r0:r0 + STRIP, ls] = (acc * _sigmoid(acc) * v_ref[r0:r0 + STRIP, ls]).astype(BF16)

    return pl.pallas_call(
        body, name=name, grid=(nb,),
        in_specs=[pl.BlockSpec((S, tc), lambda j: (0, j)), pl.BlockSpec((S, tc), lambda j: (0, j + nb)),
                  pl.BlockSpec((width, tc), lambda j: (0, j)), pl.BlockSpec((1, tc), lambda j: (0, j))],
        out_specs=pl.BlockSpec((S, tc), lambda j: (0, j)), out_shape=jax.ShapeDtypeStruct((S, Fd), BF16),
        compiler_params=_cp(1),
    )(u, u, w, b)


def _ffn_act_bwd(u, w, b, da, *, name, tc=256):
    S, F2 = u.shape
    Fd = F2 // 2
    width = w.shape[0]
    nb = Fd // tc

    def body(g_ref, v_ref, w_ref, b_ref, da_ref, du_ref, dw_ref, db_ref, a_ref):
        def dacc_of(r0, ls, acc, sg):
            rs = slice(r0, r0 + STRIP)
            dav, val, silu = da_ref[rs, ls], v_ref[rs, ls], acc * sg
            a_ref[rs, ls] = (silu * val).astype(BF16)
            du_ref[1, rs, ls] = (dav * silu).astype(BF16)
            return dav * val * (sg * (1.0 + acc * (1.0 - sg)))

        def dx_store(r0, ls, dx):
            du_ref[0, r0:r0 + STRIP, ls] = dx.astype(BF16)

        _conv_back_block(S, tc, width, w_ref, b_ref, g_ref, dacc_of, dx_store, dw_ref, db_ref)

    blk = pl.BlockSpec((S, tc), lambda j: (0, j))
    return pl.pallas_call(
        body, name=name, grid=(nb,),
        in_specs=[blk, pl.BlockSpec((S, tc), lambda j: (0, j + nb)), pl.BlockSpec((width, tc), lambda j: (0, j)),
                  pl.BlockSpec((1, tc), lambda j: (0, j)), blk],
        out_specs=[pl.BlockSpec((2, S, tc), lambda j: (0, 0, j)), pl.BlockSpec((width, tc), lambda j: (0, j)),
                   pl.BlockSpec((1, tc), lambda j: (0, j)), blk],
        out_shape=[jax.ShapeDtypeStruct((2, S, Fd), BF16),
                   jax.ShapeDtypeStruct((width, Fd), F32), jax.ShapeDtypeStruct((1, Fd), F32),
                   jax.ShapeDtypeStruct((S, Fd), BF16)],
        compiler_params=_cp(1),
    )(u, u, w, b, da)


def _ssd_prep(dtr, dt_bias, a_log, *, name="ssd_prep"):
    S = dtr.shape[0]

    def body(d_ref, b_ref, al_ref, dt_ref, ac_ref, sg_ref, act_ref):
        lane = _iota((CHUNK, 128), 1)
        valid = lane < SSM_HEADS
        z = d_ref[...] + b_ref[...]
        dt = jnp.where(valid, jnp.maximum(z, 0.0) + jnp.log(1.0 + jnp.exp(-jnp.abs(z))), 0.0)
        a = dt * (-jnp.exp(al_ref[...]))
        row = _iota((CHUNK, 128), 0)
        k = 1
        while k < CHUNK:
            a = a + jnp.where(row >= k, pltpu.roll(a, k, axis=0), 0.0)
            k *= 2
        sg = jnp.where(valid, _sigmoid(z), 0.0)
        for arr, ref in ((dt, dt_ref), (a, ac_ref), (sg, sg_ref)):
            for g in range(SSM_GROUPS):
                ref[g] = jnp.where(lane < 4, arr if g == 0 else pltpu.roll(arr, 128 - 4 * g, axis=1), 0.0)
        act_ref[...] = a.T[:SSM_HEADS, :]

    blk = pl.BlockSpec((CHUNK, 128), lambda i: (i, 0))
    vec = pl.BlockSpec((1, 128), lambda i: (0, 0))
    grp = pl.BlockSpec((SSM_GROUPS, CHUNK, 128), lambda i: (0, i, 0))
    return pl.pallas_call(
        body, name=name, grid=(S // CHUNK,), in_specs=[blk, vec, vec],
        out_specs=[grp, grp, grp, pl.BlockSpec((SSM_HEADS, CHUNK), lambda i: (0, i))],
        out_shape=[jax.ShapeDtypeStruct((SSM_GROUPS, S, 128), F32)] * 3 + [jax.ShapeDtypeStruct((SSM_HEADS, S), F32)],
        compiler_params=_cp(1),
    )(dtr, dt_bias, a_log)


SSD_GPS = 4


def _expand4(v, lanes):
    out = jnp.broadcast_to(v[:, 3:4], lanes.shape)
    for hh in (2, 1, 0):
        out = jnp.where(lanes < 64 * (hh + 1), v[:, hh:hh + 1], out)
    return out


def _ssd_fwd(xbc, dt_g, ac_g, ac_t, *, name="ssd_fwd", dep=None):
    S = xbc.shape[0]
    nc = S // CHUNK
    Lc = CHUNK

    def body(x_ref, b_ref, c_ref, dt_ref, ac_ref, act_ref, *rest):
        y_ref, st_out_ref, st_ref = rest[-3:]
        g2 = pl.program_id(0)
        c = pl.program_id(1)

        @pl.when(c == 0)
        def _():
            st_ref[...] = jnp.zeros_like(st_ref)

        causal = _iota((Lc, Lc), 0) >= _iota((Lc, Lc), 1)
        lane256 = _iota((Lc, 256), 1)
        lane128 = _iota((Lc, 128), 1)
        row128 = _iota((128, 128), 0)
        for gg in range(SSD_GPS):
            g = SSD_GPS * g2 + gg
            bv = b_ref[:, 128 * gg:128 * (gg + 1)]
            cbf = c_ref[:, 128 * gg:128 * (gg + 1)].astype(BF16)
            cb = _dot(cbf, bv.astype(BF16), 1, 1)
            dtg, acg = dt_ref[gg], ac_ref[gg]
            ac_last = ac_ref[gg, pl.ds(Lc - 1, 1), :]
            dt4 = _expand4(dtg, lane256)
            ac4 = _expand4(acg, lane256)
            e4 = jnp.exp(ac4)
            xdb = (x_ref[:, 256 * gg:256 * (gg + 1)] * dt4).astype(BF16)
            st_out_ref[gg] = st_ref[gg]
            for p in range(2):
                xd_p = xdb[:, 128 * p:128 * (p + 1)]
                st_p = st_ref[gg, p]
                ys, sn, cds = [], [], []
                for q in range(2):
                    hh = 2 * p + q
                    a_col = acg[:, hh:hh + 1]
                    a_row = act_ref[pl.ds(4 * g + hh, 1), :]
                    dec = jnp.exp(jnp.where(causal, a_col - a_row, NEG))
                    w = (cb * dec).astype(BF16)
                    ys.append(_dot(w, xd_p))
                    al = ac_last[:, hh:hh + 1]
                    dte = jnp.exp(al - a_col)
                    sn.append(_dot(xd_p, (bv * dte).astype(BF16), 0, 0))
                    cds.append(jnp.exp(al))
                y_diag = jnp.where(lane128 < 64, ys[0], ys[1])
                y_off = _dot(cbf, st_p.astype(BF16), 1, 1) * e4[:, 128 * p:128 * (p + 1)]
                y_ref[:, 256 * gg + 128 * p:256 * gg + 128 * (p + 1)] = y_diag + y_off
                st_ref[gg, p] = jnp.where(row128 < 64, st_p * cds[0] + sn[0], st_p * cds[1] + sn[1])

    G = SSD_GPS
    per_g = lambda g, c: (g, c, 0)
    return pl.pallas_call(
        body, name=name, grid=(SSM_GROUPS // G, nc),
        in_specs=[pl.BlockSpec((Lc, 256 * G), lambda g, c: (c, g)),
                  pl.BlockSpec((Lc, 128 * G), lambda g, c: (c, 16 // G + g)),
                  pl.BlockSpec((Lc, 128 * G), lambda g, c: (c, 24 // G + g)),
                  pl.BlockSpec((G, Lc, 128), per_g), pl.BlockSpec((G, Lc, 128), per_g),
                  pl.BlockSpec((SSM_HEADS, Lc), lambda g, c: (0, c))] + ([] if dep is None else [pl.BlockSpec(memory_space=pl.ANY)]),
        out_specs=[pl.BlockSpec((Lc, 256 * G), lambda g, c: (c, g)),
                   pl.BlockSpec((G, None, 2, 128, 128), lambda g, c: (g, c, 0, 0, 0))],
        out_shape=[jax.ShapeDtypeStruct((S, 2048), F32), jax.ShapeDtypeStruct((SSM_GROUPS, nc, 2, 128, 128), F32)],
        scratch_shapes=[pltpu.VMEM((G, 2, 128, 128), F32)], compiler_params=_cp(2),
    )(xbc, xbc, xbc, dt_g, ac_g, ac_t, *([] if dep is None else [dep]))


def _ssd_bwd(xbc, dt_g, ac_g, ac_t, states, dy, dexp, *, name="ssd_bwd", dep=None):
    S = xbc.shape[0]
    nc = S // CHUNK
    Lc = CHUNK

    def body(x_ref, b_ref, c_ref, dt_ref, ac_ref, act_ref, st_ref, dy_ref, d_ref, *rest):
        dx_ref, db_ref, dc_ref, dh_ref, ds_ref = rest[-5:]
        g2 = pl.program_id(0)
        cc = pl.program_id(1)

        @pl.when(cc == 0)
        def _():
            ds_ref[...] = jnp.zeros_like(ds_ref)

        causal = _iota((Lc, Lc), 0) >= _iota((Lc, Lc), 1)
        lane256 = _iota((Lc, 256), 1)
        lane128 = _iota((Lc, 128), 1)
        row128 = _iota((128, 128), 0)
        ind_rows = _iota((256, 128), 0) >> 6
        ind_cols = _iota((256, 128), 1)
        ind_a = (ind_rows == ind_cols).astype(BF16)
        ind_b = (ind_rows + 4 == ind_cols).astype(BF16)
        for gg in range(SSD_GPS):
            g = SSD_GPS * g2 + gg
            bv = b_ref[:, 128 * gg:128 * (gg + 1)]
            cv = c_ref[:, 128 * gg:128 * (gg + 1)]
            bbf, cbf = bv.astype(BF16), cv.astype(BF16)
            cb = _dot(cbf, bbf, 1, 1)
            dtg, acg = dt_ref[gg], ac_ref[gg]
            ac_last = ac_ref[gg, pl.ds(Lc - 1, 1), :]
            dt4 = _expand4(dtg, lane256)
            ac4 = _expand4(acg, lane256)
            acl4 = _expand4(ac_last, _iota((1, 256), 1))
            e4 = jnp.exp(ac4)
            dte4 = jnp.exp(acl4 - ac4)
            xv = x_ref[:, 256 * gg:256 * (gg + 1)]
            xd = xv * dt4
            xdb = xd.astype(BF16)
            dyv = dy_ref[:, 256 * gg:256 * (gg + 1)]
            dcb = jnp.zeros((Lc, Lc), F32)
            dc_acc = jnp.zeros((Lc, 128), F32)
            db_acc = jnp.zeros((Lc, 128), F32)
            u_parts, dxd_parts, ends = [], [], []
            for p in range(2):
                sl = slice(128 * p, 128 * (p + 1))
                xd_p, xdb_p, dy_p = xd[:, sl], xdb[:, sl], dyv[:, sl]
                dyb_p = dy_p.astype(BF16)
                e_p, dte_p = e4[:, sl], dte4[:, sl]
                sp = st_ref[gg, p]
                spb = sp.astype(BF16)
                dsn = ds_ref[gg, p]
                dsnb = dsn.astype(BF16)
                yds, dxds, cds = [], [], []
                for q in range(2):
                    hh = 2 * p + q
                    a_col = acg[:, hh:hh + 1]
                    a_row = act_ref[pl.ds(4 * g + hh, 1), :]
                    dec = jnp.exp(jnp.where(causal, a_col - a_row, NEG))
                    w = (cb * dec).astype(BF16)
                    head = (lane128 < 64) if q == 0 else (lane128 >= 64)
                    dym = jnp.where(head, dyb_p, jnp.zeros_like(dyb_p))
                    dw = _dot(dym, xdb_p, 1, 1)
                    dcb = dcb + dw * dec
                    yds.append(_dot(w, xdb_p))
                    dxds.append(_dot(w, dyb_p, 0, 0))
                    cds.append(jnp.exp(ac_last[:, hh:hh + 1]))
                y_diag = jnp.where(lane128 < 64, yds[0], yds[1])
                dxd_diag = jnp.where(lane128 < 64, dxds[0], dxds[1])
                y_off = _dot(cbf, spb, 1, 1) * e_p
                dgp = dy_p * e_p
                dgb = dgp.astype(BF16)
                dc_acc = dc_acc + _dot(dgb, spb)
                dsp = _dot(dgb, cbf, 0, 0)
                cd_col = jnp.where(row128[:, 0:1] < 64, cds[0], cds[1])
                qm = _dot(bbf, dsnb, 1, 1)
                dxd_state = dte_p * qm
                db_acc = db_acc + _dot((xd_p * dte_p).astype(BF16), dsnb)
                t_p = xd_p * dxd_state
                prod = dsn * sp
                e0 = jnp.sum(jnp.sum(jnp.where(row128 < 64, prod, 0.0), axis=1, keepdims=True), axis=0, keepdims=True)
                e1 = jnp.sum(jnp.sum(jnp.where(row128 >= 64, prod, 0.0), axis=1, keepdims=True), axis=0, keepdims=True)
                tcol = jnp.sum(t_p, axis=0, keepdims=True)
                lane1 = _iota((1, 128), 1)
                t0 = jnp.sum(jnp.where(lane1 < 64, tcol, 0.0), axis=1, keepdims=True)
                t1 = jnp.sum(jnp.where(lane1 >= 64, tcol, 0.0), axis=1, keepdims=True)
                ends.append(e0 * cds[0] + t0)
                ends.append(e1 * cds[1] + t1)
                ds_ref[gg, p] = dsn * cd_col + dsp
                u_parts.append(dyb_p.astype(F32) * y_diag - xdb_p.astype(F32) * dxd_diag + dy_p * y_off - t_p)
                dxd_parts.append(dxd_diag + dxd_state)
            dxd = jnp.concatenate(dxd_parts, axis=1)
            u_all = jnp.concatenate(u_parts, axis=1)
            dx_ref[:, 256 * gg:256 * (gg + 1)] = dxd * dt4 + dyv * d_ref[:, 256 * gg:256 * (gg + 1)]
            dcbb = dcb.astype(BF16)
            dc_ref[:, 128 * gg:128 * (gg + 1)] = dc_acc + _dot(dcbb, bbf)
            db_ref[:, 128 * gg:128 * (gg + 1)] = db_acc + _dot(dcbb, cbf, 0, 0)
            lane = _iota((Lc, 128), 1)
            endv = jnp.zeros((Lc, 128), F32)
            for hh in range(4):
                endv = jnp.where(lane == 8 + hh, ends[hh], endv)
            dh_ref[gg] = _dot3(dxd * xv, ind_a) + _dot3(u_all, ind_b) + endv

    G = SSD_GPS
    rev = lambda c: nc - 1 - c
    per_g = lambda g, c: (g, rev(c), 0)
    return pl.pallas_call(
        body, name=name, grid=(SSM_GROUPS // G, nc),
        in_specs=[pl.BlockSpec((Lc, 256 * G), lambda g, c: (rev(c), g)),
                  pl.BlockSpec((Lc, 128 * G), lambda g, c: (rev(c), 16 // G + g)),
                  pl.BlockSpec((Lc, 128 * G), lambda g, c: (rev(c), 24 // G + g)),
                  pl.BlockSpec((G, Lc, 128), per_g), pl.BlockSpec((G, Lc, 128), per_g),
                  pl.BlockSpec((SSM_HEADS, Lc), lambda g, c: (0, rev(c))),
                  pl.BlockSpec((G, None, 2, 128, 128), lambda g, c: (g, rev(c), 0, 0, 0)),
                  pl.BlockSpec((Lc, 256 * G), lambda g, c: (rev(c), g)),
                  pl.BlockSpec((1, 256 * G), lambda g, c: (0, g))] + ([] if dep is None else [pl.BlockSpec(memory_space=pl.ANY)]),
        out_specs=[pl.BlockSpec((Lc, 256 * G), lambda g, c: (rev(c), g)),
                   pl.BlockSpec((Lc, 128 * G), lambda g, c: (rev(c), g)),
                   pl.BlockSpec((Lc, 128 * G), lambda g, c: (rev(c), g)),
                   pl.BlockSpec((G, Lc, 128), per_g)],
        out_shape=[jax.ShapeDtypeStruct((S, 2048), F32), jax.ShapeDtypeStruct((S, 1024), F32),
                   jax.ShapeDtypeStruct((S, 1024), F32), jax.ShapeDtypeStruct((SSM_GROUPS, S, 128), F32)],
        scratch_shapes=[pltpu.VMEM((G, 2, 128, 128), F32)], compiler_params=_cp(2),
    )(xbc, xbc, xbc, dt_g, ac_g, ac_t, states, dy, dexp, *([] if dep is None else [dep]))


def _ssd_post(dhead, dt_g, sg_g, alog_g, *, name="ssd_post"):
    S = dhead.shape[1]
    nc = S // CHUNK
    Lc = CHUNK

    def body(dh_ref, dt_ref, sg_ref, al_ref, o_ref, s_ref):
        @pl.when(pl.program_id(0) == 0)
        def _():
            s_ref[...] = jnp.zeros_like(s_ref)

        lane = _iota((Lc, 128), 1)
        row = _iota((Lc, 128), 0)
        row8 = _iota((8, 128), 0)
        out = jnp.zeros((Lc, 128), F32)
        for g in range(SSM_GROUPS):
            dh = dh_ref[g]
            a_neg = -jnp.exp(al_ref[g])
            dac = jnp.where(lane < 4, pltpu.roll(dh, 124, axis=1), 0.0)
            end = jnp.where(lane < 4, pltpu.roll(dh, 120, axis=1), 0.0)
            k = 1
            while k < Lc:
                dac = dac + jnp.where(row < Lc - k, pltpu.roll(dac, Lc - k, axis=0), 0.0)
                k *= 2
            da = dac + end
            ddt = jnp.where(lane < 4, da * a_neg + dh, 0.0)
            ddtr = ddt * sg_ref[g]
            out = out + (ddtr if g == 0 else pltpu.roll(ddtr, 4 * g, axis=1))
            dal = jnp.sum(da * dt_ref[g], axis=0, keepdims=True) * a_neg
            dbias = jnp.sum(ddtr, axis=0, keepdims=True)
            part = jnp.where(row8 == 0, dal, jnp.where(row8 == 1, dbias, 0.0))
            s_ref[g] += part
        o_ref[...] = out.astype(BF16)

    grp = pl.BlockSpec((SSM_GROUPS, Lc, 128), lambda c: (0, c, 0))
    whole = lambda r: pl.BlockSpec((SSM_GROUPS, r, 128), lambda c: (0, 0, 0))
    return pl.pallas_call(
        body, name=name, grid=(nc,), in_specs=[grp, grp, grp, whole(1)],
        out_specs=[pl.BlockSpec((Lc, 128), lambda c: (c, 0)), whole(8)],
        out_shape=[jax.ShapeDtypeStruct((S, 128), BF16), jax.ShapeDtypeStruct((SSM_GROUPS, 8, 128), F32)],
        compiler_params=_cp(1),
    )(dhead, dt_g, sg_g, alog_g)


def _gate_fwd(y, xbc, zx, dexp, gn, *, name="gate_fwd", tr=256):
    S = y.shape[0]
    W = 2048
    gw = W // SSM_GROUPS

    def body(y_ref, x_ref, z_ref, d_ref, g_ref, o_ref):
        z = z_ref[...]
        u = (y_ref[...] + x_ref[...] * d_ref[...]) * (z * _sigmoid(z))
        gv = g_ref[...]
        for q in range(SSM_GROUPS):
            sl = slice(gw * q, gw * (q + 1))
            uq = u[:, sl]
            r = lax.rsqrt(jnp.mean(uq * uq, axis=-1, keepdims=True) + EPS)
            o_ref[:, sl] = (uq * r * gv[:, sl]).astype(BF16)

    row = pl.BlockSpec((tr, W), lambda i: (i, 0))
    vec = pl.BlockSpec((1, W), lambda i: (0, 0))
    return pl.pallas_call(
        body, name=name, grid=(S // tr,), in_specs=[row, row, row, vec, vec], out_specs=row,
        out_shape=jax.ShapeDtypeStruct((S, W), BF16), compiler_params=_cp(1),
    )(y, xbc, zx, dexp, gn)


def _gate_bwd(y, xbc, zx, dexp, gn, dout, *, name="gate_bwd", tr=256):
    S = y.shape[0]
    W = 2048
    gw = W // SSM_GROUPS
    steps = S // tr

    def body(y_ref, x_ref, z_ref, d_ref, g_ref, do_ref, dy_ref, dz_ref, dg_ref, dd_ref, acc_ref):
        i = pl.program_id(0)

        @pl.when(i == 0)
        def _():
            acc_ref[...] = jnp.zeros_like(acc_ref)

        z = z_ref[...]
        sg = _sigmoid(z)
        sz = z * sg
        xs = x_ref[...]
        yt = y_ref[...] + xs * d_ref[...]
        u = yt * sz
        gv = g_ref[...]
        do = do_ref[...]
        dgs = []
        for q in range(SSM_GROUPS):
            sl = slice(gw * q, gw * (q + 1))
            uq = u[:, sl]
            r = lax.rsqrt(jnp.mean(uq * uq, axis=-1, keepdims=True) + EPS)
            uh = uq * r
            dq = do[:, sl]
            duh = dq * gv[:, sl]
            duq = r * (duh - uh * jnp.mean(duh * uh, axis=-1, keepdims=True))
            dgs.append(jnp.sum(dq * uh, axis=0, keepdims=True))
            dyt = duq * sz[:, sl]
            dy_ref[:, sl] = dyt
            dz_ref[:, sl] = (duq * yt[:, sl] * (sg[:, sl] * (1.0 + z[:, sl] * (1.0 - sg[:, sl])))).astype(BF16)
            acc_ref[:, sl] += jnp.sum(dyt * xs[:, sl], axis=0, keepdims=True)
        dg = jnp.concatenate(dgs, axis=1)

        @pl.when(i == 0)
        def _():
            dg_ref[...] = dg

        @pl.when(i > 0)
        def _():
            dg_ref[...] += dg

        @pl.when(i == steps - 1)
        def _():
            ind = ((_iota((W, 128), 0) >> 6) == _iota((W, 128), 1)).astype(BF16)
            dd_ref[...] = _dot3(jnp.broadcast_to(acc_ref[...], (8, W)), ind)[0:1, :]

    row = pl.BlockSpec((tr, W), lambda i: (i, 0))
    vec = pl.BlockSpec((1, W), lambda i: (0, 0))
    return pl.pallas_call(
        body, name=name, grid=(steps,), in_specs=[row, row, row, vec, vec, row],
        out_specs=[row, row, vec, pl.BlockSpec((1, 128), lambda i: (0, 0))],
        out_shape=[jax.ShapeDtypeStruct((S, W), F32), jax.ShapeDtypeStruct((S, W), BF16),
                   jax.ShapeDtypeStruct((1, W), F32), jax.ShapeDtypeStruct((1, 128), F32)],
        scratch_shapes=[pltpu.VMEM((1, W), F32)], compiler_params=_cp(1),
    )(y, xbc, zx, dexp, gn, dout)


def _rope_cs(posf, *, name="rope_tables", tr=256):
    S = posf.shape[0]

    def body(p_ref, c_ref, s_ref):
        j = (_iota((tr, 128), 1) & 31).astype(F32)
        ang = p_ref[...] * jnp.exp(j * (-math.log(ROPE_THETA) / 32.0))
        c_ref[...] = jnp.cos(ang)
        s_ref[...] = jnp.sin(ang)

    blk = pl.BlockSpec((tr, 128), lambda i: (i, 0))
    return pl.pallas_call(
        body, name=name, grid=(S // tr,), in_specs=[pl.BlockSpec((tr, 1), lambda i: (i, 0))], out_specs=[blk, blk],
        out_shape=[jax.ShapeDtypeStruct((S, 128), F32)] * 2, compiler_params=_cp(1),
    )(posf)


def _rope_tables(c_ref, s_ref, shape):
    reps = shape[1] // 128
    return jnp.tile(c_ref[...], (1, reps)), jnp.tile(s_ref[...], (1, reps)), (_iota(shape, 1) & 63) < 32


def _hn_inds(W):
    ind = ((_iota((W, 128), 0) >> 6) == _iota((W, 128), 1)).astype(BF16)
    ind_t = ((_iota((128, W), 1) >> 6) == _iota((128, W), 0)).astype(BF16)
    return ind, ind_t


def _hnrope_fwd(xin, col0, W, gain_w, rope, *, name, tr=256):
    S = xin.shape[0]
    off = col0 // W
    nh = W // HEAD

    def body(x_ref, g_ref, c_ref, s_ref, o_ref):
        x = x_ref[...]
        ind, ind_t = _hn_inds(W)
        r = lax.rsqrt(_dot3(x * x, ind) * (1.0 / HEAD) + EPS)
        xn = x * _dot3(r, ind_t) * g_ref[...]
        cs, sn, half = _rope_tables(c_ref, s_ref, (tr, W))
        rot = jnp.where(half, -pltpu.roll(xn, W - 32, axis=1), pltpu.roll(xn, 32, axis=1))
        out = (xn * cs + rot * sn).astype(BF16)
        for h in range(nh):
            o_ref[h] = out[:, HEAD * h:HEAD * (h + 1)]

    tab = pl.BlockSpec((tr, 128), lambda i: (i, 0))
    return pl.pallas_call(
        body, name=name, grid=(S // tr,),
        in_specs=[pl.BlockSpec((tr, W), lambda i: (i, off)), pl.BlockSpec((1, W), lambda i: (0, 0)), tab, tab],
        out_specs=pl.BlockSpec((nh, tr, HEAD), lambda i: (0, i, 0)), out_shape=jax.ShapeDtypeStruct((nh, S, HEAD), BF16),
        compiler_params=_cp(1),
    )(xin, gain_w, *rope)


def _hnrope_bwd(xin, col0, W, gain_w, rope, dout, *, name, tr=256):
    S = xin.shape[0]
    off = col0 // W
    steps = S // tr
    nh = W // HEAD

    def body(x_ref, g_ref, c_ref, s_ref, do_ref, dx_ref, cs_ref, dg_ref, acc_ref):
        i = pl.program_id(0)
        x = x_ref[...]
        ind, ind_t = _hn_inds(W)
        r = lax.rsqrt(_dot3(x * x, ind) * (1.0 / HEAD) + EPS)
        rw = _dot3(r, ind_t)
        xh = x * rw
        cs, sn, half = _rope_tables(c_ref, s_ref, (tr, W))
        do = jnp.concatenate([do_ref[h] for h in range(nh)], axis=1).astype(F32)
        gs = do * sn
        g1 = do * cs + jnp.where(half, pltpu.roll(gs, W - 32, axis=1), -pltpu.roll(gs, 32, axis=1))
        dxh = g1 * g_ref[...]
        t = _dot3(dxh * xh, ind) * (1.0 / HEAD)
        dx = rw * (dxh - xh * _dot3(t, ind_t))
        dx_ref[...] = dx.astype(BF16)
        cpart = jnp.sum(dx, axis=0, keepdims=True)
        gpart = jnp.sum(g1 * xh, axis=0, keepdims=True)

        @pl.when(i == 0)
        def _():
            cs_ref[...] = cpart
            acc_ref[...] = gpart

        @pl.when(i > 0)
        def _():
            cs_ref[...] += cpart
            acc_ref[...] += gpart

        @pl.when(i == steps - 1)
        def _():
            fold = ((_iota((W, 128), 0) & 63) == _iota((W, 128), 1)).astype(BF16)
            dg_ref[...] = _dot3(jnp.broadcast_to(acc_ref[...], (8, W)), fold)[0:1, :]

    tab = pl.BlockSpec((tr, 128), lambda i: (i, 0))
    return pl.pallas_call(
        body, name=name, grid=(steps,),
        in_specs=[pl.BlockSpec((tr, W), lambda i: (i, off)), pl.BlockSpec((1, W), lambda i: (0, 0)), tab, tab,
                  pl.BlockSpec((nh, tr, HEAD), lambda i: (0, i, 0))],
        out_specs=[pl.BlockSpec((tr, W), lambda i: (i, 0)), pl.BlockSpec((1, W), lambda i: (0, 0)),
                   pl.BlockSpec((1, 128), lambda i: (0, 0))],
        out_shape=[jax.ShapeDtypeStruct((S, W), BF16), jax.ShapeDtypeStruct((1, W), F32),
                   jax.ShapeDtypeStruct((1, 128), F32)],
        scratch_shapes=[pltpu.VMEM((1, W), F32)], compiler_params=_cp(1),
    )(xin, gain_w, *rope, dout)


def _attn_band():
    qi = jnp.arange(ATT_G * WINDOW)[:, None] % WINDOW
    ki = jnp.arange(2 * WINDOW)[None, :]
    rel = qi + WINDOW - ki
    ok = (rel >= 0) & (rel < WINDOW)
    return jnp.stack([jnp.where(ok & (ki >= WINDOW), 0.0, NEG), jnp.where(ok, 0.0, NEG)]).astype(F32)


def _attn_probs(q, kb, sink_ref, band_ref, h, i):
    s = _dot(q, kb, 1, 1) * (HEAD ** -0.5) + band_ref[jnp.minimum(i, 1)]
    r1 = _iota((4 * WINDOW, 1), 0)
    sink = jnp.where(r1 < WINDOW, sink_ref[4 * h], jnp.where(r1 < 2 * WINDOW, sink_ref[4 * h + 1],
                     jnp.where(r1 < 3 * WINDOW, sink_ref[4 * h + 2], sink_ref[4 * h + 3])))
    m = jnp.maximum(jnp.max(s, axis=1, keepdims=True), sink)
    p = jnp.exp(s - m)
    ps = jnp.exp(sink - m)
    inv = 1.0 / (jnp.sum(p, axis=1, keepdims=True) + ps)
    return p * inv, ps * inv


ATT_HPS = 4
_BAND = pl.BlockSpec((2, ATT_G * WINDOW, 2 * WINDOW), lambda h, i: (0, 0, 0))


def _attn_specs(S):
    qspec = pl.BlockSpec((ATT_HPS, ATT_G, WINDOW, HEAD), lambda h, i: (h, 0, i, 0))
    cur = pl.BlockSpec((ATT_HPS, WINDOW, HEAD), lambda h, i: (h, i, 0))
    prev = pl.BlockSpec((ATT_HPS, WINDOW, HEAD), lambda h, i: (h, jnp.maximum(i - 1, 0), 0))
    tok = pl.BlockSpec((WINDOW, ATT_HPS * ATT_G * HEAD), lambda h, i: (i, h))
    return qspec, cur, prev, tok


def _attn_fwd(qh, kh, vh, sinks, *, name="attn_fwd"):
    S = kh.shape[1]
    nb = S // WINDOW

    def body(s_ref, band_ref, q_ref, kc_ref, kp_ref, vc_ref, vp_ref, o_ref):
        h2, i = pl.program_id(0), pl.program_id(1)
        outs = []
        for hh in range(ATT_HPS):
            q = q_ref[hh].reshape(ATT_G * WINDOW, HEAD)
            kb = jnp.concatenate([kp_ref[hh], kc_ref[hh]], axis=0)
            vb = jnp.concatenate([vp_ref[hh], vc_ref[hh]], axis=0)
            probs, _ = _attn_probs(q, kb, s_ref, band_ref, ATT_HPS * h2 + hh, i)
            o = _dot(probs.astype(BF16), vb).astype(BF16)
            outs += [o[WINDOW * g:WINDOW * (g + 1)] for g in range(ATT_G)]
        o_ref[...] = jnp.concatenate(outs, axis=1)

    qspec, cur, prev, tok = _attn_specs(S)
    return pl.pallas_call(
        body, name=name, grid=(ATT_KV // ATT_HPS, nb),
        in_specs=[pl.BlockSpec(memory_space=pltpu.SMEM), _BAND, qspec, cur, prev, cur, prev], out_specs=tok,
        out_shape=jax.ShapeDtypeStruct((S, ATT_KV * ATT_G * HEAD), BF16), compiler_params=_cp(2),
    )(sinks, _attn_band(), qh, kh, kh, vh, vh)


def _attn_bwd(qh, kh, vh, sinks, doh, *, name="attn_bwd"):
    S = kh.shape[1]
    nb = S // WINDOW

    def body(s_ref, band_ref, q_ref, kc_ref, kp_ref, vc_ref, vp_ref, do_ref, dq_ref, dk_ref, dv_ref, dsk_ref):
        h2, i = pl.program_id(0), pl.program_id(1)

        @pl.when(i == 0)
        def _():
            dk_ref[...] = jnp.zeros_like(dk_ref)
            dv_ref[...] = jnp.zeros_like(dv_ref)
            dsk_ref[...] = jnp.zeros_like(dsk_ref)

        dov = do_ref[...]
        cur = pl.multiple_of(i * WINDOW, WINDOW)
        lane = _iota((8, 128), 1)
        row = _iota((8, 128), 0)
        scale = HEAD ** -0.5
        for hh in range(ATT_HPS):
            q = q_ref[hh].reshape(ATT_G * WINDOW, HEAD)
            do = jnp.concatenate([dov[:, HEAD * (ATT_G * hh + g):HEAD * (ATT_G * hh + g + 1)] for g in range(ATT_G)], axis=0)
            kb = jnp.concatenate([kp_ref[hh], kc_ref[hh]], axis=0)
            vb = jnp.concatenate([vp_ref[hh], vc_ref[hh]], axis=0)
            probs, psink = _attn_probs(q, kb, s_ref, band_ref, ATT_HPS * h2 + hh, i)
            dp = _dot(do, vb, 1, 1)
            delta = jnp.sum(probs * dp, axis=1, keepdims=True)
            ds = (probs * (dp - delta)).astype(BF16)
            dq_ref[hh] = (_dot(ds, kb) * scale).reshape(ATT_G, WINDOW, HEAD)
            dkb = _dot(ds, q, 0, 0) * scale
            dvb = _dot(probs.astype(BF16), do, 0, 0)
            dk_ref[hh, pl.ds(cur, WINDOW), :] += dkb[WINDOW:, :]
            dv_ref[hh, pl.ds(cur, WINDOW), :] += dvb[WINDOW:, :]
            prv = pl.multiple_of(jnp.maximum(i - 1, 0) * WINDOW, WINDOW)
            dk_ref[hh, pl.ds(prv, WINDOW), :] += dkb[:WINDOW, :]
            dv_ref[hh, pl.ds(prv, WINDOW), :] += dvb[:WINDOW, :]

            dsr = -psink * delta
            upd = jnp.zeros((8, 128), F32)
            for gq in range(ATT_G):
                v = jnp.sum(dsr[gq * WINDOW:(gq + 1) * WINDOW, :], axis=0, keepdims=True)
                upd = jnp.where((lane == gq) & (row == 0), v, upd)
            dsk_ref[hh] += upd

    qspec, cur, prev, tok = _attn_specs(S)
    full = pl.BlockSpec((ATT_HPS, S, HEAD), lambda h, i: (h, 0, 0))
    return pl.pallas_call(
        body, name=name, grid=(ATT_KV // ATT_HPS, nb),
        in_specs=[pl.BlockSpec(memory_space=pltpu.SMEM), _BAND, qspec, cur, prev, cur, prev, tok],
        out_specs=[qspec, full, full, pl.BlockSpec((ATT_HPS, 8, 128), lambda h, i: (h, 0, 0))],
        out_shape=[jax.ShapeDtypeStruct((ATT_KV, ATT_G, S, HEAD), F32), jax.ShapeDtypeStruct((ATT_KV, S, HEAD), F32),
                   jax.ShapeDtypeStruct((ATT_KV, S, HEAD), F32), jax.ShapeDtypeStruct((ATT_KV, 8, 128), F32)],
        compiler_params=_cp(2),
    )(sinks, _attn_band(), qh, kh, kh, vh, vh, doh)


def _heads_major(t, nh):
    S = t.shape[0]
    return t.reshape(S, nh, HEAD).transpose(1, 0, 2)


def _tokens_major(t):
    nh, S, _ = t.shape
    return t.transpose(1, 0, 2).reshape(S, nh * HEAD)


class _NoComm:
    def late_start(self, part, after):
        return None

    def late_arrived(self, part, after):
        return None

    def late_weights(self, w, after, part):
        return w

    def advance(self, after, group=None, tensors=None):
        return None


def _local_step(x, posf, target, w, comm=None):
    S, D = x.shape
    gr = {}
    comm = comm or _NoComm()

    zx, h1 = _norm_mm(x, w["a_norm"], w["w_zx"], name="in_proj_zx", dep=w.get("dep"))
    dtr = _mm(h1, w["w_dt"], name="in_proj_dt")
    xbc = _conv_silu_fwd(zx, 2048, 4096, w["a_conv_w"], w["a_conv_b"], name="a_conv_f")
    dt_g, ac_g, sg_g, ac_t = _ssd_prep(dtr, w["a_dt_bias"], w["a_A_log"])
    tok = comm.late_start(1, xbc)
    y_ssd, states = _ssd_fwd(xbc, dt_g, ac_g, ac_t, dep=comm.late_arrived(0, [dt_g] + ([] if tok is None else [tok])))
    yg = _gate_fwd(y_ssd, xbc, zx, w["a_Dexp"], w["a_gnorm"])
    x1 = _mm(yg, w["a_out_proj"], res=x, name="out_proj")

    w = comm.late_weights(w, x1, 0)
    FW = w["f_w_in"][0].shape[2]

    def ffn_fwd(xin, l, loss_target=None):
        u, h = _norm_mm(xin, w["f_norm"][l], w["f_w_in"][l], name=f"f_in{l}", N=N_CHIPS * FW, tn=FW,
                        b_spec=pl.BlockSpec((None, D, FW), lambda i, j: (j, 0, 0)))
        a = _ffn_act_fwd(u, w["f_conv_w"][l], w["f_conv_b"][l], name=f"f_act_f{l}")
        dep = comm.late_arrived(1, [u]) if l == 0 else None
        xo = _mm(a, w["f_w_down"][l], res=xin, tk=a.shape[1], name=f"f_down{l}", target=loss_target, dep=dep)
        return xo, (h, u)

    x2, ffn0 = ffn_fwd(x1, 0)
    w = comm.late_weights(w, x2, 1)

    kv, hk = _norm_mm(x2, w["kv_norm"], w["w_kv"], bias=w["b_kv"], name="kv_proj")
    q, hq = _norm_mm(x2, w["b_norm"], w["w_q"], bias=w["b_q"], name="q_proj")
    rope = _rope_cs(posf)
    kr = _hnrope_fwd(kv, 0, 256, w["k_norm_w"], rope, name="k_rope_f")
    qr = _hnrope_fwd(q, 0, 1024, w["q_norm_w"], rope, name="q_rope_f")
    qh = qr.reshape(ATT_KV, ATT_G, S, HEAD)
    kh = kr
    vh = _heads_major(kv[:, 256:].astype(BF16), ATT_KV)
    att = _attn_fwd(qh, kh, vh, w["sinks"])
    x3 = _mm(att, w["w_o"], bias=w["b_o"], res=x2, name="o_proj")
    (dy, loss_part), ffn1 = ffn_fwd(x3, 1, target)

    def ffn_bwd(xin, l, saved, dyo, want_colsum, dep=None):
        h, u = saved
        da = _mm(dyo, w["f_w_down"][l], tb=True, name=f"f_down_dx{l}", dep=dep)
        du, dcw, dcb, a = _ffn_act_bwd(u, w["f_conv_w"][l], w["f_conv_b"][l], da, name=f"f_act_b{l}")
        dw_down = _mm(a, dyo, ta=True, out_dtype=BF16, name=f"f_down_dw{l}")
        dw_in = _mm(h, du, ta=True, out_dtype=BF16, name=f"f_in_dw{l}", dims=(D, N_CHIPS * FW, S), tm=D, tn=FW, tk=S,
                    b_spec=pl.BlockSpec((None, S, FW), lambda i, j, k: (j // 2, 0, j % 2)),
                    o_spec=pl.BlockSpec((None, D, FW), lambda i, j, k: (j, i, 0)), o_shape=(N_CHIPS, D, FW))
        ts = _pick(S, (1024, 512, 256))
        pieces = [(pl.BlockSpec((None, ts, FW), lambda i, j, k, q=q: (q // 2, i, q % 2)),
                   pl.BlockSpec((None, 512, FW), lambda i, j, k, q=q: (q, j, 0))) for q in range(N_CHIPS)]
        dh = _mm(du, w["f_w_in"][l], tb=True, name=f"f_in_dx{l}", dims=(S, D, FW), tm=ts, tn=512, tk=FW,
                 a_spec=pieces[0][0], b_spec=pieces[0][1], more=pieces[1:])
        outs = _rms_bwd(xin, [w["f_norm"][l]], [dh], dyo, name=f"f_norm_b{l}", want_colsum=want_colsum)
        g = dict(f_norm=outs[1], f_w_in=dw_in, f_conv_w=dcw, f_conv_b=dcb, f_w_down=dw_down)
        return outs[0], g, (outs[2] if want_colsum else None)

    dx3, gr["ffn1"], db_o = ffn_bwd(x3, 1, ffn1, dy, True)
    gr["b_o"] = db_o
    gr["w_o"] = _mm(att, dx3, ta=True, out_dtype=BF16, name="o_proj_dw")
    datt = _mm(dx3, w["w_o"], tb=True, out_dtype=BF16, name="o_proj_dx")
    dqh, dkh, dvh, dsk = _attn_bwd(qh, kh, vh, w["sinks"], datt)
    gr["sinks"] = dsk[:, 0, :4].reshape(1, 16)
    dv = _tokens_major(dvh).astype(BF16)
    dq, db_q, dqn = _hnrope_bwd(q, 0, 1024, w["q_norm_w"], rope, dqh.reshape(16, S, HEAD), name="q_rope_b")
    dk, db_k, dkn = _hnrope_bwd(kv, 0, 256, w["k_norm_w"], rope, dkh, name="k_rope_b")
    gr["q_norm"], gr["k_norm"] = dqn[:, :HEAD], dkn[:, :HEAD]
    gr["b_q"] = db_q
    gr["b_kv"] = jnp.concatenate([db_k, _colsum(dv, name="dv_colsum")], axis=1)
    dkv = jnp.concatenate([dk, dv], axis=1)
    gr["w_q"] = _mm(hq, dq, ta=True, out_dtype=BF16, name="q_proj_dw")
    gr["w_kv"] = _mm(hk, dkv, ta=True, out_dtype=BF16, name="kv_proj_dw")
    tok = comm.advance([gr["w_kv"]], 1, dict(f_down1=gr["ffn1"]["f_w_down"], f_in1=gr["ffn1"]["f_w_in"], w_o=gr["w_o"],
                                             w_q=gr["w_q"], w_kv=gr["w_kv"]))
    dhq = _mm(dq, w["w_q"], tb=True, name="q_proj_dx", dep=tok)
    dhk = _mm(dkv, w["w_kv"], tb=True, name="kv_proj_dx")
    dx2, gr["kv_norm"], gr["b_norm"] = _rms_bwd(x2, [w["kv_norm"], w["b_norm"]], [dhk, dhq], dx3, name="kvq_norm_b")

    dx1, gr["ffn0"], _ = ffn_bwd(x1, 0, ffn0, dx2, False, dep=comm.advance([dx2]))

    gr["a_out_proj"] = _mm(yg, dx1, ta=True, out_dtype=BF16, name="out_proj_dw")
    tok = comm.advance([dx1, gr["a_out_proj"]], 2,
                       dict(f_down0=gr["ffn0"]["f_w_down"], f_in0=gr["ffn0"]["f_w_in"], out_proj=gr["a_out_proj"]))
    dyg = _mm(dx1, w["a_out_proj"], tb=True, name="out_proj_dx", dep=tok)
    dy_ssd, dz, gr["a_gnorm"], dD = _gate_bwd(y_ssd, xbc, zx, w["a_Dexp"], w["a_gnorm"], dyg)
    gr["a_D"] = dD[:, :SSM_HEADS]
    dxs, dB, dC, dhead = _ssd_bwd(xbc, dt_g, ac_g, ac_t, states, dy_ssd, w["a_Dexp"], dep=comm.advance([dy_ssd]))
    ddtr, dsmall = _ssd_post(dhead, dt_g, sg_g, w["a_A_log_g"])
    gr["a_A_log"] = dsmall[:, 0, :4].reshape(1, SSM_HEADS)
    gr["a_dt_bias"] = dsmall[:, 1, :4].reshape(1, SSM_HEADS)
    dxbc, gr["a_conv_w"], gr["a_conv_b"] = _conv_silu_bwd(
        zx, 2048, 4096, w["a_conv_w"], w["a_conv_b"], [(dxs, 0), (dB, 2048), (dC, 3072)], name="a_conv_b")
    gr["w_z"] = _mm(h1, dz, ta=True, out_dtype=BF16, name="in_proj_dwz")
    gr["w_x"] = _mm(h1, dxbc, ta=True, out_dtype=BF16, name="in_proj_dwx")
    gr["w_dt"] = _mm(h1, ddtr, ta=True, out_dtype=BF16, name="in_proj_dwdt")
    ts = _pick(S, (1024, 512, 256))
    wblk = lambda q: pl.BlockSpec((512, 2048), lambda i, j, k: (j, q))
    dh1 = _mm(dz, w["w_zx"], tb=True, name="in_proj_dx", dims=(S, D, 2048), tm=ts, tn=512, tk=2048,
              a_spec=pl.BlockSpec((ts, 2048), lambda i, j, k: (i, 0)), b_spec=wblk(0),
              more=[(dxbc, pl.BlockSpec((ts, 2048), lambda i, j, k: (i, 0)), w["w_zx"], wblk(1)),
                    (dxbc, pl.BlockSpec((ts, 2048), lambda i, j, k: (i, 1)), w["w_zx"], wblk(2)),
                    (ddtr, pl.BlockSpec((ts, 128), lambda i, j, k: (i, 0)), w["w_dt"], pl.BlockSpec((512, 128), lambda i, j, k: (j, 0)))])
    dx0, gr["a_norm"] = _rms_bwd(x, [w["a_norm"]], [dh1], dx1, name="a_norm_b")
    tok = comm.advance([dx0], 3, dict(in_proj=_in_proj_grad(gr).reshape(D, N_CHIPS, -1).transpose(1, 0, 2)))
    return loss_part, dx0, gr, tok


def _prep_small(full, w):
    w["a_norm"] = full["a_norm"]
    w["a_conv_w"] = full["a_conv_w"][0]
    w["a_conv_b"] = full["a_conv_b"]
    pad32 = lambda v: jnp.pad(v, ((0, 0), (0, 128 - SSM_HEADS)))
    w["a_dt_bias"] = pad32(full["a_dt_bias"])
    w["a_A_log"] = pad32(full["a_A_log"])
    w["a_A_log_g"] = jnp.pad(full["a_A_log"].reshape(SSM_GROUPS, 1, 4), ((0, 0), (0, 0), (0, 124)))
    w["a_Dexp"] = jnp.repeat(full["a_D"], HEAD, axis=1)
    w["a_gnorm"] = full["a_gnorm"]
    w["f_norm"] = [full["f_norm"][l:l + 1] for l in range(2)]
    w["f_conv_w"] = [full["f_conv_w"][l] for l in range(2)]
    w["f_conv_b"] = [full["f_conv_b"][l:l + 1] for l in range(2)]
    w["kv_norm"] = full["kv_norm"].reshape(1, -1)
    w["b_kv"] = full["b_kv"].reshape(1, -1)
    w["k_norm_w"] = jnp.tile(full["k_norm"].reshape(1, HEAD), (1, ATT_KV))
    w["b_norm"] = full["b_norm"]
    w["b_q"] = full["b_q"]
    w["q_norm_w"] = jnp.tile(full["q_norm"], (1, ATT_KV * ATT_G))
    w["sinks"] = full["sinks"].reshape(-1)
    w["b_o"] = full["b_o"]
    return w


def _split_in_proj(ip):
    return ip[:, :6144].astype(BF16), jnp.pad(ip[:, 6144:], ((0, 0), (0, 128 - SSM_HEADS))).astype(BF16)


def _join_in_proj(blocks, *, name="in_proj_join", tr=256):
    _, R, cw = blocks.shape
    zx_cols = 3 * 2048
    rest = N_CHIPS * cw - zx_cols

    def body(b_ref, zx_ref, dt_ref):
        whole = jnp.concatenate([b_ref[j] for j in range(N_CHIPS)], axis=1)
        zx_ref[...] = whole[:, :zx_cols]
        dt_ref[...] = jnp.concatenate([whole[:, zx_cols:], jnp.zeros((tr, 128 - rest), BF16)], axis=1)

    return pl.pallas_call(
        body, name=name, grid=(R // tr,), in_specs=[pl.BlockSpec((N_CHIPS, tr, cw), lambda i: (0, i, 0))],
        out_specs=[pl.BlockSpec((tr, zx_cols), lambda i: (i, 0)), pl.BlockSpec((tr, 128), lambda i: (i, 0))],
        out_shape=[jax.ShapeDtypeStruct((R, zx_cols), BF16), jax.ShapeDtypeStruct((R, 128), BF16)],
        compiler_params=_cp(1),
    )(blocks)


def _prep_weights(full):
    w = _prep_small(full, {})
    w["w_zx"], w["w_dt"] = _split_in_proj(full["a_in_proj"][0])
    w["a_out_proj"] = full["a_out_proj"][0].astype(BF16)
    w["f_w_in"] = [full["f_w_in"][l].reshape(1024, N_CHIPS, -1).transpose(1, 0, 2).astype(BF16) for l in range(2)]
    w["f_w_down"] = [full["f_w_down"][l].astype(BF16) for l in range(2)]
    w["w_kv"] = full["w_kv"].astype(BF16)
    w["w_q"] = full["w_q"][0].astype(BF16)
    w["w_o"] = full["w_o"][0].astype(BF16)
    return w


def _small_grads(gr):
    g = {}
    g["a_norm"] = gr["a_norm"]
    g["a_conv_w"] = gr["a_conv_w"][None]
    g["a_conv_b"] = gr["a_conv_b"]
    g["a_dt_bias"], g["a_A_log"], g["a_D"] = gr["a_dt_bias"], gr["a_A_log"], gr["a_D"]
    g["a_gnorm"] = gr["a_gnorm"]
    g["kv_norm"] = gr["kv_norm"].reshape(-1)
    g["b_kv"] = gr["b_kv"].reshape(-1)
    g["k_norm"] = gr["k_norm"].reshape(-1)
    g["b_norm"] = gr["b_norm"]
    g["b_q"] = gr["b_q"]
    g["q_norm"] = gr["q_norm"]
    g["sinks"] = gr["sinks"]
    g["b_o"] = gr["b_o"]
    f = [gr["ffn0"], gr["ffn1"]]
    g["f_norm"] = jnp.concatenate([f[0]["f_norm"], f[1]["f_norm"]], axis=0)
    g["f_conv_w"] = jnp.stack([f[l]["f_conv_w"] for l in range(2)])
    g["f_conv_b"] = jnp.concatenate([f[l]["f_conv_b"] for l in range(2)], axis=0)
    return g


def _in_proj_grad(gr):
    return jnp.concatenate([gr["w_z"], gr["w_x"], gr["w_dt"][:, :SSM_HEADS]], axis=1)


def _full_grads(gr):
    g = _small_grads(gr)
    f32 = lambda t: t.astype(F32)
    g["a_in_proj"] = f32(_in_proj_grad(gr))[None]
    g["a_out_proj"] = f32(gr["a_out_proj"])[None]
    g["w_kv"] = f32(gr["w_kv"])
    g["w_q"] = f32(gr["w_q"])[None]
    g["w_o"] = f32(gr["w_o"])[None]
    f = [gr["ffn0"], gr["ffn1"]]
    g["f_w_in"] = jnp.stack([f32(f[l]["f_w_in"]).transpose(1, 0, 2).reshape(1024, -1) for l in range(2)])
    g["f_w_down"] = jnp.stack([f32(f[l]["f_w_down"]) for l in range(2)])
    return g


MESH = pl.DeviceIdType.MESH
WEIGHTS = ("a_norm", "a_in_proj", "a_conv_w", "a_conv_b", "a_dt_bias", "a_A_log", "a_D", "a_gnorm", "a_out_proj",
           "kv_norm", "w_kv", "b_kv", "k_norm", "b_norm", "w_q", "b_q", "q_norm", "sinks", "w_o", "b_o", "f_norm",
           "f_w_in", "f_conv_w", "f_conv_b", "f_w_down")
MATS = (("in_proj", "a_in_proj", 0), ("out_proj", "a_out_proj", 0), ("w_kv", "w_kv", None), ("w_q", "w_q", 0),
        ("w_o", "w_o", 0), ("f_in0", "f_w_in", 0), ("f_in1", "f_w_in", 1), ("f_down0", "f_w_down", 0),
        ("f_down1", "f_w_down", 1))
SMALL_CUT = (("a_norm", 1), ("a_conv_w", 2), ("a_conv_b", 1), ("a_gnorm", 1), ("f_conv_w", 2))
SMALL_REP = ("a_dt_bias", "a_A_log", "a_D", "kv_norm", "b_kv", "k_norm", "b_norm", "b_q", "q_norm", "sinks", "b_o",
             "f_norm", "f_conv_b")


def _coords():
    return lax.axis_index("x"), lax.axis_index("y"), lax.axis_index("c")


def _other_chips(x, y):
    return [(1 - x, y), (x, 1 - y), (1 - x, 1 - y)]


def _pack(arrs, rows_align, lanes, dtype):
    flat = jnp.concatenate([a.reshape(-1).astype(dtype) for a in arrs])
    per = rows_align * lanes
    total = -(-flat.shape[0] // per) * per
    return jnp.pad(flat, (0, total - flat.shape[0])).reshape(total // lanes, lanes)


def _unpack(flat, shapes):
    out, off = [], 0
    for s in shapes:
        n = math.prod(s)
        out.append(flat[off:off + n].reshape(s))
        off += n
    return out


def _remote(src, dst, send, recv, k, dev):
    return pltpu.make_async_remote_copy(src_ref=src, dst_ref=dst, send_sem=send.at[k], recv_sem=recv.at[k],
                                        device_id=dev, device_id_type=MESH)


_ANY = pl.BlockSpec(memory_space=pl.ANY)


def _halves(t):
    r, c = t.shape
    return t.reshape(2, r // 2, c)


def _gather_weights(shards, sp):
    n = len(shards)
    per = 9
    n_sem = per * n + 3

    def body(*refs):
        sh, sp_ref = refs[:n], refs[n]
        outs, sout = refs[n + 1:2 * n + 1], refs[2 * n + 1]
        send, recv, loc = refs[2 * n + 2:]
        x, y, c = _coords()
        me = 2 * x + y
        cx_, cy_, cd_ = _other_chips(x, y)
        ix, iy, idg = (2 * p[0] + p[1] for p in (cx_, cy_, cd_))
        to_x, to_y, sib = (*cx_, c), (*cy_, c), (x, y, 1 - c)
        l1 = pltpu.make_async_copy(sp_ref, sout.at[me], loc.at[0])
        l1.start()
        sends = [_remote(sp_ref, sout.at[me], send, recv, per * n + j, (*p, c)) for j, p in enumerate((cx_, cy_, cd_))]
        for t in range(n):
            sends.append(_remote(sh[t].at[c], outs[t].at[me, c], send, recv, per * t + 0, to_x))
            sends.append(_remote(sh[t].at[c], outs[t].at[me, c], send, recv, per * t + 1, to_y))
            sends.append(_remote(sh[t], outs[t].at[me], send, recv, per * t + 8, sib))
        for cp in sends:
            cp.start()

        def go(src, dst, k, dev):
            cp = _remote(src, dst, send, recv, k, dev)
            cp.start()
            sends.append(cp)

        def piece(t, owner, first):
            q = sh[t].shape[1] // 2
            return outs[t].at[owner, c, pl.ds(0 if first else q, q)]

        for t in range(n):
            _remote(sh[t].at[c], outs[t].at[ix, c], send, recv, per * t + 0, to_x).wait_recv()
            go(piece(t, ix, False), piece(t, ix, False), per * t + 3, to_y)
            go(outs[t].at[ix, c], outs[t].at[ix, c], per * t + 4, sib)
        for t in range(n):
            _remote(sh[t].at[c], outs[t].at[iy, c], send, recv, per * t + 1, to_y).wait_recv()
            go(piece(t, iy, True), piece(t, iy, True), per * t + 2, to_x)
            go(outs[t].at[iy, c], outs[t].at[iy, c], per * t + 5, sib)
        for t in range(n):
            _remote(piece(t, idg, True), piece(t, idg, True), send, recv, per * t + 2, to_x).wait_recv()
            go(piece(t, idg, True), piece(t, idg, True), per * t + 6, sib)
            _remote(piece(t, idg, False), piece(t, idg, False), send, recv, per * t + 3, to_y).wait_recv()
            go(piece(t, idg, False), piece(t, idg, False), per * t + 7, sib)
        for j, p in enumerate((cx_, cy_, cd_)):
            _remote(sp_ref, sout.at[2 * p[0] + p[1]], send, recv, per * n + j, (*p, c)).wait_recv()
        for t in range(n):
            q = sh[t].shape[1] // 2
            other = lambda owner, lo=None: outs[t].at[owner, 1 - c] if lo is None else outs[t].at[owner, 1 - c, pl.ds(lo, q)]
            _remote(other(ix), other(ix), send, recv, per * t + 4, sib).wait_recv()
            _remote(other(iy), other(iy), send, recv, per * t + 5, sib).wait_recv()
            _remote(other(idg, 0), other(idg, 0), send, recv, per * t + 6, sib).wait_recv()
            _remote(other(idg, q), other(idg, q), send, recv, per * t + 7, sib).wait_recv()
            _remote(sh[t], outs[t].at[me], send, recv, per * t + 8, sib).wait_recv()
        for cp in sends:
            cp.wait_send()
        l1.wait()

    res = pl.pallas_call(
        body, name="gather_weights", in_specs=[_ANY] * (n + 1), out_specs=[_ANY] * (n + 1),
        out_shape=[jax.ShapeDtypeStruct((N_CHIPS,) + t.shape, t.dtype) for t in shards]
        + [jax.ShapeDtypeStruct((N_CHIPS,) + sp.shape, sp.dtype)],
        scratch_shapes=[pltpu.SemaphoreType.DMA((n_sem,)), pltpu.SemaphoreType.DMA((n_sem,)),
                        pltpu.SemaphoreType.DMA((1,))],
    )(*shards, sp)
    return res[:n], res[n]


_HBM = pl.BlockSpec(memory_space=pltpu.HBM)
_SEMS = pl.BlockSpec(memory_space=pltpu.SEMAPHORE)
_DATAFLOW = pltpu.SideEffectType.DATAFLOW_SIDE_EFFECTING


def _in_hbm(a):
    return pltpu.with_memory_space_constraint(a, pltpu.HBM)


def _start_copies(copies, arrays, n_sem, after, *, name):
    n, na = len(arrays), len(after)

    def body(*refs):
        for mine, _ in copies(refs[:n], refs[n + na], refs[n + na + 1]):
            mine.start()
        refs[-1][...] = jnp.zeros_like(refs[-1])

    res = pl.pallas_call(
        body, name=name, in_specs=[_HBM] * n + [_ANY] * na,
        out_specs=[_SEMS, _SEMS] + [_HBM] * n + [pl.BlockSpec(memory_space=pltpu.VMEM)],
        out_shape=[pltpu.SemaphoreType.DMA((n_sem,)), pltpu.SemaphoreType.DMA((n_sem,))]
        + [pltpu.HBM(a.shape, a.dtype) for a in arrays] + [jax.ShapeDtypeStruct((8, 128), F32)],
        input_output_aliases={t: 2 + t for t in range(n)},
        compiler_params=pltpu.CompilerParams(has_side_effects=_DATAFLOW),
    )(*[_in_hbm(a) for a in arrays], *after)
    return res[0], res[1], list(res[2:2 + n]), res[-1]


def _wait_copies(copies, send, recv, arrays, after, *, name):
    n = len(arrays)

    def body(*refs):
        for mine, theirs in copies(refs[:n], refs[n], refs[n + 1]):
            mine.wait_send()
            theirs.wait_recv()

    return list(pl.pallas_call(
        body, name=name, in_specs=[_HBM] * n + [_SEMS, _SEMS] + [_ANY] * len(after), out_specs=[_HBM] * n,
        out_shape=[pltpu.HBM(a.shape, a.dtype) for a in arrays], input_output_aliases={t: t for t in range(n)},
        compiler_params=pltpu.CompilerParams(has_side_effects=_DATAFLOW),
    )(*arrays, send, recv, *after))


def _sibling_copies(refs, send, recv):
    n = len(refs) // 2
    x, y, c = _coords()
    cps = [_remote(refs[t].at[:, 1 - c], refs[n + t], send, recv, t, (x, y, 1 - c)) for t in range(n)]
    return [(cp, cp) for cp in cps]


def _join_copies(refs, send, recv):
    x, y, c = _coords()
    sib = (x, y, 1 - c)
    return [(_remote(o.at[c], o.at[c], send, recv, t, sib), _remote(o.at[1 - c], o.at[1 - c], send, recv, t, sib))
            for t, o in enumerate(refs)]


def _gather_copies(sh, land, send, recv):
    x, y, c = _coords()
    me = 2 * x + y
    out = []
    for t in range(len(sh)):
        for j, (cx, cy) in enumerate(_other_chips(x, y)):
            dev = (cx, cy, c)
            out.append((_remote(sh[t].at[c], land[t].at[me, c], send, recv, 4 * t + j, dev),
                        _remote(sh[t].at[c], land[t].at[2 * cx + cy, c], send, recv, 4 * t + j, dev)))
        sib = (x, y, 1 - c)
        out.append((_remote(sh[t], land[t].at[me], send, recv, 4 * t + 3, sib),
                    _remote(sh[t], land[t].at[me], send, recv, 4 * t + 3, sib)))
    return out


def _gather_start(shards, after, *, name):
    n = len(shards)

    def body(*refs):
        sh, land = refs[:n], refs[n:2 * n]
        send, recv = refs[2 * n + 1], refs[2 * n + 2]
        token = refs[-1]
        for mine, _ in _gather_copies(sh, land, send, recv):
            mine.start()
        token[...] = jnp.zeros_like(token)

    lands = [_in_hbm(lax.empty((N_CHIPS,) + s.shape, s.dtype)) for s in shards]
    res = pl.pallas_call(
        body, name=name, in_specs=[_HBM] * (2 * n) + [_ANY],
        out_specs=[_SEMS, _SEMS] + [_HBM] * (2 * n) + [pl.BlockSpec(memory_space=pltpu.VMEM)],
        out_shape=[pltpu.SemaphoreType.DMA((4 * n,)), pltpu.SemaphoreType.DMA((4 * n,))]
        + [pltpu.HBM(s.shape, s.dtype) for s in shards] + [pltpu.HBM(l.shape, l.dtype) for l in lands]
        + [jax.ShapeDtypeStruct((8, 128), F32)],
        input_output_aliases={t: 2 + t for t in range(2 * n)},
        compiler_params=pltpu.CompilerParams(has_side_effects=_DATAFLOW),
    )(*[_in_hbm(s) for s in shards], *lands, after)
    return res[0], res[1], res[2:2 + n], res[2 + n:2 + 2 * n], res[-1]


def _gather_wait(send, recv, shards, lands, after, *, name):
    n = len(shards)

    def body(*refs):
        sh, land = refs[:n], refs[n:2 * n]
        send_r, recv_r = refs[2 * n], refs[2 * n + 1]
        for mine, theirs in _gather_copies(sh, land, send_r, recv_r):
            mine.wait_send()
            theirs.wait_recv()

    res = pl.pallas_call(
        body, name=name, in_specs=[_HBM] * (2 * n) + [_SEMS, _SEMS, _ANY], out_specs=[_HBM] * (2 * n),
        out_shape=[pltpu.HBM(s.shape, s.dtype) for s in shards] + [pltpu.HBM(l.shape, l.dtype) for l in lands],
        input_output_aliases={t: t for t in range(2 * n)},
        compiler_params=pltpu.CompilerParams(has_side_effects=_DATAFLOW),
    )(*shards, *lands, send, recv, after)
    return res[n:]


def _forward_copies(refs, send, recv):
    x, y, c = _coords()
    sib = (x, y, 1 - c)
    srcs = [2 * cx + cy for cx, cy in _other_chips(x, y)]
    return [(_remote(o.at[s, c], o.at[s, c], send, recv, 3 * t + j, sib),
             _remote(o.at[s, 1 - c], o.at[s, 1 - c], send, recv, 3 * t + j, sib))
            for t, o in enumerate(refs) for j, s in enumerate(srcs)]


def _small_copies(v, land, send, recv):
    x, y, c = _coords()
    me = 4 * x + 2 * y + c
    out = []
    for k in range(1, 8):
        px = 1 - x if k & 4 else x
        py = 1 - y if k & 2 else y
        pc = 1 - c if k & 1 else c
        out.append((_remote(v, land.at[me], send, recv, k - 1, (px, py, pc)),
                    _remote(v, land.at[4 * px + 2 * py + pc], send, recv, k - 1, (px, py, pc))))
    return out


def _small_start(v, after, *, name):
    def body(v_ref, land_ref, after_ref, send, recv, v_thru, land_thru, token):
        for mine, _ in _small_copies(v_ref, land_ref, send, recv):
            mine.start()
        token[...] = jnp.zeros_like(token)

    land = _in_hbm(lax.empty((8,) + v.shape, v.dtype))
    return pl.pallas_call(
        body, name=name, in_specs=[_HBM, _HBM, _ANY],
        out_specs=[_SEMS, _SEMS, _HBM, _HBM, pl.BlockSpec(memory_space=pltpu.VMEM)],
        out_shape=[pltpu.SemaphoreType.DMA((7,)), pltpu.SemaphoreType.DMA((7,)), pltpu.HBM(v.shape, v.dtype),
                   pltpu.HBM(land.shape, land.dtype), jax.ShapeDtypeStruct((8, 128), F32)],
        input_output_aliases={0: 2, 1: 3}, compiler_params=pltpu.CompilerParams(has_side_effects=_DATAFLOW),
    )(_in_hbm(v), land, after)


def _small_wait(send, recv, v, land, after, *, name):
    def body(v_ref, land_ref, send_r, recv_r, *rest):
        for mine, theirs in _small_copies(v_ref, land_ref, send_r, recv_r):
            mine.wait_send()
            theirs.wait_recv()

    return pl.pallas_call(
        body, name=name, in_specs=[_HBM, _HBM, _SEMS, _SEMS] + [_ANY] * len(after), out_specs=[_HBM, _HBM],
        out_shape=[pltpu.HBM(v.shape, v.dtype), pltpu.HBM(land.shape, land.dtype)],
        input_output_aliases={0: 0, 1: 1}, compiler_params=pltpu.CompilerParams(has_side_effects=_DATAFLOW),
    )(v, land, send, recv, *after)


def _small_sum(v, land, me_idx, *, name="small_sum"):
    def body(me_ref, v_ref, land_ref, o_ref):
        acc = None
        for s in range(8):
            term = jnp.where(me_ref[0] == s, v_ref[...], land_ref[s])
            acc = term if acc is None else acc + term
        o_ref[...] = acc

    whole = lambda shape: pl.BlockSpec(shape, lambda i, me_ref: (0,) * len(shape))
    return pl.pallas_call(
        body, name=name,
        grid_spec=pltpu.PrefetchScalarGridSpec(num_scalar_prefetch=1, grid=(1,), in_specs=[whole(v.shape), whole(land.shape)],
                                               out_specs=whole(v.shape)),
        out_shape=jax.ShapeDtypeStruct(v.shape, F32), compiler_params=_cp(1),
    )(me_idx, v, land)


RS_ROW_SPLIT = 2


def _rs_add_pair(gs, as_, c_idx, *, name):
    n = len(gs)

    def body(c_ref, *refs):
        for t in range(n):
            refs[2 * n + t][...] = (refs[t][...].astype(F32) + refs[n + t][...].astype(F32)).astype(BF16)

    def gspec(g):
        _, _, rh, cols = g.shape
        return pl.BlockSpec((None, None, rh // RS_ROW_SPLIT, cols), lambda j, i, c_ref: (j, c_ref[0], i, 0))

    def pspec(g):
        _, _, rh, cols = g.shape
        return pl.BlockSpec((None, rh // RS_ROW_SPLIT, cols), lambda j, i, c_ref: (j, i, 0))

    return pl.pallas_call(
        body, name=name,
        grid_spec=pltpu.PrefetchScalarGridSpec(
            num_scalar_prefetch=1, grid=(N_CHIPS, RS_ROW_SPLIT),
            in_specs=[gspec(g) for g in gs] + [pspec(g) for g in gs], out_specs=[pspec(g) for g in gs]),
        out_shape=[jax.ShapeDtypeStruct((N_CHIPS,) + g.shape[2:], BF16) for g in gs], compiler_params=_cp(2),
    )(c_idx, *gs, *as_)


def _chips_copies(p, r, send, recv):
    x, y, c = _coords()
    return [_remote(p[t].at[2 * cx + cy], r[t].at[k], send, recv, 3 * t + k, (cx, cy, c))
            for k, (cx, cy) in enumerate(_other_chips(x, y)) for t in range(len(p))]


def _rs_chips_start(ps, after, *, name):
    n, na = len(ps), len(after)

    def body(*refs):
        p, r = refs[:n], refs[n:2 * n]
        send, recv = refs[2 * n + na], refs[2 * n + na + 1]
        token = refs[-1]
        for cp in _chips_copies(p, r, send, recv):
            cp.start()
        token[...] = jnp.zeros_like(token)

    lands = [_in_hbm(lax.empty((3,) + p.shape[1:], p.dtype)) for p in ps]
    res = pl.pallas_call(
        body, name=name, in_specs=[_HBM] * (2 * n) + [_ANY] * na,
        out_specs=[_SEMS, _SEMS] + [_HBM] * (2 * n) + [pl.BlockSpec(memory_space=pltpu.VMEM)],
        out_shape=[pltpu.SemaphoreType.DMA((3 * n,)), pltpu.SemaphoreType.DMA((3 * n,))]
        + [pltpu.HBM(p.shape, p.dtype) for p in ps] + [pltpu.HBM(l.shape, l.dtype) for l in lands]
        + [jax.ShapeDtypeStruct((8, 128), F32)],
        input_output_aliases={t: 2 + t for t in range(2 * n)},
        compiler_params=pltpu.CompilerParams(has_side_effects=_DATAFLOW),
    )(*[_in_hbm(p) for p in ps], *lands, *after)
    return res[0], res[1], res[2:2 + n], res[2 + n:2 + 2 * n], res[-1]


def _rs_chips_wait(send, recv, ps, lands, after, *, name):
    n = len(ps)

    def body(*refs):
        p, r = refs[:n], refs[n:2 * n]
        for cp in _chips_copies(p, r, refs[2 * n], refs[2 * n + 1]):
            cp.wait_send()
            cp.wait_recv()

    res = pl.pallas_call(
        body, name=name, in_specs=[_HBM] * (2 * n) + [_SEMS, _SEMS] + [_ANY] * len(after), out_specs=[_HBM] * (2 * n),
        out_shape=[pltpu.HBM(p.shape, p.dtype) for p in ps] + [pltpu.HBM(l.shape, l.dtype) for l in lands],
        input_output_aliases={t: t for t in range(2 * n)},
        compiler_params=pltpu.CompilerParams(has_side_effects=_DATAFLOW),
    )(*ps, *lands, send, recv, *after)
    return res[:n], res[n:]


def _rs_add_chips(ps, rs, idx, *, name):
    n = len(ps)

    def body(idx_ref, *refs):
        for t in range(n):
            p_ref, r0, r1, r2 = refs[4 * t:4 * t + 4]
            refs[4 * n + t][...] = ((p_ref[...].astype(F32) + r0[...].astype(F32)) + r1[...].astype(F32)) + r2[...].astype(F32)

    in_specs, args = [], []
    for p, r in zip(ps, rs):
        _, rh, cols = p.shape
        blk = (None, rh // RS_ROW_SPLIT, cols)
        in_specs.append(pl.BlockSpec(blk, lambda i, idx_ref: (idx_ref[0], i, 0)))
        in_specs += [pl.BlockSpec(blk, lambda i, idx_ref, k=k: (k, i, 0)) for k in range(3)]
        args += [p, r, r, r]
    out_specs = [pl.BlockSpec((None, p.shape[1] // RS_ROW_SPLIT, p.shape[2]), lambda i, idx_ref: (idx_ref[1], i, 0))
                 for p in ps]
    return pl.pallas_call(
        body, name=name,
        grid_spec=pltpu.PrefetchScalarGridSpec(num_scalar_prefetch=1, grid=(RS_ROW_SPLIT,), in_specs=in_specs,
                                               out_specs=out_specs),
        out_shape=[jax.ShapeDtypeStruct((2,) + p.shape[1:], F32) for p in ps], compiler_params=_cp(1),
    )(idx, *args)


def _adamw(w, gs, m, v, *, name, dep=None):
    L, Rr, C = w.shape
    tr, tc = _pick(Rr, (256, 128, 64)), C
    if tr == Rr and Rr * C > 512 * 1024:
        tc = 256
    bc1 = 1.0 - ADAM_B1 ** ADAM_STEP
    bc2 = 1.0 - ADAM_B2 ** ADAM_STEP
    nd = 0 if dep is None else 1

    def body(*refs):
        w_ref, m_ref, v_ref = refs[0], refs[1], refs[2]
        g_refs = refs[3:3 + L]
        d_ref, mo_ref, vo_ref, go_ref = refs[3 + L + nd:]
        layer = pl.program_id(0)
        gv = g_refs[0][...]
        for q in range(1, L):
            gv = jnp.where(layer == q, g_refs[q][...], gv)
        mn = ADAM_B1 * m_ref[...] + (1.0 - ADAM_B1) * gv
        vn = ADAM_B2 * v_ref[...] + (1.0 - ADAM_B2) * (gv * gv)
        go_ref[...] = gv
        mo_ref[...] = mn
        vo_ref[...] = vn
        d_ref[...] = -ADAM_LR * ((mn / bc1) / (jnp.sqrt(vn / bc2) + ADAM_EPS) + ADAM_WD * w_ref[...])

    blk = pl.BlockSpec((None, tr, tc), lambda l, i, j: (l, i, j))
    gblks = [pl.BlockSpec((tr, tc), lambda l, i, j, q=q: (jnp.where(l == q, i, 0), jnp.where(l == q, j, 0))) for q in range(L)]
    return pl.pallas_call(
        body, name=name, grid=(L, Rr // tr, C // tc), in_specs=[blk] * 3 + gblks + [_ANY] * nd, out_specs=[blk] * 4,
        out_shape=[jax.ShapeDtypeStruct((L, Rr, C), F32)] * 4, compiler_params=_cp(3),
    )(w, m, v, *gs, *([] if dep is None else [dep]))


def kernel(x, positions, a_norm, a_in_proj, a_conv_w, a_conv_b, a_dt_bias, a_A_log, a_D, a_gnorm, a_out_proj,
           kv_norm, w_kv, b_kv, k_norm, b_norm, w_q, b_q, q_norm, sinks, w_o, b_o, f_norm, f_w_in, f_conv_w,
           f_conv_b, f_w_down, loss_target, m_a_norm, m_a_in_proj, m_a_conv_w, m_a_conv_b, m_a_dt_bias, m_a_A_log,
           m_a_D, m_a_gnorm, m_a_out_proj, m_kv_norm, m_w_kv, m_b_kv, m_k_norm, m_b_norm, m_w_q, m_b_q, m_q_norm,
           m_sinks, m_w_o, m_b_o, m_f_norm, m_f_w_in, m_f_conv_w, m_f_conv_b, m_f_w_down, v_a_norm, v_a_in_proj,
           v_a_conv_w, v_a_conv_b, v_a_dt_bias, v_a_A_log, v_a_D, v_a_gnorm, v_a_out_proj, v_kv_norm, v_w_kv,
           v_b_kv, v_k_norm, v_b_norm, v_w_q, v_b_q, v_q_norm, v_sinks, v_w_o, v_b_o, v_f_norm, v_f_w_in,
           v_f_conv_w, v_f_conv_b, v_f_w_down):
    wl = dict(zip(WEIGHTS, (a_norm, a_in_proj, a_conv_w, a_conv_b, a_dt_bias, a_A_log, a_D, a_gnorm, a_out_proj,
                            kv_norm, w_kv, b_kv, k_norm, b_norm, w_q, b_q, q_norm, sinks, w_o, b_o, f_norm, f_w_in,
                            f_conv_w, f_conv_b, f_w_down)))
    ml = dict(zip(WEIGHTS, (m_a_norm, m_a_in_proj, m_a_conv_w, m_a_conv_b, m_a_dt_bias, m_a_A_log, m_a_D, m_a_gnorm,
                            m_a_out_proj, m_kv_norm, m_w_kv, m_b_kv, m_k_norm, m_b_norm, m_w_q, m_b_q, m_q_norm,
                            m_sinks, m_w_o, m_b_o, m_f_norm, m_f_w_in, m_f_conv_w, m_f_conv_b, m_f_w_down)))
    vl = dict(zip(WEIGHTS, (v_a_norm, v_a_in_proj, v_a_conv_w, v_a_conv_b, v_a_dt_bias, v_a_A_log, v_a_D, v_a_gnorm,
                            v_a_out_proj, v_kv_norm, v_w_kv, v_b_kv, v_k_norm, v_b_norm, v_w_q, v_b_q, v_q_norm,
                            v_sinks, v_w_o, v_b_o, v_f_norm, v_f_w_in, v_f_conv_w, v_f_conv_b, v_f_w_down)))
    xi, yi, ci = _coords()
    me = 2 * xi + yi
    S = x.shape[1]

    def block_of(n, layer):
        t = wl[n]
        return t if layer is None else t[layer]

    rows = lambda t: t.reshape(-1, t.shape[-1])
    c_idx = jnp.reshape(ci, (1,)).astype(jnp.int32)
    me_c = jnp.stack([me, ci]).astype(jnp.int32)
    early = ("in_proj", "out_proj")
    late = (("f_in0", "f_down0"), ("w_kv", "w_q", "w_o", "f_in1", "f_down1"))
    shards = {name: _halves(block_of(wn, layer).astype(BF16)) for name, wn, layer in MATS}

    sp = _pack([wl[n] for n, _ in SMALL_CUT], 8, 128, F32)
    gathered, gs = _gather_weights([shards[k] for k in early], sp)
    gt = {k: t.reshape(N_CHIPS, -1, t.shape[-1]) for k, t in zip(early, gathered)}
    started = {0: _gather_start([shards[k] for k in late[0]], gs, name="gather_late_start0")}
    full = {n: wl[n] for n in SMALL_REP}
    gs = gs.reshape(N_CHIPS, -1)
    pieces = [_unpack(gs[j], [wl[n].shape for n, _ in SMALL_CUT]) for j in range(N_CHIPS)]
    for q, (n, ax) in enumerate(SMALL_CUT):
        full[n] = jnp.concatenate([pieces[j][q] for j in range(N_CHIPS)], axis=ax)
    w = _prep_small(full, {})
    w["w_zx"], w["w_dt"] = _join_in_proj(gt["in_proj"])
    w["a_out_proj"] = rows(gt["out_proj"])
    w["dep"] = started[0][4]

    class Comm:
        flight = []
        reduced = {}

        forwarding = {}

        def late_start(self, part, after):
            started[part] = _gather_start([shards[k] for k in late[part]], after, name=f"gather_late_start{part}")
            return started[part][4]

        def late_arrived(self, part, after):
            send, recv, shs, lands, _ = started[part]
            lands = _gather_wait(send, recv, shs, lands, after[0], name=f"gather_late_wait{part}")
            send, recv, lands, token = _start_copies(_forward_copies, list(lands), 3 * len(lands), after,
                                                     name=f"gather_late_forward_start{part}")
            self.forwarding[part] = (send, recv, lands)
            return token

        def late_weights(self, w, after, part):
            send, recv, lands = self.forwarding[part]
            lands = _wait_copies(_forward_copies, send, recv, lands, [after], name=f"gather_late_forward_wait{part}")
            lt = {k: t.reshape(N_CHIPS, -1, t.shape[-1]) for k, t in zip(late[part], lands)}
            w = dict(w)
            if part == 0:
                w["f_w_in"], w["f_w_down"] = [lt["f_in0"]], [rows(lt["f_down0"])]
            else:
                w["w_kv"], w["w_q"], w["w_o"] = (rows(lt[k]) for k in ("w_kv", "w_q", "w_o"))
                w["f_w_in"], w["f_w_down"] = w["f_w_in"] + [lt["f_in1"]], w["f_w_down"] + [rows(lt["f_down1"])]
            return w

        def advance(self, after, group=None, tensors=None):
            token = None
            for grp in list(self.flight):
                tag, n = grp["tag"], len(grp["names"])
                dep = list(after) + ([] if token is None else [token])
                if grp["stage"] == "sibling":
                    arrs = _wait_copies(_sibling_copies, grp["send"], grp["recv"], grp["arrays"], dep, name=f"rs_sibling_wait{tag}")
                    pairs = _rs_add_pair(arrs[:n], arrs[n:], c_idx, name=f"rs_add_pair{tag}")
                    send, recv, ps, lands, token = _rs_chips_start(pairs, dep, name=f"rs_chips_start{tag}")
                    grp.update(stage="chips", send=send, recv=recv, ps=ps, lands=lands)
                elif grp["stage"] == "chips":
                    ps, rs = _rs_chips_wait(grp["send"], grp["recv"], grp["ps"], grp["lands"], dep, name=f"rs_chips_wait{tag}")
                    halves = _rs_add_chips(ps, rs, me_c, name=f"rs_add_chips{tag}")
                    send, recv, arrs, token = _start_copies(_join_copies, halves, n, dep, name=f"rs_join_start{tag}")
                    grp.update(stage="join", send=send, recv=recv, arrays=arrs)
                else:
                    joined = _wait_copies(_join_copies, grp["send"], grp["recv"], grp["arrays"], dep, name=f"rs_join_wait{tag}")
                    self.reduced.update({k: rows(t) for k, t in zip(grp["names"], joined)})
                    self.flight.remove(grp)
            if group is not None:
                names = list(tensors)
                glist = [tensors[k].reshape(N_CHIPS, 2, -1, tensors[k].shape[-1]) for k in names]
                lands = [lax.empty((N_CHIPS,) + gq.shape[2:], gq.dtype) for gq in glist]
                dep = list(after) + ([] if token is None else [token])
                send, recv, arrs, token = _start_copies(_sibling_copies, glist + lands, len(names), dep,
                                                        name=f"rs_sibling_start{group}")
                self.flight.append(dict(tag=group, names=names, stage="sibling", send=send, recv=recv, arrays=arrs))
            return token

    comm = Comm()

    posf = positions.reshape(S, 1).astype(F32)
    loss_part, dx0, gr, tok = _local_step(x[0], posf, loss_target[0], w, comm)
    g = _small_grads(gr)

    small_names = [n for n, _ in SMALL_CUT] + list(SMALL_REP)
    sv = _pack([g[n] for n in small_names] + [loss_part[0:1, 0:1]], 8, 128, F32)
    s_send, s_recv, sv, s_land, s_token = _small_start(sv, tok, name="small_start")

    grads, delta, new_m, new_v = {}, {}, {}, {}

    def update(wn, dep):
        gl = [comm.reduced[name] for name, n2, _ in MATS if n2 == wn]
        shp = wl[wn].shape
        three = (len(gl),) + gl[0].shape
        flip = shp[-1] % 128 != 0
        view = (lambda t: t.reshape(three).transpose(0, 2, 1)) if flip else (lambda t: t.reshape(three))
        back = (lambda t: t.transpose(0, 2, 1).reshape(shp)) if flip else (lambda t: t.reshape(shp))
        if flip:
            gl = [t.T for t in gl]
        d, mn, vn, go = _adamw(view(wl[wn]), gl, view(ml[wn]), view(vl[wn]), name="adamw_" + wn, dep=dep)
        grads[wn], delta[wn], new_m[wn], new_v[wn] = back(go), back(d), back(mn), back(vn)
        return d

    first = [update(wn, s_token) for wn in ("w_q", "w_o", "w_kv")]
    tok = comm.advance(first)
    second = [update(wn, tok) for wn in ("f_w_in", "f_w_down", "a_out_proj")]
    comm.advance(second)
    comm.advance(second)
    update("a_in_proj", None)
    done = first + second

    sv, s_land = _small_wait(s_send, s_recv, sv, s_land, done, name="small_wait")
    sred = _small_sum(sv, s_land, jnp.reshape(2 * me + ci, (1,)).astype(jnp.int32)).reshape(-1)
    small_shapes = [g[n].shape for n in small_names] + [(1,)]
    sg = dict(zip(small_names + ["loss"], _unpack(sred, small_shapes)))
    loss = sg["loss"].reshape(())
    g_small = {}
    for n, ax in SMALL_CUT:
        size = wl[n].shape[ax]
        g_small[n] = lax.dynamic_slice_in_dim(sg[n], me * size, size, axis=ax)
    for n in SMALL_REP:
        g_small[n] = sg[n].reshape(wl[n].shape)

    pk = lambda d: _pack([d[n] for n in small_names], 8, 128, F32)[None]
    d, mn, vn, _ = _adamw(pk(wl), [pk(g_small)[0]], pk(ml), pk(vl), name="adamw_small")
    shapes = [wl[n].shape for n in small_names]
    for n, dd, mm, vv in zip(small_names, _unpack(d.reshape(-1), shapes), _unpack(mn.reshape(-1), shapes),
                             _unpack(vn.reshape(-1), shapes)):
        grads[n], delta[n], new_m[n], new_v[n] = g_small[n], dd, mm, vv

    return (loss, dx0[None], *[grads[n] for n in WEIGHTS], *[delta[n] for n in WEIGHTS],
            *[new_m[n] for n in WEIGHTS], *[new_v[n] for n in WEIGHTS])
```

```python
import math

import jax
import jax.numpy as jnp
from jax import lax
from jax.experimental import pallas as pl
from jax.experimental.pallas import tpu as pltpu

F32 = jnp.float32
BF16 = jnp.bfloat16

EPS = 1e-5
CHUNK = 256
WINDOW = 128
HEAD = 64
SSM_HEADS = 32
SSM_GROUPS = 8
SSM_STATE = 128
ATT_KV = 4
ATT_G = 4
ROPE_THETA = 10000.0
NEG = -1e30
N_CHIPS = 4
VMEM_LIMIT = 56 * 1024 * 1024

ADAM_LR, ADAM_B1, ADAM_B2, ADAM_EPS, ADAM_WD, ADAM_STEP = 0.001, 0.9, 0.999, 1e-08, 0.01, 10


def _cp(n_axes):
    return pltpu.CompilerParams(dimension_semantics=("arbitrary",) * n_axes, vmem_limit_bytes=VMEM_LIMIT)


def _pick(dim, prefs):
    for p in prefs:
        if dim % p == 0:
            return p
    return dim


def _iota(shape, dim):
    return lax.broadcasted_iota(jnp.int32, shape, dim)


def _dot(a, b, ca=1, cb=0):
    return lax.dot_general(a, b, (((ca,), (cb,)), ((), ())), preferred_element_type=F32)


def _dot3(x, ind):
    h = x.astype(BF16)
    r = x - h.astype(F32)
    m = r.astype(BF16)
    lo = (r - m.astype(F32)).astype(BF16)
    return _dot(h, ind) + _dot(m, ind) + _dot(lo, ind)


def _sigmoid(x):
    return jax.nn.sigmoid(x)


def _mm(a, b, *, name, ta=False, tb=False, bias=None, res=None, out_dtype=F32, b_koff=0, tm=None, tn=None, tk=None,
        dims=None, a_spec=None, b_spec=None, o_spec=None, o_shape=None, dep=None, more=(), target=None):
    if dims is not None:
        M, N, K = dims
    else:
        if ta:
            K, M = a.shape
        else:
            M, K = a.shape
        N = b.shape[0] if tb else b.shape[1]
    tm = tm or _pick(M, (1024, 1408, 512, 256, 128))
    tn = tn or _pick(N, (512, 1408, 256, 128))
    tk = tk or (K if K <= 2048 else _pick(K, (2048, 1408, 1024, 512)))
    assert M % tm == 0 and N % tn == 0 and K % tk == 0 and b_koff % tk == 0
    nk = K // tk
    kb0 = b_koff // tk
    has_bias, has_res = bias is not None, res is not None

    def body(*refs):
        a_ref, b_ref = refs[0], refs[1]
        pos = 2
        bias_ref = res_ref = acc_ref = None
        if has_bias:
            bias_ref = refs[pos]
            pos += 1
        if has_res:
            res_ref = refs[pos]
            pos += 1
        if dep is not None:
            pos += 1
        extra = refs[pos:pos + 2 * len(more)]
        pos += 2 * len(more)
        tgt_ref = lp_ref = None
        if target is not None:
            tgt_ref, o_ref, lp_ref = refs[pos], refs[pos + 1], refs[pos + 2]
            pos += 2
        else:
            o_ref = refs[pos]
        if nk > 1:
            acc_ref = refs[pos + 1]
        part = _dot(a_ref[...].astype(BF16), b_ref[...].astype(BF16), 0 if ta else 1, 1 if tb else 0)
        for q in range(len(more)):
            part = part + _dot(extra[2 * q][...].astype(BF16), extra[2 * q + 1][...].astype(BF16),
                               0 if ta else 1, 1 if tb else 0)

        def finish(acc):
            if has_bias:
                acc = acc + bias_ref[...]
            if has_res:
                acc = acc + res_ref[...]
            if target is not None:
                err = acc - tgt_ref[...]
                acc = err * (1.0 / N)
                part_loss = jnp.sum(jnp.sum(err * err, axis=1, keepdims=True), axis=0, keepdims=True) * (0.5 / N)
                first = (pl.program_id(0) == 0) & (pl.program_id(1) == 0)

                @pl.when(first)
                def _():
                    lp_ref[...] = jnp.broadcast_to(part_loss, lp_ref.shape)

                @pl.when(jnp.logical_not(first))
                def _():
                    lp_ref[...] += jnp.broadcast_to(part_loss, lp_ref.shape)

            o_ref[...] = acc.astype(out_dtype)

        if nk == 1:
            finish(part)
        else:
            k = pl.program_id(2)

            @pl.when(k == 0)
            def _():
                acc_ref[...] = part

            @pl.when(k > 0)
            def _():
                acc_ref[...] += part

            @pl.when(k == nk - 1)
            def _():
                finish(acc_ref[...])

    if a_spec is None:
        a_spec = pl.BlockSpec((tk, tm), lambda i, j, k: (k, i)) if ta else pl.BlockSpec((tm, tk), lambda i, j, k: (i, k))
    if b_spec is None:
        b_spec = (pl.BlockSpec((tn, tk), lambda i, j, k: (j, k + kb0)) if tb
                  else pl.BlockSpec((tk, tn), lambda i, j, k: (k + kb0, j)))
    if o_spec is None:
        o_spec = pl.BlockSpec((tm, tn), lambda i, j, k: (i, j))
    in_specs, args = [a_spec, b_spec], [a, b]
    if has_bias:
        in_specs.append(pl.BlockSpec((1, tn), lambda i, j, k: (0, j)))
        args.append(bias)
    if has_res:
        in_specs.append(pl.BlockSpec((tm, tn), lambda i, j, k: (i, j)))
        args.append(res)
    if dep is not None:
        in_specs.append(pl.BlockSpec(memory_space=pl.ANY))
        args.append(dep)
    for piece in more:
        a2, sa, b2, sb = piece if len(piece) == 4 else (a, piece[0], b, piece[1])
        in_specs += [sa, sb]
        args += [a2, b2]
    out_specs, out_shape = o_spec, jax.ShapeDtypeStruct(o_shape or (M, N), out_dtype)
    if target is not None:
        in_specs.append(pl.BlockSpec((tm, tn), lambda i, j, k: (i, j)))
        args.append(target)
        out_specs = [o_spec, pl.BlockSpec((8, 128), lambda i, j, k: (0, 0))]
        out_shape = [out_shape, jax.ShapeDtypeStruct((8, 128), F32)]
    return pl.pallas_call(
        body, name=name, grid=(M // tm, N // tn, nk), in_specs=in_specs, out_specs=out_specs, out_shape=out_shape,
        scratch_shapes=[pltpu.VMEM((tm, tn), F32)] if nk > 1 else [],
        compiler_params=_cp(3),
    )(*args)


def _norm_mm(x, gain, b, *, name, bias=None, N=None, tn=None, b_spec=None, dep=None):
    M, K = x.shape
    N = N or b.shape[1]
    tm = _pick(M, (1024, 512, 256))
    tn = tn or _pick(N, (512, 1408, 256, 128))
    has_bias = bias is not None

    def body(*refs):
        x_ref, g_ref, b_ref = refs[:3]
        pos = 3 + (1 if has_bias else 0) + (0 if dep is None else 1)
        o_ref, h_ref = refs[pos], refs[pos + 1]

        @pl.when(pl.program_id(1) == 0)
        def _():
            xv = x_ref[...]
            h_ref[...] = (xv * lax.rsqrt(jnp.mean(xv * xv, axis=-1, keepdims=True) + EPS) * g_ref[...]).astype(BF16)

        acc = _dot(h_ref[...], b_ref[...].astype(BF16))
        if has_bias:
            acc = acc + refs[3][...]
        o_ref[...] = acc

    in_specs = [pl.BlockSpec((tm, K), lambda i, j: (i, 0)), pl.BlockSpec((1, K), lambda i, j: (0, 0)),
                b_spec or pl.BlockSpec((K, tn), lambda i, j: (0, j))]
    args = [x, gain, b]
    if has_bias:
        in_specs.append(pl.BlockSpec((1, tn), lambda i, j: (0, j)))
        args.append(bias)
    if dep is not None:
        in_specs.append(pl.BlockSpec(memory_space=pl.ANY))
        args.append(dep)
    return pl.pallas_call(
        body, name=name, grid=(M // tm, N // tn), in_specs=in_specs,
        out_specs=[pl.BlockSpec((tm, tn), lambda i, j: (i, j)), pl.BlockSpec((tm, K), lambda i, j: (i, 0))],
        out_shape=[jax.ShapeDtypeStruct((M, N), F32), jax.ShapeDtypeStruct((M, K), BF16)], compiler_params=_cp(2),
    )(*args)


def _rms_bwd(x, gains, dhs, dres, *, name, tr=256, want_colsum=False):
    S, D = x.shape
    n = len(gains)
    steps = S // tr

    def body(*refs):
        x_ref = refs[0]
        g_refs = refs[1:1 + n]
        dh_refs = refs[1 + n:1 + 2 * n]
        dres_ref = refs[1 + 2 * n]
        dx_ref = refs[2 + 2 * n]
        dg_refs = refs[3 + 2 * n:3 + 3 * n]
        cs_ref = refs[3 + 3 * n] if want_colsum else None
        i = pl.program_id(0)
        xv = x_ref[...]
        r = lax.rsqrt(jnp.mean(xv * xv, axis=-1, keepdims=True) + EPS)
        xh = xv * r
        dx = dres_ref[...]
        for q in range(n):
            dh = dh_refs[q][...]
            dxh = dh * g_refs[q][...]
            dx = dx + r * (dxh - xh * jnp.mean(dxh * xh, axis=-1, keepdims=True))
            part = jnp.sum(dh * xh, axis=0, keepdims=True)

            @pl.when(i == 0)
            def _():
                dg_refs[q][...] = part

            @pl.when(i > 0)
            def _():
                dg_refs[q][...] += part

        dx_ref[...] = dx
        if want_colsum:
            cpart = jnp.sum(dx, axis=0, keepdims=True)

            @pl.when(i == 0)
            def _():
                cs_ref[...] = cpart

            @pl.when(i > 0)
            def _():
                cs_ref[...] += cpart

    row = pl.BlockSpec((tr, D), lambda i: (i, 0))
    vec = pl.BlockSpec((1, D), lambda i: (0, 0))
    n_vec_out = n + (1 if want_colsum else 0)
    outs = pl.pallas_call(
        body, name=name, grid=(steps,), in_specs=[row] + [vec] * n + [row] * n + [row],
        out_specs=[row] + [vec] * n_vec_out,
        out_shape=[jax.ShapeDtypeStruct((S, D), F32)] + [jax.ShapeDtypeStruct((1, D), F32)] * n_vec_out,
        compiler_params=_cp(1),
    )(x, *gains, *dhs, dres)
    return outs


def _colsum(x, *, name, tr=256):
    S, D = x.shape

    def body(x_ref, o_ref):
        i = pl.program_id(0)
        part = jnp.sum(x_ref[...].astype(F32), axis=0, keepdims=True)

        @pl.when(i == 0)
        def _():
            o_ref[...] = part

        @pl.when(i > 0)
        def _():
            o_ref[...] += part

    return pl.pallas_call(
        body, name=name, grid=(S // tr,), in_specs=[pl.BlockSpec((tr, D), lambda i: (i, 0))],
        out_specs=pl.BlockSpec((1, D), lambda i: (0, 0)), out_shape=jax.ShapeDtypeStruct((1, D), F32),
        compiler_params=_cp(1),
    )(x)


STRIP = 64
HALO = 8


def _strips(S, tc):
    return [(r0, slice(l0, l0 + 128)) for l0 in range(0, tc, 128) for r0 in range(S - STRIP, -1, -STRIP)]


def _with_halo(ref, r0, ls):
    if r0 == 0:
        return jnp.concatenate([jnp.zeros((HALO, 128), F32), ref[0:STRIP, ls]], axis=0)
    return ref[r0 - HALO:r0 + STRIP, ls]


def _conv_strip(xw, w_ref, b_ref, ls, width):
    acc = b_ref[:, ls] + w_ref[pl.ds(width - 1, 1), ls] * xw[HALO:]
    shifted = []
    for s in range(1, width):
        xs = pltpu.roll(xw, s, axis=0)[HALO:]
        shifted.append(xs)
        acc = acc + w_ref[pl.ds(width - 1 - s, 1), ls] * xs
    return acc, shifted


def _conv_strip_back(dacc, after, xc, shifted, w_ref, ls, width):
    ext = jnp.concatenate([dacc, after], axis=0)
    dx = w_ref[pl.ds(width - 1, 1), ls] * dacc
    dws = [None] * width
    dws[width - 1] = jnp.sum(dacc * xc, axis=0, keepdims=True)
    for s in range(1, width):
        dx = dx + w_ref[pl.ds(width - 1 - s, 1), ls] * pltpu.roll(ext, STRIP + HALO - s, axis=0)[:STRIP]
        dws[width - 1 - s] = jnp.sum(dacc * shifted[s - 1], axis=0, keepdims=True)
    return dx, dws, jnp.sum(dacc, axis=0, keepdims=True)


def _conv_back_block(S, tc, width, w_ref, b_ref, x_ref, dacc_of, dx_store, dw_ref, db_ref):
    for l0 in range(0, tc, 128):
        ls = slice(l0, l0 + 128)
        after = jnp.zeros((HALO, 128), F32)
        tot = None
        for r0 in range(S - STRIP, -1, -STRIP):
            xw = _with_halo(x_ref, r0, ls)
            acc, shifted = _conv_strip(xw, w_ref, b_ref, ls, width)
            dacc = dacc_of(r0, ls, acc, _sigmoid(acc))
            dx, dws, db = _conv_strip_back(dacc, after, xw[HALO:], shifted, w_ref, ls, width)
            dx_store(r0, ls, dx)
            after = dacc[:HALO]
            part = dws + [db]
            tot = part if tot is None else [p + q for p, q in zip(tot, part)]
        for k in range(width):
            dw_ref[pl.ds(k, 1), ls] = tot[k]
        db_ref[:, ls] = tot[width]


def _conv_silu_fwd(xin, col0, C, w, b, *, name, tc=512):
    S = xin.shape[0]
    width = w.shape[0]
    off = col0 // tc

    def body(x_ref, w_ref, b_ref, o_ref):
        for r0, ls in _strips(S, tc):
            acc, _ = _conv_strip(_with_halo(x_ref, r0, ls), w_ref, b_ref, ls, width)
            o_ref[r0:r0 + STRIP, ls] = acc * _sigmoid(acc)

    return pl.pallas_call(
        body, name=name, grid=(C // tc,),
        in_specs=[pl.BlockSpec((S, tc), lambda j: (0, j + off)), pl.BlockSpec((width, tc), lambda j: (0, j)),
                  pl.BlockSpec((1, tc), lambda j: (0, j))],
        out_specs=pl.BlockSpec((S, tc), lambda j: (0, j)), out_shape=jax.ShapeDtypeStruct((S, C), F32),
        compiler_params=_cp(1),
    )(xin, w, b)


def _conv_silu_bwd(xin, col0, C, w, b, douts, *, name, tc=256):
    S = xin.shape[0]
    width = w.shape[0]
    off = col0 // tc
    nd = len(douts)
    ranges = [(o // tc, (o + d.shape[1]) // tc) for d, o in douts]

    def body(*refs):
        x_ref, w_ref, b_ref = refs[0], refs[1], refs[2]
        d_refs = refs[3:3 + nd]
        dx_ref, dw_ref, db_ref = refs[3 + nd], refs[4 + nd], refs[5 + nd]
        j = pl.program_id(0)

        def dacc_of(r0, ls, acc, sg):
            dout = jnp.zeros((STRIP, 128), F32)
            for q in range(nd):
                lo, hi = ranges[q]
                dout = dout + jnp.where((j >= lo) & (j < hi), d_refs[q][r0:r0 + STRIP, ls], 0.0)
            return dout * (sg * (1.0 + acc * (1.0 - sg)))

        def dx_store(r0, ls, dx):
            dx_ref[r0:r0 + STRIP, ls] = dx.astype(BF16)

        _conv_back_block(S, tc, width, w_ref, b_ref, x_ref, dacc_of, dx_store, dw_ref, db_ref)

    d_specs = [pl.BlockSpec((S, tc), (lambda j, lo=lo, hi=hi: (0, jnp.clip(j - lo, 0, hi - lo - 1)))) for lo, hi in ranges]
    return pl.pallas_call(
        body, name=name, grid=(C // tc,),
        in_specs=[pl.BlockSpec((S, tc), lambda j: (0, j + off)), pl.BlockSpec((width, tc), lambda j: (0, j)),
                  pl.BlockSpec((1, tc), lambda j: (0, j))] + d_specs,
        out_specs=[pl.BlockSpec((S, tc), lambda j: (0, j)), pl.BlockSpec((width, tc), lambda j: (0, j)),
                   pl.BlockSpec((1, tc), lambda j: (0, j))],
        out_shape=[jax.ShapeDtypeStruct((S, C), BF16), jax.ShapeDtypeStruct((width, C), F32),
                   jax.ShapeDtypeStruct((1, C), F32)],
        compiler_params=_cp(1),
    )(xin, w, b, *[d for d, _ in douts])


def _ffn_act_fwd(u, w, b, *, name, tc=256):
    S, F2 = u.shape
    Fd = F2 // 2
    width = w.shape[0]
    nb = Fd // tc

    def body(g_ref, v_ref, w_ref, b_ref, o_ref):
        for r0, ls in _strips(S, tc):
            acc, _ = _conv_strip(_with_halo(g_ref, r0, ls), w_ref, b_ref, ls, width)
            o_ref[r0:r0 + STRIP, ls] = (acc * _sigmoid(acc) * v_ref[r0:r0 + STRIP, ls]).astype(BF16)

    return pl.pallas_call(
        body, name=name, grid=(nb,),
        in_specs=[pl.BlockSpec((S, tc), lambda j: (0, j)), pl.BlockSpec((S, tc), lambda j: (0, j + nb)),
                  pl.BlockSpec((width, tc), lambda j: (0, j)), pl.BlockSpec((1, tc), lambda j: (0, j))],
        out_specs=pl.BlockSpec((S, tc), lambda j: (0, j)), out_shape=jax.ShapeDtypeStruct((S, Fd), BF16),
        compiler_params=_cp(1),
    )(u, u, w, b)


def _ffn_act_bwd(u, w, b, da, *, name, tc=256):
    S, F2 = u.shape
    Fd = F2 // 2
    width = w.shape[0]
    nb = Fd // tc

    def body(g_ref, v_ref, w_ref, b_ref, da_ref, du_ref, dw_ref, db_ref, a_ref):
        def dacc_of(r0, ls, acc, sg):
            rs = slice(r0, r0 + STRIP)
            dav, val, silu = da_ref[rs, ls], v_ref[rs, ls], acc * sg
            a_ref[rs, ls] = (silu * val).astype(BF16)
            du_ref[1, rs, ls] = (dav * silu).astype(BF16)
            return dav * val * (sg * (1.0 + acc * (1.0 - sg)))

        def dx_store(r0, ls, dx):
            du_ref[0, r0:r0 + STRIP, ls] = dx.astype(BF16)

        _conv_back_block(S, tc, width, w_ref, b_ref, g_ref, dacc_of, dx_store, dw_ref, db_ref)

    blk = pl.BlockSpec((S, tc), lambda j: (0, j))
    return pl.pallas_call(
        body, name=name, grid=(nb,),
        in_specs=[blk, pl.BlockSpec((S, tc), lambda j: (0, j + nb)), pl.BlockSpec((width, tc), lambda j: (0, j)),
                  pl.BlockSpec((1, tc), lambda j: (0, j)), blk],
        out_specs=[pl.BlockSpec((2, S, tc), lambda j: (0, 0, j)), pl.BlockSpec((width, tc), lambda j: (0, j)),
                   pl.BlockSpec((1, tc), lambda j: (0, j)), blk],
        out_shape=[jax.ShapeDtypeStruct((2, S, Fd), BF16),
                   jax.ShapeDtypeStruct((width, Fd), F32), jax.ShapeDtypeStruct((1, Fd), F32),
                   jax.ShapeDtypeStruct((S, Fd), BF16)],
        compiler_params=_cp(1),
    )(u, u, w, b, da)


def _ssd_prep(dtr, dt_bias, a_log, *, name="ssd_prep"):
    S = dtr.shape[0]

    def body(d_ref, b_ref, al_ref, dt_ref, ac_ref, sg_ref, act_ref):
        lane = _iota((CHUNK, 128), 1)
        valid = lane < SSM_HEADS
        z = d_ref[...] + b_ref[...]
        dt = jnp.where(valid, jnp.maximum(z, 0.0) + jnp.log(1.0 + jnp.exp(-jnp.abs(z))), 0.0)
        a = dt * (-jnp.exp(al_ref[...]))
        row = _iota((CHUNK, 128), 0)
        k = 1
        while k < CHUNK:
            a = a + jnp.where(row >= k, pltpu.roll(a, k, axis=0), 0.0)
            k *= 2
        sg = jnp.where(valid, _sigmoid(z), 0.0)
        for arr, ref in ((dt, dt_ref), (a, ac_ref), (sg, sg_ref)):
            for g in range(SSM_GROUPS):
                ref[g] = jnp.where(lane < 4, arr if g == 0 else pltpu.roll(arr, 128 - 4 * g, axis=1), 0.0)
        act_ref[...] = a.T[:SSM_HEADS, :]

    blk = pl.BlockSpec((CHUNK, 128), lambda i: (i, 0))
    vec = pl.BlockSpec((1, 128), lambda i: (0, 0))
    grp = pl.BlockSpec((SSM_GROUPS, CHUNK, 128), lambda i: (0, i, 0))
    return pl.pallas_call(
        body, name=name, grid=(S // CHUNK,), in_specs=[blk, vec, vec],
        out_specs=[grp, grp, grp, pl.BlockSpec((SSM_HEADS, CHUNK), lambda i: (0, i))],
        out_shape=[jax.ShapeDtypeStruct((SSM_GROUPS, S, 128), F32)] * 3 + [jax.ShapeDtypeStruct((SSM_HEADS, S), F32)],
        compiler_params=_cp(1),
    )(dtr, dt_bias, a_log)


SSD_GPS = 4


def _expand4(v, lanes):
    out = jnp.broadcast_to(v[:, 3:4], lanes.shape)
    for hh in (2, 1, 0):
        out = jnp.where(lanes < 64 * (hh + 1), v[:, hh:hh + 1], out)
    return out


def _ssd_fwd(xbc, dt_g, ac_g, ac_t, *, name="ssd_fwd", dep=None):
    S = xbc.shape[0]
    nc = S // CHUNK
    Lc = CHUNK

    def body(x_ref, b_ref, c_ref, dt_ref, ac_ref, act_ref, *rest):
        y_ref, st_out_ref, st_ref = rest[-3:]
        g2 = pl.program_id(0)
        c = pl.program_id(1)

        @pl.when(c == 0)
        def _():
            st_ref[...] = jnp.zeros_like(st_ref)

        causal = _iota((Lc, Lc), 0) >= _iota((Lc, Lc), 1)
        lane256 = _iota((Lc, 256), 1)
        lane128 = _iota((Lc, 128), 1)
        row128 = _iota((128, 128), 0)
        for gg in range(SSD_GPS):
            g = SSD_GPS * g2 + gg
            bv = b_ref[:, 128 * gg:128 * (gg + 1)]
            cbf = c_ref[:, 128 * gg:128 * (gg + 1)].astype(BF16)
            cb = _dot(cbf, bv.astype(BF16), 1, 1)
            dtg, acg = dt_ref[gg], ac_ref[gg]
            ac_last = ac_ref[gg, pl.ds(Lc - 1, 1), :]
            dt4 = _expand4(dtg, lane256)
            ac4 = _expand4(acg, lane256)
            e4 = jnp.exp(ac4)
            xdb = (x_ref[:, 256 * gg:256 * (gg + 1)] * dt4).astype(BF16)
            st_out_ref[gg] = st_ref[gg]
            for p in range(2):
                xd_p = xdb[:, 128 * p:128 * (p + 1)]
                st_p = st_ref[gg, p]
                ys, sn, cds = [], [], []
                for q in range(2):
                    hh = 2 * p + q
                    a_col = acg[:, hh:hh + 1]
                    a_row = act_ref[pl.ds(4 * g + hh, 1), :]
                    dec = jnp.exp(jnp.where(causal, a_col - a_row, NEG))
                    w = (cb * dec).astype(BF16)
                    ys.append(_dot(w, xd_p))
                    al = ac_last[:, hh:hh + 1]
                    dte = jnp.exp(al - a_col)
                    sn.append(_dot(xd_p, (bv * dte).astype(BF16), 0, 0))
                    cds.append(jnp.exp(al))
                y_diag = jnp.where(lane128 < 64, ys[0], ys[1])
                y_off = _dot(cbf, st_p.astype(BF16), 1, 1) * e4[:, 128 * p:128 * (p + 1)]
                y_ref[:, 256 * gg + 128 * p:256 * gg + 128 * (p + 1)] = y_diag + y_off
                st_ref[gg, p] = jnp.where(row128 < 64, st_p * cds[0] + sn[0], st_p * cds[1] + sn[1])

    G = SSD_GPS
    per_g = lambda g, c: (g, c, 0)
    return pl.pallas_call(
        body, name=name, grid=(SSM_GROUPS // G, nc),
        in_specs=[pl.BlockSpec((Lc, 256 * G), lambda g, c: (c, g)),
                  pl.BlockSpec((Lc, 128 * G), lambda g, c: (c, 16 // G + g)),
                  pl.BlockSpec((Lc, 128 * G), lambda g, c: (c, 24 // G + g)),
                  pl.BlockSpec((G, Lc, 128), per_g), pl.BlockSpec((G, Lc, 128), per_g),
                  pl.BlockSpec((SSM_HEADS, Lc), lambda g, c: (0, c))] + ([] if dep is None else [pl.BlockSpec(memory_space=pl.ANY)]),
        out_specs=[pl.BlockSpec((Lc, 256 * G), lambda g, c: (c, g)),
                   pl.BlockSpec((G, None, 2, 128, 128), lambda g, c: (g, c, 0, 0, 0))],
        out_shape=[jax.ShapeDtypeStruct((S, 2048), F32), jax.ShapeDtypeStruct((SSM_GROUPS, nc, 2, 128, 128), F32)],
        scratch_shapes=[pltpu.VMEM((G, 2, 128, 128), F32)], compiler_params=_cp(2),
    )(xbc, xbc, xbc, dt_g, ac_g, ac_t, *([] if dep is None else [dep]))


def _ssd_bwd(xbc, dt_g, ac_g, ac_t, states, dy, dexp, *, name="ssd_bwd", dep=None):
    S = xbc.shape[0]
    nc = S // CHUNK
    Lc = CHUNK

    def body(x_ref, b_ref, c_ref, dt_ref, ac_ref, act_ref, st_ref, dy_ref, d_ref, *rest):
        dx_ref, db_ref, dc_ref, dh_ref, ds_ref = rest[-5:]
        g2 = pl.program_id(0)
        cc = pl.program_id(1)

        @pl.when(cc == 0)
        def _():
            ds_ref[...] = jnp.zeros_like(ds_ref)

        causal = _iota((Lc, Lc), 0) >= _iota((Lc, Lc), 1)
        lane256 = _iota((Lc, 256), 1)
        lane128 = _iota((Lc, 128), 1)
        row128 = _iota((128, 128), 0)
        ind_rows = _iota((256, 128), 0) >> 6
        ind_cols = _iota((256, 128), 1)
        ind_a = (ind_rows == ind_cols).astype(BF16)
        ind_b = (ind_rows + 4 == ind_cols).astype(BF16)
        for gg in range(SSD_GPS):
            g = SSD_GPS * g2 + gg
            bv = b_ref[:, 128 * gg:128 * (gg + 1)]
            cv = c_ref[:, 128 * gg:128 * (gg + 1)]
            bbf, cbf = bv.astype(BF16), cv.astype(BF16)
            cb = _dot(cbf, bbf, 1, 1)
            dtg, acg = dt_ref[gg], ac_ref[gg]
            ac_last = ac_ref[gg, pl.ds(Lc - 1, 1), :]
            dt4 = _expand4(dtg, lane256)
            ac4 = _expand4(acg, lane256)
            acl4 = _expand4(ac_last, _iota((1, 256), 1))
            e4 = jnp.exp(ac4)
            dte4 = jnp.exp(acl4 - ac4)
            xv = x_ref[:, 256 * gg:256 * (gg + 1)]
            xd = xv * dt4
            xdb = xd.astype(BF16)
            dyv = dy_ref[:, 256 * gg:256 * (gg + 1)]
            dcb = jnp.zeros((Lc, Lc), F32)
            dc_acc = jnp.zeros((Lc, 128), F32)
            db_acc = jnp.zeros((Lc, 128), F32)
            u_parts, dxd_parts, ends = [], [], []
            for p in range(2):
                sl = slice(128 * p, 128 * (p + 1))
                xd_p, xdb_p, dy_p = xd[:, sl], xdb[:, sl], dyv[:, sl]
                dyb_p = dy_p.astype(BF16)
                e_p, dte_p = e4[:, sl], dte4[:, sl]
                sp = st_ref[gg, p]
                spb = sp.astype(BF16)
                dsn = ds_ref[gg, p]
                dsnb = dsn.astype(BF16)
                yds, dxds, cds = [], [], []
                for q in range(2):
                    hh = 2 * p + q
                    a_col = acg[:, hh:hh + 1]
                    a_row = act_ref[pl.ds(4 * g + hh, 1), :]
                    dec = jnp.exp(jnp.where(causal, a_col - a_row, NEG))
                    w = (cb * dec).astype(BF16)
                    head = (lane128 < 64) if q == 0 else (lane128 >= 64)
                    dym = jnp.where(head, dyb_p, jnp.zeros_like(dyb_p))
                    dw = _dot(dym, xdb_p, 1, 1)
                    dcb = dcb + dw * dec
                    yds.append(_dot(w, xdb_p))
                    dxds.append(_dot(w, dyb_p, 0, 0))
                    cds.append(jnp.exp(ac_last[:, hh:hh + 1]))
                y_diag = jnp.where(lane128 < 64, yds[0], yds[1])
                dxd_diag = jnp.where(lane128 < 64, dxds[0], dxds[1])
                y_off = _dot(cbf, spb, 1, 1) * e_p
                dgp = dy_p * e_p
                dgb = dgp.astype(BF16)
                dc_acc = dc_acc + _dot(dgb, spb)
                dsp = _dot(dgb, cbf, 0, 0)
                cd_col = jnp.where(row128[:, 0:1] < 64, cds[0], cds[1])
                qm = _dot(bbf, dsnb, 1, 1)
                dxd_state = dte_p * qm
                db_acc = db_acc + _dot((xd_p * dte_p).astype(BF16), dsnb)
                t_p = xd_p * dxd_state
                prod = dsn * sp
                e0 = jnp.sum(jnp.sum(jnp.where(row128 < 64, prod, 0.0), axis=1, keepdims=True), axis=0, keepdims=True)
                e1 = jnp.sum(jnp.sum(jnp.where(row128 >= 64, prod, 0.0), axis=1, keepdims=True), axis=0, keepdims=True)
                tcol = jnp.sum(t_p, axis=0, keepdims=True)
                lane1 = _iota((1, 128), 1)
                t0 = jnp.sum(jnp.where(lane1 < 64, tcol, 0.0), axis=1, keepdims=True)
                t1 = jnp.sum(jnp.where(lane1 >= 64, tcol, 0.0), axis=1, keepdims=True)
                ends.append(e0 * cds[0] + t0)
                ends.append(e1 * cds[1] + t1)
                ds_ref[gg, p] = dsn * cd_col + dsp
                u_parts.append(dyb_p.astype(F32) * y_diag - xdb_p.astype(F32) * dxd_diag + dy_p * y_off - t_p)
                dxd_parts.append(dxd_diag + dxd_state)
            dxd = jnp.concatenate(dxd_parts, axis=1)
            u_all = jnp.concatenate(u_parts, axis=1)
            dx_ref[:, 256 * gg:256 * (gg + 1)] = dxd * dt4 + dyv * d_ref[:, 256 * gg:256 * (gg + 1)]
            dcbb = dcb.astype(BF16)
            dc_ref[:, 128 * gg:128 * (gg + 1)] = dc_acc + _dot(dcbb, bbf)
            db_ref[:, 128 * gg:128 * (gg + 1)] = db_acc + _dot(dcbb, cbf, 0, 0)
            lane = _iota((Lc, 128), 1)
            endv = jnp.zeros((Lc, 128), F32)
            for hh in range(4):
                endv = jnp.where(lane == 8 + hh, ends[hh], endv)
            dh_ref[gg] = _dot3(dxd * xv, ind_a) + _dot3(u_all, ind_b) + endv

    G = SSD_GPS
    rev = lambda c: nc - 1 - c
    per_g = lambda g, c: (g, rev(c), 0)
    return pl.pallas_call(
        body, name=name, grid=(SSM_GROUPS // G, nc),
        in_specs=[pl.BlockSpec((Lc, 256 * G), lambda g, c: (rev(c), g)),
                  pl.BlockSpec((Lc, 128 * G), lambda g, c: (rev(c), 16 // G + g)),
                  pl.BlockSpec((Lc, 128 * G), lambda g, c: (rev(c), 24 // G + g)),
                  pl.BlockSpec((G, Lc, 128), per_g), pl.BlockSpec((G, Lc, 128), per_g),
                  pl.BlockSpec((SSM_HEADS, Lc), lambda g, c: (0, rev(c))),
                  pl.BlockSpec((G, None, 2, 128, 128), lambda g, c: (g, rev(c), 0, 0, 0)),
                  pl.BlockSpec((Lc, 256 * G), lambda g, c: (rev(c), g)),
                  pl.BlockSpec((1, 256 * G), lambda g, c: (0, g))] + ([] if dep is None else [pl.BlockSpec(memory_space=pl.ANY)]),
        out_specs=[pl.BlockSpec((Lc, 256 * G), lambda g, c: (rev(c), g)),
                   pl.BlockSpec((Lc, 128 * G), lambda g, c: (rev(c), g)),
                   pl.BlockSpec((Lc, 128 * G), lambda g, c: (rev(c), g)),
                   pl.BlockSpec((G, Lc, 128), per_g)],
        out_shape=[jax.ShapeDtypeStruct((S, 2048), F32), jax.ShapeDtypeStruct((S, 1024), F32),
                   jax.ShapeDtypeStruct((S, 1024), F32), jax.ShapeDtypeStruct((SSM_GROUPS, S, 128), F32)],
        scratch_shapes=[pltpu.VMEM((G, 2, 128, 128), F32)], compiler_params=_cp(2),
    )(xbc, xbc, xbc, dt_g, ac_g, ac_t, states, dy, dexp, *([] if dep is None else [dep]))


def _ssd_post(dhead, dt_g, sg_g, alog_g, *, name="ssd_post"):
    S = dhead.shape[1]
    nc = S // CHUNK
    Lc = CHUNK

    def body(dh_ref, dt_ref, sg_ref, al_ref, o_ref, s_ref):
        @pl.when(pl.program_id(0) == 0)
        def _():
            s_ref[...] = jnp.zeros_like(s_ref)

        lane = _iota((Lc, 128), 1)
        row = _iota((Lc, 128), 0)
        row8 = _iota((8, 128), 0)
        out = jnp.zeros((Lc, 128), F32)
        for g in range(SSM_GROUPS):
            dh = dh_ref[g]
            a_neg = -jnp.exp(al_ref[g])
            dac = jnp.where(lane < 4, pltpu.roll(dh, 124, axis=1), 0.0)
            end = jnp.where(lane < 4, pltpu.roll(dh, 120, axis=1), 0.0)
            k = 1
            while k < Lc:
                dac = dac + jnp.where(row < Lc - k, pltpu.roll(dac, Lc - k, axis=0), 0.0)
                k *= 2
            da = dac + end
            ddt = jnp.where(lane < 4, da * a_neg + dh, 0.0)
            ddtr = ddt * sg_ref[g]
            out = out + (ddtr if g == 0 else pltpu.roll(ddtr, 4 * g, axis=1))
            dal = jnp.sum(da * dt_ref[g], axis=0, keepdims=True) * a_neg
            dbias = jnp.sum(ddtr, axis=0, keepdims=True)
            part = jnp.where(row8 == 0, dal, jnp.where(row8 == 1, dbias, 0.0))
            s_ref[g] += part
        o_ref[...] = out.astype(BF16)

    grp = pl.BlockSpec((SSM_GROUPS, Lc, 128), lambda c: (0, c, 0))
    whole = lambda r: pl.BlockSpec((SSM_GROUPS, r, 128), lambda c: (0, 0, 0))
    return pl.pallas_call(
        body, name=name, grid=(nc,), in_specs=[grp, grp, grp, whole(1)],
        out_specs=[pl.BlockSpec((Lc, 128), lambda c: (c, 0)), whole(8)],
        out_shape=[jax.ShapeDtypeStruct((S, 128), BF16), jax.ShapeDtypeStruct((SSM_GROUPS, 8, 128), F32)],
        compiler_params=_cp(1),
    )(dhead, dt_g, sg_g, alog_g)


def _gate_fwd(y, xbc, zx, dexp, gn, *, name="gate_fwd", tr=256):
    S = y.shape[0]
    W = 2048
    gw = W // SSM_GROUPS

    def body(y_ref, x_ref, z_ref, d_ref, g_ref, o_ref):
        z = z_ref[...]
        u = (y_ref[...] + x_ref[...] * d_ref[...]) * (z * _sigmoid(z))
        gv = g_ref[...]
        for q in range(SSM_GROUPS):
            sl = slice(gw * q, gw * (q + 1))
            uq = u[:, sl]
            r = lax.rsqrt(jnp.mean(uq * uq, axis=-1, keepdims=True) + EPS)
            o_ref[:, sl] = (uq * r * gv[:, sl]).astype(BF16)

    row = pl.BlockSpec((tr, W), lambda i: (i, 0))
    vec = pl.BlockSpec((1, W), lambda i: (0, 0))
    return pl.pallas_call(
        body, name=name, grid=(S // tr,), in_specs=[row, row, row, vec, vec], out_specs=row,
        out_shape=jax.ShapeDtypeStruct((S, W), BF16), compiler_params=_cp(1),
    )(y, xbc, zx, dexp, gn)


def _gate_bwd(y, xbc, zx, dexp, gn, dout, *, name="gate_bwd", tr=256):
    S = y.shape[0]
    W = 2048
    gw = W // SSM_GROUPS
    steps = S // tr

    def body(y_ref, x_ref, z_ref, d_ref, g_ref, do_ref, dy_ref, dz_ref, dg_ref, dd_ref, acc_ref):
        i = pl.program_id(0)

        @pl.when(i == 0)
        def _():
            acc_ref[...] = jnp.zeros_like(acc_ref)

        z = z_ref[...]
        sg = _sigmoid(z)
        sz = z * sg
        xs = x_ref[...]
        yt = y_ref[...] + xs * d_ref[...]
        u = yt * sz
        gv = g_ref[...]
        do = do_ref[...]
        dgs = []
        for q in range(SSM_GROUPS):
            sl = slice(gw * q, gw * (q + 1))
            uq = u[:, sl]
            r = lax.rsqrt(jnp.mean(uq * uq, axis=-1, keepdims=True) + EPS)
            uh = uq * r
            dq = do[:, sl]
            duh = dq * gv[:, sl]
            duq = r * (duh - uh * jnp.mean(duh * uh, axis=-1, keepdims=True))
            dgs.append(jnp.sum(dq * uh, axis=0, keepdims=True))
            dyt = duq * sz[:, sl]
            dy_ref[:, sl] = dyt
            dz_ref[:, sl] = (duq * yt[:, sl] * (sg[:, sl] * (1.0 + z[:, sl] * (1.0 - sg[:, sl])))).astype(BF16)
            acc_ref[:, sl] += jnp.sum(dyt * xs[:, sl], axis=0, keepdims=True)
        dg = jnp.concatenate(dgs, axis=1)

        @pl.when(i == 0)
        def _():
            dg_ref[...] = dg

        @pl.when(i > 0)
        def _():
            dg_ref[...] += dg

        @pl.when(i == steps - 1)
        def _():
            ind = ((_iota((W, 128), 0) >> 6) == _iota((W, 128), 1)).astype(BF16)
            dd_ref[...] = _dot3(jnp.broadcast_to(acc_ref[...], (8, W)), ind)[0:1, :]

    row = pl.BlockSpec((tr, W), lambda i: (i, 0))
    vec = pl.BlockSpec((1, W), lambda i: (0, 0))
    return pl.pallas_call(
        body, name=name, grid=(steps,), in_specs=[row, row, row, vec, vec, row],
        out_specs=[row, row, vec, pl.BlockSpec((1, 128), lambda i: (0, 0))],
        out_shape=[jax.ShapeDtypeStruct((S, W), F32), jax.ShapeDtypeStruct((S, W), BF16),
                   jax.ShapeDtypeStruct((1, W), F32), jax.ShapeDtypeStruct((1, 128), F32)],
        scratch_shapes=[pltpu.VMEM((1, W), F32)], compiler_params=_cp(1),
    )(y, xbc, zx, dexp, gn, dout)


def _rope_cs(posf, *, name="rope_tables", tr=256):
    S = posf.shape[0]

    def body(p_ref, c_ref, s_ref):
        j = (_iota((tr, 128), 1) & 31).astype(F32)
        ang = p_ref[...] * jnp.exp(j * (-math.log(ROPE_THETA) / 32.0))
        c_ref[...] = jnp.cos(ang)
        s_ref[...] = jnp.sin(ang)

    blk = pl.BlockSpec((tr, 128), lambda i: (i, 0))
    return pl.pallas_call(
        body, name=name, grid=(S // tr,), in_specs=[pl.BlockSpec((tr, 1), lambda i: (i, 0))], out_specs=[blk, blk],
        out_shape=[jax.ShapeDtypeStruct((S, 128), F32)] * 2, compiler_params=_cp(1),
    )(posf)


def _rope_tables(c_ref, s_ref, shape):
    reps = shape[1] // 128
    return jnp.tile(c_ref[...], (1, reps)), jnp.tile(s_ref[...], (1, reps)), (_iota(shape, 1) & 63) < 32


def _hn_inds(W):
    ind = ((_iota((W, 128), 0) >> 6) == _iota((W, 128), 1)).astype(BF16)
    ind_t = ((_iota((128, W), 1) >> 6) == _iota((128, W), 0)).astype(BF16)
    return ind, ind_t


def _hnrope_fwd(xin, col0, W, gain_w, rope, *, name, tr=256):
    S = xin.shape[0]
    off = col0 // W
    nh = W // HEAD

    def body(x_ref, g_ref, c_ref, s_ref, o_ref):
        x = x_ref[...]
        ind, ind_t = _hn_inds(W)
        r = lax.rsqrt(_dot3(x * x, ind) * (1.0 / HEAD) + EPS)
        xn = x * _dot3(r, ind_t) * g_ref[...]
        cs, sn, half = _rope_tables(c_ref, s_ref, (tr, W))
        rot = jnp.where(half, -pltpu.roll(xn, W - 32, axis=1), pltpu.roll(xn, 32, axis=1))
        out = (xn * cs + rot * sn).astype(BF16)
        for h in range(nh):
            o_ref[h] = out[:, HEAD * h:HEAD * (h + 1)]

    tab = pl.BlockSpec((tr, 128), lambda i: (i, 0))
    return pl.pallas_call(
        body, name=name, grid=(S // tr,),
        in_specs=[pl.BlockSpec((tr, W), lambda i: (i, off)), pl.BlockSpec((1, W), lambda i: (0, 0)), tab, tab],
        out_specs=pl.BlockSpec((nh, tr, HEAD), lambda i: (0, i, 0)), out_shape=jax.ShapeDtypeStruct((nh, S, HEAD), BF16),
        compiler_params=_cp(1),
    )(xin, gain_w, *rope)


def _hnrope_bwd(xin, col0, W, gain_w, rope, dout, *, name, tr=256):
    S = xin.shape[0]
    off = col0 // W
    steps = S // tr
    nh = W // HEAD

    def body(x_ref, g_ref, c_ref, s_ref, do_ref, dx_ref, cs_ref, dg_ref, acc_ref):
        i = pl.program_id(0)
        x = x_ref[...]
        ind, ind_t = _hn_inds(W)
        r = lax.rsqrt(_dot3(x * x, ind) * (1.0 / HEAD) + EPS)
        rw = _dot3(r, ind_t)
        xh = x * rw
        cs, sn, half = _rope_tables(c_ref, s_ref, (tr, W))
        do = jnp.concatenate([do_ref[h] for h in range(nh)], axis=1).astype(F32)
        gs = do * sn
        g1 = do * cs + jnp.where(half, pltpu.roll(gs, W - 32, axis=1), -pltpu.roll(gs, 32, axis=1))
        dxh = g1 * g_ref[...]
        t = _dot3(dxh * xh, ind) * (1.0 / HEAD)
        dx = rw * (dxh - xh * _dot3(t, ind_t))
        dx_ref[...] = dx.astype(BF16)
        cpart = jnp.sum(dx, axis=0, keepdims=True)
        gpart = jnp.sum(g1 * xh, axis=0, keepdims=True)

        @pl.when(i == 0)
        def _():
            cs_ref[...] = cpart
            acc_ref[...] = gpart

        @pl.when(i > 0)
        def _():
            cs_ref[...] += cpart
            acc_ref[...] += gpart

        @pl.when(i == steps - 1)
        def _():
            fold = ((_iota((W, 128), 0) & 63) == _iota((W, 128), 1)).astype(BF16)
            dg_ref[...] = _dot3(jnp.broadcast_to(acc_ref[...], (8, W)), fold)[0:1, :]

    tab = pl.BlockSpec((tr, 128), lambda i: (i, 0))
    return pl.pallas_call(
        body, name=name, grid=(steps,),
        in_specs=[pl.BlockSpec((tr, W), lambda i: (i, off)), pl.BlockSpec((1, W), lambda i: (0, 0)), tab, tab,
                  pl.BlockSpec((nh, tr, HEAD), lambda i: (0, i, 0))],
        out_specs=[pl.BlockSpec((tr, W), lambda i: (i, 0)), pl.BlockSpec((1, W), lambda i: (0, 0)),
                   pl.BlockSpec((1, 128), lambda i: (0, 0))],
        out_shape=[jax.ShapeDtypeStruct((S, W), BF16), jax.ShapeDtypeStruct((1, W), F32),
                   jax.ShapeDtypeStruct((1, 128), F32)],
        scratch_shapes=[pltpu.VMEM((1, W), F32)], compiler_params=_cp(1),
    )(xin, gain_w, *rope, dout)


def _attn_band():
    qi = jnp.arange(ATT_G * WINDOW)[:, None] % WINDOW
    ki = jnp.arange(2 * WINDOW)[None, :]
    rel = qi + WINDOW - ki
    ok = (rel >= 0) & (rel < WINDOW)
    return jnp.stack([jnp.where(ok & (ki >= WINDOW), 0.0, NEG), jnp.where(ok, 0.0, NEG)]).astype(F32)


def _attn_probs(q, kb, sink_ref, band_ref, h, i):
    s = _dot(q, kb, 1, 1) * (HEAD ** -0.5) + band_ref[jnp.minimum(i, 1)]
    r1 = _iota((4 * WINDOW, 1), 0)
    sink = jnp.where(r1 < WINDOW, sink_ref[4 * h], jnp.where(r1 < 2 * WINDOW, sink_ref[4 * h + 1],
                     jnp.where(r1 < 3 * WINDOW, sink_ref[4 * h + 2], sink_ref[4 * h + 3])))
    m = jnp.maximum(jnp.max(s, axis=1, keepdims=True), sink)
    p = jnp.exp(s - m)
    ps = jnp.exp(sink - m)
    inv = 1.0 / (jnp.sum(p, axis=1, keepdims=True) + ps)
    return p * inv, ps * inv


ATT_HPS = 4
_BAND = pl.BlockSpec((2, ATT_G * WINDOW, 2 * WINDOW), lambda h, i: (0, 0, 0))


def _attn_specs(S):
    qspec = pl.BlockSpec((ATT_HPS, ATT_G, WINDOW, HEAD), lambda h, i: (h, 0, i, 0))
    cur = pl.BlockSpec((ATT_HPS, WINDOW, HEAD), lambda h, i: (h, i, 0))
    prev = pl.BlockSpec((ATT_HPS, WINDOW, HEAD), lambda h, i: (h, jnp.maximum(i - 1, 0), 0))
    tok = pl.BlockSpec((WINDOW, ATT_HPS * ATT_G * HEAD), lambda h, i: (i, h))
    return qspec, cur, prev, tok


def _attn_fwd(qh, kh, vh, sinks, *, name="attn_fwd"):
    S = kh.shape[1]
    nb = S // WINDOW

    def body(s_ref, band_ref, q_ref, kc_ref, kp_ref, vc_ref, vp_ref, o_ref):
        h2, i = pl.program_id(0), pl.program_id(1)
        outs = []
        for hh in range(ATT_HPS):
            q = q_ref[hh].reshape(ATT_G * WINDOW, HEAD)
            kb = jnp.concatenate([kp_ref[hh], kc_ref[hh]], axis=0)
            vb = jnp.concatenate([vp_ref[hh], vc_ref[hh]], axis=0)
            probs, _ = _attn_probs(q, kb, s_ref, band_ref, ATT_HPS * h2 + hh, i)
            o = _dot(probs.astype(BF16), vb).astype(BF16)
            outs += [o[WINDOW * g:WINDOW * (g + 1)] for g in range(ATT_G)]
        o_ref[...] = jnp.concatenate(outs, axis=1)

    qspec, cur, prev, tok = _attn_specs(S)
    return pl.pallas_call(
        body, name=name, grid=(ATT_KV // ATT_HPS, nb),
        in_specs=[pl.BlockSpec(memory_space=pltpu.SMEM), _BAND, qspec, cur, prev, cur, prev], out_specs=tok,
        out_shape=jax.ShapeDtypeStruct((S, ATT_KV * ATT_G * HEAD), BF16), compiler_params=_cp(2),
    )(sinks, _attn_band(), qh, kh, kh, vh, vh)


def _attn_bwd(qh, kh, vh, sinks, doh, *, name="attn_bwd"):
    S = kh.shape[1]
    nb = S // WINDOW

    def body(s_ref, band_ref, q_ref, kc_ref, kp_ref, vc_ref, vp_ref, do_ref, dq_ref, dk_ref, dv_ref, dsk_ref):
        h2, i = pl.program_id(0), pl.program_id(1)

        @pl.when(i == 0)
        def _():
            dk_ref[...] = jnp.zeros_like(dk_ref)
            dv_ref[...] = jnp.zeros_like(dv_ref)
            dsk_ref[...] = jnp.zeros_like(dsk_ref)

        dov = do_ref[...]
        cur = pl.multiple_of(i * WINDOW, WINDOW)
        lane = _iota((8, 128), 1)
        row = _iota((8, 128), 0)
        scale = HEAD ** -0.5
        for hh in range(ATT_HPS):
            q = q_ref[hh].reshape(ATT_G * WINDOW, HEAD)
            do = jnp.concatenate([dov[:, HEAD * (ATT_G * hh + g):HEAD * (ATT_G * hh + g + 1)] for g in range(ATT_G)], axis=0)
            kb = jnp.concatenate([kp_ref[hh], kc_ref[hh]], axis=0)
            vb = jnp.concatenate([vp_ref[hh], vc_ref[hh]], axis=0)
            probs, psink = _attn_probs(q, kb, s_ref, band_ref, ATT_HPS * h2 + hh, i)
            dp = _dot(do, vb, 1, 1)
            delta = jnp.sum(probs * dp, axis=1, keepdims=True)
            ds = (probs * (dp - delta)).astype(BF16)
            dq_ref[hh] = (_dot(ds, kb) * scale).reshape(ATT_G, WINDOW, HEAD)
            dkb = _dot(ds, q, 0, 0) * scale
            dvb = _dot(probs.astype(BF16), do, 0, 0)
            dk_ref[hh, pl.ds(cur, WINDOW), :] += dkb[WINDOW:, :]
            dv_ref[hh, pl.ds(cur, WINDOW), :] += dvb[WINDOW:, :]
            prv = pl.multiple_of(jnp.maximum(i - 1, 0) * WINDOW, WINDOW)
            dk_ref[hh, pl.ds(prv, WINDOW), :] += dkb[:WINDOW, :]
            dv_ref[hh, pl.ds(prv, WINDOW), :] += dvb[:WINDOW, :]

            dsr = -psink * delta
            upd = jnp.zeros((8, 128), F32)
            for gq in range(ATT_G):
                v = jnp.sum(dsr[gq * WINDOW:(gq + 1) * WINDOW, :], axis=0, keepdims=True)
                upd = jnp.where((lane == gq) & (row == 0), v, upd)
            dsk_ref[hh] += upd

    qspec, cur, prev, tok = _attn_specs(S)
    full = pl.BlockSpec((ATT_HPS, S, HEAD), lambda h, i: (h, 0, 0))
    return pl.pallas_call(
        body, name=name, grid=(ATT_KV // ATT_HPS, nb),
        in_specs=[pl.BlockSpec(memory_space=pltpu.SMEM), _BAND, qspec, cur, prev, cur, prev, tok],
        out_specs=[qspec, full, full, pl.BlockSpec((ATT_HPS, 8, 128), lambda h, i: (h, 0, 0))],
        out_shape=[jax.ShapeDtypeStruct((ATT_KV, ATT_G, S, HEAD), F32), jax.ShapeDtypeStruct((ATT_KV, S, HEAD), F32),
                   jax.ShapeDtypeStruct((ATT_KV, S, HEAD), F32), jax.ShapeDtypeStruct((ATT_KV, 8, 128), F32)],
        compiler_params=_cp(2),
    )(sinks, _attn_band(), qh, kh, kh, vh, vh, doh)


def _heads_major(t, nh):
    S = t.shape[0]
    return t.reshape(S, nh, HEAD).transpose(1, 0, 2)


def _tokens_major(t):
    nh, S, _ = t.shape
    return t.transpose(1, 0, 2).reshape(S, nh * HEAD)


class _NoComm:
    def late_start(self, part, after):
        return None

    def late_arrived(self, part, after):
        return None

    def late_weights(self, w, after, part):
        return w

    def advance(self, after, group=None, tensors=None):
        return None


def _local_step(x, posf, target, w, comm=None):
    S, D = x.shape
    gr = {}
    comm = comm or _NoComm()

    zx, h1 = _norm_mm(x, w["a_norm"], w["w_zx"], name="in_proj_zx", dep=w.get("dep"))
    dtr = _mm(h1, w["w_dt"], name="in_proj_dt")
    xbc = _conv_silu_fwd(zx, 2048, 4096, w["a_conv_w"], w["a_conv_b"], name="a_conv_f")
    dt_g, ac_g, sg_g, ac_t = _ssd_prep(dtr, w["a_dt_bias"], w["a_A_log"])
    tok = comm.late_start(1, xbc)
    y_ssd, states = _ssd_fwd(xbc, dt_g, ac_g, ac_t, dep=comm.late_arrived(0, [dt_g] + ([] if tok is None else [tok])))
    yg = _gate_fwd(y_ssd, xbc, zx, w["a_Dexp"], w["a_gnorm"])
    w = comm.late_weights(w, yg, 0)
    x1 = _mm(yg, w["a_out_proj"], res=x, name="out_proj")

    FW = w["f_w_in"][0].shape[2]

    def ffn_fwd(xin, l, loss_target=None):
        u, h = _norm_mm(xin, w["f_norm"][l], w["f_w_in"][l], name=f"f_in{l}", N=N_CHIPS * FW, tn=FW,
                        b_spec=pl.BlockSpec((None, D, FW), lambda i, j: (j, 0, 0)))
        a = _ffn_act_fwd(u, w["f_conv_w"][l], w["f_conv_b"][l], name=f"f_act_f{l}")
        dep = comm.late_arrived(1, [u]) if l == 0 else None
        xo = _mm(a, w["f_w_down"][l], res=xin, tk=a.shape[1], name=f"f_down{l}", target=loss_target, dep=dep)
        return xo, (h, u)

    x2, ffn0 = ffn_fwd(x1, 0)
    w = comm.late_weights(w, x2, 1)

    kv, hk = _norm_mm(x2, w["kv_norm"], w["w_kv"], bias=w["b_kv"], name="kv_proj")
    q, hq = _norm_mm(x2, w["b_norm"], w["w_q"], bias=w["b_q"], name="q_proj")
    rope = _rope_cs(posf)
    kr = _hnrope_fwd(kv, 0, 256, w["k_norm_w"], rope, name="k_rope_f")
    qr = _hnrope_fwd(q, 0, 1024, w["q_norm_w"], rope, name="q_rope_f")
    qh = qr.reshape(ATT_KV, ATT_G, S, HEAD)
    kh = kr
    vh = _heads_major(kv[:, 256:].astype(BF16), ATT_KV)
    att = _attn_fwd(qh, kh, vh, w["sinks"])
    x3 = _mm(att, w["w_o"], bias=w["b_o"], res=x2, name="o_proj")
    (dy, loss_part), ffn1 = ffn_fwd(x3, 1, target)

    def ffn_bwd(xin, l, saved, dyo, want_colsum, dep=None):
        h, u = saved
        da = _mm(dyo, w["f_w_down"][l], tb=True, name=f"f_down_dx{l}", dep=dep)
        du, dcw, dcb, a = _ffn_act_bwd(u, w["f_conv_w"][l], w["f_conv_b"][l], da, name=f"f_act_b{l}")
        dw_down = _mm(a, dyo, ta=True, out_dtype=BF16, name=f"f_down_dw{l}")
        dw_in = _mm(h, du, ta=True, out_dtype=BF16, name=f"f_in_dw{l}", dims=(D, N_CHIPS * FW, S), tm=D, tn=FW, tk=S,
                    b_spec=pl.BlockSpec((None, S, FW), lambda i, j, k: (j // 2, 0, j % 2)),
                    o_spec=pl.BlockSpec((None, D, FW), lambda i, j, k: (j, i, 0)), o_shape=(N_CHIPS, D, FW))
        ts = _pick(S, (1024, 512, 256))
        pieces = [(pl.BlockSpec((None, ts, FW), lambda i, j, k, q=q: (q // 2, i, q % 2)),
                   pl.BlockSpec((None, 512, FW), lambda i, j, k, q=q: (q, j, 0))) for q in range(N_CHIPS)]
        dh = _mm(du, w["f_w_in"][l], tb=True, name=f"f_in_dx{l}", dims=(S, D, FW), tm=ts, tn=512, tk=FW,
                 a_spec=pieces[0][0], b_spec=pieces[0][1], more=pieces[1:])
        outs = _rms_bwd(xin, [w["f_norm"][l]], [dh], dyo, name=f"f_norm_b{l}", want_colsum=want_colsum)
        g = dict(f_norm=outs[1], f_w_in=dw_in, f_conv_w=dcw, f_conv_b=dcb, f_w_down=dw_down)
        return outs[0], g, (outs[2] if want_colsum else None)

    dx3, gr["ffn1"], db_o = ffn_bwd(x3, 1, ffn1, dy, True)
    gr["b_o"] = db_o
    gr["w_o"] = _mm(att, dx3, ta=True, out_dtype=BF16, name="o_proj_dw")
    datt = _mm(dx3, w["w_o"], tb=True, out_dtype=BF16, name="o_proj_dx")
    dqh, dkh, dvh, dsk = _attn_bwd(qh, kh, vh, w["sinks"], datt)
    gr["sinks"] = dsk[:, 0, :4].reshape(1, 16)
    dv = _tokens_major(dvh).astype(BF16)
    dq, db_q, dqn = _hnrope_bwd(q, 0, 1024, w["q_norm_w"], rope, dqh.reshape(16, S, HEAD), name="q_rope_b")
    dk, db_k, dkn = _hnrope_bwd(kv, 0, 256, w["k_norm_w"], rope, dkh, name="k_rope_b")
    gr["q_norm"], gr["k_norm"] = dqn[:, :HEAD], dkn[:, :HEAD]
    gr["b_q"] = db_q
    gr["b_kv"] = jnp.concatenate([db_k, _colsum(dv, name="dv_colsum")], axis=1)
    dkv = jnp.concatenate([dk, dv], axis=1)
    gr["w_q"] = _mm(hq, dq, ta=True, out_dtype=BF16, name="q_proj_dw")
    gr["w_kv"] = _mm(hk, dkv, ta=True, out_dtype=BF16, name="kv_proj_dw")
    tok = comm.advance([gr["w_kv"]], 1, dict(f_down1=gr["ffn1"]["f_w_down"], f_in1=gr["ffn1"]["f_w_in"], w_o=gr["w_o"],
                                             w_q=gr["w_q"], w_kv=gr["w_kv"]))
    dhq = _mm(dq, w["w_q"], tb=True, name="q_proj_dx", dep=tok)
    dhk = _mm(dkv, w["w_kv"], tb=True, name="kv_proj_dx")
    dx2, gr["kv_norm"], gr["b_norm"] = _rms_bwd(x2, [w["kv_norm"], w["b_norm"]], [dhk, dhq], dx3, name="kvq_norm_b")

    dx1, gr["ffn0"], _ = ffn_bwd(x1, 0, ffn0, dx2, False, dep=comm.advance([dx2]))

    gr["a_out_proj"] = _mm(yg, dx1, ta=True, out_dtype=BF16, name="out_proj_dw")
    tok = comm.advance([dx1, gr["a_out_proj"]], 2,
                       dict(f_down0=gr["ffn0"]["f_w_down"], f_in0=gr["ffn0"]["f_w_in"], out_proj=gr["a_out_proj"]))
    dyg = _mm(dx1, w["a_out_proj"], tb=True, name="out_proj_dx", dep=tok)
    dy_ssd, dz, gr["a_gnorm"], dD = _gate_bwd(y_ssd, xbc, zx, w["a_Dexp"], w["a_gnorm"], dyg)
    gr["a_D"] = dD[:, :SSM_HEADS]
    dxs, dB, dC, dhead = _ssd_bwd(xbc, dt_g, ac_g, ac_t, states, dy_ssd, w["a_Dexp"], dep=comm.advance([dy_ssd]))
    ddtr, dsmall = _ssd_post(dhead, dt_g, sg_g, w["a_A_log_g"])
    gr["a_A_log"] = dsmall[:, 0, :4].reshape(1, SSM_HEADS)
    gr["a_dt_bias"] = dsmall[:, 1, :4].reshape(1, SSM_HEADS)
    dxbc, gr["a_conv_w"], gr["a_conv_b"] = _conv_silu_bwd(
        zx, 2048, 4096, w["a_conv_w"], w["a_conv_b"], [(dxs, 0), (dB, 2048), (dC, 3072)], name="a_conv_b")
    gr["w_z"] = _mm(h1, dz, ta=True, out_dtype=BF16, name="in_proj_dwz")
    gr["w_x"] = _mm(h1, dxbc, ta=True, out_dtype=BF16, name="in_proj_dwx")
    gr["w_dt"] = _mm(h1, ddtr, ta=True, out_dtype=BF16, name="in_proj_dwdt")
    ts = _pick(S, (1024, 512, 256))
    wblk = lambda q: pl.BlockSpec((512, 2048), lambda i, j, k: (j, q))
    dh1 = _mm(dz, w["w_zx"], tb=True, name="in_proj_dx", dims=(S, D, 2048), tm=ts, tn=512, tk=2048,
              a_spec=pl.BlockSpec((ts, 2048), lambda i, j, k: (i, 0)), b_spec=wblk(0),
              more=[(dxbc, pl.BlockSpec((ts, 2048), lambda i, j, k: (i, 0)), w["w_zx"], wblk(1)),
                    (dxbc, pl.BlockSpec((ts, 2048), lambda i, j, k: (i, 1)), w["w_zx"], wblk(2)),
                    (ddtr, pl.BlockSpec((ts, 128), lambda i, j, k: (i, 0)), w["w_dt"], pl.BlockSpec((512, 128), lambda i, j, k: (j, 0)))])
    dx0, gr["a_norm"] = _rms_bwd(x, [w["a_norm"]], [dh1], dx1, name="a_norm_b")
    tok = comm.advance([dx0], 3, dict(in_proj=_in_proj_grad(gr).reshape(D, N_CHIPS, -1).transpose(1, 0, 2)))
    return loss_part, dx0, gr, tok


def _prep_small(full, w):
    w["a_norm"] = full["a_norm"]
    w["a_conv_w"] = full["a_conv_w"][0]
    w["a_conv_b"] = full["a_conv_b"]
    pad32 = lambda v: jnp.pad(v, ((0, 0), (0, 128 - SSM_HEADS)))
    w["a_dt_bias"] = pad32(full["a_dt_bias"])
    w["a_A_log"] = pad32(full["a_A_log"])
    w["a_A_log_g"] = jnp.pad(full["a_A_log"].reshape(SSM_GROUPS, 1, 4), ((0, 0), (0, 0), (0, 124)))
    w["a_Dexp"] = jnp.repeat(full["a_D"], HEAD, axis=1)
    w["a_gnorm"] = full["a_gnorm"]
    w["f_norm"] = [full["f_norm"][l:l + 1] for l in range(2)]
    w["f_conv_w"] = [full["f_conv_w"][l] for l in range(2)]
    w["f_conv_b"] = [full["f_conv_b"][l:l + 1] for l in range(2)]
    w["kv_norm"] = full["kv_norm"].reshape(1, -1)
    w["b_kv"] = full["b_kv"].reshape(1, -1)
    w["k_norm_w"] = jnp.tile(full["k_norm"].reshape(1, HEAD), (1, ATT_KV))
    w["b_norm"] = full["b_norm"]
    w["b_q"] = full["b_q"]
    w["q_norm_w"] = jnp.tile(full["q_norm"], (1, ATT_KV * ATT_G))
    w["sinks"] = full["sinks"].reshape(-1)
    w["b_o"] = full["b_o"]
    return w


def _split_in_proj(ip):
    return ip[:, :6144].astype(BF16), jnp.pad(ip[:, 6144:], ((0, 0), (0, 128 - SSM_HEADS))).astype(BF16)


def _join_in_proj(blocks, *, name="in_proj_join", tr=256):
    _, R, cw = blocks.shape
    zx_cols = 3 * 2048
    rest = N_CHIPS * cw - zx_cols

    def body(b_ref, zx_ref, dt_ref):
        whole = jnp.concatenate([b_ref[j] for j in range(N_CHIPS)], axis=1)
        zx_ref[...] = whole[:, :zx_cols]
        dt_ref[...] = jnp.concatenate([whole[:, zx_cols:], jnp.zeros((tr, 128 - rest), BF16)], axis=1)

    return pl.pallas_call(
        body, name=name, grid=(R // tr,), in_specs=[pl.BlockSpec((N_CHIPS, tr, cw), lambda i: (0, i, 0))],
        out_specs=[pl.BlockSpec((tr, zx_cols), lambda i: (i, 0)), pl.BlockSpec((tr, 128), lambda i: (i, 0))],
        out_shape=[jax.ShapeDtypeStruct((R, zx_cols), BF16), jax.ShapeDtypeStruct((R, 128), BF16)],
        compiler_params=_cp(1),
    )(blocks)


def _prep_weights(full):
    w = _prep_small(full, {})
    w["w_zx"], w["w_dt"] = _split_in_proj(full["a_in_proj"][0])
    w["a_out_proj"] = full["a_out_proj"][0].astype(BF16)
    w["f_w_in"] = [full["f_w_in"][l].reshape(1024, N_CHIPS, -1).transpose(1, 0, 2).astype(BF16) for l in range(2)]
    w["f_w_down"] = [full["f_w_down"][l].astype(BF16) for l in range(2)]
    w["w_kv"] = full["w_kv"].astype(BF16)
    w["w_q"] = full["w_q"][0].astype(BF16)
    w["w_o"] = full["w_o"][0].astype(BF16)
    return w


def _small_grads(gr):
    g = {}
    g["a_norm"] = gr["a_norm"]
    g["a_conv_w"] = gr["a_conv_w"][None]
    g["a_conv_b"] = gr["a_conv_b"]
    g["a_dt_bias"], g["a_A_log"], g["a_D"] = gr["a_dt_bias"], gr["a_A_log"], gr["a_D"]
    g["a_gnorm"] = gr["a_gnorm"]
    g["kv_norm"] = gr["kv_norm"].reshape(-1)
    g["b_kv"] = gr["b_kv"].reshape(-1)
    g["k_norm"] = gr["k_norm"].reshape(-1)
    g["b_norm"] = gr["b_norm"]
    g["b_q"] = gr["b_q"]
    g["q_norm"] = gr["q_norm"]
    g["sinks"] = gr["sinks"]
    g["b_o"] = gr["b_o"]
    f = [gr["ffn0"], gr["ffn1"]]
    g["f_norm"] = jnp.concatenate([f[0]["f_norm"], f[1]["f_norm"]], axis=0)
    g["f_conv_w"] = jnp.stack([f[l]["f_conv_w"] for l in range(2)])
    g["f_conv_b"] = jnp.concatenate([f[l]["f_conv_b"] for l in range(2)], axis=0)
    return g


def _in_proj_grad(gr):
    return jnp.concatenate([gr["w_z"], gr["w_x"], gr["w_dt"][:, :SSM_HEADS]], axis=1)


def _full_grads(gr):
    g = _small_grads(gr)
    f32 = lambda t: t.astype(F32)
    g["a_in_proj"] = f32(_in_proj_grad(gr))[None]
    g["a_out_proj"] = f32(gr["a_out_proj"])[None]
    g["w_kv"] = f32(gr["w_kv"])
    g["w_q"] = f32(gr["w_q"])[None]
    g["w_o"] = f32(gr["w_o"])[None]
    f = [gr["ffn0"], gr["ffn1"]]
    g["f_w_in"] = jnp.stack([f32(f[l]["f_w_in"]).transpose(1, 0, 2).reshape(1024, -1) for l in range(2)])
    g["f_w_down"] = jnp.stack([f32(f[l]["f_w_down"]) for l in range(2)])
    return g


MESH = pl.DeviceIdType.MESH
WEIGHTS = ("a_norm", "a_in_proj", "a_conv_w", "a_conv_b", "a_dt_bias", "a_A_log", "a_D", "a_gnorm", "a_out_proj",
           "kv_norm", "w_kv", "b_kv", "k_norm", "b_norm", "w_q", "b_q", "q_norm", "sinks", "w_o", "b_o", "f_norm",
           "f_w_in", "f_conv_w", "f_conv_b", "f_w_down")
MATS = (("in_proj", "a_in_proj", 0), ("out_proj", "a_out_proj", 0), ("w_kv", "w_kv", None), ("w_q", "w_q", 0),
        ("w_o", "w_o", 0), ("f_in0", "f_w_in", 0), ("f_in1", "f_w_in", 1), ("f_down0", "f_w_down", 0),
        ("f_down1", "f_w_down", 1))
SMALL_CUT = (("a_norm", 1), ("a_conv_w", 2), ("a_conv_b", 1), ("a_gnorm", 1), ("f_conv_w", 2))
SMALL_REP = ("a_dt_bias", "a_A_log", "a_D", "kv_norm", "b_kv", "k_norm", "b_norm", "b_q", "q_norm", "sinks", "b_o",
             "f_norm", "f_conv_b")


def _coords():
    return lax.axis_index("x"), lax.axis_index("y"), lax.axis_index("c")


def _other_chips(x, y):
    return [(1 - x, y), (x, 1 - y), (1 - x, 1 - y)]


def _pack(arrs, rows_align, lanes, dtype):
    flat = jnp.concatenate([a.reshape(-1).astype(dtype) for a in arrs])
    per = rows_align * lanes
    total = -(-flat.shape[0] // per) * per
    return jnp.pad(flat, (0, total - flat.shape[0])).reshape(total // lanes, lanes)


def _unpack(flat, shapes):
    out, off = [], 0
    for s in shapes:
        n = math.prod(s)
        out.append(flat[off:off + n].reshape(s))
        off += n
    return out


def _remote(src, dst, send, recv, k, dev):
    return pltpu.make_async_remote_copy(src_ref=src, dst_ref=dst, send_sem=send.at[k], recv_sem=recv.at[k],
                                        device_id=dev, device_id_type=MESH)


_ANY = pl.BlockSpec(memory_space=pl.ANY)


def _halves(t):
    r, c = t.shape
    return t.reshape(2, r // 2, c)


def _gather_weights(shards, sp):
    n = len(shards)
    per = 9
    n_sem = per * n + 3

    def body(*refs):
        sh, sp_ref = refs[:n], refs[n]
        outs, sout = refs[n + 1:2 * n + 1], refs[2 * n + 1]
        send, recv, loc = refs[2 * n + 2:]
        x, y, c = _coords()
        me = 2 * x + y
        cx_, cy_, cd_ = _other_chips(x, y)
        ix, iy, idg = (2 * p[0] + p[1] for p in (cx_, cy_, cd_))
        to_x, to_y, sib = (*cx_, c), (*cy_, c), (x, y, 1 - c)
        l1 = pltpu.make_async_copy(sp_ref, sout.at[me], loc.at[0])
        l1.start()
        sends = [_remote(sp_ref, sout.at[me], send, recv, per * n + j, (*p, c)) for j, p in enumerate((cx_, cy_, cd_))]
        for t in range(n):
            sends.append(_remote(sh[t].at[c], outs[t].at[me, c], send, recv, per * t + 0, to_x))
            sends.append(_remote(sh[t].at[c], outs[t].at[me, c], send, recv, per * t + 1, to_y))
            sends.append(_remote(sh[t], outs[t].at[me], send, recv, per * t + 8, sib))
        for cp in sends:
            cp.start()

        def go(src, dst, k, dev):
            cp = _remote(src, dst, send, recv, k, dev)
            cp.start()
            sends.append(cp)

        def piece(t, owner, first):
            q = sh[t].shape[1] // 2
            return outs[t].at[owner, c, pl.ds(0 if first else q, q)]

        for t in range(n):
            _remote(sh[t].at[c], outs[t].at[ix, c], send, recv, per * t + 0, to_x).wait_recv()
            go(piece(t, ix, False), piece(t, ix, False), per * t + 3, to_y)
            go(outs[t].at[ix, c], outs[t].at[ix, c], per * t + 4, sib)
        for t in range(n):
            _remote(sh[t].at[c], outs[t].at[iy, c], send, recv, per * t + 1, to_y).wait_recv()
            go(piece(t, iy, True), piece(t, iy, True), per * t + 2, to_x)
            go(outs[t].at[iy, c], outs[t].at[iy, c], per * t + 5, sib)
        for t in range(n):
            _remote(piece(t, idg, True), piece(t, idg, True), send, recv, per * t + 2, to_x).wait_recv()
            go(piece(t, idg, True), piece(t, idg, True), per * t + 6, sib)
            _remote(piece(t, idg, False), piece(t, idg, False), send, recv, per * t + 3, to_y).wait_recv()
            go(piece(t, idg, False), piece(t, idg, False), per * t + 7, sib)
        for j, p in enumerate((cx_, cy_, cd_)):
            _remote(sp_ref, sout.at[2 * p[0] + p[1]], send, recv, per * n + j, (*p, c)).wait_recv()
        for t in range(n):
            q = sh[t].shape[1] // 2
            other = lambda owner, lo=None: outs[t].at[owner, 1 - c] if lo is None else outs[t].at[owner, 1 - c, pl.ds(lo, q)]
            _remote(other(ix), other(ix), send, recv, per * t + 4, sib).wait_recv()
            _remote(other(iy), other(iy), send, recv, per * t + 5, sib).wait_recv()
            _remote(other(idg, 0), other(idg, 0), send, recv, per * t + 6, sib).wait_recv()
            _remote(other(idg, q), other(idg, q), send, recv, per * t + 7, sib).wait_recv()
            _remote(sh[t], outs[t].at[me], send, recv, per * t + 8, sib).wait_recv()
        for cp in sends:
            cp.wait_send()
        l1.wait()

    res = pl.pallas_call(
        body, name="gather_weights", in_specs=[_ANY] * (n + 1), out_specs=[_ANY] * (n + 1),
        out_shape=[jax.ShapeDtypeStruct((N_CHIPS,) + t.shape, t.dtype) for t in shards]
        + [jax.ShapeDtypeStruct((N_CHIPS,) + sp.shape, sp.dtype)],
        scratch_shapes=[pltpu.SemaphoreType.DMA((n_sem,)), pltpu.SemaphoreType.DMA((n_sem,)),
                        pltpu.SemaphoreType.DMA((1,))],
    )(*shards, sp)
    return res[:n], res[n]


_HBM = pl.BlockSpec(memory_space=pltpu.HBM)
_SEMS = pl.BlockSpec(memory_space=pltpu.SEMAPHORE)
_DATAFLOW = pltpu.SideEffectType.DATAFLOW_SIDE_EFFECTING


def _in_hbm(a):
    return pltpu.with_memory_space_constraint(a, pltpu.HBM)


def _start_copies(copies, arrays, n_sem, after, *, name):
    n, na = len(arrays), len(after)

    def body(*refs):
        for mine, _ in copies(refs[:n], refs[n + na], refs[n + na + 1]):
            mine.start()
        refs[-1][...] = jnp.zeros_like(refs[-1])

    res = pl.pallas_call(
        body, name=name, in_specs=[_HBM] * n + [_ANY] * na,
        out_specs=[_SEMS, _SEMS] + [_HBM] * n + [pl.BlockSpec(memory_space=pltpu.VMEM)],
        out_shape=[pltpu.SemaphoreType.DMA((n_sem,)), pltpu.SemaphoreType.DMA((n_sem,))]
        + [pltpu.HBM(a.shape, a.dtype) for a in arrays] + [jax.ShapeDtypeStruct((8, 128), F32)],
        input_output_aliases={t: 2 + t for t in range(n)},
        compiler_params=pltpu.CompilerParams(has_side_effects=_DATAFLOW),
    )(*[_in_hbm(a) for a in arrays], *after)
    return res[0], res[1], list(res[2:2 + n]), res[-1]


def _wait_copies(copies, send, recv, arrays, after, *, name):
    n = len(arrays)

    def body(*refs):
        for mine, theirs in copies(refs[:n], refs[n], refs[n + 1]):
            mine.wait_send()
            theirs.wait_recv()

    return list(pl.pallas_call(
        body, name=name, in_specs=[_HBM] * n + [_SEMS, _SEMS] + [_ANY] * len(after), out_specs=[_HBM] * n,
        out_shape=[pltpu.HBM(a.shape, a.dtype) for a in arrays], input_output_aliases={t: t for t in range(n)},
        compiler_params=pltpu.CompilerParams(has_side_effects=_DATAFLOW),
    )(*arrays, send, recv, *after))


def _sibling_copies(refs, send, recv):
    n = len(refs) // 2
    x, y, c = _coords()
    cps = [_remote(refs[t].at[:, 1 - c], refs[n + t], send, recv, t, (x, y, 1 - c)) for t in range(n)]
    return [(cp, cp) for cp in cps]


def _join_copies(refs, send, recv):
    x, y, c = _coords()
    sib = (x, y, 1 - c)
    return [(_remote(o.at[c], o.at[c], send, recv, t, sib), _remote(o.at[1 - c], o.at[1 - c], send, recv, t, sib))
            for t, o in enumerate(refs)]


def _gather_copies(sh, land, send, recv):
    x, y, c = _coords()
    me = 2 * x + y
    out = []
    for t in range(len(sh)):
        for j, (cx, cy) in enumerate(_other_chips(x, y)):
            dev = (cx, cy, c)
            out.append((_remote(sh[t].at[c], land[t].at[me, c], send, recv, 4 * t + j, dev),
                        _remote(sh[t].at[c], land[t].at[2 * cx + cy, c], send, recv, 4 * t + j, dev)))
        sib = (x, y, 1 - c)
        out.append((_remote(sh[t], land[t].at[me], send, recv, 4 * t + 3, sib),
                    _remote(sh[t], land[t].at[me], send, recv, 4 * t + 3, sib)))
    return out


def _gather_start(shards, after, *, name):
    n = len(shards)

    def body(*refs):
        sh, land = refs[:n], refs[n:2 * n]
        send, recv = refs[2 * n + 1], refs[2 * n + 2]
        token = refs[-1]
        for mine, _ in _gather_copies(sh, land, send, recv):
            mine.start()
        token[...] = jnp.zeros_like(token)

    lands = [_in_hbm(lax.empty((N_CHIPS,) + s.shape, s.dtype)) for s in shards]
    res = pl.pallas_call(
        body, name=name, in_specs=[_HBM] * (2 * n) + [_ANY],
        out_specs=[_SEMS, _SEMS] + [_HBM] * (2 * n) + [pl.BlockSpec(memory_space=pltpu.VMEM)],
        out_shape=[pltpu.SemaphoreType.DMA((4 * n,)), pltpu.SemaphoreType.DMA((4 * n,))]
        + [pltpu.HBM(s.shape, s.dtype) for s in shards] + [pltpu.HBM(l.shape, l.dtype) for l in lands]
        + [jax.ShapeDtypeStruct((8, 128), F32)],
        input_output_aliases={t: 2 + t for t in range(2 * n)},
        compiler_params=pltpu.CompilerParams(has_side_effects=_DATAFLOW),
    )(*[_in_hbm(s) for s in shards], *lands, after)
    return res[0], res[1], res[2:2 + n], res[2 + n:2 + 2 * n], res[-1]


def _gather_wait(send, recv, shards, lands, after, *, name):
    n = len(shards)

    def body(*refs):
        sh, land = refs[:n], refs[n:2 * n]
        send_r, recv_r = refs[2 * n], refs[2 * n + 1]
        for mine, theirs in _gather_copies(sh, land, send_r, recv_r):
            mine.wait_send()
            theirs.wait_recv()

    res = pl.pallas_call(
        body, name=name, in_specs=[_HBM] * (2 * n) + [_SEMS, _SEMS, _ANY], out_specs=[_HBM] * (2 * n),
        out_shape=[pltpu.HBM(s.shape, s.dtype) for s in shards] + [pltpu.HBM(l.shape, l.dtype) for l in lands],
        input_output_aliases={t: t for t in range(2 * n)},
        compiler_params=pltpu.CompilerParams(has_side_effects=_DATAFLOW),
    )(*shards, *lands, send, recv, after)
    return res[n:]


def _forward_copies(refs, send, recv):
    x, y, c = _coords()
    sib = (x, y, 1 - c)
    srcs = [2 * cx + cy for cx, cy in _other_chips(x, y)]
    return [(_remote(o.at[s, c], o.at[s, c], send, recv, 3 * t + j, sib),
             _remote(o.at[s, 1 - c], o.at[s, 1 - c], send, recv, 3 * t + j, sib))
            for t, o in enumerate(refs) for j, s in enumerate(srcs)]


def _small_copies(v, land, send, recv):
    x, y, c = _coords()
    me = 4 * x + 2 * y + c
    out = []
    for k in range(1, 8):
        px = 1 - x if k & 4 else x
        py = 1 - y if k & 2 else y
        pc = 1 - c if k & 1 else c
        out.append((_remote(v, land.at[me], send, recv, k - 1, (px, py, pc)),
                    _remote(v, land.at[4 * px + 2 * py + pc], send, recv, k - 1, (px, py, pc))))
    return out


def _small_start(v, after, *, name):
    def body(v_ref, land_ref, after_ref, send, recv, v_thru, land_thru, token):
        for mine, _ in _small_copies(v_ref, land_ref, send, recv):
            mine.start()
        token[...] = jnp.zeros_like(token)

    land = _in_hbm(lax.empty((8,) + v.shape, v.dtype))
    return pl.pallas_call(
        body, name=name, in_specs=[_HBM, _HBM, _ANY],
        out_specs=[_SEMS, _SEMS, _HBM, _HBM, pl.BlockSpec(memory_space=pltpu.VMEM)],
        out_shape=[pltpu.SemaphoreType.DMA((7,)), pltpu.SemaphoreType.DMA((7,)), pltpu.HBM(v.shape, v.dtype),
                   pltpu.HBM(land.shape, land.dtype), jax.ShapeDtypeStruct((8, 128), F32)],
        input_output_aliases={0: 2, 1: 3}, compiler_params=pltpu.CompilerParams(has_side_effects=_DATAFLOW),
    )(_in_hbm(v), land, after)


def _small_wait(send, recv, v, land, after, *, name):
    def body(v_ref, land_ref, send_r, recv_r, *rest):
        for mine, theirs in _small_copies(v_ref, land_ref, send_r, recv_r):
            mine.wait_send()
            theirs.wait_recv()

    return pl.pallas_call(
        body, name=name, in_specs=[_HBM, _HBM, _SEMS, _SEMS] + [_ANY] * len(after), out_specs=[_HBM, _HBM],
        out_shape=[pltpu.HBM(v.shape, v.dtype), pltpu.HBM(land.shape, land.dtype)],
        input_output_aliases={0: 0, 1: 1}, compiler_params=pltpu.CompilerParams(has_side_effects=_DATAFLOW),
    )(v, land, send, recv, *after)


def _small_sum(v, land, me_idx, *, name="small_sum"):
    def body(me_ref, v_ref, land_ref, o_ref):
        acc = None
        for s in range(8):
            term = jnp.where(me_ref[0] == s, v_ref[...], land_ref[s])
            acc = term if acc is None else acc + term
        o_ref[...] = acc

    whole = lambda shape: pl.BlockSpec(shape, lambda i, me_ref: (0,) * len(shape))
    return pl.pallas_call(
        body, name=name,
        grid_spec=pltpu.PrefetchScalarGridSpec(num_scalar_prefetch=1, grid=(1,), in_specs=[whole(v.shape), whole(land.shape)],
                                               out_specs=whole(v.shape)),
        out_shape=jax.ShapeDtypeStruct(v.shape, F32), compiler_params=_cp(1),
    )(me_idx, v, land)


RS_ROW_SPLIT = 2


def _rs_add_pair(gs, as_, c_idx, *, name):
    n = len(gs)

    def body(c_ref, *refs):
        for t in range(n):
            refs[2 * n + t][...] = (refs[t][...].astype(F32) + refs[n + t][...].astype(F32)).astype(BF16)

    def gspec(g):
        _, _, rh, cols = g.shape
        return pl.BlockSpec((None, None, rh // RS_ROW_SPLIT, cols), lambda j, i, c_ref: (j, c_ref[0], i, 0))

    def pspec(g):
        _, _, rh, cols = g.shape
        return pl.BlockSpec((None, rh // RS_ROW_SPLIT, cols), lambda j, i, c_ref: (j, i, 0))

    return pl.pallas_call(
        body, name=name,
        grid_spec=pltpu.PrefetchScalarGridSpec(
            num_scalar_prefetch=1, grid=(N_CHIPS, RS_ROW_SPLIT),
            in_specs=[gspec(g) for g in gs] + [pspec(g) for g in gs], out_specs=[pspec(g) for g in gs]),
        out_shape=[jax.ShapeDtypeStruct((N_CHIPS,) + g.shape[2:], BF16) for g in gs], compiler_params=_cp(2),
    )(c_idx, *gs, *as_)


def _chips_copies(p, r, send, recv):
    x, y, c = _coords()
    return [_remote(p[t].at[2 * cx + cy], r[t].at[k], send, recv, 3 * t + k, (cx, cy, c))
            for k, (cx, cy) in enumerate(_other_chips(x, y)) for t in range(len(p))]


def _rs_chips_start(ps, after, *, name):
    n, na = len(ps), len(after)

    def body(*refs):
        p, r = refs[:n], refs[n:2 * n]
        send, recv = refs[2 * n + na], refs[2 * n + na + 1]
        token = refs[-1]
        for cp in _chips_copies(p, r, send, recv):
            cp.start()
        token[...] = jnp.zeros_like(token)

    lands = [_in_hbm(lax.empty((3,) + p.shape[1:], p.dtype)) for p in ps]
    res = pl.pallas_call(
        body, name=name, in_specs=[_HBM] * (2 * n) + [_ANY] * na,
        out_specs=[_SEMS, _SEMS] + [_HBM] * (2 * n) + [pl.BlockSpec(memory_space=pltpu.VMEM)],
        out_shape=[pltpu.SemaphoreType.DMA((3 * n,)), pltpu.SemaphoreType.DMA((3 * n,))]
        + [pltpu.HBM(p.shape, p.dtype) for p in ps] + [pltpu.HBM(l.shape, l.dtype) for l in lands]
        + [jax.ShapeDtypeStruct((8, 128), F32)],
        input_output_aliases={t: 2 + t for t in range(2 * n)},
        compiler_params=pltpu.CompilerParams(has_side_effects=_DATAFLOW),
    )(*[_in_hbm(p) for p in ps], *lands, *after)
    return res[0], res[1], res[2:2 + n], res[2 + n:2 + 2 * n], res[-1]


def _rs_chips_wait(send, recv, ps, lands, after, *, name):
    n = len(ps)

    def body(*refs):
        p, r = refs[:n], refs[n:2 * n]
        for cp in _chips_copies(p, r, refs[2 * n], refs[2 * n + 1]):
            cp.wait_send()
            cp.wait_recv()

    res = pl.pallas_call(
        body, name=name, in_specs=[_HBM] * (2 * n) + [_SEMS, _SEMS] + [_ANY] * len(after), out_specs=[_HBM] * (2 * n),
        out_shape=[pltpu.HBM(p.shape, p.dtype) for p in ps] + [pltpu.HBM(l.shape, l.dtype) for l in lands],
        input_output_aliases={t: t for t in range(2 * n)},
        compiler_params=pltpu.CompilerParams(has_side_effects=_DATAFLOW),
    )(*ps, *lands, send, recv, *after)
    return res[:n], res[n:]


def _rs_add_chips(ps, rs, idx, *, name):
    n = len(ps)

    def body(idx_ref, *refs):
        for t in range(n):
            p_ref, r0, r1, r2 = refs[4 * t:4 * t + 4]
            refs[4 * n + t][...] = ((p_ref[...].astype(F32) + r0[...].astype(F32)) + r1[...].astype(F32)) + r2[...].astype(F32)

    in_specs, args = [], []
    for p, r in zip(ps, rs):
        _, rh, cols = p.shape
        blk = (None, rh // RS_ROW_SPLIT, cols)
        in_specs.append(pl.BlockSpec(blk, lambda i, idx_ref: (idx_ref[0], i, 0)))
        in_specs += [pl.BlockSpec(blk, lambda i, idx_ref, k=k: (k, i, 0)) for k in range(3)]
        args += [p, r, r, r]
    out_specs = [pl.BlockSpec((None, p.shape[1] // RS_ROW_SPLIT, p.shape[2]), lambda i, idx_ref: (idx_ref[1], i, 0))
                 for p in ps]
    return pl.pallas_call(
        body, name=name,
        grid_spec=pltpu.PrefetchScalarGridSpec(num_scalar_prefetch=1, grid=(RS_ROW_SPLIT,), in_specs=in_specs,
                                               out_specs=out_specs),
        out_shape=[jax.ShapeDtypeStruct((2,) + p.shape[1:], F32) for p in ps], compiler_params=_cp(1),
    )(idx, *args)


def _adamw(w, gs, m, v, *, name, dep=None):
    L, Rr, C = w.shape
    tr, tc = _pick(Rr, (256, 128, 64)), C
    if tr == Rr and Rr * C > 512 * 1024:
        tc = 256
    bc1 = 1.0 - ADAM_B1 ** ADAM_STEP
    bc2 = 1.0 - ADAM_B2 ** ADAM_STEP
    nd = 0 if dep is None else 1

    def body(*refs):
        w_ref, m_ref, v_ref = refs[0], refs[1], refs[2]
        g_refs = refs[3:3 + L]
        d_ref, mo_ref, vo_ref, go_ref = refs[3 + L + nd:]
        layer = pl.program_id(0)
        gv = g_refs[0][...]
        for q in range(1, L):
            gv = jnp.where(layer == q, g_refs[q][...], gv)
        mn = ADAM_B1 * m_ref[...] + (1.0 - ADAM_B1) * gv
        vn = ADAM_B2 * v_ref[...] + (1.0 - ADAM_B2) * (gv * gv)
        go_ref[...] = gv
        mo_ref[...] = mn
        vo_ref[...] = vn
        d_ref[...] = -ADAM_LR * ((mn / bc1) / (jnp.sqrt(vn / bc2) + ADAM_EPS) + ADAM_WD * w_ref[...])

    blk = pl.BlockSpec((None, tr, tc), lambda l, i, j: (l, i, j))
    gblks = [pl.BlockSpec((tr, tc), lambda l, i, j, q=q: (jnp.where(l == q, i, 0), jnp.where(l == q, j, 0))) for q in range(L)]
    return pl.pallas_call(
        body, name=name, grid=(L, Rr // tr, C // tc), in_specs=[blk] * 3 + gblks + [_ANY] * nd, out_specs=[blk] * 4,
        out_shape=[jax.ShapeDtypeStruct((L, Rr, C), F32)] * 4, compiler_params=_cp(3),
    )(w, m, v, *gs, *([] if dep is None else [dep]))


def kernel(x, positions, a_norm, a_in_proj, a_conv_w, a_conv_b, a_dt_bias, a_A_log, a_D, a_gnorm, a_out_proj,
           kv_norm, w_kv, b_kv, k_norm, b_norm, w_q, b_q, q_norm, sinks, w_o, b_o, f_norm, f_w_in, f_conv_w,
           f_conv_b, f_w_down, loss_target, m_a_norm, m_a_in_proj, m_a_conv_w, m_a_conv_b, m_a_dt_bias, m_a_A_log,
           m_a_D, m_a_gnorm, m_a_out_proj, m_kv_norm, m_w_kv, m_b_kv, m_k_norm, m_b_norm, m_w_q, m_b_q, m_q_norm,
           m_sinks, m_w_o, m_b_o, m_f_norm, m_f_w_in, m_f_conv_w, m_f_conv_b, m_f_w_down, v_a_norm, v_a_in_proj,
           v_a_conv_w, v_a_conv_b, v_a_dt_bias, v_a_A_log, v_a_D, v_a_gnorm, v_a_out_proj, v_kv_norm, v_w_kv,
           v_b_kv, v_k_norm, v_b_norm, v_w_q, v_b_q, v_q_norm, v_sinks, v_w_o, v_b_o, v_f_norm, v_f_w_in,
           v_f_conv_w, v_f_conv_b, v_f_w_down):
    wl = dict(zip(WEIGHTS, (a_norm, a_in_proj, a_conv_w, a_conv_b, a_dt_bias, a_A_log, a_D, a_gnorm, a_out_proj,
                            kv_norm, w_kv, b_kv, k_norm, b_norm, w_q, b_q, q_norm, sinks, w_o, b_o, f_norm, f_w_in,
                            f_conv_w, f_conv_b, f_w_down)))
    ml = dict(zip(WEIGHTS, (m_a_norm, m_a_in_proj, m_a_conv_w, m_a_conv_b, m_a_dt_bias, m_a_A_log, m_a_D, m_a_gnorm,
                            m_a_out_proj, m_kv_norm, m_w_kv, m_b_kv, m_k_norm, m_b_norm, m_w_q, m_b_q, m_q_norm,
                            m_sinks, m_w_o, m_b_o, m_f_norm, m_f_w_in, m_f_conv_w, m_f_conv_b, m_f_w_down)))
    vl = dict(zip(WEIGHTS, (v_a_norm, v_a_in_proj, v_a_conv_w, v_a_conv_b, v_a_dt_bias, v_a_A_log, v_a_D, v_a_gnorm,
                            v_a_out_proj, v_kv_norm, v_w_kv, v_b_kv, v_k_norm, v_b_norm, v_w_q, v_b_q, v_q_norm,
                            v_sinks, v_w_o, v_b_o, v_f_norm, v_f_w_in, v_f_conv_w, v_f_conv_b, v_f_w_down)))
    xi, yi, ci = _coords()
    me = 2 * xi + yi
    S = x.shape[1]

    def block_of(n, layer):
        t = wl[n]
        return t if layer is None else t[layer]

    rows = lambda t: t.reshape(-1, t.shape[-1])
    c_idx = jnp.reshape(ci, (1,)).astype(jnp.int32)
    me_c = jnp.stack([me, ci]).astype(jnp.int32)
    early = ("in_proj",)
    late = (("out_proj", "f_in0", "f_down0"), ("w_kv", "w_q", "w_o", "f_in1", "f_down1"))
    shards = {name: _halves(block_of(wn, layer).astype(BF16)) for name, wn, layer in MATS}

    sp = _pack([wl[n] for n, _ in SMALL_CUT], 8, 128, F32)
    gathered, gs = _gather_weights([shards[k] for k in early], sp)
    gt = {k: t.reshape(N_CHIPS, -1, t.shape[-1]) for k, t in zip(early, gathered)}
    started = {0: _gather_start([shards[k] for k in late[0]], gs, name="gather_late_start0")}
    full = {n: wl[n] for n in SMALL_REP}
    gs = gs.reshape(N_CHIPS, -1)
    pieces = [_unpack(gs[j], [wl[n].shape for n, _ in SMALL_CUT]) for j in range(N_CHIPS)]
    for q, (n, ax) in enumerate(SMALL_CUT):
        full[n] = jnp.concatenate([pieces[j][q] for j in range(N_CHIPS)], axis=ax)
    w = _prep_small(full, {})
    w["w_zx"], w["w_dt"] = _join_in_proj(gt["in_proj"])
    w["dep"] = started[0][4]

    class Comm:
        flight = []
        reduced = {}

        forwarding = {}

        def late_start(self, part, after):
            started[part] = _gather_start([shards[k] for k in late[part]], after, name=f"gather_late_start{part}")
            return started[part][4]

        def late_arrived(self, part, after):
            send, recv, shs, lands, _ = started[part]
            lands = _gather_wait(send, recv, shs, lands, after[0], name=f"gather_late_wait{part}")
            send, recv, lands, token = _start_copies(_forward_copies, list(lands), 3 * len(lands), after,
                                                     name=f"gather_late_forward_start{part}")
            self.forwarding[part] = (send, recv, lands)
            return token

        def late_weights(self, w, after, part):
            send, recv, lands = self.forwarding[part]
            lands = _wait_copies(_forward_copies, send, recv, lands, [after], name=f"gather_late_forward_wait{part}")
            lt = {k: t.reshape(N_CHIPS, -1, t.shape[-1]) for k, t in zip(late[part], lands)}
            w = dict(w)
            if part == 0:
                w["a_out_proj"], w["f_w_in"], w["f_w_down"] = rows(lt["out_proj"]), [lt["f_in0"]], [rows(lt["f_down0"])]
            else:
                w["w_kv"], w["w_q"], w["w_o"] = (rows(lt[k]) for k in ("w_kv", "w_q", "w_o"))
                w["f_w_in"], w["f_w_down"] = w["f_w_in"] + [lt["f_in1"]], w["f_w_down"] + [rows(lt["f_down1"])]
            return w

        def advance(self, after, group=None, tensors=None):
            token = None
            for grp in list(self.flight):
                tag, n = grp["tag"], len(grp["names"])
                dep = list(after) + ([] if token is None else [token])
                if grp["stage"] == "sibling":
                    arrs = _wait_copies(_sibling_copies, grp["send"], grp["recv"], grp["arrays"], dep, name=f"rs_sibling_wait{tag}")
                    pairs = _rs_add_pair(arrs[:n], arrs[n:], c_idx, name=f"rs_add_pair{tag}")
                    send, recv, ps, lands, token = _rs_chips_start(pairs, dep, name=f"rs_chips_start{tag}")
                    grp.update(stage="chips", send=send, recv=recv, ps=ps, lands=lands)
                elif grp["stage"] == "chips":
                    ps, rs = _rs_chips_wait(grp["send"], grp["recv"], grp["ps"], grp["lands"], dep, name=f"rs_chips_wait{tag}")
                    halves = _rs_add_chips(ps, rs, me_c, name=f"rs_add_chips{tag}")
                    send, recv, arrs, token = _start_copies(_join_copies, halves, n, dep, name=f"rs_join_start{tag}")
                    grp.update(stage="join", send=send, recv=recv, arrays=arrs)
                else:
                    joined = _wait_copies(_join_copies, grp["send"], grp["recv"], grp["arrays"], dep, name=f"rs_join_wait{tag}")
                    self.reduced.update({k: rows(t) for k, t in zip(grp["names"], joined)})
                    self.flight.remove(grp)
            if group is not None:
                names = list(tensors)
                glist = [tensors[k].reshape(N_CHIPS, 2, -1, tensors[k].shape[-1]) for k in names]
                lands = [lax.empty((N_CHIPS,) + gq.shape[2:], gq.dtype) for gq in glist]
                dep = list(after) + ([] if token is None else [token])
                send, recv, arrs, token = _start_copies(_sibling_copies, glist + lands, len(names), dep,
                                                        name=f"rs_sibling_start{group}")
                self.flight.append(dict(tag=group, names=names, stage="sibling", send=send, recv=recv, arrays=arrs))
            return token

    comm = Comm()

    posf = positions.reshape(S, 1).astype(F32)
    loss_part, dx0, gr, tok = _local_step(x[0], posf, loss_target[0], w, comm)
    g = _small_grads(gr)

    small_names = [n for n, _ in SMALL_CUT] + list(SMALL_REP)
    sv = _pack([g[n] for n in small_names] + [loss_part[0:1, 0:1]], 8, 128, F32)
    s_send, s_recv, sv, s_land, s_token = _small_start(sv, tok, name="small_start")

    grads, delta, new_m, new_v = {}, {}, {}, {}

    def update(wn, dep):
        gl = [comm.reduced[name] for name, n2, _ in MATS if n2 == wn]
        shp = wl[wn].shape
        three = (len(gl),) + gl[0].shape
        flip = shp[-1] % 128 != 0
        view = (lambda t: t.reshape(three).transpose(0, 2, 1)) if flip else (lambda t: t.reshape(three))
        back = (lambda t: t.transpose(0, 2, 1).reshape(shp)) if flip else (lambda t: t.reshape(shp))
        if flip:
            gl = [t.T for t in gl]
        d, mn, vn, go = _adamw(view(wl[wn]), gl, view(ml[wn]), view(vl[wn]), name="adamw_" + wn, dep=dep)
        grads[wn], delta[wn], new_m[wn], new_v[wn] = back(go), back(d), back(mn), back(vn)
        return d

    first = [update(wn, s_token) for wn in ("w_q", "w_o", "w_kv")]
    tok = comm.advance(first)
    second = [update(wn, tok) for wn in ("f_w_in", "f_w_down", "a_out_proj")]
    comm.advance(second)
    comm.advance(second)
    update("a_in_proj", None)
    done = first + second

    sv, s_land = _small_wait(s_send, s_recv, sv, s_land, done, name="small_wait")
    sred = _small_sum(sv, s_land, jnp.reshape(2 * me + ci, (1,)).astype(jnp.int32)).reshape(-1)
    small_shapes = [g[n].shape for n in small_names] + [(1,)]
    sg = dict(zip(small_names + ["loss"], _unpack(sred, small_shapes)))
    loss = sg["loss"].reshape(())
    g_small = {}
    for n, ax in SMALL_CUT:
        size = wl[n].shape[ax]
        g_small[n] = lax.dynamic_slice_in_dim(sg[n], me * size, size, axis=ax)
    for n in SMALL_REP:
        g_small[n] = sg[n].reshape(wl[n].shape)

    pk = lambda d: _pack([d[n] for n in small_names], 8, 128, F32)[None]
    d, mn, vn, _ = _adamw(pk(wl), [pk(g_small)[0]], pk(ml), pk(vl), name="adamw_small")
    shapes = [wl[n].shape for n in small_names]
    for n, dd, mm, vv in zip(small_names, _unpack(d.reshape(-1), shapes), _unpack(mn.reshape(-1), shapes),
                             _unpack(vn.reshape(-1), shapes)):
        grads[n], delta[n], new_m[n], new_v[n] = g_small[n], dd, mm, vv

    return (loss, dx0[None], *[grads[n] for n in WEIGHTS], *[delta[n] for n in WEIGHTS],
            *[new_m[n] for n in WEIGHTS], *[new_v[n] for n in WEIGHTS])
```

```python
import math

import jax
import jax.numpy as jnp
from jax import lax
from jax.experimental import pallas as pl
from jax.experimental.pallas import tpu as pltpu

F32 = jnp.float32
BF16 = jnp.bfloat16

EPS = 1e-5
CHUNK = 256
WINDOW = 128
HEAD = 64
SSM_HEADS = 32
SSM_GROUPS = 8
SSM_STATE = 128
ATT_KV = 4
ATT_G = 4
ROPE_THETA = 10000.0
NEG = -1e30
N_CHIPS = 4
VMEM_LIMIT = 56 * 1024 * 1024

ADAM_LR, ADAM_B1, ADAM_B2, ADAM_EPS, ADAM_WD, ADAM_STEP = 0.001, 0.9, 0.999, 1e-08, 0.01, 10


def _cp(n_axes):
    return pltpu.CompilerParams(dimension_semantics=("arbitrary",) * n_axes, vmem_limit_bytes=VMEM_LIMIT)


def _pick(dim, prefs):
    for p in prefs:
        if dim % p == 0:
            return p
    return dim


def _iota(shape, dim):
    return lax.broadcasted_iota(jnp.int32, shape, dim)


def _dot(a, b, ca=1, cb=0):
    return lax.dot_general(a, b, (((ca,), (cb,)), ((), ())), preferred_element_type=F32)


def _dot3(x, ind):
    h = x.astype(BF16)
    r = x - h.astype(F32)
    m = r.astype(BF16)
    lo = (r - m.astype(F32)).astype(BF16)
    return _dot(h, ind) + _dot(m, ind) + _dot(lo, ind)


def _sigmoid(x):
    return jax.nn.sigmoid(x)


def _mm(a, b, *, name, ta=False, tb=False, bias=None, res=None, out_dtype=F32, b_koff=0, tm=None, tn=None, tk=None,
        dims=None, a_spec=None, b_spec=None, o_spec=None, o_shape=None, dep=None, more=(), target=None):
    if dims is not None:
        M, N, K = dims
    else:
        if ta:
            K, M = a.shape
        else:
            M, K = a.shape
        N = b.shape[0] if tb else b.shape[1]
    tm = tm or _pick(M, (1024, 1408, 512, 256, 128))
    tn = tn or _pick(N, (512, 1408, 256, 128))
    tk = tk or (K if K <= 2048 else _pick(K, (2048, 1408, 1024, 512)))
    assert M % tm == 0 and N % tn == 0 and K % tk == 0 and b_koff % tk == 0
    nk = K // tk
    kb0 = b_koff // tk
    has_bias, has_res = bias is not None, res is not None

    def body(*refs):
        a_ref, b_ref = refs[0], refs[1]
        pos = 2
        bias_ref = res_ref = acc_ref = None
        if has_bias:
            bias_ref = refs[pos]
            pos += 1
        if has_res:
            res_ref = refs[pos]
            pos += 1
        if dep is not None:
            pos += 1
        extra = refs[pos:pos + 2 * len(more)]
        pos += 2 * len(more)
        tgt_ref = lp_ref = None
        if target is not None:
            tgt_ref, o_ref, lp_ref = refs[pos], refs[pos + 1], refs[pos + 2]
            pos += 2
        else:
            o_ref = refs[pos]
        if nk > 1:
            acc_ref = refs[pos + 1]
        part = _dot(a_ref[...].astype(BF16), b_ref[...].astype(BF16), 0 if ta else 1, 1 if tb else 0)
        for q in range(len(more)):
            part = part + _dot(extra[2 * q][...].astype(BF16), extra[2 * q + 1][...].astype(BF16),
                               0 if ta else 1, 1 if tb else 0)

        def finish(acc):
            if has_bias:
                acc = acc + bias_ref[...]
            if has_res:
                acc = acc + res_ref[...]
            if target is not None:
                err = acc - tgt_ref[...]
                acc = err * (1.0 / N)
                part_loss = jnp.sum(jnp.sum(err * err, axis=1, keepdims=True), axis=0, keepdims=True) * (0.5 / N)
                first = (pl.program_id(0) == 0) & (pl.program_id(1) == 0)

                @pl.when(first)
                def _():
                    lp_ref[...] = jnp.broadcast_to(part_loss, lp_ref.shape)

                @pl.when(jnp.logical_not(first))
                def _():
                    lp_ref[...] += jnp.broadcast_to(part_loss, lp_ref.shape)

            o_ref[...] = acc.astype(out_dtype)

        if nk == 1:
            finish(part)
        else:
            k = pl.program_id(2)

            @pl.when(k == 0)
            def _():
                acc_ref[...] = part

            @pl.when(k > 0)
            def _():
                acc_ref[...] += part

            @pl.when(k == nk - 1)
            def _():
                finish(acc_ref[...])

    if a_spec is None:
        a_spec = pl.BlockSpec((tk, tm), lambda i, j, k: (k, i)) if ta else pl.BlockSpec((tm, tk), lambda i, j, k: (i, k))
    if b_spec is None:
        b_spec = (pl.BlockSpec((tn, tk), lambda i, j, k: (j, k + kb0)) if tb
                  else pl.BlockSpec((tk, tn), lambda i, j, k: (k + kb0, j)))
    if o_spec is None:
        o_spec = pl.BlockSpec((tm, tn), lambda i, j, k: (i, j))
    in_specs, args = [a_spec, b_spec], [a, b]
    if has_bias:
        in_specs.append(pl.BlockSpec((1, tn), lambda i, j, k: (0, j)))
        args.append(bias)
    if has_res:
        in_specs.append(pl.BlockSpec((tm, tn), lambda i, j, k: (i, j)))
        args.append(res)
    if dep is not None:
        in_specs.append(pl.BlockSpec(memory_space=pl.ANY))
        args.append(dep)
    for piece in more:
        a2, sa, b2, sb = piece if len(piece) == 4 else (a, piece[0], b, piece[1])
        in_specs += [sa, sb]
        args += [a2, b2]
    out_specs, out_shape = o_spec, jax.ShapeDtypeStruct(o_shape or (M, N), out_dtype)
    if target is not None:
        in_specs.append(pl.BlockSpec((tm, tn), lambda i, j, k: (i, j)))
        args.append(target)
        out_specs = [o_spec, pl.BlockSpec((8, 128), lambda i, j, k: (0, 0))]
        out_shape = [out_shape, jax.ShapeDtypeStruct((8, 128), F32)]
    return pl.pallas_call(
        body, name=name, grid=(M // tm, N // tn, nk), in_specs=in_specs, out_specs=out_specs, out_shape=out_shape,
        scratch_shapes=[pltpu.VMEM((tm, tn), F32)] if nk > 1 else [],
        compiler_params=_cp(3),
    )(*args)


def _norm_mm(x, gain, b, *, name, bias=None, N=None, tn=None, b_spec=None, dep=None):
    M, K = x.shape
    N = N or b.shape[1]
    tm = _pick(M, (1024, 512, 256))
    tn = tn or _pick(N, (512, 1408, 256, 128))
    has_bias = bias is not None

    def body(*refs):
        x_ref, g_ref, b_ref = refs[:3]
        pos = 3 + (1 if has_bias else 0) + (0 if dep is None else 1)
        o_ref, h_ref = refs[pos], refs[pos + 1]

        @pl.when(pl.program_id(1) == 0)
        def _():
            xv = x_ref[...]
            h_ref[...] = (xv * lax.rsqrt(jnp.mean(xv * xv, axis=-1, keepdims=True) + EPS) * g_ref[...]).astype(BF16)

        acc = _dot(h_ref[...], b_ref[...].astype(BF16))
        if has_bias:
            acc = acc + refs[3][...]
        o_ref[...] = acc

    in_specs = [pl.BlockSpec((tm, K), lambda i, j: (i, 0)), pl.BlockSpec((1, K), lambda i, j: (0, 0)),
                b_spec or pl.BlockSpec((K, tn), lambda i, j: (0, j))]
    args = [x, gain, b]
    if has_bias:
        in_specs.append(pl.BlockSpec((1, tn), lambda i, j: (0, j)))
        args.append(bias)
    if dep is not None:
        in_specs.append(pl.BlockSpec(memory_space=pl.ANY))
        args.append(dep)
    return pl.pallas_call(
        body, name=name, grid=(M // tm, N // tn), in_specs=in_specs,
        out_specs=[pl.BlockSpec((tm, tn), lambda i, j: (i, j)), pl.BlockSpec((tm, K), lambda i, j: (i, 0))],
        out_shape=[jax.ShapeDtypeStruct((M, N), F32), jax.ShapeDtypeStruct((M, K), BF16)], compiler_params=_cp(2),
    )(*args)


def _rms_bwd(x, gains, dhs, dres, *, name, tr=256, want_colsum=False):
    S, D = x.shape
    n = len(gains)
    steps = S // tr

    def body(*refs):
        x_ref = refs[0]
        g_refs = refs[1:1 + n]
        dh_refs = refs[1 + n:1 + 2 * n]
        dres_ref = refs[1 + 2 * n]
        dx_ref = refs[2 + 2 * n]
        dg_refs = refs[3 + 2 * n:3 + 3 * n]
        cs_ref = refs[3 + 3 * n] if want_colsum else None
        i = pl.program_id(0)
        xv = x_ref[...]
        r = lax.rsqrt(jnp.mean(xv * xv, axis=-1, keepdims=True) + EPS)
        xh = xv * r
        dx = dres_ref[...]
        for q in range(n):
            dh = dh_refs[q][...]
            dxh = dh * g_refs[q][...]
            dx = dx + r * (dxh - xh * jnp.mean(dxh * xh, axis=-1, keepdims=True))
            part = jnp.sum(dh * xh, axis=0, keepdims=True)

            @pl.when(i == 0)
            def _():
                dg_refs[q][...] = part

            @pl.when(i > 0)
            def _():
                dg_refs[q][...] += part

        dx_ref[...] = dx
        if want_colsum:
            cpart = jnp.sum(dx, axis=0, keepdims=True)

            @pl.when(i == 0)
            def _():
                cs_ref[...] = cpart

            @pl.when(i > 0)
            def _():
                cs_ref[...] += cpart

    row = pl.BlockSpec((tr, D), lambda i: (i, 0))
    vec = pl.BlockSpec((1, D), lambda i: (0, 0))
    n_vec_out = n + (1 if want_colsum else 0)
    outs = pl.pallas_call(
        body, name=name, grid=(steps,), in_specs=[row] + [vec] * n + [row] * n + [row],
        out_specs=[row] + [vec] * n_vec_out,
        out_shape=[jax.ShapeDtypeStruct((S, D), F32)] + [jax.ShapeDtypeStruct((1, D), F32)] * n_vec_out,
        compiler_params=_cp(1),
    )(x, *gains, *dhs, dres)
    return outs


def _colsum(x, *, name, tr=256):
    S, D = x.shape

    def body(x_ref, o_ref):
        i = pl.program_id(0)
        part = jnp.sum(x_ref[...].astype(F32), axis=0, keepdims=True)

        @pl.when(i == 0)
        def _():
            o_ref[...] = part

        @pl.when(i > 0)
        def _():
            o_ref[...] += part

    return pl.pallas_call(
        body, name=name, grid=(S // tr,), in_specs=[pl.BlockSpec((tr, D), lambda i: (i, 0))],
        out_specs=pl.BlockSpec((1, D), lambda i: (0, 0)), out_shape=jax.ShapeDtypeStruct((1, D), F32),
        compiler_params=_cp(1),
    )(x)


STRIP = 64
HALO = 8


def _strips(S, tc):
    return [(r0, slice(l0, l0 + 128)) for l0 in range(0, tc, 128) for r0 in range(S - STRIP, -1, -STRIP)]


def _with_halo(ref, r0, ls):
    if r0 == 0:
        return jnp.concatenate([jnp.zeros((HALO, 128), F32), ref[0:STRIP, ls]], axis=0)
    return ref[r0 - HALO:r0 + STRIP, ls]


def _conv_strip(xw, w_ref, b_ref, ls, width):
    acc = b_ref[:, ls] + w_ref[pl.ds(width - 1, 1), ls] * xw[HALO:]
    shifted = []
    for s in range(1, width):
        xs = pltpu.roll(xw, s, axis=0)[HALO:]
        shifted.append(xs)
        acc = acc + w_ref[pl.ds(width - 1 - s, 1), ls] * xs
    return acc, shifted


def _conv_strip_back(dacc, after, xc, shifted, w_ref, ls, width):
    ext = jnp.concatenate([dacc, after], axis=0)
    dx = w_ref[pl.ds(width - 1, 1), ls] * dacc
    dws = [None] * width
    dws[width - 1] = jnp.sum(dacc * xc, axis=0, keepdims=True)
    for s in range(1, width):
        dx = dx + w_ref[pl.ds(width - 1 - s, 1), ls] * pltpu.roll(ext, STRIP + HALO - s, axis=0)[:STRIP]
        dws[width - 1 - s] = jnp.sum(dacc * shifted[s - 1], axis=0, keepdims=True)
    return dx, dws, jnp.sum(dacc, axis=0, keepdims=True)


def _conv_back_block(S, tc, width, w_ref, b_ref, x_ref, dacc_of, dx_store, dw_ref, db_ref):
    for l0 in range(0, tc, 128):
        ls = slice(l0, l0 + 128)
        after = jnp.zeros((HALO, 128), F32)
        tot = None
        for r0 in range(S - STRIP, -1, -STRIP):
            xw = _with_halo(x_ref, r0, ls)
            acc, shifted = _conv_strip(xw, w_ref, b_ref, ls, width)
            dacc = dacc_of(r0, ls, acc, _sigmoid(acc))
            dx, dws, db = _conv_strip_back(dacc, after, xw[HALO:], shifted, w_ref, ls, width)
            dx_store(r0, ls, dx)
            after = dacc[:HALO]
            part = dws + [db]
            tot = part if tot is None else [p + q for p, q in zip(tot, part)]
        for k in range(width):
            dw_ref[pl.ds(k, 1), ls] = tot[k]
        db_ref[:, ls] = tot[width]


def _conv_silu_fwd(xin, col0, C, w, b, *, name, tc=512):
    S = xin.shape[0]
    width = w.shape[0]
    off = col0 // tc

    def body(x_ref, w_ref, b_ref, o_ref):
        for r0, ls in _strips(S, tc):
            acc, _ = _conv_strip(_with_halo(x_ref, r0, ls), w_ref, b_ref, ls, width)
            o_ref[r0:r0 + STRIP, ls] = acc * _sigmoid(acc)

    return pl.pallas_call(
        body, name=name, grid=(C // tc,),
        in_specs=[pl.BlockSpec((S, tc), lambda j: (0, j + off)), pl.BlockSpec((width, tc), lambda j: (0, j)),
                  pl.BlockSpec((1, tc), lambda j: (0, j))],
        out_specs=pl.BlockSpec((S, tc), lambda j: (0, j)), out_shape=jax.ShapeDtypeStruct((S, C), F32),
        compiler_params=_cp(1),
    )(xin, w, b)


def _conv_silu_bwd(xin, col0, C, w, b, douts, *, name, tc=256):
    S = xin.shape[0]
    width = w.shape[0]
    off = col0 // tc
    nd = len(douts)
    ranges = [(o // tc, (o + d.shape[1]) // tc) for d, o in douts]

    def body(*refs):
        x_ref, w_ref, b_ref = refs[0], refs[1], refs[2]
        d_refs = refs[3:3 + nd]
        dx_ref, dw_ref, db_ref = refs[3 + nd], refs[4 + nd], refs[5 + nd]
        j = pl.program_id(0)

        def dacc_of(r0, ls, acc, sg):
            dout = jnp.zeros((STRIP, 128), F32)
            for q in range(nd):
                lo, hi = ranges[q]
                dout = dout + jnp.where((j >= lo) & (j < hi), d_refs[q][r0:r0 + STRIP, ls], 0.0)
            return dout * (sg * (1.0 + acc * (1.0 - sg)))

        def dx_store(r0, ls, dx):
            dx_ref[r0:r0 + STRIP, ls] = dx.astype(BF16)

        _conv_back_block(S, tc, width, w_ref, b_ref, x_ref, dacc_of, dx_store, dw_ref, db_ref)

    d_specs = [pl.BlockSpec((S, tc), (lambda j, lo=lo, hi=hi: (0, jnp.clip(j - lo, 0, hi - lo - 1)))) for lo, hi in ranges]
    return pl.pallas_call(
        body, name=name, grid=(C // tc,),
        in_specs=[pl.BlockSpec((S, tc), lambda j: (0, j + off)), pl.BlockSpec((width, tc), lambda j: (0, j)),
                  pl.BlockSpec((1, tc), lambda j: (0, j))] + d_specs,
        out_specs=[pl.BlockSpec((S, tc), lambda j: (0, j)), pl.BlockSpec((width, tc), lambda j: (0, j)),
                   pl.BlockSpec((1, tc), lambda j: (0, j))],
        out_shape=[jax.ShapeDtypeStruct((S, C), BF16), jax.ShapeDtypeStruct((width, C), F32),
                   jax.ShapeDtypeStruct((1, C), F32)],
        compiler_params=_cp(1),
    )(xin, w, b, *[d for d, _ in douts])


def _ffn_act_fwd(u, w, b, *, name, tc=256):
    S, F2 = u.shape
    Fd = F2 // 2
    width = w.shape[0]
    nb = Fd // tc

    def body(g_ref, v_ref, w_ref, b_ref, o_ref):
        for r0, ls in _strips(S, tc):
            acc, _ = _conv_strip(_with_halo(g_ref, r0, ls), w_ref, b_ref, ls, width)
            o_ref[r0:r0 + STRIP, ls] = (acc * _sigmoid(acc) * v_ref[r0:r0 + STRIP, ls]).astype(BF16)

    return pl.pallas_call(
        body, name=name, grid=(nb,),
        in_specs=[pl.BlockSpec((S, tc), lambda j: (0, j)), pl.BlockSpec((S, tc), lambda j: (0, j + nb)),
                  pl.BlockSpec((width, tc), lambda j: (0, j)), pl.BlockSpec((1, tc), lambda j: (0, j))],
        out_specs=pl.BlockSpec((S, tc), lambda j: (0, j)), out_shape=jax.ShapeDtypeStruct((S, Fd), BF16),
        compiler_params=_cp(1),
    )(u, u, w, b)


def _ffn_act_bwd(u, w, b, da, *, name, tc=256):
    S, F2 = u.shape
    Fd = F2 // 2
    width = w.shape[0]
    nb = Fd // tc

    def body(g_ref, v_ref, w_ref, b_ref, da_ref, du_ref, dw_ref, db_ref, a_ref):
        def dacc_of(r0, ls, acc, sg):
            rs = slice(r0, r0 + STRIP)
            dav, val, silu = da_ref[rs, ls], v_ref[rs, ls], acc * sg
            a_ref[rs, ls] = (silu * val).astype(BF16)
            du_ref[1, rs, ls] = (dav * silu).astype(BF16)
            return dav * val * (sg * (1.0 + acc * (1.0 - sg)))

        def dx_store(r0, ls, dx):
            du_ref[0, r0:r0 + STRIP, ls] = dx.astype(BF16)

        _conv_back_block(S, tc, width, w_ref, b_ref, g_ref, dacc_of, dx_store, dw_ref, db_ref)

    blk = pl.BlockSpec((S, tc), lambda j: (0, j))
    return pl.pallas_call(
        body, name=name, grid=(nb,),
        in_specs=[blk, pl.BlockSpec((S, tc), lambda j: (0, j + nb)), pl.BlockSpec((width, tc), lambda j: (0, j)),
                  pl.BlockSpec((1, tc), lambda j: (0, j)), blk],
        out_specs=[pl.BlockSpec((2, S, tc), lambda j: (0, 0, j)), pl.BlockSpec((width, tc), lambda j: (0, j)),
                   pl.BlockSpec((1, tc), lambda j: (0, j)), blk],
        out_shape=[jax.ShapeDtypeStruct((2, S, Fd), BF16),
                   jax.ShapeDtypeStruct((width, Fd), F32), jax.ShapeDtypeStruct((1, Fd), F32),
                   jax.ShapeDtypeStruct((S, Fd), BF16)],
        compiler_params=_cp(1),
    )(u, u, w, b, da)


def _ssd_prep(dtr, dt_bias, a_log, *, name="ssd_prep"):
    S = dtr.shape[0]

    def body(d_ref, b_ref, al_ref, dt_ref, ac_ref, sg_ref, act_ref):
        lane = _iota((CHUNK, 128), 1)
        valid = lane < SSM_HEADS
        z = d_ref[...] + b_ref[...]
        dt = jnp.where(valid, jnp.maximum(z, 0.0) + jnp.log(1.0 + jnp.exp(-jnp.abs(z))), 0.0)
        a = dt * (-jnp.exp(al_ref[...]))
        row = _iota((CHUNK, 128), 0)
        k = 1
        while k < CHUNK:
            a = a + jnp.where(row >= k, pltpu.roll(a, k, axis=0), 0.0)
            k *= 2
        sg = jnp.where(valid, _sigmoid(z), 0.0)
        for arr, ref in ((dt, dt_ref), (a, ac_ref), (sg, sg_ref)):
            for g in range(SSM_GROUPS):
                ref[g] = jnp.where(lane < 4, arr if g == 0 else pltpu.roll(arr, 128 - 4 * g, axis=1), 0.0)
        act_ref[...] = a.T[:SSM_HEADS, :]

    blk = pl.BlockSpec((CHUNK, 128), lambda i: (i, 0))
    vec = pl.BlockSpec((1, 128), lambda i: (0, 0))
    grp = pl.BlockSpec((SSM_GROUPS, CHUNK, 128), lambda i: (0, i, 0))
    return pl.pallas_call(
        body, name=name, grid=(S // CHUNK,), in_specs=[blk, vec, vec],
        out_specs=[grp, grp, grp, pl.BlockSpec((SSM_HEADS, CHUNK), lambda i: (0, i))],
        out_shape=[jax.ShapeDtypeStruct((SSM_GROUPS, S, 128), F32)] * 3 + [jax.ShapeDtypeStruct((SSM_HEADS, S), F32)],
        compiler_params=_cp(1),
    )(dtr, dt_bias, a_log)


SSD_GPS = 4


def _expand4(v, lanes):
    out = jnp.broadcast_to(v[:, 3:4], lanes.shape)
    for hh in (2, 1, 0):
        out = jnp.where(lanes < 64 * (hh + 1), v[:, hh:hh + 1], out)
    return out


def _ssd_fwd(xbc, dt_g, ac_g, ac_t, *, name="ssd_fwd", dep=None):
    S = xbc.shape[0]
    nc = S // CHUNK
    Lc = CHUNK

    def body(x_ref, b_ref, c_ref, dt_ref, ac_ref, act_ref, *rest):
        y_ref, st_out_ref, st_ref = rest[-3:]
        g2 = pl.program_id(0)
        c = pl.program_id(1)

        @pl.when(c == 0)
        def _():
            st_ref[...] = jnp.zeros_like(st_ref)

        causal = _iota((Lc, Lc), 0) >= _iota((Lc, Lc), 1)
        lane256 = _iota((Lc, 256), 1)
        lane128 = _iota((Lc, 128), 1)
        row128 = _iota((128, 128), 0)
        for gg in range(SSD_GPS):
            g = SSD_GPS * g2 + gg
            bv = b_ref[:, 128 * gg:128 * (gg + 1)]
            cbf = c_ref[:, 128 * gg:128 * (gg + 1)].astype(BF16)
            cb = _dot(cbf, bv.astype(BF16), 1, 1)
            dtg, acg = dt_ref[gg], ac_ref[gg]
            ac_last = ac_ref[gg, pl.ds(Lc - 1, 1), :]
            dt4 = _expand4(dtg, lane256)
            ac4 = _expand4(acg, lane256)
            e4 = jnp.exp(ac4)
            xdb = (x_ref[:, 256 * gg:256 * (gg + 1)] * dt4).astype(BF16)
            st_out_ref[gg] = st_ref[gg]
            for p in range(2):
                xd_p = xdb[:, 128 * p:128 * (p + 1)]
                st_p = st_ref[gg, p]
                ys, sn, cds = [], [], []
                for q in range(2):
                    hh = 2 * p + q
                    a_col = acg[:, hh:hh + 1]
                    a_row = act_ref[pl.ds(4 * g + hh, 1), :]
                    dec = jnp.exp(jnp.where(causal, a_col - a_row, NEG))
                    w = (cb * dec).astype(BF16)
                    ys.append(_dot(w, xd_p))
                    al = ac_last[:, hh:hh + 1]
                    dte = jnp.exp(al - a_col)
                    sn.append(_dot(xd_p, (bv * dte).astype(BF16), 0, 0))
                    cds.append(jnp.exp(al))
                y_diag = jnp.where(lane128 < 64, ys[0], ys[1])
                y_off = _dot(cbf, st_p.astype(BF16), 1, 1) * e4[:, 128 * p:128 * (p + 1)]
                y_ref[:, 256 * gg + 128 * p:256 * gg + 128 * (p + 1)] = y_diag + y_off
                st_ref[gg, p] = jnp.where(row128 < 64, st_p * cds[0] + sn[0], st_p * cds[1] + sn[1])

    G = SSD_GPS
    per_g = lambda g, c: (g, c, 0)
    return pl.pallas_call(
        body, name=name, grid=(SSM_GROUPS // G, nc),
        in_specs=[pl.BlockSpec((Lc, 256 * G), lambda g, c: (c, g)),
                  pl.BlockSpec((Lc, 128 * G), lambda g, c: (c, 16 // G + g)),
                  pl.BlockSpec((Lc, 128 * G), lambda g, c: (c, 24 // G + g)),
                  pl.BlockSpec((G, Lc, 128), per_g), pl.BlockSpec((G, Lc, 128), per_g),
                  pl.BlockSpec((SSM_HEADS, Lc), lambda g, c: (0, c))] + ([] if dep is None else [pl.BlockSpec(memory_space=pl.ANY)]),
        out_specs=[pl.BlockSpec((Lc, 256 * G), lambda g, c: (c, g)),
                   pl.BlockSpec((G, None, 2, 128, 128), lambda g, c: (g, c, 0, 0, 0))],
        out_shape=[jax.ShapeDtypeStruct((S, 2048), F32), jax.ShapeDtypeStruct((SSM_GROUPS, nc, 2, 128, 128), F32)],
        scratch_shapes=[pltpu.VMEM((G, 2, 128, 128), F32)], compiler_params=_cp(2),
    )(xbc, xbc, xbc, dt_g, ac_g, ac_t, *([] if dep is None else [dep]))


def _ssd_bwd(xbc, dt_g, ac_g, ac_t, states, dy, dexp, *, name="ssd_bwd", dep=None):
    S = xbc.shape[0]
    nc = S // CHUNK
    Lc = CHUNK

    def body(x_ref, b_ref, c_ref, dt_ref, ac_ref, act_ref, st_ref, dy_ref, d_ref, *rest):
        dx_ref, db_ref, dc_ref, dh_ref, ds_ref = rest[-5:]
        g2 = pl.program_id(0)
        cc = pl.program_id(1)

        @pl.when(cc == 0)
        def _():
            ds_ref[...] = jnp.zeros_like(ds_ref)

        causal = _iota((Lc, Lc), 0) >= _iota((Lc, Lc), 1)
        lane256 = _iota((Lc, 256), 1)
        lane128 = _iota((Lc, 128), 1)
        row128 = _iota((128, 128), 0)
        ind_rows = _iota((256, 128), 0) >> 6
        ind_cols = _iota((256, 128), 1)
        ind_a = (ind_rows == ind_cols).astype(BF16)
        ind_b = (ind_rows + 4 == ind_cols).astype(BF16)
        for gg in range(SSD_GPS):
            g = SSD_GPS * g2 + gg
            bv = b_ref[:, 128 * gg:128 * (gg + 1)]
            cv = c_ref[:, 128 * gg:128 * (gg + 1)]
            bbf, cbf = bv.astype(BF16), cv.astype(BF16)
            cb = _dot(cbf, bbf, 1, 1)
            dtg, acg = dt_ref[gg], ac_ref[gg]
            ac_last = ac_ref[gg, pl.ds(Lc - 1, 1), :]
            dt4 = _expand4(dtg, lane256)
            ac4 = _expand4(acg, lane256)
            acl4 = _expand4(ac_last, _iota((1, 256), 1))
            e4 = jnp.exp(ac4)
            dte4 = jnp.exp(acl4 - ac4)
            xv = x_ref[:, 256 * gg:256 * (gg + 1)]
            xd = xv * dt4
            xdb = xd.astype(BF16)
            dyv = dy_ref[:, 256 * gg:256 * (gg + 1)]
            dcb = jnp.zeros((Lc, Lc), F32)
            dc_acc = jnp.zeros((Lc, 128), F32)
            db_acc = jnp.zeros((Lc, 128), F32)
            u_parts, dxd_parts, ends = [], [], []
            for p in range(2):
                sl = slice(128 * p, 128 * (p + 1))
                xd_p, xdb_p, dy_p = xd[:, sl], xdb[:, sl], dyv[:, sl]
                dyb_p = dy_p.astype(BF16)
                e_p, dte_p = e4[:, sl], dte4[:, sl]
                sp = st_ref[gg, p]
                spb = sp.astype(BF16)
                dsn = ds_ref[gg, p]
                dsnb = dsn.astype(BF16)
                yds, dxds, cds = [], [], []
                for q in range(2):
                    hh = 2 * p + q
                    a_col = acg[:, hh:hh + 1]
                    a_row = act_ref[pl.ds(4 * g + hh, 1), :]
                    dec = jnp.exp(jnp.where(causal, a_col - a_row, NEG))
                    w = (cb * dec).astype(BF16)
                    head = (lane128 < 64) if q == 0 else (lane128 >= 64)
                    dym = jnp.where(head, dyb_p, jnp.zeros_like(dyb_p))
                    dw = _dot(dym, xdb_p, 1, 1)
                    dcb = dcb + dw * dec
                    yds.append(_dot(w, xdb_p))
                    dxds.append(_dot(w, dyb_p, 0, 0))
                    cds.append(jnp.exp(ac_last[:, hh:hh + 1]))
                y_diag = jnp.where(lane128 < 64, yds[0], yds[1])
                dxd_diag = jnp.where(lane128 < 64, dxds[0], dxds[1])
                y_off = _dot(cbf, spb, 1, 1) * e_p
                dgp = dy_p * e_p
                dgb = dgp.astype(BF16)
                dc_acc = dc_acc + _dot(dgb, spb)
                dsp = _dot(dgb, cbf, 0, 0)
                cd_col = jnp.where(row128[:, 0:1] < 64, cds[0], cds[1])
                qm = _dot(bbf, dsnb, 1, 1)
                dxd_state = dte_p * qm
                db_acc = db_acc + _dot((xd_p * dte_p).astype(BF16), dsnb)
                t_p = xd_p * dxd_state
                prod = dsn * sp
                e0 = jnp.sum(jnp.sum(jnp.where(row128 < 64, prod, 0.0), axis=1, keepdims=True), axis=0, keepdims=True)
                e1 = jnp.sum(jnp.sum(jnp.where(row128 >= 64, prod, 0.0), axis=1, keepdims=True), axis=0, keepdims=True)
                tcol = jnp.sum(t_p, axis=0, keepdims=True)
                lane1 = _iota((1, 128), 1)
                t0 = jnp.sum(jnp.where(lane1 < 64, tcol, 0.0), axis=1, keepdims=True)
                t1 = jnp.sum(jnp.where(lane1 >= 64, tcol, 0.0), axis=1, keepdims=True)
                ends.append(e0 * cds[0] + t0)
                ends.append(e1 * cds[1] + t1)
                ds_ref[gg, p] = dsn * cd_col + dsp
                u_parts.append(dyb_p.astype(F32) * y_diag - xdb_p.astype(F32) * dxd_diag + dy_p * y_off - t_p)
                dxd_parts.append(dxd_diag + dxd_state)
            dxd = jnp.concatenate(dxd_parts, axis=1)
            u_all = jnp.concatenate(u_parts, axis=1)
            dx_ref[:, 256 * gg:256 * (gg + 1)] = dxd * dt4 + dyv * d_ref[:, 256 * gg:256 * (gg + 1)]
            dcbb = dcb.astype(BF16)
            dc_ref[:, 128 * gg:128 * (gg + 1)] = dc_acc + _dot(dcbb, bbf)
            db_ref[:, 128 * gg:128 * (gg + 1)] = db_acc + _dot(dcbb, cbf, 0, 0)
            lane = _iota((Lc, 128), 1)
            endv = jnp.zeros((Lc, 128), F32)
            for hh in range(4):
                endv = jnp.where(lane == 8 + hh, ends[hh], endv)
            dh_ref[gg] = _dot3(dxd * xv, ind_a) + _dot3(u_all, ind_b) + endv

    G = SSD_GPS
    rev = lambda c: nc - 1 - c
    per_g = lambda g, c: (g, rev(c), 0)
    return pl.pallas_call(
        body, name=name, grid=(SSM_GROUPS // G, nc),
        in_specs=[pl.BlockSpec((Lc, 256 * G), lambda g, c: (rev(c), g)),
                  pl.BlockSpec((Lc, 128 * G), lambda g, c: (rev(c), 16 // G + g)),
                  pl.BlockSpec((Lc, 128 * G), lambda g, c: (rev(c), 24 // G + g)),
                  pl.BlockSpec((G, Lc, 128), per_g), pl.BlockSpec((G, Lc, 128), per_g),
                  pl.BlockSpec((SSM_HEADS, Lc), lambda g, c: (0, rev(c))),
                  pl.BlockSpec((G, None, 2, 128, 128), lambda g, c: (g, rev(c), 0, 0, 0)),
                  pl.BlockSpec((Lc, 256 * G), lambda g, c: (rev(c), g)),
                  pl.BlockSpec((1, 256 * G), lambda g, c: (0, g))] + ([] if dep is None else [pl.BlockSpec(memory_space=pl.ANY)]),
        out_specs=[pl.BlockSpec((Lc, 256 * G), lambda g, c: (rev(c), g)),
                   pl.BlockSpec((Lc, 128 * G), lambda g, c: (rev(c), g)),
                   pl.BlockSpec((Lc, 128 * G), lambda g, c: (rev(c), g)),
                   pl.BlockSpec((G, Lc, 128), per_g)],
        out_shape=[jax.ShapeDtypeStruct((S, 2048), F32), jax.ShapeDtypeStruct((S, 1024), F32),
                   jax.ShapeDtypeStruct((S, 1024), F32), jax.ShapeDtypeStruct((SSM_GROUPS, S, 128), F32)],
        scratch_shapes=[pltpu.VMEM((G, 2, 128, 128), F32)], compiler_params=_cp(2),
    )(xbc, xbc, xbc, dt_g, ac_g, ac_t, states, dy, dexp, *([] if dep is None else [dep]))


def _ssd_post(dhead, dt_g, sg_g, alog_g, *, name="ssd_post"):
    S = dhead.shape[1]
    nc = S // CHUNK
    Lc = CHUNK

    def body(dh_ref, dt_ref, sg_ref, al_ref, o_ref, s_ref):
        @pl.when(pl.program_id(0) == 0)
        def _():
            s_ref[...] = jnp.zeros_like(s_ref)

        lane = _iota((Lc, 128), 1)
        row = _iota((Lc, 128), 0)
        row8 = _iota((8, 128), 0)
        out = jnp.zeros((Lc, 128), F32)
        for g in range(SSM_GROUPS):
            dh = dh_ref[g]
            a_neg = -jnp.exp(al_ref[g])
            dac = jnp.where(lane < 4, pltpu.roll(dh, 124, axis=1), 0.0)
            end = jnp.where(lane < 4, pltpu.roll(dh, 120, axis=1), 0.0)
            k = 1
            while k < Lc:
                dac = dac + jnp.where(row < Lc - k, pltpu.roll(dac, Lc - k, axis=0), 0.0)
                k *= 2
            da = dac + end
            ddt = jnp.where(lane < 4, da * a_neg + dh, 0.0)
            ddtr = ddt * sg_ref[g]
            out = out + (ddtr if g == 0 else pltpu.roll(ddtr, 4 * g, axis=1))
            dal = jnp.sum(da * dt_ref[g], axis=0, keepdims=True) * a_neg
            dbias = jnp.sum(ddtr, axis=0, keepdims=True)
            part = jnp.where(row8 == 0, dal, jnp.where(row8 == 1, dbias, 0.0))
            s_ref[g] += part
        o_ref[...] = out.astype(BF16)

    grp = pl.BlockSpec((SSM_GROUPS, Lc, 128), lambda c: (0, c, 0))
    whole = lambda r: pl.BlockSpec((SSM_GROUPS, r, 128), lambda c: (0, 0, 0))
    return pl.pallas_call(
        body, name=name, grid=(nc,), in_specs=[grp, grp, grp, whole(1)],
        out_specs=[pl.BlockSpec((Lc, 128), lambda c: (c, 0)), whole(8)],
        out_shape=[jax.ShapeDtypeStruct((S, 128), BF16), jax.ShapeDtypeStruct((SSM_GROUPS, 8, 128), F32)],
        compiler_params=_cp(1),
    )(dhead, dt_g, sg_g, alog_g)


def _gate_fwd(y, xbc, zx, dexp, gn, *, name="gate_fwd", tr=256, dep=None):
    S = y.shape[0]
    W = 2048
    gw = W // SSM_GROUPS

    def body(y_ref, x_ref, z_ref, d_ref, g_ref, *rest):
        o_ref = rest[-1]
        z = z_ref[...]
        u = (y_ref[...] + x_ref[...] * d_ref[...]) * (z * _sigmoid(z))
        gv = g_ref[...]
        for q in range(SSM_GROUPS):
            sl = slice(gw * q, gw * (q + 1))
            uq = u[:, sl]
            r = lax.rsqrt(jnp.mean(uq * uq, axis=-1, keepdims=True) + EPS)
            o_ref[:, sl] = (uq * r * gv[:, sl]).astype(BF16)

    row = pl.BlockSpec((tr, W), lambda i: (i, 0))
    vec = pl.BlockSpec((1, W), lambda i: (0, 0))
    return pl.pallas_call(
        body, name=name, grid=(S // tr,),
        in_specs=[row, row, row, vec, vec] + ([] if dep is None else [pl.BlockSpec(memory_space=pl.ANY)]), out_specs=row,
        out_shape=jax.ShapeDtypeStruct((S, W), BF16), compiler_params=_cp(1),
    )(y, xbc, zx, dexp, gn, *([] if dep is None else [dep]))


def _gate_bwd(y, xbc, zx, dexp, gn, dout, *, name="gate_bwd", tr=256):
    S = y.shape[0]
    W = 2048
    gw = W // SSM_GROUPS
    steps = S // tr

    def body(y_ref, x_ref, z_ref, d_ref, g_ref, do_ref, dy_ref, dz_ref, dg_ref, dd_ref, acc_ref):
        i = pl.program_id(0)

        @pl.when(i == 0)
        def _():
            acc_ref[...] = jnp.zeros_like(acc_ref)

        z = z_ref[...]
        sg = _sigmoid(z)
        sz = z * sg
        xs = x_ref[...]
        yt = y_ref[...] + xs * d_ref[...]
        u = yt * sz
        gv = g_ref[...]
        do = do_ref[...]
        dgs = []
        for q in range(SSM_GROUPS):
            sl = slice(gw * q, gw * (q + 1))
            uq = u[:, sl]
            r = lax.rsqrt(jnp.mean(uq * uq, axis=-1, keepdims=True) + EPS)
            uh = uq * r
            dq = do[:, sl]
            duh = dq * gv[:, sl]
            duq = r * (duh - uh * jnp.mean(duh * uh, axis=-1, keepdims=True))
            dgs.append(jnp.sum(dq * uh, axis=0, keepdims=True))
            dyt = duq * sz[:, sl]
            dy_ref[:, sl] = dyt
            dz_ref[:, sl] = (duq * yt[:, sl] * (sg[:, sl] * (1.0 + z[:, sl] * (1.0 - sg[:, sl])))).astype(BF16)
            acc_ref[:, sl] += jnp.sum(dyt * xs[:, sl], axis=0, keepdims=True)
        dg = jnp.concatenate(dgs, axis=1)

        @pl.when(i == 0)
        def _():
            dg_ref[...] = dg

        @pl.when(i > 0)
        def _():
            dg_ref[...] += dg

        @pl.when(i == steps - 1)
        def _():
            ind = ((_iota((W, 128), 0) >> 6) == _iota((W, 128), 1)).astype(BF16)
            dd_ref[...] = _dot3(jnp.broadcast_to(acc_ref[...], (8, W)), ind)[0:1, :]

    row = pl.BlockSpec((tr, W), lambda i: (i, 0))
    vec = pl.BlockSpec((1, W), lambda i: (0, 0))
    return pl.pallas_call(
        body, name=name, grid=(steps,), in_specs=[row, row, row, vec, vec, row],
        out_specs=[row, row, vec, pl.BlockSpec((1, 128), lambda i: (0, 0))],
        out_shape=[jax.ShapeDtypeStruct((S, W), F32), jax.ShapeDtypeStruct((S, W), BF16),
                   jax.ShapeDtypeStruct((1, W), F32), jax.ShapeDtypeStruct((1, 128), F32)],
        scratch_shapes=[pltpu.VMEM((1, W), F32)], compiler_params=_cp(1),
    )(y, xbc, zx, dexp, gn, dout)


def _rope_cs(posf, *, name="rope_tables", tr=256):
    S = posf.shape[0]

    def body(p_ref, c_ref, s_ref):
        j = (_iota((tr, 128), 1) & 31).astype(F32)
        ang = p_ref[...] * jnp.exp(j * (-math.log(ROPE_THETA) / 32.0))
        c_ref[...] = jnp.cos(ang)
        s_ref[...] = jnp.sin(ang)

    blk = pl.BlockSpec((tr, 128), lambda i: (i, 0))
    return pl.pallas_call(
        body, name=name, grid=(S // tr,), in_specs=[pl.BlockSpec((tr, 1), lambda i: (i, 0))], out_specs=[blk, blk],
        out_shape=[jax.ShapeDtypeStruct((S, 128), F32)] * 2, compiler_params=_cp(1),
    )(posf)


def _rope_tables(c_ref, s_ref, shape):
    reps = shape[1] // 128
    return jnp.tile(c_ref[...], (1, reps)), jnp.tile(s_ref[...], (1, reps)), (_iota(shape, 1) & 63) < 32


def _hn_inds(W):
    ind = ((_iota((W, 128), 0) >> 6) == _iota((W, 128), 1)).astype(BF16)
    ind_t = ((_iota((128, W), 1) >> 6) == _iota((128, W), 0)).astype(BF16)
    return ind, ind_t


def _hnrope_fwd(xin, col0, W, gain_w, rope, *, name, tr=256):
    S = xin.shape[0]
    off = col0 // W
    nh = W // HEAD

    def body(x_ref, g_ref, c_ref, s_ref, o_ref):
        x = x_ref[...]
        ind, ind_t = _hn_inds(W)
        r = lax.rsqrt(_dot3(x * x, ind) * (1.0 / HEAD) + EPS)
        xn = x * _dot3(r, ind_t) * g_ref[...]
        cs, sn, half = _rope_tables(c_ref, s_ref, (tr, W))
        rot = jnp.where(half, -pltpu.roll(xn, W - 32, axis=1), pltpu.roll(xn, 32, axis=1))
        out = (xn * cs + rot * sn).astype(BF16)
        for h in range(nh):
            o_ref[h] = out[:, HEAD * h:HEAD * (h + 1)]

    tab = pl.BlockSpec((tr, 128), lambda i: (i, 0))
    return pl.pallas_call(
        body, name=name, grid=(S // tr,),
        in_specs=[pl.BlockSpec((tr, W), lambda i: (i, off)), pl.BlockSpec((1, W), lambda i: (0, 0)), tab, tab],
        out_specs=pl.BlockSpec((nh, tr, HEAD), lambda i: (0, i, 0)), out_shape=jax.ShapeDtypeStruct((nh, S, HEAD), BF16),
        compiler_params=_cp(1),
    )(xin, gain_w, *rope)


def _hnrope_bwd(xin, col0, W, gain_w, rope, dout, *, name, tr=256):
    S = xin.shape[0]
    off = col0 // W
    steps = S // tr
    nh = W // HEAD

    def body(x_ref, g_ref, c_ref, s_ref, do_ref, dx_ref, cs_ref, dg_ref, acc_ref):
        i = pl.program_id(0)
        x = x_ref[...]
        ind, ind_t = _hn_inds(W)
        r = lax.rsqrt(_dot3(x * x, ind) * (1.0 / HEAD) + EPS)
        rw = _dot3(r, ind_t)
        xh = x * rw
        cs, sn, half = _rope_tables(c_ref, s_ref, (tr, W))
        do = jnp.concatenate([do_ref[h] for h in range(nh)], axis=1).astype(F32)
        gs = do * sn
        g1 = do * cs + jnp.where(half, pltpu.roll(gs, W - 32, axis=1), -pltpu.roll(gs, 32, axis=1))
        dxh = g1 * g_ref[...]
        t = _dot3(dxh * xh, ind) * (1.0 / HEAD)
        dx = rw * (dxh - xh * _dot3(t, ind_t))
        dx_ref[...] = dx.astype(BF16)
        cpart = jnp.sum(dx, axis=0, keepdims=True)
        gpart = jnp.sum(g1 * xh, axis=0, keepdims=True)

        @pl.when(i == 0)
        def _():
            cs_ref[...] = cpart
            acc_ref[...] = gpart

        @pl.when(i > 0)
        def _():
            cs_ref[...] += cpart
            acc_ref[...] += gpart

        @pl.when(i == steps - 1)
        def _():
            fold = ((_iota((W, 128), 0) & 63) == _iota((W, 128), 1)).astype(BF16)
            dg_ref[...] = _dot3(jnp.broadcast_to(acc_ref[...], (8, W)), fold)[0:1, :]

    tab = pl.BlockSpec((tr, 128), lambda i: (i, 0))
    return pl.pallas_call(
        body, name=name, grid=(steps,),
        in_specs=[pl.BlockSpec((tr, W), lambda i: (i, off)), pl.BlockSpec((1, W), lambda i: (0, 0)), tab, tab,
                  pl.BlockSpec((nh, tr, HEAD), lambda i: (0, i, 0))],
        out_specs=[pl.BlockSpec((tr, W), lambda i: (i, 0)), pl.BlockSpec((1, W), lambda i: (0, 0)),
                   pl.BlockSpec((1, 128), lambda i: (0, 0))],
        out_shape=[jax.ShapeDtypeStruct((S, W), BF16), jax.ShapeDtypeStruct((1, W), F32),
                   jax.ShapeDtypeStruct((1, 128), F32)],
        scratch_shapes=[pltpu.VMEM((1, W), F32)], compiler_params=_cp(1),
    )(xin, gain_w, *rope, dout)


def _attn_band():
    qi = jnp.arange(ATT_G * WINDOW)[:, None] % WINDOW
    ki = jnp.arange(2 * WINDOW)[None, :]
    rel = qi + WINDOW - ki
    ok = (rel >= 0) & (rel < WINDOW)
    return jnp.stack([jnp.where(ok & (ki >= WINDOW), 0.0, NEG), jnp.where(ok, 0.0, NEG)]).astype(F32)


def _attn_probs(q, kb, sink_ref, band_ref, h, i):
    s = _dot(q, kb, 1, 1) * (HEAD ** -0.5) + band_ref[jnp.minimum(i, 1)]
    r1 = _iota((4 * WINDOW, 1), 0)
    sink = jnp.where(r1 < WINDOW, sink_ref[4 * h], jnp.where(r1 < 2 * WINDOW, sink_ref[4 * h + 1],
                     jnp.where(r1 < 3 * WINDOW, sink_ref[4 * h + 2], sink_ref[4 * h + 3])))
    m = jnp.maximum(jnp.max(s, axis=1, keepdims=True), sink)
    p = jnp.exp(s - m)
    ps = jnp.exp(sink - m)
    inv = 1.0 / (jnp.sum(p, axis=1, keepdims=True) + ps)
    return p * inv, ps * inv


ATT_HPS = 4
_BAND = pl.BlockSpec((2, ATT_G * WINDOW, 2 * WINDOW), lambda h, i: (0, 0, 0))


def _attn_specs(S):
    qspec = pl.BlockSpec((ATT_HPS, ATT_G, WINDOW, HEAD), lambda h, i: (h, 0, i, 0))
    cur = pl.BlockSpec((ATT_HPS, WINDOW, HEAD), lambda h, i: (h, i, 0))
    prev = pl.BlockSpec((ATT_HPS, WINDOW, HEAD), lambda h, i: (h, jnp.maximum(i - 1, 0), 0))
    tok = pl.BlockSpec((WINDOW, ATT_HPS * ATT_G * HEAD), lambda h, i: (i, h))
    return qspec, cur, prev, tok


def _attn_fwd(qh, kh, vh, sinks, *, name="attn_fwd"):
    S = kh.shape[1]
    nb = S // WINDOW

    def body(s_ref, band_ref, q_ref, kc_ref, kp_ref, vc_ref, vp_ref, o_ref):
        h2, i = pl.program_id(0), pl.program_id(1)
        outs = []
        for hh in range(ATT_HPS):
            q = q_ref[hh].reshape(ATT_G * WINDOW, HEAD)
            kb = jnp.concatenate([kp_ref[hh], kc_ref[hh]], axis=0)
            vb = jnp.concatenate([vp_ref[hh], vc_ref[hh]], axis=0)
            probs, _ = _attn_probs(q, kb, s_ref, band_ref, ATT_HPS * h2 + hh, i)
            o = _dot(probs.astype(BF16), vb).astype(BF16)
            outs += [o[WINDOW * g:WINDOW * (g + 1)] for g in range(ATT_G)]
        o_ref[...] = jnp.concatenate(outs, axis=1)

    qspec, cur, prev, tok = _attn_specs(S)
    return pl.pallas_call(
        body, name=name, grid=(ATT_KV // ATT_HPS, nb),
        in_specs=[pl.BlockSpec(memory_space=pltpu.SMEM), _BAND, qspec, cur, prev, cur, prev], out_specs=tok,
        out_shape=jax.ShapeDtypeStruct((S, ATT_KV * ATT_G * HEAD), BF16), compiler_params=_cp(2),
    )(sinks, _attn_band(), qh, kh, kh, vh, vh)


def _attn_bwd(qh, kh, vh, sinks, doh, *, name="attn_bwd"):
    S = kh.shape[1]
    nb = S // WINDOW

    def body(s_ref, band_ref, q_ref, kc_ref, kp_ref, vc_ref, vp_ref, do_ref, dq_ref, dk_ref, dv_ref, dsk_ref):
        h2, i = pl.program_id(0), pl.program_id(1)

        @pl.when(i == 0)
        def _():
            dk_ref[...] = jnp.zeros_like(dk_ref)
            dv_ref[...] = jnp.zeros_like(dv_ref)
            dsk_ref[...] = jnp.zeros_like(dsk_ref)

        dov = do_ref[...]
        cur = pl.multiple_of(i * WINDOW, WINDOW)
        lane = _iota((8, 128), 1)
        row = _iota((8, 128), 0)
        scale = HEAD ** -0.5
        for hh in range(ATT_HPS):
            q = q_ref[hh].reshape(ATT_G * WINDOW, HEAD)
            do = jnp.concatenate([dov[:, HEAD * (ATT_G * hh + g):HEAD * (ATT_G * hh + g + 1)] for g in range(ATT_G)], axis=0)
            kb = jnp.concatenate([kp_ref[hh], kc_ref[hh]], axis=0)
            vb = jnp.concatenate([vp_ref[hh], vc_ref[hh]], axis=0)
            probs, psink = _attn_probs(q, kb, s_ref, band_ref, ATT_HPS * h2 + hh, i)
            dp = _dot(do, vb, 1, 1)
            delta = jnp.sum(probs * dp, axis=1, keepdims=True)
            ds = (probs * (dp - delta)).astype(BF16)
            dq_ref[hh] = (_dot(ds, kb) * scale).reshape(ATT_G, WINDOW, HEAD)
            dkb = _dot(ds, q, 0, 0) * scale
            dvb = _dot(probs.astype(BF16), do, 0, 0)
            dk_ref[hh, pl.ds(cur, WINDOW), :] += dkb[WINDOW:, :]
            dv_ref[hh, pl.ds(cur, WINDOW), :] += dvb[WINDOW:, :]
            prv = pl.multiple_of(jnp.maximum(i - 1, 0) * WINDOW, WINDOW)
            dk_ref[hh, pl.ds(prv, WINDOW), :] += dkb[:WINDOW, :]
            dv_ref[hh, pl.ds(prv, WINDOW), :] += dvb[:WINDOW, :]

            dsr = -psink * delta
            upd = jnp.zeros((8, 128), F32)
            for gq in range(ATT_G):
                v = jnp.sum(dsr[gq * WINDOW:(gq + 1) * WINDOW, :], axis=0, keepdims=True)
                upd = jnp.where((lane == gq) & (row == 0), v, upd)
            dsk_ref[hh] += upd

    qspec, cur, prev, tok = _attn_specs(S)
    full = pl.BlockSpec((ATT_HPS, S, HEAD), lambda h, i: (h, 0, 0))
    return pl.pallas_call(
        body, name=name, grid=(ATT_KV // ATT_HPS, nb),
        in_specs=[pl.BlockSpec(memory_space=pltpu.SMEM), _BAND, qspec, cur, prev, cur, prev, tok],
        out_specs=[qspec, full, full, pl.BlockSpec((ATT_HPS, 8, 128), lambda h, i: (h, 0, 0))],
        out_shape=[jax.ShapeDtypeStruct((ATT_KV, ATT_G, S, HEAD), F32), jax.ShapeDtypeStruct((ATT_KV, S, HEAD), F32),
                   jax.ShapeDtypeStruct((ATT_KV, S, HEAD), F32), jax.ShapeDtypeStruct((ATT_KV, 8, 128), F32)],
        compiler_params=_cp(2),
    )(sinks, _attn_band(), qh, kh, kh, vh, vh, doh)


def _heads_major(t, nh):
    S = t.shape[0]
    return t.reshape(S, nh, HEAD).transpose(1, 0, 2)


def _tokens_major(t):
    nh, S, _ = t.shape
    return t.transpose(1, 0, 2).reshape(S, nh * HEAD)


class _NoComm:
    def late_start(self, part, after):
        return None

    def late_arrived(self, part, after):
        return None

    def late_weights(self, w, after, part):
        return w

    def advance(self, after, group=None, tensors=None):
        return None


def _local_step(x, posf, target, w, comm=None):
    S, D = x.shape
    gr = {}
    comm = comm or _NoComm()

    zx, h1 = _norm_mm(x, w["a_norm"], w["w_zx"], name="in_proj_zx", dep=w.get("dep"))
    dtr = _mm(h1, w["w_dt"], name="in_proj_dt")
    xbc = _conv_silu_fwd(zx, 2048, 4096, w["a_conv_w"], w["a_conv_b"], name="a_conv_f")
    dt_g, ac_g, sg_g, ac_t = _ssd_prep(dtr, w["a_dt_bias"], w["a_A_log"])
    y_ssd, states = _ssd_fwd(xbc, dt_g, ac_g, ac_t, dep=comm.late_start(1, xbc))
    yg = _gate_fwd(y_ssd, xbc, zx, w["a_Dexp"], w["a_gnorm"], dep=comm.late_arrived(0, [y_ssd]))
    w = comm.late_weights(w, yg, 0)
    x1 = _mm(yg, w["a_out_proj"], res=x, name="out_proj")

    FW = w["f_w_in"][0].shape[2]

    def ffn_fwd(xin, l, loss_target=None):
        u, h = _norm_mm(xin, w["f_norm"][l], w["f_w_in"][l], name=f"f_in{l}", N=N_CHIPS * FW, tn=FW,
                        b_spec=pl.BlockSpec((None, D, FW), lambda i, j: (j, 0, 0)))
        a = _ffn_act_fwd(u, w["f_conv_w"][l], w["f_conv_b"][l], name=f"f_act_f{l}")
        dep = comm.late_arrived(1, [u]) if l == 0 else None
        xo = _mm(a, w["f_w_down"][l], res=xin, tk=a.shape[1], name=f"f_down{l}", target=loss_target, dep=dep)
        return xo, (h, u)

    x2, ffn0 = ffn_fwd(x1, 0)
    w = comm.late_weights(w, x2, 1)

    kv, hk = _norm_mm(x2, w["kv_norm"], w["w_kv"], bias=w["b_kv"], name="kv_proj")
    q, hq = _norm_mm(x2, w["b_norm"], w["w_q"], bias=w["b_q"], name="q_proj")
    rope = _rope_cs(posf)
    kr = _hnrope_fwd(kv, 0, 256, w["k_norm_w"], rope, name="k_rope_f")
    qr = _hnrope_fwd(q, 0, 1024, w["q_norm_w"], rope, name="q_rope_f")
    qh = qr.reshape(ATT_KV, ATT_G, S, HEAD)
    kh = kr
    vh = _heads_major(kv[:, 256:].astype(BF16), ATT_KV)
    att = _attn_fwd(qh, kh, vh, w["sinks"])
    x3 = _mm(att, w["w_o"], bias=w["b_o"], res=x2, name="o_proj")
    (dy, loss_part), ffn1 = ffn_fwd(x3, 1, target)

    def ffn_bwd(xin, l, saved, dyo, want_colsum, dep=None):
        h, u = saved
        da = _mm(dyo, w["f_w_down"][l], tb=True, name=f"f_down_dx{l}", dep=dep)
        du, dcw, dcb, a = _ffn_act_bwd(u, w["f_conv_w"][l], w["f_conv_b"][l], da, name=f"f_act_b{l}")
        dw_down = _mm(a, dyo, ta=True, out_dtype=BF16, name=f"f_down_dw{l}")
        dw_in = _mm(h, du, ta=True, out_dtype=BF16, name=f"f_in_dw{l}", dims=(D, N_CHIPS * FW, S), tm=D, tn=FW, tk=S,
                    b_spec=pl.BlockSpec((None, S, FW), lambda i, j, k: (j // 2, 0, j % 2)),
                    o_spec=pl.BlockSpec((None, D, FW), lambda i, j, k: (j, i, 0)), o_shape=(N_CHIPS, D, FW))
        ts = _pick(S, (1024, 512, 256))
        pieces = [(pl.BlockSpec((None, ts, FW), lambda i, j, k, q=q: (q // 2, i, q % 2)),
                   pl.BlockSpec((None, 512, FW), lambda i, j, k, q=q: (q, j, 0))) for q in range(N_CHIPS)]
        dh = _mm(du, w["f_w_in"][l], tb=True, name=f"f_in_dx{l}", dims=(S, D, FW), tm=ts, tn=512, tk=FW,
                 a_spec=pieces[0][0], b_spec=pieces[0][1], more=pieces[1:])
        outs = _rms_bwd(xin, [w["f_norm"][l]], [dh], dyo, name=f"f_norm_b{l}", want_colsum=want_colsum)
        g = dict(f_norm=outs[1], f_w_in=dw_in, f_conv_w=dcw, f_conv_b=dcb, f_w_down=dw_down)
        return outs[0], g, (outs[2] if want_colsum else None)

    dx3, gr["ffn1"], db_o = ffn_bwd(x3, 1, ffn1, dy, True)
    gr["b_o"] = db_o
    gr["w_o"] = _mm(att, dx3, ta=True, out_dtype=BF16, name="o_proj_dw")
    datt = _mm(dx3, w["w_o"], tb=True, out_dtype=BF16, name="o_proj_dx")
    dqh, dkh, dvh, dsk = _attn_bwd(qh, kh, vh, w["sinks"], datt)
    gr["sinks"] = dsk[:, 0, :4].reshape(1, 16)
    dv = _tokens_major(dvh).astype(BF16)
    dq, db_q, dqn = _hnrope_bwd(q, 0, 1024, w["q_norm_w"], rope, dqh.reshape(16, S, HEAD), name="q_rope_b")
    dk, db_k, dkn = _hnrope_bwd(kv, 0, 256, w["k_norm_w"], rope, dkh, name="k_rope_b")
    gr["q_norm"], gr["k_norm"] = dqn[:, :HEAD], dkn[:, :HEAD]
    gr["b_q"] = db_q
    gr["b_kv"] = jnp.concatenate([db_k, _colsum(dv, name="dv_colsum")], axis=1)
    dkv = jnp.concatenate([dk, dv], axis=1)
    gr["w_q"] = _mm(hq, dq, ta=True, out_dtype=BF16, name="q_proj_dw")
    gr["w_kv"] = _mm(hk, dkv, ta=True, out_dtype=BF16, name="kv_proj_dw")
    tok = comm.advance([gr["w_kv"]], 1, dict(f_down1=gr["ffn1"]["f_w_down"], f_in1=gr["ffn1"]["f_w_in"], w_o=gr["w_o"],
                                             w_q=gr["w_q"], w_kv=gr["w_kv"]))
    dhq = _mm(dq, w["w_q"], tb=True, name="q_proj_dx", dep=tok)
    dhk = _mm(dkv, w["w_kv"], tb=True, name="kv_proj_dx")
    dx2, gr["kv_norm"], gr["b_norm"] = _rms_bwd(x2, [w["kv_norm"], w["b_norm"]], [dhk, dhq], dx3, name="kvq_norm_b")

    dx1, gr["ffn0"], _ = ffn_bwd(x1, 0, ffn0, dx2, False, dep=comm.advance([dx2]))

    gr["a_out_proj"] = _mm(yg, dx1, ta=True, out_dtype=BF16, name="out_proj_dw")
    tok = comm.advance([dx1, gr["a_out_proj"]], 2,
                       dict(f_down0=gr["ffn0"]["f_w_down"], f_in0=gr["ffn0"]["f_w_in"], out_proj=gr["a_out_proj"]))
    dyg = _mm(dx1, w["a_out_proj"], tb=True, name="out_proj_dx", dep=tok)
    dy_ssd, dz, gr["a_gnorm"], dD = _gate_bwd(y_ssd, xbc, zx, w["a_Dexp"], w["a_gnorm"], dyg)
    gr["a_D"] = dD[:, :SSM_HEADS]
    dxs, dB, dC, dhead = _ssd_bwd(xbc, dt_g, ac_g, ac_t, states, dy_ssd, w["a_Dexp"], dep=comm.advance([dy_ssd]))
    ddtr, dsmall = _ssd_post(dhead, dt_g, sg_g, w["a_A_log_g"])
    gr["a_A_log"] = dsmall[:, 0, :4].reshape(1, SSM_HEADS)
    gr["a_dt_bias"] = dsmall[:, 1, :4].reshape(1, SSM_HEADS)
    dxbc, gr["a_conv_w"], gr["a_conv_b"] = _conv_silu_bwd(
        zx, 2048, 4096, w["a_conv_w"], w["a_conv_b"], [(dxs, 0), (dB, 2048), (dC, 3072)], name="a_conv_b")
    gr["w_z"] = _mm(h1, dz, ta=True, out_dtype=BF16, name="in_proj_dwz")
    gr["w_x"] = _mm(h1, dxbc, ta=True, out_dtype=BF16, name="in_proj_dwx")
    gr["w_dt"] = _mm(h1, ddtr, ta=True, out_dtype=BF16, name="in_proj_dwdt")
    ts = _pick(S, (1024, 512, 256))
    wblk = lambda q: pl.BlockSpec((512, 2048), lambda i, j, k: (j, q))
    dh1 = _mm(dz, w["w_zx"], tb=True, name="in_proj_dx", dims=(S, D, 2048), tm=ts, tn=512, tk=2048,
              a_spec=pl.BlockSpec((ts, 2048), lambda i, j, k: (i, 0)), b_spec=wblk(0),
              more=[(dxbc, pl.BlockSpec((ts, 2048), lambda i, j, k: (i, 0)), w["w_zx"], wblk(1)),
                    (dxbc, pl.BlockSpec((ts, 2048), lambda i, j, k: (i, 1)), w["w_zx"], wblk(2)),
                    (ddtr, pl.BlockSpec((ts, 128), lambda i, j, k: (i, 0)), w["w_dt"], pl.BlockSpec((512, 128), lambda i, j, k: (j, 0)))])
    dx0, gr["a_norm"] = _rms_bwd(x, [w["a_norm"]], [dh1], dx1, name="a_norm_b")
    tok = comm.advance([dx0], 3, dict(in_proj=_in_proj_grad(gr).reshape(D, N_CHIPS, -1).transpose(1, 0, 2)))
    return loss_part, dx0, gr, tok


def _prep_small(full, w):
    w["a_norm"] = full["a_norm"]
    w["a_conv_w"] = full["a_conv_w"][0]
    w["a_conv_b"] = full["a_conv_b"]
    pad32 = lambda v: jnp.pad(v, ((0, 0), (0, 128 - SSM_HEADS)))
    w["a_dt_bias"] = pad32(full["a_dt_bias"])
    w["a_A_log"] = pad32(full["a_A_log"])
    w["a_A_log_g"] = jnp.pad(full["a_A_log"].reshape(SSM_GROUPS, 1, 4), ((0, 0), (0, 0), (0, 124)))
    w["a_Dexp"] = jnp.repeat(full["a_D"], HEAD, axis=1)
    w["a_gnorm"] = full["a_gnorm"]
    w["f_norm"] = [full["f_norm"][l:l + 1] for l in range(2)]
    w["f_conv_w"] = [full["f_conv_w"][l] for l in range(2)]
    w["f_conv_b"] = [full["f_conv_b"][l:l + 1] for l in range(2)]
    w["kv_norm"] = full["kv_norm"].reshape(1, -1)
    w["b_kv"] = full["b_kv"].reshape(1, -1)
    w["k_norm_w"] = jnp.tile(full["k_norm"].reshape(1, HEAD), (1, ATT_KV))
    w["b_norm"] = full["b_norm"]
    w["b_q"] = full["b_q"]
    w["q_norm_w"] = jnp.tile(full["q_norm"], (1, ATT_KV * ATT_G))
    w["sinks"] = full["sinks"].reshape(-1)
    w["b_o"] = full["b_o"]
    return w


def _split_in_proj(ip):
    return ip[:, :6144].astype(BF16), jnp.pad(ip[:, 6144:], ((0, 0), (0, 128 - SSM_HEADS))).astype(BF16)


def _join_in_proj(blocks, *, name="in_proj_join", tr=256):
    _, R, cw = blocks.shape
    zx_cols = 3 * 2048
    rest = N_CHIPS * cw - zx_cols

    def body(b_ref, zx_ref, dt_ref):
        whole = jnp.concatenate([b_ref[j] for j in range(N_CHIPS)], axis=1)
        zx_ref[...] = whole[:, :zx_cols]
        dt_ref[...] = jnp.concatenate([whole[:, zx_cols:], jnp.zeros((tr, 128 - rest), BF16)], axis=1)

    return pl.pallas_call(
        body, name=name, grid=(R // tr,), in_specs=[pl.BlockSpec((N_CHIPS, tr, cw), lambda i: (0, i, 0))],
        out_specs=[pl.BlockSpec((tr, zx_cols), lambda i: (i, 0)), pl.BlockSpec((tr, 128), lambda i: (i, 0))],
        out_shape=[jax.ShapeDtypeStruct((R, zx_cols), BF16), jax.ShapeDtypeStruct((R, 128), BF16)],
        compiler_params=_cp(1),
    )(blocks)


def _prep_weights(full):
    w = _prep_small(full, {})
    w["w_zx"], w["w_dt"] = _split_in_proj(full["a_in_proj"][0])
    w["a_out_proj"] = full["a_out_proj"][0].astype(BF16)
    w["f_w_in"] = [full["f_w_in"][l].reshape(1024, N_CHIPS, -1).transpose(1, 0, 2).astype(BF16) for l in range(2)]
    w["f_w_down"] = [full["f_w_down"][l].astype(BF16) for l in range(2)]
    w["w_kv"] = full["w_kv"].astype(BF16)
    w["w_q"] = full["w_q"][0].astype(BF16)
    w["w_o"] = full["w_o"][0].astype(BF16)
    return w


def _small_grads(gr):
    g = {}
    g["a_norm"] = gr["a_norm"]
    g["a_conv_w"] = gr["a_conv_w"][None]
    g["a_conv_b"] = gr["a_conv_b"]
    g["a_dt_bias"], g["a_A_log"], g["a_D"] = gr["a_dt_bias"], gr["a_A_log"], gr["a_D"]
    g["a_gnorm"] = gr["a_gnorm"]
    g["kv_norm"] = gr["kv_norm"].reshape(-1)
    g["b_kv"] = gr["b_kv"].reshape(-1)
    g["k_norm"] = gr["k_norm"].reshape(-1)
    g["b_norm"] = gr["b_norm"]
    g["b_q"] = gr["b_q"]
    g["q_norm"] = gr["q_norm"]
    g["sinks"] = gr["sinks"]
    g["b_o"] = gr["b_o"]
    f = [gr["ffn0"], gr["ffn1"]]
    g["f_norm"] = jnp.concatenate([f[0]["f_norm"], f[1]["f_norm"]], axis=0)
    g["f_conv_w"] = jnp.stack([f[l]["f_conv_w"] for l in range(2)])
    g["f_conv_b"] = jnp.concatenate([f[l]["f_conv_b"] for l in range(2)], axis=0)
    return g


def _in_proj_grad(gr):
    return jnp.concatenate([gr["w_z"], gr["w_x"], gr["w_dt"][:, :SSM_HEADS]], axis=1)


def _full_grads(gr):
    g = _small_grads(gr)
    f32 = lambda t: t.astype(F32)
    g["a_in_proj"] = f32(_in_proj_grad(gr))[None]
    g["a_out_proj"] = f32(gr["a_out_proj"])[None]
    g["w_kv"] = f32(gr["w_kv"])
    g["w_q"] = f32(gr["w_q"])[None]
    g["w_o"] = f32(gr["w_o"])[None]
    f = [gr["ffn0"], gr["ffn1"]]
    g["f_w_in"] = jnp.stack([f32(f[l]["f_w_in"]).transpose(1, 0, 2).reshape(1024, -1) for l in range(2)])
    g["f_w_down"] = jnp.stack([f32(f[l]["f_w_down"]) for l in range(2)])
    return g


MESH = pl.DeviceIdType.MESH
WEIGHTS = ("a_norm", "a_in_proj", "a_conv_w", "a_conv_b", "a_dt_bias", "a_A_log", "a_D", "a_gnorm", "a_out_proj",
           "kv_norm", "w_kv", "b_kv", "k_norm", "b_norm", "w_q", "b_q", "q_norm", "sinks", "w_o", "b_o", "f_norm",
           "f_w_in", "f_conv_w", "f_conv_b", "f_w_down")
MATS = (("in_proj", "a_in_proj", 0), ("out_proj", "a_out_proj", 0), ("w_kv", "w_kv", None), ("w_q", "w_q", 0),
        ("w_o", "w_o", 0), ("f_in0", "f_w_in", 0), ("f_in1", "f_w_in", 1), ("f_down0", "f_w_down", 0),
        ("f_down1", "f_w_down", 1))
SMALL_CUT = (("a_norm", 1), ("a_conv_w", 2), ("a_conv_b", 1), ("a_gnorm", 1), ("f_conv_w", 2))
SMALL_REP = ("a_dt_bias", "a_A_log", "a_D", "kv_norm", "b_kv", "k_norm", "b_norm", "b_q", "q_norm", "sinks", "b_o",
             "f_norm", "f_conv_b")


def _coords():
    return lax.axis_index("x"), lax.axis_index("y"), lax.axis_index("c")


def _other_chips(x, y):
    return [(1 - x, y), (x, 1 - y), (1 - x, 1 - y)]


def _pack(arrs, rows_align, lanes, dtype):
    flat = jnp.concatenate([a.reshape(-1).astype(dtype) for a in arrs])
    per = rows_align * lanes
    total = -(-flat.shape[0] // per) * per
    return jnp.pad(flat, (0, total - flat.shape[0])).reshape(total // lanes, lanes)


def _unpack(flat, shapes):
    out, off = [], 0
    for s in shapes:
        n = math.prod(s)
        out.append(flat[off:off + n].reshape(s))
        off += n
    return out


def _remote(src, dst, send, recv, k, dev):
    return pltpu.make_async_remote_copy(src_ref=src, dst_ref=dst, send_sem=send.at[k], recv_sem=recv.at[k],
                                        device_id=dev, device_id_type=MESH)


_ANY = pl.BlockSpec(memory_space=pl.ANY)


def _halves(t):
    r, c = t.shape
    return t.reshape(2, r // 2, c)


def _gather_weights(shards, sp):
    n = len(shards)
    per = 9
    n_sem = per * n + 3

    def body(*refs):
        sh, sp_ref = refs[:n], refs[n]
        outs, sout = refs[n + 1:2 * n + 1], refs[2 * n + 1]
        send, recv, loc = refs[2 * n + 2:]
        x, y, c = _coords()
        me = 2 * x + y
        cx_, cy_, cd_ = _other_chips(x, y)
        ix, iy, idg = (2 * p[0] + p[1] for p in (cx_, cy_, cd_))
        to_x, to_y, sib = (*cx_, c), (*cy_, c), (x, y, 1 - c)
        l1 = pltpu.make_async_copy(sp_ref, sout.at[me], loc.at[0])
        l1.start()
        sends = [_remote(sp_ref, sout.at[me], send, recv, per * n + j, (*p, c)) for j, p in enumerate((cx_, cy_, cd_))]
        for t in range(n):
            sends.append(_remote(sh[t].at[c], outs[t].at[me, c], send, recv, per * t + 0, to_x))
            sends.append(_remote(sh[t].at[c], outs[t].at[me, c], send, recv, per * t + 1, to_y))
            sends.append(_remote(sh[t], outs[t].at[me], send, recv, per * t + 8, sib))
        for cp in sends:
            cp.start()

        def go(src, dst, k, dev):
            cp = _remote(src, dst, send, recv, k, dev)
            cp.start()
            sends.append(cp)

        def piece(t, owner, first):
            q = sh[t].shape[1] // 2
            return outs[t].at[owner, c, pl.ds(0 if first else q, q)]

        for t in range(n):
            _remote(sh[t].at[c], outs[t].at[ix, c], send, recv, per * t + 0, to_x).wait_recv()
            go(piece(t, ix, False), piece(t, ix, False), per * t + 3, to_y)
            go(outs[t].at[ix, c], outs[t].at[ix, c], per * t + 4, sib)
        for t in range(n):
            _remote(sh[t].at[c], outs[t].at[iy, c], send, recv, per * t + 1, to_y).wait_recv()
            go(piece(t, iy, True), piece(t, iy, True), per * t + 2, to_x)
            go(outs[t].at[iy, c], outs[t].at[iy, c], per * t + 5, sib)
        for t in range(n):
            _remote(piece(t, idg, True), piece(t, idg, True), send, recv, per * t + 2, to_x).wait_recv()
            go(piece(t, idg, True), piece(t, idg, True), per * t + 6, sib)
            _remote(piece(t, idg, False), piece(t, idg, False), send, recv, per * t + 3, to_y).wait_recv()
            go(piece(t, idg, False), piece(t, idg, False), per * t + 7, sib)
        for j, p in enumerate((cx_, cy_, cd_)):
            _remote(sp_ref, sout.at[2 * p[0] + p[1]], send, recv, per * n + j, (*p, c)).wait_recv()
        for t in range(n):
            q = sh[t].shape[1] // 2
            other = lambda owner, lo=None: outs[t].at[owner, 1 - c] if lo is None else outs[t].at[owner, 1 - c, pl.ds(lo, q)]
            _remote(other(ix), other(ix), send, recv, per * t + 4, sib).wait_recv()
            _remote(other(iy), other(iy), send, recv, per * t + 5, sib).wait_recv()
            _remote(other(idg, 0), other(idg, 0), send, recv, per * t + 6, sib).wait_recv()
            _remote(other(idg, q), other(idg, q), send, recv, per * t + 7, sib).wait_recv()
            _remote(sh[t], outs[t].at[me], send, recv, per * t + 8, sib).wait_recv()
        for cp in sends:
            cp.wait_send()
        l1.wait()

    res = pl.pallas_call(
        body, name="gather_weights", in_specs=[_ANY] * (n + 1), out_specs=[_ANY] * (n + 1),
        out_shape=[jax.ShapeDtypeStruct((N_CHIPS,) + t.shape, t.dtype) for t in shards]
        + [jax.ShapeDtypeStruct((N_CHIPS,) + sp.shape, sp.dtype)],
        scratch_shapes=[pltpu.SemaphoreType.DMA((n_sem,)), pltpu.SemaphoreType.DMA((n_sem,)),
                        pltpu.SemaphoreType.DMA((1,))],
    )(*shards, sp)
    return res[:n], res[n]


_HBM = pl.BlockSpec(memory_space=pltpu.HBM)
_SEMS = pl.BlockSpec(memory_space=pltpu.SEMAPHORE)
_DATAFLOW = pltpu.SideEffectType.DATAFLOW_SIDE_EFFECTING


def _in_hbm(a):
    return pltpu.with_memory_space_constraint(a, pltpu.HBM)


def _start_copies(copies, arrays, n_sem, after, *, name):
    n, na = len(arrays), len(after)

    def body(*refs):
        for mine, _ in copies(refs[:n], refs[n + na], refs[n + na + 1]):
            mine.start()
        refs[-1][...] = jnp.zeros_like(refs[-1])

    res = pl.pallas_call(
        body, name=name, in_specs=[_HBM] * n + [_ANY] * na,
        out_specs=[_SEMS, _SEMS] + [_HBM] * n + [pl.BlockSpec(memory_space=pltpu.VMEM)],
        out_shape=[pltpu.SemaphoreType.DMA((n_sem,)), pltpu.SemaphoreType.DMA((n_sem,))]
        + [pltpu.HBM(a.shape, a.dtype) for a in arrays] + [jax.ShapeDtypeStruct((8, 128), F32)],
        input_output_aliases={t: 2 + t for t in range(n)},
        compiler_params=pltpu.CompilerParams(has_side_effects=_DATAFLOW),
    )(*[_in_hbm(a) for a in arrays], *after)
    return res[0], res[1], list(res[2:2 + n]), res[-1]


def _wait_copies(copies, send, recv, arrays, after, *, name):
    n = len(arrays)

    def body(*refs):
        for mine, theirs in copies(refs[:n], refs[n], refs[n + 1]):
            mine.wait_send()
            theirs.wait_recv()

    return list(pl.pallas_call(
        body, name=name, in_specs=[_HBM] * n + [_SEMS, _SEMS] + [_ANY] * len(after), out_specs=[_HBM] * n,
        out_shape=[pltpu.HBM(a.shape, a.dtype) for a in arrays], input_output_aliases={t: t for t in range(n)},
        compiler_params=pltpu.CompilerParams(has_side_effects=_DATAFLOW),
    )(*arrays, send, recv, *after))


def _sibling_copies(refs, send, recv):
    n = len(refs) // 2
    x, y, c = _coords()
    cps = [_remote(refs[t].at[:, 1 - c], refs[n + t], send, recv, t, (x, y, 1 - c)) for t in range(n)]
    return [(cp, cp) for cp in cps]


def _join_copies(refs, send, recv):
    x, y, c = _coords()
    sib = (x, y, 1 - c)
    return [(_remote(o.at[c], o.at[c], send, recv, t, sib), _remote(o.at[1 - c], o.at[1 - c], send, recv, t, sib))
            for t, o in enumerate(refs)]


def _gather_copies(sh, land, send, recv):
    x, y, c = _coords()
    me = 2 * x + y
    out = []
    for t in range(len(sh)):
        for j, (cx, cy) in enumerate(_other_chips(x, y)):
            dev = (cx, cy, c)
            out.append((_remote(sh[t].at[c], land[t].at[me, c], send, recv, 4 * t + j, dev),
                        _remote(sh[t].at[c], land[t].at[2 * cx + cy, c], send, recv, 4 * t + j, dev)))
        sib = (x, y, 1 - c)
        out.append((_remote(sh[t], land[t].at[me], send, recv, 4 * t + 3, sib),
                    _remote(sh[t], land[t].at[me], send, recv, 4 * t + 3, sib)))
    return out


def _gather_start(shards, after, *, name):
    n = len(shards)

    def body(*refs):
        sh, land = refs[:n], refs[n:2 * n]
        send, recv = refs[2 * n + 1], refs[2 * n + 2]
        token = refs[-1]
        for mine, _ in _gather_copies(sh, land, send, recv):
            mine.start()
        token[...] = jnp.zeros_like(token)

    lands = [_in_hbm(lax.empty((N_CHIPS,) + s.shape, s.dtype)) for s in shards]
    res = pl.pallas_call(
        body, name=name, in_specs=[_HBM] * (2 * n) + [_ANY],
        out_specs=[_SEMS, _SEMS] + [_HBM] * (2 * n) + [pl.BlockSpec(memory_space=pltpu.VMEM)],
        out_shape=[pltpu.SemaphoreType.DMA((4 * n,)), pltpu.SemaphoreType.DMA((4 * n,))]
        + [pltpu.HBM(s.shape, s.dtype) for s in shards] + [pltpu.HBM(l.shape, l.dtype) for l in lands]
        + [jax.ShapeDtypeStruct((8, 128), F32)],
        input_output_aliases={t: 2 + t for t in range(2 * n)},
        compiler_params=pltpu.CompilerParams(has_side_effects=_DATAFLOW),
    )(*[_in_hbm(s) for s in shards], *lands, after)
    return res[0], res[1], res[2:2 + n], res[2 + n:2 + 2 * n], res[-1]


def _gather_wait(send, recv, shards, lands, after, *, name):
    n = len(shards)

    def body(*refs):
        sh, land = refs[:n], refs[n:2 * n]
        send_r, recv_r = refs[2 * n], refs[2 * n + 1]
        for mine, theirs in _gather_copies(sh, land, send_r, recv_r):
            mine.wait_send()
            theirs.wait_recv()

    res = pl.pallas_call(
        body, name=name, in_specs=[_HBM] * (2 * n) + [_SEMS, _SEMS, _ANY], out_specs=[_HBM] * (2 * n),
        out_shape=[pltpu.HBM(s.shape, s.dtype) for s in shards] + [pltpu.HBM(l.shape, l.dtype) for l in lands],
        input_output_aliases={t: t for t in range(2 * n)},
        compiler_params=pltpu.CompilerParams(has_side_effects=_DATAFLOW),
    )(*shards, *lands, send, recv, after)
    return res[n:]


def _forward_copies(refs, send, recv):
    x, y, c = _coords()
    sib = (x, y, 1 - c)
    srcs = [2 * cx + cy for cx, cy in _other_chips(x, y)]
    return [(_remote(o.at[s, c], o.at[s, c], send, recv, 3 * t + j, sib),
             _remote(o.at[s, 1 - c], o.at[s, 1 - c], send, recv, 3 * t + j, sib))
            for t, o in enumerate(refs) for j, s in enumerate(srcs)]


def _small_copies(v, land, send, recv):
    x, y, c = _coords()
    me = 4 * x + 2 * y + c
    out = []
    for k in range(1, 8):
        px = 1 - x if k & 4 else x
        py = 1 - y if k & 2 else y
        pc = 1 - c if k & 1 else c
        out.append((_remote(v, land.at[me], send, recv, k - 1, (px, py, pc)),
                    _remote(v, land.at[4 * px + 2 * py + pc], send, recv, k - 1, (px, py, pc))))
    return out


def _small_start(v, after, *, name):
    def body(v_ref, land_ref, after_ref, send, recv, v_thru, land_thru, token):
        for mine, _ in _small_copies(v_ref, land_ref, send, recv):
            mine.start()
        token[...] = jnp.zeros_like(token)

    land = _in_hbm(lax.empty((8,) + v.shape, v.dtype))
    return pl.pallas_call(
        body, name=name, in_specs=[_HBM, _HBM, _ANY],
        out_specs=[_SEMS, _SEMS, _HBM, _HBM, pl.BlockSpec(memory_space=pltpu.VMEM)],
        out_shape=[pltpu.SemaphoreType.DMA((7,)), pltpu.SemaphoreType.DMA((7,)), pltpu.HBM(v.shape, v.dtype),
                   pltpu.HBM(land.shape, land.dtype), jax.ShapeDtypeStruct((8, 128), F32)],
        input_output_aliases={0: 2, 1: 3}, compiler_params=pltpu.CompilerParams(has_side_effects=_DATAFLOW),
    )(_in_hbm(v), land, after)


def _small_wait(send, recv, v, land, after, *, name):
    def body(v_ref, land_ref, send_r, recv_r, *rest):
        for mine, theirs in _small_copies(v_ref, land_ref, send_r, recv_r):
            mine.wait_send()
            theirs.wait_recv()

    return pl.pallas_call(
        body, name=name, in_specs=[_HBM, _HBM, _SEMS, _SEMS] + [_ANY] * len(after), out_specs=[_HBM, _HBM],
        out_shape=[pltpu.HBM(v.shape, v.dtype), pltpu.HBM(land.shape, land.dtype)],
        input_output_aliases={0: 0, 1: 1}, compiler_params=pltpu.CompilerParams(has_side_effects=_DATAFLOW),
    )(v, land, send, recv, *after)


def _small_sum(v, land, me_idx, *, name="small_sum"):
    def body(me_ref, v_ref, land_ref, o_ref):
        acc = None
        for s in range(8):
            term = jnp.where(me_ref[0] == s, v_ref[...], land_ref[s])
            acc = term if acc is None else acc + term
        o_ref[...] = acc

    whole = lambda shape: pl.BlockSpec(shape, lambda i, me_ref: (0,) * len(shape))
    return pl.pallas_call(
        body, name=name,
        grid_spec=pltpu.PrefetchScalarGridSpec(num_scalar_prefetch=1, grid=(1,), in_specs=[whole(v.shape), whole(land.shape)],
                                               out_specs=whole(v.shape)),
        out_shape=jax.ShapeDtypeStruct(v.shape, F32), compiler_params=_cp(1),
    )(me_idx, v, land)


RS_ROW_SPLIT = 2


def _rs_add_pair(gs, as_, c_idx, *, name):
    n = len(gs)

    def body(c_ref, *refs):
        for t in range(n):
            refs[2 * n + t][...] = (refs[t][...].astype(F32) + refs[n + t][...].astype(F32)).astype(BF16)

    def gspec(g):
        _, _, rh, cols = g.shape
        return pl.BlockSpec((None, None, rh // RS_ROW_SPLIT, cols), lambda j, i, c_ref: (j, c_ref[0], i, 0))

    def pspec(g):
        _, _, rh, cols = g.shape
        return pl.BlockSpec((None, rh // RS_ROW_SPLIT, cols), lambda j, i, c_ref: (j, i, 0))

    return pl.pallas_call(
        body, name=name,
        grid_spec=pltpu.PrefetchScalarGridSpec(
            num_scalar_prefetch=1, grid=(N_CHIPS, RS_ROW_SPLIT),
            in_specs=[gspec(g) for g in gs] + [pspec(g) for g in gs], out_specs=[pspec(g) for g in gs]),
        out_shape=[jax.ShapeDtypeStruct((N_CHIPS,) + g.shape[2:], BF16) for g in gs], compiler_params=_cp(2),
    )(c_idx, *gs, *as_)


def _chips_copies(p, r, send, recv):
    x, y, c = _coords()
    return [_remote(p[t].at[2 * cx + cy], r[t].at[k], send, recv, 3 * t + k, (cx, cy, c))
            for k, (cx, cy) in enumerate(_other_chips(x, y)) for t in range(len(p))]


def _rs_chips_start(ps, after, *, name):
    n, na = len(ps), len(after)

    def body(*refs):
        p, r = refs[:n], refs[n:2 * n]
        send, recv = refs[2 * n + na], refs[2 * n + na + 1]
        token = refs[-1]
        for cp in _chips_copies(p, r, send, recv):
            cp.start()
        token[...] = jnp.zeros_like(token)

    lands = [_in_hbm(lax.empty((3,) + p.shape[1:], p.dtype)) for p in ps]
    res = pl.pallas_call(
        body, name=name, in_specs=[_HBM] * (2 * n) + [_ANY] * na,
        out_specs=[_SEMS, _SEMS] + [_HBM] * (2 * n) + [pl.BlockSpec(memory_space=pltpu.VMEM)],
        out_shape=[pltpu.SemaphoreType.DMA((3 * n,)), pltpu.SemaphoreType.DMA((3 * n,))]
        + [pltpu.HBM(p.shape, p.dtype) for p in ps] + [pltpu.HBM(l.shape, l.dtype) for l in lands]
        + [jax.ShapeDtypeStruct((8, 128), F32)],
        input_output_aliases={t: 2 + t for t in range(2 * n)},
        compiler_params=pltpu.CompilerParams(has_side_effects=_DATAFLOW),
    )(*[_in_hbm(p) for p in ps], *lands, *after)
    return res[0], res[1], res[2:2 + n], res[2 + n:2 + 2 * n], res[-1]


def _rs_chips_wait(send, recv, ps, lands, after, *, name):
    n = len(ps)

    def body(*refs):
        p, r = refs[:n], refs[n:2 * n]
        for cp in _chips_copies(p, r, refs[2 * n], refs[2 * n + 1]):
            cp.wait_send()
            cp.wait_recv()

    res = pl.pallas_call(
        body, name=name, in_specs=[_HBM] * (2 * n) + [_SEMS, _SEMS] + [_ANY] * len(after), out_specs=[_HBM] * (2 * n),
        out_shape=[pltpu.HBM(p.shape, p.dtype) for p in ps] + [pltpu.HBM(l.shape, l.dtype) for l in lands],
        input_output_aliases={t: t for t in range(2 * n)},
        compiler_params=pltpu.CompilerParams(has_side_effects=_DATAFLOW),
    )(*ps, *lands, send, recv, *after)
    return res[:n], res[n:]


def _rs_add_chips(ps, rs, idx, *, name):
    n = len(ps)

    def body(idx_ref, *refs):
        for t in range(n):
            p_ref, r0, r1, r2 = refs[4 * t:4 * t + 4]
            refs[4 * n + t][...] = ((p_ref[...].astype(F32) + r0[...].astype(F32)) + r1[...].astype(F32)) + r2[...].astype(F32)

    in_specs, args = [], []
    for p, r in zip(ps, rs):
        _, rh, cols = p.shape
        blk = (None, rh // RS_ROW_SPLIT, cols)
        in_specs.append(pl.BlockSpec(blk, lambda i, idx_ref: (idx_ref[0], i, 0)))
        in_specs += [pl.BlockSpec(blk, lambda i, idx_ref, k=k: (k, i, 0)) for k in range(3)]
        args += [p, r, r, r]
    out_specs = [pl.BlockSpec((None, p.shape[1] // RS_ROW_SPLIT, p.shape[2]), lambda i, idx_ref: (idx_ref[1], i, 0))
                 for p in ps]
    return pl.pallas_call(
        body, name=name,
        grid_spec=pltpu.PrefetchScalarGridSpec(num_scalar_prefetch=1, grid=(RS_ROW_SPLIT,), in_specs=in_specs,
                                               out_specs=out_specs),
        out_shape=[jax.ShapeDtypeStruct((2,) + p.shape[1:], F32) for p in ps], compiler_params=_cp(1),
    )(idx, *args)


def _adamw(w, gs, m, v, *, name, dep=None):
    L, Rr, C = w.shape
    tr, tc = _pick(Rr, (256, 128, 64)), C
    if tr == Rr and Rr * C > 512 * 1024:
        tc = 256
    bc1 = 1.0 - ADAM_B1 ** ADAM_STEP
    bc2 = 1.0 - ADAM_B2 ** ADAM_STEP
    nd = 0 if dep is None else 1

    def body(*refs):
        w_ref, m_ref, v_ref = refs[0], refs[1], refs[2]
        g_refs = refs[3:3 + L]
        d_ref, mo_ref, vo_ref, go_ref = refs[3 + L + nd:]
        layer = pl.program_id(0)
        gv = g_refs[0][...]
        for q in range(1, L):
            gv = jnp.where(layer == q, g_refs[q][...], gv)
        mn = ADAM_B1 * m_ref[...] + (1.0 - ADAM_B1) * gv
        vn = ADAM_B2 * v_ref[...] + (1.0 - ADAM_B2) * (gv * gv)
        go_ref[...] = gv
        mo_ref[...] = mn
        vo_ref[...] = vn
        d_ref[...] = -ADAM_LR * ((mn / bc1) / (jnp.sqrt(vn / bc2) + ADAM_EPS) + ADAM_WD * w_ref[...])

    blk = pl.BlockSpec((None, tr, tc), lambda l, i, j: (l, i, j))
    gblks = [pl.BlockSpec((tr, tc), lambda l, i, j, q=q: (jnp.where(l == q, i, 0), jnp.where(l == q, j, 0))) for q in range(L)]
    return pl.pallas_call(
        body, name=name, grid=(L, Rr // tr, C // tc), in_specs=[blk] * 3 + gblks + [_ANY] * nd, out_specs=[blk] * 4,
        out_shape=[jax.ShapeDtypeStruct((L, Rr, C), F32)] * 4, compiler_params=_cp(3),
    )(w, m, v, *gs, *([] if dep is None else [dep]))


def kernel(x, positions, a_norm, a_in_proj, a_conv_w, a_conv_b, a_dt_bias, a_A_log, a_D, a_gnorm, a_out_proj,
           kv_norm, w_kv, b_kv, k_norm, b_norm, w_q, b_q, q_norm, sinks, w_o, b_o, f_norm, f_w_in, f_conv_w,
           f_conv_b, f_w_down, loss_target, m_a_norm, m_a_in_proj, m_a_conv_w, m_a_conv_b, m_a_dt_bias, m_a_A_log,
           m_a_D, m_a_gnorm, m_a_out_proj, m_kv_norm, m_w_kv, m_b_kv, m_k_norm, m_b_norm, m_w_q, m_b_q, m_q_norm,
           m_sinks, m_w_o, m_b_o, m_f_norm, m_f_w_in, m_f_conv_w, m_f_conv_b, m_f_w_down, v_a_norm, v_a_in_proj,
           v_a_conv_w, v_a_conv_b, v_a_dt_bias, v_a_A_log, v_a_D, v_a_gnorm, v_a_out_proj, v_kv_norm, v_w_kv,
           v_b_kv, v_k_norm, v_b_norm, v_w_q, v_b_q, v_q_norm, v_sinks, v_w_o, v_b_o, v_f_norm, v_f_w_in,
           v_f_conv_w, v_f_conv_b, v_f_w_down):
    wl = dict(zip(WEIGHTS, (a_norm, a_in_proj, a_conv_w, a_conv_b, a_dt_bias, a_A_log, a_D, a_gnorm, a_out_proj,
                            kv_norm, w_kv, b_kv, k_norm, b_norm, w_q, b_q, q_norm, sinks, w_o, b_o, f_norm, f_w_in,
                            f_conv_w, f_conv_b, f_w_down)))
    ml = dict(zip(WEIGHTS, (m_a_norm, m_a_in_proj, m_a_conv_w, m_a_conv_b, m_a_dt_bias, m_a_A_log, m_a_D, m_a_gnorm,
                            m_a_out_proj, m_kv_norm, m_w_kv, m_b_kv, m_k_norm, m_b_norm, m_w_q, m_b_q, m_q_norm,
                            m_sinks, m_w_o, m_b_o, m_f_norm, m_f_w_in, m_f_conv_w, m_f_conv_b, m_f_w_down)))
    vl = dict(zip(WEIGHTS, (v_a_norm, v_a_in_proj, v_a_conv_w, v_a_conv_b, v_a_dt_bias, v_a_A_log, v_a_D, v_a_gnorm,
                            v_a_out_proj, v_kv_norm, v_w_kv, v_b_kv, v_k_norm, v_b_norm, v_w_q, v_b_q, v_q_norm,
                            v_sinks, v_w_o, v_b_o, v_f_norm, v_f_w_in, v_f_conv_w, v_f_conv_b, v_f_w_down)))
    xi, yi, ci = _coords()
    me = 2 * xi + yi
    S = x.shape[1]

    def block_of(n, layer):
        t = wl[n]
        return t if layer is None else t[layer]

    rows = lambda t: t.reshape(-1, t.shape[-1])
    c_idx = jnp.reshape(ci, (1,)).astype(jnp.int32)
    me_c = jnp.stack([me, ci]).astype(jnp.int32)
    early = ("in_proj",)
    late = (("out_proj", "f_in0", "f_down0"), ("w_kv", "w_q", "w_o", "f_in1", "f_down1"))
    shards = {name: _halves(block_of(wn, layer).astype(BF16)) for name, wn, layer in MATS}

    sp = _pack([wl[n] for n, _ in SMALL_CUT], 8, 128, F32)
    gathered, gs = _gather_weights([shards[k] for k in early], sp)
    gt = {k: t.reshape(N_CHIPS, -1, t.shape[-1]) for k, t in zip(early, gathered)}
    started = {0: _gather_start([shards[k] for k in late[0]], gs, name="gather_late_start0")}
    full = {n: wl[n] for n in SMALL_REP}
    gs = gs.reshape(N_CHIPS, -1)
    pieces = [_unpack(gs[j], [wl[n].shape for n, _ in SMALL_CUT]) for j in range(N_CHIPS)]
    for q, (n, ax) in enumerate(SMALL_CUT):
        full[n] = jnp.concatenate([pieces[j][q] for j in range(N_CHIPS)], axis=ax)
    w = _prep_small(full, {})
    w["w_zx"], w["w_dt"] = _join_in_proj(gt["in_proj"])
    w["dep"] = started[0][4]

    class Comm:
        flight = []
        reduced = {}

        forwarding = {}

        def late_start(self, part, after):
            started[part] = _gather_start([shards[k] for k in late[part]], after, name=f"gather_late_start{part}")
            return started[part][4]

        def late_arrived(self, part, after):
            send, recv, shs, lands, _ = started[part]
            lands = _gather_wait(send, recv, shs, lands, after[0], name=f"gather_late_wait{part}")
            send, recv, lands, token = _start_copies(_forward_copies, list(lands), 3 * len(lands), after,
                                                     name=f"gather_late_forward_start{part}")
            self.forwarding[part] = (send, recv, lands)
            return token

        def late_weights(self, w, after, part):
            send, recv, lands = self.forwarding[part]
            lands = _wait_copies(_forward_copies, send, recv, lands, [after], name=f"gather_late_forward_wait{part}")
            lt = {k: t.reshape(N_CHIPS, -1, t.shape[-1]) for k, t in zip(late[part], lands)}
            w = dict(w)
            if part == 0:
                w["a_out_proj"], w["f_w_in"], w["f_w_down"] = rows(lt["out_proj"]), [lt["f_in0"]], [rows(lt["f_down0"])]
            else:
                w["w_kv"], w["w_q"], w["w_o"] = (rows(lt[k]) for k in ("w_kv", "w_q", "w_o"))
                w["f_w_in"], w["f_w_down"] = w["f_w_in"] + [lt["f_in1"]], w["f_w_down"] + [rows(lt["f_down1"])]
            return w

        def advance(self, after, group=None, tensors=None):
            token = None
            for grp in list(self.flight):
                tag, n = grp["tag"], len(grp["names"])
                dep = list(after) + ([] if token is None else [token])
                if grp["stage"] == "sibling":
                    arrs = _wait_copies(_sibling_copies, grp["send"], grp["recv"], grp["arrays"], dep, name=f"rs_sibling_wait{tag}")
                    pairs = _rs_add_pair(arrs[:n], arrs[n:], c_idx, name=f"rs_add_pair{tag}")
                    send, recv, ps, lands, token = _rs_chips_start(pairs, dep, name=f"rs_chips_start{tag}")
                    grp.update(stage="chips", send=send, recv=recv, ps=ps, lands=lands)
                elif grp["stage"] == "chips":
                    ps, rs = _rs_chips_wait(grp["send"], grp["recv"], grp["ps"], grp["lands"], dep, name=f"rs_chips_wait{tag}")
                    halves = _rs_add_chips(ps, rs, me_c, name=f"rs_add_chips{tag}")
                    send, recv, arrs, token = _start_copies(_join_copies, halves, n, dep, name=f"rs_join_start{tag}")
                    grp.update(stage="join", send=send, recv=recv, arrays=arrs)
                else:
                    joined = _wait_copies(_join_copies, grp["send"], grp["recv"], grp["arrays"], dep, name=f"rs_join_wait{tag}")
                    self.reduced.update({k: rows(t) for k, t in zip(grp["names"], joined)})
                    self.flight.remove(grp)
            if group is not None:
                names = list(tensors)
                glist = [tensors[k].reshape(N_CHIPS, 2, -1, tensors[k].shape[-1]) for k in names]
                lands = [lax.empty((N_CHIPS,) + gq.shape[2:], gq.dtype) for gq in glist]
                dep = list(after) + ([] if token is None else [token])
                send, recv, arrs, token = _start_copies(_sibling_copies, glist + lands, len(names), dep,
                                                        name=f"rs_sibling_start{group}")
                self.flight.append(dict(tag=group, names=names, stage="sibling", send=send, recv=recv, arrays=arrs))
            return token

    comm = Comm()

    posf = positions.reshape(S, 1).astype(F32)
    loss_part, dx0, gr, tok = _local_step(x[0], posf, loss_target[0], w, comm)
    g = _small_grads(gr)

    small_names = [n for n, _ in SMALL_CUT] + list(SMALL_REP)
    sv = _pack([g[n] for n in small_names] + [loss_part[0:1, 0:1]], 8, 128, F32)
    s_send, s_recv, sv, s_land, s_token = _small_start(sv, tok, name="small_start")

    grads, delta, new_m, new_v = {}, {}, {}, {}

    def update(wn, dep):
        gl = [comm.reduced[name] for name, n2, _ in MATS if n2 == wn]
        shp = wl[wn].shape
        three = (len(gl),) + gl[0].shape
        flip = shp[-1] % 128 != 0
        view = (lambda t: t.reshape(three).transpose(0, 2, 1)) if flip else (lambda t: t.reshape(three))
        back = (lambda t: t.transpose(0, 2, 1).reshape(shp)) if flip else (lambda t: t.reshape(shp))
        if flip:
            gl = [t.T for t in gl]
        d, mn, vn, go = _adamw(view(wl[wn]), gl, view(ml[wn]), view(vl[wn]), name="adamw_" + wn, dep=dep)
        grads[wn], delta[wn], new_m[wn], new_v[wn] = back(go), back(d), back(mn), back(vn)
        return d

    first = [update(wn, s_token) for wn in ("w_q", "w_o", "w_kv")]
    tok = comm.advance(first)
    second = [update(wn, tok) for wn in ("f_w_in", "f_w_down", "a_out_proj")]
    comm.advance(second)
    comm.advance(second)
    update("a_in_proj", None)
    done = first + second

    sv, s_land = _small_wait(s_send, s_recv, sv, s_land, done, name="small_wait")
    sred = _small_sum(sv, s_land, jnp.reshape(2 * me + ci, (1,)).astype(jnp.int32)).reshape(-1)
    small_shapes = [g[n].shape for n in small_names] + [(1,)]
    sg = dict(zip(small_names + ["loss"], _unpack(sred, small_shapes)))
    loss = sg["loss"].reshape(())
    g_small = {}
    for n, ax in SMALL_CUT:
        size = wl[n].shape[ax]
        g_small[n] = lax.dynamic_slice_in_dim(sg[n], me * size, size, axis=ax)
    for n in SMALL_REP:
        g_small[n] = sg[n].reshape(wl[n].shape)

    pk = lambda d: _pack([d[n] for n in small_names], 8, 128, F32)[None]
    d, mn, vn, _ = _adamw(pk(wl), [pk(g_small)[0]], pk(ml), pk(vl), name="adamw_small")
    shapes = [wl[n].shape for n in small_names]
    for n, dd, mm, vv in zip(small_names, _unpack(d.reshape(-1), shapes), _unpack(mn.reshape(-1), shapes),
                             _unpack(vn.reshape(-1), shapes)):
        grads[n], delta[n], new_m[n], new_v[n] = g_small[n], dd, mm, vv

    return (loss, dx0[None], *[grads[n] for n in WEIGHTS], *[delta[n] for n in WEIGHTS],
            *[new_m[n] for n in WEIGHTS], *[new_v[n] for n in WEIGHTS])
```

```python
import math

import jax
import jax.numpy as jnp
from jax import lax
from jax.experimental import pallas as pl
from jax.experimental.pallas import tpu as pltpu

F32 = jnp.float32
BF16 = jnp.bfloat16

EPS = 1e-5
CHUNK = 256
WINDOW = 128
HEAD = 64
SSM_HEADS = 32
SSM_GROUPS = 8
SSM_STATE = 128
ATT_KV = 4
ATT_G = 4
ROPE_THETA = 10000.0
NEG = -1e30
N_CHIPS = 4
VMEM_LIMIT = 56 * 1024 * 1024

ADAM_LR, ADAM_B1, ADAM_B2, ADAM_EPS, ADAM_WD, ADAM_STEP = 0.001, 0.9, 0.999, 1e-08, 0.01, 10


def _cp(n_axes):
    return pltpu.CompilerParams(dimension_semantics=("arbitrary",) * n_axes, vmem_limit_bytes=VMEM_LIMIT)


def _pick(dim, prefs):
    for p in prefs:
        if dim % p == 0:
            return p
    return dim


def _iota(shape, dim):
    return lax.broadcasted_iota(jnp.int32, shape, dim)


def _dot(a, b, ca=1, cb=0):
    return lax.dot_general(a, b, (((ca,), (cb,)), ((), ())), preferred_element_type=F32)


def _dot3(x, ind):
    h = x.astype(BF16)
    r = x - h.astype(F32)
    m = r.astype(BF16)
    lo = (r - m.astype(F32)).astype(BF16)
    return _dot(h, ind) + _dot(m, ind) + _dot(lo, ind)


def _sigmoid(x):
    return jax.nn.sigmoid(x)


def _mm(a, b, *, name, ta=False, tb=False, bias=None, res=None, out_dtype=F32, b_koff=0, tm=None, tn=None, tk=None,
        dims=None, a_spec=None, b_spec=None, o_spec=None, o_shape=None, dep=None, more=(), target=None,
        rms=None, rms_colsum=False):
    if dims is not None:
        M, N, K = dims
    else:
        if ta:
            K, M = a.shape
        else:
            M, K = a.shape
        N = b.shape[0] if tb else b.shape[1]
    tm = tm or _pick(M, (1024, 1408, 512, 256, 128))
    tn = tn or _pick(N, (512, 1408, 256, 128))
    tk = tk or (K if K <= 2048 else _pick(K, (2048, 1408, 1024, 512)))
    assert M % tm == 0 and N % tn == 0 and K % tk == 0 and b_koff % tk == 0
    nk = K // tk
    kb0 = b_koff // tk
    has_bias, has_res = bias is not None, res is not None

    def body(*refs):
        a_ref, b_ref = refs[0], refs[1]
        pos = 2
        bias_ref = res_ref = acc_ref = None
        if has_bias:
            bias_ref = refs[pos]
            pos += 1
        if has_res:
            res_ref = refs[pos]
            pos += 1
        if dep is not None:
            pos += 1
        extra = refs[pos:pos + 2 * len(more)]
        pos += 2 * len(more)
        tgt_ref = lp_ref = rx_ref = rg_ref = rd_ref = dg_ref = cs_ref = None
        if target is not None:
            tgt_ref = refs[pos]
            pos += 1
        if rms is not None:
            rx_ref, rg_ref, rd_ref = refs[pos:pos + 3]
            pos += 3
        o_ref = refs[pos]
        pos += 1
        if target is not None:
            lp_ref = refs[pos]
            pos += 1
        if rms is not None:
            dg_ref = refs[pos]
            pos += 1
            if rms_colsum:
                cs_ref = refs[pos]
                pos += 1
        if nk > 1:
            acc_ref = refs[pos]
        part = _dot(a_ref[...].astype(BF16), b_ref[...].astype(BF16), 0 if ta else 1, 1 if tb else 0)
        for q in range(len(more)):
            part = part + _dot(extra[2 * q][...].astype(BF16), extra[2 * q + 1][...].astype(BF16),
                               0 if ta else 1, 1 if tb else 0)

        def finish(acc):
            if has_bias:
                acc = acc + bias_ref[...]
            if has_res:
                acc = acc + res_ref[...]
            if target is not None:
                err = acc - tgt_ref[...]
                acc = err * (1.0 / N)
                part_loss = jnp.sum(jnp.sum(err * err, axis=1, keepdims=True), axis=0, keepdims=True) * (0.5 / N)
                first = (pl.program_id(0) == 0) & (pl.program_id(1) == 0)

                @pl.when(first)
                def _():
                    lp_ref[...] = jnp.broadcast_to(part_loss, lp_ref.shape)

                @pl.when(jnp.logical_not(first))
                def _():
                    lp_ref[...] += jnp.broadcast_to(part_loss, lp_ref.shape)

            if rms is not None:
                xv = rx_ref[...]
                r = lax.rsqrt(jnp.mean(xv * xv, axis=-1, keepdims=True) + EPS)
                xh = xv * r
                dxh = acc * rg_ref[...]
                dg_part = jnp.sum(acc * xh, axis=0, keepdims=True)
                acc = rd_ref[...] + r * (dxh - xh * jnp.mean(dxh * xh, axis=-1, keepdims=True))
                cs_part = jnp.sum(acc, axis=0, keepdims=True) if rms_colsum else None
                first_rows = pl.program_id(0) == 0

                @pl.when(first_rows)
                def _():
                    dg_ref[...] = dg_part
                    if rms_colsum:
                        cs_ref[...] = cs_part

                @pl.when(jnp.logical_not(first_rows))
                def _():
                    dg_ref[...] += dg_part
                    if rms_colsum:
                        cs_ref[...] += cs_part

            o_ref[...] = acc.astype(out_dtype)

        if nk == 1:
            finish(part)
        else:
            k = pl.program_id(2)

            @pl.when(k == 0)
            def _():
                acc_ref[...] = part

            @pl.when(k > 0)
            def _():
                acc_ref[...] += part

            @pl.when(k == nk - 1)
            def _():
                finish(acc_ref[...])

    if a_spec is None:
        a_spec = pl.BlockSpec((tk, tm), lambda i, j, k: (k, i)) if ta else pl.BlockSpec((tm, tk), lambda i, j, k: (i, k))
    if b_spec is None:
        b_spec = (pl.BlockSpec((tn, tk), lambda i, j, k: (j, k + kb0)) if tb
                  else pl.BlockSpec((tk, tn), lambda i, j, k: (k + kb0, j)))
    if o_spec is None:
        o_spec = pl.BlockSpec((tm, tn), lambda i, j, k: (i, j))
    in_specs, args = [a_spec, b_spec], [a, b]
    if has_bias:
        in_specs.append(pl.BlockSpec((1, tn), lambda i, j, k: (0, j)))
        args.append(bias)
    if has_res:
        in_specs.append(pl.BlockSpec((tm, tn), lambda i, j, k: (i, j)))
        args.append(res)
    if dep is not None:
        in_specs.append(pl.BlockSpec(memory_space=pl.ANY))
        args.append(dep)
    for piece in more:
        a2, sa, b2, sb = piece if len(piece) == 4 else (a, piece[0], b, piece[1])
        in_specs += [sa, sb]
        args += [a2, b2]
    out_specs, out_shape = [o_spec], [jax.ShapeDtypeStruct(o_shape or (M, N), out_dtype)]
    if target is not None:
        in_specs.append(pl.BlockSpec((tm, tn), lambda i, j, k: (i, j)))
        args.append(target)
        out_specs.append(pl.BlockSpec((8, 128), lambda i, j, k: (0, 0)))
        out_shape.append(jax.ShapeDtypeStruct((8, 128), F32))
    if rms is not None:
        assert tn == N and nk == 1
        row, vec = pl.BlockSpec((tm, N), lambda i, j, k: (i, 0)), pl.BlockSpec((1, N), lambda i, j, k: (0, 0))
        in_specs += [row, vec, row]
        args += list(rms)
        out_specs += [vec] * (2 if rms_colsum else 1)
        out_shape += [jax.ShapeDtypeStruct((1, N), F32)] * (2 if rms_colsum else 1)
    if len(out_specs) == 1:
        out_specs, out_shape = out_specs[0], out_shape[0]
    return pl.pallas_call(
        body, name=name, grid=(M // tm, N // tn, nk), in_specs=in_specs, out_specs=out_specs, out_shape=out_shape,
        scratch_shapes=[pltpu.VMEM((tm, tn), F32)] if nk > 1 else [],
        compiler_params=_cp(3),
    )(*args)


def _norm_mm(x, gain, b, *, name, bias=None, N=None, tn=None, b_spec=None, dep=None):
    M, K = x.shape
    N = N or b.shape[1]
    tm = _pick(M, (1024, 512, 256))
    tn = tn or _pick(N, (512, 1408, 256, 128))
    has_bias = bias is not None

    def body(*refs):
        x_ref, g_ref, b_ref = refs[:3]
        pos = 3 + (1 if has_bias else 0) + (0 if dep is None else 1)
        o_ref, h_ref = refs[pos], refs[pos + 1]

        @pl.when(pl.program_id(1) == 0)
        def _():
            xv = x_ref[...]
            h_ref[...] = (xv * lax.rsqrt(jnp.mean(xv * xv, axis=-1, keepdims=True) + EPS) * g_ref[...]).astype(BF16)

        acc = _dot(h_ref[...], b_ref[...].astype(BF16))
        if has_bias:
            acc = acc + refs[3][...]
        o_ref[...] = acc

    in_specs = [pl.BlockSpec((tm, K), lambda i, j: (i, 0)), pl.BlockSpec((1, K), lambda i, j: (0, 0)),
                b_spec or pl.BlockSpec((K, tn), lambda i, j: (0, j))]
    args = [x, gain, b]
    if has_bias:
        in_specs.append(pl.BlockSpec((1, tn), lambda i, j: (0, j)))
        args.append(bias)
    if dep is not None:
        in_specs.append(pl.BlockSpec(memory_space=pl.ANY))
        args.append(dep)
    return pl.pallas_call(
        body, name=name, grid=(M // tm, N // tn), in_specs=in_specs,
        out_specs=[pl.BlockSpec((tm, tn), lambda i, j: (i, j)), pl.BlockSpec((tm, K), lambda i, j: (i, 0))],
        out_shape=[jax.ShapeDtypeStruct((M, N), F32), jax.ShapeDtypeStruct((M, K), BF16)], compiler_params=_cp(2),
    )(*args)


def _rms_bwd(x, gains, dhs, dres, *, name, tr=256, want_colsum=False):
    S, D = x.shape
    n = len(gains)
    steps = S // tr

    def body(*refs):
        x_ref = refs[0]
        g_refs = refs[1:1 + n]
        dh_refs = refs[1 + n:1 + 2 * n]
        dres_ref = refs[1 + 2 * n]
        dx_ref = refs[2 + 2 * n]
        dg_refs = refs[3 + 2 * n:3 + 3 * n]
        cs_ref = refs[3 + 3 * n] if want_colsum else None
        i = pl.program_id(0)
        xv = x_ref[...]
        r = lax.rsqrt(jnp.mean(xv * xv, axis=-1, keepdims=True) + EPS)
        xh = xv * r
        dx = dres_ref[...]
        for q in range(n):
            dh = dh_refs[q][...]
            dxh = dh * g_refs[q][...]
            dx = dx + r * (dxh - xh * jnp.mean(dxh * xh, axis=-1, keepdims=True))
            part = jnp.sum(dh * xh, axis=0, keepdims=True)

            @pl.when(i == 0)
            def _():
                dg_refs[q][...] = part

            @pl.when(i > 0)
            def _():
                dg_refs[q][...] += part

        dx_ref[...] = dx
        if want_colsum:
            cpart = jnp.sum(dx, axis=0, keepdims=True)

            @pl.when(i == 0)
            def _():
                cs_ref[...] = cpart

            @pl.when(i > 0)
            def _():
                cs_ref[...] += cpart

    row = pl.BlockSpec((tr, D), lambda i: (i, 0))
    vec = pl.BlockSpec((1, D), lambda i: (0, 0))
    n_vec_out = n + (1 if want_colsum else 0)
    outs = pl.pallas_call(
        body, name=name, grid=(steps,), in_specs=[row] + [vec] * n + [row] * n + [row],
        out_specs=[row] + [vec] * n_vec_out,
        out_shape=[jax.ShapeDtypeStruct((S, D), F32)] + [jax.ShapeDtypeStruct((1, D), F32)] * n_vec_out,
        compiler_params=_cp(1),
    )(x, *gains, *dhs, dres)
    return outs


def _colsum(x, *, name, tr=256):
    S, D = x.shape

    def body(x_ref, o_ref):
        i = pl.program_id(0)
        part = jnp.sum(x_ref[...].astype(F32), axis=0, keepdims=True)

        @pl.when(i == 0)
        def _():
            o_ref[...] = part

        @pl.when(i > 0)
        def _():
            o_ref[...] += part

    return pl.pallas_call(
        body, name=name, grid=(S // tr,), in_specs=[pl.BlockSpec((tr, D), lambda i: (i, 0))],
        out_specs=pl.BlockSpec((1, D), lambda i: (0, 0)), out_shape=jax.ShapeDtypeStruct((1, D), F32),
        compiler_params=_cp(1),
    )(x)


STRIP = 64
HALO = 8


def _strips(S, tc):
    return [(r0, slice(l0, l0 + 128)) for l0 in range(0, tc, 128) for r0 in range(S - STRIP, -1, -STRIP)]


def _with_halo(ref, r0, ls):
    if r0 == 0:
        return jnp.concatenate([jnp.zeros((HALO, 128), F32), ref[0:STRIP, ls]], axis=0)
    return ref[r0 - HALO:r0 + STRIP, ls]


def _conv_strip(xw, w_ref, b_ref, ls, width):
    acc = b_ref[:, ls] + w_ref[pl.ds(width - 1, 1), ls] * xw[HALO:]
    shifted = []
    for s in range(1, width):
        xs = pltpu.roll(xw, s, axis=0)[HALO:]
        shifted.append(xs)
        acc = acc + w_ref[pl.ds(width - 1 - s, 1), ls] * xs
    return acc, shifted


def _conv_strip_back(dacc, after, xc, shifted, w_ref, ls, width):
    ext = jnp.concatenate([dacc, after], axis=0)
    dx = w_ref[pl.ds(width - 1, 1), ls] * dacc
    dws = [None] * width
    dws[width - 1] = jnp.sum(dacc * xc, axis=0, keepdims=True)
    for s in range(1, width):
        dx = dx + w_ref[pl.ds(width - 1 - s, 1), ls] * pltpu.roll(ext, STRIP + HALO - s, axis=0)[:STRIP]
        dws[width - 1 - s] = jnp.sum(dacc * shifted[s - 1], axis=0, keepdims=True)
    return dx, dws, jnp.sum(dacc, axis=0, keepdims=True)


def _conv_back_block(S, tc, width, w_ref, b_ref, x_ref, dacc_of, dx_store, dw_ref, db_ref):
    for l0 in range(0, tc, 128):
        ls = slice(l0, l0 + 128)
        after = jnp.zeros((HALO, 128), F32)
        tot = None
        for r0 in range(S - STRIP, -1, -STRIP):
            xw = _with_halo(x_ref, r0, ls)
            acc, shifted = _conv_strip(xw, w_ref, b_ref, ls, width)
            dacc = dacc_of(r0, ls, acc, _sigmoid(acc))
            dx, dws, db = _conv_strip_back(dacc, after, xw[HALO:], shifted, w_ref, ls, width)
            dx_store(r0, ls, dx)
            after = dacc[:HALO]
            part = dws + [db]
            tot = part if tot is None else [p + q for p, q in zip(tot, part)]
        for k in range(width):
            dw_ref[pl.ds(k, 1), ls] = tot[k]
        db_ref[:, ls] = tot[width]


def _conv_silu_fwd(xin, col0, C, w, b, *, name, tc=512):
    S = xin.shape[0]
    width = w.shape[0]
    off = col0 // tc

    def body(x_ref, w_ref, b_ref, o_ref):
        for r0, ls in _strips(S, tc):
            acc, _ = _conv_strip(_with_halo(x_ref, r0, ls), w_ref, b_ref, ls, width)
            o_ref[r0:r0 + STRIP, ls] = acc * _sigmoid(acc)

    return pl.pallas_call(
        body, name=name, grid=(C // tc,),
        in_specs=[pl.BlockSpec((S, tc), lambda j: (0, j + off)), pl.BlockSpec((width, tc), lambda j: (0, j)),
                  pl.BlockSpec((1, tc), lambda j: (0, j))],
        out_specs=pl.BlockSpec((S, tc), lambda j: (0, j)), out_shape=jax.ShapeDtypeStruct((S, C), F32),
        compiler_params=_cp(1),
    )(xin, w, b)


def _conv_silu_bwd(xin, col0, C, w, b, douts, *, name, tc=256):
    S = xin.shape[0]
    width = w.shape[0]
    off = col0 // tc
    nd = len(douts)
    ranges = [(o // tc, (o + d.shape[1]) // tc) for d, o in douts]

    def body(*refs):
        x_ref, w_ref, b_ref = refs[0], refs[1], refs[2]
        d_refs = refs[3:3 + nd]
        dx_ref, dw_ref, db_ref = refs[3 + nd], refs[4 + nd], refs[5 + nd]
        j = pl.program_id(0)

        def dacc_of(r0, ls, acc, sg):
            dout = jnp.zeros((STRIP, 128), F32)
            for q in range(nd):
                lo, hi = ranges[q]
                dout = dout + jnp.where((j >= lo) & (j < hi), d_refs[q][r0:r0 + STRIP, ls], 0.0)
            return dout * (sg * (1.0 + acc * (1.0 - sg)))

        def dx_store(r0, ls, dx):
            dx_ref[r0:r0 + STRIP, ls] = dx.astype(BF16)

        _conv_back_block(S, tc, width, w_ref, b_ref, x_ref, dacc_of, dx_store, dw_ref, db_ref)

    d_specs = [pl.BlockSpec((S, tc), (lambda j, lo=lo, hi=hi: (0, jnp.clip(j - lo, 0, hi - lo - 1)))) for lo, hi in ranges]
    return pl.pallas_call(
        body, name=name, grid=(C // tc,),
        in_specs=[pl.BlockSpec((S, tc), lambda j: (0, j + off)), pl.BlockSpec((width, tc), lambda j: (0, j)),
                  pl.BlockSpec((1, tc), lambda j: (0, j))] + d_specs,
        out_specs=[pl.BlockSpec((S, tc), lambda j: (0, j)), pl.BlockSpec((width, tc), lambda j: (0, j)),
                   pl.BlockSpec((1, tc), lambda j: (0, j))],
        out_shape=[jax.ShapeDtypeStruct((S, C), BF16), jax.ShapeDtypeStruct((width, C), F32),
                   jax.ShapeDtypeStruct((1, C), F32)],
        compiler_params=_cp(1),
    )(xin, w, b, *[d for d, _ in douts])


def _ffn_act_fwd(u, w, b, *, name, tc=256):
    S, F2 = u.shape
    Fd = F2 // 2
    width = w.shape[0]
    nb = Fd // tc

    def body(g_ref, v_ref, w_ref, b_ref, o_ref):
        for r0, ls in _strips(S, tc):
            acc, _ = _conv_strip(_with_halo(g_ref, r0, ls), w_ref, b_ref, ls, width)
            o_ref[r0:r0 + STRIP, ls] = (acc * _sigmoid(acc) * v_ref[r0:r0 + STRIP, ls]).astype(BF16)

    return pl.pallas_call(
        body, name=name, grid=(nb,),
        in_specs=[pl.BlockSpec((S, tc), lambda j: (0, j)), pl.BlockSpec((S, tc), lambda j: (0, j + nb)),
                  pl.BlockSpec((width, tc), lambda j: (0, j)), pl.BlockSpec((1, tc), lambda j: (0, j))],
        out_specs=pl.BlockSpec((S, tc), lambda j: (0, j)), out_shape=jax.ShapeDtypeStruct((S, Fd), BF16),
        compiler_params=_cp(1),
    )(u, u, w, b)


def _ffn_act_bwd(u, w, b, da, *, name, tc=256):
    S, F2 = u.shape
    Fd = F2 // 2
    width = w.shape[0]
    nb = Fd // tc

    def body(g_ref, v_ref, w_ref, b_ref, da_ref, du_ref, dw_ref, db_ref, a_ref):
        def dacc_of(r0, ls, acc, sg):
            rs = slice(r0, r0 + STRIP)
            dav, val, silu = da_ref[rs, ls], v_ref[rs, ls], acc * sg
            a_ref[rs, ls] = (silu * val).astype(BF16)
            du_ref[1, rs, ls] = (dav * silu).astype(BF16)
            return dav * val * (sg * (1.0 + acc * (1.0 - sg)))

        def dx_store(r0, ls, dx):
            du_ref[0, r0:r0 + STRIP, ls] = dx.astype(BF16)

        _conv_back_block(S, tc, width, w_ref, b_ref, g_ref, dacc_of, dx_store, dw_ref, db_ref)

    blk = pl.BlockSpec((S, tc), lambda j: (0, j))
    return pl.pallas_call(
        body, name=name, grid=(nb,),
        in_specs=[blk, pl.BlockSpec((S, tc), lambda j: (0, j + nb)), pl.BlockSpec((width, tc), lambda j: (0, j)),
                  pl.BlockSpec((1, tc), lambda j: (0, j)), blk],
        out_specs=[pl.BlockSpec((2, S, tc), lambda j: (0, 0, j)), pl.BlockSpec((width, tc), lambda j: (0, j)),
                   pl.BlockSpec((1, tc), lambda j: (0, j)), blk],
        out_shape=[jax.ShapeDtypeStruct((2, S, Fd), BF16),
                   jax.ShapeDtypeStruct((width, Fd), F32), jax.ShapeDtypeStruct((1, Fd), F32),
                   jax.ShapeDtypeStruct((S, Fd), BF16)],
        compiler_params=_cp(1),
    )(u, u, w, b, da)


def _ssd_prep(dtr, dt_bias, a_log, *, name="ssd_prep"):
    S = dtr.shape[0]

    def body(d_ref, b_ref, al_ref, dt_ref, ac_ref, sg_ref, act_ref):
        lane = _iota((CHUNK, 128), 1)
        valid = lane < SSM_HEADS
        z = d_ref[...] + b_ref[...]
        dt = jnp.where(valid, jnp.maximum(z, 0.0) + jnp.log(1.0 + jnp.exp(-jnp.abs(z))), 0.0)
        a = dt * (-jnp.exp(al_ref[...]))
        row = _iota((CHUNK, 128), 0)
        k = 1
        while k < CHUNK:
            a = a + jnp.where(row >= k, pltpu.roll(a, k, axis=0), 0.0)
            k *= 2
        sg = jnp.where(valid, _sigmoid(z), 0.0)
        for arr, ref in ((dt, dt_ref), (a, ac_ref), (sg, sg_ref)):
            for g in range(SSM_GROUPS):
                ref[g] = jnp.where(lane < 4, arr if g == 0 else pltpu.roll(arr, 128 - 4 * g, axis=1), 0.0)
        act_ref[...] = a.T[:SSM_HEADS, :]

    blk = pl.BlockSpec((CHUNK, 128), lambda i: (i, 0))
    vec = pl.BlockSpec((1, 128), lambda i: (0, 0))
    grp = pl.BlockSpec((SSM_GROUPS, CHUNK, 128), lambda i: (0, i, 0))
    return pl.pallas_call(
        body, name=name, grid=(S // CHUNK,), in_specs=[blk, vec, vec],
        out_specs=[grp, grp, grp, pl.BlockSpec((SSM_HEADS, CHUNK), lambda i: (0, i))],
        out_shape=[jax.ShapeDtypeStruct((SSM_GROUPS, S, 128), F32)] * 3 + [jax.ShapeDtypeStruct((SSM_HEADS, S), F32)],
        compiler_params=_cp(1),
    )(dtr, dt_bias, a_log)


SSD_GPS = 4


def _expand4(v, lanes):
    out = jnp.broadcast_to(v[:, 3:4], lanes.shape)
    for hh in (2, 1, 0):
        out = jnp.where(lanes < 64 * (hh + 1), v[:, hh:hh + 1], out)
    return out


def _ssd_fwd(xbc, dt_g, ac_g, ac_t, *, name="ssd_fwd", dep=None):
    S = xbc.shape[0]
    nc = S // CHUNK
    Lc = CHUNK

    def body(x_ref, b_ref, c_ref, dt_ref, ac_ref, act_ref, *rest):
        y_ref, st_out_ref, st_ref = rest[-3:]
        g2 = pl.program_id(0)
        c = pl.program_id(1)

        @pl.when(c == 0)
        def _():
            st_ref[...] = jnp.zeros_like(st_ref)

        causal = _iota((Lc, Lc), 0) >= _iota((Lc, Lc), 1)
        lane256 = _iota((Lc, 256), 1)
        lane128 = _iota((Lc, 128), 1)
        row128 = _iota((128, 128), 0)
        for gg in range(SSD_GPS):
            g = SSD_GPS * g2 + gg
            bv = b_ref[:, 128 * gg:128 * (gg + 1)]
            cbf = c_ref[:, 128 * gg:128 * (gg + 1)].astype(BF16)
            cb = _dot(cbf, bv.astype(BF16), 1, 1)
            dtg, acg = dt_ref[gg], ac_ref[gg]
            ac_last = ac_ref[gg, pl.ds(Lc - 1, 1), :]
            dt4 = _expand4(dtg, lane256)
            ac4 = _expand4(acg, lane256)
            e4 = jnp.exp(ac4)
            xdb = (x_ref[:, 256 * gg:256 * (gg + 1)] * dt4).astype(BF16)
            st_out_ref[gg] = st_ref[gg]
            for p in range(2):
                xd_p = xdb[:, 128 * p:128 * (p + 1)]
                st_p = st_ref[gg, p]
                ys, sn, cds = [], [], []
                for q in range(2):
                    hh = 2 * p + q
                    a_col = acg[:, hh:hh + 1]
                    a_row = act_ref[pl.ds(4 * g + hh, 1), :]
                    dec = jnp.exp(jnp.where(causal, a_col - a_row, NEG))
                    w = (cb * dec).astype(BF16)
                    ys.append(_dot(w, xd_p))
                    al = ac_last[:, hh:hh + 1]
                    dte = jnp.exp(al - a_col)
                    sn.append(_dot(xd_p, (bv * dte).astype(BF16), 0, 0))
                    cds.append(jnp.exp(al))
                y_diag = jnp.where(lane128 < 64, ys[0], ys[1])
                y_off = _dot(cbf, st_p.astype(BF16), 1, 1) * e4[:, 128 * p:128 * (p + 1)]
                y_ref[:, 256 * gg + 128 * p:256 * gg + 128 * (p + 1)] = y_diag + y_off
                st_ref[gg, p] = jnp.where(row128 < 64, st_p * cds[0] + sn[0], st_p * cds[1] + sn[1])

    G = SSD_GPS
    per_g = lambda g, c: (g, c, 0)
    return pl.pallas_call(
        body, name=name, grid=(SSM_GROUPS // G, nc),
        in_specs=[pl.BlockSpec((Lc, 256 * G), lambda g, c: (c, g)),
                  pl.BlockSpec((Lc, 128 * G), lambda g, c: (c, 16 // G + g)),
                  pl.BlockSpec((Lc, 128 * G), lambda g, c: (c, 24 // G + g)),
                  pl.BlockSpec((G, Lc, 128), per_g), pl.BlockSpec((G, Lc, 128), per_g),
                  pl.BlockSpec((SSM_HEADS, Lc), lambda g, c: (0, c))] + ([] if dep is None else [pl.BlockSpec(memory_space=pl.ANY)]),
        out_specs=[pl.BlockSpec((Lc, 256 * G), lambda g, c: (c, g)),
                   pl.BlockSpec((G, None, 2, 128, 128), lambda g, c: (g, c, 0, 0, 0))],
        out_shape=[jax.ShapeDtypeStruct((S, 2048), F32), jax.ShapeDtypeStruct((SSM_GROUPS, nc, 2, 128, 128), F32)],
        scratch_shapes=[pltpu.VMEM((G, 2, 128, 128), F32)], compiler_params=_cp(2),
    )(xbc, xbc, xbc, dt_g, ac_g, ac_t, *([] if dep is None else [dep]))


def _ssd_bwd(xbc, dt_g, ac_g, ac_t, states, dy, dexp, *, name="ssd_bwd", dep=None):
    S = xbc.shape[0]
    nc = S // CHUNK
    Lc = CHUNK

    def body(x_ref, b_ref, c_ref, dt_ref, ac_ref, act_ref, st_ref, dy_ref, d_ref, *rest):
        dx_ref, db_ref, dc_ref, dh_ref, ds_ref = rest[-5:]
        g2 = pl.program_id(0)
        cc = pl.program_id(1)

        @pl.when(cc == 0)
        def _():
            ds_ref[...] = jnp.zeros_like(ds_ref)

        causal = _iota((Lc, Lc), 0) >= _iota((Lc, Lc), 1)
        lane256 = _iota((Lc, 256), 1)
        lane128 = _iota((Lc, 128), 1)
        row128 = _iota((128, 128), 0)
        ind_rows = _iota((256, 128), 0) >> 6
        ind_cols = _iota((256, 128), 1)
        ind_a = (ind_rows == ind_cols).astype(BF16)
        ind_b = (ind_rows + 4 == ind_cols).astype(BF16)
        for gg in range(SSD_GPS):
            g = SSD_GPS * g2 + gg
            bv = b_ref[:, 128 * gg:128 * (gg + 1)]
            cv = c_ref[:, 128 * gg:128 * (gg + 1)]
            bbf, cbf = bv.astype(BF16), cv.astype(BF16)
            cb = _dot(cbf, bbf, 1, 1)
            dtg, acg = dt_ref[gg], ac_ref[gg]
            ac_last = ac_ref[gg, pl.ds(Lc - 1, 1), :]
            dt4 = _expand4(dtg, lane256)
            ac4 = _expand4(acg, lane256)
            acl4 = _expand4(ac_last, _iota((1, 256), 1))
            e4 = jnp.exp(ac4)
            dte4 = jnp.exp(acl4 - ac4)
            xv = x_ref[:, 256 * gg:256 * (gg + 1)]
            xd = xv * dt4
            xdb = xd.astype(BF16)
            dyv = dy_ref[:, 256 * gg:256 * (gg + 1)]
            dcb = jnp.zeros((Lc, Lc), F32)
            dc_acc = jnp.zeros((Lc, 128), F32)
            db_acc = jnp.zeros((Lc, 128), F32)
            u_parts, dxd_parts, ends = [], [], []
            for p in range(2):
                sl = slice(128 * p, 128 * (p + 1))
                xd_p, xdb_p, dy_p = xd[:, sl], xdb[:, sl], dyv[:, sl]
                dyb_p = dy_p.astype(BF16)
                e_p, dte_p = e4[:, sl], dte4[:, sl]
                sp = st_ref[gg, p]
                spb = sp.astype(BF16)
                dsn = ds_ref[gg, p]
                dsnb = dsn.astype(BF16)
                yds, dxds, cds = [], [], []
                for q in range(2):
                    hh = 2 * p + q
                    a_col = acg[:, hh:hh + 1]
                    a_row = act_ref[pl.ds(4 * g + hh, 1), :]
                    dec = jnp.exp(jnp.where(causal, a_col - a_row, NEG))
                    w = (cb * dec).astype(BF16)
                    head = (lane128 < 64) if q == 0 else (lane128 >= 64)
                    dym = jnp.where(head, dyb_p, jnp.zeros_like(dyb_p))
                    dw = _dot(dym, xdb_p, 1, 1)
                    dcb = dcb + dw * dec
                    yds.append(_dot(w, xdb_p))
                    dxds.append(_dot(w, dyb_p, 0, 0))
                    cds.append(jnp.exp(ac_last[:, hh:hh + 1]))
                y_diag = jnp.where(lane128 < 64, yds[0], yds[1])
                dxd_diag = jnp.where(lane128 < 64, dxds[0], dxds[1])
                y_off = _dot(cbf, spb, 1, 1) * e_p
                dgp = dy_p * e_p
                dgb = dgp.astype(BF16)
                dc_acc = dc_acc + _dot(dgb, spb)
                dsp = _dot(dgb, cbf, 0, 0)
                cd_col = jnp.where(row128[:, 0:1] < 64, cds[0], cds[1])
                qm = _dot(bbf, dsnb, 1, 1)
                dxd_state = dte_p * qm
                db_acc = db_acc + _dot((xd_p * dte_p).astype(BF16), dsnb)
                t_p = xd_p * dxd_state
                prod = dsn * sp
                e0 = jnp.sum(jnp.sum(jnp.where(row128 < 64, prod, 0.0), axis=1, keepdims=True), axis=0, keepdims=True)
                e1 = jnp.sum(jnp.sum(jnp.where(row128 >= 64, prod, 0.0), axis=1, keepdims=True), axis=0, keepdims=True)
                tcol = jnp.sum(t_p, axis=0, keepdims=True)
                lane1 = _iota((1, 128), 1)
                t0 = jnp.sum(jnp.where(lane1 < 64, tcol, 0.0), axis=1, keepdims=True)
                t1 = jnp.sum(jnp.where(lane1 >= 64, tcol, 0.0), axis=1, keepdims=True)
                ends.append(e0 * cds[0] + t0)
                ends.append(e1 * cds[1] + t1)
                ds_ref[gg, p] = dsn * cd_col + dsp
                u_parts.append(dyb_p.astype(F32) * y_diag - xdb_p.astype(F32) * dxd_diag + dy_p * y_off - t_p)
                dxd_parts.append(dxd_diag + dxd_state)
            dxd = jnp.concatenate(dxd_parts, axis=1)
            u_all = jnp.concatenate(u_parts, axis=1)
            dx_ref[:, 256 * gg:256 * (gg + 1)] = dxd * dt4 + dyv * d_ref[:, 256 * gg:256 * (gg + 1)]
            dcbb = dcb.astype(BF16)
            dc_ref[:, 128 * gg:128 * (gg + 1)] = dc_acc + _dot(dcbb, bbf)
            db_ref[:, 128 * gg:128 * (gg + 1)] = db_acc + _dot(dcbb, cbf, 0, 0)
            lane = _iota((Lc, 128), 1)
            endv = jnp.zeros((Lc, 128), F32)
            for hh in range(4):
                endv = jnp.where(lane == 8 + hh, ends[hh], endv)
            dh_ref[gg] = _dot3(dxd * xv, ind_a) + _dot3(u_all, ind_b) + endv

    G = SSD_GPS
    rev = lambda c: nc - 1 - c
    per_g = lambda g, c: (g, rev(c), 0)
    return pl.pallas_call(
        body, name=name, grid=(SSM_GROUPS // G, nc),
        in_specs=[pl.BlockSpec((Lc, 256 * G), lambda g, c: (rev(c), g)),
                  pl.BlockSpec((Lc, 128 * G), lambda g, c: (rev(c), 16 // G + g)),
                  pl.BlockSpec((Lc, 128 * G), lambda g, c: (rev(c), 24 // G + g)),
                  pl.BlockSpec((G, Lc, 128), per_g), pl.BlockSpec((G, Lc, 128), per_g),
                  pl.BlockSpec((SSM_HEADS, Lc), lambda g, c: (0, rev(c))),
                  pl.BlockSpec((G, None, 2, 128, 128), lambda g, c: (g, rev(c), 0, 0, 0)),
                  pl.BlockSpec((Lc, 256 * G), lambda g, c: (rev(c), g)),
                  pl.BlockSpec((1, 256 * G), lambda g, c: (0, g))] + ([] if dep is None else [pl.BlockSpec(memory_space=pl.ANY)]),
        out_specs=[pl.BlockSpec((Lc, 256 * G), lambda g, c: (rev(c), g)),
                   pl.BlockSpec((Lc, 128 * G), lambda g, c: (rev(c), g)),
                   pl.BlockSpec((Lc, 128 * G), lambda g, c: (rev(c), g)),
                   pl.BlockSpec((G, Lc, 128), per_g)],
        out_shape=[jax.ShapeDtypeStruct((S, 2048), F32), jax.ShapeDtypeStruct((S, 1024), F32),
                   jax.ShapeDtypeStruct((S, 1024), F32), jax.ShapeDtypeStruct((SSM_GROUPS, S, 128), F32)],
        scratch_shapes=[pltpu.VMEM((G, 2, 128, 128), F32)], compiler_params=_cp(2),
    )(xbc, xbc, xbc, dt_g, ac_g, ac_t, states, dy, dexp, *([] if dep is None else [dep]))


def _ssd_post(dhead, dt_g, sg_g, alog_g, *, name="ssd_post"):
    S = dhead.shape[1]
    nc = S // CHUNK
    Lc = CHUNK

    def body(dh_ref, dt_ref, sg_ref, al_ref, o_ref, s_ref):
        @pl.when(pl.program_id(0) == 0)
        def _():
            s_ref[...] = jnp.zeros_like(s_ref)

        lane = _iota((Lc, 128), 1)
        row = _iota((Lc, 128), 0)
        row8 = _iota((8, 128), 0)
        out = jnp.zeros((Lc, 128), F32)
        for g in range(SSM_GROUPS):
            dh = dh_ref[g]
            a_neg = -jnp.exp(al_ref[g])
            dac = jnp.where(lane < 4, pltpu.roll(dh, 124, axis=1), 0.0)
            end = jnp.where(lane < 4, pltpu.roll(dh, 120, axis=1), 0.0)
            k = 1
            while k < Lc:
                dac = dac + jnp.where(row < Lc - k, pltpu.roll(dac, Lc - k, axis=0), 0.0)
                k *= 2
            da = dac + end
            ddt = jnp.where(lane < 4, da * a_neg + dh, 0.0)
            ddtr = ddt * sg_ref[g]
            out = out + (ddtr if g == 0 else pltpu.roll(ddtr, 4 * g, axis=1))
            dal = jnp.sum(da * dt_ref[g], axis=0, keepdims=True) * a_neg
            dbias = jnp.sum(ddtr, axis=0, keepdims=True)
            part = jnp.where(row8 == 0, dal, jnp.where(row8 == 1, dbias, 0.0))
            s_ref[g] += part
        o_ref[...] = out.astype(BF16)

    grp = pl.BlockSpec((SSM_GROUPS, Lc, 128), lambda c: (0, c, 0))
    whole = lambda r: pl.BlockSpec((SSM_GROUPS, r, 128), lambda c: (0, 0, 0))
    return pl.pallas_call(
        body, name=name, grid=(nc,), in_specs=[grp, grp, grp, whole(1)],
        out_specs=[pl.BlockSpec((Lc, 128), lambda c: (c, 0)), whole(8)],
        out_shape=[jax.ShapeDtypeStruct((S, 128), BF16), jax.ShapeDtypeStruct((SSM_GROUPS, 8, 128), F32)],
        compiler_params=_cp(1),
    )(dhead, dt_g, sg_g, alog_g)


def _gate_fwd(y, xbc, zx, dexp, gn, *, name="gate_fwd", tr=256, dep=None):
    S = y.shape[0]
    W = 2048
    gw = W // SSM_GROUPS

    def body(y_ref, x_ref, z_ref, d_ref, g_ref, *rest):
        o_ref = rest[-1]
        z = z_ref[...]
        u = (y_ref[...] + x_ref[...] * d_ref[...]) * (z * _sigmoid(z))
        gv = g_ref[...]
        for q in range(SSM_GROUPS):
            sl = slice(gw * q, gw * (q + 1))
            uq = u[:, sl]
            r = lax.rsqrt(jnp.mean(uq * uq, axis=-1, keepdims=True) + EPS)
            o_ref[:, sl] = (uq * r * gv[:, sl]).astype(BF16)

    row = pl.BlockSpec((tr, W), lambda i: (i, 0))
    vec = pl.BlockSpec((1, W), lambda i: (0, 0))
    return pl.pallas_call(
        body, name=name, grid=(S // tr,),
        in_specs=[row, row, row, vec, vec] + ([] if dep is None else [pl.BlockSpec(memory_space=pl.ANY)]), out_specs=row,
        out_shape=jax.ShapeDtypeStruct((S, W), BF16), compiler_params=_cp(1),
    )(y, xbc, zx, dexp, gn, *([] if dep is None else [dep]))


def _gate_bwd(y, xbc, zx, dexp, gn, dout, *, name="gate_bwd", tr=256):
    S = y.shape[0]
    W = 2048
    gw = W // SSM_GROUPS
    steps = S // tr

    def body(y_ref, x_ref, z_ref, d_ref, g_ref, do_ref, dy_ref, dz_ref, dg_ref, dd_ref, acc_ref):
        i = pl.program_id(0)

        @pl.when(i == 0)
        def _():
            acc_ref[...] = jnp.zeros_like(acc_ref)

        z = z_ref[...]
        sg = _sigmoid(z)
        sz = z * sg
        xs = x_ref[...]
        yt = y_ref[...] + xs * d_ref[...]
        u = yt * sz
        gv = g_ref[...]
        do = do_ref[...]
        dgs = []
        for q in range(SSM_GROUPS):
            sl = slice(gw * q, gw * (q + 1))
            uq = u[:, sl]
            r = lax.rsqrt(jnp.mean(uq * uq, axis=-1, keepdims=True) + EPS)
            uh = uq * r
            dq = do[:, sl]
            duh = dq * gv[:, sl]
            duq = r * (duh - uh * jnp.mean(duh * uh, axis=-1, keepdims=True))
            dgs.append(jnp.sum(dq * uh, axis=0, keepdims=True))
            dyt = duq * sz[:, sl]
            dy_ref[:, sl] = dyt
            dz_ref[:, sl] = (duq * yt[:, sl] * (sg[:, sl] * (1.0 + z[:, sl] * (1.0 - sg[:, sl])))).astype(BF16)
            acc_ref[:, sl] += jnp.sum(dyt * xs[:, sl], axis=0, keepdims=True)
        dg = jnp.concatenate(dgs, axis=1)

        @pl.when(i == 0)
        def _():
            dg_ref[...] = dg

        @pl.when(i > 0)
        def _():
            dg_ref[...] += dg

        @pl.when(i == steps - 1)
        def _():
            ind = ((_iota((W, 128), 0) >> 6) == _iota((W, 128), 1)).astype(BF16)
            dd_ref[...] = _dot3(jnp.broadcast_to(acc_ref[...], (8, W)), ind)[0:1, :]

    row = pl.BlockSpec((tr, W), lambda i: (i, 0))
    vec = pl.BlockSpec((1, W), lambda i: (0, 0))
    return pl.pallas_call(
        body, name=name, grid=(steps,), in_specs=[row, row, row, vec, vec, row],
        out_specs=[row, row, vec, pl.BlockSpec((1, 128), lambda i: (0, 0))],
        out_shape=[jax.ShapeDtypeStruct((S, W), F32), jax.ShapeDtypeStruct((S, W), BF16),
                   jax.ShapeDtypeStruct((1, W), F32), jax.ShapeDtypeStruct((1, 128), F32)],
        scratch_shapes=[pltpu.VMEM((1, W), F32)], compiler_params=_cp(1),
    )(y, xbc, zx, dexp, gn, dout)


def _rope_cs(posf, *, name="rope_tables", tr=256):
    S = posf.shape[0]

    def body(p_ref, c_ref, s_ref):
        j = (_iota((tr, 128), 1) & 31).astype(F32)
        ang = p_ref[...] * jnp.exp(j * (-math.log(ROPE_THETA) / 32.0))
        c_ref[...] = jnp.cos(ang)
        s_ref[...] = jnp.sin(ang)

    blk = pl.BlockSpec((tr, 128), lambda i: (i, 0))
    return pl.pallas_call(
        body, name=name, grid=(S // tr,), in_specs=[pl.BlockSpec((tr, 1), lambda i: (i, 0))], out_specs=[blk, blk],
        out_shape=[jax.ShapeDtypeStruct((S, 128), F32)] * 2, compiler_params=_cp(1),
    )(posf)


def _rope_tables(c_ref, s_ref, shape):
    reps = shape[1] // 128
    return jnp.tile(c_ref[...], (1, reps)), jnp.tile(s_ref[...], (1, reps)), (_iota(shape, 1) & 63) < 32


def _hn_inds(W):
    ind = ((_iota((W, 128), 0) >> 6) == _iota((W, 128), 1)).astype(BF16)
    ind_t = ((_iota((128, W), 1) >> 6) == _iota((128, W), 0)).astype(BF16)
    return ind, ind_t


def _hnrope_fwd(xin, col0, W, gain_w, rope, *, name, tr=256):
    S = xin.shape[0]
    off = col0 // W
    nh = W // HEAD

    def body(x_ref, g_ref, c_ref, s_ref, o_ref):
        x = x_ref[...]
        ind, ind_t = _hn_inds(W)
        r = lax.rsqrt(_dot3(x * x, ind) * (1.0 / HEAD) + EPS)
        xn = x * _dot3(r, ind_t) * g_ref[...]
        cs, sn, half = _rope_tables(c_ref, s_ref, (tr, W))
        rot = jnp.where(half, -pltpu.roll(xn, W - 32, axis=1), pltpu.roll(xn, 32, axis=1))
        out = (xn * cs + rot * sn).astype(BF16)
        for h in range(nh):
            o_ref[h] = out[:, HEAD * h:HEAD * (h + 1)]

    tab = pl.BlockSpec((tr, 128), lambda i: (i, 0))
    return pl.pallas_call(
        body, name=name, grid=(S // tr,),
        in_specs=[pl.BlockSpec((tr, W), lambda i: (i, off)), pl.BlockSpec((1, W), lambda i: (0, 0)), tab, tab],
        out_specs=pl.BlockSpec((nh, tr, HEAD), lambda i: (0, i, 0)), out_shape=jax.ShapeDtypeStruct((nh, S, HEAD), BF16),
        compiler_params=_cp(1),
    )(xin, gain_w, *rope)


def _hnrope_bwd(xin, col0, W, gain_w, rope, dout, *, name, tr=256):
    S = xin.shape[0]
    off = col0 // W
    steps = S // tr
    nh = W // HEAD

    def body(x_ref, g_ref, c_ref, s_ref, do_ref, dx_ref, cs_ref, dg_ref, acc_ref):
        i = pl.program_id(0)
        x = x_ref[...]
        ind, ind_t = _hn_inds(W)
        r = lax.rsqrt(_dot3(x * x, ind) * (1.0 / HEAD) + EPS)
        rw = _dot3(r, ind_t)
        xh = x * rw
        cs, sn, half = _rope_tables(c_ref, s_ref, (tr, W))
        do = jnp.concatenate([do_ref[h] for h in range(nh)], axis=1).astype(F32)
        gs = do * sn
        g1 = do * cs + jnp.where(half, pltpu.roll(gs, W - 32, axis=1), -pltpu.roll(gs, 32, axis=1))
        dxh = g1 * g_ref[...]
        t = _dot3(dxh * xh, ind) * (1.0 / HEAD)
        dx = rw * (dxh - xh * _dot3(t, ind_t))
        dx_ref[...] = dx.astype(BF16)
        cpart = jnp.sum(dx, axis=0, keepdims=True)
        gpart = jnp.sum(g1 * xh, axis=0, keepdims=True)

        @pl.when(i == 0)
        def _():
            cs_ref[...] = cpart
            acc_ref[...] = gpart

        @pl.when(i > 0)
        def _():
            cs_ref[...] += cpart
            acc_ref[...] += gpart

        @pl.when(i == steps - 1)
        def _():
            fold = ((_iota((W, 128), 0) & 63) == _iota((W, 128), 1)).astype(BF16)
            dg_ref[...] = _dot3(jnp.broadcast_to(acc_ref[...], (8, W)), fold)[0:1, :]

    tab = pl.BlockSpec((tr, 128), lambda i: (i, 0))
    return pl.pallas_call(
        body, name=name, grid=(steps,),
        in_specs=[pl.BlockSpec((tr, W), lambda i: (i, off)), pl.BlockSpec((1, W), lambda i: (0, 0)), tab, tab,
                  pl.BlockSpec((nh, tr, HEAD), lambda i: (0, i, 0))],
        out_specs=[pl.BlockSpec((tr, W), lambda i: (i, 0)), pl.BlockSpec((1, W), lambda i: (0, 0)),
                   pl.BlockSpec((1, 128), lambda i: (0, 0))],
        out_shape=[jax.ShapeDtypeStruct((S, W), BF16), jax.ShapeDtypeStruct((1, W), F32),
                   jax.ShapeDtypeStruct((1, 128), F32)],
        scratch_shapes=[pltpu.VMEM((1, W), F32)], compiler_params=_cp(1),
    )(xin, gain_w, *rope, dout)


def _attn_band():
    qi = jnp.arange(ATT_G * WINDOW)[:, None] % WINDOW
    ki = jnp.arange(2 * WINDOW)[None, :]
    rel = qi + WINDOW - ki
    ok = (rel >= 0) & (rel < WINDOW)
    return jnp.stack([jnp.where(ok & (ki >= WINDOW), 0.0, NEG), jnp.where(ok, 0.0, NEG)]).astype(F32)


def _attn_probs(q, kb, sink_ref, band_ref, h, i):
    s = _dot(q, kb, 1, 1) * (HEAD ** -0.5) + band_ref[jnp.minimum(i, 1)]
    r1 = _iota((4 * WINDOW, 1), 0)
    sink = jnp.where(r1 < WINDOW, sink_ref[4 * h], jnp.where(r1 < 2 * WINDOW, sink_ref[4 * h + 1],
                     jnp.where(r1 < 3 * WINDOW, sink_ref[4 * h + 2], sink_ref[4 * h + 3])))
    m = jnp.maximum(jnp.max(s, axis=1, keepdims=True), sink)
    p = jnp.exp(s - m)
    ps = jnp.exp(sink - m)
    inv = 1.0 / (jnp.sum(p, axis=1, keepdims=True) + ps)
    return p * inv, ps * inv


ATT_HPS = 4
_BAND = pl.BlockSpec((2, ATT_G * WINDOW, 2 * WINDOW), lambda h, i: (0, 0, 0))


def _attn_specs(S):
    qspec = pl.BlockSpec((ATT_HPS, ATT_G, WINDOW, HEAD), lambda h, i: (h, 0, i, 0))
    cur = pl.BlockSpec((ATT_HPS, WINDOW, HEAD), lambda h, i: (h, i, 0))
    prev = pl.BlockSpec((ATT_HPS, WINDOW, HEAD), lambda h, i: (h, jnp.maximum(i - 1, 0), 0))
    tok = pl.BlockSpec((WINDOW, ATT_HPS * ATT_G * HEAD), lambda h, i: (i, h))
    return qspec, cur, prev, tok


def _attn_fwd(qh, kh, vh, sinks, *, name="attn_fwd"):
    S = kh.shape[1]
    nb = S // WINDOW

    def body(s_ref, band_ref, q_ref, kc_ref, kp_ref, vc_ref, vp_ref, o_ref):
        h2, i = pl.program_id(0), pl.program_id(1)
        outs = []
        for hh in range(ATT_HPS):
            q = q_ref[hh].reshape(ATT_G * WINDOW, HEAD)
            kb = jnp.concatenate([kp_ref[hh], kc_ref[hh]], axis=0)
            vb = jnp.concatenate([vp_ref[hh], vc_ref[hh]], axis=0)
            probs, _ = _attn_probs(q, kb, s_ref, band_ref, ATT_HPS * h2 + hh, i)
            o = _dot(probs.astype(BF16), vb).astype(BF16)
            outs += [o[WINDOW * g:WINDOW * (g + 1)] for g in range(ATT_G)]
        o_ref[...] = jnp.concatenate(outs, axis=1)

    qspec, cur, prev, tok = _attn_specs(S)
    return pl.pallas_call(
        body, name=name, grid=(ATT_KV // ATT_HPS, nb),
        in_specs=[pl.BlockSpec(memory_space=pltpu.SMEM), _BAND, qspec, cur, prev, cur, prev], out_specs=tok,
        out_shape=jax.ShapeDtypeStruct((S, ATT_KV * ATT_G * HEAD), BF16), compiler_params=_cp(2),
    )(sinks, _attn_band(), qh, kh, kh, vh, vh)


def _attn_bwd(qh, kh, vh, sinks, doh, *, name="attn_bwd"):
    S = kh.shape[1]
    nb = S // WINDOW

    def body(s_ref, band_ref, q_ref, kc_ref, kp_ref, vc_ref, vp_ref, do_ref, dq_ref, dk_ref, dv_ref, dsk_ref):
        h2, i = pl.program_id(0), pl.program_id(1)

        @pl.when(i == 0)
        def _():
            dk_ref[...] = jnp.zeros_like(dk_ref)
            dv_ref[...] = jnp.zeros_like(dv_ref)
            dsk_ref[...] = jnp.zeros_like(dsk_ref)

        dov = do_ref[...]
        cur = pl.multiple_of(i * WINDOW, WINDOW)
        lane = _iota((8, 128), 1)
        row = _iota((8, 128), 0)
        scale = HEAD ** -0.5
        for hh in range(ATT_HPS):
            q = q_ref[hh].reshape(ATT_G * WINDOW, HEAD)
            do = jnp.concatenate([dov[:, HEAD * (ATT_G * hh + g):HEAD * (ATT_G * hh + g + 1)] for g in range(ATT_G)], axis=0)
            kb = jnp.concatenate([kp_ref[hh], kc_ref[hh]], axis=0)
            vb = jnp.concatenate([vp_ref[hh], vc_ref[hh]], axis=0)
            probs, psink = _attn_probs(q, kb, s_ref, band_ref, ATT_HPS * h2 + hh, i)
            dp = _dot(do, vb, 1, 1)
            delta = jnp.sum(probs * dp, axis=1, keepdims=True)
            ds = (probs * (dp - delta)).astype(BF16)
            dq_ref[hh] = (_dot(ds, kb) * scale).reshape(ATT_G, WINDOW, HEAD)
            dkb = _dot(ds, q, 0, 0) * scale
            dvb = _dot(probs.astype(BF16), do, 0, 0)
            dk_ref[hh, pl.ds(cur, WINDOW), :] += dkb[WINDOW:, :]
            dv_ref[hh, pl.ds(cur, WINDOW), :] += dvb[WINDOW:, :]
            prv = pl.multiple_of(jnp.maximum(i - 1, 0) * WINDOW, WINDOW)
            dk_ref[hh, pl.ds(prv, WINDOW), :] += dkb[:WINDOW, :]
            dv_ref[hh, pl.ds(prv, WINDOW), :] += dvb[:WINDOW, :]

            dsr = -psink * delta
            upd = jnp.zeros((8, 128), F32)
            for gq in range(ATT_G):
                v = jnp.sum(dsr[gq * WINDOW:(gq + 1) * WINDOW, :], axis=0, keepdims=True)
                upd = jnp.where((lane == gq) & (row == 0), v, upd)
            dsk_ref[hh] += upd

    qspec, cur, prev, tok = _attn_specs(S)
    full = pl.BlockSpec((ATT_HPS, S, HEAD), lambda h, i: (h, 0, 0))
    return pl.pallas_call(
        body, name=name, grid=(ATT_KV // ATT_HPS, nb),
        in_specs=[pl.BlockSpec(memory_space=pltpu.SMEM), _BAND, qspec, cur, prev, cur, prev, tok],
        out_specs=[qspec, full, full, pl.BlockSpec((ATT_HPS, 8, 128), lambda h, i: (h, 0, 0))],
        out_shape=[jax.ShapeDtypeStruct((ATT_KV, ATT_G, S, HEAD), F32), jax.ShapeDtypeStruct((ATT_KV, S, HEAD), F32),
                   jax.ShapeDtypeStruct((ATT_KV, S, HEAD), F32), jax.ShapeDtypeStruct((ATT_KV, 8, 128), F32)],
        compiler_params=_cp(2),
    )(sinks, _attn_band(), qh, kh, kh, vh, vh, doh)


def _heads_major(t, nh):
    S = t.shape[0]
    return t.reshape(S, nh, HEAD).transpose(1, 0, 2)


def _tokens_major(t):
    nh, S, _ = t.shape
    return t.transpose(1, 0, 2).reshape(S, nh * HEAD)


class _NoComm:
    def late_start(self, part, after):
        return None

    def late_arrived(self, part, after):
        return None

    def late_weights(self, w, after, part):
        return w

    def advance(self, after, group=None, tensors=None):
        return None


def _local_step(x, posf, target, w, comm=None):
    S, D = x.shape
    gr = {}
    comm = comm or _NoComm()

    zx, h1 = _norm_mm(x, w["a_norm"], w["w_zx"], name="in_proj_zx", dep=w.get("dep"))
    dtr = _mm(h1, w["w_dt"], name="in_proj_dt")
    xbc = _conv_silu_fwd(zx, 2048, 4096, w["a_conv_w"], w["a_conv_b"], name="a_conv_f")
    dt_g, ac_g, sg_g, ac_t = _ssd_prep(dtr, w["a_dt_bias"], w["a_A_log"])
    y_ssd, states = _ssd_fwd(xbc, dt_g, ac_g, ac_t, dep=comm.late_start(1, xbc))
    yg = _gate_fwd(y_ssd, xbc, zx, w["a_Dexp"], w["a_gnorm"], dep=comm.late_arrived(0, [y_ssd]))
    w = comm.late_weights(w, yg, 0)
    x1 = _mm(yg, w["a_out_proj"], res=x, name="out_proj")

    FW = w["f_w_in"][0].shape[2]

    def ffn_fwd(xin, l, loss_target=None):
        u, h = _norm_mm(xin, w["f_norm"][l], w["f_w_in"][l], name=f"f_in{l}", N=N_CHIPS * FW, tn=FW,
                        b_spec=pl.BlockSpec((None, D, FW), lambda i, j: (j, 0, 0)))
        a = _ffn_act_fwd(u, w["f_conv_w"][l], w["f_conv_b"][l], name=f"f_act_f{l}")
        dep = comm.late_arrived(1, [u]) if l == 0 else None
        xo = _mm(a, w["f_w_down"][l], res=xin, tk=a.shape[1], name=f"f_down{l}", target=loss_target, dep=dep)
        return xo, (h, u)

    x2, ffn0 = ffn_fwd(x1, 0)
    w = comm.late_weights(w, x2, 1)

    kv, hk = _norm_mm(x2, w["kv_norm"], w["w_kv"], bias=w["b_kv"], name="kv_proj")
    q, hq = _norm_mm(x2, w["b_norm"], w["w_q"], bias=w["b_q"], name="q_proj")
    rope = _rope_cs(posf)
    kr = _hnrope_fwd(kv, 0, 256, w["k_norm_w"], rope, name="k_rope_f")
    qr = _hnrope_fwd(q, 0, 1024, w["q_norm_w"], rope, name="q_rope_f")
    qh = qr.reshape(ATT_KV, ATT_G, S, HEAD)
    kh = kr
    vh = _heads_major(kv[:, 256:].astype(BF16), ATT_KV)
    att = _attn_fwd(qh, kh, vh, w["sinks"])
    x3 = _mm(att, w["w_o"], bias=w["b_o"], res=x2, name="o_proj")
    (dy, loss_part), ffn1 = ffn_fwd(x3, 1, target)

    def ffn_bwd(xin, l, saved, dyo, want_colsum, dep=None):
        h, u = saved
        da = _mm(dyo, w["f_w_down"][l], tb=True, name=f"f_down_dx{l}", dep=dep)
        du, dcw, dcb, a = _ffn_act_bwd(u, w["f_conv_w"][l], w["f_conv_b"][l], da, name=f"f_act_b{l}")
        dw_down = _mm(a, dyo, ta=True, out_dtype=BF16, name=f"f_down_dw{l}")
        dw_in = _mm(h, du, ta=True, out_dtype=BF16, name=f"f_in_dw{l}", dims=(D, N_CHIPS * FW, S), tm=D, tn=FW, tk=S,
                    b_spec=pl.BlockSpec((None, S, FW), lambda i, j, k: (j // 2, 0, j % 2)),
                    o_spec=pl.BlockSpec((None, D, FW), lambda i, j, k: (j, i, 0)), o_shape=(N_CHIPS, D, FW))
        ts = _pick(S, (512, 256))
        pieces = [(pl.BlockSpec((None, ts, FW), lambda i, j, k, q=q: (q // 2, i, q % 2)),
                   pl.BlockSpec((None, D, FW), lambda i, j, k, q=q: (q, 0, 0), pipeline_mode=pl.Buffered(1)))
                  for q in range(N_CHIPS)]
        outs = _mm(du, w["f_w_in"][l], tb=True, name=f"f_in_dx{l}", dims=(S, D, FW), tm=ts, tn=D, tk=FW,
                   a_spec=pieces[0][0], b_spec=pieces[0][1], more=pieces[1:],
                   rms=(xin, w["f_norm"][l], dyo), rms_colsum=want_colsum)
        g = dict(f_norm=outs[1], f_w_in=dw_in, f_conv_w=dcw, f_conv_b=dcb, f_w_down=dw_down)
        return outs[0], g, (outs[2] if want_colsum else None)

    dx3, gr["ffn1"], db_o = ffn_bwd(x3, 1, ffn1, dy, True)
    gr["b_o"] = db_o
    gr["w_o"] = _mm(att, dx3, ta=True, out_dtype=BF16, name="o_proj_dw")
    datt = _mm(dx3, w["w_o"], tb=True, out_dtype=BF16, name="o_proj_dx")
    dqh, dkh, dvh, dsk = _attn_bwd(qh, kh, vh, w["sinks"], datt)
    gr["sinks"] = dsk[:, 0, :4].reshape(1, 16)
    dv = _tokens_major(dvh).astype(BF16)
    dq, db_q, dqn = _hnrope_bwd(q, 0, 1024, w["q_norm_w"], rope, dqh.reshape(16, S, HEAD), name="q_rope_b")
    dk, db_k, dkn = _hnrope_bwd(kv, 0, 256, w["k_norm_w"], rope, dkh, name="k_rope_b")
    gr["q_norm"], gr["k_norm"] = dqn[:, :HEAD], dkn[:, :HEAD]
    gr["b_q"] = db_q
    gr["b_kv"] = jnp.concatenate([db_k, _colsum(dv, name="dv_colsum")], axis=1)
    dkv = jnp.concatenate([dk, dv], axis=1)
    gr["w_q"] = _mm(hq, dq, ta=True, out_dtype=BF16, name="q_proj_dw")
    gr["w_kv"] = _mm(hk, dkv, ta=True, out_dtype=BF16, name="kv_proj_dw")
    tok = comm.advance([gr["w_kv"]], 1, dict(f_down1=gr["ffn1"]["f_w_down"], f_in1=gr["ffn1"]["f_w_in"], w_o=gr["w_o"],
                                             w_q=gr["w_q"], w_kv=gr["w_kv"]))
    tsm = _pick(S, (512, 256))
    dx2, gr["b_norm"] = _mm(dq, w["w_q"], tb=True, name="q_proj_dx", dep=tok, tm=tsm, tn=D, rms=(x2, w["b_norm"], dx3))
    dx2, gr["kv_norm"] = _mm(dkv, w["w_kv"], tb=True, name="kv_proj_dx", tm=tsm, tn=D, rms=(x2, w["kv_norm"], dx2))

    dx1, gr["ffn0"], _ = ffn_bwd(x1, 0, ffn0, dx2, False, dep=comm.advance([dx2]))

    gr["a_out_proj"] = _mm(yg, dx1, ta=True, out_dtype=BF16, name="out_proj_dw")
    tok = comm.advance([dx1, gr["a_out_proj"]], 2,
                       dict(f_down0=gr["ffn0"]["f_w_down"], f_in0=gr["ffn0"]["f_w_in"], out_proj=gr["a_out_proj"]))
    dyg = _mm(dx1, w["a_out_proj"], tb=True, name="out_proj_dx", dep=tok)
    dy_ssd, dz, gr["a_gnorm"], dD = _gate_bwd(y_ssd, xbc, zx, w["a_Dexp"], w["a_gnorm"], dyg)
    gr["a_D"] = dD[:, :SSM_HEADS]
    dxs, dB, dC, dhead = _ssd_bwd(xbc, dt_g, ac_g, ac_t, states, dy_ssd, w["a_Dexp"], dep=comm.advance([dy_ssd]))
    ddtr, dsmall = _ssd_post(dhead, dt_g, sg_g, w["a_A_log_g"])
    gr["a_A_log"] = dsmall[:, 0, :4].reshape(1, SSM_HEADS)
    gr["a_dt_bias"] = dsmall[:, 1, :4].reshape(1, SSM_HEADS)
    dxbc, gr["a_conv_w"], gr["a_conv_b"] = _conv_silu_bwd(
        zx, 2048, 4096, w["a_conv_w"], w["a_conv_b"], [(dxs, 0), (dB, 2048), (dC, 3072)], name="a_conv_b")
    gr["w_z"] = _mm(h1, dz, ta=True, out_dtype=BF16, name="in_proj_dwz")
    gr["w_x"] = _mm(h1, dxbc, ta=True, out_dtype=BF16, name="in_proj_dwx")
    gr["w_dt"] = _mm(h1, ddtr, ta=True, out_dtype=BF16, name="in_proj_dwdt")
    ts = _pick(S, (512, 256))
    once = pl.Buffered(1)
    wblk = lambda q: pl.BlockSpec((D, 2048), lambda i, j, k: (0, q), pipeline_mode=once)
    dx0, gr["a_norm"] = _mm(
        dz, w["w_zx"], tb=True, name="in_proj_dx", dims=(S, D, 2048), tm=ts, tn=D, tk=2048,
        a_spec=pl.BlockSpec((ts, 2048), lambda i, j, k: (i, 0)), b_spec=wblk(0),
        more=[(dxbc, pl.BlockSpec((ts, 2048), lambda i, j, k: (i, 0)), w["w_zx"], wblk(1)),
              (dxbc, pl.BlockSpec((ts, 2048), lambda i, j, k: (i, 1)), w["w_zx"], wblk(2)),
              (ddtr, pl.BlockSpec((ts, 128), lambda i, j, k: (i, 0)), w["w_dt"],
               pl.BlockSpec((D, 128), lambda i, j, k: (0, 0), pipeline_mode=once))],
        rms=(x, w["a_norm"], dx1))
    tok = comm.advance([dx0], 3, dict(in_proj=_in_proj_grad(gr).reshape(D, N_CHIPS, -1).transpose(1, 0, 2)))
    return loss_part, dx0, gr, tok


def _prep_small(full, w):
    w["a_norm"] = full["a_norm"]
    w["a_conv_w"] = full["a_conv_w"][0]
    w["a_conv_b"] = full["a_conv_b"]
    pad32 = lambda v: jnp.pad(v, ((0, 0), (0, 128 - SSM_HEADS)))
    w["a_dt_bias"] = pad32(full["a_dt_bias"])
    w["a_A_log"] = pad32(full["a_A_log"])
    w["a_A_log_g"] = jnp.pad(full["a_A_log"].reshape(SSM_GROUPS, 1, 4), ((0, 0), (0, 0), (0, 124)))
    w["a_Dexp"] = jnp.repeat(full["a_D"], HEAD, axis=1)
    w["a_gnorm"] = full["a_gnorm"]
    w["f_norm"] = [full["f_norm"][l:l + 1] for l in range(2)]
    w["f_conv_w"] = [full["f_conv_w"][l] for l in range(2)]
    w["f_conv_b"] = [full["f_conv_b"][l:l + 1] for l in range(2)]
    w["kv_norm"] = full["kv_norm"].reshape(1, -1)
    w["b_kv"] = full["b_kv"].reshape(1, -1)
    w["k_norm_w"] = jnp.tile(full["k_norm"].reshape(1, HEAD), (1, ATT_KV))
    w["b_norm"] = full["b_norm"]
    w["b_q"] = full["b_q"]
    w["q_norm_w"] = jnp.tile(full["q_norm"], (1, ATT_KV * ATT_G))
    w["sinks"] = full["sinks"].reshape(-1)
    w["b_o"] = full["b_o"]
    return w


def _split_in_proj(ip):
    return ip[:, :6144].astype(BF16), jnp.pad(ip[:, 6144:], ((0, 0), (0, 128 - SSM_HEADS))).astype(BF16)


def _join_in_proj(blocks, *, name="in_proj_join", tr=256):
    _, R, cw = blocks.shape
    zx_cols = 3 * 2048
    rest = N_CHIPS * cw - zx_cols

    def body(b_ref, zx_ref, dt_ref):
        whole = jnp.concatenate([b_ref[j] for j in range(N_CHIPS)], axis=1)
        zx_ref[...] = whole[:, :zx_cols]
        dt_ref[...] = jnp.concatenate([whole[:, zx_cols:], jnp.zeros((tr, 128 - rest), BF16)], axis=1)

    return pl.pallas_call(
        body, name=name, grid=(R // tr,), in_specs=[pl.BlockSpec((N_CHIPS, tr, cw), lambda i: (0, i, 0))],
        out_specs=[pl.BlockSpec((tr, zx_cols), lambda i: (i, 0)), pl.BlockSpec((tr, 128), lambda i: (i, 0))],
        out_shape=[jax.ShapeDtypeStruct((R, zx_cols), BF16), jax.ShapeDtypeStruct((R, 128), BF16)],
        compiler_params=_cp(1),
    )(blocks)


def _prep_weights(full):
    w = _prep_small(full, {})
    w["w_zx"], w["w_dt"] = _split_in_proj(full["a_in_proj"][0])
    w["a_out_proj"] = full["a_out_proj"][0].astype(BF16)
    w["f_w_in"] = [full["f_w_in"][l].reshape(1024, N_CHIPS, -1).transpose(1, 0, 2).astype(BF16) for l in range(2)]
    w["f_w_down"] = [full["f_w_down"][l].astype(BF16) for l in range(2)]
    w["w_kv"] = full["w_kv"].astype(BF16)
    w["w_q"] = full["w_q"][0].astype(BF16)
    w["w_o"] = full["w_o"][0].astype(BF16)
    return w


def _small_grads(gr):
    g = {}
    g["a_norm"] = gr["a_norm"]
    g["a_conv_w"] = gr["a_conv_w"][None]
    g["a_conv_b"] = gr["a_conv_b"]
    g["a_dt_bias"], g["a_A_log"], g["a_D"] = gr["a_dt_bias"], gr["a_A_log"], gr["a_D"]
    g["a_gnorm"] = gr["a_gnorm"]
    g["kv_norm"] = gr["kv_norm"].reshape(-1)
    g["b_kv"] = gr["b_kv"].reshape(-1)
    g["k_norm"] = gr["k_norm"].reshape(-1)
    g["b_norm"] = gr["b_norm"]
    g["b_q"] = gr["b_q"]
    g["q_norm"] = gr["q_norm"]
    g["sinks"] = gr["sinks"]
    g["b_o"] = gr["b_o"]
    f = [gr["ffn0"], gr["ffn1"]]
    g["f_norm"] = jnp.concatenate([f[0]["f_norm"], f[1]["f_norm"]], axis=0)
    g["f_conv_w"] = jnp.stack([f[l]["f_conv_w"] for l in range(2)])
    g["f_conv_b"] = jnp.concatenate([f[l]["f_conv_b"] for l in range(2)], axis=0)
    return g


def _in_proj_grad(gr):
    return jnp.concatenate([gr["w_z"], gr["w_x"], gr["w_dt"][:, :SSM_HEADS]], axis=1)


def _full_grads(gr):
    g = _small_grads(gr)
    f32 = lambda t: t.astype(F32)
    g["a_in_proj"] = f32(_in_proj_grad(gr))[None]
    g["a_out_proj"] = f32(gr["a_out_proj"])[None]
    g["w_kv"] = f32(gr["w_kv"])
    g["w_q"] = f32(gr["w_q"])[None]
    g["w_o"] = f32(gr["w_o"])[None]
    f = [gr["ffn0"], gr["ffn1"]]
    g["f_w_in"] = jnp.stack([f32(f[l]["f_w_in"]).transpose(1, 0, 2).reshape(1024, -1) for l in range(2)])
    g["f_w_down"] = jnp.stack([f32(f[l]["f_w_down"]) for l in range(2)])
    return g


MESH = pl.DeviceIdType.MESH
WEIGHTS = ("a_norm", "a_in_proj", "a_conv_w", "a_conv_b", "a_dt_bias", "a_A_log", "a_D", "a_gnorm", "a_out_proj",
           "kv_norm", "w_kv", "b_kv", "k_norm", "b_norm", "w_q", "b_q", "q_norm", "sinks", "w_o", "b_o", "f_norm",
           "f_w_in", "f_conv_w", "f_conv_b", "f_w_down")
MATS = (("in_proj", "a_in_proj", 0), ("out_proj", "a_out_proj", 0), ("w_kv", "w_kv", None), ("w_q", "w_q", 0),
        ("w_o", "w_o", 0), ("f_in0", "f_w_in", 0), ("f_in1", "f_w_in", 1), ("f_down0", "f_w_down", 0),
        ("f_down1", "f_w_down", 1))
SMALL_CUT = (("a_norm", 1), ("a_conv_w", 2), ("a_conv_b", 1), ("a_gnorm", 1), ("f_conv_w", 2))
SMALL_REP = ("a_dt_bias", "a_A_log", "a_D", "kv_norm", "b_kv", "k_norm", "b_norm", "b_q", "q_norm", "sinks", "b_o",
             "f_norm", "f_conv_b")


def _coords():
    return lax.axis_index("x"), lax.axis_index("y"), lax.axis_index("c")


def _other_chips(x, y):
    return [(1 - x, y), (x, 1 - y), (1 - x, 1 - y)]


def _pack(arrs, rows_align, lanes, dtype):
    flat = jnp.concatenate([a.reshape(-1).astype(dtype) for a in arrs])
    per = rows_align * lanes
    total = -(-flat.shape[0] // per) * per
    return jnp.pad(flat, (0, total - flat.shape[0])).reshape(total // lanes, lanes)


def _unpack(flat, shapes):
    out, off = [], 0
    for s in shapes:
        n = math.prod(s)
        out.append(flat[off:off + n].reshape(s))
        off += n
    return out


def _remote(src, dst, send, recv, k, dev):
    return pltpu.make_async_remote_copy(src_ref=src, dst_ref=dst, send_sem=send.at[k], recv_sem=recv.at[k],
                                        device_id=dev, device_id_type=MESH)


_ANY = pl.BlockSpec(memory_space=pl.ANY)


def _halves(t):
    r, c = t.shape
    return t.reshape(2, r // 2, c)


def _gather_weights(shards, sp):
    n = len(shards)
    per = 9
    n_sem = per * n + 3

    def body(*refs):
        sh, sp_ref = refs[:n], refs[n]
        outs, sout = refs[n + 1:2 * n + 1], refs[2 * n + 1]
        send, recv, loc = refs[2 * n + 2:]
        x, y, c = _coords()
        me = 2 * x + y
        cx_, cy_, cd_ = _other_chips(x, y)
        ix, iy, idg = (2 * p[0] + p[1] for p in (cx_, cy_, cd_))
        to_x, to_y, sib = (*cx_, c), (*cy_, c), (x, y, 1 - c)
        l1 = pltpu.make_async_copy(sp_ref, sout.at[me], loc.at[0])
        l1.start()
        sends = [_remote(sp_ref, sout.at[me], send, recv, per * n + j, (*p, c)) for j, p in enumerate((cx_, cy_, cd_))]
        for t in range(n):
            sends.append(_remote(sh[t].at[c], outs[t].at[me, c], send, recv, per * t + 0, to_x))
            sends.append(_remote(sh[t].at[c], outs[t].at[me, c], send, recv, per * t + 1, to_y))
            sends.append(_remote(sh[t], outs[t].at[me], send, recv, per * t + 8, sib))
        for cp in sends:
            cp.start()

        def go(src, dst, k, dev):
            cp = _remote(src, dst, send, recv, k, dev)
            cp.start()
            sends.append(cp)

        def piece(t, owner, first):
            q = sh[t].shape[1] // 2
            return outs[t].at[owner, c, pl.ds(0 if first else q, q)]

        for t in range(n):
            _remote(sh[t].at[c], outs[t].at[ix, c], send, recv, per * t + 0, to_x).wait_recv()
            go(piece(t, ix, False), piece(t, ix, False), per * t + 3, to_y)
            go(outs[t].at[ix, c], outs[t].at[ix, c], per * t + 4, sib)
        for t in range(n):
            _remote(sh[t].at[c], outs[t].at[iy, c], send, recv, per * t + 1, to_y).wait_recv()
            go(piece(t, iy, True), piece(t, iy, True), per * t + 2, to_x)
            go(outs[t].at[iy, c], outs[t].at[iy, c], per * t + 5, sib)
        for t in range(n):
            _remote(piece(t, idg, True), piece(t, idg, True), send, recv, per * t + 2, to_x).wait_recv()
            go(piece(t, idg, True), piece(t, idg, True), per * t + 6, sib)
            _remote(piece(t, idg, False), piece(t, idg, False), send, recv, per * t + 3, to_y).wait_recv()
            go(piece(t, idg, False), piece(t, idg, False), per * t + 7, sib)
        for j, p in enumerate((cx_, cy_, cd_)):
            _remote(sp_ref, sout.at[2 * p[0] + p[1]], send, recv, per * n + j, (*p, c)).wait_recv()
        for t in range(n):
            q = sh[t].shape[1] // 2
            other = lambda owner, lo=None: outs[t].at[owner, 1 - c] if lo is None else outs[t].at[owner, 1 - c, pl.ds(lo, q)]
            _remote(other(ix), other(ix), send, recv, per * t + 4, sib).wait_recv()
            _remote(other(iy), other(iy), send, recv, per * t + 5, sib).wait_recv()
            _remote(other(idg, 0), other(idg, 0), send, recv, per * t + 6, sib).wait_recv()
            _remote(other(idg, q), other(idg, q), send, recv, per * t + 7, sib).wait_recv()
            _remote(sh[t], outs[t].at[me], send, recv, per * t + 8, sib).wait_recv()
        for cp in sends:
            cp.wait_send()
        l1.wait()

    res = pl.pallas_call(
        body, name="gather_weights", in_specs=[_ANY] * (n + 1), out_specs=[_ANY] * (n + 1),
        out_shape=[jax.ShapeDtypeStruct((N_CHIPS,) + t.shape, t.dtype) for t in shards]
        + [jax.ShapeDtypeStruct((N_CHIPS,) + sp.shape, sp.dtype)],
        scratch_shapes=[pltpu.SemaphoreType.DMA((n_sem,)), pltpu.SemaphoreType.DMA((n_sem,)),
                        pltpu.SemaphoreType.DMA((1,))],
    )(*shards, sp)
    return res[:n], res[n]


_HBM = pl.BlockSpec(memory_space=pltpu.HBM)
_SEMS = pl.BlockSpec(memory_space=pltpu.SEMAPHORE)
_DATAFLOW = pltpu.SideEffectType.DATAFLOW_SIDE_EFFECTING


def _in_hbm(a):
    return pltpu.with_memory_space_constraint(a, pltpu.HBM)


def _start_copies(copies, arrays, n_sem, after, *, name):
    n, na = len(arrays), len(after)

    def body(*refs):
        for mine, _ in copies(refs[:n], refs[n + na], refs[n + na + 1]):
            mine.start()
        refs[-1][...] = jnp.zeros_like(refs[-1])

    res = pl.pallas_call(
        body, name=name, in_specs=[_HBM] * n + [_ANY] * na,
        out_specs=[_SEMS, _SEMS] + [_HBM] * n + [pl.BlockSpec(memory_space=pltpu.VMEM)],
        out_shape=[pltpu.SemaphoreType.DMA((n_sem,)), pltpu.SemaphoreType.DMA((n_sem,))]
        + [pltpu.HBM(a.shape, a.dtype) for a in arrays] + [jax.ShapeDtypeStruct((8, 128), F32)],
        input_output_aliases={t: 2 + t for t in range(n)},
        compiler_params=pltpu.CompilerParams(has_side_effects=_DATAFLOW),
    )(*[_in_hbm(a) for a in arrays], *after)
    return res[0], res[1], list(res[2:2 + n]), res[-1]


def _wait_copies(copies, send, recv, arrays, after, *, name):
    n = len(arrays)

    def body(*refs):
        for mine, theirs in copies(refs[:n], refs[n], refs[n + 1]):
            mine.wait_send()
            theirs.wait_recv()

    return list(pl.pallas_call(
        body, name=name, in_specs=[_HBM] * n + [_SEMS, _SEMS] + [_ANY] * len(after), out_specs=[_HBM] * n,
        out_shape=[pltpu.HBM(a.shape, a.dtype) for a in arrays], input_output_aliases={t: t for t in range(n)},
        compiler_params=pltpu.CompilerParams(has_side_effects=_DATAFLOW),
    )(*arrays, send, recv, *after))


def _sibling_copies(refs, send, recv):
    n = len(refs) // 2
    x, y, c = _coords()
    cps = [_remote(refs[t].at[:, 1 - c], refs[n + t], send, recv, t, (x, y, 1 - c)) for t in range(n)]
    return [(cp, cp) for cp in cps]


def _join_copies(refs, send, recv):
    x, y, c = _coords()
    sib = (x, y, 1 - c)
    return [(_remote(o.at[c], o.at[c], send, recv, t, sib), _remote(o.at[1 - c], o.at[1 - c], send, recv, t, sib))
            for t, o in enumerate(refs)]


def _gather_copies(sh, land, send, recv):
    x, y, c = _coords()
    me = 2 * x + y
    out = []
    for t in range(len(sh)):
        for j, (cx, cy) in enumerate(_other_chips(x, y)):
            dev = (cx, cy, c)
            out.append((_remote(sh[t].at[c], land[t].at[me, c], send, recv, 4 * t + j, dev),
                        _remote(sh[t].at[c], land[t].at[2 * cx + cy, c], send, recv, 4 * t + j, dev)))
        sib = (x, y, 1 - c)
        out.append((_remote(sh[t], land[t].at[me], send, recv, 4 * t + 3, sib),
                    _remote(sh[t], land[t].at[me], send, recv, 4 * t + 3, sib)))
    return out


def _gather_start(shards, after, *, name):
    n = len(shards)

    def body(*refs):
        sh, land = refs[:n], refs[n:2 * n]
        send, recv = refs[2 * n + 1], refs[2 * n + 2]
        token = refs[-1]
        for mine, _ in _gather_copies(sh, land, send, recv):
            mine.start()
        token[...] = jnp.zeros_like(token)

    lands = [_in_hbm(lax.empty((N_CHIPS,) + s.shape, s.dtype)) for s in shards]
    res = pl.pallas_call(
        body, name=name, in_specs=[_HBM] * (2 * n) + [_ANY],
        out_specs=[_SEMS, _SEMS] + [_HBM] * (2 * n) + [pl.BlockSpec(memory_space=pltpu.VMEM)],
        out_shape=[pltpu.SemaphoreType.DMA((4 * n,)), pltpu.SemaphoreType.DMA((4 * n,))]
        + [pltpu.HBM(s.shape, s.dtype) for s in shards] + [pltpu.HBM(l.shape, l.dtype) for l in lands]
        + [jax.ShapeDtypeStruct((8, 128), F32)],
        input_output_aliases={t: 2 + t for t in range(2 * n)},
        compiler_params=pltpu.CompilerParams(has_side_effects=_DATAFLOW),
    )(*[_in_hbm(s) for s in shards], *lands, after)
    return res[0], res[1], res[2:2 + n], res[2 + n:2 + 2 * n], res[-1]


def _gather_wait(send, recv, shards, lands, after, *, name):
    n = len(shards)

    def body(*refs):
        sh, land = refs[:n], refs[n:2 * n]
        send_r, recv_r = refs[2 * n], refs[2 * n + 1]
        for mine, theirs in _gather_copies(sh, land, send_r, recv_r):
            mine.wait_send()
            theirs.wait_recv()

    res = pl.pallas_call(
        body, name=name, in_specs=[_HBM] * (2 * n) + [_SEMS, _SEMS, _ANY], out_specs=[_HBM] * (2 * n),
        out_shape=[pltpu.HBM(s.shape, s.dtype) for s in shards] + [pltpu.HBM(l.shape, l.dtype) for l in lands],
        input_output_aliases={t: t for t in range(2 * n)},
        compiler_params=pltpu.CompilerParams(has_side_effects=_DATAFLOW),
    )(*shards, *lands, send, recv, after)
    return res[n:]


def _forward_copies(refs, send, recv):
    x, y, c = _coords()
    sib = (x, y, 1 - c)
    srcs = [2 * cx + cy for cx, cy in _other_chips(x, y)]
    return [(_remote(o.at[s, c], o.at[s, c], send, recv, 3 * t + j, sib),
             _remote(o.at[s, 1 - c], o.at[s, 1 - c], send, recv, 3 * t + j, sib))
            for t, o in enumerate(refs) for j, s in enumerate(srcs)]


def _small_copies(v, land, send, recv):
    x, y, c = _coords()
    me = 4 * x + 2 * y + c
    out = []
    for k in range(1, 8):
        px = 1 - x if k & 4 else x
        py = 1 - y if k & 2 else y
        pc = 1 - c if k & 1 else c
        out.append((_remote(v, land.at[me], send, recv, k - 1, (px, py, pc)),
                    _remote(v, land.at[4 * px + 2 * py + pc], send, recv, k - 1, (px, py, pc))))
    return out


def _small_start(v, after, *, name):
    def body(v_ref, land_ref, after_ref, send, recv, v_thru, land_thru, token):
        for mine, _ in _small_copies(v_ref, land_ref, send, recv):
            mine.start()
        token[...] = jnp.zeros_like(token)

    land = _in_hbm(lax.empty((8,) + v.shape, v.dtype))
    return pl.pallas_call(
        body, name=name, in_specs=[_HBM, _HBM, _ANY],
        out_specs=[_SEMS, _SEMS, _HBM, _HBM, pl.BlockSpec(memory_space=pltpu.VMEM)],
        out_shape=[pltpu.SemaphoreType.DMA((7,)), pltpu.SemaphoreType.DMA((7,)), pltpu.HBM(v.shape, v.dtype),
                   pltpu.HBM(land.shape, land.dtype), jax.ShapeDtypeStruct((8, 128), F32)],
        input_output_aliases={0: 2, 1: 3}, compiler_params=pltpu.CompilerParams(has_side_effects=_DATAFLOW),
    )(_in_hbm(v), land, after)


def _small_wait(send, recv, v, land, after, *, name):
    def body(v_ref, land_ref, send_r, recv_r, *rest):
        for mine, theirs in _small_copies(v_ref, land_ref, send_r, recv_r):
            mine.wait_send()
            theirs.wait_recv()

    return pl.pallas_call(
        body, name=name, in_specs=[_HBM, _HBM, _SEMS, _SEMS] + [_ANY] * len(after), out_specs=[_HBM, _HBM],
        out_shape=[pltpu.HBM(v.shape, v.dtype), pltpu.HBM(land.shape, land.dtype)],
        input_output_aliases={0: 0, 1: 1}, compiler_params=pltpu.CompilerParams(has_side_effects=_DATAFLOW),
    )(v, land, send, recv, *after)


def _small_sum(v, land, me_idx, *, name="small_sum"):
    def body(me_ref, v_ref, land_ref, o_ref):
        acc = None
        for s in range(8):
            term = jnp.where(me_ref[0] == s, v_ref[...], land_ref[s])
            acc = term if acc is None else acc + term
        o_ref[...] = acc

    whole = lambda shape: pl.BlockSpec(shape, lambda i, me_ref: (0,) * len(shape))
    return pl.pallas_call(
        body, name=name,
        grid_spec=pltpu.PrefetchScalarGridSpec(num_scalar_prefetch=1, grid=(1,), in_specs=[whole(v.shape), whole(land.shape)],
                                               out_specs=whole(v.shape)),
        out_shape=jax.ShapeDtypeStruct(v.shape, F32), compiler_params=_cp(1),
    )(me_idx, v, land)


RS_ROW_SPLIT = 2


def _rs_add_pair(gs, as_, c_idx, *, name):
    n = len(gs)

    def body(c_ref, *refs):
        for t in range(n):
            refs[2 * n + t][...] = (refs[t][...].astype(F32) + refs[n + t][...].astype(F32)).astype(BF16)

    def gspec(g):
        _, _, rh, cols = g.shape
        return pl.BlockSpec((None, None, rh // RS_ROW_SPLIT, cols), lambda j, i, c_ref: (j, c_ref[0], i, 0))

    def pspec(g):
        _, _, rh, cols = g.shape
        return pl.BlockSpec((None, rh // RS_ROW_SPLIT, cols), lambda j, i, c_ref: (j, i, 0))

    return pl.pallas_call(
        body, name=name,
        grid_spec=pltpu.PrefetchScalarGridSpec(
            num_scalar_prefetch=1, grid=(N_CHIPS, RS_ROW_SPLIT),
            in_specs=[gspec(g) for g in gs] + [pspec(g) for g in gs], out_specs=[pspec(g) for g in gs]),
        out_shape=[jax.ShapeDtypeStruct((N_CHIPS,) + g.shape[2:], BF16) for g in gs], compiler_params=_cp(2),
    )(c_idx, *gs, *as_)


def _chips_copies(p, r, send, recv):
    x, y, c = _coords()
    return [_remote(p[t].at[2 * cx + cy], r[t].at[k], send, recv, 3 * t + k, (cx, cy, c))
            for k, (cx, cy) in enumerate(_other_chips(x, y)) for t in range(len(p))]


def _rs_chips_start(ps, after, *, name):
    n, na = len(ps), len(after)

    def body(*refs):
        p, r = refs[:n], refs[n:2 * n]
        send, recv = refs[2 * n + na], refs[2 * n + na + 1]
        token = refs[-1]
        for cp in _chips_copies(p, r, send, recv):
            cp.start()
        token[...] = jnp.zeros_like(token)

    lands = [_in_hbm(lax.empty((3,) + p.shape[1:], p.dtype)) for p in ps]
    res = pl.pallas_call(
        body, name=name, in_specs=[_HBM] * (2 * n) + [_ANY] * na,
        out_specs=[_SEMS, _SEMS] + [_HBM] * (2 * n) + [pl.BlockSpec(memory_space=pltpu.VMEM)],
        out_shape=[pltpu.SemaphoreType.DMA((3 * n,)), pltpu.SemaphoreType.DMA((3 * n,))]
        + [pltpu.HBM(p.shape, p.dtype) for p in ps] + [pltpu.HBM(l.shape, l.dtype) for l in lands]
        + [jax.ShapeDtypeStruct((8, 128), F32)],
        input_output_aliases={t: 2 + t for t in range(2 * n)},
        compiler_params=pltpu.CompilerParams(has_side_effects=_DATAFLOW),
    )(*[_in_hbm(p) for p in ps], *lands, *after)
    return res[0], res[1], res[2:2 + n], res[2 + n:2 + 2 * n], res[-1]


def _rs_chips_wait(send, recv, ps, lands, after, *, name):
    n = len(ps)

    def body(*refs):
        p, r = refs[:n], refs[n:2 * n]
        for cp in _chips_copies(p, r, refs[2 * n], refs[2 * n + 1]):
            cp.wait_send()
            cp.wait_recv()

    res = pl.pallas_call(
        body, name=name, in_specs=[_HBM] * (2 * n) + [_SEMS, _SEMS] + [_ANY] * len(after), out_specs=[_HBM] * (2 * n),
        out_shape=[pltpu.HBM(p.shape, p.dtype) for p in ps] + [pltpu.HBM(l.shape, l.dtype) for l in lands],
        input_output_aliases={t: t for t in range(2 * n)},
        compiler_params=pltpu.CompilerParams(has_side_effects=_DATAFLOW),
    )(*ps, *lands, send, recv, *after)
    return res[:n], res[n:]


def _rs_add_chips(ps, rs, idx, *, name):
    n = len(ps)

    def body(idx_ref, *refs):
        for t in range(n):
            p_ref, r0, r1, r2 = refs[4 * t:4 * t + 4]
            refs[4 * n + t][...] = ((p_ref[...].astype(F32) + r0[...].astype(F32)) + r1[...].astype(F32)) + r2[...].astype(F32)

    in_specs, args = [], []
    for p, r in zip(ps, rs):
        _, rh, cols = p.shape
        blk = (None, rh // RS_ROW_SPLIT, cols)
        in_specs.append(pl.BlockSpec(blk, lambda i, idx_ref: (idx_ref[0], i, 0)))
        in_specs += [pl.BlockSpec(blk, lambda i, idx_ref, k=k: (k, i, 0)) for k in range(3)]
        args += [p, r, r, r]
    out_specs = [pl.BlockSpec((None, p.shape[1] // RS_ROW_SPLIT, p.shape[2]), lambda i, idx_ref: (idx_ref[1], i, 0))
                 for p in ps]
    return pl.pallas_call(
        body, name=name,
        grid_spec=pltpu.PrefetchScalarGridSpec(num_scalar_prefetch=1, grid=(RS_ROW_SPLIT,), in_specs=in_specs,
                                               out_specs=out_specs),
        out_shape=[jax.ShapeDtypeStruct((2,) + p.shape[1:], F32) for p in ps], compiler_params=_cp(1),
    )(idx, *args)


def _adamw(w, gs, m, v, *, name, dep=None):
    L, Rr, C = w.shape
    tr, tc = _pick(Rr, (256, 128, 64)), C
    if tr == Rr and Rr * C > 512 * 1024:
        tc = 256
    bc1 = 1.0 - ADAM_B1 ** ADAM_STEP
    bc2 = 1.0 - ADAM_B2 ** ADAM_STEP
    nd = 0 if dep is None else 1

    def body(*refs):
        w_ref, m_ref, v_ref = refs[0], refs[1], refs[2]
        g_refs = refs[3:3 + L]
        d_ref, mo_ref, vo_ref, go_ref = refs[3 + L + nd:]
        layer = pl.program_id(0)
        gv = g_refs[0][...]
        for q in range(1, L):
            gv = jnp.where(layer == q, g_refs[q][...], gv)
        mn = ADAM_B1 * m_ref[...] + (1.0 - ADAM_B1) * gv
        vn = ADAM_B2 * v_ref[...] + (1.0 - ADAM_B2) * (gv * gv)
        go_ref[...] = gv
        mo_ref[...] = mn
        vo_ref[...] = vn
        d_ref[...] = -ADAM_LR * ((mn / bc1) / (jnp.sqrt(vn / bc2) + ADAM_EPS) + ADAM_WD * w_ref[...])

    blk = pl.BlockSpec((None, tr, tc), lambda l, i, j: (l, i, j))
    gblks = [pl.BlockSpec((tr, tc), lambda l, i, j, q=q: (jnp.where(l == q, i, 0), jnp.where(l == q, j, 0))) for q in range(L)]
    return pl.pallas_call(
        body, name=name, grid=(L, Rr // tr, C // tc), in_specs=[blk] * 3 + gblks + [_ANY] * nd, out_specs=[blk] * 4,
        out_shape=[jax.ShapeDtypeStruct((L, Rr, C), F32)] * 4, compiler_params=_cp(3),
    )(w, m, v, *gs, *([] if dep is None else [dep]))


def kernel(x, positions, a_norm, a_in_proj, a_conv_w, a_conv_b, a_dt_bias, a_A_log, a_D, a_gnorm, a_out_proj,
           kv_norm, w_kv, b_kv, k_norm, b_norm, w_q, b_q, q_norm, sinks, w_o, b_o, f_norm, f_w_in, f_conv_w,
           f_conv_b, f_w_down, loss_target, m_a_norm, m_a_in_proj, m_a_conv_w, m_a_conv_b, m_a_dt_bias, m_a_A_log,
           m_a_D, m_a_gnorm, m_a_out_proj, m_kv_norm, m_w_kv, m_b_kv, m_k_norm, m_b_norm, m_w_q, m_b_q, m_q_norm,
           m_sinks, m_w_o, m_b_o, m_f_norm, m_f_w_in, m_f_conv_w, m_f_conv_b, m_f_w_down, v_a_norm, v_a_in_proj,
           v_a_conv_w, v_a_conv_b, v_a_dt_bias, v_a_A_log, v_a_D, v_a_gnorm, v_a_out_proj, v_kv_norm, v_w_kv,
           v_b_kv, v_k_norm, v_b_norm, v_w_q, v_b_q, v_q_norm, v_sinks, v_w_o, v_b_o, v_f_norm, v_f_w_in,
           v_f_conv_w, v_f_conv_b, v_f_w_down):
    wl = dict(zip(WEIGHTS, (a_norm, a_in_proj, a_conv_w, a_conv_b, a_dt_bias, a_A_log, a_D, a_gnorm, a_out_proj,
                            kv_norm, w_kv, b_kv, k_norm, b_norm, w_q, b_q, q_norm, sinks, w_o, b_o, f_norm, f_w_in,
                            f_conv_w, f_conv_b, f_w_down)))
    ml = dict(zip(WEIGHTS, (m_a_norm, m_a_in_proj, m_a_conv_w, m_a_conv_b, m_a_dt_bias, m_a_A_log, m_a_D, m_a_gnorm,
                            m_a_out_proj, m_kv_norm, m_w_kv, m_b_kv, m_k_norm, m_b_norm, m_w_q, m_b_q, m_q_norm,
                            m_sinks, m_w_o, m_b_o, m_f_norm, m_f_w_in, m_f_conv_w, m_f_conv_b, m_f_w_down)))
    vl = dict(zip(WEIGHTS, (v_a_norm, v_a_in_proj, v_a_conv_w, v_a_conv_b, v_a_dt_bias, v_a_A_log, v_a_D, v_a_gnorm,
                            v_a_out_proj, v_kv_norm, v_w_kv, v_b_kv, v_k_norm, v_b_norm, v_w_q, v_b_q, v_q_norm,
                            v_sinks, v_w_o, v_b_o, v_f_norm, v_f_w_in, v_f_conv_w, v_f_conv_b, v_f_w_down)))
    xi, yi, ci = _coords()
    me = 2 * xi + yi
    S = x.shape[1]

    def block_of(n, layer):
        t = wl[n]
        return t if layer is None else t[layer]

    rows = lambda t: t.reshape(-1, t.shape[-1])
    c_idx = jnp.reshape(ci, (1,)).astype(jnp.int32)
    me_c = jnp.stack([me, ci]).astype(jnp.int32)
    early = ("in_proj",)
    late = (("out_proj", "f_in0", "f_down0"), ("w_kv", "w_q", "w_o", "f_in1", "f_down1"))
    shards = {name: _halves(block_of(wn, layer).astype(BF16)) for name, wn, layer in MATS}

    sp = _pack([wl[n] for n, _ in SMALL_CUT], 8, 128, F32)
    gathered, gs = _gather_weights([shards[k] for k in early], sp)
    gt = {k: t.reshape(N_CHIPS, -1, t.shape[-1]) for k, t in zip(early, gathered)}
    started = {0: _gather_start([shards[k] for k in late[0]], gs, name="gather_late_start0")}
    full = {n: wl[n] for n in SMALL_REP}
    gs = gs.reshape(N_CHIPS, -1)
    pieces = [_unpack(gs[j], [wl[n].shape for n, _ in SMALL_CUT]) for j in range(N_CHIPS)]
    for q, (n, ax) in enumerate(SMALL_CUT):
        full[n] = jnp.concatenate([pieces[j][q] for j in range(N_CHIPS)], axis=ax)
    w = _prep_small(full, {})
    w["w_zx"], w["w_dt"] = _join_in_proj(gt["in_proj"])
    w["dep"] = started[0][4]

    class Comm:
        flight = []
        reduced = {}

        forwarding = {}

        def late_start(self, part, after):
            started[part] = _gather_start([shards[k] for k in late[part]], after, name=f"gather_late_start{part}")
            return started[part][4]

        def late_arrived(self, part, after):
            send, recv, shs, lands, _ = started[part]
            lands = _gather_wait(send, recv, shs, lands, after[0], name=f"gather_late_wait{part}")
            send, recv, lands, token = _start_copies(_forward_copies, list(lands), 3 * len(lands), after,
                                                     name=f"gather_late_forward_start{part}")
            self.forwarding[part] = (send, recv, lands)
            return token

        def late_weights(self, w, after, part):
            send, recv, lands = self.forwarding[part]
            lands = _wait_copies(_forward_copies, send, recv, lands, [after], name=f"gather_late_forward_wait{part}")
            lt = {k: t.reshape(N_CHIPS, -1, t.shape[-1]) for k, t in zip(late[part], lands)}
            w = dict(w)
            if part == 0:
                w["a_out_proj"], w["f_w_in"], w["f_w_down"] = rows(lt["out_proj"]), [lt["f_in0"]], [rows(lt["f_down0"])]
            else:
                w["w_kv"], w["w_q"], w["w_o"] = (rows(lt[k]) for k in ("w_kv", "w_q", "w_o"))
                w["f_w_in"], w["f_w_down"] = w["f_w_in"] + [lt["f_in1"]], w["f_w_down"] + [rows(lt["f_down1"])]
            return w

        def advance(self, after, group=None, tensors=None):
            token = None
            for grp in list(self.flight):
                tag, n = grp["tag"], len(grp["names"])
                dep = list(after) + ([] if token is None else [token])
                if grp["stage"] == "sibling":
                    arrs = _wait_copies(_sibling_copies, grp["send"], grp["recv"], grp["arrays"], dep, name=f"rs_sibling_wait{tag}")
                    pairs = _rs_add_pair(arrs[:n], arrs[n:], c_idx, name=f"rs_add_pair{tag}")
                    send, recv, ps, lands, token = _rs_chips_start(pairs, dep, name=f"rs_chips_start{tag}")
                    grp.update(stage="chips", send=send, recv=recv, ps=ps, lands=lands)
                elif grp["stage"] == "chips":
                    ps, rs = _rs_chips_wait(grp["send"], grp["recv"], grp["ps"], grp["lands"], dep, name=f"rs_chips_wait{tag}")
                    halves = _rs_add_chips(ps, rs, me_c, name=f"rs_add_chips{tag}")
                    send, recv, arrs, token = _start_copies(_join_copies, halves, n, dep, name=f"rs_join_start{tag}")
                    grp.update(stage="join", send=send, recv=recv, arrays=arrs)
                else:
                    joined = _wait_copies(_join_copies, grp["send"], grp["recv"], grp["arrays"], dep, name=f"rs_join_wait{tag}")
                    self.reduced.update({k: rows(t) for k, t in zip(grp["names"], joined)})
                    self.flight.remove(grp)
            if group is not None:
                names = list(tensors)
                glist = [tensors[k].reshape(N_CHIPS, 2, -1, tensors[k].shape[-1]) for k in names]
                lands = [lax.empty((N_CHIPS,) + gq.shape[2:], gq.dtype) for gq in glist]
                dep = list(after) + ([] if token is None else [token])
                send, recv, arrs, token = _start_copies(_sibling_copies, glist + lands, len(names), dep,
                                                        name=f"rs_sibling_start{group}")
                self.flight.append(dict(tag=group, names=names, stage="sibling", send=send, recv=recv, arrays=arrs))
            return token

    comm = Comm()

    posf = positions.reshape(S, 1).astype(F32)
    loss_part, dx0, gr, tok = _local_step(x[0], posf, loss_target[0], w, comm)
    g = _small_grads(gr)

    small_names = [n for n, _ in SMALL_CUT] + list(SMALL_REP)
    sv = _pack([g[n] for n in small_names] + [loss_part[0:1, 0:1]], 8, 128, F32)
    s_send, s_recv, sv, s_land, s_token = _small_start(sv, tok, name="small_start")

    grads, delta, new_m, new_v = {}, {}, {}, {}

    def update(wn, dep):
        gl = [comm.reduced[name] for name, n2, _ in MATS if n2 == wn]
        shp = wl[wn].shape
        three = (len(gl),) + gl[0].shape
        flip = shp[-1] % 128 != 0
        view = (lambda t: t.reshape(three).transpose(0, 2, 1)) if flip else (lambda t: t.reshape(three))
        back = (lambda t: t.transpose(0, 2, 1).reshape(shp)) if flip else (lambda t: t.reshape(shp))
        if flip:
            gl = [t.T for t in gl]
        d, mn, vn, go = _adamw(view(wl[wn]), gl, view(ml[wn]), view(vl[wn]), name="adamw_" + wn, dep=dep)
        grads[wn], delta[wn], new_m[wn], new_v[wn] = back(go), back(d), back(mn), back(vn)
        return d

    first = [update(wn, s_token) for wn in ("w_q", "w_o", "w_kv")]
    tok = comm.advance(first)
    second = [update(wn, tok) for wn in ("f_w_in", "f_w_down", "a_out_proj")]
    comm.advance(second)
    comm.advance(second)
    update("a_in_proj", None)
    done = first + second

    sv, s_land = _small_wait(s_send, s_recv, sv, s_land, done, name="small_wait")
    sred = _small_sum(sv, s_land, jnp.reshape(2 * me + ci, (1,)).astype(jnp.int32)).reshape(-1)
    small_shapes = [g[n].shape for n in small_names] + [(1,)]
    sg = dict(zip(small_names + ["loss"], _unpack(sred, small_shapes)))
    loss = sg["loss"].reshape(())
    g_small = {}
    for n, ax in SMALL_CUT:
        size = wl[n].shape[ax]
        g_small[n] = lax.dynamic_slice_in_dim(sg[n], me * size, size, axis=ax)
    for n in SMALL_REP:
        g_small[n] = sg[n].reshape(wl[n].shape)

    pk = lambda d: _pack([d[n] for n in small_names], 8, 128, F32)[None]
    d, mn, vn, _ = _adamw(pk(wl), [pk(g_small)[0]], pk(ml), pk(vl), name="adamw_small")
    shapes = [wl[n].shape for n in small_names]
    for n, dd, mm, vv in zip(small_names, _unpack(d.reshape(-1), shapes), _unpack(mn.reshape(-1), shapes),
                             _unpack(vn.reshape(-1), shapes)):
        grads[n], delta[n], new_m[n], new_v[n] = g_small[n], dd, mm, vv

    return (loss, dx0[None], *[grads[n] for n in WEIGHTS], *[delta[n] for n in WEIGHTS],
            *[new_m[n] for n in WEIGHTS], *[new_v[n] for n in WEIGHTS])
```

```python
import math

import jax
import jax.numpy as jnp
from jax import lax
from jax.experimental import pallas as pl
from jax.experimental.pallas import tpu as pltpu

F32 = jnp.float32
BF16 = jnp.bfloat16

EPS = 1e-5
CHUNK = 256
WINDOW = 128
HEAD = 64
SSM_HEADS = 32
SSM_GROUPS = 8
SSM_STATE = 128
ATT_KV = 4
ATT_G = 4
ROPE_THETA = 10000.0
NEG = -1e30
N_CHIPS = 4
VMEM_LIMIT = 56 * 1024 * 1024

ADAM_LR, ADAM_B1, ADAM_B2, ADAM_EPS, ADAM_WD, ADAM_STEP = 0.001, 0.9, 0.999, 1e-08, 0.01, 10


def _cp(n_axes):
    return pltpu.CompilerParams(dimension_semantics=("arbitrary",) * n_axes, vmem_limit_bytes=VMEM_LIMIT)


def _pick(dim, prefs):
    for p in prefs:
        if dim % p == 0:
            return p
    return dim


def _iota(shape, dim):
    return lax.broadcasted_iota(jnp.int32, shape, dim)


def _dot(a, b, ca=1, cb=0):
    return lax.dot_general(a, b, (((ca,), (cb,)), ((), ())), preferred_element_type=F32)


def _dot3(x, ind):
    h = x.astype(BF16)
    r = x - h.astype(F32)
    m = r.astype(BF16)
    lo = (r - m.astype(F32)).astype(BF16)
    return _dot(h, ind) + _dot(m, ind) + _dot(lo, ind)


def _sigmoid(x):
    return jax.nn.sigmoid(x)


def _mm(a, b, *, name, ta=False, tb=False, bias=None, res=None, out_dtype=F32, b_koff=0, tm=None, tn=None, tk=None,
        dims=None, a_spec=None, b_spec=None, o_spec=None, o_shape=None, dep=None, more=(), target=None,
        rms=None, rms_colsum=False):
    if dims is not None:
        M, N, K = dims
    else:
        if ta:
            K, M = a.shape
        else:
            M, K = a.shape
        N = b.shape[0] if tb else b.shape[1]
    tm = tm or _pick(M, (1024, 1408, 512, 256, 128))
    tn = tn or _pick(N, (512, 1408, 256, 128))
    tk = tk or (K if K <= 2048 else _pick(K, (2048, 1408, 1024, 512)))
    assert M % tm == 0 and N % tn == 0 and K % tk == 0 and b_koff % tk == 0
    nk = K // tk
    kb0 = b_koff // tk
    has_bias, has_res = bias is not None, res is not None

    def body(*refs):
        a_ref, b_ref = refs[0], refs[1]
        pos = 2
        bias_ref = res_ref = acc_ref = None
        if has_bias:
            bias_ref = refs[pos]
            pos += 1
        if has_res:
            res_ref = refs[pos]
            pos += 1
        if dep is not None:
            pos += 1
        extra = refs[pos:pos + 2 * len(more)]
        pos += 2 * len(more)
        tgt_ref = lp_ref = rx_ref = rg_ref = rd_ref = dg_ref = cs_ref = None
        if target is not None:
            tgt_ref = refs[pos]
            pos += 1
        if rms is not None:
            rx_ref, rg_ref, rd_ref = refs[pos:pos + 3]
            pos += 3
        o_ref = refs[pos]
        pos += 1
        if target is not None:
            lp_ref = refs[pos]
            pos += 1
        if rms is not None:
            dg_ref = refs[pos]
            pos += 1
            if rms_colsum:
                cs_ref = refs[pos]
                pos += 1
        if nk > 1:
            acc_ref = refs[pos]
        part = _dot(a_ref[...].astype(BF16), b_ref[...].astype(BF16), 0 if ta else 1, 1 if tb else 0)
        for q in range(len(more)):
            part = part + _dot(extra[2 * q][...].astype(BF16), extra[2 * q + 1][...].astype(BF16),
                               0 if ta else 1, 1 if tb else 0)

        def finish(acc):
            if has_bias:
                acc = acc + bias_ref[...]
            if has_res:
                acc = acc + res_ref[...]
            if target is not None:
                err = acc - tgt_ref[...]
                acc = err * (1.0 / N)
                part_loss = jnp.sum(jnp.sum(err * err, axis=1, keepdims=True), axis=0, keepdims=True) * (0.5 / N)
                first = (pl.program_id(0) == 0) & (pl.program_id(1) == 0)

                @pl.when(first)
                def _():
                    lp_ref[...] = jnp.broadcast_to(part_loss, lp_ref.shape)

                @pl.when(jnp.logical_not(first))
                def _():
                    lp_ref[...] += jnp.broadcast_to(part_loss, lp_ref.shape)

            if rms is not None:
                xv = rx_ref[...]
                r = lax.rsqrt(jnp.mean(xv * xv, axis=-1, keepdims=True) + EPS)
                xh = xv * r
                dxh = acc * rg_ref[...]
                dg_part = jnp.sum(acc * xh, axis=0, keepdims=True)
                acc = rd_ref[...] + r * (dxh - xh * jnp.mean(dxh * xh, axis=-1, keepdims=True))
                cs_part = jnp.sum(acc, axis=0, keepdims=True) if rms_colsum else None
                first_rows = pl.program_id(0) == 0

                @pl.when(first_rows)
                def _():
                    dg_ref[...] = dg_part
                    if rms_colsum:
                        cs_ref[...] = cs_part

                @pl.when(jnp.logical_not(first_rows))
                def _():
                    dg_ref[...] += dg_part
                    if rms_colsum:
                        cs_ref[...] += cs_part

            o_ref[...] = acc.astype(out_dtype)

        if nk == 1:
            finish(part)
        else:
            k = pl.program_id(2)

            @pl.when(k == 0)
            def _():
                acc_ref[...] = part

            @pl.when(k > 0)
            def _():
                acc_ref[...] += part

            @pl.when(k == nk - 1)
            def _():
                finish(acc_ref[...])

    if a_spec is None:
        a_spec = pl.BlockSpec((tk, tm), lambda i, j, k: (k, i)) if ta else pl.BlockSpec((tm, tk), lambda i, j, k: (i, k))
    if b_spec is None:
        b_spec = (pl.BlockSpec((tn, tk), lambda i, j, k: (j, k + kb0)) if tb
                  else pl.BlockSpec((tk, tn), lambda i, j, k: (k + kb0, j)))
    if o_spec is None:
        o_spec = pl.BlockSpec((tm, tn), lambda i, j, k: (i, j))
    in_specs, args = [a_spec, b_spec], [a, b]
    if has_bias:
        in_specs.append(pl.BlockSpec((1, tn), lambda i, j, k: (0, j)))
        args.append(bias)
    if has_res:
        in_specs.append(pl.BlockSpec((tm, tn), lambda i, j, k: (i, j)))
        args.append(res)
    if dep is not None:
        in_specs.append(pl.BlockSpec(memory_space=pl.ANY))
        args.append(dep)
    for piece in more:
        a2, sa, b2, sb = piece if len(piece) == 4 else (a, piece[0], b, piece[1])
        in_specs += [sa, sb]
        args += [a2, b2]
    out_specs, out_shape = [o_spec], [jax.ShapeDtypeStruct(o_shape or (M, N), out_dtype)]
    if target is not None:
        in_specs.append(pl.BlockSpec((tm, tn), lambda i, j, k: (i, j)))
        args.append(target)
        out_specs.append(pl.BlockSpec((8, 128), lambda i, j, k: (0, 0)))
        out_shape.append(jax.ShapeDtypeStruct((8, 128), F32))
    if rms is not None:
        assert tn == N and nk == 1
        row, vec = pl.BlockSpec((tm, N), lambda i, j, k: (i, 0)), pl.BlockSpec((1, N), lambda i, j, k: (0, 0))
        in_specs += [row, vec, row]
        args += list(rms)
        out_specs += [vec] * (2 if rms_colsum else 1)
        out_shape += [jax.ShapeDtypeStruct((1, N), F32)] * (2 if rms_colsum else 1)
    if len(out_specs) == 1:
        out_specs, out_shape = out_specs[0], out_shape[0]
    return pl.pallas_call(
        body, name=name, grid=(M // tm, N // tn, nk), in_specs=in_specs, out_specs=out_specs, out_shape=out_shape,
        scratch_shapes=[pltpu.VMEM((tm, tn), F32)] if nk > 1 else [],
        compiler_params=_cp(3),
    )(*args)


def _norm_mm(x, gain, b, *, name, bias=None, N=None, tn=None, b_spec=None, dep=None):
    M, K = x.shape
    N = N or b.shape[1]
    tm = _pick(M, (1024, 512, 256))
    tn = tn or _pick(N, (512, 1408, 256, 128))
    has_bias = bias is not None

    def body(*refs):
        x_ref, g_ref, b_ref = refs[:3]
        pos = 3 + (1 if has_bias else 0) + (0 if dep is None else 1)
        o_ref, h_ref = refs[pos], refs[pos + 1]

        @pl.when(pl.program_id(1) == 0)
        def _():
            xv = x_ref[...]
            h_ref[...] = (xv * lax.rsqrt(jnp.mean(xv * xv, axis=-1, keepdims=True) + EPS) * g_ref[...]).astype(BF16)

        acc = _dot(h_ref[...], b_ref[...].astype(BF16))
        if has_bias:
            acc = acc + refs[3][...]
        o_ref[...] = acc

    in_specs = [pl.BlockSpec((tm, K), lambda i, j: (i, 0)), pl.BlockSpec((1, K), lambda i, j: (0, 0)),
                b_spec or pl.BlockSpec((K, tn), lambda i, j: (0, j))]
    args = [x, gain, b]
    if has_bias:
        in_specs.append(pl.BlockSpec((1, tn), lambda i, j: (0, j)))
        args.append(bias)
    if dep is not None:
        in_specs.append(pl.BlockSpec(memory_space=pl.ANY))
        args.append(dep)
    return pl.pallas_call(
        body, name=name, grid=(M // tm, N // tn), in_specs=in_specs,
        out_specs=[pl.BlockSpec((tm, tn), lambda i, j: (i, j)), pl.BlockSpec((tm, K), lambda i, j: (i, 0))],
        out_shape=[jax.ShapeDtypeStruct((M, N), F32), jax.ShapeDtypeStruct((M, K), BF16)], compiler_params=_cp(2),
    )(*args)


def _rms_bwd(x, gains, dhs, dres, *, name, tr=256, want_colsum=False):
    S, D = x.shape
    n = len(gains)
    steps = S // tr

    def body(*refs):
        x_ref = refs[0]
        g_refs = refs[1:1 + n]
        dh_refs = refs[1 + n:1 + 2 * n]
        dres_ref = refs[1 + 2 * n]
        dx_ref = refs[2 + 2 * n]
        dg_refs = refs[3 + 2 * n:3 + 3 * n]
        cs_ref = refs[3 + 3 * n] if want_colsum else None
        i = pl.program_id(0)
        xv = x_ref[...]
        r = lax.rsqrt(jnp.mean(xv * xv, axis=-1, keepdims=True) + EPS)
        xh = xv * r
        dx = dres_ref[...]
        for q in range(n):
            dh = dh_refs[q][...]
            dxh = dh * g_refs[q][...]
            dx = dx + r * (dxh - xh * jnp.mean(dxh * xh, axis=-1, keepdims=True))
            part = jnp.sum(dh * xh, axis=0, keepdims=True)

            @pl.when(i == 0)
            def _():
                dg_refs[q][...] = part

            @pl.when(i > 0)
            def _():
                dg_refs[q][...] += part

        dx_ref[...] = dx
        if want_colsum:
            cpart = jnp.sum(dx, axis=0, keepdims=True)

            @pl.when(i == 0)
            def _():
                cs_ref[...] = cpart

            @pl.when(i > 0)
            def _():
                cs_ref[...] += cpart

    row = pl.BlockSpec((tr, D), lambda i: (i, 0))
    vec = pl.BlockSpec((1, D), lambda i: (0, 0))
    n_vec_out = n + (1 if want_colsum else 0)
    outs = pl.pallas_call(
        body, name=name, grid=(steps,), in_specs=[row] + [vec] * n + [row] * n + [row],
        out_specs=[row] + [vec] * n_vec_out,
        out_shape=[jax.ShapeDtypeStruct((S, D), F32)] + [jax.ShapeDtypeStruct((1, D), F32)] * n_vec_out,
        compiler_params=_cp(1),
    )(x, *gains, *dhs, dres)
    return outs


def _colsum(x, *, name, tr=256):
    S, D = x.shape

    def body(x_ref, o_ref):
        i = pl.program_id(0)
        part = jnp.sum(x_ref[...].astype(F32), axis=0, keepdims=True)

        @pl.when(i == 0)
        def _():
            o_ref[...] = part

        @pl.when(i > 0)
        def _():
            o_ref[...] += part

    return pl.pallas_call(
        body, name=name, grid=(S // tr,), in_specs=[pl.BlockSpec((tr, D), lambda i: (i, 0))],
        out_specs=pl.BlockSpec((1, D), lambda i: (0, 0)), out_shape=jax.ShapeDtypeStruct((1, D), F32),
        compiler_params=_cp(1),
    )(x)


STRIP = 64
HALO = 8


def _strips(S, tc):
    return [(r0, slice(l0, l0 + 128)) for l0 in range(0, tc, 128) for r0 in range(S - STRIP, -1, -STRIP)]


def _with_halo(ref, r0, ls):
    if r0 == 0:
        return jnp.concatenate([jnp.zeros((HALO, 128), F32), ref[0:STRIP, ls]], axis=0)
    return ref[r0 - HALO:r0 + STRIP, ls]


def _conv_strip(xw, w_ref, b_ref, ls, width):
    acc = b_ref[:, ls] + w_ref[pl.ds(width - 1, 1), ls] * xw[HALO:]
    shifted = []
    for s in range(1, width):
        xs = pltpu.roll(xw, s, axis=0)[HALO:]
        shifted.append(xs)
        acc = acc + w_ref[pl.ds(width - 1 - s, 1), ls] * xs
    return acc, shifted


def _conv_strip_back(dacc, after, xc, shifted, w_ref, ls, width):
    ext = jnp.concatenate([dacc, after], axis=0)
    dx = w_ref[pl.ds(width - 1, 1), ls] * dacc
    dws = [None] * width
    dws[width - 1] = jnp.sum(dacc * xc, axis=0, keepdims=True)
    for s in range(1, width):
        dx = dx + w_ref[pl.ds(width - 1 - s, 1), ls] * pltpu.roll(ext, STRIP + HALO - s, axis=0)[:STRIP]
        dws[width - 1 - s] = jnp.sum(dacc * shifted[s - 1], axis=0, keepdims=True)
    return dx, dws, jnp.sum(dacc, axis=0, keepdims=True)


def _conv_back_block(S, tc, width, w_ref, b_ref, x_ref, dacc_of, dx_store, dw_ref, db_ref):
    for l0 in range(0, tc, 128):
        ls = slice(l0, l0 + 128)
        after = jnp.zeros((HALO, 128), F32)
        tot = None
        for r0 in range(S - STRIP, -1, -STRIP):
            xw = _with_halo(x_ref, r0, ls)
            acc, shifted = _conv_strip(xw, w_ref, b_ref, ls, width)
            dacc = dacc_of(r0, ls, acc, _sigmoid(acc))
            dx, dws, db = _conv_strip_back(dacc, after, xw[HALO:], shifted, w_ref, ls, width)
            dx_store(r0, ls, dx)
            after = dacc[:HALO]
            part = dws + [db]
            tot = part if tot is None else [p + q for p, q in zip(tot, part)]
        for k in range(width):
            dw_ref[pl.ds(k, 1), ls] = tot[k]
        db_ref[:, ls] = tot[width]


def _conv_silu_fwd(xin, col0, C, w, b, *, name, tc=512):
    S = xin.shape[0]
    width = w.shape[0]
    off = col0 // tc

    def body(x_ref, w_ref, b_ref, o_ref):
        for r0, ls in _strips(S, tc):
            acc, _ = _conv_strip(_with_halo(x_ref, r0, ls), w_ref, b_ref, ls, width)
            o_ref[r0:r0 + STRIP, ls] = acc * _sigmoid(acc)

    return pl.pallas_call(
        body, name=name, grid=(C // tc,),
        in_specs=[pl.BlockSpec((S, tc), lambda j: (0, j + off)), pl.BlockSpec((width, tc), lambda j: (0, j)),
                  pl.BlockSpec((1, tc), lambda j: (0, j))],
        out_specs=pl.BlockSpec((S, tc), lambda j: (0, j)), out_shape=jax.ShapeDtypeStruct((S, C), F32),
        compiler_params=_cp(1),
    )(xin, w, b)


def _conv_silu_bwd(xin, col0, C, w, b, douts, *, name, tc=256):
    S = xin.shape[0]
    width = w.shape[0]
    off = col0 // tc
    nd = len(douts)
    ranges = [(o // tc, (o + d.shape[1]) // tc) for d, o in douts]

    def body(*refs):
        x_ref, w_ref, b_ref = refs[0], refs[1], refs[2]
        d_refs = refs[3:3 + nd]
        dx_ref, dw_ref, db_ref = refs[3 + nd], refs[4 + nd], refs[5 + nd]
        j = pl.program_id(0)

        def dacc_of(r0, ls, acc, sg):
            dout = jnp.zeros((STRIP, 128), F32)
            for q in range(nd):
                lo, hi = ranges[q]
                dout = dout + jnp.where((j >= lo) & (j < hi), d_refs[q][r0:r0 + STRIP, ls], 0.0)
            return dout * (sg * (1.0 + acc * (1.0 - sg)))

        def dx_store(r0, ls, dx):
            dx_ref[r0:r0 + STRIP, ls] = dx.astype(BF16)

        _conv_back_block(S, tc, width, w_ref, b_ref, x_ref, dacc_of, dx_store, dw_ref, db_ref)

    d_specs = [pl.BlockSpec((S, tc), (lambda j, lo=lo, hi=hi: (0, jnp.clip(j - lo, 0, hi - lo - 1)))) for lo, hi in ranges]
    return pl.pallas_call(
        body, name=name, grid=(C // tc,),
        in_specs=[pl.BlockSpec((S, tc), lambda j: (0, j + off)), pl.BlockSpec((width, tc), lambda j: (0, j)),
                  pl.BlockSpec((1, tc), lambda j: (0, j))] + d_specs,
        out_specs=[pl.BlockSpec((S, tc), lambda j: (0, j)), pl.BlockSpec((width, tc), lambda j: (0, j)),
                   pl.BlockSpec((1, tc), lambda j: (0, j))],
        out_shape=[jax.ShapeDtypeStruct((S, C), BF16), jax.ShapeDtypeStruct((width, C), F32),
                   jax.ShapeDtypeStruct((1, C), F32)],
        compiler_params=_cp(1),
    )(xin, w, b, *[d for d, _ in douts])


def _ffn_act_fwd(u, w, b, *, name, tc=256):
    S, F2 = u.shape
    Fd = F2 // 2
    width = w.shape[0]
    nb = Fd // tc

    def body(g_ref, v_ref, w_ref, b_ref, o_ref):
        for r0, ls in _strips(S, tc):
            acc, _ = _conv_strip(_with_halo(g_ref, r0, ls), w_ref, b_ref, ls, width)
            o_ref[r0:r0 + STRIP, ls] = (acc * _sigmoid(acc) * v_ref[r0:r0 + STRIP, ls]).astype(BF16)

    return pl.pallas_call(
        body, name=name, grid=(nb,),
        in_specs=[pl.BlockSpec((S, tc), lambda j: (0, j)), pl.BlockSpec((S, tc), lambda j: (0, j + nb)),
                  pl.BlockSpec((width, tc), lambda j: (0, j)), pl.BlockSpec((1, tc), lambda j: (0, j))],
        out_specs=pl.BlockSpec((S, tc), lambda j: (0, j)), out_shape=jax.ShapeDtypeStruct((S, Fd), BF16),
        compiler_params=_cp(1),
    )(u, u, w, b)


def _ffn_act_bwd(u, w, b, da, *, name, tc=256):
    S, F2 = u.shape
    Fd = F2 // 2
    width = w.shape[0]
    nb = Fd // tc

    def body(g_ref, v_ref, w_ref, b_ref, da_ref, du_ref, dw_ref, db_ref, a_ref):
        def dacc_of(r0, ls, acc, sg):
            rs = slice(r0, r0 + STRIP)
            dav, val, silu = da_ref[rs, ls], v_ref[rs, ls], acc * sg
            a_ref[rs, ls] = (silu * val).astype(BF16)
            du_ref[1, rs, ls] = (dav * silu).astype(BF16)
            return dav * val * (sg * (1.0 + acc * (1.0 - sg)))

        def dx_store(r0, ls, dx):
            du_ref[0, r0:r0 + STRIP, ls] = dx.astype(BF16)

        _conv_back_block(S, tc, width, w_ref, b_ref, g_ref, dacc_of, dx_store, dw_ref, db_ref)

    blk = pl.BlockSpec((S, tc), lambda j: (0, j))
    return pl.pallas_call(
        body, name=name, grid=(nb,),
        in_specs=[blk, pl.BlockSpec((S, tc), lambda j: (0, j + nb)), pl.BlockSpec((width, tc), lambda j: (0, j)),
                  pl.BlockSpec((1, tc), lambda j: (0, j)), blk],
        out_specs=[pl.BlockSpec((2, S, tc), lambda j: (0, 0, j)), pl.BlockSpec((width, tc), lambda j: (0, j)),
                   pl.BlockSpec((1, tc), lambda j: (0, j)), blk],
        out_shape=[jax.ShapeDtypeStruct((2, S, Fd), BF16),
                   jax.ShapeDtypeStruct((width, Fd), F32), jax.ShapeDtypeStruct((1, Fd), F32),
                   jax.ShapeDtypeStruct((S, Fd), BF16)],
        compiler_params=_cp(1),
    )(u, u, w, b, da)


def _ssd_prep(dtr, dt_bias, a_log, *, name="ssd_prep"):
    S = dtr.shape[0]

    def body(d_ref, b_ref, al_ref, dt_ref, ac_ref, sg_ref, act_ref):
        lane = _iota((CHUNK, 128), 1)
        valid = lane < SSM_HEADS
        z = d_ref[...] + b_ref[...]
        dt = jnp.where(valid, jnp.maximum(z, 0.0) + jnp.log(1.0 + jnp.exp(-jnp.abs(z))), 0.0)
        a = dt * (-jnp.exp(al_ref[...]))
        row = _iota((CHUNK, 128), 0)
        k = 1
        while k < CHUNK:
            a = a + jnp.where(row >= k, pltpu.roll(a, k, axis=0), 0.0)
            k *= 2
        sg = jnp.where(valid, _sigmoid(z), 0.0)
        for arr, ref in ((dt, dt_ref), (a, ac_ref), (sg, sg_ref)):
            for g in range(SSM_GROUPS):
                ref[g] = jnp.where(lane < 4, arr if g == 0 else pltpu.roll(arr, 128 - 4 * g, axis=1), 0.0)
        act_ref[...] = a.T[:SSM_HEADS, :]

    blk = pl.BlockSpec((CHUNK, 128), lambda i: (i, 0))
    vec = pl.BlockSpec((1, 128), lambda i: (0, 0))
    grp = pl.BlockSpec((SSM_GROUPS, CHUNK, 128), lambda i: (0, i, 0))
    return pl.pallas_call(
        body, name=name, grid=(S // CHUNK,), in_specs=[blk, vec, vec],
        out_specs=[grp, grp, grp, pl.BlockSpec((SSM_HEADS, CHUNK), lambda i: (0, i))],
        out_shape=[jax.ShapeDtypeStruct((SSM_GROUPS, S, 128), F32)] * 3 + [jax.ShapeDtypeStruct((SSM_HEADS, S), F32)],
        compiler_params=_cp(1),
    )(dtr, dt_bias, a_log)


SSD_GPS = 4


def _expand4(v, lanes):
    out = jnp.broadcast_to(v[:, 3:4], lanes.shape)
    for hh in (2, 1, 0):
        out = jnp.where(lanes < 64 * (hh + 1), v[:, hh:hh + 1], out)
    return out


def _ssd_fwd(xbc, dt_g, ac_g, ac_t, *, name="ssd_fwd", dep=None):
    S = xbc.shape[0]
    nc = S // CHUNK
    Lc = CHUNK

    def body(x_ref, b_ref, c_ref, dt_ref, ac_ref, act_ref, *rest):
        y_ref, st_out_ref, st_ref = rest[-3:]
        g2 = pl.program_id(0)
        c = pl.program_id(1)

        @pl.when(c == 0)
        def _():
            st_ref[...] = jnp.zeros_like(st_ref)

        causal = _iota((Lc, Lc), 0) >= _iota((Lc, Lc), 1)
        lane256 = _iota((Lc, 256), 1)
        lane128 = _iota((Lc, 128), 1)
        row128 = _iota((128, 128), 0)
        for gg in range(SSD_GPS):
            g = SSD_GPS * g2 + gg
            bv = b_ref[:, 128 * gg:128 * (gg + 1)]
            cbf = c_ref[:, 128 * gg:128 * (gg + 1)].astype(BF16)
            cb = _dot(cbf, bv.astype(BF16), 1, 1)
            dtg, acg = dt_ref[gg], ac_ref[gg]
            ac_last = ac_ref[gg, pl.ds(Lc - 1, 1), :]
            dt4 = _expand4(dtg, lane256)
            ac4 = _expand4(acg, lane256)
            e4 = jnp.exp(ac4)
            xdb = (x_ref[:, 256 * gg:256 * (gg + 1)] * dt4).astype(BF16)
            st_out_ref[gg] = st_ref[gg]
            for p in range(2):
                xd_p = xdb[:, 128 * p:128 * (p + 1)]
                st_p = st_ref[gg, p]
                ys, sn, cds = [], [], []
                for q in range(2):
                    hh = 2 * p + q
                    a_col = acg[:, hh:hh + 1]
                    a_row = act_ref[pl.ds(4 * g + hh, 1), :]
                    dec = jnp.exp(jnp.where(causal, a_col - a_row, NEG))
                    w = (cb * dec).astype(BF16)
                    ys.append(_dot(w, xd_p))
                    al = ac_last[:, hh:hh + 1]
                    dte = jnp.exp(al - a_col)
                    sn.append(_dot(xd_p, (bv * dte).astype(BF16), 0, 0))
                    cds.append(jnp.exp(al))
                y_diag = jnp.where(lane128 < 64, ys[0], ys[1])
                y_off = _dot(cbf, st_p.astype(BF16), 1, 1) * e4[:, 128 * p:128 * (p + 1)]
                y_ref[:, 256 * gg + 128 * p:256 * gg + 128 * (p + 1)] = y_diag + y_off
                st_ref[gg, p] = jnp.where(row128 < 64, st_p * cds[0] + sn[0], st_p * cds[1] + sn[1])

    G = SSD_GPS
    per_g = lambda g, c: (g, c, 0)
    return pl.pallas_call(
        body, name=name, grid=(SSM_GROUPS // G, nc),
        in_specs=[pl.BlockSpec((Lc, 256 * G), lambda g, c: (c, g)),
                  pl.BlockSpec((Lc, 128 * G), lambda g, c: (c, 16 // G + g)),
                  pl.BlockSpec((Lc, 128 * G), lambda g, c: (c, 24 // G + g)),
                  pl.BlockSpec((G, Lc, 128), per_g), pl.BlockSpec((G, Lc, 128), per_g),
                  pl.BlockSpec((SSM_HEADS, Lc), lambda g, c: (0, c))] + ([] if dep is None else [pl.BlockSpec(memory_space=pl.ANY)]),
        out_specs=[pl.BlockSpec((Lc, 256 * G), lambda g, c: (c, g)),
                   pl.BlockSpec((G, None, 2, 128, 128), lambda g, c: (g, c, 0, 0, 0))],
        out_shape=[jax.ShapeDtypeStruct((S, 2048), F32), jax.ShapeDtypeStruct((SSM_GROUPS, nc, 2, 128, 128), F32)],
        scratch_shapes=[pltpu.VMEM((G, 2, 128, 128), F32)], compiler_params=_cp(2),
    )(xbc, xbc, xbc, dt_g, ac_g, ac_t, *([] if dep is None else [dep]))


def _ssd_bwd(xbc, dt_g, ac_g, ac_t, states, dy, dexp, *, name="ssd_bwd", dep=None):
    S = xbc.shape[0]
    nc = S // CHUNK
    Lc = CHUNK

    def body(x_ref, b_ref, c_ref, dt_ref, ac_ref, act_ref, st_ref, dy_ref, d_ref, *rest):
        dx_ref, db_ref, dc_ref, dh_ref, ds_ref = rest[-5:]
        g2 = pl.program_id(0)
        cc = pl.program_id(1)

        @pl.when(cc == 0)
        def _():
            ds_ref[...] = jnp.zeros_like(ds_ref)

        causal = _iota((Lc, Lc), 0) >= _iota((Lc, Lc), 1)
        lane256 = _iota((Lc, 256), 1)
        lane128 = _iota((Lc, 128), 1)
        row128 = _iota((128, 128), 0)
        ind_rows = _iota((256, 128), 0) >> 6
        ind_cols = _iota((256, 128), 1)
        ind_a = (ind_rows == ind_cols).astype(BF16)
        ind_b = (ind_rows + 4 == ind_cols).astype(BF16)
        for gg in range(SSD_GPS):
            g = SSD_GPS * g2 + gg
            bv = b_ref[:, 128 * gg:128 * (gg + 1)]
            cv = c_ref[:, 128 * gg:128 * (gg + 1)]
            bbf, cbf = bv.astype(BF16), cv.astype(BF16)
            cb = _dot(cbf, bbf, 1, 1)
            dtg, acg = dt_ref[gg], ac_ref[gg]
            ac_last = ac_ref[gg, pl.ds(Lc - 1, 1), :]
            dt4 = _expand4(dtg, lane256)
            ac4 = _expand4(acg, lane256)
            acl4 = _expand4(ac_last, _iota((1, 256), 1))
            e4 = jnp.exp(ac4)
            dte4 = jnp.exp(acl4 - ac4)
            xv = x_ref[:, 256 * gg:256 * (gg + 1)]
            xd = xv * dt4
            xdb = xd.astype(BF16)
            dyv = dy_ref[:, 256 * gg:256 * (gg + 1)]
            dcb = jnp.zeros((Lc, Lc), F32)
            dc_acc = jnp.zeros((Lc, 128), F32)
            db_acc = jnp.zeros((Lc, 128), F32)
            u_parts, dxd_parts, ends = [], [], []
            for p in range(2):
                sl = slice(128 * p, 128 * (p + 1))
                xd_p, xdb_p, dy_p = xd[:, sl], xdb[:, sl], dyv[:, sl]
                dyb_p = dy_p.astype(BF16)
                e_p, dte_p = e4[:, sl], dte4[:, sl]
                sp = st_ref[gg, p]
                spb = sp.astype(BF16)
                dsn = ds_ref[gg, p]
                dsnb = dsn.astype(BF16)
                yds, dxds, cds = [], [], []
                for q in range(2):
                    hh = 2 * p + q
                    a_col = acg[:, hh:hh + 1]
                    a_row = act_ref[pl.ds(4 * g + hh, 1), :]
                    dec = jnp.exp(jnp.where(causal, a_col - a_row, NEG))
                    w = (cb * dec).astype(BF16)
                    head = (lane128 < 64) if q == 0 else (lane128 >= 64)
                    dym = jnp.where(head, dyb_p, jnp.zeros_like(dyb_p))
                    dw = _dot(dym, xdb_p, 1, 1)
                    dcb = dcb + dw * dec
                    yds.append(_dot(w, xdb_p))
                    dxds.append(_dot(w, dyb_p, 0, 0))
                    cds.append(jnp.exp(ac_last[:, hh:hh + 1]))
                y_diag = jnp.where(lane128 < 64, yds[0], yds[1])
                dxd_diag = jnp.where(lane128 < 64, dxds[0], dxds[1])
                y_off = _dot(cbf, spb, 1, 1) * e_p
                dgp = dy_p * e_p
                dgb = dgp.astype(BF16)
                dc_acc = dc_acc + _dot(dgb, spb)
                dsp = _dot(dgb, cbf, 0, 0)
                cd_col = jnp.where(row128[:, 0:1] < 64, cds[0], cds[1])
                qm = _dot(bbf, dsnb, 1, 1)
                dxd_state = dte_p * qm
                db_acc = db_acc + _dot((xd_p * dte_p).astype(BF16), dsnb)
                t_p = xd_p * dxd_state
                prod = dsn * sp
                e0 = jnp.sum(jnp.sum(jnp.where(row128 < 64, prod, 0.0), axis=1, keepdims=True), axis=0, keepdims=True)
                e1 = jnp.sum(jnp.sum(jnp.where(row128 >= 64, prod, 0.0), axis=1, keepdims=True), axis=0, keepdims=True)
                tcol = jnp.sum(t_p, axis=0, keepdims=True)
                lane1 = _iota((1, 128), 1)
                t0 = jnp.sum(jnp.where(lane1 < 64, tcol, 0.0), axis=1, keepdims=True)
                t1 = jnp.sum(jnp.where(lane1 >= 64, tcol, 0.0), axis=1, keepdims=True)
                ends.append(e0 * cds[0] + t0)
                ends.append(e1 * cds[1] + t1)
                ds_ref[gg, p] = dsn * cd_col + dsp
                u_parts.append(dyb_p.astype(F32) * y_diag - xdb_p.astype(F32) * dxd_diag + dy_p * y_off - t_p)
                dxd_parts.append(dxd_diag + dxd_state)
            dxd = jnp.concatenate(dxd_parts, axis=1)
            u_all = jnp.concatenate(u_parts, axis=1)
            dx_ref[:, 256 * gg:256 * (gg + 1)] = dxd * dt4 + dyv * d_ref[:, 256 * gg:256 * (gg + 1)]
            dcbb = dcb.astype(BF16)
            dc_ref[:, 128 * gg:128 * (gg + 1)] = dc_acc + _dot(dcbb, bbf)
            db_ref[:, 128 * gg:128 * (gg + 1)] = db_acc + _dot(dcbb, cbf, 0, 0)
            lane = _iota((Lc, 128), 1)
            endv = jnp.zeros((Lc, 128), F32)
            for hh in range(4):
                endv = jnp.where(lane == 8 + hh, ends[hh], endv)
            dh_ref[gg] = _dot3(dxd * xv, ind_a) + _dot3(u_all, ind_b) + endv

    G = SSD_GPS
    rev = lambda c: nc - 1 - c
    per_g = lambda g, c: (g, rev(c), 0)
    return pl.pallas_call(
        body, name=name, grid=(SSM_GROUPS // G, nc),
        in_specs=[pl.BlockSpec((Lc, 256 * G), lambda g, c: (rev(c), g)),
                  pl.BlockSpec((Lc, 128 * G), lambda g, c: (rev(c), 16 // G + g)),
                  pl.BlockSpec((Lc, 128 * G), lambda g, c: (rev(c), 24 // G + g)),
                  pl.BlockSpec((G, Lc, 128), per_g), pl.BlockSpec((G, Lc, 128), per_g),
                  pl.BlockSpec((SSM_HEADS, Lc), lambda g, c: (0, rev(c))),
                  pl.BlockSpec((G, None, 2, 128, 128), lambda g, c: (g, rev(c), 0, 0, 0)),
                  pl.BlockSpec((Lc, 256 * G), lambda g, c: (rev(c), g)),
                  pl.BlockSpec((1, 256 * G), lambda g, c: (0, g))] + ([] if dep is None else [pl.BlockSpec(memory_space=pl.ANY)]),
        out_specs=[pl.BlockSpec((Lc, 256 * G), lambda g, c: (rev(c), g)),
                   pl.BlockSpec((Lc, 128 * G), lambda g, c: (rev(c), g)),
                   pl.BlockSpec((Lc, 128 * G), lambda g, c: (rev(c), g)),
                   pl.BlockSpec((G, Lc, 128), per_g)],
        out_shape=[jax.ShapeDtypeStruct((S, 2048), F32), jax.ShapeDtypeStruct((S, 1024), F32),
                   jax.ShapeDtypeStruct((S, 1024), F32), jax.ShapeDtypeStruct((SSM_GROUPS, S, 128), F32)],
        scratch_shapes=[pltpu.VMEM((G, 2, 128, 128), F32)], compiler_params=_cp(2),
    )(xbc, xbc, xbc, dt_g, ac_g, ac_t, states, dy, dexp, *([] if dep is None else [dep]))


def _ssd_post(dhead, dt_g, sg_g, alog_g, *, name="ssd_post"):
    S = dhead.shape[1]
    nc = S // CHUNK
    Lc = CHUNK

    def body(dh_ref, dt_ref, sg_ref, al_ref, o_ref, s_ref):
        @pl.when(pl.program_id(0) == 0)
        def _():
            s_ref[...] = jnp.zeros_like(s_ref)

        lane = _iota((Lc, 128), 1)
        row = _iota((Lc, 128), 0)
        row8 = _iota((8, 128), 0)
        out = jnp.zeros((Lc, 128), F32)
        for g in range(SSM_GROUPS):
            dh = dh_ref[g]
            a_neg = -jnp.exp(al_ref[g])
            dac = jnp.where(lane < 4, pltpu.roll(dh, 124, axis=1), 0.0)
            end = jnp.where(lane < 4, pltpu.roll(dh, 120, axis=1), 0.0)
            k = 1
            while k < Lc:
                dac = dac + jnp.where(row < Lc - k, pltpu.roll(dac, Lc - k, axis=0), 0.0)
                k *= 2
            da = dac + end
            ddt = jnp.where(lane < 4, da * a_neg + dh, 0.0)
            ddtr = ddt * sg_ref[g]
            out = out + (ddtr if g == 0 else pltpu.roll(ddtr, 4 * g, axis=1))
            dal = jnp.sum(da * dt_ref[g], axis=0, keepdims=True) * a_neg
            dbias = jnp.sum(ddtr, axis=0, keepdims=True)
            part = jnp.where(row8 == 0, dal, jnp.where(row8 == 1, dbias, 0.0))
            s_ref[g] += part
        o_ref[...] = out.astype(BF16)

    grp = pl.BlockSpec((SSM_GROUPS, Lc, 128), lambda c: (0, c, 0))
    whole = lambda r: pl.BlockSpec((SSM_GROUPS, r, 128), lambda c: (0, 0, 0))
    return pl.pallas_call(
        body, name=name, grid=(nc,), in_specs=[grp, grp, grp, whole(1)],
        out_specs=[pl.BlockSpec((Lc, 128), lambda c: (c, 0)), whole(8)],
        out_shape=[jax.ShapeDtypeStruct((S, 128), BF16), jax.ShapeDtypeStruct((SSM_GROUPS, 8, 128), F32)],
        compiler_params=_cp(1),
    )(dhead, dt_g, sg_g, alog_g)


def _gate_fwd(y, xbc, zx, dexp, gn, *, name="gate_fwd", tr=256, dep=None):
    S = y.shape[0]
    W = 2048
    gw = W // SSM_GROUPS

    def body(y_ref, x_ref, z_ref, d_ref, g_ref, *rest):
        o_ref = rest[-1]
        z = z_ref[...]
        u = (y_ref[...] + x_ref[...] * d_ref[...]) * (z * _sigmoid(z))
        gv = g_ref[...]
        for q in range(SSM_GROUPS):
            sl = slice(gw * q, gw * (q + 1))
            uq = u[:, sl]
            r = lax.rsqrt(jnp.mean(uq * uq, axis=-1, keepdims=True) + EPS)
            o_ref[:, sl] = (uq * r * gv[:, sl]).astype(BF16)

    row = pl.BlockSpec((tr, W), lambda i: (i, 0))
    vec = pl.BlockSpec((1, W), lambda i: (0, 0))
    return pl.pallas_call(
        body, name=name, grid=(S // tr,),
        in_specs=[row, row, row, vec, vec] + ([] if dep is None else [pl.BlockSpec(memory_space=pl.ANY)]), out_specs=row,
        out_shape=jax.ShapeDtypeStruct((S, W), BF16), compiler_params=_cp(1),
    )(y, xbc, zx, dexp, gn, *([] if dep is None else [dep]))


def _gate_bwd(y, xbc, zx, dexp, gn, dout, *, name="gate_bwd", tr=256):
    S = y.shape[0]
    W = 2048
    gw = W // SSM_GROUPS
    steps = S // tr

    def body(y_ref, x_ref, z_ref, d_ref, g_ref, do_ref, dy_ref, dz_ref, dg_ref, dd_ref, acc_ref):
        i = pl.program_id(0)

        @pl.when(i == 0)
        def _():
            acc_ref[...] = jnp.zeros_like(acc_ref)

        z = z_ref[...]
        sg = _sigmoid(z)
        sz = z * sg
        xs = x_ref[...]
        yt = y_ref[...] + xs * d_ref[...]
        u = yt * sz
        gv = g_ref[...]
        do = do_ref[...]
        dgs = []
        for q in range(SSM_GROUPS):
            sl = slice(gw * q, gw * (q + 1))
            uq = u[:, sl]
            r = lax.rsqrt(jnp.mean(uq * uq, axis=-1, keepdims=True) + EPS)
            uh = uq * r
            dq = do[:, sl]
            duh = dq * gv[:, sl]
            duq = r * (duh - uh * jnp.mean(duh * uh, axis=-1, keepdims=True))
            dgs.append(jnp.sum(dq * uh, axis=0, keepdims=True))
            dyt = duq * sz[:, sl]
            dy_ref[:, sl] = dyt
            dz_ref[:, sl] = (duq * yt[:, sl] * (sg[:, sl] * (1.0 + z[:, sl] * (1.0 - sg[:, sl])))).astype(BF16)
            acc_ref[:, sl] += jnp.sum(dyt * xs[:, sl], axis=0, keepdims=True)
        dg = jnp.concatenate(dgs, axis=1)

        @pl.when(i == 0)
        def _():
            dg_ref[...] = dg

        @pl.when(i > 0)
        def _():
            dg_ref[...] += dg

        @pl.when(i == steps - 1)
        def _():
            ind = ((_iota((W, 128), 0) >> 6) == _iota((W, 128), 1)).astype(BF16)
            dd_ref[...] = _dot3(jnp.broadcast_to(acc_ref[...], (8, W)), ind)[0:1, :]

    row = pl.BlockSpec((tr, W), lambda i: (i, 0))
    vec = pl.BlockSpec((1, W), lambda i: (0, 0))
    return pl.pallas_call(
        body, name=name, grid=(steps,), in_specs=[row, row, row, vec, vec, row],
        out_specs=[row, row, vec, pl.BlockSpec((1, 128), lambda i: (0, 0))],
        out_shape=[jax.ShapeDtypeStruct((S, W), F32), jax.ShapeDtypeStruct((S, W), BF16),
                   jax.ShapeDtypeStruct((1, W), F32), jax.ShapeDtypeStruct((1, 128), F32)],
        scratch_shapes=[pltpu.VMEM((1, W), F32)], compiler_params=_cp(1),
    )(y, xbc, zx, dexp, gn, dout)


def _rope_cs(posf, *, name="rope_tables", tr=256):
    S = posf.shape[0]

    def body(p_ref, c_ref, s_ref):
        j = (_iota((tr, 128), 1) & 31).astype(F32)
        ang = p_ref[...] * jnp.exp(j * (-math.log(ROPE_THETA) / 32.0))
        c_ref[...] = jnp.cos(ang)
        s_ref[...] = jnp.sin(ang)

    blk = pl.BlockSpec((tr, 128), lambda i: (i, 0))
    return pl.pallas_call(
        body, name=name, grid=(S // tr,), in_specs=[pl.BlockSpec((tr, 1), lambda i: (i, 0))], out_specs=[blk, blk],
        out_shape=[jax.ShapeDtypeStruct((S, 128), F32)] * 2, compiler_params=_cp(1),
    )(posf)


def _rope_tables(c_ref, s_ref, shape):
    reps = shape[1] // 128
    return jnp.tile(c_ref[...], (1, reps)), jnp.tile(s_ref[...], (1, reps)), (_iota(shape, 1) & 63) < 32


def _hn_inds(W):
    ind = ((_iota((W, 128), 0) >> 6) == _iota((W, 128), 1)).astype(BF16)
    ind_t = ((_iota((128, W), 1) >> 6) == _iota((128, W), 0)).astype(BF16)
    return ind, ind_t


def _hnrope_fwd(xin, col0, W, gain_w, rope, *, name, tr=256):
    S = xin.shape[0]
    off = col0 // W
    nh = W // HEAD

    def body(x_ref, g_ref, c_ref, s_ref, o_ref):
        x = x_ref[...]
        ind, ind_t = _hn_inds(W)
        r = lax.rsqrt(_dot3(x * x, ind) * (1.0 / HEAD) + EPS)
        xn = x * _dot3(r, ind_t) * g_ref[...]
        cs, sn, half = _rope_tables(c_ref, s_ref, (tr, W))
        rot = jnp.where(half, -pltpu.roll(xn, W - 32, axis=1), pltpu.roll(xn, 32, axis=1))
        out = (xn * cs + rot * sn).astype(BF16)
        for h in range(nh):
            o_ref[h] = out[:, HEAD * h:HEAD * (h + 1)]

    tab = pl.BlockSpec((tr, 128), lambda i: (i, 0))
    return pl.pallas_call(
        body, name=name, grid=(S // tr,),
        in_specs=[pl.BlockSpec((tr, W), lambda i: (i, off)), pl.BlockSpec((1, W), lambda i: (0, 0)), tab, tab],
        out_specs=pl.BlockSpec((nh, tr, HEAD), lambda i: (0, i, 0)), out_shape=jax.ShapeDtypeStruct((nh, S, HEAD), BF16),
        compiler_params=_cp(1),
    )(xin, gain_w, *rope)


def _hnrope_bwd(xin, col0, W, gain_w, rope, dout, *, name, tr=256):
    S = xin.shape[0]
    off = col0 // W
    steps = S // tr
    nh = W // HEAD

    def body(x_ref, g_ref, c_ref, s_ref, do_ref, dx_ref, cs_ref, dg_ref, acc_ref):
        i = pl.program_id(0)
        x = x_ref[...]
        ind, ind_t = _hn_inds(W)
        r = lax.rsqrt(_dot3(x * x, ind) * (1.0 / HEAD) + EPS)
        rw = _dot3(r, ind_t)
        xh = x * rw
        cs, sn, half = _rope_tables(c_ref, s_ref, (tr, W))
        do = jnp.concatenate([do_ref[h] for h in range(nh)], axis=1).astype(F32)
        gs = do * sn
        g1 = do * cs + jnp.where(half, pltpu.roll(gs, W - 32, axis=1), -pltpu.roll(gs, 32, axis=1))
        dxh = g1 * g_ref[...]
        t = _dot3(dxh * xh, ind) * (1.0 / HEAD)
        dx = rw * (dxh - xh * _dot3(t, ind_t))
        dx_ref[...] = dx.astype(BF16)
        cpart = jnp.sum(dx, axis=0, keepdims=True)
        gpart = jnp.sum(g1 * xh, axis=0, keepdims=True)

        @pl.when(i == 0)
        def _():
            cs_ref[...] = cpart
            acc_ref[...] = gpart

        @pl.when(i > 0)
        def _():
            cs_ref[...] += cpart
            acc_ref[...] += gpart

        @pl.when(i == steps - 1)
        def _():
            fold = ((_iota((W, 128), 0) & 63) == _iota((W, 128), 1)).astype(BF16)
            dg_ref[...] = _dot3(jnp.broadcast_to(acc_ref[...], (8, W)), fold)[0:1, :]

    tab = pl.BlockSpec((tr, 128), lambda i: (i, 0))
    return pl.pallas_call(
        body, name=name, grid=(steps,),
        in_specs=[pl.BlockSpec((tr, W), lambda i: (i, off)), pl.BlockSpec((1, W), lambda i: (0, 0)), tab, tab,
                  pl.BlockSpec((nh, tr, HEAD), lambda i: (0, i, 0))],
        out_specs=[pl.BlockSpec((tr, W), lambda i: (i, 0)), pl.BlockSpec((1, W), lambda i: (0, 0)),
                   pl.BlockSpec((1, 128), lambda i: (0, 0))],
        out_shape=[jax.ShapeDtypeStruct((S, W), BF16), jax.ShapeDtypeStruct((1, W), F32),
                   jax.ShapeDtypeStruct((1, 128), F32)],
        scratch_shapes=[pltpu.VMEM((1, W), F32)], compiler_params=_cp(1),
    )(xin, gain_w, *rope, dout)


def _attn_band():
    qi = jnp.arange(ATT_G * WINDOW)[:, None] % WINDOW
    ki = jnp.arange(2 * WINDOW)[None, :]
    rel = qi + WINDOW - ki
    ok = (rel >= 0) & (rel < WINDOW)
    return jnp.stack([jnp.where(ok & (ki >= WINDOW), 0.0, NEG), jnp.where(ok, 0.0, NEG)]).astype(F32)


def _attn_probs(q, kb, sink_ref, band_ref, h, i):
    s = _dot(q, kb, 1, 1) * (HEAD ** -0.5) + band_ref[jnp.minimum(i, 1)]
    r1 = _iota((4 * WINDOW, 1), 0)
    sink = jnp.where(r1 < WINDOW, sink_ref[4 * h], jnp.where(r1 < 2 * WINDOW, sink_ref[4 * h + 1],
                     jnp.where(r1 < 3 * WINDOW, sink_ref[4 * h + 2], sink_ref[4 * h + 3])))
    m = jnp.maximum(jnp.max(s, axis=1, keepdims=True), sink)
    p = jnp.exp(s - m)
    ps = jnp.exp(sink - m)
    inv = 1.0 / (jnp.sum(p, axis=1, keepdims=True) + ps)
    return p * inv, ps * inv


ATT_HPS = 4
_BAND = pl.BlockSpec((2, ATT_G * WINDOW, 2 * WINDOW), lambda h, i: (0, 0, 0))


def _attn_specs(S):
    qspec = pl.BlockSpec((ATT_HPS, ATT_G, WINDOW, HEAD), lambda h, i: (h, 0, i, 0))
    cur = pl.BlockSpec((ATT_HPS, WINDOW, HEAD), lambda h, i: (h, i, 0))
    prev = pl.BlockSpec((ATT_HPS, WINDOW, HEAD), lambda h, i: (h, jnp.maximum(i - 1, 0), 0))
    tok = pl.BlockSpec((WINDOW, ATT_HPS * ATT_G * HEAD), lambda h, i: (i, h))
    return qspec, cur, prev, tok


def _attn_fwd(qh, kh, vh, sinks, *, name="attn_fwd"):
    S = kh.shape[1]
    nb = S // WINDOW

    def body(s_ref, band_ref, q_ref, kc_ref, kp_ref, vc_ref, vp_ref, o_ref):
        h2, i = pl.program_id(0), pl.program_id(1)
        outs = []
        for hh in range(ATT_HPS):
            q = q_ref[hh].reshape(ATT_G * WINDOW, HEAD)
            kb = jnp.concatenate([kp_ref[hh], kc_ref[hh]], axis=0)
            vb = jnp.concatenate([vp_ref[hh], vc_ref[hh]], axis=0)
            probs, _ = _attn_probs(q, kb, s_ref, band_ref, ATT_HPS * h2 + hh, i)
            o = _dot(probs.astype(BF16), vb).astype(BF16)
            outs += [o[WINDOW * g:WINDOW * (g + 1)] for g in range(ATT_G)]
        o_ref[...] = jnp.concatenate(outs, axis=1)

    qspec, cur, prev, tok = _attn_specs(S)
    return pl.pallas_call(
        body, name=name, grid=(ATT_KV // ATT_HPS, nb),
        in_specs=[pl.BlockSpec(memory_space=pltpu.SMEM), _BAND, qspec, cur, prev, cur, prev], out_specs=tok,
        out_shape=jax.ShapeDtypeStruct((S, ATT_KV * ATT_G * HEAD), BF16), compiler_params=_cp(2),
    )(sinks, _attn_band(), qh, kh, kh, vh, vh)


def _attn_bwd(qh, kh, vh, sinks, doh, *, name="attn_bwd"):
    S = kh.shape[1]
    nb = S // WINDOW

    def body(s_ref, band_ref, q_ref, kc_ref, kp_ref, vc_ref, vp_ref, do_ref, dq_ref, dk_ref, dv_ref, dsk_ref):
        h2, i = pl.program_id(0), pl.program_id(1)

        @pl.when(i == 0)
        def _():
            dk_ref[...] = jnp.zeros_like(dk_ref)
            dv_ref[...] = jnp.zeros_like(dv_ref)
            dsk_ref[...] = jnp.zeros_like(dsk_ref)

        dov = do_ref[...]
        cur = pl.multiple_of(i * WINDOW, WINDOW)
        lane = _iota((8, 128), 1)
        row = _iota((8, 128), 0)
        scale = HEAD ** -0.5
        for hh in range(ATT_HPS):
            q = q_ref[hh].reshape(ATT_G * WINDOW, HEAD)
            do = jnp.concatenate([dov[:, HEAD * (ATT_G * hh + g):HEAD * (ATT_G * hh + g + 1)] for g in range(ATT_G)], axis=0)
            kb = jnp.concatenate([kp_ref[hh], kc_ref[hh]], axis=0)
            vb = jnp.concatenate([vp_ref[hh], vc_ref[hh]], axis=0)
            probs, psink = _attn_probs(q, kb, s_ref, band_ref, ATT_HPS * h2 + hh, i)
            dp = _dot(do, vb, 1, 1)
            delta = jnp.sum(probs * dp, axis=1, keepdims=True)
            ds = (probs * (dp - delta)).astype(BF16)
            dq_ref[hh] = (_dot(ds, kb) * scale).reshape(ATT_G, WINDOW, HEAD)
            dkb = _dot(ds, q, 0, 0) * scale
            dvb = _dot(probs.astype(BF16), do, 0, 0)
            dk_ref[hh, pl.ds(cur, WINDOW), :] += dkb[WINDOW:, :]
            dv_ref[hh, pl.ds(cur, WINDOW), :] += dvb[WINDOW:, :]
            prv = pl.multiple_of(jnp.maximum(i - 1, 0) * WINDOW, WINDOW)
            dk_ref[hh, pl.ds(prv, WINDOW), :] += dkb[:WINDOW, :]
            dv_ref[hh, pl.ds(prv, WINDOW), :] += dvb[:WINDOW, :]

            dsr = -psink * delta
            upd = jnp.zeros((8, 128), F32)
            for gq in range(ATT_G):
                v = jnp.sum(dsr[gq * WINDOW:(gq + 1) * WINDOW, :], axis=0, keepdims=True)
                upd = jnp.where((lane == gq) & (row == 0), v, upd)
            dsk_ref[hh] += upd

    qspec, cur, prev, tok = _attn_specs(S)
    full = pl.BlockSpec((ATT_HPS, S, HEAD), lambda h, i: (h, 0, 0))
    return pl.pallas_call(
        body, name=name, grid=(ATT_KV // ATT_HPS, nb),
        in_specs=[pl.BlockSpec(memory_space=pltpu.SMEM), _BAND, qspec, cur, prev, cur, prev, tok],
        out_specs=[qspec, full, full, pl.BlockSpec((ATT_HPS, 8, 128), lambda h, i: (h, 0, 0))],
        out_shape=[jax.ShapeDtypeStruct((ATT_KV, ATT_G, S, HEAD), F32), jax.ShapeDtypeStruct((ATT_KV, S, HEAD), F32),
                   jax.ShapeDtypeStruct((ATT_KV, S, HEAD), F32), jax.ShapeDtypeStruct((ATT_KV, 8, 128), F32)],
        compiler_params=_cp(2),
    )(sinks, _attn_band(), qh, kh, kh, vh, vh, doh)


def _heads_major(t, nh):
    S = t.shape[0]
    return t.reshape(S, nh, HEAD).transpose(1, 0, 2)


def _tokens_major(t):
    nh, S, _ = t.shape
    return t.transpose(1, 0, 2).reshape(S, nh * HEAD)


class _NoComm:
    def late_start(self, part, after):
        return None

    def late_arrived(self, part, after):
        return None

    def late_weights(self, w, after, part):
        return w

    def advance(self, after, group=None, tensors=None):
        return None


def _local_step(x, posf, target, w, comm=None):
    S, D = x.shape
    gr = {}
    comm = comm or _NoComm()

    zx, h1 = _norm_mm(x, w["a_norm"], w["w_zx"], name="in_proj_zx", dep=w.get("dep"))
    dtr = _mm(h1, w["w_dt"], name="in_proj_dt")
    xbc = _conv_silu_fwd(zx, 2048, 4096, w["a_conv_w"], w["a_conv_b"], name="a_conv_f")
    dt_g, ac_g, sg_g, ac_t = _ssd_prep(dtr, w["a_dt_bias"], w["a_A_log"])
    y_ssd, states = _ssd_fwd(xbc, dt_g, ac_g, ac_t, dep=comm.late_start(1, xbc))
    yg = _gate_fwd(y_ssd, xbc, zx, w["a_Dexp"], w["a_gnorm"], dep=comm.late_arrived(0, [y_ssd]))
    w = comm.late_weights(w, yg, 0)
    x1 = _mm(yg, w["a_out_proj"], res=x, name="out_proj")

    FW = w["f_w_in"][0].shape[2]

    def ffn_fwd(xin, l, loss_target=None):
        u, h = _norm_mm(xin, w["f_norm"][l], w["f_w_in"][l], name=f"f_in{l}", N=N_CHIPS * FW, tn=FW,
                        b_spec=pl.BlockSpec((None, D, FW), lambda i, j: (j, 0, 0)))
        a = _ffn_act_fwd(u, w["f_conv_w"][l], w["f_conv_b"][l], name=f"f_act_f{l}")
        dep = comm.late_arrived(1, [u]) if l == 0 else None
        xo = _mm(a, w["f_w_down"][l], res=xin, tk=a.shape[1], name=f"f_down{l}", target=loss_target, dep=dep)
        return xo, (h, u)

    x2, ffn0 = ffn_fwd(x1, 0)
    w = comm.late_weights(w, x2, 1)

    kv, hk = _norm_mm(x2, w["kv_norm"], w["w_kv"], bias=w["b_kv"], name="kv_proj")
    q, hq = _norm_mm(x2, w["b_norm"], w["w_q"], bias=w["b_q"], name="q_proj")
    rope = _rope_cs(posf)
    kr = _hnrope_fwd(kv, 0, 256, w["k_norm_w"], rope, name="k_rope_f")
    qr = _hnrope_fwd(q, 0, 1024, w["q_norm_w"], rope, name="q_rope_f")
    qh = qr.reshape(ATT_KV, ATT_G, S, HEAD)
    kh = kr
    vh = _heads_major(kv[:, 256:].astype(BF16), ATT_KV)
    att = _attn_fwd(qh, kh, vh, w["sinks"])
    x3 = _mm(att, w["w_o"], bias=w["b_o"], res=x2, name="o_proj")
    (dy, loss_part), ffn1 = ffn_fwd(x3, 1, target)

    def ffn_bwd(xin, l, saved, dyo, want_colsum, dep=None):
        h, u = saved
        da = _mm(dyo, w["f_w_down"][l], tb=True, name=f"f_down_dx{l}", dep=dep)
        du, dcw, dcb, a = _ffn_act_bwd(u, w["f_conv_w"][l], w["f_conv_b"][l], da, name=f"f_act_b{l}")
        dw_down = _mm(a, dyo, ta=True, out_dtype=BF16, name=f"f_down_dw{l}")
        dw_in = _mm(h, du, ta=True, out_dtype=BF16, name=f"f_in_dw{l}", dims=(D, N_CHIPS * FW, S), tm=D, tn=FW, tk=S,
                    b_spec=pl.BlockSpec((None, S, FW), lambda i, j, k: (j // 2, 0, j % 2)),
                    o_spec=pl.BlockSpec((None, D, FW), lambda i, j, k: (j, i, 0)), o_shape=(N_CHIPS, D, FW))
        ts = _pick(S, (512, 256))
        pieces = [(pl.BlockSpec((None, ts, FW), lambda i, j, k, q=q: (q // 2, i, q % 2)),
                   pl.BlockSpec((None, D, FW), lambda i, j, k, q=q: (q, 0, 0), pipeline_mode=pl.Buffered(1)))
                  for q in range(N_CHIPS)]
        outs = _mm(du, w["f_w_in"][l], tb=True, name=f"f_in_dx{l}", dims=(S, D, FW), tm=ts, tn=D, tk=FW,
                   a_spec=pieces[0][0], b_spec=pieces[0][1], more=pieces[1:],
                   rms=(xin, w["f_norm"][l], dyo), rms_colsum=want_colsum)
        g = dict(f_norm=outs[1], f_w_in=dw_in, f_conv_w=dcw, f_conv_b=dcb, f_w_down=dw_down)
        return outs[0], g, (outs[2] if want_colsum else None)

    dx3, gr["ffn1"], db_o = ffn_bwd(x3, 1, ffn1, dy, True)
    gr["b_o"] = db_o
    gr["w_o"] = _mm(att, dx3, ta=True, out_dtype=BF16, name="o_proj_dw")
    datt = _mm(dx3, w["w_o"], tb=True, out_dtype=BF16, name="o_proj_dx")
    dqh, dkh, dvh, dsk = _attn_bwd(qh, kh, vh, w["sinks"], datt)
    gr["sinks"] = dsk[:, 0, :4].reshape(1, 16)
    dv = _tokens_major(dvh).astype(BF16)
    dq, db_q, dqn = _hnrope_bwd(q, 0, 1024, w["q_norm_w"], rope, dqh.reshape(16, S, HEAD), name="q_rope_b")
    dk, db_k, dkn = _hnrope_bwd(kv, 0, 256, w["k_norm_w"], rope, dkh, name="k_rope_b")
    gr["q_norm"], gr["k_norm"] = dqn[:, :HEAD], dkn[:, :HEAD]
    gr["b_q"] = db_q
    gr["b_kv"] = jnp.concatenate([db_k, _colsum(dv, name="dv_colsum")], axis=1)
    dkv = jnp.concatenate([dk, dv], axis=1)
    gr["w_q"] = _mm(hq, dq, ta=True, out_dtype=BF16, name="q_proj_dw")
    gr["w_kv"] = _mm(hk, dkv, ta=True, out_dtype=BF16, name="kv_proj_dw")
    tok = comm.advance([gr["w_kv"]], 1, dict(f_down1=gr["ffn1"]["f_w_down"], f_in1=gr["ffn1"]["f_w_in"], w_o=gr["w_o"],
                                             w_q=gr["w_q"], w_kv=gr["w_kv"]))
    tsm = _pick(S, (512, 256))
    dx2, gr["b_norm"] = _mm(dq, w["w_q"], tb=True, name="q_proj_dx", dep=tok, tm=tsm, tn=D, rms=(x2, w["b_norm"], dx3))
    dx2, gr["kv_norm"] = _mm(dkv, w["w_kv"], tb=True, name="kv_proj_dx", tm=tsm, tn=D, rms=(x2, w["kv_norm"], dx2))

    dx1, gr["ffn0"], _ = ffn_bwd(x1, 0, ffn0, dx2, False, dep=comm.advance([dx2]))

    gr["a_out_proj"] = _mm(yg, dx1, ta=True, out_dtype=BF16, name="out_proj_dw")
    tok = comm.advance([dx1, gr["a_out_proj"]], 2,
                       dict(f_down0=gr["ffn0"]["f_w_down"], f_in0=gr["ffn0"]["f_w_in"], out_proj=gr["a_out_proj"]))
    dyg = _mm(dx1, w["a_out_proj"], tb=True, name="out_proj_dx", dep=tok)
    dy_ssd, dz, gr["a_gnorm"], dD = _gate_bwd(y_ssd, xbc, zx, w["a_Dexp"], w["a_gnorm"], dyg)
    gr["a_D"] = dD[:, :SSM_HEADS]
    dxs, dB, dC, dhead = _ssd_bwd(xbc, dt_g, ac_g, ac_t, states, dy_ssd, w["a_Dexp"], dep=comm.advance([dy_ssd]))
    ddtr, dsmall = _ssd_post(dhead, dt_g, sg_g, w["a_A_log_g"])
    gr["a_A_log"] = dsmall[:, 0, :4].reshape(1, SSM_HEADS)
    gr["a_dt_bias"] = dsmall[:, 1, :4].reshape(1, SSM_HEADS)
    dxbc, gr["a_conv_w"], gr["a_conv_b"] = _conv_silu_bwd(
        zx, 2048, 4096, w["a_conv_w"], w["a_conv_b"], [(dxs, 0), (dB, 2048), (dC, 3072)], name="a_conv_b")
    gr["in_proj"] = _in_proj_dw(h1, dz, dxbc, ddtr)
    ts = _pick(S, (512, 256))
    once = pl.Buffered(1)
    wblk = lambda q: pl.BlockSpec((D, 2048), lambda i, j, k: (0, q), pipeline_mode=once)
    dx0, gr["a_norm"] = _mm(
        dz, w["w_zx"], tb=True, name="in_proj_dx", dims=(S, D, 2048), tm=ts, tn=D, tk=2048,
        a_spec=pl.BlockSpec((ts, 2048), lambda i, j, k: (i, 0)), b_spec=wblk(0),
        more=[(dxbc, pl.BlockSpec((ts, 2048), lambda i, j, k: (i, 0)), w["w_zx"], wblk(1)),
              (dxbc, pl.BlockSpec((ts, 2048), lambda i, j, k: (i, 1)), w["w_zx"], wblk(2)),
              (ddtr, pl.BlockSpec((ts, 128), lambda i, j, k: (i, 0)), w["w_dt"],
               pl.BlockSpec((D, 128), lambda i, j, k: (0, 0), pipeline_mode=once))],
        rms=(x, w["a_norm"], dx1))
    tok = comm.advance([dx0], 3, dict(in_proj=gr["in_proj"].reshape(D, N_CHIPS, -1).transpose(1, 0, 2)))
    return loss_part, dx0, gr, tok


def _prep_small(full, w):
    w["a_norm"] = full["a_norm"]
    w["a_conv_w"] = full["a_conv_w"][0]
    w["a_conv_b"] = full["a_conv_b"]
    pad32 = lambda v: jnp.pad(v, ((0, 0), (0, 128 - SSM_HEADS)))
    w["a_dt_bias"] = pad32(full["a_dt_bias"])
    w["a_A_log"] = pad32(full["a_A_log"])
    w["a_A_log_g"] = jnp.pad(full["a_A_log"].reshape(SSM_GROUPS, 1, 4), ((0, 0), (0, 0), (0, 124)))
    w["a_Dexp"] = jnp.repeat(full["a_D"], HEAD, axis=1)
    w["a_gnorm"] = full["a_gnorm"]
    w["f_norm"] = [full["f_norm"][l:l + 1] for l in range(2)]
    w["f_conv_w"] = [full["f_conv_w"][l] for l in range(2)]
    w["f_conv_b"] = [full["f_conv_b"][l:l + 1] for l in range(2)]
    w["kv_norm"] = full["kv_norm"].reshape(1, -1)
    w["b_kv"] = full["b_kv"].reshape(1, -1)
    w["k_norm_w"] = jnp.tile(full["k_norm"].reshape(1, HEAD), (1, ATT_KV))
    w["b_norm"] = full["b_norm"]
    w["b_q"] = full["b_q"]
    w["q_norm_w"] = jnp.tile(full["q_norm"], (1, ATT_KV * ATT_G))
    w["sinks"] = full["sinks"].reshape(-1)
    w["b_o"] = full["b_o"]
    return w


def _split_in_proj(ip):
    return ip[:, :6144].astype(BF16), jnp.pad(ip[:, 6144:], ((0, 0), (0, 128 - SSM_HEADS))).astype(BF16)


def _join_in_proj(blocks, *, name="in_proj_join", tr=256):
    _, R, cw = blocks.shape
    zx_cols = 3 * 2048
    rest = N_CHIPS * cw - zx_cols

    def body(b_ref, zx_ref, dt_ref):
        whole = jnp.concatenate([b_ref[j] for j in range(N_CHIPS)], axis=1)
        zx_ref[...] = whole[:, :zx_cols]
        dt_ref[...] = jnp.concatenate([whole[:, zx_cols:], jnp.zeros((tr, 128 - rest), BF16)], axis=1)

    return pl.pallas_call(
        body, name=name, grid=(R // tr,), in_specs=[pl.BlockSpec((N_CHIPS, tr, cw), lambda i: (0, i, 0))],
        out_specs=[pl.BlockSpec((tr, zx_cols), lambda i: (i, 0)), pl.BlockSpec((tr, 128), lambda i: (i, 0))],
        out_shape=[jax.ShapeDtypeStruct((R, zx_cols), BF16), jax.ShapeDtypeStruct((R, 128), BF16)],
        compiler_params=_cp(1),
    )(blocks)


def _in_proj_dw(h, dz, dxbc, ddtr, *, name="in_proj_dw", tn=512):
    S, D = h.shape
    nz, nx = dz.shape[1] // tn, dxbc.shape[1] // tn
    N = dz.shape[1] + dxbc.shape[1] + SSM_HEADS

    def body(h_ref, z_ref, x_ref, t_ref, o_ref):
        j = pl.program_id(0)
        hb = h_ref[...].astype(BF16)

        @pl.when(j < nz)
        def _():
            o_ref[...] = _dot(hb, z_ref[...].astype(BF16), 0, 0).astype(BF16)

        @pl.when((j >= nz) & (j < nz + nx))
        def _():
            o_ref[...] = _dot(hb, x_ref[...].astype(BF16), 0, 0).astype(BF16)

        @pl.when(j == nz + nx)
        def _():
            o_ref[:, :128] = _dot(hb, t_ref[...].astype(BF16), 0, 0).astype(BF16)
            o_ref[:, 128:] = jnp.zeros((D, tn - 128), BF16)

    return pl.pallas_call(
        body, name=name, grid=(nz + nx + 1,),
        in_specs=[pl.BlockSpec((S, D), lambda j: (0, 0), pipeline_mode=pl.Buffered(1)),
                  pl.BlockSpec((S, tn), lambda j: (0, jnp.minimum(j, nz - 1))),
                  pl.BlockSpec((S, tn), lambda j: (0, jnp.clip(j - nz, 0, nx - 1))),
                  pl.BlockSpec((S, 128), lambda j: (0, 0))],
        out_specs=pl.BlockSpec((D, tn), lambda j: (0, j)),
        out_shape=jax.ShapeDtypeStruct((D, N), BF16), compiler_params=_cp(1),
    )(h, dz, dxbc, ddtr)


def _prep_weights(full):
    w = _prep_small(full, {})
    w["w_zx"], w["w_dt"] = _split_in_proj(full["a_in_proj"][0])
    w["a_out_proj"] = full["a_out_proj"][0].astype(BF16)
    w["f_w_in"] = [full["f_w_in"][l].reshape(1024, N_CHIPS, -1).transpose(1, 0, 2).astype(BF16) for l in range(2)]
    w["f_w_down"] = [full["f_w_down"][l].astype(BF16) for l in range(2)]
    w["w_kv"] = full["w_kv"].astype(BF16)
    w["w_q"] = full["w_q"][0].astype(BF16)
    w["w_o"] = full["w_o"][0].astype(BF16)
    return w


def _small_grads(gr):
    g = {}
    g["a_norm"] = gr["a_norm"]
    g["a_conv_w"] = gr["a_conv_w"][None]
    g["a_conv_b"] = gr["a_conv_b"]
    g["a_dt_bias"], g["a_A_log"], g["a_D"] = gr["a_dt_bias"], gr["a_A_log"], gr["a_D"]
    g["a_gnorm"] = gr["a_gnorm"]
    g["kv_norm"] = gr["kv_norm"].reshape(-1)
    g["b_kv"] = gr["b_kv"].reshape(-1)
    g["k_norm"] = gr["k_norm"].reshape(-1)
    g["b_norm"] = gr["b_norm"]
    g["b_q"] = gr["b_q"]
    g["q_norm"] = gr["q_norm"]
    g["sinks"] = gr["sinks"]
    g["b_o"] = gr["b_o"]
    f = [gr["ffn0"], gr["ffn1"]]
    g["f_norm"] = jnp.concatenate([f[0]["f_norm"], f[1]["f_norm"]], axis=0)
    g["f_conv_w"] = jnp.stack([f[l]["f_conv_w"] for l in range(2)])
    g["f_conv_b"] = jnp.concatenate([f[l]["f_conv_b"] for l in range(2)], axis=0)
    return g


def _full_grads(gr):
    g = _small_grads(gr)
    f32 = lambda t: t.astype(F32)
    g["a_in_proj"] = f32(gr["in_proj"])[None]
    g["a_out_proj"] = f32(gr["a_out_proj"])[None]
    g["w_kv"] = f32(gr["w_kv"])
    g["w_q"] = f32(gr["w_q"])[None]
    g["w_o"] = f32(gr["w_o"])[None]
    f = [gr["ffn0"], gr["ffn1"]]
    g["f_w_in"] = jnp.stack([f32(f[l]["f_w_in"]).transpose(1, 0, 2).reshape(1024, -1) for l in range(2)])
    g["f_w_down"] = jnp.stack([f32(f[l]["f_w_down"]) for l in range(2)])
    return g


MESH = pl.DeviceIdType.MESH
WEIGHTS = ("a_norm", "a_in_proj", "a_conv_w", "a_conv_b", "a_dt_bias", "a_A_log", "a_D", "a_gnorm", "a_out_proj",
           "kv_norm", "w_kv", "b_kv", "k_norm", "b_norm", "w_q", "b_q", "q_norm", "sinks", "w_o", "b_o", "f_norm",
           "f_w_in", "f_conv_w", "f_conv_b", "f_w_down")
MATS = (("in_proj", "a_in_proj", 0), ("out_proj", "a_out_proj", 0), ("w_kv", "w_kv", None), ("w_q", "w_q", 0),
        ("w_o", "w_o", 0), ("f_in0", "f_w_in", 0), ("f_in1", "f_w_in", 1), ("f_down0", "f_w_down", 0),
        ("f_down1", "f_w_down", 1))
SMALL_CUT = (("a_norm", 1), ("a_conv_w", 2), ("a_conv_b", 1), ("a_gnorm", 1), ("f_conv_w", 2))
SMALL_REP = ("a_dt_bias", "a_A_log", "a_D", "kv_norm", "b_kv", "k_norm", "b_norm", "b_q", "q_norm", "sinks", "b_o",
             "f_norm", "f_conv_b")


def _coords():
    return lax.axis_index("x"), lax.axis_index("y"), lax.axis_index("c")


def _other_chips(x, y):
    return [(1 - x, y), (x, 1 - y), (1 - x, 1 - y)]


def _pack(arrs, rows_align, lanes, dtype):
    flat = jnp.concatenate([a.reshape(-1).astype(dtype) for a in arrs])
    per = rows_align * lanes
    total = -(-flat.shape[0] // per) * per
    return jnp.pad(flat, (0, total - flat.shape[0])).reshape(total // lanes, lanes)


def _unpack(flat, shapes):
    out, off = [], 0
    for s in shapes:
        n = math.prod(s)
        out.append(flat[off:off + n].reshape(s))
        off += n
    return out


def _remote(src, dst, send, recv, k, dev):
    return pltpu.make_async_remote_copy(src_ref=src, dst_ref=dst, send_sem=send.at[k], recv_sem=recv.at[k],
                                        device_id=dev, device_id_type=MESH)


_ANY = pl.BlockSpec(memory_space=pl.ANY)


def _halves(t):
    r, c = t.shape
    return t.reshape(2, r // 2, c)


def _gather_weights(shards, sp):
    n = len(shards)
    per = 9
    n_sem = per * n + 3

    def body(*refs):
        sh, sp_ref = refs[:n], refs[n]
        outs, sout = refs[n + 1:2 * n + 1], refs[2 * n + 1]
        send, recv, loc = refs[2 * n + 2:]
        x, y, c = _coords()
        me = 2 * x + y
        cx_, cy_, cd_ = _other_chips(x, y)
        ix, iy, idg = (2 * p[0] + p[1] for p in (cx_, cy_, cd_))
        to_x, to_y, sib = (*cx_, c), (*cy_, c), (x, y, 1 - c)
        l1 = pltpu.make_async_copy(sp_ref, sout.at[me], loc.at[0])
        l1.start()
        sends = [_remote(sp_ref, sout.at[me], send, recv, per * n + j, (*p, c)) for j, p in enumerate((cx_, cy_, cd_))]
        for t in range(n):
            sends.append(_remote(sh[t].at[c], outs[t].at[me, c], send, recv, per * t + 0, to_x))
            sends.append(_remote(sh[t].at[c], outs[t].at[me, c], send, recv, per * t + 1, to_y))
            sends.append(_remote(sh[t], outs[t].at[me], send, recv, per * t + 8, sib))
        for cp in sends:
            cp.start()

        def go(src, dst, k, dev):
            cp = _remote(src, dst, send, recv, k, dev)
            cp.start()
            sends.append(cp)

        def piece(t, owner, first):
            q = sh[t].shape[1] // 2
            return outs[t].at[owner, c, pl.ds(0 if first else q, q)]

        for t in range(n):
            _remote(sh[t].at[c], outs[t].at[ix, c], send, recv, per * t + 0, to_x).wait_recv()
            go(piece(t, ix, False), piece(t, ix, False), per * t + 3, to_y)
            go(outs[t].at[ix, c], outs[t].at[ix, c], per * t + 4, sib)
        for t in range(n):
            _remote(sh[t].at[c], outs[t].at[iy, c], send, recv, per * t + 1, to_y).wait_recv()
            go(piece(t, iy, True), piece(t, iy, True), per * t + 2, to_x)
            go(outs[t].at[iy, c], outs[t].at[iy, c], per * t + 5, sib)
        for t in range(n):
            _remote(piece(t, idg, True), piece(t, idg, True), send, recv, per * t + 2, to_x).wait_recv()
            go(piece(t, idg, True), piece(t, idg, True), per * t + 6, sib)
            _remote(piece(t, idg, False), piece(t, idg, False), send, recv, per * t + 3, to_y).wait_recv()
            go(piece(t, idg, False), piece(t, idg, False), per * t + 7, sib)
        for j, p in enumerate((cx_, cy_, cd_)):
            _remote(sp_ref, sout.at[2 * p[0] + p[1]], send, recv, per * n + j, (*p, c)).wait_recv()
        for t in range(n):
            q = sh[t].shape[1] // 2
            other = lambda owner, lo=None: outs[t].at[owner, 1 - c] if lo is None else outs[t].at[owner, 1 - c, pl.ds(lo, q)]
            _remote(other(ix), other(ix), send, recv, per * t + 4, sib).wait_recv()
            _remote(other(iy), other(iy), send, recv, per * t + 5, sib).wait_recv()
            _remote(other(idg, 0), other(idg, 0), send, recv, per * t + 6, sib).wait_recv()
            _remote(other(idg, q), other(idg, q), send, recv, per * t + 7, sib).wait_recv()
            _remote(sh[t], outs[t].at[me], send, recv, per * t + 8, sib).wait_recv()
        for cp in sends:
            cp.wait_send()
        l1.wait()

    res = pl.pallas_call(
        body, name="gather_weights", in_specs=[_ANY] * (n + 1), out_specs=[_ANY] * (n + 1),
        out_shape=[jax.ShapeDtypeStruct((N_CHIPS,) + t.shape, t.dtype) for t in shards]
        + [jax.ShapeDtypeStruct((N_CHIPS,) + sp.shape, sp.dtype)],
        scratch_shapes=[pltpu.SemaphoreType.DMA((n_sem,)), pltpu.SemaphoreType.DMA((n_sem,)),
                        pltpu.SemaphoreType.DMA((1,))],
    )(*shards, sp)
    return res[:n], res[n]


_HBM = pl.BlockSpec(memory_space=pltpu.HBM)
_SEMS = pl.BlockSpec(memory_space=pltpu.SEMAPHORE)
_DATAFLOW = pltpu.SideEffectType.DATAFLOW_SIDE_EFFECTING


def _in_hbm(a):
    return pltpu.with_memory_space_constraint(a, pltpu.HBM)


def _start_copies(copies, arrays, n_sem, after, *, name):
    n, na = len(arrays), len(after)

    def body(*refs):
        for mine, _ in copies(refs[:n], refs[n + na], refs[n + na + 1]):
            mine.start()
        refs[-1][...] = jnp.zeros_like(refs[-1])

    res = pl.pallas_call(
        body, name=name, in_specs=[_HBM] * n + [_ANY] * na,
        out_specs=[_SEMS, _SEMS] + [_HBM] * n + [pl.BlockSpec(memory_space=pltpu.VMEM)],
        out_shape=[pltpu.SemaphoreType.DMA((n_sem,)), pltpu.SemaphoreType.DMA((n_sem,))]
        + [pltpu.HBM(a.shape, a.dtype) for a in arrays] + [jax.ShapeDtypeStruct((8, 128), F32)],
        input_output_aliases={t: 2 + t for t in range(n)},
        compiler_params=pltpu.CompilerParams(has_side_effects=_DATAFLOW),
    )(*[_in_hbm(a) for a in arrays], *after)
    return res[0], res[1], list(res[2:2 + n]), res[-1]


def _wait_copies(copies, send, recv, arrays, after, *, name):
    n = len(arrays)

    def body(*refs):
        for mine, theirs in copies(refs[:n], refs[n], refs[n + 1]):
            mine.wait_send()
            theirs.wait_recv()

    return list(pl.pallas_call(
        body, name=name, in_specs=[_HBM] * n + [_SEMS, _SEMS] + [_ANY] * len(after), out_specs=[_HBM] * n,
        out_shape=[pltpu.HBM(a.shape, a.dtype) for a in arrays], input_output_aliases={t: t for t in range(n)},
        compiler_params=pltpu.CompilerParams(has_side_effects=_DATAFLOW),
    )(*arrays, send, recv, *after))


def _sibling_copies(refs, send, recv):
    n = len(refs) // 2
    x, y, c = _coords()
    cps = [_remote(refs[t].at[:, 1 - c], refs[n + t], send, recv, t, (x, y, 1 - c)) for t in range(n)]
    return [(cp, cp) for cp in cps]


def _join_copies(refs, send, recv):
    x, y, c = _coords()
    sib = (x, y, 1 - c)
    return [(_remote(o.at[c], o.at[c], send, recv, t, sib), _remote(o.at[1 - c], o.at[1 - c], send, recv, t, sib))
            for t, o in enumerate(refs)]


def _gather_copies(sh, land, send, recv):
    x, y, c = _coords()
    me = 2 * x + y
    out = []
    for t in range(len(sh)):
        for j, (cx, cy) in enumerate(_other_chips(x, y)):
            dev = (cx, cy, c)
            out.append((_remote(sh[t].at[c], land[t].at[me, c], send, recv, 4 * t + j, dev),
                        _remote(sh[t].at[c], land[t].at[2 * cx + cy, c], send, recv, 4 * t + j, dev)))
        sib = (x, y, 1 - c)
        out.append((_remote(sh[t], land[t].at[me], send, recv, 4 * t + 3, sib),
                    _remote(sh[t], land[t].at[me], send, recv, 4 * t + 3, sib)))
    return out


def _gather_start(shards, after, *, name):
    n = len(shards)

    def body(*refs):
        sh, land = refs[:n], refs[n:2 * n]
        send, recv = refs[2 * n + 1], refs[2 * n + 2]
        token = refs[-1]
        for mine, _ in _gather_copies(sh, land, send, recv):
            mine.start()
        token[...] = jnp.zeros_like(token)

    lands = [_in_hbm(lax.empty((N_CHIPS,) + s.shape, s.dtype)) for s in shards]
    res = pl.pallas_call(
        body, name=name, in_specs=[_HBM] * (2 * n) + [_ANY],
        out_specs=[_SEMS, _SEMS] + [_HBM] * (2 * n) + [pl.BlockSpec(memory_space=pltpu.VMEM)],
        out_shape=[pltpu.SemaphoreType.DMA((4 * n,)), pltpu.SemaphoreType.DMA((4 * n,))]
        + [pltpu.HBM(s.shape, s.dtype) for s in shards] + [pltpu.HBM(l.shape, l.dtype) for l in lands]
        + [jax.ShapeDtypeStruct((8, 128), F32)],
        input_output_aliases={t: 2 + t for t in range(2 * n)},
        compiler_params=pltpu.CompilerParams(has_side_effects=_DATAFLOW),
    )(*[_in_hbm(s) for s in shards], *lands, after)
    return res[0], res[1], res[2:2 + n], res[2 + n:2 + 2 * n], res[-1]


def _gather_wait(send, recv, shards, lands, after, *, name):
    n = len(shards)

    def body(*refs):
        sh, land = refs[:n], refs[n:2 * n]
        send_r, recv_r = refs[2 * n], refs[2 * n + 1]
        for mine, theirs in _gather_copies(sh, land, send_r, recv_r):
            mine.wait_send()
            theirs.wait_recv()

    res = pl.pallas_call(
        body, name=name, in_specs=[_HBM] * (2 * n) + [_SEMS, _SEMS, _ANY], out_specs=[_HBM] * (2 * n),
        out_shape=[pltpu.HBM(s.shape, s.dtype) for s in shards] + [pltpu.HBM(l.shape, l.dtype) for l in lands],
        input_output_aliases={t: t for t in range(2 * n)},
        compiler_params=pltpu.CompilerParams(has_side_effects=_DATAFLOW),
    )(*shards, *lands, send, recv, after)
    return res[n:]


def _forward_copies(refs, send, recv):
    x, y, c = _coords()
    sib = (x, y, 1 - c)
    srcs = [2 * cx + cy for cx, cy in _other_chips(x, y)]
    return [(_remote(o.at[s, c], o.at[s, c], send, recv, 3 * t + j, sib),
             _remote(o.at[s, 1 - c], o.at[s, 1 - c], send, recv, 3 * t + j, sib))
            for t, o in enumerate(refs) for j, s in enumerate(srcs)]


def _small_copies(v, land, send, recv):
    x, y, c = _coords()
    me = 4 * x + 2 * y + c
    out = []
    for k in range(1, 8):
        px = 1 - x if k & 4 else x
        py = 1 - y if k & 2 else y
        pc = 1 - c if k & 1 else c
        out.append((_remote(v, land.at[me], send, recv, k - 1, (px, py, pc)),
                    _remote(v, land.at[4 * px + 2 * py + pc], send, recv, k - 1, (px, py, pc))))
    return out


def _small_start(v, after, *, name):
    def body(v_ref, land_ref, after_ref, send, recv, v_thru, land_thru, token):
        for mine, _ in _small_copies(v_ref, land_ref, send, recv):
            mine.start()
        token[...] = jnp.zeros_like(token)

    land = _in_hbm(lax.empty((8,) + v.shape, v.dtype))
    return pl.pallas_call(
        body, name=name, in_specs=[_HBM, _HBM, _ANY],
        out_specs=[_SEMS, _SEMS, _HBM, _HBM, pl.BlockSpec(memory_space=pltpu.VMEM)],
        out_shape=[pltpu.SemaphoreType.DMA((7,)), pltpu.SemaphoreType.DMA((7,)), pltpu.HBM(v.shape, v.dtype),
                   pltpu.HBM(land.shape, land.dtype), jax.ShapeDtypeStruct((8, 128), F32)],
        input_output_aliases={0: 2, 1: 3}, compiler_params=pltpu.CompilerParams(has_side_effects=_DATAFLOW),
    )(_in_hbm(v), land, after)


def _small_wait(send, recv, v, land, after, *, name):
    def body(v_ref, land_ref, send_r, recv_r, *rest):
        for mine, theirs in _small_copies(v_ref, land_ref, send_r, recv_r):
            mine.wait_send()
            theirs.wait_recv()

    return pl.pallas_call(
        body, name=name, in_specs=[_HBM, _HBM, _SEMS, _SEMS] + [_ANY] * len(after), out_specs=[_HBM, _HBM],
        out_shape=[pltpu.HBM(v.shape, v.dtype), pltpu.HBM(land.shape, land.dtype)],
        input_output_aliases={0: 0, 1: 1}, compiler_params=pltpu.CompilerParams(has_side_effects=_DATAFLOW),
    )(v, land, send, recv, *after)


def _small_sum(v, land, me_idx, *, name="small_sum"):
    def body(me_ref, v_ref, land_ref, o_ref):
        acc = None
        for s in range(8):
            term = jnp.where(me_ref[0] == s, v_ref[...], land_ref[s])
            acc = term if acc is None else acc + term
        o_ref[...] = acc

    whole = lambda shape: pl.BlockSpec(shape, lambda i, me_ref: (0,) * len(shape))
    return pl.pallas_call(
        body, name=name,
        grid_spec=pltpu.PrefetchScalarGridSpec(num_scalar_prefetch=1, grid=(1,), in_specs=[whole(v.shape), whole(land.shape)],
                                               out_specs=whole(v.shape)),
        out_shape=jax.ShapeDtypeStruct(v.shape, F32), compiler_params=_cp(1),
    )(me_idx, v, land)


RS_ROW_SPLIT = 2


def _rs_add_pair(gs, as_, c_idx, *, name):
    n = len(gs)

    def body(c_ref, *refs):
        for t in range(n):
            refs[2 * n + t][...] = (refs[t][...].astype(F32) + refs[n + t][...].astype(F32)).astype(BF16)

    def gspec(g):
        _, _, rh, cols = g.shape
        return pl.BlockSpec((None, None, rh // RS_ROW_SPLIT, cols), lambda j, i, c_ref: (j, c_ref[0], i, 0))

    def pspec(g):
        _, _, rh, cols = g.shape
        return pl.BlockSpec((None, rh // RS_ROW_SPLIT, cols), lambda j, i, c_ref: (j, i, 0))

    return pl.pallas_call(
        body, name=name,
        grid_spec=pltpu.PrefetchScalarGridSpec(
            num_scalar_prefetch=1, grid=(N_CHIPS, RS_ROW_SPLIT),
            in_specs=[gspec(g) for g in gs] + [pspec(g) for g in gs], out_specs=[pspec(g) for g in gs]),
        out_shape=[jax.ShapeDtypeStruct((N_CHIPS,) + g.shape[2:], BF16) for g in gs], compiler_params=_cp(2),
    )(c_idx, *gs, *as_)


def _chips_copies(p, r, send, recv):
    x, y, c = _coords()
    return [_remote(p[t].at[2 * cx + cy], r[t].at[k], send, recv, 3 * t + k, (cx, cy, c))
            for k, (cx, cy) in enumerate(_other_chips(x, y)) for t in range(len(p))]


def _rs_chips_start(ps, after, *, name):
    n, na = len(ps), len(after)

    def body(*refs):
        p, r = refs[:n], refs[n:2 * n]
        send, recv = refs[2 * n + na], refs[2 * n + na + 1]
        token = refs[-1]
        for cp in _chips_copies(p, r, send, recv):
            cp.start()
        token[...] = jnp.zeros_like(token)

    lands = [_in_hbm(lax.empty((3,) + p.shape[1:], p.dtype)) for p in ps]
    res = pl.pallas_call(
        body, name=name, in_specs=[_HBM] * (2 * n) + [_ANY] * na,
        out_specs=[_SEMS, _SEMS] + [_HBM] * (2 * n) + [pl.BlockSpec(memory_space=pltpu.VMEM)],
        out_shape=[pltpu.SemaphoreType.DMA((3 * n,)), pltpu.SemaphoreType.DMA((3 * n,))]
        + [pltpu.HBM(p.shape, p.dtype) for p in ps] + [pltpu.HBM(l.shape, l.dtype) for l in lands]
        + [jax.ShapeDtypeStruct((8, 128), F32)],
        input_output_aliases={t: 2 + t for t in range(2 * n)},
        compiler_params=pltpu.CompilerParams(has_side_effects=_DATAFLOW),
    )(*[_in_hbm(p) for p in ps], *lands, *after)
    return res[0], res[1], res[2:2 + n], res[2 + n:2 + 2 * n], res[-1]


def _rs_chips_wait(send, recv, ps, lands, after, *, name):
    n = len(ps)

    def body(*refs):
        p, r = refs[:n], refs[n:2 * n]
        for cp in _chips_copies(p, r, refs[2 * n], refs[2 * n + 1]):
            cp.wait_send()
            cp.wait_recv()

    res = pl.pallas_call(
        body, name=name, in_specs=[_HBM] * (2 * n) + [_SEMS, _SEMS] + [_ANY] * len(after), out_specs=[_HBM] * (2 * n),
        out_shape=[pltpu.HBM(p.shape, p.dtype) for p in ps] + [pltpu.HBM(l.shape, l.dtype) for l in lands],
        input_output_aliases={t: t for t in range(2 * n)},
        compiler_params=pltpu.CompilerParams(has_side_effects=_DATAFLOW),
    )(*ps, *lands, send, recv, *after)
    return res[:n], res[n:]


def _rs_add_chips(ps, rs, idx, *, name):
    n = len(ps)

    def body(idx_ref, *refs):
        for t in range(n):
            p_ref, r0, r1, r2 = refs[4 * t:4 * t + 4]
            refs[4 * n + t][...] = ((p_ref[...].astype(F32) + r0[...].astype(F32)) + r1[...].astype(F32)) + r2[...].astype(F32)

    in_specs, args = [], []
    for p, r in zip(ps, rs):
        _, rh, cols = p.shape
        blk = (None, rh // RS_ROW_SPLIT, cols)
        in_specs.append(pl.BlockSpec(blk, lambda i, idx_ref: (idx_ref[0], i, 0)))
        in_specs += [pl.BlockSpec(blk, lambda i, idx_ref, k=k: (k, i, 0)) for k in range(3)]
        args += [p, r, r, r]
    out_specs = [pl.BlockSpec((None, p.shape[1] // RS_ROW_SPLIT, p.shape[2]), lambda i, idx_ref: (idx_ref[1], i, 0))
                 for p in ps]
    return pl.pallas_call(
        body, name=name,
        grid_spec=pltpu.PrefetchScalarGridSpec(num_scalar_prefetch=1, grid=(RS_ROW_SPLIT,), in_specs=in_specs,
                                               out_specs=out_specs),
        out_shape=[jax.ShapeDtypeStruct((2,) + p.shape[1:], F32) for p in ps], compiler_params=_cp(1),
    )(idx, *args)


def _adamw(w, gs, m, v, *, name, dep=None):
    L, Rr, C = w.shape
    tr, tc = _pick(Rr, (256, 128, 64)), C
    if tr == Rr and Rr * C > 512 * 1024:
        tc = 256
    bc1 = 1.0 - ADAM_B1 ** ADAM_STEP
    bc2 = 1.0 - ADAM_B2 ** ADAM_STEP
    nd = 0 if dep is None else 1

    def body(*refs):
        w_ref, m_ref, v_ref = refs[0], refs[1], refs[2]
        g_refs = refs[3:3 + L]
        d_ref, mo_ref, vo_ref, go_ref = refs[3 + L + nd:]
        layer = pl.program_id(0)
        gv = g_refs[0][...]
        for q in range(1, L):
            gv = jnp.where(layer == q, g_refs[q][...], gv)
        mn = ADAM_B1 * m_ref[...] + (1.0 - ADAM_B1) * gv
        vn = ADAM_B2 * v_ref[...] + (1.0 - ADAM_B2) * (gv * gv)
        go_ref[...] = gv
        mo_ref[...] = mn
        vo_ref[...] = vn
        d_ref[...] = -ADAM_LR * ((mn / bc1) / (jnp.sqrt(vn / bc2) + ADAM_EPS) + ADAM_WD * w_ref[...])

    blk = pl.BlockSpec((None, tr, tc), lambda l, i, j: (l, i, j))
    gblks = [pl.BlockSpec((tr, tc), lambda l, i, j, q=q: (jnp.where(l == q, i, 0), jnp.where(l == q, j, 0))) for q in range(L)]
    return pl.pallas_call(
        body, name=name, grid=(L, Rr // tr, C // tc), in_specs=[blk] * 3 + gblks + [_ANY] * nd, out_specs=[blk] * 4,
        out_shape=[jax.ShapeDtypeStruct((L, Rr, C), F32)] * 4, compiler_params=_cp(3),
    )(w, m, v, *gs, *([] if dep is None else [dep]))


def kernel(x, positions, a_norm, a_in_proj, a_conv_w, a_conv_b, a_dt_bias, a_A_log, a_D, a_gnorm, a_out_proj,
           kv_norm, w_kv, b_kv, k_norm, b_norm, w_q, b_q, q_norm, sinks, w_o, b_o, f_norm, f_w_in, f_conv_w,
           f_conv_b, f_w_down, loss_target, m_a_norm, m_a_in_proj, m_a_conv_w, m_a_conv_b, m_a_dt_bias, m_a_A_log,
           m_a_D, m_a_gnorm, m_a_out_proj, m_kv_norm, m_w_kv, m_b_kv, m_k_norm, m_b_norm, m_w_q, m_b_q, m_q_norm,
           m_sinks, m_w_o, m_b_o, m_f_norm, m_f_w_in, m_f_conv_w, m_f_conv_b, m_f_w_down, v_a_norm, v_a_in_proj,
           v_a_conv_w, v_a_conv_b, v_a_dt_bias, v_a_A_log, v_a_D, v_a_gnorm, v_a_out_proj, v_kv_norm, v_w_kv,
           v_b_kv, v_k_norm, v_b_norm, v_w_q, v_b_q, v_q_norm, v_sinks, v_w_o, v_b_o, v_f_norm, v_f_w_in,
           v_f_conv_w, v_f_conv_b, v_f_w_down):
    wl = dict(zip(WEIGHTS, (a_norm, a_in_proj, a_conv_w, a_conv_b, a_dt_bias, a_A_log, a_D, a_gnorm, a_out_proj,
                            kv_norm, w_kv, b_kv, k_norm, b_norm, w_q, b_q, q_norm, sinks, w_o, b_o, f_norm, f_w_in,
                            f_conv_w, f_conv_b, f_w_down)))
    ml = dict(zip(WEIGHTS, (m_a_norm, m_a_in_proj, m_a_conv_w, m_a_conv_b, m_a_dt_bias, m_a_A_log, m_a_D, m_a_gnorm,
                            m_a_out_proj, m_kv_norm, m_w_kv, m_b_kv, m_k_norm, m_b_norm, m_w_q, m_b_q, m_q_norm,
                            m_sinks, m_w_o, m_b_o, m_f_norm, m_f_w_in, m_f_conv_w, m_f_conv_b, m_f_w_down)))
    vl = dict(zip(WEIGHTS, (v_a_norm, v_a_in_proj, v_a_conv_w, v_a_conv_b, v_a_dt_bias, v_a_A_log, v_a_D, v_a_gnorm,
                            v_a_out_proj, v_kv_norm, v_w_kv, v_b_kv, v_k_norm, v_b_norm, v_w_q, v_b_q, v_q_norm,
                            v_sinks, v_w_o, v_b_o, v_f_norm, v_f_w_in, v_f_conv_w, v_f_conv_b, v_f_w_down)))
    xi, yi, ci = _coords()
    me = 2 * xi + yi
    S = x.shape[1]

    def block_of(n, layer):
        t = wl[n]
        return t if layer is None else t[layer]

    rows = lambda t: t.reshape(-1, t.shape[-1])
    c_idx = jnp.reshape(ci, (1,)).astype(jnp.int32)
    me_c = jnp.stack([me, ci]).astype(jnp.int32)
    early = ("in_proj",)
    late = (("out_proj", "f_in0", "f_down0"), ("w_kv", "w_q", "w_o", "f_in1", "f_down1"))
    shards = {name: _halves(block_of(wn, layer).astype(BF16)) for name, wn, layer in MATS}

    sp = _pack([wl[n] for n, _ in SMALL_CUT], 8, 128, F32)
    gathered, gs = _gather_weights([shards[k] for k in early], sp)
    gt = {k: t.reshape(N_CHIPS, -1, t.shape[-1]) for k, t in zip(early, gathered)}
    started = {0: _gather_start([shards[k] for k in late[0]], gs, name="gather_late_start0")}
    full = {n: wl[n] for n in SMALL_REP}
    gs = gs.reshape(N_CHIPS, -1)
    off = 0
    for n, ax in SMALL_CUT:
        shp = wl[n].shape
        size = math.prod(shp)
        piece = jnp.moveaxis(gs[:, off:off + size].reshape((N_CHIPS,) + shp), 0, ax)
        full[n] = piece.reshape(shp[:ax] + (N_CHIPS * shp[ax],) + shp[ax + 1:])
        off += size
    w = _prep_small(full, {})
    w["w_zx"], w["w_dt"] = _join_in_proj(gt["in_proj"])
    w["dep"] = started[0][4]

    class Comm:
        flight = []
        reduced = {}

        forwarding = {}

        def late_start(self, part, after):
            started[part] = _gather_start([shards[k] for k in late[part]], after, name=f"gather_late_start{part}")
            return started[part][4]

        def late_arrived(self, part, after):
            send, recv, shs, lands, _ = started[part]
            lands = _gather_wait(send, recv, shs, lands, after[0], name=f"gather_late_wait{part}")
            send, recv, lands, token = _start_copies(_forward_copies, list(lands), 3 * len(lands), after,
                                                     name=f"gather_late_forward_start{part}")
            self.forwarding[part] = (send, recv, lands)
            return token

        def late_weights(self, w, after, part):
            send, recv, lands = self.forwarding[part]
            lands = _wait_copies(_forward_copies, send, recv, lands, [after], name=f"gather_late_forward_wait{part}")
            lt = {k: t.reshape(N_CHIPS, -1, t.shape[-1]) for k, t in zip(late[part], lands)}
            w = dict(w)
            if part == 0:
                w["a_out_proj"], w["f_w_in"], w["f_w_down"] = rows(lt["out_proj"]), [lt["f_in0"]], [rows(lt["f_down0"])]
            else:
                w["w_kv"], w["w_q"], w["w_o"] = (rows(lt[k]) for k in ("w_kv", "w_q", "w_o"))
                w["f_w_in"], w["f_w_down"] = w["f_w_in"] + [lt["f_in1"]], w["f_w_down"] + [rows(lt["f_down1"])]
            return w

        def advance(self, after, group=None, tensors=None):
            token = None
            for grp in list(self.flight):
                tag, n = grp["tag"], len(grp["names"])
                dep = list(after) + ([] if token is None else [token])
                if grp["stage"] == "sibling":
                    arrs = _wait_copies(_sibling_copies, grp["send"], grp["recv"], grp["arrays"], dep, name=f"rs_sibling_wait{tag}")
                    pairs = _rs_add_pair(arrs[:n], arrs[n:], c_idx, name=f"rs_add_pair{tag}")
                    send, recv, ps, lands, token = _rs_chips_start(pairs, dep, name=f"rs_chips_start{tag}")
                    grp.update(stage="chips", send=send, recv=recv, ps=ps, lands=lands)
                elif grp["stage"] == "chips":
                    ps, rs = _rs_chips_wait(grp["send"], grp["recv"], grp["ps"], grp["lands"], dep, name=f"rs_chips_wait{tag}")
                    halves = _rs_add_chips(ps, rs, me_c, name=f"rs_add_chips{tag}")
                    send, recv, arrs, token = _start_copies(_join_copies, halves, n, dep, name=f"rs_join_start{tag}")
                    grp.update(stage="join", send=send, recv=recv, arrays=arrs)
                else:
                    joined = _wait_copies(_join_copies, grp["send"], grp["recv"], grp["arrays"], dep, name=f"rs_join_wait{tag}")
                    self.reduced.update({k: rows(t) for k, t in zip(grp["names"], joined)})
                    self.flight.remove(grp)
            if group is not None:
                names = list(tensors)
                glist = [tensors[k].reshape(N_CHIPS, 2, -1, tensors[k].shape[-1]) for k in names]
                lands = [lax.empty((N_CHIPS,) + gq.shape[2:], gq.dtype) for gq in glist]
                dep = list(after) + ([] if token is None else [token])
                send, recv, arrs, token = _start_copies(_sibling_copies, glist + lands, len(names), dep,
                                                        name=f"rs_sibling_start{group}")
                self.flight.append(dict(tag=group, names=names, stage="sibling", send=send, recv=recv, arrays=arrs))
            return token

    comm = Comm()

    posf = positions.reshape(S, 1).astype(F32)
    loss_part, dx0, gr, tok = _local_step(x[0], posf, loss_target[0], w, comm)
    g = _small_grads(gr)

    small_names = [n for n, _ in SMALL_CUT] + list(SMALL_REP)
    sv = _pack([g[n] for n in small_names] + [loss_part[0:1, 0:1]], 8, 128, F32)
    s_send, s_recv, sv, s_land, s_token = _small_start(sv, tok, name="small_start")

    grads, delta, new_m, new_v = {}, {}, {}, {}

    def update(wn, dep):
        gl = [comm.reduced[name] for name, n2, _ in MATS if n2 == wn]
        shp = wl[wn].shape
        three = (len(gl),) + gl[0].shape
        flip = shp[-1] % 128 != 0
        view = (lambda t: t.reshape(three).transpose(0, 2, 1)) if flip else (lambda t: t.reshape(three))
        back = (lambda t: t.transpose(0, 2, 1).reshape(shp)) if flip else (lambda t: t.reshape(shp))
        if flip:
            gl = [t.T for t in gl]
        d, mn, vn, go = _adamw(view(wl[wn]), gl, view(ml[wn]), view(vl[wn]), name="adamw_" + wn, dep=dep)
        grads[wn], delta[wn], new_m[wn], new_v[wn] = back(go), back(d), back(mn), back(vn)
        return d

    first = [update(wn, s_token) for wn in ("w_q", "w_o", "w_kv")]
    tok = comm.advance(first)
    second = [update(wn, tok) for wn in ("f_w_in", "f_w_down", "a_out_proj")]
    comm.advance(second)
    done = first + second

    sv, s_land = _small_wait(s_send, s_recv, sv, s_land, done, name="small_wait")
    sred = _small_sum(sv, s_land, jnp.reshape(2 * me + ci, (1,)).astype(jnp.int32)).reshape(-1)
    small_shapes = [g[n].shape for n in small_names] + [(1,)]
    sg = dict(zip(small_names + ["loss"], _unpack(sred, small_shapes)))
    loss = sg["loss"].reshape(())
    g_small = {}
    for n, ax in SMALL_CUT:
        size = wl[n].shape[ax]
        g_small[n] = lax.dynamic_slice_in_dim(sg[n], me * size, size, axis=ax)
    for n in SMALL_REP:
        g_small[n] = sg[n].reshape(wl[n].shape)

    pk = lambda d: _pack([d[n] for n in small_names], 8, 128, F32)[None]
    d, mn, vn, _ = _adamw(pk(wl), [pk(g_small)[0]], pk(ml), pk(vl), name="adamw_small")
    comm.advance([d])
    update("a_in_proj", None)
    shapes = [wl[n].shape for n in small_names]
    for n, dd, mm, vv in zip(small_names, _unpack(d.reshape(-1), shapes), _unpack(mn.reshape(-1), shapes),
                             _unpack(vn.reshape(-1), shapes)):
        grads[n], delta[n], new_m[n], new_v[n] = g_small[n], dd, mm, vv

    return (loss, dx0[None], *[grads[n] for n in WEIGHTS], *[delta[n] for n in WEIGHTS],
            *[new_m[n] for n in WEIGHTS], *[new_v[n] for n in WEIGHTS])
```

```python
import math

import jax
import jax.numpy as jnp
from jax import lax
from jax.experimental import pallas as pl
from jax.experimental.pallas import tpu as pltpu

F32 = jnp.float32
BF16 = jnp.bfloat16

EPS = 1e-5
CHUNK = 256
WINDOW = 128
HEAD = 64
SSM_HEADS = 32
SSM_GROUPS = 8
SSM_STATE = 128
ATT_KV = 4
ATT_G = 4
ROPE_THETA = 10000.0
NEG = -1e30
N_CHIPS = 4
VMEM_LIMIT = 56 * 1024 * 1024

ADAM_LR, ADAM_B1, ADAM_B2, ADAM_EPS, ADAM_WD, ADAM_STEP = 0.001, 0.9, 0.999, 1e-08, 0.01, 10


def _cp(n_axes):
    return pltpu.CompilerParams(dimension_semantics=("arbitrary",) * n_axes, vmem_limit_bytes=VMEM_LIMIT)


def _pick(dim, prefs):
    for p in prefs:
        if dim % p == 0:
            return p
    return dim


def _iota(shape, dim):
    return lax.broadcasted_iota(jnp.int32, shape, dim)


def _dot(a, b, ca=1, cb=0):
    return lax.dot_general(a, b, (((ca,), (cb,)), ((), ())), preferred_element_type=F32)


def _dot3(x, ind):
    h = x.astype(BF16)
    r = x - h.astype(F32)
    m = r.astype(BF16)
    lo = (r - m.astype(F32)).astype(BF16)
    return _dot(h, ind) + _dot(m, ind) + _dot(lo, ind)


def _sigmoid(x):
    return jax.nn.sigmoid(x)


def _mm(a, b, *, name, ta=False, tb=False, bias=None, res=None, out_dtype=F32, b_koff=0, tm=None, tn=None, tk=None,
        dims=None, a_spec=None, b_spec=None, o_spec=None, o_shape=None, dep=None, more=(), target=None,
        rms=None, rms_colsum=False):
    if dims is not None:
        M, N, K = dims
    else:
        if ta:
            K, M = a.shape
        else:
            M, K = a.shape
        N = b.shape[0] if tb else b.shape[1]
    tm = tm or _pick(M, (1024, 1408, 512, 256, 128))
    tn = tn or _pick(N, (512, 1408, 256, 128))
    tk = tk or (K if K <= 2048 else _pick(K, (2048, 1408, 1024, 512)))
    assert M % tm == 0 and N % tn == 0 and K % tk == 0 and b_koff % tk == 0
    nk = K // tk
    kb0 = b_koff // tk
    has_bias, has_res = bias is not None, res is not None

    def body(*refs):
        a_ref, b_ref = refs[0], refs[1]
        pos = 2
        bias_ref = res_ref = acc_ref = None
        if has_bias:
            bias_ref = refs[pos]
            pos += 1
        if has_res:
            res_ref = refs[pos]
            pos += 1
        if dep is not None:
            pos += 1
        extra = refs[pos:pos + 2 * len(more)]
        pos += 2 * len(more)
        tgt_ref = lp_ref = rx_ref = rg_ref = rd_ref = dg_ref = cs_ref = None
        if target is not None:
            tgt_ref = refs[pos]
            pos += 1
        if rms is not None:
            rx_ref, rg_ref, rd_ref = refs[pos:pos + 3]
            pos += 3
        o_ref = refs[pos]
        pos += 1
        if target is not None:
            lp_ref = refs[pos]
            pos += 1
        if rms is not None:
            dg_ref = refs[pos]
            pos += 1
            if rms_colsum:
                cs_ref = refs[pos]
                pos += 1
        if nk > 1:
            acc_ref = refs[pos]
        part = _dot(a_ref[...].astype(BF16), b_ref[...].astype(BF16), 0 if ta else 1, 1 if tb else 0)
        for q in range(len(more)):
            part = part + _dot(extra[2 * q][...].astype(BF16), extra[2 * q + 1][...].astype(BF16),
                               0 if ta else 1, 1 if tb else 0)

        def finish(acc):
            if has_bias:
                acc = acc + bias_ref[...]
            if has_res:
                acc = acc + res_ref[...]
            if target is not None:
                err = acc - tgt_ref[...]
                acc = err * (1.0 / N)
                part_loss = jnp.sum(jnp.sum(err * err, axis=1, keepdims=True), axis=0, keepdims=True) * (0.5 / N)
                first = (pl.program_id(0) == 0) & (pl.program_id(1) == 0)

                @pl.when(first)
                def _():
                    lp_ref[...] = jnp.broadcast_to(part_loss, lp_ref.shape)

                @pl.when(jnp.logical_not(first))
                def _():
                    lp_ref[...] += jnp.broadcast_to(part_loss, lp_ref.shape)

            if rms is not None:
                xv = rx_ref[...]
                r = lax.rsqrt(jnp.mean(xv * xv, axis=-1, keepdims=True) + EPS)
                xh = xv * r
                dxh = acc * rg_ref[...]
                dg_part = jnp.sum(acc * xh, axis=0, keepdims=True)
                acc = rd_ref[...] + r * (dxh - xh * jnp.mean(dxh * xh, axis=-1, keepdims=True))
                cs_part = jnp.sum(acc, axis=0, keepdims=True) if rms_colsum else None
                first_rows = pl.program_id(0) == 0

                @pl.when(first_rows)
                def _():
                    dg_ref[...] = dg_part
                    if rms_colsum:
                        cs_ref[...] = cs_part

                @pl.when(jnp.logical_not(first_rows))
                def _():
                    dg_ref[...] += dg_part
                    if rms_colsum:
                        cs_ref[...] += cs_part

            o_ref[...] = acc.astype(out_dtype)

        if nk == 1:
            finish(part)
        else:
            k = pl.program_id(2)

            @pl.when(k == 0)
            def _():
                acc_ref[...] = part

            @pl.when(k > 0)
            def _():
                acc_ref[...] += part

            @pl.when(k == nk - 1)
            def _():
                finish(acc_ref[...])

    if a_spec is None:
        a_spec = pl.BlockSpec((tk, tm), lambda i, j, k: (k, i)) if ta else pl.BlockSpec((tm, tk), lambda i, j, k: (i, k))
    if b_spec is None:
        b_spec = (pl.BlockSpec((tn, tk), lambda i, j, k: (j, k + kb0)) if tb
                  else pl.BlockSpec((tk, tn), lambda i, j, k: (k + kb0, j)))
    if o_spec is None:
        o_spec = pl.BlockSpec((tm, tn), lambda i, j, k: (i, j))
    in_specs, args = [a_spec, b_spec], [a, b]
    if has_bias:
        in_specs.append(pl.BlockSpec((1, tn), lambda i, j, k: (0, j)))
        args.append(bias)
    if has_res:
        in_specs.append(pl.BlockSpec((tm, tn), lambda i, j, k: (i, j)))
        args.append(res)
    if dep is not None:
        in_specs.append(pl.BlockSpec(memory_space=pl.ANY))
        args.append(dep)
    for piece in more:
        a2, sa, b2, sb = piece if len(piece) == 4 else (a, piece[0], b, piece[1])
        in_specs += [sa, sb]
        args += [a2, b2]
    out_specs, out_shape = [o_spec], [jax.ShapeDtypeStruct(o_shape or (M, N), out_dtype)]
    if target is not None:
        in_specs.append(pl.BlockSpec((tm, tn), lambda i, j, k: (i, j)))
        args.append(target)
        out_specs.append(pl.BlockSpec((8, 128), lambda i, j, k: (0, 0)))
        out_shape.append(jax.ShapeDtypeStruct((8, 128), F32))
    if rms is not None:
        assert tn == N and nk == 1
        row, vec = pl.BlockSpec((tm, N), lambda i, j, k: (i, 0)), pl.BlockSpec((1, N), lambda i, j, k: (0, 0))
        in_specs += [row, vec, row]
        args += list(rms)
        out_specs += [vec] * (2 if rms_colsum else 1)
        out_shape += [jax.ShapeDtypeStruct((1, N), F32)] * (2 if rms_colsum else 1)
    if len(out_specs) == 1:
        out_specs, out_shape = out_specs[0], out_shape[0]
    return pl.pallas_call(
        body, name=name, grid=(M // tm, N // tn, nk), in_specs=in_specs, out_specs=out_specs, out_shape=out_shape,
        scratch_shapes=[pltpu.VMEM((tm, tn), F32)] if nk > 1 else [],
        compiler_params=_cp(3),
    )(*args)


def _norm_mm(x, gain, b, *, name, bias=None, N=None, tn=None, b_spec=None, dep=None):
    M, K = x.shape
    N = N or b.shape[1]
    tm = _pick(M, (1024, 512, 256))
    tn = tn or _pick(N, (512, 1408, 256, 128))
    has_bias = bias is not None

    def body(*refs):
        x_ref, g_ref, b_ref = refs[:3]
        pos = 3 + (1 if has_bias else 0) + (0 if dep is None else 1)
        o_ref, h_ref = refs[pos], refs[pos + 1]

        @pl.when(pl.program_id(1) == 0)
        def _():
            xv = x_ref[...]
            h_ref[...] = (xv * lax.rsqrt(jnp.mean(xv * xv, axis=-1, keepdims=True) + EPS) * g_ref[...]).astype(BF16)

        acc = _dot(h_ref[...], b_ref[...].astype(BF16))
        if has_bias:
            acc = acc + refs[3][...]
        o_ref[...] = acc

    in_specs = [pl.BlockSpec((tm, K), lambda i, j: (i, 0)), pl.BlockSpec((1, K), lambda i, j: (0, 0)),
                b_spec or pl.BlockSpec((K, tn), lambda i, j: (0, j))]
    args = [x, gain, b]
    if has_bias:
        in_specs.append(pl.BlockSpec((1, tn), lambda i, j: (0, j)))
        args.append(bias)
    if dep is not None:
        in_specs.append(pl.BlockSpec(memory_space=pl.ANY))
        args.append(dep)
    return pl.pallas_call(
        body, name=name, grid=(M // tm, N // tn), in_specs=in_specs,
        out_specs=[pl.BlockSpec((tm, tn), lambda i, j: (i, j)), pl.BlockSpec((tm, K), lambda i, j: (i, 0))],
        out_shape=[jax.ShapeDtypeStruct((M, N), F32), jax.ShapeDtypeStruct((M, K), BF16)], compiler_params=_cp(2),
    )(*args)


def _rms_bwd(x, gains, dhs, dres, *, name, tr=256, want_colsum=False):
    S, D = x.shape
    n = len(gains)
    steps = S // tr

    def body(*refs):
        x_ref = refs[0]
        g_refs = refs[1:1 + n]
        dh_refs = refs[1 + n:1 + 2 * n]
        dres_ref = refs[1 + 2 * n]
        dx_ref = refs[2 + 2 * n]
        dg_refs = refs[3 + 2 * n:3 + 3 * n]
        cs_ref = refs[3 + 3 * n] if want_colsum else None
        i = pl.program_id(0)
        xv = x_ref[...]
        r = lax.rsqrt(jnp.mean(xv * xv, axis=-1, keepdims=True) + EPS)
        xh = xv * r
        dx = dres_ref[...]
        for q in range(n):
            dh = dh_refs[q][...]
            dxh = dh * g_refs[q][...]
            dx = dx + r * (dxh - xh * jnp.mean(dxh * xh, axis=-1, keepdims=True))
            part = jnp.sum(dh * xh, axis=0, keepdims=True)

            @pl.when(i == 0)
            def _():
                dg_refs[q][...] = part

            @pl.when(i > 0)
            def _():
                dg_refs[q][...] += part

        dx_ref[...] = dx
        if want_colsum:
            cpart = jnp.sum(dx, axis=0, keepdims=True)

            @pl.when(i == 0)
            def _():
                cs_ref[...] = cpart

            @pl.when(i > 0)
            def _():
                cs_ref[...] += cpart

    row = pl.BlockSpec((tr, D), lambda i: (i, 0))
    vec = pl.BlockSpec((1, D), lambda i: (0, 0))
    n_vec_out = n + (1 if want_colsum else 0)
    outs = pl.pallas_call(
        body, name=name, grid=(steps,), in_specs=[row] + [vec] * n + [row] * n + [row],
        out_specs=[row] + [vec] * n_vec_out,
        out_shape=[jax.ShapeDtypeStruct((S, D), F32)] + [jax.ShapeDtypeStruct((1, D), F32)] * n_vec_out,
        compiler_params=_cp(1),
    )(x, *gains, *dhs, dres)
    return outs


def _colsum(x, *, name, tr=256):
    S, D = x.shape

    def body(x_ref, o_ref):
        i = pl.program_id(0)
        part = jnp.sum(x_ref[...].astype(F32), axis=0, keepdims=True)

        @pl.when(i == 0)
        def _():
            o_ref[...] = part

        @pl.when(i > 0)
        def _():
            o_ref[...] += part

    return pl.pallas_call(
        body, name=name, grid=(S // tr,), in_specs=[pl.BlockSpec((tr, D), lambda i: (i, 0))],
        out_specs=pl.BlockSpec((1, D), lambda i: (0, 0)), out_shape=jax.ShapeDtypeStruct((1, D), F32),
        compiler_params=_cp(1),
    )(x)


STRIP = 64
HALO = 8


def _strips(S, tc):
    return [(r0, slice(l0, l0 + 128)) for l0 in range(0, tc, 128) for r0 in range(S - STRIP, -1, -STRIP)]


def _with_halo(ref, r0, ls):
    if r0 == 0:
        return jnp.concatenate([jnp.zeros((HALO, 128), F32), ref[0:STRIP, ls]], axis=0)
    return ref[r0 - HALO:r0 + STRIP, ls]


def _conv_strip(xw, w_ref, b_ref, ls, width):
    acc = b_ref[:, ls] + w_ref[pl.ds(width - 1, 1), ls] * xw[HALO:]
    shifted = []
    for s in range(1, width):
        xs = pltpu.roll(xw, s, axis=0)[HALO:]
        shifted.append(xs)
        acc = acc + w_ref[pl.ds(width - 1 - s, 1), ls] * xs
    return acc, shifted


def _conv_strip_back(dacc, after, xc, shifted, w_ref, ls, width):
    ext = jnp.concatenate([dacc, after], axis=0)
    dx = w_ref[pl.ds(width - 1, 1), ls] * dacc
    dws = [None] * width
    dws[width - 1] = jnp.sum(dacc * xc, axis=0, keepdims=True)
    for s in range(1, width):
        dx = dx + w_ref[pl.ds(width - 1 - s, 1), ls] * pltpu.roll(ext, STRIP + HALO - s, axis=0)[:STRIP]
        dws[width - 1 - s] = jnp.sum(dacc * shifted[s - 1], axis=0, keepdims=True)
    return dx, dws, jnp.sum(dacc, axis=0, keepdims=True)


def _conv_back_block(S, tc, width, w_ref, b_ref, x_ref, dacc_of, dx_store, dw_ref, db_ref):
    for l0 in range(0, tc, 128):
        ls = slice(l0, l0 + 128)
        after = jnp.zeros((HALO, 128), F32)
        tot = None
        for r0 in range(S - STRIP, -1, -STRIP):
            xw = _with_halo(x_ref, r0, ls)
            acc, shifted = _conv_strip(xw, w_ref, b_ref, ls, width)
            dacc = dacc_of(r0, ls, acc, _sigmoid(acc))
            dx, dws, db = _conv_strip_back(dacc, after, xw[HALO:], shifted, w_ref, ls, width)
            dx_store(r0, ls, dx)
            after = dacc[:HALO]
            part = dws + [db]
            tot = part if tot is None else [p + q for p, q in zip(tot, part)]
        for k in range(width):
            dw_ref[pl.ds(k, 1), ls] = tot[k]
        db_ref[:, ls] = tot[width]


def _conv_silu_fwd(xin, col0, C, w, b, *, name, tc=512):
    S = xin.shape[0]
    width = w.shape[0]
    off = col0 // tc

    def body(x_ref, w_ref, b_ref, o_ref):
        for r0, ls in _strips(S, tc):
            acc, _ = _conv_strip(_with_halo(x_ref, r0, ls), w_ref, b_ref, ls, width)
            o_ref[r0:r0 + STRIP, ls] = acc * _sigmoid(acc)

    return pl.pallas_call(
        body, name=name, grid=(C // tc,),
        in_specs=[pl.BlockSpec((S, tc), lambda j: (0, j + off)), pl.BlockSpec((width, tc), lambda j: (0, j)),
                  pl.BlockSpec((1, tc), lambda j: (0, j))],
        out_specs=pl.BlockSpec((S, tc), lambda j: (0, j)), out_shape=jax.ShapeDtypeStruct((S, C), F32),
        compiler_params=_cp(1),
    )(xin, w, b)


def _conv_silu_bwd(xin, col0, C, w, b, douts, *, name, tc=256):
    S = xin.shape[0]
    width = w.shape[0]
    off = col0 // tc
    nd = len(douts)
    ranges = [(o // tc, (o + d.shape[1]) // tc) for d, o in douts]

    def body(*refs):
        x_ref, w_ref, b_ref = refs[0], refs[1], refs[2]
        d_refs = refs[3:3 + nd]
        dx_ref, dw_ref, db_ref = refs[3 + nd], refs[4 + nd], refs[5 + nd]
        j = pl.program_id(0)

        def dacc_of(r0, ls, acc, sg):
            dout = jnp.zeros((STRIP, 128), F32)
            for q in range(nd):
                lo, hi = ranges[q]
                dout = dout + jnp.where((j >= lo) & (j < hi), d_refs[q][r0:r0 + STRIP, ls], 0.0)
            return dout * (sg * (1.0 + acc * (1.0 - sg)))

        def dx_store(r0, ls, dx):
            dx_ref[r0:r0 + STRIP, ls] = dx.astype(BF16)

        _conv_back_block(S, tc, width, w_ref, b_ref, x_ref, dacc_of, dx_store, dw_ref, db_ref)

    d_specs = [pl.BlockSpec((S, tc), (lambda j, lo=lo, hi=hi: (0, jnp.clip(j - lo, 0, hi - lo - 1)))) for lo, hi in ranges]
    return pl.pallas_call(
        body, name=name, grid=(C // tc,),
        in_specs=[pl.BlockSpec((S, tc), lambda j: (0, j + off)), pl.BlockSpec((width, tc), lambda j: (0, j)),
                  pl.BlockSpec((1, tc), lambda j: (0, j))] + d_specs,
        out_specs=[pl.BlockSpec((S, tc), lambda j: (0, j)), pl.BlockSpec((width, tc), lambda j: (0, j)),
                   pl.BlockSpec((1, tc), lambda j: (0, j))],
        out_shape=[jax.ShapeDtypeStruct((S, C), BF16), jax.ShapeDtypeStruct((width, C), F32),
                   jax.ShapeDtypeStruct((1, C), F32)],
        compiler_params=_cp(1),
    )(xin, w, b, *[d for d, _ in douts])


def _ffn_act_fwd(u, w, b, *, name, tc=256):
    S, F2 = u.shape
    Fd = F2 // 2
    width = w.shape[0]
    nb = Fd // tc

    def body(g_ref, v_ref, w_ref, b_ref, o_ref):
        for r0, ls in _strips(S, tc):
            acc, _ = _conv_strip(_with_halo(g_ref, r0, ls), w_ref, b_ref, ls, width)
            o_ref[r0:r0 + STRIP, ls] = (acc * _sigmoid(acc) * v_ref[r0:r0 + STRIP, ls]).astype(BF16)

    return pl.pallas_call(
        body, name=name, grid=(nb,),
        in_specs=[pl.BlockSpec((S, tc), lambda j: (0, j)), pl.BlockSpec((S, tc), lambda j: (0, j + nb)),
                  pl.BlockSpec((width, tc), lambda j: (0, j)), pl.BlockSpec((1, tc), lambda j: (0, j))],
        out_specs=pl.BlockSpec((S, tc), lambda j: (0, j)), out_shape=jax.ShapeDtypeStruct((S, Fd), BF16),
        compiler_params=_cp(1),
    )(u, u, w, b)


def _ffn_act_bwd(u, w, b, da, *, name, tc=256):
    S, F2 = u.shape
    Fd = F2 // 2
    width = w.shape[0]
    nb = Fd // tc

    def body(g_ref, v_ref, w_ref, b_ref, da_ref, du_ref, dw_ref, db_ref, a_ref):
        def dacc_of(r0, ls, acc, sg):
            rs = slice(r0, r0 + STRIP)
            dav, val, silu = da_ref[rs, ls], v_ref[rs, ls], acc * sg
            a_ref[rs, ls] = (silu * val).astype(BF16)
            du_ref[1, rs, ls] = (dav * silu).astype(BF16)
            return dav * val * (sg * (1.0 + acc * (1.0 - sg)))

        def dx_store(r0, ls, dx):
            du_ref[0, r0:r0 + STRIP, ls] = dx.astype(BF16)

        _conv_back_block(S, tc, width, w_ref, b_ref, g_ref, dacc_of, dx_store, dw_ref, db_ref)

    blk = pl.BlockSpec((S, tc), lambda j: (0, j))
    return pl.pallas_call(
        body, name=name, grid=(nb,),
        in_specs=[blk, pl.BlockSpec((S, tc), lambda j: (0, j + nb)), pl.BlockSpec((width, tc), lambda j: (0, j)),
                  pl.BlockSpec((1, tc), lambda j: (0, j)), blk],
        out_specs=[pl.BlockSpec((2, S, tc), lambda j: (0, 0, j)), pl.BlockSpec((width, tc), lambda j: (0, j)),
                   pl.BlockSpec((1, tc), lambda j: (0, j)), blk],
        out_shape=[jax.ShapeDtypeStruct((2, S, Fd), BF16),
                   jax.ShapeDtypeStruct((width, Fd), F32), jax.ShapeDtypeStruct((1, Fd), F32),
                   jax.ShapeDtypeStruct((S, Fd), BF16)],
        compiler_params=_cp(1),
    )(u, u, w, b, da)


def _ssd_prep(dtr, dt_bias, a_log, *, name="ssd_prep"):
    S = dtr.shape[0]

    def body(d_ref, b_ref, al_ref, dt_ref, ac_ref, sg_ref, act_ref):
        lane = _iota((CHUNK, 128), 1)
        valid = lane < SSM_HEADS
        z = d_ref[...] + b_ref[...]
        dt = jnp.where(valid, jnp.maximum(z, 0.0) + jnp.log(1.0 + jnp.exp(-jnp.abs(z))), 0.0)
        a = dt * (-jnp.exp(al_ref[...]))
        row = _iota((CHUNK, 128), 0)
        k = 1
        while k < CHUNK:
            a = a + jnp.where(row >= k, pltpu.roll(a, k, axis=0), 0.0)
            k *= 2
        sg = jnp.where(valid, _sigmoid(z), 0.0)
        for arr, ref in ((dt, dt_ref), (a, ac_ref), (sg, sg_ref)):
            for g in range(SSM_GROUPS):
                ref[g] = jnp.where(lane < 4, arr if g == 0 else pltpu.roll(arr, 128 - 4 * g, axis=1), 0.0)
        act_ref[...] = a.T[:SSM_HEADS, :]

    blk = pl.BlockSpec((CHUNK, 128), lambda i: (i, 0))
    vec = pl.BlockSpec((1, 128), lambda i: (0, 0))
    grp = pl.BlockSpec((SSM_GROUPS, CHUNK, 128), lambda i: (0, i, 0))
    return pl.pallas_call(
        body, name=name, grid=(S // CHUNK,), in_specs=[blk, vec, vec],
        out_specs=[grp, grp, grp, pl.BlockSpec((SSM_HEADS, CHUNK), lambda i: (0, i))],
        out_shape=[jax.ShapeDtypeStruct((SSM_GROUPS, S, 128), F32)] * 3 + [jax.ShapeDtypeStruct((SSM_HEADS, S), F32)],
        compiler_params=_cp(1),
    )(dtr, dt_bias, a_log)


SSD_GPS = 4


def _expand4(v, lanes):
    out = jnp.broadcast_to(v[:, 3:4], lanes.shape)
    for hh in (2, 1, 0):
        out = jnp.where(lanes < 64 * (hh + 1), v[:, hh:hh + 1], out)
    return out


def _ssd_fwd(xbc, dt_g, ac_g, ac_t, *, name="ssd_fwd", dep=None):
    S = xbc.shape[0]
    nc = S // CHUNK
    Lc = CHUNK

    def body(x_ref, b_ref, c_ref, dt_ref, ac_ref, act_ref, *rest):
        y_ref, st_out_ref, st_ref = rest[-3:]
        g2 = pl.program_id(0)
        c = pl.program_id(1)

        @pl.when(c == 0)
        def _():
            st_ref[...] = jnp.zeros_like(st_ref)

        causal = _iota((Lc, Lc), 0) >= _iota((Lc, Lc), 1)
        lane256 = _iota((Lc, 256), 1)
        lane128 = _iota((Lc, 128), 1)
        row128 = _iota((128, 128), 0)
        for gg in range(SSD_GPS):
            g = SSD_GPS * g2 + gg
            bv = b_ref[:, 128 * gg:128 * (gg + 1)]
            cbf = c_ref[:, 128 * gg:128 * (gg + 1)].astype(BF16)
            cb = _dot(cbf, bv.astype(BF16), 1, 1)
            dtg, acg = dt_ref[gg], ac_ref[gg]
            ac_last = ac_ref[gg, pl.ds(Lc - 1, 1), :]
            dt4 = _expand4(dtg, lane256)
            ac4 = _expand4(acg, lane256)
            e4 = jnp.exp(ac4)
            xdb = (x_ref[:, 256 * gg:256 * (gg + 1)] * dt4).astype(BF16)
            st_out_ref[gg] = st_ref[gg]
            for p in range(2):
                xd_p = xdb[:, 128 * p:128 * (p + 1)]
                st_p = st_ref[gg, p]
                ys, sn, cds = [], [], []
                for q in range(2):
                    hh = 2 * p + q
                    a_col = acg[:, hh:hh + 1]
                    a_row = act_ref[pl.ds(4 * g + hh, 1), :]
                    dec = jnp.exp(jnp.where(causal, a_col - a_row, NEG))
                    w = (cb * dec).astype(BF16)
                    ys.append(_dot(w, xd_p))
                    al = ac_last[:, hh:hh + 1]
                    dte = jnp.exp(al - a_col)
                    sn.append(_dot(xd_p, (bv * dte).astype(BF16), 0, 0))
                    cds.append(jnp.exp(al))
                y_diag = jnp.where(lane128 < 64, ys[0], ys[1])
                y_off = _dot(cbf, st_p.astype(BF16), 1, 1) * e4[:, 128 * p:128 * (p + 1)]
                y_ref[:, 256 * gg + 128 * p:256 * gg + 128 * (p + 1)] = y_diag + y_off
                st_ref[gg, p] = jnp.where(row128 < 64, st_p * cds[0] + sn[0], st_p * cds[1] + sn[1])

    G = SSD_GPS
    per_g = lambda g, c: (g, c, 0)
    return pl.pallas_call(
        body, name=name, grid=(SSM_GROUPS // G, nc),
        in_specs=[pl.BlockSpec((Lc, 256 * G), lambda g, c: (c, g)),
                  pl.BlockSpec((Lc, 128 * G), lambda g, c: (c, 16 // G + g)),
                  pl.BlockSpec((Lc, 128 * G), lambda g, c: (c, 24 // G + g)),
                  pl.BlockSpec((G, Lc, 128), per_g), pl.BlockSpec((G, Lc, 128), per_g),
                  pl.BlockSpec((SSM_HEADS, Lc), lambda g, c: (0, c))] + ([] if dep is None else [pl.BlockSpec(memory_space=pl.ANY)]),
        out_specs=[pl.BlockSpec((Lc, 256 * G), lambda g, c: (c, g)),
                   pl.BlockSpec((G, None, 2, 128, 128), lambda g, c: (g, c, 0, 0, 0))],
        out_shape=[jax.ShapeDtypeStruct((S, 2048), F32), jax.ShapeDtypeStruct((SSM_GROUPS, nc, 2, 128, 128), F32)],
        scratch_shapes=[pltpu.VMEM((G, 2, 128, 128), F32)], compiler_params=_cp(2),
    )(xbc, xbc, xbc, dt_g, ac_g, ac_t, *([] if dep is None else [dep]))


def _ssd_bwd(xbc, dt_g, ac_g, ac_t, states, dy, dexp, *, name="ssd_bwd", dep=None):
    S = xbc.shape[0]
    nc = S // CHUNK
    Lc = CHUNK

    def body(x_ref, b_ref, c_ref, dt_ref, ac_ref, act_ref, st_ref, dy_ref, d_ref, *rest):
        dx_ref, db_ref, dc_ref, dh_ref, ds_ref = rest[-5:]
        g2 = pl.program_id(0)
        cc = pl.program_id(1)

        @pl.when(cc == 0)
        def _():
            ds_ref[...] = jnp.zeros_like(ds_ref)

        causal = _iota((Lc, Lc), 0) >= _iota((Lc, Lc), 1)
        lane256 = _iota((Lc, 256), 1)
        lane128 = _iota((Lc, 128), 1)
        row128 = _iota((128, 128), 0)
        ind_rows = _iota((256, 128), 0) >> 6
        ind_cols = _iota((256, 128), 1)
        ind_a = (ind_rows == ind_cols).astype(BF16)
        ind_b = (ind_rows + 4 == ind_cols).astype(BF16)
        for gg in range(SSD_GPS):
            g = SSD_GPS * g2 + gg
            bv = b_ref[:, 128 * gg:128 * (gg + 1)]
            cv = c_ref[:, 128 * gg:128 * (gg + 1)]
            bbf, cbf = bv.astype(BF16), cv.astype(BF16)
            cb = _dot(cbf, bbf, 1, 1)
            dtg, acg = dt_ref[gg], ac_ref[gg]
            ac_last = ac_ref[gg, pl.ds(Lc - 1, 1), :]
            dt4 = _expand4(dtg, lane256)
            ac4 = _expand4(acg, lane256)
            acl4 = _expand4(ac_last, _iota((1, 256), 1))
            e4 = jnp.exp(ac4)
            dte4 = jnp.exp(acl4 - ac4)
            xv = x_ref[:, 256 * gg:256 * (gg + 1)]
            xd = xv * dt4
            xdb = xd.astype(BF16)
            dyv = dy_ref[:, 256 * gg:256 * (gg + 1)]
            dcb = jnp.zeros((Lc, Lc), F32)
            dc_acc = jnp.zeros((Lc, 128), F32)
            db_acc = jnp.zeros((Lc, 128), F32)
            u_parts, dxd_parts, ends = [], [], []
            for p in range(2):
                sl = slice(128 * p, 128 * (p + 1))
                xd_p, xdb_p, dy_p = xd[:, sl], xdb[:, sl], dyv[:, sl]
                dyb_p = dy_p.astype(BF16)
                e_p, dte_p = e4[:, sl], dte4[:, sl]
                sp = st_ref[gg, p]
                spb = sp.astype(BF16)
                dsn = ds_ref[gg, p]
                dsnb = dsn.astype(BF16)
                yds, dxds, cds = [], [], []
                for q in range(2):
                    hh = 2 * p + q
                    a_col = acg[:, hh:hh + 1]
                    a_row = act_ref[pl.ds(4 * g + hh, 1), :]
                    dec = jnp.exp(jnp.where(causal, a_col - a_row, NEG))
                    w = (cb * dec).astype(BF16)
                    head = (lane128 < 64) if q == 0 else (lane128 >= 64)
                    dym = jnp.where(head, dyb_p, jnp.zeros_like(dyb_p))
                    dw = _dot(dym, xdb_p, 1, 1)
                    dcb = dcb + dw * dec
                    yds.append(_dot(w, xdb_p))
                    dxds.append(_dot(w, dyb_p, 0, 0))
                    cds.append(jnp.exp(ac_last[:, hh:hh + 1]))
                y_diag = jnp.where(lane128 < 64, yds[0], yds[1])
                dxd_diag = jnp.where(lane128 < 64, dxds[0], dxds[1])
                y_off = _dot(cbf, spb, 1, 1) * e_p
                dgp = dy_p * e_p
                dgb = dgp.astype(BF16)
                dc_acc = dc_acc + _dot(dgb, spb)
                dsp = _dot(dgb, cbf, 0, 0)
                cd_col = jnp.where(row128[:, 0:1] < 64, cds[0], cds[1])
                qm = _dot(bbf, dsnb, 1, 1)
                dxd_state = dte_p * qm
                db_acc = db_acc + _dot((xd_p * dte_p).astype(BF16), dsnb)
                t_p = xd_p * dxd_state
                prod = dsn * sp
                e0 = jnp.sum(jnp.sum(jnp.where(row128 < 64, prod, 0.0), axis=1, keepdims=True), axis=0, keepdims=True)
                e1 = jnp.sum(jnp.sum(jnp.where(row128 >= 64, prod, 0.0), axis=1, keepdims=True), axis=0, keepdims=True)
                tcol = jnp.sum(t_p, axis=0, keepdims=True)
                lane1 = _iota((1, 128), 1)
                t0 = jnp.sum(jnp.where(lane1 < 64, tcol, 0.0), axis=1, keepdims=True)
                t1 = jnp.sum(jnp.where(lane1 >= 64, tcol, 0.0), axis=1, keepdims=True)
                ends.append(e0 * cds[0] + t0)
                ends.append(e1 * cds[1] + t1)
                ds_ref[gg, p] = dsn * cd_col + dsp
                u_parts.append(dyb_p.astype(F32) * y_diag - xdb_p.astype(F32) * dxd_diag + dy_p * y_off - t_p)
                dxd_parts.append(dxd_diag + dxd_state)
            dxd = jnp.concatenate(dxd_parts, axis=1)
            u_all = jnp.concatenate(u_parts, axis=1)
            dx_ref[:, 256 * gg:256 * (gg + 1)] = dxd * dt4 + dyv * d_ref[:, 256 * gg:256 * (gg + 1)]
            dcbb = dcb.astype(BF16)
            dc_ref[:, 128 * gg:128 * (gg + 1)] = dc_acc + _dot(dcbb, bbf)
            db_ref[:, 128 * gg:128 * (gg + 1)] = db_acc + _dot(dcbb, cbf, 0, 0)
            lane = _iota((Lc, 128), 1)
            endv = jnp.zeros((Lc, 128), F32)
            for hh in range(4):
                endv = jnp.where(lane == 8 + hh, ends[hh], endv)
            dh_ref[gg] = _dot3(dxd * xv, ind_a) + _dot3(u_all, ind_b) + endv

    G = SSD_GPS
    rev = lambda c: nc - 1 - c
    per_g = lambda g, c: (g, rev(c), 0)
    return pl.pallas_call(
        body, name=name, grid=(SSM_GROUPS // G, nc),
        in_specs=[pl.BlockSpec((Lc, 256 * G), lambda g, c: (rev(c), g)),
                  pl.BlockSpec((Lc, 128 * G), lambda g, c: (rev(c), 16 // G + g)),
                  pl.BlockSpec((Lc, 128 * G), lambda g, c: (rev(c), 24 // G + g)),
                  pl.BlockSpec((G, Lc, 128), per_g), pl.BlockSpec((G, Lc, 128), per_g),
                  pl.BlockSpec((SSM_HEADS, Lc), lambda g, c: (0, rev(c))),
                  pl.BlockSpec((G, None, 2, 128, 128), lambda g, c: (g, rev(c), 0, 0, 0)),
                  pl.BlockSpec((Lc, 256 * G), lambda g, c: (rev(c), g)),
                  pl.BlockSpec((1, 256 * G), lambda g, c: (0, g))] + ([] if dep is None else [pl.BlockSpec(memory_space=pl.ANY)]),
        out_specs=[pl.BlockSpec((Lc, 256 * G), lambda g, c: (rev(c), g)),
                   pl.BlockSpec((Lc, 128 * G), lambda g, c: (rev(c), g)),
                   pl.BlockSpec((Lc, 128 * G), lambda g, c: (rev(c), g)),
                   pl.BlockSpec((G, Lc, 128), per_g)],
        out_shape=[jax.ShapeDtypeStruct((S, 2048), F32), jax.ShapeDtypeStruct((S, 1024), F32),
                   jax.ShapeDtypeStruct((S, 1024), F32), jax.ShapeDtypeStruct((SSM_GROUPS, S, 128), F32)],
        scratch_shapes=[pltpu.VMEM((G, 2, 128, 128), F32)], compiler_params=_cp(2),
    )(xbc, xbc, xbc, dt_g, ac_g, ac_t, states, dy, dexp, *([] if dep is None else [dep]))


def _ssd_post(dhead, dt_g, sg_g, alog_g, *, name="ssd_post"):
    S = dhead.shape[1]
    nc = S // CHUNK
    Lc = CHUNK

    def body(dh_ref, dt_ref, sg_ref, al_ref, o_ref, s_ref):
        @pl.when(pl.program_id(0) == 0)
        def _():
            s_ref[...] = jnp.zeros_like(s_ref)

        lane = _iota((Lc, 128), 1)
        row = _iota((Lc, 128), 0)
        row8 = _iota((8, 128), 0)
        out = jnp.zeros((Lc, 128), F32)
        for g in range(SSM_GROUPS):
            dh = dh_ref[g]
            a_neg = -jnp.exp(al_ref[g])
            dac = jnp.where(lane < 4, pltpu.roll(dh, 124, axis=1), 0.0)
            end = jnp.where(lane < 4, pltpu.roll(dh, 120, axis=1), 0.0)
            k = 1
            while k < Lc:
                dac = dac + jnp.where(row < Lc - k, pltpu.roll(dac, Lc - k, axis=0), 0.0)
                k *= 2
            da = dac + end
            ddt = jnp.where(lane < 4, da * a_neg + dh, 0.0)
            ddtr = ddt * sg_ref[g]
            out = out + (ddtr if g == 0 else pltpu.roll(ddtr, 4 * g, axis=1))
            dal = jnp.sum(da * dt_ref[g], axis=0, keepdims=True) * a_neg
            dbias = jnp.sum(ddtr, axis=0, keepdims=True)
            part = jnp.where(row8 == 0, dal, jnp.where(row8 == 1, dbias, 0.0))
            s_ref[g] += part
        o_ref[...] = out.astype(BF16)

    grp = pl.BlockSpec((SSM_GROUPS, Lc, 128), lambda c: (0, c, 0))
    whole = lambda r: pl.BlockSpec((SSM_GROUPS, r, 128), lambda c: (0, 0, 0))
    return pl.pallas_call(
        body, name=name, grid=(nc,), in_specs=[grp, grp, grp, whole(1)],
        out_specs=[pl.BlockSpec((Lc, 128), lambda c: (c, 0)), whole(8)],
        out_shape=[jax.ShapeDtypeStruct((S, 128), BF16), jax.ShapeDtypeStruct((SSM_GROUPS, 8, 128), F32)],
        compiler_params=_cp(1),
    )(dhead, dt_g, sg_g, alog_g)


def _gate_fwd(y, xbc, zx, dexp, gn, *, name="gate_fwd", tr=256, dep=None):
    S = y.shape[0]
    W = 2048
    gw = W // SSM_GROUPS

    def body(y_ref, x_ref, z_ref, d_ref, g_ref, *rest):
        o_ref = rest[-1]
        z = z_ref[...]
        u = (y_ref[...] + x_ref[...] * d_ref[...]) * (z * _sigmoid(z))
        gv = g_ref[...]
        for q in range(SSM_GROUPS):
            sl = slice(gw * q, gw * (q + 1))
            uq = u[:, sl]
            r = lax.rsqrt(jnp.mean(uq * uq, axis=-1, keepdims=True) + EPS)
            o_ref[:, sl] = (uq * r * gv[:, sl]).astype(BF16)

    row = pl.BlockSpec((tr, W), lambda i: (i, 0))
    vec = pl.BlockSpec((1, W), lambda i: (0, 0))
    return pl.pallas_call(
        body, name=name, grid=(S // tr,),
        in_specs=[row, row, row, vec, vec] + ([] if dep is None else [pl.BlockSpec(memory_space=pl.ANY)]), out_specs=row,
        out_shape=jax.ShapeDtypeStruct((S, W), BF16), compiler_params=_cp(1),
    )(y, xbc, zx, dexp, gn, *([] if dep is None else [dep]))


def _gate_bwd(y, xbc, zx, dexp, gn, dout, *, name="gate_bwd", tr=256):
    S = y.shape[0]
    W = 2048
    gw = W // SSM_GROUPS
    steps = S // tr

    def body(y_ref, x_ref, z_ref, d_ref, g_ref, do_ref, dy_ref, dz_ref, dg_ref, dd_ref, acc_ref):
        i = pl.program_id(0)

        @pl.when(i == 0)
        def _():
            acc_ref[...] = jnp.zeros_like(acc_ref)

        z = z_ref[...]
        sg = _sigmoid(z)
        sz = z * sg
        xs = x_ref[...]
        yt = y_ref[...] + xs * d_ref[...]
        u = yt * sz
        gv = g_ref[...]
        do = do_ref[...]
        dgs = []
        for q in range(SSM_GROUPS):
            sl = slice(gw * q, gw * (q + 1))
            uq = u[:, sl]
            r = lax.rsqrt(jnp.mean(uq * uq, axis=-1, keepdims=True) + EPS)
            uh = uq * r
            dq = do[:, sl]
            duh = dq * gv[:, sl]
            duq = r * (duh - uh * jnp.mean(duh * uh, axis=-1, keepdims=True))
            dgs.append(jnp.sum(dq * uh, axis=0, keepdims=True))
            dyt = duq * sz[:, sl]
            dy_ref[:, sl] = dyt
            dz_ref[:, sl] = (duq * yt[:, sl] * (sg[:, sl] * (1.0 + z[:, sl] * (1.0 - sg[:, sl])))).astype(BF16)
            acc_ref[:, sl] += jnp.sum(dyt * xs[:, sl], axis=0, keepdims=True)
        dg = jnp.concatenate(dgs, axis=1)

        @pl.when(i == 0)
        def _():
            dg_ref[...] = dg

        @pl.when(i > 0)
        def _():
            dg_ref[...] += dg

        @pl.when(i == steps - 1)
        def _():
            ind = ((_iota((W, 128), 0) >> 6) == _iota((W, 128), 1)).astype(BF16)
            dd_ref[...] = _dot3(jnp.broadcast_to(acc_ref[...], (8, W)), ind)[0:1, :]

    row = pl.BlockSpec((tr, W), lambda i: (i, 0))
    vec = pl.BlockSpec((1, W), lambda i: (0, 0))
    return pl.pallas_call(
        body, name=name, grid=(steps,), in_specs=[row, row, row, vec, vec, row],
        out_specs=[row, row, vec, pl.BlockSpec((1, 128), lambda i: (0, 0))],
        out_shape=[jax.ShapeDtypeStruct((S, W), F32), jax.ShapeDtypeStruct((S, W), BF16),
                   jax.ShapeDtypeStruct((1, W), F32), jax.ShapeDtypeStruct((1, 128), F32)],
        scratch_shapes=[pltpu.VMEM((1, W), F32)], compiler_params=_cp(1),
    )(y, xbc, zx, dexp, gn, dout)


def _rope_cs(posf, *, name="rope_tables", tr=256):
    S = posf.shape[0]

    def body(p_ref, c_ref, s_ref):
        j = (_iota((tr, 128), 1) & 31).astype(F32)
        ang = p_ref[...] * jnp.exp(j * (-math.log(ROPE_THETA) / 32.0))
        c_ref[...] = jnp.cos(ang)
        s_ref[...] = jnp.sin(ang)

    blk = pl.BlockSpec((tr, 128), lambda i: (i, 0))
    return pl.pallas_call(
        body, name=name, grid=(S // tr,), in_specs=[pl.BlockSpec((tr, 1), lambda i: (i, 0))], out_specs=[blk, blk],
        out_shape=[jax.ShapeDtypeStruct((S, 128), F32)] * 2, compiler_params=_cp(1),
    )(posf)


def _rope_tables(c_ref, s_ref, shape):
    reps = shape[1] // 128
    return jnp.tile(c_ref[...], (1, reps)), jnp.tile(s_ref[...], (1, reps)), (_iota(shape, 1) & 63) < 32


def _hn_inds(W):
    ind = ((_iota((W, 128), 0) >> 6) == _iota((W, 128), 1)).astype(BF16)
    ind_t = ((_iota((128, W), 1) >> 6) == _iota((128, W), 0)).astype(BF16)
    return ind, ind_t


def _hnrope_fwd(xin, col0, W, gain_w, rope, *, name, tr=256):
    S = xin.shape[0]
    off = col0 // W
    nh = W // HEAD

    def body(x_ref, g_ref, c_ref, s_ref, o_ref):
        x = x_ref[...]
        ind, ind_t = _hn_inds(W)
        r = lax.rsqrt(_dot3(x * x, ind) * (1.0 / HEAD) + EPS)
        xn = x * _dot3(r, ind_t) * g_ref[...]
        cs, sn, half = _rope_tables(c_ref, s_ref, (tr, W))
        rot = jnp.where(half, -pltpu.roll(xn, W - 32, axis=1), pltpu.roll(xn, 32, axis=1))
        out = (xn * cs + rot * sn).astype(BF16)
        for h in range(nh):
            o_ref[h] = out[:, HEAD * h:HEAD * (h + 1)]

    tab = pl.BlockSpec((tr, 128), lambda i: (i, 0))
    return pl.pallas_call(
        body, name=name, grid=(S // tr,),
        in_specs=[pl.BlockSpec((tr, W), lambda i: (i, off)), pl.BlockSpec((1, W), lambda i: (0, 0)), tab, tab],
        out_specs=pl.BlockSpec((nh, tr, HEAD), lambda i: (0, i, 0)), out_shape=jax.ShapeDtypeStruct((nh, S, HEAD), BF16),
        compiler_params=_cp(1),
    )(xin, gain_w, *rope)


def _hnrope_bwd(xin, col0, W, gain_w, rope, dout, *, name, tr=256):
    S = xin.shape[0]
    off = col0 // W
    steps = S // tr
    nh = W // HEAD

    def body(x_ref, g_ref, c_ref, s_ref, do_ref, dx_ref, cs_ref, dg_ref, acc_ref):
        i = pl.program_id(0)
        x = x_ref[...]
        ind, ind_t = _hn_inds(W)
        r = lax.rsqrt(_dot3(x * x, ind) * (1.0 / HEAD) + EPS)
        rw = _dot3(r, ind_t)
        xh = x * rw
        cs, sn, half = _rope_tables(c_ref, s_ref, (tr, W))
        do = jnp.concatenate([do_ref[h] for h in range(nh)], axis=1).astype(F32)
        gs = do * sn
        g1 = do * cs + jnp.where(half, pltpu.roll(gs, W - 32, axis=1), -pltpu.roll(gs, 32, axis=1))
        dxh = g1 * g_ref[...]
        t = _dot3(dxh * xh, ind) * (1.0 / HEAD)
        dx = rw * (dxh - xh * _dot3(t, ind_t))
        dx_ref[...] = dx.astype(BF16)
        cpart = jnp.sum(dx, axis=0, keepdims=True)
        gpart = jnp.sum(g1 * xh, axis=0, keepdims=True)

        @pl.when(i == 0)
        def _():
            cs_ref[...] = cpart
            acc_ref[...] = gpart

        @pl.when(i > 0)
        def _():
            cs_ref[...] += cpart
            acc_ref[...] += gpart

        @pl.when(i == steps - 1)
        def _():
            fold = ((_iota((W, 128), 0) & 63) == _iota((W, 128), 1)).astype(BF16)
            dg_ref[...] = _dot3(jnp.broadcast_to(acc_ref[...], (8, W)), fold)[0:1, :]

    tab = pl.BlockSpec((tr, 128), lambda i: (i, 0))
    return pl.pallas_call(
        body, name=name, grid=(steps,),
        in_specs=[pl.BlockSpec((tr, W), lambda i: (i, off)), pl.BlockSpec((1, W), lambda i: (0, 0)), tab, tab,
                  pl.BlockSpec((nh, tr, HEAD), lambda i: (0, i, 0))],
        out_specs=[pl.BlockSpec((tr, W), lambda i: (i, 0)), pl.BlockSpec((1, W), lambda i: (0, 0)),
                   pl.BlockSpec((1, 128), lambda i: (0, 0))],
        out_shape=[jax.ShapeDtypeStruct((S, W), BF16), jax.ShapeDtypeStruct((1, W), F32),
                   jax.ShapeDtypeStruct((1, 128), F32)],
        scratch_shapes=[pltpu.VMEM((1, W), F32)], compiler_params=_cp(1),
    )(xin, gain_w, *rope, dout)


def _attn_band():
    qi = jnp.arange(ATT_G * WINDOW)[:, None] % WINDOW
    ki = jnp.arange(2 * WINDOW)[None, :]
    rel = qi + WINDOW - ki
    ok = (rel >= 0) & (rel < WINDOW)
    return jnp.stack([jnp.where(ok & (ki >= WINDOW), 0.0, NEG), jnp.where(ok, 0.0, NEG)]).astype(F32)


def _attn_probs(q, kb, sink_ref, band_ref, h, i):
    s = _dot(q, kb, 1, 1) * (HEAD ** -0.5) + band_ref[jnp.minimum(i, 1)]
    r1 = _iota((4 * WINDOW, 1), 0)
    sink = jnp.where(r1 < WINDOW, sink_ref[4 * h], jnp.where(r1 < 2 * WINDOW, sink_ref[4 * h + 1],
                     jnp.where(r1 < 3 * WINDOW, sink_ref[4 * h + 2], sink_ref[4 * h + 3])))
    m = jnp.maximum(jnp.max(s, axis=1, keepdims=True), sink)
    p = jnp.exp(s - m)
    ps = jnp.exp(sink - m)
    inv = 1.0 / (jnp.sum(p, axis=1, keepdims=True) + ps)
    return p * inv, ps * inv


ATT_HPS = 4
_BAND = pl.BlockSpec((2, ATT_G * WINDOW, 2 * WINDOW), lambda h, i: (0, 0, 0))


def _attn_specs(S):
    qspec = pl.BlockSpec((ATT_HPS, ATT_G, WINDOW, HEAD), lambda h, i: (h, 0, i, 0))
    cur = pl.BlockSpec((ATT_HPS, WINDOW, HEAD), lambda h, i: (h, i, 0))
    prev = pl.BlockSpec((ATT_HPS, WINDOW, HEAD), lambda h, i: (h, jnp.maximum(i - 1, 0), 0))
    tok = pl.BlockSpec((WINDOW, ATT_HPS * ATT_G * HEAD), lambda h, i: (i, h))
    return qspec, cur, prev, tok


def _attn_fwd(qh, kh, vh, sinks, *, name="attn_fwd"):
    S = kh.shape[1]
    nb = S // WINDOW

    def body(s_ref, band_ref, q_ref, kc_ref, kp_ref, vc_ref, vp_ref, o_ref):
        h2, i = pl.program_id(0), pl.program_id(1)
        outs = []
        for hh in range(ATT_HPS):
            q = q_ref[hh].reshape(ATT_G * WINDOW, HEAD)
            kb = jnp.concatenate([kp_ref[hh], kc_ref[hh]], axis=0)
            vb = jnp.concatenate([vp_ref[hh], vc_ref[hh]], axis=0)
            probs, _ = _attn_probs(q, kb, s_ref, band_ref, ATT_HPS * h2 + hh, i)
            o = _dot(probs.astype(BF16), vb).astype(BF16)
            outs += [o[WINDOW * g:WINDOW * (g + 1)] for g in range(ATT_G)]
        o_ref[...] = jnp.concatenate(outs, axis=1)

    qspec, cur, prev, tok = _attn_specs(S)
    return pl.pallas_call(
        body, name=name, grid=(ATT_KV // ATT_HPS, nb),
        in_specs=[pl.BlockSpec(memory_space=pltpu.SMEM), _BAND, qspec, cur, prev, cur, prev], out_specs=tok,
        out_shape=jax.ShapeDtypeStruct((S, ATT_KV * ATT_G * HEAD), BF16), compiler_params=_cp(2),
    )(sinks, _attn_band(), qh, kh, kh, vh, vh)


def _attn_bwd(qh, kh, vh, sinks, doh, *, name="attn_bwd"):
    S = kh.shape[1]
    nb = S // WINDOW

    def body(s_ref, band_ref, q_ref, kc_ref, kp_ref, vc_ref, vp_ref, do_ref, dq_ref, dk_ref, dv_ref, dsk_ref):
        h2, i = pl.program_id(0), pl.program_id(1)

        @pl.when(i == 0)
        def _():
            dk_ref[...] = jnp.zeros_like(dk_ref)
            dv_ref[...] = jnp.zeros_like(dv_ref)
            dsk_ref[...] = jnp.zeros_like(dsk_ref)

        dov = do_ref[...]
        cur = pl.multiple_of(i * WINDOW, WINDOW)
        lane = _iota((8, 128), 1)
        row = _iota((8, 128), 0)
        scale = HEAD ** -0.5
        for hh in range(ATT_HPS):
            q = q_ref[hh].reshape(ATT_G * WINDOW, HEAD)
            do = jnp.concatenate([dov[:, HEAD * (ATT_G * hh + g):HEAD * (ATT_G * hh + g + 1)] for g in range(ATT_G)], axis=0)
            kb = jnp.concatenate([kp_ref[hh], kc_ref[hh]], axis=0)
            vb = jnp.concatenate([vp_ref[hh], vc_ref[hh]], axis=0)
            probs, psink = _attn_probs(q, kb, s_ref, band_ref, ATT_HPS * h2 + hh, i)
            dp = _dot(do, vb, 1, 1)
            delta = jnp.sum(probs * dp, axis=1, keepdims=True)
            ds = (probs * (dp - delta)).astype(BF16)
            dq_ref[hh] = (_dot(ds, kb) * scale).reshape(ATT_G, WINDOW, HEAD)
            dkb = _dot(ds, q, 0, 0) * scale
            dvb = _dot(probs.astype(BF16), do, 0, 0)
            dk_ref[hh, pl.ds(cur, WINDOW), :] += dkb[WINDOW:, :]
            dv_ref[hh, pl.ds(cur, WINDOW), :] += dvb[WINDOW:, :]
            prv = pl.multiple_of(jnp.maximum(i - 1, 0) * WINDOW, WINDOW)
            dk_ref[hh, pl.ds(prv, WINDOW), :] += dkb[:WINDOW, :]
            dv_ref[hh, pl.ds(prv, WINDOW), :] += dvb[:WINDOW, :]

            dsr = -psink * delta
            upd = jnp.zeros((8, 128), F32)
            for gq in range(ATT_G):
                v = jnp.sum(dsr[gq * WINDOW:(gq + 1) * WINDOW, :], axis=0, keepdims=True)
                upd = jnp.where((lane == gq) & (row == 0), v, upd)
            dsk_ref[hh] += upd

    qspec, cur, prev, tok = _attn_specs(S)
    full = pl.BlockSpec((ATT_HPS, S, HEAD), lambda h, i: (h, 0, 0))
    return pl.pallas_call(
        body, name=name, grid=(ATT_KV // ATT_HPS, nb),
        in_specs=[pl.BlockSpec(memory_space=pltpu.SMEM), _BAND, qspec, cur, prev, cur, prev, tok],
        out_specs=[qspec, full, full, pl.BlockSpec((ATT_HPS, 8, 128), lambda h, i: (h, 0, 0))],
        out_shape=[jax.ShapeDtypeStruct((ATT_KV, ATT_G, S, HEAD), F32), jax.ShapeDtypeStruct((ATT_KV, S, HEAD), F32),
                   jax.ShapeDtypeStruct((ATT_KV, S, HEAD), F32), jax.ShapeDtypeStruct((ATT_KV, 8, 128), F32)],
        compiler_params=_cp(2),
    )(sinks, _attn_band(), qh, kh, kh, vh, vh, doh)


def _heads_major(t, nh):
    S = t.shape[0]
    return t.reshape(S, nh, HEAD).transpose(1, 0, 2)


def _tokens_major(t):
    nh, S, _ = t.shape
    return t.transpose(1, 0, 2).reshape(S, nh * HEAD)


class _NoComm:
    def late_start(self, part, after):
        return None

    def late_arrived(self, part, after):
        return None

    def late_weights(self, w, after, part):
        return w

    def advance(self, after, group=None, tensors=None):
        return None


def _local_step(x, posf, target, w, comm=None):
    S, D = x.shape
    gr = {}
    comm = comm or _NoComm()

    zx, h1 = _norm_mm(x, w["a_norm"], w["w_zx"], name="in_proj_zx", dep=w.get("dep"))
    dtr = _mm(h1, w["w_dt"], name="in_proj_dt")
    xbc = _conv_silu_fwd(zx, 2048, 4096, w["a_conv_w"], w["a_conv_b"], name="a_conv_f")
    dt_g, ac_g, sg_g, ac_t = _ssd_prep(dtr, w["a_dt_bias"], w["a_A_log"])
    y_ssd, states = _ssd_fwd(xbc, dt_g, ac_g, ac_t, dep=comm.late_start(1, xbc))
    yg = _gate_fwd(y_ssd, xbc, zx, w["a_Dexp"], w["a_gnorm"], dep=comm.late_arrived(0, [y_ssd]))
    w = comm.late_weights(w, yg, 0)
    x1 = _mm(yg, w["a_out_proj"], res=x, name="out_proj")

    FW = w["f_w_in"][0].shape[2]

    def ffn_fwd(xin, l, loss_target=None):
        u, h = _norm_mm(xin, w["f_norm"][l], w["f_w_in"][l], name=f"f_in{l}", N=N_CHIPS * FW, tn=FW,
                        b_spec=pl.BlockSpec((None, D, FW), lambda i, j: (j, 0, 0)))
        a = _ffn_act_fwd(u, w["f_conv_w"][l], w["f_conv_b"][l], name=f"f_act_f{l}")
        dep = comm.late_arrived(1, [u]) if l == 0 else None
        xo = _mm(a, w["f_w_down"][l], res=xin, tk=a.shape[1], name=f"f_down{l}", target=loss_target, dep=dep)
        return xo, (h, u)

    x2, ffn0 = ffn_fwd(x1, 0)
    w = comm.late_weights(w, x2, 1)

    kv, hk = _norm_mm(x2, w["kv_norm"], w["w_kv"], bias=w["b_kv"], name="kv_proj")
    q, hq = _norm_mm(x2, w["b_norm"], w["w_q"], bias=w["b_q"], name="q_proj")
    rope = _rope_cs(posf)
    kr = _hnrope_fwd(kv, 0, 256, w["k_norm_w"], rope, name="k_rope_f")
    qr = _hnrope_fwd(q, 0, 1024, w["q_norm_w"], rope, name="q_rope_f")
    qh = qr.reshape(ATT_KV, ATT_G, S, HEAD)
    kh = kr
    vh = _heads_major(kv[:, 256:].astype(BF16), ATT_KV)
    att = _attn_fwd(qh, kh, vh, w["sinks"])
    x3 = _mm(att, w["w_o"], bias=w["b_o"], res=x2, name="o_proj")
    (dy, loss_part), ffn1 = ffn_fwd(x3, 1, target)

    def ffn_bwd(xin, l, saved, dyo, want_colsum, dep=None):
        h, u = saved
        da = _mm(dyo, w["f_w_down"][l], tb=True, name=f"f_down_dx{l}", dep=dep)
        du, dcw, dcb, a = _ffn_act_bwd(u, w["f_conv_w"][l], w["f_conv_b"][l], da, name=f"f_act_b{l}")
        dw_down = _mm(a, dyo, ta=True, out_dtype=BF16, name=f"f_down_dw{l}")
        dw_in = _mm(h, du, ta=True, out_dtype=BF16, name=f"f_in_dw{l}", dims=(D, N_CHIPS * FW, S), tm=D, tn=FW, tk=S,
                    b_spec=pl.BlockSpec((None, S, FW), lambda i, j, k: (j // 2, 0, j % 2)),
                    o_spec=pl.BlockSpec((None, D, FW), lambda i, j, k: (j, i, 0)), o_shape=(N_CHIPS, D, FW))
        ts = _pick(S, (512, 256))
        pieces = [(pl.BlockSpec((None, ts, FW), lambda i, j, k, q=q: (q // 2, i, q % 2)),
                   pl.BlockSpec((None, D, FW), lambda i, j, k, q=q: (q, 0, 0), pipeline_mode=pl.Buffered(1)))
                  for q in range(N_CHIPS)]
        outs = _mm(du, w["f_w_in"][l], tb=True, name=f"f_in_dx{l}", dims=(S, D, FW), tm=ts, tn=D, tk=FW,
                   a_spec=pieces[0][0], b_spec=pieces[0][1], more=pieces[1:],
                   rms=(xin, w["f_norm"][l], dyo), rms_colsum=want_colsum)
        g = dict(f_norm=outs[1], f_w_in=dw_in, f_conv_w=dcw, f_conv_b=dcb, f_w_down=dw_down)
        return outs[0], g, (outs[2] if want_colsum else None)

    dx3, gr["ffn1"], db_o = ffn_bwd(x3, 1, ffn1, dy, True)
    gr["b_o"] = db_o
    gr["w_o"] = _mm(att, dx3, ta=True, out_dtype=BF16, name="o_proj_dw")
    datt = _mm(dx3, w["w_o"], tb=True, out_dtype=BF16, name="o_proj_dx")
    dqh, dkh, dvh, dsk = _attn_bwd(qh, kh, vh, w["sinks"], datt)
    gr["sinks"] = dsk[:, 0, :4].reshape(1, 16)
    dv = _tokens_major(dvh).astype(BF16)
    dq, db_q, dqn = _hnrope_bwd(q, 0, 1024, w["q_norm_w"], rope, dqh.reshape(16, S, HEAD), name="q_rope_b")
    dk, db_k, dkn = _hnrope_bwd(kv, 0, 256, w["k_norm_w"], rope, dkh, name="k_rope_b")
    gr["q_norm"], gr["k_norm"] = dqn[:, :HEAD], dkn[:, :HEAD]
    gr["b_q"] = db_q
    gr["b_kv"] = jnp.concatenate([db_k, _colsum(dv, name="dv_colsum")], axis=1)
    dkv = jnp.concatenate([dk, dv], axis=1)
    gr["w_q"] = _mm(hq, dq, ta=True, out_dtype=BF16, name="q_proj_dw")
    gr["w_kv"] = _mm(hk, dkv, ta=True, out_dtype=BF16, name="kv_proj_dw")
    tok = comm.advance([gr["w_kv"]], 1, dict(f_down1=gr["ffn1"]["f_w_down"], f_in1=gr["ffn1"]["f_w_in"], w_o=gr["w_o"],
                                             w_q=gr["w_q"], w_kv=gr["w_kv"]))
    tsm = _pick(S, (512, 256))
    dx2, gr["b_norm"] = _mm(dq, w["w_q"], tb=True, name="q_proj_dx", dep=tok, tm=tsm, tn=D, rms=(x2, w["b_norm"], dx3))
    dx2, gr["kv_norm"] = _mm(dkv, w["w_kv"], tb=True, name="kv_proj_dx", tm=tsm, tn=D, rms=(x2, w["kv_norm"], dx2))

    dx1, gr["ffn0"], _ = ffn_bwd(x1, 0, ffn0, dx2, False, dep=comm.advance([dx2]))

    gr["a_out_proj"] = _mm(yg, dx1, ta=True, out_dtype=BF16, name="out_proj_dw")
    tok = comm.advance([dx1, gr["a_out_proj"]], 2,
                       dict(f_down0=gr["ffn0"]["f_w_down"], f_in0=gr["ffn0"]["f_w_in"], out_proj=gr["a_out_proj"]))
    dyg = _mm(dx1, w["a_out_proj"], tb=True, name="out_proj_dx", dep=tok)
    dy_ssd, dz, gr["a_gnorm"], dD = _gate_bwd(y_ssd, xbc, zx, w["a_Dexp"], w["a_gnorm"], dyg)
    gr["a_D"] = dD[:, :SSM_HEADS]
    dxs, dB, dC, dhead = _ssd_bwd(xbc, dt_g, ac_g, ac_t, states, dy_ssd, w["a_Dexp"], dep=comm.advance([dy_ssd]))
    ddtr, dsmall = _ssd_post(dhead, dt_g, sg_g, w["a_A_log_g"])
    gr["a_A_log"] = dsmall[:, 0, :4].reshape(1, SSM_HEADS)
    gr["a_dt_bias"] = dsmall[:, 1, :4].reshape(1, SSM_HEADS)
    dxbc, gr["a_conv_w"], gr["a_conv_b"] = _conv_silu_bwd(
        zx, 2048, 4096, w["a_conv_w"], w["a_conv_b"], [(dxs, 0), (dB, 2048), (dC, 3072)], name="a_conv_b")
    gr["in_proj"] = _in_proj_dw(h1, dz, dxbc, ddtr)
    ts = _pick(S, (512, 256))
    once = pl.Buffered(1)
    wblk = lambda q: pl.BlockSpec((D, 2048), lambda i, j, k: (0, q), pipeline_mode=once)
    dx0, gr["a_norm"] = _mm(
        dz, w["w_zx"], tb=True, name="in_proj_dx", dims=(S, D, 2048), tm=ts, tn=D, tk=2048,
        a_spec=pl.BlockSpec((ts, 2048), lambda i, j, k: (i, 0)), b_spec=wblk(0),
        more=[(dxbc, pl.BlockSpec((ts, 2048), lambda i, j, k: (i, 0)), w["w_zx"], wblk(1)),
              (dxbc, pl.BlockSpec((ts, 2048), lambda i, j, k: (i, 1)), w["w_zx"], wblk(2)),
              (ddtr, pl.BlockSpec((ts, 128), lambda i, j, k: (i, 0)), w["w_dt"],
               pl.BlockSpec((D, 128), lambda i, j, k: (0, 0), pipeline_mode=once))],
        rms=(x, w["a_norm"], dx1))
    tok = comm.advance([dx0], 3, dict(in_proj=gr["in_proj"].reshape(D, N_CHIPS, -1).transpose(1, 0, 2)))
    return loss_part, dx0, gr, tok


def _prep_small(full, w):
    w["a_norm"] = full["a_norm"]
    w["a_conv_w"] = full["a_conv_w"][0]
    w["a_conv_b"] = full["a_conv_b"]
    pad32 = lambda v: jnp.pad(v, ((0, 0), (0, 128 - SSM_HEADS)))
    w["a_dt_bias"] = pad32(full["a_dt_bias"])
    w["a_A_log"] = pad32(full["a_A_log"])
    w["a_A_log_g"] = jnp.pad(full["a_A_log"].reshape(SSM_GROUPS, 1, 4), ((0, 0), (0, 0), (0, 124)))
    w["a_Dexp"] = jnp.repeat(full["a_D"], HEAD, axis=1)
    w["a_gnorm"] = full["a_gnorm"]
    w["f_norm"] = [full["f_norm"][l:l + 1] for l in range(2)]
    w["f_conv_w"] = [full["f_conv_w"][l] for l in range(2)]
    w["f_conv_b"] = [full["f_conv_b"][l:l + 1] for l in range(2)]
    w["kv_norm"] = full["kv_norm"].reshape(1, -1)
    w["b_kv"] = full["b_kv"].reshape(1, -1)
    w["k_norm_w"] = jnp.tile(full["k_norm"].reshape(1, HEAD), (1, ATT_KV))
    w["b_norm"] = full["b_norm"]
    w["b_q"] = full["b_q"]
    w["q_norm_w"] = jnp.tile(full["q_norm"], (1, ATT_KV * ATT_G))
    w["sinks"] = full["sinks"].reshape(-1)
    w["b_o"] = full["b_o"]
    return w


def _split_in_proj(ip):
    return ip[:, :6144].astype(BF16), jnp.pad(ip[:, 6144:], ((0, 0), (0, 128 - SSM_HEADS))).astype(BF16)


def _join_in_proj(blocks, *, name="in_proj_join", tr=256):
    _, R, cw = blocks.shape
    zx_cols = 3 * 2048
    rest = N_CHIPS * cw - zx_cols

    def body(b_ref, zx_ref, dt_ref):
        whole = jnp.concatenate([b_ref[j] for j in range(N_CHIPS)], axis=1)
        zx_ref[...] = whole[:, :zx_cols]
        dt_ref[...] = jnp.concatenate([whole[:, zx_cols:], jnp.zeros((tr, 128 - rest), BF16)], axis=1)

    return pl.pallas_call(
        body, name=name, grid=(R // tr,), in_specs=[pl.BlockSpec((N_CHIPS, tr, cw), lambda i: (0, i, 0))],
        out_specs=[pl.BlockSpec((tr, zx_cols), lambda i: (i, 0)), pl.BlockSpec((tr, 128), lambda i: (i, 0))],
        out_shape=[jax.ShapeDtypeStruct((R, zx_cols), BF16), jax.ShapeDtypeStruct((R, 128), BF16)],
        compiler_params=_cp(1),
    )(blocks)


def _in_proj_dw(h, dz, dxbc, ddtr, *, name="in_proj_dw", tn=512):
    S, D = h.shape
    nz, nx = dz.shape[1] // tn, dxbc.shape[1] // tn
    N = dz.shape[1] + dxbc.shape[1] + SSM_HEADS

    def body(h_ref, z_ref, x_ref, t_ref, o_ref):
        j = pl.program_id(0)
        hb = h_ref[...].astype(BF16)

        @pl.when(j < nz)
        def _():
            o_ref[...] = _dot(hb, z_ref[...].astype(BF16), 0, 0).astype(BF16)

        @pl.when((j >= nz) & (j < nz + nx))
        def _():
            o_ref[...] = _dot(hb, x_ref[...].astype(BF16), 0, 0).astype(BF16)

        @pl.when(j == nz + nx)
        def _():
            o_ref[:, :128] = _dot(hb, t_ref[...].astype(BF16), 0, 0).astype(BF16)
            o_ref[:, 128:] = jnp.zeros((D, tn - 128), BF16)

    return pl.pallas_call(
        body, name=name, grid=(nz + nx + 1,),
        in_specs=[pl.BlockSpec((S, D), lambda j: (0, 0), pipeline_mode=pl.Buffered(1)),
                  pl.BlockSpec((S, tn), lambda j: (0, jnp.minimum(j, nz - 1))),
                  pl.BlockSpec((S, tn), lambda j: (0, jnp.clip(j - nz, 0, nx - 1))),
                  pl.BlockSpec((S, 128), lambda j: (0, 0))],
        out_specs=pl.BlockSpec((D, tn), lambda j: (0, j)),
        out_shape=jax.ShapeDtypeStruct((D, N), BF16), compiler_params=_cp(1),
    )(h, dz, dxbc, ddtr)


def _prep_weights(full):
    w = _prep_small(full, {})
    w["w_zx"], w["w_dt"] = _split_in_proj(full["a_in_proj"][0])
    w["a_out_proj"] = full["a_out_proj"][0].astype(BF16)
    w["f_w_in"] = [full["f_w_in"][l].reshape(1024, N_CHIPS, -1).transpose(1, 0, 2).astype(BF16) for l in range(2)]
    w["f_w_down"] = [full["f_w_down"][l].astype(BF16) for l in range(2)]
    w["w_kv"] = full["w_kv"].astype(BF16)
    w["w_q"] = full["w_q"][0].astype(BF16)
    w["w_o"] = full["w_o"][0].astype(BF16)
    return w


def _small_grads(gr):
    g = {}
    g["a_norm"] = gr["a_norm"]
    g["a_conv_w"] = gr["a_conv_w"][None]
    g["a_conv_b"] = gr["a_conv_b"]
    g["a_dt_bias"], g["a_A_log"], g["a_D"] = gr["a_dt_bias"], gr["a_A_log"], gr["a_D"]
    g["a_gnorm"] = gr["a_gnorm"]
    g["kv_norm"] = gr["kv_norm"].reshape(-1)
    g["b_kv"] = gr["b_kv"].reshape(-1)
    g["k_norm"] = gr["k_norm"].reshape(-1)
    g["b_norm"] = gr["b_norm"]
    g["b_q"] = gr["b_q"]
    g["q_norm"] = gr["q_norm"]
    g["sinks"] = gr["sinks"]
    g["b_o"] = gr["b_o"]
    f = [gr["ffn0"], gr["ffn1"]]
    g["f_norm"] = jnp.concatenate([f[0]["f_norm"], f[1]["f_norm"]], axis=0)
    g["f_conv_w"] = jnp.stack([f[l]["f_conv_w"] for l in range(2)])
    g["f_conv_b"] = jnp.concatenate([f[l]["f_conv_b"] for l in range(2)], axis=0)
    return g


def _full_grads(gr):
    g = _small_grads(gr)
    f32 = lambda t: t.astype(F32)
    g["a_in_proj"] = f32(gr["in_proj"])[None]
    g["a_out_proj"] = f32(gr["a_out_proj"])[None]
    g["w_kv"] = f32(gr["w_kv"])
    g["w_q"] = f32(gr["w_q"])[None]
    g["w_o"] = f32(gr["w_o"])[None]
    f = [gr["ffn0"], gr["ffn1"]]
    g["f_w_in"] = jnp.stack([f32(f[l]["f_w_in"]).transpose(1, 0, 2).reshape(1024, -1) for l in range(2)])
    g["f_w_down"] = jnp.stack([f32(f[l]["f_w_down"]) for l in range(2)])
    return g


MESH = pl.DeviceIdType.MESH
WEIGHTS = ("a_norm", "a_in_proj", "a_conv_w", "a_conv_b", "a_dt_bias", "a_A_log", "a_D", "a_gnorm", "a_out_proj",
           "kv_norm", "w_kv", "b_kv", "k_norm", "b_norm", "w_q", "b_q", "q_norm", "sinks", "w_o", "b_o", "f_norm",
           "f_w_in", "f_conv_w", "f_conv_b", "f_w_down")
MATS = (("in_proj", "a_in_proj", 0), ("out_proj", "a_out_proj", 0), ("w_kv", "w_kv", None), ("w_q", "w_q", 0),
        ("w_o", "w_o", 0), ("f_in0", "f_w_in", 0), ("f_in1", "f_w_in", 1), ("f_down0", "f_w_down", 0),
        ("f_down1", "f_w_down", 1))
SMALL_CUT = (("a_norm", 1), ("a_conv_w", 2), ("a_conv_b", 1), ("a_gnorm", 1), ("f_conv_w", 2))
SMALL_REP = ("a_dt_bias", "a_A_log", "a_D", "kv_norm", "b_kv", "k_norm", "b_norm", "b_q", "q_norm", "sinks", "b_o",
             "f_norm", "f_conv_b")


def _coords():
    return lax.axis_index("x"), lax.axis_index("y"), lax.axis_index("c")


def _other_chips(x, y):
    return [(1 - x, y), (x, 1 - y), (1 - x, 1 - y)]


def _pack(arrs, rows_align, lanes, dtype):
    flat = jnp.concatenate([a.reshape(-1).astype(dtype) for a in arrs])
    per = rows_align * lanes
    total = -(-flat.shape[0] // per) * per
    return jnp.pad(flat, (0, total - flat.shape[0])).reshape(total // lanes, lanes)


def _unpack(flat, shapes):
    out, off = [], 0
    for s in shapes:
        n = math.prod(s)
        out.append(flat[off:off + n].reshape(s))
        off += n
    return out


def _remote(src, dst, send, recv, k, dev):
    return pltpu.make_async_remote_copy(src_ref=src, dst_ref=dst, send_sem=send.at[k], recv_sem=recv.at[k],
                                        device_id=dev, device_id_type=MESH)


_ANY = pl.BlockSpec(memory_space=pl.ANY)


def _halves(t):
    r, c = t.shape
    return t.reshape(2, r // 2, c)


def _gather_weights(shards, sp):
    n = len(shards)
    per = 9
    n_sem = per * n + 3

    def body(*refs):
        sh, sp_ref = refs[:n], refs[n]
        outs, sout = refs[n + 1:2 * n + 1], refs[2 * n + 1]
        send, recv, loc = refs[2 * n + 2:]
        x, y, c = _coords()
        me = 2 * x + y
        cx_, cy_, cd_ = _other_chips(x, y)
        ix, iy, idg = (2 * p[0] + p[1] for p in (cx_, cy_, cd_))
        to_x, to_y, sib = (*cx_, c), (*cy_, c), (x, y, 1 - c)
        l1 = pltpu.make_async_copy(sp_ref, sout.at[me], loc.at[0])
        l1.start()
        sends = [_remote(sp_ref, sout.at[me], send, recv, per * n + j, (*p, c)) for j, p in enumerate((cx_, cy_, cd_))]
        for t in range(n):
            sends.append(_remote(sh[t].at[c], outs[t].at[me, c], send, recv, per * t + 0, to_x))
            sends.append(_remote(sh[t].at[c], outs[t].at[me, c], send, recv, per * t + 1, to_y))
            sends.append(_remote(sh[t], outs[t].at[me], send, recv, per * t + 8, sib))
        for cp in sends:
            cp.start()

        def go(src, dst, k, dev):
            cp = _remote(src, dst, send, recv, k, dev)
            cp.start()
            sends.append(cp)

        def piece(t, owner, first):
            q = sh[t].shape[1] // 2
            return outs[t].at[owner, c, pl.ds(0 if first else q, q)]

        for t in range(n):
            _remote(sh[t].at[c], outs[t].at[ix, c], send, recv, per * t + 0, to_x).wait_recv()
            go(piece(t, ix, False), piece(t, ix, False), per * t + 3, to_y)
            go(outs[t].at[ix, c], outs[t].at[ix, c], per * t + 4, sib)
        for t in range(n):
            _remote(sh[t].at[c], outs[t].at[iy, c], send, recv, per * t + 1, to_y).wait_recv()
            go(piece(t, iy, True), piece(t, iy, True), per * t + 2, to_x)
            go(outs[t].at[iy, c], outs[t].at[iy, c], per * t + 5, sib)
        for t in range(n):
            _remote(piece(t, idg, True), piece(t, idg, True), send, recv, per * t + 2, to_x).wait_recv()
            go(piece(t, idg, True), piece(t, idg, True), per * t + 6, sib)
            _remote(piece(t, idg, False), piece(t, idg, False), send, recv, per * t + 3, to_y).wait_recv()
            go(piece(t, idg, False), piece(t, idg, False), per * t + 7, sib)
        for j, p in enumerate((cx_, cy_, cd_)):
            _remote(sp_ref, sout.at[2 * p[0] + p[1]], send, recv, per * n + j, (*p, c)).wait_recv()
        for t in range(n):
            q = sh[t].shape[1] // 2
            other = lambda owner, lo=None: outs[t].at[owner, 1 - c] if lo is None else outs[t].at[owner, 1 - c, pl.ds(lo, q)]
            _remote(other(ix), other(ix), send, recv, per * t + 4, sib).wait_recv()
            _remote(other(iy), other(iy), send, recv, per * t + 5, sib).wait_recv()
            _remote(other(idg, 0), other(idg, 0), send, recv, per * t + 6, sib).wait_recv()
            _remote(other(idg, q), other(idg, q), send, recv, per * t + 7, sib).wait_recv()
            _remote(sh[t], outs[t].at[me], send, recv, per * t + 8, sib).wait_recv()
        for cp in sends:
            cp.wait_send()
        l1.wait()

    res = pl.pallas_call(
        body, name="gather_weights", in_specs=[_ANY] * (n + 1), out_specs=[_ANY] * (n + 1),
        out_shape=[jax.ShapeDtypeStruct((N_CHIPS,) + t.shape, t.dtype) for t in shards]
        + [jax.ShapeDtypeStruct((N_CHIPS,) + sp.shape, sp.dtype)],
        scratch_shapes=[pltpu.SemaphoreType.DMA((n_sem,)), pltpu.SemaphoreType.DMA((n_sem,)),
                        pltpu.SemaphoreType.DMA((1,))],
    )(*shards, sp)
    return res[:n], res[n]


_HBM = pl.BlockSpec(memory_space=pltpu.HBM)
_SEMS = pl.BlockSpec(memory_space=pltpu.SEMAPHORE)
_DATAFLOW = pltpu.SideEffectType.DATAFLOW_SIDE_EFFECTING


def _in_hbm(a):
    return pltpu.with_memory_space_constraint(a, pltpu.HBM)


def _start_copies(copies, arrays, n_sem, after, *, name):
    n, na = len(arrays), len(after)

    def body(*refs):
        for mine, _ in copies(refs[:n], refs[n + na], refs[n + na + 1]):
            mine.start()
        refs[-1][...] = jnp.zeros_like(refs[-1])

    res = pl.pallas_call(
        body, name=name, in_specs=[_HBM] * n + [_ANY] * na,
        out_specs=[_SEMS, _SEMS] + [_HBM] * n + [pl.BlockSpec(memory_space=pltpu.VMEM)],
        out_shape=[pltpu.SemaphoreType.DMA((n_sem,)), pltpu.SemaphoreType.DMA((n_sem,))]
        + [pltpu.HBM(a.shape, a.dtype) for a in arrays] + [jax.ShapeDtypeStruct((8, 128), F32)],
        input_output_aliases={t: 2 + t for t in range(n)},
        compiler_params=pltpu.CompilerParams(has_side_effects=_DATAFLOW),
    )(*[_in_hbm(a) for a in arrays], *after)
    return res[0], res[1], list(res[2:2 + n]), res[-1]


def _wait_copies(copies, send, recv, arrays, after, *, name):
    n = len(arrays)

    def body(*refs):
        for mine, theirs in copies(refs[:n], refs[n], refs[n + 1]):
            mine.wait_send()
            theirs.wait_recv()

    return list(pl.pallas_call(
        body, name=name, in_specs=[_HBM] * n + [_SEMS, _SEMS] + [_ANY] * len(after), out_specs=[_HBM] * n,
        out_shape=[pltpu.HBM(a.shape, a.dtype) for a in arrays], input_output_aliases={t: t for t in range(n)},
        compiler_params=pltpu.CompilerParams(has_side_effects=_DATAFLOW),
    )(*arrays, send, recv, *after))


def _sibling_copies(refs, send, recv):
    n = len(refs) // 2
    x, y, c = _coords()
    cps = [_remote(refs[t].at[:, 1 - c], refs[n + t], send, recv, t, (x, y, 1 - c)) for t in range(n)]
    return [(cp, cp) for cp in cps]


def _join_copies(refs, send, recv):
    x, y, c = _coords()
    sib = (x, y, 1 - c)
    return [(_remote(o.at[c], o.at[c], send, recv, t, sib), _remote(o.at[1 - c], o.at[1 - c], send, recv, t, sib))
            for t, o in enumerate(refs)]


def _gather_copies(sh, land, send, recv):
    x, y, c = _coords()
    me = 2 * x + y
    out = []
    for t in range(len(sh)):
        for j, (cx, cy) in enumerate(_other_chips(x, y)):
            dev = (cx, cy, c)
            out.append((_remote(sh[t].at[c], land[t].at[me, c], send, recv, 4 * t + j, dev),
                        _remote(sh[t].at[c], land[t].at[2 * cx + cy, c], send, recv, 4 * t + j, dev)))
        sib = (x, y, 1 - c)
        out.append((_remote(sh[t], land[t].at[me], send, recv, 4 * t + 3, sib),
                    _remote(sh[t], land[t].at[me], send, recv, 4 * t + 3, sib)))
    return out


def _gather_start(shards, after, *, name):
    n = len(shards)

    def body(*refs):
        sh, land = refs[:n], refs[n:2 * n]
        send, recv = refs[2 * n + 1], refs[2 * n + 2]
        token = refs[-1]
        for mine, _ in _gather_copies(sh, land, send, recv):
            mine.start()
        token[...] = jnp.zeros_like(token)

    lands = [_in_hbm(lax.empty((N_CHIPS,) + s.shape, s.dtype)) for s in shards]
    res = pl.pallas_call(
        body, name=name, in_specs=[_HBM] * (2 * n) + [_ANY],
        out_specs=[_SEMS, _SEMS] + [_HBM] * (2 * n) + [pl.BlockSpec(memory_space=pltpu.VMEM)],
        out_shape=[pltpu.SemaphoreType.DMA((4 * n,)), pltpu.SemaphoreType.DMA((4 * n,))]
        + [pltpu.HBM(s.shape, s.dtype) for s in shards] + [pltpu.HBM(l.shape, l.dtype) for l in lands]
        + [jax.ShapeDtypeStruct((8, 128), F32)],
        input_output_aliases={t: 2 + t for t in range(2 * n)},
        compiler_params=pltpu.CompilerParams(has_side_effects=_DATAFLOW),
    )(*[_in_hbm(s) for s in shards], *lands, after)
    return res[0], res[1], res[2:2 + n], res[2 + n:2 + 2 * n], res[-1]


def _gather_wait(send, recv, shards, lands, after, *, name):
    n = len(shards)

    def body(*refs):
        sh, land = refs[:n], refs[n:2 * n]
        send_r, recv_r = refs[2 * n], refs[2 * n + 1]
        for mine, theirs in _gather_copies(sh, land, send_r, recv_r):
            mine.wait_send()
            theirs.wait_recv()

    res = pl.pallas_call(
        body, name=name, in_specs=[_HBM] * (2 * n) + [_SEMS, _SEMS, _ANY], out_specs=[_HBM] * (2 * n),
        out_shape=[pltpu.HBM(s.shape, s.dtype) for s in shards] + [pltpu.HBM(l.shape, l.dtype) for l in lands],
        input_output_aliases={t: t for t in range(2 * n)},
        compiler_params=pltpu.CompilerParams(has_side_effects=_DATAFLOW),
    )(*shards, *lands, send, recv, after)
    return res[n:]


def _forward_copies(refs, send, recv):
    x, y, c = _coords()
    sib = (x, y, 1 - c)
    srcs = [2 * cx + cy for cx, cy in _other_chips(x, y)]
    return [(_remote(o.at[s, c], o.at[s, c], send, recv, 3 * t + j, sib),
             _remote(o.at[s, 1 - c], o.at[s, 1 - c], send, recv, 3 * t + j, sib))
            for t, o in enumerate(refs) for j, s in enumerate(srcs)]


def _small_copies(v, land, send, recv):
    x, y, c = _coords()
    me = 4 * x + 2 * y + c
    out = []
    for k in range(1, 8):
        px = 1 - x if k & 4 else x
        py = 1 - y if k & 2 else y
        pc = 1 - c if k & 1 else c
        out.append((_remote(v, land.at[me], send, recv, k - 1, (px, py, pc)),
                    _remote(v, land.at[4 * px + 2 * py + pc], send, recv, k - 1, (px, py, pc))))
    return out


def _small_start(v, after, *, name):
    def body(v_ref, land_ref, after_ref, send, recv, v_thru, land_thru, token):
        for mine, _ in _small_copies(v_ref, land_ref, send, recv):
            mine.start()
        token[...] = jnp.zeros_like(token)

    land = _in_hbm(lax.empty((8,) + v.shape, v.dtype))
    return pl.pallas_call(
        body, name=name, in_specs=[_HBM, _HBM, _ANY],
        out_specs=[_SEMS, _SEMS, _HBM, _HBM, pl.BlockSpec(memory_space=pltpu.VMEM)],
        out_shape=[pltpu.SemaphoreType.DMA((7,)), pltpu.SemaphoreType.DMA((7,)), pltpu.HBM(v.shape, v.dtype),
                   pltpu.HBM(land.shape, land.dtype), jax.ShapeDtypeStruct((8, 128), F32)],
        input_output_aliases={0: 2, 1: 3}, compiler_params=pltpu.CompilerParams(has_side_effects=_DATAFLOW),
    )(_in_hbm(v), land, after)


def _small_wait(send, recv, v, land, after, *, name):
    def body(v_ref, land_ref, send_r, recv_r, *rest):
        for mine, theirs in _small_copies(v_ref, land_ref, send_r, recv_r):
            mine.wait_send()
            theirs.wait_recv()

    return pl.pallas_call(
        body, name=name, in_specs=[_HBM, _HBM, _SEMS, _SEMS] + [_ANY] * len(after), out_specs=[_HBM, _HBM],
        out_shape=[pltpu.HBM(v.shape, v.dtype), pltpu.HBM(land.shape, land.dtype)],
        input_output_aliases={0: 0, 1: 1}, compiler_params=pltpu.CompilerParams(has_side_effects=_DATAFLOW),
    )(v, land, send, recv, *after)


def _small_sum(v, land, me_idx, *, name="small_sum"):
    def body(me_ref, v_ref, land_ref, o_ref):
        acc = None
        for s in range(8):
            term = jnp.where(me_ref[0] == s, v_ref[...], land_ref[s])
            acc = term if acc is None else acc + term
        o_ref[...] = acc

    whole = lambda shape: pl.BlockSpec(shape, lambda i, me_ref: (0,) * len(shape))
    return pl.pallas_call(
        body, name=name,
        grid_spec=pltpu.PrefetchScalarGridSpec(num_scalar_prefetch=1, grid=(1,), in_specs=[whole(v.shape), whole(land.shape)],
                                               out_specs=whole(v.shape)),
        out_shape=jax.ShapeDtypeStruct(v.shape, F32), compiler_params=_cp(1),
    )(me_idx, v, land)


RS_ROW_SPLIT = 2


def _rs_add_pair(gs, as_, c_idx, *, name):
    n = len(gs)

    def body(c_ref, *refs):
        for t in range(n):
            refs[2 * n + t][...] = (refs[t][...].astype(F32) + refs[n + t][...].astype(F32)).astype(BF16)

    def gspec(g):
        _, _, rh, cols = g.shape
        return pl.BlockSpec((None, None, rh // RS_ROW_SPLIT, cols), lambda j, i, c_ref: (j, c_ref[0], i, 0))

    def pspec(g):
        _, _, rh, cols = g.shape
        return pl.BlockSpec((None, rh // RS_ROW_SPLIT, cols), lambda j, i, c_ref: (j, i, 0))

    return pl.pallas_call(
        body, name=name,
        grid_spec=pltpu.PrefetchScalarGridSpec(
            num_scalar_prefetch=1, grid=(N_CHIPS, RS_ROW_SPLIT),
            in_specs=[gspec(g) for g in gs] + [pspec(g) for g in gs], out_specs=[pspec(g) for g in gs]),
        out_shape=[jax.ShapeDtypeStruct((N_CHIPS,) + g.shape[2:], BF16) for g in gs], compiler_params=_cp(2),
    )(c_idx, *gs, *as_)


def _chips_copies(p, r, send, recv):
    x, y, c = _coords()
    return [_remote(p[t].at[2 * cx + cy], r[t].at[k], send, recv, 3 * t + k, (cx, cy, c))
            for k, (cx, cy) in enumerate(_other_chips(x, y)) for t in range(len(p))]


def _rs_chips_start(ps, after, *, name):
    n, na = len(ps), len(after)

    def body(*refs):
        p, r = refs[:n], refs[n:2 * n]
        send, recv = refs[2 * n + na], refs[2 * n + na + 1]
        token = refs[-1]
        for cp in _chips_copies(p, r, send, recv):
            cp.start()
        token[...] = jnp.zeros_like(token)

    lands = [_in_hbm(lax.empty((3,) + p.shape[1:], p.dtype)) for p in ps]
    res = pl.pallas_call(
        body, name=name, in_specs=[_HBM] * (2 * n) + [_ANY] * na,
        out_specs=[_SEMS, _SEMS] + [_HBM] * (2 * n) + [pl.BlockSpec(memory_space=pltpu.VMEM)],
        out_shape=[pltpu.SemaphoreType.DMA((3 * n,)), pltpu.SemaphoreType.DMA((3 * n,))]
        + [pltpu.HBM(p.shape, p.dtype) for p in ps] + [pltpu.HBM(l.shape, l.dtype) for l in lands]
        + [jax.ShapeDtypeStruct((8, 128), F32)],
        input_output_aliases={t: 2 + t for t in range(2 * n)},
        compiler_params=pltpu.CompilerParams(has_side_effects=_DATAFLOW),
    )(*[_in_hbm(p) for p in ps], *lands, *after)
    return res[0], res[1], res[2:2 + n], res[2 + n:2 + 2 * n], res[-1]


def _rs_chips_wait(send, recv, ps, lands, after, *, name):
    n = len(ps)

    def body(*refs):
        p, r = refs[:n], refs[n:2 * n]
        for cp in _chips_copies(p, r, refs[2 * n], refs[2 * n + 1]):
            cp.wait_send()
            cp.wait_recv()

    res = pl.pallas_call(
        body, name=name, in_specs=[_HBM] * (2 * n) + [_SEMS, _SEMS] + [_ANY] * len(after), out_specs=[_HBM] * (2 * n),
        out_shape=[pltpu.HBM(p.shape, p.dtype) for p in ps] + [pltpu.HBM(l.shape, l.dtype) for l in lands],
        input_output_aliases={t: t for t in range(2 * n)},
        compiler_params=pltpu.CompilerParams(has_side_effects=_DATAFLOW),
    )(*ps, *lands, send, recv, *after)
    return res[:n], res[n:]


def _rs_add_chips(ps, rs, idx, *, name):
    n = len(ps)

    def body(idx_ref, *refs):
        for t in range(n):
            p_ref, r0, r1, r2 = refs[4 * t:4 * t + 4]
            refs[4 * n + t][...] = ((p_ref[...].astype(F32) + r0[...].astype(F32)) + r1[...].astype(F32)) + r2[...].astype(F32)

    in_specs, args = [], []
    for p, r in zip(ps, rs):
        _, rh, cols = p.shape
        blk = (None, rh // RS_ROW_SPLIT, cols)
        in_specs.append(pl.BlockSpec(blk, lambda i, idx_ref: (idx_ref[0], i, 0)))
        in_specs += [pl.BlockSpec(blk, lambda i, idx_ref, k=k: (k, i, 0)) for k in range(3)]
        args += [p, r, r, r]
    out_specs = [pl.BlockSpec((None, p.shape[1] // RS_ROW_SPLIT, p.shape[2]), lambda i, idx_ref: (idx_ref[1], i, 0))
                 for p in ps]
    return pl.pallas_call(
        body, name=name,
        grid_spec=pltpu.PrefetchScalarGridSpec(num_scalar_prefetch=1, grid=(RS_ROW_SPLIT,), in_specs=in_specs,
                                               out_specs=out_specs),
        out_shape=[jax.ShapeDtypeStruct((2,) + p.shape[1:], F32) for p in ps], compiler_params=_cp(1),
    )(idx, *args)


def _adamw(w, gs, m, v, *, name, dep=None):
    L, Rr, C = w.shape
    tr, tc = _pick(Rr, (256, 128, 64)), C
    if tr == Rr and Rr * C > 512 * 1024:
        tc = 256
    bc1 = 1.0 - ADAM_B1 ** ADAM_STEP
    bc2 = 1.0 - ADAM_B2 ** ADAM_STEP
    nd = 0 if dep is None else 1

    def body(*refs):
        w_ref, m_ref, v_ref = refs[0], refs[1], refs[2]
        g_refs = refs[3:3 + L]
        d_ref, mo_ref, vo_ref, go_ref = refs[3 + L + nd:]
        layer = pl.program_id(0)
        gv = g_refs[0][...]
        for q in range(1, L):
            gv = jnp.where(layer == q, g_refs[q][...], gv)
        mn = ADAM_B1 * m_ref[...] + (1.0 - ADAM_B1) * gv
        vn = ADAM_B2 * v_ref[...] + (1.0 - ADAM_B2) * (gv * gv)
        go_ref[...] = gv
        mo_ref[...] = mn
        vo_ref[...] = vn
        d_ref[...] = -ADAM_LR * ((mn / bc1) / (jnp.sqrt(vn / bc2) + ADAM_EPS) + ADAM_WD * w_ref[...])

    blk = pl.BlockSpec((None, tr, tc), lambda l, i, j: (l, i, j))
    gblks = [pl.BlockSpec((tr, tc), lambda l, i, j, q=q: (jnp.where(l == q, i, 0), jnp.where(l == q, j, 0))) for q in range(L)]
    return pl.pallas_call(
        body, name=name, grid=(L, Rr // tr, C // tc), in_specs=[blk] * 3 + gblks + [_ANY] * nd, out_specs=[blk] * 4,
        out_shape=[jax.ShapeDtypeStruct((L, Rr, C), F32)] * 4, compiler_params=_cp(3),
    )(w, m, v, *gs, *([] if dep is None else [dep]))


def _adamw_leaves(ws, gs, ms, vs, *, name):
    n = len(ws)
    bc1 = 1.0 - ADAM_B1 ** ADAM_STEP
    bc2 = 1.0 - ADAM_B2 ** ADAM_STEP

    def body(*refs):
        w_refs, g_refs, m_refs, v_refs, d_refs, mo_refs, vo_refs = (refs[q * n:(q + 1) * n] for q in range(7))
        for k in range(n):
            gv = g_refs[k][...]
            mn = ADAM_B1 * m_refs[k][...] + (1.0 - ADAM_B1) * gv
            vn = ADAM_B2 * v_refs[k][...] + (1.0 - ADAM_B2) * (gv * gv)
            mo_refs[k][...] = mn
            vo_refs[k][...] = vn
            d_refs[k][...] = -ADAM_LR * ((mn / bc1) / (jnp.sqrt(vn / bc2) + ADAM_EPS) + ADAM_WD * w_refs[k][...])

    in_vmem = pl.BlockSpec(memory_space=pltpu.VMEM)
    outs = pl.pallas_call(
        body, name=name, in_specs=[in_vmem] * (4 * n), out_specs=[in_vmem] * (3 * n),
        out_shape=[jax.ShapeDtypeStruct(t.shape, F32) for t in ws] * 3,
    )(*ws, *gs, *ms, *vs)
    return outs[:n], outs[n:2 * n], outs[2 * n:]


def kernel(x, positions, a_norm, a_in_proj, a_conv_w, a_conv_b, a_dt_bias, a_A_log, a_D, a_gnorm, a_out_proj,
           kv_norm, w_kv, b_kv, k_norm, b_norm, w_q, b_q, q_norm, sinks, w_o, b_o, f_norm, f_w_in, f_conv_w,
           f_conv_b, f_w_down, loss_target, m_a_norm, m_a_in_proj, m_a_conv_w, m_a_conv_b, m_a_dt_bias, m_a_A_log,
           m_a_D, m_a_gnorm, m_a_out_proj, m_kv_norm, m_w_kv, m_b_kv, m_k_norm, m_b_norm, m_w_q, m_b_q, m_q_norm,
           m_sinks, m_w_o, m_b_o, m_f_norm, m_f_w_in, m_f_conv_w, m_f_conv_b, m_f_w_down, v_a_norm, v_a_in_proj,
           v_a_conv_w, v_a_conv_b, v_a_dt_bias, v_a_A_log, v_a_D, v_a_gnorm, v_a_out_proj, v_kv_norm, v_w_kv,
           v_b_kv, v_k_norm, v_b_norm, v_w_q, v_b_q, v_q_norm, v_sinks, v_w_o, v_b_o, v_f_norm, v_f_w_in,
           v_f_conv_w, v_f_conv_b, v_f_w_down):
    wl = dict(zip(WEIGHTS, (a_norm, a_in_proj, a_conv_w, a_conv_b, a_dt_bias, a_A_log, a_D, a_gnorm, a_out_proj,
                            kv_norm, w_kv, b_kv, k_norm, b_norm, w_q, b_q, q_norm, sinks, w_o, b_o, f_norm, f_w_in,
                            f_conv_w, f_conv_b, f_w_down)))
    ml = dict(zip(WEIGHTS, (m_a_norm, m_a_in_proj, m_a_conv_w, m_a_conv_b, m_a_dt_bias, m_a_A_log, m_a_D, m_a_gnorm,
                            m_a_out_proj, m_kv_norm, m_w_kv, m_b_kv, m_k_norm, m_b_norm, m_w_q, m_b_q, m_q_norm,
                            m_sinks, m_w_o, m_b_o, m_f_norm, m_f_w_in, m_f_conv_w, m_f_conv_b, m_f_w_down)))
    vl = dict(zip(WEIGHTS, (v_a_norm, v_a_in_proj, v_a_conv_w, v_a_conv_b, v_a_dt_bias, v_a_A_log, v_a_D, v_a_gnorm,
                            v_a_out_proj, v_kv_norm, v_w_kv, v_b_kv, v_k_norm, v_b_norm, v_w_q, v_b_q, v_q_norm,
                            v_sinks, v_w_o, v_b_o, v_f_norm, v_f_w_in, v_f_conv_w, v_f_conv_b, v_f_w_down)))
    xi, yi, ci = _coords()
    me = 2 * xi + yi
    S = x.shape[1]

    def block_of(n, layer):
        t = wl[n]
        return t if layer is None else t[layer]

    rows = lambda t: t.reshape(-1, t.shape[-1])
    c_idx = jnp.reshape(ci, (1,)).astype(jnp.int32)
    me_c = jnp.stack([me, ci]).astype(jnp.int32)
    early = ("in_proj",)
    late = (("out_proj", "f_in0", "f_down0"), ("w_kv", "w_q", "w_o", "f_in1", "f_down1"))
    shards = {name: _halves(block_of(wn, layer).astype(BF16)) for name, wn, layer in MATS}

    sp = _pack([wl[n] for n, _ in SMALL_CUT], 8, 128, F32)
    gathered, gs = _gather_weights([shards[k] for k in early], sp)
    gt = {k: t.reshape(N_CHIPS, -1, t.shape[-1]) for k, t in zip(early, gathered)}
    started = {0: _gather_start([shards[k] for k in late[0]], gs, name="gather_late_start0")}
    full = {n: wl[n] for n in SMALL_REP}
    gs = gs.reshape(N_CHIPS, -1)
    off = 0
    for n, ax in SMALL_CUT:
        shp = wl[n].shape
        size = math.prod(shp)
        piece = jnp.moveaxis(gs[:, off:off + size].reshape((N_CHIPS,) + shp), 0, ax)
        full[n] = piece.reshape(shp[:ax] + (N_CHIPS * shp[ax],) + shp[ax + 1:])
        off += size
    w = _prep_small(full, {})
    w["w_zx"], w["w_dt"] = _join_in_proj(gt["in_proj"])
    w["dep"] = started[0][4]

    class Comm:
        flight = []
        reduced = {}

        forwarding = {}

        def late_start(self, part, after):
            started[part] = _gather_start([shards[k] for k in late[part]], after, name=f"gather_late_start{part}")
            return started[part][4]

        def late_arrived(self, part, after):
            send, recv, shs, lands, _ = started[part]
            lands = _gather_wait(send, recv, shs, lands, after[0], name=f"gather_late_wait{part}")
            send, recv, lands, token = _start_copies(_forward_copies, list(lands), 3 * len(lands), after,
                                                     name=f"gather_late_forward_start{part}")
            self.forwarding[part] = (send, recv, lands)
            return token

        def late_weights(self, w, after, part):
            send, recv, lands = self.forwarding[part]
            lands = _wait_copies(_forward_copies, send, recv, lands, [after], name=f"gather_late_forward_wait{part}")
            lt = {k: t.reshape(N_CHIPS, -1, t.shape[-1]) for k, t in zip(late[part], lands)}
            w = dict(w)
            if part == 0:
                w["a_out_proj"], w["f_w_in"], w["f_w_down"] = rows(lt["out_proj"]), [lt["f_in0"]], [rows(lt["f_down0"])]
            else:
                w["w_kv"], w["w_q"], w["w_o"] = (rows(lt[k]) for k in ("w_kv", "w_q", "w_o"))
                w["f_w_in"], w["f_w_down"] = w["f_w_in"] + [lt["f_in1"]], w["f_w_down"] + [rows(lt["f_down1"])]
            return w

        def advance(self, after, group=None, tensors=None):
            token = None
            for grp in list(self.flight):
                tag, n = grp["tag"], len(grp["names"])
                dep = list(after) + ([] if token is None else [token])
                if grp["stage"] == "sibling":
                    arrs = _wait_copies(_sibling_copies, grp["send"], grp["recv"], grp["arrays"], dep, name=f"rs_sibling_wait{tag}")
                    pairs = _rs_add_pair(arrs[:n], arrs[n:], c_idx, name=f"rs_add_pair{tag}")
                    send, recv, ps, lands, token = _rs_chips_start(pairs, dep, name=f"rs_chips_start{tag}")
                    grp.update(stage="chips", send=send, recv=recv, ps=ps, lands=lands)
                elif grp["stage"] == "chips":
                    ps, rs = _rs_chips_wait(grp["send"], grp["recv"], grp["ps"], grp["lands"], dep, name=f"rs_chips_wait{tag}")
                    halves = _rs_add_chips(ps, rs, me_c, name=f"rs_add_chips{tag}")
                    send, recv, arrs, token = _start_copies(_join_copies, halves, n, dep, name=f"rs_join_start{tag}")
                    grp.update(stage="join", send=send, recv=recv, arrays=arrs)
                else:
                    joined = _wait_copies(_join_copies, grp["send"], grp["recv"], grp["arrays"], dep, name=f"rs_join_wait{tag}")
                    self.reduced.update({k: rows(t) for k, t in zip(grp["names"], joined)})
                    self.flight.remove(grp)
            if group is not None:
                names = list(tensors)
                glist = [tensors[k].reshape(N_CHIPS, 2, -1, tensors[k].shape[-1]) for k in names]
                lands = [lax.empty((N_CHIPS,) + gq.shape[2:], gq.dtype) for gq in glist]
                dep = list(after) + ([] if token is None else [token])
                send, recv, arrs, token = _start_copies(_sibling_copies, glist + lands, len(names), dep,
                                                        name=f"rs_sibling_start{group}")
                self.flight.append(dict(tag=group, names=names, stage="sibling", send=send, recv=recv, arrays=arrs))
            return token

    comm = Comm()

    posf = positions.reshape(S, 1).astype(F32)
    loss_part, dx0, gr, tok = _local_step(x[0], posf, loss_target[0], w, comm)
    g = _small_grads(gr)

    small_names = [n for n, _ in SMALL_CUT] + list(SMALL_REP)
    sv = _pack([g[n] for n in small_names] + [loss_part[0:1, 0:1]], 8, 128, F32)
    s_send, s_recv, sv, s_land, s_token = _small_start(sv, tok, name="small_start")

    grads, delta, new_m, new_v = {}, {}, {}, {}

    def update(wn, dep):
        gl = [comm.reduced[name] for name, n2, _ in MATS if n2 == wn]
        shp = wl[wn].shape
        three = (len(gl),) + gl[0].shape
        flip = shp[-1] % 128 != 0
        view = (lambda t: t.reshape(three).transpose(0, 2, 1)) if flip else (lambda t: t.reshape(three))
        back = (lambda t: t.transpose(0, 2, 1).reshape(shp)) if flip else (lambda t: t.reshape(shp))
        if flip:
            gl = [t.T for t in gl]
        d, mn, vn, go = _adamw(view(wl[wn]), gl, view(ml[wn]), view(vl[wn]), name="adamw_" + wn, dep=dep)
        grads[wn], delta[wn], new_m[wn], new_v[wn] = back(go), back(d), back(mn), back(vn)
        return d

    first = [update(wn, s_token) for wn in ("w_q", "w_o", "w_kv")]
    tok = comm.advance(first)
    second = [update(wn, tok) for wn in ("f_w_in", "f_w_down", "a_out_proj")]
    tok = comm.advance(second)
    done = first + second + [tok]

    sv, s_land = _small_wait(s_send, s_recv, sv, s_land, done, name="small_wait")
    sred = _small_sum(sv, s_land, jnp.reshape(2 * me + ci, (1,)).astype(jnp.int32)).reshape(-1)
    small_shapes = [g[n].shape for n in small_names] + [(1,)]
    sg = dict(zip(small_names + ["loss"], _unpack(sred, small_shapes)))
    loss = sg["loss"].reshape(())
    g_small = {}
    for n, ax in SMALL_CUT:
        size = wl[n].shape[ax]
        g_small[n] = lax.dynamic_slice_in_dim(sg[n], me * size, size, axis=ax)
    for n in SMALL_REP:
        g_small[n] = sg[n].reshape(wl[n].shape)

    leaves = lambda d: [d[n].reshape(1, -1) if d[n].ndim == 1 else d[n] for n in small_names]
    ds, mns, vns = _adamw_leaves(leaves(wl), leaves(g_small), leaves(ml), leaves(vl), name="adamw_small")
    comm.advance([ds[0]])
    update("a_in_proj", None)
    for n, dd, mm, vv in zip(small_names, ds, mns, vns):
        shp = wl[n].shape
        grads[n], delta[n], new_m[n], new_v[n] = g_small[n], dd.reshape(shp), mm.reshape(shp), vv.reshape(shp)

    return (loss, dx0[None], *[grads[n] for n in WEIGHTS], *[delta[n] for n in WEIGHTS],
            *[new_m[n] for n in WEIGHTS], *[new_v[n] for n in WEIGHTS])
```

```python
import math

import jax
import jax.numpy as jnp
from jax import lax
from jax.experimental import pallas as pl
from jax.experimental.pallas import tpu as pltpu

F32 = jnp.float32
BF16 = jnp.bfloat16

EPS = 1e-5
CHUNK = 256
WINDOW = 128
HEAD = 64
SSM_HEADS = 32
SSM_GROUPS = 8
SSM_STATE = 128
ATT_KV = 4
ATT_G = 4
ROPE_THETA = 10000.0
NEG = -1e30
N_CHIPS = 4
VMEM_LIMIT = 56 * 1024 * 1024

ADAM_LR, ADAM_B1, ADAM_B2, ADAM_EPS, ADAM_WD, ADAM_STEP = 0.001, 0.9, 0.999, 1e-08, 0.01, 10


def _cp(n_axes):
    return pltpu.CompilerParams(dimension_semantics=("arbitrary",) * n_axes, vmem_limit_bytes=VMEM_LIMIT)


def _pick(dim, prefs):
    for p in prefs:
        if dim % p == 0:
            return p
    return dim


def _iota(shape, dim):
    return lax.broadcasted_iota(jnp.int32, shape, dim)


def _dot(a, b, ca=1, cb=0):
    return lax.dot_general(a, b, (((ca,), (cb,)), ((), ())), preferred_element_type=F32)


def _dot3(x, ind):
    h = x.astype(BF16)
    r = x - h.astype(F32)
    m = r.astype(BF16)
    lo = (r - m.astype(F32)).astype(BF16)
    return _dot(h, ind) + _dot(m, ind) + _dot(lo, ind)


def _sigmoid(x):
    return jax.nn.sigmoid(x)


def _mm(a, b, *, name, ta=False, tb=False, bias=None, res=None, out_dtype=F32, b_koff=0, tm=None, tn=None, tk=None,
        dims=None, a_spec=None, b_spec=None, o_spec=None, o_shape=None, dep=None, more=(), target=None,
        rms=None, rms_colsum=False):
    if dims is not None:
        M, N, K = dims
    else:
        if ta:
            K, M = a.shape
        else:
            M, K = a.shape
        N = b.shape[0] if tb else b.shape[1]
    tm = tm or _pick(M, (1024, 1408, 512, 256, 128))
    tn = tn or _pick(N, (512, 1408, 256, 128))
    tk = tk or (K if K <= 2048 else _pick(K, (2048, 1408, 1024, 512)))
    assert M % tm == 0 and N % tn == 0 and K % tk == 0 and b_koff % tk == 0
    nk = K // tk
    kb0 = b_koff // tk
    has_bias, has_res = bias is not None, res is not None

    def body(*refs):
        a_ref, b_ref = refs[0], refs[1]
        pos = 2
        bias_ref = res_ref = acc_ref = None
        if has_bias:
            bias_ref = refs[pos]
            pos += 1
        if has_res:
            res_ref = refs[pos]
            pos += 1
        if dep is not None:
            pos += 1
        extra = refs[pos:pos + 2 * len(more)]
        pos += 2 * len(more)
        tgt_ref = lp_ref = rx_ref = rg_ref = rd_ref = dg_ref = cs_ref = None
        if target is not None:
            tgt_ref = refs[pos]
            pos += 1
        if rms is not None:
            rx_ref, rg_ref, rd_ref = refs[pos:pos + 3]
            pos += 3
        o_ref = refs[pos]
        pos += 1
        if target is not None:
            lp_ref = refs[pos]
            pos += 1
        if rms is not None:
            dg_ref = refs[pos]
            pos += 1
            if rms_colsum:
                cs_ref = refs[pos]
                pos += 1
        if nk > 1:
            acc_ref = refs[pos]
        part = _dot(a_ref[...].astype(BF16), b_ref[...].astype(BF16), 0 if ta else 1, 1 if tb else 0)
        for q in range(len(more)):
            part = part + _dot(extra[2 * q][...].astype(BF16), extra[2 * q + 1][...].astype(BF16),
                               0 if ta else 1, 1 if tb else 0)

        def finish(acc):
            if has_bias:
                acc = acc + bias_ref[...]
            if has_res:
                acc = acc + res_ref[...]
            if target is not None:
                err = acc - tgt_ref[...]
                acc = err * (1.0 / N)
                part_loss = jnp.sum(jnp.sum(err * err, axis=1, keepdims=True), axis=0, keepdims=True) * (0.5 / N)
                first = (pl.program_id(0) == 0) & (pl.program_id(1) == 0)

                @pl.when(first)
                def _():
                    lp_ref[...] = jnp.broadcast_to(part_loss, lp_ref.shape)

                @pl.when(jnp.logical_not(first))
                def _():
                    lp_ref[...] += jnp.broadcast_to(part_loss, lp_ref.shape)

            if rms is not None:
                xv = rx_ref[...]
                r = lax.rsqrt(jnp.mean(xv * xv, axis=-1, keepdims=True) + EPS)
                xh = xv * r
                dxh = acc * rg_ref[...]
                dg_part = jnp.sum(acc * xh, axis=0, keepdims=True)
                acc = rd_ref[...] + r * (dxh - xh * jnp.mean(dxh * xh, axis=-1, keepdims=True))
                cs_part = jnp.sum(acc, axis=0, keepdims=True) if rms_colsum else None
                first_rows = pl.program_id(0) == 0

                @pl.when(first_rows)
                def _():
                    dg_ref[...] = dg_part
                    if rms_colsum:
                        cs_ref[...] = cs_part

                @pl.when(jnp.logical_not(first_rows))
                def _():
                    dg_ref[...] += dg_part
                    if rms_colsum:
                        cs_ref[...] += cs_part

            o_ref[...] = acc.astype(out_dtype)

        if nk == 1:
            finish(part)
        else:
            k = pl.program_id(2)

            @pl.when(k == 0)
            def _():
                acc_ref[...] = part

            @pl.when(k > 0)
            def _():
                acc_ref[...] += part

            @pl.when(k == nk - 1)
            def _():
                finish(acc_ref[...])

    if a_spec is None:
        a_spec = pl.BlockSpec((tk, tm), lambda i, j, k: (k, i)) if ta else pl.BlockSpec((tm, tk), lambda i, j, k: (i, k))
    if b_spec is None:
        b_spec = (pl.BlockSpec((tn, tk), lambda i, j, k: (j, k + kb0)) if tb
                  else pl.BlockSpec((tk, tn), lambda i, j, k: (k + kb0, j)))
    if o_spec is None:
        o_spec = pl.BlockSpec((tm, tn), lambda i, j, k: (i, j))
    in_specs, args = [a_spec, b_spec], [a, b]
    if has_bias:
        in_specs.append(pl.BlockSpec((1, tn), lambda i, j, k: (0, j)))
        args.append(bias)
    if has_res:
        in_specs.append(pl.BlockSpec((tm, tn), lambda i, j, k: (i, j)))
        args.append(res)
    if dep is not None:
        in_specs.append(pl.BlockSpec(memory_space=pl.ANY))
        args.append(dep)
    for piece in more:
        a2, sa, b2, sb = piece if len(piece) == 4 else (a, piece[0], b, piece[1])
        in_specs += [sa, sb]
        args += [a2, b2]
    out_specs, out_shape = [o_spec], [jax.ShapeDtypeStruct(o_shape or (M, N), out_dtype)]
    if target is not None:
        in_specs.append(pl.BlockSpec((tm, tn), lambda i, j, k: (i, j)))
        args.append(target)
        out_specs.append(pl.BlockSpec((8, 128), lambda i, j, k: (0, 0)))
        out_shape.append(jax.ShapeDtypeStruct((8, 128), F32))
    if rms is not None:
        assert tn == N and nk == 1
        row, vec = pl.BlockSpec((tm, N), lambda i, j, k: (i, 0)), pl.BlockSpec((1, N), lambda i, j, k: (0, 0))
        in_specs += [row, vec, row]
        args += list(rms)
        out_specs += [vec] * (2 if rms_colsum else 1)
        out_shape += [jax.ShapeDtypeStruct((1, N), F32)] * (2 if rms_colsum else 1)
    if len(out_specs) == 1:
        out_specs, out_shape = out_specs[0], out_shape[0]
    return pl.pallas_call(
        body, name=name, grid=(M // tm, N // tn, nk), in_specs=in_specs, out_specs=out_specs, out_shape=out_shape,
        scratch_shapes=[pltpu.VMEM((tm, tn), F32)] if nk > 1 else [],
        compiler_params=_cp(3),
    )(*args)


def _norm_mm(x, gain, b, *, name, bias=None, N=None, tn=None, b_spec=None, dep=None):
    M, K = x.shape
    N = N or b.shape[1]
    tm = _pick(M, (1024, 512, 256))
    tn = tn or _pick(N, (512, 1408, 256, 128))
    has_bias = bias is not None

    def body(*refs):
        x_ref, g_ref, b_ref = refs[:3]
        pos = 3 + (1 if has_bias else 0) + (0 if dep is None else 1)
        o_ref, h_ref = refs[pos], refs[pos + 1]

        @pl.when(pl.program_id(1) == 0)
        def _():
            xv = x_ref[...]
            h_ref[...] = (xv * lax.rsqrt(jnp.mean(xv * xv, axis=-1, keepdims=True) + EPS) * g_ref[...]).astype(BF16)

        acc = _dot(h_ref[...], b_ref[...].astype(BF16))
        if has_bias:
            acc = acc + refs[3][...]
        o_ref[...] = acc

    in_specs = [pl.BlockSpec((tm, K), lambda i, j: (i, 0)), pl.BlockSpec((1, K), lambda i, j: (0, 0)),
                b_spec or pl.BlockSpec((K, tn), lambda i, j: (0, j))]
    args = [x, gain, b]
    if has_bias:
        in_specs.append(pl.BlockSpec((1, tn), lambda i, j: (0, j)))
        args.append(bias)
    if dep is not None:
        in_specs.append(pl.BlockSpec(memory_space=pl.ANY))
        args.append(dep)
    return pl.pallas_call(
        body, name=name, grid=(M // tm, N // tn), in_specs=in_specs,
        out_specs=[pl.BlockSpec((tm, tn), lambda i, j: (i, j)), pl.BlockSpec((tm, K), lambda i, j: (i, 0))],
        out_shape=[jax.ShapeDtypeStruct((M, N), F32), jax.ShapeDtypeStruct((M, K), BF16)], compiler_params=_cp(2),
    )(*args)


def _rms_bwd(x, gains, dhs, dres, *, name, tr=256, want_colsum=False):
    S, D = x.shape
    n = len(gains)
    steps = S // tr

    def body(*refs):
        x_ref = refs[0]
        g_refs = refs[1:1 + n]
        dh_refs = refs[1 + n:1 + 2 * n]
        dres_ref = refs[1 + 2 * n]
        dx_ref = refs[2 + 2 * n]
        dg_refs = refs[3 + 2 * n:3 + 3 * n]
        cs_ref = refs[3 + 3 * n] if want_colsum else None
        i = pl.program_id(0)
        xv = x_ref[...]
        r = lax.rsqrt(jnp.mean(xv * xv, axis=-1, keepdims=True) + EPS)
        xh = xv * r
        dx = dres_ref[...]
        for q in range(n):
            dh = dh_refs[q][...]
            dxh = dh * g_refs[q][...]
            dx = dx + r * (dxh - xh * jnp.mean(dxh * xh, axis=-1, keepdims=True))
            part = jnp.sum(dh * xh, axis=0, keepdims=True)

            @pl.when(i == 0)
            def _():
                dg_refs[q][...] = part

            @pl.when(i > 0)
            def _():
                dg_refs[q][...] += part

        dx_ref[...] = dx
        if want_colsum:
            cpart = jnp.sum(dx, axis=0, keepdims=True)

            @pl.when(i == 0)
            def _():
                cs_ref[...] = cpart

            @pl.when(i > 0)
            def _():
                cs_ref[...] += cpart

    row = pl.BlockSpec((tr, D), lambda i: (i, 0))
    vec = pl.BlockSpec((1, D), lambda i: (0, 0))
    n_vec_out = n + (1 if want_colsum else 0)
    outs = pl.pallas_call(
        body, name=name, grid=(steps,), in_specs=[row] + [vec] * n + [row] * n + [row],
        out_specs=[row] + [vec] * n_vec_out,
        out_shape=[jax.ShapeDtypeStruct((S, D), F32)] + [jax.ShapeDtypeStruct((1, D), F32)] * n_vec_out,
        compiler_params=_cp(1),
    )(x, *gains, *dhs, dres)
    return outs


def _colsum(x, *, name, tr=256):
    S, D = x.shape

    def body(x_ref, o_ref):
        i = pl.program_id(0)
        part = jnp.sum(x_ref[...].astype(F32), axis=0, keepdims=True)

        @pl.when(i == 0)
        def _():
            o_ref[...] = part

        @pl.when(i > 0)
        def _():
            o_ref[...] += part

    return pl.pallas_call(
        body, name=name, grid=(S // tr,), in_specs=[pl.BlockSpec((tr, D), lambda i: (i, 0))],
        out_specs=pl.BlockSpec((1, D), lambda i: (0, 0)), out_shape=jax.ShapeDtypeStruct((1, D), F32),
        compiler_params=_cp(1),
    )(x)


STRIP = 64
HALO = 8


def _strips(S, tc):
    return [(r0, slice(l0, l0 + 128)) for l0 in range(0, tc, 128) for r0 in range(S - STRIP, -1, -STRIP)]


def _with_halo(ref, r0, ls):
    if r0 == 0:
        return jnp.concatenate([jnp.zeros((HALO, 128), F32), ref[0:STRIP, ls]], axis=0)
    return ref[r0 - HALO:r0 + STRIP, ls]


def _conv_strip(xw, w_ref, b_ref, ls, width):
    acc = b_ref[:, ls] + w_ref[pl.ds(width - 1, 1), ls] * xw[HALO:]
    shifted = []
    for s in range(1, width):
        xs = pltpu.roll(xw, s, axis=0)[HALO:]
        shifted.append(xs)
        acc = acc + w_ref[pl.ds(width - 1 - s, 1), ls] * xs
    return acc, shifted


def _conv_strip_back(dacc, after, xc, shifted, w_ref, ls, width):
    ext = jnp.concatenate([dacc, after], axis=0)
    dx = w_ref[pl.ds(width - 1, 1), ls] * dacc
    dws = [None] * width
    dws[width - 1] = jnp.sum(dacc * xc, axis=0, keepdims=True)
    for s in range(1, width):
        dx = dx + w_ref[pl.ds(width - 1 - s, 1), ls] * pltpu.roll(ext, STRIP + HALO - s, axis=0)[:STRIP]
        dws[width - 1 - s] = jnp.sum(dacc * shifted[s - 1], axis=0, keepdims=True)
    return dx, dws, jnp.sum(dacc, axis=0, keepdims=True)


def _conv_back_block(S, tc, width, w_ref, b_ref, x_ref, dacc_of, dx_store, dw_ref, db_ref):
    for l0 in range(0, tc, 128):
        ls = slice(l0, l0 + 128)
        after = jnp.zeros((HALO, 128), F32)
        tot = None
        for r0 in range(S - STRIP, -1, -STRIP):
            xw = _with_halo(x_ref, r0, ls)
            acc, shifted = _conv_strip(xw, w_ref, b_ref, ls, width)
            dacc = dacc_of(r0, ls, acc, _sigmoid(acc))
            dx, dws, db = _conv_strip_back(dacc, after, xw[HALO:], shifted, w_ref, ls, width)
            dx_store(r0, ls, dx)
            after = dacc[:HALO]
            part = dws + [db]
            tot = part if tot is None else [p + q for p, q in zip(tot, part)]
        for k in range(width):
            dw_ref[pl.ds(k, 1), ls] = tot[k]
        db_ref[:, ls] = tot[width]


def _conv_silu_fwd(xin, col0, C, w, b, *, name, tc=512):
    S = xin.shape[0]
    width = w.shape[0]
    off = col0 // tc

    def body(x_ref, w_ref, b_ref, o_ref):
        for r0, ls in _strips(S, tc):
            acc, _ = _conv_strip(_with_halo(x_ref, r0, ls), w_ref, b_ref, ls, width)
            o_ref[r0:r0 + STRIP, ls] = acc * _sigmoid(acc)

    return pl.pallas_call(
        body, name=name, grid=(C // tc,),
        in_specs=[pl.BlockSpec((S, tc), lambda j: (0, j + off)), pl.BlockSpec((width, tc), lambda j: (0, j)),
                  pl.BlockSpec((1, tc), lambda j: (0, j))],
        out_specs=pl.BlockSpec((S, tc), lambda j: (0, j)), out_shape=jax.ShapeDtypeStruct((S, C), F32),
        compiler_params=_cp(1),
    )(xin, w, b)


def _conv_silu_bwd(xin, col0, C, w, b, douts, *, name, tc=256):
    S = xin.shape[0]
    width = w.shape[0]
    off = col0 // tc
    nd = len(douts)
    ranges = [(o // tc, (o + d.shape[1]) // tc) for d, o in douts]

    def body(*refs):
        x_ref, w_ref, b_ref = refs[0], refs[1], refs[2]
        d_refs = refs[3:3 + nd]
        dx_ref, dw_ref, db_ref = refs[3 + nd], refs[4 + nd], refs[5 + nd]
        j = pl.program_id(0)

        def dacc_of(r0, ls, acc, sg):
            dout = jnp.zeros((STRIP, 128), F32)
            for q in range(nd):
                lo, hi = ranges[q]
                dout = dout + jnp.where((j >= lo) & (j < hi), d_refs[q][r0:r0 + STRIP, ls], 0.0)
            return dout * (sg * (1.0 + acc * (1.0 - sg)))

        def dx_store(r0, ls, dx):
            dx_ref[r0:r0 + STRIP, ls] = dx.astype(BF16)

        _conv_back_block(S, tc, width, w_ref, b_ref, x_ref, dacc_of, dx_store, dw_ref, db_ref)

    d_specs = [pl.BlockSpec((S, tc), (lambda j, lo=lo, hi=hi: (0, jnp.clip(j - lo, 0, hi - lo - 1)))) for lo, hi in ranges]
    return pl.pallas_call(
        body, name=name, grid=(C // tc,),
        in_specs=[pl.BlockSpec((S, tc), lambda j: (0, j + off)), pl.BlockSpec((width, tc), lambda j: (0, j)),
                  pl.BlockSpec((1, tc), lambda j: (0, j))] + d_specs,
        out_specs=[pl.BlockSpec((S, tc), lambda j: (0, j)), pl.BlockSpec((width, tc), lambda j: (0, j)),
                   pl.BlockSpec((1, tc), lambda j: (0, j))],
        out_shape=[jax.ShapeDtypeStruct((S, C), BF16), jax.ShapeDtypeStruct((width, C), F32),
                   jax.ShapeDtypeStruct((1, C), F32)],
        compiler_params=_cp(1),
    )(xin, w, b, *[d for d, _ in douts])


def _ffn_act_fwd(u, w, b, *, name, tc=256):
    S, F2 = u.shape
    Fd = F2 // 2
    width = w.shape[0]
    nb = Fd // tc

    def body(g_ref, v_ref, w_ref, b_ref, o_ref):
        for r0, ls in _strips(S, tc):
            acc, _ = _conv_strip(_with_halo(g_ref, r0, ls), w_ref, b_ref, ls, width)
            o_ref[r0:r0 + STRIP, ls] = (acc * _sigmoid(acc) * v_ref[r0:r0 + STRIP, ls]).astype(BF16)

    return pl.pallas_call(
        body, name=name, grid=(nb,),
        in_specs=[pl.BlockSpec((S, tc), lambda j: (0, j)), pl.BlockSpec((S, tc), lambda j: (0, j + nb)),
                  pl.BlockSpec((width, tc), lambda j: (0, j)), pl.BlockSpec((1, tc), lambda j: (0, j))],
        out_specs=pl.BlockSpec((S, tc), lambda j: (0, j)), out_shape=jax.ShapeDtypeStruct((S, Fd), BF16),
        compiler_params=_cp(1),
    )(u, u, w, b)


def _ffn_act_bwd(u, w, b, da, *, name, tc=256):
    S, F2 = u.shape
    Fd = F2 // 2
    width = w.shape[0]
    nb = Fd // tc

    def body(g_ref, v_ref, w_ref, b_ref, da_ref, du_ref, dw_ref, db_ref, a_ref):
        def dacc_of(r0, ls, acc, sg):
            rs = slice(r0, r0 + STRIP)
            dav, val, silu = da_ref[rs, ls], v_ref[rs, ls], acc * sg
            a_ref[rs, ls] = (silu * val).astype(BF16)
            du_ref[1, rs, ls] = (dav * silu).astype(BF16)
            return dav * val * (sg * (1.0 + acc * (1.0 - sg)))

        def dx_store(r0, ls, dx):
            du_ref[0, r0:r0 + STRIP, ls] = dx.astype(BF16)

        _conv_back_block(S, tc, width, w_ref, b_ref, g_ref, dacc_of, dx_store, dw_ref, db_ref)

    blk = pl.BlockSpec((S, tc), lambda j: (0, j))
    return pl.pallas_call(
        body, name=name, grid=(nb,),
        in_specs=[blk, pl.BlockSpec((S, tc), lambda j: (0, j + nb)), pl.BlockSpec((width, tc), lambda j: (0, j)),
                  pl.BlockSpec((1, tc), lambda j: (0, j)), blk],
        out_specs=[pl.BlockSpec((2, S, tc), lambda j: (0, 0, j)), pl.BlockSpec((width, tc), lambda j: (0, j)),
                   pl.BlockSpec((1, tc), lambda j: (0, j)), blk],
        out_shape=[jax.ShapeDtypeStruct((2, S, Fd), BF16),
                   jax.ShapeDtypeStruct((width, Fd), F32), jax.ShapeDtypeStruct((1, Fd), F32),
                   jax.ShapeDtypeStruct((S, Fd), BF16)],
        compiler_params=_cp(1),
    )(u, u, w, b, da)


def _ssd_prep(dtr, dt_bias, a_log, *, name="ssd_prep"):
    S = dtr.shape[0]

    def body(d_ref, b_ref, al_ref, dt_ref, ac_ref, sg_ref, act_ref):
        lane = _iota((CHUNK, 128), 1)
        valid = lane < SSM_HEADS
        z = d_ref[...] + b_ref[...]
        dt = jnp.where(valid, jnp.maximum(z, 0.0) + jnp.log(1.0 + jnp.exp(-jnp.abs(z))), 0.0)
        a = dt * (-jnp.exp(al_ref[...]))
        row = _iota((CHUNK, 128), 0)
        k = 1
        while k < CHUNK:
            a = a + jnp.where(row >= k, pltpu.roll(a, k, axis=0), 0.0)
            k *= 2
        sg = jnp.where(valid, _sigmoid(z), 0.0)
        for arr, ref in ((dt, dt_ref), (a, ac_ref), (sg, sg_ref)):
            for g in range(SSM_GROUPS):
                ref[g] = jnp.where(lane < 4, arr if g == 0 else pltpu.roll(arr, 128 - 4 * g, axis=1), 0.0)
        act_ref[...] = a.T[:SSM_HEADS, :]

    blk = pl.BlockSpec((CHUNK, 128), lambda i: (i, 0))
    vec = pl.BlockSpec((1, 128), lambda i: (0, 0))
    grp = pl.BlockSpec((SSM_GROUPS, CHUNK, 128), lambda i: (0, i, 0))
    return pl.pallas_call(
        body, name=name, grid=(S // CHUNK,), in_specs=[blk, vec, vec],
        out_specs=[grp, grp, grp, pl.BlockSpec((SSM_HEADS, CHUNK), lambda i: (0, i))],
        out_shape=[jax.ShapeDtypeStruct((SSM_GROUPS, S, 128), F32)] * 3 + [jax.ShapeDtypeStruct((SSM_HEADS, S), F32)],
        compiler_params=_cp(1),
    )(dtr, dt_bias, a_log)


SSD_GPS = 4


def _expand4(v, lanes):
    out = jnp.broadcast_to(v[:, 3:4], lanes.shape)
    for hh in (2, 1, 0):
        out = jnp.where(lanes < 64 * (hh + 1), v[:, hh:hh + 1], out)
    return out


def _ssd_fwd(xbc, dt_g, ac_g, ac_t, *, name="ssd_fwd", dep=None):
    S = xbc.shape[0]
    nc = S // CHUNK
    Lc = CHUNK

    def body(x_ref, b_ref, c_ref, dt_ref, ac_ref, act_ref, *rest):
        y_ref, st_out_ref, st_ref = rest[-3:]
        g2 = pl.program_id(0)
        c = pl.program_id(1)

        @pl.when(c == 0)
        def _():
            st_ref[...] = jnp.zeros_like(st_ref)

        causal = _iota((Lc, Lc), 0) >= _iota((Lc, Lc), 1)
        lane256 = _iota((Lc, 256), 1)
        lane128 = _iota((Lc, 128), 1)
        row128 = _iota((128, 128), 0)
        for gg in range(SSD_GPS):
            g = SSD_GPS * g2 + gg
            bv = b_ref[:, 128 * gg:128 * (gg + 1)]
            cbf = c_ref[:, 128 * gg:128 * (gg + 1)].astype(BF16)
            cb = _dot(cbf, bv.astype(BF16), 1, 1)
            dtg, acg = dt_ref[gg], ac_ref[gg]
            ac_last = ac_ref[gg, pl.ds(Lc - 1, 1), :]
            dt4 = _expand4(dtg, lane256)
            ac4 = _expand4(acg, lane256)
            e4 = jnp.exp(ac4)
            xdb = (x_ref[:, 256 * gg:256 * (gg + 1)] * dt4).astype(BF16)
            st_out_ref[gg] = st_ref[gg]
            for p in range(2):
                xd_p = xdb[:, 128 * p:128 * (p + 1)]
                st_p = st_ref[gg, p]
                ys, sn, cds = [], [], []
                for q in range(2):
                    hh = 2 * p + q
                    a_col = acg[:, hh:hh + 1]
                    a_row = act_ref[pl.ds(4 * g + hh, 1), :]
                    dec = jnp.exp(jnp.where(causal, a_col - a_row, NEG))
                    w = (cb * dec).astype(BF16)
                    ys.append(_dot(w, xd_p))
                    al = ac_last[:, hh:hh + 1]
                    dte = jnp.exp(al - a_col)
                    sn.append(_dot(xd_p, (bv * dte).astype(BF16), 0, 0))
                    cds.append(jnp.exp(al))
                y_diag = jnp.where(lane128 < 64, ys[0], ys[1])
                y_off = _dot(cbf, st_p.astype(BF16), 1, 1) * e4[:, 128 * p:128 * (p + 1)]
                y_ref[:, 256 * gg + 128 * p:256 * gg + 128 * (p + 1)] = y_diag + y_off
                st_ref[gg, p] = jnp.where(row128 < 64, st_p * cds[0] + sn[0], st_p * cds[1] + sn[1])

    G = SSD_GPS
    per_g = lambda g, c: (g, c, 0)
    return pl.pallas_call(
        body, name=name, grid=(SSM_GROUPS // G, nc),
        in_specs=[pl.BlockSpec((Lc, 256 * G), lambda g, c: (c, g)),
                  pl.BlockSpec((Lc, 128 * G), lambda g, c: (c, 16 // G + g)),
                  pl.BlockSpec((Lc, 128 * G), lambda g, c: (c, 24 // G + g)),
                  pl.BlockSpec((G, Lc, 128), per_g), pl.BlockSpec((G, Lc, 128), per_g),
                  pl.BlockSpec((SSM_HEADS, Lc), lambda g, c: (0, c))] + ([] if dep is None else [pl.BlockSpec(memory_space=pl.ANY)]),
        out_specs=[pl.BlockSpec((Lc, 256 * G), lambda g, c: (c, g)),
                   pl.BlockSpec((G, None, 2, 128, 128), lambda g, c: (g, c, 0, 0, 0))],
        out_shape=[jax.ShapeDtypeStruct((S, 2048), F32), jax.ShapeDtypeStruct((SSM_GROUPS, nc, 2, 128, 128), F32)],
        scratch_shapes=[pltpu.VMEM((G, 2, 128, 128), F32)], compiler_params=_cp(2),
    )(xbc, xbc, xbc, dt_g, ac_g, ac_t, *([] if dep is None else [dep]))


def _ssd_bwd(xbc, dt_g, ac_g, ac_t, states, dy, dexp, *, name="ssd_bwd", dep=None):
    S = xbc.shape[0]
    nc = S // CHUNK
    Lc = CHUNK

    def body(x_ref, b_ref, c_ref, dt_ref, ac_ref, act_ref, st_ref, dy_ref, d_ref, *rest):
        dx_ref, db_ref, dc_ref, dh_ref, ds_ref = rest[-5:]
        g2 = pl.program_id(0)
        cc = pl.program_id(1)

        @pl.when(cc == 0)
        def _():
            ds_ref[...] = jnp.zeros_like(ds_ref)

        causal = _iota((Lc, Lc), 0) >= _iota((Lc, Lc), 1)
        lane256 = _iota((Lc, 256), 1)
        lane128 = _iota((Lc, 128), 1)
        row128 = _iota((128, 128), 0)
        ind_rows = _iota((256, 128), 0) >> 6
        ind_cols = _iota((256, 128), 1)
        ind_a = (ind_rows == ind_cols).astype(BF16)
        ind_b = (ind_rows + 4 == ind_cols).astype(BF16)
        for gg in range(SSD_GPS):
            g = SSD_GPS * g2 + gg
            bv = b_ref[:, 128 * gg:128 * (gg + 1)]
            cv = c_ref[:, 128 * gg:128 * (gg + 1)]
            bbf, cbf = bv.astype(BF16), cv.astype(BF16)
            cb = _dot(cbf, bbf, 1, 1)
            dtg, acg = dt_ref[gg], ac_ref[gg]
            ac_last = ac_ref[gg, pl.ds(Lc - 1, 1), :]
            dt4 = _expand4(dtg, lane256)
            ac4 = _expand4(acg, lane256)
            acl4 = _expand4(ac_last, _iota((1, 256), 1))
            e4 = jnp.exp(ac4)
            dte4 = jnp.exp(acl4 - ac4)
            xv = x_ref[:, 256 * gg:256 * (gg + 1)]
            xd = xv * dt4
            xdb = xd.astype(BF16)
            dyv = dy_ref[:, 256 * gg:256 * (gg + 1)]
            dcb = jnp.zeros((Lc, Lc), F32)
            dc_acc = jnp.zeros((Lc, 128), F32)
            db_acc = jnp.zeros((Lc, 128), F32)
            u_parts, dxd_parts, ends = [], [], []
            for p in range(2):
                sl = slice(128 * p, 128 * (p + 1))
                xd_p, xdb_p, dy_p = xd[:, sl], xdb[:, sl], dyv[:, sl]
                dyb_p = dy_p.astype(BF16)
                e_p, dte_p = e4[:, sl], dte4[:, sl]
                sp = st_ref[gg, p]
                spb = sp.astype(BF16)
                dsn = ds_ref[gg, p]
                dsnb = dsn.astype(BF16)
                yds, dxds, cds = [], [], []
                for q in range(2):
                    hh = 2 * p + q
                    a_col = acg[:, hh:hh + 1]
                    a_row = act_ref[pl.ds(4 * g + hh, 1), :]
                    dec = jnp.exp(jnp.where(causal, a_col - a_row, NEG))
                    w = (cb * dec).astype(BF16)
                    head = (lane128 < 64) if q == 0 else (lane128 >= 64)
                    dym = jnp.where(head, dyb_p, jnp.zeros_like(dyb_p))
                    dw = _dot(dym, xdb_p, 1, 1)
                    dcb = dcb + dw * dec
                    yds.append(_dot(w, xdb_p))
                    dxds.append(_dot(w, dyb_p, 0, 0))
                    cds.append(jnp.exp(ac_last[:, hh:hh + 1]))
                y_diag = jnp.where(lane128 < 64, yds[0], yds[1])
                dxd_diag = jnp.where(lane128 < 64, dxds[0], dxds[1])
                y_off = _dot(cbf, spb, 1, 1) * e_p
                dgp = dy_p * e_p
                dgb = dgp.astype(BF16)
                dc_acc = dc_acc + _dot(dgb, spb)
                dsp = _dot(dgb, cbf, 0, 0)
                cd_col = jnp.where(row128[:, 0:1] < 64, cds[0], cds[1])
                qm = _dot(bbf, dsnb, 1, 1)
                dxd_state = dte_p * qm
                db_acc = db_acc + _dot((xd_p * dte_p).astype(BF16), dsnb)
                t_p = xd_p * dxd_state
                prod = dsn * sp
                e0 = jnp.sum(jnp.sum(jnp.where(row128 < 64, prod, 0.0), axis=1, keepdims=True), axis=0, keepdims=True)
                e1 = jnp.sum(jnp.sum(jnp.where(row128 >= 64, prod, 0.0), axis=1, keepdims=True), axis=0, keepdims=True)
                tcol = jnp.sum(t_p, axis=0, keepdims=True)
                lane1 = _iota((1, 128), 1)
                t0 = jnp.sum(jnp.where(lane1 < 64, tcol, 0.0), axis=1, keepdims=True)
                t1 = jnp.sum(jnp.where(lane1 >= 64, tcol, 0.0), axis=1, keepdims=True)
                ends.append(e0 * cds[0] + t0)
                ends.append(e1 * cds[1] + t1)
                ds_ref[gg, p] = dsn * cd_col + dsp
                u_parts.append(dyb_p.astype(F32) * y_diag - xdb_p.astype(F32) * dxd_diag + dy_p * y_off - t_p)
                dxd_parts.append(dxd_diag + dxd_state)
            dxd = jnp.concatenate(dxd_parts, axis=1)
            u_all = jnp.concatenate(u_parts, axis=1)
            dx_ref[:, 256 * gg:256 * (gg + 1)] = dxd * dt4 + dyv * d_ref[:, 256 * gg:256 * (gg + 1)]
            dcbb = dcb.astype(BF16)
            dc_ref[:, 128 * gg:128 * (gg + 1)] = dc_acc + _dot(dcbb, bbf)
            db_ref[:, 128 * gg:128 * (gg + 1)] = db_acc + _dot(dcbb, cbf, 0, 0)
            lane = _iota((Lc, 128), 1)
            endv = jnp.zeros((Lc, 128), F32)
            for hh in range(4):
                endv = jnp.where(lane == 8 + hh, ends[hh], endv)
            dh_ref[gg] = _dot3(dxd * xv, ind_a) + _dot3(u_all, ind_b) + endv

    G = SSD_GPS
    rev = lambda c: nc - 1 - c
    per_g = lambda g, c: (g, rev(c), 0)
    return pl.pallas_call(
        body, name=name, grid=(SSM_GROUPS // G, nc),
        in_specs=[pl.BlockSpec((Lc, 256 * G), lambda g, c: (rev(c), g)),
                  pl.BlockSpec((Lc, 128 * G), lambda g, c: (rev(c), 16 // G + g)),
                  pl.BlockSpec((Lc, 128 * G), lambda g, c: (rev(c), 24 // G + g)),
                  pl.BlockSpec((G, Lc, 128), per_g), pl.BlockSpec((G, Lc, 128), per_g),
                  pl.BlockSpec((SSM_HEADS, Lc), lambda g, c: (0, rev(c))),
                  pl.BlockSpec((G, None, 2, 128, 128), lambda g, c: (g, rev(c), 0, 0, 0)),
                  pl.BlockSpec((Lc, 256 * G), lambda g, c: (rev(c), g)),
                  pl.BlockSpec((1, 256 * G), lambda g, c: (0, g))] + ([] if dep is None else [pl.BlockSpec(memory_space=pl.ANY)]),
        out_specs=[pl.BlockSpec((Lc, 256 * G), lambda g, c: (rev(c), g)),
                   pl.BlockSpec((Lc, 128 * G), lambda g, c: (rev(c), g)),
                   pl.BlockSpec((Lc, 128 * G), lambda g, c: (rev(c), g)),
                   pl.BlockSpec((G, Lc, 128), per_g)],
        out_shape=[jax.ShapeDtypeStruct((S, 2048), F32), jax.ShapeDtypeStruct((S, 1024), F32),
                   jax.ShapeDtypeStruct((S, 1024), F32), jax.ShapeDtypeStruct((SSM_GROUPS, S, 128), F32)],
        scratch_shapes=[pltpu.VMEM((G, 2, 128, 128), F32)], compiler_params=_cp(2),
    )(xbc, xbc, xbc, dt_g, ac_g, ac_t, states, dy, dexp, *([] if dep is None else [dep]))


def _ssd_post(dhead, dt_g, sg_g, alog_g, *, name="ssd_post"):
    S = dhead.shape[1]
    nc = S // CHUNK
    Lc = CHUNK

    def body(dh_ref, dt_ref, sg_ref, al_ref, o_ref, s_ref):
        @pl.when(pl.program_id(0) == 0)
        def _():
            s_ref[...] = jnp.zeros_like(s_ref)

        lane = _iota((Lc, 128), 1)
        row = _iota((Lc, 128), 0)
        row8 = _iota((8, 128), 0)
        out = jnp.zeros((Lc, 128), F32)
        for g in range(SSM_GROUPS):
            dh = dh_ref[g]
            a_neg = -jnp.exp(al_ref[g])
            dac = jnp.where(lane < 4, pltpu.roll(dh, 124, axis=1), 0.0)
            end = jnp.where(lane < 4, pltpu.roll(dh, 120, axis=1), 0.0)
            k = 1
            while k < Lc:
                dac = dac + jnp.where(row < Lc - k, pltpu.roll(dac, Lc - k, axis=0), 0.0)
                k *= 2
            da = dac + end
            ddt = jnp.where(lane < 4, da * a_neg + dh, 0.0)
            ddtr = ddt * sg_ref[g]
            out = out + (ddtr if g == 0 else pltpu.roll(ddtr, 4 * g, axis=1))
            dal = jnp.sum(da * dt_ref[g], axis=0, keepdims=True) * a_neg
            dbias = jnp.sum(ddtr, axis=0, keepdims=True)
            part = jnp.where(row8 == 0, dal, jnp.where(row8 == 1, dbias, 0.0))
            s_ref[g] += part
        o_ref[...] = out.astype(BF16)

    grp = pl.BlockSpec((SSM_GROUPS, Lc, 128), lambda c: (0, c, 0))
    whole = lambda r: pl.BlockSpec((SSM_GROUPS, r, 128), lambda c: (0, 0, 0))
    return pl.pallas_call(
        body, name=name, grid=(nc,), in_specs=[grp, grp, grp, whole(1)],
        out_specs=[pl.BlockSpec((Lc, 128), lambda c: (c, 0)), whole(8)],
        out_shape=[jax.ShapeDtypeStruct((S, 128), BF16), jax.ShapeDtypeStruct((SSM_GROUPS, 8, 128), F32)],
        compiler_params=_cp(1),
    )(dhead, dt_g, sg_g, alog_g)


def _gate_fwd(y, xbc, zx, dexp, gn, *, name="gate_fwd", tr=256, dep=None):
    S = y.shape[0]
    W = 2048
    gw = W // SSM_GROUPS

    def body(y_ref, x_ref, z_ref, d_ref, g_ref, *rest):
        o_ref = rest[-1]
        z = z_ref[...]
        u = (y_ref[...] + x_ref[...] * d_ref[...]) * (z * _sigmoid(z))
        gv = g_ref[...]
        for q in range(SSM_GROUPS):
            sl = slice(gw * q, gw * (q + 1))
            uq = u[:, sl]
            r = lax.rsqrt(jnp.mean(uq * uq, axis=-1, keepdims=True) + EPS)
            o_ref[:, sl] = (uq * r * gv[:, sl]).astype(BF16)

    row = pl.BlockSpec((tr, W), lambda i: (i, 0))
    vec = pl.BlockSpec((1, W), lambda i: (0, 0))
    return pl.pallas_call(
        body, name=name, grid=(S // tr,),
        in_specs=[row, row, row, vec, vec] + ([] if dep is None else [pl.BlockSpec(memory_space=pl.ANY)]), out_specs=row,
        out_shape=jax.ShapeDtypeStruct((S, W), BF16), compiler_params=_cp(1),
    )(y, xbc, zx, dexp, gn, *([] if dep is None else [dep]))


def _gate_bwd(y, xbc, zx, dexp, gn, dout, *, name="gate_bwd", tr=256):
    S = y.shape[0]
    W = 2048
    gw = W // SSM_GROUPS
    steps = S // tr

    def body(y_ref, x_ref, z_ref, d_ref, g_ref, do_ref, dy_ref, dz_ref, dg_ref, dd_ref, acc_ref):
        i = pl.program_id(0)

        @pl.when(i == 0)
        def _():
            acc_ref[...] = jnp.zeros_like(acc_ref)

        z = z_ref[...]
        sg = _sigmoid(z)
        sz = z * sg
        xs = x_ref[...]
        yt = y_ref[...] + xs * d_ref[...]
        u = yt * sz
        gv = g_ref[...]
        do = do_ref[...]
        dgs = []
        for q in range(SSM_GROUPS):
            sl = slice(gw * q, gw * (q + 1))
            uq = u[:, sl]
            r = lax.rsqrt(jnp.mean(uq * uq, axis=-1, keepdims=True) + EPS)
            uh = uq * r
            dq = do[:, sl]
            duh = dq * gv[:, sl]
            duq = r * (duh - uh * jnp.mean(duh * uh, axis=-1, keepdims=True))
            dgs.append(jnp.sum(dq * uh, axis=0, keepdims=True))
            dyt = duq * sz[:, sl]
            dy_ref[:, sl] = dyt
            dz_ref[:, sl] = (duq * yt[:, sl] * (sg[:, sl] * (1.0 + z[:, sl] * (1.0 - sg[:, sl])))).astype(BF16)
            acc_ref[:, sl] += jnp.sum(dyt * xs[:, sl], axis=0, keepdims=True)
        dg = jnp.concatenate(dgs, axis=1)

        @pl.when(i == 0)
        def _():
            dg_ref[...] = dg

        @pl.when(i > 0)
        def _():
            dg_ref[...] += dg

        @pl.when(i == steps - 1)
        def _():
            ind = ((_iota((W, 128), 0) >> 6) == _iota((W, 128), 1)).astype(BF16)
            dd_ref[...] = _dot3(jnp.broadcast_to(acc_ref[...], (8, W)), ind)[0:1, :]

    row = pl.BlockSpec((tr, W), lambda i: (i, 0))
    vec = pl.BlockSpec((1, W), lambda i: (0, 0))
    return pl.pallas_call(
        body, name=name, grid=(steps,), in_specs=[row, row, row, vec, vec, row],
        out_specs=[row, row, vec, pl.BlockSpec((1, 128), lambda i: (0, 0))],
        out_shape=[jax.ShapeDtypeStruct((S, W), F32), jax.ShapeDtypeStruct((S, W), BF16),
                   jax.ShapeDtypeStruct((1, W), F32), jax.ShapeDtypeStruct((1, 128), F32)],
        scratch_shapes=[pltpu.VMEM((1, W), F32)], compiler_params=_cp(1),
    )(y, xbc, zx, dexp, gn, dout)


def _rope_cs(posf, *, name="rope_tables", tr=256):
    S = posf.shape[0]

    def body(p_ref, c_ref, s_ref):
        j = (_iota((tr, 128), 1) & 31).astype(F32)
        ang = p_ref[...] * jnp.exp(j * (-math.log(ROPE_THETA) / 32.0))
        c_ref[...] = jnp.cos(ang)
        s_ref[...] = jnp.sin(ang)

    blk = pl.BlockSpec((tr, 128), lambda i: (i, 0))
    return pl.pallas_call(
        body, name=name, grid=(S // tr,), in_specs=[pl.BlockSpec((tr, 1), lambda i: (i, 0))], out_specs=[blk, blk],
        out_shape=[jax.ShapeDtypeStruct((S, 128), F32)] * 2, compiler_params=_cp(1),
    )(posf)


def _rope_tables(c_ref, s_ref, shape):
    reps = shape[1] // 128
    return jnp.tile(c_ref[...], (1, reps)), jnp.tile(s_ref[...], (1, reps)), (_iota(shape, 1) & 63) < 32


def _hn_inds(W):
    ind = ((_iota((W, 128), 0) >> 6) == _iota((W, 128), 1)).astype(BF16)
    ind_t = ((_iota((128, W), 1) >> 6) == _iota((128, W), 0)).astype(BF16)
    return ind, ind_t


def _hnrope_fwd(xin, col0, W, gain_w, rope, *, name, tr=256):
    S = xin.shape[0]
    off = col0 // W
    nh = W // HEAD

    def body(x_ref, g_ref, c_ref, s_ref, o_ref):
        x = x_ref[...]
        ind, ind_t = _hn_inds(W)
        r = lax.rsqrt(_dot3(x * x, ind) * (1.0 / HEAD) + EPS)
        xn = x * _dot3(r, ind_t) * g_ref[...]
        cs, sn, half = _rope_tables(c_ref, s_ref, (tr, W))
        rot = jnp.where(half, -pltpu.roll(xn, W - 32, axis=1), pltpu.roll(xn, 32, axis=1))
        out = (xn * cs + rot * sn).astype(BF16)
        for h in range(nh):
            o_ref[h] = out[:, HEAD * h:HEAD * (h + 1)]

    tab = pl.BlockSpec((tr, 128), lambda i: (i, 0))
    return pl.pallas_call(
        body, name=name, grid=(S // tr,),
        in_specs=[pl.BlockSpec((tr, W), lambda i: (i, off)), pl.BlockSpec((1, W), lambda i: (0, 0)), tab, tab],
        out_specs=pl.BlockSpec((nh, tr, HEAD), lambda i: (0, i, 0)), out_shape=jax.ShapeDtypeStruct((nh, S, HEAD), BF16),
        compiler_params=_cp(1),
    )(xin, gain_w, *rope)


def _hnrope_bwd(xin, col0, W, gain_w, rope, dout, *, name, tr=256):
    S = xin.shape[0]
    off = col0 // W
    steps = S // tr
    nh = W // HEAD

    def body(x_ref, g_ref, c_ref, s_ref, do_ref, dx_ref, cs_ref, dg_ref, acc_ref):
        i = pl.program_id(0)
        x = x_ref[...]
        ind, ind_t = _hn_inds(W)
        r = lax.rsqrt(_dot3(x * x, ind) * (1.0 / HEAD) + EPS)
        rw = _dot3(r, ind_t)
        xh = x * rw
        cs, sn, half = _rope_tables(c_ref, s_ref, (tr, W))
        do = jnp.concatenate([do_ref[h] for h in range(nh)], axis=1).astype(F32)
        gs = do * sn
        g1 = do * cs + jnp.where(half, pltpu.roll(gs, W - 32, axis=1), -pltpu.roll(gs, 32, axis=1))
        dxh = g1 * g_ref[...]
        t = _dot3(dxh * xh, ind) * (1.0 / HEAD)
        dx = rw * (dxh - xh * _dot3(t, ind_t))
        dx_ref[...] = dx.astype(BF16)
        cpart = jnp.sum(dx, axis=0, keepdims=True)
        gpart = jnp.sum(g1 * xh, axis=0, keepdims=True)

        @pl.when(i == 0)
        def _():
            cs_ref[...] = cpart
            acc_ref[...] = gpart

        @pl.when(i > 0)
        def _():
            cs_ref[...] += cpart
            acc_ref[...] += gpart

        @pl.when(i == steps - 1)
        def _():
            fold = ((_iota((W, 128), 0) & 63) == _iota((W, 128), 1)).astype(BF16)
            dg_ref[...] = _dot3(jnp.broadcast_to(acc_ref[...], (8, W)), fold)[0:1, :]

    tab = pl.BlockSpec((tr, 128), lambda i: (i, 0))
    return pl.pallas_call(
        body, name=name, grid=(steps,),
        in_specs=[pl.BlockSpec((tr, W), lambda i: (i, off)), pl.BlockSpec((1, W), lambda i: (0, 0)), tab, tab,
                  pl.BlockSpec((nh, tr, HEAD), lambda i: (0, i, 0))],
        out_specs=[pl.BlockSpec((tr, W), lambda i: (i, 0)), pl.BlockSpec((1, W), lambda i: (0, 0)),
                   pl.BlockSpec((1, 128), lambda i: (0, 0))],
        out_shape=[jax.ShapeDtypeStruct((S, W), BF16), jax.ShapeDtypeStruct((1, W), F32),
                   jax.ShapeDtypeStruct((1, 128), F32)],
        scratch_shapes=[pltpu.VMEM((1, W), F32)], compiler_params=_cp(1),
    )(xin, gain_w, *rope, dout)


def _attn_band():
    qi = jnp.arange(ATT_G * WINDOW)[:, None] % WINDOW
    ki = jnp.arange(2 * WINDOW)[None, :]
    rel = qi + WINDOW - ki
    ok = (rel >= 0) & (rel < WINDOW)
    return jnp.stack([jnp.where(ok & (ki >= WINDOW), 0.0, NEG), jnp.where(ok, 0.0, NEG)]).astype(F32)


def _attn_probs(q, kb, sink_ref, band_ref, h, i):
    s = _dot(q, kb, 1, 1) * (HEAD ** -0.5) + band_ref[jnp.minimum(i, 1)]
    r1 = _iota((4 * WINDOW, 1), 0)
    sink = jnp.where(r1 < WINDOW, sink_ref[4 * h], jnp.where(r1 < 2 * WINDOW, sink_ref[4 * h + 1],
                     jnp.where(r1 < 3 * WINDOW, sink_ref[4 * h + 2], sink_ref[4 * h + 3])))
    m = jnp.maximum(jnp.max(s, axis=1, keepdims=True), sink)
    p = jnp.exp(s - m)
    ps = jnp.exp(sink - m)
    inv = 1.0 / (jnp.sum(p, axis=1, keepdims=True) + ps)
    return p * inv, ps * inv


ATT_HPS = 4
_BAND = pl.BlockSpec((2, ATT_G * WINDOW, 2 * WINDOW), lambda h, i: (0, 0, 0))


def _attn_specs(S):
    qspec = pl.BlockSpec((ATT_HPS, ATT_G, WINDOW, HEAD), lambda h, i: (h, 0, i, 0))
    cur = pl.BlockSpec((ATT_HPS, WINDOW, HEAD), lambda h, i: (h, i, 0))
    prev = pl.BlockSpec((ATT_HPS, WINDOW, HEAD), lambda h, i: (h, jnp.maximum(i - 1, 0), 0))
    tok = pl.BlockSpec((WINDOW, ATT_HPS * ATT_G * HEAD), lambda h, i: (i, h))
    return qspec, cur, prev, tok


def _attn_fwd(qh, kh, vh, sinks, *, name="attn_fwd"):
    S = kh.shape[1]
    nb = S // WINDOW

    def body(s_ref, band_ref, q_ref, kc_ref, kp_ref, vc_ref, vp_ref, o_ref):
        h2, i = pl.program_id(0), pl.program_id(1)
        outs = []
        for hh in range(ATT_HPS):
            q = q_ref[hh].reshape(ATT_G * WINDOW, HEAD)
            kb = jnp.concatenate([kp_ref[hh], kc_ref[hh]], axis=0)
            vb = jnp.concatenate([vp_ref[hh], vc_ref[hh]], axis=0)
            probs, _ = _attn_probs(q, kb, s_ref, band_ref, ATT_HPS * h2 + hh, i)
            o = _dot(probs.astype(BF16), vb).astype(BF16)
            outs += [o[WINDOW * g:WINDOW * (g + 1)] for g in range(ATT_G)]
        o_ref[...] = jnp.concatenate(outs, axis=1)

    qspec, cur, prev, tok = _attn_specs(S)
    return pl.pallas_call(
        body, name=name, grid=(ATT_KV // ATT_HPS, nb),
        in_specs=[pl.BlockSpec(memory_space=pltpu.SMEM), _BAND, qspec, cur, prev, cur, prev], out_specs=tok,
        out_shape=jax.ShapeDtypeStruct((S, ATT_KV * ATT_G * HEAD), BF16), compiler_params=_cp(2),
    )(sinks, _attn_band(), qh, kh, kh, vh, vh)


def _attn_bwd(qh, kh, vh, sinks, doh, *, name="attn_bwd"):
    S = kh.shape[1]
    nb = S // WINDOW

    def body(s_ref, band_ref, q_ref, kc_ref, kp_ref, vc_ref, vp_ref, do_ref, dq_ref, dk_ref, dv_ref, dsk_ref):
        h2, i = pl.program_id(0), pl.program_id(1)

        @pl.when(i == 0)
        def _():
            dk_ref[...] = jnp.zeros_like(dk_ref)
            dv_ref[...] = jnp.zeros_like(dv_ref)
            dsk_ref[...] = jnp.zeros_like(dsk_ref)

        dov = do_ref[...]
        cur = pl.multiple_of(i * WINDOW, WINDOW)
        lane = _iota((8, 128), 1)
        row = _iota((8, 128), 0)
        scale = HEAD ** -0.5
        for hh in range(ATT_HPS):
            q = q_ref[hh].reshape(ATT_G * WINDOW, HEAD)
            do = jnp.concatenate([dov[:, HEAD * (ATT_G * hh + g):HEAD * (ATT_G * hh + g + 1)] for g in range(ATT_G)], axis=0)
            kb = jnp.concatenate([kp_ref[hh], kc_ref[hh]], axis=0)
            vb = jnp.concatenate([vp_ref[hh], vc_ref[hh]], axis=0)
            probs, psink = _attn_probs(q, kb, s_ref, band_ref, ATT_HPS * h2 + hh, i)
            dp = _dot(do, vb, 1, 1)
            delta = jnp.sum(probs * dp, axis=1, keepdims=True)
            ds = (probs * (dp - delta)).astype(BF16)
            dq_ref[hh] = (_dot(ds, kb) * scale).reshape(ATT_G, WINDOW, HEAD)
            dkb = _dot(ds, q, 0, 0) * scale
            dvb = _dot(probs.astype(BF16), do, 0, 0)
            dk_ref[hh, pl.ds(cur, WINDOW), :] += dkb[WINDOW:, :]
            dv_ref[hh, pl.ds(cur, WINDOW), :] += dvb[WINDOW:, :]
            prv = pl.multiple_of(jnp.maximum(i - 1, 0) * WINDOW, WINDOW)
            dk_ref[hh, pl.ds(prv, WINDOW), :] += dkb[:WINDOW, :]
            dv_ref[hh, pl.ds(prv, WINDOW), :] += dvb[:WINDOW, :]

            dsr = -psink * delta
            upd = jnp.zeros((8, 128), F32)
            for gq in range(ATT_G):
                v = jnp.sum(dsr[gq * WINDOW:(gq + 1) * WINDOW, :], axis=0, keepdims=True)
                upd = jnp.where((lane == gq) & (row == 0), v, upd)
            dsk_ref[hh] += upd

    qspec, cur, prev, tok = _attn_specs(S)
    full = pl.BlockSpec((ATT_HPS, S, HEAD), lambda h, i: (h, 0, 0))
    return pl.pallas_call(
        body, name=name, grid=(ATT_KV // ATT_HPS, nb),
        in_specs=[pl.BlockSpec(memory_space=pltpu.SMEM), _BAND, qspec, cur, prev, cur, prev, tok],
        out_specs=[qspec, full, full, pl.BlockSpec((ATT_HPS, 8, 128), lambda h, i: (h, 0, 0))],
        out_shape=[jax.ShapeDtypeStruct((ATT_KV, ATT_G, S, HEAD), F32), jax.ShapeDtypeStruct((ATT_KV, S, HEAD), F32),
                   jax.ShapeDtypeStruct((ATT_KV, S, HEAD), F32), jax.ShapeDtypeStruct((ATT_KV, 8, 128), F32)],
        compiler_params=_cp(2),
    )(sinks, _attn_band(), qh, kh, kh, vh, vh, doh)


def _heads_major(t, nh):
    S = t.shape[0]
    return t.reshape(S, nh, HEAD).transpose(1, 0, 2)


def _tokens_major(t):
    nh, S, _ = t.shape
    return t.transpose(1, 0, 2).reshape(S, nh * HEAD)


class _NoComm:
    def late_start(self, part, after):
        return None

    def late_arrived(self, part, after):
        return None

    def late_weights(self, w, after, part):
        return w

    def advance(self, after, group=None, tensors=None):
        return None


def _local_step(x, posf, target, w, comm=None):
    S, D = x.shape
    gr = {}
    comm = comm or _NoComm()

    zx, h1 = _norm_mm(x, w["a_norm"], w["w_zx"], name="in_proj_zx", dep=w.get("dep"))
    dtr = _mm(h1, w["w_dt"], name="in_proj_dt")
    xbc = _conv_silu_fwd(zx, 2048, 4096, w["a_conv_w"], w["a_conv_b"], name="a_conv_f")
    dt_g, ac_g, sg_g, ac_t = _ssd_prep(dtr, w["a_dt_bias"], w["a_A_log"])
    y_ssd, states = _ssd_fwd(xbc, dt_g, ac_g, ac_t, dep=comm.late_start(1, xbc))
    yg = _gate_fwd(y_ssd, xbc, zx, w["a_Dexp"], w["a_gnorm"], dep=comm.late_arrived(0, [y_ssd]))
    w = comm.late_weights(w, yg, 0)
    x1 = _mm(yg, w["a_out_proj"], res=x, name="out_proj")

    FW = w["f_w_in"][0].shape[2]

    def ffn_fwd(xin, l, loss_target=None):
        u, h = _norm_mm(xin, w["f_norm"][l], w["f_w_in"][l], name=f"f_in{l}", N=N_CHIPS * FW, tn=FW,
                        b_spec=pl.BlockSpec((None, D, FW), lambda i, j: (j, 0, 0)))
        a = _ffn_act_fwd(u, w["f_conv_w"][l], w["f_conv_b"][l], name=f"f_act_f{l}")
        dep = comm.late_arrived(1, [u]) if l == 0 else None
        xo = _mm(a, w["f_w_down"][l], res=xin, tk=a.shape[1], name=f"f_down{l}", target=loss_target, dep=dep)
        return xo, (h, u)

    x2, ffn0 = ffn_fwd(x1, 0)
    w = comm.late_weights(w, x2, 1)

    kv, hk = _norm_mm(x2, w["kv_norm"], w["w_kv"], bias=w["b_kv"], name="kv_proj")
    q, hq = _norm_mm(x2, w["b_norm"], w["w_q"], bias=w["b_q"], name="q_proj")
    rope = _rope_cs(posf)
    kr = _hnrope_fwd(kv, 0, 256, w["k_norm_w"], rope, name="k_rope_f")
    qr = _hnrope_fwd(q, 0, 1024, w["q_norm_w"], rope, name="q_rope_f")
    qh = qr.reshape(ATT_KV, ATT_G, S, HEAD)
    kh = kr
    vh = _heads_major(kv[:, 256:].astype(BF16), ATT_KV)
    att = _attn_fwd(qh, kh, vh, w["sinks"])
    x3 = _mm(att, w["w_o"], bias=w["b_o"], res=x2, name="o_proj")
    (dy, loss_part), ffn1 = ffn_fwd(x3, 1, target)

    def ffn_bwd(xin, l, saved, dyo, want_colsum, dep=None):
        h, u = saved
        da = _mm(dyo, w["f_w_down"][l], tb=True, name=f"f_down_dx{l}", dep=dep)
        du, dcw, dcb, a = _ffn_act_bwd(u, w["f_conv_w"][l], w["f_conv_b"][l], da, name=f"f_act_b{l}")
        dw_down = _mm(a, dyo, ta=True, out_dtype=BF16, name=f"f_down_dw{l}")
        dw_in = _mm(h, du, ta=True, out_dtype=BF16, name=f"f_in_dw{l}", dims=(D, N_CHIPS * FW, S), tm=D, tn=FW, tk=S,
                    b_spec=pl.BlockSpec((None, S, FW), lambda i, j, k: (j // 2, 0, j % 2)),
                    o_spec=pl.BlockSpec((None, D, FW), lambda i, j, k: (j, i, 0)), o_shape=(N_CHIPS, D, FW))
        ts = _pick(S, (512, 256))
        pieces = [(pl.BlockSpec((None, ts, FW), lambda i, j, k, q=q: (q // 2, i, q % 2)),
                   pl.BlockSpec((None, D, FW), lambda i, j, k, q=q: (q, 0, 0), pipeline_mode=pl.Buffered(1)))
                  for q in range(N_CHIPS)]
        outs = _mm(du, w["f_w_in"][l], tb=True, name=f"f_in_dx{l}", dims=(S, D, FW), tm=ts, tn=D, tk=FW,
                   a_spec=pieces[0][0], b_spec=pieces[0][1], more=pieces[1:],
                   rms=(xin, w["f_norm"][l], dyo), rms_colsum=want_colsum)
        g = dict(f_norm=outs[1], f_w_in=dw_in, f_conv_w=dcw, f_conv_b=dcb, f_w_down=dw_down)
        return outs[0], g, (outs[2] if want_colsum else None)

    dx3, gr["ffn1"], db_o = ffn_bwd(x3, 1, ffn1, dy, True)
    gr["b_o"] = db_o
    gr["w_o"] = _mm(att, dx3, ta=True, out_dtype=BF16, name="o_proj_dw")
    datt = _mm(dx3, w["w_o"], tb=True, out_dtype=BF16, name="o_proj_dx")
    dqh, dkh, dvh, dsk = _attn_bwd(qh, kh, vh, w["sinks"], datt)
    gr["sinks"] = dsk[:, 0, :4].reshape(1, 16)
    dv = _tokens_major(dvh).astype(BF16)
    dq, db_q, dqn = _hnrope_bwd(q, 0, 1024, w["q_norm_w"], rope, dqh.reshape(16, S, HEAD), name="q_rope_b")
    dk, db_k, dkn = _hnrope_bwd(kv, 0, 256, w["k_norm_w"], rope, dkh, name="k_rope_b")
    gr["q_norm"], gr["k_norm"] = dqn[:, :HEAD], dkn[:, :HEAD]
    gr["b_q"] = db_q
    gr["b_kv"] = jnp.concatenate([db_k, _colsum(dv, name="dv_colsum")], axis=1)
    dkv = jnp.concatenate([dk, dv], axis=1)
    gr["w_q"] = _mm(hq, dq, ta=True, out_dtype=BF16, name="q_proj_dw")
    gr["w_kv"] = _mm(hk, dkv, ta=True, out_dtype=BF16, name="kv_proj_dw")
    tok = comm.advance([gr["w_kv"]], 1, dict(f_down1=gr["ffn1"]["f_w_down"], f_in1=gr["ffn1"]["f_w_in"], w_o=gr["w_o"],
                                             w_q=gr["w_q"], w_kv=gr["w_kv"]))
    tsm = _pick(S, (512, 256))
    dx2, gr["b_norm"] = _mm(dq, w["w_q"], tb=True, name="q_proj_dx", dep=tok, tm=tsm, tn=D, rms=(x2, w["b_norm"], dx3))
    dx2, gr["kv_norm"] = _mm(dkv, w["w_kv"], tb=True, name="kv_proj_dx", tm=tsm, tn=D, rms=(x2, w["kv_norm"], dx2))

    dx1, gr["ffn0"], _ = ffn_bwd(x1, 0, ffn0, dx2, False, dep=comm.advance([dx2]))

    gr["a_out_proj"] = _mm(yg, dx1, ta=True, out_dtype=BF16, name="out_proj_dw")
    tok = comm.advance([dx1, gr["a_out_proj"]], 2,
                       dict(f_down0=gr["ffn0"]["f_w_down"], f_in0=gr["ffn0"]["f_w_in"], out_proj=gr["a_out_proj"]))
    dyg = _mm(dx1, w["a_out_proj"], tb=True, name="out_proj_dx", dep=tok)
    dy_ssd, dz, gr["a_gnorm"], dD = _gate_bwd(y_ssd, xbc, zx, w["a_Dexp"], w["a_gnorm"], dyg)
    gr["a_D"] = dD[:, :SSM_HEADS]
    dxs, dB, dC, dhead = _ssd_bwd(xbc, dt_g, ac_g, ac_t, states, dy_ssd, w["a_Dexp"], dep=comm.advance([dy_ssd]))
    ddtr, dsmall = _ssd_post(dhead, dt_g, sg_g, w["a_A_log_g"])
    gr["a_A_log"] = dsmall[:, 0, :4].reshape(1, SSM_HEADS)
    gr["a_dt_bias"] = dsmall[:, 1, :4].reshape(1, SSM_HEADS)
    dxbc, gr["a_conv_w"], gr["a_conv_b"] = _conv_silu_bwd(
        zx, 2048, 4096, w["a_conv_w"], w["a_conv_b"], [(dxs, 0), (dB, 2048), (dC, 3072)], name="a_conv_b")
    gr["in_proj"] = _in_proj_dw(h1, dz, dxbc, ddtr).T
    ts = _pick(S, (512, 256))
    once = pl.Buffered(1)
    wblk = lambda q: pl.BlockSpec((D, 2048), lambda i, j, k: (0, q), pipeline_mode=once)
    dx0, gr["a_norm"] = _mm(
        dz, w["w_zx"], tb=True, name="in_proj_dx", dims=(S, D, 2048), tm=ts, tn=D, tk=2048,
        a_spec=pl.BlockSpec((ts, 2048), lambda i, j, k: (i, 0)), b_spec=wblk(0),
        more=[(dxbc, pl.BlockSpec((ts, 2048), lambda i, j, k: (i, 0)), w["w_zx"], wblk(1)),
              (dxbc, pl.BlockSpec((ts, 2048), lambda i, j, k: (i, 1)), w["w_zx"], wblk(2)),
              (ddtr, pl.BlockSpec((ts, 128), lambda i, j, k: (i, 0)), w["w_dt"],
               pl.BlockSpec((D, 128), lambda i, j, k: (0, 0), pipeline_mode=once))],
        rms=(x, w["a_norm"], dx1))
    tok = comm.advance([dx0], 3, dict(in_proj=gr["in_proj"].reshape(D, N_CHIPS, -1).transpose(1, 0, 2)))
    return loss_part, dx0, gr, tok


def _prep_small(full, w):
    w["a_norm"] = full["a_norm"]
    w["a_conv_w"] = full["a_conv_w"][0]
    w["a_conv_b"] = full["a_conv_b"]
    pad32 = lambda v: jnp.pad(v, ((0, 0), (0, 128 - SSM_HEADS)))
    w["a_dt_bias"] = pad32(full["a_dt_bias"])
    w["a_A_log"] = pad32(full["a_A_log"])
    w["a_A_log_g"] = jnp.pad(full["a_A_log"].reshape(SSM_GROUPS, 1, 4), ((0, 0), (0, 0), (0, 124)))
    w["a_Dexp"] = jnp.repeat(full["a_D"], HEAD, axis=1)
    w["a_gnorm"] = full["a_gnorm"]
    w["f_norm"] = [full["f_norm"][l:l + 1] for l in range(2)]
    w["f_conv_w"] = [full["f_conv_w"][l] for l in range(2)]
    w["f_conv_b"] = [full["f_conv_b"][l:l + 1] for l in range(2)]
    w["kv_norm"] = full["kv_norm"].reshape(1, -1)
    w["b_kv"] = full["b_kv"].reshape(1, -1)
    w["k_norm_w"] = jnp.tile(full["k_norm"].reshape(1, HEAD), (1, ATT_KV))
    w["b_norm"] = full["b_norm"]
    w["b_q"] = full["b_q"]
    w["q_norm_w"] = jnp.tile(full["q_norm"], (1, ATT_KV * ATT_G))
    w["sinks"] = full["sinks"].reshape(-1)
    w["b_o"] = full["b_o"]
    return w


def _split_in_proj(ip):
    return ip[:, :6144].astype(BF16), jnp.pad(ip[:, 6144:], ((0, 0), (0, 128 - SSM_HEADS))).astype(BF16)


def _join_in_proj(blocks, *, name="in_proj_join", tr=256):
    _, R, cw = blocks.shape
    zx_cols = 3 * 2048
    rest = N_CHIPS * cw - zx_cols

    def body(b_ref, zx_ref, dt_ref):
        whole = jnp.concatenate([b_ref[j] for j in range(N_CHIPS)], axis=1)
        zx_ref[...] = whole[:, :zx_cols]
        dt_ref[...] = jnp.concatenate([whole[:, zx_cols:], jnp.zeros((tr, 128 - rest), BF16)], axis=1)

    return pl.pallas_call(
        body, name=name, grid=(R // tr,), in_specs=[pl.BlockSpec((N_CHIPS, tr, cw), lambda i: (0, i, 0))],
        out_specs=[pl.BlockSpec((tr, zx_cols), lambda i: (i, 0)), pl.BlockSpec((tr, 128), lambda i: (i, 0))],
        out_shape=[jax.ShapeDtypeStruct((R, zx_cols), BF16), jax.ShapeDtypeStruct((R, 128), BF16)],
        compiler_params=_cp(1),
    )(blocks)


def _in_proj_dw(h, dz, dxbc, ddtr, *, name="in_proj_dw", tn=512):
    S, D = h.shape
    nz, nx = dz.shape[1] // tn, dxbc.shape[1] // tn
    N = dz.shape[1] + dxbc.shape[1] + SSM_HEADS

    def body(h_ref, z_ref, x_ref, t_ref, o_ref):
        j = pl.program_id(0)
        hb = h_ref[...].astype(BF16)

        @pl.when(j < nz)
        def _():
            o_ref[...] = _dot(z_ref[...].astype(BF16), hb, 0, 0).astype(BF16)

        @pl.when((j >= nz) & (j < nz + nx))
        def _():
            o_ref[...] = _dot(x_ref[...].astype(BF16), hb, 0, 0).astype(BF16)

        @pl.when(j == nz + nx)
        def _():
            o_ref[:128, :] = _dot(t_ref[...].astype(BF16), hb, 0, 0).astype(BF16)
            o_ref[128:, :] = jnp.zeros((tn - 128, D), BF16)

    return pl.pallas_call(
        body, name=name, grid=(nz + nx + 1,),
        in_specs=[pl.BlockSpec((S, D), lambda j: (0, 0), pipeline_mode=pl.Buffered(1)),
                  pl.BlockSpec((S, tn), lambda j: (0, jnp.minimum(j, nz - 1))),
                  pl.BlockSpec((S, tn), lambda j: (0, jnp.clip(j - nz, 0, nx - 1))),
                  pl.BlockSpec((S, 128), lambda j: (0, 0))],
        out_specs=pl.BlockSpec((tn, D), lambda j: (j, 0)),
        out_shape=jax.ShapeDtypeStruct((N, D), BF16), compiler_params=_cp(1),
    )(h, dz, dxbc, ddtr)


def _prep_weights(full):
    w = _prep_small(full, {})
    w["w_zx"], w["w_dt"] = _split_in_proj(full["a_in_proj"][0])
    w["a_out_proj"] = full["a_out_proj"][0].astype(BF16)
    w["f_w_in"] = [full["f_w_in"][l].reshape(1024, N_CHIPS, -1).transpose(1, 0, 2).astype(BF16) for l in range(2)]
    w["f_w_down"] = [full["f_w_down"][l].astype(BF16) for l in range(2)]
    w["w_kv"] = full["w_kv"].astype(BF16)
    w["w_q"] = full["w_q"][0].astype(BF16)
    w["w_o"] = full["w_o"][0].astype(BF16)
    return w


def _small_grads(gr):
    g = {}
    g["a_norm"] = gr["a_norm"]
    g["a_conv_w"] = gr["a_conv_w"][None]
    g["a_conv_b"] = gr["a_conv_b"]
    g["a_dt_bias"], g["a_A_log"], g["a_D"] = gr["a_dt_bias"], gr["a_A_log"], gr["a_D"]
    g["a_gnorm"] = gr["a_gnorm"]
    g["kv_norm"] = gr["kv_norm"].reshape(-1)
    g["b_kv"] = gr["b_kv"].reshape(-1)
    g["k_norm"] = gr["k_norm"].reshape(-1)
    g["b_norm"] = gr["b_norm"]
    g["b_q"] = gr["b_q"]
    g["q_norm"] = gr["q_norm"]
    g["sinks"] = gr["sinks"]
    g["b_o"] = gr["b_o"]
    f = [gr["ffn0"], gr["ffn1"]]
    g["f_norm"] = jnp.concatenate([f[0]["f_norm"], f[1]["f_norm"]], axis=0)
    g["f_conv_w"] = jnp.stack([f[l]["f_conv_w"] for l in range(2)])
    g["f_conv_b"] = jnp.concatenate([f[l]["f_conv_b"] for l in range(2)], axis=0)
    return g


def _full_grads(gr):
    g = _small_grads(gr)
    f32 = lambda t: t.astype(F32)
    g["a_in_proj"] = f32(gr["in_proj"])[None]
    g["a_out_proj"] = f32(gr["a_out_proj"])[None]
    g["w_kv"] = f32(gr["w_kv"])
    g["w_q"] = f32(gr["w_q"])[None]
    g["w_o"] = f32(gr["w_o"])[None]
    f = [gr["ffn0"], gr["ffn1"]]
    g["f_w_in"] = jnp.stack([f32(f[l]["f_w_in"]).transpose(1, 0, 2).reshape(1024, -1) for l in range(2)])
    g["f_w_down"] = jnp.stack([f32(f[l]["f_w_down"]) for l in range(2)])
    return g


MESH = pl.DeviceIdType.MESH
WEIGHTS = ("a_norm", "a_in_proj", "a_conv_w", "a_conv_b", "a_dt_bias", "a_A_log", "a_D", "a_gnorm", "a_out_proj",
           "kv_norm", "w_kv", "b_kv", "k_norm", "b_norm", "w_q", "b_q", "q_norm", "sinks", "w_o", "b_o", "f_norm",
           "f_w_in", "f_conv_w", "f_conv_b", "f_w_down")
MATS = (("in_proj", "a_in_proj", 0), ("out_proj", "a_out_proj", 0), ("w_kv", "w_kv", None), ("w_q", "w_q", 0),
        ("w_o", "w_o", 0), ("f_in0", "f_w_in", 0), ("f_in1", "f_w_in", 1), ("f_down0", "f_w_down", 0),
        ("f_down1", "f_w_down", 1))
SMALL_CUT = (("a_norm", 1), ("a_conv_w", 2), ("a_conv_b", 1), ("a_gnorm", 1), ("f_conv_w", 2))
SMALL_REP = ("a_dt_bias", "a_A_log", "a_D", "kv_norm", "b_kv", "k_norm", "b_norm", "b_q", "q_norm", "sinks", "b_o",
             "f_norm", "f_conv_b")


def _coords():
    return lax.axis_index("x"), lax.axis_index("y"), lax.axis_index("c")


def _other_chips(x, y):
    return [(1 - x, y), (x, 1 - y), (1 - x, 1 - y)]


def _pack(arrs, rows_align, lanes, dtype):
    flat = jnp.concatenate([a.reshape(-1).astype(dtype) for a in arrs])
    per = rows_align * lanes
    total = -(-flat.shape[0] // per) * per
    return jnp.pad(flat, (0, total - flat.shape[0])).reshape(total // lanes, lanes)


def _unpack(flat, shapes):
    out, off = [], 0
    for s in shapes:
        n = math.prod(s)
        out.append(flat[off:off + n].reshape(s))
        off += n
    return out


def _remote(src, dst, send, recv, k, dev):
    return pltpu.make_async_remote_copy(src_ref=src, dst_ref=dst, send_sem=send.at[k], recv_sem=recv.at[k],
                                        device_id=dev, device_id_type=MESH)


_ANY = pl.BlockSpec(memory_space=pl.ANY)


def _halves(t):
    r, c = t.shape
    return t.reshape(2, r // 2, c)


def _gather_weights(shards, sp):
    n = len(shards)
    per = 9
    n_sem = per * n + 3

    def body(*refs):
        sh, sp_ref = refs[:n], refs[n]
        outs, sout = refs[n + 1:2 * n + 1], refs[2 * n + 1]
        send, recv, loc = refs[2 * n + 2:]
        x, y, c = _coords()
        me = 2 * x + y
        cx_, cy_, cd_ = _other_chips(x, y)
        ix, iy, idg = (2 * p[0] + p[1] for p in (cx_, cy_, cd_))
        to_x, to_y, sib = (*cx_, c), (*cy_, c), (x, y, 1 - c)
        l1 = pltpu.make_async_copy(sp_ref, sout.at[me], loc.at[0])
        l1.start()
        sends = [_remote(sp_ref, sout.at[me], send, recv, per * n + j, (*p, c)) for j, p in enumerate((cx_, cy_, cd_))]
        for t in range(n):
            sends.append(_remote(sh[t].at[c], outs[t].at[me, c], send, recv, per * t + 0, to_x))
            sends.append(_remote(sh[t].at[c], outs[t].at[me, c], send, recv, per * t + 1, to_y))
            sends.append(_remote(sh[t], outs[t].at[me], send, recv, per * t + 8, sib))
        for cp in sends:
            cp.start()

        def go(src, dst, k, dev):
            cp = _remote(src, dst, send, recv, k, dev)
            cp.start()
            sends.append(cp)

        def piece(t, owner, first):
            q = sh[t].shape[1] // 2
            return outs[t].at[owner, c, pl.ds(0 if first else q, q)]

        for t in range(n):
            _remote(sh[t].at[c], outs[t].at[ix, c], send, recv, per * t + 0, to_x).wait_recv()
            go(piece(t, ix, False), piece(t, ix, False), per * t + 3, to_y)
            go(outs[t].at[ix, c], outs[t].at[ix, c], per * t + 4, sib)
        for t in range(n):
            _remote(sh[t].at[c], outs[t].at[iy, c], send, recv, per * t + 1, to_y).wait_recv()
            go(piece(t, iy, True), piece(t, iy, True), per * t + 2, to_x)
            go(outs[t].at[iy, c], outs[t].at[iy, c], per * t + 5, sib)
        for t in range(n):
            _remote(piece(t, idg, True), piece(t, idg, True), send, recv, per * t + 2, to_x).wait_recv()
            go(piece(t, idg, True), piece(t, idg, True), per * t + 6, sib)
            _remote(piece(t, idg, False), piece(t, idg, False), send, recv, per * t + 3, to_y).wait_recv()
            go(piece(t, idg, False), piece(t, idg, False), per * t + 7, sib)
        for j, p in enumerate((cx_, cy_, cd_)):
            _remote(sp_ref, sout.at[2 * p[0] + p[1]], send, recv, per * n + j, (*p, c)).wait_recv()
        for t in range(n):
            q = sh[t].shape[1] // 2
            other = lambda owner, lo=None: outs[t].at[owner, 1 - c] if lo is None else outs[t].at[owner, 1 - c, pl.ds(lo, q)]
            _remote(other(ix), other(ix), send, recv, per * t + 4, sib).wait_recv()
            _remote(other(iy), other(iy), send, recv, per * t + 5, sib).wait_recv()
            _remote(other(idg, 0), other(idg, 0), send, recv, per * t + 6, sib).wait_recv()
            _remote(other(idg, q), other(idg, q), send, recv, per * t + 7, sib).wait_recv()
            _remote(sh[t], outs[t].at[me], send, recv, per * t + 8, sib).wait_recv()
        for cp in sends:
            cp.wait_send()
        l1.wait()

    res = pl.pallas_call(
        body, name="gather_weights", in_specs=[_ANY] * (n + 1), out_specs=[_ANY] * (n + 1),
        out_shape=[jax.ShapeDtypeStruct((N_CHIPS,) + t.shape, t.dtype) for t in shards]
        + [jax.ShapeDtypeStruct((N_CHIPS,) + sp.shape, sp.dtype)],
        scratch_shapes=[pltpu.SemaphoreType.DMA((n_sem,)), pltpu.SemaphoreType.DMA((n_sem,)),
                        pltpu.SemaphoreType.DMA((1,))],
    )(*shards, sp)
    return res[:n], res[n]


_HBM = pl.BlockSpec(memory_space=pltpu.HBM)
_SEMS = pl.BlockSpec(memory_space=pltpu.SEMAPHORE)
_DATAFLOW = pltpu.SideEffectType.DATAFLOW_SIDE_EFFECTING


def _in_hbm(a):
    return pltpu.with_memory_space_constraint(a, pltpu.HBM)


def _start_copies(copies, arrays, n_sem, after, *, name):
    n, na = len(arrays), len(after)

    def body(*refs):
        for mine, _ in copies(refs[:n], refs[n + na], refs[n + na + 1]):
            mine.start()
        refs[-1][...] = jnp.zeros_like(refs[-1])

    res = pl.pallas_call(
        body, name=name, in_specs=[_HBM] * n + [_ANY] * na,
        out_specs=[_SEMS, _SEMS] + [_HBM] * n + [pl.BlockSpec(memory_space=pltpu.VMEM)],
        out_shape=[pltpu.SemaphoreType.DMA((n_sem,)), pltpu.SemaphoreType.DMA((n_sem,))]
        + [pltpu.HBM(a.shape, a.dtype) for a in arrays] + [jax.ShapeDtypeStruct((8, 128), F32)],
        input_output_aliases={t: 2 + t for t in range(n)},
        compiler_params=pltpu.CompilerParams(has_side_effects=_DATAFLOW),
    )(*[_in_hbm(a) for a in arrays], *after)
    return res[0], res[1], list(res[2:2 + n]), res[-1]


def _wait_copies(copies, send, recv, arrays, after, *, name):
    n = len(arrays)

    def body(*refs):
        for mine, theirs in copies(refs[:n], refs[n], refs[n + 1]):
            mine.wait_send()
            theirs.wait_recv()

    return list(pl.pallas_call(
        body, name=name, in_specs=[_HBM] * n + [_SEMS, _SEMS] + [_ANY] * len(after), out_specs=[_HBM] * n,
        out_shape=[pltpu.HBM(a.shape, a.dtype) for a in arrays], input_output_aliases={t: t for t in range(n)},
        compiler_params=pltpu.CompilerParams(has_side_effects=_DATAFLOW),
    )(*arrays, send, recv, *after))


def _sibling_copies(refs, send, recv):
    n = len(refs) // 2
    x, y, c = _coords()
    cps = [_remote(refs[t].at[:, 1 - c], refs[n + t], send, recv, t, (x, y, 1 - c)) for t in range(n)]
    return [(cp, cp) for cp in cps]


def _join_copies(refs, send, recv):
    x, y, c = _coords()
    sib = (x, y, 1 - c)
    return [(_remote(o.at[c], o.at[c], send, recv, t, sib), _remote(o.at[1 - c], o.at[1 - c], send, recv, t, sib))
            for t, o in enumerate(refs)]


def _gather_copies(sh, land, send, recv):
    x, y, c = _coords()
    me = 2 * x + y
    out = []
    for t in range(len(sh)):
        for j, (cx, cy) in enumerate(_other_chips(x, y)):
            dev = (cx, cy, c)
            out.append((_remote(sh[t].at[c], land[t].at[me, c], send, recv, 4 * t + j, dev),
                        _remote(sh[t].at[c], land[t].at[2 * cx + cy, c], send, recv, 4 * t + j, dev)))
        sib = (x, y, 1 - c)
        out.append((_remote(sh[t], land[t].at[me], send, recv, 4 * t + 3, sib),
                    _remote(sh[t], land[t].at[me], send, recv, 4 * t + 3, sib)))
    return out


def _gather_start(shards, after, *, name):
    n = len(shards)

    def body(*refs):
        sh, land = refs[:n], refs[n:2 * n]
        send, recv = refs[2 * n + 1], refs[2 * n + 2]
        token = refs[-1]
        for mine, _ in _gather_copies(sh, land, send, recv):
            mine.start()
        token[...] = jnp.zeros_like(token)

    lands = [_in_hbm(lax.empty((N_CHIPS,) + s.shape, s.dtype)) for s in shards]
    res = pl.pallas_call(
        body, name=name, in_specs=[_HBM] * (2 * n) + [_ANY],
        out_specs=[_SEMS, _SEMS] + [_HBM] * (2 * n) + [pl.BlockSpec(memory_space=pltpu.VMEM)],
        out_shape=[pltpu.SemaphoreType.DMA((4 * n,)), pltpu.SemaphoreType.DMA((4 * n,))]
        + [pltpu.HBM(s.shape, s.dtype) for s in shards] + [pltpu.HBM(l.shape, l.dtype) for l in lands]
        + [jax.ShapeDtypeStruct((8, 128), F32)],
        input_output_aliases={t: 2 + t for t in range(2 * n)},
        compiler_params=pltpu.CompilerParams(has_side_effects=_DATAFLOW),
    )(*[_in_hbm(s) for s in shards], *lands, after)
    return res[0], res[1], res[2:2 + n], res[2 + n:2 + 2 * n], res[-1]


def _gather_wait(send, recv, shards, lands, after, *, name):
    n = len(shards)

    def body(*refs):
        sh, land = refs[:n], refs[n:2 * n]
        send_r, recv_r = refs[2 * n], refs[2 * n + 1]
        for mine, theirs in _gather_copies(sh, land, send_r, recv_r):
            mine.wait_send()
            theirs.wait_recv()

    res = pl.pallas_call(
        body, name=name, in_specs=[_HBM] * (2 * n) + [_SEMS, _SEMS, _ANY], out_specs=[_HBM] * (2 * n),
        out_shape=[pltpu.HBM(s.shape, s.dtype) for s in shards] + [pltpu.HBM(l.shape, l.dtype) for l in lands],
        input_output_aliases={t: t for t in range(2 * n)},
        compiler_params=pltpu.CompilerParams(has_side_effects=_DATAFLOW),
    )(*shards, *lands, send, recv, after)
    return res[n:]


def _forward_copies(refs, send, recv):
    x, y, c = _coords()
    sib = (x, y, 1 - c)
    srcs = [2 * cx + cy for cx, cy in _other_chips(x, y)]
    return [(_remote(o.at[s, c], o.at[s, c], send, recv, 3 * t + j, sib),
             _remote(o.at[s, 1 - c], o.at[s, 1 - c], send, recv, 3 * t + j, sib))
            for t, o in enumerate(refs) for j, s in enumerate(srcs)]


def _small_copies(v, land, send, recv):
    x, y, c = _coords()
    me = 4 * x + 2 * y + c
    out = []
    for k in range(1, 8):
        px = 1 - x if k & 4 else x
        py = 1 - y if k & 2 else y
        pc = 1 - c if k & 1 else c
        out.append((_remote(v, land.at[me], send, recv, k - 1, (px, py, pc)),
                    _remote(v, land.at[4 * px + 2 * py + pc], send, recv, k - 1, (px, py, pc))))
    return out


def _small_start(v, after, *, name):
    def body(v_ref, land_ref, after_ref, send, recv, v_thru, land_thru, token):
        for mine, _ in _small_copies(v_ref, land_ref, send, recv):
            mine.start()
        token[...] = jnp.zeros_like(token)

    land = _in_hbm(lax.empty((8,) + v.shape, v.dtype))
    return pl.pallas_call(
        body, name=name, in_specs=[_HBM, _HBM, _ANY],
        out_specs=[_SEMS, _SEMS, _HBM, _HBM, pl.BlockSpec(memory_space=pltpu.VMEM)],
        out_shape=[pltpu.SemaphoreType.DMA((7,)), pltpu.SemaphoreType.DMA((7,)), pltpu.HBM(v.shape, v.dtype),
                   pltpu.HBM(land.shape, land.dtype), jax.ShapeDtypeStruct((8, 128), F32)],
        input_output_aliases={0: 2, 1: 3}, compiler_params=pltpu.CompilerParams(has_side_effects=_DATAFLOW),
    )(_in_hbm(v), land, after)


def _small_wait(send, recv, v, land, after, *, name):
    def body(v_ref, land_ref, send_r, recv_r, *rest):
        for mine, theirs in _small_copies(v_ref, land_ref, send_r, recv_r):
            mine.wait_send()
            theirs.wait_recv()

    return pl.pallas_call(
        body, name=name, in_specs=[_HBM, _HBM, _SEMS, _SEMS] + [_ANY] * len(after), out_specs=[_HBM, _HBM],
        out_shape=[pltpu.HBM(v.shape, v.dtype), pltpu.HBM(land.shape, land.dtype)],
        input_output_aliases={0: 0, 1: 1}, compiler_params=pltpu.CompilerParams(has_side_effects=_DATAFLOW),
    )(v, land, send, recv, *after)


def _small_sum(v, land, me_idx, *, name="small_sum"):
    def body(me_ref, v_ref, land_ref, o_ref):
        acc = None
        for s in range(8):
            term = jnp.where(me_ref[0] == s, v_ref[...], land_ref[s])
            acc = term if acc is None else acc + term
        o_ref[...] = acc

    whole = lambda shape: pl.BlockSpec(shape, lambda i, me_ref: (0,) * len(shape))
    return pl.pallas_call(
        body, name=name,
        grid_spec=pltpu.PrefetchScalarGridSpec(num_scalar_prefetch=1, grid=(1,), in_specs=[whole(v.shape), whole(land.shape)],
                                               out_specs=whole(v.shape)),
        out_shape=jax.ShapeDtypeStruct(v.shape, F32), compiler_params=_cp(1),
    )(me_idx, v, land)


RS_ROW_SPLIT = 2


def _rs_add_pair(gs, as_, c_idx, *, name):
    n = len(gs)

    def body(c_ref, *refs):
        for t in range(n):
            refs[2 * n + t][...] = (refs[t][...].astype(F32) + refs[n + t][...].astype(F32)).astype(BF16)

    def gspec(g):
        _, _, rh, cols = g.shape
        return pl.BlockSpec((None, None, rh // RS_ROW_SPLIT, cols), lambda j, i, c_ref: (j, c_ref[0], i, 0))

    def pspec(g):
        _, _, rh, cols = g.shape
        return pl.BlockSpec((None, rh // RS_ROW_SPLIT, cols), lambda j, i, c_ref: (j, i, 0))

    return pl.pallas_call(
        body, name=name,
        grid_spec=pltpu.PrefetchScalarGridSpec(
            num_scalar_prefetch=1, grid=(N_CHIPS, RS_ROW_SPLIT),
            in_specs=[gspec(g) for g in gs] + [pspec(g) for g in gs], out_specs=[pspec(g) for g in gs]),
        out_shape=[jax.ShapeDtypeStruct((N_CHIPS,) + g.shape[2:], BF16) for g in gs], compiler_params=_cp(2),
    )(c_idx, *gs, *as_)


def _chips_copies(p, r, send, recv):
    x, y, c = _coords()
    return [_remote(p[t].at[2 * cx + cy], r[t].at[k], send, recv, 3 * t + k, (cx, cy, c))
            for k, (cx, cy) in enumerate(_other_chips(x, y)) for t in range(len(p))]


def _rs_chips_start(ps, after, *, name):
    n, na = len(ps), len(after)

    def body(*refs):
        p, r = refs[:n], refs[n:2 * n]
        send, recv = refs[2 * n + na], refs[2 * n + na + 1]
        token = refs[-1]
        for cp in _chips_copies(p, r, send, recv):
            cp.start()
        token[...] = jnp.zeros_like(token)

    lands = [_in_hbm(lax.empty((3,) + p.shape[1:], p.dtype)) for p in ps]
    res = pl.pallas_call(
        body, name=name, in_specs=[_HBM] * (2 * n) + [_ANY] * na,
        out_specs=[_SEMS, _SEMS] + [_HBM] * (2 * n) + [pl.BlockSpec(memory_space=pltpu.VMEM)],
        out_shape=[pltpu.SemaphoreType.DMA((3 * n,)), pltpu.SemaphoreType.DMA((3 * n,))]
        + [pltpu.HBM(p.shape, p.dtype) for p in ps] + [pltpu.HBM(l.shape, l.dtype) for l in lands]
        + [jax.ShapeDtypeStruct((8, 128), F32)],
        input_output_aliases={t: 2 + t for t in range(2 * n)},
        compiler_params=pltpu.CompilerParams(has_side_effects=_DATAFLOW),
    )(*[_in_hbm(p) for p in ps], *lands, *after)
    return res[0], res[1], res[2:2 + n], res[2 + n:2 + 2 * n], res[-1]


def _rs_chips_wait(send, recv, ps, lands, after, *, name):
    n = len(ps)

    def body(*refs):
        p, r = refs[:n], refs[n:2 * n]
        for cp in _chips_copies(p, r, refs[2 * n], refs[2 * n + 1]):
            cp.wait_send()
            cp.wait_recv()

    res = pl.pallas_call(
        body, name=name, in_specs=[_HBM] * (2 * n) + [_SEMS, _SEMS] + [_ANY] * len(after), out_specs=[_HBM] * (2 * n),
        out_shape=[pltpu.HBM(p.shape, p.dtype) for p in ps] + [pltpu.HBM(l.shape, l.dtype) for l in lands],
        input_output_aliases={t: t for t in range(2 * n)},
        compiler_params=pltpu.CompilerParams(has_side_effects=_DATAFLOW),
    )(*ps, *lands, send, recv, *after)
    return res[:n], res[n:]


def _rs_add_chips(ps, rs, idx, *, name):
    n = len(ps)

    def body(idx_ref, *refs):
        for t in range(n):
            p_ref, r0, r1, r2 = refs[4 * t:4 * t + 4]
            refs[4 * n + t][...] = ((p_ref[...].astype(F32) + r0[...].astype(F32)) + r1[...].astype(F32)) + r2[...].astype(F32)

    in_specs, args = [], []
    for p, r in zip(ps, rs):
        _, rh, cols = p.shape
        blk = (None, rh // RS_ROW_SPLIT, cols)
        in_specs.append(pl.BlockSpec(blk, lambda i, idx_ref: (idx_ref[0], i, 0)))
        in_specs += [pl.BlockSpec(blk, lambda i, idx_ref, k=k: (k, i, 0)) for k in range(3)]
        args += [p, r, r, r]
    out_specs = [pl.BlockSpec((None, p.shape[1] // RS_ROW_SPLIT, p.shape[2]), lambda i, idx_ref: (idx_ref[1], i, 0))
                 for p in ps]
    return pl.pallas_call(
        body, name=name,
        grid_spec=pltpu.PrefetchScalarGridSpec(num_scalar_prefetch=1, grid=(RS_ROW_SPLIT,), in_specs=in_specs,
                                               out_specs=out_specs),
        out_shape=[jax.ShapeDtypeStruct((2,) + p.shape[1:], F32) for p in ps], compiler_params=_cp(1),
    )(idx, *args)


def _adamw(w, gs, m, v, *, name, dep=None):
    L, Rr, C = w.shape
    tr, tc = _pick(Rr, (256, 128, 64)), C
    if tr == Rr and Rr * C > 512 * 1024:
        tc = 256
    bc1 = 1.0 - ADAM_B1 ** ADAM_STEP
    bc2 = 1.0 - ADAM_B2 ** ADAM_STEP
    nd = 0 if dep is None else 1

    def body(*refs):
        w_ref, m_ref, v_ref = refs[0], refs[1], refs[2]
        g_refs = refs[3:3 + L]
        d_ref, mo_ref, vo_ref, go_ref = refs[3 + L + nd:]
        layer = pl.program_id(0)
        gv = g_refs[0][...]
        for q in range(1, L):
            gv = jnp.where(layer == q, g_refs[q][...], gv)
        mn = ADAM_B1 * m_ref[...] + (1.0 - ADAM_B1) * gv
        vn = ADAM_B2 * v_ref[...] + (1.0 - ADAM_B2) * (gv * gv)
        go_ref[...] = gv
        mo_ref[...] = mn
        vo_ref[...] = vn
        d_ref[...] = -ADAM_LR * ((mn / bc1) / (jnp.sqrt(vn / bc2) + ADAM_EPS) + ADAM_WD * w_ref[...])

    blk = pl.BlockSpec((None, tr, tc), lambda l, i, j: (l, i, j))
    gblks = [pl.BlockSpec((tr, tc), lambda l, i, j, q=q: (jnp.where(l == q, i, 0), jnp.where(l == q, j, 0))) for q in range(L)]
    return pl.pallas_call(
        body, name=name, grid=(L, Rr // tr, C // tc), in_specs=[blk] * 3 + gblks + [_ANY] * nd, out_specs=[blk] * 4,
        out_shape=[jax.ShapeDtypeStruct((L, Rr, C), F32)] * 4, compiler_params=_cp(3),
    )(w, m, v, *gs, *([] if dep is None else [dep]))


def _adamw_leaves(ws, gs, ms, vs, *, name):
    n = len(ws)
    bc1 = 1.0 - ADAM_B1 ** ADAM_STEP
    bc2 = 1.0 - ADAM_B2 ** ADAM_STEP

    def body(*refs):
        w_refs, g_refs, m_refs, v_refs, d_refs, mo_refs, vo_refs = (refs[q * n:(q + 1) * n] for q in range(7))
        for k in range(n):
            gv = g_refs[k][...]
            mn = ADAM_B1 * m_refs[k][...] + (1.0 - ADAM_B1) * gv
            vn = ADAM_B2 * v_refs[k][...] + (1.0 - ADAM_B2) * (gv * gv)
            mo_refs[k][...] = mn
            vo_refs[k][...] = vn
            d_refs[k][...] = -ADAM_LR * ((mn / bc1) / (jnp.sqrt(vn / bc2) + ADAM_EPS) + ADAM_WD * w_refs[k][...])

    in_vmem = pl.BlockSpec(memory_space=pltpu.VMEM)
    outs = pl.pallas_call(
        body, name=name, in_specs=[in_vmem] * (4 * n), out_specs=[in_vmem] * (3 * n),
        out_shape=[jax.ShapeDtypeStruct(t.shape, F32) for t in ws] * 3,
    )(*ws, *gs, *ms, *vs)
    return outs[:n], outs[n:2 * n], outs[2 * n:]


def kernel(x, positions, a_norm, a_in_proj, a_conv_w, a_conv_b, a_dt_bias, a_A_log, a_D, a_gnorm, a_out_proj,
           kv_norm, w_kv, b_kv, k_norm, b_norm, w_q, b_q, q_norm, sinks, w_o, b_o, f_norm, f_w_in, f_conv_w,
           f_conv_b, f_w_down, loss_target, m_a_norm, m_a_in_proj, m_a_conv_w, m_a_conv_b, m_a_dt_bias, m_a_A_log,
           m_a_D, m_a_gnorm, m_a_out_proj, m_kv_norm, m_w_kv, m_b_kv, m_k_norm, m_b_norm, m_w_q, m_b_q, m_q_norm,
           m_sinks, m_w_o, m_b_o, m_f_norm, m_f_w_in, m_f_conv_w, m_f_conv_b, m_f_w_down, v_a_norm, v_a_in_proj,
           v_a_conv_w, v_a_conv_b, v_a_dt_bias, v_a_A_log, v_a_D, v_a_gnorm, v_a_out_proj, v_kv_norm, v_w_kv,
           v_b_kv, v_k_norm, v_b_norm, v_w_q, v_b_q, v_q_norm, v_sinks, v_w_o, v_b_o, v_f_norm, v_f_w_in,
           v_f_conv_w, v_f_conv_b, v_f_w_down):
    wl = dict(zip(WEIGHTS, (a_norm, a_in_proj, a_conv_w, a_conv_b, a_dt_bias, a_A_log, a_D, a_gnorm, a_out_proj,
                            kv_norm, w_kv, b_kv, k_norm, b_norm, w_q, b_q, q_norm, sinks, w_o, b_o, f_norm, f_w_in,
                            f_conv_w, f_conv_b, f_w_down)))
    ml = dict(zip(WEIGHTS, (m_a_norm, m_a_in_proj, m_a_conv_w, m_a_conv_b, m_a_dt_bias, m_a_A_log, m_a_D, m_a_gnorm,
                            m_a_out_proj, m_kv_norm, m_w_kv, m_b_kv, m_k_norm, m_b_norm, m_w_q, m_b_q, m_q_norm,
                            m_sinks, m_w_o, m_b_o, m_f_norm, m_f_w_in, m_f_conv_w, m_f_conv_b, m_f_w_down)))
    vl = dict(zip(WEIGHTS, (v_a_norm, v_a_in_proj, v_a_conv_w, v_a_conv_b, v_a_dt_bias, v_a_A_log, v_a_D, v_a_gnorm,
                            v_a_out_proj, v_kv_norm, v_w_kv, v_b_kv, v_k_norm, v_b_norm, v_w_q, v_b_q, v_q_norm,
                            v_sinks, v_w_o, v_b_o, v_f_norm, v_f_w_in, v_f_conv_w, v_f_conv_b, v_f_w_down)))
    xi, yi, ci = _coords()
    me = 2 * xi + yi
    S = x.shape[1]

    def block_of(n, layer):
        t = wl[n]
        return t if layer is None else t[layer]

    rows = lambda t: t.reshape(-1, t.shape[-1])
    c_idx = jnp.reshape(ci, (1,)).astype(jnp.int32)
    me_c = jnp.stack([me, ci]).astype(jnp.int32)
    early = ("in_proj",)
    late = (("out_proj", "f_in0", "f_down0"), ("w_kv", "w_q", "w_o", "f_in1", "f_down1"))
    shards = {name: _halves(block_of(wn, layer).astype(BF16)) for name, wn, layer in MATS}

    sp = _pack([wl[n] for n, _ in SMALL_CUT], 8, 128, F32)
    gathered, gs = _gather_weights([shards[k] for k in early], sp)
    gt = {k: t.reshape(N_CHIPS, -1, t.shape[-1]) for k, t in zip(early, gathered)}
    started = {0: _gather_start([shards[k] for k in late[0]], gs, name="gather_late_start0")}
    full = {n: wl[n] for n in SMALL_REP}
    gs = gs.reshape(N_CHIPS, -1)
    off = 0
    for n, ax in SMALL_CUT:
        shp = wl[n].shape
        size = math.prod(shp)
        piece = jnp.moveaxis(gs[:, off:off + size].reshape((N_CHIPS,) + shp), 0, ax)
        full[n] = piece.reshape(shp[:ax] + (N_CHIPS * shp[ax],) + shp[ax + 1:])
        off += size
    w = _prep_small(full, {})
    w["w_zx"], w["w_dt"] = _join_in_proj(gt["in_proj"])
    w["dep"] = started[0][4]

    class Comm:
        flight = []
        reduced = {}

        forwarding = {}

        def late_start(self, part, after):
            started[part] = _gather_start([shards[k] for k in late[part]], after, name=f"gather_late_start{part}")
            return started[part][4]

        def late_arrived(self, part, after):
            send, recv, shs, lands, _ = started[part]
            lands = _gather_wait(send, recv, shs, lands, after[0], name=f"gather_late_wait{part}")
            send, recv, lands, token = _start_copies(_forward_copies, list(lands), 3 * len(lands), after,
                                                     name=f"gather_late_forward_start{part}")
            self.forwarding[part] = (send, recv, lands)
            return token

        def late_weights(self, w, after, part):
            send, recv, lands = self.forwarding[part]
            lands = _wait_copies(_forward_copies, send, recv, lands, [after], name=f"gather_late_forward_wait{part}")
            lt = {k: t.reshape(N_CHIPS, -1, t.shape[-1]) for k, t in zip(late[part], lands)}
            w = dict(w)
            if part == 0:
                w["a_out_proj"], w["f_w_in"], w["f_w_down"] = rows(lt["out_proj"]), [lt["f_in0"]], [rows(lt["f_down0"])]
            else:
                w["w_kv"], w["w_q"], w["w_o"] = (rows(lt[k]) for k in ("w_kv", "w_q", "w_o"))
                w["f_w_in"], w["f_w_down"] = w["f_w_in"] + [lt["f_in1"]], w["f_w_down"] + [rows(lt["f_down1"])]
            return w

        def advance(self, after, group=None, tensors=None):
            token = None
            for grp in list(self.flight):
                tag, n = grp["tag"], len(grp["names"])
                dep = list(after) + ([] if token is None else [token])
                if grp["stage"] == "sibling":
                    arrs = _wait_copies(_sibling_copies, grp["send"], grp["recv"], grp["arrays"], dep, name=f"rs_sibling_wait{tag}")
                    pairs = _rs_add_pair(arrs[:n], arrs[n:], c_idx, name=f"rs_add_pair{tag}")
                    send, recv, ps, lands, token = _rs_chips_start(pairs, dep, name=f"rs_chips_start{tag}")
                    grp.update(stage="chips", send=send, recv=recv, ps=ps, lands=lands)
                elif grp["stage"] == "chips":
                    ps, rs = _rs_chips_wait(grp["send"], grp["recv"], grp["ps"], grp["lands"], dep, name=f"rs_chips_wait{tag}")
                    halves = _rs_add_chips(ps, rs, me_c, name=f"rs_add_chips{tag}")
                    send, recv, arrs, token = _start_copies(_join_copies, halves, n, dep, name=f"rs_join_start{tag}")
                    grp.update(stage="join", send=send, recv=recv, arrays=arrs)
                else:
                    joined = _wait_copies(_join_copies, grp["send"], grp["recv"], grp["arrays"], dep, name=f"rs_join_wait{tag}")
                    self.reduced.update({k: rows(t) for k, t in zip(grp["names"], joined)})
                    self.flight.remove(grp)
            if group is not None:
                names = list(tensors)
                glist = [tensors[k].reshape(N_CHIPS, 2, -1, tensors[k].shape[-1]) for k in names]
                lands = [lax.empty((N_CHIPS,) + gq.shape[2:], gq.dtype) for gq in glist]
                dep = [a for a in after if not any(a is t for t in tensors.values())] + ([] if token is None else [token])
                send, recv, arrs, token = _start_copies(_sibling_copies, glist + lands, len(names), dep,
                                                        name=f"rs_sibling_start{group}")
                self.flight.append(dict(tag=group, names=names, stage="sibling", send=send, recv=recv, arrays=arrs))
            return token

    comm = Comm()

    posf = positions.reshape(S, 1).astype(F32)
    loss_part, dx0, gr, tok = _local_step(x[0], posf, loss_target[0], w, comm)
    g = _small_grads(gr)

    small_names = [n for n, _ in SMALL_CUT] + list(SMALL_REP)
    sv = _pack([g[n] for n in small_names] + [loss_part[0:1, 0:1]], 8, 128, F32)
    s_send, s_recv, sv, s_land, s_token = _small_start(sv, tok, name="small_start")

    grads, delta, new_m, new_v = {}, {}, {}, {}

    def update(wn, dep):
        gl = [comm.reduced[name] for name, n2, _ in MATS if n2 == wn]
        shp = wl[wn].shape
        three = (len(gl),) + gl[0].shape
        flip = shp[-1] % 128 != 0
        view = (lambda t: t.reshape(three).transpose(0, 2, 1)) if flip else (lambda t: t.reshape(three))
        back = (lambda t: t.transpose(0, 2, 1).reshape(shp)) if flip else (lambda t: t.reshape(shp))
        if flip:
            gl = [t.T for t in gl]
        d, mn, vn, go = _adamw(view(wl[wn]), gl, view(ml[wn]), view(vl[wn]), name="adamw_" + wn, dep=dep)
        grads[wn], delta[wn], new_m[wn], new_v[wn] = back(go), back(d), back(mn), back(vn)
        return d

    first = [update(wn, s_token) for wn in ("w_q", "w_o", "w_kv")]
    tok = comm.advance(first)
    second = [update(wn, tok) for wn in ("f_w_in", "f_w_down", "a_out_proj")]
    tok = comm.advance(second)
    done = first + second + [tok]

    sv, s_land = _small_wait(s_send, s_recv, sv, s_land, done, name="small_wait")
    sred = _small_sum(sv, s_land, jnp.reshape(2 * me + ci, (1,)).astype(jnp.int32)).reshape(-1)
    small_shapes = [g[n].shape for n in small_names] + [(1,)]
    sg = dict(zip(small_names + ["loss"], _unpack(sred, small_shapes)))
    loss = sg["loss"].reshape(())
    g_small = {}
    for n, ax in SMALL_CUT:
        size = wl[n].shape[ax]
        g_small[n] = lax.dynamic_slice_in_dim(sg[n], me * size, size, axis=ax)
    for n in SMALL_REP:
        g_small[n] = sg[n].reshape(wl[n].shape)

    leaves = lambda d: [d[n].reshape(1, -1) if d[n].ndim == 1 else d[n] for n in small_names]
    ds, mns, vns = _adamw_leaves(leaves(wl), leaves(g_small), leaves(ml), leaves(vl), name="adamw_small")
    comm.advance([ds[0]])
    update("a_in_proj", None)
    for n, dd, mm, vv in zip(small_names, ds, mns, vns):
        shp = wl[n].shape
        grads[n], delta[n], new_m[n], new_v[n] = g_small[n], dd.reshape(shp), mm.reshape(shp), vv.reshape(shp)

    return (loss, dx0[None], *[grads[n] for n in WEIGHTS], *[delta[n] for n in WEIGHTS],
            *[new_m[n] for n in WEIGHTS], *[new_v[n] for n in WEIGHTS])
```

```python
import math

import jax
import jax.numpy as jnp
from jax import lax
from jax.experimental import pallas as pl
from jax.experimental.pallas import tpu as pltpu

F32 = jnp.float32
BF16 = jnp.bfloat16

EPS = 1e-5
CHUNK = 256
WINDOW = 128
HEAD = 64
SSM_HEADS = 32
SSM_GROUPS = 8
SSM_STATE = 128
ATT_KV = 4
ATT_G = 4
ROPE_THETA = 10000.0
NEG = -1e30
N_CHIPS = 4
VMEM_LIMIT = 56 * 1024 * 1024

ADAM_LR, ADAM_B1, ADAM_B2, ADAM_EPS, ADAM_WD, ADAM_STEP = 0.001, 0.9, 0.999, 1e-08, 0.01, 10


def _cp(n_axes):
    return pltpu.CompilerParams(dimension_semantics=("arbitrary",) * n_axes, vmem_limit_bytes=VMEM_LIMIT)


def _pick(dim, prefs):
    for p in prefs:
        if dim % p == 0:
            return p
    return dim


def _iota(shape, dim):
    return lax.broadcasted_iota(jnp.int32, shape, dim)


def _dot(a, b, ca=1, cb=0):
    return lax.dot_general(a, b, (((ca,), (cb,)), ((), ())), preferred_element_type=F32)


def _dot3(x, ind):
    h = x.astype(BF16)
    r = x - h.astype(F32)
    m = r.astype(BF16)
    lo = (r - m.astype(F32)).astype(BF16)
    return _dot(h, ind) + _dot(m, ind) + _dot(lo, ind)


def _sigmoid(x):
    return jax.nn.sigmoid(x)


def _mm(a, b, *, name, ta=False, tb=False, bias=None, res=None, out_dtype=F32, b_koff=0, tm=None, tn=None, tk=None,
        dims=None, a_spec=None, b_spec=None, o_spec=None, o_shape=None, dep=None, more=(), target=None,
        rms=None, rms_colsum=False):
    if dims is not None:
        M, N, K = dims
    else:
        if ta:
            K, M = a.shape
        else:
            M, K = a.shape
        N = b.shape[0] if tb else b.shape[1]
    tm = tm or _pick(M, (1024, 1408, 512, 256, 128))
    tn = tn or _pick(N, (512, 1408, 256, 128))
    tk = tk or (K if K <= 2048 else _pick(K, (2048, 1408, 1024, 512)))
    assert M % tm == 0 and N % tn == 0 and K % tk == 0 and b_koff % tk == 0
    nk = K // tk
    kb0 = b_koff // tk
    has_bias, has_res = bias is not None, res is not None

    def body(*refs):
        a_ref, b_ref = refs[0], refs[1]
        pos = 2
        bias_ref = res_ref = acc_ref = None
        if has_bias:
            bias_ref = refs[pos]
            pos += 1
        if has_res:
            res_ref = refs[pos]
            pos += 1
        if dep is not None:
            pos += 1
        extra = refs[pos:pos + 2 * len(more)]
        pos += 2 * len(more)
        tgt_ref = lp_ref = rx_ref = rg_ref = rd_ref = dg_ref = cs_ref = None
        if target is not None:
            tgt_ref = refs[pos]
            pos += 1
        if rms is not None:
            rx_ref, rg_ref, rd_ref = refs[pos:pos + 3]
            pos += 3
        o_ref = refs[pos]
        pos += 1
        if target is not None:
            lp_ref = refs[pos]
            pos += 1
        if rms is not None:
            dg_ref = refs[pos]
            pos += 1
            if rms_colsum:
                cs_ref = refs[pos]
                pos += 1
        if nk > 1:
            acc_ref = refs[pos]
        part = _dot(a_ref[...].astype(BF16), b_ref[...].astype(BF16), 0 if ta else 1, 1 if tb else 0)
        for q in range(len(more)):
            part = part + _dot(extra[2 * q][...].astype(BF16), extra[2 * q + 1][...].astype(BF16),
                               0 if ta else 1, 1 if tb else 0)

        def finish(acc):
            if has_bias:
                acc = acc + bias_ref[...]
            if has_res:
                acc = acc + res_ref[...]
            if target is not None:
                err = acc - tgt_ref[...]
                acc = err * (1.0 / N)
                part_loss = jnp.sum(jnp.sum(err * err, axis=1, keepdims=True), axis=0, keepdims=True) * (0.5 / N)
                first = (pl.program_id(0) == 0) & (pl.program_id(1) == 0)

                @pl.when(first)
                def _():
                    lp_ref[...] = jnp.broadcast_to(part_loss, lp_ref.shape)

                @pl.when(jnp.logical_not(first))
                def _():
                    lp_ref[...] += jnp.broadcast_to(part_loss, lp_ref.shape)

            if rms is not None:
                xv = rx_ref[...]
                r = lax.rsqrt(jnp.mean(xv * xv, axis=-1, keepdims=True) + EPS)
                xh = xv * r
                dxh = acc * rg_ref[...]
                dg_part = jnp.sum(acc * xh, axis=0, keepdims=True)
                acc = rd_ref[...] + r * (dxh - xh * jnp.mean(dxh * xh, axis=-1, keepdims=True))
                cs_part = jnp.sum(acc, axis=0, keepdims=True) if rms_colsum else None
                first_rows = pl.program_id(0) == 0

                @pl.when(first_rows)
                def _():
                    dg_ref[...] = dg_part
                    if rms_colsum:
                        cs_ref[...] = cs_part

                @pl.when(jnp.logical_not(first_rows))
                def _():
                    dg_ref[...] += dg_part
                    if rms_colsum:
                        cs_ref[...] += cs_part

            o_ref[...] = acc.astype(out_dtype)

        if nk == 1:
            finish(part)
        else:
            k = pl.program_id(2)

            @pl.when(k == 0)
            def _():
                acc_ref[...] = part

            @pl.when(k > 0)
            def _():
                acc_ref[...] += part

            @pl.when(k == nk - 1)
            def _():
                finish(acc_ref[...])

    if a_spec is None:
        a_spec = pl.BlockSpec((tk, tm), lambda i, j, k: (k, i)) if ta else pl.BlockSpec((tm, tk), lambda i, j, k: (i, k))
    if b_spec is None:
        b_spec = (pl.BlockSpec((tn, tk), lambda i, j, k: (j, k + kb0)) if tb
                  else pl.BlockSpec((tk, tn), lambda i, j, k: (k + kb0, j)))
    if o_spec is None:
        o_spec = pl.BlockSpec((tm, tn), lambda i, j, k: (i, j))
    in_specs, args = [a_spec, b_spec], [a, b]
    if has_bias:
        in_specs.append(pl.BlockSpec((1, tn), lambda i, j, k: (0, j)))
        args.append(bias)
    if has_res:
        in_specs.append(pl.BlockSpec((tm, tn), lambda i, j, k: (i, j)))
        args.append(res)
    if dep is not None:
        in_specs.append(pl.BlockSpec(memory_space=pl.ANY))
        args.append(dep)
    for piece in more:
        a2, sa, b2, sb = piece if len(piece) == 4 else (a, piece[0], b, piece[1])
        in_specs += [sa, sb]
        args += [a2, b2]
    out_specs, out_shape = [o_spec], [jax.ShapeDtypeStruct(o_shape or (M, N), out_dtype)]
    if target is not None:
        in_specs.append(pl.BlockSpec((tm, tn), lambda i, j, k: (i, j)))
        args.append(target)
        out_specs.append(pl.BlockSpec((8, 128), lambda i, j, k: (0, 0)))
        out_shape.append(jax.ShapeDtypeStruct((8, 128), F32))
    if rms is not None:
        assert tn == N and nk == 1
        row, vec = pl.BlockSpec((tm, N), lambda i, j, k: (i, 0)), pl.BlockSpec((1, N), lambda i, j, k: (0, 0))
        in_specs += [row, vec, row]
        args += list(rms)
        out_specs += [vec] * (2 if rms_colsum else 1)
        out_shape += [jax.ShapeDtypeStruct((1, N), F32)] * (2 if rms_colsum else 1)
    if len(out_specs) == 1:
        out_specs, out_shape = out_specs[0], out_shape[0]
    return pl.pallas_call(
        body, name=name, grid=(M // tm, N // tn, nk), in_specs=in_specs, out_specs=out_specs, out_shape=out_shape,
        scratch_shapes=[pltpu.VMEM((tm, tn), F32)] if nk > 1 else [],
        compiler_params=_cp(3),
    )(*args)


def _norm_mm(x, gain, b, *, name, bias=None, N=None, tn=None, b_spec=None, dep=None):
    M, K = x.shape
    N = N or b.shape[1]
    tm = _pick(M, (1024, 512, 256))
    tn = tn or _pick(N, (512, 1408, 256, 128))
    has_bias = bias is not None

    def body(*refs):
        x_ref, g_ref, b_ref = refs[:3]
        pos = 3 + (1 if has_bias else 0) + (0 if dep is None else 1)
        o_ref, h_ref = refs[pos], refs[pos + 1]

        @pl.when(pl.program_id(1) == 0)
        def _():
            xv = x_ref[...]
            h_ref[...] = (xv * lax.rsqrt(jnp.mean(xv * xv, axis=-1, keepdims=True) + EPS) * g_ref[...]).astype(BF16)

        acc = _dot(h_ref[...], b_ref[...].astype(BF16))
        if has_bias:
            acc = acc + refs[3][...]
        o_ref[...] = acc

    in_specs = [pl.BlockSpec((tm, K), lambda i, j: (i, 0)), pl.BlockSpec((1, K), lambda i, j: (0, 0)),
                b_spec or pl.BlockSpec((K, tn), lambda i, j: (0, j))]
    args = [x, gain, b]
    if has_bias:
        in_specs.append(pl.BlockSpec((1, tn), lambda i, j: (0, j)))
        args.append(bias)
    if dep is not None:
        in_specs.append(pl.BlockSpec(memory_space=pl.ANY))
        args.append(dep)
    return pl.pallas_call(
        body, name=name, grid=(M // tm, N // tn), in_specs=in_specs,
        out_specs=[pl.BlockSpec((tm, tn), lambda i, j: (i, j)), pl.BlockSpec((tm, K), lambda i, j: (i, 0))],
        out_shape=[jax.ShapeDtypeStruct((M, N), F32), jax.ShapeDtypeStruct((M, K), BF16)], compiler_params=_cp(2),
    )(*args)


def _rms_bwd(x, gains, dhs, dres, *, name, tr=256, want_colsum=False):
    S, D = x.shape
    n = len(gains)
    steps = S // tr

    def body(*refs):
        x_ref = refs[0]
        g_refs = refs[1:1 + n]
        dh_refs = refs[1 + n:1 + 2 * n]
        dres_ref = refs[1 + 2 * n]
        dx_ref = refs[2 + 2 * n]
        dg_refs = refs[3 + 2 * n:3 + 3 * n]
        cs_ref = refs[3 + 3 * n] if want_colsum else None
        i = pl.program_id(0)
        xv = x_ref[...]
        r = lax.rsqrt(jnp.mean(xv * xv, axis=-1, keepdims=True) + EPS)
        xh = xv * r
        dx = dres_ref[...]
        for q in range(n):
            dh = dh_refs[q][...]
            dxh = dh * g_refs[q][...]
            dx = dx + r * (dxh - xh * jnp.mean(dxh * xh, axis=-1, keepdims=True))
            part = jnp.sum(dh * xh, axis=0, keepdims=True)

            @pl.when(i == 0)
            def _():
                dg_refs[q][...] = part

            @pl.when(i > 0)
            def _():
                dg_refs[q][...] += part

        dx_ref[...] = dx
        if want_colsum:
            cpart = jnp.sum(dx, axis=0, keepdims=True)

            @pl.when(i == 0)
            def _():
                cs_ref[...] = cpart

            @pl.when(i > 0)
            def _():
                cs_ref[...] += cpart

    row = pl.BlockSpec((tr, D), lambda i: (i, 0))
    vec = pl.BlockSpec((1, D), lambda i: (0, 0))
    n_vec_out = n + (1 if want_colsum else 0)
    outs = pl.pallas_call(
        body, name=name, grid=(steps,), in_specs=[row] + [vec] * n + [row] * n + [row],
        out_specs=[row] + [vec] * n_vec_out,
        out_shape=[jax.ShapeDtypeStruct((S, D), F32)] + [jax.ShapeDtypeStruct((1, D), F32)] * n_vec_out,
        compiler_params=_cp(1),
    )(x, *gains, *dhs, dres)
    return outs


def _colsum(x, *, name, tr=256):
    S, D = x.shape

    def body(x_ref, o_ref):
        i = pl.program_id(0)
        part = jnp.sum(x_ref[...].astype(F32), axis=0, keepdims=True)

        @pl.when(i == 0)
        def _():
            o_ref[...] = part

        @pl.when(i > 0)
        def _():
            o_ref[...] += part

    return pl.pallas_call(
        body, name=name, grid=(S // tr,), in_specs=[pl.BlockSpec((tr, D), lambda i: (i, 0))],
        out_specs=pl.BlockSpec((1, D), lambda i: (0, 0)), out_shape=jax.ShapeDtypeStruct((1, D), F32),
        compiler_params=_cp(1),
    )(x)


STRIP = 64
HALO = 8


def _strips(S, tc):
    return [(r0, slice(l0, l0 + 128)) for l0 in range(0, tc, 128) for r0 in range(S - STRIP, -1, -STRIP)]


def _with_halo(ref, r0, ls):
    if r0 == 0:
        return jnp.concatenate([jnp.zeros((HALO, 128), F32), ref[0:STRIP, ls]], axis=0)
    return ref[r0 - HALO:r0 + STRIP, ls]


def _conv_strip(xw, w_ref, b_ref, ls, width):
    acc = b_ref[:, ls] + w_ref[pl.ds(width - 1, 1), ls] * xw[HALO:]
    shifted = []
    for s in range(1, width):
        xs = pltpu.roll(xw, s, axis=0)[HALO:]
        shifted.append(xs)
        acc = acc + w_ref[pl.ds(width - 1 - s, 1), ls] * xs
    return acc, shifted


def _conv_strip_back(dacc, after, xc, shifted, w_ref, ls, width):
    ext = jnp.concatenate([dacc, after], axis=0)
    dx = w_ref[pl.ds(width - 1, 1), ls] * dacc
    dws = [None] * width
    dws[width - 1] = jnp.sum(dacc * xc, axis=0, keepdims=True)
    for s in range(1, width):
        dx = dx + w_ref[pl.ds(width - 1 - s, 1), ls] * pltpu.roll(ext, STRIP + HALO - s, axis=0)[:STRIP]
        dws[width - 1 - s] = jnp.sum(dacc * shifted[s - 1], axis=0, keepdims=True)
    return dx, dws, jnp.sum(dacc, axis=0, keepdims=True)


def _conv_back_block(S, tc, width, w_ref, b_ref, x_ref, dacc_of, dx_store, dw_ref, db_ref):
    for l0 in range(0, tc, 128):
        ls = slice(l0, l0 + 128)
        after = jnp.zeros((HALO, 128), F32)
        tot = None
        for r0 in range(S - STRIP, -1, -STRIP):
            xw = _with_halo(x_ref, r0, ls)
            acc, shifted = _conv_strip(xw, w_ref, b_ref, ls, width)
            dacc = dacc_of(r0, ls, acc, _sigmoid(acc))
            dx, dws, db = _conv_strip_back(dacc, after, xw[HALO:], shifted, w_ref, ls, width)
            dx_store(r0, ls, dx)
            after = dacc[:HALO]
            part = dws + [db]
            tot = part if tot is None else [p + q for p, q in zip(tot, part)]
        for k in range(width):
            dw_ref[pl.ds(k, 1), ls] = tot[k]
        db_ref[:, ls] = tot[width]


def _conv_silu_fwd(xin, col0, C, w, b, *, name, tc=512):
    S = xin.shape[0]
    width = w.shape[0]
    off = col0 // tc

    def body(x_ref, w_ref, b_ref, o_ref):
        for r0, ls in _strips(S, tc):
            acc, _ = _conv_strip(_with_halo(x_ref, r0, ls), w_ref, b_ref, ls, width)
            o_ref[r0:r0 + STRIP, ls] = acc * _sigmoid(acc)

    return pl.pallas_call(
        body, name=name, grid=(C // tc,),
        in_specs=[pl.BlockSpec((S, tc), lambda j: (0, j + off)), pl.BlockSpec((width, tc), lambda j: (0, j)),
                  pl.BlockSpec((1, tc), lambda j: (0, j))],
        out_specs=pl.BlockSpec((S, tc), lambda j: (0, j)), out_shape=jax.ShapeDtypeStruct((S, C), F32),
        compiler_params=_cp(1),
    )(xin, w, b)


def _conv_silu_bwd(xin, col0, C, w, b, douts, *, name, tc=256):
    S = xin.shape[0]
    width = w.shape[0]
    off = col0 // tc
    nd = len(douts)
    ranges = [(o // tc, (o + d.shape[1]) // tc) for d, o in douts]

    def body(*refs):
        x_ref, w_ref, b_ref = refs[0], refs[1], refs[2]
        d_refs = refs[3:3 + nd]
        dx_ref, dw_ref, db_ref = refs[3 + nd], refs[4 + nd], refs[5 + nd]
        j = pl.program_id(0)

        def dacc_of(r0, ls, acc, sg):
            dout = jnp.zeros((STRIP, 128), F32)
            for q in range(nd):
                lo, hi = ranges[q]
                dout = dout + jnp.where((j >= lo) & (j < hi), d_refs[q][r0:r0 + STRIP, ls], 0.0)
            return dout * (sg * (1.0 + acc * (1.0 - sg)))

        def dx_store(r0, ls, dx):
            dx_ref[r0:r0 + STRIP, ls] = dx.astype(BF16)

        _conv_back_block(S, tc, width, w_ref, b_ref, x_ref, dacc_of, dx_store, dw_ref, db_ref)

    d_specs = [pl.BlockSpec((S, tc), (lambda j, lo=lo, hi=hi: (0, jnp.clip(j - lo, 0, hi - lo - 1)))) for lo, hi in ranges]
    return pl.pallas_call(
        body, name=name, grid=(C // tc,),
        in_specs=[pl.BlockSpec((S, tc), lambda j: (0, j + off)), pl.BlockSpec((width, tc), lambda j: (0, j)),
                  pl.BlockSpec((1, tc), lambda j: (0, j))] + d_specs,
        out_specs=[pl.BlockSpec((S, tc), lambda j: (0, j)), pl.BlockSpec((width, tc), lambda j: (0, j)),
                   pl.BlockSpec((1, tc), lambda j: (0, j))],
        out_shape=[jax.ShapeDtypeStruct((S, C), BF16), jax.ShapeDtypeStruct((width, C), F32),
                   jax.ShapeDtypeStruct((1, C), F32)],
        compiler_params=_cp(1),
    )(xin, w, b, *[d for d, _ in douts])


def _ffn_act_fwd(u, w, b, *, name, tc=256):
    S, F2 = u.shape
    Fd = F2 // 2
    width = w.shape[0]
    nb = Fd // tc

    def body(g_ref, v_ref, w_ref, b_ref, o_ref):
        for r0, ls in _strips(S, tc):
            acc, _ = _conv_strip(_with_halo(g_ref, r0, ls), w_ref, b_ref, ls, width)
            o_ref[r0:r0 + STRIP, ls] = (acc * _sigmoid(acc) * v_ref[r0:r0 + STRIP, ls]).astype(BF16)

    return pl.pallas_call(
        body, name=name, grid=(nb,),
        in_specs=[pl.BlockSpec((S, tc), lambda j: (0, j)), pl.BlockSpec((S, tc), lambda j: (0, j + nb)),
                  pl.BlockSpec((width, tc), lambda j: (0, j)), pl.BlockSpec((1, tc), lambda j: (0, j))],
        out_specs=pl.BlockSpec((S, tc), lambda j: (0, j)), out_shape=jax.ShapeDtypeStruct((S, Fd), BF16),
        compiler_params=_cp(1),
    )(u, u, w, b)


def _ffn_act_bwd(u, w, b, da, *, name, tc=256):
    S, F2 = u.shape
    Fd = F2 // 2
    width = w.shape[0]
    nb = Fd // tc

    def body(g_ref, v_ref, w_ref, b_ref, da_ref, du_ref, dw_ref, db_ref, a_ref):
        def dacc_of(r0, ls, acc, sg):
            rs = slice(r0, r0 + STRIP)
            dav, val, silu = da_ref[rs, ls], v_ref[rs, ls], acc * sg
            a_ref[rs, ls] = (silu * val).astype(BF16)
            du_ref[1, rs, ls] = (dav * silu).astype(BF16)
            return dav * val * (sg * (1.0 + acc * (1.0 - sg)))

        def dx_store(r0, ls, dx):
            du_ref[0, r0:r0 + STRIP, ls] = dx.astype(BF16)

        _conv_back_block(S, tc, width, w_ref, b_ref, g_ref, dacc_of, dx_store, dw_ref, db_ref)

    blk = pl.BlockSpec((S, tc), lambda j: (0, j))
    return pl.pallas_call(
        body, name=name, grid=(nb,),
        in_specs=[blk, pl.BlockSpec((S, tc), lambda j: (0, j + nb)), pl.BlockSpec((width, tc), lambda j: (0, j)),
                  pl.BlockSpec((1, tc), lambda j: (0, j)), blk],
        out_specs=[pl.BlockSpec((2, S, tc), lambda j: (0, 0, j)), pl.BlockSpec((width, tc), lambda j: (0, j)),
                   pl.BlockSpec((1, tc), lambda j: (0, j)), blk],
        out_shape=[jax.ShapeDtypeStruct((2, S, Fd), BF16),
                   jax.ShapeDtypeStruct((width, Fd), F32), jax.ShapeDtypeStruct((1, Fd), F32),
                   jax.ShapeDtypeStruct((S, Fd), BF16)],
        compiler_params=_cp(1),
    )(u, u, w, b, da)


def _ssd_prep(dtr, dt_bias, a_log, *, name="ssd_prep"):
    S = dtr.shape[0]

    def body(d_ref, b_ref, al_ref, dt_ref, ac_ref, sg_ref, act_ref):
        lane = _iota((CHUNK, 128), 1)
        valid = lane < SSM_HEADS
        z = d_ref[...] + b_ref[...]
        dt = jnp.where(valid, jnp.maximum(z, 0.0) + jnp.log(1.0 + jnp.exp(-jnp.abs(z))), 0.0)
        a = dt * (-jnp.exp(al_ref[...]))
        row = _iota((CHUNK, 128), 0)
        k = 1
        while k < CHUNK:
            a = a + jnp.where(row >= k, pltpu.roll(a, k, axis=0), 0.0)
            k *= 2
        sg = jnp.where(valid, _sigmoid(z), 0.0)
        for arr, ref in ((dt, dt_ref), (a, ac_ref), (sg, sg_ref)):
            for g in range(SSM_GROUPS):
                ref[g] = jnp.where(lane < 4, arr if g == 0 else pltpu.roll(arr, 128 - 4 * g, axis=1), 0.0)
        act_ref[...] = a.T[:SSM_HEADS, :]

    blk = pl.BlockSpec((CHUNK, 128), lambda i: (i, 0))
    vec = pl.BlockSpec((1, 128), lambda i: (0, 0))
    grp = pl.BlockSpec((SSM_GROUPS, CHUNK, 128), lambda i: (0, i, 0))
    return pl.pallas_call(
        body, name=name, grid=(S // CHUNK,), in_specs=[blk, vec, vec],
        out_specs=[grp, grp, grp, pl.BlockSpec((SSM_HEADS, CHUNK), lambda i: (0, i))],
        out_shape=[jax.ShapeDtypeStruct((SSM_GROUPS, S, 128), F32)] * 3 + [jax.ShapeDtypeStruct((SSM_HEADS, S), F32)],
        compiler_params=_cp(1),
    )(dtr, dt_bias, a_log)


SSD_GPS = 4


def _expand4(v, lanes):
    out = jnp.broadcast_to(v[:, 3:4], lanes.shape)
    for hh in (2, 1, 0):
        out = jnp.where(lanes < 64 * (hh + 1), v[:, hh:hh + 1], out)
    return out


def _ssd_fwd(xbc, dt_g, ac_g, ac_t, *, name="ssd_fwd", dep=None):
    S = xbc.shape[0]
    nc = S // CHUNK
    Lc = CHUNK

    def body(x_ref, b_ref, c_ref, dt_ref, ac_ref, act_ref, *rest):
        y_ref, st_out_ref, st_ref = rest[-3:]
        g2 = pl.program_id(0)
        c = pl.program_id(1)

        @pl.when(c == 0)
        def _():
            st_ref[...] = jnp.zeros_like(st_ref)

        causal = _iota((Lc, Lc), 0) >= _iota((Lc, Lc), 1)
        lane256 = _iota((Lc, 256), 1)
        lane128 = _iota((Lc, 128), 1)
        row128 = _iota((128, 128), 0)
        for gg in range(SSD_GPS):
            g = SSD_GPS * g2 + gg
            bv = b_ref[:, 128 * gg:128 * (gg + 1)]
            cbf = c_ref[:, 128 * gg:128 * (gg + 1)].astype(BF16)
            cb = _dot(cbf, bv.astype(BF16), 1, 1)
            dtg, acg = dt_ref[gg], ac_ref[gg]
            ac_last = ac_ref[gg, pl.ds(Lc - 1, 1), :]
            dt4 = _expand4(dtg, lane256)
            ac4 = _expand4(acg, lane256)
            e4 = jnp.exp(ac4)
            xdb = (x_ref[:, 256 * gg:256 * (gg + 1)] * dt4).astype(BF16)
            st_out_ref[gg] = st_ref[gg]
            for p in range(2):
                xd_p = xdb[:, 128 * p:128 * (p + 1)]
                st_p = st_ref[gg, p]
                ys, sn, cds = [], [], []
                for q in range(2):
                    hh = 2 * p + q
                    a_col = acg[:, hh:hh + 1]
                    a_row = act_ref[pl.ds(4 * g + hh, 1), :]
                    dec = jnp.exp(jnp.where(causal, a_col - a_row, NEG))
                    w = (cb * dec).astype(BF16)
                    ys.append(_dot(w, xd_p))
                    al = ac_last[:, hh:hh + 1]
                    dte = jnp.exp(al - a_col)
                    sn.append(_dot(xd_p, (bv * dte).astype(BF16), 0, 0))
                    cds.append(jnp.exp(al))
                y_diag = jnp.where(lane128 < 64, ys[0], ys[1])
                y_off = _dot(cbf, st_p.astype(BF16), 1, 1) * e4[:, 128 * p:128 * (p + 1)]
                y_ref[:, 256 * gg + 128 * p:256 * gg + 128 * (p + 1)] = y_diag + y_off
                st_ref[gg, p] = jnp.where(row128 < 64, st_p * cds[0] + sn[0], st_p * cds[1] + sn[1])

    G = SSD_GPS
    per_g = lambda g, c: (g, c, 0)
    return pl.pallas_call(
        body, name=name, grid=(SSM_GROUPS // G, nc),
        in_specs=[pl.BlockSpec((Lc, 256 * G), lambda g, c: (c, g)),
                  pl.BlockSpec((Lc, 128 * G), lambda g, c: (c, 16 // G + g)),
                  pl.BlockSpec((Lc, 128 * G), lambda g, c: (c, 24 // G + g)),
                  pl.BlockSpec((G, Lc, 128), per_g), pl.BlockSpec((G, Lc, 128), per_g),
                  pl.BlockSpec((SSM_HEADS, Lc), lambda g, c: (0, c))] + ([] if dep is None else [pl.BlockSpec(memory_space=pl.ANY)]),
        out_specs=[pl.BlockSpec((Lc, 256 * G), lambda g, c: (c, g)),
                   pl.BlockSpec((G, None, 2, 128, 128), lambda g, c: (g, c, 0, 0, 0))],
        out_shape=[jax.ShapeDtypeStruct((S, 2048), F32), jax.ShapeDtypeStruct((SSM_GROUPS, nc, 2, 128, 128), F32)],
        scratch_shapes=[pltpu.VMEM((G, 2, 128, 128), F32)], compiler_params=_cp(2),
    )(xbc, xbc, xbc, dt_g, ac_g, ac_t, *([] if dep is None else [dep]))


def _ssd_bwd(xbc, dt_g, ac_g, ac_t, states, dy, dexp, *, name="ssd_bwd", dep=None):
    S = xbc.shape[0]
    nc = S // CHUNK
    Lc = CHUNK

    def body(x_ref, b_ref, c_ref, dt_ref, ac_ref, act_ref, st_ref, dy_ref, d_ref, *rest):
        dx_ref, db_ref, dc_ref, dh_ref, ds_ref = rest[-5:]
        g2 = pl.program_id(0)
        cc = pl.program_id(1)

        @pl.when(cc == 0)
        def _():
            ds_ref[...] = jnp.zeros_like(ds_ref)

        causal = _iota((Lc, Lc), 0) >= _iota((Lc, Lc), 1)
        lane256 = _iota((Lc, 256), 1)
        lane128 = _iota((Lc, 128), 1)
        row128 = _iota((128, 128), 0)
        ind_rows = _iota((256, 128), 0) >> 6
        ind_cols = _iota((256, 128), 1)
        ind_a = (ind_rows == ind_cols).astype(BF16)
        ind_b = (ind_rows + 4 == ind_cols).astype(BF16)
        for gg in range(SSD_GPS):
            g = SSD_GPS * g2 + gg
            bv = b_ref[:, 128 * gg:128 * (gg + 1)]
            cv = c_ref[:, 128 * gg:128 * (gg + 1)]
            bbf, cbf = bv.astype(BF16), cv.astype(BF16)
            cb = _dot(cbf, bbf, 1, 1)
            dtg, acg = dt_ref[gg], ac_ref[gg]
            ac_last = ac_ref[gg, pl.ds(Lc - 1, 1), :]
            dt4 = _expand4(dtg, lane256)
            ac4 = _expand4(acg, lane256)
            acl4 = _expand4(ac_last, _iota((1, 256), 1))
            e4 = jnp.exp(ac4)
            dte4 = jnp.exp(acl4 - ac4)
            xv = x_ref[:, 256 * gg:256 * (gg + 1)]
            xd = xv * dt4
            xdb = xd.astype(BF16)
            dyv = dy_ref[:, 256 * gg:256 * (gg + 1)]
            dcb = jnp.zeros((Lc, Lc), F32)
            dc_acc = jnp.zeros((Lc, 128), F32)
            db_acc = jnp.zeros((Lc, 128), F32)
            u_parts, dxd_parts, ends = [], [], []
            for p in range(2):
                sl = slice(128 * p, 128 * (p + 1))
                xd_p, xdb_p, dy_p = xd[:, sl], xdb[:, sl], dyv[:, sl]
                dyb_p = dy_p.astype(BF16)
                e_p, dte_p = e4[:, sl], dte4[:, sl]
                sp = st_ref[gg, p]
                spb = sp.astype(BF16)
                dsn = ds_ref[gg, p]
                dsnb = dsn.astype(BF16)
                yds, dxds, cds = [], [], []
                for q in range(2):
                    hh = 2 * p + q
                    a_col = acg[:, hh:hh + 1]
                    a_row = act_ref[pl.ds(4 * g + hh, 1), :]
                    dec = jnp.exp(jnp.where(causal, a_col - a_row, NEG))
                    w = (cb * dec).astype(BF16)
                    head = (lane128 < 64) if q == 0 else (lane128 >= 64)
                    dym = jnp.where(head, dyb_p, jnp.zeros_like(dyb_p))
                    dw = _dot(dym, xdb_p, 1, 1)
                    dcb = dcb + dw * dec
                    yds.append(_dot(w, xdb_p))
                    dxds.append(_dot(w, dyb_p, 0, 0))
                    cds.append(jnp.exp(ac_last[:, hh:hh + 1]))
                y_diag = jnp.where(lane128 < 64, yds[0], yds[1])
                dxd_diag = jnp.where(lane128 < 64, dxds[0], dxds[1])
                y_off = _dot(cbf, spb, 1, 1) * e_p
                dgp = dy_p * e_p
                dgb = dgp.astype(BF16)
                dc_acc = dc_acc + _dot(dgb, spb)
                dsp = _dot(dgb, cbf, 0, 0)
                cd_col = jnp.where(row128[:, 0:1] < 64, cds[0], cds[1])
                qm = _dot(bbf, dsnb, 1, 1)
                dxd_state = dte_p * qm
                db_acc = db_acc + _dot((xd_p * dte_p).astype(BF16), dsnb)
                t_p = xd_p * dxd_state
                prod = dsn * sp
                e0 = jnp.sum(jnp.sum(jnp.where(row128 < 64, prod, 0.0), axis=1, keepdims=True), axis=0, keepdims=True)
                e1 = jnp.sum(jnp.sum(jnp.where(row128 >= 64, prod, 0.0), axis=1, keepdims=True), axis=0, keepdims=True)
                tcol = jnp.sum(t_p, axis=0, keepdims=True)
                lane1 = _iota((1, 128), 1)
                t0 = jnp.sum(jnp.where(lane1 < 64, tcol, 0.0), axis=1, keepdims=True)
                t1 = jnp.sum(jnp.where(lane1 >= 64, tcol, 0.0), axis=1, keepdims=True)
                ends.append(e0 * cds[0] + t0)
                ends.append(e1 * cds[1] + t1)
                ds_ref[gg, p] = dsn * cd_col + dsp
                u_parts.append(dyb_p.astype(F32) * y_diag - xdb_p.astype(F32) * dxd_diag + dy_p * y_off - t_p)
                dxd_parts.append(dxd_diag + dxd_state)
            dxd = jnp.concatenate(dxd_parts, axis=1)
            u_all = jnp.concatenate(u_parts, axis=1)
            dx_ref[:, 256 * gg:256 * (gg + 1)] = dxd * dt4 + dyv * d_ref[:, 256 * gg:256 * (gg + 1)]
            dcbb = dcb.astype(BF16)
            dc_ref[:, 128 * gg:128 * (gg + 1)] = dc_acc + _dot(dcbb, bbf)
            db_ref[:, 128 * gg:128 * (gg + 1)] = db_acc + _dot(dcbb, cbf, 0, 0)
            lane = _iota((Lc, 128), 1)
            endv = jnp.zeros((Lc, 128), F32)
            for hh in range(4):
                endv = jnp.where(lane == 8 + hh, ends[hh], endv)
            dh_ref[gg] = _dot3(dxd * xv, ind_a) + _dot3(u_all, ind_b) + endv

    G = SSD_GPS
    rev = lambda c: nc - 1 - c
    per_g = lambda g, c: (g, rev(c), 0)
    return pl.pallas_call(
        body, name=name, grid=(SSM_GROUPS // G, nc),
        in_specs=[pl.BlockSpec((Lc, 256 * G), lambda g, c: (rev(c), g)),
                  pl.BlockSpec((Lc, 128 * G), lambda g, c: (rev(c), 16 // G + g)),
                  pl.BlockSpec((Lc, 128 * G), lambda g, c: (rev(c), 24 // G + g)),
                  pl.BlockSpec((G, Lc, 128), per_g), pl.BlockSpec((G, Lc, 128), per_g),
                  pl.BlockSpec((SSM_HEADS, Lc), lambda g, c: (0, rev(c))),
                  pl.BlockSpec((G, None, 2, 128, 128), lambda g, c: (g, rev(c), 0, 0, 0)),
                  pl.BlockSpec((Lc, 256 * G), lambda g, c: (rev(c), g)),
                  pl.BlockSpec((1, 256 * G), lambda g, c: (0, g))] + ([] if dep is None else [pl.BlockSpec(memory_space=pl.ANY)]),
        out_specs=[pl.BlockSpec((Lc, 256 * G), lambda g, c: (rev(c), g)),
                   pl.BlockSpec((Lc, 128 * G), lambda g, c: (rev(c), g)),
                   pl.BlockSpec((Lc, 128 * G), lambda g, c: (rev(c), g)),
                   pl.BlockSpec((G, Lc, 128), per_g)],
        out_shape=[jax.ShapeDtypeStruct((S, 2048), F32), jax.ShapeDtypeStruct((S, 1024), F32),
                   jax.ShapeDtypeStruct((S, 1024), F32), jax.ShapeDtypeStruct((SSM_GROUPS, S, 128), F32)],
        scratch_shapes=[pltpu.VMEM((G, 2, 128, 128), F32)], compiler_params=_cp(2),
    )(xbc, xbc, xbc, dt_g, ac_g, ac_t, states, dy, dexp, *([] if dep is None else [dep]))


def _ssd_post(dhead, dt_g, sg_g, alog_g, *, name="ssd_post"):
    S = dhead.shape[1]
    nc = S // CHUNK
    Lc = CHUNK

    def body(dh_ref, dt_ref, sg_ref, al_ref, o_ref, s_ref):
        @pl.when(pl.program_id(0) == 0)
        def _():
            s_ref[...] = jnp.zeros_like(s_ref)

        lane = _iota((Lc, 128), 1)
        row = _iota((Lc, 128), 0)
        row8 = _iota((8, 128), 0)
        out = jnp.zeros((Lc, 128), F32)
        for g in range(SSM_GROUPS):
            dh = dh_ref[g]
            a_neg = -jnp.exp(al_ref[g])
            dac = jnp.where(lane < 4, pltpu.roll(dh, 124, axis=1), 0.0)
            end = jnp.where(lane < 4, pltpu.roll(dh, 120, axis=1), 0.0)
            k = 1
            while k < Lc:
                dac = dac + jnp.where(row < Lc - k, pltpu.roll(dac, Lc - k, axis=0), 0.0)
                k *= 2
            da = dac + end
            ddt = jnp.where(lane < 4, da * a_neg + dh, 0.0)
            ddtr = ddt * sg_ref[g]
            out = out + (ddtr if g == 0 else pltpu.roll(ddtr, 4 * g, axis=1))
            dal = jnp.sum(da * dt_ref[g], axis=0, keepdims=True) * a_neg
            dbias = jnp.sum(ddtr, axis=0, keepdims=True)
            part = jnp.where(row8 == 0, dal, jnp.where(row8 == 1, dbias, 0.0))
            s_ref[g] += part
        o_ref[...] = out.astype(BF16)

    grp = pl.BlockSpec((SSM_GROUPS, Lc, 128), lambda c: (0, c, 0))
    whole = lambda r: pl.BlockSpec((SSM_GROUPS, r, 128), lambda c: (0, 0, 0))
    return pl.pallas_call(
        body, name=name, grid=(nc,), in_specs=[grp, grp, grp, whole(1)],
        out_specs=[pl.BlockSpec((Lc, 128), lambda c: (c, 0)), whole(8)],
        out_shape=[jax.ShapeDtypeStruct((S, 128), BF16), jax.ShapeDtypeStruct((SSM_GROUPS, 8, 128), F32)],
        compiler_params=_cp(1),
    )(dhead, dt_g, sg_g, alog_g)


def _gate_fwd(y, xbc, zx, dexp, gn, *, name="gate_fwd", tr=256, dep=None):
    S = y.shape[0]
    W = 2048
    gw = W // SSM_GROUPS

    def body(y_ref, x_ref, z_ref, d_ref, g_ref, *rest):
        o_ref = rest[-1]
        z = z_ref[...]
        u = (y_ref[...] + x_ref[...] * d_ref[...]) * (z * _sigmoid(z))
        gv = g_ref[...]
        for q in range(SSM_GROUPS):
            sl = slice(gw * q, gw * (q + 1))
            uq = u[:, sl]
            r = lax.rsqrt(jnp.mean(uq * uq, axis=-1, keepdims=True) + EPS)
            o_ref[:, sl] = (uq * r * gv[:, sl]).astype(BF16)

    row = pl.BlockSpec((tr, W), lambda i: (i, 0))
    vec = pl.BlockSpec((1, W), lambda i: (0, 0))
    return pl.pallas_call(
        body, name=name, grid=(S // tr,),
        in_specs=[row, row, row, vec, vec] + ([] if dep is None else [pl.BlockSpec(memory_space=pl.ANY)]), out_specs=row,
        out_shape=jax.ShapeDtypeStruct((S, W), BF16), compiler_params=_cp(1),
    )(y, xbc, zx, dexp, gn, *([] if dep is None else [dep]))


def _gate_bwd(y, xbc, zx, dexp, gn, dout, *, name="gate_bwd", tr=256):
    S = y.shape[0]
    W = 2048
    gw = W // SSM_GROUPS
    steps = S // tr

    def body(y_ref, x_ref, z_ref, d_ref, g_ref, do_ref, dy_ref, dz_ref, dg_ref, dd_ref, acc_ref):
        i = pl.program_id(0)

        @pl.when(i == 0)
        def _():
            acc_ref[...] = jnp.zeros_like(acc_ref)

        z = z_ref[...]
        sg = _sigmoid(z)
        sz = z * sg
        xs = x_ref[...]
        yt = y_ref[...] + xs * d_ref[...]
        u = yt * sz
        gv = g_ref[...]
        do = do_ref[...]
        dgs = []
        for q in range(SSM_GROUPS):
            sl = slice(gw * q, gw * (q + 1))
            uq = u[:, sl]
            r = lax.rsqrt(jnp.mean(uq * uq, axis=-1, keepdims=True) + EPS)
            uh = uq * r
            dq = do[:, sl]
            duh = dq * gv[:, sl]
            duq = r * (duh - uh * jnp.mean(duh * uh, axis=-1, keepdims=True))
            dgs.append(jnp.sum(dq * uh, axis=0, keepdims=True))
            dyt = duq * sz[:, sl]
            dy_ref[:, sl] = dyt
            dz_ref[:, sl] = (duq * yt[:, sl] * (sg[:, sl] * (1.0 + z[:, sl] * (1.0 - sg[:, sl])))).astype(BF16)
            acc_ref[:, sl] += jnp.sum(dyt * xs[:, sl], axis=0, keepdims=True)
        dg = jnp.concatenate(dgs, axis=1)

        @pl.when(i == 0)
        def _():
            dg_ref[...] = dg

        @pl.when(i > 0)
        def _():
            dg_ref[...] += dg

        @pl.when(i == steps - 1)
        def _():
            ind = ((_iota((W, 128), 0) >> 6) == _iota((W, 128), 1)).astype(BF16)
            dd_ref[...] = _dot3(jnp.broadcast_to(acc_ref[...], (8, W)), ind)[0:1, :]

    row = pl.BlockSpec((tr, W), lambda i: (i, 0))
    vec = pl.BlockSpec((1, W), lambda i: (0, 0))
    return pl.pallas_call(
        body, name=name, grid=(steps,), in_specs=[row, row, row, vec, vec, row],
        out_specs=[row, row, vec, pl.BlockSpec((1, 128), lambda i: (0, 0))],
        out_shape=[jax.ShapeDtypeStruct((S, W), F32), jax.ShapeDtypeStruct((S, W), BF16),
                   jax.ShapeDtypeStruct((1, W), F32), jax.ShapeDtypeStruct((1, 128), F32)],
        scratch_shapes=[pltpu.VMEM((1, W), F32)], compiler_params=_cp(1),
    )(y, xbc, zx, dexp, gn, dout)


def _rope_cs(posf, *, name="rope_tables", tr=256):
    S = posf.shape[0]

    def body(p_ref, c_ref, s_ref):
        j = (_iota((tr, 128), 1) & 31).astype(F32)
        ang = p_ref[...] * jnp.exp(j * (-math.log(ROPE_THETA) / 32.0))
        c_ref[...] = jnp.cos(ang)
        s_ref[...] = jnp.sin(ang)

    blk = pl.BlockSpec((tr, 128), lambda i: (i, 0))
    return pl.pallas_call(
        body, name=name, grid=(S // tr,), in_specs=[pl.BlockSpec((tr, 1), lambda i: (i, 0))], out_specs=[blk, blk],
        out_shape=[jax.ShapeDtypeStruct((S, 128), F32)] * 2, compiler_params=_cp(1),
    )(posf)


def _rope_tables(c_ref, s_ref, shape):
    reps = shape[1] // 128
    return jnp.tile(c_ref[...], (1, reps)), jnp.tile(s_ref[...], (1, reps)), (_iota(shape, 1) & 63) < 32


def _hn_inds(W):
    ind = ((_iota((W, 128), 0) >> 6) == _iota((W, 128), 1)).astype(BF16)
    ind_t = ((_iota((128, W), 1) >> 6) == _iota((128, W), 0)).astype(BF16)
    return ind, ind_t


def _hnrope_fwd(xin, col0, W, gain_w, rope, *, name, tr=256):
    S = xin.shape[0]
    off = col0 // W
    nh = W // HEAD

    def body(x_ref, g_ref, c_ref, s_ref, o_ref):
        x = x_ref[...]
        ind, ind_t = _hn_inds(W)
        r = lax.rsqrt(_dot3(x * x, ind) * (1.0 / HEAD) + EPS)
        xn = x * _dot3(r, ind_t) * g_ref[...]
        cs, sn, half = _rope_tables(c_ref, s_ref, (tr, W))
        rot = jnp.where(half, -pltpu.roll(xn, W - 32, axis=1), pltpu.roll(xn, 32, axis=1))
        out = (xn * cs + rot * sn).astype(BF16)
        for h in range(nh):
            o_ref[h] = out[:, HEAD * h:HEAD * (h + 1)]

    tab = pl.BlockSpec((tr, 128), lambda i: (i, 0))
    return pl.pallas_call(
        body, name=name, grid=(S // tr,),
        in_specs=[pl.BlockSpec((tr, W), lambda i: (i, off)), pl.BlockSpec((1, W), lambda i: (0, 0)), tab, tab],
        out_specs=pl.BlockSpec((nh, tr, HEAD), lambda i: (0, i, 0)), out_shape=jax.ShapeDtypeStruct((nh, S, HEAD), BF16),
        compiler_params=_cp(1),
    )(xin, gain_w, *rope)


def _hnrope_bwd(xin, col0, W, gain_w, rope, dout, *, name, tr=256):
    S = xin.shape[0]
    off = col0 // W
    steps = S // tr
    nh = W // HEAD

    def body(x_ref, g_ref, c_ref, s_ref, do_ref, dx_ref, cs_ref, dg_ref, acc_ref):
        i = pl.program_id(0)
        x = x_ref[...]
        ind, ind_t = _hn_inds(W)
        r = lax.rsqrt(_dot3(x * x, ind) * (1.0 / HEAD) + EPS)
        rw = _dot3(r, ind_t)
        xh = x * rw
        cs, sn, half = _rope_tables(c_ref, s_ref, (tr, W))
        do = jnp.concatenate([do_ref[h] for h in range(nh)], axis=1).astype(F32)
        gs = do * sn
        g1 = do * cs + jnp.where(half, pltpu.roll(gs, W - 32, axis=1), -pltpu.roll(gs, 32, axis=1))
        dxh = g1 * g_ref[...]
        t = _dot3(dxh * xh, ind) * (1.0 / HEAD)
        dx = rw * (dxh - xh * _dot3(t, ind_t))
        dx_ref[...] = dx.astype(BF16)
        cpart = jnp.sum(dx, axis=0, keepdims=True)
        gpart = jnp.sum(g1 * xh, axis=0, keepdims=True)

        @pl.when(i == 0)
        def _():
            cs_ref[...] = cpart
            acc_ref[...] = gpart

        @pl.when(i > 0)
        def _():
            cs_ref[...] += cpart
            acc_ref[...] += gpart

        @pl.when(i == steps - 1)
        def _():
            fold = ((_iota((W, 128), 0) & 63) == _iota((W, 128), 1)).astype(BF16)
            dg_ref[...] = _dot3(jnp.broadcast_to(acc_ref[...], (8, W)), fold)[0:1, :]

    tab = pl.BlockSpec((tr, 128), lambda i: (i, 0))
    return pl.pallas_call(
        body, name=name, grid=(steps,),
        in_specs=[pl.BlockSpec((tr, W), lambda i: (i, off)), pl.BlockSpec((1, W), lambda i: (0, 0)), tab, tab,
                  pl.BlockSpec((nh, tr, HEAD), lambda i: (0, i, 0))],
        out_specs=[pl.BlockSpec((tr, W), lambda i: (i, 0)), pl.BlockSpec((1, W), lambda i: (0, 0)),
                   pl.BlockSpec((1, 128), lambda i: (0, 0))],
        out_shape=[jax.ShapeDtypeStruct((S, W), BF16), jax.ShapeDtypeStruct((1, W), F32),
                   jax.ShapeDtypeStruct((1, 128), F32)],
        scratch_shapes=[pltpu.VMEM((1, W), F32)], compiler_params=_cp(1),
    )(xin, gain_w, *rope, dout)


def _attn_band():
    qi = jnp.arange(ATT_G * WINDOW)[:, None] % WINDOW
    ki = jnp.arange(2 * WINDOW)[None, :]
    rel = qi + WINDOW - ki
    ok = (rel >= 0) & (rel < WINDOW)
    return jnp.stack([jnp.where(ok & (ki >= WINDOW), 0.0, NEG), jnp.where(ok, 0.0, NEG)]).astype(F32)


def _attn_probs(q, kb, sink_ref, band_ref, h, i):
    s = _dot(q, kb, 1, 1) * (HEAD ** -0.5) + band_ref[jnp.minimum(i, 1)]
    r1 = _iota((4 * WINDOW, 1), 0)
    sink = jnp.where(r1 < WINDOW, sink_ref[4 * h], jnp.where(r1 < 2 * WINDOW, sink_ref[4 * h + 1],
                     jnp.where(r1 < 3 * WINDOW, sink_ref[4 * h + 2], sink_ref[4 * h + 3])))
    m = jnp.maximum(jnp.max(s, axis=1, keepdims=True), sink)
    p = jnp.exp(s - m)
    ps = jnp.exp(sink - m)
    inv = 1.0 / (jnp.sum(p, axis=1, keepdims=True) + ps)
    return p * inv, ps * inv


ATT_HPS = 4
_BAND = pl.BlockSpec((2, ATT_G * WINDOW, 2 * WINDOW), lambda h, i: (0, 0, 0))


def _attn_specs(S):
    qspec = pl.BlockSpec((ATT_HPS, ATT_G, WINDOW, HEAD), lambda h, i: (h, 0, i, 0))
    cur = pl.BlockSpec((ATT_HPS, WINDOW, HEAD), lambda h, i: (h, i, 0))
    prev = pl.BlockSpec((ATT_HPS, WINDOW, HEAD), lambda h, i: (h, jnp.maximum(i - 1, 0), 0))
    tok = pl.BlockSpec((WINDOW, ATT_HPS * ATT_G * HEAD), lambda h, i: (i, h))
    return qspec, cur, prev, tok


def _attn_fwd(qh, kh, vh, sinks, *, name="attn_fwd"):
    S = kh.shape[1]
    nb = S // WINDOW

    def body(s_ref, band_ref, q_ref, kc_ref, kp_ref, vc_ref, vp_ref, o_ref):
        h2, i = pl.program_id(0), pl.program_id(1)
        outs = []
        for hh in range(ATT_HPS):
            q = q_ref[hh].reshape(ATT_G * WINDOW, HEAD)
            kb = jnp.concatenate([kp_ref[hh], kc_ref[hh]], axis=0)
            vb = jnp.concatenate([vp_ref[hh], vc_ref[hh]], axis=0)
            probs, _ = _attn_probs(q, kb, s_ref, band_ref, ATT_HPS * h2 + hh, i)
            o = _dot(probs.astype(BF16), vb).astype(BF16)
            outs += [o[WINDOW * g:WINDOW * (g + 1)] for g in range(ATT_G)]
        o_ref[...] = jnp.concatenate(outs, axis=1)

    qspec, cur, prev, tok = _attn_specs(S)
    return pl.pallas_call(
        body, name=name, grid=(ATT_KV // ATT_HPS, nb),
        in_specs=[pl.BlockSpec(memory_space=pltpu.SMEM), _BAND, qspec, cur, prev, cur, prev], out_specs=tok,
        out_shape=jax.ShapeDtypeStruct((S, ATT_KV * ATT_G * HEAD), BF16), compiler_params=_cp(2),
    )(sinks, _attn_band(), qh, kh, kh, vh, vh)


def _attn_bwd(qh, kh, vh, sinks, doh, *, name="attn_bwd"):
    S = kh.shape[1]
    nb = S // WINDOW

    def body(s_ref, band_ref, q_ref, kc_ref, kp_ref, vc_ref, vp_ref, do_ref, dq_ref, dk_ref, dv_ref, dsk_ref):
        h2, i = pl.program_id(0), pl.program_id(1)

        @pl.when(i == 0)
        def _():
            dk_ref[...] = jnp.zeros_like(dk_ref)
            dv_ref[...] = jnp.zeros_like(dv_ref)
            dsk_ref[...] = jnp.zeros_like(dsk_ref)

        dov = do_ref[...]
        cur = pl.multiple_of(i * WINDOW, WINDOW)
        lane = _iota((8, 128), 1)
        row = _iota((8, 128), 0)
        scale = HEAD ** -0.5
        for hh in range(ATT_HPS):
            q = q_ref[hh].reshape(ATT_G * WINDOW, HEAD)
            do = jnp.concatenate([dov[:, HEAD * (ATT_G * hh + g):HEAD * (ATT_G * hh + g + 1)] for g in range(ATT_G)], axis=0)
            kb = jnp.concatenate([kp_ref[hh], kc_ref[hh]], axis=0)
            vb = jnp.concatenate([vp_ref[hh], vc_ref[hh]], axis=0)
            probs, psink = _attn_probs(q, kb, s_ref, band_ref, ATT_HPS * h2 + hh, i)
            dp = _dot(do, vb, 1, 1)
            delta = jnp.sum(probs * dp, axis=1, keepdims=True)
            ds = (probs * (dp - delta)).astype(BF16)
            dq_ref[hh] = (_dot(ds, kb) * scale).reshape(ATT_G, WINDOW, HEAD)
            dkb = _dot(ds, q, 0, 0) * scale
            dvb = _dot(probs.astype(BF16), do, 0, 0)
            dk_ref[hh, pl.ds(cur, WINDOW), :] += dkb[WINDOW:, :]
            dv_ref[hh, pl.ds(cur, WINDOW), :] += dvb[WINDOW:, :]
            prv = pl.multiple_of(jnp.maximum(i - 1, 0) * WINDOW, WINDOW)
            dk_ref[hh, pl.ds(prv, WINDOW), :] += dkb[:WINDOW, :]
            dv_ref[hh, pl.ds(prv, WINDOW), :] += dvb[:WINDOW, :]

            dsr = -psink * delta
            upd = jnp.zeros((8, 128), F32)
            for gq in range(ATT_G):
                v = jnp.sum(dsr[gq * WINDOW:(gq + 1) * WINDOW, :], axis=0, keepdims=True)
                upd = jnp.where((lane == gq) & (row == 0), v, upd)
            dsk_ref[hh] += upd

    qspec, cur, prev, tok = _attn_specs(S)
    full = pl.BlockSpec((ATT_HPS, S, HEAD), lambda h, i: (h, 0, 0))
    return pl.pallas_call(
        body, name=name, grid=(ATT_KV // ATT_HPS, nb),
        in_specs=[pl.BlockSpec(memory_space=pltpu.SMEM), _BAND, qspec, cur, prev, cur, prev, tok],
        out_specs=[qspec, full, full, pl.BlockSpec((ATT_HPS, 8, 128), lambda h, i: (h, 0, 0))],
        out_shape=[jax.ShapeDtypeStruct((ATT_KV, ATT_G, S, HEAD), F32), jax.ShapeDtypeStruct((ATT_KV, S, HEAD), F32),
                   jax.ShapeDtypeStruct((ATT_KV, S, HEAD), F32), jax.ShapeDtypeStruct((ATT_KV, 8, 128), F32)],
        compiler_params=_cp(2),
    )(sinks, _attn_band(), qh, kh, kh, vh, vh, doh)


def _heads_major(t, nh):
    S = t.shape[0]
    return t.reshape(S, nh, HEAD).transpose(1, 0, 2)


def _tokens_major(t):
    nh, S, _ = t.shape
    return t.transpose(1, 0, 2).reshape(S, nh * HEAD)


class _NoComm:
    def late_start(self, part, after):
        return None

    def late_arrived(self, part, after):
        return None

    def late_weights(self, w, after, part):
        return w

    def advance(self, after, group=None, tensors=None):
        return None


def _local_step(x, posf, target, w, comm=None):
    S, D = x.shape
    gr = {}
    comm = comm or _NoComm()

    zx, h1 = _norm_mm(x, w["a_norm"], w["w_zx"], name="in_proj_zx", dep=w.get("dep"))
    dtr = _mm(h1, w["w_dt"], name="in_proj_dt")
    xbc = _conv_silu_fwd(zx, 2048, 4096, w["a_conv_w"], w["a_conv_b"], name="a_conv_f")
    dt_g, ac_g, sg_g, ac_t = _ssd_prep(dtr, w["a_dt_bias"], w["a_A_log"])
    y_ssd, states = _ssd_fwd(xbc, dt_g, ac_g, ac_t, dep=comm.late_start(1, xbc))
    yg = _gate_fwd(y_ssd, xbc, zx, w["a_Dexp"], w["a_gnorm"], dep=comm.late_arrived(0, [y_ssd]))
    w = comm.late_weights(w, yg, 0)
    x1 = _mm(yg, w["a_out_proj"], res=x, name="out_proj")

    FW = w["f_w_in"][0].shape[2]

    def ffn_fwd(xin, l, loss_target=None):
        u, h = _norm_mm(xin, w["f_norm"][l], w["f_w_in"][l], name=f"f_in{l}", N=N_CHIPS * FW, tn=FW,
                        b_spec=pl.BlockSpec((None, D, FW), lambda i, j: (j, 0, 0)))
        a = _ffn_act_fwd(u, w["f_conv_w"][l], w["f_conv_b"][l], name=f"f_act_f{l}")
        dep = comm.late_arrived(1, [u]) if l == 0 else None
        xo = _mm(a, w["f_w_down"][l], res=xin, tk=a.shape[1], name=f"f_down{l}", target=loss_target, dep=dep)
        return xo, (h, u)

    x2, ffn0 = ffn_fwd(x1, 0)
    w = comm.late_weights(w, x2, 1)

    kv, hk = _norm_mm(x2, w["kv_norm"], w["w_kv"], bias=w["b_kv"], name="kv_proj")
    q, hq = _norm_mm(x2, w["b_norm"], w["w_q"], bias=w["b_q"], name="q_proj")
    rope = _rope_cs(posf)
    kr = _hnrope_fwd(kv, 0, 256, w["k_norm_w"], rope, name="k_rope_f")
    qr = _hnrope_fwd(q, 0, 1024, w["q_norm_w"], rope, name="q_rope_f")
    qh = qr.reshape(ATT_KV, ATT_G, S, HEAD)
    kh = kr
    vh = _heads_major(kv[:, 256:].astype(BF16), ATT_KV)
    att = _attn_fwd(qh, kh, vh, w["sinks"])
    x3 = _mm(att, w["w_o"], bias=w["b_o"], res=x2, name="o_proj")
    (dy, loss_part), ffn1 = ffn_fwd(x3, 1, target)

    def ffn_bwd(xin, l, saved, dyo, want_colsum, dep=None):
        h, u = saved
        da = _mm(dyo, w["f_w_down"][l], tb=True, name=f"f_down_dx{l}", dep=dep)
        du, dcw, dcb, a = _ffn_act_bwd(u, w["f_conv_w"][l], w["f_conv_b"][l], da, name=f"f_act_b{l}")
        dw_down = _mm(a, dyo, ta=True, out_dtype=BF16, name=f"f_down_dw{l}")
        dw_in = _mm(h, du, ta=True, out_dtype=BF16, name=f"f_in_dw{l}", dims=(D, N_CHIPS * FW, S), tm=D, tn=FW, tk=S,
                    b_spec=pl.BlockSpec((None, S, FW), lambda i, j, k: (j // 2, 0, j % 2)),
                    o_spec=pl.BlockSpec((None, D, FW), lambda i, j, k: (j, i, 0)), o_shape=(N_CHIPS, D, FW))
        ts = _pick(S, (512, 256))
        pieces = [(pl.BlockSpec((None, ts, FW), lambda i, j, k, q=q: (q // 2, i, q % 2)),
                   pl.BlockSpec((None, D, FW), lambda i, j, k, q=q: (q, 0, 0), pipeline_mode=pl.Buffered(1)))
                  for q in range(N_CHIPS)]
        outs = _mm(du, w["f_w_in"][l], tb=True, name=f"f_in_dx{l}", dims=(S, D, FW), tm=ts, tn=D, tk=FW,
                   a_spec=pieces[0][0], b_spec=pieces[0][1], more=pieces[1:],
                   rms=(xin, w["f_norm"][l], dyo), rms_colsum=want_colsum)
        g = dict(f_norm=outs[1], f_w_in=dw_in, f_conv_w=dcw, f_conv_b=dcb, f_w_down=dw_down)
        return outs[0], g, (outs[2] if want_colsum else None)

    dx3, gr["ffn1"], db_o = ffn_bwd(x3, 1, ffn1, dy, True)
    gr["b_o"] = db_o
    gr["w_o"] = _mm(att, dx3, ta=True, out_dtype=BF16, name="o_proj_dw")
    datt = _mm(dx3, w["w_o"], tb=True, out_dtype=BF16, name="o_proj_dx")
    dqh, dkh, dvh, dsk = _attn_bwd(qh, kh, vh, w["sinks"], datt)
    gr["sinks"] = dsk[:, 0, :4].reshape(1, 16)
    dv = _tokens_major(dvh).astype(BF16)
    dq, db_q, dqn = _hnrope_bwd(q, 0, 1024, w["q_norm_w"], rope, dqh.reshape(16, S, HEAD), name="q_rope_b")
    dk, db_k, dkn = _hnrope_bwd(kv, 0, 256, w["k_norm_w"], rope, dkh, name="k_rope_b")
    gr["q_norm"], gr["k_norm"] = dqn[:, :HEAD], dkn[:, :HEAD]
    gr["b_q"] = db_q
    gr["b_kv"] = jnp.concatenate([db_k, _colsum(dv, name="dv_colsum")], axis=1)
    dkv = jnp.concatenate([dk, dv], axis=1)
    gr["w_q"] = _mm(hq, dq, ta=True, out_dtype=BF16, name="q_proj_dw")
    gr["w_kv"] = _mm(hk, dkv, ta=True, out_dtype=BF16, name="kv_proj_dw")
    tok = comm.advance([gr["w_kv"]], 1, dict(f_down1=gr["ffn1"]["f_w_down"], f_in1=gr["ffn1"]["f_w_in"], w_o=gr["w_o"],
                                             w_q=gr["w_q"], w_kv=gr["w_kv"]))
    tsm = _pick(S, (512, 256))
    dx2, gr["b_norm"] = _mm(dq, w["w_q"], tb=True, name="q_proj_dx", dep=tok, tm=tsm, tn=D, rms=(x2, w["b_norm"], dx3))
    dx2, gr["kv_norm"] = _mm(dkv, w["w_kv"], tb=True, name="kv_proj_dx", tm=tsm, tn=D, rms=(x2, w["kv_norm"], dx2))

    dx1, gr["ffn0"], _ = ffn_bwd(x1, 0, ffn0, dx2, False, dep=comm.advance([dx2]))

    gr["a_out_proj"] = _mm(yg, dx1, ta=True, out_dtype=BF16, name="out_proj_dw")
    tok = comm.advance([dx1, gr["a_out_proj"]], 2,
                       dict(f_down0=gr["ffn0"]["f_w_down"], f_in0=gr["ffn0"]["f_w_in"], out_proj=gr["a_out_proj"]))
    dyg = _mm(dx1, w["a_out_proj"], tb=True, name="out_proj_dx", dep=tok)
    dy_ssd, dz, gr["a_gnorm"], dD = _gate_bwd(y_ssd, xbc, zx, w["a_Dexp"], w["a_gnorm"], dyg)
    gr["a_D"] = dD[:, :SSM_HEADS]
    dxs, dB, dC, dhead = _ssd_bwd(xbc, dt_g, ac_g, ac_t, states, dy_ssd, w["a_Dexp"], dep=comm.advance([dy_ssd]))
    ddtr, dsmall = _ssd_post(dhead, dt_g, sg_g, w["a_A_log_g"])
    gr["a_A_log"] = dsmall[:, 0, :4].reshape(1, SSM_HEADS)
    gr["a_dt_bias"] = dsmall[:, 1, :4].reshape(1, SSM_HEADS)
    dxbc, gr["a_conv_w"], gr["a_conv_b"] = _conv_silu_bwd(
        zx, 2048, 4096, w["a_conv_w"], w["a_conv_b"], [(dxs, 0), (dB, 2048), (dC, 3072)], name="a_conv_b")
    gr["in_proj"] = _in_proj_dw(h1, dz, dxbc, ddtr).T
    tok = comm.advance([dxbc], 3, dict(in_proj=gr["in_proj"].reshape(D, N_CHIPS, -1).transpose(1, 0, 2)))
    ts = _pick(S, (512, 256))
    once = pl.Buffered(1)
    wblk = lambda q: pl.BlockSpec((D, 2048), lambda i, j, k: (0, q), pipeline_mode=once)
    dx0, gr["a_norm"] = _mm(
        dz, w["w_zx"], tb=True, name="in_proj_dx", dims=(S, D, 2048), tm=ts, tn=D, tk=2048,
        a_spec=pl.BlockSpec((ts, 2048), lambda i, j, k: (i, 0)), b_spec=wblk(0),
        more=[(dxbc, pl.BlockSpec((ts, 2048), lambda i, j, k: (i, 0)), w["w_zx"], wblk(1)),
              (dxbc, pl.BlockSpec((ts, 2048), lambda i, j, k: (i, 1)), w["w_zx"], wblk(2)),
              (ddtr, pl.BlockSpec((ts, 128), lambda i, j, k: (i, 0)), w["w_dt"],
               pl.BlockSpec((D, 128), lambda i, j, k: (0, 0), pipeline_mode=once))],
        rms=(x, w["a_norm"], dx1), dep=tok)
    tok = comm.advance([dx0])
    return loss_part, dx0, gr, tok


def _prep_small(full, w):
    w["a_norm"] = full["a_norm"]
    w["a_conv_w"] = full["a_conv_w"][0]
    w["a_conv_b"] = full["a_conv_b"]
    pad32 = lambda v: jnp.pad(v, ((0, 0), (0, 128 - SSM_HEADS)))
    w["a_dt_bias"] = pad32(full["a_dt_bias"])
    w["a_A_log"] = pad32(full["a_A_log"])
    w["a_A_log_g"] = jnp.pad(full["a_A_log"].reshape(SSM_GROUPS, 1, 4), ((0, 0), (0, 0), (0, 124)))
    w["a_Dexp"] = jnp.repeat(full["a_D"], HEAD, axis=1)
    w["a_gnorm"] = full["a_gnorm"]
    w["f_norm"] = [full["f_norm"][l:l + 1] for l in range(2)]
    w["f_conv_w"] = [full["f_conv_w"][l] for l in range(2)]
    w["f_conv_b"] = [full["f_conv_b"][l:l + 1] for l in range(2)]
    w["kv_norm"] = full["kv_norm"].reshape(1, -1)
    w["b_kv"] = full["b_kv"].reshape(1, -1)
    w["k_norm_w"] = jnp.tile(full["k_norm"].reshape(1, HEAD), (1, ATT_KV))
    w["b_norm"] = full["b_norm"]
    w["b_q"] = full["b_q"]
    w["q_norm_w"] = jnp.tile(full["q_norm"], (1, ATT_KV * ATT_G))
    w["sinks"] = full["sinks"].reshape(-1)
    w["b_o"] = full["b_o"]
    return w


def _split_in_proj(ip):
    return ip[:, :6144].astype(BF16), jnp.pad(ip[:, 6144:], ((0, 0), (0, 128 - SSM_HEADS))).astype(BF16)


def _join_in_proj(blocks, *, name="in_proj_join", tr=256):
    _, R, cw = blocks.shape
    zx_cols = 3 * 2048
    rest = N_CHIPS * cw - zx_cols

    def body(b_ref, zx_ref, dt_ref):
        whole = jnp.concatenate([b_ref[j] for j in range(N_CHIPS)], axis=1)
        zx_ref[...] = whole[:, :zx_cols]
        dt_ref[...] = jnp.concatenate([whole[:, zx_cols:], jnp.zeros((tr, 128 - rest), BF16)], axis=1)

    return pl.pallas_call(
        body, name=name, grid=(R // tr,), in_specs=[pl.BlockSpec((N_CHIPS, tr, cw), lambda i: (0, i, 0))],
        out_specs=[pl.BlockSpec((tr, zx_cols), lambda i: (i, 0)), pl.BlockSpec((tr, 128), lambda i: (i, 0))],
        out_shape=[jax.ShapeDtypeStruct((R, zx_cols), BF16), jax.ShapeDtypeStruct((R, 128), BF16)],
        compiler_params=_cp(1),
    )(blocks)


def _in_proj_dw(h, dz, dxbc, ddtr, *, name="in_proj_dw", tn=512):
    S, D = h.shape
    nz, nx = dz.shape[1] // tn, dxbc.shape[1] // tn
    N = dz.shape[1] + dxbc.shape[1] + SSM_HEADS

    def body(h_ref, z_ref, x_ref, t_ref, o_ref):
        j = pl.program_id(0)
        hb = h_ref[...].astype(BF16)

        @pl.when(j < nz)
        def _():
            o_ref[...] = _dot(z_ref[...].astype(BF16), hb, 0, 0).astype(BF16)

        @pl.when((j >= nz) & (j < nz + nx))
        def _():
            o_ref[...] = _dot(x_ref[...].astype(BF16), hb, 0, 0).astype(BF16)

        @pl.when(j == nz + nx)
        def _():
            o_ref[:128, :] = _dot(t_ref[...].astype(BF16), hb, 0, 0).astype(BF16)
            o_ref[128:, :] = jnp.zeros((tn - 128, D), BF16)

    return pl.pallas_call(
        body, name=name, grid=(nz + nx + 1,),
        in_specs=[pl.BlockSpec((S, D), lambda j: (0, 0), pipeline_mode=pl.Buffered(1)),
                  pl.BlockSpec((S, tn), lambda j: (0, jnp.minimum(j, nz - 1))),
                  pl.BlockSpec((S, tn), lambda j: (0, jnp.clip(j - nz, 0, nx - 1))),
                  pl.BlockSpec((S, 128), lambda j: (0, 0))],
        out_specs=pl.BlockSpec((tn, D), lambda j: (j, 0)),
        out_shape=jax.ShapeDtypeStruct((N, D), BF16), compiler_params=_cp(1),
    )(h, dz, dxbc, ddtr)


def _prep_weights(full):
    w = _prep_small(full, {})
    w["w_zx"], w["w_dt"] = _split_in_proj(full["a_in_proj"][0])
    w["a_out_proj"] = full["a_out_proj"][0].astype(BF16)
    w["f_w_in"] = [full["f_w_in"][l].reshape(1024, N_CHIPS, -1).transpose(1, 0, 2).astype(BF16) for l in range(2)]
    w["f_w_down"] = [full["f_w_down"][l].astype(BF16) for l in range(2)]
    w["w_kv"] = full["w_kv"].astype(BF16)
    w["w_q"] = full["w_q"][0].astype(BF16)
    w["w_o"] = full["w_o"][0].astype(BF16)
    return w


def _small_grads(gr):
    g = {}
    g["a_norm"] = gr["a_norm"]
    g["a_conv_w"] = gr["a_conv_w"][None]
    g["a_conv_b"] = gr["a_conv_b"]
    g["a_dt_bias"], g["a_A_log"], g["a_D"] = gr["a_dt_bias"], gr["a_A_log"], gr["a_D"]
    g["a_gnorm"] = gr["a_gnorm"]
    g["kv_norm"] = gr["kv_norm"].reshape(-1)
    g["b_kv"] = gr["b_kv"].reshape(-1)
    g["k_norm"] = gr["k_norm"].reshape(-1)
    g["b_norm"] = gr["b_norm"]
    g["b_q"] = gr["b_q"]
    g["q_norm"] = gr["q_norm"]
    g["sinks"] = gr["sinks"]
    g["b_o"] = gr["b_o"]
    f = [gr["ffn0"], gr["ffn1"]]
    g["f_norm"] = jnp.concatenate([f[0]["f_norm"], f[1]["f_norm"]], axis=0)
    g["f_conv_w"] = jnp.stack([f[l]["f_conv_w"] for l in range(2)])
    g["f_conv_b"] = jnp.concatenate([f[l]["f_conv_b"] for l in range(2)], axis=0)
    return g


def _full_grads(gr):
    g = _small_grads(gr)
    f32 = lambda t: t.astype(F32)
    g["a_in_proj"] = f32(gr["in_proj"])[None]
    g["a_out_proj"] = f32(gr["a_out_proj"])[None]
    g["w_kv"] = f32(gr["w_kv"])
    g["w_q"] = f32(gr["w_q"])[None]
    g["w_o"] = f32(gr["w_o"])[None]
    f = [gr["ffn0"], gr["ffn1"]]
    g["f_w_in"] = jnp.stack([f32(f[l]["f_w_in"]).transpose(1, 0, 2).reshape(1024, -1) for l in range(2)])
    g["f_w_down"] = jnp.stack([f32(f[l]["f_w_down"]) for l in range(2)])
    return g


MESH = pl.DeviceIdType.MESH
WEIGHTS = ("a_norm", "a_in_proj", "a_conv_w", "a_conv_b", "a_dt_bias", "a_A_log", "a_D", "a_gnorm", "a_out_proj",
           "kv_norm", "w_kv", "b_kv", "k_norm", "b_norm", "w_q", "b_q", "q_norm", "sinks", "w_o", "b_o", "f_norm",
           "f_w_in", "f_conv_w", "f_conv_b", "f_w_down")
MATS = (("in_proj", "a_in_proj", 0), ("out_proj", "a_out_proj", 0), ("w_kv", "w_kv", None), ("w_q", "w_q", 0),
        ("w_o", "w_o", 0), ("f_in0", "f_w_in", 0), ("f_in1", "f_w_in", 1), ("f_down0", "f_w_down", 0),
        ("f_down1", "f_w_down", 1))
SMALL_CUT = (("a_norm", 1), ("a_conv_w", 2), ("a_conv_b", 1), ("a_gnorm", 1), ("f_conv_w", 2))
SMALL_REP = ("a_dt_bias", "a_A_log", "a_D", "kv_norm", "b_kv", "k_norm", "b_norm", "b_q", "q_norm", "sinks", "b_o",
             "f_norm", "f_conv_b")


def _coords():
    return lax.axis_index("x"), lax.axis_index("y"), lax.axis_index("c")


def _other_chips(x, y):
    return [(1 - x, y), (x, 1 - y), (1 - x, 1 - y)]


def _pack(arrs, rows_align, lanes, dtype):
    flat = jnp.concatenate([a.reshape(-1).astype(dtype) for a in arrs])
    per = rows_align * lanes
    total = -(-flat.shape[0] // per) * per
    return jnp.pad(flat, (0, total - flat.shape[0])).reshape(total // lanes, lanes)


def _unpack(flat, shapes):
    out, off = [], 0
    for s in shapes:
        n = math.prod(s)
        out.append(flat[off:off + n].reshape(s))
        off += n
    return out


def _remote(src, dst, send, recv, k, dev):
    return pltpu.make_async_remote_copy(src_ref=src, dst_ref=dst, send_sem=send.at[k], recv_sem=recv.at[k],
                                        device_id=dev, device_id_type=MESH)


_ANY = pl.BlockSpec(memory_space=pl.ANY)


def _halves(t):
    r, c = t.shape
    return t.reshape(2, r // 2, c)


def _gather_weights(shards, sp):
    n = len(shards)
    per = 9
    n_sem = per * n + 3

    def body(*refs):
        sh, sp_ref = refs[:n], refs[n]
        outs, sout = refs[n + 1:2 * n + 1], refs[2 * n + 1]
        send, recv, loc = refs[2 * n + 2:]
        x, y, c = _coords()
        me = 2 * x + y
        cx_, cy_, cd_ = _other_chips(x, y)
        ix, iy, idg = (2 * p[0] + p[1] for p in (cx_, cy_, cd_))
        to_x, to_y, sib = (*cx_, c), (*cy_, c), (x, y, 1 - c)
        l1 = pltpu.make_async_copy(sp_ref, sout.at[me], loc.at[0])
        l1.start()
        sends = [_remote(sp_ref, sout.at[me], send, recv, per * n + j, (*p, c)) for j, p in enumerate((cx_, cy_, cd_))]
        for t in range(n):
            sends.append(_remote(sh[t].at[c], outs[t].at[me, c], send, recv, per * t + 0, to_x))
            sends.append(_remote(sh[t].at[c], outs[t].at[me, c], send, recv, per * t + 1, to_y))
            sends.append(_remote(sh[t], outs[t].at[me], send, recv, per * t + 8, sib))
        for cp in sends:
            cp.start()

        def go(src, dst, k, dev):
            cp = _remote(src, dst, send, recv, k, dev)
            cp.start()
            sends.append(cp)

        def piece(t, owner, first):
            q = sh[t].shape[1] // 2
            return outs[t].at[owner, c, pl.ds(0 if first else q, q)]

        for t in range(n):
            _remote(sh[t].at[c], outs[t].at[ix, c], send, recv, per * t + 0, to_x).wait_recv()
            go(piece(t, ix, False), piece(t, ix, False), per * t + 3, to_y)
            go(outs[t].at[ix, c], outs[t].at[ix, c], per * t + 4, sib)
        for t in range(n):
            _remote(sh[t].at[c], outs[t].at[iy, c], send, recv, per * t + 1, to_y).wait_recv()
            go(piece(t, iy, True), piece(t, iy, True), per * t + 2, to_x)
            go(outs[t].at[iy, c], outs[t].at[iy, c], per * t + 5, sib)
        for t in range(n):
            _remote(piece(t, idg, True), piece(t, idg, True), send, recv, per * t + 2, to_x).wait_recv()
            go(piece(t, idg, True), piece(t, idg, True), per * t + 6, sib)
            _remote(piece(t, idg, False), piece(t, idg, False), send, recv, per * t + 3, to_y).wait_recv()
            go(piece(t, idg, False), piece(t, idg, False), per * t + 7, sib)
        for j, p in enumerate((cx_, cy_, cd_)):
            _remote(sp_ref, sout.at[2 * p[0] + p[1]], send, recv, per * n + j, (*p, c)).wait_recv()
        for t in range(n):
            q = sh[t].shape[1] // 2
            other = lambda owner, lo=None: outs[t].at[owner, 1 - c] if lo is None else outs[t].at[owner, 1 - c, pl.ds(lo, q)]
            _remote(other(ix), other(ix), send, recv, per * t + 4, sib).wait_recv()
            _remote(other(iy), other(iy), send, recv, per * t + 5, sib).wait_recv()
            _remote(other(idg, 0), other(idg, 0), send, recv, per * t + 6, sib).wait_recv()
            _remote(other(idg, q), other(idg, q), send, recv, per * t + 7, sib).wait_recv()
            _remote(sh[t], outs[t].at[me], send, recv, per * t + 8, sib).wait_recv()
        for cp in sends:
            cp.wait_send()
        l1.wait()

    res = pl.pallas_call(
        body, name="gather_weights", in_specs=[_ANY] * (n + 1), out_specs=[_ANY] * (n + 1),
        out_shape=[jax.ShapeDtypeStruct((N_CHIPS,) + t.shape, t.dtype) for t in shards]
        + [jax.ShapeDtypeStruct((N_CHIPS,) + sp.shape, sp.dtype)],
        scratch_shapes=[pltpu.SemaphoreType.DMA((n_sem,)), pltpu.SemaphoreType.DMA((n_sem,)),
                        pltpu.SemaphoreType.DMA((1,))],
    )(*shards, sp)
    return res[:n], res[n]


_HBM = pl.BlockSpec(memory_space=pltpu.HBM)
_SEMS = pl.BlockSpec(memory_space=pltpu.SEMAPHORE)
_DATAFLOW = pltpu.SideEffectType.DATAFLOW_SIDE_EFFECTING


def _in_hbm(a):
    return pltpu.with_memory_space_constraint(a, pltpu.HBM)


def _start_copies(copies, arrays, n_sem, after, *, name):
    n, na = len(arrays), len(after)

    def body(*refs):
        for mine, _ in copies(refs[:n], refs[n + na], refs[n + na + 1]):
            mine.start()
        refs[-1][...] = jnp.zeros_like(refs[-1])

    res = pl.pallas_call(
        body, name=name, in_specs=[_HBM] * n + [_ANY] * na,
        out_specs=[_SEMS, _SEMS] + [_HBM] * n + [pl.BlockSpec(memory_space=pltpu.VMEM)],
        out_shape=[pltpu.SemaphoreType.DMA((n_sem,)), pltpu.SemaphoreType.DMA((n_sem,))]
        + [pltpu.HBM(a.shape, a.dtype) for a in arrays] + [jax.ShapeDtypeStruct((8, 128), F32)],
        input_output_aliases={t: 2 + t for t in range(n)},
        compiler_params=pltpu.CompilerParams(has_side_effects=_DATAFLOW),
    )(*[_in_hbm(a) for a in arrays], *after)
    return res[0], res[1], list(res[2:2 + n]), res[-1]


def _wait_copies(copies, send, recv, arrays, after, *, name):
    n = len(arrays)

    def body(*refs):
        for mine, theirs in copies(refs[:n], refs[n], refs[n + 1]):
            mine.wait_send()
            theirs.wait_recv()

    return list(pl.pallas_call(
        body, name=name, in_specs=[_HBM] * n + [_SEMS, _SEMS] + [_ANY] * len(after), out_specs=[_HBM] * n,
        out_shape=[pltpu.HBM(a.shape, a.dtype) for a in arrays], input_output_aliases={t: t for t in range(n)},
        compiler_params=pltpu.CompilerParams(has_side_effects=_DATAFLOW),
    )(*arrays, send, recv, *after))


def _sibling_copies(refs, send, recv):
    n = len(refs) // 2
    x, y, c = _coords()
    cps = [_remote(refs[t].at[:, 1 - c], refs[n + t], send, recv, t, (x, y, 1 - c)) for t in range(n)]
    return [(cp, cp) for cp in cps]


def _join_copies(refs, send, recv):
    x, y, c = _coords()
    sib = (x, y, 1 - c)
    return [(_remote(o.at[c], o.at[c], send, recv, t, sib), _remote(o.at[1 - c], o.at[1 - c], send, recv, t, sib))
            for t, o in enumerate(refs)]


def _gather_copies(sh, land, send, recv):
    x, y, c = _coords()
    me = 2 * x + y
    out = []
    for t in range(len(sh)):
        for j, (cx, cy) in enumerate(_other_chips(x, y)):
            dev = (cx, cy, c)
            out.append((_remote(sh[t].at[c], land[t].at[me, c], send, recv, 4 * t + j, dev),
                        _remote(sh[t].at[c], land[t].at[2 * cx + cy, c], send, recv, 4 * t + j, dev)))
        sib = (x, y, 1 - c)
        out.append((_remote(sh[t], land[t].at[me], send, recv, 4 * t + 3, sib),
                    _remote(sh[t], land[t].at[me], send, recv, 4 * t + 3, sib)))
    return out


def _gather_start(shards, after, *, name):
    n = len(shards)

    def body(*refs):
        sh, land = refs[:n], refs[n:2 * n]
        send, recv = refs[2 * n + 1], refs[2 * n + 2]
        token = refs[-1]
        for mine, _ in _gather_copies(sh, land, send, recv):
            mine.start()
        token[...] = jnp.zeros_like(token)

    lands = [_in_hbm(lax.empty((N_CHIPS,) + s.shape, s.dtype)) for s in shards]
    res = pl.pallas_call(
        body, name=name, in_specs=[_HBM] * (2 * n) + [_ANY],
        out_specs=[_SEMS, _SEMS] + [_HBM] * (2 * n) + [pl.BlockSpec(memory_space=pltpu.VMEM)],
        out_shape=[pltpu.SemaphoreType.DMA((4 * n,)), pltpu.SemaphoreType.DMA((4 * n,))]
        + [pltpu.HBM(s.shape, s.dtype) for s in shards] + [pltpu.HBM(l.shape, l.dtype) for l in lands]
        + [jax.ShapeDtypeStruct((8, 128), F32)],
        input_output_aliases={t: 2 + t for t in range(2 * n)},
        compiler_params=pltpu.CompilerParams(has_side_effects=_DATAFLOW),
    )(*[_in_hbm(s) for s in shards], *lands, after)
    return res[0], res[1], res[2:2 + n], res[2 + n:2 + 2 * n], res[-1]


def _gather_wait(send, recv, shards, lands, after, *, name):
    n = len(shards)

    def body(*refs):
        sh, land = refs[:n], refs[n:2 * n]
        send_r, recv_r = refs[2 * n], refs[2 * n + 1]
        for mine, theirs in _gather_copies(sh, land, send_r, recv_r):
            mine.wait_send()
            theirs.wait_recv()

    res = pl.pallas_call(
        body, name=name, in_specs=[_HBM] * (2 * n) + [_SEMS, _SEMS, _ANY], out_specs=[_HBM] * (2 * n),
        out_shape=[pltpu.HBM(s.shape, s.dtype) for s in shards] + [pltpu.HBM(l.shape, l.dtype) for l in lands],
        input_output_aliases={t: t for t in range(2 * n)},
        compiler_params=pltpu.CompilerParams(has_side_effects=_DATAFLOW),
    )(*shards, *lands, send, recv, after)
    return res[n:]


def _forward_copies(refs, send, recv):
    x, y, c = _coords()
    sib = (x, y, 1 - c)
    srcs = [2 * cx + cy for cx, cy in _other_chips(x, y)]
    return [(_remote(o.at[s, c], o.at[s, c], send, recv, 3 * t + j, sib),
             _remote(o.at[s, 1 - c], o.at[s, 1 - c], send, recv, 3 * t + j, sib))
            for t, o in enumerate(refs) for j, s in enumerate(srcs)]


def _small_copies(v, land, send, recv):
    x, y, c = _coords()
    me = 4 * x + 2 * y + c
    out = []
    for k in range(1, 8):
        px = 1 - x if k & 4 else x
        py = 1 - y if k & 2 else y
        pc = 1 - c if k & 1 else c
        out.append((_remote(v, land.at[me], send, recv, k - 1, (px, py, pc)),
                    _remote(v, land.at[4 * px + 2 * py + pc], send, recv, k - 1, (px, py, pc))))
    return out


def _small_start(v, after, *, name):
    def body(v_ref, land_ref, after_ref, send, recv, v_thru, land_thru, token):
        for mine, _ in _small_copies(v_ref, land_ref, send, recv):
            mine.start()
        token[...] = jnp.zeros_like(token)

    land = _in_hbm(lax.empty((8,) + v.shape, v.dtype))
    return pl.pallas_call(
        body, name=name, in_specs=[_HBM, _HBM, _ANY],
        out_specs=[_SEMS, _SEMS, _HBM, _HBM, pl.BlockSpec(memory_space=pltpu.VMEM)],
        out_shape=[pltpu.SemaphoreType.DMA((7,)), pltpu.SemaphoreType.DMA((7,)), pltpu.HBM(v.shape, v.dtype),
                   pltpu.HBM(land.shape, land.dtype), jax.ShapeDtypeStruct((8, 128), F32)],
        input_output_aliases={0: 2, 1: 3}, compiler_params=pltpu.CompilerParams(has_side_effects=_DATAFLOW),
    )(_in_hbm(v), land, after)


def _small_wait(send, recv, v, land, after, *, name):
    def body(v_ref, land_ref, send_r, recv_r, *rest):
        for mine, theirs in _small_copies(v_ref, land_ref, send_r, recv_r):
            mine.wait_send()
            theirs.wait_recv()

    return pl.pallas_call(
        body, name=name, in_specs=[_HBM, _HBM, _SEMS, _SEMS] + [_ANY] * len(after), out_specs=[_HBM, _HBM],
        out_shape=[pltpu.HBM(v.shape, v.dtype), pltpu.HBM(land.shape, land.dtype)],
        input_output_aliases={0: 0, 1: 1}, compiler_params=pltpu.CompilerParams(has_side_effects=_DATAFLOW),
    )(v, land, send, recv, *after)


def _small_sum(v, land, me_idx, *, name="small_sum"):
    def body(me_ref, v_ref, land_ref, o_ref):
        acc = None
        for s in range(8):
            term = jnp.where(me_ref[0] == s, v_ref[...], land_ref[s])
            acc = term if acc is None else acc + term
        o_ref[...] = acc

    whole = lambda shape: pl.BlockSpec(shape, lambda i, me_ref: (0,) * len(shape))
    return pl.pallas_call(
        body, name=name,
        grid_spec=pltpu.PrefetchScalarGridSpec(num_scalar_prefetch=1, grid=(1,), in_specs=[whole(v.shape), whole(land.shape)],
                                               out_specs=whole(v.shape)),
        out_shape=jax.ShapeDtypeStruct(v.shape, F32), compiler_params=_cp(1),
    )(me_idx, v, land)


RS_ROW_SPLIT = 2


def _rs_add_pair(gs, as_, c_idx, *, name):
    n = len(gs)

    def body(c_ref, *refs):
        for t in range(n):
            refs[2 * n + t][...] = (refs[t][...].astype(F32) + refs[n + t][...].astype(F32)).astype(BF16)

    def gspec(g):
        _, _, rh, cols = g.shape
        return pl.BlockSpec((None, None, rh // RS_ROW_SPLIT, cols), lambda j, i, c_ref: (j, c_ref[0], i, 0))

    def pspec(g):
        _, _, rh, cols = g.shape
        return pl.BlockSpec((None, rh // RS_ROW_SPLIT, cols), lambda j, i, c_ref: (j, i, 0))

    return pl.pallas_call(
        body, name=name,
        grid_spec=pltpu.PrefetchScalarGridSpec(
            num_scalar_prefetch=1, grid=(N_CHIPS, RS_ROW_SPLIT),
            in_specs=[gspec(g) for g in gs] + [pspec(g) for g in gs], out_specs=[pspec(g) for g in gs]),
        out_shape=[jax.ShapeDtypeStruct((N_CHIPS,) + g.shape[2:], BF16) for g in gs], compiler_params=_cp(2),
    )(c_idx, *gs, *as_)


def _chips_copies(p, r, send, recv):
    x, y, c = _coords()
    return [_remote(p[t].at[2 * cx + cy], r[t].at[k], send, recv, 3 * t + k, (cx, cy, c))
            for k, (cx, cy) in enumerate(_other_chips(x, y)) for t in range(len(p))]


def _rs_chips_start(ps, after, *, name):
    n, na = len(ps), len(after)

    def body(*refs):
        p, r = refs[:n], refs[n:2 * n]
        send, recv = refs[2 * n + na], refs[2 * n + na + 1]
        token = refs[-1]
        for cp in _chips_copies(p, r, send, recv):
            cp.start()
        token[...] = jnp.zeros_like(token)

    lands = [_in_hbm(lax.empty((3,) + p.shape[1:], p.dtype)) for p in ps]
    res = pl.pallas_call(
        body, name=name, in_specs=[_HBM] * (2 * n) + [_ANY] * na,
        out_specs=[_SEMS, _SEMS] + [_HBM] * (2 * n) + [pl.BlockSpec(memory_space=pltpu.VMEM)],
        out_shape=[pltpu.SemaphoreType.DMA((3 * n,)), pltpu.SemaphoreType.DMA((3 * n,))]
        + [pltpu.HBM(p.shape, p.dtype) for p in ps] + [pltpu.HBM(l.shape, l.dtype) for l in lands]
        + [jax.ShapeDtypeStruct((8, 128), F32)],
        input_output_aliases={t: 2 + t for t in range(2 * n)},
        compiler_params=pltpu.CompilerParams(has_side_effects=_DATAFLOW),
    )(*[_in_hbm(p) for p in ps], *lands, *after)
    return res[0], res[1], res[2:2 + n], res[2 + n:2 + 2 * n], res[-1]


def _rs_chips_wait(send, recv, ps, lands, after, *, name):
    n = len(ps)

    def body(*refs):
        p, r = refs[:n], refs[n:2 * n]
        for cp in _chips_copies(p, r, refs[2 * n], refs[2 * n + 1]):
            cp.wait_send()
            cp.wait_recv()

    res = pl.pallas_call(
        body, name=name, in_specs=[_HBM] * (2 * n) + [_SEMS, _SEMS] + [_ANY] * len(after), out_specs=[_HBM] * (2 * n),
        out_shape=[pltpu.HBM(p.shape, p.dtype) for p in ps] + [pltpu.HBM(l.shape, l.dtype) for l in lands],
        input_output_aliases={t: t for t in range(2 * n)},
        compiler_params=pltpu.CompilerParams(has_side_effects=_DATAFLOW),
    )(*ps, *lands, send, recv, *after)
    return res[:n], res[n:]


def _rs_add_chips(ps, rs, idx, *, name):
    n = len(ps)

    def body(idx_ref, *refs):
        for t in range(n):
            p_ref, r0, r1, r2 = refs[4 * t:4 * t + 4]
            refs[4 * n + t][...] = ((p_ref[...].astype(F32) + r0[...].astype(F32)) + r1[...].astype(F32)) + r2[...].astype(F32)

    in_specs, args = [], []
    for p, r in zip(ps, rs):
        _, rh, cols = p.shape
        blk = (None, rh // RS_ROW_SPLIT, cols)
        in_specs.append(pl.BlockSpec(blk, lambda i, idx_ref: (idx_ref[0], i, 0)))
        in_specs += [pl.BlockSpec(blk, lambda i, idx_ref, k=k: (k, i, 0)) for k in range(3)]
        args += [p, r, r, r]
    out_specs = [pl.BlockSpec((None, p.shape[1] // RS_ROW_SPLIT, p.shape[2]), lambda i, idx_ref: (idx_ref[1], i, 0))
                 for p in ps]
    return pl.pallas_call(
        body, name=name,
        grid_spec=pltpu.PrefetchScalarGridSpec(num_scalar_prefetch=1, grid=(RS_ROW_SPLIT,), in_specs=in_specs,
                                               out_specs=out_specs),
        out_shape=[jax.ShapeDtypeStruct((2,) + p.shape[1:], F32) for p in ps], compiler_params=_cp(1),
    )(idx, *args)


def _adamw(w, gs, m, v, *, name, dep=None):
    L, Rr, C = w.shape
    tr, tc = _pick(Rr, (256, 128, 64)), C
    if tr == Rr and Rr * C > 512 * 1024:
        tc = 256
    bc1 = 1.0 - ADAM_B1 ** ADAM_STEP
    bc2 = 1.0 - ADAM_B2 ** ADAM_STEP
    nd = 0 if dep is None else 1

    def body(*refs):
        w_ref, m_ref, v_ref = refs[0], refs[1], refs[2]
        g_refs = refs[3:3 + L]
        d_ref, mo_ref, vo_ref, go_ref = refs[3 + L + nd:]
        layer = pl.program_id(0)
        gv = g_refs[0][...]
        for q in range(1, L):
            gv = jnp.where(layer == q, g_refs[q][...], gv)
        mn = ADAM_B1 * m_ref[...] + (1.0 - ADAM_B1) * gv
        vn = ADAM_B2 * v_ref[...] + (1.0 - ADAM_B2) * (gv * gv)
        go_ref[...] = gv
        mo_ref[...] = mn
        vo_ref[...] = vn
        d_ref[...] = -ADAM_LR * ((mn / bc1) / (jnp.sqrt(vn / bc2) + ADAM_EPS) + ADAM_WD * w_ref[...])

    blk = pl.BlockSpec((None, tr, tc), lambda l, i, j: (l, i, j))
    gblks = [pl.BlockSpec((tr, tc), lambda l, i, j, q=q: (jnp.where(l == q, i, 0), jnp.where(l == q, j, 0))) for q in range(L)]
    return pl.pallas_call(
        body, name=name, grid=(L, Rr // tr, C // tc), in_specs=[blk] * 3 + gblks + [_ANY] * nd, out_specs=[blk] * 4,
        out_shape=[jax.ShapeDtypeStruct((L, Rr, C), F32)] * 4, compiler_params=_cp(3),
    )(w, m, v, *gs, *([] if dep is None else [dep]))


def _adamw_leaves(ws, gs, ms, vs, *, name):
    n = len(ws)
    bc1 = 1.0 - ADAM_B1 ** ADAM_STEP
    bc2 = 1.0 - ADAM_B2 ** ADAM_STEP

    def body(*refs):
        w_refs, g_refs, m_refs, v_refs, d_refs, mo_refs, vo_refs = (refs[q * n:(q + 1) * n] for q in range(7))
        for k in range(n):
            gv = g_refs[k][...]
            mn = ADAM_B1 * m_refs[k][...] + (1.0 - ADAM_B1) * gv
            vn = ADAM_B2 * v_refs[k][...] + (1.0 - ADAM_B2) * (gv * gv)
            mo_refs[k][...] = mn
            vo_refs[k][...] = vn
            d_refs[k][...] = -ADAM_LR * ((mn / bc1) / (jnp.sqrt(vn / bc2) + ADAM_EPS) + ADAM_WD * w_refs[k][...])

    in_vmem = pl.BlockSpec(memory_space=pltpu.VMEM)
    outs = pl.pallas_call(
        body, name=name, in_specs=[in_vmem] * (4 * n), out_specs=[in_vmem] * (3 * n),
        out_shape=[jax.ShapeDtypeStruct(t.shape, F32) for t in ws] * 3,
    )(*ws, *gs, *ms, *vs)
    return outs[:n], outs[n:2 * n], outs[2 * n:]


def kernel(x, positions, a_norm, a_in_proj, a_conv_w, a_conv_b, a_dt_bias, a_A_log, a_D, a_gnorm, a_out_proj,
           kv_norm, w_kv, b_kv, k_norm, b_norm, w_q, b_q, q_norm, sinks, w_o, b_o, f_norm, f_w_in, f_conv_w,
           f_conv_b, f_w_down, loss_target, m_a_norm, m_a_in_proj, m_a_conv_w, m_a_conv_b, m_a_dt_bias, m_a_A_log,
           m_a_D, m_a_gnorm, m_a_out_proj, m_kv_norm, m_w_kv, m_b_kv, m_k_norm, m_b_norm, m_w_q, m_b_q, m_q_norm,
           m_sinks, m_w_o, m_b_o, m_f_norm, m_f_w_in, m_f_conv_w, m_f_conv_b, m_f_w_down, v_a_norm, v_a_in_proj,
           v_a_conv_w, v_a_conv_b, v_a_dt_bias, v_a_A_log, v_a_D, v_a_gnorm, v_a_out_proj, v_kv_norm, v_w_kv,
           v_b_kv, v_k_norm, v_b_norm, v_w_q, v_b_q, v_q_norm, v_sinks, v_w_o, v_b_o, v_f_norm, v_f_w_in,
           v_f_conv_w, v_f_conv_b, v_f_w_down):
    wl = dict(zip(WEIGHTS, (a_norm, a_in_proj, a_conv_w, a_conv_b, a_dt_bias, a_A_log, a_D, a_gnorm, a_out_proj,
                            kv_norm, w_kv, b_kv, k_norm, b_norm, w_q, b_q, q_norm, sinks, w_o, b_o, f_norm, f_w_in,
                            f_conv_w, f_conv_b, f_w_down)))
    ml = dict(zip(WEIGHTS, (m_a_norm, m_a_in_proj, m_a_conv_w, m_a_conv_b, m_a_dt_bias, m_a_A_log, m_a_D, m_a_gnorm,
                            m_a_out_proj, m_kv_norm, m_w_kv, m_b_kv, m_k_norm, m_b_norm, m_w_q, m_b_q, m_q_norm,
                            m_sinks, m_w_o, m_b_o, m_f_norm, m_f_w_in, m_f_conv_w, m_f_conv_b, m_f_w_down)))
    vl = dict(zip(WEIGHTS, (v_a_norm, v_a_in_proj, v_a_conv_w, v_a_conv_b, v_a_dt_bias, v_a_A_log, v_a_D, v_a_gnorm,
                            v_a_out_proj, v_kv_norm, v_w_kv, v_b_kv, v_k_norm, v_b_norm, v_w_q, v_b_q, v_q_norm,
                            v_sinks, v_w_o, v_b_o, v_f_norm, v_f_w_in, v_f_conv_w, v_f_conv_b, v_f_w_down)))
    xi, yi, ci = _coords()
    me = 2 * xi + yi
    S = x.shape[1]

    def block_of(n, layer):
        t = wl[n]
        return t if layer is None else t[layer]

    rows = lambda t: t.reshape(-1, t.shape[-1])
    c_idx = jnp.reshape(ci, (1,)).astype(jnp.int32)
    me_c = jnp.stack([me, ci]).astype(jnp.int32)
    early = ("in_proj",)
    late = (("out_proj", "f_in0", "f_down0"), ("w_kv", "w_q", "w_o", "f_in1", "f_down1"))
    shards = {name: _halves(block_of(wn, layer).astype(BF16)) for name, wn, layer in MATS}

    sp = _pack([wl[n] for n, _ in SMALL_CUT], 8, 128, F32)
    gathered, gs = _gather_weights([shards[k] for k in early], sp)
    gt = {k: t.reshape(N_CHIPS, -1, t.shape[-1]) for k, t in zip(early, gathered)}
    started = {0: _gather_start([shards[k] for k in late[0]], gs, name="gather_late_start0")}
    full = {n: wl[n] for n in SMALL_REP}
    gs = gs.reshape(N_CHIPS, -1)
    off = 0
    for n, ax in SMALL_CUT:
        shp = wl[n].shape
        size = math.prod(shp)
        piece = jnp.moveaxis(gs[:, off:off + size].reshape((N_CHIPS,) + shp), 0, ax)
        full[n] = piece.reshape(shp[:ax] + (N_CHIPS * shp[ax],) + shp[ax + 1:])
        off += size
    w = _prep_small(full, {})
    w["w_zx"], w["w_dt"] = _join_in_proj(gt["in_proj"])
    w["dep"] = started[0][4]

    class Comm:
        flight = []
        reduced = {}

        forwarding = {}

        def late_start(self, part, after):
            started[part] = _gather_start([shards[k] for k in late[part]], after, name=f"gather_late_start{part}")
            return started[part][4]

        def late_arrived(self, part, after):
            send, recv, shs, lands, _ = started[part]
            lands = _gather_wait(send, recv, shs, lands, after[0], name=f"gather_late_wait{part}")
            send, recv, lands, token = _start_copies(_forward_copies, list(lands), 3 * len(lands), after,
                                                     name=f"gather_late_forward_start{part}")
            self.forwarding[part] = (send, recv, lands)
            return token

        def late_weights(self, w, after, part):
            send, recv, lands = self.forwarding[part]
            lands = _wait_copies(_forward_copies, send, recv, lands, [after], name=f"gather_late_forward_wait{part}")
            lt = {k: t.reshape(N_CHIPS, -1, t.shape[-1]) for k, t in zip(late[part], lands)}
            w = dict(w)
            if part == 0:
                w["a_out_proj"], w["f_w_in"], w["f_w_down"] = rows(lt["out_proj"]), [lt["f_in0"]], [rows(lt["f_down0"])]
            else:
                w["w_kv"], w["w_q"], w["w_o"] = (rows(lt[k]) for k in ("w_kv", "w_q", "w_o"))
                w["f_w_in"], w["f_w_down"] = w["f_w_in"] + [lt["f_in1"]], w["f_w_down"] + [rows(lt["f_down1"])]
            return w

        def advance(self, after, group=None, tensors=None):
            token = None
            for grp in list(self.flight):
                tag, n = grp["tag"], len(grp["names"])
                dep = list(after) + ([] if token is None else [token])
                if grp["stage"] == "sibling":
                    arrs = _wait_copies(_sibling_copies, grp["send"], grp["recv"], grp["arrays"], dep, name=f"rs_sibling_wait{tag}")
                    pairs = _rs_add_pair(arrs[:n], arrs[n:], c_idx, name=f"rs_add_pair{tag}")
                    send, recv, ps, lands, token = _rs_chips_start(pairs, dep, name=f"rs_chips_start{tag}")
                    grp.update(stage="chips", send=send, recv=recv, ps=ps, lands=lands)
                elif grp["stage"] == "chips":
                    ps, rs = _rs_chips_wait(grp["send"], grp["recv"], grp["ps"], grp["lands"], dep, name=f"rs_chips_wait{tag}")
                    halves = _rs_add_chips(ps, rs, me_c, name=f"rs_add_chips{tag}")
                    send, recv, arrs, token = _start_copies(_join_copies, halves, n, dep, name=f"rs_join_start{tag}")
                    grp.update(stage="join", send=send, recv=recv, arrays=arrs)
                else:
                    joined = _wait_copies(_join_copies, grp["send"], grp["recv"], grp["arrays"], dep, name=f"rs_join_wait{tag}")
                    self.reduced.update({k: rows(t) for k, t in zip(grp["names"], joined)})
                    self.flight.remove(grp)
            if group is not None:
                names = list(tensors)
                glist = [tensors[k].reshape(N_CHIPS, 2, -1, tensors[k].shape[-1]) for k in names]
                lands = [lax.empty((N_CHIPS,) + gq.shape[2:], gq.dtype) for gq in glist]
                dep = [a for a in after if not any(a is t for t in tensors.values())] + ([] if token is None else [token])
                send, recv, arrs, token = _start_copies(_sibling_copies, glist + lands, len(names), dep,
                                                        name=f"rs_sibling_start{group}")
                self.flight.append(dict(tag=group, names=names, stage="sibling", send=send, recv=recv, arrays=arrs))
            return token

    comm = Comm()

    posf = positions.reshape(S, 1).astype(F32)
    loss_part, dx0, gr, tok = _local_step(x[0], posf, loss_target[0], w, comm)
    g = _small_grads(gr)

    small_names = [n for n, _ in SMALL_CUT] + list(SMALL_REP)
    sv = _pack([g[n] for n in small_names] + [loss_part[0:1, 0:1]], 8, 128, F32)
    s_send, s_recv, sv, s_land, s_token = _small_start(sv, tok, name="small_start")

    grads, delta, new_m, new_v = {}, {}, {}, {}

    def update(wn, dep):
        gl = [comm.reduced[name] for name, n2, _ in MATS if n2 == wn]
        shp = wl[wn].shape
        three = (len(gl),) + gl[0].shape
        flip = shp[-1] % 128 != 0
        view = (lambda t: t.reshape(three).transpose(0, 2, 1)) if flip else (lambda t: t.reshape(three))
        back = (lambda t: t.transpose(0, 2, 1).reshape(shp)) if flip else (lambda t: t.reshape(shp))
        if flip:
            gl = [t.T for t in gl]
        d, mn, vn, go = _adamw(view(wl[wn]), gl, view(ml[wn]), view(vl[wn]), name="adamw_" + wn, dep=dep)
        grads[wn], delta[wn], new_m[wn], new_v[wn] = back(go), back(d), back(mn), back(vn)
        return d

    first = [update(wn, s_token) for wn in ("w_q", "w_o", "w_kv")]
    second = [update(wn, first[-1]) for wn in ("f_w_in", "f_w_down", "a_out_proj")]
    tok = comm.advance(first + second)
    done = first + second + [tok]

    sv, s_land = _small_wait(s_send, s_recv, sv, s_land, done, name="small_wait")
    sred = _small_sum(sv, s_land, jnp.reshape(2 * me + ci, (1,)).astype(jnp.int32)).reshape(-1)
    small_shapes = [g[n].shape for n in small_names] + [(1,)]
    sg = dict(zip(small_names + ["loss"], _unpack(sred, small_shapes)))
    loss = sg["loss"].reshape(())
    g_small = {}
    for n, ax in SMALL_CUT:
        size = wl[n].shape[ax]
        g_small[n] = lax.dynamic_slice_in_dim(sg[n], me * size, size, axis=ax)
    for n in SMALL_REP:
        g_small[n] = sg[n].reshape(wl[n].shape)

    leaves = lambda d: [d[n].reshape(1, -1) if d[n].ndim == 1 else d[n] for n in small_names]
    ds, mns, vns = _adamw_leaves(leaves(wl), leaves(g_small), leaves(ml), leaves(vl), name="adamw_small")
    comm.advance([ds[0]])
    update("a_in_proj", None)
    for n, dd, mm, vv in zip(small_names, ds, mns, vns):
        shp = wl[n].shape
        grads[n], delta[n], new_m[n], new_v[n] = g_small[n], dd.reshape(shp), mm.reshape(shp), vv.reshape(shp)

    return (loss, dx0[None], *[grads[n] for n in WEIGHTS], *[delta[n] for n in WEIGHTS],
            *[new_m[n] for n in WEIGHTS], *[new_v[n] for n in WEIGHTS])
```

```python
import math

import jax
import jax.numpy as jnp
from jax import lax
from jax.experimental import pallas as pl
from jax.experimental.pallas import tpu as pltpu

F32 = jnp.float32
BF16 = jnp.bfloat16

EPS = 1e-5
CHUNK = 256
WINDOW = 128
HEAD = 64
SSM_HEADS = 32
SSM_GROUPS = 8
SSM_STATE = 128
ATT_KV = 4
ATT_G = 4
ROPE_THETA = 10000.0
NEG = -1e30
N_CHIPS = 4
VMEM_LIMIT = 56 * 1024 * 1024

ADAM_LR, ADAM_B1, ADAM_B2, ADAM_EPS, ADAM_WD, ADAM_STEP = 0.001, 0.9, 0.999, 1e-08, 0.01, 10


def _cp(n_axes):
    return pltpu.CompilerParams(dimension_semantics=("arbitrary",) * n_axes, vmem_limit_bytes=VMEM_LIMIT)


def _pick(dim, prefs):
    for p in prefs:
        if dim % p == 0:
            return p
    return dim


def _iota(shape, dim):
    return lax.broadcasted_iota(jnp.int32, shape, dim)


def _dot(a, b, ca=1, cb=0):
    return lax.dot_general(a, b, (((ca,), (cb,)), ((), ())), preferred_element_type=F32)


def _dot3(x, ind):
    h = x.astype(BF16)
    r = x - h.astype(F32)
    m = r.astype(BF16)
    lo = (r - m.astype(F32)).astype(BF16)
    return _dot(h, ind) + _dot(m, ind) + _dot(lo, ind)


def _sigmoid(x):
    return jax.nn.sigmoid(x)


def _mm(a, b, *, name, ta=False, tb=False, bias=None, res=None, out_dtype=F32, b_koff=0, tm=None, tn=None, tk=None,
        dims=None, a_spec=None, b_spec=None, o_spec=None, o_shape=None, dep=None, more=(), target=None,
        rms=None, rms_colsum=False):
    if dims is not None:
        M, N, K = dims
    else:
        if ta:
            K, M = a.shape
        else:
            M, K = a.shape
        N = b.shape[0] if tb else b.shape[1]
    tm = tm or _pick(M, (1024, 1408, 512, 256, 128))
    tn = tn or _pick(N, (512, 1408, 256, 128))
    tk = tk or (K if K <= 2048 else _pick(K, (2048, 1408, 1024, 512)))
    assert M % tm == 0 and N % tn == 0 and K % tk == 0 and b_koff % tk == 0
    nk = K // tk
    kb0 = b_koff // tk
    has_bias, has_res = bias is not None, res is not None

    def body(*refs):
        a_ref, b_ref = refs[0], refs[1]
        pos = 2
        bias_ref = res_ref = acc_ref = None
        if has_bias:
            bias_ref = refs[pos]
            pos += 1
        if has_res:
            res_ref = refs[pos]
            pos += 1
        if dep is not None:
            pos += 1
        extra = refs[pos:pos + 2 * len(more)]
        pos += 2 * len(more)
        tgt_ref = lp_ref = rx_ref = rg_ref = rd_ref = dg_ref = cs_ref = None
        if target is not None:
            tgt_ref = refs[pos]
            pos += 1
        if rms is not None:
            rx_ref, rg_ref, rd_ref = refs[pos:pos + 3]
            pos += 3
        o_ref = refs[pos]
        pos += 1
        if target is not None:
            lp_ref = refs[pos]
            pos += 1
        if rms is not None:
            dg_ref = refs[pos]
            pos += 1
            if rms_colsum:
                cs_ref = refs[pos]
                pos += 1
        if nk > 1:
            acc_ref = refs[pos]
        part = _dot(a_ref[...].astype(BF16), b_ref[...].astype(BF16), 0 if ta else 1, 1 if tb else 0)
        for q in range(len(more)):
            part = part + _dot(extra[2 * q][...].astype(BF16), extra[2 * q + 1][...].astype(BF16),
                               0 if ta else 1, 1 if tb else 0)

        def finish(acc):
            if has_bias:
                acc = acc + bias_ref[...]
            if has_res:
                acc = acc + res_ref[...]
            if target is not None:
                err = acc - tgt_ref[...]
                acc = err * (1.0 / N)
                part_loss = jnp.sum(jnp.sum(err * err, axis=1, keepdims=True), axis=0, keepdims=True) * (0.5 / N)
                first = (pl.program_id(0) == 0) & (pl.program_id(1) == 0)

                @pl.when(first)
                def _():
                    lp_ref[...] = jnp.broadcast_to(part_loss, lp_ref.shape)

                @pl.when(jnp.logical_not(first))
                def _():
                    lp_ref[...] += jnp.broadcast_to(part_loss, lp_ref.shape)

            if rms is not None:
                xv = rx_ref[...]
                r = lax.rsqrt(jnp.mean(xv * xv, axis=-1, keepdims=True) + EPS)
                xh = xv * r
                dxh = acc * rg_ref[...]
                dg_part = jnp.sum(acc * xh, axis=0, keepdims=True)
                acc = rd_ref[...] + r * (dxh - xh * jnp.mean(dxh * xh, axis=-1, keepdims=True))
                cs_part = jnp.sum(acc, axis=0, keepdims=True) if rms_colsum else None
                first_rows = pl.program_id(0) == 0

                @pl.when(first_rows)
                def _():
                    dg_ref[...] = dg_part
                    if rms_colsum:
                        cs_ref[...] = cs_part

                @pl.when(jnp.logical_not(first_rows))
                def _():
                    dg_ref[...] += dg_part
                    if rms_colsum:
                        cs_ref[...] += cs_part

            o_ref[...] = acc.astype(out_dtype)

        if nk == 1:
            finish(part)
        else:
            k = pl.program_id(2)

            @pl.when(k == 0)
            def _():
                acc_ref[...] = part

            @pl.when(k > 0)
            def _():
                acc_ref[...] += part

            @pl.when(k == nk - 1)
            def _():
                finish(acc_ref[...])

    if a_spec is None:
        a_spec = pl.BlockSpec((tk, tm), lambda i, j, k: (k, i)) if ta else pl.BlockSpec((tm, tk), lambda i, j, k: (i, k))
    if b_spec is None:
        b_spec = (pl.BlockSpec((tn, tk), lambda i, j, k: (j, k + kb0)) if tb
                  else pl.BlockSpec((tk, tn), lambda i, j, k: (k + kb0, j)))
    if o_spec is None:
        o_spec = pl.BlockSpec((tm, tn), lambda i, j, k: (i, j))
    in_specs, args = [a_spec, b_spec], [a, b]
    if has_bias:
        in_specs.append(pl.BlockSpec((1, tn), lambda i, j, k: (0, j)))
        args.append(bias)
    if has_res:
        in_specs.append(pl.BlockSpec((tm, tn), lambda i, j, k: (i, j)))
        args.append(res)
    if dep is not None:
        in_specs.append(pl.BlockSpec(memory_space=pl.ANY))
        args.append(dep)
    for piece in more:
        a2, sa, b2, sb = piece if len(piece) == 4 else (a, piece[0], b, piece[1])
        in_specs += [sa, sb]
        args += [a2, b2]
    out_specs, out_shape = [o_spec], [jax.ShapeDtypeStruct(o_shape or (M, N), out_dtype)]
    if target is not None:
        in_specs.append(pl.BlockSpec((tm, tn), lambda i, j, k: (i, j)))
        args.append(target)
        out_specs.append(pl.BlockSpec((8, 128), lambda i, j, k: (0, 0)))
        out_shape.append(jax.ShapeDtypeStruct((8, 128), F32))
    if rms is not None:
        assert tn == N and nk == 1
        row, vec = pl.BlockSpec((tm, N), lambda i, j, k: (i, 0)), pl.BlockSpec((1, N), lambda i, j, k: (0, 0))
        in_specs += [row, vec, row]
        args += list(rms)
        out_specs += [vec] * (2 if rms_colsum else 1)
        out_shape += [jax.ShapeDtypeStruct((1, N), F32)] * (2 if rms_colsum else 1)
    if len(out_specs) == 1:
        out_specs, out_shape = out_specs[0], out_shape[0]
    return pl.pallas_call(
        body, name=name, grid=(M // tm, N // tn, nk), in_specs=in_specs, out_specs=out_specs, out_shape=out_shape,
        scratch_shapes=[pltpu.VMEM((tm, tn), F32)] if nk > 1 else [],
        compiler_params=_cp(3),
    )(*args)


def _norm_mm(x, gain, b, *, name, bias=None, N=None, tn=None, b_spec=None, dep=None):
    M, K = x.shape
    N = N or b.shape[1]
    tm = _pick(M, (1024, 512, 256))
    tn = tn or _pick(N, (512, 1408, 256, 128))
    has_bias = bias is not None

    def body(*refs):
        x_ref, g_ref, b_ref = refs[:3]
        pos = 3 + (1 if has_bias else 0) + (0 if dep is None else 1)
        o_ref, h_ref = refs[pos], refs[pos + 1]

        @pl.when(pl.program_id(1) == 0)
        def _():
            xv = x_ref[...]
            h_ref[...] = (xv * lax.rsqrt(jnp.mean(xv * xv, axis=-1, keepdims=True) + EPS) * g_ref[...]).astype(BF16)

        acc = _dot(h_ref[...], b_ref[...].astype(BF16))
        if has_bias:
            acc = acc + refs[3][...]
        o_ref[...] = acc

    in_specs = [pl.BlockSpec((tm, K), lambda i, j: (i, 0)), pl.BlockSpec((1, K), lambda i, j: (0, 0)),
                b_spec or pl.BlockSpec((K, tn), lambda i, j: (0, j))]
    args = [x, gain, b]
    if has_bias:
        in_specs.append(pl.BlockSpec((1, tn), lambda i, j: (0, j)))
        args.append(bias)
    if dep is not None:
        in_specs.append(pl.BlockSpec(memory_space=pl.ANY))
        args.append(dep)
    return pl.pallas_call(
        body, name=name, grid=(M // tm, N // tn), in_specs=in_specs,
        out_specs=[pl.BlockSpec((tm, tn), lambda i, j: (i, j)), pl.BlockSpec((tm, K), lambda i, j: (i, 0))],
        out_shape=[jax.ShapeDtypeStruct((M, N), F32), jax.ShapeDtypeStruct((M, K), BF16)], compiler_params=_cp(2),
    )(*args)


def _rms_bwd(x, gains, dhs, dres, *, name, tr=256, want_colsum=False):
    S, D = x.shape
    n = len(gains)
    steps = S // tr

    def body(*refs):
        x_ref = refs[0]
        g_refs = refs[1:1 + n]
        dh_refs = refs[1 + n:1 + 2 * n]
        dres_ref = refs[1 + 2 * n]
        dx_ref = refs[2 + 2 * n]
        dg_refs = refs[3 + 2 * n:3 + 3 * n]
        cs_ref = refs[3 + 3 * n] if want_colsum else None
        i = pl.program_id(0)
        xv = x_ref[...]
        r = lax.rsqrt(jnp.mean(xv * xv, axis=-1, keepdims=True) + EPS)
        xh = xv * r
        dx = dres_ref[...]
        for q in range(n):
            dh = dh_refs[q][...]
            dxh = dh * g_refs[q][...]
            dx = dx + r * (dxh - xh * jnp.mean(dxh * xh, axis=-1, keepdims=True))
            part = jnp.sum(dh * xh, axis=0, keepdims=True)

            @pl.when(i == 0)
            def _():
                dg_refs[q][...] = part

            @pl.when(i > 0)
            def _():
                dg_refs[q][...] += part

        dx_ref[...] = dx
        if want_colsum:
            cpart = jnp.sum(dx, axis=0, keepdims=True)

            @pl.when(i == 0)
            def _():
                cs_ref[...] = cpart

            @pl.when(i > 0)
            def _():
                cs_ref[...] += cpart

    row = pl.BlockSpec((tr, D), lambda i: (i, 0))
    vec = pl.BlockSpec((1, D), lambda i: (0, 0))
    n_vec_out = n + (1 if want_colsum else 0)
    outs = pl.pallas_call(
        body, name=name, grid=(steps,), in_specs=[row] + [vec] * n + [row] * n + [row],
        out_specs=[row] + [vec] * n_vec_out,
        out_shape=[jax.ShapeDtypeStruct((S, D), F32)] + [jax.ShapeDtypeStruct((1, D), F32)] * n_vec_out,
        compiler_params=_cp(1),
    )(x, *gains, *dhs, dres)
    return outs


def _colsum(x, *, name, tr=256):
    S, D = x.shape

    def body(x_ref, o_ref):
        i = pl.program_id(0)
        part = jnp.sum(x_ref[...].astype(F32), axis=0, keepdims=True)

        @pl.when(i == 0)
        def _():
            o_ref[...] = part

        @pl.when(i > 0)
        def _():
            o_ref[...] += part

    return pl.pallas_call(
        body, name=name, grid=(S // tr,), in_specs=[pl.BlockSpec((tr, D), lambda i: (i, 0))],
        out_specs=pl.BlockSpec((1, D), lambda i: (0, 0)), out_shape=jax.ShapeDtypeStruct((1, D), F32),
        compiler_params=_cp(1),
    )(x)


STRIP = 64
HALO = 8


def _strips(S, tc):
    return [(r0, slice(l0, l0 + 128)) for l0 in range(0, tc, 128) for r0 in range(S - STRIP, -1, -STRIP)]


def _with_halo(ref, r0, ls):
    if r0 == 0:
        return jnp.concatenate([jnp.zeros((HALO, 128), F32), ref[0:STRIP, ls]], axis=0)
    return ref[r0 - HALO:r0 + STRIP, ls]


def _conv_strip(xw, w_ref, b_ref, ls, width):
    acc = b_ref[:, ls] + w_ref[pl.ds(width - 1, 1), ls] * xw[HALO:]
    shifted = []
    for s in range(1, width):
        xs = pltpu.roll(xw, s, axis=0)[HALO:]
        shifted.append(xs)
        acc = acc + w_ref[pl.ds(width - 1 - s, 1), ls] * xs
    return acc, shifted


def _conv_strip_back(dacc, after, xc, shifted, w_ref, ls, width):
    ext = jnp.concatenate([dacc, after], axis=0)
    dx = w_ref[pl.ds(width - 1, 1), ls] * dacc
    dws = [None] * width
    dws[width - 1] = jnp.sum(dacc * xc, axis=0, keepdims=True)
    for s in range(1, width):
        dx = dx + w_ref[pl.ds(width - 1 - s, 1), ls] * pltpu.roll(ext, STRIP + HALO - s, axis=0)[:STRIP]
        dws[width - 1 - s] = jnp.sum(dacc * shifted[s - 1], axis=0, keepdims=True)
    return dx, dws, jnp.sum(dacc, axis=0, keepdims=True)


def _conv_back_block(S, tc, width, w_ref, b_ref, x_ref, dacc_of, dx_store, dw_ref, db_ref):
    for l0 in range(0, tc, 128):
        ls = slice(l0, l0 + 128)
        after = jnp.zeros((HALO, 128), F32)
        tot = None
        for r0 in range(S - STRIP, -1, -STRIP):
            xw = _with_halo(x_ref, r0, ls)
            acc, shifted = _conv_strip(xw, w_ref, b_ref, ls, width)
            dacc = dacc_of(r0, ls, acc, _sigmoid(acc))
            dx, dws, db = _conv_strip_back(dacc, after, xw[HALO:], shifted, w_ref, ls, width)
            dx_store(r0, ls, dx)
            after = dacc[:HALO]
            part = dws + [db]
            tot = part if tot is None else [p + q for p, q in zip(tot, part)]
        for k in range(width):
            dw_ref[pl.ds(k, 1), ls] = tot[k]
        db_ref[:, ls] = tot[width]


def _conv_silu_fwd(xin, col0, C, w, b, *, name, tc=512):
    S = xin.shape[0]
    width = w.shape[0]
    off = col0 // tc

    def body(x_ref, w_ref, b_ref, o_ref):
        for r0, ls in _strips(S, tc):
            acc, _ = _conv_strip(_with_halo(x_ref, r0, ls), w_ref, b_ref, ls, width)
            o_ref[r0:r0 + STRIP, ls] = acc * _sigmoid(acc)

    return pl.pallas_call(
        body, name=name, grid=(C // tc,),
        in_specs=[pl.BlockSpec((S, tc), lambda j: (0, j + off)), pl.BlockSpec((width, tc), lambda j: (0, j)),
                  pl.BlockSpec((1, tc), lambda j: (0, j))],
        out_specs=pl.BlockSpec((S, tc), lambda j: (0, j)), out_shape=jax.ShapeDtypeStruct((S, C), F32),
        compiler_params=_cp(1),
    )(xin, w, b)


def _conv_silu_bwd(xin, col0, C, w, b, douts, *, name, tc=256):
    S = xin.shape[0]
    width = w.shape[0]
    off = col0 // tc
    nd = len(douts)
    ranges = [(o // tc, (o + d.shape[1]) // tc) for d, o in douts]

    def body(*refs):
        x_ref, w_ref, b_ref = refs[0], refs[1], refs[2]
        d_refs = refs[3:3 + nd]
        dx_ref, dw_ref, db_ref = refs[3 + nd], refs[4 + nd], refs[5 + nd]
        j = pl.program_id(0)

        def dacc_of(r0, ls, acc, sg):
            dout = jnp.zeros((STRIP, 128), F32)
            for q in range(nd):
                lo, hi = ranges[q]
                dout = dout + jnp.where((j >= lo) & (j < hi), d_refs[q][r0:r0 + STRIP, ls], 0.0)
            return dout * (sg * (1.0 + acc * (1.0 - sg)))

        def dx_store(r0, ls, dx):
            dx_ref[r0:r0 + STRIP, ls] = dx.astype(BF16)

        _conv_back_block(S, tc, width, w_ref, b_ref, x_ref, dacc_of, dx_store, dw_ref, db_ref)

    d_specs = [pl.BlockSpec((S, tc), (lambda j, lo=lo, hi=hi: (0, jnp.clip(j - lo, 0, hi - lo - 1)))) for lo, hi in ranges]
    return pl.pallas_call(
        body, name=name, grid=(C // tc,),
        in_specs=[pl.BlockSpec((S, tc), lambda j: (0, j + off)), pl.BlockSpec((width, tc), lambda j: (0, j)),
                  pl.BlockSpec((1, tc), lambda j: (0, j))] + d_specs,
        out_specs=[pl.BlockSpec((S, tc), lambda j: (0, j)), pl.BlockSpec((width, tc), lambda j: (0, j)),
                   pl.BlockSpec((1, tc), lambda j: (0, j))],
        out_shape=[jax.ShapeDtypeStruct((S, C), BF16), jax.ShapeDtypeStruct((width, C), F32),
                   jax.ShapeDtypeStruct((1, C), F32)],
        compiler_params=_cp(1),
    )(xin, w, b, *[d for d, _ in douts])


def _ffn_act_fwd(u, w, b, *, name, tc=256):
    S, F2 = u.shape
    Fd = F2 // 2
    width = w.shape[0]
    nb = Fd // tc

    def body(g_ref, v_ref, w_ref, b_ref, o_ref):
        for r0, ls in _strips(S, tc):
            acc, _ = _conv_strip(_with_halo(g_ref, r0, ls), w_ref, b_ref, ls, width)
            o_ref[r0:r0 + STRIP, ls] = (acc * _sigmoid(acc) * v_ref[r0:r0 + STRIP, ls]).astype(BF16)

    return pl.pallas_call(
        body, name=name, grid=(nb,),
        in_specs=[pl.BlockSpec((S, tc), lambda j: (0, j)), pl.BlockSpec((S, tc), lambda j: (0, j + nb)),
                  pl.BlockSpec((width, tc), lambda j: (0, j)), pl.BlockSpec((1, tc), lambda j: (0, j))],
        out_specs=pl.BlockSpec((S, tc), lambda j: (0, j)), out_shape=jax.ShapeDtypeStruct((S, Fd), BF16),
        compiler_params=_cp(1),
    )(u, u, w, b)


def _ffn_act_bwd(u, w, b, da, *, name, tc=256):
    S, F2 = u.shape
    Fd = F2 // 2
    width = w.shape[0]
    nb = Fd // tc

    def body(g_ref, v_ref, w_ref, b_ref, da_ref, du_ref, dw_ref, db_ref, a_ref):
        def dacc_of(r0, ls, acc, sg):
            rs = slice(r0, r0 + STRIP)
            dav, val, silu = da_ref[rs, ls], v_ref[rs, ls], acc * sg
            a_ref[rs, ls] = (silu * val).astype(BF16)
            du_ref[1, rs, ls] = (dav * silu).astype(BF16)
            return dav * val * (sg * (1.0 + acc * (1.0 - sg)))

        def dx_store(r0, ls, dx):
            du_ref[0, r0:r0 + STRIP, ls] = dx.astype(BF16)

        _conv_back_block(S, tc, width, w_ref, b_ref, g_ref, dacc_of, dx_store, dw_ref, db_ref)

    blk = pl.BlockSpec((S, tc), lambda j: (0, j))
    return pl.pallas_call(
        body, name=name, grid=(nb,),
        in_specs=[blk, pl.BlockSpec((S, tc), lambda j: (0, j + nb)), pl.BlockSpec((width, tc), lambda j: (0, j)),
                  pl.BlockSpec((1, tc), lambda j: (0, j)), blk],
        out_specs=[pl.BlockSpec((2, S, tc), lambda j: (0, 0, j)), pl.BlockSpec((width, tc), lambda j: (0, j)),
                   pl.BlockSpec((1, tc), lambda j: (0, j)), blk],
        out_shape=[jax.ShapeDtypeStruct((2, S, Fd), BF16),
                   jax.ShapeDtypeStruct((width, Fd), F32), jax.ShapeDtypeStruct((1, Fd), F32),
                   jax.ShapeDtypeStruct((S, Fd), BF16)],
        compiler_params=_cp(1),
    )(u, u, w, b, da)


def _ssd_prep(dtr, dt_bias, a_log, *, name="ssd_prep"):
    S = dtr.shape[0]

    def body(d_ref, b_ref, al_ref, dt_ref, ac_ref, sg_ref, act_ref):
        lane = _iota((CHUNK, 128), 1)
        valid = lane < SSM_HEADS
        z = d_ref[...] + b_ref[...]
        dt = jnp.where(valid, jnp.maximum(z, 0.0) + jnp.log(1.0 + jnp.exp(-jnp.abs(z))), 0.0)
        a = dt * (-jnp.exp(al_ref[...]))
        row = _iota((CHUNK, 128), 0)
        k = 1
        while k < CHUNK:
            a = a + jnp.where(row >= k, pltpu.roll(a, k, axis=0), 0.0)
            k *= 2
        sg = jnp.where(valid, _sigmoid(z), 0.0)
        for arr, ref in ((dt, dt_ref), (a, ac_ref), (sg, sg_ref)):
            for g in range(SSM_GROUPS):
                ref[g] = jnp.where(lane < 4, arr if g == 0 else pltpu.roll(arr, 128 - 4 * g, axis=1), 0.0)
        act_ref[...] = a.T[:SSM_HEADS, :]

    blk = pl.BlockSpec((CHUNK, 128), lambda i: (i, 0))
    vec = pl.BlockSpec((1, 128), lambda i: (0, 0))
    grp = pl.BlockSpec((SSM_GROUPS, CHUNK, 128), lambda i: (0, i, 0))
    return pl.pallas_call(
        body, name=name, grid=(S // CHUNK,), in_specs=[blk, vec, vec],
        out_specs=[grp, grp, grp, pl.BlockSpec((SSM_HEADS, CHUNK), lambda i: (0, i))],
        out_shape=[jax.ShapeDtypeStruct((SSM_GROUPS, S, 128), F32)] * 3 + [jax.ShapeDtypeStruct((SSM_HEADS, S), F32)],
        compiler_params=_cp(1),
    )(dtr, dt_bias, a_log)


SSD_GPS = 4


def _expand4(v, lanes):
    out = jnp.broadcast_to(v[:, 3:4], lanes.shape)
    for hh in (2, 1, 0):
        out = jnp.where(lanes < 64 * (hh + 1), v[:, hh:hh + 1], out)
    return out


def _ssd_fwd(xbc, dt_g, ac_g, ac_t, *, name="ssd_fwd", dep=None):
    S = xbc.shape[0]
    nc = S // CHUNK
    Lc = CHUNK

    def body(x_ref, b_ref, c_ref, dt_ref, ac_ref, act_ref, *rest):
        y_ref, st_out_ref, st_ref = rest[-3:]
        g2 = pl.program_id(0)
        c = pl.program_id(1)

        @pl.when(c == 0)
        def _():
            st_ref[...] = jnp.zeros_like(st_ref)

        causal = _iota((Lc, Lc), 0) >= _iota((Lc, Lc), 1)
        lane256 = _iota((Lc, 256), 1)
        lane128 = _iota((Lc, 128), 1)
        row128 = _iota((128, 128), 0)
        for gg in range(SSD_GPS):
            g = SSD_GPS * g2 + gg
            bv = b_ref[:, 128 * gg:128 * (gg + 1)]
            cbf = c_ref[:, 128 * gg:128 * (gg + 1)].astype(BF16)
            cb = _dot(cbf, bv.astype(BF16), 1, 1)
            dtg, acg = dt_ref[gg], ac_ref[gg]
            ac_last = ac_ref[gg, pl.ds(Lc - 1, 1), :]
            dt4 = _expand4(dtg, lane256)
            ac4 = _expand4(acg, lane256)
            e4 = jnp.exp(ac4)
            xdb = (x_ref[:, 256 * gg:256 * (gg + 1)] * dt4).astype(BF16)
            st_out_ref[gg] = st_ref[gg]
            for p in range(2):
                xd_p = xdb[:, 128 * p:128 * (p + 1)]
                st_p = st_ref[gg, p]
                ys, sn, cds = [], [], []
                for q in range(2):
                    hh = 2 * p + q
                    a_col = acg[:, hh:hh + 1]
                    a_row = act_ref[pl.ds(4 * g + hh, 1), :]
                    dec = jnp.exp(jnp.where(causal, a_col - a_row, NEG))
                    w = (cb * dec).astype(BF16)
                    ys.append(_dot(w, xd_p))
                    al = ac_last[:, hh:hh + 1]
                    dte = jnp.exp(al - a_col)
                    sn.append(_dot(xd_p, (bv * dte).astype(BF16), 0, 0))
                    cds.append(jnp.exp(al))
                y_diag = jnp.where(lane128 < 64, ys[0], ys[1])
                y_off = _dot(cbf, st_p.astype(BF16), 1, 1) * e4[:, 128 * p:128 * (p + 1)]
                y_ref[:, 256 * gg + 128 * p:256 * gg + 128 * (p + 1)] = y_diag + y_off
                st_ref[gg, p] = jnp.where(row128 < 64, st_p * cds[0] + sn[0], st_p * cds[1] + sn[1])

    G = SSD_GPS
    per_g = lambda g, c: (g, c, 0)
    return pl.pallas_call(
        body, name=name, grid=(SSM_GROUPS // G, nc),
        in_specs=[pl.BlockSpec((Lc, 256 * G), lambda g, c: (c, g)),
                  pl.BlockSpec((Lc, 128 * G), lambda g, c: (c, 16 // G + g)),
                  pl.BlockSpec((Lc, 128 * G), lambda g, c: (c, 24 // G + g)),
                  pl.BlockSpec((G, Lc, 128), per_g), pl.BlockSpec((G, Lc, 128), per_g),
                  pl.BlockSpec((SSM_HEADS, Lc), lambda g, c: (0, c))] + ([] if dep is None else [pl.BlockSpec(memory_space=pl.ANY)]),
        out_specs=[pl.BlockSpec((Lc, 256 * G), lambda g, c: (c, g)),
                   pl.BlockSpec((G, None, 2, 128, 128), lambda g, c: (g, c, 0, 0, 0))],
        out_shape=[jax.ShapeDtypeStruct((S, 2048), F32), jax.ShapeDtypeStruct((SSM_GROUPS, nc, 2, 128, 128), F32)],
        scratch_shapes=[pltpu.VMEM((G, 2, 128, 128), F32)], compiler_params=_cp(2),
    )(xbc, xbc, xbc, dt_g, ac_g, ac_t, *([] if dep is None else [dep]))


def _ssd_bwd(xbc, dt_g, ac_g, ac_t, states, dy, dexp, *, name="ssd_bwd", dep=None):
    S = xbc.shape[0]
    nc = S // CHUNK
    Lc = CHUNK

    def body(x_ref, b_ref, c_ref, dt_ref, ac_ref, act_ref, st_ref, dy_ref, d_ref, *rest):
        dx_ref, db_ref, dc_ref, dh_ref, ds_ref = rest[-5:]
        g2 = pl.program_id(0)
        cc = pl.program_id(1)

        @pl.when(cc == 0)
        def _():
            ds_ref[...] = jnp.zeros_like(ds_ref)

        causal = _iota((Lc, Lc), 0) >= _iota((Lc, Lc), 1)
        lane256 = _iota((Lc, 256), 1)
        lane128 = _iota((Lc, 128), 1)
        row128 = _iota((128, 128), 0)
        ind_rows = _iota((256, 128), 0) >> 6
        ind_cols = _iota((256, 128), 1)
        ind_a = (ind_rows == ind_cols).astype(BF16)
        ind_b = (ind_rows + 4 == ind_cols).astype(BF16)
        for gg in range(SSD_GPS):
            g = SSD_GPS * g2 + gg
            bv = b_ref[:, 128 * gg:128 * (gg + 1)]
            cv = c_ref[:, 128 * gg:128 * (gg + 1)]
            bbf, cbf = bv.astype(BF16), cv.astype(BF16)
            cb = _dot(cbf, bbf, 1, 1)
            dtg, acg = dt_ref[gg], ac_ref[gg]
            ac_last = ac_ref[gg, pl.ds(Lc - 1, 1), :]
            dt4 = _expand4(dtg, lane256)
            ac4 = _expand4(acg, lane256)
            acl4 = _expand4(ac_last, _iota((1, 256), 1))
            e4 = jnp.exp(ac4)
            dte4 = jnp.exp(acl4 - ac4)
            xv = x_ref[:, 256 * gg:256 * (gg + 1)]
            xd = xv * dt4
            xdb = xd.astype(BF16)
            dyv = dy_ref[:, 256 * gg:256 * (gg + 1)]
            dcb = jnp.zeros((Lc, Lc), F32)
            dc_acc = jnp.zeros((Lc, 128), F32)
            db_acc = jnp.zeros((Lc, 128), F32)
            u_parts, dxd_parts, ends = [], [], []
            for p in range(2):
                sl = slice(128 * p, 128 * (p + 1))
                xd_p, xdb_p, dy_p = xd[:, sl], xdb[:, sl], dyv[:, sl]
                dyb_p = dy_p.astype(BF16)
                e_p, dte_p = e4[:, sl], dte4[:, sl]
                sp = st_ref[gg, p]
                spb = sp.astype(BF16)
                dsn = ds_ref[gg, p]
                dsnb = dsn.astype(BF16)
                yds, dxds, cds = [], [], []
                for q in range(2):
                    hh = 2 * p + q
                    a_col = acg[:, hh:hh + 1]
                    a_row = act_ref[pl.ds(4 * g + hh, 1), :]
                    dec = jnp.exp(jnp.where(causal, a_col - a_row, NEG))
                    w = (cb * dec).astype(BF16)
                    head = (lane128 < 64) if q == 0 else (lane128 >= 64)
                    dym = jnp.where(head, dyb_p, jnp.zeros_like(dyb_p))
                    dw = _dot(dym, xdb_p, 1, 1)
                    dcb = dcb + dw * dec
                    yds.append(_dot(w, xdb_p))
                    dxds.append(_dot(w, dyb_p, 0, 0))
                    cds.append(jnp.exp(ac_last[:, hh:hh + 1]))
                y_diag = jnp.where(lane128 < 64, yds[0], yds[1])
                dxd_diag = jnp.where(lane128 < 64, dxds[0], dxds[1])
                y_off = _dot(cbf, spb, 1, 1) * e_p
                dgp = dy_p * e_p
                dgb = dgp.astype(BF16)
                dc_acc = dc_acc + _dot(dgb, spb)
                dsp = _dot(dgb, cbf, 0, 0)
                cd_col = jnp.where(row128[:, 0:1] < 64, cds[0], cds[1])
                qm = _dot(bbf, dsnb, 1, 1)
                dxd_state = dte_p * qm
                db_acc = db_acc + _dot((xd_p * dte_p).astype(BF16), dsnb)
                t_p = xd_p * dxd_state
                prod = dsn * sp
                e0 = jnp.sum(jnp.sum(jnp.where(row128 < 64, prod, 0.0), axis=1, keepdims=True), axis=0, keepdims=True)
                e1 = jnp.sum(jnp.sum(jnp.where(row128 >= 64, prod, 0.0), axis=1, keepdims=True), axis=0, keepdims=True)
                tcol = jnp.sum(t_p, axis=0, keepdims=True)
                lane1 = _iota((1, 128), 1)
                t0 = jnp.sum(jnp.where(lane1 < 64, tcol, 0.0), axis=1, keepdims=True)
                t1 = jnp.sum(jnp.where(lane1 >= 64, tcol, 0.0), axis=1, keepdims=True)
                ends.append(e0 * cds[0] + t0)
                ends.append(e1 * cds[1] + t1)
                ds_ref[gg, p] = dsn * cd_col + dsp
                u_parts.append(dyb_p.astype(F32) * y_diag - xdb_p.astype(F32) * dxd_diag + dy_p * y_off - t_p)
                dxd_parts.append(dxd_diag + dxd_state)
            dxd = jnp.concatenate(dxd_parts, axis=1)
            u_all = jnp.concatenate(u_parts, axis=1)
            dx_ref[:, 256 * gg:256 * (gg + 1)] = dxd * dt4 + dyv * d_ref[:, 256 * gg:256 * (gg + 1)]
            dcbb = dcb.astype(BF16)
            dc_ref[:, 128 * gg:128 * (gg + 1)] = dc_acc + _dot(dcbb, bbf)
            db_ref[:, 128 * gg:128 * (gg + 1)] = db_acc + _dot(dcbb, cbf, 0, 0)
            lane = _iota((Lc, 128), 1)
            endv = jnp.zeros((Lc, 128), F32)
            for hh in range(4):
                endv = jnp.where(lane == 8 + hh, ends[hh], endv)
            dh_ref[gg] = _dot3(dxd * xv, ind_a) + _dot3(u_all, ind_b) + endv

    G = SSD_GPS
    rev = lambda c: nc - 1 - c
    per_g = lambda g, c: (g, rev(c), 0)
    return pl.pallas_call(
        body, name=name, grid=(SSM_GROUPS // G, nc),
        in_specs=[pl.BlockSpec((Lc, 256 * G), lambda g, c: (rev(c), g)),
                  pl.BlockSpec((Lc, 128 * G), lambda g, c: (rev(c), 16 // G + g)),
                  pl.BlockSpec((Lc, 128 * G), lambda g, c: (rev(c), 24 // G + g)),
                  pl.BlockSpec((G, Lc, 128), per_g), pl.BlockSpec((G, Lc, 128), per_g),
                  pl.BlockSpec((SSM_HEADS, Lc), lambda g, c: (0, rev(c))),
                  pl.BlockSpec((G, None, 2, 128, 128), lambda g, c: (g, rev(c), 0, 0, 0)),
                  pl.BlockSpec((Lc, 256 * G), lambda g, c: (rev(c), g)),
                  pl.BlockSpec((1, 256 * G), lambda g, c: (0, g))] + ([] if dep is None else [pl.BlockSpec(memory_space=pl.ANY)]),
        out_specs=[pl.BlockSpec((Lc, 256 * G), lambda g, c: (rev(c), g)),
                   pl.BlockSpec((Lc, 128 * G), lambda g, c: (rev(c), g)),
                   pl.BlockSpec((Lc, 128 * G), lambda g, c: (rev(c), g)),
                   pl.BlockSpec((G, Lc, 128), per_g)],
        out_shape=[jax.ShapeDtypeStruct((S, 2048), F32), jax.ShapeDtypeStruct((S, 1024), F32),
                   jax.ShapeDtypeStruct((S, 1024), F32), jax.ShapeDtypeStruct((SSM_GROUPS, S, 128), F32)],
        scratch_shapes=[pltpu.VMEM((G, 2, 128, 128), F32)], compiler_params=_cp(2),
    )(xbc, xbc, xbc, dt_g, ac_g, ac_t, states, dy, dexp, *([] if dep is None else [dep]))


def _ssd_post(dhead, dt_g, sg_g, alog_g, *, name="ssd_post"):
    S = dhead.shape[1]
    nc = S // CHUNK
    Lc = CHUNK

    def body(dh_ref, dt_ref, sg_ref, al_ref, o_ref, s_ref):
        @pl.when(pl.program_id(0) == 0)
        def _():
            s_ref[...] = jnp.zeros_like(s_ref)

        lane = _iota((Lc, 128), 1)
        row = _iota((Lc, 128), 0)
        row8 = _iota((8, 128), 0)
        out = jnp.zeros((Lc, 128), F32)
        for g in range(SSM_GROUPS):
            dh = dh_ref[g]
            a_neg = -jnp.exp(al_ref[g])
            dac = jnp.where(lane < 4, pltpu.roll(dh, 124, axis=1), 0.0)
            end = jnp.where(lane < 4, pltpu.roll(dh, 120, axis=1), 0.0)
            k = 1
            while k < Lc:
                dac = dac + jnp.where(row < Lc - k, pltpu.roll(dac, Lc - k, axis=0), 0.0)
                k *= 2
            da = dac + end
            ddt = jnp.where(lane < 4, da * a_neg + dh, 0.0)
            ddtr = ddt * sg_ref[g]
            out = out + (ddtr if g == 0 else pltpu.roll(ddtr, 4 * g, axis=1))
            dal = jnp.sum(da * dt_ref[g], axis=0, keepdims=True) * a_neg
            dbias = jnp.sum(ddtr, axis=0, keepdims=True)
            part = jnp.where(row8 == 0, dal, jnp.where(row8 == 1, dbias, 0.0))
            s_ref[g] += part
        o_ref[...] = out.astype(BF16)

    grp = pl.BlockSpec((SSM_GROUPS, Lc, 128), lambda c: (0, c, 0))
    whole = lambda r: pl.BlockSpec((SSM_GROUPS, r, 128), lambda c: (0, 0, 0))
    return pl.pallas_call(
        body, name=name, grid=(nc,), in_specs=[grp, grp, grp, whole(1)],
        out_specs=[pl.BlockSpec((Lc, 128), lambda c: (c, 0)), whole(8)],
        out_shape=[jax.ShapeDtypeStruct((S, 128), BF16), jax.ShapeDtypeStruct((SSM_GROUPS, 8, 128), F32)],
        compiler_params=_cp(1),
    )(dhead, dt_g, sg_g, alog_g)


def _gate_fwd(y, xbc, zx, dexp, gn, *, name="gate_fwd", tr=256, dep=None):
    S = y.shape[0]
    W = 2048
    gw = W // SSM_GROUPS

    def body(y_ref, x_ref, z_ref, d_ref, g_ref, *rest):
        o_ref = rest[-1]
        z = z_ref[...]
        u = (y_ref[...] + x_ref[...] * d_ref[...]) * (z * _sigmoid(z))
        gv = g_ref[...]
        for q in range(SSM_GROUPS):
            sl = slice(gw * q, gw * (q + 1))
            uq = u[:, sl]
            r = lax.rsqrt(jnp.mean(uq * uq, axis=-1, keepdims=True) + EPS)
            o_ref[:, sl] = (uq * r * gv[:, sl]).astype(BF16)

    row = pl.BlockSpec((tr, W), lambda i: (i, 0))
    vec = pl.BlockSpec((1, W), lambda i: (0, 0))
    return pl.pallas_call(
        body, name=name, grid=(S // tr,),
        in_specs=[row, row, row, vec, vec] + ([] if dep is None else [pl.BlockSpec(memory_space=pl.ANY)]), out_specs=row,
        out_shape=jax.ShapeDtypeStruct((S, W), BF16), compiler_params=_cp(1),
    )(y, xbc, zx, dexp, gn, *([] if dep is None else [dep]))


def _gate_bwd(y, xbc, zx, dexp, gn, dout, *, name="gate_bwd", tr=256):
    S = y.shape[0]
    W = 2048
    gw = W // SSM_GROUPS
    steps = S // tr

    def body(y_ref, x_ref, z_ref, d_ref, g_ref, do_ref, dy_ref, dz_ref, dg_ref, dd_ref, acc_ref):
        i = pl.program_id(0)

        @pl.when(i == 0)
        def _():
            acc_ref[...] = jnp.zeros_like(acc_ref)

        z = z_ref[...]
        sg = _sigmoid(z)
        sz = z * sg
        xs = x_ref[...]
        yt = y_ref[...] + xs * d_ref[...]
        u = yt * sz
        gv = g_ref[...]
        do = do_ref[...]
        dgs = []
        for q in range(SSM_GROUPS):
            sl = slice(gw * q, gw * (q + 1))
            uq = u[:, sl]
            r = lax.rsqrt(jnp.mean(uq * uq, axis=-1, keepdims=True) + EPS)
            uh = uq * r
            dq = do[:, sl]
            duh = dq * gv[:, sl]
            duq = r * (duh - uh * jnp.mean(duh * uh, axis=-1, keepdims=True))
            dgs.append(jnp.sum(dq * uh, axis=0, keepdims=True))
            dyt = duq * sz[:, sl]
            dy_ref[:, sl] = dyt
            dz_ref[:, sl] = (duq * yt[:, sl] * (sg[:, sl] * (1.0 + z[:, sl] * (1.0 - sg[:, sl])))).astype(BF16)
            acc_ref[:, sl] += jnp.sum(dyt * xs[:, sl], axis=0, keepdims=True)
        dg = jnp.concatenate(dgs, axis=1)

        @pl.when(i == 0)
        def _():
            dg_ref[...] = dg

        @pl.when(i > 0)
        def _():
            dg_ref[...] += dg

        @pl.when(i == steps - 1)
        def _():
            ind = ((_iota((W, 128), 0) >> 6) == _iota((W, 128), 1)).astype(BF16)
            dd_ref[...] = _dot3(jnp.broadcast_to(acc_ref[...], (8, W)), ind)[0:1, :]

    row = pl.BlockSpec((tr, W), lambda i: (i, 0))
    vec = pl.BlockSpec((1, W), lambda i: (0, 0))
    return pl.pallas_call(
        body, name=name, grid=(steps,), in_specs=[row, row, row, vec, vec, row],
        out_specs=[row, row, vec, pl.BlockSpec((1, 128), lambda i: (0, 0))],
        out_shape=[jax.ShapeDtypeStruct((S, W), F32), jax.ShapeDtypeStruct((S, W), BF16),
                   jax.ShapeDtypeStruct((1, W), F32), jax.ShapeDtypeStruct((1, 128), F32)],
        scratch_shapes=[pltpu.VMEM((1, W), F32)], compiler_params=_cp(1),
    )(y, xbc, zx, dexp, gn, dout)


def _rope_cs(posf, *, name="rope_tables", tr=256):
    S = posf.shape[0]

    def body(p_ref, c_ref, s_ref):
        j = (_iota((tr, 128), 1) & 31).astype(F32)
        ang = p_ref[...] * jnp.exp(j * (-math.log(ROPE_THETA) / 32.0))
        c_ref[...] = jnp.cos(ang)
        s_ref[...] = jnp.sin(ang)

    blk = pl.BlockSpec((tr, 128), lambda i: (i, 0))
    return pl.pallas_call(
        body, name=name, grid=(S // tr,), in_specs=[pl.BlockSpec((tr, 1), lambda i: (i, 0))], out_specs=[blk, blk],
        out_shape=[jax.ShapeDtypeStruct((S, 128), F32)] * 2, compiler_params=_cp(1),
    )(posf)


def _rope_tables(c_ref, s_ref, shape):
    reps = shape[1] // 128
    return jnp.tile(c_ref[...], (1, reps)), jnp.tile(s_ref[...], (1, reps)), (_iota(shape, 1) & 63) < 32


def _hn_inds(W):
    ind = ((_iota((W, 128), 0) >> 6) == _iota((W, 128), 1)).astype(BF16)
    ind_t = ((_iota((128, W), 1) >> 6) == _iota((128, W), 0)).astype(BF16)
    return ind, ind_t


def _hnrope_fwd(xin, col0, W, gain_w, rope, *, name, tr=256):
    S = xin.shape[0]
    off = col0 // W
    nh = W // HEAD

    def body(x_ref, g_ref, c_ref, s_ref, o_ref):
        x = x_ref[...]
        ind, ind_t = _hn_inds(W)
        r = lax.rsqrt(_dot3(x * x, ind) * (1.0 / HEAD) + EPS)
        xn = x * _dot3(r, ind_t) * g_ref[...]
        cs, sn, half = _rope_tables(c_ref, s_ref, (tr, W))
        rot = jnp.where(half, -pltpu.roll(xn, W - 32, axis=1), pltpu.roll(xn, 32, axis=1))
        out = (xn * cs + rot * sn).astype(BF16)
        for h in range(nh):
            o_ref[h] = out[:, HEAD * h:HEAD * (h + 1)]

    tab = pl.BlockSpec((tr, 128), lambda i: (i, 0))
    return pl.pallas_call(
        body, name=name, grid=(S // tr,),
        in_specs=[pl.BlockSpec((tr, W), lambda i: (i, off)), pl.BlockSpec((1, W), lambda i: (0, 0)), tab, tab],
        out_specs=pl.BlockSpec((nh, tr, HEAD), lambda i: (0, i, 0)), out_shape=jax.ShapeDtypeStruct((nh, S, HEAD), BF16),
        compiler_params=_cp(1),
    )(xin, gain_w, *rope)


def _hnrope_bwd(xin, col0, W, gain_w, rope, dout, *, name, tr=256):
    S = xin.shape[0]
    off = col0 // W
    steps = S // tr
    nh = W // HEAD

    def body(x_ref, g_ref, c_ref, s_ref, do_ref, dx_ref, cs_ref, dg_ref, acc_ref):
        i = pl.program_id(0)
        x = x_ref[...]
        ind, ind_t = _hn_inds(W)
        r = lax.rsqrt(_dot3(x * x, ind) * (1.0 / HEAD) + EPS)
        rw = _dot3(r, ind_t)
        xh = x * rw
        cs, sn, half = _rope_tables(c_ref, s_ref, (tr, W))
        do = jnp.concatenate([do_ref[h] for h in range(nh)], axis=1).astype(F32)
        gs = do * sn
        g1 = do * cs + jnp.where(half, pltpu.roll(gs, W - 32, axis=1), -pltpu.roll(gs, 32, axis=1))
        dxh = g1 * g_ref[...]
        t = _dot3(dxh * xh, ind) * (1.0 / HEAD)
        dx = rw * (dxh - xh * _dot3(t, ind_t))
        dx_ref[...] = dx.astype(BF16)
        cpart = jnp.sum(dx, axis=0, keepdims=True)
        gpart = jnp.sum(g1 * xh, axis=0, keepdims=True)

        @pl.when(i == 0)
        def _():
            cs_ref[...] = cpart
            acc_ref[...] = gpart

        @pl.when(i > 0)
        def _():
            cs_ref[...] += cpart
            acc_ref[...] += gpart

        @pl.when(i == steps - 1)
        def _():
            fold = ((_iota((W, 128), 0) & 63) == _iota((W, 128), 1)).astype(BF16)
            dg_ref[...] = _dot3(jnp.broadcast_to(acc_ref[...], (8, W)), fold)[0:1, :]

    tab = pl.BlockSpec((tr, 128), lambda i: (i, 0))
    return pl.pallas_call(
        body, name=name, grid=(steps,),
        in_specs=[pl.BlockSpec((tr, W), lambda i: (i, off)), pl.BlockSpec((1, W), lambda i: (0, 0)), tab, tab,
                  pl.BlockSpec((nh, tr, HEAD), lambda i: (0, i, 0))],
        out_specs=[pl.BlockSpec((tr, W), lambda i: (i, 0)), pl.BlockSpec((1, W), lambda i: (0, 0)),
                   pl.BlockSpec((1, 128), lambda i: (0, 0))],
        out_shape=[jax.ShapeDtypeStruct((S, W), BF16), jax.ShapeDtypeStruct((1, W), F32),
                   jax.ShapeDtypeStruct((1, 128), F32)],
        scratch_shapes=[pltpu.VMEM((1, W), F32)], compiler_params=_cp(1),
    )(xin, gain_w, *rope, dout)


def _attn_band():
    qi = jnp.arange(ATT_G * WINDOW)[:, None] % WINDOW
    ki = jnp.arange(2 * WINDOW)[None, :]
    rel = qi + WINDOW - ki
    ok = (rel >= 0) & (rel < WINDOW)
    return jnp.stack([jnp.where(ok & (ki >= WINDOW), 0.0, NEG), jnp.where(ok, 0.0, NEG)]).astype(F32)


def _attn_probs(q, kb, sink_ref, band_ref, h, i):
    s = _dot(q, kb, 1, 1) * (HEAD ** -0.5) + band_ref[jnp.minimum(i, 1)]
    r1 = _iota((4 * WINDOW, 1), 0)
    sink = jnp.where(r1 < WINDOW, sink_ref[4 * h], jnp.where(r1 < 2 * WINDOW, sink_ref[4 * h + 1],
                     jnp.where(r1 < 3 * WINDOW, sink_ref[4 * h + 2], sink_ref[4 * h + 3])))
    m = jnp.maximum(jnp.max(s, axis=1, keepdims=True), sink)
    p = jnp.exp(s - m)
    ps = jnp.exp(sink - m)
    inv = 1.0 / (jnp.sum(p, axis=1, keepdims=True) + ps)
    return p * inv, ps * inv


ATT_HPS = 4
_BAND = pl.BlockSpec((2, ATT_G * WINDOW, 2 * WINDOW), lambda h, i: (0, 0, 0))


def _attn_specs(S):
    qspec = pl.BlockSpec((ATT_HPS, ATT_G, WINDOW, HEAD), lambda h, i: (h, 0, i, 0))
    cur = pl.BlockSpec((ATT_HPS, WINDOW, HEAD), lambda h, i: (h, i, 0))
    prev = pl.BlockSpec((ATT_HPS, WINDOW, HEAD), lambda h, i: (h, jnp.maximum(i - 1, 0), 0))
    tok = pl.BlockSpec((WINDOW, ATT_HPS * ATT_G * HEAD), lambda h, i: (i, h))
    return qspec, cur, prev, tok


def _attn_fwd(qh, kh, vh, sinks, *, name="attn_fwd"):
    S = kh.shape[1]
    nb = S // WINDOW

    def body(s_ref, band_ref, q_ref, kc_ref, kp_ref, vc_ref, vp_ref, o_ref):
        h2, i = pl.program_id(0), pl.program_id(1)
        outs = []
        for hh in range(ATT_HPS):
            q = q_ref[hh].reshape(ATT_G * WINDOW, HEAD)
            kb = jnp.concatenate([kp_ref[hh], kc_ref[hh]], axis=0)
            vb = jnp.concatenate([vp_ref[hh], vc_ref[hh]], axis=0)
            probs, _ = _attn_probs(q, kb, s_ref, band_ref, ATT_HPS * h2 + hh, i)
            o = _dot(probs.astype(BF16), vb).astype(BF16)
            outs += [o[WINDOW * g:WINDOW * (g + 1)] for g in range(ATT_G)]
        o_ref[...] = jnp.concatenate(outs, axis=1)

    qspec, cur, prev, tok = _attn_specs(S)
    return pl.pallas_call(
        body, name=name, grid=(ATT_KV // ATT_HPS, nb),
        in_specs=[pl.BlockSpec(memory_space=pltpu.SMEM), _BAND, qspec, cur, prev, cur, prev], out_specs=tok,
        out_shape=jax.ShapeDtypeStruct((S, ATT_KV * ATT_G * HEAD), BF16), compiler_params=_cp(2),
    )(sinks, _attn_band(), qh, kh, kh, vh, vh)


def _attn_bwd(qh, kh, vh, sinks, doh, *, name="attn_bwd"):
    S = kh.shape[1]
    nb = S // WINDOW

    def body(s_ref, band_ref, q_ref, kc_ref, kp_ref, vc_ref, vp_ref, do_ref, dq_ref, dk_ref, dv_ref, dsk_ref):
        h2, i = pl.program_id(0), pl.program_id(1)

        @pl.when(i == 0)
        def _():
            dk_ref[...] = jnp.zeros_like(dk_ref)
            dv_ref[...] = jnp.zeros_like(dv_ref)
            dsk_ref[...] = jnp.zeros_like(dsk_ref)

        dov = do_ref[...]
        cur = pl.multiple_of(i * WINDOW, WINDOW)
        lane = _iota((8, 128), 1)
        row = _iota((8, 128), 0)
        scale = HEAD ** -0.5
        for hh in range(ATT_HPS):
            q = q_ref[hh].reshape(ATT_G * WINDOW, HEAD)
            do = jnp.concatenate([dov[:, HEAD * (ATT_G * hh + g):HEAD * (ATT_G * hh + g + 1)] for g in range(ATT_G)], axis=0)
            kb = jnp.concatenate([kp_ref[hh], kc_ref[hh]], axis=0)
            vb = jnp.concatenate([vp_ref[hh], vc_ref[hh]], axis=0)
            probs, psink = _attn_probs(q, kb, s_ref, band_ref, ATT_HPS * h2 + hh, i)
            dp = _dot(do, vb, 1, 1)
            delta = jnp.sum(probs * dp, axis=1, keepdims=True)
            ds = (probs * (dp - delta)).astype(BF16)
            dq_ref[hh] = (_dot(ds, kb) * scale).reshape(ATT_G, WINDOW, HEAD)
            dkb = _dot(ds, q, 0, 0) * scale
            dvb = _dot(probs.astype(BF16), do, 0, 0)
            dk_ref[hh, pl.ds(cur, WINDOW), :] += dkb[WINDOW:, :]
            dv_ref[hh, pl.ds(cur, WINDOW), :] += dvb[WINDOW:, :]
            prv = pl.multiple_of(jnp.maximum(i - 1, 0) * WINDOW, WINDOW)
            dk_ref[hh, pl.ds(prv, WINDOW), :] += dkb[:WINDOW, :]
            dv_ref[hh, pl.ds(prv, WINDOW), :] += dvb[:WINDOW, :]

            dsr = -psink * delta
            upd = jnp.zeros((8, 128), F32)
            for gq in range(ATT_G):
                v = jnp.sum(dsr[gq * WINDOW:(gq + 1) * WINDOW, :], axis=0, keepdims=True)
                upd = jnp.where((lane == gq) & (row == 0), v, upd)
            dsk_ref[hh] += upd

    qspec, cur, prev, tok = _attn_specs(S)
    full = pl.BlockSpec((ATT_HPS, S, HEAD), lambda h, i: (h, 0, 0))
    return pl.pallas_call(
        body, name=name, grid=(ATT_KV // ATT_HPS, nb),
        in_specs=[pl.BlockSpec(memory_space=pltpu.SMEM), _BAND, qspec, cur, prev, cur, prev, tok],
        out_specs=[qspec, full, full, pl.BlockSpec((ATT_HPS, 8, 128), lambda h, i: (h, 0, 0))],
        out_shape=[jax.ShapeDtypeStruct((ATT_KV, ATT_G, S, HEAD), F32), jax.ShapeDtypeStruct((ATT_KV, S, HEAD), F32),
                   jax.ShapeDtypeStruct((ATT_KV, S, HEAD), F32), jax.ShapeDtypeStruct((ATT_KV, 8, 128), F32)],
        compiler_params=_cp(2),
    )(sinks, _attn_band(), qh, kh, kh, vh, vh, doh)


def _heads_major(t, nh):
    S = t.shape[0]
    return t.reshape(S, nh, HEAD).transpose(1, 0, 2)


def _tokens_major(t):
    nh, S, _ = t.shape
    return t.transpose(1, 0, 2).reshape(S, nh * HEAD)


class _NoComm:
    def late_start(self, part, after):
        return None

    def late_arrived(self, part, after):
        return None

    def late_weights(self, w, after, part):
        return w

    def advance(self, after, group=None, tensors=None):
        return None


def _local_step(x, posf, target, w, comm=None):
    S, D = x.shape
    gr = {}
    comm = comm or _NoComm()

    zx, h1 = _norm_mm(x, w["a_norm"], w["w_zx"], name="in_proj_zx", dep=w.get("dep"))
    dtr = _mm(h1, w["w_dt"], name="in_proj_dt")
    xbc = _conv_silu_fwd(zx, 2048, 4096, w["a_conv_w"], w["a_conv_b"], name="a_conv_f")
    dt_g, ac_g, sg_g, ac_t = _ssd_prep(dtr, w["a_dt_bias"], w["a_A_log"])
    y_ssd, states = _ssd_fwd(xbc, dt_g, ac_g, ac_t, dep=comm.late_start(1, xbc))
    yg = _gate_fwd(y_ssd, xbc, zx, w["a_Dexp"], w["a_gnorm"], dep=comm.late_arrived(0, [y_ssd]))
    w = comm.late_weights(w, yg, 0)
    x1 = _mm(yg, w["a_out_proj"], res=x, name="out_proj")

    FW = w["f_w_in"][0].shape[2]

    def ffn_fwd(xin, l, loss_target=None):
        u, h = _norm_mm(xin, w["f_norm"][l], w["f_w_in"][l], name=f"f_in{l}", N=N_CHIPS * FW, tn=FW,
                        b_spec=pl.BlockSpec((None, D, FW), lambda i, j: (j, 0, 0)))
        a = _ffn_act_fwd(u, w["f_conv_w"][l], w["f_conv_b"][l], name=f"f_act_f{l}")
        dep = comm.late_arrived(1, [u]) if l == 0 else None
        xo = _mm(a, w["f_w_down"][l], res=xin, tk=a.shape[1], name=f"f_down{l}", target=loss_target, dep=dep)
        return xo, (h, u)

    x2, ffn0 = ffn_fwd(x1, 0)
    w = comm.late_weights(w, x2, 1)

    kv, hk = _norm_mm(x2, w["kv_norm"], w["w_kv"], bias=w["b_kv"], name="kv_proj")
    q, hq = _norm_mm(x2, w["b_norm"], w["w_q"], bias=w["b_q"], name="q_proj")
    rope = w["rope"] if "rope" in w else _rope_cs(posf)
    kr = _hnrope_fwd(kv, 0, 256, w["k_norm_w"], rope, name="k_rope_f")
    qr = _hnrope_fwd(q, 0, 1024, w["q_norm_w"], rope, name="q_rope_f")
    qh = qr.reshape(ATT_KV, ATT_G, S, HEAD)
    kh = kr
    vh = _heads_major(kv[:, 256:].astype(BF16), ATT_KV)
    att = _attn_fwd(qh, kh, vh, w["sinks"])
    x3 = _mm(att, w["w_o"], bias=w["b_o"], res=x2, name="o_proj")
    (dy, loss_part), ffn1 = ffn_fwd(x3, 1, target)

    def ffn_bwd(xin, l, saved, dyo, want_colsum, dep=None):
        h, u = saved
        da = _mm(dyo, w["f_w_down"][l], tb=True, name=f"f_down_dx{l}", dep=dep)
        du, dcw, dcb, a = _ffn_act_bwd(u, w["f_conv_w"][l], w["f_conv_b"][l], da, name=f"f_act_b{l}")
        dw_down = _mm(a, dyo, ta=True, out_dtype=BF16, name=f"f_down_dw{l}")
        dw_in = _mm(h, du, ta=True, out_dtype=BF16, name=f"f_in_dw{l}", dims=(D, N_CHIPS * FW, S), tm=D, tn=FW, tk=S,
                    b_spec=pl.BlockSpec((None, S, FW), lambda i, j, k: (j // 2, 0, j % 2)),
                    o_spec=pl.BlockSpec((None, D, FW), lambda i, j, k: (j, i, 0)), o_shape=(N_CHIPS, D, FW))
        ts = _pick(S, (512, 256))
        pieces = [(pl.BlockSpec((None, ts, FW), lambda i, j, k, q=q: (q // 2, i, q % 2)),
                   pl.BlockSpec((None, D, FW), lambda i, j, k, q=q: (q, 0, 0), pipeline_mode=pl.Buffered(1)))
                  for q in range(N_CHIPS)]
        outs = _mm(du, w["f_w_in"][l], tb=True, name=f"f_in_dx{l}", dims=(S, D, FW), tm=ts, tn=D, tk=FW,
                   a_spec=pieces[0][0], b_spec=pieces[0][1], more=pieces[1:],
                   rms=(xin, w["f_norm"][l], dyo), rms_colsum=want_colsum)
        g = dict(f_norm=outs[1], f_w_in=dw_in, f_conv_w=dcw, f_conv_b=dcb, f_w_down=dw_down)
        return outs[0], g, (outs[2] if want_colsum else None)

    dx3, gr["ffn1"], db_o = ffn_bwd(x3, 1, ffn1, dy, True)
    gr["b_o"] = db_o
    gr["w_o"] = _mm(att, dx3, ta=True, out_dtype=BF16, name="o_proj_dw")
    datt = _mm(dx3, w["w_o"], tb=True, out_dtype=BF16, name="o_proj_dx")
    dqh, dkh, dvh, dsk = _attn_bwd(qh, kh, vh, w["sinks"], datt)
    gr["sinks"] = dsk[:, 0, :4].reshape(1, 16)
    dv = _tokens_major(dvh).astype(BF16)
    dq, db_q, dqn = _hnrope_bwd(q, 0, 1024, w["q_norm_w"], rope, dqh.reshape(16, S, HEAD), name="q_rope_b")
    dk, db_k, dkn = _hnrope_bwd(kv, 0, 256, w["k_norm_w"], rope, dkh, name="k_rope_b")
    gr["q_norm"], gr["k_norm"] = dqn[:, :HEAD], dkn[:, :HEAD]
    gr["b_q"] = db_q
    gr["b_kv"] = jnp.concatenate([db_k, _colsum(dv, name="dv_colsum")], axis=1)
    dkv = jnp.concatenate([dk, dv], axis=1)
    gr["w_q"] = _mm(hq, dq, ta=True, out_dtype=BF16, name="q_proj_dw")
    gr["w_kv"] = _mm(hk, dkv, ta=True, out_dtype=BF16, name="kv_proj_dw")
    tok = comm.advance([gr["w_kv"]], 1, dict(f_down1=gr["ffn1"]["f_w_down"], f_in1=gr["ffn1"]["f_w_in"], w_o=gr["w_o"],
                                             w_q=gr["w_q"], w_kv=gr["w_kv"]))
    tsm = _pick(S, (512, 256))
    dx2, gr["b_norm"] = _mm(dq, w["w_q"], tb=True, name="q_proj_dx", dep=tok, tm=tsm, tn=D, rms=(x2, w["b_norm"], dx3))
    dx2, gr["kv_norm"] = _mm(dkv, w["w_kv"], tb=True, name="kv_proj_dx", tm=tsm, tn=D, rms=(x2, w["kv_norm"], dx2))

    dx1, gr["ffn0"], _ = ffn_bwd(x1, 0, ffn0, dx2, False, dep=comm.advance([dx2]))

    gr["a_out_proj"] = _mm(yg, dx1, ta=True, out_dtype=BF16, name="out_proj_dw")
    tok = comm.advance([dx1, gr["a_out_proj"]], 2,
                       dict(f_down0=gr["ffn0"]["f_w_down"], f_in0=gr["ffn0"]["f_w_in"], out_proj=gr["a_out_proj"]))
    dyg = _mm(dx1, w["a_out_proj"], tb=True, name="out_proj_dx", dep=tok)
    dy_ssd, dz, gr["a_gnorm"], dD = _gate_bwd(y_ssd, xbc, zx, w["a_Dexp"], w["a_gnorm"], dyg)
    gr["a_D"] = dD[:, :SSM_HEADS]
    dxs, dB, dC, dhead = _ssd_bwd(xbc, dt_g, ac_g, ac_t, states, dy_ssd, w["a_Dexp"], dep=comm.advance([dy_ssd]))
    ddtr, dsmall = _ssd_post(dhead, dt_g, sg_g, w["a_A_log_g"])
    gr["a_A_log"] = dsmall[:, 0, :4].reshape(1, SSM_HEADS)
    gr["a_dt_bias"] = dsmall[:, 1, :4].reshape(1, SSM_HEADS)
    dxbc, gr["a_conv_w"], gr["a_conv_b"] = _conv_silu_bwd(
        zx, 2048, 4096, w["a_conv_w"], w["a_conv_b"], [(dxs, 0), (dB, 2048), (dC, 3072)], name="a_conv_b")
    gr["in_proj"] = _in_proj_dw(h1, dz, dxbc, ddtr).T
    tok = comm.advance([dxbc], 3, dict(in_proj=gr["in_proj"].reshape(D, N_CHIPS, -1).transpose(1, 0, 2)))
    ts = _pick(S, (512, 256))
    once = pl.Buffered(1)
    wblk = lambda q: pl.BlockSpec((D, 2048), lambda i, j, k: (0, q), pipeline_mode=once)
    dx0, gr["a_norm"] = _mm(
        dz, w["w_zx"], tb=True, name="in_proj_dx", dims=(S, D, 2048), tm=ts, tn=D, tk=2048,
        a_spec=pl.BlockSpec((ts, 2048), lambda i, j, k: (i, 0)), b_spec=wblk(0),
        more=[(dxbc, pl.BlockSpec((ts, 2048), lambda i, j, k: (i, 0)), w["w_zx"], wblk(1)),
              (dxbc, pl.BlockSpec((ts, 2048), lambda i, j, k: (i, 1)), w["w_zx"], wblk(2)),
              (ddtr, pl.BlockSpec((ts, 128), lambda i, j, k: (i, 0)), w["w_dt"],
               pl.BlockSpec((D, 128), lambda i, j, k: (0, 0), pipeline_mode=once))],
        rms=(x, w["a_norm"], dx1), dep=tok)
    tok = comm.advance([dx0])
    return loss_part, dx0, gr, tok


def _prep_small(full, w):
    w["a_norm"] = full["a_norm"]
    w["a_conv_w"] = full["a_conv_w"][0]
    w["a_conv_b"] = full["a_conv_b"]
    pad32 = lambda v: jnp.pad(v, ((0, 0), (0, 128 - SSM_HEADS)))
    w["a_dt_bias"] = pad32(full["a_dt_bias"])
    w["a_A_log"] = pad32(full["a_A_log"])
    w["a_A_log_g"] = jnp.pad(full["a_A_log"].reshape(SSM_GROUPS, 1, 4), ((0, 0), (0, 0), (0, 124)))
    w["a_Dexp"] = jnp.repeat(full["a_D"], HEAD, axis=1)
    w["a_gnorm"] = full["a_gnorm"]
    w["f_norm"] = [full["f_norm"][l:l + 1] for l in range(2)]
    w["f_conv_w"] = [full["f_conv_w"][l] for l in range(2)]
    w["f_conv_b"] = [full["f_conv_b"][l:l + 1] for l in range(2)]
    w["kv_norm"] = full["kv_norm"].reshape(1, -1)
    w["b_kv"] = full["b_kv"].reshape(1, -1)
    w["k_norm_w"] = jnp.tile(full["k_norm"].reshape(1, HEAD), (1, ATT_KV))
    w["b_norm"] = full["b_norm"]
    w["b_q"] = full["b_q"]
    w["q_norm_w"] = jnp.tile(full["q_norm"], (1, ATT_KV * ATT_G))
    w["sinks"] = full["sinks"].reshape(-1)
    w["b_o"] = full["b_o"]
    return w


def _split_in_proj(ip):
    return ip[:, :6144].astype(BF16), jnp.pad(ip[:, 6144:], ((0, 0), (0, 128 - SSM_HEADS))).astype(BF16)


def _join_in_proj(blocks, *, name="in_proj_join", tr=256):
    _, R, cw = blocks.shape
    zx_cols = 3 * 2048
    rest = N_CHIPS * cw - zx_cols

    def body(b_ref, zx_ref, dt_ref):
        whole = jnp.concatenate([b_ref[j] for j in range(N_CHIPS)], axis=1)
        zx_ref[...] = whole[:, :zx_cols]
        dt_ref[...] = jnp.concatenate([whole[:, zx_cols:], jnp.zeros((tr, 128 - rest), BF16)], axis=1)

    return pl.pallas_call(
        body, name=name, grid=(R // tr,), in_specs=[pl.BlockSpec((N_CHIPS, tr, cw), lambda i: (0, i, 0))],
        out_specs=[pl.BlockSpec((tr, zx_cols), lambda i: (i, 0)), pl.BlockSpec((tr, 128), lambda i: (i, 0))],
        out_shape=[jax.ShapeDtypeStruct((R, zx_cols), BF16), jax.ShapeDtypeStruct((R, 128), BF16)],
        compiler_params=_cp(1),
    )(blocks)


def _in_proj_dw(h, dz, dxbc, ddtr, *, name="in_proj_dw", tn=512):
    S, D = h.shape
    nz, nx = dz.shape[1] // tn, dxbc.shape[1] // tn
    N = dz.shape[1] + dxbc.shape[1] + SSM_HEADS

    def body(h_ref, z_ref, x_ref, t_ref, o_ref):
        j = pl.program_id(0)
        hb = h_ref[...].astype(BF16)

        @pl.when(j < nz)
        def _():
            o_ref[...] = _dot(z_ref[...].astype(BF16), hb, 0, 0).astype(BF16)

        @pl.when((j >= nz) & (j < nz + nx))
        def _():
            o_ref[...] = _dot(x_ref[...].astype(BF16), hb, 0, 0).astype(BF16)

        @pl.when(j == nz + nx)
        def _():
            o_ref[:128, :] = _dot(t_ref[...].astype(BF16), hb, 0, 0).astype(BF16)
            o_ref[128:, :] = jnp.zeros((tn - 128, D), BF16)

    return pl.pallas_call(
        body, name=name, grid=(nz + nx + 1,),
        in_specs=[pl.BlockSpec((S, D), lambda j: (0, 0), pipeline_mode=pl.Buffered(1)),
                  pl.BlockSpec((S, tn), lambda j: (0, jnp.minimum(j, nz - 1))),
                  pl.BlockSpec((S, tn), lambda j: (0, jnp.clip(j - nz, 0, nx - 1))),
                  pl.BlockSpec((S, 128), lambda j: (0, 0))],
        out_specs=pl.BlockSpec((tn, D), lambda j: (j, 0)),
        out_shape=jax.ShapeDtypeStruct((N, D), BF16), compiler_params=_cp(1),
    )(h, dz, dxbc, ddtr)


def _prep_weights(full):
    w = _prep_small(full, {})
    w["w_zx"], w["w_dt"] = _split_in_proj(full["a_in_proj"][0])
    w["a_out_proj"] = full["a_out_proj"][0].astype(BF16)
    w["f_w_in"] = [full["f_w_in"][l].reshape(1024, N_CHIPS, -1).transpose(1, 0, 2).astype(BF16) for l in range(2)]
    w["f_w_down"] = [full["f_w_down"][l].astype(BF16) for l in range(2)]
    w["w_kv"] = full["w_kv"].astype(BF16)
    w["w_q"] = full["w_q"][0].astype(BF16)
    w["w_o"] = full["w_o"][0].astype(BF16)
    return w


def _small_grads(gr):
    g = {}
    g["a_norm"] = gr["a_norm"]
    g["a_conv_w"] = gr["a_conv_w"][None]
    g["a_conv_b"] = gr["a_conv_b"]
    g["a_dt_bias"], g["a_A_log"], g["a_D"] = gr["a_dt_bias"], gr["a_A_log"], gr["a_D"]
    g["a_gnorm"] = gr["a_gnorm"]
    g["kv_norm"] = gr["kv_norm"].reshape(-1)
    g["b_kv"] = gr["b_kv"].reshape(-1)
    g["k_norm"] = gr["k_norm"].reshape(-1)
    g["b_norm"] = gr["b_norm"]
    g["b_q"] = gr["b_q"]
    g["q_norm"] = gr["q_norm"]
    g["sinks"] = gr["sinks"]
    g["b_o"] = gr["b_o"]
    f = [gr["ffn0"], gr["ffn1"]]
    g["f_norm"] = jnp.concatenate([f[0]["f_norm"], f[1]["f_norm"]], axis=0)
    g["f_conv_w"] = jnp.stack([f[l]["f_conv_w"] for l in range(2)])
    g["f_conv_b"] = jnp.concatenate([f[l]["f_conv_b"] for l in range(2)], axis=0)
    return g


def _full_grads(gr):
    g = _small_grads(gr)
    f32 = lambda t: t.astype(F32)
    g["a_in_proj"] = f32(gr["in_proj"])[None]
    g["a_out_proj"] = f32(gr["a_out_proj"])[None]
    g["w_kv"] = f32(gr["w_kv"])
    g["w_q"] = f32(gr["w_q"])[None]
    g["w_o"] = f32(gr["w_o"])[None]
    f = [gr["ffn0"], gr["ffn1"]]
    g["f_w_in"] = jnp.stack([f32(f[l]["f_w_in"]).transpose(1, 0, 2).reshape(1024, -1) for l in range(2)])
    g["f_w_down"] = jnp.stack([f32(f[l]["f_w_down"]) for l in range(2)])
    return g


MESH = pl.DeviceIdType.MESH
WEIGHTS = ("a_norm", "a_in_proj", "a_conv_w", "a_conv_b", "a_dt_bias", "a_A_log", "a_D", "a_gnorm", "a_out_proj",
           "kv_norm", "w_kv", "b_kv", "k_norm", "b_norm", "w_q", "b_q", "q_norm", "sinks", "w_o", "b_o", "f_norm",
           "f_w_in", "f_conv_w", "f_conv_b", "f_w_down")
MATS = (("in_proj", "a_in_proj", 0), ("out_proj", "a_out_proj", 0), ("w_kv", "w_kv", None), ("w_q", "w_q", 0),
        ("w_o", "w_o", 0), ("f_in0", "f_w_in", 0), ("f_in1", "f_w_in", 1), ("f_down0", "f_w_down", 0),
        ("f_down1", "f_w_down", 1))
SMALL_CUT = (("a_norm", 1), ("a_conv_w", 2), ("a_conv_b", 1), ("a_gnorm", 1), ("f_conv_w", 2))
SMALL_REP = ("a_dt_bias", "a_A_log", "a_D", "kv_norm", "b_kv", "k_norm", "b_norm", "b_q", "q_norm", "sinks", "b_o",
             "f_norm", "f_conv_b")


def _coords():
    return lax.axis_index("x"), lax.axis_index("y"), lax.axis_index("c")


def _other_chips(x, y):
    return [(1 - x, y), (x, 1 - y), (1 - x, 1 - y)]


def _pack(arrs, rows_align, lanes, dtype):
    flat = jnp.concatenate([a.reshape(-1).astype(dtype) for a in arrs])
    per = rows_align * lanes
    total = -(-flat.shape[0] // per) * per
    return jnp.pad(flat, (0, total - flat.shape[0])).reshape(total // lanes, lanes)


def _unpack(flat, shapes):
    out, off = [], 0
    for s in shapes:
        n = math.prod(s)
        out.append(flat[off:off + n].reshape(s))
        off += n
    return out


def _remote(src, dst, send, recv, k, dev):
    return pltpu.make_async_remote_copy(src_ref=src, dst_ref=dst, send_sem=send.at[k], recv_sem=recv.at[k],
                                        device_id=dev, device_id_type=MESH)


_ANY = pl.BlockSpec(memory_space=pl.ANY)


def _halves(t):
    r, c = t.shape
    return t.reshape(2, r // 2, c)


RING_SEMS = 9


def _ring_first_copies(refs, send, recv):
    n = (len(refs) - 2) // 2
    sh, sp_ref, outs, sout = refs[:n], refs[n], refs[n + 1:2 * n + 1], refs[2 * n + 1]
    per = RING_SEMS
    x, y, c = _coords()
    me = 2 * x + y
    cx_, cy_, cd_ = _other_chips(x, y)
    sib = (x, y, 1 - c)
    out = [(_remote(sp_ref, sout.at[me], send, recv, per * n + j, (*p, c)),
            _remote(sp_ref, sout.at[2 * p[0] + p[1]], send, recv, per * n + j, (*p, c))) for j, p in enumerate((cx_, cy_, cd_))]
    for t in range(n):
        for k, p in enumerate((cx_, cy_)):
            out.append((_remote(sh[t].at[c], outs[t].at[me, c], send, recv, per * t + k, (*p, c)),
                        _remote(sh[t].at[c], outs[t].at[2 * p[0] + p[1], c], send, recv, per * t + k, (*p, c))))
        own = _remote(sh[t], outs[t].at[me], send, recv, per * t + 8, sib)
        out.append((own, own))
    return out


def _ring_rest(sh, sp_ref, outs, sout, send, recv, loc):
    n, per = len(sh), RING_SEMS
    x, y, c = _coords()
    me = 2 * x + y
    cx_, cy_, cd_ = _other_chips(x, y)
    ix, iy, idg = (2 * p[0] + p[1] for p in (cx_, cy_, cd_))
    to_x, to_y, sib = (*cx_, c), (*cy_, c), (x, y, 1 - c)
    own_small = pltpu.make_async_copy(sp_ref, sout.at[me], loc.at[0])
    own_small.start()
    sends = []

    def go(src, dst, k, dev):
        cp = _remote(src, dst, send, recv, k, dev)
        cp.start()
        sends.append(cp)

    def piece(t, owner, first):
        q = sh[t].shape[1] // 2
        return outs[t].at[owner, c, pl.ds(0 if first else q, q)]

    for t in range(n):
        go(piece(t, ix, False), piece(t, ix, False), per * t + 3, to_y)
        go(outs[t].at[ix, c], outs[t].at[ix, c], per * t + 4, sib)
    for t in range(n):
        go(piece(t, iy, True), piece(t, iy, True), per * t + 2, to_x)
        go(outs[t].at[iy, c], outs[t].at[iy, c], per * t + 5, sib)
    for t in range(n):
        _remote(piece(t, idg, True), piece(t, idg, True), send, recv, per * t + 2, to_x).wait_recv()
        go(piece(t, idg, True), piece(t, idg, True), per * t + 6, sib)
        _remote(piece(t, idg, False), piece(t, idg, False), send, recv, per * t + 3, to_y).wait_recv()
        go(piece(t, idg, False), piece(t, idg, False), per * t + 7, sib)
    for t in range(n):
        q = sh[t].shape[1] // 2
        other = lambda owner, lo=None: outs[t].at[owner, 1 - c] if lo is None else outs[t].at[owner, 1 - c, pl.ds(lo, q)]
        _remote(other(ix), other(ix), send, recv, per * t + 4, sib).wait_recv()
        _remote(other(iy), other(iy), send, recv, per * t + 5, sib).wait_recv()
        _remote(other(idg, 0), other(idg, 0), send, recv, per * t + 6, sib).wait_recv()
        _remote(other(idg, q), other(idg, q), send, recv, per * t + 7, sib).wait_recv()
    for cp in sends:
        cp.wait_send()
    own_small.wait()


def _gather_weights_forward(shards, sp, lands, sland):
    n = len(shards)

    def body(*refs):
        sh, sp_ref = refs[:n], refs[n]
        outs, sout = refs[2 * n + 2:3 * n + 2], refs[3 * n + 2]
        _ring_rest(sh, sp_ref, outs, sout, *refs[3 * n + 3:])

    res = pl.pallas_call(
        body, name="gather_weights_forward", in_specs=[_ANY] * (2 * n + 2), out_specs=[_ANY] * (n + 1),
        out_shape=[jax.ShapeDtypeStruct(a.shape, a.dtype) for a in (*lands, sland)],
        input_output_aliases={n + 1 + t: t for t in range(n + 1)},
        scratch_shapes=[pltpu.SemaphoreType.DMA((RING_SEMS * n,)), pltpu.SemaphoreType.DMA((RING_SEMS * n,)),
                        pltpu.SemaphoreType.DMA((1,))],
    )(*shards, sp, *lands, sland)
    return res[:n], res[n]


_HBM = pl.BlockSpec(memory_space=pltpu.HBM)
_SEMS = pl.BlockSpec(memory_space=pltpu.SEMAPHORE)
_DATAFLOW = pltpu.SideEffectType.DATAFLOW_SIDE_EFFECTING


def _in_hbm(a):
    return pltpu.with_memory_space_constraint(a, pltpu.HBM)


def _start_copies(copies, arrays, n_sem, after, *, name):
    n, na = len(arrays), len(after)

    def body(*refs):
        for mine, _ in copies(refs[:n], refs[n + na], refs[n + na + 1]):
            mine.start()
        refs[-1][...] = jnp.zeros_like(refs[-1])

    res = pl.pallas_call(
        body, name=name, in_specs=[_HBM] * n + [_ANY] * na,
        out_specs=[_SEMS, _SEMS] + [_HBM] * n + [pl.BlockSpec(memory_space=pltpu.VMEM)],
        out_shape=[pltpu.SemaphoreType.DMA((n_sem,)), pltpu.SemaphoreType.DMA((n_sem,))]
        + [pltpu.HBM(a.shape, a.dtype) for a in arrays] + [jax.ShapeDtypeStruct((8, 128), F32)],
        input_output_aliases={t: 2 + t for t in range(n)},
        compiler_params=pltpu.CompilerParams(has_side_effects=_DATAFLOW),
    )(*[_in_hbm(a) for a in arrays], *after)
    return res[0], res[1], list(res[2:2 + n]), res[-1]


def _wait_copies(copies, send, recv, arrays, after, *, name):
    n = len(arrays)

    def body(*refs):
        for mine, theirs in copies(refs[:n], refs[n], refs[n + 1]):
            mine.wait_send()
            theirs.wait_recv()

    return list(pl.pallas_call(
        body, name=name, in_specs=[_HBM] * n + [_SEMS, _SEMS] + [_ANY] * len(after), out_specs=[_HBM] * n,
        out_shape=[pltpu.HBM(a.shape, a.dtype) for a in arrays], input_output_aliases={t: t for t in range(n)},
        compiler_params=pltpu.CompilerParams(has_side_effects=_DATAFLOW),
    )(*arrays, send, recv, *after))


def _sibling_copies(refs, send, recv):
    n = len(refs) // 2
    x, y, c = _coords()
    cps = [_remote(refs[t].at[:, 1 - c], refs[n + t], send, recv, t, (x, y, 1 - c)) for t in range(n)]
    return [(cp, cp) for cp in cps]


def _join_copies(refs, send, recv):
    x, y, c = _coords()
    sib = (x, y, 1 - c)
    return [(_remote(o.at[c], o.at[c], send, recv, t, sib), _remote(o.at[1 - c], o.at[1 - c], send, recv, t, sib))
            for t, o in enumerate(refs)]


def _gather_copies(sh, land, send, recv):
    x, y, c = _coords()
    me = 2 * x + y
    out = []
    for t in range(len(sh)):
        for j, (cx, cy) in enumerate(_other_chips(x, y)):
            dev = (cx, cy, c)
            out.append((_remote(sh[t].at[c], land[t].at[me, c], send, recv, 4 * t + j, dev),
                        _remote(sh[t].at[c], land[t].at[2 * cx + cy, c], send, recv, 4 * t + j, dev)))
        sib = (x, y, 1 - c)
        out.append((_remote(sh[t], land[t].at[me], send, recv, 4 * t + 3, sib),
                    _remote(sh[t], land[t].at[me], send, recv, 4 * t + 3, sib)))
    return out


def _gather_start(shards, after, *, name):
    n = len(shards)

    def body(*refs):
        sh, land = refs[:n], refs[n:2 * n]
        send, recv = refs[2 * n + 1], refs[2 * n + 2]
        token = refs[-1]
        for mine, _ in _gather_copies(sh, land, send, recv):
            mine.start()
        token[...] = jnp.zeros_like(token)

    lands = [_in_hbm(lax.empty((N_CHIPS,) + s.shape, s.dtype)) for s in shards]
    res = pl.pallas_call(
        body, name=name, in_specs=[_HBM] * (2 * n) + [_ANY],
        out_specs=[_SEMS, _SEMS] + [_HBM] * (2 * n) + [pl.BlockSpec(memory_space=pltpu.VMEM)],
        out_shape=[pltpu.SemaphoreType.DMA((4 * n,)), pltpu.SemaphoreType.DMA((4 * n,))]
        + [pltpu.HBM(s.shape, s.dtype) for s in shards] + [pltpu.HBM(l.shape, l.dtype) for l in lands]
        + [jax.ShapeDtypeStruct((8, 128), F32)],
        input_output_aliases={t: 2 + t for t in range(2 * n)},
        compiler_params=pltpu.CompilerParams(has_side_effects=_DATAFLOW),
    )(*[_in_hbm(s) for s in shards], *lands, after)
    return res[0], res[1], res[2:2 + n], res[2 + n:2 + 2 * n], res[-1]


def _gather_wait(send, recv, shards, lands, after, *, name):
    n = len(shards)

    def body(*refs):
        sh, land = refs[:n], refs[n:2 * n]
        send_r, recv_r = refs[2 * n], refs[2 * n + 1]
        for mine, theirs in _gather_copies(sh, land, send_r, recv_r):
            mine.wait_send()
            theirs.wait_recv()

    res = pl.pallas_call(
        body, name=name, in_specs=[_HBM] * (2 * n) + [_SEMS, _SEMS, _ANY], out_specs=[_HBM] * (2 * n),
        out_shape=[pltpu.HBM(s.shape, s.dtype) for s in shards] + [pltpu.HBM(l.shape, l.dtype) for l in lands],
        input_output_aliases={t: t for t in range(2 * n)},
        compiler_params=pltpu.CompilerParams(has_side_effects=_DATAFLOW),
    )(*shards, *lands, send, recv, after)
    return res[n:]


def _forward_copies(refs, send, recv):
    x, y, c = _coords()
    sib = (x, y, 1 - c)
    srcs = [2 * cx + cy for cx, cy in _other_chips(x, y)]
    return [(_remote(o.at[s, c], o.at[s, c], send, recv, 3 * t + j, sib),
             _remote(o.at[s, 1 - c], o.at[s, 1 - c], send, recv, 3 * t + j, sib))
            for t, o in enumerate(refs) for j, s in enumerate(srcs)]


def _small_copies(v, land, send, recv):
    x, y, c = _coords()
    me = 4 * x + 2 * y + c
    out = []
    for k in range(1, 8):
        px = 1 - x if k & 4 else x
        py = 1 - y if k & 2 else y
        pc = 1 - c if k & 1 else c
        out.append((_remote(v, land.at[me], send, recv, k - 1, (px, py, pc)),
                    _remote(v, land.at[4 * px + 2 * py + pc], send, recv, k - 1, (px, py, pc))))
    return out


def _small_start(v, after, *, name):
    def body(v_ref, land_ref, after_ref, send, recv, v_thru, land_thru, token):
        for mine, _ in _small_copies(v_ref, land_ref, send, recv):
            mine.start()
        token[...] = jnp.zeros_like(token)

    land = _in_hbm(lax.empty((8,) + v.shape, v.dtype))
    return pl.pallas_call(
        body, name=name, in_specs=[_HBM, _HBM, _ANY],
        out_specs=[_SEMS, _SEMS, _HBM, _HBM, pl.BlockSpec(memory_space=pltpu.VMEM)],
        out_shape=[pltpu.SemaphoreType.DMA((7,)), pltpu.SemaphoreType.DMA((7,)), pltpu.HBM(v.shape, v.dtype),
                   pltpu.HBM(land.shape, land.dtype), jax.ShapeDtypeStruct((8, 128), F32)],
        input_output_aliases={0: 2, 1: 3}, compiler_params=pltpu.CompilerParams(has_side_effects=_DATAFLOW),
    )(_in_hbm(v), land, after)


def _small_wait(send, recv, v, land, after, *, name):
    def body(v_ref, land_ref, send_r, recv_r, *rest):
        for mine, theirs in _small_copies(v_ref, land_ref, send_r, recv_r):
            mine.wait_send()
            theirs.wait_recv()

    return pl.pallas_call(
        body, name=name, in_specs=[_HBM, _HBM, _SEMS, _SEMS] + [_ANY] * len(after), out_specs=[_HBM, _HBM],
        out_shape=[pltpu.HBM(v.shape, v.dtype), pltpu.HBM(land.shape, land.dtype)],
        input_output_aliases={0: 0, 1: 1}, compiler_params=pltpu.CompilerParams(has_side_effects=_DATAFLOW),
    )(v, land, send, recv, *after)


def _small_sum(v, land, me_idx, *, name="small_sum"):
    def body(me_ref, v_ref, land_ref, o_ref):
        acc = None
        for s in range(8):
            term = jnp.where(me_ref[0] == s, v_ref[...], land_ref[s])
            acc = term if acc is None else acc + term
        o_ref[...] = acc

    whole = lambda shape: pl.BlockSpec(shape, lambda i, me_ref: (0,) * len(shape))
    return pl.pallas_call(
        body, name=name,
        grid_spec=pltpu.PrefetchScalarGridSpec(num_scalar_prefetch=1, grid=(1,), in_specs=[whole(v.shape), whole(land.shape)],
                                               out_specs=whole(v.shape)),
        out_shape=jax.ShapeDtypeStruct(v.shape, F32), compiler_params=_cp(1),
    )(me_idx, v, land)


RS_ROW_SPLIT = 2


def _rs_add_pair(gs, as_, c_idx, *, name):
    n = len(gs)

    def body(c_ref, *refs):
        for t in range(n):
            refs[2 * n + t][...] = (refs[t][...].astype(F32) + refs[n + t][...].astype(F32)).astype(BF16)

    def gspec(g):
        _, _, rh, cols = g.shape
        return pl.BlockSpec((None, None, rh // RS_ROW_SPLIT, cols), lambda j, i, c_ref: (j, c_ref[0], i, 0))

    def pspec(g):
        _, _, rh, cols = g.shape
        return pl.BlockSpec((None, rh // RS_ROW_SPLIT, cols), lambda j, i, c_ref: (j, i, 0))

    return pl.pallas_call(
        body, name=name,
        grid_spec=pltpu.PrefetchScalarGridSpec(
            num_scalar_prefetch=1, grid=(N_CHIPS, RS_ROW_SPLIT),
            in_specs=[gspec(g) for g in gs] + [pspec(g) for g in gs], out_specs=[pspec(g) for g in gs]),
        out_shape=[jax.ShapeDtypeStruct((N_CHIPS,) + g.shape[2:], BF16) for g in gs], compiler_params=_cp(2),
    )(c_idx, *gs, *as_)


def _chips_copies(p, r, send, recv):
    x, y, c = _coords()
    return [_remote(p[t].at[2 * cx + cy], r[t].at[k], send, recv, 3 * t + k, (cx, cy, c))
            for k, (cx, cy) in enumerate(_other_chips(x, y)) for t in range(len(p))]


def _rs_chips_start(ps, after, *, name):
    n, na = len(ps), len(after)

    def body(*refs):
        p, r = refs[:n], refs[n:2 * n]
        send, recv = refs[2 * n + na], refs[2 * n + na + 1]
        token = refs[-1]
        for cp in _chips_copies(p, r, send, recv):
            cp.start()
        token[...] = jnp.zeros_like(token)

    lands = [_in_hbm(lax.empty((3,) + p.shape[1:], p.dtype)) for p in ps]
    res = pl.pallas_call(
        body, name=name, in_specs=[_HBM] * (2 * n) + [_ANY] * na,
        out_specs=[_SEMS, _SEMS] + [_HBM] * (2 * n) + [pl.BlockSpec(memory_space=pltpu.VMEM)],
        out_shape=[pltpu.SemaphoreType.DMA((3 * n,)), pltpu.SemaphoreType.DMA((3 * n,))]
        + [pltpu.HBM(p.shape, p.dtype) for p in ps] + [pltpu.HBM(l.shape, l.dtype) for l in lands]
        + [jax.ShapeDtypeStruct((8, 128), F32)],
        input_output_aliases={t: 2 + t for t in range(2 * n)},
        compiler_params=pltpu.CompilerParams(has_side_effects=_DATAFLOW),
    )(*[_in_hbm(p) for p in ps], *lands, *after)
    return res[0], res[1], res[2:2 + n], res[2 + n:2 + 2 * n], res[-1]


def _rs_chips_wait(send, recv, ps, lands, after, *, name):
    n = len(ps)

    def body(*refs):
        p, r = refs[:n], refs[n:2 * n]
        for cp in _chips_copies(p, r, refs[2 * n], refs[2 * n + 1]):
            cp.wait_send()
            cp.wait_recv()

    res = pl.pallas_call(
        body, name=name, in_specs=[_HBM] * (2 * n) + [_SEMS, _SEMS] + [_ANY] * len(after), out_specs=[_HBM] * (2 * n),
        out_shape=[pltpu.HBM(p.shape, p.dtype) for p in ps] + [pltpu.HBM(l.shape, l.dtype) for l in lands],
        input_output_aliases={t: t for t in range(2 * n)},
        compiler_params=pltpu.CompilerParams(has_side_effects=_DATAFLOW),
    )(*ps, *lands, send, recv, *after)
    return res[:n], res[n:]


def _rs_add_chips(ps, rs, idx, *, name):
    n = len(ps)

    def body(idx_ref, *refs):
        for t in range(n):
            p_ref, r0, r1, r2 = refs[4 * t:4 * t + 4]
            refs[4 * n + t][...] = ((p_ref[...].astype(F32) + r0[...].astype(F32)) + r1[...].astype(F32)) + r2[...].astype(F32)

    in_specs, args = [], []
    for p, r in zip(ps, rs):
        _, rh, cols = p.shape
        blk = (None, rh // RS_ROW_SPLIT, cols)
        in_specs.append(pl.BlockSpec(blk, lambda i, idx_ref: (idx_ref[0], i, 0)))
        in_specs += [pl.BlockSpec(blk, lambda i, idx_ref, k=k: (k, i, 0)) for k in range(3)]
        args += [p, r, r, r]
    out_specs = [pl.BlockSpec((None, p.shape[1] // RS_ROW_SPLIT, p.shape[2]), lambda i, idx_ref: (idx_ref[1], i, 0))
                 for p in ps]
    return pl.pallas_call(
        body, name=name,
        grid_spec=pltpu.PrefetchScalarGridSpec(num_scalar_prefetch=1, grid=(RS_ROW_SPLIT,), in_specs=in_specs,
                                               out_specs=out_specs),
        out_shape=[jax.ShapeDtypeStruct((2,) + p.shape[1:], F32) for p in ps], compiler_params=_cp(1),
    )(idx, *args)


def _adamw(w, gs, m, v, *, name, dep=None):
    L, Rr, C = w.shape
    tr, tc = _pick(Rr, (256, 128, 64)), C
    if tr == Rr and Rr * C > 512 * 1024:
        tc = 256
    bc1 = 1.0 - ADAM_B1 ** ADAM_STEP
    bc2 = 1.0 - ADAM_B2 ** ADAM_STEP
    nd = 0 if dep is None else 1

    def body(*refs):
        w_ref, m_ref, v_ref = refs[0], refs[1], refs[2]
        g_refs = refs[3:3 + L]
        d_ref, mo_ref, vo_ref, go_ref = refs[3 + L + nd:]
        layer = pl.program_id(0)
        gv = g_refs[0][...]
        for q in range(1, L):
            gv = jnp.where(layer == q, g_refs[q][...], gv)
        mn = ADAM_B1 * m_ref[...] + (1.0 - ADAM_B1) * gv
        vn = ADAM_B2 * v_ref[...] + (1.0 - ADAM_B2) * (gv * gv)
        go_ref[...] = gv
        mo_ref[...] = mn
        vo_ref[...] = vn
        d_ref[...] = -ADAM_LR * ((mn / bc1) / (jnp.sqrt(vn / bc2) + ADAM_EPS) + ADAM_WD * w_ref[...])

    blk = pl.BlockSpec((None, tr, tc), lambda l, i, j: (l, i, j))
    gblks = [pl.BlockSpec((tr, tc), lambda l, i, j, q=q: (jnp.where(l == q, i, 0), jnp.where(l == q, j, 0))) for q in range(L)]
    return pl.pallas_call(
        body, name=name, grid=(L, Rr // tr, C // tc), in_specs=[blk] * 3 + gblks + [_ANY] * nd, out_specs=[blk] * 4,
        out_shape=[jax.ShapeDtypeStruct((L, Rr, C), F32)] * 4, compiler_params=_cp(3),
    )(w, m, v, *gs, *([] if dep is None else [dep]))


def _adamw_leaves(ws, gs, ms, vs, *, name):
    n = len(ws)
    bc1 = 1.0 - ADAM_B1 ** ADAM_STEP
    bc2 = 1.0 - ADAM_B2 ** ADAM_STEP

    def body(*refs):
        w_refs, g_refs, m_refs, v_refs, d_refs, mo_refs, vo_refs = (refs[q * n:(q + 1) * n] for q in range(7))
        for k in range(n):
            gv = g_refs[k][...]
            mn = ADAM_B1 * m_refs[k][...] + (1.0 - ADAM_B1) * gv
            vn = ADAM_B2 * v_refs[k][...] + (1.0 - ADAM_B2) * (gv * gv)
            mo_refs[k][...] = mn
            vo_refs[k][...] = vn
            d_refs[k][...] = -ADAM_LR * ((mn / bc1) / (jnp.sqrt(vn / bc2) + ADAM_EPS) + ADAM_WD * w_refs[k][...])

    in_vmem = pl.BlockSpec(memory_space=pltpu.VMEM)
    outs = pl.pallas_call(
        body, name=name, in_specs=[in_vmem] * (4 * n), out_specs=[in_vmem] * (3 * n),
        out_shape=[jax.ShapeDtypeStruct(t.shape, F32) for t in ws] * 3,
    )(*ws, *gs, *ms, *vs)
    return outs[:n], outs[n:2 * n], outs[2 * n:]


def kernel(x, positions, a_norm, a_in_proj, a_conv_w, a_conv_b, a_dt_bias, a_A_log, a_D, a_gnorm, a_out_proj,
           kv_norm, w_kv, b_kv, k_norm, b_norm, w_q, b_q, q_norm, sinks, w_o, b_o, f_norm, f_w_in, f_conv_w,
           f_conv_b, f_w_down, loss_target, m_a_norm, m_a_in_proj, m_a_conv_w, m_a_conv_b, m_a_dt_bias, m_a_A_log,
           m_a_D, m_a_gnorm, m_a_out_proj, m_kv_norm, m_w_kv, m_b_kv, m_k_norm, m_b_norm, m_w_q, m_b_q, m_q_norm,
           m_sinks, m_w_o, m_b_o, m_f_norm, m_f_w_in, m_f_conv_w, m_f_conv_b, m_f_w_down, v_a_norm, v_a_in_proj,
           v_a_conv_w, v_a_conv_b, v_a_dt_bias, v_a_A_log, v_a_D, v_a_gnorm, v_a_out_proj, v_kv_norm, v_w_kv,
           v_b_kv, v_k_norm, v_b_norm, v_w_q, v_b_q, v_q_norm, v_sinks, v_w_o, v_b_o, v_f_norm, v_f_w_in,
           v_f_conv_w, v_f_conv_b, v_f_w_down):
    wl = dict(zip(WEIGHTS, (a_norm, a_in_proj, a_conv_w, a_conv_b, a_dt_bias, a_A_log, a_D, a_gnorm, a_out_proj,
                            kv_norm, w_kv, b_kv, k_norm, b_norm, w_q, b_q, q_norm, sinks, w_o, b_o, f_norm, f_w_in,
                            f_conv_w, f_conv_b, f_w_down)))
    ml = dict(zip(WEIGHTS, (m_a_norm, m_a_in_proj, m_a_conv_w, m_a_conv_b, m_a_dt_bias, m_a_A_log, m_a_D, m_a_gnorm,
                            m_a_out_proj, m_kv_norm, m_w_kv, m_b_kv, m_k_norm, m_b_norm, m_w_q, m_b_q, m_q_norm,
                            m_sinks, m_w_o, m_b_o, m_f_norm, m_f_w_in, m_f_conv_w, m_f_conv_b, m_f_w_down)))
    vl = dict(zip(WEIGHTS, (v_a_norm, v_a_in_proj, v_a_conv_w, v_a_conv_b, v_a_dt_bias, v_a_A_log, v_a_D, v_a_gnorm,
                            v_a_out_proj, v_kv_norm, v_w_kv, v_b_kv, v_k_norm, v_b_norm, v_w_q, v_b_q, v_q_norm,
                            v_sinks, v_w_o, v_b_o, v_f_norm, v_f_w_in, v_f_conv_w, v_f_conv_b, v_f_w_down)))
    xi, yi, ci = _coords()
    me = 2 * xi + yi
    S = x.shape[1]

    def shard(name, src):
        _, wn, layer = next(m for m in MATS if m[0] == name)
        return _halves((src[wn] if layer is None else src[wn][layer]).astype(BF16))

    rows = lambda t: t.reshape(-1, t.shape[-1])
    c_idx = jnp.reshape(ci, (1,)).astype(jnp.int32)
    me_c = jnp.stack([me, ci]).astype(jnp.int32)
    early = ("in_proj",)
    late = (("out_proj", "f_in0", "f_down0"), ("w_kv", "w_q", "w_o", "f_in1", "f_down1"))

    sp = _pack([wl[n] for n, _ in SMALL_CUT], 8, 128, F32)
    e_sh = [shard(k, wl) for k in early]
    ring = e_sh + [sp] + [lax.empty((N_CHIPS,) + t.shape, t.dtype) for t in e_sh + [sp]]
    g_send, g_recv, ring, g_tok = _start_copies(_ring_first_copies, ring, RING_SEMS * len(e_sh) + 3, [],
                                                name="gather_weights_start")
    late_w = sorted({wn for name, wn, _ in MATS if name not in early})
    _, tied, posf = lax.optimization_barrier((g_tok, {wn: wl[wn] for wn in late_w}, positions.reshape(S, 1).astype(F32)))
    shards = {k: shard(k, tied) for part in late for k in part}
    rope = _rope_cs(posf)
    ring = _wait_copies(_ring_first_copies, g_send, g_recv, ring, [*rope, *shards.values()], name="gather_weights_arrived")
    ne = len(e_sh)
    gathered, gs = _gather_weights_forward(ring[:ne], ring[ne], ring[ne + 1:2 * ne + 1], ring[2 * ne + 1])
    gt = {k: t.reshape(N_CHIPS, -1, t.shape[-1]) for k, t in zip(early, gathered)}
    started = {0: _gather_start([shards[k] for k in late[0]], gs, name="gather_late_start0")}
    full = {n: wl[n] for n in SMALL_REP}
    gs = gs.reshape(N_CHIPS, -1)
    off = 0
    for n, ax in SMALL_CUT:
        shp = wl[n].shape
        size = math.prod(shp)
        piece = jnp.moveaxis(gs[:, off:off + size].reshape((N_CHIPS,) + shp), 0, ax)
        full[n] = piece.reshape(shp[:ax] + (N_CHIPS * shp[ax],) + shp[ax + 1:])
        off += size
    w = _prep_small(full, {})
    w["w_zx"], w["w_dt"] = _join_in_proj(gt["in_proj"])
    w["dep"] = started[0][4]
    w["rope"] = rope

    class Comm:
        flight = []
        reduced = {}

        forwarding = {}

        def late_start(self, part, after):
            started[part] = _gather_start([shards[k] for k in late[part]], after, name=f"gather_late_start{part}")
            return started[part][4]

        def late_arrived(self, part, after):
            send, recv, shs, lands, _ = started[part]
            lands = _gather_wait(send, recv, shs, lands, after[0], name=f"gather_late_wait{part}")
            send, recv, lands, token = _start_copies(_forward_copies, list(lands), 3 * len(lands), after,
                                                     name=f"gather_late_forward_start{part}")
            self.forwarding[part] = (send, recv, lands)
            return token

        def late_weights(self, w, after, part):
            send, recv, lands = self.forwarding[part]
            lands = _wait_copies(_forward_copies, send, recv, lands, [after], name=f"gather_late_forward_wait{part}")
            lt = {k: t.reshape(N_CHIPS, -1, t.shape[-1]) for k, t in zip(late[part], lands)}
            w = dict(w)
            if part == 0:
                w["a_out_proj"], w["f_w_in"], w["f_w_down"] = rows(lt["out_proj"]), [lt["f_in0"]], [rows(lt["f_down0"])]
            else:
                w["w_kv"], w["w_q"], w["w_o"] = (rows(lt[k]) for k in ("w_kv", "w_q", "w_o"))
                w["f_w_in"], w["f_w_down"] = w["f_w_in"] + [lt["f_in1"]], w["f_w_down"] + [rows(lt["f_down1"])]
            return w

        def advance(self, after, group=None, tensors=None):
            token = None
            for grp in list(self.flight):
                tag, n = grp["tag"], len(grp["names"])
                dep = list(after) + ([] if token is None else [token])
                if grp["stage"] == "sibling":
                    arrs = _wait_copies(_sibling_copies, grp["send"], grp["recv"], grp["arrays"], dep, name=f"rs_sibling_wait{tag}")
                    pairs = _rs_add_pair(arrs[:n], arrs[n:], c_idx, name=f"rs_add_pair{tag}")
                    send, recv, ps, lands, token = _rs_chips_start(pairs, dep, name=f"rs_chips_start{tag}")
                    grp.update(stage="chips", send=send, recv=recv, ps=ps, lands=lands)
                elif grp["stage"] == "chips":
                    ps, rs = _rs_chips_wait(grp["send"], grp["recv"], grp["ps"], grp["lands"], dep, name=f"rs_chips_wait{tag}")
                    halves = _rs_add_chips(ps, rs, me_c, name=f"rs_add_chips{tag}")
                    send, recv, arrs, token = _start_copies(_join_copies, halves, n, dep, name=f"rs_join_start{tag}")
                    grp.update(stage="join", send=send, recv=recv, arrays=arrs)
                else:
                    joined = _wait_copies(_join_copies, grp["send"], grp["recv"], grp["arrays"], dep, name=f"rs_join_wait{tag}")
                    self.reduced.update({k: rows(t) for k, t in zip(grp["names"], joined)})
                    self.flight.remove(grp)
            if group is not None:
                names = list(tensors)
                glist = [tensors[k].reshape(N_CHIPS, 2, -1, tensors[k].shape[-1]) for k in names]
                lands = [lax.empty((N_CHIPS,) + gq.shape[2:], gq.dtype) for gq in glist]
                dep = [a for a in after if not any(a is t for t in tensors.values())] + ([] if token is None else [token])
                send, recv, arrs, token = _start_copies(_sibling_copies, glist + lands, len(names), dep,
                                                        name=f"rs_sibling_start{group}")
                self.flight.append(dict(tag=group, names=names, stage="sibling", send=send, recv=recv, arrays=arrs))
            return token

    comm = Comm()

    loss_part, dx0, gr, tok = _local_step(x[0], posf, loss_target[0], w, comm)
    g = _small_grads(gr)

    small_names = [n for n, _ in SMALL_CUT] + list(SMALL_REP)
    sv = _pack([g[n] for n in small_names] + [loss_part[0:1, 0:1]], 8, 128, F32)
    s_send, s_recv, sv, s_land, s_token = _small_start(sv, tok, name="small_start")

    grads, delta, new_m, new_v = {}, {}, {}, {}

    def update(wn, dep):
        gl = [comm.reduced[name] for name, n2, _ in MATS if n2 == wn]
        shp = wl[wn].shape
        three = (len(gl),) + gl[0].shape
        flip = shp[-1] % 128 != 0
        view = (lambda t: t.reshape(three).transpose(0, 2, 1)) if flip else (lambda t: t.reshape(three))
        back = (lambda t: t.transpose(0, 2, 1).reshape(shp)) if flip else (lambda t: t.reshape(shp))
        if flip:
            gl = [t.T for t in gl]
        d, mn, vn, go = _adamw(view(wl[wn]), gl, view(ml[wn]), view(vl[wn]), name="adamw_" + wn, dep=dep)
        grads[wn], delta[wn], new_m[wn], new_v[wn] = back(go), back(d), back(mn), back(vn)
        return d

    first = [update(wn, s_token) for wn in ("w_q", "w_o", "w_kv")]
    second = [update(wn, first[-1]) for wn in ("f_w_in", "f_w_down", "a_out_proj")]
    tok = comm.advance(first + second)
    done = first + second + [tok]

    sv, s_land = _small_wait(s_send, s_recv, sv, s_land, done, name="small_wait")
    sred = _small_sum(sv, s_land, jnp.reshape(2 * me + ci, (1,)).astype(jnp.int32)).reshape(-1)
    small_shapes = [g[n].shape for n in small_names] + [(1,)]
    sg = dict(zip(small_names + ["loss"], _unpack(sred, small_shapes)))
    loss = sg["loss"].reshape(())
    g_small = {}
    for n, ax in SMALL_CUT:
        size = wl[n].shape[ax]
        g_small[n] = lax.dynamic_slice_in_dim(sg[n], me * size, size, axis=ax)
    for n in SMALL_REP:
        g_small[n] = sg[n].reshape(wl[n].shape)

    leaves = lambda d: [d[n].reshape(1, -1) if d[n].ndim == 1 else d[n] for n in small_names]
    ds, mns, vns = _adamw_leaves(leaves(wl), leaves(g_small), leaves(ml), leaves(vl), name="adamw_small")
    comm.advance([ds[0]])
    update("a_in_proj", None)
    for n, dd, mm, vv in zip(small_names, ds, mns, vns):
        shp = wl[n].shape
        grads[n], delta[n], new_m[n], new_v[n] = g_small[n], dd.reshape(shp), mm.reshape(shp), vv.reshape(shp)

    return (loss, dx0[None], *[grads[n] for n in WEIGHTS], *[delta[n] for n in WEIGHTS],
            *[new_m[n] for n in WEIGHTS], *[new_v[n] for n in WEIGHTS])
```

```python
import math

import jax
import jax.numpy as jnp
from jax import lax
from jax.experimental import pallas as pl
from jax.experimental.pallas import tpu as pltpu

F32 = jnp.float32
BF16 = jnp.bfloat16

EPS = 1e-5
CHUNK = 256
WINDOW = 128
HEAD = 64
SSM_HEADS = 32
SSM_GROUPS = 8
SSM_STATE = 128
ATT_KV = 4
ATT_G = 4
ROPE_THETA = 10000.0
NEG = -1e30
N_CHIPS = 4
VMEM_LIMIT = 56 * 1024 * 1024

ADAM_LR, ADAM_B1, ADAM_B2, ADAM_EPS, ADAM_WD, ADAM_STEP = 0.001, 0.9, 0.999, 1e-08, 0.01, 10


def _cp(n_axes):
    return pltpu.CompilerParams(dimension_semantics=("arbitrary",) * n_axes, vmem_limit_bytes=VMEM_LIMIT)


def _pick(dim, prefs):
    for p in prefs:
        if dim % p == 0:
            return p
    return dim


def _iota(shape, dim):
    return lax.broadcasted_iota(jnp.int32, shape, dim)


def _dot(a, b, ca=1, cb=0):
    return lax.dot_general(a, b, (((ca,), (cb,)), ((), ())), preferred_element_type=F32)


def _dot3(x, ind):
    h = x.astype(BF16)
    r = x - h.astype(F32)
    m = r.astype(BF16)
    lo = (r - m.astype(F32)).astype(BF16)
    return _dot(h, ind) + _dot(m, ind) + _dot(lo, ind)


def _sigmoid(x):
    return jax.nn.sigmoid(x)


def _mm(a, b, *, name, ta=False, tb=False, bias=None, res=None, out_dtype=F32, b_koff=0, tm=None, tn=None, tk=None,
        dims=None, a_spec=None, b_spec=None, o_spec=None, o_shape=None, dep=None, more=(), target=None,
        rms=None, rms_colsum=False):
    if dims is not None:
        M, N, K = dims
    else:
        if ta:
            K, M = a.shape
        else:
            M, K = a.shape
        N = b.shape[0] if tb else b.shape[1]
    tm = tm or _pick(M, (1024, 1408, 512, 256, 128))
    tn = tn or _pick(N, (512, 1408, 256, 128))
    tk = tk or (K if K <= 2048 else _pick(K, (2048, 1408, 1024, 512)))
    assert M % tm == 0 and N % tn == 0 and K % tk == 0 and b_koff % tk == 0
    nk = K // tk
    kb0 = b_koff // tk
    has_bias, has_res = bias is not None, res is not None

    def body(*refs):
        a_ref, b_ref = refs[0], refs[1]
        pos = 2
        bias_ref = res_ref = acc_ref = None
        if has_bias:
            bias_ref = refs[pos]
            pos += 1
        if has_res:
            res_ref = refs[pos]
            pos += 1
        if dep is not None:
            pos += 1
        extra = refs[pos:pos + 2 * len(more)]
        pos += 2 * len(more)
        tgt_ref = lp_ref = rx_ref = rg_ref = rd_ref = dg_ref = cs_ref = None
        if target is not None:
            tgt_ref = refs[pos]
            pos += 1
        if rms is not None:
            rx_ref, rg_ref, rd_ref = refs[pos:pos + 3]
            pos += 3
        o_ref = refs[pos]
        pos += 1
        if target is not None:
            lp_ref = refs[pos]
            pos += 1
        if rms is not None:
            dg_ref = refs[pos]
            pos += 1
            if rms_colsum:
                cs_ref = refs[pos]
                pos += 1
        if nk > 1:
            acc_ref = refs[pos]
        part = _dot(a_ref[...].astype(BF16), b_ref[...].astype(BF16), 0 if ta else 1, 1 if tb else 0)
        for q in range(len(more)):
            part = part + _dot(extra[2 * q][...].astype(BF16), extra[2 * q + 1][...].astype(BF16),
                               0 if ta else 1, 1 if tb else 0)

        def finish(acc):
            if has_bias:
                acc = acc + bias_ref[...]
            if has_res:
                acc = acc + res_ref[...]
            if target is not None:
                err = acc - tgt_ref[...]
                acc = err * (1.0 / N)
                part_loss = jnp.sum(jnp.sum(err * err, axis=1, keepdims=True), axis=0, keepdims=True) * (0.5 / N)
                first = (pl.program_id(0) == 0) & (pl.program_id(1) == 0)

                @pl.when(first)
                def _():
                    lp_ref[...] = jnp.broadcast_to(part_loss, lp_ref.shape)

                @pl.when(jnp.logical_not(first))
                def _():
                    lp_ref[...] += jnp.broadcast_to(part_loss, lp_ref.shape)

            if rms is not None:
                xv = rx_ref[...]
                r = lax.rsqrt(jnp.mean(xv * xv, axis=-1, keepdims=True) + EPS)
                xh = xv * r
                dxh = acc * rg_ref[...]
                dg_part = jnp.sum(acc * xh, axis=0, keepdims=True)
                acc = rd_ref[...] + r * (dxh - xh * jnp.mean(dxh * xh, axis=-1, keepdims=True))
                cs_part = jnp.sum(acc, axis=0, keepdims=True) if rms_colsum else None
                first_rows = pl.program_id(0) == 0

                @pl.when(first_rows)
                def _():
                    dg_ref[...] = dg_part
                    if rms_colsum:
                        cs_ref[...] = cs_part

                @pl.when(jnp.logical_not(first_rows))
                def _():
                    dg_ref[...] += dg_part
                    if rms_colsum:
                        cs_ref[...] += cs_part

            o_ref[...] = acc.astype(out_dtype)

        if nk == 1:
            finish(part)
        else:
            k = pl.program_id(2)

            @pl.when(k == 0)
            def _():
                acc_ref[...] = part

            @pl.when(k > 0)
            def _():
                acc_ref[...] += part

            @pl.when(k == nk - 1)
            def _():
                finish(acc_ref[...])

    if a_spec is None:
        a_spec = pl.BlockSpec((tk, tm), lambda i, j, k: (k, i)) if ta else pl.BlockSpec((tm, tk), lambda i, j, k: (i, k))
    if b_spec is None:
        b_spec = (pl.BlockSpec((tn, tk), lambda i, j, k: (j, k + kb0)) if tb
                  else pl.BlockSpec((tk, tn), lambda i, j, k: (k + kb0, j)))
    if o_spec is None:
        o_spec = pl.BlockSpec((tm, tn), lambda i, j, k: (i, j))
    in_specs, args = [a_spec, b_spec], [a, b]
    if has_bias:
        in_specs.append(pl.BlockSpec((1, tn), lambda i, j, k: (0, j)))
        args.append(bias)
    if has_res:
        in_specs.append(pl.BlockSpec((tm, tn), lambda i, j, k: (i, j)))
        args.append(res)
    if dep is not None:
        in_specs.append(pl.BlockSpec(memory_space=pl.ANY))
        args.append(dep)
    for piece in more:
        a2, sa, b2, sb = piece if len(piece) == 4 else (a, piece[0], b, piece[1])
        in_specs += [sa, sb]
        args += [a2, b2]
    out_specs, out_shape = [o_spec], [jax.ShapeDtypeStruct(o_shape or (M, N), out_dtype)]
    if target is not None:
        in_specs.append(pl.BlockSpec((tm, tn), lambda i, j, k: (i, j)))
        args.append(target)
        out_specs.append(pl.BlockSpec((8, 128), lambda i, j, k: (0, 0)))
        out_shape.append(jax.ShapeDtypeStruct((8, 128), F32))
    if rms is not None:
        assert tn == N and nk == 1
        row, vec = pl.BlockSpec((tm, N), lambda i, j, k: (i, 0)), pl.BlockSpec((1, N), lambda i, j, k: (0, 0))
        in_specs += [row, vec, row]
        args += list(rms)
        out_specs += [vec] * (2 if rms_colsum else 1)
        out_shape += [jax.ShapeDtypeStruct((1, N), F32)] * (2 if rms_colsum else 1)
    if len(out_specs) == 1:
        out_specs, out_shape = out_specs[0], out_shape[0]
    return pl.pallas_call(
        body, name=name, grid=(M // tm, N // tn, nk), in_specs=in_specs, out_specs=out_specs, out_shape=out_shape,
        scratch_shapes=[pltpu.VMEM((tm, tn), F32)] if nk > 1 else [],
        compiler_params=_cp(3),
    )(*args)


def _norm_mm(x, gain, b, *, name, bias=None, N=None, tn=None, b_spec=None, dep=None):
    M, K = x.shape
    N = N or b.shape[1]
    tm = _pick(M, (1024, 512, 256))
    tn = tn or _pick(N, (512, 1408, 256, 128))
    has_bias = bias is not None

    def body(*refs):
        x_ref, g_ref, b_ref = refs[:3]
        pos = 3 + (1 if has_bias else 0) + (0 if dep is None else 1)
        o_ref, h_ref = refs[pos], refs[pos + 1]

        @pl.when(pl.program_id(1) == 0)
        def _():
            xv = x_ref[...]
            h_ref[...] = (xv * lax.rsqrt(jnp.mean(xv * xv, axis=-1, keepdims=True) + EPS) * g_ref[...]).astype(BF16)

        acc = _dot(h_ref[...], b_ref[...].astype(BF16))
        if has_bias:
            acc = acc + refs[3][...]
        o_ref[...] = acc

    in_specs = [pl.BlockSpec((tm, K), lambda i, j: (i, 0)), pl.BlockSpec((1, K), lambda i, j: (0, 0)),
                b_spec or pl.BlockSpec((K, tn), lambda i, j: (0, j))]
    args = [x, gain, b]
    if has_bias:
        in_specs.append(pl.BlockSpec((1, tn), lambda i, j: (0, j)))
        args.append(bias)
    if dep is not None:
        in_specs.append(pl.BlockSpec(memory_space=pl.ANY))
        args.append(dep)
    return pl.pallas_call(
        body, name=name, grid=(M // tm, N // tn), in_specs=in_specs,
        out_specs=[pl.BlockSpec((tm, tn), lambda i, j: (i, j)), pl.BlockSpec((tm, K), lambda i, j: (i, 0))],
        out_shape=[jax.ShapeDtypeStruct((M, N), F32), jax.ShapeDtypeStruct((M, K), BF16)], compiler_params=_cp(2),
    )(*args)


def _rms_bwd(x, gains, dhs, dres, *, name, tr=256, want_colsum=False):
    S, D = x.shape
    n = len(gains)
    steps = S // tr

    def body(*refs):
        x_ref = refs[0]
        g_refs = refs[1:1 + n]
        dh_refs = refs[1 + n:1 + 2 * n]
        dres_ref = refs[1 + 2 * n]
        dx_ref = refs[2 + 2 * n]
        dg_refs = refs[3 + 2 * n:3 + 3 * n]
        cs_ref = refs[3 + 3 * n] if want_colsum else None
        i = pl.program_id(0)
        xv = x_ref[...]
        r = lax.rsqrt(jnp.mean(xv * xv, axis=-1, keepdims=True) + EPS)
        xh = xv * r
        dx = dres_ref[...]
        for q in range(n):
            dh = dh_refs[q][...]
            dxh = dh * g_refs[q][...]
            dx = dx + r * (dxh - xh * jnp.mean(dxh * xh, axis=-1, keepdims=True))
            part = jnp.sum(dh * xh, axis=0, keepdims=True)

            @pl.when(i == 0)
            def _():
                dg_refs[q][...] = part

            @pl.when(i > 0)
            def _():
                dg_refs[q][...] += part

        dx_ref[...] = dx
        if want_colsum:
            cpart = jnp.sum(dx, axis=0, keepdims=True)

            @pl.when(i == 0)
            def _():
                cs_ref[...] = cpart

            @pl.when(i > 0)
            def _():
                cs_ref[...] += cpart

    row = pl.BlockSpec((tr, D), lambda i: (i, 0))
    vec = pl.BlockSpec((1, D), lambda i: (0, 0))
    n_vec_out = n + (1 if want_colsum else 0)
    outs = pl.pallas_call(
        body, name=name, grid=(steps,), in_specs=[row] + [vec] * n + [row] * n + [row],
        out_specs=[row] + [vec] * n_vec_out,
        out_shape=[jax.ShapeDtypeStruct((S, D), F32)] + [jax.ShapeDtypeStruct((1, D), F32)] * n_vec_out,
        compiler_params=_cp(1),
    )(x, *gains, *dhs, dres)
    return outs


def _colsum(x, *, name, tr=256):
    S, D = x.shape

    def body(x_ref, o_ref):
        i = pl.program_id(0)
        part = jnp.sum(x_ref[...].astype(F32), axis=0, keepdims=True)

        @pl.when(i == 0)
        def _():
            o_ref[...] = part

        @pl.when(i > 0)
        def _():
            o_ref[...] += part

    return pl.pallas_call(
        body, name=name, grid=(S // tr,), in_specs=[pl.BlockSpec((tr, D), lambda i: (i, 0))],
        out_specs=pl.BlockSpec((1, D), lambda i: (0, 0)), out_shape=jax.ShapeDtypeStruct((1, D), F32),
        compiler_params=_cp(1),
    )(x)


STRIP = 64
HALO = 8


def _strips(S, tc):
    return [(r0, slice(l0, l0 + 128)) for l0 in range(0, tc, 128) for r0 in range(S - STRIP, -1, -STRIP)]


def _with_halo(ref, r0, ls):
    if r0 == 0:
        return jnp.concatenate([jnp.zeros((HALO, 128), F32), ref[0:STRIP, ls]], axis=0)
    return ref[r0 - HALO:r0 + STRIP, ls]


def _conv_strip(xw, w_ref, b_ref, ls, width):
    acc = b_ref[:, ls] + w_ref[pl.ds(width - 1, 1), ls] * xw[HALO:]
    shifted = []
    for s in range(1, width):
        xs = pltpu.roll(xw, s, axis=0)[HALO:]
        shifted.append(xs)
        acc = acc + w_ref[pl.ds(width - 1 - s, 1), ls] * xs
    return acc, shifted


def _conv_strip_back(dacc, after, xc, shifted, w_ref, ls, width):
    ext = jnp.concatenate([dacc, after], axis=0)
    dx = w_ref[pl.ds(width - 1, 1), ls] * dacc
    dws = [None] * width
    dws[width - 1] = jnp.sum(dacc * xc, axis=0, keepdims=True)
    for s in range(1, width):
        dx = dx + w_ref[pl.ds(width - 1 - s, 1), ls] * pltpu.roll(ext, STRIP + HALO - s, axis=0)[:STRIP]
        dws[width - 1 - s] = jnp.sum(dacc * shifted[s - 1], axis=0, keepdims=True)
    return dx, dws, jnp.sum(dacc, axis=0, keepdims=True)


def _conv_back_block(S, tc, width, w_ref, b_ref, x_ref, dacc_of, dx_store, dw_ref, db_ref):
    for l0 in range(0, tc, 128):
        ls = slice(l0, l0 + 128)
        after = jnp.zeros((HALO, 128), F32)
        tot = None
        for r0 in range(S - STRIP, -1, -STRIP):
            xw = _with_halo(x_ref, r0, ls)
            acc, shifted = _conv_strip(xw, w_ref, b_ref, ls, width)
            dacc = dacc_of(r0, ls, acc, _sigmoid(acc))
            dx, dws, db = _conv_strip_back(dacc, after, xw[HALO:], shifted, w_ref, ls, width)
            dx_store(r0, ls, dx)
            after = dacc[:HALO]
            part = dws + [db]
            tot = part if tot is None else [p + q for p, q in zip(tot, part)]
        for k in range(width):
            dw_ref[pl.ds(k, 1), ls] = tot[k]
        db_ref[:, ls] = tot[width]


def _conv_silu_fwd(xin, col0, C, w, b, *, name, tc=512):
    S = xin.shape[0]
    width = w.shape[0]
    off = col0 // tc

    def body(x_ref, w_ref, b_ref, o_ref):
        for r0, ls in _strips(S, tc):
            acc, _ = _conv_strip(_with_halo(x_ref, r0, ls), w_ref, b_ref, ls, width)
            o_ref[r0:r0 + STRIP, ls] = acc * _sigmoid(acc)

    return pl.pallas_call(
        body, name=name, grid=(C // tc,),
        in_specs=[pl.BlockSpec((S, tc), lambda j: (0, j + off)), pl.BlockSpec((width, tc), lambda j: (0, j)),
                  pl.BlockSpec((1, tc), lambda j: (0, j))],
        out_specs=pl.BlockSpec((S, tc), lambda j: (0, j)), out_shape=jax.ShapeDtypeStruct((S, C), F32),
        compiler_params=_cp(1),
    )(xin, w, b)


def _conv_silu_bwd(xin, col0, C, w, b, douts, *, name, tc=256):
    S = xin.shape[0]
    width = w.shape[0]
    off = col0 // tc
    nd = len(douts)
    ranges = [(o // tc, (o + d.shape[1]) // tc) for d, o in douts]

    def body(*refs):
        x_ref, w_ref, b_ref = refs[0], refs[1], refs[2]
        d_refs = refs[3:3 + nd]
        dx_ref, dw_ref, db_ref = refs[3 + nd], refs[4 + nd], refs[5 + nd]
        j = pl.program_id(0)

        def dacc_of(r0, ls, acc, sg):
            dout = jnp.zeros((STRIP, 128), F32)
            for q in range(nd):
                lo, hi = ranges[q]
                dout = dout + jnp.where((j >= lo) & (j < hi), d_refs[q][r0:r0 + STRIP, ls], 0.0)
            return dout * (sg * (1.0 + acc * (1.0 - sg)))

        def dx_store(r0, ls, dx):
            dx_ref[r0:r0 + STRIP, ls] = dx.astype(BF16)

        _conv_back_block(S, tc, width, w_ref, b_ref, x_ref, dacc_of, dx_store, dw_ref, db_ref)

    d_specs = [pl.BlockSpec((S, tc), (lambda j, lo=lo, hi=hi: (0, jnp.clip(j - lo, 0, hi - lo - 1)))) for lo, hi in ranges]
    return pl.pallas_call(
        body, name=name, grid=(C // tc,),
        in_specs=[pl.BlockSpec((S, tc), lambda j: (0, j + off)), pl.BlockSpec((width, tc), lambda j: (0, j)),
                  pl.BlockSpec((1, tc), lambda j: (0, j))] + d_specs,
        out_specs=[pl.BlockSpec((S, tc), lambda j: (0, j)), pl.BlockSpec((width, tc), lambda j: (0, j)),
                   pl.BlockSpec((1, tc), lambda j: (0, j))],
        out_shape=[jax.ShapeDtypeStruct((S, C), BF16), jax.ShapeDtypeStruct((width, C), F32),
                   jax.ShapeDtypeStruct((1, C), F32)],
        compiler_params=_cp(1),
    )(xin, w, b, *[d for d, _ in douts])


def _ffn_act_fwd(u, w, b, *, name, tc=256):
    S, F2 = u.shape
    Fd = F2 // 2
    width = w.shape[0]
    nb = Fd // tc

    def body(g_ref, v_ref, w_ref, b_ref, o_ref):
        for r0, ls in _strips(S, tc):
            acc, _ = _conv_strip(_with_halo(g_ref, r0, ls), w_ref, b_ref, ls, width)
            o_ref[r0:r0 + STRIP, ls] = (acc * _sigmoid(acc) * v_ref[r0:r0 + STRIP, ls]).astype(BF16)

    return pl.pallas_call(
        body, name=name, grid=(nb,),
        in_specs=[pl.BlockSpec((S, tc), lambda j: (0, j)), pl.BlockSpec((S, tc), lambda j: (0, j + nb)),
                  pl.BlockSpec((width, tc), lambda j: (0, j)), pl.BlockSpec((1, tc), lambda j: (0, j))],
        out_specs=pl.BlockSpec((S, tc), lambda j: (0, j)), out_shape=jax.ShapeDtypeStruct((S, Fd), BF16),
        compiler_params=_cp(1),
    )(u, u, w, b)


def _ffn_act_bwd(u, w, b, da, *, name, tc=256):
    S, F2 = u.shape
    Fd = F2 // 2
    width = w.shape[0]
    nb = Fd // tc

    def body(g_ref, v_ref, w_ref, b_ref, da_ref, du_ref, dw_ref, db_ref, a_ref):
        def dacc_of(r0, ls, acc, sg):
            rs = slice(r0, r0 + STRIP)
            dav, val, silu = da_ref[rs, ls], v_ref[rs, ls], acc * sg
            a_ref[rs, ls] = (silu * val).astype(BF16)
            du_ref[1, rs, ls] = (dav * silu).astype(BF16)
            return dav * val * (sg * (1.0 + acc * (1.0 - sg)))

        def dx_store(r0, ls, dx):
            du_ref[0, r0:r0 + STRIP, ls] = dx.astype(BF16)

        _conv_back_block(S, tc, width, w_ref, b_ref, g_ref, dacc_of, dx_store, dw_ref, db_ref)

    blk = pl.BlockSpec((S, tc), lambda j: (0, j))
    return pl.pallas_call(
        body, name=name, grid=(nb,),
        in_specs=[blk, pl.BlockSpec((S, tc), lambda j: (0, j + nb)), pl.BlockSpec((width, tc), lambda j: (0, j)),
                  pl.BlockSpec((1, tc), lambda j: (0, j)), blk],
        out_specs=[pl.BlockSpec((2, S, tc), lambda j: (0, 0, j)), pl.BlockSpec((width, tc), lambda j: (0, j)),
                   pl.BlockSpec((1, tc), lambda j: (0, j)), blk],
        out_shape=[jax.ShapeDtypeStruct((2, S, Fd), BF16),
                   jax.ShapeDtypeStruct((width, Fd), F32), jax.ShapeDtypeStruct((1, Fd), F32),
                   jax.ShapeDtypeStruct((S, Fd), BF16)],
        compiler_params=_cp(1),
    )(u, u, w, b, da)


def _ssd_prep(dtr, dt_bias, a_log, *, name="ssd_prep"):
    S = dtr.shape[0]

    def body(d_ref, b_ref, al_ref, dt_ref, ac_ref, sg_ref, act_ref):
        lane = _iota((CHUNK, 128), 1)
        valid = lane < SSM_HEADS
        z = d_ref[...] + b_ref[...]
        dt = jnp.where(valid, jnp.maximum(z, 0.0) + jnp.log(1.0 + jnp.exp(-jnp.abs(z))), 0.0)
        a = dt * (-jnp.exp(al_ref[...]))
        row = _iota((CHUNK, 128), 0)
        k = 1
        while k < CHUNK:
            a = a + jnp.where(row >= k, pltpu.roll(a, k, axis=0), 0.0)
            k *= 2
        sg = jnp.where(valid, _sigmoid(z), 0.0)
        for arr, ref in ((dt, dt_ref), (a, ac_ref), (sg, sg_ref)):
            for g in range(SSM_GROUPS):
                ref[g] = jnp.where(lane < 4, arr if g == 0 else pltpu.roll(arr, 128 - 4 * g, axis=1), 0.0)
        act_ref[...] = a.T[:SSM_HEADS, :]

    blk = pl.BlockSpec((CHUNK, 128), lambda i: (i, 0))
    vec = pl.BlockSpec((1, 128), lambda i: (0, 0))
    grp = pl.BlockSpec((SSM_GROUPS, CHUNK, 128), lambda i: (0, i, 0))
    return pl.pallas_call(
        body, name=name, grid=(S // CHUNK,), in_specs=[blk, vec, vec],
        out_specs=[grp, grp, grp, pl.BlockSpec((SSM_HEADS, CHUNK), lambda i: (0, i))],
        out_shape=[jax.ShapeDtypeStruct((SSM_GROUPS, S, 128), F32)] * 3 + [jax.ShapeDtypeStruct((SSM_HEADS, S), F32)],
        compiler_params=_cp(1),
    )(dtr, dt_bias, a_log)


SSD_GPS = 4


def _expand4(v, lanes):
    out = jnp.broadcast_to(v[:, 3:4], lanes.shape)
    for hh in (2, 1, 0):
        out = jnp.where(lanes < 64 * (hh + 1), v[:, hh:hh + 1], out)
    return out


def _ssd_fwd(xbc, dt_g, ac_g, ac_t, *, name="ssd_fwd", dep=None):
    S = xbc.shape[0]
    nc = S // CHUNK
    Lc = CHUNK

    def body(x_ref, b_ref, c_ref, dt_ref, ac_ref, act_ref, *rest):
        y_ref, st_out_ref, st_ref = rest[-3:]
        g2 = pl.program_id(0)
        c = pl.program_id(1)

        @pl.when(c == 0)
        def _():
            st_ref[...] = jnp.zeros_like(st_ref)

        causal = _iota((Lc, Lc), 0) >= _iota((Lc, Lc), 1)
        lane256 = _iota((Lc, 256), 1)
        lane128 = _iota((Lc, 128), 1)
        row128 = _iota((128, 128), 0)
        for gg in range(SSD_GPS):
            g = SSD_GPS * g2 + gg
            bv = b_ref[:, 128 * gg:128 * (gg + 1)]
            cbf = c_ref[:, 128 * gg:128 * (gg + 1)].astype(BF16)
            cb = _dot(cbf, bv.astype(BF16), 1, 1)
            dtg, acg = dt_ref[gg], ac_ref[gg]
            ac_last = ac_ref[gg, pl.ds(Lc - 1, 1), :]
            dt4 = _expand4(dtg, lane256)
            ac4 = _expand4(acg, lane256)
            e4 = jnp.exp(ac4)
            xdb = (x_ref[:, 256 * gg:256 * (gg + 1)] * dt4).astype(BF16)
            st_out_ref[gg] = st_ref[gg]
            for p in range(2):
                xd_p = xdb[:, 128 * p:128 * (p + 1)]
                st_p = st_ref[gg, p]
                ys, sn, cds = [], [], []
                for q in range(2):
                    hh = 2 * p + q
                    a_col = acg[:, hh:hh + 1]
                    a_row = act_ref[pl.ds(4 * g + hh, 1), :]
                    dec = jnp.exp(jnp.where(causal, a_col - a_row, NEG))
                    w = (cb * dec).astype(BF16)
                    ys.append(_dot(w, xd_p))
                    al = ac_last[:, hh:hh + 1]
                    dte = jnp.exp(al - a_col)
                    sn.append(_dot(xd_p, (bv * dte).astype(BF16), 0, 0))
                    cds.append(jnp.exp(al))
                y_diag = jnp.where(lane128 < 64, ys[0], ys[1])
                y_off = _dot(cbf, st_p.astype(BF16), 1, 1) * e4[:, 128 * p:128 * (p + 1)]
                y_ref[:, 256 * gg + 128 * p:256 * gg + 128 * (p + 1)] = y_diag + y_off
                st_ref[gg, p] = jnp.where(row128 < 64, st_p * cds[0] + sn[0], st_p * cds[1] + sn[1])

    G = SSD_GPS
    per_g = lambda g, c: (g, c, 0)
    return pl.pallas_call(
        body, name=name, grid=(SSM_GROUPS // G, nc),
        in_specs=[pl.BlockSpec((Lc, 256 * G), lambda g, c: (c, g)),
                  pl.BlockSpec((Lc, 128 * G), lambda g, c: (c, 16 // G + g)),
                  pl.BlockSpec((Lc, 128 * G), lambda g, c: (c, 24 // G + g)),
                  pl.BlockSpec((G, Lc, 128), per_g), pl.BlockSpec((G, Lc, 128), per_g),
                  pl.BlockSpec((SSM_HEADS, Lc), lambda g, c: (0, c))] + ([] if dep is None else [pl.BlockSpec(memory_space=pl.ANY)]),
        out_specs=[pl.BlockSpec((Lc, 256 * G), lambda g, c: (c, g)),
                   pl.BlockSpec((G, None, 2, 128, 128), lambda g, c: (g, c, 0, 0, 0))],
        out_shape=[jax.ShapeDtypeStruct((S, 2048), F32), jax.ShapeDtypeStruct((SSM_GROUPS, nc, 2, 128, 128), F32)],
        scratch_shapes=[pltpu.VMEM((G, 2, 128, 128), F32)], compiler_params=_cp(2),
    )(xbc, xbc, xbc, dt_g, ac_g, ac_t, *([] if dep is None else [dep]))


def _ssd_bwd(xbc, dt_g, ac_g, ac_t, states, dy, dexp, *, name="ssd_bwd", dep=None):
    S = xbc.shape[0]
    nc = S // CHUNK
    Lc = CHUNK

    def body(x_ref, b_ref, c_ref, dt_ref, ac_ref, act_ref, st_ref, dy_ref, d_ref, *rest):
        dx_ref, db_ref, dc_ref, dh_ref, ds_ref = rest[-5:]
        g2 = pl.program_id(0)
        cc = pl.program_id(1)

        @pl.when(cc == 0)
        def _():
            ds_ref[...] = jnp.zeros_like(ds_ref)

        causal = _iota((Lc, Lc), 0) >= _iota((Lc, Lc), 1)
        lane256 = _iota((Lc, 256), 1)
        lane128 = _iota((Lc, 128), 1)
        row128 = _iota((128, 128), 0)
        ind_rows = _iota((256, 128), 0) >> 6
        ind_cols = _iota((256, 128), 1)
        ind_a = (ind_rows == ind_cols).astype(BF16)
        ind_b = (ind_rows + 4 == ind_cols).astype(BF16)
        for gg in range(SSD_GPS):
            g = SSD_GPS * g2 + gg
            bv = b_ref[:, 128 * gg:128 * (gg + 1)]
            cv = c_ref[:, 128 * gg:128 * (gg + 1)]
            bbf, cbf = bv.astype(BF16), cv.astype(BF16)
            cb = _dot(cbf, bbf, 1, 1)
            dtg, acg = dt_ref[gg], ac_ref[gg]
            ac_last = ac_ref[gg, pl.ds(Lc - 1, 1), :]
            dt4 = _expand4(dtg, lane256)
            ac4 = _expand4(acg, lane256)
            acl4 = _expand4(ac_last, _iota((1, 256), 1))
            e4 = jnp.exp(ac4)
            dte4 = jnp.exp(acl4 - ac4)
            xv = x_ref[:, 256 * gg:256 * (gg + 1)]
            xd = xv * dt4
            xdb = xd.astype(BF16)
            dyv = dy_ref[:, 256 * gg:256 * (gg + 1)]
            dcb = jnp.zeros((Lc, Lc), F32)
            dc_acc = jnp.zeros((Lc, 128), F32)
            db_acc = jnp.zeros((Lc, 128), F32)
            u_parts, dxd_parts, ends = [], [], []
            for p in range(2):
                sl = slice(128 * p, 128 * (p + 1))
                xd_p, xdb_p, dy_p = xd[:, sl], xdb[:, sl], dyv[:, sl]
                dyb_p = dy_p.astype(BF16)
                e_p, dte_p = e4[:, sl], dte4[:, sl]
                sp = st_ref[gg, p]
                spb = sp.astype(BF16)
                dsn = ds_ref[gg, p]
                dsnb = dsn.astype(BF16)
                yds, dxds, cds = [], [], []
                for q in range(2):
                    hh = 2 * p + q
                    a_col = acg[:, hh:hh + 1]
                    a_row = act_ref[pl.ds(4 * g + hh, 1), :]
                    dec = jnp.exp(jnp.where(causal, a_col - a_row, NEG))
                    w = (cb * dec).astype(BF16)
                    head = (lane128 < 64) if q == 0 else (lane128 >= 64)
                    dym = jnp.where(head, dyb_p, jnp.zeros_like(dyb_p))
                    dw = _dot(dym, xdb_p, 1, 1)
                    dcb = dcb + dw * dec
                    yds.append(_dot(w, xdb_p))
                    dxds.append(_dot(w, dyb_p, 0, 0))
                    cds.append(jnp.exp(ac_last[:, hh:hh + 1]))
                y_diag = jnp.where(lane128 < 64, yds[0], yds[1])
                dxd_diag = jnp.where(lane128 < 64, dxds[0], dxds[1])
                y_off = _dot(cbf, spb, 1, 1) * e_p
                dgp = dy_p * e_p
                dgb = dgp.astype(BF16)
                dc_acc = dc_acc + _dot(dgb, spb)
                dsp = _dot(dgb, cbf, 0, 0)
                cd_col = jnp.where(row128[:, 0:1] < 64, cds[0], cds[1])
                qm = _dot(bbf, dsnb, 1, 1)
                dxd_state = dte_p * qm
                db_acc = db_acc + _dot((xd_p * dte_p).astype(BF16), dsnb)
                t_p = xd_p * dxd_state
                prod = dsn * sp
                e0 = jnp.sum(jnp.sum(jnp.where(row128 < 64, prod, 0.0), axis=1, keepdims=True), axis=0, keepdims=True)
                e1 = jnp.sum(jnp.sum(jnp.where(row128 >= 64, prod, 0.0), axis=1, keepdims=True), axis=0, keepdims=True)
                tcol = jnp.sum(t_p, axis=0, keepdims=True)
                lane1 = _iota((1, 128), 1)
                t0 = jnp.sum(jnp.where(lane1 < 64, tcol, 0.0), axis=1, keepdims=True)
                t1 = jnp.sum(jnp.where(lane1 >= 64, tcol, 0.0), axis=1, keepdims=True)
                ends.append(e0 * cds[0] + t0)
                ends.append(e1 * cds[1] + t1)
                ds_ref[gg, p] = dsn * cd_col + dsp
                u_parts.append(dyb_p.astype(F32) * y_diag - xdb_p.astype(F32) * dxd_diag + dy_p * y_off - t_p)
                dxd_parts.append(dxd_diag + dxd_state)
            dxd = jnp.concatenate(dxd_parts, axis=1)
            u_all = jnp.concatenate(u_parts, axis=1)
            dx_ref[:, 256 * gg:256 * (gg + 1)] = dxd * dt4 + dyv * d_ref[:, 256 * gg:256 * (gg + 1)]
            dcbb = dcb.astype(BF16)
            dc_ref[:, 128 * gg:128 * (gg + 1)] = dc_acc + _dot(dcbb, bbf)
            db_ref[:, 128 * gg:128 * (gg + 1)] = db_acc + _dot(dcbb, cbf, 0, 0)
            lane = _iota((Lc, 128), 1)
            endv = jnp.zeros((Lc, 128), F32)
            for hh in range(4):
                endv = jnp.where(lane == 8 + hh, ends[hh], endv)
            dh_ref[gg] = _dot3(dxd * xv, ind_a) + _dot3(u_all, ind_b) + endv

    G = SSD_GPS
    rev = lambda c: nc - 1 - c
    per_g = lambda g, c: (g, rev(c), 0)
    return pl.pallas_call(
        body, name=name, grid=(SSM_GROUPS // G, nc),
        in_specs=[pl.BlockSpec((Lc, 256 * G), lambda g, c: (rev(c), g)),
                  pl.BlockSpec((Lc, 128 * G), lambda g, c: (rev(c), 16 // G + g)),
                  pl.BlockSpec((Lc, 128 * G), lambda g, c: (rev(c), 24 // G + g)),
                  pl.BlockSpec((G, Lc, 128), per_g), pl.BlockSpec((G, Lc, 128), per_g),
                  pl.BlockSpec((SSM_HEADS, Lc), lambda g, c: (0, rev(c))),
                  pl.BlockSpec((G, None, 2, 128, 128), lambda g, c: (g, rev(c), 0, 0, 0)),
                  pl.BlockSpec((Lc, 256 * G), lambda g, c: (rev(c), g)),
                  pl.BlockSpec((1, 256 * G), lambda g, c: (0, g))] + ([] if dep is None else [pl.BlockSpec(memory_space=pl.ANY)]),
        out_specs=[pl.BlockSpec((Lc, 256 * G), lambda g, c: (rev(c), g)),
                   pl.BlockSpec((Lc, 128 * G), lambda g, c: (rev(c), g)),
                   pl.BlockSpec((Lc, 128 * G), lambda g, c: (rev(c), g)),
                   pl.BlockSpec((G, Lc, 128), per_g)],
        out_shape=[jax.ShapeDtypeStruct((S, 2048), F32), jax.ShapeDtypeStruct((S, 1024), F32),
                   jax.ShapeDtypeStruct((S, 1024), F32), jax.ShapeDtypeStruct((SSM_GROUPS, S, 128), F32)],
        scratch_shapes=[pltpu.VMEM((G, 2, 128, 128), F32)], compiler_params=_cp(2),
    )(xbc, xbc, xbc, dt_g, ac_g, ac_t, states, dy, dexp, *([] if dep is None else [dep]))


def _ssd_post(dhead, dt_g, sg_g, alog_g, *, name="ssd_post"):
    S = dhead.shape[1]
    nc = S // CHUNK
    Lc = CHUNK

    def body(dh_ref, dt_ref, sg_ref, al_ref, o_ref, s_ref):
        @pl.when(pl.program_id(0) == 0)
        def _():
            s_ref[...] = jnp.zeros_like(s_ref)

        lane = _iota((Lc, 128), 1)
        row = _iota((Lc, 128), 0)
        row8 = _iota((8, 128), 0)
        out = jnp.zeros((Lc, 128), F32)
        for g in range(SSM_GROUPS):
            dh = dh_ref[g]
            a_neg = -jnp.exp(al_ref[g])
            dac = jnp.where(lane < 4, pltpu.roll(dh, 124, axis=1), 0.0)
            end = jnp.where(lane < 4, pltpu.roll(dh, 120, axis=1), 0.0)
            k = 1
            while k < Lc:
                dac = dac + jnp.where(row < Lc - k, pltpu.roll(dac, Lc - k, axis=0), 0.0)
                k *= 2
            da = dac + end
            ddt = jnp.where(lane < 4, da * a_neg + dh, 0.0)
            ddtr = ddt * sg_ref[g]
            out = out + (ddtr if g == 0 else pltpu.roll(ddtr, 4 * g, axis=1))
            dal = jnp.sum(da * dt_ref[g], axis=0, keepdims=True) * a_neg
            dbias = jnp.sum(ddtr, axis=0, keepdims=True)
            part = jnp.where(row8 == 0, dal, jnp.where(row8 == 1, dbias, 0.0))
            s_ref[g] += part
        o_ref[...] = out.astype(BF16)

    grp = pl.BlockSpec((SSM_GROUPS, Lc, 128), lambda c: (0, c, 0))
    whole = lambda r: pl.BlockSpec((SSM_GROUPS, r, 128), lambda c: (0, 0, 0))
    return pl.pallas_call(
        body, name=name, grid=(nc,), in_specs=[grp, grp, grp, whole(1)],
        out_specs=[pl.BlockSpec((Lc, 128), lambda c: (c, 0)), whole(8)],
        out_shape=[jax.ShapeDtypeStruct((S, 128), BF16), jax.ShapeDtypeStruct((SSM_GROUPS, 8, 128), F32)],
        compiler_params=_cp(1),
    )(dhead, dt_g, sg_g, alog_g)


def _gate_fwd(y, xbc, zx, dexp, gn, *, name="gate_fwd", tr=256, dep=None):
    S = y.shape[0]
    W = 2048
    gw = W // SSM_GROUPS

    def body(y_ref, x_ref, z_ref, d_ref, g_ref, *rest):
        o_ref = rest[-1]
        z = z_ref[...]
        u = (y_ref[...] + x_ref[...] * d_ref[...]) * (z * _sigmoid(z))
        gv = g_ref[...]
        for q in range(SSM_GROUPS):
            sl = slice(gw * q, gw * (q + 1))
            uq = u[:, sl]
            r = lax.rsqrt(jnp.mean(uq * uq, axis=-1, keepdims=True) + EPS)
            o_ref[:, sl] = (uq * r * gv[:, sl]).astype(BF16)

    row = pl.BlockSpec((tr, W), lambda i: (i, 0))
    vec = pl.BlockSpec((1, W), lambda i: (0, 0))
    return pl.pallas_call(
        body, name=name, grid=(S // tr,),
        in_specs=[row, row, row, vec, vec] + ([] if dep is None else [pl.BlockSpec(memory_space=pl.ANY)]), out_specs=row,
        out_shape=jax.ShapeDtypeStruct((S, W), BF16), compiler_params=_cp(1),
    )(y, xbc, zx, dexp, gn, *([] if dep is None else [dep]))


def _gate_bwd(y, xbc, zx, dexp, gn, dout, *, name="gate_bwd", tr=256):
    S = y.shape[0]
    W = 2048
    gw = W // SSM_GROUPS
    steps = S // tr

    def body(y_ref, x_ref, z_ref, d_ref, g_ref, do_ref, dy_ref, dz_ref, dg_ref, dd_ref, acc_ref):
        i = pl.program_id(0)

        @pl.when(i == 0)
        def _():
            acc_ref[...] = jnp.zeros_like(acc_ref)

        z = z_ref[...]
        sg = _sigmoid(z)
        sz = z * sg
        xs = x_ref[...]
        yt = y_ref[...] + xs * d_ref[...]
        u = yt * sz
        gv = g_ref[...]
        do = do_ref[...]
        dgs = []
        for q in range(SSM_GROUPS):
            sl = slice(gw * q, gw * (q + 1))
            uq = u[:, sl]
            r = lax.rsqrt(jnp.mean(uq * uq, axis=-1, keepdims=True) + EPS)
            uh = uq * r
            dq = do[:, sl]
            duh = dq * gv[:, sl]
            duq = r * (duh - uh * jnp.mean(duh * uh, axis=-1, keepdims=True))
            dgs.append(jnp.sum(dq * uh, axis=0, keepdims=True))
            dyt = duq * sz[:, sl]
            dy_ref[:, sl] = dyt
            dz_ref[:, sl] = (duq * yt[:, sl] * (sg[:, sl] * (1.0 + z[:, sl] * (1.0 - sg[:, sl])))).astype(BF16)
            acc_ref[:, sl] += jnp.sum(dyt * xs[:, sl], axis=0, keepdims=True)
        dg = jnp.concatenate(dgs, axis=1)

        @pl.when(i == 0)
        def _():
            dg_ref[...] = dg

        @pl.when(i > 0)
        def _():
            dg_ref[...] += dg

        @pl.when(i == steps - 1)
        def _():
            ind = ((_iota((W, 128), 0) >> 6) == _iota((W, 128), 1)).astype(BF16)
            dd_ref[...] = _dot3(jnp.broadcast_to(acc_ref[...], (8, W)), ind)[0:1, :]

    row = pl.BlockSpec((tr, W), lambda i: (i, 0))
    vec = pl.BlockSpec((1, W), lambda i: (0, 0))
    return pl.pallas_call(
        body, name=name, grid=(steps,), in_specs=[row, row, row, vec, vec, row],
        out_specs=[row, row, vec, pl.BlockSpec((1, 128), lambda i: (0, 0))],
        out_shape=[jax.ShapeDtypeStruct((S, W), F32), jax.ShapeDtypeStruct((S, W), BF16),
                   jax.ShapeDtypeStruct((1, W), F32), jax.ShapeDtypeStruct((1, 128), F32)],
        scratch_shapes=[pltpu.VMEM((1, W), F32)], compiler_params=_cp(1),
    )(y, xbc, zx, dexp, gn, dout)


def _rope_cs(posf, *, name="rope_tables", tr=256):
    S = posf.shape[0]

    def body(p_ref, c_ref, s_ref):
        j = (_iota((tr, 128), 1) & 31).astype(F32)
        ang = p_ref[...] * jnp.exp(j * (-math.log(ROPE_THETA) / 32.0))
        c_ref[...] = jnp.cos(ang)
        s_ref[...] = jnp.sin(ang)

    blk = pl.BlockSpec((tr, 128), lambda i: (i, 0))
    return pl.pallas_call(
        body, name=name, grid=(S // tr,), in_specs=[pl.BlockSpec((tr, 1), lambda i: (i, 0))], out_specs=[blk, blk],
        out_shape=[jax.ShapeDtypeStruct((S, 128), F32)] * 2, compiler_params=_cp(1),
    )(posf)


def _rope_tables(c_ref, s_ref, shape):
    reps = shape[1] // 128
    return jnp.tile(c_ref[...], (1, reps)), jnp.tile(s_ref[...], (1, reps)), (_iota(shape, 1) & 63) < 32


def _hn_inds(W):
    ind = ((_iota((W, 128), 0) >> 6) == _iota((W, 128), 1)).astype(BF16)
    ind_t = ((_iota((128, W), 1) >> 6) == _iota((128, W), 0)).astype(BF16)
    return ind, ind_t


def _hnrope_fwd(xin, col0, W, gain_w, rope, *, name, tr=256):
    S = xin.shape[0]
    off = col0 // W
    nh = W // HEAD

    def body(x_ref, g_ref, c_ref, s_ref, o_ref):
        x = x_ref[...]
        ind, ind_t = _hn_inds(W)
        r = lax.rsqrt(_dot3(x * x, ind) * (1.0 / HEAD) + EPS)
        xn = x * _dot3(r, ind_t) * g_ref[...]
        cs, sn, half = _rope_tables(c_ref, s_ref, (tr, W))
        rot = jnp.where(half, -pltpu.roll(xn, W - 32, axis=1), pltpu.roll(xn, 32, axis=1))
        out = (xn * cs + rot * sn).astype(BF16)
        for h in range(nh):
            o_ref[h] = out[:, HEAD * h:HEAD * (h + 1)]

    tab = pl.BlockSpec((tr, 128), lambda i: (i, 0))
    return pl.pallas_call(
        body, name=name, grid=(S // tr,),
        in_specs=[pl.BlockSpec((tr, W), lambda i: (i, off)), pl.BlockSpec((1, W), lambda i: (0, 0)), tab, tab],
        out_specs=pl.BlockSpec((nh, tr, HEAD), lambda i: (0, i, 0)), out_shape=jax.ShapeDtypeStruct((nh, S, HEAD), BF16),
        compiler_params=_cp(1),
    )(xin, gain_w, *rope)


def _hnrope_bwd(xin, col0, W, gain_w, rope, dout, *, name, tr=256):
    S = xin.shape[0]
    off = col0 // W
    steps = S // tr
    nh = W // HEAD

    def body(x_ref, g_ref, c_ref, s_ref, do_ref, dx_ref, cs_ref, dg_ref, acc_ref):
        i = pl.program_id(0)
        x = x_ref[...]
        ind, ind_t = _hn_inds(W)
        r = lax.rsqrt(_dot3(x * x, ind) * (1.0 / HEAD) + EPS)
        rw = _dot3(r, ind_t)
        xh = x * rw
        cs, sn, half = _rope_tables(c_ref, s_ref, (tr, W))
        do = jnp.concatenate([do_ref[h] for h in range(nh)], axis=1).astype(F32)
        gs = do * sn
        g1 = do * cs + jnp.where(half, pltpu.roll(gs, W - 32, axis=1), -pltpu.roll(gs, 32, axis=1))
        dxh = g1 * g_ref[...]
        t = _dot3(dxh * xh, ind) * (1.0 / HEAD)
        dx = rw * (dxh - xh * _dot3(t, ind_t))
        dx_ref[...] = dx.astype(BF16)
        cpart = jnp.sum(dx, axis=0, keepdims=True)
        gpart = jnp.sum(g1 * xh, axis=0, keepdims=True)

        @pl.when(i == 0)
        def _():
            cs_ref[...] = cpart
            acc_ref[...] = gpart

        @pl.when(i > 0)
        def _():
            cs_ref[...] += cpart
            acc_ref[...] += gpart

        @pl.when(i == steps - 1)
        def _():
            fold = ((_iota((W, 128), 0) & 63) == _iota((W, 128), 1)).astype(BF16)
            dg_ref[...] = _dot3(jnp.broadcast_to(acc_ref[...], (8, W)), fold)[0:1, :]

    tab = pl.BlockSpec((tr, 128), lambda i: (i, 0))
    return pl.pallas_call(
        body, name=name, grid=(steps,),
        in_specs=[pl.BlockSpec((tr, W), lambda i: (i, off)), pl.BlockSpec((1, W), lambda i: (0, 0)), tab, tab,
                  pl.BlockSpec((nh, tr, HEAD), lambda i: (0, i, 0))],
        out_specs=[pl.BlockSpec((tr, W), lambda i: (i, 0)), pl.BlockSpec((1, W), lambda i: (0, 0)),
                   pl.BlockSpec((1, 128), lambda i: (0, 0))],
        out_shape=[jax.ShapeDtypeStruct((S, W), BF16), jax.ShapeDtypeStruct((1, W), F32),
                   jax.ShapeDtypeStruct((1, 128), F32)],
        scratch_shapes=[pltpu.VMEM((1, W), F32)], compiler_params=_cp(1),
    )(xin, gain_w, *rope, dout)


def _attn_band():
    qi = jnp.arange(ATT_G * WINDOW)[:, None] % WINDOW
    ki = jnp.arange(2 * WINDOW)[None, :]
    rel = qi + WINDOW - ki
    ok = (rel >= 0) & (rel < WINDOW)
    return jnp.stack([jnp.where(ok & (ki >= WINDOW), 0.0, NEG), jnp.where(ok, 0.0, NEG)]).astype(F32)


def _attn_probs(q, kb, sink_ref, band_ref, h, i):
    s = _dot(q, kb, 1, 1) * (HEAD ** -0.5) + band_ref[jnp.minimum(i, 1)]
    r1 = _iota((4 * WINDOW, 1), 0)
    sink = jnp.where(r1 < WINDOW, sink_ref[4 * h], jnp.where(r1 < 2 * WINDOW, sink_ref[4 * h + 1],
                     jnp.where(r1 < 3 * WINDOW, sink_ref[4 * h + 2], sink_ref[4 * h + 3])))
    m = jnp.maximum(jnp.max(s, axis=1, keepdims=True), sink)
    p = jnp.exp(s - m)
    ps = jnp.exp(sink - m)
    inv = 1.0 / (jnp.sum(p, axis=1, keepdims=True) + ps)
    return p * inv, ps * inv


ATT_HPS = 4
_BAND = pl.BlockSpec((2, ATT_G * WINDOW, 2 * WINDOW), lambda h, i: (0, 0, 0))


def _attn_specs(S):
    qspec = pl.BlockSpec((ATT_HPS, ATT_G, WINDOW, HEAD), lambda h, i: (h, 0, i, 0))
    cur = pl.BlockSpec((ATT_HPS, WINDOW, HEAD), lambda h, i: (h, i, 0))
    prev = pl.BlockSpec((ATT_HPS, WINDOW, HEAD), lambda h, i: (h, jnp.maximum(i - 1, 0), 0))
    tok = pl.BlockSpec((WINDOW, ATT_HPS * ATT_G * HEAD), lambda h, i: (i, h))
    return qspec, cur, prev, tok


def _attn_fwd(qh, kh, vh, sinks, *, name="attn_fwd"):
    S = kh.shape[1]
    nb = S // WINDOW

    def body(s_ref, band_ref, q_ref, kc_ref, kp_ref, vc_ref, vp_ref, o_ref):
        h2, i = pl.program_id(0), pl.program_id(1)
        outs = []
        for hh in range(ATT_HPS):
            q = q_ref[hh].reshape(ATT_G * WINDOW, HEAD)
            kb = jnp.concatenate([kp_ref[hh], kc_ref[hh]], axis=0)
            vb = jnp.concatenate([vp_ref[hh], vc_ref[hh]], axis=0)
            probs, _ = _attn_probs(q, kb, s_ref, band_ref, ATT_HPS * h2 + hh, i)
            o = _dot(probs.astype(BF16), vb).astype(BF16)
            outs += [o[WINDOW * g:WINDOW * (g + 1)] for g in range(ATT_G)]
        o_ref[...] = jnp.concatenate(outs, axis=1)

    qspec, cur, prev, tok = _attn_specs(S)
    return pl.pallas_call(
        body, name=name, grid=(ATT_KV // ATT_HPS, nb),
        in_specs=[pl.BlockSpec(memory_space=pltpu.SMEM), _BAND, qspec, cur, prev, cur, prev], out_specs=tok,
        out_shape=jax.ShapeDtypeStruct((S, ATT_KV * ATT_G * HEAD), BF16), compiler_params=_cp(2),
    )(sinks, _attn_band(), qh, kh, kh, vh, vh)


def _attn_bwd(qh, kh, vh, sinks, doh, *, name="attn_bwd"):
    S = kh.shape[1]
    nb = S // WINDOW

    def body(s_ref, band_ref, q_ref, kc_ref, kp_ref, vc_ref, vp_ref, do_ref, dq_ref, dk_ref, dv_ref, dsk_ref):
        h2, i = pl.program_id(0), pl.program_id(1)

        @pl.when(i == 0)
        def _():
            dk_ref[...] = jnp.zeros_like(dk_ref)
            dv_ref[...] = jnp.zeros_like(dv_ref)
            dsk_ref[...] = jnp.zeros_like(dsk_ref)

        dov = do_ref[...]
        cur = pl.multiple_of(i * WINDOW, WINDOW)
        lane = _iota((8, 128), 1)
        row = _iota((8, 128), 0)
        scale = HEAD ** -0.5
        for hh in range(ATT_HPS):
            q = q_ref[hh].reshape(ATT_G * WINDOW, HEAD)
            do = jnp.concatenate([dov[:, HEAD * (ATT_G * hh + g):HEAD * (ATT_G * hh + g + 1)] for g in range(ATT_G)], axis=0)
            kb = jnp.concatenate([kp_ref[hh], kc_ref[hh]], axis=0)
            vb = jnp.concatenate([vp_ref[hh], vc_ref[hh]], axis=0)
            probs, psink = _attn_probs(q, kb, s_ref, band_ref, ATT_HPS * h2 + hh, i)
            dp = _dot(do, vb, 1, 1)
            delta = jnp.sum(probs * dp, axis=1, keepdims=True)
            ds = (probs * (dp - delta)).astype(BF16)
            dq_ref[hh] = (_dot(ds, kb) * scale).reshape(ATT_G, WINDOW, HEAD)
            dkb = _dot(ds, q, 0, 0) * scale
            dvb = _dot(probs.astype(BF16), do, 0, 0)
            dk_ref[hh, pl.ds(cur, WINDOW), :] += dkb[WINDOW:, :]
            dv_ref[hh, pl.ds(cur, WINDOW), :] += dvb[WINDOW:, :]
            prv = pl.multiple_of(jnp.maximum(i - 1, 0) * WINDOW, WINDOW)
            dk_ref[hh, pl.ds(prv, WINDOW), :] += dkb[:WINDOW, :]
            dv_ref[hh, pl.ds(prv, WINDOW), :] += dvb[:WINDOW, :]

            dsr = -psink * delta
            upd = jnp.zeros((8, 128), F32)
            for gq in range(ATT_G):
                v = jnp.sum(dsr[gq * WINDOW:(gq + 1) * WINDOW, :], axis=0, keepdims=True)
                upd = jnp.where((lane == gq) & (row == 0), v, upd)
            dsk_ref[hh] += upd

    qspec, cur, prev, tok = _attn_specs(S)
    full = pl.BlockSpec((ATT_HPS, S, HEAD), lambda h, i: (h, 0, 0))
    return pl.pallas_call(
        body, name=name, grid=(ATT_KV // ATT_HPS, nb),
        in_specs=[pl.BlockSpec(memory_space=pltpu.SMEM), _BAND, qspec, cur, prev, cur, prev, tok],
        out_specs=[qspec, full, full, pl.BlockSpec((ATT_HPS, 8, 128), lambda h, i: (h, 0, 0))],
        out_shape=[jax.ShapeDtypeStruct((ATT_KV, ATT_G, S, HEAD), F32), jax.ShapeDtypeStruct((ATT_KV, S, HEAD), F32),
                   jax.ShapeDtypeStruct((ATT_KV, S, HEAD), F32), jax.ShapeDtypeStruct((ATT_KV, 8, 128), F32)],
        compiler_params=_cp(2),
    )(sinks, _attn_band(), qh, kh, kh, vh, vh, doh)


def _heads_major(t, nh):
    S = t.shape[0]
    return t.reshape(S, nh, HEAD).transpose(1, 0, 2)


def _tokens_major(t):
    nh, S, _ = t.shape
    return t.transpose(1, 0, 2).reshape(S, nh * HEAD)


class _NoComm:
    def late_start(self, part, after):
        return None

    def late_arrived(self, part, after):
        return None

    def late_weights(self, w, after, part):
        return w

    def advance(self, after, group=None, tensors=None):
        return None


def _local_step(x, posf, target, w, comm=None):
    S, D = x.shape
    gr = {}
    comm = comm or _NoComm()

    zx, h1 = _norm_mm(x, w["a_norm"], w["w_zx"], name="in_proj_zx", dep=w.get("dep"))
    dtr = _mm(h1, w["w_dt"], name="in_proj_dt")
    xbc = _conv_silu_fwd(zx, 2048, 4096, w["a_conv_w"], w["a_conv_b"], name="a_conv_f")
    dt_g, ac_g, sg_g, ac_t = _ssd_prep(dtr, w["a_dt_bias"], w["a_A_log"])
    y_ssd, states = _ssd_fwd(xbc, dt_g, ac_g, ac_t, dep=comm.late_start(1, xbc))
    yg = _gate_fwd(y_ssd, xbc, zx, w["a_Dexp"], w["a_gnorm"], dep=comm.late_arrived(0, [y_ssd]))
    w = comm.late_weights(w, yg, 0)
    x1 = _mm(yg, w["a_out_proj"], res=x, name="out_proj")

    FW = w["f_w_in"][0].shape[2]

    def ffn_fwd(xin, l, loss_target=None):
        u, h = _norm_mm(xin, w["f_norm"][l], w["f_w_in"][l], name=f"f_in{l}", N=N_CHIPS * FW, tn=FW,
                        b_spec=pl.BlockSpec((None, D, FW), lambda i, j: (j, 0, 0)))
        a = _ffn_act_fwd(u, w["f_conv_w"][l], w["f_conv_b"][l], name=f"f_act_f{l}")
        dep = comm.late_arrived(1, [u]) if l == 0 else None
        xo = _mm(a, w["f_w_down"][l], res=xin, tk=a.shape[1], name=f"f_down{l}", target=loss_target, dep=dep)
        return xo, (h, u)

    x2, ffn0 = ffn_fwd(x1, 0)
    w = comm.late_weights(w, x2, 1)

    kv, hk = _norm_mm(x2, w["kv_norm"], w["w_kv"], bias=w["b_kv"], name="kv_proj")
    q, hq = _norm_mm(x2, w["b_norm"], w["w_q"], bias=w["b_q"], name="q_proj")
    rope = w["rope"] if "rope" in w else _rope_cs(posf)
    kr = _hnrope_fwd(kv, 0, 256, w["k_norm_w"], rope, name="k_rope_f")
    qr = _hnrope_fwd(q, 0, 1024, w["q_norm_w"], rope, name="q_rope_f")
    qh = qr.reshape(ATT_KV, ATT_G, S, HEAD)
    kh = kr
    vh = _heads_major(kv[:, 256:].astype(BF16), ATT_KV)
    att = _attn_fwd(qh, kh, vh, w["sinks"])
    x3 = _mm(att, w["w_o"], bias=w["b_o"], res=x2, name="o_proj")
    (dy, loss_part), ffn1 = ffn_fwd(x3, 1, target)

    def ffn_bwd(xin, l, saved, dyo, want_colsum, dep=None):
        h, u = saved
        da = _mm(dyo, w["f_w_down"][l], tb=True, name=f"f_down_dx{l}", dep=dep)
        du, dcw, dcb, a = _ffn_act_bwd(u, w["f_conv_w"][l], w["f_conv_b"][l], da, name=f"f_act_b{l}")
        dw_down = _mm(a, dyo, ta=True, out_dtype=BF16, name=f"f_down_dw{l}")
        dw_in = _mm(h, du, ta=True, out_dtype=BF16, name=f"f_in_dw{l}", dims=(D, N_CHIPS * FW, S), tm=D, tn=FW, tk=S,
                    b_spec=pl.BlockSpec((None, S, FW), lambda i, j, k: (j // 2, 0, j % 2)),
                    o_spec=pl.BlockSpec((None, D, FW), lambda i, j, k: (j, i, 0)), o_shape=(N_CHIPS, D, FW))
        ts = _pick(S, (512, 256))
        pieces = [(pl.BlockSpec((None, ts, FW), lambda i, j, k, q=q: (q // 2, i, q % 2)),
                   pl.BlockSpec((None, D, FW), lambda i, j, k, q=q: (q, 0, 0), pipeline_mode=pl.Buffered(1)))
                  for q in range(N_CHIPS)]
        outs = _mm(du, w["f_w_in"][l], tb=True, name=f"f_in_dx{l}", dims=(S, D, FW), tm=ts, tn=D, tk=FW,
                   a_spec=pieces[0][0], b_spec=pieces[0][1], more=pieces[1:],
                   rms=(xin, w["f_norm"][l], dyo), rms_colsum=want_colsum)
        g = dict(f_norm=outs[1], f_w_in=dw_in, f_conv_w=dcw, f_conv_b=dcb, f_w_down=dw_down)
        return outs[0], g, (outs[2] if want_colsum else None)

    dx3, gr["ffn1"], db_o = ffn_bwd(x3, 1, ffn1, dy, True)
    gr["b_o"] = db_o
    gr["w_o"] = _mm(att, dx3, ta=True, out_dtype=BF16, name="o_proj_dw")
    datt = _mm(dx3, w["w_o"], tb=True, out_dtype=BF16, name="o_proj_dx")
    dqh, dkh, dvh, dsk = _attn_bwd(qh, kh, vh, w["sinks"], datt)
    gr["sinks"] = dsk[:, 0, :4].reshape(1, 16)
    dv = _tokens_major(dvh).astype(BF16)
    dq, db_q, dqn = _hnrope_bwd(q, 0, 1024, w["q_norm_w"], rope, dqh.reshape(16, S, HEAD), name="q_rope_b")
    dk, db_k, dkn = _hnrope_bwd(kv, 0, 256, w["k_norm_w"], rope, dkh, name="k_rope_b")
    gr["q_norm"], gr["k_norm"] = dqn[:, :HEAD], dkn[:, :HEAD]
    gr["b_q"] = db_q
    gr["b_kv"] = jnp.concatenate([db_k, _colsum(dv, name="dv_colsum")], axis=1)
    dkv = jnp.concatenate([dk, dv], axis=1)
    gr["w_q"] = _mm(hq, dq, ta=True, out_dtype=BF16, name="q_proj_dw")
    gr["w_kv"] = _mm(hk, dkv, ta=True, out_dtype=BF16, name="kv_proj_dw")
    tok = comm.advance([gr["w_kv"]], 1, dict(f_down1=gr["ffn1"]["f_w_down"], f_in1=gr["ffn1"]["f_w_in"], w_o=gr["w_o"],
                                             w_q=gr["w_q"], w_kv=gr["w_kv"]))
    tsm = _pick(S, (512, 256))
    dx2, gr["b_norm"] = _mm(dq, w["w_q"], tb=True, name="q_proj_dx", dep=tok, tm=tsm, tn=D, rms=(x2, w["b_norm"], dx3))
    dx2, gr["kv_norm"] = _mm(dkv, w["w_kv"], tb=True, name="kv_proj_dx", tm=tsm, tn=D, rms=(x2, w["kv_norm"], dx2))

    dx1, gr["ffn0"], _ = ffn_bwd(x1, 0, ffn0, dx2, False, dep=comm.advance([dx2]))

    gr["a_out_proj"] = _mm(yg, dx1, ta=True, out_dtype=BF16, name="out_proj_dw")
    tok = comm.advance([dx1, gr["a_out_proj"]], 2,
                       dict(f_down0=gr["ffn0"]["f_w_down"], f_in0=gr["ffn0"]["f_w_in"], out_proj=gr["a_out_proj"]))
    dyg = _mm(dx1, w["a_out_proj"], tb=True, name="out_proj_dx", dep=tok)
    dy_ssd, dz, gr["a_gnorm"], dD = _gate_bwd(y_ssd, xbc, zx, w["a_Dexp"], w["a_gnorm"], dyg)
    gr["a_D"] = dD[:, :SSM_HEADS]
    dxs, dB, dC, dhead = _ssd_bwd(xbc, dt_g, ac_g, ac_t, states, dy_ssd, w["a_Dexp"], dep=comm.advance([dy_ssd]))
    ddtr, dsmall = _ssd_post(dhead, dt_g, sg_g, w["a_A_log_g"])
    gr["a_A_log"] = dsmall[:, 0, :4].reshape(1, SSM_HEADS)
    gr["a_dt_bias"] = dsmall[:, 1, :4].reshape(1, SSM_HEADS)
    dxbc, gr["a_conv_w"], gr["a_conv_b"] = _conv_silu_bwd(
        zx, 2048, 4096, w["a_conv_w"], w["a_conv_b"], [(dxs, 0), (dB, 2048), (dC, 3072)], name="a_conv_b")
    gr["in_proj"] = _in_proj_dw(h1, dz, dxbc, ddtr).T
    tok = comm.advance([dxbc], 3, dict(in_proj=gr["in_proj"].reshape(D, N_CHIPS, -1).transpose(1, 0, 2)))
    ts = _pick(S, (512, 256))
    once = pl.Buffered(1)
    wblk = lambda q: pl.BlockSpec((D, 2048), lambda i, j, k: (0, q), pipeline_mode=once)
    dx0, gr["a_norm"] = _mm(
        dz, w["w_zx"], tb=True, name="in_proj_dx", dims=(S, D, 2048), tm=ts, tn=D, tk=2048,
        a_spec=pl.BlockSpec((ts, 2048), lambda i, j, k: (i, 0)), b_spec=wblk(0),
        more=[(dxbc, pl.BlockSpec((ts, 2048), lambda i, j, k: (i, 0)), w["w_zx"], wblk(1)),
              (dxbc, pl.BlockSpec((ts, 2048), lambda i, j, k: (i, 1)), w["w_zx"], wblk(2)),
              (ddtr, pl.BlockSpec((ts, 128), lambda i, j, k: (i, 0)), w["w_dt"],
               pl.BlockSpec((D, 128), lambda i, j, k: (0, 0), pipeline_mode=once))],
        rms=(x, w["a_norm"], dx1), dep=tok)
    tok = comm.advance([dx0])
    return loss_part, dx0, gr, tok


def _prep_small(full, w):
    w["a_norm"] = full["a_norm"]
    w["a_conv_w"] = full["a_conv_w"][0]
    w["a_conv_b"] = full["a_conv_b"]
    pad32 = lambda v: jnp.pad(v, ((0, 0), (0, 128 - SSM_HEADS)))
    w["a_dt_bias"] = pad32(full["a_dt_bias"])
    w["a_A_log"] = pad32(full["a_A_log"])
    w["a_A_log_g"] = jnp.pad(full["a_A_log"].reshape(SSM_GROUPS, 1, 4), ((0, 0), (0, 0), (0, 124)))
    w["a_Dexp"] = jnp.repeat(full["a_D"], HEAD, axis=1)
    w["a_gnorm"] = full["a_gnorm"]
    w["f_norm"] = [full["f_norm"][l:l + 1] for l in range(2)]
    w["f_conv_w"] = [full["f_conv_w"][l] for l in range(2)]
    w["f_conv_b"] = [full["f_conv_b"][l:l + 1] for l in range(2)]
    w["kv_norm"] = full["kv_norm"].reshape(1, -1)
    w["b_kv"] = full["b_kv"].reshape(1, -1)
    w["k_norm_w"] = jnp.tile(full["k_norm"].reshape(1, HEAD), (1, ATT_KV))
    w["b_norm"] = full["b_norm"]
    w["b_q"] = full["b_q"]
    w["q_norm_w"] = jnp.tile(full["q_norm"], (1, ATT_KV * ATT_G))
    w["sinks"] = full["sinks"].reshape(-1)
    w["b_o"] = full["b_o"]
    return w


def _split_in_proj(ip):
    return ip[:, :6144].astype(BF16), jnp.pad(ip[:, 6144:], ((0, 0), (0, 128 - SSM_HEADS))).astype(BF16)


def _join_in_proj(blocks, *, name="in_proj_join", tr=256):
    _, R, cw = blocks.shape
    zx_cols = 3 * 2048
    rest = N_CHIPS * cw - zx_cols

    def body(b_ref, zx_ref, dt_ref):
        whole = jnp.concatenate([b_ref[j] for j in range(N_CHIPS)], axis=1)
        zx_ref[...] = whole[:, :zx_cols]
        dt_ref[...] = jnp.concatenate([whole[:, zx_cols:], jnp.zeros((tr, 128 - rest), BF16)], axis=1)

    return pl.pallas_call(
        body, name=name, grid=(R // tr,), in_specs=[pl.BlockSpec((N_CHIPS, tr, cw), lambda i: (0, i, 0))],
        out_specs=[pl.BlockSpec((tr, zx_cols), lambda i: (i, 0)), pl.BlockSpec((tr, 128), lambda i: (i, 0))],
        out_shape=[jax.ShapeDtypeStruct((R, zx_cols), BF16), jax.ShapeDtypeStruct((R, 128), BF16)],
        compiler_params=_cp(1),
    )(blocks)


def _in_proj_dw(h, dz, dxbc, ddtr, *, name="in_proj_dw", tn=512):
    S, D = h.shape
    nz, nx = dz.shape[1] // tn, dxbc.shape[1] // tn
    N = dz.shape[1] + dxbc.shape[1] + SSM_HEADS

    def body(h_ref, z_ref, x_ref, t_ref, o_ref):
        j = pl.program_id(0)
        hb = h_ref[...].astype(BF16)

        @pl.when(j < nz)
        def _():
            o_ref[...] = _dot(z_ref[...].astype(BF16), hb, 0, 0).astype(BF16)

        @pl.when((j >= nz) & (j < nz + nx))
        def _():
            o_ref[...] = _dot(x_ref[...].astype(BF16), hb, 0, 0).astype(BF16)

        @pl.when(j == nz + nx)
        def _():
            o_ref[:128, :] = _dot(t_ref[...].astype(BF16), hb, 0, 0).astype(BF16)
            o_ref[128:, :] = jnp.zeros((tn - 128, D), BF16)

    return pl.pallas_call(
        body, name=name, grid=(nz + nx + 1,),
        in_specs=[pl.BlockSpec((S, D), lambda j: (0, 0), pipeline_mode=pl.Buffered(1)),
                  pl.BlockSpec((S, tn), lambda j: (0, jnp.minimum(j, nz - 1))),
                  pl.BlockSpec((S, tn), lambda j: (0, jnp.clip(j - nz, 0, nx - 1))),
                  pl.BlockSpec((S, 128), lambda j: (0, 0))],
        out_specs=pl.BlockSpec((tn, D), lambda j: (j, 0)),
        out_shape=jax.ShapeDtypeStruct((N, D), BF16), compiler_params=_cp(1),
    )(h, dz, dxbc, ddtr)


def _prep_weights(full):
    w = _prep_small(full, {})
    w["w_zx"], w["w_dt"] = _split_in_proj(full["a_in_proj"][0])
    w["a_out_proj"] = full["a_out_proj"][0].astype(BF16)
    w["f_w_in"] = [full["f_w_in"][l].reshape(1024, N_CHIPS, -1).transpose(1, 0, 2).astype(BF16) for l in range(2)]
    w["f_w_down"] = [full["f_w_down"][l].astype(BF16) for l in range(2)]
    w["w_kv"] = full["w_kv"].astype(BF16)
    w["w_q"] = full["w_q"][0].astype(BF16)
    w["w_o"] = full["w_o"][0].astype(BF16)
    return w


def _small_grads(gr):
    g = {}
    g["a_norm"] = gr["a_norm"]
    g["a_conv_w"] = gr["a_conv_w"][None]
    g["a_conv_b"] = gr["a_conv_b"]
    g["a_dt_bias"], g["a_A_log"], g["a_D"] = gr["a_dt_bias"], gr["a_A_log"], gr["a_D"]
    g["a_gnorm"] = gr["a_gnorm"]
    g["kv_norm"] = gr["kv_norm"].reshape(-1)
    g["b_kv"] = gr["b_kv"].reshape(-1)
    g["k_norm"] = gr["k_norm"].reshape(-1)
    g["b_norm"] = gr["b_norm"]
    g["b_q"] = gr["b_q"]
    g["q_norm"] = gr["q_norm"]
    g["sinks"] = gr["sinks"]
    g["b_o"] = gr["b_o"]
    f = [gr["ffn0"], gr["ffn1"]]
    g["f_norm"] = jnp.concatenate([f[0]["f_norm"], f[1]["f_norm"]], axis=0)
    g["f_conv_w"] = jnp.stack([f[l]["f_conv_w"] for l in range(2)])
    g["f_conv_b"] = jnp.concatenate([f[l]["f_conv_b"] for l in range(2)], axis=0)
    return g


def _full_grads(gr):
    g = _small_grads(gr)
    f32 = lambda t: t.astype(F32)
    g["a_in_proj"] = f32(gr["in_proj"])[None]
    g["a_out_proj"] = f32(gr["a_out_proj"])[None]
    g["w_kv"] = f32(gr["w_kv"])
    g["w_q"] = f32(gr["w_q"])[None]
    g["w_o"] = f32(gr["w_o"])[None]
    f = [gr["ffn0"], gr["ffn1"]]
    g["f_w_in"] = jnp.stack([f32(f[l]["f_w_in"]).transpose(1, 0, 2).reshape(1024, -1) for l in range(2)])
    g["f_w_down"] = jnp.stack([f32(f[l]["f_w_down"]) for l in range(2)])
    return g


MESH = pl.DeviceIdType.MESH
WEIGHTS = ("a_norm", "a_in_proj", "a_conv_w", "a_conv_b", "a_dt_bias", "a_A_log", "a_D", "a_gnorm", "a_out_proj",
           "kv_norm", "w_kv", "b_kv", "k_norm", "b_norm", "w_q", "b_q", "q_norm", "sinks", "w_o", "b_o", "f_norm",
           "f_w_in", "f_conv_w", "f_conv_b", "f_w_down")
MATS = (("in_proj", "a_in_proj", 0), ("out_proj", "a_out_proj", 0), ("w_kv", "w_kv", None), ("w_q", "w_q", 0),
        ("w_o", "w_o", 0), ("f_in0", "f_w_in", 0), ("f_in1", "f_w_in", 1), ("f_down0", "f_w_down", 0),
        ("f_down1", "f_w_down", 1))
SMALL_CUT = (("a_norm", 1), ("a_conv_w", 2), ("a_conv_b", 1), ("a_gnorm", 1), ("f_conv_w", 2))
SMALL_REP = ("a_dt_bias", "a_A_log", "a_D", "kv_norm", "b_kv", "k_norm", "b_norm", "b_q", "q_norm", "sinks", "b_o",
             "f_norm", "f_conv_b")


def _coords():
    return lax.axis_index("x"), lax.axis_index("y"), lax.axis_index("c")


def _other_chips(x, y):
    return [(1 - x, y), (x, 1 - y), (1 - x, 1 - y)]


def _pack(arrs, rows_align, lanes, dtype):
    flat = jnp.concatenate([a.reshape(-1).astype(dtype) for a in arrs])
    per = rows_align * lanes
    total = -(-flat.shape[0] // per) * per
    return jnp.pad(flat, (0, total - flat.shape[0])).reshape(total // lanes, lanes)


def _unpack(flat, shapes):
    out, off = [], 0
    for s in shapes:
        n = math.prod(s)
        out.append(flat[off:off + n].reshape(s))
        off += n
    return out


def _remote(src, dst, send, recv, k, dev):
    return pltpu.make_async_remote_copy(src_ref=src, dst_ref=dst, send_sem=send.at[k], recv_sem=recv.at[k],
                                        device_id=dev, device_id_type=MESH)


_ANY = pl.BlockSpec(memory_space=pl.ANY)


def _halves(t):
    r, c = t.shape
    return t.reshape(2, r // 2, c)


RING_SEMS = 9


def _ring_first_copies(refs, send, recv):
    n = (len(refs) - 2) // 2
    sh, sp_ref, outs, sout = refs[:n], refs[n], refs[n + 1:2 * n + 1], refs[2 * n + 1]
    per = RING_SEMS
    x, y, c = _coords()
    me = 2 * x + y
    cx_, cy_, cd_ = _other_chips(x, y)
    sib = (x, y, 1 - c)
    out = [(_remote(sp_ref, sout.at[me], send, recv, per * n + j, (*p, c)),
            _remote(sp_ref, sout.at[2 * p[0] + p[1]], send, recv, per * n + j, (*p, c))) for j, p in enumerate((cx_, cy_, cd_))]
    for t in range(n):
        for k, p in enumerate((cx_, cy_)):
            out.append((_remote(sh[t].at[c], outs[t].at[me, c], send, recv, per * t + k, (*p, c)),
                        _remote(sh[t].at[c], outs[t].at[2 * p[0] + p[1], c], send, recv, per * t + k, (*p, c))))
        own = _remote(sh[t], outs[t].at[me], send, recv, per * t + 8, sib)
        out.append((own, own))
    return out


def _ring_rest(sh, sp_ref, outs, sout, send, recv, loc):
    n, per = len(sh), RING_SEMS
    x, y, c = _coords()
    me = 2 * x + y
    cx_, cy_, cd_ = _other_chips(x, y)
    ix, iy, idg = (2 * p[0] + p[1] for p in (cx_, cy_, cd_))
    to_x, to_y, sib = (*cx_, c), (*cy_, c), (x, y, 1 - c)
    own_small = pltpu.make_async_copy(sp_ref, sout.at[me], loc.at[0])
    own_small.start()
    sends = []

    def go(src, dst, k, dev):
        cp = _remote(src, dst, send, recv, k, dev)
        cp.start()
        sends.append(cp)

    def piece(t, owner, first):
        q = sh[t].shape[1] // 2
        return outs[t].at[owner, c, pl.ds(0 if first else q, q)]

    for t in range(n):
        go(piece(t, ix, False), piece(t, ix, False), per * t + 3, to_y)
        go(outs[t].at[ix, c], outs[t].at[ix, c], per * t + 4, sib)
    for t in range(n):
        go(piece(t, iy, True), piece(t, iy, True), per * t + 2, to_x)
        go(outs[t].at[iy, c], outs[t].at[iy, c], per * t + 5, sib)
    for t in range(n):
        _remote(piece(t, idg, True), piece(t, idg, True), send, recv, per * t + 2, to_x).wait_recv()
        go(piece(t, idg, True), piece(t, idg, True), per * t + 6, sib)
        _remote(piece(t, idg, False), piece(t, idg, False), send, recv, per * t + 3, to_y).wait_recv()
        go(piece(t, idg, False), piece(t, idg, False), per * t + 7, sib)
    for t in range(n):
        q = sh[t].shape[1] // 2
        other = lambda owner, lo=None: outs[t].at[owner, 1 - c] if lo is None else outs[t].at[owner, 1 - c, pl.ds(lo, q)]
        _remote(other(ix), other(ix), send, recv, per * t + 4, sib).wait_recv()
        _remote(other(iy), other(iy), send, recv, per * t + 5, sib).wait_recv()
        _remote(other(idg, 0), other(idg, 0), send, recv, per * t + 6, sib).wait_recv()
        _remote(other(idg, q), other(idg, q), send, recv, per * t + 7, sib).wait_recv()
    for cp in sends:
        cp.wait_send()
    own_small.wait()


def _gather_weights_forward(shards, sp, lands, sland):
    n = len(shards)

    def body(*refs):
        sh, sp_ref = refs[:n], refs[n]
        outs, sout = refs[2 * n + 2:3 * n + 2], refs[3 * n + 2]
        _ring_rest(sh, sp_ref, outs, sout, *refs[3 * n + 3:])

    res = pl.pallas_call(
        body, name="gather_weights_forward", in_specs=[_ANY] * (2 * n + 2), out_specs=[_ANY] * (n + 1),
        out_shape=[jax.ShapeDtypeStruct(a.shape, a.dtype) for a in (*lands, sland)],
        input_output_aliases={n + 1 + t: t for t in range(n + 1)},
        scratch_shapes=[pltpu.SemaphoreType.DMA((RING_SEMS * n,)), pltpu.SemaphoreType.DMA((RING_SEMS * n,)),
                        pltpu.SemaphoreType.DMA((1,))],
    )(*shards, sp, *lands, sland)
    return res[:n], res[n]


_HBM = pl.BlockSpec(memory_space=pltpu.HBM)
_SEMS = pl.BlockSpec(memory_space=pltpu.SEMAPHORE)
_DATAFLOW = pltpu.SideEffectType.DATAFLOW_SIDE_EFFECTING


def _in_hbm(a):
    return pltpu.with_memory_space_constraint(a, pltpu.HBM)


def _start_copies(copies, arrays, n_sem, after, *, name):
    n, na = len(arrays), len(after)

    def body(*refs):
        for mine, _ in copies(refs[:n], refs[n + na], refs[n + na + 1]):
            mine.start()
        refs[-1][...] = jnp.zeros_like(refs[-1])

    res = pl.pallas_call(
        body, name=name, in_specs=[_HBM] * n + [_ANY] * na,
        out_specs=[_SEMS, _SEMS] + [_HBM] * n + [pl.BlockSpec(memory_space=pltpu.VMEM)],
        out_shape=[pltpu.SemaphoreType.DMA((n_sem,)), pltpu.SemaphoreType.DMA((n_sem,))]
        + [pltpu.HBM(a.shape, a.dtype) for a in arrays] + [jax.ShapeDtypeStruct((8, 128), F32)],
        input_output_aliases={t: 2 + t for t in range(n)},
        compiler_params=pltpu.CompilerParams(has_side_effects=_DATAFLOW),
    )(*[_in_hbm(a) for a in arrays], *after)
    return res[0], res[1], list(res[2:2 + n]), res[-1]


def _wait_copies(copies, send, recv, arrays, after, *, name):
    n = len(arrays)

    def body(*refs):
        for mine, theirs in copies(refs[:n], refs[n], refs[n + 1]):
            mine.wait_send()
            theirs.wait_recv()

    return list(pl.pallas_call(
        body, name=name, in_specs=[_HBM] * n + [_SEMS, _SEMS] + [_ANY] * len(after), out_specs=[_HBM] * n,
        out_shape=[pltpu.HBM(a.shape, a.dtype) for a in arrays], input_output_aliases={t: t for t in range(n)},
        compiler_params=pltpu.CompilerParams(has_side_effects=_DATAFLOW),
    )(*arrays, send, recv, *after))


def _sibling_copies(refs, send, recv):
    n = len(refs) // 2
    x, y, c = _coords()
    cps = [_remote(refs[t].at[:, 1 - c], refs[n + t], send, recv, t, (x, y, 1 - c)) for t in range(n)]
    return [(cp, cp) for cp in cps]


def _join_copies(refs, send, recv):
    x, y, c = _coords()
    sib = (x, y, 1 - c)
    return [(_remote(o.at[c], o.at[c], send, recv, t, sib), _remote(o.at[1 - c], o.at[1 - c], send, recv, t, sib))
            for t, o in enumerate(refs)]


def _gather_copies(sh, land, send, recv):
    x, y, c = _coords()
    me = 2 * x + y
    out = []
    for t in range(len(sh)):
        for j, (cx, cy) in enumerate(_other_chips(x, y)):
            dev = (cx, cy, c)
            out.append((_remote(sh[t].at[c], land[t].at[me, c], send, recv, 4 * t + j, dev),
                        _remote(sh[t].at[c], land[t].at[2 * cx + cy, c], send, recv, 4 * t + j, dev)))
        sib = (x, y, 1 - c)
        out.append((_remote(sh[t], land[t].at[me], send, recv, 4 * t + 3, sib),
                    _remote(sh[t], land[t].at[me], send, recv, 4 * t + 3, sib)))
    return out


def _gather_start(shards, after, *, name):
    n = len(shards)

    def body(*refs):
        sh, land = refs[:n], refs[n:2 * n]
        send, recv = refs[2 * n + 1], refs[2 * n + 2]
        token = refs[-1]
        for mine, _ in _gather_copies(sh, land, send, recv):
            mine.start()
        token[...] = jnp.zeros_like(token)

    lands = [_in_hbm(lax.empty((N_CHIPS,) + s.shape, s.dtype)) for s in shards]
    res = pl.pallas_call(
        body, name=name, in_specs=[_HBM] * (2 * n) + [_ANY],
        out_specs=[_SEMS, _SEMS] + [_HBM] * (2 * n) + [pl.BlockSpec(memory_space=pltpu.VMEM)],
        out_shape=[pltpu.SemaphoreType.DMA((4 * n,)), pltpu.SemaphoreType.DMA((4 * n,))]
        + [pltpu.HBM(s.shape, s.dtype) for s in shards] + [pltpu.HBM(l.shape, l.dtype) for l in lands]
        + [jax.ShapeDtypeStruct((8, 128), F32)],
        input_output_aliases={t: 2 + t for t in range(2 * n)},
        compiler_params=pltpu.CompilerParams(has_side_effects=_DATAFLOW),
    )(*[_in_hbm(s) for s in shards], *lands, after)
    return res[0], res[1], res[2:2 + n], res[2 + n:2 + 2 * n], res[-1]


def _gather_wait(send, recv, shards, lands, after, *, name):
    n = len(shards)

    def body(*refs):
        sh, land = refs[:n], refs[n:2 * n]
        send_r, recv_r = refs[2 * n], refs[2 * n + 1]
        for mine, theirs in _gather_copies(sh, land, send_r, recv_r):
            mine.wait_send()
            theirs.wait_recv()

    res = pl.pallas_call(
        body, name=name, in_specs=[_HBM] * (2 * n) + [_SEMS, _SEMS, _ANY], out_specs=[_HBM] * (2 * n),
        out_shape=[pltpu.HBM(s.shape, s.dtype) for s in shards] + [pltpu.HBM(l.shape, l.dtype) for l in lands],
        input_output_aliases={t: t for t in range(2 * n)},
        compiler_params=pltpu.CompilerParams(has_side_effects=_DATAFLOW),
    )(*shards, *lands, send, recv, after)
    return res[n:]


def _forward_copies(refs, send, recv):
    x, y, c = _coords()
    sib = (x, y, 1 - c)
    srcs = [2 * cx + cy for cx, cy in _other_chips(x, y)]
    return [(_remote(o.at[s, c], o.at[s, c], send, recv, 3 * t + j, sib),
             _remote(o.at[s, 1 - c], o.at[s, 1 - c], send, recv, 3 * t + j, sib))
            for t, o in enumerate(refs) for j, s in enumerate(srcs)]


def _small_copies(v, land, send, recv):
    x, y, c = _coords()
    me = 4 * x + 2 * y + c
    out = []
    for k in range(1, 8):
        px = 1 - x if k & 4 else x
        py = 1 - y if k & 2 else y
        pc = 1 - c if k & 1 else c
        out.append((_remote(v, land.at[me], send, recv, k - 1, (px, py, pc)),
                    _remote(v, land.at[4 * px + 2 * py + pc], send, recv, k - 1, (px, py, pc))))
    return out


def _small_start(v, after, *, name):
    def body(v_ref, land_ref, after_ref, send, recv, v_thru, land_thru, token):
        for mine, _ in _small_copies(v_ref, land_ref, send, recv):
            mine.start()
        token[...] = jnp.zeros_like(token)

    land = _in_hbm(lax.empty((8,) + v.shape, v.dtype))
    return pl.pallas_call(
        body, name=name, in_specs=[_HBM, _HBM, _ANY],
        out_specs=[_SEMS, _SEMS, _HBM, _HBM, pl.BlockSpec(memory_space=pltpu.VMEM)],
        out_shape=[pltpu.SemaphoreType.DMA((7,)), pltpu.SemaphoreType.DMA((7,)), pltpu.HBM(v.shape, v.dtype),
                   pltpu.HBM(land.shape, land.dtype), jax.ShapeDtypeStruct((8, 128), F32)],
        input_output_aliases={0: 2, 1: 3}, compiler_params=pltpu.CompilerParams(has_side_effects=_DATAFLOW),
    )(_in_hbm(v), land, after)


def _small_wait(send, recv, v, land, after, *, name):
    def body(v_ref, land_ref, send_r, recv_r, *rest):
        for mine, theirs in _small_copies(v_ref, land_ref, send_r, recv_r):
            mine.wait_send()
            theirs.wait_recv()

    return pl.pallas_call(
        body, name=name, in_specs=[_HBM, _HBM, _SEMS, _SEMS] + [_ANY] * len(after), out_specs=[_HBM, _HBM],
        out_shape=[pltpu.HBM(v.shape, v.dtype), pltpu.HBM(land.shape, land.dtype)],
        input_output_aliases={0: 0, 1: 1}, compiler_params=pltpu.CompilerParams(has_side_effects=_DATAFLOW),
    )(v, land, send, recv, *after)


def _small_sum(v, land, me_idx, *, name="small_sum"):
    def body(me_ref, v_ref, land_ref, o_ref):
        acc = None
        for s in range(8):
            term = jnp.where(me_ref[0] == s, v_ref[...], land_ref[s])
            acc = term if acc is None else acc + term
        o_ref[...] = acc

    whole = lambda shape: pl.BlockSpec(shape, lambda i, me_ref: (0,) * len(shape))
    return pl.pallas_call(
        body, name=name,
        grid_spec=pltpu.PrefetchScalarGridSpec(num_scalar_prefetch=1, grid=(1,), in_specs=[whole(v.shape), whole(land.shape)],
                                               out_specs=whole(v.shape)),
        out_shape=jax.ShapeDtypeStruct(v.shape, F32), compiler_params=_cp(1),
    )(me_idx, v, land)


RS_ROW_SPLIT = 2


def _rs_add_pair(gs, as_, c_idx, *, name):
    n = len(gs)

    def body(c_ref, *refs):
        for t in range(n):
            refs[2 * n + t][...] = (refs[t][...].astype(F32) + refs[n + t][...].astype(F32)).astype(BF16)

    def gspec(g):
        _, _, rh, cols = g.shape
        return pl.BlockSpec((None, None, rh // RS_ROW_SPLIT, cols), lambda j, i, c_ref: (j, c_ref[0], i, 0))

    def pspec(g):
        _, _, rh, cols = g.shape
        return pl.BlockSpec((None, rh // RS_ROW_SPLIT, cols), lambda j, i, c_ref: (j, i, 0))

    return pl.pallas_call(
        body, name=name,
        grid_spec=pltpu.PrefetchScalarGridSpec(
            num_scalar_prefetch=1, grid=(N_CHIPS, RS_ROW_SPLIT),
            in_specs=[gspec(g) for g in gs] + [pspec(g) for g in gs], out_specs=[pspec(g) for g in gs]),
        out_shape=[jax.ShapeDtypeStruct((N_CHIPS,) + g.shape[2:], BF16) for g in gs], compiler_params=_cp(2),
    )(c_idx, *gs, *as_)


def _chips_copies(p, r, send, recv):
    x, y, c = _coords()
    return [_remote(p[t].at[2 * cx + cy], r[t].at[k], send, recv, 3 * t + k, (cx, cy, c))
            for k, (cx, cy) in enumerate(_other_chips(x, y)) for t in range(len(p))]


def _rs_chips_start(ps, after, *, name):
    n, na = len(ps), len(after)

    def body(*refs):
        p, r = refs[:n], refs[n:2 * n]
        send, recv = refs[2 * n + na], refs[2 * n + na + 1]
        token = refs[-1]
        for cp in _chips_copies(p, r, send, recv):
            cp.start()
        token[...] = jnp.zeros_like(token)

    lands = [_in_hbm(lax.empty((3,) + p.shape[1:], p.dtype)) for p in ps]
    res = pl.pallas_call(
        body, name=name, in_specs=[_HBM] * (2 * n) + [_ANY] * na,
        out_specs=[_SEMS, _SEMS] + [_HBM] * (2 * n) + [pl.BlockSpec(memory_space=pltpu.VMEM)],
        out_shape=[pltpu.SemaphoreType.DMA((3 * n,)), pltpu.SemaphoreType.DMA((3 * n,))]
        + [pltpu.HBM(p.shape, p.dtype) for p in ps] + [pltpu.HBM(l.shape, l.dtype) for l in lands]
        + [jax.ShapeDtypeStruct((8, 128), F32)],
        input_output_aliases={t: 2 + t for t in range(2 * n)},
        compiler_params=pltpu.CompilerParams(has_side_effects=_DATAFLOW),
    )(*[_in_hbm(p) for p in ps], *lands, *after)
    return res[0], res[1], res[2:2 + n], res[2 + n:2 + 2 * n], res[-1]


def _rs_chips_wait(send, recv, ps, lands, after, *, name):
    n = len(ps)

    def body(*refs):
        p, r = refs[:n], refs[n:2 * n]
        for cp in _chips_copies(p, r, refs[2 * n], refs[2 * n + 1]):
            cp.wait_send()
            cp.wait_recv()

    res = pl.pallas_call(
        body, name=name, in_specs=[_HBM] * (2 * n) + [_SEMS, _SEMS] + [_ANY] * len(after), out_specs=[_HBM] * (2 * n),
        out_shape=[pltpu.HBM(p.shape, p.dtype) for p in ps] + [pltpu.HBM(l.shape, l.dtype) for l in lands],
        input_output_aliases={t: t for t in range(2 * n)},
        compiler_params=pltpu.CompilerParams(has_side_effects=_DATAFLOW),
    )(*ps, *lands, send, recv, *after)
    return res[:n], res[n:]


def _rs_add_chips(ps, rs, idx, *, name):
    n = len(ps)

    def body(idx_ref, *refs):
        for t in range(n):
            p_ref, r0, r1, r2 = refs[4 * t:4 * t + 4]
            refs[4 * n + t][...] = ((p_ref[...].astype(F32) + r0[...].astype(F32)) + r1[...].astype(F32)) + r2[...].astype(F32)

    in_specs, args = [], []
    for p, r in zip(ps, rs):
        _, rh, cols = p.shape
        blk = (None, rh // RS_ROW_SPLIT, cols)
        in_specs.append(pl.BlockSpec(blk, lambda i, idx_ref: (idx_ref[0], i, 0)))
        in_specs += [pl.BlockSpec(blk, lambda i, idx_ref, k=k: (k, i, 0)) for k in range(3)]
        args += [p, r, r, r]
    out_specs = [pl.BlockSpec((None, p.shape[1] // RS_ROW_SPLIT, p.shape[2]), lambda i, idx_ref: (idx_ref[1], i, 0))
                 for p in ps]
    return pl.pallas_call(
        body, name=name,
        grid_spec=pltpu.PrefetchScalarGridSpec(num_scalar_prefetch=1, grid=(RS_ROW_SPLIT,), in_specs=in_specs,
                                               out_specs=out_specs),
        out_shape=[jax.ShapeDtypeStruct((2,) + p.shape[1:], F32) for p in ps], compiler_params=_cp(1),
    )(idx, *args)


def _adamw(w, gs, m, v, *, name, dep=None):
    L, Rr, C = w.shape
    tr, tc = _pick(Rr, (256, 128, 64)), C
    if tr == Rr and Rr * C > 512 * 1024:
        tc = 256
    bc1 = 1.0 - ADAM_B1 ** ADAM_STEP
    bc2 = 1.0 - ADAM_B2 ** ADAM_STEP
    nd = 0 if dep is None else 1

    def body(*refs):
        w_ref, m_ref, v_ref = refs[0], refs[1], refs[2]
        g_refs = refs[3:3 + L]
        d_ref, mo_ref, vo_ref, go_ref = refs[3 + L + nd:]
        layer = pl.program_id(0)
        gv = g_refs[0][...]
        for q in range(1, L):
            gv = jnp.where(layer == q, g_refs[q][...], gv)
        mn = ADAM_B1 * m_ref[...] + (1.0 - ADAM_B1) * gv
        vn = ADAM_B2 * v_ref[...] + (1.0 - ADAM_B2) * (gv * gv)
        go_ref[...] = gv
        mo_ref[...] = mn
        vo_ref[...] = vn
        d_ref[...] = -ADAM_LR * ((mn / bc1) / (jnp.sqrt(vn / bc2) + ADAM_EPS) + ADAM_WD * w_ref[...])

    blk = pl.BlockSpec((None, tr, tc), lambda l, i, j: (l, i, j))
    gblks = [pl.BlockSpec((tr, tc), lambda l, i, j, q=q: (jnp.where(l == q, i, 0), jnp.where(l == q, j, 0))) for q in range(L)]
    return pl.pallas_call(
        body, name=name, grid=(L, Rr // tr, C // tc), in_specs=[blk] * 3 + gblks + [_ANY] * nd, out_specs=[blk] * 4,
        out_shape=[jax.ShapeDtypeStruct((L, Rr, C), F32)] * 4, compiler_params=_cp(3),
    )(w, m, v, *gs, *([] if dep is None else [dep]))


def _adamw_leaves(ws, gs, ms, vs, *, name):
    n = len(ws)
    bc1 = 1.0 - ADAM_B1 ** ADAM_STEP
    bc2 = 1.0 - ADAM_B2 ** ADAM_STEP

    def body(*refs):
        w_refs, g_refs, m_refs, v_refs, d_refs, mo_refs, vo_refs = (refs[q * n:(q + 1) * n] for q in range(7))
        for k in range(n):
            gv = g_refs[k][...]
            mn = ADAM_B1 * m_refs[k][...] + (1.0 - ADAM_B1) * gv
            vn = ADAM_B2 * v_refs[k][...] + (1.0 - ADAM_B2) * (gv * gv)
            mo_refs[k][...] = mn
            vo_refs[k][...] = vn
            d_refs[k][...] = -ADAM_LR * ((mn / bc1) / (jnp.sqrt(vn / bc2) + ADAM_EPS) + ADAM_WD * w_refs[k][...])

    in_vmem = pl.BlockSpec(memory_space=pltpu.VMEM)
    outs = pl.pallas_call(
        body, name=name, in_specs=[in_vmem] * (4 * n), out_specs=[in_vmem] * (3 * n),
        out_shape=[jax.ShapeDtypeStruct(t.shape, F32) for t in ws] * 3,
    )(*ws, *gs, *ms, *vs)
    return outs[:n], outs[n:2 * n], outs[2 * n:]


def kernel(x, positions, a_norm, a_in_proj, a_conv_w, a_conv_b, a_dt_bias, a_A_log, a_D, a_gnorm, a_out_proj,
           kv_norm, w_kv, b_kv, k_norm, b_norm, w_q, b_q, q_norm, sinks, w_o, b_o, f_norm, f_w_in, f_conv_w,
           f_conv_b, f_w_down, loss_target, m_a_norm, m_a_in_proj, m_a_conv_w, m_a_conv_b, m_a_dt_bias, m_a_A_log,
           m_a_D, m_a_gnorm, m_a_out_proj, m_kv_norm, m_w_kv, m_b_kv, m_k_norm, m_b_norm, m_w_q, m_b_q, m_q_norm,
           m_sinks, m_w_o, m_b_o, m_f_norm, m_f_w_in, m_f_conv_w, m_f_conv_b, m_f_w_down, v_a_norm, v_a_in_proj,
           v_a_conv_w, v_a_conv_b, v_a_dt_bias, v_a_A_log, v_a_D, v_a_gnorm, v_a_out_proj, v_kv_norm, v_w_kv,
           v_b_kv, v_k_norm, v_b_norm, v_w_q, v_b_q, v_q_norm, v_sinks, v_w_o, v_b_o, v_f_norm, v_f_w_in,
           v_f_conv_w, v_f_conv_b, v_f_w_down):
    wl = dict(zip(WEIGHTS, (a_norm, a_in_proj, a_conv_w, a_conv_b, a_dt_bias, a_A_log, a_D, a_gnorm, a_out_proj,
                            kv_norm, w_kv, b_kv, k_norm, b_norm, w_q, b_q, q_norm, sinks, w_o, b_o, f_norm, f_w_in,
                            f_conv_w, f_conv_b, f_w_down)))
    ml = dict(zip(WEIGHTS, (m_a_norm, m_a_in_proj, m_a_conv_w, m_a_conv_b, m_a_dt_bias, m_a_A_log, m_a_D, m_a_gnorm,
                            m_a_out_proj, m_kv_norm, m_w_kv, m_b_kv, m_k_norm, m_b_norm, m_w_q, m_b_q, m_q_norm,
                            m_sinks, m_w_o, m_b_o, m_f_norm, m_f_w_in, m_f_conv_w, m_f_conv_b, m_f_w_down)))
    vl = dict(zip(WEIGHTS, (v_a_norm, v_a_in_proj, v_a_conv_w, v_a_conv_b, v_a_dt_bias, v_a_A_log, v_a_D, v_a_gnorm,
                            v_a_out_proj, v_kv_norm, v_w_kv, v_b_kv, v_k_norm, v_b_norm, v_w_q, v_b_q, v_q_norm,
                            v_sinks, v_w_o, v_b_o, v_f_norm, v_f_w_in, v_f_conv_w, v_f_conv_b, v_f_w_down)))
    xi, yi, ci = _coords()
    me = 2 * xi + yi
    S = x.shape[1]

    def shard(name, src):
        _, wn, layer = next(m for m in MATS if m[0] == name)
        return _halves((src[wn] if layer is None else src[wn][layer]).astype(BF16))

    rows = lambda t: t.reshape(-1, t.shape[-1])
    c_idx = jnp.reshape(ci, (1,)).astype(jnp.int32)
    me_c = jnp.stack([me, ci]).astype(jnp.int32)
    early = ("in_proj",)
    late = (("out_proj", "f_in0", "f_down0"), ("w_kv", "w_q", "w_o", "f_in1", "f_down1"))

    sp = _pack([wl[n] for n, _ in SMALL_CUT], 8, 128, F32)
    e_sh = [shard(k, wl) for k in early]
    ring = e_sh + [sp] + [lax.empty((N_CHIPS,) + t.shape, t.dtype) for t in e_sh + [sp]]
    g_send, g_recv, ring, g_tok = _start_copies(_ring_first_copies, ring, RING_SEMS * len(e_sh) + 3, [],
                                                name="gather_weights_start")
    anchor = 0.0 * g_tok[0, 0]
    tied = {wn: wl[wn] + anchor for wn in sorted({wn for name, wn, _ in MATS if name not in early})}
    posf = positions.reshape(S, 1).astype(F32) + anchor
    shards = {k: shard(k, tied) for part in late for k in part}
    rope = _rope_cs(posf)
    ring = _wait_copies(_ring_first_copies, g_send, g_recv, ring, [*rope, *shards.values()], name="gather_weights_arrived")
    ne = len(e_sh)
    gathered, gs = _gather_weights_forward(ring[:ne], ring[ne], ring[ne + 1:2 * ne + 1], ring[2 * ne + 1])
    gt = {k: t.reshape(N_CHIPS, -1, t.shape[-1]) for k, t in zip(early, gathered)}
    started = {0: _gather_start([shards[k] for k in late[0]], gs, name="gather_late_start0")}
    full = {n: wl[n] for n in SMALL_REP}
    gs = gs.reshape(N_CHIPS, -1)
    off = 0
    for n, ax in SMALL_CUT:
        shp = wl[n].shape
        size = math.prod(shp)
        piece = jnp.moveaxis(gs[:, off:off + size].reshape((N_CHIPS,) + shp), 0, ax)
        full[n] = piece.reshape(shp[:ax] + (N_CHIPS * shp[ax],) + shp[ax + 1:])
        off += size
    w = _prep_small(full, {})
    w["w_zx"], w["w_dt"] = _join_in_proj(gt["in_proj"])
    w["dep"] = started[0][4]
    w["rope"] = rope

    class Comm:
        flight = []
        reduced = {}

        forwarding = {}

        def late_start(self, part, after):
            started[part] = _gather_start([shards[k] for k in late[part]], after, name=f"gather_late_start{part}")
            return started[part][4]

        def late_arrived(self, part, after):
            send, recv, shs, lands, _ = started[part]
            lands = _gather_wait(send, recv, shs, lands, after[0], name=f"gather_late_wait{part}")
            send, recv, lands, token = _start_copies(_forward_copies, list(lands), 3 * len(lands), after,
                                                     name=f"gather_late_forward_start{part}")
            self.forwarding[part] = (send, recv, lands)
            return token

        def late_weights(self, w, after, part):
            send, recv, lands = self.forwarding[part]
            lands = _wait_copies(_forward_copies, send, recv, lands, [after], name=f"gather_late_forward_wait{part}")
            lt = {k: t.reshape(N_CHIPS, -1, t.shape[-1]) for k, t in zip(late[part], lands)}
            w = dict(w)
            if part == 0:
                w["a_out_proj"], w["f_w_in"], w["f_w_down"] = rows(lt["out_proj"]), [lt["f_in0"]], [rows(lt["f_down0"])]
            else:
                w["w_kv"], w["w_q"], w["w_o"] = (rows(lt[k]) for k in ("w_kv", "w_q", "w_o"))
                w["f_w_in"], w["f_w_down"] = w["f_w_in"] + [lt["f_in1"]], w["f_w_down"] + [rows(lt["f_down1"])]
            return w

        def advance(self, after, group=None, tensors=None):
            token = None
            for grp in list(self.flight):
                tag, n = grp["tag"], len(grp["names"])
                dep = list(after) + ([] if token is None else [token])
                if grp["stage"] == "sibling":
                    arrs = _wait_copies(_sibling_copies, grp["send"], grp["recv"], grp["arrays"], dep, name=f"rs_sibling_wait{tag}")
                    pairs = _rs_add_pair(arrs[:n], arrs[n:], c_idx, name=f"rs_add_pair{tag}")
                    send, recv, ps, lands, token = _rs_chips_start(pairs, dep, name=f"rs_chips_start{tag}")
                    grp.update(stage="chips", send=send, recv=recv, ps=ps, lands=lands)
                elif grp["stage"] == "chips":
                    ps, rs = _rs_chips_wait(grp["send"], grp["recv"], grp["ps"], grp["lands"], dep, name=f"rs_chips_wait{tag}")
                    halves = _rs_add_chips(ps, rs, me_c, name=f"rs_add_chips{tag}")
                    send, recv, arrs, token = _start_copies(_join_copies, halves, n, dep, name=f"rs_join_start{tag}")
                    grp.update(stage="join", send=send, recv=recv, arrays=arrs)
                else:
                    joined = _wait_copies(_join_copies, grp["send"], grp["recv"], grp["arrays"], dep, name=f"rs_join_wait{tag}")
                    self.reduced.update({k: rows(t) for k, t in zip(grp["names"], joined)})
                    self.flight.remove(grp)
            if group is not None:
                names = list(tensors)
                glist = [tensors[k].reshape(N_CHIPS, 2, -1, tensors[k].shape[-1]) for k in names]
                lands = [lax.empty((N_CHIPS,) + gq.shape[2:], gq.dtype) for gq in glist]
                dep = [a for a in after if not any(a is t for t in tensors.values())] + ([] if token is None else [token])
                send, recv, arrs, token = _start_copies(_sibling_copies, glist + lands, len(names), dep,
                                                        name=f"rs_sibling_start{group}")
                self.flight.append(dict(tag=group, names=names, stage="sibling", send=send, recv=recv, arrays=arrs))
            return token

    comm = Comm()

    loss_part, dx0, gr, tok = _local_step(x[0], posf, loss_target[0], w, comm)
    g = _small_grads(gr)

    small_names = [n for n, _ in SMALL_CUT] + list(SMALL_REP)
    sv = _pack([g[n] for n in small_names] + [loss_part[0:1, 0:1]], 8, 128, F32)
    s_send, s_recv, sv, s_land, s_token = _small_start(sv, tok, name="small_start")

    grads, delta, new_m, new_v = {}, {}, {}, {}

    def update(wn, dep):
        gl = [comm.reduced[name] for name, n2, _ in MATS if n2 == wn]
        shp = wl[wn].shape
        three = (len(gl),) + gl[0].shape
        flip = shp[-1] % 128 != 0
        view = (lambda t: t.reshape(three).transpose(0, 2, 1)) if flip else (lambda t: t.reshape(three))
        back = (lambda t: t.transpose(0, 2, 1).reshape(shp)) if flip else (lambda t: t.reshape(shp))
        if flip:
            gl = [t.T for t in gl]
        d, mn, vn, go = _adamw(view(wl[wn]), gl, view(ml[wn]), view(vl[wn]), name="adamw_" + wn, dep=dep)
        grads[wn], delta[wn], new_m[wn], new_v[wn] = back(go), back(d), back(mn), back(vn)
        return d

    first = [update(wn, s_token) for wn in ("w_q", "w_o", "w_kv")]
    second = [update(wn, first[-1]) for wn in ("f_w_in", "f_w_down", "a_out_proj")]
    tok = comm.advance(first + second)
    done = first + second + [tok]

    sv, s_land = _small_wait(s_send, s_recv, sv, s_land, done, name="small_wait")
    sred = _small_sum(sv, s_land, jnp.reshape(2 * me + ci, (1,)).astype(jnp.int32)).reshape(-1)
    small_shapes = [g[n].shape for n in small_names] + [(1,)]
    sg = dict(zip(small_names + ["loss"], _unpack(sred, small_shapes)))
    loss = sg["loss"].reshape(())
    g_small = {}
    for n, ax in SMALL_CUT:
        size = wl[n].shape[ax]
        g_small[n] = lax.dynamic_slice_in_dim(sg[n], me * size, size, axis=ax)
    for n in SMALL_REP:
        g_small[n] = sg[n].reshape(wl[n].shape)

    leaves = lambda d: [d[n].reshape(1, -1) if d[n].ndim == 1 else d[n] for n in small_names]
    ds, mns, vns = _adamw_leaves(leaves(wl), leaves(g_small), leaves(ml), leaves(vl), name="adamw_small")
    comm.advance([ds[0]])
    update("a_in_proj", None)
    for n, dd, mm, vv in zip(small_names, ds, mns, vns):
        shp = wl[n].shape
        grads[n], delta[n], new_m[n], new_v[n] = g_small[n], dd.reshape(shp), mm.reshape(shp), vv.reshape(shp)

    return (loss, dx0[None], *[grads[n] for n in WEIGHTS], *[delta[n] for n in WEIGHTS],
            *[new_m[n] for n in WEIGHTS], *[new_v[n] for n in WEIGHTS])
```

```python
import math

import jax
import jax.numpy as jnp
from jax import lax
from jax.experimental import pallas as pl
from jax.experimental.pallas import tpu as pltpu

F32 = jnp.float32
BF16 = jnp.bfloat16

EPS = 1e-5
CHUNK = 256
WINDOW = 128
HEAD = 64
SSM_HEADS = 32
SSM_GROUPS = 8
SSM_STATE = 128
ATT_KV = 4
ATT_G = 4
ROPE_THETA = 10000.0
NEG = -1e30
N_CHIPS = 4
VMEM_LIMIT = 56 * 1024 * 1024

ADAM_LR, ADAM_B1, ADAM_B2, ADAM_EPS, ADAM_WD, ADAM_STEP = 0.001, 0.9, 0.999, 1e-08, 0.01, 10


def _cp(n_axes):
    return pltpu.CompilerParams(dimension_semantics=("arbitrary",) * n_axes, vmem_limit_bytes=VMEM_LIMIT)


def _pick(dim, prefs):
    for p in prefs:
        if dim % p == 0:
            return p
    return dim


def _iota(shape, dim):
    return lax.broadcasted_iota(jnp.int32, shape, dim)


def _dot(a, b, ca=1, cb=0):
    return lax.dot_general(a, b, (((ca,), (cb,)), ((), ())), preferred_element_type=F32)


def _dot3(x, ind):
    h = x.astype(BF16)
    r = x - h.astype(F32)
    m = r.astype(BF16)
    lo = (r - m.astype(F32)).astype(BF16)
    return _dot(h, ind) + _dot(m, ind) + _dot(lo, ind)


def _sigmoid(x):
    return jax.nn.sigmoid(x)


def _mm(a, b, *, name, ta=False, tb=False, bias=None, res=None, out_dtype=F32, b_koff=0, tm=None, tn=None, tk=None,
        dims=None, a_spec=None, b_spec=None, o_spec=None, o_shape=None, dep=None, more=(), target=None,
        rms=None, rms_colsum=False):
    if dims is not None:
        M, N, K = dims
    else:
        if ta:
            K, M = a.shape
        else:
            M, K = a.shape
        N = b.shape[0] if tb else b.shape[1]
    tm = tm or _pick(M, (1024, 1408, 512, 256, 128))
    tn = tn or _pick(N, (512, 1408, 256, 128))
    tk = tk or (K if K <= 2048 else _pick(K, (2048, 1408, 1024, 512)))
    assert M % tm == 0 and N % tn == 0 and K % tk == 0 and b_koff % tk == 0
    nk = K // tk
    kb0 = b_koff // tk
    has_bias, has_res = bias is not None, res is not None

    def body(*refs):
        a_ref, b_ref = refs[0], refs[1]
        pos = 2
        bias_ref = res_ref = acc_ref = None
        if has_bias:
            bias_ref = refs[pos]
            pos += 1
        if has_res:
            res_ref = refs[pos]
            pos += 1
        if dep is not None:
            pos += 1
        extra = refs[pos:pos + 2 * len(more)]
        pos += 2 * len(more)
        tgt_ref = lp_ref = rx_ref = rg_ref = rd_ref = dg_ref = cs_ref = None
        if target is not None:
            tgt_ref = refs[pos]
            pos += 1
        if rms is not None:
            rx_ref, rg_ref, rd_ref = refs[pos:pos + 3]
            pos += 3
        o_ref = refs[pos]
        pos += 1
        if target is not None:
            lp_ref = refs[pos]
            pos += 1
        if rms is not None:
            dg_ref = refs[pos]
            pos += 1
            if rms_colsum:
                cs_ref = refs[pos]
                pos += 1
        if nk > 1:
            acc_ref = refs[pos]
        part = _dot(a_ref[...].astype(BF16), b_ref[...].astype(BF16), 0 if ta else 1, 1 if tb else 0)
        for q in range(len(more)):
            part = part + _dot(extra[2 * q][...].astype(BF16), extra[2 * q + 1][...].astype(BF16),
                               0 if ta else 1, 1 if tb else 0)

        def finish(acc):
            if has_bias:
                acc = acc + bias_ref[...]
            if has_res:
                acc = acc + res_ref[...]
            if target is not None:
                err = acc - tgt_ref[...]
                acc = err * (1.0 / N)
                part_loss = jnp.sum(jnp.sum(err * err, axis=1, keepdims=True), axis=0, keepdims=True) * (0.5 / N)
                first = (pl.program_id(0) == 0) & (pl.program_id(1) == 0)

                @pl.when(first)
                def _():
                    lp_ref[...] = jnp.broadcast_to(part_loss, lp_ref.shape)

                @pl.when(jnp.logical_not(first))
                def _():
                    lp_ref[...] += jnp.broadcast_to(part_loss, lp_ref.shape)

            if rms is not None:
                xv = rx_ref[...]
                r = lax.rsqrt(jnp.mean(xv * xv, axis=-1, keepdims=True) + EPS)
                xh = xv * r
                dxh = acc * rg_ref[...]
                dg_part = jnp.sum(acc * xh, axis=0, keepdims=True)
                acc = rd_ref[...] + r * (dxh - xh * jnp.mean(dxh * xh, axis=-1, keepdims=True))
                cs_part = jnp.sum(acc, axis=0, keepdims=True) if rms_colsum else None
                first_rows = pl.program_id(0) == 0

                @pl.when(first_rows)
                def _():
                    dg_ref[...] = dg_part
                    if rms_colsum:
                        cs_ref[...] = cs_part

                @pl.when(jnp.logical_not(first_rows))
                def _():
                    dg_ref[...] += dg_part
                    if rms_colsum:
                        cs_ref[...] += cs_part

            o_ref[...] = acc.astype(out_dtype)

        if nk == 1:
            finish(part)
        else:
            k = pl.program_id(2)

            @pl.when(k == 0)
            def _():
                acc_ref[...] = part

            @pl.when(k > 0)
            def _():
                acc_ref[...] += part

            @pl.when(k == nk - 1)
            def _():
                finish(acc_ref[...])

    if a_spec is None:
        a_spec = pl.BlockSpec((tk, tm), lambda i, j, k: (k, i)) if ta else pl.BlockSpec((tm, tk), lambda i, j, k: (i, k))
    if b_spec is None:
        b_spec = (pl.BlockSpec((tn, tk), lambda i, j, k: (j, k + kb0)) if tb
                  else pl.BlockSpec((tk, tn), lambda i, j, k: (k + kb0, j)))
    if o_spec is None:
        o_spec = pl.BlockSpec((tm, tn), lambda i, j, k: (i, j))
    in_specs, args = [a_spec, b_spec], [a, b]
    if has_bias:
        in_specs.append(pl.BlockSpec((1, tn), lambda i, j, k: (0, j)))
        args.append(bias)
    if has_res:
        in_specs.append(pl.BlockSpec((tm, tn), lambda i, j, k: (i, j)))
        args.append(res)
    if dep is not None:
        in_specs.append(pl.BlockSpec(memory_space=pl.ANY))
        args.append(dep)
    for piece in more:
        a2, sa, b2, sb = piece if len(piece) == 4 else (a, piece[0], b, piece[1])
        in_specs += [sa, sb]
        args += [a2, b2]
    out_specs, out_shape = [o_spec], [jax.ShapeDtypeStruct(o_shape or (M, N), out_dtype)]
    if target is not None:
        in_specs.append(pl.BlockSpec((tm, tn), lambda i, j, k: (i, j)))
        args.append(target)
        out_specs.append(pl.BlockSpec((8, 128), lambda i, j, k: (0, 0)))
        out_shape.append(jax.ShapeDtypeStruct((8, 128), F32))
    if rms is not None:
        assert tn == N and nk == 1
        row, vec = pl.BlockSpec((tm, N), lambda i, j, k: (i, 0)), pl.BlockSpec((1, N), lambda i, j, k: (0, 0))
        in_specs += [row, vec, row]
        args += list(rms)
        out_specs += [vec] * (2 if rms_colsum else 1)
        out_shape += [jax.ShapeDtypeStruct((1, N), F32)] * (2 if rms_colsum else 1)
    if len(out_specs) == 1:
        out_specs, out_shape = out_specs[0], out_shape[0]
    return pl.pallas_call(
        body, name=name, grid=(M // tm, N // tn, nk), in_specs=in_specs, out_specs=out_specs, out_shape=out_shape,
        scratch_shapes=[pltpu.VMEM((tm, tn), F32)] if nk > 1 else [],
        compiler_params=_cp(3),
    )(*args)


def _norm_mm(x, gain, b, *, name, bias=None, N=None, tn=None, b_spec=None, dep=None):
    M, K = x.shape
    N = N or b.shape[1]
    tm = _pick(M, (1024, 512, 256))
    tn = tn or _pick(N, (512, 1408, 256, 128))
    has_bias = bias is not None

    def body(*refs):
        x_ref, g_ref, b_ref = refs[:3]
        pos = 3 + (1 if has_bias else 0) + (0 if dep is None else 1)
        o_ref, h_ref = refs[pos], refs[pos + 1]

        @pl.when(pl.program_id(1) == 0)
        def _():
            xv = x_ref[...]
            h_ref[...] = (xv * lax.rsqrt(jnp.mean(xv * xv, axis=-1, keepdims=True) + EPS) * g_ref[...]).astype(BF16)

        acc = _dot(h_ref[...], b_ref[...].astype(BF16))
        if has_bias:
            acc = acc + refs[3][...]
        o_ref[...] = acc

    in_specs = [pl.BlockSpec((tm, K), lambda i, j: (i, 0)), pl.BlockSpec((1, K), lambda i, j: (0, 0)),
                b_spec or pl.BlockSpec((K, tn), lambda i, j: (0, j))]
    args = [x, gain, b]
    if has_bias:
        in_specs.append(pl.BlockSpec((1, tn), lambda i, j: (0, j)))
        args.append(bias)
    if dep is not None:
        in_specs.append(pl.BlockSpec(memory_space=pl.ANY))
        args.append(dep)
    return pl.pallas_call(
        body, name=name, grid=(M // tm, N // tn), in_specs=in_specs,
        out_specs=[pl.BlockSpec((tm, tn), lambda i, j: (i, j)), pl.BlockSpec((tm, K), lambda i, j: (i, 0))],
        out_shape=[jax.ShapeDtypeStruct((M, N), F32), jax.ShapeDtypeStruct((M, K), BF16)], compiler_params=_cp(2),
    )(*args)


def _colsum(x, *, name, tr=256):
    S, D = x.shape

    def body(x_ref, o_ref):
        i = pl.program_id(0)
        part = jnp.sum(x_ref[...].astype(F32), axis=0, keepdims=True)

        @pl.when(i == 0)
        def _():
            o_ref[...] = part

        @pl.when(i > 0)
        def _():
            o_ref[...] += part

    return pl.pallas_call(
        body, name=name, grid=(S // tr,), in_specs=[pl.BlockSpec((tr, D), lambda i: (i, 0))],
        out_specs=pl.BlockSpec((1, D), lambda i: (0, 0)), out_shape=jax.ShapeDtypeStruct((1, D), F32),
        compiler_params=_cp(1),
    )(x)


STRIP = 64
HALO = 8


def _strips(S, tc):
    return [(r0, slice(l0, l0 + 128)) for l0 in range(0, tc, 128) for r0 in range(S - STRIP, -1, -STRIP)]


def _with_halo(ref, r0, ls):
    if r0 == 0:
        return jnp.concatenate([jnp.zeros((HALO, 128), F32), ref[0:STRIP, ls]], axis=0)
    return ref[r0 - HALO:r0 + STRIP, ls]


def _conv_strip(xw, w_ref, b_ref, ls, width):
    acc = b_ref[:, ls] + w_ref[pl.ds(width - 1, 1), ls] * xw[HALO:]
    shifted = []
    for s in range(1, width):
        xs = pltpu.roll(xw, s, axis=0)[HALO:]
        shifted.append(xs)
        acc = acc + w_ref[pl.ds(width - 1 - s, 1), ls] * xs
    return acc, shifted


def _conv_strip_back(dacc, after, xc, shifted, w_ref, ls, width):
    ext = jnp.concatenate([dacc, after], axis=0)
    dx = w_ref[pl.ds(width - 1, 1), ls] * dacc
    dws = [None] * width
    dws[width - 1] = jnp.sum(dacc * xc, axis=0, keepdims=True)
    for s in range(1, width):
        dx = dx + w_ref[pl.ds(width - 1 - s, 1), ls] * pltpu.roll(ext, STRIP + HALO - s, axis=0)[:STRIP]
        dws[width - 1 - s] = jnp.sum(dacc * shifted[s - 1], axis=0, keepdims=True)
    return dx, dws, jnp.sum(dacc, axis=0, keepdims=True)


def _conv_back_block(S, tc, width, w_ref, b_ref, x_ref, dacc_of, dx_store, dw_ref, db_ref):
    for l0 in range(0, tc, 128):
        ls = slice(l0, l0 + 128)
        after = jnp.zeros((HALO, 128), F32)
        tot = None
        for r0 in range(S - STRIP, -1, -STRIP):
            xw = _with_halo(x_ref, r0, ls)
            acc, shifted = _conv_strip(xw, w_ref, b_ref, ls, width)
            dacc = dacc_of(r0, ls, acc, _sigmoid(acc))
            dx, dws, db = _conv_strip_back(dacc, after, xw[HALO:], shifted, w_ref, ls, width)
            dx_store(r0, ls, dx)
            after = dacc[:HALO]
            part = dws + [db]
            tot = part if tot is None else [p + q for p, q in zip(tot, part)]
        for k in range(width):
            dw_ref[pl.ds(k, 1), ls] = tot[k]
        db_ref[:, ls] = tot[width]


def _conv_silu_fwd(xin, col0, C, w, b, *, name, tc=512):
    S = xin.shape[0]
    width = w.shape[0]
    off = col0 // tc

    def body(x_ref, w_ref, b_ref, o_ref):
        for r0, ls in _strips(S, tc):
            acc, _ = _conv_strip(_with_halo(x_ref, r0, ls), w_ref, b_ref, ls, width)
            o_ref[r0:r0 + STRIP, ls] = acc * _sigmoid(acc)

    return pl.pallas_call(
        body, name=name, grid=(C // tc,),
        in_specs=[pl.BlockSpec((S, tc), lambda j: (0, j + off)), pl.BlockSpec((width, tc), lambda j: (0, j)),
                  pl.BlockSpec((1, tc), lambda j: (0, j))],
        out_specs=pl.BlockSpec((S, tc), lambda j: (0, j)), out_shape=jax.ShapeDtypeStruct((S, C), F32),
        compiler_params=_cp(1),
    )(xin, w, b)


def _conv_silu_bwd(xin, col0, C, w, b, douts, *, name, tc=256):
    S = xin.shape[0]
    width = w.shape[0]
    off = col0 // tc
    nd = len(douts)
    ranges = [(o // tc, (o + d.shape[1]) // tc) for d, o in douts]

    def body(*refs):
        x_ref, w_ref, b_ref = refs[0], refs[1], refs[2]
        d_refs = refs[3:3 + nd]
        dx_ref, dw_ref, db_ref = refs[3 + nd], refs[4 + nd], refs[5 + nd]
        j = pl.program_id(0)

        def dacc_of(r0, ls, acc, sg):
            dout = jnp.zeros((STRIP, 128), F32)
            for q in range(nd):
                lo, hi = ranges[q]
                dout = dout + jnp.where((j >= lo) & (j < hi), d_refs[q][r0:r0 + STRIP, ls], 0.0)
            return dout * (sg * (1.0 + acc * (1.0 - sg)))

        def dx_store(r0, ls, dx):
            dx_ref[r0:r0 + STRIP, ls] = dx.astype(BF16)

        _conv_back_block(S, tc, width, w_ref, b_ref, x_ref, dacc_of, dx_store, dw_ref, db_ref)

    d_specs = [pl.BlockSpec((S, tc), (lambda j, lo=lo, hi=hi: (0, jnp.clip(j - lo, 0, hi - lo - 1)))) for lo, hi in ranges]
    return pl.pallas_call(
        body, name=name, grid=(C // tc,),
        in_specs=[pl.BlockSpec((S, tc), lambda j: (0, j + off)), pl.BlockSpec((width, tc), lambda j: (0, j)),
                  pl.BlockSpec((1, tc), lambda j: (0, j))] + d_specs,
        out_specs=[pl.BlockSpec((S, tc), lambda j: (0, j)), pl.BlockSpec((width, tc), lambda j: (0, j)),
                   pl.BlockSpec((1, tc), lambda j: (0, j))],
        out_shape=[jax.ShapeDtypeStruct((S, C), BF16), jax.ShapeDtypeStruct((width, C), F32),
                   jax.ShapeDtypeStruct((1, C), F32)],
        compiler_params=_cp(1),
    )(xin, w, b, *[d for d, _ in douts])


def _ffn_act_fwd(u, w, b, *, name, tc=256):
    S, F2 = u.shape
    Fd = F2 // 2
    width = w.shape[0]
    nb = Fd // tc

    def body(g_ref, v_ref, w_ref, b_ref, o_ref):
        for r0, ls in _strips(S, tc):
            acc, _ = _conv_strip(_with_halo(g_ref, r0, ls), w_ref, b_ref, ls, width)
            o_ref[r0:r0 + STRIP, ls] = (acc * _sigmoid(acc) * v_ref[r0:r0 + STRIP, ls]).astype(BF16)

    return pl.pallas_call(
        body, name=name, grid=(nb,),
        in_specs=[pl.BlockSpec((S, tc), lambda j: (0, j)), pl.BlockSpec((S, tc), lambda j: (0, j + nb)),
                  pl.BlockSpec((width, tc), lambda j: (0, j)), pl.BlockSpec((1, tc), lambda j: (0, j))],
        out_specs=pl.BlockSpec((S, tc), lambda j: (0, j)), out_shape=jax.ShapeDtypeStruct((S, Fd), BF16),
        compiler_params=_cp(1),
    )(u, u, w, b)


def _ffn_act_bwd(u, w, b, da, *, name, tc=256):
    S, F2 = u.shape
    Fd = F2 // 2
    width = w.shape[0]
    nb = Fd // tc

    def body(g_ref, v_ref, w_ref, b_ref, da_ref, du_ref, dw_ref, db_ref, a_ref):
        def dacc_of(r0, ls, acc, sg):
            rs = slice(r0, r0 + STRIP)
            dav, val, silu = da_ref[rs, ls], v_ref[rs, ls], acc * sg
            a_ref[rs, ls] = (silu * val).astype(BF16)
            du_ref[1, rs, ls] = (dav * silu).astype(BF16)
            return dav * val * (sg * (1.0 + acc * (1.0 - sg)))

        def dx_store(r0, ls, dx):
            du_ref[0, r0:r0 + STRIP, ls] = dx.astype(BF16)

        _conv_back_block(S, tc, width, w_ref, b_ref, g_ref, dacc_of, dx_store, dw_ref, db_ref)

    blk = pl.BlockSpec((S, tc), lambda j: (0, j))
    return pl.pallas_call(
        body, name=name, grid=(nb,),
        in_specs=[blk, pl.BlockSpec((S, tc), lambda j: (0, j + nb)), pl.BlockSpec((width, tc), lambda j: (0, j)),
                  pl.BlockSpec((1, tc), lambda j: (0, j)), blk],
        out_specs=[pl.BlockSpec((2, S, tc), lambda j: (0, 0, j)), pl.BlockSpec((width, tc), lambda j: (0, j)),
                   pl.BlockSpec((1, tc), lambda j: (0, j)), blk],
        out_shape=[jax.ShapeDtypeStruct((2, S, Fd), BF16),
                   jax.ShapeDtypeStruct((width, Fd), F32), jax.ShapeDtypeStruct((1, Fd), F32),
                   jax.ShapeDtypeStruct((S, Fd), BF16)],
        compiler_params=_cp(1),
    )(u, u, w, b, da)


def _ssd_prep(dtr, dt_bias, a_log, *, name="ssd_prep"):
    S = dtr.shape[0]

    def body(d_ref, b_ref, al_ref, dt_ref, ac_ref, sg_ref, act_ref):
        lane = _iota((CHUNK, 128), 1)
        valid = lane < SSM_HEADS
        z = d_ref[...] + b_ref[...]
        dt = jnp.where(valid, jnp.maximum(z, 0.0) + jnp.log(1.0 + jnp.exp(-jnp.abs(z))), 0.0)
        a = dt * (-jnp.exp(al_ref[...]))
        row = _iota((CHUNK, 128), 0)
        k = 1
        while k < CHUNK:
            a = a + jnp.where(row >= k, pltpu.roll(a, k, axis=0), 0.0)
            k *= 2
        sg = jnp.where(valid, _sigmoid(z), 0.0)
        for arr, ref in ((dt, dt_ref), (a, ac_ref), (sg, sg_ref)):
            for g in range(SSM_GROUPS):
                ref[g] = jnp.where(lane < 4, arr if g == 0 else pltpu.roll(arr, 128 - 4 * g, axis=1), 0.0)
        act_ref[...] = a.T[:SSM_HEADS, :]

    blk = pl.BlockSpec((CHUNK, 128), lambda i: (i, 0))
    vec = pl.BlockSpec((1, 128), lambda i: (0, 0))
    grp = pl.BlockSpec((SSM_GROUPS, CHUNK, 128), lambda i: (0, i, 0))
    return pl.pallas_call(
        body, name=name, grid=(S // CHUNK,), in_specs=[blk, vec, vec],
        out_specs=[grp, grp, grp, pl.BlockSpec((SSM_HEADS, CHUNK), lambda i: (0, i))],
        out_shape=[jax.ShapeDtypeStruct((SSM_GROUPS, S, 128), F32)] * 3 + [jax.ShapeDtypeStruct((SSM_HEADS, S), F32)],
        compiler_params=_cp(1),
    )(dtr, dt_bias, a_log)


SSD_GPS = 4


def _expand4(v, lanes):
    out = jnp.broadcast_to(v[:, 3:4], lanes.shape)
    for hh in (2, 1, 0):
        out = jnp.where(lanes < 64 * (hh + 1), v[:, hh:hh + 1], out)
    return out


def _ssd_fwd(xbc, dt_g, ac_g, ac_t, *, name="ssd_fwd", dep=None):
    S = xbc.shape[0]
    nc = S // CHUNK
    Lc = CHUNK

    def body(x_ref, b_ref, c_ref, dt_ref, ac_ref, act_ref, *rest):
        y_ref, st_out_ref, st_ref = rest[-3:]
        g2 = pl.program_id(0)
        c = pl.program_id(1)

        @pl.when(c == 0)
        def _():
            st_ref[...] = jnp.zeros_like(st_ref)

        causal = _iota((Lc, Lc), 0) >= _iota((Lc, Lc), 1)
        lane256 = _iota((Lc, 256), 1)
        lane128 = _iota((Lc, 128), 1)
        row128 = _iota((128, 128), 0)
        for gg in range(SSD_GPS):
            g = SSD_GPS * g2 + gg
            bv = b_ref[:, 128 * gg:128 * (gg + 1)]
            cbf = c_ref[:, 128 * gg:128 * (gg + 1)].astype(BF16)
            cb = _dot(cbf, bv.astype(BF16), 1, 1)
            dtg, acg = dt_ref[gg], ac_ref[gg]
            ac_last = ac_ref[gg, pl.ds(Lc - 1, 1), :]
            dt4 = _expand4(dtg, lane256)
            ac4 = _expand4(acg, lane256)
            e4 = jnp.exp(ac4)
            xdb = (x_ref[:, 256 * gg:256 * (gg + 1)] * dt4).astype(BF16)
            st_out_ref[gg] = st_ref[gg]
            for p in range(2):
                xd_p = xdb[:, 128 * p:128 * (p + 1)]
                st_p = st_ref[gg, p]
                ys, sn, cds = [], [], []
                for q in range(2):
                    hh = 2 * p + q
                    a_col = acg[:, hh:hh + 1]
                    a_row = act_ref[pl.ds(4 * g + hh, 1), :]
                    dec = jnp.exp(jnp.where(causal, a_col - a_row, NEG))
                    w = (cb * dec).astype(BF16)
                    ys.append(_dot(w, xd_p))
                    al = ac_last[:, hh:hh + 1]
                    dte = jnp.exp(al - a_col)
                    sn.append(_dot(xd_p, (bv * dte).astype(BF16), 0, 0))
                    cds.append(jnp.exp(al))
                y_diag = jnp.where(lane128 < 64, ys[0], ys[1])
                y_off = _dot(cbf, st_p.astype(BF16), 1, 1) * e4[:, 128 * p:128 * (p + 1)]
                y_ref[:, 256 * gg + 128 * p:256 * gg + 128 * (p + 1)] = y_diag + y_off
                st_ref[gg, p] = jnp.where(row128 < 64, st_p * cds[0] + sn[0], st_p * cds[1] + sn[1])

    G = SSD_GPS
    per_g = lambda g, c: (g, c, 0)
    return pl.pallas_call(
        body, name=name, grid=(SSM_GROUPS // G, nc),
        in_specs=[pl.BlockSpec((Lc, 256 * G), lambda g, c: (c, g)),
                  pl.BlockSpec((Lc, 128 * G), lambda g, c: (c, 16 // G + g)),
                  pl.BlockSpec((Lc, 128 * G), lambda g, c: (c, 24 // G + g)),
                  pl.BlockSpec((G, Lc, 128), per_g), pl.BlockSpec((G, Lc, 128), per_g),
                  pl.BlockSpec((SSM_HEADS, Lc), lambda g, c: (0, c))] + ([] if dep is None else [pl.BlockSpec(memory_space=pl.ANY)]),
        out_specs=[pl.BlockSpec((Lc, 256 * G), lambda g, c: (c, g)),
                   pl.BlockSpec((G, None, 2, 128, 128), lambda g, c: (g, c, 0, 0, 0))],
        out_shape=[jax.ShapeDtypeStruct((S, 2048), F32), jax.ShapeDtypeStruct((SSM_GROUPS, nc, 2, 128, 128), F32)],
        scratch_shapes=[pltpu.VMEM((G, 2, 128, 128), F32)], compiler_params=_cp(2),
    )(xbc, xbc, xbc, dt_g, ac_g, ac_t, *([] if dep is None else [dep]))


def _ssd_bwd(xbc, dt_g, ac_g, ac_t, states, dy, dexp, *, name="ssd_bwd", dep=None):
    S = xbc.shape[0]
    nc = S // CHUNK
    Lc = CHUNK

    def body(x_ref, b_ref, c_ref, dt_ref, ac_ref, act_ref, st_ref, dy_ref, d_ref, *rest):
        dx_ref, db_ref, dc_ref, dh_ref, ds_ref = rest[-5:]
        g2 = pl.program_id(0)
        cc = pl.program_id(1)

        @pl.when(cc == 0)
        def _():
            ds_ref[...] = jnp.zeros_like(ds_ref)

        causal = _iota((Lc, Lc), 0) >= _iota((Lc, Lc), 1)
        lane256 = _iota((Lc, 256), 1)
        lane128 = _iota((Lc, 128), 1)
        row128 = _iota((128, 128), 0)
        ind_rows = _iota((256, 128), 0) >> 6
        ind_cols = _iota((256, 128), 1)
        ind_a = (ind_rows == ind_cols).astype(BF16)
        ind_b = (ind_rows + 4 == ind_cols).astype(BF16)
        for gg in range(SSD_GPS):
            g = SSD_GPS * g2 + gg
            bv = b_ref[:, 128 * gg:128 * (gg + 1)]
            cv = c_ref[:, 128 * gg:128 * (gg + 1)]
            bbf, cbf = bv.astype(BF16), cv.astype(BF16)
            cb = _dot(cbf, bbf, 1, 1)
            dtg, acg = dt_ref[gg], ac_ref[gg]
            ac_last = ac_ref[gg, pl.ds(Lc - 1, 1), :]
            dt4 = _expand4(dtg, lane256)
            ac4 = _expand4(acg, lane256)
            acl4 = _expand4(ac_last, _iota((1, 256), 1))
            e4 = jnp.exp(ac4)
            dte4 = jnp.exp(acl4 - ac4)
            xv = x_ref[:, 256 * gg:256 * (gg + 1)]
            xd = xv * dt4
            xdb = xd.astype(BF16)
            dyv = dy_ref[:, 256 * gg:256 * (gg + 1)]
            dcb = jnp.zeros((Lc, Lc), F32)
            dc_acc = jnp.zeros((Lc, 128), F32)
            db_acc = jnp.zeros((Lc, 128), F32)
            u_parts, dxd_parts, ends = [], [], []
            for p in range(2):
                sl = slice(128 * p, 128 * (p + 1))
                xd_p, xdb_p, dy_p = xd[:, sl], xdb[:, sl], dyv[:, sl]
                dyb_p = dy_p.astype(BF16)
                e_p, dte_p = e4[:, sl], dte4[:, sl]
                sp = st_ref[gg, p]
                spb = sp.astype(BF16)
                dsn = ds_ref[gg, p]
                dsnb = dsn.astype(BF16)
                yds, dxds, cds = [], [], []
                for q in range(2):
                    hh = 2 * p + q
                    a_col = acg[:, hh:hh + 1]
                    a_row = act_ref[pl.ds(4 * g + hh, 1), :]
                    dec = jnp.exp(jnp.where(causal, a_col - a_row, NEG))
                    w = (cb * dec).astype(BF16)
                    head = (lane128 < 64) if q == 0 else (lane128 >= 64)
                    dym = jnp.where(head, dyb_p, jnp.zeros_like(dyb_p))
                    dw = _dot(dym, xdb_p, 1, 1)
                    dcb = dcb + dw * dec
                    yds.append(_dot(w, xdb_p))
                    dxds.append(_dot(w, dyb_p, 0, 0))
                    cds.append(jnp.exp(ac_last[:, hh:hh + 1]))
                y_diag = jnp.where(lane128 < 64, yds[0], yds[1])
                dxd_diag = jnp.where(lane128 < 64, dxds[0], dxds[1])
                y_off = _dot(cbf, spb, 1, 1) * e_p
                dgp = dy_p * e_p
                dgb = dgp.astype(BF16)
                dc_acc = dc_acc + _dot(dgb, spb)
                dsp = _dot(dgb, cbf, 0, 0)
                cd_col = jnp.where(row128[:, 0:1] < 64, cds[0], cds[1])
                qm = _dot(bbf, dsnb, 1, 1)
                dxd_state = dte_p * qm
                db_acc = db_acc + _dot((xd_p * dte_p).astype(BF16), dsnb)
                t_p = xd_p * dxd_state
                prod = dsn * sp
                e0 = jnp.sum(jnp.sum(jnp.where(row128 < 64, prod, 0.0), axis=1, keepdims=True), axis=0, keepdims=True)
                e1 = jnp.sum(jnp.sum(jnp.where(row128 >= 64, prod, 0.0), axis=1, keepdims=True), axis=0, keepdims=True)
                tcol = jnp.sum(t_p, axis=0, keepdims=True)
                lane1 = _iota((1, 128), 1)
                t0 = jnp.sum(jnp.where(lane1 < 64, tcol, 0.0), axis=1, keepdims=True)
                t1 = jnp.sum(jnp.where(lane1 >= 64, tcol, 0.0), axis=1, keepdims=True)
                ends.append(e0 * cds[0] + t0)
                ends.append(e1 * cds[1] + t1)
                ds_ref[gg, p] = dsn * cd_col + dsp
                u_parts.append(dyb_p.astype(F32) * y_diag - xdb_p.astype(F32) * dxd_diag + dy_p * y_off - t_p)
                dxd_parts.append(dxd_diag + dxd_state)
            dxd = jnp.concatenate(dxd_parts, axis=1)
            u_all = jnp.concatenate(u_parts, axis=1)
            dx_ref[:, 256 * gg:256 * (gg + 1)] = dxd * dt4 + dyv * d_ref[:, 256 * gg:256 * (gg + 1)]
            dcbb = dcb.astype(BF16)
            dc_ref[:, 128 * gg:128 * (gg + 1)] = dc_acc + _dot(dcbb, bbf)
            db_ref[:, 128 * gg:128 * (gg + 1)] = db_acc + _dot(dcbb, cbf, 0, 0)
            lane = _iota((Lc, 128), 1)
            endv = jnp.zeros((Lc, 128), F32)
            for hh in range(4):
                endv = jnp.where(lane == 8 + hh, ends[hh], endv)
            dh_ref[gg] = _dot3(dxd * xv, ind_a) + _dot3(u_all, ind_b) + endv

    G = SSD_GPS
    rev = lambda c: nc - 1 - c
    per_g = lambda g, c: (g, rev(c), 0)
    return pl.pallas_call(
        body, name=name, grid=(SSM_GROUPS // G, nc),
        in_specs=[pl.BlockSpec((Lc, 256 * G), lambda g, c: (rev(c), g)),
                  pl.BlockSpec((Lc, 128 * G), lambda g, c: (rev(c), 16 // G + g)),
                  pl.BlockSpec((Lc, 128 * G), lambda g, c: (rev(c), 24 // G + g)),
                  pl.BlockSpec((G, Lc, 128), per_g), pl.BlockSpec((G, Lc, 128), per_g),
                  pl.BlockSpec((SSM_HEADS, Lc), lambda g, c: (0, rev(c))),
                  pl.BlockSpec((G, None, 2, 128, 128), lambda g, c: (g, rev(c), 0, 0, 0)),
                  pl.BlockSpec((Lc, 256 * G), lambda g, c: (rev(c), g)),
                  pl.BlockSpec((1, 256 * G), lambda g, c: (0, g))] + ([] if dep is None else [pl.BlockSpec(memory_space=pl.ANY)]),
        out_specs=[pl.BlockSpec((Lc, 256 * G), lambda g, c: (rev(c), g)),
                   pl.BlockSpec((Lc, 128 * G), lambda g, c: (rev(c), g)),
                   pl.BlockSpec((Lc, 128 * G), lambda g, c: (rev(c), g)),
                   pl.BlockSpec((G, Lc, 128), per_g)],
        out_shape=[jax.ShapeDtypeStruct((S, 2048), F32), jax.ShapeDtypeStruct((S, 1024), F32),
                   jax.ShapeDtypeStruct((S, 1024), F32), jax.ShapeDtypeStruct((SSM_GROUPS, S, 128), F32)],
        scratch_shapes=[pltpu.VMEM((G, 2, 128, 128), F32)], compiler_params=_cp(2),
    )(xbc, xbc, xbc, dt_g, ac_g, ac_t, states, dy, dexp, *([] if dep is None else [dep]))


def _ssd_post(dhead, dt_g, sg_g, alog_g, *, name="ssd_post"):
    S = dhead.shape[1]
    nc = S // CHUNK
    Lc = CHUNK

    def body(dh_ref, dt_ref, sg_ref, al_ref, o_ref, s_ref):
        @pl.when(pl.program_id(0) == 0)
        def _():
            s_ref[...] = jnp.zeros_like(s_ref)

        lane = _iota((Lc, 128), 1)
        row = _iota((Lc, 128), 0)
        row8 = _iota((8, 128), 0)
        out = jnp.zeros((Lc, 128), F32)
        for g in range(SSM_GROUPS):
            dh = dh_ref[g]
            a_neg = -jnp.exp(al_ref[g])
            dac = jnp.where(lane < 4, pltpu.roll(dh, 124, axis=1), 0.0)
            end = jnp.where(lane < 4, pltpu.roll(dh, 120, axis=1), 0.0)
            k = 1
            while k < Lc:
                dac = dac + jnp.where(row < Lc - k, pltpu.roll(dac, Lc - k, axis=0), 0.0)
                k *= 2
            da = dac + end
            ddt = jnp.where(lane < 4, da * a_neg + dh, 0.0)
            ddtr = ddt * sg_ref[g]
            out = out + (ddtr if g == 0 else pltpu.roll(ddtr, 4 * g, axis=1))
            dal = jnp.sum(da * dt_ref[g], axis=0, keepdims=True) * a_neg
            dbias = jnp.sum(ddtr, axis=0, keepdims=True)
            part = jnp.where(row8 == 0, dal, jnp.where(row8 == 1, dbias, 0.0))
            s_ref[g] += part
        o_ref[...] = out.astype(BF16)

    grp = pl.BlockSpec((SSM_GROUPS, Lc, 128), lambda c: (0, c, 0))
    whole = lambda r: pl.BlockSpec((SSM_GROUPS, r, 128), lambda c: (0, 0, 0))
    return pl.pallas_call(
        body, name=name, grid=(nc,), in_specs=[grp, grp, grp, whole(1)],
        out_specs=[pl.BlockSpec((Lc, 128), lambda c: (c, 0)), whole(8)],
        out_shape=[jax.ShapeDtypeStruct((S, 128), BF16), jax.ShapeDtypeStruct((SSM_GROUPS, 8, 128), F32)],
        compiler_params=_cp(1),
    )(dhead, dt_g, sg_g, alog_g)


def _gate_fwd(y, xbc, zx, dexp, gn, *, name="gate_fwd", tr=256, dep=None):
    S = y.shape[0]
    W = 2048
    gw = W // SSM_GROUPS

    def body(y_ref, x_ref, z_ref, d_ref, g_ref, *rest):
        o_ref = rest[-1]
        z = z_ref[...]
        u = (y_ref[...] + x_ref[...] * d_ref[...]) * (z * _sigmoid(z))
        gv = g_ref[...]
        for q in range(SSM_GROUPS):
            sl = slice(gw * q, gw * (q + 1))
            uq = u[:, sl]
            r = lax.rsqrt(jnp.mean(uq * uq, axis=-1, keepdims=True) + EPS)
            o_ref[:, sl] = (uq * r * gv[:, sl]).astype(BF16)

    row = pl.BlockSpec((tr, W), lambda i: (i, 0))
    vec = pl.BlockSpec((1, W), lambda i: (0, 0))
    return pl.pallas_call(
        body, name=name, grid=(S // tr,),
        in_specs=[row, row, row, vec, vec] + ([] if dep is None else [pl.BlockSpec(memory_space=pl.ANY)]), out_specs=row,
        out_shape=jax.ShapeDtypeStruct((S, W), BF16), compiler_params=_cp(1),
    )(y, xbc, zx, dexp, gn, *([] if dep is None else [dep]))


def _gate_bwd(y, xbc, zx, dexp, gn, dout, *, name="gate_bwd", tr=256):
    S = y.shape[0]
    W = 2048
    gw = W // SSM_GROUPS
    steps = S // tr

    def body(y_ref, x_ref, z_ref, d_ref, g_ref, do_ref, dy_ref, dz_ref, dg_ref, dd_ref, acc_ref):
        i = pl.program_id(0)

        @pl.when(i == 0)
        def _():
            acc_ref[...] = jnp.zeros_like(acc_ref)

        z = z_ref[...]
        sg = _sigmoid(z)
        sz = z * sg
        xs = x_ref[...]
        yt = y_ref[...] + xs * d_ref[...]
        u = yt * sz
        gv = g_ref[...]
        do = do_ref[...]
        dgs = []
        for q in range(SSM_GROUPS):
            sl = slice(gw * q, gw * (q + 1))
            uq = u[:, sl]
            r = lax.rsqrt(jnp.mean(uq * uq, axis=-1, keepdims=True) + EPS)
            uh = uq * r
            dq = do[:, sl]
            duh = dq * gv[:, sl]
            duq = r * (duh - uh * jnp.mean(duh * uh, axis=-1, keepdims=True))
            dgs.append(jnp.sum(dq * uh, axis=0, keepdims=True))
            dyt = duq * sz[:, sl]
            dy_ref[:, sl] = dyt
            dz_ref[:, sl] = (duq * yt[:, sl] * (sg[:, sl] * (1.0 + z[:, sl] * (1.0 - sg[:, sl])))).astype(BF16)
            acc_ref[:, sl] += jnp.sum(dyt * xs[:, sl], axis=0, keepdims=True)
        dg = jnp.concatenate(dgs, axis=1)

        @pl.when(i == 0)
        def _():
            dg_ref[...] = dg

        @pl.when(i > 0)
        def _():
            dg_ref[...] += dg

        @pl.when(i == steps - 1)
        def _():
            ind = ((_iota((W, 128), 0) >> 6) == _iota((W, 128), 1)).astype(BF16)
            dd_ref[...] = _dot3(jnp.broadcast_to(acc_ref[...], (8, W)), ind)[0:1, :]

    row = pl.BlockSpec((tr, W), lambda i: (i, 0))
    vec = pl.BlockSpec((1, W), lambda i: (0, 0))
    return pl.pallas_call(
        body, name=name, grid=(steps,), in_specs=[row, row, row, vec, vec, row],
        out_specs=[row, row, vec, pl.BlockSpec((1, 128), lambda i: (0, 0))],
        out_shape=[jax.ShapeDtypeStruct((S, W), F32), jax.ShapeDtypeStruct((S, W), BF16),
                   jax.ShapeDtypeStruct((1, W), F32), jax.ShapeDtypeStruct((1, 128), F32)],
        scratch_shapes=[pltpu.VMEM((1, W), F32)], compiler_params=_cp(1),
    )(y, xbc, zx, dexp, gn, dout)


def _rope_cs(posf, *, name="rope_tables", tr=256):
    S = posf.shape[0]

    def body(p_ref, c_ref, s_ref):
        j = (_iota((tr, 128), 1) & 31).astype(F32)
        ang = p_ref[...] * jnp.exp(j * (-math.log(ROPE_THETA) / 32.0))
        c_ref[...] = jnp.cos(ang)
        s_ref[...] = jnp.sin(ang)

    blk = pl.BlockSpec((tr, 128), lambda i: (i, 0))
    return pl.pallas_call(
        body, name=name, grid=(S // tr,), in_specs=[pl.BlockSpec((tr, 1), lambda i: (i, 0))], out_specs=[blk, blk],
        out_shape=[jax.ShapeDtypeStruct((S, 128), F32)] * 2, compiler_params=_cp(1),
    )(posf)


def _rope_tables(c_ref, s_ref, shape):
    reps = shape[1] // 128
    return jnp.tile(c_ref[...], (1, reps)), jnp.tile(s_ref[...], (1, reps)), (_iota(shape, 1) & 63) < 32


def _hn_inds(W):
    ind = ((_iota((W, 128), 0) >> 6) == _iota((W, 128), 1)).astype(BF16)
    ind_t = ((_iota((128, W), 1) >> 6) == _iota((128, W), 0)).astype(BF16)
    return ind, ind_t


def _hnrope_fwd(xin, col0, W, gain_w, rope, *, name, tr=256):
    S = xin.shape[0]
    off = col0 // W
    nh = W // HEAD

    def body(x_ref, g_ref, c_ref, s_ref, o_ref):
        x = x_ref[...]
        ind, ind_t = _hn_inds(W)
        r = lax.rsqrt(_dot3(x * x, ind) * (1.0 / HEAD) + EPS)
        xn = x * _dot3(r, ind_t) * g_ref[...]
        cs, sn, half = _rope_tables(c_ref, s_ref, (tr, W))
        rot = jnp.where(half, -pltpu.roll(xn, W - 32, axis=1), pltpu.roll(xn, 32, axis=1))
        out = (xn * cs + rot * sn).astype(BF16)
        for h in range(nh):
            o_ref[h] = out[:, HEAD * h:HEAD * (h + 1)]

    tab = pl.BlockSpec((tr, 128), lambda i: (i, 0))
    return pl.pallas_call(
        body, name=name, grid=(S // tr,),
        in_specs=[pl.BlockSpec((tr, W), lambda i: (i, off)), pl.BlockSpec((1, W), lambda i: (0, 0)), tab, tab],
        out_specs=pl.BlockSpec((nh, tr, HEAD), lambda i: (0, i, 0)), out_shape=jax.ShapeDtypeStruct((nh, S, HEAD), BF16),
        compiler_params=_cp(1),
    )(xin, gain_w, *rope)


def _hnrope_bwd(xin, col0, W, gain_w, rope, dout, *, name, tr=256):
    S = xin.shape[0]
    off = col0 // W
    steps = S // tr
    nh = W // HEAD

    def body(x_ref, g_ref, c_ref, s_ref, do_ref, dx_ref, cs_ref, dg_ref, acc_ref):
        i = pl.program_id(0)
        x = x_ref[...]
        ind, ind_t = _hn_inds(W)
        r = lax.rsqrt(_dot3(x * x, ind) * (1.0 / HEAD) + EPS)
        rw = _dot3(r, ind_t)
        xh = x * rw
        cs, sn, half = _rope_tables(c_ref, s_ref, (tr, W))
        do = jnp.concatenate([do_ref[h] for h in range(nh)], axis=1).astype(F32)
        gs = do * sn
        g1 = do * cs + jnp.where(half, pltpu.roll(gs, W - 32, axis=1), -pltpu.roll(gs, 32, axis=1))
        dxh = g1 * g_ref[...]
        t = _dot3(dxh * xh, ind) * (1.0 / HEAD)
        dx = rw * (dxh - xh * _dot3(t, ind_t))
        dx_ref[...] = dx.astype(BF16)
        cpart = jnp.sum(dx, axis=0, keepdims=True)
        gpart = jnp.sum(g1 * xh, axis=0, keepdims=True)

        @pl.when(i == 0)
        def _():
            cs_ref[...] = cpart
            acc_ref[...] = gpart

        @pl.when(i > 0)
        def _():
            cs_ref[...] += cpart
            acc_ref[...] += gpart

        @pl.when(i == steps - 1)
        def _():
            fold = ((_iota((W, 128), 0) & 63) == _iota((W, 128), 1)).astype(BF16)
            dg_ref[...] = _dot3(jnp.broadcast_to(acc_ref[...], (8, W)), fold)[0:1, :]

    tab = pl.BlockSpec((tr, 128), lambda i: (i, 0))
    return pl.pallas_call(
        body, name=name, grid=(steps,),
        in_specs=[pl.BlockSpec((tr, W), lambda i: (i, off)), pl.BlockSpec((1, W), lambda i: (0, 0)), tab, tab,
                  pl.BlockSpec((nh, tr, HEAD), lambda i: (0, i, 0))],
        out_specs=[pl.BlockSpec((tr, W), lambda i: (i, 0)), pl.BlockSpec((1, W), lambda i: (0, 0)),
                   pl.BlockSpec((1, 128), lambda i: (0, 0))],
        out_shape=[jax.ShapeDtypeStruct((S, W), BF16), jax.ShapeDtypeStruct((1, W), F32),
                   jax.ShapeDtypeStruct((1, 128), F32)],
        scratch_shapes=[pltpu.VMEM((1, W), F32)], compiler_params=_cp(1),
    )(xin, gain_w, *rope, dout)


def _attn_band():
    qi = jnp.arange(ATT_G * WINDOW)[:, None] % WINDOW
    ki = jnp.arange(2 * WINDOW)[None, :]
    rel = qi + WINDOW - ki
    ok = (rel >= 0) & (rel < WINDOW)
    return jnp.stack([jnp.where(ok & (ki >= WINDOW), 0.0, NEG), jnp.where(ok, 0.0, NEG)]).astype(F32)


def _attn_probs(q, kb, sink_ref, band_ref, h, i):
    s = _dot(q, kb, 1, 1) * (HEAD ** -0.5) + band_ref[jnp.minimum(i, 1)]
    r1 = _iota((4 * WINDOW, 1), 0)
    sink = jnp.where(r1 < WINDOW, sink_ref[4 * h], jnp.where(r1 < 2 * WINDOW, sink_ref[4 * h + 1],
                     jnp.where(r1 < 3 * WINDOW, sink_ref[4 * h + 2], sink_ref[4 * h + 3])))
    m = jnp.maximum(jnp.max(s, axis=1, keepdims=True), sink)
    p = jnp.exp(s - m)
    ps = jnp.exp(sink - m)
    inv = 1.0 / (jnp.sum(p, axis=1, keepdims=True) + ps)
    return p * inv, ps * inv


ATT_HPS = 4
_BAND = pl.BlockSpec((2, ATT_G * WINDOW, 2 * WINDOW), lambda h, i: (0, 0, 0))


def _attn_specs(S):
    qspec = pl.BlockSpec((ATT_HPS, ATT_G, WINDOW, HEAD), lambda h, i: (h, 0, i, 0))
    cur = pl.BlockSpec((ATT_HPS, WINDOW, HEAD), lambda h, i: (h, i, 0))
    prev = pl.BlockSpec((ATT_HPS, WINDOW, HEAD), lambda h, i: (h, jnp.maximum(i - 1, 0), 0))
    tok = pl.BlockSpec((WINDOW, ATT_HPS * ATT_G * HEAD), lambda h, i: (i, h))
    return qspec, cur, prev, tok


def _attn_fwd(qh, kh, vh, sinks, *, name="attn_fwd"):
    S = kh.shape[1]
    nb = S // WINDOW

    def body(s_ref, band_ref, q_ref, kc_ref, kp_ref, vc_ref, vp_ref, o_ref):
        h2, i = pl.program_id(0), pl.program_id(1)
        outs = []
        for hh in range(ATT_HPS):
            q = q_ref[hh].reshape(ATT_G * WINDOW, HEAD)
            kb = jnp.concatenate([kp_ref[hh], kc_ref[hh]], axis=0)
            vb = jnp.concatenate([vp_ref[hh], vc_ref[hh]], axis=0)
            probs, _ = _attn_probs(q, kb, s_ref, band_ref, ATT_HPS * h2 + hh, i)
            o = _dot(probs.astype(BF16), vb).astype(BF16)
            outs += [o[WINDOW * g:WINDOW * (g + 1)] for g in range(ATT_G)]
        o_ref[...] = jnp.concatenate(outs, axis=1)

    qspec, cur, prev, tok = _attn_specs(S)
    return pl.pallas_call(
        body, name=name, grid=(ATT_KV // ATT_HPS, nb),
        in_specs=[pl.BlockSpec(memory_space=pltpu.SMEM), _BAND, qspec, cur, prev, cur, prev], out_specs=tok,
        out_shape=jax.ShapeDtypeStruct((S, ATT_KV * ATT_G * HEAD), BF16), compiler_params=_cp(2),
    )(sinks, _attn_band(), qh, kh, kh, vh, vh)


def _attn_bwd(qh, kh, vh, sinks, doh, *, name="attn_bwd"):
    S = kh.shape[1]
    nb = S // WINDOW

    def body(s_ref, band_ref, q_ref, kc_ref, kp_ref, vc_ref, vp_ref, do_ref, dq_ref, dk_ref, dv_ref, dsk_ref):
        h2, i = pl.program_id(0), pl.program_id(1)

        @pl.when(i == 0)
        def _():
            dk_ref[...] = jnp.zeros_like(dk_ref)
            dv_ref[...] = jnp.zeros_like(dv_ref)
            dsk_ref[...] = jnp.zeros_like(dsk_ref)

        dov = do_ref[...]
        cur = pl.multiple_of(i * WINDOW, WINDOW)
        lane = _iota((8, 128), 1)
        row = _iota((8, 128), 0)
        scale = HEAD ** -0.5
        for hh in range(ATT_HPS):
            q = q_ref[hh].reshape(ATT_G * WINDOW, HEAD)
            do = jnp.concatenate([dov[:, HEAD * (ATT_G * hh + g):HEAD * (ATT_G * hh + g + 1)] for g in range(ATT_G)], axis=0)
            kb = jnp.concatenate([kp_ref[hh], kc_ref[hh]], axis=0)
            vb = jnp.concatenate([vp_ref[hh], vc_ref[hh]], axis=0)
            probs, psink = _attn_probs(q, kb, s_ref, band_ref, ATT_HPS * h2 + hh, i)
            dp = _dot(do, vb, 1, 1)
            delta = jnp.sum(probs * dp, axis=1, keepdims=True)
            ds = (probs * (dp - delta)).astype(BF16)
            dq_ref[hh] = (_dot(ds, kb) * scale).reshape(ATT_G, WINDOW, HEAD)
            dkb = _dot(ds, q, 0, 0) * scale
            dvb = _dot(probs.astype(BF16), do, 0, 0)
            dk_ref[hh, pl.ds(cur, WINDOW), :] += dkb[WINDOW:, :]
            dv_ref[hh, pl.ds(cur, WINDOW), :] += dvb[WINDOW:, :]
            prv = pl.multiple_of(jnp.maximum(i - 1, 0) * WINDOW, WINDOW)
            dk_ref[hh, pl.ds(prv, WINDOW), :] += dkb[:WINDOW, :]
            dv_ref[hh, pl.ds(prv, WINDOW), :] += dvb[:WINDOW, :]

            dsr = -psink * delta
            upd = jnp.zeros((8, 128), F32)
            for gq in range(ATT_G):
                v = jnp.sum(dsr[gq * WINDOW:(gq + 1) * WINDOW, :], axis=0, keepdims=True)
                upd = jnp.where((lane == gq) & (row == 0), v, upd)
            dsk_ref[hh] += upd

    qspec, cur, prev, tok = _attn_specs(S)
    full = pl.BlockSpec((ATT_HPS, S, HEAD), lambda h, i: (h, 0, 0))
    return pl.pallas_call(
        body, name=name, grid=(ATT_KV // ATT_HPS, nb),
        in_specs=[pl.BlockSpec(memory_space=pltpu.SMEM), _BAND, qspec, cur, prev, cur, prev, tok],
        out_specs=[qspec, full, full, pl.BlockSpec((ATT_HPS, 8, 128), lambda h, i: (h, 0, 0))],
        out_shape=[jax.ShapeDtypeStruct((ATT_KV, ATT_G, S, HEAD), F32), jax.ShapeDtypeStruct((ATT_KV, S, HEAD), F32),
                   jax.ShapeDtypeStruct((ATT_KV, S, HEAD), F32), jax.ShapeDtypeStruct((ATT_KV, 8, 128), F32)],
        compiler_params=_cp(2),
    )(sinks, _attn_band(), qh, kh, kh, vh, vh, doh)


def _heads_major(t, nh):
    S = t.shape[0]
    return t.reshape(S, nh, HEAD).transpose(1, 0, 2)


def _tokens_major(t):
    nh, S, _ = t.shape
    return t.transpose(1, 0, 2).reshape(S, nh * HEAD)


class _NoComm:
    def late_start(self, part, after):
        return None

    def late_arrived(self, part, after):
        return None

    def late_weights(self, w, after, part):
        return w

    def advance(self, after, group=None, tensors=None):
        return None


def _local_step(x, posf, target, w, comm=None):
    S, D = x.shape
    gr = {}
    comm = comm or _NoComm()

    zx, h1 = _norm_mm(x, w["a_norm"], w["w_zx"], name="in_proj_zx", dep=w.get("dep"))
    dtr = _mm(h1, w["w_dt"], name="in_proj_dt")
    xbc = _conv_silu_fwd(zx, 2048, 4096, w["a_conv_w"], w["a_conv_b"], name="a_conv_f")
    dt_g, ac_g, sg_g, ac_t = _ssd_prep(dtr, w["a_dt_bias"], w["a_A_log"])
    y_ssd, states = _ssd_fwd(xbc, dt_g, ac_g, ac_t, dep=comm.late_start(1, xbc))
    yg = _gate_fwd(y_ssd, xbc, zx, w["a_Dexp"], w["a_gnorm"], dep=comm.late_arrived(0, [y_ssd]))
    w = comm.late_weights(w, yg, 0)
    x1 = _mm(yg, w["a_out_proj"], res=x, name="out_proj")

    FW = w["f_w_in"][0].shape[2]

    def ffn_fwd(xin, l, loss_target=None):
        u, h = _norm_mm(xin, w["f_norm"][l], w["f_w_in"][l], name=f"f_in{l}", N=N_CHIPS * FW, tn=FW,
                        b_spec=pl.BlockSpec((None, D, FW), lambda i, j: (j, 0, 0)))
        a = _ffn_act_fwd(u, w["f_conv_w"][l], w["f_conv_b"][l], name=f"f_act_f{l}")
        dep = comm.late_arrived(1, [u]) if l == 0 else None
        xo = _mm(a, w["f_w_down"][l], res=xin, tk=a.shape[1], name=f"f_down{l}", target=loss_target, dep=dep)
        return xo, (h, u)

    x2, ffn0 = ffn_fwd(x1, 0)
    w = comm.late_weights(w, x2, 1)

    kv, hk = _norm_mm(x2, w["kv_norm"], w["w_kv"], bias=w["b_kv"], name="kv_proj")
    q, hq = _norm_mm(x2, w["b_norm"], w["w_q"], bias=w["b_q"], name="q_proj")
    rope = w["rope"] if "rope" in w else _rope_cs(posf)
    kr = _hnrope_fwd(kv, 0, 256, w["k_norm_w"], rope, name="k_rope_f")
    qr = _hnrope_fwd(q, 0, 1024, w["q_norm_w"], rope, name="q_rope_f")
    qh = qr.reshape(ATT_KV, ATT_G, S, HEAD)
    kh = kr
    vh = _heads_major(kv[:, 256:].astype(BF16), ATT_KV)
    att = _attn_fwd(qh, kh, vh, w["sinks"])
    x3 = _mm(att, w["w_o"], bias=w["b_o"], res=x2, name="o_proj")
    (dy, loss_part), ffn1 = ffn_fwd(x3, 1, target)

    def ffn_bwd(xin, l, saved, dyo, want_colsum, dep=None):
        h, u = saved
        da = _mm(dyo, w["f_w_down"][l], tb=True, name=f"f_down_dx{l}", dep=dep)
        du, dcw, dcb, a = _ffn_act_bwd(u, w["f_conv_w"][l], w["f_conv_b"][l], da, name=f"f_act_b{l}")
        dw_down = _mm(a, dyo, ta=True, out_dtype=BF16, name=f"f_down_dw{l}")
        dw_in = _mm(h, du, ta=True, out_dtype=BF16, name=f"f_in_dw{l}", dims=(D, N_CHIPS * FW, S), tm=D, tn=FW, tk=S,
                    b_spec=pl.BlockSpec((None, S, FW), lambda i, j, k: (j // 2, 0, j % 2)),
                    o_spec=pl.BlockSpec((None, D, FW), lambda i, j, k: (j, i, 0)), o_shape=(N_CHIPS, D, FW))
        ts = _pick(S, (512, 256))
        pieces = [(pl.BlockSpec((None, ts, FW), lambda i, j, k, q=q: (q // 2, i, q % 2)),
                   pl.BlockSpec((None, D, FW), lambda i, j, k, q=q: (q, 0, 0), pipeline_mode=pl.Buffered(1)))
                  for q in range(N_CHIPS)]
        outs = _mm(du, w["f_w_in"][l], tb=True, name=f"f_in_dx{l}", dims=(S, D, FW), tm=ts, tn=D, tk=FW,
                   a_spec=pieces[0][0], b_spec=pieces[0][1], more=pieces[1:],
                   rms=(xin, w["f_norm"][l], dyo), rms_colsum=want_colsum)
        g = dict(f_norm=outs[1], f_w_in=dw_in, f_conv_w=dcw, f_conv_b=dcb, f_w_down=dw_down)
        return outs[0], g, (outs[2] if want_colsum else None)

    dx3, gr["ffn1"], db_o = ffn_bwd(x3, 1, ffn1, dy, True)
    gr["b_o"] = db_o
    gr["w_o"] = _mm(att, dx3, ta=True, out_dtype=BF16, name="o_proj_dw")
    datt = _mm(dx3, w["w_o"], tb=True, out_dtype=BF16, name="o_proj_dx")
    dqh, dkh, dvh, dsk = _attn_bwd(qh, kh, vh, w["sinks"], datt)
    gr["sinks"] = dsk[:, 0, :4].reshape(1, 16)
    dv = _tokens_major(dvh).astype(BF16)
    dq, db_q, dqn = _hnrope_bwd(q, 0, 1024, w["q_norm_w"], rope, dqh.reshape(16, S, HEAD), name="q_rope_b")
    dk, db_k, dkn = _hnrope_bwd(kv, 0, 256, w["k_norm_w"], rope, dkh, name="k_rope_b")
    gr["q_norm"], gr["k_norm"] = dqn[:, :HEAD], dkn[:, :HEAD]
    gr["b_q"] = db_q
    gr["b_kv"] = jnp.concatenate([db_k, _colsum(dv, name="dv_colsum")], axis=1)
    dkv = jnp.concatenate([dk, dv], axis=1)
    gr["w_q"] = _mm(hq, dq, ta=True, out_dtype=BF16, name="q_proj_dw")
    gr["w_kv"] = _mm(hk, dkv, ta=True, out_dtype=BF16, name="kv_proj_dw")
    tok = comm.advance([gr["w_kv"]], 1, dict(f_down1=gr["ffn1"]["f_w_down"], f_in1=gr["ffn1"]["f_w_in"], w_o=gr["w_o"],
                                             w_q=gr["w_q"], w_kv=gr["w_kv"]))
    tsm = _pick(S, (512, 256))
    dx2, gr["b_norm"] = _mm(dq, w["w_q"], tb=True, name="q_proj_dx", dep=tok, tm=tsm, tn=D, rms=(x2, w["b_norm"], dx3))
    dx2, gr["kv_norm"] = _mm(dkv, w["w_kv"], tb=True, name="kv_proj_dx", tm=tsm, tn=D, rms=(x2, w["kv_norm"], dx2))

    dx1, gr["ffn0"], _ = ffn_bwd(x1, 0, ffn0, dx2, False, dep=comm.advance([dx2]))

    gr["a_out_proj"] = _mm(yg, dx1, ta=True, out_dtype=BF16, name="out_proj_dw")
    tok = comm.advance([dx1, gr["a_out_proj"]], 2,
                       dict(f_down0=gr["ffn0"]["f_w_down"], f_in0=gr["ffn0"]["f_w_in"], out_proj=gr["a_out_proj"]))
    dyg = _mm(dx1, w["a_out_proj"], tb=True, name="out_proj_dx", dep=tok)
    dy_ssd, dz, gr["a_gnorm"], dD = _gate_bwd(y_ssd, xbc, zx, w["a_Dexp"], w["a_gnorm"], dyg)
    gr["a_D"] = dD[:, :SSM_HEADS]
    dxs, dB, dC, dhead = _ssd_bwd(xbc, dt_g, ac_g, ac_t, states, dy_ssd, w["a_Dexp"], dep=comm.advance([dy_ssd]))
    ddtr, dsmall = _ssd_post(dhead, dt_g, sg_g, w["a_A_log_g"])
    gr["a_A_log"] = dsmall[:, 0, :4].reshape(1, SSM_HEADS)
    gr["a_dt_bias"] = dsmall[:, 1, :4].reshape(1, SSM_HEADS)
    dxbc, gr["a_conv_w"], gr["a_conv_b"] = _conv_silu_bwd(
        zx, 2048, 4096, w["a_conv_w"], w["a_conv_b"], [(dxs, 0), (dB, 2048), (dC, 3072)], name="a_conv_b")
    gr["in_proj"] = _in_proj_dw(h1, dz, dxbc, ddtr).T
    tok = comm.advance([dxbc], 3, dict(in_proj=gr["in_proj"].reshape(D, N_CHIPS, -1).transpose(1, 0, 2)))
    ts = _pick(S, (512, 256))
    once = pl.Buffered(1)
    wblk = lambda q: pl.BlockSpec((D, 2048), lambda i, j, k: (0, q), pipeline_mode=once)
    dx0, gr["a_norm"] = _mm(
        dz, w["w_zx"], tb=True, name="in_proj_dx", dims=(S, D, 2048), tm=ts, tn=D, tk=2048,
        a_spec=pl.BlockSpec((ts, 2048), lambda i, j, k: (i, 0)), b_spec=wblk(0),
        more=[(dxbc, pl.BlockSpec((ts, 2048), lambda i, j, k: (i, 0)), w["w_zx"], wblk(1)),
              (dxbc, pl.BlockSpec((ts, 2048), lambda i, j, k: (i, 1)), w["w_zx"], wblk(2)),
              (ddtr, pl.BlockSpec((ts, 128), lambda i, j, k: (i, 0)), w["w_dt"],
               pl.BlockSpec((D, 128), lambda i, j, k: (0, 0), pipeline_mode=once))],
        rms=(x, w["a_norm"], dx1), dep=tok)
    tok = comm.advance([dx0])
    return loss_part, dx0, gr, tok


def _prep_small(full, w):
    w["a_norm"] = full["a_norm"]
    w["a_conv_w"] = full["a_conv_w"][0]
    w["a_conv_b"] = full["a_conv_b"]
    pad32 = lambda v: jnp.pad(v, ((0, 0), (0, 128 - SSM_HEADS)))
    w["a_dt_bias"] = pad32(full["a_dt_bias"])
    w["a_A_log"] = pad32(full["a_A_log"])
    w["a_A_log_g"] = jnp.pad(full["a_A_log"].reshape(SSM_GROUPS, 1, 4), ((0, 0), (0, 0), (0, 124)))
    w["a_Dexp"] = jnp.repeat(full["a_D"], HEAD, axis=1)
    w["a_gnorm"] = full["a_gnorm"]
    w["f_norm"] = [full["f_norm"][l:l + 1] for l in range(2)]
    w["f_conv_w"] = [full["f_conv_w"][l] for l in range(2)]
    w["f_conv_b"] = [full["f_conv_b"][l:l + 1] for l in range(2)]
    w["kv_norm"] = full["kv_norm"].reshape(1, -1)
    w["b_kv"] = full["b_kv"].reshape(1, -1)
    w["k_norm_w"] = jnp.tile(full["k_norm"].reshape(1, HEAD), (1, ATT_KV))
    w["b_norm"] = full["b_norm"]
    w["b_q"] = full["b_q"]
    w["q_norm_w"] = jnp.tile(full["q_norm"], (1, ATT_KV * ATT_G))
    w["sinks"] = full["sinks"].reshape(-1)
    w["b_o"] = full["b_o"]
    return w


def _split_in_proj(ip):
    return ip[:, :6144].astype(BF16), jnp.pad(ip[:, 6144:], ((0, 0), (0, 128 - SSM_HEADS))).astype(BF16)


def _join_in_proj(blocks, *, name="in_proj_join", tr=256):
    _, R, cw = blocks.shape
    zx_cols = 3 * 2048
    rest = N_CHIPS * cw - zx_cols

    def body(b_ref, zx_ref, dt_ref):
        whole = jnp.concatenate([b_ref[j] for j in range(N_CHIPS)], axis=1)
        zx_ref[...] = whole[:, :zx_cols]
        dt_ref[...] = jnp.concatenate([whole[:, zx_cols:], jnp.zeros((tr, 128 - rest), BF16)], axis=1)

    return pl.pallas_call(
        body, name=name, grid=(R // tr,), in_specs=[pl.BlockSpec((N_CHIPS, tr, cw), lambda i: (0, i, 0))],
        out_specs=[pl.BlockSpec((tr, zx_cols), lambda i: (i, 0)), pl.BlockSpec((tr, 128), lambda i: (i, 0))],
        out_shape=[jax.ShapeDtypeStruct((R, zx_cols), BF16), jax.ShapeDtypeStruct((R, 128), BF16)],
        compiler_params=_cp(1),
    )(blocks)


def _in_proj_dw(h, dz, dxbc, ddtr, *, name="in_proj_dw", tn=512):
    S, D = h.shape
    nz, nx = dz.shape[1] // tn, dxbc.shape[1] // tn
    N = dz.shape[1] + dxbc.shape[1] + SSM_HEADS

    def body(h_ref, z_ref, x_ref, t_ref, o_ref):
        j = pl.program_id(0)
        hb = h_ref[...].astype(BF16)

        @pl.when(j < nz)
        def _():
            o_ref[...] = _dot(z_ref[...].astype(BF16), hb, 0, 0).astype(BF16)

        @pl.when((j >= nz) & (j < nz + nx))
        def _():
            o_ref[...] = _dot(x_ref[...].astype(BF16), hb, 0, 0).astype(BF16)

        @pl.when(j == nz + nx)
        def _():
            o_ref[:128, :] = _dot(t_ref[...].astype(BF16), hb, 0, 0).astype(BF16)
            o_ref[128:, :] = jnp.zeros((tn - 128, D), BF16)

    return pl.pallas_call(
        body, name=name, grid=(nz + nx + 1,),
        in_specs=[pl.BlockSpec((S, D), lambda j: (0, 0), pipeline_mode=pl.Buffered(1)),
                  pl.BlockSpec((S, tn), lambda j: (0, jnp.minimum(j, nz - 1))),
                  pl.BlockSpec((S, tn), lambda j: (0, jnp.clip(j - nz, 0, nx - 1))),
                  pl.BlockSpec((S, 128), lambda j: (0, 0))],
        out_specs=pl.BlockSpec((tn, D), lambda j: (j, 0)),
        out_shape=jax.ShapeDtypeStruct((N, D), BF16), compiler_params=_cp(1),
    )(h, dz, dxbc, ddtr)


def _prep_weights(full):
    w = _prep_small(full, {})
    w["w_zx"], w["w_dt"] = _split_in_proj(full["a_in_proj"][0])
    w["a_out_proj"] = full["a_out_proj"][0].astype(BF16)
    w["f_w_in"] = [full["f_w_in"][l].reshape(1024, N_CHIPS, -1).transpose(1, 0, 2).astype(BF16) for l in range(2)]
    w["f_w_down"] = [full["f_w_down"][l].astype(BF16) for l in range(2)]
    w["w_kv"] = full["w_kv"].astype(BF16)
    w["w_q"] = full["w_q"][0].astype(BF16)
    w["w_o"] = full["w_o"][0].astype(BF16)
    return w


def _small_grads(gr):
    g = {}
    g["a_norm"] = gr["a_norm"]
    g["a_conv_w"] = gr["a_conv_w"][None]
    g["a_conv_b"] = gr["a_conv_b"]
    g["a_dt_bias"], g["a_A_log"], g["a_D"] = gr["a_dt_bias"], gr["a_A_log"], gr["a_D"]
    g["a_gnorm"] = gr["a_gnorm"]
    g["kv_norm"] = gr["kv_norm"].reshape(-1)
    g["b_kv"] = gr["b_kv"].reshape(-1)
    g["k_norm"] = gr["k_norm"].reshape(-1)
    g["b_norm"] = gr["b_norm"]
    g["b_q"] = gr["b_q"]
    g["q_norm"] = gr["q_norm"]
    g["sinks"] = gr["sinks"]
    g["b_o"] = gr["b_o"]
    f = [gr["ffn0"], gr["ffn1"]]
    g["f_norm"] = jnp.concatenate([f[0]["f_norm"], f[1]["f_norm"]], axis=0)
    g["f_conv_w"] = jnp.stack([f[l]["f_conv_w"] for l in range(2)])
    g["f_conv_b"] = jnp.concatenate([f[l]["f_conv_b"] for l in range(2)], axis=0)
    return g


def _full_grads(gr):
    g = _small_grads(gr)
    f32 = lambda t: t.astype(F32)
    g["a_in_proj"] = f32(gr["in_proj"])[None]
    g["a_out_proj"] = f32(gr["a_out_proj"])[None]
    g["w_kv"] = f32(gr["w_kv"])
    g["w_q"] = f32(gr["w_q"])[None]
    g["w_o"] = f32(gr["w_o"])[None]
    f = [gr["ffn0"], gr["ffn1"]]
    g["f_w_in"] = jnp.stack([f32(f[l]["f_w_in"]).transpose(1, 0, 2).reshape(1024, -1) for l in range(2)])
    g["f_w_down"] = jnp.stack([f32(f[l]["f_w_down"]) for l in range(2)])
    return g


MESH = pl.DeviceIdType.MESH
WEIGHTS = ("a_norm", "a_in_proj", "a_conv_w", "a_conv_b", "a_dt_bias", "a_A_log", "a_D", "a_gnorm", "a_out_proj",
           "kv_norm", "w_kv", "b_kv", "k_norm", "b_norm", "w_q", "b_q", "q_norm", "sinks", "w_o", "b_o", "f_norm",
           "f_w_in", "f_conv_w", "f_conv_b", "f_w_down")
MATS = (("in_proj", "a_in_proj", 0), ("out_proj", "a_out_proj", 0), ("w_kv", "w_kv", None), ("w_q", "w_q", 0),
        ("w_o", "w_o", 0), ("f_in0", "f_w_in", 0), ("f_in1", "f_w_in", 1), ("f_down0", "f_w_down", 0),
        ("f_down1", "f_w_down", 1))
SMALL_CUT = (("a_norm", 1), ("a_conv_w", 2), ("a_conv_b", 1), ("a_gnorm", 1), ("f_conv_w", 2))
SMALL_REP = ("a_dt_bias", "a_A_log", "a_D", "kv_norm", "b_kv", "k_norm", "b_norm", "b_q", "q_norm", "sinks", "b_o",
             "f_norm", "f_conv_b")


def _coords():
    return lax.axis_index("x"), lax.axis_index("y"), lax.axis_index("c")


def _other_chips(x, y):
    return [(1 - x, y), (x, 1 - y), (1 - x, 1 - y)]


def _pack(arrs, rows_align, lanes, dtype):
    flat = jnp.concatenate([a.reshape(-1).astype(dtype) for a in arrs])
    per = rows_align * lanes
    total = -(-flat.shape[0] // per) * per
    return jnp.pad(flat, (0, total - flat.shape[0])).reshape(total // lanes, lanes)


def _unpack(flat, shapes):
    out, off = [], 0
    for s in shapes:
        n = math.prod(s)
        out.append(flat[off:off + n].reshape(s))
        off += n
    return out


def _remote(src, dst, send, recv, k, dev):
    return pltpu.make_async_remote_copy(src_ref=src, dst_ref=dst, send_sem=send.at[k], recv_sem=recv.at[k],
                                        device_id=dev, device_id_type=MESH)


_ANY = pl.BlockSpec(memory_space=pl.ANY)


def _halves(t):
    r, c = t.shape
    return t.reshape(2, r // 2, c)


RING_SEMS = 9


def _ring_first_copies(refs, send, recv):
    n = (len(refs) - 2) // 2
    sh, sp_ref, outs, sout = refs[:n], refs[n], refs[n + 1:2 * n + 1], refs[2 * n + 1]
    per = RING_SEMS
    x, y, c = _coords()
    me = 2 * x + y
    cx_, cy_, cd_ = _other_chips(x, y)
    sib = (x, y, 1 - c)
    out = [(_remote(sp_ref, sout.at[me], send, recv, per * n + j, (*p, c)),
            _remote(sp_ref, sout.at[2 * p[0] + p[1]], send, recv, per * n + j, (*p, c))) for j, p in enumerate((cx_, cy_, cd_))]
    for t in range(n):
        for k, p in enumerate((cx_, cy_)):
            out.append((_remote(sh[t].at[c], outs[t].at[me, c], send, recv, per * t + k, (*p, c)),
                        _remote(sh[t].at[c], outs[t].at[2 * p[0] + p[1], c], send, recv, per * t + k, (*p, c))))
        own = _remote(sh[t], outs[t].at[me], send, recv, per * t + 8, sib)
        out.append((own, own))
    return out


def _ring_rest(sh, sp_ref, outs, sout, send, recv, loc):
    n, per = len(sh), RING_SEMS
    x, y, c = _coords()
    me = 2 * x + y
    cx_, cy_, cd_ = _other_chips(x, y)
    ix, iy, idg = (2 * p[0] + p[1] for p in (cx_, cy_, cd_))
    to_x, to_y, sib = (*cx_, c), (*cy_, c), (x, y, 1 - c)
    own_small = pltpu.make_async_copy(sp_ref, sout.at[me], loc.at[0])
    own_small.start()
    sends = []

    def go(src, dst, k, dev):
        cp = _remote(src, dst, send, recv, k, dev)
        cp.start()
        sends.append(cp)

    def piece(t, owner, first):
        q = sh[t].shape[1] // 2
        return outs[t].at[owner, c, pl.ds(0 if first else q, q)]

    for t in range(n):
        go(piece(t, ix, False), piece(t, ix, False), per * t + 3, to_y)
        go(outs[t].at[ix, c], outs[t].at[ix, c], per * t + 4, sib)
    for t in range(n):
        go(piece(t, iy, True), piece(t, iy, True), per * t + 2, to_x)
        go(outs[t].at[iy, c], outs[t].at[iy, c], per * t + 5, sib)
    for t in range(n):
        _remote(piece(t, idg, True), piece(t, idg, True), send, recv, per * t + 2, to_x).wait_recv()
        go(piece(t, idg, True), piece(t, idg, True), per * t + 6, sib)
        _remote(piece(t, idg, False), piece(t, idg, False), send, recv, per * t + 3, to_y).wait_recv()
        go(piece(t, idg, False), piece(t, idg, False), per * t + 7, sib)
    for t in range(n):
        q = sh[t].shape[1] // 2
        other = lambda owner, lo=None: outs[t].at[owner, 1 - c] if lo is None else outs[t].at[owner, 1 - c, pl.ds(lo, q)]
        _remote(other(ix), other(ix), send, recv, per * t + 4, sib).wait_recv()
        _remote(other(iy), other(iy), send, recv, per * t + 5, sib).wait_recv()
        _remote(other(idg, 0), other(idg, 0), send, recv, per * t + 6, sib).wait_recv()
        _remote(other(idg, q), other(idg, q), send, recv, per * t + 7, sib).wait_recv()
    for cp in sends:
        cp.wait_send()
    own_small.wait()


def _gather_weights_forward(shards, sp, lands, sland):
    n = len(shards)

    def body(*refs):
        sh, sp_ref = refs[:n], refs[n]
        outs, sout = refs[2 * n + 2:3 * n + 2], refs[3 * n + 2]
        _ring_rest(sh, sp_ref, outs, sout, *refs[3 * n + 3:])

    res = pl.pallas_call(
        body, name="gather_weights_forward", in_specs=[_ANY] * (2 * n + 2), out_specs=[_ANY] * (n + 1),
        out_shape=[jax.ShapeDtypeStruct(a.shape, a.dtype) for a in (*lands, sland)],
        input_output_aliases={n + 1 + t: t for t in range(n + 1)},
        scratch_shapes=[pltpu.SemaphoreType.DMA((RING_SEMS * n,)), pltpu.SemaphoreType.DMA((RING_SEMS * n,)),
                        pltpu.SemaphoreType.DMA((1,))],
    )(*shards, sp, *lands, sland)
    return res[:n], res[n]


_HBM = pl.BlockSpec(memory_space=pltpu.HBM)
_SEMS = pl.BlockSpec(memory_space=pltpu.SEMAPHORE)
_DATAFLOW = pltpu.SideEffectType.DATAFLOW_SIDE_EFFECTING


def _in_hbm(a):
    return pltpu.with_memory_space_constraint(a, pltpu.HBM)


def _start_copies(copies, arrays, n_sem, after, *, name):
    n, na = len(arrays), len(after)

    def body(*refs):
        for mine, _ in copies(refs[:n], refs[n + na], refs[n + na + 1]):
            mine.start()
        refs[-1][...] = jnp.zeros_like(refs[-1])

    res = pl.pallas_call(
        body, name=name, in_specs=[_HBM] * n + [_ANY] * na,
        out_specs=[_SEMS, _SEMS] + [_HBM] * n + [pl.BlockSpec(memory_space=pltpu.VMEM)],
        out_shape=[pltpu.SemaphoreType.DMA((n_sem,)), pltpu.SemaphoreType.DMA((n_sem,))]
        + [pltpu.HBM(a.shape, a.dtype) for a in arrays] + [jax.ShapeDtypeStruct((8, 128), F32)],
        input_output_aliases={t: 2 + t for t in range(n)},
        compiler_params=pltpu.CompilerParams(has_side_effects=_DATAFLOW),
    )(*[_in_hbm(a) for a in arrays], *after)
    return res[0], res[1], list(res[2:2 + n]), res[-1]


def _wait_copies(copies, send, recv, arrays, after, *, name):
    n = len(arrays)

    def body(*refs):
        for mine, theirs in copies(refs[:n], refs[n], refs[n + 1]):
            mine.wait_send()
            theirs.wait_recv()

    return list(pl.pallas_call(
        body, name=name, in_specs=[_HBM] * n + [_SEMS, _SEMS] + [_ANY] * len(after), out_specs=[_HBM] * n,
        out_shape=[pltpu.HBM(a.shape, a.dtype) for a in arrays], input_output_aliases={t: t for t in range(n)},
        compiler_params=pltpu.CompilerParams(has_side_effects=_DATAFLOW),
    )(*arrays, send, recv, *after))


def _sibling_copies(refs, send, recv):
    n = len(refs) // 2
    x, y, c = _coords()
    cps = [_remote(refs[t].at[:, 1 - c], refs[n + t], send, recv, t, (x, y, 1 - c)) for t in range(n)]
    return [(cp, cp) for cp in cps]


def _join_copies(refs, send, recv):
    x, y, c = _coords()
    sib = (x, y, 1 - c)
    return [(_remote(o.at[c], o.at[c], send, recv, t, sib), _remote(o.at[1 - c], o.at[1 - c], send, recv, t, sib))
            for t, o in enumerate(refs)]


def _gather_copies(sh, land, send, recv):
    x, y, c = _coords()
    me = 2 * x + y
    out = []
    for t in range(len(sh)):
        for j, (cx, cy) in enumerate(_other_chips(x, y)):
            dev = (cx, cy, c)
            out.append((_remote(sh[t].at[c], land[t].at[me, c], send, recv, 4 * t + j, dev),
                        _remote(sh[t].at[c], land[t].at[2 * cx + cy, c], send, recv, 4 * t + j, dev)))
        sib = (x, y, 1 - c)
        out.append((_remote(sh[t], land[t].at[me], send, recv, 4 * t + 3, sib),
                    _remote(sh[t], land[t].at[me], send, recv, 4 * t + 3, sib)))
    return out


def _gather_start(shards, after, *, name):
    n = len(shards)

    def body(*refs):
        sh, land = refs[:n], refs[n:2 * n]
        send, recv = refs[2 * n + 1], refs[2 * n + 2]
        token = refs[-1]
        for mine, _ in _gather_copies(sh, land, send, recv):
            mine.start()
        token[...] = jnp.zeros_like(token)

    lands = [_in_hbm(lax.empty((N_CHIPS,) + s.shape, s.dtype)) for s in shards]
    res = pl.pallas_call(
        body, name=name, in_specs=[_HBM] * (2 * n) + [_ANY],
        out_specs=[_SEMS, _SEMS] + [_HBM] * (2 * n) + [pl.BlockSpec(memory_space=pltpu.VMEM)],
        out_shape=[pltpu.SemaphoreType.DMA((4 * n,)), pltpu.SemaphoreType.DMA((4 * n,))]
        + [pltpu.HBM(s.shape, s.dtype) for s in shards] + [pltpu.HBM(l.shape, l.dtype) for l in lands]
        + [jax.ShapeDtypeStruct((8, 128), F32)],
        input_output_aliases={t: 2 + t for t in range(2 * n)},
        compiler_params=pltpu.CompilerParams(has_side_effects=_DATAFLOW),
    )(*[_in_hbm(s) for s in shards], *lands, after)
    return res[0], res[1], res[2:2 + n], res[2 + n:2 + 2 * n], res[-1]


def _forward_copies(refs, send, recv):
    x, y, c = _coords()
    sib = (x, y, 1 - c)
    srcs = [2 * cx + cy for cx, cy in _other_chips(x, y)]
    return [(_remote(o.at[s, c], o.at[s, c], send, recv, 3 * t + j, sib),
             _remote(o.at[s, 1 - c], o.at[s, 1 - c], send, recv, 3 * t + j, sib))
            for t, o in enumerate(refs) for j, s in enumerate(srcs)]


def _gather_arrived_forward(send, recv, shards, lands, after, *, name):
    n = len(shards)
    na = len(after)

    def body(*refs):
        sh, land = refs[:n], refs[n:2 * n]
        for mine, theirs in _gather_copies(sh, land, refs[2 * n], refs[2 * n + 1]):
            mine.wait_send()
            theirs.wait_recv()
        out = refs[2 * n + 2 + na:]
        for mine, _ in _forward_copies(out[2:2 + n], out[0], out[1]):
            mine.start()
        out[-1][...] = jnp.zeros_like(out[-1])

    res = pl.pallas_call(
        body, name=name, in_specs=[_HBM] * (2 * n) + [_SEMS, _SEMS] + [_ANY] * na,
        out_specs=[_SEMS, _SEMS] + [_HBM] * n + [pl.BlockSpec(memory_space=pltpu.VMEM)],
        out_shape=[pltpu.SemaphoreType.DMA((3 * n,)), pltpu.SemaphoreType.DMA((3 * n,))]
        + [pltpu.HBM(l.shape, l.dtype) for l in lands] + [jax.ShapeDtypeStruct((8, 128), F32)],
        input_output_aliases={n + t: 2 + t for t in range(n)},
        compiler_params=pltpu.CompilerParams(has_side_effects=_DATAFLOW),
    )(*shards, *lands, send, recv, *after)
    return res[0], res[1], list(res[2:2 + n]), res[-1]


def _small_copies(v, land, send, recv):
    x, y, c = _coords()
    me = 4 * x + 2 * y + c
    out = []
    for k in range(1, 8):
        px = 1 - x if k & 4 else x
        py = 1 - y if k & 2 else y
        pc = 1 - c if k & 1 else c
        out.append((_remote(v, land.at[me], send, recv, k - 1, (px, py, pc)),
                    _remote(v, land.at[4 * px + 2 * py + pc], send, recv, k - 1, (px, py, pc))))
    return out


def _small_start(v, after, *, name):
    def body(v_ref, land_ref, after_ref, send, recv, v_thru, land_thru, token):
        for mine, _ in _small_copies(v_ref, land_ref, send, recv):
            mine.start()
        token[...] = jnp.zeros_like(token)

    land = _in_hbm(lax.empty((8,) + v.shape, v.dtype))
    return pl.pallas_call(
        body, name=name, in_specs=[_HBM, _HBM, _ANY],
        out_specs=[_SEMS, _SEMS, _HBM, _HBM, pl.BlockSpec(memory_space=pltpu.VMEM)],
        out_shape=[pltpu.SemaphoreType.DMA((7,)), pltpu.SemaphoreType.DMA((7,)), pltpu.HBM(v.shape, v.dtype),
                   pltpu.HBM(land.shape, land.dtype), jax.ShapeDtypeStruct((8, 128), F32)],
        input_output_aliases={0: 2, 1: 3}, compiler_params=pltpu.CompilerParams(has_side_effects=_DATAFLOW),
    )(_in_hbm(v), land, after)


def _small_wait(send, recv, v, land, after, *, name):
    def body(v_ref, land_ref, send_r, recv_r, *rest):
        for mine, theirs in _small_copies(v_ref, land_ref, send_r, recv_r):
            mine.wait_send()
            theirs.wait_recv()

    return pl.pallas_call(
        body, name=name, in_specs=[_HBM, _HBM, _SEMS, _SEMS] + [_ANY] * len(after), out_specs=[_HBM, _HBM],
        out_shape=[pltpu.HBM(v.shape, v.dtype), pltpu.HBM(land.shape, land.dtype)],
        input_output_aliases={0: 0, 1: 1}, compiler_params=pltpu.CompilerParams(has_side_effects=_DATAFLOW),
    )(v, land, send, recv, *after)


def _small_sum(v, land, me_idx, *, name="small_sum"):
    def body(me_ref, v_ref, land_ref, o_ref):
        acc = None
        for s in range(8):
            term = jnp.where(me_ref[0] == s, v_ref[...], land_ref[s])
            acc = term if acc is None else acc + term
        o_ref[...] = acc

    whole = lambda shape: pl.BlockSpec(shape, lambda i, me_ref: (0,) * len(shape))
    return pl.pallas_call(
        body, name=name,
        grid_spec=pltpu.PrefetchScalarGridSpec(num_scalar_prefetch=1, grid=(1,), in_specs=[whole(v.shape), whole(land.shape)],
                                               out_specs=whole(v.shape)),
        out_shape=jax.ShapeDtypeStruct(v.shape, F32), compiler_params=_cp(1),
    )(me_idx, v, land)


RS_ROW_SPLIT = 2


def _rs_add_pair(gs, as_, c_idx, *, name):
    n = len(gs)

    def body(c_ref, *refs):
        for t in range(n):
            refs[2 * n + t][...] = (refs[t][...].astype(F32) + refs[n + t][...].astype(F32)).astype(BF16)

    def gspec(g):
        _, _, rh, cols = g.shape
        return pl.BlockSpec((None, None, rh // RS_ROW_SPLIT, cols), lambda j, i, c_ref: (j, c_ref[0], i, 0))

    def pspec(g):
        _, _, rh, cols = g.shape
        return pl.BlockSpec((None, rh // RS_ROW_SPLIT, cols), lambda j, i, c_ref: (j, i, 0))

    return pl.pallas_call(
        body, name=name,
        grid_spec=pltpu.PrefetchScalarGridSpec(
            num_scalar_prefetch=1, grid=(N_CHIPS, RS_ROW_SPLIT),
            in_specs=[gspec(g) for g in gs] + [pspec(g) for g in gs], out_specs=[pspec(g) for g in gs]),
        out_shape=[jax.ShapeDtypeStruct((N_CHIPS,) + g.shape[2:], BF16) for g in gs], compiler_params=_cp(2),
    )(c_idx, *gs, *as_)


def _chips_copies(p, r, send, recv):
    x, y, c = _coords()
    return [_remote(p[t].at[2 * cx + cy], r[t].at[k], send, recv, 3 * t + k, (cx, cy, c))
            for k, (cx, cy) in enumerate(_other_chips(x, y)) for t in range(len(p))]


def _rs_chips_start(ps, after, *, name):
    n, na = len(ps), len(after)

    def body(*refs):
        p, r = refs[:n], refs[n:2 * n]
        send, recv = refs[2 * n + na], refs[2 * n + na + 1]
        token = refs[-1]
        for cp in _chips_copies(p, r, send, recv):
            cp.start()
        token[...] = jnp.zeros_like(token)

    lands = [_in_hbm(lax.empty((3,) + p.shape[1:], p.dtype)) for p in ps]
    res = pl.pallas_call(
        body, name=name, in_specs=[_HBM] * (2 * n) + [_ANY] * na,
        out_specs=[_SEMS, _SEMS] + [_HBM] * (2 * n) + [pl.BlockSpec(memory_space=pltpu.VMEM)],
        out_shape=[pltpu.SemaphoreType.DMA((3 * n,)), pltpu.SemaphoreType.DMA((3 * n,))]
        + [pltpu.HBM(p.shape, p.dtype) for p in ps] + [pltpu.HBM(l.shape, l.dtype) for l in lands]
        + [jax.ShapeDtypeStruct((8, 128), F32)],
        input_output_aliases={t: 2 + t for t in range(2 * n)},
        compiler_params=pltpu.CompilerParams(has_side_effects=_DATAFLOW),
    )(*[_in_hbm(p) for p in ps], *lands, *after)
    return res[0], res[1], res[2:2 + n], res[2 + n:2 + 2 * n], res[-1]


def _rs_chips_wait(send, recv, ps, lands, after, *, name):
    n = len(ps)

    def body(*refs):
        p, r = refs[:n], refs[n:2 * n]
        for cp in _chips_copies(p, r, refs[2 * n], refs[2 * n + 1]):
            cp.wait_send()
            cp.wait_recv()

    res = pl.pallas_call(
        body, name=name, in_specs=[_HBM] * (2 * n) + [_SEMS, _SEMS] + [_ANY] * len(after), out_specs=[_HBM] * (2 * n),
        out_shape=[pltpu.HBM(p.shape, p.dtype) for p in ps] + [pltpu.HBM(l.shape, l.dtype) for l in lands],
        input_output_aliases={t: t for t in range(2 * n)},
        compiler_params=pltpu.CompilerParams(has_side_effects=_DATAFLOW),
    )(*ps, *lands, send, recv, *after)
    return res[:n], res[n:]


def _rs_add_chips(ps, rs, idx, *, name):
    n = len(ps)

    def body(idx_ref, *refs):
        for t in range(n):
            p_ref, r0, r1, r2 = refs[4 * t:4 * t + 4]
            refs[4 * n + t][...] = ((p_ref[...].astype(F32) + r0[...].astype(F32)) + r1[...].astype(F32)) + r2[...].astype(F32)

    in_specs, args = [], []
    for p, r in zip(ps, rs):
        _, rh, cols = p.shape
        blk = (None, rh // RS_ROW_SPLIT, cols)
        in_specs.append(pl.BlockSpec(blk, lambda i, idx_ref: (idx_ref[0], i, 0)))
        in_specs += [pl.BlockSpec(blk, lambda i, idx_ref, k=k: (k, i, 0)) for k in range(3)]
        args += [p, r, r, r]
    out_specs = [pl.BlockSpec((None, p.shape[1] // RS_ROW_SPLIT, p.shape[2]), lambda i, idx_ref: (idx_ref[1], i, 0))
                 for p in ps]
    return pl.pallas_call(
        body, name=name,
        grid_spec=pltpu.PrefetchScalarGridSpec(num_scalar_prefetch=1, grid=(RS_ROW_SPLIT,), in_specs=in_specs,
                                               out_specs=out_specs),
        out_shape=[jax.ShapeDtypeStruct((2,) + p.shape[1:], F32) for p in ps], compiler_params=_cp(1),
    )(idx, *args)


def _adamw(w, gs, m, v, *, name, dep=None):
    L, Rr, C = w.shape
    tr, tc = _pick(Rr, (256, 128, 64)), C
    if tr == Rr and Rr * C > 512 * 1024:
        tc = 256
    bc1 = 1.0 - ADAM_B1 ** ADAM_STEP
    bc2 = 1.0 - ADAM_B2 ** ADAM_STEP
    nd = 0 if dep is None else 1

    def body(*refs):
        w_ref, m_ref, v_ref = refs[0], refs[1], refs[2]
        g_refs = refs[3:3 + L]
        d_ref, mo_ref, vo_ref, go_ref = refs[3 + L + nd:]
        layer = pl.program_id(0)
        gv = g_refs[0][...]
        for q in range(1, L):
            gv = jnp.where(layer == q, g_refs[q][...], gv)
        mn = ADAM_B1 * m_ref[...] + (1.0 - ADAM_B1) * gv
        vn = ADAM_B2 * v_ref[...] + (1.0 - ADAM_B2) * (gv * gv)
        go_ref[...] = gv
        mo_ref[...] = mn
        vo_ref[...] = vn
        d_ref[...] = -ADAM_LR * ((mn / bc1) / (jnp.sqrt(vn / bc2) + ADAM_EPS) + ADAM_WD * w_ref[...])

    blk = pl.BlockSpec((None, tr, tc), lambda l, i, j: (l, i, j))
    gblks = [pl.BlockSpec((tr, tc), lambda l, i, j, q=q: (jnp.where(l == q, i, 0), jnp.where(l == q, j, 0))) for q in range(L)]
    return pl.pallas_call(
        body, name=name, grid=(L, Rr // tr, C // tc), in_specs=[blk] * 3 + gblks + [_ANY] * nd, out_specs=[blk] * 4,
        out_shape=[jax.ShapeDtypeStruct((L, Rr, C), F32)] * 4, compiler_params=_cp(3),
    )(w, m, v, *gs, *([] if dep is None else [dep]))


def _adamw_leaves(ws, gs, ms, vs, *, name):
    n = len(ws)
    bc1 = 1.0 - ADAM_B1 ** ADAM_STEP
    bc2 = 1.0 - ADAM_B2 ** ADAM_STEP

    def body(*refs):
        w_refs, g_refs, m_refs, v_refs, d_refs, mo_refs, vo_refs = (refs[q * n:(q + 1) * n] for q in range(7))
        for k in range(n):
            gv = g_refs[k][...]
            mn = ADAM_B1 * m_refs[k][...] + (1.0 - ADAM_B1) * gv
            vn = ADAM_B2 * v_refs[k][...] + (1.0 - ADAM_B2) * (gv * gv)
            mo_refs[k][...] = mn
            vo_refs[k][...] = vn
            d_refs[k][...] = -ADAM_LR * ((mn / bc1) / (jnp.sqrt(vn / bc2) + ADAM_EPS) + ADAM_WD * w_refs[k][...])

    in_vmem = pl.BlockSpec(memory_space=pltpu.VMEM)
    outs = pl.pallas_call(
        body, name=name, in_specs=[in_vmem] * (4 * n), out_specs=[in_vmem] * (3 * n),
        out_shape=[jax.ShapeDtypeStruct(t.shape, F32) for t in ws] * 3,
    )(*ws, *gs, *ms, *vs)
    return outs[:n], outs[n:2 * n], outs[2 * n:]


def kernel(x, positions, a_norm, a_in_proj, a_conv_w, a_conv_b, a_dt_bias, a_A_log, a_D, a_gnorm, a_out_proj,
           kv_norm, w_kv, b_kv, k_norm, b_norm, w_q, b_q, q_norm, sinks, w_o, b_o, f_norm, f_w_in, f_conv_w,
           f_conv_b, f_w_down, loss_target, m_a_norm, m_a_in_proj, m_a_conv_w, m_a_conv_b, m_a_dt_bias, m_a_A_log,
           m_a_D, m_a_gnorm, m_a_out_proj, m_kv_norm, m_w_kv, m_b_kv, m_k_norm, m_b_norm, m_w_q, m_b_q, m_q_norm,
           m_sinks, m_w_o, m_b_o, m_f_norm, m_f_w_in, m_f_conv_w, m_f_conv_b, m_f_w_down, v_a_norm, v_a_in_proj,
           v_a_conv_w, v_a_conv_b, v_a_dt_bias, v_a_A_log, v_a_D, v_a_gnorm, v_a_out_proj, v_kv_norm, v_w_kv,
           v_b_kv, v_k_norm, v_b_norm, v_w_q, v_b_q, v_q_norm, v_sinks, v_w_o, v_b_o, v_f_norm, v_f_w_in,
           v_f_conv_w, v_f_conv_b, v_f_w_down):
    wl = dict(zip(WEIGHTS, (a_norm, a_in_proj, a_conv_w, a_conv_b, a_dt_bias, a_A_log, a_D, a_gnorm, a_out_proj,
                            kv_norm, w_kv, b_kv, k_norm, b_norm, w_q, b_q, q_norm, sinks, w_o, b_o, f_norm, f_w_in,
                            f_conv_w, f_conv_b, f_w_down)))
    ml = dict(zip(WEIGHTS, (m_a_norm, m_a_in_proj, m_a_conv_w, m_a_conv_b, m_a_dt_bias, m_a_A_log, m_a_D, m_a_gnorm,
                            m_a_out_proj, m_kv_norm, m_w_kv, m_b_kv, m_k_norm, m_b_norm, m_w_q, m_b_q, m_q_norm,
                            m_sinks, m_w_o, m_b_o, m_f_norm, m_f_w_in, m_f_conv_w, m_f_conv_b, m_f_w_down)))
    vl = dict(zip(WEIGHTS, (v_a_norm, v_a_in_proj, v_a_conv_w, v_a_conv_b, v_a_dt_bias, v_a_A_log, v_a_D, v_a_gnorm,
                            v_a_out_proj, v_kv_norm, v_w_kv, v_b_kv, v_k_norm, v_b_norm, v_w_q, v_b_q, v_q_norm,
                            v_sinks, v_w_o, v_b_o, v_f_norm, v_f_w_in, v_f_conv_w, v_f_conv_b, v_f_w_down)))
    xi, yi, ci = _coords()
    me = 2 * xi + yi
    S = x.shape[1]

    def shard(name, src):
        _, wn, layer = next(m for m in MATS if m[0] == name)
        return _halves((src[wn] if layer is None else src[wn][layer]).astype(BF16))

    rows = lambda t: t.reshape(-1, t.shape[-1])
    c_idx = jnp.reshape(ci, (1,)).astype(jnp.int32)
    me_c = jnp.stack([me, ci]).astype(jnp.int32)
    early = ("in_proj",)
    late = (("out_proj", "f_in0", "f_down0"), ("w_kv", "w_q", "w_o", "f_in1", "f_down1"))

    sp = _pack([wl[n] for n, _ in SMALL_CUT], 8, 128, F32)
    e_sh = [shard(k, wl) for k in early]
    ring = e_sh + [sp] + [lax.empty((N_CHIPS,) + t.shape, t.dtype) for t in e_sh + [sp]]
    g_send, g_recv, ring, g_tok = _start_copies(_ring_first_copies, ring, RING_SEMS * len(e_sh) + 3, [],
                                                name="gather_weights_start")
    anchor = 0.0 * g_tok[0, 0]
    tied = {wn: wl[wn] + anchor for wn in sorted({wn for name, wn, _ in MATS if name not in early})}
    posf = positions.reshape(S, 1).astype(F32) + anchor
    shards = {k: shard(k, tied) for part in late for k in part}
    rope = _rope_cs(posf)
    ring = _wait_copies(_ring_first_copies, g_send, g_recv, ring, [*rope, *shards.values()], name="gather_weights_arrived")
    ne = len(e_sh)
    gathered, gs = _gather_weights_forward(ring[:ne], ring[ne], ring[ne + 1:2 * ne + 1], ring[2 * ne + 1])
    gt = {k: t.reshape(N_CHIPS, -1, t.shape[-1]) for k, t in zip(early, gathered)}
    started = {0: _gather_start([shards[k] for k in late[0]], gs, name="gather_late_start0")}
    full = {n: wl[n] for n in SMALL_REP}
    gs = gs.reshape(N_CHIPS, -1)
    off = 0
    for n, ax in SMALL_CUT:
        shp = wl[n].shape
        size = math.prod(shp)
        piece = jnp.moveaxis(gs[:, off:off + size].reshape((N_CHIPS,) + shp), 0, ax)
        full[n] = piece.reshape(shp[:ax] + (N_CHIPS * shp[ax],) + shp[ax + 1:])
        off += size
    w = _prep_small(full, {})
    w["w_zx"], w["w_dt"] = _join_in_proj(gt["in_proj"])
    w["dep"] = started[0][4]
    w["rope"] = rope

    class Comm:
        flight = []
        reduced = {}

        forwarding = {}

        def late_start(self, part, after):
            started[part] = _gather_start([shards[k] for k in late[part]], after, name=f"gather_late_start{part}")
            return started[part][4]

        def late_arrived(self, part, after):
            send, recv, shs, lands, _ = started[part]
            send, recv, lands, token = _gather_arrived_forward(send, recv, shs, lands, after,
                                                               name=f"gather_late_arrived{part}")
            self.forwarding[part] = (send, recv, lands)
            return token

        def late_weights(self, w, after, part):
            send, recv, lands = self.forwarding[part]
            lands = _wait_copies(_forward_copies, send, recv, lands, [after], name=f"gather_late_forward_wait{part}")
            lt = {k: t.reshape(N_CHIPS, -1, t.shape[-1]) for k, t in zip(late[part], lands)}
            w = dict(w)
            if part == 0:
                w["a_out_proj"], w["f_w_in"], w["f_w_down"] = rows(lt["out_proj"]), [lt["f_in0"]], [rows(lt["f_down0"])]
            else:
                w["w_kv"], w["w_q"], w["w_o"] = (rows(lt[k]) for k in ("w_kv", "w_q", "w_o"))
                w["f_w_in"], w["f_w_down"] = w["f_w_in"] + [lt["f_in1"]], w["f_w_down"] + [rows(lt["f_down1"])]
            return w

        def advance(self, after, group=None, tensors=None):
            token = None
            for grp in list(self.flight):
                tag, n = grp["tag"], len(grp["names"])
                dep = list(after) + ([] if token is None else [token])
                if grp["stage"] == "sibling":
                    arrs = _wait_copies(_sibling_copies, grp["send"], grp["recv"], grp["arrays"], dep, name=f"rs_sibling_wait{tag}")
                    pairs = _rs_add_pair(arrs[:n], arrs[n:], c_idx, name=f"rs_add_pair{tag}")
                    send, recv, ps, lands, token = _rs_chips_start(pairs, dep, name=f"rs_chips_start{tag}")
                    grp.update(stage="chips", send=send, recv=recv, ps=ps, lands=lands)
                elif grp["stage"] == "chips":
                    ps, rs = _rs_chips_wait(grp["send"], grp["recv"], grp["ps"], grp["lands"], dep, name=f"rs_chips_wait{tag}")
                    halves = _rs_add_chips(ps, rs, me_c, name=f"rs_add_chips{tag}")
                    send, recv, arrs, token = _start_copies(_join_copies, halves, n, dep, name=f"rs_join_start{tag}")
                    grp.update(stage="join", send=send, recv=recv, arrays=arrs)
                else:
                    joined = _wait_copies(_join_copies, grp["send"], grp["recv"], grp["arrays"], dep, name=f"rs_join_wait{tag}")
                    self.reduced.update({k: rows(t) for k, t in zip(grp["names"], joined)})
                    self.flight.remove(grp)
            if group is not None:
                names = list(tensors)
                glist = [tensors[k].reshape(N_CHIPS, 2, -1, tensors[k].shape[-1]) for k in names]
                lands = [lax.empty((N_CHIPS,) + gq.shape[2:], gq.dtype) for gq in glist]
                dep = [a for a in after if not any(a is t for t in tensors.values())] + ([] if token is None else [token])
                send, recv, arrs, token = _start_copies(_sibling_copies, glist + lands, len(names), dep,
                                                        name=f"rs_sibling_start{group}")
                self.flight.append(dict(tag=group, names=names, stage="sibling", send=send, recv=recv, arrays=arrs))
            return token

    comm = Comm()

    loss_part, dx0, gr, tok = _local_step(x[0], posf, loss_target[0], w, comm)
    g = _small_grads(gr)

    small_names = [n for n, _ in SMALL_CUT] + list(SMALL_REP)
    sv = _pack([g[n] for n in small_names] + [loss_part[0:1, 0:1]], 8, 128, F32)
    s_send, s_recv, sv, s_land, s_token = _small_start(sv, tok, name="small_start")

    grads, delta, new_m, new_v = {}, {}, {}, {}

    def update(wn, dep):
        gl = [comm.reduced[name] for name, n2, _ in MATS if n2 == wn]
        shp = wl[wn].shape
        three = (len(gl),) + gl[0].shape
        flip = shp[-1] % 128 != 0
        view = (lambda t: t.reshape(three).transpose(0, 2, 1)) if flip else (lambda t: t.reshape(three))
        back = (lambda t: t.transpose(0, 2, 1).reshape(shp)) if flip else (lambda t: t.reshape(shp))
        if flip:
            gl = [t.T for t in gl]
        d, mn, vn, go = _adamw(view(wl[wn]), gl, view(ml[wn]), view(vl[wn]), name="adamw_" + wn, dep=dep)
        grads[wn], delta[wn], new_m[wn], new_v[wn] = back(go), back(d), back(mn), back(vn)
        return d

    first = [update(wn, s_token) for wn in ("w_q", "w_o", "w_kv")]
    second = [update(wn, first[-1]) for wn in ("f_w_in", "f_w_down", "a_out_proj")]
    tok = comm.advance(first + second)
    done = first + second + [tok]

    sv, s_land = _small_wait(s_send, s_recv, sv, s_land, done, name="small_wait")
    sred = _small_sum(sv, s_land, jnp.reshape(2 * me + ci, (1,)).astype(jnp.int32)).reshape(-1)
    small_shapes = [g[n].shape for n in small_names] + [(1,)]
    sg = dict(zip(small_names + ["loss"], _unpack(sred, small_shapes)))
    loss = sg["loss"].reshape(())
    g_small = {}
    for n, ax in SMALL_CUT:
        size = wl[n].shape[ax]
        g_small[n] = lax.dynamic_slice_in_dim(sg[n], me * size, size, axis=ax)
    for n in SMALL_REP:
        g_small[n] = sg[n].reshape(wl[n].shape)

    leaves = lambda d: [d[n].reshape(1, -1) if d[n].ndim == 1 else d[n] for n in small_names]
    ds, mns, vns = _adamw_leaves(leaves(wl), leaves(g_small), leaves(ml), leaves(vl), name="adamw_small")
    comm.advance([ds[0]])
    update("a_in_proj", None)
    for n, dd, mm, vv in zip(small_names, ds, mns, vns):
        shp = wl[n].shape
        grads[n], delta[n], new_m[n], new_v[n] = g_small[n], dd.reshape(shp), mm.reshape(shp), vv.reshape(shp)

    return (loss, dx0[None], *[grads[n] for n in WEIGHTS], *[delta[n] for n in WEIGHTS],
            *[new_m[n] for n in WEIGHTS], *[new_v[n] for n in WEIGHTS])
```

```python
import math

import jax
import jax.numpy as jnp
from jax import lax
from jax.experimental import pallas as pl
from jax.experimental.pallas import tpu as pltpu

F32 = jnp.float32
BF16 = jnp.bfloat16

EPS = 1e-5
CHUNK = 256
WINDOW = 128
HEAD = 64
SSM_HEADS = 32
SSM_GROUPS = 8
SSM_STATE = 128
ATT_KV = 4
ATT_G = 4
ROPE_THETA = 10000.0
NEG = -1e30
N_CHIPS = 4
VMEM_LIMIT = 56 * 1024 * 1024

ADAM_LR, ADAM_B1, ADAM_B2, ADAM_EPS, ADAM_WD, ADAM_STEP = 0.001, 0.9, 0.999, 1e-08, 0.01, 10


def _cp(n_axes):
    return pltpu.CompilerParams(dimension_semantics=("arbitrary",) * n_axes, vmem_limit_bytes=VMEM_LIMIT)


def _pick(dim, prefs):
    for p in prefs:
        if dim % p == 0:
            return p
    return dim


def _iota(shape, dim):
    return lax.broadcasted_iota(jnp.int32, shape, dim)


def _dot(a, b, ca=1, cb=0):
    return lax.dot_general(a, b, (((ca,), (cb,)), ((), ())), preferred_element_type=F32)


def _dot3(x, ind):
    h = x.astype(BF16)
    r = x - h.astype(F32)
    m = r.astype(BF16)
    lo = (r - m.astype(F32)).astype(BF16)
    return _dot(h, ind) + _dot(m, ind) + _dot(lo, ind)


def _sigmoid(x):
    return jax.nn.sigmoid(x)


def _mm(a, b, *, name, ta=False, tb=False, bias=None, res=None, out_dtype=F32, b_koff=0, tm=None, tn=None, tk=None,
        dims=None, a_spec=None, b_spec=None, o_spec=None, o_shape=None, dep=None, more=(), target=None,
        rms=None, rms_colsum=False):
    if dims is not None:
        M, N, K = dims
    else:
        if ta:
            K, M = a.shape
        else:
            M, K = a.shape
        N = b.shape[0] if tb else b.shape[1]
    tm = tm or _pick(M, (1024, 1408, 512, 256, 128))
    tn = tn or _pick(N, (512, 1408, 256, 128))
    tk = tk or (K if K <= 2048 else _pick(K, (2048, 1408, 1024, 512)))
    assert M % tm == 0 and N % tn == 0 and K % tk == 0 and b_koff % tk == 0
    nk = K // tk
    kb0 = b_koff // tk
    has_bias, has_res = bias is not None, res is not None

    def body(*refs):
        a_ref, b_ref = refs[0], refs[1]
        pos = 2
        bias_ref = res_ref = acc_ref = None
        if has_bias:
            bias_ref = refs[pos]
            pos += 1
        if has_res:
            res_ref = refs[pos]
            pos += 1
        if dep is not None:
            pos += 1
        extra = refs[pos:pos + 2 * len(more)]
        pos += 2 * len(more)
        tgt_ref = lp_ref = rx_ref = rg_ref = rd_ref = dg_ref = cs_ref = None
        if target is not None:
            tgt_ref = refs[pos]
            pos += 1
        if rms is not None:
            rx_ref, rg_ref, rd_ref = refs[pos:pos + 3]
            pos += 3
        o_ref = refs[pos]
        pos += 1
        if target is not None:
            lp_ref = refs[pos]
            pos += 1
        if rms is not None:
            dg_ref = refs[pos]
            pos += 1
            if rms_colsum:
                cs_ref = refs[pos]
                pos += 1
        if nk > 1:
            acc_ref = refs[pos]
        part = _dot(a_ref[...].astype(BF16), b_ref[...].astype(BF16), 0 if ta else 1, 1 if tb else 0)
        for q in range(len(more)):
            part = part + _dot(extra[2 * q][...].astype(BF16), extra[2 * q + 1][...].astype(BF16),
                               0 if ta else 1, 1 if tb else 0)

        def finish(acc):
            if has_bias:
                acc = acc + bias_ref[...]
            if has_res:
                acc = acc + res_ref[...]
            if target is not None:
                err = acc - tgt_ref[...]
                acc = err * (1.0 / N)
                part_loss = jnp.sum(jnp.sum(err * err, axis=1, keepdims=True), axis=0, keepdims=True) * (0.5 / N)
                first = (pl.program_id(0) == 0) & (pl.program_id(1) == 0)

                @pl.when(first)
                def _():
                    lp_ref[...] = jnp.broadcast_to(part_loss, lp_ref.shape)

                @pl.when(jnp.logical_not(first))
                def _():
                    lp_ref[...] += jnp.broadcast_to(part_loss, lp_ref.shape)

            if rms is not None:
                xv = rx_ref[...]
                r = lax.rsqrt(jnp.mean(xv * xv, axis=-1, keepdims=True) + EPS)
                xh = xv * r
                dxh = acc * rg_ref[...]
                dg_part = jnp.sum(acc * xh, axis=0, keepdims=True)
                acc = rd_ref[...] + r * (dxh - xh * jnp.mean(dxh * xh, axis=-1, keepdims=True))
                cs_part = jnp.sum(acc, axis=0, keepdims=True) if rms_colsum else None
                first_rows = pl.program_id(0) == 0

                @pl.when(first_rows)
                def _():
                    dg_ref[...] = dg_part
                    if rms_colsum:
                        cs_ref[...] = cs_part

                @pl.when(jnp.logical_not(first_rows))
                def _():
                    dg_ref[...] += dg_part
                    if rms_colsum:
                        cs_ref[...] += cs_part

            o_ref[...] = acc.astype(out_dtype)

        if nk == 1:
            finish(part)
        else:
            k = pl.program_id(2)

            @pl.when(k == 0)
            def _():
                acc_ref[...] = part

            @pl.when(k > 0)
            def _():
                acc_ref[...] += part

            @pl.when(k == nk - 1)
            def _():
                finish(acc_ref[...])

    if a_spec is None:
        a_spec = pl.BlockSpec((tk, tm), lambda i, j, k: (k, i)) if ta else pl.BlockSpec((tm, tk), lambda i, j, k: (i, k))
    if b_spec is None:
        b_spec = (pl.BlockSpec((tn, tk), lambda i, j, k: (j, k + kb0)) if tb
                  else pl.BlockSpec((tk, tn), lambda i, j, k: (k + kb0, j)))
    if o_spec is None:
        o_spec = pl.BlockSpec((tm, tn), lambda i, j, k: (i, j))
    in_specs, args = [a_spec, b_spec], [a, b]
    if has_bias:
        in_specs.append(pl.BlockSpec((1, tn), lambda i, j, k: (0, j)))
        args.append(bias)
    if has_res:
        in_specs.append(pl.BlockSpec((tm, tn), lambda i, j, k: (i, j)))
        args.append(res)
    if dep is not None:
        in_specs.append(pl.BlockSpec(memory_space=pl.ANY))
        args.append(dep)
    for piece in more:
        a2, sa, b2, sb = piece if len(piece) == 4 else (a, piece[0], b, piece[1])
        in_specs += [sa, sb]
        args += [a2, b2]
    out_specs, out_shape = [o_spec], [jax.ShapeDtypeStruct(o_shape or (M, N), out_dtype)]
    if target is not None:
        in_specs.append(pl.BlockSpec((tm, tn), lambda i, j, k: (i, j)))
        args.append(target)
        out_specs.append(pl.BlockSpec((8, 128), lambda i, j, k: (0, 0)))
        out_shape.append(jax.ShapeDtypeStruct((8, 128), F32))
    if rms is not None:
        assert tn == N and nk == 1
        row, vec = pl.BlockSpec((tm, N), lambda i, j, k: (i, 0)), pl.BlockSpec((1, N), lambda i, j, k: (0, 0))
        in_specs += [row, vec, row]
        args += list(rms)
        out_specs += [vec] * (2 if rms_colsum else 1)
        out_shape += [jax.ShapeDtypeStruct((1, N), F32)] * (2 if rms_colsum else 1)
    if len(out_specs) == 1:
        out_specs, out_shape = out_specs[0], out_shape[0]
    return pl.pallas_call(
        body, name=name, grid=(M // tm, N // tn, nk), in_specs=in_specs, out_specs=out_specs, out_shape=out_shape,
        scratch_shapes=[pltpu.VMEM((tm, tn), F32)] if nk > 1 else [],
        compiler_params=_cp(3),
    )(*args)


def _norm_mm(x, gain, b, *, name, bias=None, N=None, tn=None, b_spec=None, dep=None):
    M, K = x.shape
    N = N or b.shape[1]
    tm = _pick(M, (1024, 512, 256))
    tn = tn or _pick(N, (512, 1408, 256, 128))
    has_bias = bias is not None

    def body(*refs):
        x_ref, g_ref, b_ref = refs[:3]
        pos = 3 + (1 if has_bias else 0) + (0 if dep is None else 1)
        o_ref, h_ref = refs[pos], refs[pos + 1]

        @pl.when(pl.program_id(1) == 0)
        def _():
            xv = x_ref[...]
            h_ref[...] = (xv * lax.rsqrt(jnp.mean(xv * xv, axis=-1, keepdims=True) + EPS) * g_ref[...]).astype(BF16)

        acc = _dot(h_ref[...], b_ref[...].astype(BF16))
        if has_bias:
            acc = acc + refs[3][...]
        o_ref[...] = acc

    in_specs = [pl.BlockSpec((tm, K), lambda i, j: (i, 0)), pl.BlockSpec((1, K), lambda i, j: (0, 0)),
                b_spec or pl.BlockSpec((K, tn), lambda i, j: (0, j))]
    args = [x, gain, b]
    if has_bias:
        in_specs.append(pl.BlockSpec((1, tn), lambda i, j: (0, j)))
        args.append(bias)
    if dep is not None:
        in_specs.append(pl.BlockSpec(memory_space=pl.ANY))
        args.append(dep)
    return pl.pallas_call(
        body, name=name, grid=(M // tm, N // tn), in_specs=in_specs,
        out_specs=[pl.BlockSpec((tm, tn), lambda i, j: (i, j)), pl.BlockSpec((tm, K), lambda i, j: (i, 0))],
        out_shape=[jax.ShapeDtypeStruct((M, N), F32), jax.ShapeDtypeStruct((M, K), BF16)], compiler_params=_cp(2),
    )(*args)


def _colsum(x, *, name, tr=256):
    S, D = x.shape

    def body(x_ref, o_ref):
        i = pl.program_id(0)
        part = jnp.sum(x_ref[...].astype(F32), axis=0, keepdims=True)

        @pl.when(i == 0)
        def _():
            o_ref[...] = part

        @pl.when(i > 0)
        def _():
            o_ref[...] += part

    return pl.pallas_call(
        body, name=name, grid=(S // tr,), in_specs=[pl.BlockSpec((tr, D), lambda i: (i, 0))],
        out_specs=pl.BlockSpec((1, D), lambda i: (0, 0)), out_shape=jax.ShapeDtypeStruct((1, D), F32),
        compiler_params=_cp(1),
    )(x)


STRIP = 64
HALO = 8


def _strips(S, tc):
    return [(r0, slice(l0, l0 + 128)) for l0 in range(0, tc, 128) for r0 in range(S - STRIP, -1, -STRIP)]


def _with_halo(ref, r0, ls):
    if r0 == 0:
        return jnp.concatenate([jnp.zeros((HALO, 128), F32), ref[0:STRIP, ls]], axis=0)
    return ref[r0 - HALO:r0 + STRIP, ls]


def _conv_strip(xw, w_ref, b_ref, ls, width):
    acc = b_ref[:, ls] + w_ref[pl.ds(width - 1, 1), ls] * xw[HALO:]
    shifted = []
    for s in range(1, width):
        xs = pltpu.roll(xw, s, axis=0)[HALO:]
        shifted.append(xs)
        acc = acc + w_ref[pl.ds(width - 1 - s, 1), ls] * xs
    return acc, shifted


def _conv_strip_back(dacc, after, xc, shifted, w_ref, ls, width):
    ext = jnp.concatenate([dacc, after], axis=0)
    dx = w_ref[pl.ds(width - 1, 1), ls] * dacc
    dws = [None] * width
    dws[width - 1] = jnp.sum(dacc * xc, axis=0, keepdims=True)
    for s in range(1, width):
        dx = dx + w_ref[pl.ds(width - 1 - s, 1), ls] * pltpu.roll(ext, STRIP + HALO - s, axis=0)[:STRIP]
        dws[width - 1 - s] = jnp.sum(dacc * shifted[s - 1], axis=0, keepdims=True)
    return dx, dws, jnp.sum(dacc, axis=0, keepdims=True)


def _conv_back_block(S, tc, width, w_ref, b_ref, x_ref, dacc_of, dx_store, dw_ref, db_ref):
    for l0 in range(0, tc, 128):
        ls = slice(l0, l0 + 128)
        after = jnp.zeros((HALO, 128), F32)
        tot = None
        for r0 in range(S - STRIP, -1, -STRIP):
            xw = _with_halo(x_ref, r0, ls)
            acc, shifted = _conv_strip(xw, w_ref, b_ref, ls, width)
            dacc = dacc_of(r0, ls, acc, _sigmoid(acc))
            dx, dws, db = _conv_strip_back(dacc, after, xw[HALO:], shifted, w_ref, ls, width)
            dx_store(r0, ls, dx)
            after = dacc[:HALO]
            part = dws + [db]
            tot = part if tot is None else [p + q for p, q in zip(tot, part)]
        for k in range(width):
            dw_ref[pl.ds(k, 1), ls] = tot[k]
        db_ref[:, ls] = tot[width]


def _conv_silu_fwd(xin, col0, C, w, b, *, name, tc=512):
    S = xin.shape[0]
    width = w.shape[0]
    off = col0 // tc

    def body(x_ref, w_ref, b_ref, o_ref):
        for r0, ls in _strips(S, tc):
            acc, _ = _conv_strip(_with_halo(x_ref, r0, ls), w_ref, b_ref, ls, width)
            o_ref[r0:r0 + STRIP, ls] = acc * _sigmoid(acc)

    return pl.pallas_call(
        body, name=name, grid=(C // tc,),
        in_specs=[pl.BlockSpec((S, tc), lambda j: (0, j + off)), pl.BlockSpec((width, tc), lambda j: (0, j)),
                  pl.BlockSpec((1, tc), lambda j: (0, j))],
        out_specs=pl.BlockSpec((S, tc), lambda j: (0, j)), out_shape=jax.ShapeDtypeStruct((S, C), F32),
        compiler_params=_cp(1),
    )(xin, w, b)


def _conv_silu_bwd(xin, col0, C, w, b, douts, *, name, tc=256):
    S = xin.shape[0]
    width = w.shape[0]
    off = col0 // tc
    nd = len(douts)
    ranges = [(o // tc, (o + d.shape[1]) // tc) for d, o in douts]

    def body(*refs):
        x_ref, w_ref, b_ref = refs[0], refs[1], refs[2]
        d_refs = refs[3:3 + nd]
        dx_ref, dw_ref, db_ref = refs[3 + nd], refs[4 + nd], refs[5 + nd]
        j = pl.program_id(0)

        def dacc_of(r0, ls, acc, sg):
            dout = jnp.zeros((STRIP, 128), F32)
            for q in range(nd):
                lo, hi = ranges[q]
                dout = dout + jnp.where((j >= lo) & (j < hi), d_refs[q][r0:r0 + STRIP, ls], 0.0)
            return dout * (sg * (1.0 + acc * (1.0 - sg)))

        def dx_store(r0, ls, dx):
            dx_ref[r0:r0 + STRIP, ls] = dx.astype(BF16)

        _conv_back_block(S, tc, width, w_ref, b_ref, x_ref, dacc_of, dx_store, dw_ref, db_ref)

    d_specs = [pl.BlockSpec((S, tc), (lambda j, lo=lo, hi=hi: (0, jnp.clip(j - lo, 0, hi - lo - 1)))) for lo, hi in ranges]
    return pl.pallas_call(
        body, name=name, grid=(C // tc,),
        in_specs=[pl.BlockSpec((S, tc), lambda j: (0, j + off)), pl.BlockSpec((width, tc), lambda j: (0, j)),
                  pl.BlockSpec((1, tc), lambda j: (0, j))] + d_specs,
        out_specs=[pl.BlockSpec((S, tc), lambda j: (0, j)), pl.BlockSpec((width, tc), lambda j: (0, j)),
                   pl.BlockSpec((1, tc), lambda j: (0, j))],
        out_shape=[jax.ShapeDtypeStruct((S, C), BF16), jax.ShapeDtypeStruct((width, C), F32),
                   jax.ShapeDtypeStruct((1, C), F32)],
        compiler_params=_cp(1),
    )(xin, w, b, *[d for d, _ in douts])


def _ffn_act_fwd(u, w, b, *, name, tc=256):
    S, F2 = u.shape
    Fd = F2 // 2
    width = w.shape[0]
    nb = Fd // tc

    def body(g_ref, v_ref, w_ref, b_ref, o_ref):
        for r0, ls in _strips(S, tc):
            acc, _ = _conv_strip(_with_halo(g_ref, r0, ls), w_ref, b_ref, ls, width)
            o_ref[r0:r0 + STRIP, ls] = (acc * _sigmoid(acc) * v_ref[r0:r0 + STRIP, ls]).astype(BF16)

    return pl.pallas_call(
        body, name=name, grid=(nb,),
        in_specs=[pl.BlockSpec((S, tc), lambda j: (0, j)), pl.BlockSpec((S, tc), lambda j: (0, j + nb)),
                  pl.BlockSpec((width, tc), lambda j: (0, j)), pl.BlockSpec((1, tc), lambda j: (0, j))],
        out_specs=pl.BlockSpec((S, tc), lambda j: (0, j)), out_shape=jax.ShapeDtypeStruct((S, Fd), BF16),
        compiler_params=_cp(1),
    )(u, u, w, b)


def _ffn_act_bwd(u, w, b, da, *, name, tc=256):
    S, F2 = u.shape
    Fd = F2 // 2
    width = w.shape[0]
    nb = Fd // tc

    def body(g_ref, v_ref, w_ref, b_ref, da_ref, du_ref, dw_ref, db_ref, a_ref):
        def dacc_of(r0, ls, acc, sg):
            rs = slice(r0, r0 + STRIP)
            dav, val, silu = da_ref[rs, ls], v_ref[rs, ls], acc * sg
            a_ref[rs, ls] = (silu * val).astype(BF16)
            du_ref[1, rs, ls] = (dav * silu).astype(BF16)
            return dav * val * (sg * (1.0 + acc * (1.0 - sg)))

        def dx_store(r0, ls, dx):
            du_ref[0, r0:r0 + STRIP, ls] = dx.astype(BF16)

        _conv_back_block(S, tc, width, w_ref, b_ref, g_ref, dacc_of, dx_store, dw_ref, db_ref)

    blk = pl.BlockSpec((S, tc), lambda j: (0, j))
    return pl.pallas_call(
        body, name=name, grid=(nb,),
        in_specs=[blk, pl.BlockSpec((S, tc), lambda j: (0, j + nb)), pl.BlockSpec((width, tc), lambda j: (0, j)),
                  pl.BlockSpec((1, tc), lambda j: (0, j)), blk],
        out_specs=[pl.BlockSpec((2, S, tc), lambda j: (0, 0, j)), pl.BlockSpec((width, tc), lambda j: (0, j)),
                   pl.BlockSpec((1, tc), lambda j: (0, j)), blk],
        out_shape=[jax.ShapeDtypeStruct((2, S, Fd), BF16),
                   jax.ShapeDtypeStruct((width, Fd), F32), jax.ShapeDtypeStruct((1, Fd), F32),
                   jax.ShapeDtypeStruct((S, Fd), BF16)],
        compiler_params=_cp(1),
    )(u, u, w, b, da)


def _ssd_prep(dtr, dt_bias, a_log, *, name="ssd_prep"):
    S = dtr.shape[0]

    def body(d_ref, b_ref, al_ref, dt_ref, ac_ref, sg_ref, act_ref):
        lane = _iota((CHUNK, 128), 1)
        valid = lane < SSM_HEADS
        z = d_ref[...] + b_ref[...]
        dt = jnp.where(valid, jnp.maximum(z, 0.0) + jnp.log(1.0 + jnp.exp(-jnp.abs(z))), 0.0)
        a = dt * (-jnp.exp(al_ref[...]))
        row = _iota((CHUNK, 128), 0)
        k = 1
        while k < CHUNK:
            a = a + jnp.where(row >= k, pltpu.roll(a, k, axis=0), 0.0)
            k *= 2
        sg = jnp.where(valid, _sigmoid(z), 0.0)
        for arr, ref in ((dt, dt_ref), (a, ac_ref), (sg, sg_ref)):
            for g in range(SSM_GROUPS):
                ref[g] = jnp.where(lane < 4, arr if g == 0 else pltpu.roll(arr, 128 - 4 * g, axis=1), 0.0)
        act_ref[...] = a.T[:SSM_HEADS, :]

    blk = pl.BlockSpec((CHUNK, 128), lambda i: (i, 0))
    vec = pl.BlockSpec((1, 128), lambda i: (0, 0))
    grp = pl.BlockSpec((SSM_GROUPS, CHUNK, 128), lambda i: (0, i, 0))
    return pl.pallas_call(
        body, name=name, grid=(S // CHUNK,), in_specs=[blk, vec, vec],
        out_specs=[grp, grp, grp, pl.BlockSpec((SSM_HEADS, CHUNK), lambda i: (0, i))],
        out_shape=[jax.ShapeDtypeStruct((SSM_GROUPS, S, 128), F32)] * 3 + [jax.ShapeDtypeStruct((SSM_HEADS, S), F32)],
        compiler_params=_cp(1),
    )(dtr, dt_bias, a_log)


SSD_GPS = 4


def _expand4(v, lanes):
    out = jnp.broadcast_to(v[:, 3:4], lanes.shape)
    for hh in (2, 1, 0):
        out = jnp.where(lanes < 64 * (hh + 1), v[:, hh:hh + 1], out)
    return out


def _ssd_fwd(xbc, dt_g, ac_g, ac_t, *, name="ssd_fwd", dep=None):
    S = xbc.shape[0]
    nc = S // CHUNK
    Lc = CHUNK

    def body(x_ref, b_ref, c_ref, dt_ref, ac_ref, act_ref, *rest):
        y_ref, st_out_ref, st_ref = rest[-3:]
        g2 = pl.program_id(0)
        c = pl.program_id(1)

        @pl.when(c == 0)
        def _():
            st_ref[...] = jnp.zeros_like(st_ref)

        causal = _iota((Lc, Lc), 0) >= _iota((Lc, Lc), 1)
        lane256 = _iota((Lc, 256), 1)
        lane128 = _iota((Lc, 128), 1)
        row128 = _iota((128, 128), 0)
        for gg in range(SSD_GPS):
            g = SSD_GPS * g2 + gg
            bv = b_ref[:, 128 * gg:128 * (gg + 1)]
            cbf = c_ref[:, 128 * gg:128 * (gg + 1)].astype(BF16)
            cb = _dot(cbf, bv.astype(BF16), 1, 1)
            dtg, acg = dt_ref[gg], ac_ref[gg]
            ac_last = ac_ref[gg, pl.ds(Lc - 1, 1), :]
            dt4 = _expand4(dtg, lane256)
            ac4 = _expand4(acg, lane256)
            e4 = jnp.exp(ac4)
            xdb = (x_ref[:, 256 * gg:256 * (gg + 1)] * dt4).astype(BF16)
            st_out_ref[gg] = st_ref[gg]
            for p in range(2):
                xd_p = xdb[:, 128 * p:128 * (p + 1)]
                st_p = st_ref[gg, p]
                ys, sn, cds = [], [], []
                for q in range(2):
                    hh = 2 * p + q
                    a_col = acg[:, hh:hh + 1]
                    a_row = act_ref[pl.ds(4 * g + hh, 1), :]
                    dec = jnp.exp(jnp.where(causal, a_col - a_row, NEG))
                    w = (cb * dec).astype(BF16)
                    ys.append(_dot(w, xd_p))
                    al = ac_last[:, hh:hh + 1]
                    dte = jnp.exp(al - a_col)
                    sn.append(_dot(xd_p, (bv * dte).astype(BF16), 0, 0))
                    cds.append(jnp.exp(al))
                y_diag = jnp.where(lane128 < 64, ys[0], ys[1])
                y_off = _dot(cbf, st_p.astype(BF16), 1, 1) * e4[:, 128 * p:128 * (p + 1)]
                y_ref[:, 256 * gg + 128 * p:256 * gg + 128 * (p + 1)] = y_diag + y_off
                st_ref[gg, p] = jnp.where(row128 < 64, st_p * cds[0] + sn[0], st_p * cds[1] + sn[1])

    G = SSD_GPS
    per_g = lambda g, c: (g, c, 0)
    return pl.pallas_call(
        body, name=name, grid=(SSM_GROUPS // G, nc),
        in_specs=[pl.BlockSpec((Lc, 256 * G), lambda g, c: (c, g)),
                  pl.BlockSpec((Lc, 128 * G), lambda g, c: (c, 16 // G + g)),
                  pl.BlockSpec((Lc, 128 * G), lambda g, c: (c, 24 // G + g)),
                  pl.BlockSpec((G, Lc, 128), per_g), pl.BlockSpec((G, Lc, 128), per_g),
                  pl.BlockSpec((SSM_HEADS, Lc), lambda g, c: (0, c))] + ([] if dep is None else [pl.BlockSpec(memory_space=pl.ANY)]),
        out_specs=[pl.BlockSpec((Lc, 256 * G), lambda g, c: (c, g)),
                   pl.BlockSpec((G, None, 2, 128, 128), lambda g, c: (g, c, 0, 0, 0))],
        out_shape=[jax.ShapeDtypeStruct((S, 2048), F32), jax.ShapeDtypeStruct((SSM_GROUPS, nc, 2, 128, 128), F32)],
        scratch_shapes=[pltpu.VMEM((G, 2, 128, 128), F32)], compiler_params=_cp(2),
    )(xbc, xbc, xbc, dt_g, ac_g, ac_t, *([] if dep is None else [dep]))


def _ssd_bwd(xbc, dt_g, ac_g, ac_t, states, dy, dexp, *, name="ssd_bwd", dep=None):
    S = xbc.shape[0]
    nc = S // CHUNK
    Lc = CHUNK

    def body(x_ref, b_ref, c_ref, dt_ref, ac_ref, act_ref, st_ref, dy_ref, d_ref, *rest):
        dx_ref, db_ref, dc_ref, dh_ref, ds_ref = rest[-5:]
        g2 = pl.program_id(0)
        cc = pl.program_id(1)

        @pl.when(cc == 0)
        def _():
            ds_ref[...] = jnp.zeros_like(ds_ref)

        causal = _iota((Lc, Lc), 0) >= _iota((Lc, Lc), 1)
        lane256 = _iota((Lc, 256), 1)
        lane128 = _iota((Lc, 128), 1)
        row128 = _iota((128, 128), 0)
        ind_rows = _iota((256, 128), 0) >> 6
        ind_cols = _iota((256, 128), 1)
        ind_a = (ind_rows == ind_cols).astype(BF16)
        ind_b = (ind_rows + 4 == ind_cols).astype(BF16)
        for gg in range(SSD_GPS):
            g = SSD_GPS * g2 + gg
            bv = b_ref[:, 128 * gg:128 * (gg + 1)]
            cv = c_ref[:, 128 * gg:128 * (gg + 1)]
            bbf, cbf = bv.astype(BF16), cv.astype(BF16)
            cb = _dot(cbf, bbf, 1, 1)
            dtg, acg = dt_ref[gg], ac_ref[gg]
            ac_last = ac_ref[gg, pl.ds(Lc - 1, 1), :]
            dt4 = _expand4(dtg, lane256)
            ac4 = _expand4(acg, lane256)
            acl4 = _expand4(ac_last, _iota((1, 256), 1))
            e4 = jnp.exp(ac4)
            dte4 = jnp.exp(acl4 - ac4)
            xv = x_ref[:, 256 * gg:256 * (gg + 1)]
            xd = xv * dt4
            xdb = xd.astype(BF16)
            dyv = dy_ref[:, 256 * gg:256 * (gg + 1)]
            dcb = jnp.zeros((Lc, Lc), F32)
            dc_acc = jnp.zeros((Lc, 128), F32)
            db_acc = jnp.zeros((Lc, 128), F32)
            u_parts, dxd_parts, ends = [], [], []
            for p in range(2):
                sl = slice(128 * p, 128 * (p + 1))
                xd_p, xdb_p, dy_p = xd[:, sl], xdb[:, sl], dyv[:, sl]
                dyb_p = dy_p.astype(BF16)
                e_p, dte_p = e4[:, sl], dte4[:, sl]
                sp = st_ref[gg, p]
                spb = sp.astype(BF16)
                dsn = ds_ref[gg, p]
                dsnb = dsn.astype(BF16)
                yds, dxds, cds = [], [], []
                for q in range(2):
                    hh = 2 * p + q
                    a_col = acg[:, hh:hh + 1]
                    a_row = act_ref[pl.ds(4 * g + hh, 1), :]
                    dec = jnp.exp(jnp.where(causal, a_col - a_row, NEG))
                    w = (cb * dec).astype(BF16)
                    head = (lane128 < 64) if q == 0 else (lane128 >= 64)
                    dym = jnp.where(head, dyb_p, jnp.zeros_like(dyb_p))
                    dw = _dot(dym, xdb_p, 1, 1)
                    dcb = dcb + dw * dec
                    yds.append(_dot(w, xdb_p))
                    dxds.append(_dot(w, dyb_p, 0, 0))
                    cds.append(jnp.exp(ac_last[:, hh:hh + 1]))
                y_diag = jnp.where(lane128 < 64, yds[0], yds[1])
                dxd_diag = jnp.where(lane128 < 64, dxds[0], dxds[1])
                y_off = _dot(cbf, spb, 1, 1) * e_p
                dgp = dy_p * e_p
                dgb = dgp.astype(BF16)
                dc_acc = dc_acc + _dot(dgb, spb)
                dsp = _dot(dgb, cbf, 0, 0)
                cd_col = jnp.where(row128[:, 0:1] < 64, cds[0], cds[1])
                qm = _dot(bbf, dsnb, 1, 1)
                dxd_state = dte_p * qm
                db_acc = db_acc + _dot((xd_p * dte_p).astype(BF16), dsnb)
                t_p = xd_p * dxd_state
                prod = dsn * sp
                e0 = jnp.sum(jnp.sum(jnp.where(row128 < 64, prod, 0.0), axis=1, keepdims=True), axis=0, keepdims=True)
                e1 = jnp.sum(jnp.sum(jnp.where(row128 >= 64, prod, 0.0), axis=1, keepdims=True), axis=0, keepdims=True)
                tcol = jnp.sum(t_p, axis=0, keepdims=True)
                lane1 = _iota((1, 128), 1)
                t0 = jnp.sum(jnp.where(lane1 < 64, tcol, 0.0), axis=1, keepdims=True)
                t1 = jnp.sum(jnp.where(lane1 >= 64, tcol, 0.0), axis=1, keepdims=True)
                ends.append(e0 * cds[0] + t0)
                ends.append(e1 * cds[1] + t1)
                ds_ref[gg, p] = dsn * cd_col + dsp
                u_parts.append(dyb_p.astype(F32) * y_diag - xdb_p.astype(F32) * dxd_diag + dy_p * y_off - t_p)
                dxd_parts.append(dxd_diag + dxd_state)
            dxd = jnp.concatenate(dxd_parts, axis=1)
            u_all = jnp.concatenate(u_parts, axis=1)
            dx_ref[:, 256 * gg:256 * (gg + 1)] = dxd * dt4 + dyv * d_ref[:, 256 * gg:256 * (gg + 1)]
            dcbb = dcb.astype(BF16)
            dc_ref[:, 128 * gg:128 * (gg + 1)] = dc_acc + _dot(dcbb, bbf)
            db_ref[:, 128 * gg:128 * (gg + 1)] = db_acc + _dot(dcbb, cbf, 0, 0)
            lane = _iota((Lc, 128), 1)
            endv = jnp.zeros((Lc, 128), F32)
            for hh in range(4):
                endv = jnp.where(lane == 8 + hh, ends[hh], endv)
            dh_ref[gg] = _dot3(dxd * xv, ind_a) + _dot3(u_all, ind_b) + endv

    G = SSD_GPS
    rev = lambda c: nc - 1 - c
    per_g = lambda g, c: (g, rev(c), 0)
    return pl.pallas_call(
        body, name=name, grid=(SSM_GROUPS // G, nc),
        in_specs=[pl.BlockSpec((Lc, 256 * G), lambda g, c: (rev(c), g)),
                  pl.BlockSpec((Lc, 128 * G), lambda g, c: (rev(c), 16 // G + g)),
                  pl.BlockSpec((Lc, 128 * G), lambda g, c: (rev(c), 24 // G + g)),
                  pl.BlockSpec((G, Lc, 128), per_g), pl.BlockSpec((G, Lc, 128), per_g),
                  pl.BlockSpec((SSM_HEADS, Lc), lambda g, c: (0, rev(c))),
                  pl.BlockSpec((G, None, 2, 128, 128), lambda g, c: (g, rev(c), 0, 0, 0)),
                  pl.BlockSpec((Lc, 256 * G), lambda g, c: (rev(c), g)),
                  pl.BlockSpec((1, 256 * G), lambda g, c: (0, g))] + ([] if dep is None else [pl.BlockSpec(memory_space=pl.ANY)]),
        out_specs=[pl.BlockSpec((Lc, 256 * G), lambda g, c: (rev(c), g)),
                   pl.BlockSpec((Lc, 128 * G), lambda g, c: (rev(c), g)),
                   pl.BlockSpec((Lc, 128 * G), lambda g, c: (rev(c), g)),
                   pl.BlockSpec((G, Lc, 128), per_g)],
        out_shape=[jax.ShapeDtypeStruct((S, 2048), F32), jax.ShapeDtypeStruct((S, 1024), F32),
                   jax.ShapeDtypeStruct((S, 1024), F32), jax.ShapeDtypeStruct((SSM_GROUPS, S, 128), F32)],
        scratch_shapes=[pltpu.VMEM((G, 2, 128, 128), F32)], compiler_params=_cp(2),
    )(xbc, xbc, xbc, dt_g, ac_g, ac_t, states, dy, dexp, *([] if dep is None else [dep]))


def _ssd_post(dhead, dt_g, sg_g, alog_g, *, name="ssd_post"):
    S = dhead.shape[1]
    nc = S // CHUNK
    Lc = CHUNK

    def body(dh_ref, dt_ref, sg_ref, al_ref, o_ref, s_ref):
        @pl.when(pl.program_id(0) == 0)
        def _():
            s_ref[...] = jnp.zeros_like(s_ref)

        lane = _iota((Lc, 128), 1)
        row = _iota((Lc, 128), 0)
        row8 = _iota((8, 128), 0)
        out = jnp.zeros((Lc, 128), F32)
        for g in range(SSM_GROUPS):
            dh = dh_ref[g]
            a_neg = -jnp.exp(al_ref[g])
            dac = jnp.where(lane < 4, pltpu.roll(dh, 124, axis=1), 0.0)
            end = jnp.where(lane < 4, pltpu.roll(dh, 120, axis=1), 0.0)
            k = 1
            while k < Lc:
                dac = dac + jnp.where(row < Lc - k, pltpu.roll(dac, Lc - k, axis=0), 0.0)
                k *= 2
            da = dac + end
            ddt = jnp.where(lane < 4, da * a_neg + dh, 0.0)
            ddtr = ddt * sg_ref[g]
            out = out + (ddtr if g == 0 else pltpu.roll(ddtr, 4 * g, axis=1))
            dal = jnp.sum(da * dt_ref[g], axis=0, keepdims=True) * a_neg
            dbias = jnp.sum(ddtr, axis=0, keepdims=True)
            part = jnp.where(row8 == 0, dal, jnp.where(row8 == 1, dbias, 0.0))
            s_ref[g] += part
        o_ref[...] = out.astype(BF16)

    grp = pl.BlockSpec((SSM_GROUPS, Lc, 128), lambda c: (0, c, 0))
    whole = lambda r: pl.BlockSpec((SSM_GROUPS, r, 128), lambda c: (0, 0, 0))
    return pl.pallas_call(
        body, name=name, grid=(nc,), in_specs=[grp, grp, grp, whole(1)],
        out_specs=[pl.BlockSpec((Lc, 128), lambda c: (c, 0)), whole(8)],
        out_shape=[jax.ShapeDtypeStruct((S, 128), BF16), jax.ShapeDtypeStruct((SSM_GROUPS, 8, 128), F32)],
        compiler_params=_cp(1),
    )(dhead, dt_g, sg_g, alog_g)


def _gate_fwd(y, xbc, zx, dexp, gn, *, name="gate_fwd", tr=256, dep=None):
    S = y.shape[0]
    W = 2048
    gw = W // SSM_GROUPS

    def body(y_ref, x_ref, z_ref, d_ref, g_ref, *rest):
        o_ref = rest[-1]
        z = z_ref[...]
        u = (y_ref[...] + x_ref[...] * d_ref[...]) * (z * _sigmoid(z))
        gv = g_ref[...]
        for q in range(SSM_GROUPS):
            sl = slice(gw * q, gw * (q + 1))
            uq = u[:, sl]
            r = lax.rsqrt(jnp.mean(uq * uq, axis=-1, keepdims=True) + EPS)
            o_ref[:, sl] = (uq * r * gv[:, sl]).astype(BF16)

    row = pl.BlockSpec((tr, W), lambda i: (i, 0))
    vec = pl.BlockSpec((1, W), lambda i: (0, 0))
    return pl.pallas_call(
        body, name=name, grid=(S // tr,),
        in_specs=[row, row, row, vec, vec] + ([] if dep is None else [pl.BlockSpec(memory_space=pl.ANY)]), out_specs=row,
        out_shape=jax.ShapeDtypeStruct((S, W), BF16), compiler_params=_cp(1),
    )(y, xbc, zx, dexp, gn, *([] if dep is None else [dep]))


def _gate_bwd(y, xbc, zx, dexp, gn, dout, *, name="gate_bwd", tr=256):
    S = y.shape[0]
    W = 2048
    gw = W // SSM_GROUPS
    steps = S // tr

    def body(y_ref, x_ref, z_ref, d_ref, g_ref, do_ref, dy_ref, dz_ref, dg_ref, dd_ref, acc_ref):
        i = pl.program_id(0)

        @pl.when(i == 0)
        def _():
            acc_ref[...] = jnp.zeros_like(acc_ref)

        z = z_ref[...]
        sg = _sigmoid(z)
        sz = z * sg
        xs = x_ref[...]
        yt = y_ref[...] + xs * d_ref[...]
        u = yt * sz
        gv = g_ref[...]
        do = do_ref[...]
        dgs = []
        for q in range(SSM_GROUPS):
            sl = slice(gw * q, gw * (q + 1))
            uq = u[:, sl]
            r = lax.rsqrt(jnp.mean(uq * uq, axis=-1, keepdims=True) + EPS)
            uh = uq * r
            dq = do[:, sl]
            duh = dq * gv[:, sl]
            duq = r * (duh - uh * jnp.mean(duh * uh, axis=-1, keepdims=True))
            dgs.append(jnp.sum(dq * uh, axis=0, keepdims=True))
            dyt = duq * sz[:, sl]
            dy_ref[:, sl] = dyt
            dz_ref[:, sl] = (duq * yt[:, sl] * (sg[:, sl] * (1.0 + z[:, sl] * (1.0 - sg[:, sl])))).astype(BF16)
            acc_ref[:, sl] += jnp.sum(dyt * xs[:, sl], axis=0, keepdims=True)
        dg = jnp.concatenate(dgs, axis=1)

        @pl.when(i == 0)
        def _():
            dg_ref[...] = dg

        @pl.when(i > 0)
        def _():
            dg_ref[...] += dg

        @pl.when(i == steps - 1)
        def _():
            ind = ((_iota((W, 128), 0) >> 6) == _iota((W, 128), 1)).astype(BF16)
            dd_ref[...] = _dot3(jnp.broadcast_to(acc_ref[...], (8, W)), ind)[0:1, :]

    row = pl.BlockSpec((tr, W), lambda i: (i, 0))
    vec = pl.BlockSpec((1, W), lambda i: (0, 0))
    return pl.pallas_call(
        body, name=name, grid=(steps,), in_specs=[row, row, row, vec, vec, row],
        out_specs=[row, row, vec, pl.BlockSpec((1, 128), lambda i: (0, 0))],
        out_shape=[jax.ShapeDtypeStruct((S, W), F32), jax.ShapeDtypeStruct((S, W), BF16),
                   jax.ShapeDtypeStruct((1, W), F32), jax.ShapeDtypeStruct((1, 128), F32)],
        scratch_shapes=[pltpu.VMEM((1, W), F32)], compiler_params=_cp(1),
    )(y, xbc, zx, dexp, gn, dout)


def _rope_cs(posf, *, name="rope_tables", tr=256):
    S = posf.shape[0]

    def body(p_ref, c_ref, s_ref):
        j = (_iota((tr, 128), 1) & 31).astype(F32)
        ang = p_ref[...] * jnp.exp(j * (-math.log(ROPE_THETA) / 32.0))
        c_ref[...] = jnp.cos(ang)
        s_ref[...] = jnp.sin(ang)

    blk = pl.BlockSpec((tr, 128), lambda i: (i, 0))
    return pl.pallas_call(
        body, name=name, grid=(S // tr,), in_specs=[pl.BlockSpec((tr, 1), lambda i: (i, 0))], out_specs=[blk, blk],
        out_shape=[jax.ShapeDtypeStruct((S, 128), F32)] * 2, compiler_params=_cp(1),
    )(posf)


def _rope_tables(c_ref, s_ref, shape):
    reps = shape[1] // 128
    return jnp.tile(c_ref[...], (1, reps)), jnp.tile(s_ref[...], (1, reps)), (_iota(shape, 1) & 63) < 32


def _hn_inds(W):
    ind = ((_iota((W, 128), 0) >> 6) == _iota((W, 128), 1)).astype(BF16)
    ind_t = ((_iota((128, W), 1) >> 6) == _iota((128, W), 0)).astype(BF16)
    return ind, ind_t


def _hnrope_fwd(xin, col0, W, gain_w, rope, *, name, tr=256, plain_col=None):
    S = xin.shape[0]
    off = col0 // W
    nh = W // HEAD
    plain = plain_col is not None

    def body(*refs):
        x_ref, g_ref, c_ref, s_ref = refs[:4]
        o_ref = refs[5 if plain else 4]
        if plain:
            pv = refs[4][...].astype(BF16)
            for h in range(nh):
                refs[6][h] = pv[:, HEAD * h:HEAD * (h + 1)]
        x = x_ref[...]
        ind, ind_t = _hn_inds(W)
        r = lax.rsqrt(_dot3(x * x, ind) * (1.0 / HEAD) + EPS)
        xn = x * _dot3(r, ind_t) * g_ref[...]
        cs, sn, half = _rope_tables(c_ref, s_ref, (tr, W))
        rot = jnp.where(half, -pltpu.roll(xn, W - 32, axis=1), pltpu.roll(xn, 32, axis=1))
        out = (xn * cs + rot * sn).astype(BF16)
        for h in range(nh):
            o_ref[h] = out[:, HEAD * h:HEAD * (h + 1)]

    tab = pl.BlockSpec((tr, 128), lambda i: (i, 0))
    heads = pl.BlockSpec((nh, tr, HEAD), lambda i: (0, i, 0))
    shape = jax.ShapeDtypeStruct((nh, S, HEAD), BF16)
    extra = [pl.BlockSpec((tr, W), lambda i: (i, plain_col // W))] if plain else []
    return pl.pallas_call(
        body, name=name, grid=(S // tr,),
        in_specs=[pl.BlockSpec((tr, W), lambda i: (i, off)), pl.BlockSpec((1, W), lambda i: (0, 0)), tab, tab] + extra,
        out_specs=[heads, heads] if plain else heads, out_shape=[shape, shape] if plain else shape,
        compiler_params=_cp(1),
    )(xin, gain_w, *rope, *([xin] if plain else []))


def _hnrope_bwd(xin, col0, W, gain_w, rope, dout, *, name, tr=256):
    S = xin.shape[0]
    off = col0 // W
    steps = S // tr
    nh = W // HEAD

    def body(x_ref, g_ref, c_ref, s_ref, do_ref, dx_ref, cs_ref, dg_ref, acc_ref):
        i = pl.program_id(0)
        x = x_ref[...]
        ind, ind_t = _hn_inds(W)
        r = lax.rsqrt(_dot3(x * x, ind) * (1.0 / HEAD) + EPS)
        rw = _dot3(r, ind_t)
        xh = x * rw
        cs, sn, half = _rope_tables(c_ref, s_ref, (tr, W))
        do = jnp.concatenate([do_ref[h] for h in range(nh)], axis=1).astype(F32)
        gs = do * sn
        g1 = do * cs + jnp.where(half, pltpu.roll(gs, W - 32, axis=1), -pltpu.roll(gs, 32, axis=1))
        dxh = g1 * g_ref[...]
        t = _dot3(dxh * xh, ind) * (1.0 / HEAD)
        dx = rw * (dxh - xh * _dot3(t, ind_t))
        dx_ref[...] = dx.astype(BF16)
        cpart = jnp.sum(dx, axis=0, keepdims=True)
        gpart = jnp.sum(g1 * xh, axis=0, keepdims=True)

        @pl.when(i == 0)
        def _():
            cs_ref[...] = cpart
            acc_ref[...] = gpart

        @pl.when(i > 0)
        def _():
            cs_ref[...] += cpart
            acc_ref[...] += gpart

        @pl.when(i == steps - 1)
        def _():
            fold = ((_iota((W, 128), 0) & 63) == _iota((W, 128), 1)).astype(BF16)
            dg_ref[...] = _dot3(jnp.broadcast_to(acc_ref[...], (8, W)), fold)[0:1, :]

    tab = pl.BlockSpec((tr, 128), lambda i: (i, 0))
    return pl.pallas_call(
        body, name=name, grid=(steps,),
        in_specs=[pl.BlockSpec((tr, W), lambda i: (i, off)), pl.BlockSpec((1, W), lambda i: (0, 0)), tab, tab,
                  pl.BlockSpec((nh, tr, HEAD), lambda i: (0, i, 0))],
        out_specs=[pl.BlockSpec((tr, W), lambda i: (i, 0)), pl.BlockSpec((1, W), lambda i: (0, 0)),
                   pl.BlockSpec((1, 128), lambda i: (0, 0))],
        out_shape=[jax.ShapeDtypeStruct((S, W), BF16), jax.ShapeDtypeStruct((1, W), F32),
                   jax.ShapeDtypeStruct((1, 128), F32)],
        scratch_shapes=[pltpu.VMEM((1, W), F32)], compiler_params=_cp(1),
    )(xin, gain_w, *rope, dout)


def _attn_band():
    qi = jnp.arange(ATT_G * WINDOW)[:, None] % WINDOW
    ki = jnp.arange(2 * WINDOW)[None, :]
    rel = qi + WINDOW - ki
    ok = (rel >= 0) & (rel < WINDOW)
    return jnp.stack([jnp.where(ok & (ki >= WINDOW), 0.0, NEG), jnp.where(ok, 0.0, NEG)]).astype(F32)


def _attn_probs(q, kb, sink_ref, band_ref, h, i):
    s = _dot(q, kb, 1, 1) * (HEAD ** -0.5) + band_ref[jnp.minimum(i, 1)]
    r1 = _iota((4 * WINDOW, 1), 0)
    sink = jnp.where(r1 < WINDOW, sink_ref[4 * h], jnp.where(r1 < 2 * WINDOW, sink_ref[4 * h + 1],
                     jnp.where(r1 < 3 * WINDOW, sink_ref[4 * h + 2], sink_ref[4 * h + 3])))
    m = jnp.maximum(jnp.max(s, axis=1, keepdims=True), sink)
    p = jnp.exp(s - m)
    ps = jnp.exp(sink - m)
    inv = 1.0 / (jnp.sum(p, axis=1, keepdims=True) + ps)
    return p * inv, ps * inv


ATT_HPS = 4
_BAND = pl.BlockSpec((2, ATT_G * WINDOW, 2 * WINDOW), lambda h, i: (0, 0, 0))


def _attn_specs(S):
    qspec = pl.BlockSpec((ATT_HPS, ATT_G, WINDOW, HEAD), lambda h, i: (h, 0, i, 0))
    cur = pl.BlockSpec((ATT_HPS, WINDOW, HEAD), lambda h, i: (h, i, 0))
    prev = pl.BlockSpec((ATT_HPS, WINDOW, HEAD), lambda h, i: (h, jnp.maximum(i - 1, 0), 0))
    tok = pl.BlockSpec((WINDOW, ATT_HPS * ATT_G * HEAD), lambda h, i: (i, h))
    return qspec, cur, prev, tok


def _attn_fwd(qh, kh, vh, sinks, *, name="attn_fwd"):
    S = kh.shape[1]
    nb = S // WINDOW

    def body(s_ref, band_ref, q_ref, kc_ref, kp_ref, vc_ref, vp_ref, o_ref):
        h2, i = pl.program_id(0), pl.program_id(1)
        outs = []
        for hh in range(ATT_HPS):
            q = q_ref[hh].reshape(ATT_G * WINDOW, HEAD)
            kb = jnp.concatenate([kp_ref[hh], kc_ref[hh]], axis=0)
            vb = jnp.concatenate([vp_ref[hh], vc_ref[hh]], axis=0)
            probs, _ = _attn_probs(q, kb, s_ref, band_ref, ATT_HPS * h2 + hh, i)
            o = _dot(probs.astype(BF16), vb).astype(BF16)
            outs += [o[WINDOW * g:WINDOW * (g + 1)] for g in range(ATT_G)]
        o_ref[...] = jnp.concatenate(outs, axis=1)

    qspec, cur, prev, tok = _attn_specs(S)
    return pl.pallas_call(
        body, name=name, grid=(ATT_KV // ATT_HPS, nb),
        in_specs=[pl.BlockSpec(memory_space=pltpu.SMEM), _BAND, qspec, cur, prev, cur, prev], out_specs=tok,
        out_shape=jax.ShapeDtypeStruct((S, ATT_KV * ATT_G * HEAD), BF16), compiler_params=_cp(2),
    )(sinks, _attn_band(), qh, kh, kh, vh, vh)


def _attn_bwd(qh, kh, vh, sinks, doh, *, name="attn_bwd"):
    S = kh.shape[1]
    nb = S // WINDOW

    def body(s_ref, band_ref, q_ref, kc_ref, kp_ref, vc_ref, vp_ref, do_ref, dq_ref, dk_ref, dv_ref, dsk_ref):
        h2, i = pl.program_id(0), pl.program_id(1)

        @pl.when(i == 0)
        def _():
            dk_ref[...] = jnp.zeros_like(dk_ref)
            dv_ref[...] = jnp.zeros_like(dv_ref)
            dsk_ref[...] = jnp.zeros_like(dsk_ref)

        dov = do_ref[...]
        cur = pl.multiple_of(i * WINDOW, WINDOW)
        lane = _iota((8, 128), 1)
        row = _iota((8, 128), 0)
        scale = HEAD ** -0.5
        for hh in range(ATT_HPS):
            q = q_ref[hh].reshape(ATT_G * WINDOW, HEAD)
            do = jnp.concatenate([dov[:, HEAD * (ATT_G * hh + g):HEAD * (ATT_G * hh + g + 1)] for g in range(ATT_G)], axis=0)
            kb = jnp.concatenate([kp_ref[hh], kc_ref[hh]], axis=0)
            vb = jnp.concatenate([vp_ref[hh], vc_ref[hh]], axis=0)
            probs, psink = _attn_probs(q, kb, s_ref, band_ref, ATT_HPS * h2 + hh, i)
            dp = _dot(do, vb, 1, 1)
            delta = jnp.sum(probs * dp, axis=1, keepdims=True)
            ds = (probs * (dp - delta)).astype(BF16)
            dq_ref[hh] = (_dot(ds, kb) * scale).reshape(ATT_G, WINDOW, HEAD)
            dkb = _dot(ds, q, 0, 0) * scale
            dvb = _dot(probs.astype(BF16), do, 0, 0)
            dk_ref[hh, pl.ds(cur, WINDOW), :] += dkb[WINDOW:, :]
            dv_ref[hh, pl.ds(cur, WINDOW), :] += dvb[WINDOW:, :]
            prv = pl.multiple_of(jnp.maximum(i - 1, 0) * WINDOW, WINDOW)
            dk_ref[hh, pl.ds(prv, WINDOW), :] += dkb[:WINDOW, :]
            dv_ref[hh, pl.ds(prv, WINDOW), :] += dvb[:WINDOW, :]

            dsr = -psink * delta
            upd = jnp.zeros((8, 128), F32)
            for gq in range(ATT_G):
                v = jnp.sum(dsr[gq * WINDOW:(gq + 1) * WINDOW, :], axis=0, keepdims=True)
                upd = jnp.where((lane == gq) & (row == 0), v, upd)
            dsk_ref[hh] += upd

    qspec, cur, prev, tok = _attn_specs(S)
    full = pl.BlockSpec((ATT_HPS, S, HEAD), lambda h, i: (h, 0, 0))
    return pl.pallas_call(
        body, name=name, grid=(ATT_KV // ATT_HPS, nb),
        in_specs=[pl.BlockSpec(memory_space=pltpu.SMEM), _BAND, qspec, cur, prev, cur, prev, tok],
        out_specs=[qspec, full, full, pl.BlockSpec((ATT_HPS, 8, 128), lambda h, i: (h, 0, 0))],
        out_shape=[jax.ShapeDtypeStruct((ATT_KV, ATT_G, S, HEAD), F32), jax.ShapeDtypeStruct((ATT_KV, S, HEAD), F32),
                   jax.ShapeDtypeStruct((ATT_KV, S, HEAD), F32), jax.ShapeDtypeStruct((ATT_KV, 8, 128), F32)],
        compiler_params=_cp(2),
    )(sinks, _attn_band(), qh, kh, kh, vh, vh, doh)


def _tokens_major(t):
    nh, S, _ = t.shape
    return t.transpose(1, 0, 2).reshape(S, nh * HEAD)


class _NoComm:
    def late_start(self, part, after):
        return None

    def late_arrived(self, part, after):
        return None

    def late_weights(self, w, after, part):
        return w

    def advance(self, after, group=None, tensors=None):
        return None


def _local_step(x, posf, target, w, comm=None):
    S, D = x.shape
    gr = {}
    comm = comm or _NoComm()

    zx, h1 = _norm_mm(x, w["a_norm"], w["w_zx"], name="in_proj_zx", dep=w.get("dep"))
    dtr = _mm(h1, w["w_dt"], name="in_proj_dt")
    xbc = _conv_silu_fwd(zx, 2048, 4096, w["a_conv_w"], w["a_conv_b"], name="a_conv_f")
    dt_g, ac_g, sg_g, ac_t = _ssd_prep(dtr, w["a_dt_bias"], w["a_A_log"])
    y_ssd, states = _ssd_fwd(xbc, dt_g, ac_g, ac_t, dep=comm.late_start(1, xbc))
    yg = _gate_fwd(y_ssd, xbc, zx, w["a_Dexp"], w["a_gnorm"], dep=comm.late_arrived(0, [y_ssd]))
    w = comm.late_weights(w, yg, 0)
    x1 = _mm(yg, w["a_out_proj"], res=x, name="out_proj")

    FW = w["f_w_in"][0].shape[2]

    def ffn_fwd(xin, l, loss_target=None):
        u, h = _norm_mm(xin, w["f_norm"][l], w["f_w_in"][l], name=f"f_in{l}", N=N_CHIPS * FW, tn=FW,
                        b_spec=pl.BlockSpec((None, D, FW), lambda i, j: (j, 0, 0)))
        a = _ffn_act_fwd(u, w["f_conv_w"][l], w["f_conv_b"][l], name=f"f_act_f{l}")
        dep = comm.late_arrived(1, [u]) if l == 0 else None
        xo = _mm(a, w["f_w_down"][l], res=xin, tk=a.shape[1], name=f"f_down{l}", target=loss_target, dep=dep)
        return xo, (h, u)

    x2, ffn0 = ffn_fwd(x1, 0)
    w = comm.late_weights(w, x2, 1)

    kv, hk = _norm_mm(x2, w["kv_norm"], w["w_kv"], bias=w["b_kv"], name="kv_proj")
    q, hq = _norm_mm(x2, w["b_norm"], w["w_q"], bias=w["b_q"], name="q_proj")
    rope = w["rope"] if "rope" in w else _rope_cs(posf)
    kh, vh = _hnrope_fwd(kv, 0, 256, w["k_norm_w"], rope, name="k_rope_f", plain_col=256)
    qr = _hnrope_fwd(q, 0, 1024, w["q_norm_w"], rope, name="q_rope_f")
    qh = qr.reshape(ATT_KV, ATT_G, S, HEAD)
    att = _attn_fwd(qh, kh, vh, w["sinks"])
    x3 = _mm(att, w["w_o"], bias=w["b_o"], res=x2, name="o_proj")
    (dy, loss_part), ffn1 = ffn_fwd(x3, 1, target)

    def ffn_bwd(xin, l, saved, dyo, want_colsum, dep=None):
        h, u = saved
        da = _mm(dyo, w["f_w_down"][l], tb=True, name=f"f_down_dx{l}", dep=dep)
        du, dcw, dcb, a = _ffn_act_bwd(u, w["f_conv_w"][l], w["f_conv_b"][l], da, name=f"f_act_b{l}")
        dw_down = _mm(a, dyo, ta=True, out_dtype=BF16, name=f"f_down_dw{l}")
        dw_in = _mm(h, du, ta=True, out_dtype=BF16, name=f"f_in_dw{l}", dims=(D, N_CHIPS * FW, S), tm=D, tn=FW, tk=S,
                    b_spec=pl.BlockSpec((None, S, FW), lambda i, j, k: (j // 2, 0, j % 2)),
                    o_spec=pl.BlockSpec((None, D, FW), lambda i, j, k: (j, i, 0)), o_shape=(N_CHIPS, D, FW))
        ts = _pick(S, (512, 256))
        pieces = [(pl.BlockSpec((None, ts, FW), lambda i, j, k, q=q: (q // 2, i, q % 2)),
                   pl.BlockSpec((None, D, FW), lambda i, j, k, q=q: (q, 0, 0), pipeline_mode=pl.Buffered(1)))
                  for q in range(N_CHIPS)]
        outs = _mm(du, w["f_w_in"][l], tb=True, name=f"f_in_dx{l}", dims=(S, D, FW), tm=ts, tn=D, tk=FW,
                   a_spec=pieces[0][0], b_spec=pieces[0][1], more=pieces[1:],
                   rms=(xin, w["f_norm"][l], dyo), rms_colsum=want_colsum)
        g = dict(f_norm=outs[1], f_w_in=dw_in, f_conv_w=dcw, f_conv_b=dcb, f_w_down=dw_down)
        return outs[0], g, (outs[2] if want_colsum else None)

    dx3, gr["ffn1"], db_o = ffn_bwd(x3, 1, ffn1, dy, True)
    gr["b_o"] = db_o
    gr["w_o"] = _mm(att, dx3, ta=True, out_dtype=BF16, name="o_proj_dw")
    datt = _mm(dx3, w["w_o"], tb=True, out_dtype=BF16, name="o_proj_dx")
    dqh, dkh, dvh, dsk = _attn_bwd(qh, kh, vh, w["sinks"], datt)
    gr["sinks"] = dsk[:, 0, :4].reshape(1, 16)
    dv = _tokens_major(dvh).astype(BF16)
    dq, db_q, dqn = _hnrope_bwd(q, 0, 1024, w["q_norm_w"], rope, dqh.reshape(16, S, HEAD), name="q_rope_b")
    dk, db_k, dkn = _hnrope_bwd(kv, 0, 256, w["k_norm_w"], rope, dkh, name="k_rope_b")
    gr["q_norm"], gr["k_norm"] = dqn[:, :HEAD], dkn[:, :HEAD]
    gr["b_q"] = db_q
    gr["b_kv"] = jnp.concatenate([db_k, _colsum(dv, name="dv_colsum")], axis=1)
    dkv = jnp.concatenate([dk, dv], axis=1)
    gr["w_q"] = _mm(hq, dq, ta=True, out_dtype=BF16, name="q_proj_dw")
    gr["w_kv"] = _mm(hk, dkv, ta=True, out_dtype=BF16, name="kv_proj_dw")
    tok = comm.advance([gr["w_kv"]], 1, dict(f_down1=gr["ffn1"]["f_w_down"], f_in1=gr["ffn1"]["f_w_in"], w_o=gr["w_o"],
                                             w_q=gr["w_q"], w_kv=gr["w_kv"]))
    tsm = _pick(S, (512, 256))
    dx2, gr["b_norm"] = _mm(dq, w["w_q"], tb=True, name="q_proj_dx", dep=tok, tm=tsm, tn=D, rms=(x2, w["b_norm"], dx3))
    dx2, gr["kv_norm"] = _mm(dkv, w["w_kv"], tb=True, name="kv_proj_dx", tm=tsm, tn=D, rms=(x2, w["kv_norm"], dx2))

    dx1, gr["ffn0"], _ = ffn_bwd(x1, 0, ffn0, dx2, False, dep=comm.advance([dx2]))

    gr["a_out_proj"] = _mm(yg, dx1, ta=True, out_dtype=BF16, name="out_proj_dw")
    tok = comm.advance([dx1, gr["a_out_proj"]], 2,
                       dict(f_down0=gr["ffn0"]["f_w_down"], f_in0=gr["ffn0"]["f_w_in"], out_proj=gr["a_out_proj"]))
    dyg = _mm(dx1, w["a_out_proj"], tb=True, name="out_proj_dx", dep=tok)
    dy_ssd, dz, gr["a_gnorm"], dD = _gate_bwd(y_ssd, xbc, zx, w["a_Dexp"], w["a_gnorm"], dyg)
    gr["a_D"] = dD[:, :SSM_HEADS]
    dxs, dB, dC, dhead = _ssd_bwd(xbc, dt_g, ac_g, ac_t, states, dy_ssd, w["a_Dexp"], dep=comm.advance([dy_ssd]))
    ddtr, dsmall = _ssd_post(dhead, dt_g, sg_g, w["a_A_log_g"])
    gr["a_A_log"] = dsmall[:, 0, :4].reshape(1, SSM_HEADS)
    gr["a_dt_bias"] = dsmall[:, 1, :4].reshape(1, SSM_HEADS)
    dxbc, gr["a_conv_w"], gr["a_conv_b"] = _conv_silu_bwd(
        zx, 2048, 4096, w["a_conv_w"], w["a_conv_b"], [(dxs, 0), (dB, 2048), (dC, 3072)], name="a_conv_b")
    gr["in_proj"] = _in_proj_dw(h1, dz, dxbc, ddtr).T
    tok = comm.advance([dxbc], 3, dict(in_proj=gr["in_proj"].reshape(D, N_CHIPS, -1).transpose(1, 0, 2)))
    ts = _pick(S, (512, 256))
    once = pl.Buffered(1)
    wblk = lambda q: pl.BlockSpec((D, 2048), lambda i, j, k: (0, q), pipeline_mode=once)
    dx0, gr["a_norm"] = _mm(
        dz, w["w_zx"], tb=True, name="in_proj_dx", dims=(S, D, 2048), tm=ts, tn=D, tk=2048,
        a_spec=pl.BlockSpec((ts, 2048), lambda i, j, k: (i, 0)), b_spec=wblk(0),
        more=[(dxbc, pl.BlockSpec((ts, 2048), lambda i, j, k: (i, 0)), w["w_zx"], wblk(1)),
              (dxbc, pl.BlockSpec((ts, 2048), lambda i, j, k: (i, 1)), w["w_zx"], wblk(2)),
              (ddtr, pl.BlockSpec((ts, 128), lambda i, j, k: (i, 0)), w["w_dt"],
               pl.BlockSpec((D, 128), lambda i, j, k: (0, 0), pipeline_mode=once))],
        rms=(x, w["a_norm"], dx1), dep=tok)
    tok = comm.advance([dx0])
    return loss_part, dx0, gr, tok


def _prep_small(full, w):
    w["a_norm"] = full["a_norm"]
    w["a_conv_w"] = full["a_conv_w"][0]
    w["a_conv_b"] = full["a_conv_b"]
    pad32 = lambda v: jnp.pad(v, ((0, 0), (0, 128 - SSM_HEADS)))
    w["a_dt_bias"] = pad32(full["a_dt_bias"])
    w["a_A_log"] = pad32(full["a_A_log"])
    w["a_A_log_g"] = jnp.pad(full["a_A_log"].reshape(SSM_GROUPS, 1, 4), ((0, 0), (0, 0), (0, 124)))
    w["a_Dexp"] = jnp.repeat(full["a_D"], HEAD, axis=1)
    w["a_gnorm"] = full["a_gnorm"]
    w["f_norm"] = [full["f_norm"][l:l + 1] for l in range(2)]
    w["f_conv_w"] = [full["f_conv_w"][l] for l in range(2)]
    w["f_conv_b"] = [full["f_conv_b"][l:l + 1] for l in range(2)]
    w["kv_norm"] = full["kv_norm"].reshape(1, -1)
    w["b_kv"] = full["b_kv"].reshape(1, -1)
    w["k_norm_w"] = jnp.tile(full["k_norm"].reshape(1, HEAD), (1, ATT_KV))
    w["b_norm"] = full["b_norm"]
    w["b_q"] = full["b_q"]
    w["q_norm_w"] = jnp.tile(full["q_norm"], (1, ATT_KV * ATT_G))
    w["sinks"] = full["sinks"].reshape(-1)
    w["b_o"] = full["b_o"]
    return w


def _split_in_proj(ip):
    return ip[:, :6144].astype(BF16), jnp.pad(ip[:, 6144:], ((0, 0), (0, 128 - SSM_HEADS))).astype(BF16)


def _join_in_proj(blocks, *, name="in_proj_join", tr=256):
    _, R, cw = blocks.shape
    zx_cols = 3 * 2048
    rest = N_CHIPS * cw - zx_cols

    def body(b_ref, zx_ref, dt_ref):
        whole = jnp.concatenate([b_ref[j] for j in range(N_CHIPS)], axis=1)
        zx_ref[...] = whole[:, :zx_cols]
        dt_ref[...] = jnp.concatenate([whole[:, zx_cols:], jnp.zeros((tr, 128 - rest), BF16)], axis=1)

    return pl.pallas_call(
        body, name=name, grid=(R // tr,), in_specs=[pl.BlockSpec((N_CHIPS, tr, cw), lambda i: (0, i, 0))],
        out_specs=[pl.BlockSpec((tr, zx_cols), lambda i: (i, 0)), pl.BlockSpec((tr, 128), lambda i: (i, 0))],
        out_shape=[jax.ShapeDtypeStruct((R, zx_cols), BF16), jax.ShapeDtypeStruct((R, 128), BF16)],
        compiler_params=_cp(1),
    )(blocks)


def _in_proj_dw(h, dz, dxbc, ddtr, *, name="in_proj_dw", tn=512):
    S, D = h.shape
    nz, nx = dz.shape[1] // tn, dxbc.shape[1] // tn
    N = dz.shape[1] + dxbc.shape[1] + SSM_HEADS

    def body(h_ref, z_ref, x_ref, t_ref, o_ref):
        j = pl.program_id(0)
        hb = h_ref[...].astype(BF16)

        @pl.when(j < nz)
        def _():
            o_ref[...] = _dot(z_ref[...].astype(BF16), hb, 0, 0).astype(BF16)

        @pl.when((j >= nz) & (j < nz + nx))
        def _():
            o_ref[...] = _dot(x_ref[...].astype(BF16), hb, 0, 0).astype(BF16)

        @pl.when(j == nz + nx)
        def _():
            o_ref[:128, :] = _dot(t_ref[...].astype(BF16), hb, 0, 0).astype(BF16)
            o_ref[128:, :] = jnp.zeros((tn - 128, D), BF16)

    return pl.pallas_call(
        body, name=name, grid=(nz + nx + 1,),
        in_specs=[pl.BlockSpec((S, D), lambda j: (0, 0), pipeline_mode=pl.Buffered(1)),
                  pl.BlockSpec((S, tn), lambda j: (0, jnp.minimum(j, nz - 1))),
                  pl.BlockSpec((S, tn), lambda j: (0, jnp.clip(j - nz, 0, nx - 1))),
                  pl.BlockSpec((S, 128), lambda j: (0, 0))],
        out_specs=pl.BlockSpec((tn, D), lambda j: (j, 0)),
        out_shape=jax.ShapeDtypeStruct((N, D), BF16), compiler_params=_cp(1),
    )(h, dz, dxbc, ddtr)


def _prep_weights(full):
    w = _prep_small(full, {})
    w["w_zx"], w["w_dt"] = _split_in_proj(full["a_in_proj"][0])
    w["a_out_proj"] = full["a_out_proj"][0].astype(BF16)
    w["f_w_in"] = [full["f_w_in"][l].reshape(1024, N_CHIPS, -1).transpose(1, 0, 2).astype(BF16) for l in range(2)]
    w["f_w_down"] = [full["f_w_down"][l].astype(BF16) for l in range(2)]
    w["w_kv"] = full["w_kv"].astype(BF16)
    w["w_q"] = full["w_q"][0].astype(BF16)
    w["w_o"] = full["w_o"][0].astype(BF16)
    return w


def _small_grads(gr):
    g = {}
    g["a_norm"] = gr["a_norm"]
    g["a_conv_w"] = gr["a_conv_w"][None]
    g["a_conv_b"] = gr["a_conv_b"]
    g["a_dt_bias"], g["a_A_log"], g["a_D"] = gr["a_dt_bias"], gr["a_A_log"], gr["a_D"]
    g["a_gnorm"] = gr["a_gnorm"]
    g["kv_norm"] = gr["kv_norm"].reshape(-1)
    g["b_kv"] = gr["b_kv"].reshape(-1)
    g["k_norm"] = gr["k_norm"].reshape(-1)
    g["b_norm"] = gr["b_norm"]
    g["b_q"] = gr["b_q"]
    g["q_norm"] = gr["q_norm"]
    g["sinks"] = gr["sinks"]
    g["b_o"] = gr["b_o"]
    f = [gr["ffn0"], gr["ffn1"]]
    g["f_norm"] = jnp.concatenate([f[0]["f_norm"], f[1]["f_norm"]], axis=0)
    g["f_conv_w"] = jnp.stack([f[l]["f_conv_w"] for l in range(2)])
    g["f_conv_b"] = jnp.concatenate([f[l]["f_conv_b"] for l in range(2)], axis=0)
    return g


def _full_grads(gr):
    g = _small_grads(gr)
    f32 = lambda t: t.astype(F32)
    g["a_in_proj"] = f32(gr["in_proj"])[None]
    g["a_out_proj"] = f32(gr["a_out_proj"])[None]
    g["w_kv"] = f32(gr["w_kv"])
    g["w_q"] = f32(gr["w_q"])[None]
    g["w_o"] = f32(gr["w_o"])[None]
    f = [gr["ffn0"], gr["ffn1"]]
    g["f_w_in"] = jnp.stack([f32(f[l]["f_w_in"]).transpose(1, 0, 2).reshape(1024, -1) for l in range(2)])
    g["f_w_down"] = jnp.stack([f32(f[l]["f_w_down"]) for l in range(2)])
    return g


MESH = pl.DeviceIdType.MESH
WEIGHTS = ("a_norm", "a_in_proj", "a_conv_w", "a_conv_b", "a_dt_bias", "a_A_log", "a_D", "a_gnorm", "a_out_proj",
           "kv_norm", "w_kv", "b_kv", "k_norm", "b_norm", "w_q", "b_q", "q_norm", "sinks", "w_o", "b_o", "f_norm",
           "f_w_in", "f_conv_w", "f_conv_b", "f_w_down")
MATS = (("in_proj", "a_in_proj", 0), ("out_proj", "a_out_proj", 0), ("w_kv", "w_kv", None), ("w_q", "w_q", 0),
        ("w_o", "w_o", 0), ("f_in0", "f_w_in", 0), ("f_in1", "f_w_in", 1), ("f_down0", "f_w_down", 0),
        ("f_down1", "f_w_down", 1))
SMALL_CUT = (("a_norm", 1), ("a_conv_w", 2), ("a_conv_b", 1), ("a_gnorm", 1), ("f_conv_w", 2))
SMALL_REP = ("a_dt_bias", "a_A_log", "a_D", "kv_norm", "b_kv", "k_norm", "b_norm", "b_q", "q_norm", "sinks", "b_o",
             "f_norm", "f_conv_b")


def _coords():
    return lax.axis_index("x"), lax.axis_index("y"), lax.axis_index("c")


def _other_chips(x, y):
    return [(1 - x, y), (x, 1 - y), (1 - x, 1 - y)]


def _pack(arrs, rows_align, lanes, dtype):
    flat = jnp.concatenate([a.reshape(-1).astype(dtype) for a in arrs])
    per = rows_align * lanes
    total = -(-flat.shape[0] // per) * per
    return jnp.pad(flat, (0, total - flat.shape[0])).reshape(total // lanes, lanes)


def _unpack(flat, shapes):
    out, off = [], 0
    for s in shapes:
        n = math.prod(s)
        out.append(flat[off:off + n].reshape(s))
        off += n
    return out


def _remote(src, dst, send, recv, k, dev):
    return pltpu.make_async_remote_copy(src_ref=src, dst_ref=dst, send_sem=send.at[k], recv_sem=recv.at[k],
                                        device_id=dev, device_id_type=MESH)


_ANY = pl.BlockSpec(memory_space=pl.ANY)


def _halves(t):
    r, c = t.shape
    return t.reshape(2, r // 2, c)


RING_SEMS = 9


def _ring_first_copies(refs, send, recv):
    n = (len(refs) - 2) // 2
    sh, sp_ref, outs, sout = refs[:n], refs[n], refs[n + 1:2 * n + 1], refs[2 * n + 1]
    per = RING_SEMS
    x, y, c = _coords()
    me = 2 * x + y
    cx_, cy_, cd_ = _other_chips(x, y)
    sib = (x, y, 1 - c)
    out = [(_remote(sp_ref, sout.at[me], send, recv, per * n + j, (*p, c)),
            _remote(sp_ref, sout.at[2 * p[0] + p[1]], send, recv, per * n + j, (*p, c))) for j, p in enumerate((cx_, cy_, cd_))]
    for t in range(n):
        for k, p in enumerate((cx_, cy_)):
            out.append((_remote(sh[t].at[c], outs[t].at[me, c], send, recv, per * t + k, (*p, c)),
                        _remote(sh[t].at[c], outs[t].at[2 * p[0] + p[1], c], send, recv, per * t + k, (*p, c))))
        own = _remote(sh[t], outs[t].at[me], send, recv, per * t + 8, sib)
        out.append((own, own))
    return out


def _ring_rest(sh, sp_ref, outs, sout, send, recv, loc):
    n, per = len(sh), RING_SEMS
    x, y, c = _coords()
    me = 2 * x + y
    cx_, cy_, cd_ = _other_chips(x, y)
    ix, iy, idg = (2 * p[0] + p[1] for p in (cx_, cy_, cd_))
    to_x, to_y, sib = (*cx_, c), (*cy_, c), (x, y, 1 - c)
    own_small = pltpu.make_async_copy(sp_ref, sout.at[me], loc.at[0])
    own_small.start()
    sends = []

    def go(src, dst, k, dev):
        cp = _remote(src, dst, send, recv, k, dev)
        cp.start()
        sends.append(cp)

    def piece(t, owner, first):
        q = sh[t].shape[1] // 2
        return outs[t].at[owner, c, pl.ds(0 if first else q, q)]

    for t in range(n):
        go(piece(t, ix, False), piece(t, ix, False), per * t + 3, to_y)
        go(outs[t].at[ix, c], outs[t].at[ix, c], per * t + 4, sib)
    for t in range(n):
        go(piece(t, iy, True), piece(t, iy, True), per * t + 2, to_x)
        go(outs[t].at[iy, c], outs[t].at[iy, c], per * t + 5, sib)
    for t in range(n):
        _remote(piece(t, idg, True), piece(t, idg, True), send, recv, per * t + 2, to_x).wait_recv()
        go(piece(t, idg, True), piece(t, idg, True), per * t + 6, sib)
        _remote(piece(t, idg, False), piece(t, idg, False), send, recv, per * t + 3, to_y).wait_recv()
        go(piece(t, idg, False), piece(t, idg, False), per * t + 7, sib)
    for t in range(n):
        q = sh[t].shape[1] // 2
        other = lambda owner, lo=None: outs[t].at[owner, 1 - c] if lo is None else outs[t].at[owner, 1 - c, pl.ds(lo, q)]
        _remote(other(ix), other(ix), send, recv, per * t + 4, sib).wait_recv()
        _remote(other(iy), other(iy), send, recv, per * t + 5, sib).wait_recv()
        _remote(other(idg, 0), other(idg, 0), send, recv, per * t + 6, sib).wait_recv()
        _remote(other(idg, q), other(idg, q), send, recv, per * t + 7, sib).wait_recv()
    for cp in sends:
        cp.wait_send()
    own_small.wait()


def _gather_weights_forward(shards, sp, lands, sland):
    n = len(shards)

    def body(*refs):
        sh, sp_ref = refs[:n], refs[n]
        outs, sout = refs[2 * n + 2:3 * n + 2], refs[3 * n + 2]
        _ring_rest(sh, sp_ref, outs, sout, *refs[3 * n + 3:])

    res = pl.pallas_call(
        body, name="gather_weights_forward", in_specs=[_ANY] * (2 * n + 2), out_specs=[_ANY] * (n + 1),
        out_shape=[jax.ShapeDtypeStruct(a.shape, a.dtype) for a in (*lands, sland)],
        input_output_aliases={n + 1 + t: t for t in range(n + 1)},
        scratch_shapes=[pltpu.SemaphoreType.DMA((RING_SEMS * n,)), pltpu.SemaphoreType.DMA((RING_SEMS * n,)),
                        pltpu.SemaphoreType.DMA((1,))],
    )(*shards, sp, *lands, sland)
    return res[:n], res[n]


_HBM = pl.BlockSpec(memory_space=pltpu.HBM)
_SEMS = pl.BlockSpec(memory_space=pltpu.SEMAPHORE)
_DATAFLOW = pltpu.SideEffectType.DATAFLOW_SIDE_EFFECTING


def _in_hbm(a):
    return pltpu.with_memory_space_constraint(a, pltpu.HBM)


def _start_copies(copies, arrays, n_sem, after, *, name):
    n, na = len(arrays), len(after)

    def body(*refs):
        for mine, _ in copies(refs[:n], refs[n + na], refs[n + na + 1]):
            mine.start()
        refs[-1][...] = jnp.zeros_like(refs[-1])

    res = pl.pallas_call(
        body, name=name, in_specs=[_HBM] * n + [_ANY] * na,
        out_specs=[_SEMS, _SEMS] + [_HBM] * n + [pl.BlockSpec(memory_space=pltpu.VMEM)],
        out_shape=[pltpu.SemaphoreType.DMA((n_sem,)), pltpu.SemaphoreType.DMA((n_sem,))]
        + [pltpu.HBM(a.shape, a.dtype) for a in arrays] + [jax.ShapeDtypeStruct((8, 128), F32)],
        input_output_aliases={t: 2 + t for t in range(n)},
        compiler_params=pltpu.CompilerParams(has_side_effects=_DATAFLOW),
    )(*[_in_hbm(a) for a in arrays], *after)
    return res[0], res[1], list(res[2:2 + n]), res[-1]


def _wait_copies(copies, send, recv, arrays, after, *, name):
    n = len(arrays)

    def body(*refs):
        for mine, theirs in copies(refs[:n], refs[n], refs[n + 1]):
            mine.wait_send()
            theirs.wait_recv()

    return list(pl.pallas_call(
        body, name=name, in_specs=[_HBM] * n + [_SEMS, _SEMS] + [_ANY] * len(after), out_specs=[_HBM] * n,
        out_shape=[pltpu.HBM(a.shape, a.dtype) for a in arrays], input_output_aliases={t: t for t in range(n)},
        compiler_params=pltpu.CompilerParams(has_side_effects=_DATAFLOW),
    )(*arrays, send, recv, *after))


def _sibling_copies(refs, send, recv):
    n = len(refs) // 2
    x, y, c = _coords()
    cps = [_remote(refs[t].at[:, 1 - c], refs[n + t], send, recv, t, (x, y, 1 - c)) for t in range(n)]
    return [(cp, cp) for cp in cps]


def _join_copies(refs, send, recv):
    x, y, c = _coords()
    sib = (x, y, 1 - c)
    return [(_remote(o.at[c], o.at[c], send, recv, t, sib), _remote(o.at[1 - c], o.at[1 - c], send, recv, t, sib))
            for t, o in enumerate(refs)]


def _gather_copies(sh, land, send, recv):
    x, y, c = _coords()
    me = 2 * x + y
    out = []
    for t in range(len(sh)):
        for j, (cx, cy) in enumerate(_other_chips(x, y)):
            dev = (cx, cy, c)
            out.append((_remote(sh[t].at[c], land[t].at[me, c], send, recv, 4 * t + j, dev),
                        _remote(sh[t].at[c], land[t].at[2 * cx + cy, c], send, recv, 4 * t + j, dev)))
        sib = (x, y, 1 - c)
        out.append((_remote(sh[t], land[t].at[me], send, recv, 4 * t + 3, sib),
                    _remote(sh[t], land[t].at[me], send, recv, 4 * t + 3, sib)))
    return out


def _gather_start(shards, after, *, name):
    n = len(shards)

    def body(*refs):
        sh, land = refs[:n], refs[n:2 * n]
        send, recv = refs[2 * n + 1], refs[2 * n + 2]
        token = refs[-1]
        for mine, _ in _gather_copies(sh, land, send, recv):
            mine.start()
        token[...] = jnp.zeros_like(token)

    lands = [_in_hbm(lax.empty((N_CHIPS,) + s.shape, s.dtype)) for s in shards]
    res = pl.pallas_call(
        body, name=name, in_specs=[_HBM] * (2 * n) + [_ANY],
        out_specs=[_SEMS, _SEMS] + [_HBM] * (2 * n) + [pl.BlockSpec(memory_space=pltpu.VMEM)],
        out_shape=[pltpu.SemaphoreType.DMA((4 * n,)), pltpu.SemaphoreType.DMA((4 * n,))]
        + [pltpu.HBM(s.shape, s.dtype) for s in shards] + [pltpu.HBM(l.shape, l.dtype) for l in lands]
        + [jax.ShapeDtypeStruct((8, 128), F32)],
        input_output_aliases={t: 2 + t for t in range(2 * n)},
        compiler_params=pltpu.CompilerParams(has_side_effects=_DATAFLOW),
    )(*[_in_hbm(s) for s in shards], *lands, after)
    return res[0], res[1], res[2:2 + n], res[2 + n:2 + 2 * n], res[-1]


def _gather_wait(send, recv, shards, lands, after, *, name):
    n = len(shards)

    def body(*refs):
        sh, land = refs[:n], refs[n:2 * n]
        send_r, recv_r = refs[2 * n], refs[2 * n + 1]
        for mine, theirs in _gather_copies(sh, land, send_r, recv_r):
            mine.wait_send()
            theirs.wait_recv()

    res = pl.pallas_call(
        body, name=name, in_specs=[_HBM] * (2 * n) + [_SEMS, _SEMS, _ANY], out_specs=[_HBM] * (2 * n),
        out_shape=[pltpu.HBM(s.shape, s.dtype) for s in shards] + [pltpu.HBM(l.shape, l.dtype) for l in lands],
        input_output_aliases={t: t for t in range(2 * n)},
        compiler_params=pltpu.CompilerParams(has_side_effects=_DATAFLOW),
    )(*shards, *lands, send, recv, after)
    return res[n:]


def _forward_copies(refs, send, recv):
    x, y, c = _coords()
    sib = (x, y, 1 - c)
    srcs = [2 * cx + cy for cx, cy in _other_chips(x, y)]
    return [(_remote(o.at[s, c], o.at[s, c], send, recv, 3 * t + j, sib),
             _remote(o.at[s, 1 - c], o.at[s, 1 - c], send, recv, 3 * t + j, sib))
            for t, o in enumerate(refs) for j, s in enumerate(srcs)]


def _small_copies(v, land, send, recv):
    x, y, c = _coords()
    me = 4 * x + 2 * y + c
    out = []
    for k in range(1, 8):
        px = 1 - x if k & 4 else x
        py = 1 - y if k & 2 else y
        pc = 1 - c if k & 1 else c
        out.append((_remote(v, land.at[me], send, recv, k - 1, (px, py, pc)),
                    _remote(v, land.at[4 * px + 2 * py + pc], send, recv, k - 1, (px, py, pc))))
    return out


def _small_start(v, after, *, name):
    def body(v_ref, land_ref, after_ref, send, recv, v_thru, land_thru, token):
        for mine, _ in _small_copies(v_ref, land_ref, send, recv):
            mine.start()
        token[...] = jnp.zeros_like(token)

    land = _in_hbm(lax.empty((8,) + v.shape, v.dtype))
    return pl.pallas_call(
        body, name=name, in_specs=[_HBM, _HBM, _ANY],
        out_specs=[_SEMS, _SEMS, _HBM, _HBM, pl.BlockSpec(memory_space=pltpu.VMEM)],
        out_shape=[pltpu.SemaphoreType.DMA((7,)), pltpu.SemaphoreType.DMA((7,)), pltpu.HBM(v.shape, v.dtype),
                   pltpu.HBM(land.shape, land.dtype), jax.ShapeDtypeStruct((8, 128), F32)],
        input_output_aliases={0: 2, 1: 3}, compiler_params=pltpu.CompilerParams(has_side_effects=_DATAFLOW),
    )(_in_hbm(v), land, after)


def _small_wait(send, recv, v, land, after, *, name):
    def body(v_ref, land_ref, send_r, recv_r, *rest):
        for mine, theirs in _small_copies(v_ref, land_ref, send_r, recv_r):
            mine.wait_send()
            theirs.wait_recv()

    return pl.pallas_call(
        body, name=name, in_specs=[_HBM, _HBM, _SEMS, _SEMS] + [_ANY] * len(after), out_specs=[_HBM, _HBM],
        out_shape=[pltpu.HBM(v.shape, v.dtype), pltpu.HBM(land.shape, land.dtype)],
        input_output_aliases={0: 0, 1: 1}, compiler_params=pltpu.CompilerParams(has_side_effects=_DATAFLOW),
    )(v, land, send, recv, *after)


def _small_sum(v, land, me_idx, *, name="small_sum"):
    def body(me_ref, v_ref, land_ref, o_ref):
        acc = None
        for s in range(8):
            term = jnp.where(me_ref[0] == s, v_ref[...], land_ref[s])
            acc = term if acc is None else acc + term
        o_ref[...] = acc

    whole = lambda shape: pl.BlockSpec(shape, lambda i, me_ref: (0,) * len(shape))
    return pl.pallas_call(
        body, name=name,
        grid_spec=pltpu.PrefetchScalarGridSpec(num_scalar_prefetch=1, grid=(1,), in_specs=[whole(v.shape), whole(land.shape)],
                                               out_specs=whole(v.shape)),
        out_shape=jax.ShapeDtypeStruct(v.shape, F32), compiler_params=_cp(1),
    )(me_idx, v, land)


RS_ROW_SPLIT = 2


def _rs_add_pair(gs, as_, c_idx, *, name):
    n = len(gs)

    def body(c_ref, *refs):
        for t in range(n):
            refs[2 * n + t][...] = (refs[t][...].astype(F32) + refs[n + t][...].astype(F32)).astype(BF16)

    def gspec(g):
        _, _, rh, cols = g.shape
        return pl.BlockSpec((None, None, rh // RS_ROW_SPLIT, cols), lambda j, i, c_ref: (j, c_ref[0], i, 0))

    def pspec(g):
        _, _, rh, cols = g.shape
        return pl.BlockSpec((None, rh // RS_ROW_SPLIT, cols), lambda j, i, c_ref: (j, i, 0))

    return pl.pallas_call(
        body, name=name,
        grid_spec=pltpu.PrefetchScalarGridSpec(
            num_scalar_prefetch=1, grid=(N_CHIPS, RS_ROW_SPLIT),
            in_specs=[gspec(g) for g in gs] + [pspec(g) for g in gs], out_specs=[pspec(g) for g in gs]),
        out_shape=[jax.ShapeDtypeStruct((N_CHIPS,) + g.shape[2:], BF16) for g in gs], compiler_params=_cp(2),
    )(c_idx, *gs, *as_)


def _chips_copies(p, r, send, recv):
    x, y, c = _coords()
    return [_remote(p[t].at[2 * cx + cy], r[t].at[k], send, recv, 3 * t + k, (cx, cy, c))
            for k, (cx, cy) in enumerate(_other_chips(x, y)) for t in range(len(p))]


def _rs_chips_start(ps, after, *, name):
    n, na = len(ps), len(after)

    def body(*refs):
        p, r = refs[:n], refs[n:2 * n]
        send, recv = refs[2 * n + na], refs[2 * n + na + 1]
        token = refs[-1]
        for cp in _chips_copies(p, r, send, recv):
            cp.start()
        token[...] = jnp.zeros_like(token)

    lands = [_in_hbm(lax.empty((3,) + p.shape[1:], p.dtype)) for p in ps]
    res = pl.pallas_call(
        body, name=name, in_specs=[_HBM] * (2 * n) + [_ANY] * na,
        out_specs=[_SEMS, _SEMS] + [_HBM] * (2 * n) + [pl.BlockSpec(memory_space=pltpu.VMEM)],
        out_shape=[pltpu.SemaphoreType.DMA((3 * n,)), pltpu.SemaphoreType.DMA((3 * n,))]
        + [pltpu.HBM(p.shape, p.dtype) for p in ps] + [pltpu.HBM(l.shape, l.dtype) for l in lands]
        + [jax.ShapeDtypeStruct((8, 128), F32)],
        input_output_aliases={t: 2 + t for t in range(2 * n)},
        compiler_params=pltpu.CompilerParams(has_side_effects=_DATAFLOW),
    )(*[_in_hbm(p) for p in ps], *lands, *after)
    return res[0], res[1], res[2:2 + n], res[2 + n:2 + 2 * n], res[-1]


def _rs_chips_wait(send, recv, ps, lands, after, *, name):
    n = len(ps)

    def body(*refs):
        p, r = refs[:n], refs[n:2 * n]
        for cp in _chips_copies(p, r, refs[2 * n], refs[2 * n + 1]):
            cp.wait_send()
            cp.wait_recv()

    res = pl.pallas_call(
        body, name=name, in_specs=[_HBM] * (2 * n) + [_SEMS, _SEMS] + [_ANY] * len(after), out_specs=[_HBM] * (2 * n),
        out_shape=[pltpu.HBM(p.shape, p.dtype) for p in ps] + [pltpu.HBM(l.shape, l.dtype) for l in lands],
        input_output_aliases={t: t for t in range(2 * n)},
        compiler_params=pltpu.CompilerParams(has_side_effects=_DATAFLOW),
    )(*ps, *lands, send, recv, *after)
    return res[:n], res[n:]


def _rs_add_chips(ps, rs, idx, *, name):
    n = len(ps)

    def body(idx_ref, *refs):
        for t in range(n):
            p_ref, r0, r1, r2 = refs[4 * t:4 * t + 4]
            refs[4 * n + t][...] = ((p_ref[...].astype(F32) + r0[...].astype(F32)) + r1[...].astype(F32)) + r2[...].astype(F32)

    in_specs, args = [], []
    for p, r in zip(ps, rs):
        _, rh, cols = p.shape
        blk = (None, rh // RS_ROW_SPLIT, cols)
        in_specs.append(pl.BlockSpec(blk, lambda i, idx_ref: (idx_ref[0], i, 0)))
        in_specs += [pl.BlockSpec(blk, lambda i, idx_ref, k=k: (k, i, 0)) for k in range(3)]
        args += [p, r, r, r]
    out_specs = [pl.BlockSpec((None, p.shape[1] // RS_ROW_SPLIT, p.shape[2]), lambda i, idx_ref: (idx_ref[1], i, 0))
                 for p in ps]
    return pl.pallas_call(
        body, name=name,
        grid_spec=pltpu.PrefetchScalarGridSpec(num_scalar_prefetch=1, grid=(RS_ROW_SPLIT,), in_specs=in_specs,
                                               out_specs=out_specs),
        out_shape=[jax.ShapeDtypeStruct((2,) + p.shape[1:], F32) for p in ps], compiler_params=_cp(1),
    )(idx, *args)


def _adamw(w, gs, m, v, *, name, dep=None):
    L, Rr, C = w.shape
    tr, tc = _pick(Rr, (256, 128, 64)), C
    if tr == Rr and Rr * C > 512 * 1024:
        tc = 256
    bc1 = 1.0 - ADAM_B1 ** ADAM_STEP
    bc2 = 1.0 - ADAM_B2 ** ADAM_STEP
    nd = 0 if dep is None else 1

    def body(*refs):
        w_ref, m_ref, v_ref = refs[0], refs[1], refs[2]
        g_refs = refs[3:3 + L]
        d_ref, mo_ref, vo_ref, go_ref = refs[3 + L + nd:]
        layer = pl.program_id(0)
        gv = g_refs[0][...]
        for q in range(1, L):
            gv = jnp.where(layer == q, g_refs[q][...], gv)
        mn = ADAM_B1 * m_ref[...] + (1.0 - ADAM_B1) * gv
        vn = ADAM_B2 * v_ref[...] + (1.0 - ADAM_B2) * (gv * gv)
        go_ref[...] = gv
        mo_ref[...] = mn
        vo_ref[...] = vn
        d_ref[...] = -ADAM_LR * ((mn / bc1) / (jnp.sqrt(vn / bc2) + ADAM_EPS) + ADAM_WD * w_ref[...])

    blk = pl.BlockSpec((None, tr, tc), lambda l, i, j: (l, i, j))
    gblks = [pl.BlockSpec((tr, tc), lambda l, i, j, q=q: (jnp.where(l == q, i, 0), jnp.where(l == q, j, 0))) for q in range(L)]
    return pl.pallas_call(
        body, name=name, grid=(L, Rr // tr, C // tc), in_specs=[blk] * 3 + gblks + [_ANY] * nd, out_specs=[blk] * 4,
        out_shape=[jax.ShapeDtypeStruct((L, Rr, C), F32)] * 4, compiler_params=_cp(3),
    )(w, m, v, *gs, *([] if dep is None else [dep]))


def _adamw_leaves(ws, gs, ms, vs, *, name):
    n = len(ws)
    bc1 = 1.0 - ADAM_B1 ** ADAM_STEP
    bc2 = 1.0 - ADAM_B2 ** ADAM_STEP

    def body(*refs):
        w_refs, g_refs, m_refs, v_refs, d_refs, mo_refs, vo_refs = (refs[q * n:(q + 1) * n] for q in range(7))
        for k in range(n):
            gv = g_refs[k][...]
            mn = ADAM_B1 * m_refs[k][...] + (1.0 - ADAM_B1) * gv
            vn = ADAM_B2 * v_refs[k][...] + (1.0 - ADAM_B2) * (gv * gv)
            mo_refs[k][...] = mn
            vo_refs[k][...] = vn
            d_refs[k][...] = -ADAM_LR * ((mn / bc1) / (jnp.sqrt(vn / bc2) + ADAM_EPS) + ADAM_WD * w_refs[k][...])

    in_vmem = pl.BlockSpec(memory_space=pltpu.VMEM)
    outs = pl.pallas_call(
        body, name=name, in_specs=[in_vmem] * (4 * n), out_specs=[in_vmem] * (3 * n),
        out_shape=[jax.ShapeDtypeStruct(t.shape, F32) for t in ws] * 3,
    )(*ws, *gs, *ms, *vs)
    return outs[:n], outs[n:2 * n], outs[2 * n:]


def kernel(x, positions, a_norm, a_in_proj, a_conv_w, a_conv_b, a_dt_bias, a_A_log, a_D, a_gnorm, a_out_proj,
           kv_norm, w_kv, b_kv, k_norm, b_norm, w_q, b_q, q_norm, sinks, w_o, b_o, f_norm, f_w_in, f_conv_w,
           f_conv_b, f_w_down, loss_target, m_a_norm, m_a_in_proj, m_a_conv_w, m_a_conv_b, m_a_dt_bias, m_a_A_log,
           m_a_D, m_a_gnorm, m_a_out_proj, m_kv_norm, m_w_kv, m_b_kv, m_k_norm, m_b_norm, m_w_q, m_b_q, m_q_norm,
           m_sinks, m_w_o, m_b_o, m_f_norm, m_f_w_in, m_f_conv_w, m_f_conv_b, m_f_w_down, v_a_norm, v_a_in_proj,
           v_a_conv_w, v_a_conv_b, v_a_dt_bias, v_a_A_log, v_a_D, v_a_gnorm, v_a_out_proj, v_kv_norm, v_w_kv,
           v_b_kv, v_k_norm, v_b_norm, v_w_q, v_b_q, v_q_norm, v_sinks, v_w_o, v_b_o, v_f_norm, v_f_w_in,
           v_f_conv_w, v_f_conv_b, v_f_w_down):
    wl = dict(zip(WEIGHTS, (a_norm, a_in_proj, a_conv_w, a_conv_b, a_dt_bias, a_A_log, a_D, a_gnorm, a_out_proj,
                            kv_norm, w_kv, b_kv, k_norm, b_norm, w_q, b_q, q_norm, sinks, w_o, b_o, f_norm, f_w_in,
                            f_conv_w, f_conv_b, f_w_down)))
    ml = dict(zip(WEIGHTS, (m_a_norm, m_a_in_proj, m_a_conv_w, m_a_conv_b, m_a_dt_bias, m_a_A_log, m_a_D, m_a_gnorm,
                            m_a_out_proj, m_kv_norm, m_w_kv, m_b_kv, m_k_norm, m_b_norm, m_w_q, m_b_q, m_q_norm,
                            m_sinks, m_w_o, m_b_o, m_f_norm, m_f_w_in, m_f_conv_w, m_f_conv_b, m_f_w_down)))
    vl = dict(zip(WEIGHTS, (v_a_norm, v_a_in_proj, v_a_conv_w, v_a_conv_b, v_a_dt_bias, v_a_A_log, v_a_D, v_a_gnorm,
                            v_a_out_proj, v_kv_norm, v_w_kv, v_b_kv, v_k_norm, v_b_norm, v_w_q, v_b_q, v_q_norm,
                            v_sinks, v_w_o, v_b_o, v_f_norm, v_f_w_in, v_f_conv_w, v_f_conv_b, v_f_w_down)))
    xi, yi, ci = _coords()
    me = 2 * xi + yi
    S = x.shape[1]

    def shard(name, src):
        _, wn, layer = next(m for m in MATS if m[0] == name)
        return _halves((src[wn] if layer is None else src[wn][layer]).astype(BF16))

    rows = lambda t: t.reshape(-1, t.shape[-1])
    c_idx = jnp.reshape(ci, (1,)).astype(jnp.int32)
    me_c = jnp.stack([me, ci]).astype(jnp.int32)
    early = ("in_proj",)
    late = (("out_proj", "f_in0", "f_down0"), ("w_kv", "w_q", "w_o", "f_in1", "f_down1"))

    sp = _pack([wl[n] for n, _ in SMALL_CUT], 8, 128, F32)
    e_sh = [shard(k, wl) for k in early]
    ring = e_sh + [sp] + [lax.empty((N_CHIPS,) + t.shape, t.dtype) for t in e_sh + [sp]]
    g_send, g_recv, ring, g_tok = _start_copies(_ring_first_copies, ring, RING_SEMS * len(e_sh) + 3, [],
                                                name="gather_weights_start")
    anchor = 0.0 * g_tok[0, 0]
    tied = {wn: wl[wn] + anchor for wn in sorted({wn for name, wn, _ in MATS if name not in early})}
    posf = positions.reshape(S, 1).astype(F32) + anchor
    shards = {k: shard(k, tied) for part in late for k in part}
    rope = _rope_cs(posf)
    ring = _wait_copies(_ring_first_copies, g_send, g_recv, ring, [*rope, *shards.values()], name="gather_weights_arrived")
    ne = len(e_sh)
    gathered, gs = _gather_weights_forward(ring[:ne], ring[ne], ring[ne + 1:2 * ne + 1], ring[2 * ne + 1])
    gt = {k: t.reshape(N_CHIPS, -1, t.shape[-1]) for k, t in zip(early, gathered)}
    started = {0: _gather_start([shards[k] for k in late[0]], gs, name="gather_late_start0")}
    full = {n: wl[n] for n in SMALL_REP}
    gs = gs.reshape(N_CHIPS, -1)
    off = 0
    for n, ax in SMALL_CUT:
        shp = wl[n].shape
        size = math.prod(shp)
        piece = jnp.moveaxis(gs[:, off:off + size].reshape((N_CHIPS,) + shp), 0, ax)
        full[n] = piece.reshape(shp[:ax] + (N_CHIPS * shp[ax],) + shp[ax + 1:])
        off += size
    w = _prep_small(full, {})
    w["w_zx"], w["w_dt"] = _join_in_proj(gt["in_proj"])
    w["dep"] = started[0][4]
    w["rope"] = rope

    class Comm:
        flight = []
        reduced = {}

        forwarding = {}

        def late_start(self, part, after):
            started[part] = _gather_start([shards[k] for k in late[part]], after, name=f"gather_late_start{part}")
            return started[part][4]

        def late_arrived(self, part, after):
            send, recv, shs, lands, _ = started[part]
            lands = _gather_wait(send, recv, shs, lands, after[0], name=f"gather_late_wait{part}")
            send, recv, lands, token = _start_copies(_forward_copies, list(lands), 3 * len(lands), after,
                                                     name=f"gather_late_forward_start{part}")
            self.forwarding[part] = (send, recv, lands)
            return token

        def late_weights(self, w, after, part):
            send, recv, lands = self.forwarding[part]
            lands = _wait_copies(_forward_copies, send, recv, lands, [after], name=f"gather_late_forward_wait{part}")
            lt = {k: t.reshape(N_CHIPS, -1, t.shape[-1]) for k, t in zip(late[part], lands)}
            w = dict(w)
            if part == 0:
                w["a_out_proj"], w["f_w_in"], w["f_w_down"] = rows(lt["out_proj"]), [lt["f_in0"]], [rows(lt["f_down0"])]
            else:
                w["w_kv"], w["w_q"], w["w_o"] = (rows(lt[k]) for k in ("w_kv", "w_q", "w_o"))
                w["f_w_in"], w["f_w_down"] = w["f_w_in"] + [lt["f_in1"]], w["f_w_down"] + [rows(lt["f_down1"])]
            return w

        def advance(self, after, group=None, tensors=None):
            token = None
            for grp in list(self.flight):
                tag, n = grp["tag"], len(grp["names"])
                dep = list(after) + ([] if token is None else [token])
                if grp["stage"] == "sibling":
                    arrs = _wait_copies(_sibling_copies, grp["send"], grp["recv"], grp["arrays"], dep, name=f"rs_sibling_wait{tag}")
                    pairs = _rs_add_pair(arrs[:n], arrs[n:], c_idx, name=f"rs_add_pair{tag}")
                    send, recv, ps, lands, token = _rs_chips_start(pairs, dep, name=f"rs_chips_start{tag}")
                    grp.update(stage="chips", send=send, recv=recv, ps=ps, lands=lands)
                elif grp["stage"] == "chips":
                    ps, rs = _rs_chips_wait(grp["send"], grp["recv"], grp["ps"], grp["lands"], dep, name=f"rs_chips_wait{tag}")
                    halves = _rs_add_chips(ps, rs, me_c, name=f"rs_add_chips{tag}")
                    send, recv, arrs, token = _start_copies(_join_copies, halves, n, dep, name=f"rs_join_start{tag}")
                    grp.update(stage="join", send=send, recv=recv, arrays=arrs)
                else:
                    joined = _wait_copies(_join_copies, grp["send"], grp["recv"], grp["arrays"], dep, name=f"rs_join_wait{tag}")
                    self.reduced.update({k: rows(t) for k, t in zip(grp["names"], joined)})
                    self.flight.remove(grp)
            if group is not None:
                names = list(tensors)
                glist = [tensors[k].reshape(N_CHIPS, 2, -1, tensors[k].shape[-1]) for k in names]
                lands = [lax.empty((N_CHIPS,) + gq.shape[2:], gq.dtype) for gq in glist]
                dep = [a for a in after if not any(a is t for t in tensors.values())] + ([] if token is None else [token])
                send, recv, arrs, token = _start_copies(_sibling_copies, glist + lands, len(names), dep,
                                                        name=f"rs_sibling_start{group}")
                self.flight.append(dict(tag=group, names=names, stage="sibling", send=send, recv=recv, arrays=arrs))
            return token

    comm = Comm()

    loss_part, dx0, gr, tok = _local_step(x[0], posf, loss_target[0], w, comm)
    g = _small_grads(gr)

    small_names = [n for n, _ in SMALL_CUT] + list(SMALL_REP)
    sv = _pack([g[n] for n in small_names] + [loss_part[0:1, 0:1]], 8, 128, F32)
    s_send, s_recv, sv, s_land, s_token = _small_start(sv, tok, name="small_start")

    grads, delta, new_m, new_v = {}, {}, {}, {}

    def update(wn, dep):
        gl = [comm.reduced[name] for name, n2, _ in MATS if n2 == wn]
        shp = wl[wn].shape
        three = (len(gl),) + gl[0].shape
        flip = shp[-1] % 128 != 0
        view = (lambda t: t.reshape(three).transpose(0, 2, 1)) if flip else (lambda t: t.reshape(three))
        back = (lambda t: t.transpose(0, 2, 1).reshape(shp)) if flip else (lambda t: t.reshape(shp))
        if flip:
            gl = [t.T for t in gl]
        d, mn, vn, go = _adamw(view(wl[wn]), gl, view(ml[wn]), view(vl[wn]), name="adamw_" + wn, dep=dep)
        grads[wn], delta[wn], new_m[wn], new_v[wn] = back(go), back(d), back(mn), back(vn)
        return d

    first = [update(wn, s_token) for wn in ("w_q", "w_o", "w_kv")]
    second = [update(wn, first[-1]) for wn in ("f_w_in", "f_w_down", "a_out_proj")]
    tok = comm.advance(first + second)
    done = first + second + [tok]

    sv, s_land = _small_wait(s_send, s_recv, sv, s_land, done, name="small_wait")
    sred = _small_sum(sv, s_land, jnp.reshape(2 * me + ci, (1,)).astype(jnp.int32)).reshape(-1)
    small_shapes = [g[n].shape for n in small_names] + [(1,)]
    sg = dict(zip(small_names + ["loss"], _unpack(sred, small_shapes)))
    loss = sg["loss"].reshape(())
    g_small = {}
    for n, ax in SMALL_CUT:
        size = wl[n].shape[ax]
        g_small[n] = lax.dynamic_slice_in_dim(sg[n], me * size, size, axis=ax)
    for n in SMALL_REP:
        g_small[n] = sg[n].reshape(wl[n].shape)

    leaves = lambda d: [d[n].reshape(1, -1) if d[n].ndim == 1 else d[n] for n in small_names]
    ds, mns, vns = _adamw_leaves(leaves(wl), leaves(g_small), leaves(ml), leaves(vl), name="adamw_small")
    comm.advance([ds[0]])
    update("a_in_proj", None)
    for n, dd, mm, vv in zip(small_names, ds, mns, vns):
        shp = wl[n].shape
        grads[n], delta[n], new_m[n], new_v[n] = g_small[n], dd.reshape(shp), mm.reshape(shp), vv.reshape(shp)

    return (loss, dx0[None], *[grads[n] for n in WEIGHTS], *[delta[n] for n in WEIGHTS],
            *[new_m[n] for n in WEIGHTS], *[new_v[n] for n in WEIGHTS])
```
